```python
import math
import jax, jax.numpy as jnp
from jax import lax
import numpy as np

D_MODEL = 1024
BATCH = 8
SEQ = 2048
DEPTH = 2

D_MIX = D_MODEL
HEAD_DIM = 64
N_ATT_HEADS = 8
D_ATT = N_ATT_HEADS * HEAD_DIM
DILATED_PATTERNS = ((128, 1), (512, 4), (2048, 16))
ATT_BLOCK = 128
SSM_GROUP = 16
D_SSM = D_MIX // 4
N_SSM_GROUPS = D_SSM // SSM_GROUP
SSM_STATE = 64
POOL_WINDOWS = (2, 4, 8, 16)
D_POOL = D_MIX - D_ATT - D_SSM
POOL_GROUP = D_POOL // len(POOL_WINDOWS)
D_IN = 3 * D_ATT + D_SSM + D_POOL
IN_SPLITS = (D_ATT, 2 * D_ATT, 3 * D_ATT, 3 * D_ATT + D_SSM)
D_FF = 256 * int(math.ceil(8 * D_MODEL / 3 / 256))
N_BUCKETS = 32
MAX_DISTANCE = 2048
ALPHA = (2 * DEPTH) ** 0.25
BETA = (8 * DEPTH) ** -0.25
FFN_RES = 0.5
LN_EPS = 1e-5
NEG = -1e30

kernel_name = "hybrid_dilated_attn_s5_pool_macaron_deepnorm"


def _layernorm(x):
    xf = x.astype(jnp.float32)
    mu = xf.mean(-1, keepdims=True)
    var = jnp.square(xf - mu).mean(-1, keepdims=True)
    return ((xf - mu) * lax.rsqrt(var + LN_EPS)).astype(x.dtype)


def _layernorm_affine(x, gain, bias):
    return _layernorm(x) * gain + bias


def _modulate(x, shift, scale):
    return _layernorm(x) * (1.0 + scale) + shift


def _swiglu(h, w_gate, w_up, w_down):
    return (jax.nn.silu(h @ w_gate) * (h @ w_up)) @ w_down


def _t5_bucket(dist):
    max_exact = N_BUCKETS // 2
    d = np.maximum(dist, 1).astype(np.float32)
    large = max_exact + (np.log(d / max_exact) / math.log(MAX_DISTANCE / max_exact)
                         * (N_BUCKETS - max_exact)).astype(np.int32)
    large = np.minimum(large, N_BUCKETS - 1)
    return np.where(dist < max_exact, dist, large).astype(np.int32)


def _dilated_branch(q, k, v, rel_bias, window, dilation):
    B, S, H, E = q.shape
    Q = ATT_BLOCK
    n_keys = window // dilation
    L = S // dilation
    nb = -(-L // Q)
    Lp = nb * Q

    def by_residue(t):
        t = t.reshape(B, L, dilation, H, E).transpose(0, 2, 1, 3, 4)
        return jnp.pad(t, ((0, 0), (0, 0), (0, Lp - L), (0, 0), (0, 0)))

    def band(t):
        t = jnp.pad(t, ((0, 0), (0, 0), (Q, 0), (0, 0), (0, 0))).reshape(B, dilation, nb + 1, Q, H, E)
        return jnp.concatenate([t[:, :, :-1], t[:, :, 1:]], axis=3)

    qb = by_residue(q).reshape(B, dilation, nb, Q, H, E)
    kb = band(by_residue(k))
    vb = band(by_residue(v))

    i = np.arange(Q)[:, None]
    j = np.arange(2 * Q)[None, :]
    r = i + Q - j
    in_band = (r >= 0) & (r <= n_keys)
    k_abs = np.arange(nb)[:, None, None] * Q + j[None] - Q
    valid = (in_band[None] & (k_abs >= 0))[:, None]
    bucket = _t5_bucket(np.clip(r, 0, None) * dilation)
    bias = jnp.transpose(rel_bias[bucket], (2, 0, 1)).astype(jnp.float32)

    s = jnp.einsum('brnqhe,brnkhe->brnhqk', qb, kb, preferred_element_type=jnp.float32)
    s = jnp.where(valid, s + bias, NEG)
    m = s.max(-1, keepdims=True)
    p = jnp.exp(s - m)
    den = p.sum(-1, keepdims=True)
    o = jnp.einsum('brnhqk,brnkhe->brnqhe', p, vb.astype(jnp.float32))
    o = o / jnp.swapaxes(den, 3, 4)
    lse = jnp.swapaxes((m + jnp.log(den))[..., 0], 3, 4)

    o = o.reshape(B, dilation, Lp, H, E)[:, :, :L].transpose(0, 2, 1, 3, 4).reshape(B, S, H, E)
    lse = lse.reshape(B, dilation, Lp, H)[:, :, :L].transpose(0, 2, 1, 3).reshape(B, S, H)
    return o, lse


def _dilated_attention(q, k, v, rel_bias):
    outs, lses = [], []
    for window, dilation in DILATED_PATTERNS:
        o, lse = _dilated_branch(q, k, v, rel_bias, window, dilation)
        outs.append(o)
        lses.append(lse)
    w = jax.nn.softmax(jnp.stack(lses, 0), axis=0)
    return jnp.einsum('pbsh,pbshe->bshe', w, jnp.stack(outs, 0))


def _s5(u, a_re, a_im, log_dt, b_re, b_im, c_re, c_im, d_skip, glu_w, glu_b):
    Bsz, S, _ = u.shape
    f32 = jnp.float32
    lam = lax.complex(a_re.astype(f32), a_im.astype(f32))
    dt = jnp.exp(log_dt.astype(f32))[:, None]
    a_bar = jnp.exp(lam * dt)
    b_bar = ((a_bar - 1.0) / lam)[:, :, None] * lax.complex(b_re.astype(f32), b_im.astype(f32))
    uf = u.astype(f32)
    bu = jnp.einsum('bsgc,gpc->bsgp', uf.reshape(Bsz, S, N_SSM_GROUPS, SSM_GROUP), b_bar)
    a_full = jnp.broadcast_to(a_bar, bu.shape)

    def combine(e1, e2):
        a1, b1 = e1
        a2, b2 = e2
        return a2 * a1, a2 * b1 + b2

    _, states = lax.associative_scan(combine, (a_full, bu), axis=1)
    cm = lax.complex(c_re.astype(f32), c_im.astype(f32))
    y = jnp.einsum('gcp,bsgp->bsgc', cm, states).real.reshape(Bsz, S, D_SSM)
    y = y + d_skip.astype(f32) * uf
    return y * jax.nn.sigmoid(jax.nn.gelu(y) @ glu_w.astype(f32) + glu_b.astype(f32))


def _pool_mixer(u, pool_w, pool_scale):
    Bsz, S, _ = u.shape
    ug = u.astype(jnp.float32).reshape(Bsz, S, len(POOL_WINDOWS), POOL_GROUP)
    cs = jnp.cumsum(ug, axis=1)
    count = jnp.arange(1, S + 1, dtype=jnp.float32)
    means = []
    for g, w in enumerate(POOL_WINDOWS):
        c_g = cs[:, :, g]
        lagged = jnp.pad(c_g[:, :-w], ((0, 0), (w, 0), (0, 0)))
        means.append((c_g - lagged) / jnp.minimum(count, float(w))[None, :, None])
    pooled = jnp.stack(means, axis=2) - ug
    y = jnp.einsum('bsgc,gcd->bsgd', pooled, pool_w.astype(jnp.float32)).reshape(Bsz, S, D_POOL)
    return y * pool_scale.astype(jnp.float32)


def _hybrid_mixer(h, rel_bias, w_in, w_out, a_re, a_im, log_dt, b_re, b_im, c_re, c_im,
                  d_skip, glu_w, glu_b, pool_w, pool_scale):
    Bsz, S, _ = h.shape
    z = h @ w_in
    q, k, v, u_ssm, u_pool = jnp.split(z, IN_SPLITS, axis=-1)
    heads = lambda t: t.reshape(Bsz, S, N_ATT_HEADS, HEAD_DIM)
    y_att = _dilated_attention(heads(q) * HEAD_DIM ** -0.5, heads(k), heads(v), rel_bias)
    y_att = y_att.reshape(Bsz, S, D_ATT)
    y_ssm = _s5(u_ssm, a_re, a_im, log_dt, b_re, b_im, c_re, c_im, d_skip, glu_w, glu_b)
    y_pool = _pool_mixer(u_pool, pool_w, pool_scale)
    y = jnp.concatenate([y_att.astype(h.dtype), y_ssm.astype(h.dtype), y_pool.astype(h.dtype)], axis=-1)
    return y @ w_out


def _fwd_setup_inputs(seed: int = 0) -> dict:
    key = jax.random.key(seed)
    ks = jax.random.split(key, 26)
    f32 = jnp.float32
    nrm = lambda i, shape, std: std * jax.random.normal(ks[i], shape, f32)
    L, G, P = DEPTH, N_SSM_GROUPS, SSM_STATE
    n = jnp.arange(P, dtype=f32)
    return {
        "x": nrm(0, (BATCH, SEQ, D_MODEL), 1.0),
        "c": nrm(1, (BATCH, D_MODEL), 1.0),
        "rel_bias": nrm(2, (N_BUCKETS, N_ATT_HEADS), 0.1),
        "ada_w": nrm(3, (L, D_MODEL, 9 * D_MODEL), D_MODEL ** -0.5),
        "ada_b": nrm(4, (L, 9 * D_MODEL), 0.02),
        "ln_g": 1.0 + nrm(5, (L, 3, D_MODEL), 0.02),
        "ln_b": nrm(6, (L, 3, D_MODEL), 0.02),
        "ffn_w_gate": nrm(7, (L, 2, D_MODEL, D_FF), D_MODEL ** -0.5),
        "ffn_w_up": nrm(8, (L, 2, D_MODEL, D_FF), D_MODEL ** -0.5),
        "ffn_w_down": nrm(9, (L, 2, D_FF, D_MODEL), BETA * D_FF ** -0.5),
        "w_in": nrm(10, (L, D_MODEL, D_IN), D_MODEL ** -0.5),
        "w_out": nrm(11, (L, D_MIX, D_MODEL), BETA * D_MIX ** -0.5),
        "ssm_a_re": -0.5 + nrm(12, (L, G, P), 0.01),
        "ssm_a_im": math.pi * n + nrm(13, (L, G, P), 0.01),
        "ssm_log_dt": jax.random.uniform(ks[14], (L, G), f32, math.log(1e-3), math.log(1e-1)),
        "ssm_b_re": nrm(15, (L, G, P, SSM_GROUP), (2 * SSM_GROUP) ** -0.5),
        "ssm_b_im": nrm(16, (L, G, P, SSM_GROUP), (2 * SSM_GROUP) ** -0.5),
        "ssm_c_re": nrm(17, (L, G, SSM_GROUP, P), (2 * P) ** -0.5),
        "ssm_c_im": nrm(18, (L, G, SSM_GROUP, P), (2 * P) ** -0.5),
        "ssm_d": nrm(19, (L, D_SSM), 1.0),
        "glu_w": nrm(20, (L, D_SSM, D_SSM), D_SSM ** -0.5),
        "glu_b": nrm(21, (L, D_SSM), 0.02),
        "pool_w": nrm(22, (L, len(POOL_WINDOWS), POOL_GROUP, POOL_GROUP), POOL_GROUP ** -0.5),
        "pool_scale": 1.0 + nrm(23, (L, D_POOL), 0.02),
    }


def _fwd_reference(x, c, rel_bias, ada_w, ada_b, ln_g, ln_b, ffn_w_gate, ffn_w_up, ffn_w_down,
              w_in, w_out, ssm_a_re, ssm_a_im, ssm_log_dt, ssm_b_re, ssm_b_im, ssm_c_re,
              ssm_c_im, ssm_d, glu_w, glu_b, pool_w, pool_scale):
    Bsz = x.shape[0]
    cond = jax.nn.silu(c)
    for l in range(DEPTH):
        mod = (cond @ ada_w[l] + ada_b[l]).reshape(Bsz, 3, 3, 1, D_MODEL)
        h = _modulate(x, mod[:, 0, 0], mod[:, 0, 1])
        f = _swiglu(h, ffn_w_gate[l, 0], ffn_w_up[l, 0], ffn_w_down[l, 0])
        x = _layernorm_affine(ALPHA * x + FFN_RES * mod[:, 0, 2] * f, ln_g[l, 0], ln_b[l, 0])
        h = _modulate(x, mod[:, 1, 0], mod[:, 1, 1])
        y = _hybrid_mixer(h, rel_bias, w_in[l], w_out[l], ssm_a_re[l], ssm_a_im[l], ssm_log_dt[l],
                          ssm_b_re[l], ssm_b_im[l], ssm_c_re[l], ssm_c_im[l], ssm_d[l],
                          glu_w[l], glu_b[l], pool_w[l], pool_scale[l])
        x = _layernorm_affine(ALPHA * x + mod[:, 1, 2] * y, ln_g[l, 1], ln_b[l, 1])
        h = _modulate(x, mod[:, 2, 0], mod[:, 2, 1])
        f = _swiglu(h, ffn_w_gate[l, 1], ffn_w_up[l, 1], ffn_w_down[l, 1])
        x = _layernorm_affine(ALPHA * x + FFN_RES * mod[:, 2, 2] * f, ln_g[l, 2], ln_b[l, 2])
    return x


import jax as _jax
import jax.numpy as _jnp

TWIN_FORMAT = 'train_step'
FWD_PARAMS = ['x', 'c', 'rel_bias', 'ada_w', 'ada_b', 'ln_g', 'ln_b', 'ffn_w_gate', 'ffn_w_up', 'ffn_w_down', 'w_in', 'w_out', 'ssm_a_re', 'ssm_a_im', 'ssm_log_dt', 'ssm_b_re', 'ssm_b_im', 'ssm_c_re', 'ssm_c_im', 'ssm_d', 'glu_w', 'glu_b', 'pool_w', 'pool_scale']
TWIN_WEIGHTS = ['rel_bias', 'ada_w', 'ada_b', 'ln_g', 'ln_b', 'ffn_w_gate', 'ffn_w_up', 'ffn_w_down', 'w_in', 'w_out', 'ssm_a_re', 'ssm_a_im', 'ssm_log_dt', 'ssm_b_re', 'ssm_b_im', 'ssm_c_re', 'ssm_c_im', 'ssm_d', 'glu_w', 'glu_b', 'pool_w', 'pool_scale']
TWIN_DIFF_INPUT = 'x'
TWIN_INPUTS = ['x', 'c', 'rel_bias', 'ada_w', 'ada_b', 'ln_g', 'ln_b', 'ffn_w_gate', 'ffn_w_up', 'ffn_w_down', 'w_in', 'w_out', 'ssm_a_re', 'ssm_a_im', 'ssm_log_dt', 'ssm_b_re', 'ssm_b_im', 'ssm_c_re', 'ssm_c_im', 'ssm_d', 'glu_w', 'glu_b', 'pool_w', 'pool_scale', 'loss_target', 'm_rel_bias', 'm_ada_w', 'm_ada_b', 'm_ln_g', 'm_ln_b', 'm_ffn_w_gate', 'm_ffn_w_up', 'm_ffn_w_down', 'm_w_in', 'm_w_out', 'm_ssm_a_re', 'm_ssm_a_im', 'm_ssm_log_dt', 'm_ssm_b_re', 'm_ssm_b_im', 'm_ssm_c_re', 'm_ssm_c_im', 'm_ssm_d', 'm_glu_w', 'm_glu_b', 'm_pool_w', 'm_pool_scale', 'v_rel_bias', 'v_ada_w', 'v_ada_b', 'v_ln_g', 'v_ln_b', 'v_ffn_w_gate', 'v_ffn_w_up', 'v_ffn_w_down', 'v_w_in', 'v_w_out', 'v_ssm_a_re', 'v_ssm_a_im', 'v_ssm_log_dt', 'v_ssm_b_re', 'v_ssm_b_im', 'v_ssm_c_re', 'v_ssm_c_im', 'v_ssm_d', 'v_glu_w', 'v_glu_b', 'v_pool_w', 'v_pool_scale']
TWIN_OUTPUTS = ['loss', 'grad_x', 'grad_rel_bias', 'grad_ada_w', 'grad_ada_b', 'grad_ln_g', 'grad_ln_b', 'grad_ffn_w_gate', 'grad_ffn_w_up', 'grad_ffn_w_down', 'grad_w_in', 'grad_w_out', 'grad_ssm_a_re', 'grad_ssm_a_im', 'grad_ssm_log_dt', 'grad_ssm_b_re', 'grad_ssm_b_im', 'grad_ssm_c_re', 'grad_ssm_c_im', 'grad_ssm_d', 'grad_glu_w', 'grad_glu_b', 'grad_pool_w', 'grad_pool_scale', 'delta_rel_bias', 'delta_ada_w', 'delta_ada_b', 'delta_ln_g', 'delta_ln_b', 'delta_ffn_w_gate', 'delta_ffn_w_up', 'delta_ffn_w_down', 'delta_w_in', 'delta_w_out', 'delta_ssm_a_re', 'delta_ssm_a_im', 'delta_ssm_log_dt', 'delta_ssm_b_re', 'delta_ssm_b_im', 'delta_ssm_c_re', 'delta_ssm_c_im', 'delta_ssm_d', 'delta_glu_w', 'delta_glu_b', 'delta_pool_w', 'delta_pool_scale', 'new_m_rel_bias', 'new_m_ada_w', 'new_m_ada_b', 'new_m_ln_g', 'new_m_ln_b', 'new_m_ffn_w_gate', 'new_m_ffn_w_up', 'new_m_ffn_w_down', 'new_m_w_in', 'new_m_w_out', 'new_m_ssm_a_re', 'new_m_ssm_a_im', 'new_m_ssm_log_dt', 'new_m_ssm_b_re', 'new_m_ssm_b_im', 'new_m_ssm_c_re', 'new_m_ssm_c_im', 'new_m_ssm_d', 'new_m_glu_w', 'new_m_glu_b', 'new_m_pool_w', 'new_m_pool_scale', 'new_v_rel_bias', 'new_v_ada_w', 'new_v_ada_b', 'new_v_ln_g', 'new_v_ln_b', 'new_v_ffn_w_gate', 'new_v_ffn_w_up', 'new_v_ffn_w_down', 'new_v_w_in', 'new_v_w_out', 'new_v_ssm_a_re', 'new_v_ssm_a_im', 'new_v_ssm_log_dt', 'new_v_ssm_b_re', 'new_v_ssm_b_im', 'new_v_ssm_c_re', 'new_v_ssm_c_im', 'new_v_ssm_d', 'new_v_glu_w', 'new_v_glu_b', 'new_v_pool_w', 'new_v_pool_scale']
TWIN_LEAF_KINDS = {'loss': 'loss', 'grad_x': 'grad_x', 'grad_rel_bias': 'grad_w', 'grad_ada_w': 'grad_w', 'grad_ada_b': 'grad_w', 'grad_ln_g': 'grad_w', 'grad_ln_b': 'grad_w', 'grad_ffn_w_gate': 'grad_w', 'grad_ffn_w_up': 'grad_w', 'grad_ffn_w_down': 'grad_w', 'grad_w_in': 'grad_w', 'grad_w_out': 'grad_w', 'grad_ssm_a_re': 'grad_w', 'grad_ssm_a_im': 'grad_w', 'grad_ssm_log_dt': 'grad_w', 'grad_ssm_b_re': 'grad_w', 'grad_ssm_b_im': 'grad_w', 'grad_ssm_c_re': 'grad_w', 'grad_ssm_c_im': 'grad_w', 'grad_ssm_d': 'grad_w', 'grad_glu_w': 'grad_w', 'grad_glu_b': 'grad_w', 'grad_pool_w': 'grad_w', 'grad_pool_scale': 'grad_w', 'delta_rel_bias': 'delta_w', 'delta_ada_w': 'delta_w', 'delta_ada_b': 'delta_w', 'delta_ln_g': 'delta_w', 'delta_ln_b': 'delta_w', 'delta_ffn_w_gate': 'delta_w', 'delta_ffn_w_up': 'delta_w', 'delta_ffn_w_down': 'delta_w', 'delta_w_in': 'delta_w', 'delta_w_out': 'delta_w', 'delta_ssm_a_re': 'delta_w', 'delta_ssm_a_im': 'delta_w', 'delta_ssm_log_dt': 'delta_w', 'delta_ssm_b_re': 'delta_w', 'delta_ssm_b_im': 'delta_w', 'delta_ssm_c_re': 'delta_w', 'delta_ssm_c_im': 'delta_w', 'delta_ssm_d': 'delta_w', 'delta_glu_w': 'delta_w', 'delta_glu_b': 'delta_w', 'delta_pool_w': 'delta_w', 'delta_pool_scale': 'delta_w', 'new_m_rel_bias': 'new_m', 'new_m_ada_w': 'new_m', 'new_m_ada_b': 'new_m', 'new_m_ln_g': 'new_m', 'new_m_ln_b': 'new_m', 'new_m_ffn_w_gate': 'new_m', 'new_m_ffn_w_up': 'new_m', 'new_m_ffn_w_down': 'new_m', 'new_m_w_in': 'new_m', 'new_m_w_out': 'new_m', 'new_m_ssm_a_re': 'new_m', 'new_m_ssm_a_im': 'new_m', 'new_m_ssm_log_dt': 'new_m', 'new_m_ssm_b_re': 'new_m', 'new_m_ssm_b_im': 'new_m', 'new_m_ssm_c_re': 'new_m', 'new_m_ssm_c_im': 'new_m', 'new_m_ssm_d': 'new_m', 'new_m_glu_w': 'new_m', 'new_m_glu_b': 'new_m', 'new_m_pool_w': 'new_m', 'new_m_pool_scale': 'new_m', 'new_v_rel_bias': 'new_v', 'new_v_ada_w': 'new_v', 'new_v_ada_b': 'new_v', 'new_v_ln_g': 'new_v', 'new_v_ln_b': 'new_v', 'new_v_ffn_w_gate': 'new_v', 'new_v_ffn_w_up': 'new_v', 'new_v_ffn_w_down': 'new_v', 'new_v_w_in': 'new_v', 'new_v_w_out': 'new_v', 'new_v_ssm_a_re': 'new_v', 'new_v_ssm_a_im': 'new_v', 'new_v_ssm_log_dt': 'new_v', 'new_v_ssm_b_re': 'new_v', 'new_v_ssm_b_im': 'new_v', 'new_v_ssm_c_re': 'new_v', 'new_v_ssm_c_im': 'new_v', 'new_v_ssm_d': 'new_v', 'new_v_glu_w': 'new_v', 'new_v_glu_b': 'new_v', 'new_v_pool_w': 'new_v', 'new_v_pool_scale': 'new_v'}


def _forward(args):
    return _fwd_reference(*[args[k] for k in FWD_PARAMS])


def _output_shape():
    out = _jax.eval_shape(lambda: _forward(_fwd_setup_inputs(0)))
    return out.shape, out.dtype

N_MICROBATCH = 1
ADAM_LR = 0.001
ADAM_B1 = 0.9
ADAM_B2 = 0.999
ADAM_EPS = 1e-08
ADAM_WD = 0.01
ADAM_STEP = 10
PER_EXAMPLE_BATCH_AXIS = {'x': 0, 'c': 0, 'loss_target': 0}
SHARED_INPUTS = []
_WEIGHT_DTYPES = {'rel_bias': _jnp.float32, 'ada_w': _jnp.float32, 'ada_b': _jnp.float32, 'ln_g': _jnp.float32, 'ln_b': _jnp.float32, 'ffn_w_gate': _jnp.float32, 'ffn_w_up': _jnp.float32, 'ffn_w_down': _jnp.float32, 'w_in': _jnp.float32, 'w_out': _jnp.float32, 'ssm_a_re': _jnp.float32, 'ssm_a_im': _jnp.float32, 'ssm_log_dt': _jnp.float32, 'ssm_b_re': _jnp.float32, 'ssm_b_im': _jnp.float32, 'ssm_c_re': _jnp.float32, 'ssm_c_im': _jnp.float32, 'ssm_d': _jnp.float32, 'glu_w': _jnp.float32, 'glu_b': _jnp.float32, 'pool_w': _jnp.float32, 'pool_scale': _jnp.float32}
MOMENT_SCALE = {'rel_bias': 1.381380e-02, 'ada_w': 1.463139e-02, 'ada_b': 2.454437e-02, 'ln_g': 6.577738e+00, 'ln_b': 6.011635e-01, 'ffn_w_gate': 9.385929e-03, 'ffn_w_up': 9.181359e-03, 'ffn_w_down': 3.053349e-02, 'w_in': 1.810981e-02, 'w_out': 4.758228e-02, 'ssm_a_re': 1.601228e-03, 'ssm_a_im': 3.088864e-03, 'ssm_log_dt': 7.054890e-01, 'ssm_b_re': 1.242887e-03, 'ssm_b_im': 1.492667e-03, 'ssm_c_re': 3.076460e-03, 'ssm_c_im': 2.775869e-03, 'ssm_d': 2.253352e-02, 'glu_w': 7.358845e-03, 'glu_b': 9.317532e-03, 'pool_w': 2.808888e-02, 'pool_scale': 2.979926e-02}


def _to_microbatches(a, axis):
    t = _jnp.moveaxis(a, axis, 0)
    t = t.reshape((N_MICROBATCH, t.shape[0] // N_MICROBATCH) + t.shape[1:])
    return _jnp.moveaxis(t, 1, axis + 1)


def setup_inputs(seed: int = 0) -> dict:
    inp = _fwd_setup_inputs(seed)
    key = _jax.random.fold_in(_jax.random.key(seed), 7919)
    shape, _ = _output_shape()
    out = dict(inp)
    out["loss_target"] = _jax.random.normal(_jax.random.fold_in(key, 0), shape, _jnp.float32)
    for i, name in enumerate(TWIN_WEIGHTS):
        w = inp[name].astype(_jnp.float32)
        if MOMENT_SCALE is None:
            s = _jnp.sqrt(_jnp.mean(_jnp.square(w)) + 1e-30)
        else:
            s = MOMENT_SCALE[name]
        km, kv = _jax.random.split(_jax.random.fold_in(key, i + 1))
        out[name] = w
        out["m_" + name] = s * _jax.random.normal(km, w.shape, _jnp.float32)
        out["v_" + name] = (s * s) * _jax.random.uniform(kv, w.shape, _jnp.float32, 0.5, 1.5)
    if N_MICROBATCH > 1:
        for name, axis in PER_EXAMPLE_BATCH_AXIS.items():
            out[name] = _to_microbatches(out[name], axis)
    return {'x': out['x'], 'c': out['c'], 'rel_bias': out['rel_bias'], 'ada_w': out['ada_w'], 'ada_b': out['ada_b'], 'ln_g': out['ln_g'], 'ln_b': out['ln_b'], 'ffn_w_gate': out['ffn_w_gate'], 'ffn_w_up': out['ffn_w_up'], 'ffn_w_down': out['ffn_w_down'], 'w_in': out['w_in'], 'w_out': out['w_out'], 'ssm_a_re': out['ssm_a_re'], 'ssm_a_im': out['ssm_a_im'], 'ssm_log_dt': out['ssm_log_dt'], 'ssm_b_re': out['ssm_b_re'], 'ssm_b_im': out['ssm_b_im'], 'ssm_c_re': out['ssm_c_re'], 'ssm_c_im': out['ssm_c_im'], 'ssm_d': out['ssm_d'], 'glu_w': out['glu_w'], 'glu_b': out['glu_b'], 'pool_w': out['pool_w'], 'pool_scale': out['pool_scale'], 'loss_target': out['loss_target'], 'm_rel_bias': out['m_rel_bias'], 'm_ada_w': out['m_ada_w'], 'm_ada_b': out['m_ada_b'], 'm_ln_g': out['m_ln_g'], 'm_ln_b': out['m_ln_b'], 'm_ffn_w_gate': out['m_ffn_w_gate'], 'm_ffn_w_up': out['m_ffn_w_up'], 'm_ffn_w_down': out['m_ffn_w_down'], 'm_w_in': out['m_w_in'], 'm_w_out': out['m_w_out'], 'm_ssm_a_re': out['m_ssm_a_re'], 'm_ssm_a_im': out['m_ssm_a_im'], 'm_ssm_log_dt': out['m_ssm_log_dt'], 'm_ssm_b_re': out['m_ssm_b_re'], 'm_ssm_b_im': out['m_ssm_b_im'], 'm_ssm_c_re': out['m_ssm_c_re'], 'm_ssm_c_im': out['m_ssm_c_im'], 'm_ssm_d': out['m_ssm_d'], 'm_glu_w': out['m_glu_w'], 'm_glu_b': out['m_glu_b'], 'm_pool_w': out['m_pool_w'], 'm_pool_scale': out['m_pool_scale'], 'v_rel_bias': out['v_rel_bias'], 'v_ada_w': out['v_ada_w'], 'v_ada_b': out['v_ada_b'], 'v_ln_g': out['v_ln_g'], 'v_ln_b': out['v_ln_b'], 'v_ffn_w_gate': out['v_ffn_w_gate'], 'v_ffn_w_up': out['v_ffn_w_up'], 'v_ffn_w_down': out['v_ffn_w_down'], 'v_w_in': out['v_w_in'], 'v_w_out': out['v_w_out'], 'v_ssm_a_re': out['v_ssm_a_re'], 'v_ssm_a_im': out['v_ssm_a_im'], 'v_ssm_log_dt': out['v_ssm_log_dt'], 'v_ssm_b_re': out['v_ssm_b_re'], 'v_ssm_b_im': out['v_ssm_b_im'], 'v_ssm_c_re': out['v_ssm_c_re'], 'v_ssm_c_im': out['v_ssm_c_im'], 'v_ssm_d': out['v_ssm_d'], 'v_glu_w': out['v_glu_w'], 'v_glu_b': out['v_glu_b'], 'v_pool_w': out['v_pool_w'], 'v_pool_scale': out['v_pool_scale']}


def _loss(weights, diff, rest, loss_target):
    with _jax.named_scope("forward"):
        args = {**rest, TWIN_DIFF_INPUT: diff, **{k: w.astype(_WEIGHT_DTYPES[k]) for k, w in weights.items()}}
        y = _forward(args)
    with _jax.named_scope("loss_head"):
        err = _jnp.square(y.astype(_jnp.float32) - loss_target)
        return 0.5 * _jnp.sum(_jnp.mean(err, axis=-1)) if err.ndim else 0.5 * err


def _adamw(w, g, m, v):
    m = ADAM_B1 * m + (1.0 - ADAM_B1) * g
    v = ADAM_B2 * v + (1.0 - ADAM_B2) * _jnp.square(g)
    m_hat = m / (1.0 - ADAM_B1 ** ADAM_STEP)
    v_hat = v / (1.0 - ADAM_B2 ** ADAM_STEP)
    delta = -ADAM_LR * (m_hat / (_jnp.sqrt(v_hat) + ADAM_EPS) + ADAM_WD * w)
    return delta, m, v


def reference(x, c, rel_bias, ada_w, ada_b, ln_g, ln_b, ffn_w_gate, ffn_w_up, ffn_w_down, w_in, w_out, ssm_a_re, ssm_a_im, ssm_log_dt, ssm_b_re, ssm_b_im, ssm_c_re, ssm_c_im, ssm_d, glu_w, glu_b, pool_w, pool_scale, loss_target, m_rel_bias, m_ada_w, m_ada_b, m_ln_g, m_ln_b, m_ffn_w_gate, m_ffn_w_up, m_ffn_w_down, m_w_in, m_w_out, m_ssm_a_re, m_ssm_a_im, m_ssm_log_dt, m_ssm_b_re, m_ssm_b_im, m_ssm_c_re, m_ssm_c_im, m_ssm_d, m_glu_w, m_glu_b, m_pool_w, m_pool_scale, v_rel_bias, v_ada_w, v_ada_b, v_ln_g, v_ln_b, v_ffn_w_gate, v_ffn_w_up, v_ffn_w_down, v_w_in, v_w_out, v_ssm_a_re, v_ssm_a_im, v_ssm_log_dt, v_ssm_b_re, v_ssm_b_im, v_ssm_c_re, v_ssm_c_im, v_ssm_d, v_glu_w, v_glu_b, v_pool_w, v_pool_scale):
    given = dict(x=x, c=c, rel_bias=rel_bias, ada_w=ada_w, ada_b=ada_b, ln_g=ln_g, ln_b=ln_b, ffn_w_gate=ffn_w_gate, ffn_w_up=ffn_w_up, ffn_w_down=ffn_w_down, w_in=w_in, w_out=w_out, ssm_a_re=ssm_a_re, ssm_a_im=ssm_a_im, ssm_log_dt=ssm_log_dt, ssm_b_re=ssm_b_re, ssm_b_im=ssm_b_im, ssm_c_re=ssm_c_re, ssm_c_im=ssm_c_im, ssm_d=ssm_d, glu_w=glu_w, glu_b=glu_b, pool_w=pool_w, pool_scale=pool_scale, loss_target=loss_target, m_rel_bias=m_rel_bias, m_ada_w=m_ada_w, m_ada_b=m_ada_b, m_ln_g=m_ln_g, m_ln_b=m_ln_b, m_ffn_w_gate=m_ffn_w_gate, m_ffn_w_up=m_ffn_w_up, m_ffn_w_down=m_ffn_w_down, m_w_in=m_w_in, m_w_out=m_w_out, m_ssm_a_re=m_ssm_a_re, m_ssm_a_im=m_ssm_a_im, m_ssm_log_dt=m_ssm_log_dt, m_ssm_b_re=m_ssm_b_re, m_ssm_b_im=m_ssm_b_im, m_ssm_c_re=m_ssm_c_re, m_ssm_c_im=m_ssm_c_im, m_ssm_d=m_ssm_d, m_glu_w=m_glu_w, m_glu_b=m_glu_b, m_pool_w=m_pool_w, m_pool_scale=m_pool_scale, v_rel_bias=v_rel_bias, v_ada_w=v_ada_w, v_ada_b=v_ada_b, v_ln_g=v_ln_g, v_ln_b=v_ln_b, v_ffn_w_gate=v_ffn_w_gate, v_ffn_w_up=v_ffn_w_up, v_ffn_w_down=v_ffn_w_down, v_w_in=v_w_in, v_w_out=v_w_out, v_ssm_a_re=v_ssm_a_re, v_ssm_a_im=v_ssm_a_im, v_ssm_log_dt=v_ssm_log_dt, v_ssm_b_re=v_ssm_b_re, v_ssm_b_im=v_ssm_b_im, v_ssm_c_re=v_ssm_c_re, v_ssm_c_im=v_ssm_c_im, v_ssm_d=v_ssm_d, v_glu_w=v_glu_w, v_glu_b=v_glu_b, v_pool_w=v_pool_w, v_pool_scale=v_pool_scale)
    weights = {n: given[n] for n in TWIN_WEIGHTS}
    shared = {n: given[n] for n in SHARED_INPUTS}
    per_example = {n: given[n] for n in ['x', 'c']}
    grad_fn = _jax.value_and_grad(_loss, argnums=(0, 1))

    def one_microbatch(ex, loss_target):
        ex = dict(ex)
        diff = ex.pop(TWIN_DIFF_INPUT)
        return grad_fn(weights, diff, {**shared, **ex}, loss_target)

    if N_MICROBATCH == 1:
        loss, (grad_w, grad_x) = one_microbatch(per_example, given["loss_target"])
    else:
        def body(carry, xs):
            loss_sum, grad_sum = carry
            l_k, (gw_k, gx_k) = one_microbatch(xs[0], xs[1])
            with _jax.named_scope("update"):
                return (loss_sum + l_k, _jax.tree.map(_jnp.add, grad_sum, gw_k)), gx_k

        init = (_jnp.zeros((), _jnp.float32), _jax.tree.map(_jnp.zeros_like, weights))
        (loss, grad_w), grad_x = _jax.lax.scan(body, init, (per_example, given["loss_target"]))
    with _jax.named_scope("update"):
        delta_w, new_m, new_v = {}, {}, {}
        for n in TWIN_WEIGHTS:
            delta_w[n], new_m[n], new_v[n] = _adamw(weights[n], grad_w[n], given["m_" + n], given["v_" + n])
    return (loss, grad_x, *[grad_w[n] for n in TWIN_WEIGHTS], *[delta_w[n] for n in TWIN_WEIGHTS],
            *[new_m[n] for n in TWIN_WEIGHTS], *[new_v[n] for n in TWIN_WEIGHTS])
```

```python
import functools
import math

import numpy as np
import jax
import jax.numpy as jnp
from jax import lax
from jax.experimental import pallas as pl
from jax.experimental.pallas import tpu as pltpu

f32 = jnp.float32
bf16 = jnp.bfloat16
MESH = pl.DeviceIdType.MESH

D_MODEL = 1024
SEQ = 2048
DEPTH = 2
HEAD_DIM = 64
N_HEADS = 8
D_ATT = 512
DILATIONS = (1, 4, 16)
BLOCKS_PER_RESIDUE = (16, 4, 1)
ATT_BLOCK = 128
N_UNITS = SEQ // ATT_BLOCK
N_GROUPS = 16
SSM_GROUP = 16
SSM_STATE = 64
D_SSM = 256
D_STATE = N_GROUPS * SSM_STATE
POOL_WINDOWS = (2, 4, 8, 16)
POOL_GROUP = 64
D_POOL = 256
POOL_HALO = 16
D_FF = 2816
N_BUCKETS = 32
MAX_DISTANCE = 2048
ALPHA = (2 * DEPTH) ** 0.25
FFN_RES = 0.5
LN_EPS = 1e-5
NEG = -1e30
N_CHIPS = 4
FF_SHARD = D_FF // N_CHIPS
SCAN_SEG = 8
SCAN_STEPS = SEQ // SCAN_SEG

ADAM_LR, ADAM_B1, ADAM_B2, ADAM_EPS, ADAM_WD, ADAM_STEP = 0.001, 0.9, 0.999, 1e-08, 0.01, 10

TOK_TILE = 512


def _cp(dims=None, vmem_mb=None):
    kw = {}
    if dims is not None:
        kw["dimension_semantics"] = dims
    if vmem_mb is not None:
        kw["vmem_limit_bytes"] = vmem_mb << 20
    return pltpu.CompilerParams(**kw)


def _dot(a, b):
    return jnp.dot(a, b, preferred_element_type=f32)


def _dot_nt(a, b):
    return lax.dot_general(a, b, (((1,), (1,)), ((), ())), preferred_element_type=f32)


def _dot_tn(a, b):
    return lax.dot_general(a, b, (((0,), (0,)), ((), ())), preferred_element_type=f32)


def _ln_stats(v):
    mu = jnp.mean(v, -1, keepdims=True)
    d = v - mu
    var = jnp.mean(d * d, -1, keepdims=True)
    rstd = lax.rsqrt(var + LN_EPS)
    return d * rstd, rstd


def _ln_bwd(dxh, xh, rstd):
    return rstd * (dxh - jnp.mean(dxh, -1, keepdims=True) - xh * jnp.mean(dxh * xh, -1, keepdims=True))


_GELU_C = math.sqrt(2.0 / math.pi)


def _gelu(y):
    return 0.5 * y * (1.0 + jnp.tanh(_GELU_C * (y + 0.044715 * y * y * y)))


def _gelu_grad(y):
    t = jnp.tanh(_GELU_C * (y + 0.044715 * y * y * y))
    return 0.5 * (1.0 + t) + 0.5 * y * (1.0 - t * t) * (_GELU_C * (1.0 + 3 * 0.044715 * y * y))


def _full(shape):
    return pl.BlockSpec(shape, lambda *_: (0,) * len(shape))


def _ffn_fwd(x, mod3, wg, wu, wd, ls, lng, lnb):
    S, D = x.shape
    Fs = wg.shape[-1]
    ts = TOK_TILE

    def body(x_ref, mod_ref, wg_ref, wu_ref, wd_ref, lng_ref, lnb_ref, xo_ref, f_ref, g_ref, u_ref, h_sc, acc_sc):
        j = pl.program_id(1)

        @pl.when(j == 0)
        def _():
            xh, _ = _ln_stats(x_ref[...])
            h_sc[...] = (xh * (1.0 + mod_ref[1:2, :]) + mod_ref[0:1, :]).astype(bf16)
            acc_sc[...] = jnp.zeros_like(acc_sc)

        h = h_sc[...]
        g = _dot(h, wg_ref[0, 0])
        u = _dot(h, wu_ref[0, 0])
        g_ref[0] = g.astype(bf16)
        u_ref[0] = u.astype(bf16)
        a = (g * jax.nn.sigmoid(g) * u).astype(bf16)
        acc_sc[...] += _dot(a, wd_ref[0, 0])

        @pl.when(j == N_CHIPS - 1)
        def _():
            f = acc_sc[...]
            f_ref[...] = f
            r = ALPHA * x_ref[...] + (FFN_RES * mod_ref[2:3, :]) * f
            rh, _ = _ln_stats(r)
            xo_ref[...] = rh * lng_ref[...] + lnb_ref[...]

    tok = pl.BlockSpec((ts, D), lambda i, j: (i, 0))
    wcol = pl.BlockSpec((1, 1, D, Fs), lambda i, j: (j, ls, 0, 0))
    wrow = pl.BlockSpec((1, 1, Fs, D), lambda i, j: (j, ls, 0, 0))
    hid = pl.BlockSpec((1, ts, Fs), lambda i, j: (j, i, 0))
    return pl.pallas_call(
        body, name="ffn_fwd", grid=(S // ts, N_CHIPS),
        in_specs=[tok, _full((3, D)), wcol, wcol, wrow, _full((1, D)), _full((1, D))],
        out_specs=[tok, tok, hid, hid],
        out_shape=[jax.ShapeDtypeStruct((S, D), f32), jax.ShapeDtypeStruct((S, D), f32),
                   jax.ShapeDtypeStruct((N_CHIPS, S, Fs), bf16), jax.ShapeDtypeStruct((N_CHIPS, S, Fs), bf16)],
        scratch_shapes=[pltpu.VMEM((ts, D), bf16), pltpu.VMEM((ts, D), f32)],
        compiler_params=_cp(("parallel", "arbitrary"), 56),
    )(x, mod3, wg, wu, wd, lng, lnb)


def _ffn_bwd(dxo, x, f, g, u, mod3, wg, wu, wd, ls, lng):
    S, D = x.shape
    Fs = wg.shape[-1]
    ts = TOK_TILE

    def body(dxo_ref, x_ref, f_ref, g_ref, u_ref, mod_ref, wg_ref, wu_ref, wd_ref, lng_ref,
             dx_ref, dg_ref, du_ref, a_ref, h_ref, df_ref, dmod_ref, dlng_ref, dlnb_ref,
             dr_sc, df_sc, acc_sc):
        i = pl.program_id(0)
        j = pl.program_id(1)

        @pl.when((i == 0) & (j == 0))
        def _():
            dmod_ref[...] = jnp.zeros_like(dmod_ref)
            dlng_ref[...] = jnp.zeros_like(dlng_ref)
            dlnb_ref[...] = jnp.zeros_like(dlnb_ref)

        @pl.when(j == 0)
        def _():
            xv = x_ref[...]
            fv = f_ref[...]
            gate = mod_ref[2:3, :]
            rh, rstd = _ln_stats(ALPHA * xv + (FFN_RES * gate) * fv)
            dy = dxo_ref[...]
            dlng_ref[...] += jnp.sum(dy * rh, 0, keepdims=True)
            dlnb_ref[...] += jnp.sum(dy, 0, keepdims=True)
            dr = _ln_bwd(dy * lng_ref[...], rh, rstd)
            dr_sc[...] = dr
            dmod_ref[2:3, :] += jnp.sum(FFN_RES * dr * fv, 0, keepdims=True)
            df = ((FFN_RES * gate) * dr).astype(bf16)
            df_sc[...] = df
            df_ref[...] = df
            xh, _ = _ln_stats(xv)
            h_ref[...] = (xh * (1.0 + mod_ref[1:2, :]) + mod_ref[0:1, :]).astype(bf16)
            acc_sc[...] = jnp.zeros_like(acc_sc)

        da = _dot_nt(df_sc[...], wd_ref[0, 0])
        gv = g_ref[0].astype(f32)
        uv = u_ref[0].astype(f32)
        sg = jax.nn.sigmoid(gv)
        si = gv * sg
        a_ref[0] = (si * uv).astype(bf16)
        dgv = (da * uv * (sg * (1.0 + gv * (1.0 - sg)))).astype(bf16)
        duv = (da * si).astype(bf16)
        dg_ref[0] = dgv
        du_ref[0] = duv
        acc_sc[...] += _dot_nt(dgv, wg_ref[0, 0]) + _dot_nt(duv, wu_ref[0, 0])

        @pl.when(j == N_CHIPS - 1)
        def _():
            dh = acc_sc[...]
            xh, rstd0 = _ln_stats(x_ref[...])
            dmod_ref[0:1, :] += jnp.sum(dh, 0, keepdims=True)
            dmod_ref[1:2, :] += jnp.sum(dh * xh, 0, keepdims=True)
            dx_ref[...] = _ln_bwd(dh * (1.0 + mod_ref[1:2, :]), xh, rstd0) + ALPHA * dr_sc[...]

    tok = pl.BlockSpec((ts, D), lambda i, j: (i, 0))
    wcol = pl.BlockSpec((1, 1, D, Fs), lambda i, j: (j, ls, 0, 0))
    wrow = pl.BlockSpec((1, 1, Fs, D), lambda i, j: (j, ls, 0, 0))
    hid = pl.BlockSpec((1, ts, Fs), lambda i, j: (j, i, 0))
    hid_shape = jax.ShapeDtypeStruct((N_CHIPS, S, Fs), bf16)
    return pl.pallas_call(
        body, name="ffn_bwd", grid=(S // ts, N_CHIPS),
        in_specs=[tok, tok, tok, hid, hid, _full((3, D)), wcol, wcol, wrow, _full((1, D))],
        out_specs=[tok, hid, hid, hid, tok, tok, _full((3, D)), _full((1, D)), _full((1, D))],
        out_shape=[jax.ShapeDtypeStruct((S, D), f32), hid_shape, hid_shape, hid_shape,
                   jax.ShapeDtypeStruct((S, D), bf16), jax.ShapeDtypeStruct((S, D), bf16),
                   jax.ShapeDtypeStruct((3, D), f32), jax.ShapeDtypeStruct((1, D), f32), jax.ShapeDtypeStruct((1, D), f32)],
        scratch_shapes=[pltpu.VMEM((ts, D), f32), pltpu.VMEM((ts, D), bf16), pltpu.VMEM((ts, D), f32)],
        compiler_params=_cp(("arbitrary", "arbitrary"), 56),
    )(dxo, x, f, g, u, mod3, wg, wu, wd, lng)


def _ffn_wgrad(h, dg, du, a, df, gwg, gwu, gwd, ls):
    S, D = h.shape
    Fs = dg.shape[-1]
    tk = TOK_TILE
    nk = S // tk

    def body(h_ref, dg_ref, du_ref, a_ref, df_ref, _g0, _g1, _g2, gwg_ref, gwu_ref, gwd_ref, ag_sc, au_sc, ad_sc):
        k = pl.program_id(1)

        @pl.when(k == 0)
        def _():
            ag_sc[...] = jnp.zeros_like(ag_sc)
            au_sc[...] = jnp.zeros_like(au_sc)
            ad_sc[...] = jnp.zeros_like(ad_sc)

        hv = h_ref[...]
        ag_sc[...] += _dot_tn(hv, dg_ref[0])
        au_sc[...] += _dot_tn(hv, du_ref[0])
        ad_sc[...] += _dot_tn(a_ref[0], df_ref[...])

        @pl.when(k == nk - 1)
        def _():
            gwg_ref[0, 0] = ag_sc[...].astype(bf16)
            gwu_ref[0, 0] = au_sc[...].astype(bf16)
            gwd_ref[0, 0] = ad_sc[...].astype(bf16)

    tok = pl.BlockSpec((tk, D), lambda p, k: (k, 0))
    hid = pl.BlockSpec((1, tk, Fs), lambda p, k: (p, k, 0))
    anyspec = pl.BlockSpec(memory_space=pl.ANY)
    ocol = pl.BlockSpec((1, 1, D, Fs), lambda p, k: (p, ls, 0, 0))
    orow = pl.BlockSpec((1, 1, Fs, D), lambda p, k: (p, ls, 0, 0))
    return pl.pallas_call(
        body, name="ffn_wgrad", grid=(N_CHIPS, nk),
        in_specs=[tok, hid, hid, hid, tok, anyspec, anyspec, anyspec],
        out_specs=[ocol, ocol, orow],
        out_shape=[jax.ShapeDtypeStruct(gwg.shape, bf16), jax.ShapeDtypeStruct(gwu.shape, bf16),
                   jax.ShapeDtypeStruct(gwd.shape, bf16)],
        scratch_shapes=[pltpu.VMEM((D, Fs), f32), pltpu.VMEM((D, Fs), f32), pltpu.VMEM((Fs, D), f32)],
        input_output_aliases={5: 0, 6: 1, 7: 2},
        compiler_params=_cp(("parallel", "arbitrary"), 48),
    )(h, dg, du, a, df, gwg, gwu, gwd)


def _mix_in_fwd(x, mod3, w_in, l):
    S, D = x.shape
    N = w_in.shape[-1]
    ts = TOK_TILE

    def body(x_ref, mod_ref, w_ref, z_ref, h_ref):
        @pl.when(pl.program_id(1) == 0)
        def _():
            xh, _ = _ln_stats(x_ref[...])
            h_ref[...] = (xh * (1.0 + mod_ref[1:2, :]) + mod_ref[0:1, :]).astype(bf16)

        z_ref[0] = _dot(h_ref[...], w_ref[0, 0])

    tok = pl.BlockSpec((ts, D), lambda i, j: (i, 0))
    return pl.pallas_call(
        body, name="mix_in_fwd", grid=(S // ts, N_CHIPS),
        in_specs=[tok, _full((3, D)), pl.BlockSpec((1, 1, D, N), lambda i, j: (j, l, 0, 0))],
        out_specs=[pl.BlockSpec((1, ts, N), lambda i, j: (j, i, 0)), tok],
        out_shape=[jax.ShapeDtypeStruct((N_CHIPS, S, N), f32), jax.ShapeDtypeStruct((S, D), bf16)],
        compiler_params=_cp(("parallel", "arbitrary"), 40),
    )(x, mod3, w_in)


def _mix_in_bwd(dz, dx_res, x, mod3, w_in, l):
    S, D = x.shape
    N = w_in.shape[-1]
    ts = TOK_TILE

    def body(dz_ref, dxr_ref, x_ref, mod_ref, w_ref, dx_ref, dmod_ref, acc_sc):
        i = pl.program_id(0)
        j = pl.program_id(1)

        @pl.when((i == 0) & (j == 0))
        def _():
            dmod_ref[...] = jnp.zeros_like(dmod_ref)

        @pl.when(j == 0)
        def _():
            acc_sc[...] = jnp.zeros_like(acc_sc)

        acc_sc[...] += _dot_nt(dz_ref[0], w_ref[0, 0])

        @pl.when(j == N_CHIPS - 1)
        def _():
            dh = acc_sc[...]
            xh, rstd0 = _ln_stats(x_ref[...])
            dmod_ref[0:1, :] += jnp.sum(dh, 0, keepdims=True)
            dmod_ref[1:2, :] += jnp.sum(dh * xh, 0, keepdims=True)
            dx_ref[...] = _ln_bwd(dh * (1.0 + mod_ref[1:2, :]), xh, rstd0) + dxr_ref[...]

    tok = pl.BlockSpec((ts, D), lambda i, j: (i, 0))
    return pl.pallas_call(
        body, name="mix_in_bwd", grid=(S // ts, N_CHIPS),
        in_specs=[pl.BlockSpec((1, ts, N), lambda i, j: (j, i, 0)), tok, tok, _full((3, D)),
                  pl.BlockSpec((1, 1, D, N), lambda i, j: (j, l, 0, 0))],
        out_specs=[tok, _full((3, D))],
        out_shape=[jax.ShapeDtypeStruct((S, D), f32), jax.ShapeDtypeStruct((3, D), f32)],
        scratch_shapes=[pltpu.VMEM((ts, D), f32)],
        compiler_params=_cp(("arbitrary", "arbitrary"), 40),
    )(dz, dx_res, x, mod3, w_in)


def _mix_in_wgrad(h, dz, gw, l):
    S, D = h.shape
    N = dz.shape[-1]
    tk = TOK_TILE
    nk = S // tk

    def body(h_ref, dz_ref, _g, gw_ref, acc_sc):
        k = pl.program_id(1)

        @pl.when(k == 0)
        def _():
            acc_sc[...] = jnp.zeros_like(acc_sc)

        acc_sc[...] += _dot_tn(h_ref[...], dz_ref[0])

        @pl.when(k == nk - 1)
        def _():
            gw_ref[0, 0] = acc_sc[...].astype(bf16)

    return pl.pallas_call(
        body, name="mix_in_wgrad", grid=(N_CHIPS, nk),
        in_specs=[pl.BlockSpec((tk, D), lambda p, k: (k, 0)), pl.BlockSpec((1, tk, N), lambda p, k: (p, k, 0)),
                  pl.BlockSpec(memory_space=pl.ANY)],
        out_specs=pl.BlockSpec((1, 1, D, N), lambda p, k: (p, l, 0, 0)),
        out_shape=jax.ShapeDtypeStruct(gw.shape, bf16),
        scratch_shapes=[pltpu.VMEM((D, N), f32)],
        input_output_aliases={2: 0},
        compiler_params=_cp(("parallel", "arbitrary"), 40),
    )(h, dz, gw)


def _mix_out_fwd(x, y_att, y_ssm, y_pool, mod3, w_out, l, lng, lnb):
    S, D = x.shape
    ts = TOK_TILE

    def body(x_ref, ya_ref, ys_ref, yp_ref, mod_ref, w_ref, lng_ref, lnb_ref, xo_ref, y_ref):
        ya = ya_ref[...].astype(bf16)
        y = (_dot(ya[:, 0:256], w_ref[0, 0]) + _dot(ya[:, 256:512], w_ref[1, 0])
             + _dot(ys_ref[...].astype(bf16), w_ref[2, 0]) + _dot(yp_ref[...].astype(bf16), w_ref[3, 0]))
        y_ref[...] = y
        rh, _ = _ln_stats(ALPHA * x_ref[...] + mod_ref[2:3, :] * y)
        xo_ref[...] = rh * lng_ref[...] + lnb_ref[...]

    tok = pl.BlockSpec((ts, D), lambda i: (i, 0))
    return pl.pallas_call(
        body, name="mix_out_fwd", grid=(S // ts,),
        in_specs=[tok, pl.BlockSpec((ts, D_ATT), lambda i: (i, 0)), pl.BlockSpec((ts, D_SSM), lambda i: (i, 0)),
                  pl.BlockSpec((ts, D_POOL), lambda i: (i, 0)), _full((3, D)),
                  pl.BlockSpec((N_CHIPS, 1, 256, D), lambda i: (0, l, 0, 0)), _full((1, D)), _full((1, D))],
        out_specs=[tok, tok],
        out_shape=[jax.ShapeDtypeStruct((S, D), f32), jax.ShapeDtypeStruct((S, D), f32)],
        compiler_params=_cp(("parallel",), 40),
    )(x, y_att, y_ssm, y_pool, mod3, w_out, lng, lnb)


def _mix_out_bwd(dxo, x, y, y_att, y_ssm, y_pool, mod3, w_out, l, lng, gw_out):
    S, D = x.shape
    ts = TOK_TILE
    nt = S // ts

    def body(dxo_ref, x_ref, y_ref, ya_ref, ys_ref, yp_ref, mod_ref, w_ref, lng_ref, _g,
             dxr_ref, da_ref, ds_ref, dp_ref, dgate_ref, dlng_ref, dlnb_ref, gw_ref, acc_sc):
        i = pl.program_id(0)

        @pl.when(i == 0)
        def _():
            dgate_ref[...] = jnp.zeros_like(dgate_ref)
            dlng_ref[...] = jnp.zeros_like(dlng_ref)
            dlnb_ref[...] = jnp.zeros_like(dlnb_ref)
            acc_sc[...] = jnp.zeros_like(acc_sc)

        gate = mod_ref[2:3, :]
        yv = y_ref[...]
        rh, rstd = _ln_stats(ALPHA * x_ref[...] + gate * yv)
        dy_out = dxo_ref[...]
        dlng_ref[...] += jnp.sum(dy_out * rh, 0, keepdims=True)
        dlnb_ref[...] += jnp.sum(dy_out, 0, keepdims=True)
        dr = _ln_bwd(dy_out * lng_ref[...], rh, rstd)
        dxr_ref[...] = ALPHA * dr
        dgate_ref[...] += jnp.sum(dr * yv, 0, keepdims=True)
        dy = (gate * dr).astype(bf16)
        da_ref[:, 0:256] = _dot_nt(dy, w_ref[0, 0])
        da_ref[:, 256:512] = _dot_nt(dy, w_ref[1, 0])
        ds_ref[...] = _dot_nt(dy, w_ref[2, 0])
        dp_ref[...] = _dot_nt(dy, w_ref[3, 0])
        ya = ya_ref[...].astype(bf16)
        acc_sc[0] += _dot_tn(ya[:, 0:256], dy)
        acc_sc[1] += _dot_tn(ya[:, 256:512], dy)
        acc_sc[2] += _dot_tn(ys_ref[...].astype(bf16), dy)
        acc_sc[3] += _dot_tn(yp_ref[...].astype(bf16), dy)

        @pl.when(i == nt - 1)
        def _():
            gw_ref[:, 0] = acc_sc[...].astype(bf16)

    tok = pl.BlockSpec((ts, D), lambda i: (i, 0))
    t512 = pl.BlockSpec((ts, D_ATT), lambda i: (i, 0))
    t256 = pl.BlockSpec((ts, 256), lambda i: (i, 0))
    wspec = pl.BlockSpec((N_CHIPS, 1, 256, D), lambda i: (0, l, 0, 0))
    return pl.pallas_call(
        body, name="mix_out_bwd", grid=(nt,),
        in_specs=[tok, tok, tok, t512, t256, t256, _full((3, D)), wspec, _full((1, D)), pl.BlockSpec(memory_space=pl.ANY)],
        out_specs=[tok, t512, t256, t256, _full((1, D)), _full((1, D)), _full((1, D)), wspec],
        out_shape=[jax.ShapeDtypeStruct((S, D), f32), jax.ShapeDtypeStruct((S, D_ATT), f32),
                   jax.ShapeDtypeStruct((S, D_SSM), f32), jax.ShapeDtypeStruct((S, D_POOL), f32),
                   jax.ShapeDtypeStruct((1, D), f32), jax.ShapeDtypeStruct((1, D), f32), jax.ShapeDtypeStruct((1, D), f32),
                   jax.ShapeDtypeStruct(gw_out.shape, bf16)],
        scratch_shapes=[pltpu.VMEM((N_CHIPS, 256, D), f32)],
        input_output_aliases={9: 7},
        compiler_params=_cp(("arbitrary",), 48),
    )(dxo, x, y, y_att, y_ssm, y_pool, mod3, w_out, lng, gw_out)


def _t5_bucket(dist):
    max_exact = N_BUCKETS // 2
    d = np.maximum(dist, 1).astype(np.float32)
    large = max_exact + (np.log(d / max_exact) / math.log(MAX_DISTANCE / max_exact)
                         * (N_BUCKETS - max_exact)).astype(np.int32)
    large = np.minimum(large, N_BUCKETS - 1)
    return np.where(dist < max_exact, dist, large).astype(np.int32)


def _bucket_table():
    q = ATT_BLOCK
    i = np.arange(q)[:, None]
    j = np.arange(2 * q)[None, :]
    r = i + q - j
    in_band = (r >= 0) & (r <= q)
    tabs = [np.where(in_band, _t5_bucket(np.clip(r, 0, None) * d), -1) for d in DILATIONS]
    return np.stack(tabs).astype(np.int32)


def _bias_fwd(rel_bias, table):
    def body(rb_ref, tab_ref, out_ref):
        for b in range(3):
            tb = tab_ref[b]
            for h in range(N_HEADS):
                def pick(k, acc):
                    return jnp.where(tb == k, rb_ref[k, h], acc)
                out_ref[b, h] = lax.fori_loop(0, N_BUCKETS, pick, jnp.where(tb < 0, NEG, 0.0).astype(f32))

    return pl.pallas_call(
        body, name="bias_fwd",
        in_specs=[pl.BlockSpec(memory_space=pltpu.SMEM), pl.BlockSpec(memory_space=pltpu.VMEM)],
        out_specs=pl.BlockSpec(memory_space=pltpu.VMEM),
        out_shape=jax.ShapeDtypeStruct((3, N_HEADS, ATT_BLOCK, 2 * ATT_BLOCK), f32),
    )(rel_bias, table)


def _bias_bwd(dbias, table):
    def body(db_ref, tab_ref, out_ref):
        def per_bucket(k, c):
            for h in range(N_HEADS):
                tot = jnp.zeros((), f32)
                for b in range(3):
                    tot = tot + jnp.sum(jnp.where(tab_ref[b] == k, db_ref[b, h], 0.0))
                out_ref[k, h] = tot
            return c
        lax.fori_loop(0, N_BUCKETS, per_bucket, 0)

    return pl.pallas_call(
        body, name="bias_bwd",
        in_specs=[pl.BlockSpec(memory_space=pltpu.VMEM), pl.BlockSpec(memory_space=pltpu.VMEM)],
        out_specs=pl.BlockSpec(memory_space=pltpu.SMEM),
        out_shape=jax.ShapeDtypeStruct((N_BUCKETS, N_HEADS), f32),
    )(dbias, table)


def _att_scores(q_ref, k_ref, b_ref, r0, h, valid_prev):
    cs = pl.ds(HEAD_DIM * h, HEAD_DIM)
    q = q_ref[0, pl.ds(r0, ATT_BLOCK), cs]
    kb = k_ref[0, pl.ds(r0, 2 * ATT_BLOCK), cs]
    s = _dot_nt(q, kb) + b_ref[0, h]
    col = lax.broadcasted_iota(jnp.int32, s.shape, 1)
    return q, kb, jnp.where((col >= ATT_BLOCK) | valid_prev, s, NEG)


def _blocks_per_residue():
    br = pl.program_id(0)
    return jnp.where(br == 0, BLOCKS_PER_RESIDUE[0], jnp.where(br == 1, BLOCKS_PER_RESIDUE[1], BLOCKS_PER_RESIDUE[2]))


def _att_fwd(qp, kp, vp, bias):
    S = qp.shape[1]

    def body(q_ref, k_ref, v_ref, b_ref, o_ref, lse_ref):
        nbr = _blocks_per_residue()

        def unit(u, c):
            r0 = pl.multiple_of(u * ATT_BLOCK, ATT_BLOCK)
            valid_prev = (u % nbr) != 0
            for h in range(N_HEADS):
                cs = pl.ds(HEAD_DIM * h, HEAD_DIM)
                _, _, s = _att_scores(q_ref, k_ref, b_ref, r0, h, valid_prev)
                m = jnp.max(s, -1, keepdims=True)
                p = jnp.exp(s - m)
                den = jnp.sum(p, -1, keepdims=True)
                vb = v_ref[0, pl.ds(r0, 2 * ATT_BLOCK), cs]
                o_ref[0, pl.ds(r0, ATT_BLOCK), cs] = _dot(p.astype(bf16), vb) / den
                lse_ref[0, pl.ds(r0, ATT_BLOCK), pl.ds(h, 1)] = m + jnp.log(den)
            return c

        lax.fori_loop(0, N_UNITS, unit, 0)

    qspec = pl.BlockSpec((1, S, D_ATT), lambda b: (b, 0, 0))
    kspec = pl.BlockSpec((1, S + ATT_BLOCK, D_ATT), lambda b: (b, 0, 0))
    return pl.pallas_call(
        body, name="att_fwd", grid=(3,),
        in_specs=[qspec, kspec, kspec, pl.BlockSpec((1, N_HEADS, ATT_BLOCK, 2 * ATT_BLOCK), lambda b: (b, 0, 0, 0))],
        out_specs=[qspec, pl.BlockSpec((1, S, N_HEADS), lambda b: (b, 0, 0))],
        out_shape=[jax.ShapeDtypeStruct((3, S, D_ATT), f32), jax.ShapeDtypeStruct((3, S, N_HEADS), f32)],
        compiler_params=_cp(("parallel",), 48),
    )(qp, kp, vp, bias)


def _att_bwd(qp, kp, vp, dop, lsep, cp, bias):
    S = qp.shape[1]

    def body(q_ref, k_ref, v_ref, do_ref, lse_ref, c_ref, b_ref, dq_ref, dk_ref, dv_ref, db_ref, dk_sc, dv_sc):
        nbr = _blocks_per_residue()
        dk_sc[...] = jnp.zeros_like(dk_sc)
        dv_sc[...] = jnp.zeros_like(dv_sc)
        db_ref[...] = jnp.zeros_like(db_ref)

        def unit(u, c):
            r0 = pl.multiple_of(u * ATT_BLOCK, ATT_BLOCK)
            valid_prev = (u % nbr) != 0
            rows = pl.ds(r0, ATT_BLOCK)
            band = pl.ds(r0, 2 * ATT_BLOCK)
            for h in range(N_HEADS):
                cs = pl.ds(HEAD_DIM * h, HEAD_DIM)
                q, kb, s = _att_scores(q_ref, k_ref, b_ref, r0, h, valid_prev)
                p = jnp.exp(s - lse_ref[0, rows, pl.ds(h, 1)])
                do = do_ref[0, rows, cs]
                vb = v_ref[0, band, cs]
                ds = p * (_dot_nt(do, vb) - c_ref[0, rows, pl.ds(h, 1)])
                db_ref[0, h] += ds
                dsb = ds.astype(bf16)
                dq_ref[0, rows, cs] = (HEAD_DIM ** -0.5 * _dot(dsb, kb)).astype(bf16)
                dk_sc[band, cs] += _dot_tn(dsb, q)
                dv_sc[band, cs] += _dot_tn(p.astype(bf16), do)
            return c

        lax.fori_loop(0, N_UNITS, unit, 0)
        dk_ref[0] = dk_sc[...].astype(bf16)
        dv_ref[0] = dv_sc[...].astype(bf16)

    qspec = pl.BlockSpec((1, S, D_ATT), lambda b: (b, 0, 0))
    kspec = pl.BlockSpec((1, S + ATT_BLOCK, D_ATT), lambda b: (b, 0, 0))
    hspec = pl.BlockSpec((1, S, N_HEADS), lambda b: (b, 0, 0))
    bspec = pl.BlockSpec((1, N_HEADS, ATT_BLOCK, 2 * ATT_BLOCK), lambda b: (b, 0, 0, 0))
    return pl.pallas_call(
        body, name="att_bwd", grid=(3,),
        in_specs=[qspec, kspec, kspec, qspec, hspec, hspec, bspec],
        out_specs=[qspec, kspec, kspec, bspec],
        out_shape=[jax.ShapeDtypeStruct((3, S, D_ATT), bf16), jax.ShapeDtypeStruct((3, S + ATT_BLOCK, D_ATT), bf16),
                   jax.ShapeDtypeStruct((3, S + ATT_BLOCK, D_ATT), bf16),
                   jax.ShapeDtypeStruct((3, N_HEADS, ATT_BLOCK, 2 * ATT_BLOCK), f32)],
        scratch_shapes=[pltpu.VMEM((S + ATT_BLOCK, D_ATT), f32), pltpu.VMEM((S + ATT_BLOCK, D_ATT), f32)],
        compiler_params=_cp(("arbitrary",), 56),
    )(qp, kp, vp, dop, lsep, cp, bias)


def _branch_weights(lse_ref):
    l0, l1, l2 = lse_ref[0], lse_ref[1], lse_ref[2]
    m = jnp.maximum(jnp.maximum(l0, l1), l2)
    e0, e1, e2 = jnp.exp(l0 - m), jnp.exp(l1 - m), jnp.exp(l2 - m)
    tot = e0 + e1 + e2
    return e0 / tot, e1 / tot, e2 / tot


def _att_merge(o3, lse3):
    S = o3.shape[1]
    ts = TOK_TILE

    def body(o_ref, lse_ref, y_ref):
        w = _branch_weights(lse_ref)
        for h in range(N_HEADS):
            cs = pl.ds(HEAD_DIM * h, HEAD_DIM)
            y_ref[:, cs] = (w[0][:, h:h + 1] * o_ref[0, :, cs] + w[1][:, h:h + 1] * o_ref[1, :, cs]
                            + w[2][:, h:h + 1] * o_ref[2, :, cs])

    return pl.pallas_call(
        body, name="att_merge", grid=(S // ts,),
        in_specs=[pl.BlockSpec((3, ts, D_ATT), lambda i: (0, i, 0)), pl.BlockSpec((3, ts, N_HEADS), lambda i: (0, i, 0))],
        out_specs=pl.BlockSpec((ts, D_ATT), lambda i: (i, 0)),
        out_shape=jax.ShapeDtypeStruct((S, D_ATT), f32),
        compiler_params=_cp(("parallel",)),
    )(o3, lse3)


def _att_merge_bwd(dy, y, lse3):
    S = dy.shape[0]
    ts = TOK_TILE

    def body(dy_ref, y_ref, lse_ref, do_ref, c_ref):
        w = _branch_weights(lse_ref)
        for h in range(N_HEADS):
            cs = pl.ds(HEAD_DIM * h, HEAD_DIM)
            dyh = dy_ref[:, cs]
            t = jnp.sum(dyh * y_ref[:, cs], -1, keepdims=True)
            for p in range(3):
                wp = w[p][:, h:h + 1]
                do_ref[p, :, cs] = (wp * dyh).astype(bf16)
                c_ref[p, :, pl.ds(h, 1)] = wp * t

    return pl.pallas_call(
        body, name="att_merge_bwd", grid=(S // ts,),
        in_specs=[pl.BlockSpec((ts, D_ATT), lambda i: (i, 0)), pl.BlockSpec((ts, D_ATT), lambda i: (i, 0)),
                  pl.BlockSpec((3, ts, N_HEADS), lambda i: (0, i, 0))],
        out_specs=[pl.BlockSpec((3, ts, D_ATT), lambda i: (0, i, 0)), pl.BlockSpec((3, ts, N_HEADS), lambda i: (0, i, 0))],
        out_shape=[jax.ShapeDtypeStruct((3, S, D_ATT), bf16), jax.ShapeDtypeStruct((3, S, N_HEADS), f32)],
        compiler_params=_cp(("parallel",)),
    )(dy, y, lse3)


def _ssm_scan(xr, xi, a2, reverse):
    S, N = xr.shape
    nst = S // SCAN_SEG

    def body(xr_ref, xi_ref, a_ref, sr_ref, si_ref):
        ar = jnp.broadcast_to(a_ref[0:1, :], (SCAN_SEG, N))
        ai = jnp.broadcast_to(a_ref[1:2, :], (SCAN_SEG, N))
        if reverse:
            ai = -ai
        row = lax.broadcasted_iota(jnp.int32, (SCAN_SEG, N), 0)
        zero = jnp.zeros((SCAN_SEG, N), f32)

        def tile(t):
            return pl.ds(pl.multiple_of((nst - 1 - t if reverse else t) * SCAN_SEG, SCAN_SEG), SCAN_SEG)

        def local(t, c):
            sr, si, pr, pi = c
            rows = tile(t)
            nsr = ar * sr - ai * si + xr_ref[rows, :]
            nsi = ar * si + ai * sr + xi_ref[rows, :]
            sr_ref[rows, :] = nsr
            si_ref[rows, :] = nsi
            return nsr, nsi, ar * pr - ai * pi, ar * pi + ai * pr

        fr, fi, apr, api = lax.fori_loop(0, nst, local, (zero, zero, zero + 1.0, zero))

        def shift(v):
            if reverse:
                return jnp.where(row == SCAN_SEG - 1, 0.0, pltpu.roll(v, SCAN_SEG - 1, axis=0))
            return jnp.where(row == 0, 0.0, pltpu.roll(v, 1, axis=0))

        cr, ci = zero, zero
        for _ in range(SCAN_SEG - 1):
            cr, ci = shift(fr + apr * cr - api * ci), shift(fi + apr * ci + api * cr)

        def fix(t, c):
            pr, pi = c
            npr, npi = ar * pr - ai * pi, ar * pi + ai * pr
            rows = tile(t)
            sr_ref[rows, :] += npr * cr - npi * ci
            si_ref[rows, :] += npr * ci + npi * cr
            return npr, npi

        lax.fori_loop(0, nst, fix, (zero + 1.0, zero))

    vm = pl.BlockSpec(memory_space=pltpu.VMEM)
    return pl.pallas_call(
        body, name="ssm_scan_rev" if reverse else "ssm_scan",
        in_specs=[vm, vm, vm], out_specs=[vm, vm],
        out_shape=[jax.ShapeDtypeStruct((S, N), f32), jax.ShapeDtypeStruct((S, N), f32)],
        compiler_params=_cp(None, 48),
    )(xr, xi, a2)


def _ssm_in(u, bre, bim):
    S = u.shape[0]
    ts = TOK_TILE

    def body(u_ref, br_ref, bi_ref, or_ref, oi_ref):
        ub = u_ref[...].astype(bf16)
        or_ref[...] = _dot(ub, br_ref[...].astype(bf16))
        oi_ref[...] = _dot(ub, bi_ref[...].astype(bf16))

    return pl.pallas_call(
        body, name="ssm_in", grid=(S // ts,),
        in_specs=[pl.BlockSpec((ts, D_SSM), lambda i: (i, 0)), _full((D_SSM, D_STATE)), _full((D_SSM, D_STATE))],
        out_specs=[pl.BlockSpec((ts, D_STATE), lambda i: (i, 0))] * 2,
        out_shape=[jax.ShapeDtypeStruct((S, D_STATE), f32)] * 2,
        compiler_params=_cp(("parallel",)),
    )(u, bre, bim)


def _ssm_out(sr, si, u, cre, cim, dskip, glu_w, glu_b):
    S = u.shape[0]
    ts = TOK_TILE

    def body(sr_ref, si_ref, u_ref, cr_ref, ci_ref, d_ref, w_ref, b_ref, out_ref, y_ref):
        y = (_dot(sr_ref[...].astype(bf16), cr_ref[...].astype(bf16))
             - _dot(si_ref[...].astype(bf16), ci_ref[...].astype(bf16)) + d_ref[...] * u_ref[...])
        y_ref[...] = y
        z = _dot(_gelu(y).astype(bf16), w_ref[...].astype(bf16)) + b_ref[...]
        out_ref[...] = y * jax.nn.sigmoid(z)

    st = pl.BlockSpec((ts, D_STATE), lambda i: (i, 0))
    ch = pl.BlockSpec((ts, D_SSM), lambda i: (i, 0))
    return pl.pallas_call(
        body, name="ssm_out", grid=(S // ts,),
        in_specs=[st, st, ch, _full((D_STATE, D_SSM)), _full((D_STATE, D_SSM)), _full((1, D_SSM)),
                  _full((D_SSM, D_SSM)), _full((1, D_SSM))],
        out_specs=[ch, ch],
        out_shape=[jax.ShapeDtypeStruct((S, D_SSM), f32)] * 2,
        compiler_params=_cp(("parallel",)),
    )(sr, si, u, cre, cim, dskip, glu_w, glu_b)


def _ssm_out_bwd(dout, y, u, sr, si, cre, cim, dskip, glu_w, glu_b):
    S = u.shape[0]
    ts = TOK_TILE

    def body(do_ref, y_ref, u_ref, sr_ref, si_ref, cr_ref, ci_ref, d_ref, w_ref, b_ref,
             gr_ref, gi_ref, du_ref, dcr_ref, dci_ref, dd_ref, dgb_ref, dgw_ref):
        @pl.when(pl.program_id(0) == 0)
        def _():
            for r in (dcr_ref, dci_ref, dd_ref, dgb_ref, dgw_ref):
                r[...] = jnp.zeros_like(r)

        y = y_ref[...]
        dout = do_ref[...]
        wb = w_ref[...].astype(bf16)
        ge = _gelu(y).astype(bf16)
        sz = jax.nn.sigmoid(_dot(ge, wb) + b_ref[...])
        dz = dout * y * sz * (1.0 - sz)
        dzb = dz.astype(bf16)
        dgb_ref[...] += jnp.sum(dz, 0, keepdims=True)
        dgw_ref[...] += _dot_tn(ge, dzb)
        dy = dout * sz + _gelu_grad(y) * _dot_nt(dzb, wb)
        uv = u_ref[...]
        dd_ref[...] += jnp.sum(dy * uv, 0, keepdims=True)
        du_ref[...] = dy * d_ref[...]
        dyb = dy.astype(bf16)
        gr_ref[...] = _dot_nt(dyb, cr_ref[...].astype(bf16))
        gi_ref[...] = -_dot_nt(dyb, ci_ref[...].astype(bf16))
        dcr_ref[...] += _dot_tn(sr_ref[...].astype(bf16), dyb)
        dci_ref[...] -= _dot_tn(si_ref[...].astype(bf16), dyb)

    st = pl.BlockSpec((ts, D_STATE), lambda i: (i, 0))
    ch = pl.BlockSpec((ts, D_SSM), lambda i: (i, 0))
    c_full = _full((D_STATE, D_SSM))
    return pl.pallas_call(
        body, name="ssm_out_bwd", grid=(S // ts,),
        in_specs=[ch, ch, ch, st, st, c_full, c_full, _full((1, D_SSM)), _full((D_SSM, D_SSM)), _full((1, D_SSM))],
        out_specs=[st, st, ch, c_full, c_full, _full((1, D_SSM)), _full((1, D_SSM)), _full((D_SSM, D_SSM))],
        out_shape=[jax.ShapeDtypeStruct((S, D_STATE), f32), jax.ShapeDtypeStruct((S, D_STATE), f32),
                   jax.ShapeDtypeStruct((S, D_SSM), f32), jax.ShapeDtypeStruct((D_STATE, D_SSM), f32),
                   jax.ShapeDtypeStruct((D_STATE, D_SSM), f32), jax.ShapeDtypeStruct((1, D_SSM), f32),
                   jax.ShapeDtypeStruct((1, D_SSM), f32), jax.ShapeDtypeStruct((D_SSM, D_SSM), f32)],
        compiler_params=_cp(("arbitrary",), 40),
    )(dout, y, u, sr, si, cre, cim, dskip, glu_w, glu_b)


def _ssm_in_bwd(lr, li, u, du_skip, bre, bim):
    S = u.shape[0]
    ts = TOK_TILE

    def body(lr_ref, li_ref, u_ref, dus_ref, br_ref, bi_ref, du_ref, dbr_ref, dbi_ref):
        @pl.when(pl.program_id(0) == 0)
        def _():
            dbr_ref[...] = jnp.zeros_like(dbr_ref)
            dbi_ref[...] = jnp.zeros_like(dbi_ref)

        lrb = lr_ref[...].astype(bf16)
        lib = li_ref[...].astype(bf16)
        du_ref[...] = dus_ref[...] + _dot_nt(lrb, br_ref[...].astype(bf16)) + _dot_nt(lib, bi_ref[...].astype(bf16))
        ub = u_ref[...].astype(bf16)
        dbr_ref[...] += _dot_tn(ub, lrb)
        dbi_ref[...] += _dot_tn(ub, lib)

    st = pl.BlockSpec((ts, D_STATE), lambda i: (i, 0))
    ch = pl.BlockSpec((ts, D_SSM), lambda i: (i, 0))
    b_full = _full((D_SSM, D_STATE))
    return pl.pallas_call(
        body, name="ssm_in_bwd", grid=(S // ts,),
        in_specs=[st, st, ch, ch, b_full, b_full],
        out_specs=[ch, b_full, b_full],
        out_shape=[jax.ShapeDtypeStruct((S, D_SSM), f32), jax.ShapeDtypeStruct((D_SSM, D_STATE), f32),
                   jax.ShapeDtypeStruct((D_SSM, D_STATE), f32)],
        compiler_params=_cp(("arbitrary",), 40),
    )(lr, li, u, du_skip, bre, bim)


def _ssm_da(lr, li, sr, si):
    S, N = lr.shape
    nst = S // SCAN_SEG

    def body(lr_ref, li_ref, sr_ref, si_ref, out_ref):
        row = lax.broadcasted_iota(jnp.int32, (SCAN_SEG, N), 0)
        last = pl.ds((nst - 1) * SCAN_SEG, SCAN_SEG)
        pr = jnp.where(row == 0, 0.0, pltpu.roll(sr_ref[last, :], 1, axis=0))
        pi = jnp.where(row == 0, 0.0, pltpu.roll(si_ref[last, :], 1, axis=0))
        first = pl.ds(0, SCAN_SEG)
        acc_r = lr_ref[first, :] * pr + li_ref[first, :] * pi
        acc_i = li_ref[first, :] * pr - lr_ref[first, :] * pi

        def step(t, c):
            acc_r, acc_i = c
            rows = pl.ds(pl.multiple_of(t * SCAN_SEG, SCAN_SEG), SCAN_SEG)
            prev = pl.ds(pl.multiple_of((t - 1) * SCAN_SEG, SCAN_SEG), SCAN_SEG)
            lrv, liv, srv, siv = lr_ref[rows, :], li_ref[rows, :], sr_ref[prev, :], si_ref[prev, :]
            return acc_r + lrv * srv + liv * siv, acc_i + liv * srv - lrv * siv

        acc_r, acc_i = lax.fori_loop(1, nst, step, (acc_r, acc_i))
        out_ref[0:1, :] = jnp.sum(acc_r, 0, keepdims=True)
        out_ref[1:2, :] = jnp.sum(acc_i, 0, keepdims=True)

    vm = pl.BlockSpec(memory_space=pltpu.VMEM)
    return pl.pallas_call(
        body, name="ssm_da", in_specs=[vm] * 4, out_specs=vm,
        out_shape=jax.ShapeDtypeStruct((2, N), f32),
        compiler_params=_cp(None, 48),
    )(lr, li, sr, si)


_POOL_TILE = 256


def _window_sums(xt, back):
    n = xt.shape[0]
    out = []
    ws = xt
    for k in (1, 2, 4, 8):
        ws = ws + pltpu.roll(ws, k if back else n - k, axis=0)
        out.append(ws)
    return out


def _pool_count(r0, w):
    t = r0 + lax.broadcasted_iota(jnp.int32, (_POOL_TILE, POOL_GROUP), 0)
    return jnp.minimum(t + 1, w).astype(f32)


def _pool_fwd(u_pad, pool_w, pool_scale):
    S = u_pad.shape[0] - POOL_HALO
    nt = S // _POOL_TILE

    def body(u_ref, w_ref, sc_ref, y_ref):
        def tile(t, c):
            r0 = pl.multiple_of(t * _POOL_TILE, _POOL_TILE)
            for g, w in enumerate(POOL_WINDOWS):
                cs = pl.ds(POOL_GROUP * g, POOL_GROUP)
                xt = u_ref[pl.ds(r0, _POOL_TILE + POOL_HALO), cs]
                ws = _window_sums(xt, True)[g][POOL_HALO:, :]
                pooled = ws / _pool_count(r0, w) - xt[POOL_HALO:, :]
                y_ref[pl.ds(r0, _POOL_TILE), cs] = _dot(pooled.astype(bf16), w_ref[g].astype(bf16)) * sc_ref[:, cs]
            return c
        lax.fori_loop(0, nt, tile, 0)

    vm = pl.BlockSpec(memory_space=pltpu.VMEM)
    return pl.pallas_call(
        body, name="pool_fwd", in_specs=[vm, vm, vm], out_specs=vm,
        out_shape=jax.ShapeDtypeStruct((S, D_POOL), f32),
    )(u_pad, pool_w, pool_scale)


def _pool_bwd(dy_pad, u_pad, pool_w, pool_scale):
    S = u_pad.shape[0] - POOL_HALO
    nt = S // _POOL_TILE
    n = _POOL_TILE + POOL_HALO

    def body(dy_ref, u_ref, w_ref, sc_ref, du_ref, dw_ref, dsc_ref):
        dw_ref[...] = jnp.zeros_like(dw_ref)
        dsc_ref[...] = jnp.zeros_like(dsc_ref)

        def tile(t, c):
            r0 = pl.multiple_of(t * _POOL_TILE, _POOL_TILE)
            for g, w in enumerate(POOL_WINDOWS):
                cs = pl.ds(POOL_GROUP * g, POOL_GROUP)
                wb = w_ref[g].astype(bf16)
                xt = u_ref[pl.ds(r0, n), cs]
                pooled = (_window_sums(xt, True)[g][POOL_HALO:, :] / _pool_count(r0, w) - xt[POOL_HALO:, :]).astype(bf16)
                dy = dy_ref[pl.ds(r0, _POOL_TILE), cs]
                dsc_ref[:, cs] += jnp.sum(dy * _dot(pooled, wb), 0, keepdims=True)
                dw_ref[g] += _dot_tn(pooled, (dy * sc_ref[:, cs]).astype(bf16))
                dyh = (dy_ref[pl.ds(r0, n), cs] * sc_ref[:, cs]).astype(bf16)
                dpl = _dot_nt(dyh, wb)
                cnt = jnp.minimum(r0 + lax.broadcasted_iota(jnp.int32, (n, POOL_GROUP), 0) + 1, w).astype(f32)
                lead = _window_sums(dpl / cnt, False)[g]
                du_ref[pl.ds(r0, _POOL_TILE), cs] = lead[:_POOL_TILE, :] - dpl[:_POOL_TILE, :]
            return c
        lax.fori_loop(0, nt, tile, 0)

    vm = pl.BlockSpec(memory_space=pltpu.VMEM)
    return pl.pallas_call(
        body, name="pool_bwd", in_specs=[vm, vm, vm, vm], out_specs=[vm, vm, vm],
        out_shape=[jax.ShapeDtypeStruct((S, D_POOL), f32), jax.ShapeDtypeStruct((4, POOL_GROUP, POOL_GROUP), f32),
                   jax.ShapeDtypeStruct((1, D_POOL), f32)],
    )(dy_pad, u_pad, pool_w, pool_scale)


def _loss_head(y, target):
    S, D = y.shape
    ts = TOK_TILE

    def body(y_ref, t_ref, loss_ref, dy_ref):
        @pl.when(pl.program_id(0) == 0)
        def _():
            loss_ref[...] = jnp.zeros_like(loss_ref)

        d = y_ref[...] - t_ref[...]
        dy_ref[...] = d * (1.0 / D)
        loss_ref[...] += 0.5 * jnp.sum(jnp.sum(d * d, -1, keepdims=True) * (1.0 / D), 0, keepdims=True)

    tok = pl.BlockSpec((ts, D), lambda i: (i, 0))
    return pl.pallas_call(
        body, name="loss_head", grid=(S // ts,),
        in_specs=[tok, tok], out_specs=[_full((1, 1)), tok],
        out_shape=[jax.ShapeDtypeStruct((1, 1), f32), jax.ShapeDtypeStruct((S, D), f32)],
        compiler_params=_cp(("arbitrary",)),
    )(y, target)


_ADA_COLS = 768


def _ada_fwd(c_all, ada_w, ada_b_cols):
    L, D, N = ada_w.shape
    B = c_all.shape[0]

    def body(c_ref, w_ref, b_ref, out_ref):
        cv = c_ref[...]
        cond = (cv * jax.nn.sigmoid(cv)).astype(bf16)
        out_ref[0] = _dot(cond, w_ref[0].astype(bf16)) + b_ref[0]

    return pl.pallas_call(
        body, name="ada_fwd", grid=(L, N // _ADA_COLS),
        in_specs=[_full((B, D)), pl.BlockSpec((1, D, _ADA_COLS), lambda l, j: (l, 0, j)),
                  pl.BlockSpec((1, 1, _ADA_COLS), lambda l, j: (l, 0, j))],
        out_specs=pl.BlockSpec((1, B, _ADA_COLS), lambda l, j: (l, 0, j)),
        out_shape=jax.ShapeDtypeStruct((L, B, N), f32),
        compiler_params=_cp(("parallel", "parallel")),
    )(c_all, ada_w, ada_b_cols)


def _ada_wgrad(c_all_t, dmod_cols):
    D, B = c_all_t.shape
    L, _, N = dmod_cols.shape

    def body(ct_ref, dm_ref, out_ref):
        cv = ct_ref[...]
        cond = cv * jax.nn.sigmoid(cv)
        acc = cond[:, 0:1] * dm_ref[0, 0:1, :]
        for b in range(1, B):
            acc = acc + cond[:, b:b + 1] * dm_ref[0, b:b + 1, :]
        out_ref[0] = acc

    return pl.pallas_call(
        body, name="ada_wgrad", grid=(L, N // _ADA_COLS),
        in_specs=[_full((D, B)), pl.BlockSpec((1, B, _ADA_COLS), lambda l, j: (l, 0, j))],
        out_specs=pl.BlockSpec((1, D, _ADA_COLS), lambda l, j: (l, 0, j)),
        out_shape=jax.ShapeDtypeStruct((L, D, N), f32),
        compiler_params=_cp(("parallel", "parallel")),
    )(c_all_t, dmod_cols)


def _adam_math(w, g, m, v):
    m = ADAM_B1 * m + (1.0 - ADAM_B1) * g
    v = ADAM_B2 * v + (1.0 - ADAM_B2) * (g * g)
    m_hat = m / (1.0 - ADAM_B1 ** ADAM_STEP)
    v_hat = v / (1.0 - ADAM_B2 ** ADAM_STEP)
    delta = -ADAM_LR * (m_hat / (jnp.sqrt(v_hat) + ADAM_EPS) + ADAM_WD * w)
    return delta, m, v


def _adamw(w, m, v, grads, row_tile):
    R, C = w.shape
    ng = len(grads)

    def body(*refs):
        w_ref, m_ref, v_ref = refs[:3]
        g_refs = refs[3:3 + ng]
        g_out, d_out, m_out, v_out = refs[3 + ng:]
        g = g_refs[0][...]
        for r in g_refs[1:]:
            g = g + r[...]
        delta, mn, vn = _adam_math(w_ref[...], g, m_ref[...], v_ref[...])
        g_out[...] = g
        d_out[...] = delta
        m_out[...] = mn
        v_out[...] = vn

    spec = pl.BlockSpec((row_tile, C), lambda i: (i, 0))
    shp = jax.ShapeDtypeStruct((R, C), f32)
    return pl.pallas_call(
        body, name="adamw", grid=(R // row_tile,),
        in_specs=[spec] * (3 + ng), out_specs=[spec] * 4, out_shape=[shp] * 4,
        compiler_params=_cp(("parallel",), 40),
    )(w, m, v, *grads)


def _sum_shards(g16, recv, p_me):
    shape = g16.shape[1:]
    rows = shape[-2]
    lead = shape[:-2]
    nl = len(lead)
    blk_own = (1,) + (1,) * nl + (rows, shape[-1])

    def body(p_ref, own_ref, r_ref, out_ref):
        own = own_ref[...].reshape(rows, shape[-1]).astype(f32)
        acc = own
        for j in range(3):
            acc = acc + r_ref[j].reshape(rows, shape[-1]).astype(f32)
        out_ref[...] = acc.reshape(out_ref.shape)

    grid = lead if nl else (1,)

    def own_map(*a):
        idx, p = a[:-1], a[-1]
        return (p[0],) + (tuple(idx) if nl else ()) + (0, 0)

    def recv_map(*a):
        idx = a[:-1]
        return (0,) + (tuple(idx) if nl else ()) + (0, 0)

    def out_map(*a):
        idx = a[:-1]
        return (tuple(idx) if nl else ()) + (0, 0)

    gs = pltpu.PrefetchScalarGridSpec(
        num_scalar_prefetch=1, grid=grid,
        in_specs=[pl.BlockSpec(blk_own, own_map), pl.BlockSpec((3,) + (1,) * nl + (rows, shape[-1]), recv_map)],
        out_specs=pl.BlockSpec((1,) * nl + (rows, shape[-1]), out_map),
    )
    return pl.pallas_call(
        body, name="sum_shards", grid_spec=gs, out_shape=jax.ShapeDtypeStruct(shape, f32),
        compiler_params=_cp(("parallel",) * len(grid), 40),
    )(p_me, g16, recv)


def _sum8(packs):
    _, R, C = packs.shape
    tr = R // 8 if R % 64 == 0 else R

    def body(p_ref, out_ref):
        acc = p_ref[0]
        for d in range(1, 8):
            acc = acc + p_ref[d]
        out_ref[...] = acc

    return pl.pallas_call(
        body, name="sum8", grid=(R // tr,),
        in_specs=[pl.BlockSpec((8, tr, C), lambda i: (0, i, 0))],
        out_specs=pl.BlockSpec((tr, C), lambda i: (i, 0)),
        out_shape=jax.ShapeDtypeStruct((R, C), f32),
        compiler_params=_cp(("parallel",)),
    )(packs)


def _allgather8(x_shard):
    m_per, n = x_shard.shape

    def body(x_ref, out_ref, send_sems, recv_sems, local_sem):
        x, y, c = lax.axis_index("x"), lax.axis_index("y"), lax.axis_index("c")
        me, sibling = (x, y, c), (x, y, 1 - c)
        chips = [(1 - x, y), (x, 1 - y), (1 - x, 1 - y)]

        def rows(px, py, pc):
            return out_ref.at[pl.ds((4 * px + 2 * py + pc) * m_per, m_per), :]

        def copy(k, block, to, src=None):
            return pltpu.make_async_remote_copy(
                src_ref=rows(*block) if src is None else src, dst_ref=rows(*block),
                send_sem=send_sems.at[k], recv_sem=recv_sems.at[k], device_id=to, device_id_type=MESH)

        mine = pltpu.make_async_copy(x_ref, rows(*me), local_sem)
        mine.start()
        first = [copy(0, me, sibling, src=x_ref)]
        first += [copy(1 + j, me, (*chip, c), src=x_ref) for j, chip in enumerate(chips)]
        for cp in first:
            cp.start()
        passed = [copy(4 + j, (*chip, c), sibling) for j, chip in enumerate(chips)]
        for j, chip in enumerate(chips):
            copy(1 + j, (*chip, c), me).wait_recv()
            passed[j].start()
        copy(0, sibling, me).wait_recv()
        for j, chip in enumerate(chips):
            copy(4 + j, (*chip, 1 - c), me).wait_recv()
        for cp in first + passed:
            cp.wait_send()
        mine.wait()

    return pl.pallas_call(
        body, name="allgather8",
        out_shape=jax.ShapeDtypeStruct((8 * m_per, n), x_shard.dtype),
        in_specs=[pl.BlockSpec(memory_space=pltpu.VMEM)],
        out_specs=pl.BlockSpec(memory_space=pltpu.VMEM),
        scratch_shapes=[pltpu.SemaphoreType.DMA((7,)), pltpu.SemaphoreType.DMA((7,)), pltpu.SemaphoreType.DMA],
        compiler_params=_cp(None, 48),
    )(x_shard)


def _other_chips():
    x, y = lax.axis_index("x"), lax.axis_index("y")
    return [(1 - x, y), (x, 1 - y), (1 - x, 1 - y)]


def _gather_weights(shards):
    n = len(shards)

    def body(*refs):
        ins, outs = refs[:n], refs[n:2 * n]
        send_sems, recv_sems, local_sems = refs[2 * n:]
        x, y, c = lax.axis_index("x"), lax.axis_index("y"), lax.axis_index("c")
        p_me = 2 * x + y
        copies = []
        for a in range(n):
            loc = pltpu.make_async_copy(ins[a], outs[a].at[p_me], local_sems.at[a])
            loc.start()
            copies.append(loc)
            for j, (cx, cy) in enumerate(_other_chips()):
                cp = pltpu.make_async_remote_copy(
                    src_ref=ins[a], dst_ref=outs[a].at[p_me], send_sem=send_sems.at[3 * a + j],
                    recv_sem=recv_sems.at[3 * a + j], device_id=(cx, cy, c), device_id_type=MESH)
                cp.start()
                copies.append(cp)
        for cp in copies:
            cp.wait()

    hbm = pl.BlockSpec(memory_space=pl.ANY)
    return pl.pallas_call(
        body, name="gather_weights",
        out_shape=[jax.ShapeDtypeStruct((N_CHIPS,) + s.shape, s.dtype) for s in shards],
        in_specs=[hbm] * n, out_specs=[hbm] * n,
        scratch_shapes=[pltpu.SemaphoreType.DMA((3 * n,)), pltpu.SemaphoreType.DMA((3 * n,)), pltpu.SemaphoreType.DMA((n,))],
    )(*shards)


def _scatter_grads(g16s):
    n = len(g16s)

    def body(*refs):
        ins, outs = refs[:n], refs[n:2 * n]
        send_sems, recv_sems = refs[2 * n:]
        c = lax.axis_index("c")
        copies = []
        for a in range(n):
            for j, (cx, cy) in enumerate(_other_chips()):
                cp = pltpu.make_async_remote_copy(
                    src_ref=ins[a].at[2 * cx + cy], dst_ref=outs[a].at[j], send_sem=send_sems.at[3 * a + j],
                    recv_sem=recv_sems.at[3 * a + j], device_id=(cx, cy, c), device_id_type=MESH)
                cp.start()
                copies.append(cp)
        for cp in copies:
            cp.wait()

    hbm = pl.BlockSpec(memory_space=pl.ANY)
    return pl.pallas_call(
        body, name="scatter_grads",
        out_shape=[jax.ShapeDtypeStruct((3,) + g.shape[1:], g.dtype) for g in g16s],
        in_specs=[hbm] * n, out_specs=[hbm] * n,
        scratch_shapes=[pltpu.SemaphoreType.DMA((3 * n,)), pltpu.SemaphoreType.DMA((3 * n,))],
    )(*g16s)


def _swap_sibling(parts):
    n = len(parts)

    def body(*refs):
        ins, outs = refs[:n], refs[n:2 * n]
        send_sems, recv_sems = refs[2 * n:]
        sibling = (lax.axis_index("x"), lax.axis_index("y"), 1 - lax.axis_index("c"))
        copies = []
        for a in range(n):
            cp = pltpu.make_async_remote_copy(src_ref=ins[a], dst_ref=outs[a], send_sem=send_sems.at[a],
                                              recv_sem=recv_sems.at[a], device_id=sibling, device_id_type=MESH)
            cp.start()
            copies.append(cp)
        for cp in copies:
            cp.wait()

    hbm = pl.BlockSpec(memory_space=pl.ANY)
    return pl.pallas_call(
        body, name="swap_sibling",
        out_shape=[jax.ShapeDtypeStruct(p.shape, p.dtype) for p in parts],
        in_specs=[hbm] * n, out_specs=[hbm] * n,
        scratch_shapes=[pltpu.SemaphoreType.DMA((n,)), pltpu.SemaphoreType.DMA((n,))],
    )(*parts)


def _to_residue(t, d):
    s, c = t.shape
    return t.reshape(s // d, d, c).transpose(1, 0, 2).reshape(s, c)


def _from_residue(t, d):
    s, c = t.shape
    return t.reshape(d, s // d, c).transpose(1, 0, 2).reshape(s, c)


def _to_segments(t):
    s, c = t.shape
    return t.reshape(SCAN_SEG, s // SCAN_SEG, c).transpose(1, 0, 2).reshape(s, c)


def _from_segments(t):
    s, c = t.shape
    return t.reshape(s // SCAN_SEG, SCAN_SEG, c).transpose(1, 0, 2).reshape(s, c)


def _ssm_operators(a_re, a_im, log_dt, b_re, b_im, c_re, c_im):
    lam = lax.complex(a_re, a_im)
    dt = jnp.exp(log_dt)[:, None]
    a_bar = jnp.exp(lam * dt)
    b_bar = ((a_bar - 1.0) / lam)[:, :, None] * lax.complex(b_re, b_im)
    eye = jnp.eye(N_GROUPS, dtype=f32)

    def embed_b(t):
        return (jnp.transpose(t, (0, 2, 1))[:, :, None, :] * eye[:, None, :, None]).reshape(D_SSM, D_STATE)

    def embed_c(t):
        return (jnp.transpose(t, (0, 2, 1))[:, :, None, :] * eye[:, None, :, None]).reshape(D_STATE, D_SSM)

    a2 = jnp.stack([a_bar.real.reshape(D_STATE), a_bar.imag.reshape(D_STATE)])
    return a2, embed_b(b_bar.real), embed_b(b_bar.imag), embed_c(c_re), embed_c(c_im)


def _local_step(x, target, mod, W, small):
    table = jnp.asarray(_bucket_table())
    bias = _bias_fwd(small["rel_bias"], table)
    L = DEPTH
    saved = []
    ssm_ops = []
    for l in range(L):
        sv = {}
        m9 = mod[l]
        sv["x0"] = x
        x, sv["f0"], sv["g0"], sv["u0"] = _ffn_fwd(x, m9[0:3], W["gate"], W["up"], W["down"], 2 * l, small["ln_g"][l, 0:1],
                                                   small["ln_b"][l, 0:1])
        sv["x1"] = x
        z, sv["h1"] = _mix_in_fwd(x, m9[3:6], W["w_in"], l)
        q16 = (z[0] * HEAD_DIM ** -0.5).astype(bf16)
        k16 = z[1].astype(bf16)
        v16 = z[2].astype(bf16)
        pad = jnp.zeros((ATT_BLOCK, D_ATT), bf16)
        qp = jnp.stack([_to_residue(q16, d) for d in DILATIONS])
        kp = jnp.stack([jnp.concatenate([pad, _to_residue(k16, d)]) for d in DILATIONS])
        vp = jnp.stack([jnp.concatenate([pad, _to_residue(v16, d)]) for d in DILATIONS])
        op, lsep = _att_fwd(qp, kp, vp, bias)
        o3 = jnp.stack([_from_residue(op[b], d) for b, d in enumerate(DILATIONS)])
        lse3 = jnp.stack([_from_residue(lsep[b], d) for b, d in enumerate(DILATIONS)])
        y_att = _att_merge(o3, lse3)
        sv.update(qp=qp, kp=kp, vp=vp, lsep=lsep, lse3=lse3, y_att=y_att)

        prm = tuple(small[k][l] for k in ("ssm_a_re", "ssm_a_im", "ssm_log_dt", "ssm_b_re", "ssm_b_im", "ssm_c_re", "ssm_c_im"))
        (a2, bre, bim, cre, cim), ops_vjp = jax.vjp(_ssm_operators, *prm)
        ssm_ops.append(ops_vjp)
        u_ssm = _to_segments(z[3][:, :D_SSM])
        bur, bui = _ssm_in(u_ssm, bre, bim)
        sr, si = _ssm_scan(bur, bui, a2, False)
        dskip = small["ssm_d"][l][None, :]
        glu_b = small["glu_b"][l][None, :]
        out_seg, y_seg = _ssm_out(sr, si, u_ssm, cre, cim, dskip, small["glu_w"][l], glu_b)
        y_ssm = _from_segments(out_seg)
        sv.update(a2=a2, bre=bre, bim=bim, cre=cre, cim=cim, u_ssm=u_ssm, sr=sr, si=si, y_seg=y_seg, y_ssm=y_ssm)

        u_pool = jnp.concatenate([jnp.zeros((POOL_HALO, D_POOL), f32), z[3][:, D_SSM:]])
        y_pool = _pool_fwd(u_pool, small["pool_w"][l], small["pool_scale"][l][None, :])
        sv.update(u_pool=u_pool, y_pool=y_pool)

        x, sv["ymix"] = _mix_out_fwd(x, y_att, y_ssm, y_pool, m9[3:6], W["w_out"], l, small["ln_g"][l, 1:2], small["ln_b"][l, 1:2])
        sv["x2"] = x
        x, sv["f2"], sv["g2"], sv["u2"] = _ffn_fwd(x, m9[6:9], W["gate"], W["up"], W["down"], 2 * l + 1, small["ln_g"][l, 2:3],
                                                   small["ln_b"][l, 2:3])
        saved.append(sv)

    loss, dx = _loss_head(x, target)

    G = {k: lax.empty((N_CHIPS,) + W[k].shape[1:], bf16) for k in ("gate", "up", "down", "w_in", "w_out")}
    dmod = [None] * L
    dln_g = [None] * L
    dln_b = [None] * L
    sg = {k: [None] * L for k in ("ssm_a_re", "ssm_a_im", "ssm_log_dt", "ssm_b_re", "ssm_b_im", "ssm_c_re", "ssm_c_im",
                                  "ssm_d", "glu_w", "glu_b", "pool_w", "pool_scale")}
    dbias_tot = None
    for l in reversed(range(L)):
        sv = saved[l]
        m9 = mod[l]
        dx, dg, du, a, h, df, dm2, dlg2, dlb2 = _ffn_bwd(dx, sv["x2"], sv["f2"], sv["g2"], sv["u2"], m9[6:9], W["gate"], W["up"],
                                                        W["down"], 2 * l + 1, small["ln_g"][l, 2:3])
        G["gate"], G["up"], G["down"] = _ffn_wgrad(h, dg, du, a, df, G["gate"], G["up"], G["down"], 2 * l + 1)
        dxr, d_att, d_ssm, d_pool, dgate1, dlg1, dlb1, G["w_out"] = _mix_out_bwd(
            dx, sv["x1"], sv["ymix"], sv["y_att"], sv["y_ssm"], sv["y_pool"], m9[3:6], W["w_out"], l, small["ln_g"][l, 1:2], G["w_out"])
        do3, c3 = _att_merge_bwd(d_att, sv["y_att"], sv["lse3"])
        dop = jnp.stack([_to_residue(do3[b], d) for b, d in enumerate(DILATIONS)])
        cp = jnp.stack([_to_residue(c3[b], d) for b, d in enumerate(DILATIONS)])
        dqp, dkp, dvp, dbias = _att_bwd(sv["qp"], sv["kp"], sv["vp"], dop, sv["lsep"], cp, bias)
        dbias_tot = dbias if dbias_tot is None else dbias_tot + dbias

        def back(t3, padded):
            parts = [_from_residue(t3[b][ATT_BLOCK:] if padded else t3[b], d).astype(f32) for b, d in enumerate(DILATIONS)]
            return (parts[0] + parts[1] + parts[2]).astype(bf16)

        dq, dk, dv = back(dqp, False), back(dkp, True), back(dvp, True)
        d_seg = _to_segments(d_ssm)
        dskip = small["ssm_d"][l][None, :]
        glu_b = small["glu_b"][l][None, :]
        gsr, gsi, du_skip, dcre, dcim, dd, dglu_b, dglu_w = _ssm_out_bwd(
            d_seg, sv["y_seg"], sv["u_ssm"], sv["sr"], sv["si"], sv["cre"], sv["cim"], dskip, small["glu_w"][l], glu_b)
        lr, li = _ssm_scan(gsr, gsi, sv["a2"], True)
        du_seg, dbre, dbim = _ssm_in_bwd(lr, li, sv["u_ssm"], du_skip, sv["bre"], sv["bim"])
        da2 = _ssm_da(lr, li, sv["sr"], sv["si"])
        d_prm = ssm_ops[l]((da2, dbre, dbim, dcre, dcim))
        for k, v in zip(("ssm_a_re", "ssm_a_im", "ssm_log_dt", "ssm_b_re", "ssm_b_im", "ssm_c_re", "ssm_c_im"), d_prm):
            sg[k][l] = v
        sg["ssm_d"][l] = dd[0]
        sg["glu_b"][l] = dglu_b[0]
        sg["glu_w"][l] = dglu_w
        du_ssm = _from_segments(du_seg)
        dyp = jnp.concatenate([d_pool, jnp.zeros((POOL_HALO, D_POOL), f32)])
        du_pool, dpw, dps = _pool_bwd(dyp, sv["u_pool"], small["pool_w"][l], small["pool_scale"][l][None, :])
        sg["pool_w"][l] = dpw
        sg["pool_scale"][l] = dps[0]
        dz = jnp.stack([dq, dk, dv, jnp.concatenate([du_ssm, du_pool], axis=1).astype(bf16)])
        dx, dm1 = _mix_in_bwd(dz, dxr, sv["x1"], m9[3:6], W["w_in"], l)
        G["w_in"] = _mix_in_wgrad(sv["h1"], dz, G["w_in"], l)
        dm1 = jnp.concatenate([dm1[0:2], dgate1])
        dx, dg, du, a, h, df, dm0, dlg0, dlb0 = _ffn_bwd(dx, sv["x0"], sv["f0"], sv["g0"], sv["u0"], m9[0:3], W["gate"], W["up"],
                                                        W["down"], 2 * l, small["ln_g"][l, 0:1])
        G["gate"], G["up"], G["down"] = _ffn_wgrad(h, dg, du, a, df, G["gate"], G["up"], G["down"], 2 * l)
        dmod[l] = jnp.concatenate([dm0, dm1, dm2])
        dln_g[l] = jnp.concatenate([dlg0, dlg1, dlg2])
        dln_b[l] = jnp.concatenate([dlb0, dlb1, dlb2])

    small_grads = {k: jnp.stack(v) for k, v in sg.items()}
    small_grads["rel_bias"] = _bias_bwd(dbias_tot, table)
    small_grads["ln_g"] = jnp.stack(dln_g)
    small_grads["ln_b"] = jnp.stack(dln_b)
    return loss, dx, G, jnp.stack(dmod), small_grads


def _pack(arrs):
    flat = jnp.concatenate([a.reshape(-1).astype(f32) for a in arrs])
    n = flat.shape[0]
    npad = -(-n // 1024) * 1024
    return jnp.pad(flat, (0, npad - n)).reshape(npad // 128, 128)


def _unpack(buf, shapes):
    flat = buf.reshape(-1)
    out, off = [], 0
    for s in shapes:
        n = int(np.prod(s))
        out.append(flat[off:off + n].reshape(s))
        off += n
    return out


_REPL = ("rel_bias", "ada_b", "ssm_a_re", "ssm_a_im", "ssm_log_dt", "ssm_b_re", "ssm_b_im", "ssm_c_re", "ssm_c_im",
         "ssm_d", "glu_b", "pool_w", "pool_scale")
_SMALL_SHARDED = ("ln_g", "ln_b", "glu_w")
_BIG = ("ffn_w_gate", "ffn_w_up", "ffn_w_down", "w_in", "w_out")
_ORDER = ("rel_bias", "ada_w", "ada_b", "ln_g", "ln_b", "ffn_w_gate", "ffn_w_up", "ffn_w_down", "w_in", "w_out",
          "ssm_a_re", "ssm_a_im", "ssm_log_dt", "ssm_b_re", "ssm_b_im", "ssm_c_re", "ssm_c_im", "ssm_d", "glu_w",
          "glu_b", "pool_w", "pool_scale")


def kernel(x, c, rel_bias, ada_w, ada_b, ln_g, ln_b, ffn_w_gate, ffn_w_up, ffn_w_down, w_in, w_out, ssm_a_re, ssm_a_im, ssm_log_dt, ssm_b_re, ssm_b_im, ssm_c_re, ssm_c_im, ssm_d, glu_w, glu_b, pool_w, pool_scale, loss_target, m_rel_bias, m_ada_w, m_ada_b, m_ln_g, m_ln_b, m_ffn_w_gate, m_ffn_w_up, m_ffn_w_down, m_w_in, m_w_out, m_ssm_a_re, m_ssm_a_im, m_ssm_log_dt, m_ssm_b_re, m_ssm_b_im, m_ssm_c_re, m_ssm_c_im, m_ssm_d, m_glu_w, m_glu_b, m_pool_w, m_pool_scale, v_rel_bias, v_ada_w, v_ada_b, v_ln_g, v_ln_b, v_ffn_w_gate, v_ffn_w_up, v_ffn_w_down, v_w_in, v_w_out, v_ssm_a_re, v_ssm_a_im, v_ssm_log_dt, v_ssm_b_re, v_ssm_b_im, v_ssm_c_re, v_ssm_c_im, v_ssm_d, v_glu_w, v_glu_b, v_pool_w, v_pool_scale):
    args = dict(locals())
    w = {k: args[k] for k in _ORDER}
    m = {k: args["m_" + k] for k in _ORDER}
    v = {k: args["v_" + k] for k in _ORDER}
    L, D = DEPTH, D_MODEL
    ax, ay, ac = lax.axis_index("x"), lax.axis_index("y"), lax.axis_index("c")
    p_me = 2 * ax + ay
    dev = 4 * ax + 2 * ay + ac

    fwd_shapes = [(1, D), (L, 3, 256), (L, 3, 256), (L, 64, 256)]
    pack = _pack([c, ln_g, ln_b, glu_w])
    rows = pack.shape[0]
    allp = _allgather8(pack).reshape(8, rows, 128)
    per_dev = [_unpack(allp[d], fwd_shapes) for d in range(8)]
    c_all = jnp.concatenate([pd[0] for pd in per_dev])
    ln_g_full = jnp.concatenate([per_dev[2 * p][1] for p in range(N_CHIPS)], axis=-1)
    ln_b_full = jnp.concatenate([per_dev[2 * p][2] for p in range(N_CHIPS)], axis=-1)
    glu_w_full = jnp.concatenate([per_dev[2 * p][3] for p in range(N_CHIPS)], axis=1)

    ncol = ada_w.shape[-1]
    ada_b_cols = lax.dynamic_slice_in_dim(ada_b, p_me * ncol, ncol, axis=1)[:, None, :]
    mod_part = _ada_fwd(c_all, ada_w, ada_b_cols)
    mrows = L * 8 * ncol // 128
    mod_all = _allgather8(mod_part.reshape(mrows, 128)).reshape(8, L, 8, ncol)
    mod_mine = lax.dynamic_index_in_dim(mod_all, dev, axis=2, keepdims=False)
    mod = jnp.concatenate([mod_mine[2 * p] for p in range(N_CHIPS)], axis=-1).reshape(L, 9, D)

    shards = [ffn_w_gate.reshape(2 * L, D, FF_SHARD), ffn_w_up.reshape(2 * L, D, FF_SHARD),
              ffn_w_down.reshape(2 * L, FF_SHARD, D), w_in, w_out]
    gathered = _gather_weights([s.astype(bf16) for s in shards])
    W = dict(zip(("gate", "up", "down", "w_in", "w_out"), gathered))

    small = {k: w[k] for k in _REPL if k != "ada_b"}
    small.update(ln_g=ln_g_full, ln_b=ln_b_full, glu_w=glu_w_full)
    loss_dev, grad_x, G, dmod, sgrads = _local_step(x[0], loss_target[0], mod, W, small)
    loss = lax.psum(loss_dev[0, 0], ("x", "y", "c"))

    names = ("rel_bias", "ln_g", "ln_b", "ssm_a_re", "ssm_a_im", "ssm_log_dt", "ssm_b_re", "ssm_b_im", "ssm_c_re",
             "ssm_c_im", "ssm_d", "glu_w", "glu_b", "pool_w", "pool_scale")
    gpack = _pack([dmod] + [sgrads[k] for k in names])
    grows = gpack.shape[0]
    gall = _allgather8(gpack).reshape(8, grows, 128)
    gsum = _unpack(_sum8(gall), [(L, 9 * D)] + [sgrads[k].shape for k in names])
    red = dict(zip(("ada_b",) + names, gsum))
    red["ln_g"] = lax.dynamic_slice_in_dim(red["ln_g"], p_me * 256, 256, axis=2)
    red["ln_b"] = lax.dynamic_slice_in_dim(red["ln_b"], p_me * 256, 256, axis=2)
    red["glu_w"] = lax.dynamic_slice_in_dim(red["glu_w"], p_me * 64, 64, axis=1)

    dmod_all = gall[:, :L * 9 * D // 128].reshape(8, L, 9 * D)
    dmod_cols = jnp.transpose(lax.dynamic_slice_in_dim(dmod_all, p_me * ncol, ncol, axis=2), (1, 0, 2))
    g_ada_w = _ada_wgrad(jnp.transpose(c_all), dmod_cols)

    keys = ("gate", "up", "down", "w_in", "w_out")
    recv = _scatter_grads([G[k] for k in keys])
    p_arr = jnp.reshape(p_me, (1,)).astype(jnp.int32)
    part = [_sum_shards(G[k], r, p_arr) for k, r in zip(keys, recv)]
    other = _swap_sibling(part)

    out_g, out_d, out_m, out_v = {}, {}, {}, {}
    for k, name in zip(keys, _BIG):
        shp = w[name].shape
        r2 = (int(np.prod(shp[:-1])), shp[-1])
        i = keys.index(k)
        res = _adamw(w[name].reshape(r2), m[name].reshape(r2), v[name].reshape(r2),
                     [part[i].reshape(r2), other[i].reshape(r2)], 256)
        out_g[name], out_d[name], out_m[name], out_v[name] = [t.reshape(shp) for t in res]
    r2 = (L * D, ncol)
    res = _adamw(ada_w.reshape(r2), m["ada_w"].reshape(r2), v["ada_w"].reshape(r2), [g_ada_w.reshape(r2)], 128)
    out_g["ada_w"], out_d["ada_w"], out_m["ada_w"], out_v["ada_w"] = [t.reshape(ada_w.shape) for t in res]

    small_names = _REPL + _SMALL_SHARDED
    wp = _pack([w[k] for k in small_names])
    res = _adamw(wp, _pack([m[k] for k in small_names]), _pack([v[k] for k in small_names]),
                 [_pack([red[k] for k in small_names])], wp.shape[0])
    for t, dst in zip(res, (out_g, out_d, out_m, out_v)):
        for k, a in zip(small_names, _unpack(t, [w[k].shape for k in small_names])):
            dst[k] = a

    return (loss, grad_x[None], *[out_g[k] for k in _ORDER], *[out_d[k] for k in _ORDER],
            *[out_m[k] for k in _ORDER], *[out_v[k] for k in _ORDER])
```

```python
import functools
import math

import numpy as np
import jax
import jax.numpy as jnp
from jax import lax
from jax.experimental import pallas as pl
from jax.experimental.pallas import tpu as pltpu

f32 = jnp.float32
bf16 = jnp.bfloat16
MESH = pl.DeviceIdType.MESH

D_MODEL = 1024
SEQ = 2048
DEPTH = 2
HEAD_DIM = 64
N_HEADS = 8
D_ATT = 512
DILATIONS = (1, 4, 16)
BLOCKS_PER_RESIDUE = (16, 4, 1)
ATT_BLOCK = 128
N_UNITS = SEQ // ATT_BLOCK
N_GROUPS = 16
SSM_GROUP = 16
SSM_STATE = 64
D_SSM = 256
D_STATE = N_GROUPS * SSM_STATE
POOL_WINDOWS = (2, 4, 8, 16)
POOL_GROUP = 64
D_POOL = 256
POOL_HALO = 16
D_FF = 2816
N_BUCKETS = 32
MAX_DISTANCE = 2048
ALPHA = (2 * DEPTH) ** 0.25
FFN_RES = 0.5
LN_EPS = 1e-5
NEG = -1e30
N_CHIPS = 4
FF_SHARD = D_FF // N_CHIPS
SCAN_SEG = 8
SCAN_STEPS = SEQ // SCAN_SEG

ADAM_LR, ADAM_B1, ADAM_B2, ADAM_EPS, ADAM_WD, ADAM_STEP = 0.001, 0.9, 0.999, 1e-08, 0.01, 10

TOK_TILE = 512


def _cp(dims=None, vmem_mb=None):
    kw = {}
    if dims is not None:
        kw["dimension_semantics"] = dims
    if vmem_mb is not None:
        kw["vmem_limit_bytes"] = vmem_mb << 20
    return pltpu.CompilerParams(**kw)


def _dot(a, b):
    return jnp.dot(a, b, preferred_element_type=f32)


def _dot_nt(a, b):
    return lax.dot_general(a, b, (((1,), (1,)), ((), ())), preferred_element_type=f32)


def _dot_tn(a, b):
    return lax.dot_general(a, b, (((0,), (0,)), ((), ())), preferred_element_type=f32)


def _ln_stats(v):
    mu = jnp.mean(v, -1, keepdims=True)
    d = v - mu
    var = jnp.mean(d * d, -1, keepdims=True)
    rstd = lax.rsqrt(var + LN_EPS)
    return d * rstd, rstd


def _ln_bwd(dxh, xh, rstd):
    return rstd * (dxh - jnp.mean(dxh, -1, keepdims=True) - xh * jnp.mean(dxh * xh, -1, keepdims=True))


_GELU_C = math.sqrt(2.0 / math.pi)


def _gelu(y):
    return 0.5 * y * (1.0 + jnp.tanh(_GELU_C * (y + 0.044715 * y * y * y)))


def _gelu_grad(y):
    t = jnp.tanh(_GELU_C * (y + 0.044715 * y * y * y))
    return 0.5 * (1.0 + t) + 0.5 * y * (1.0 - t * t) * (_GELU_C * (1.0 + 3 * 0.044715 * y * y))


def _full(shape):
    return pl.BlockSpec(shape, lambda *_: (0,) * len(shape))


def _ffn_fwd(x, mod3, wg, wu, wd, ls, lng, lnb):
    S, D = x.shape
    Fs = wg.shape[-1]
    ts = TOK_TILE

    def body(x_ref, mod_ref, wg_ref, wu_ref, wd_ref, lng_ref, lnb_ref, xo_ref, f_ref, g_ref, u_ref, h_sc, acc_sc):
        j = pl.program_id(1)

        @pl.when(j == 0)
        def _():
            xh, _ = _ln_stats(x_ref[...])
            h_sc[...] = (xh * (1.0 + mod_ref[1:2, :]) + mod_ref[0:1, :]).astype(bf16)
            acc_sc[...] = jnp.zeros_like(acc_sc)

        h = h_sc[...]
        g = _dot(h, wg_ref[0, 0])
        u = _dot(h, wu_ref[0, 0])
        g_ref[0] = g.astype(bf16)
        u_ref[0] = u.astype(bf16)
        a = (g * jax.nn.sigmoid(g) * u).astype(bf16)
        acc_sc[...] += _dot(a, wd_ref[0, 0])

        @pl.when(j == N_CHIPS - 1)
        def _():
            f = acc_sc[...]
            f_ref[...] = f
            r = ALPHA * x_ref[...] + (FFN_RES * mod_ref[2:3, :]) * f
            rh, _ = _ln_stats(r)
            xo_ref[...] = rh * lng_ref[...] + lnb_ref[...]

    tok = pl.BlockSpec((ts, D), lambda i, j: (i, 0))
    wcol = pl.BlockSpec((1, 1, D, Fs), lambda i, j: (j, ls, 0, 0))
    wrow = pl.BlockSpec((1, 1, Fs, D), lambda i, j: (j, ls, 0, 0))
    hid = pl.BlockSpec((1, ts, Fs), lambda i, j: (j, i, 0))
    return pl.pallas_call(
        body, name="ffn_fwd", grid=(S // ts, N_CHIPS),
        in_specs=[tok, _full((3, D)), wcol, wcol, wrow, _full((1, D)), _full((1, D))],
        out_specs=[tok, tok, hid, hid],
        out_shape=[jax.ShapeDtypeStruct((S, D), f32), jax.ShapeDtypeStruct((S, D), f32),
                   jax.ShapeDtypeStruct((N_CHIPS, S, Fs), bf16), jax.ShapeDtypeStruct((N_CHIPS, S, Fs), bf16)],
        scratch_shapes=[pltpu.VMEM((ts, D), bf16), pltpu.VMEM((ts, D), f32)],
        compiler_params=_cp(("parallel", "arbitrary"), 56),
    )(x, mod3, wg, wu, wd, lng, lnb)


def _ffn_bwd(dxo, x, f, g, u, mod3, wg, wu, wd, ls, lng):
    S, D = x.shape
    Fs = wg.shape[-1]
    ts = TOK_TILE

    def body(dxo_ref, x_ref, f_ref, g_ref, u_ref, mod_ref, wg_ref, wu_ref, wd_ref, lng_ref,
             dx_ref, dg_ref, du_ref, a_ref, h_ref, df_ref, dmod_ref, dlng_ref, dlnb_ref,
             dr_sc, df_sc, acc_sc):
        i = pl.program_id(0)
        j = pl.program_id(1)

        @pl.when((i == 0) & (j == 0))
        def _():
            dmod_ref[...] = jnp.zeros_like(dmod_ref)
            dlng_ref[...] = jnp.zeros_like(dlng_ref)
            dlnb_ref[...] = jnp.zeros_like(dlnb_ref)

        @pl.when(j == 0)
        def _():
            xv = x_ref[...]
            fv = f_ref[...]
            gate = mod_ref[2:3, :]
            rh, rstd = _ln_stats(ALPHA * xv + (FFN_RES * gate) * fv)
            dy = dxo_ref[...]
            dlng_ref[...] += jnp.sum(dy * rh, 0, keepdims=True)
            dlnb_ref[...] += jnp.sum(dy, 0, keepdims=True)
            dr = _ln_bwd(dy * lng_ref[...], rh, rstd)
            dr_sc[...] = dr
            dmod_ref[2:3, :] += jnp.sum(FFN_RES * dr * fv, 0, keepdims=True)
            df = ((FFN_RES * gate) * dr).astype(bf16)
            df_sc[...] = df
            df_ref[...] = df
            xh, _ = _ln_stats(xv)
            h_ref[...] = (xh * (1.0 + mod_ref[1:2, :]) + mod_ref[0:1, :]).astype(bf16)
            acc_sc[...] = jnp.zeros_like(acc_sc)

        da = _dot_nt(df_sc[...], wd_ref[0, 0])
        gv = g_ref[0].astype(f32)
        uv = u_ref[0].astype(f32)
        sg = jax.nn.sigmoid(gv)
        si = gv * sg
        a_ref[0] = (si * uv).astype(bf16)
        dgv = (da * uv * (sg * (1.0 + gv * (1.0 - sg)))).astype(bf16)
        duv = (da * si).astype(bf16)
        dg_ref[0] = dgv
        du_ref[0] = duv
        acc_sc[...] += _dot_nt(dgv, wg_ref[0, 0]) + _dot_nt(duv, wu_ref[0, 0])

        @pl.when(j == N_CHIPS - 1)
        def _():
            dh = acc_sc[...]
            xh, rstd0 = _ln_stats(x_ref[...])
            dmod_ref[0:1, :] += jnp.sum(dh, 0, keepdims=True)
            dmod_ref[1:2, :] += jnp.sum(dh * xh, 0, keepdims=True)
            dx_ref[...] = _ln_bwd(dh * (1.0 + mod_ref[1:2, :]), xh, rstd0) + ALPHA * dr_sc[...]

    tok = pl.BlockSpec((ts, D), lambda i, j: (i, 0))
    wcol = pl.BlockSpec((1, 1, D, Fs), lambda i, j: (j, ls, 0, 0))
    wrow = pl.BlockSpec((1, 1, Fs, D), lambda i, j: (j, ls, 0, 0))
    hid = pl.BlockSpec((1, ts, Fs), lambda i, j: (j, i, 0))
    hid_shape = jax.ShapeDtypeStruct((N_CHIPS, S, Fs), bf16)
    return pl.pallas_call(
        body, name="ffn_bwd", grid=(S // ts, N_CHIPS),
        in_specs=[tok, tok, tok, hid, hid, _full((3, D)), wcol, wcol, wrow, _full((1, D))],
        out_specs=[tok, hid, hid, hid, tok, tok, _full((3, D)), _full((1, D)), _full((1, D))],
        out_shape=[jax.ShapeDtypeStruct((S, D), f32), hid_shape, hid_shape, hid_shape,
                   jax.ShapeDtypeStruct((S, D), bf16), jax.ShapeDtypeStruct((S, D), bf16),
                   jax.ShapeDtypeStruct((3, D), f32), jax.ShapeDtypeStruct((1, D), f32), jax.ShapeDtypeStruct((1, D), f32)],
        scratch_shapes=[pltpu.VMEM((ts, D), f32), pltpu.VMEM((ts, D), bf16), pltpu.VMEM((ts, D), f32)],
        compiler_params=_cp(("arbitrary", "arbitrary"), 56),
    )(dxo, x, f, g, u, mod3, wg, wu, wd, lng)


def _ffn_wgrad(h, dg, du, a, df, gwg, gwu, gwd, ls):
    S, D = h.shape
    Fs = dg.shape[-1]
    tk = TOK_TILE
    nk = S // tk

    def body(h_ref, dg_ref, du_ref, a_ref, df_ref, _g0, _g1, _g2, gwg_ref, gwu_ref, gwd_ref, ag_sc, au_sc, ad_sc):
        k = pl.program_id(1)

        @pl.when(k == 0)
        def _():
            ag_sc[...] = jnp.zeros_like(ag_sc)
            au_sc[...] = jnp.zeros_like(au_sc)
            ad_sc[...] = jnp.zeros_like(ad_sc)

        hv = h_ref[...]
        ag_sc[...] += _dot_tn(hv, dg_ref[0])
        au_sc[...] += _dot_tn(hv, du_ref[0])
        ad_sc[...] += _dot_tn(a_ref[0], df_ref[...])

        @pl.when(k == nk - 1)
        def _():
            gwg_ref[0, 0] = ag_sc[...].astype(bf16)
            gwu_ref[0, 0] = au_sc[...].astype(bf16)
            gwd_ref[0, 0] = ad_sc[...].astype(bf16)

    tok = pl.BlockSpec((tk, D), lambda p, k: (k, 0))
    hid = pl.BlockSpec((1, tk, Fs), lambda p, k: (p, k, 0))
    anyspec = pl.BlockSpec(memory_space=pl.ANY)
    ocol = pl.BlockSpec((1, 1, D, Fs), lambda p, k: (p, ls, 0, 0))
    orow = pl.BlockSpec((1, 1, Fs, D), lambda p, k: (p, ls, 0, 0))
    return pl.pallas_call(
        body, name="ffn_wgrad", grid=(N_CHIPS, nk),
        in_specs=[tok, hid, hid, hid, tok, anyspec, anyspec, anyspec],
        out_specs=[ocol, ocol, orow],
        out_shape=[jax.ShapeDtypeStruct(gwg.shape, bf16), jax.ShapeDtypeStruct(gwu.shape, bf16),
                   jax.ShapeDtypeStruct(gwd.shape, bf16)],
        scratch_shapes=[pltpu.VMEM((D, Fs), f32), pltpu.VMEM((D, Fs), f32), pltpu.VMEM((Fs, D), f32)],
        input_output_aliases={5: 0, 6: 1, 7: 2},
        compiler_params=_cp(("parallel", "arbitrary"), 48),
    )(h, dg, du, a, df, gwg, gwu, gwd)


def _mix_in_fwd(x, mod3, w_in, l):
    S, D = x.shape
    N = w_in.shape[-1]
    ts = TOK_TILE

    def body(x_ref, mod_ref, w_ref, z_ref, h_ref):
        @pl.when(pl.program_id(1) == 0)
        def _():
            xh, _ = _ln_stats(x_ref[...])
            h_ref[...] = (xh * (1.0 + mod_ref[1:2, :]) + mod_ref[0:1, :]).astype(bf16)

        z_ref[0] = _dot(h_ref[...], w_ref[0, 0])

    tok = pl.BlockSpec((ts, D), lambda i, j: (i, 0))
    return pl.pallas_call(
        body, name="mix_in_fwd", grid=(S // ts, N_CHIPS),
        in_specs=[tok, _full((3, D)), pl.BlockSpec((1, 1, D, N), lambda i, j: (j, l, 0, 0))],
        out_specs=[pl.BlockSpec((1, ts, N), lambda i, j: (j, i, 0)), tok],
        out_shape=[jax.ShapeDtypeStruct((N_CHIPS, S, N), f32), jax.ShapeDtypeStruct((S, D), bf16)],
        compiler_params=_cp(("parallel", "arbitrary"), 40),
    )(x, mod3, w_in)


def _mix_in_bwd(dz, dx_res, x, mod3, w_in, l):
    S, D = x.shape
    N = w_in.shape[-1]
    ts = TOK_TILE

    def body(dz_ref, dxr_ref, x_ref, mod_ref, w_ref, dx_ref, dmod_ref, acc_sc):
        i = pl.program_id(0)
        j = pl.program_id(1)

        @pl.when((i == 0) & (j == 0))
        def _():
            dmod_ref[...] = jnp.zeros_like(dmod_ref)

        @pl.when(j == 0)
        def _():
            acc_sc[...] = jnp.zeros_like(acc_sc)

        acc_sc[...] += _dot_nt(dz_ref[0], w_ref[0, 0])

        @pl.when(j == N_CHIPS - 1)
        def _():
            dh = acc_sc[...]
            xh, rstd0 = _ln_stats(x_ref[...])
            dmod_ref[0:1, :] += jnp.sum(dh, 0, keepdims=True)
            dmod_ref[1:2, :] += jnp.sum(dh * xh, 0, keepdims=True)
            dx_ref[...] = _ln_bwd(dh * (1.0 + mod_ref[1:2, :]), xh, rstd0) + dxr_ref[...]

    tok = pl.BlockSpec((ts, D), lambda i, j: (i, 0))
    return pl.pallas_call(
        body, name="mix_in_bwd", grid=(S // ts, N_CHIPS),
        in_specs=[pl.BlockSpec((1, ts, N), lambda i, j: (j, i, 0)), tok, tok, _full((3, D)),
                  pl.BlockSpec((1, 1, D, N), lambda i, j: (j, l, 0, 0))],
        out_specs=[tok, _full((3, D))],
        out_shape=[jax.ShapeDtypeStruct((S, D), f32), jax.ShapeDtypeStruct((3, D), f32)],
        scratch_shapes=[pltpu.VMEM((ts, D), f32)],
        compiler_params=_cp(("arbitrary", "arbitrary"), 40),
    )(dz, dx_res, x, mod3, w_in)


def _mix_in_wgrad(h, dz, gw, l):
    S, D = h.shape
    N = dz.shape[-1]
    tk = TOK_TILE
    nk = S // tk

    def body(h_ref, dz_ref, _g, gw_ref, acc_sc):
        k = pl.program_id(1)

        @pl.when(k == 0)
        def _():
            acc_sc[...] = jnp.zeros_like(acc_sc)

        acc_sc[...] += _dot_tn(h_ref[...], dz_ref[0])

        @pl.when(k == nk - 1)
        def _():
            gw_ref[0, 0] = acc_sc[...].astype(bf16)

    return pl.pallas_call(
        body, name="mix_in_wgrad", grid=(N_CHIPS, nk),
        in_specs=[pl.BlockSpec((tk, D), lambda p, k: (k, 0)), pl.BlockSpec((1, tk, N), lambda p, k: (p, k, 0)),
                  pl.BlockSpec(memory_space=pl.ANY)],
        out_specs=pl.BlockSpec((1, 1, D, N), lambda p, k: (p, l, 0, 0)),
        out_shape=jax.ShapeDtypeStruct(gw.shape, bf16),
        scratch_shapes=[pltpu.VMEM((D, N), f32)],
        input_output_aliases={2: 0},
        compiler_params=_cp(("parallel", "arbitrary"), 40),
    )(h, dz, gw)


def _mix_out_fwd(x, y_att, y_ssm, y_pool, mod3, w_out, l, lng, lnb):
    S, D = x.shape
    ts = TOK_TILE

    def body(x_ref, ya_ref, ys_ref, yp_ref, mod_ref, w_ref, lng_ref, lnb_ref, xo_ref, y_ref):
        ya = ya_ref[...].astype(bf16)
        y = (_dot(ya[:, 0:256], w_ref[0, 0]) + _dot(ya[:, 256:512], w_ref[1, 0])
             + _dot(ys_ref[...].astype(bf16), w_ref[2, 0]) + _dot(yp_ref[...].astype(bf16), w_ref[3, 0]))
        y_ref[...] = y
        rh, _ = _ln_stats(ALPHA * x_ref[...] + mod_ref[2:3, :] * y)
        xo_ref[...] = rh * lng_ref[...] + lnb_ref[...]

    tok = pl.BlockSpec((ts, D), lambda i: (i, 0))
    return pl.pallas_call(
        body, name="mix_out_fwd", grid=(S // ts,),
        in_specs=[tok, pl.BlockSpec((ts, D_ATT), lambda i: (i, 0)), pl.BlockSpec((ts, D_SSM), lambda i: (i, 0)),
                  pl.BlockSpec((ts, D_POOL), lambda i: (i, 0)), _full((3, D)),
                  pl.BlockSpec((N_CHIPS, 1, 256, D), lambda i: (0, l, 0, 0)), _full((1, D)), _full((1, D))],
        out_specs=[tok, tok],
        out_shape=[jax.ShapeDtypeStruct((S, D), f32), jax.ShapeDtypeStruct((S, D), f32)],
        compiler_params=_cp(("parallel",), 40),
    )(x, y_att, y_ssm, y_pool, mod3, w_out, lng, lnb)


def _mix_out_bwd(dxo, x, y, y_att, y_ssm, y_pool, mod3, w_out, l, lng, gw_out):
    S, D = x.shape
    ts = TOK_TILE
    nt = S // ts

    def body(dxo_ref, x_ref, y_ref, ya_ref, ys_ref, yp_ref, mod_ref, w_ref, lng_ref, _g,
             dxr_ref, da_ref, ds_ref, dp_ref, dgate_ref, dlng_ref, dlnb_ref, gw_ref, acc_sc):
        i = pl.program_id(0)

        @pl.when(i == 0)
        def _():
            dgate_ref[...] = jnp.zeros_like(dgate_ref)
            dlng_ref[...] = jnp.zeros_like(dlng_ref)
            dlnb_ref[...] = jnp.zeros_like(dlnb_ref)
            acc_sc[...] = jnp.zeros_like(acc_sc)

        gate = mod_ref[2:3, :]
        yv = y_ref[...]
        rh, rstd = _ln_stats(ALPHA * x_ref[...] + gate * yv)
        dy_out = dxo_ref[...]
        dlng_ref[...] += jnp.sum(dy_out * rh, 0, keepdims=True)
        dlnb_ref[...] += jnp.sum(dy_out, 0, keepdims=True)
        dr = _ln_bwd(dy_out * lng_ref[...], rh, rstd)
        dxr_ref[...] = ALPHA * dr
        dgate_ref[...] += jnp.sum(dr * yv, 0, keepdims=True)
        dy = (gate * dr).astype(bf16)
        da_ref[:, 0:256] = _dot_nt(dy, w_ref[0, 0])
        da_ref[:, 256:512] = _dot_nt(dy, w_ref[1, 0])
        ds_ref[...] = _dot_nt(dy, w_ref[2, 0])
        dp_ref[...] = _dot_nt(dy, w_ref[3, 0])
        ya = ya_ref[...].astype(bf16)
        acc_sc[0] += _dot_tn(ya[:, 0:256], dy)
        acc_sc[1] += _dot_tn(ya[:, 256:512], dy)
        acc_sc[2] += _dot_tn(ys_ref[...].astype(bf16), dy)
        acc_sc[3] += _dot_tn(yp_ref[...].astype(bf16), dy)

        @pl.when(i == nt - 1)
        def _():
            gw_ref[:, 0] = acc_sc[...].astype(bf16)

    tok = pl.BlockSpec((ts, D), lambda i: (i, 0))
    t512 = pl.BlockSpec((ts, D_ATT), lambda i: (i, 0))
    t256 = pl.BlockSpec((ts, 256), lambda i: (i, 0))
    wspec = pl.BlockSpec((N_CHIPS, 1, 256, D), lambda i: (0, l, 0, 0))
    return pl.pallas_call(
        body, name="mix_out_bwd", grid=(nt,),
        in_specs=[tok, tok, tok, t512, t256, t256, _full((3, D)), wspec, _full((1, D)), pl.BlockSpec(memory_space=pl.ANY)],
        out_specs=[tok, t512, t256, t256, _full((1, D)), _full((1, D)), _full((1, D)), wspec],
        out_shape=[jax.ShapeDtypeStruct((S, D), f32), jax.ShapeDtypeStruct((S, D_ATT), f32),
                   jax.ShapeDtypeStruct((S, D_SSM), f32), jax.ShapeDtypeStruct((S, D_POOL), f32),
                   jax.ShapeDtypeStruct((1, D), f32), jax.ShapeDtypeStruct((1, D), f32), jax.ShapeDtypeStruct((1, D), f32),
                   jax.ShapeDtypeStruct(gw_out.shape, bf16)],
        scratch_shapes=[pltpu.VMEM((N_CHIPS, 256, D), f32)],
        input_output_aliases={9: 7},
        compiler_params=_cp(("arbitrary",), 48),
    )(dxo, x, y, y_att, y_ssm, y_pool, mod3, w_out, lng, gw_out)


def _t5_bucket(dist):
    max_exact = N_BUCKETS // 2
    d = np.maximum(dist, 1).astype(np.float32)
    large = max_exact + (np.log(d / max_exact) / math.log(MAX_DISTANCE / max_exact)
                         * (N_BUCKETS - max_exact)).astype(np.int32)
    large = np.minimum(large, N_BUCKETS - 1)
    return np.where(dist < max_exact, dist, large).astype(np.int32)


def _bucket_table():
    q = ATT_BLOCK
    i = np.arange(q)[:, None]
    j = np.arange(2 * q)[None, :]
    r = i + q - j
    in_band = (r >= 0) & (r <= q)
    tabs = [np.where(in_band, _t5_bucket(np.clip(r, 0, None) * d), -1) for d in DILATIONS]
    return np.stack(tabs).astype(np.int32)


def _bias_fwd(rel_bias, table):
    def body(rb_ref, tab_ref, out_ref):
        for b in range(3):
            tb = tab_ref[b]
            for h in range(N_HEADS):
                def pick(k, acc):
                    return jnp.where(tb == k, rb_ref[k, h], acc)
                out_ref[b, h] = lax.fori_loop(0, N_BUCKETS, pick, jnp.where(tb < 0, NEG, 0.0).astype(f32))

    return pl.pallas_call(
        body, name="bias_fwd",
        in_specs=[pl.BlockSpec(memory_space=pltpu.SMEM), pl.BlockSpec(memory_space=pltpu.VMEM)],
        out_specs=pl.BlockSpec(memory_space=pltpu.VMEM),
        out_shape=jax.ShapeDtypeStruct((3, N_HEADS, ATT_BLOCK, 2 * ATT_BLOCK), f32),
    )(rel_bias, table)


def _bias_bwd(dbias, table):
    def body(db_ref, tab_ref, out_ref):
        def per_bucket(k, c):
            for h in range(N_HEADS):
                tot = jnp.zeros((), f32)
                for b in range(3):
                    tot = tot + jnp.sum(jnp.where(tab_ref[b] == k, db_ref[b, h], 0.0))
                out_ref[k, h] = tot
            return c
        lax.fori_loop(0, N_BUCKETS, per_bucket, 0)

    return pl.pallas_call(
        body, name="bias_bwd",
        in_specs=[pl.BlockSpec(memory_space=pltpu.VMEM), pl.BlockSpec(memory_space=pltpu.VMEM)],
        out_specs=pl.BlockSpec(memory_space=pltpu.SMEM),
        out_shape=jax.ShapeDtypeStruct((N_BUCKETS, N_HEADS), f32),
    )(dbias, table)


def _att_scores(q_ref, k_ref, b_ref, r0, h, valid_prev):
    cs = pl.ds(HEAD_DIM * h, HEAD_DIM)
    q = q_ref[0, pl.ds(r0, ATT_BLOCK), cs]
    kb = k_ref[0, pl.ds(r0, 2 * ATT_BLOCK), cs]
    s = _dot_nt(q, kb) + b_ref[0, h]
    col = lax.broadcasted_iota(jnp.int32, s.shape, 1)
    return q, kb, jnp.where((col >= ATT_BLOCK) | valid_prev, s, NEG)


def _blocks_per_residue():
    br = pl.program_id(0)
    return jnp.where(br == 0, BLOCKS_PER_RESIDUE[0], jnp.where(br == 1, BLOCKS_PER_RESIDUE[1], BLOCKS_PER_RESIDUE[2]))


def _att_fwd(qp, kp, vp, bias):
    S = qp.shape[1]

    def body(q_ref, k_ref, v_ref, b_ref, o_ref, lse_ref):
        nbr = _blocks_per_residue()

        def unit(u, c):
            r0 = pl.multiple_of(u * ATT_BLOCK, ATT_BLOCK)
            valid_prev = (u % nbr) != 0
            for h in range(N_HEADS):
                cs = pl.ds(HEAD_DIM * h, HEAD_DIM)
                _, _, s = _att_scores(q_ref, k_ref, b_ref, r0, h, valid_prev)
                m = jnp.max(s, -1, keepdims=True)
                p = jnp.exp(s - m)
                den = jnp.sum(p, -1, keepdims=True)
                vb = v_ref[0, pl.ds(r0, 2 * ATT_BLOCK), cs]
                o_ref[0, pl.ds(r0, ATT_BLOCK), cs] = _dot(p.astype(bf16), vb) / den
                lse_ref[0, pl.ds(r0, ATT_BLOCK), pl.ds(h, 1)] = m + jnp.log(den)
            return c

        lax.fori_loop(0, N_UNITS, unit, 0)

    qspec = pl.BlockSpec((1, S, D_ATT), lambda b: (b, 0, 0))
    kspec = pl.BlockSpec((1, S + ATT_BLOCK, D_ATT), lambda b: (b, 0, 0))
    return pl.pallas_call(
        body, name="att_fwd", grid=(3,),
        in_specs=[qspec, kspec, kspec, pl.BlockSpec((1, N_HEADS, ATT_BLOCK, 2 * ATT_BLOCK), lambda b: (b, 0, 0, 0))],
        out_specs=[qspec, pl.BlockSpec((1, S, N_HEADS), lambda b: (b, 0, 0))],
        out_shape=[jax.ShapeDtypeStruct((3, S, D_ATT), f32), jax.ShapeDtypeStruct((3, S, N_HEADS), f32)],
        compiler_params=_cp(("parallel",), 48),
    )(qp, kp, vp, bias)


def _att_bwd(qp, kp, vp, dop, lsep, cp, bias):
    S = qp.shape[1]

    def body(q_ref, k_ref, v_ref, do_ref, lse_ref, c_ref, b_ref, dq_ref, dk_ref, dv_ref, db_ref, dk_sc, dv_sc):
        nbr = _blocks_per_residue()
        dk_sc[...] = jnp.zeros_like(dk_sc)
        dv_sc[...] = jnp.zeros_like(dv_sc)
        db_ref[...] = jnp.zeros_like(db_ref)

        def unit(u, c):
            r0 = pl.multiple_of(u * ATT_BLOCK, ATT_BLOCK)
            valid_prev = (u % nbr) != 0
            rows = pl.ds(r0, ATT_BLOCK)
            band = pl.ds(r0, 2 * ATT_BLOCK)
            for h in range(N_HEADS):
                cs = pl.ds(HEAD_DIM * h, HEAD_DIM)
                q, kb, s = _att_scores(q_ref, k_ref, b_ref, r0, h, valid_prev)
                p = jnp.exp(s - lse_ref[0, rows, pl.ds(h, 1)])
                do = do_ref[0, rows, cs]
                vb = v_ref[0, band, cs]
                ds = p * (_dot_nt(do, vb) - c_ref[0, rows, pl.ds(h, 1)])
                db_ref[0, h] += ds
                dsb = ds.astype(bf16)
                dq_ref[0, rows, cs] = (HEAD_DIM ** -0.5 * _dot(dsb, kb)).astype(bf16)
                dk_sc[band, cs] += _dot_tn(dsb, q)
                dv_sc[band, cs] += _dot_tn(p.astype(bf16), do)
            return c

        lax.fori_loop(0, N_UNITS, unit, 0)
        dk_ref[0] = dk_sc[...].astype(bf16)
        dv_ref[0] = dv_sc[...].astype(bf16)

    qspec = pl.BlockSpec((1, S, D_ATT), lambda b: (b, 0, 0))
    kspec = pl.BlockSpec((1, S + ATT_BLOCK, D_ATT), lambda b: (b, 0, 0))
    hspec = pl.BlockSpec((1, S, N_HEADS), lambda b: (b, 0, 0))
    bspec = pl.BlockSpec((1, N_HEADS, ATT_BLOCK, 2 * ATT_BLOCK), lambda b: (b, 0, 0, 0))
    return pl.pallas_call(
        body, name="att_bwd", grid=(3,),
        in_specs=[qspec, kspec, kspec, qspec, hspec, hspec, bspec],
        out_specs=[qspec, kspec, kspec, bspec],
        out_shape=[jax.ShapeDtypeStruct((3, S, D_ATT), bf16), jax.ShapeDtypeStruct((3, S + ATT_BLOCK, D_ATT), bf16),
                   jax.ShapeDtypeStruct((3, S + ATT_BLOCK, D_ATT), bf16),
                   jax.ShapeDtypeStruct((3, N_HEADS, ATT_BLOCK, 2 * ATT_BLOCK), f32)],
        scratch_shapes=[pltpu.VMEM((S + ATT_BLOCK, D_ATT), f32), pltpu.VMEM((S + ATT_BLOCK, D_ATT), f32)],
        compiler_params=_cp(("arbitrary",), 56),
    )(qp, kp, vp, dop, lsep, cp, bias)


def _branch_weights(lse_ref):
    l0, l1, l2 = lse_ref[0], lse_ref[1], lse_ref[2]
    m = jnp.maximum(jnp.maximum(l0, l1), l2)
    e0, e1, e2 = jnp.exp(l0 - m), jnp.exp(l1 - m), jnp.exp(l2 - m)
    tot = e0 + e1 + e2
    return e0 / tot, e1 / tot, e2 / tot


def _att_merge(o3, lse3):
    S = o3.shape[1]
    ts = TOK_TILE

    def body(o_ref, lse_ref, y_ref):
        w = _branch_weights(lse_ref)
        for h in range(N_HEADS):
            cs = pl.ds(HEAD_DIM * h, HEAD_DIM)
            y_ref[:, cs] = (w[0][:, h:h + 1] * o_ref[0, :, cs] + w[1][:, h:h + 1] * o_ref[1, :, cs]
                            + w[2][:, h:h + 1] * o_ref[2, :, cs])

    return pl.pallas_call(
        body, name="att_merge", grid=(S // ts,),
        in_specs=[pl.BlockSpec((3, ts, D_ATT), lambda i: (0, i, 0)), pl.BlockSpec((3, ts, N_HEADS), lambda i: (0, i, 0))],
        out_specs=pl.BlockSpec((ts, D_ATT), lambda i: (i, 0)),
        out_shape=jax.ShapeDtypeStruct((S, D_ATT), f32),
        compiler_params=_cp(("parallel",)),
    )(o3, lse3)


def _att_merge_bwd(dy, y, lse3):
    S = dy.shape[0]
    ts = TOK_TILE

    def body(dy_ref, y_ref, lse_ref, do_ref, c_ref):
        w = _branch_weights(lse_ref)
        for h in range(N_HEADS):
            cs = pl.ds(HEAD_DIM * h, HEAD_DIM)
            dyh = dy_ref[:, cs]
            t = jnp.sum(dyh * y_ref[:, cs], -1, keepdims=True)
            for p in range(3):
                wp = w[p][:, h:h + 1]
                do_ref[p, :, cs] = (wp * dyh).astype(bf16)
                c_ref[p, :, pl.ds(h, 1)] = wp * t

    return pl.pallas_call(
        body, name="att_merge_bwd", grid=(S // ts,),
        in_specs=[pl.BlockSpec((ts, D_ATT), lambda i: (i, 0)), pl.BlockSpec((ts, D_ATT), lambda i: (i, 0)),
                  pl.BlockSpec((3, ts, N_HEADS), lambda i: (0, i, 0))],
        out_specs=[pl.BlockSpec((3, ts, D_ATT), lambda i: (0, i, 0)), pl.BlockSpec((3, ts, N_HEADS), lambda i: (0, i, 0))],
        out_shape=[jax.ShapeDtypeStruct((3, S, D_ATT), bf16), jax.ShapeDtypeStruct((3, S, N_HEADS), f32)],
        compiler_params=_cp(("parallel",)),
    )(dy, y, lse3)


def _ssm_scan(xr, xi, a2, reverse):
    S, N = xr.shape
    nst = S // SCAN_SEG

    def body(xr_ref, xi_ref, a_ref, sr_ref, si_ref):
        ar = jnp.broadcast_to(a_ref[0:1, :], (SCAN_SEG, N))
        ai = jnp.broadcast_to(a_ref[1:2, :], (SCAN_SEG, N))
        if reverse:
            ai = -ai
        row = lax.broadcasted_iota(jnp.int32, (SCAN_SEG, N), 0)
        zero = jnp.zeros((SCAN_SEG, N), f32)

        def tile(t):
            return pl.ds(pl.multiple_of((nst - 1 - t if reverse else t) * SCAN_SEG, SCAN_SEG), SCAN_SEG)

        def local(t, c):
            sr, si, pr, pi = c
            rows = tile(t)
            nsr = ar * sr - ai * si + xr_ref[rows, :]
            nsi = ar * si + ai * sr + xi_ref[rows, :]
            sr_ref[rows, :] = nsr
            si_ref[rows, :] = nsi
            return nsr, nsi, ar * pr - ai * pi, ar * pi + ai * pr

        fr, fi, apr, api = lax.fori_loop(0, nst, local, (zero, zero, zero + 1.0, zero))

        def shift(v):
            if reverse:
                return jnp.where(row == SCAN_SEG - 1, 0.0, pltpu.roll(v, SCAN_SEG - 1, axis=0))
            return jnp.where(row == 0, 0.0, pltpu.roll(v, 1, axis=0))

        cr, ci = zero, zero
        for _ in range(SCAN_SEG - 1):
            cr, ci = shift(fr + apr * cr - api * ci), shift(fi + apr * ci + api * cr)

        def fix(t, c):
            pr, pi = c
            npr, npi = ar * pr - ai * pi, ar * pi + ai * pr
            rows = tile(t)
            sr_ref[rows, :] += npr * cr - npi * ci
            si_ref[rows, :] += npr * ci + npi * cr
            return npr, npi

        lax.fori_loop(0, nst, fix, (zero + 1.0, zero))

    vm = pl.BlockSpec(memory_space=pltpu.VMEM)
    return pl.pallas_call(
        body, name="ssm_scan_rev" if reverse else "ssm_scan",
        in_specs=[vm, vm, vm], out_specs=[vm, vm],
        out_shape=[jax.ShapeDtypeStruct((S, N), f32), jax.ShapeDtypeStruct((S, N), f32)],
        compiler_params=_cp(None, 48),
    )(xr, xi, a2)


def _ssm_in(u, bre, bim):
    S = u.shape[0]
    ts = TOK_TILE

    def body(u_ref, br_ref, bi_ref, or_ref, oi_ref):
        ub = u_ref[...].astype(bf16)
        or_ref[...] = _dot(ub, br_ref[...].astype(bf16))
        oi_ref[...] = _dot(ub, bi_ref[...].astype(bf16))

    return pl.pallas_call(
        body, name="ssm_in", grid=(S // ts,),
        in_specs=[pl.BlockSpec((ts, D_SSM), lambda i: (i, 0)), _full((D_SSM, D_STATE)), _full((D_SSM, D_STATE))],
        out_specs=[pl.BlockSpec((ts, D_STATE), lambda i: (i, 0))] * 2,
        out_shape=[jax.ShapeDtypeStruct((S, D_STATE), f32)] * 2,
        compiler_params=_cp(("parallel",)),
    )(u, bre, bim)


def _ssm_out(sr, si, u, cre, cim, dskip, glu_w, glu_b):
    S = u.shape[0]
    ts = TOK_TILE

    def body(sr_ref, si_ref, u_ref, cr_ref, ci_ref, d_ref, w_ref, b_ref, out_ref, y_ref):
        y = (_dot(sr_ref[...].astype(bf16), cr_ref[...].astype(bf16))
             - _dot(si_ref[...].astype(bf16), ci_ref[...].astype(bf16)) + d_ref[...] * u_ref[...])
        y_ref[...] = y
        z = _dot(_gelu(y).astype(bf16), w_ref[...].astype(bf16)) + b_ref[...]
        out_ref[...] = y * jax.nn.sigmoid(z)

    st = pl.BlockSpec((ts, D_STATE), lambda i: (i, 0))
    ch = pl.BlockSpec((ts, D_SSM), lambda i: (i, 0))
    return pl.pallas_call(
        body, name="ssm_out", grid=(S // ts,),
        in_specs=[st, st, ch, _full((D_STATE, D_SSM)), _full((D_STATE, D_SSM)), _full((1, D_SSM)),
                  _full((D_SSM, D_SSM)), _full((1, D_SSM))],
        out_specs=[ch, ch],
        out_shape=[jax.ShapeDtypeStruct((S, D_SSM), f32)] * 2,
        compiler_params=_cp(("parallel",)),
    )(sr, si, u, cre, cim, dskip, glu_w, glu_b)


def _ssm_out_bwd(dout, y, u, sr, si, cre, cim, dskip, glu_w, glu_b):
    S = u.shape[0]
    ts = TOK_TILE

    def body(do_ref, y_ref, u_ref, sr_ref, si_ref, cr_ref, ci_ref, d_ref, w_ref, b_ref,
             gr_ref, gi_ref, du_ref, dcr_ref, dci_ref, dd_ref, dgb_ref, dgw_ref):
        @pl.when(pl.program_id(0) == 0)
        def _():
            for r in (dcr_ref, dci_ref, dd_ref, dgb_ref, dgw_ref):
                r[...] = jnp.zeros_like(r)

        y = y_ref[...]
        dout = do_ref[...]
        wb = w_ref[...].astype(bf16)
        ge = _gelu(y).astype(bf16)
        sz = jax.nn.sigmoid(_dot(ge, wb) + b_ref[...])
        dz = dout * y * sz * (1.0 - sz)
        dzb = dz.astype(bf16)
        dgb_ref[...] += jnp.sum(dz, 0, keepdims=True)
        dgw_ref[...] += _dot_tn(ge, dzb)
        dy = dout * sz + _gelu_grad(y) * _dot_nt(dzb, wb)
        uv = u_ref[...]
        dd_ref[...] += jnp.sum(dy * uv, 0, keepdims=True)
        du_ref[...] = dy * d_ref[...]
        dyb = dy.astype(bf16)
        gr_ref[...] = _dot_nt(dyb, cr_ref[...].astype(bf16))
        gi_ref[...] = -_dot_nt(dyb, ci_ref[...].astype(bf16))
        dcr_ref[...] += _dot_tn(sr_ref[...].astype(bf16), dyb)
        dci_ref[...] -= _dot_tn(si_ref[...].astype(bf16), dyb)

    st = pl.BlockSpec((ts, D_STATE), lambda i: (i, 0))
    ch = pl.BlockSpec((ts, D_SSM), lambda i: (i, 0))
    c_full = _full((D_STATE, D_SSM))
    return pl.pallas_call(
        body, name="ssm_out_bwd", grid=(S // ts,),
        in_specs=[ch, ch, ch, st, st, c_full, c_full, _full((1, D_SSM)), _full((D_SSM, D_SSM)), _full((1, D_SSM))],
        out_specs=[st, st, ch, c_full, c_full, _full((1, D_SSM)), _full((1, D_SSM)), _full((D_SSM, D_SSM))],
        out_shape=[jax.ShapeDtypeStruct((S, D_STATE), f32), jax.ShapeDtypeStruct((S, D_STATE), f32),
                   jax.ShapeDtypeStruct((S, D_SSM), f32), jax.ShapeDtypeStruct((D_STATE, D_SSM), f32),
                   jax.ShapeDtypeStruct((D_STATE, D_SSM), f32), jax.ShapeDtypeStruct((1, D_SSM), f32),
                   jax.ShapeDtypeStruct((1, D_SSM), f32), jax.ShapeDtypeStruct((D_SSM, D_SSM), f32)],
        compiler_params=_cp(("arbitrary",), 40),
    )(dout, y, u, sr, si, cre, cim, dskip, glu_w, glu_b)


def _ssm_in_bwd(lr, li, u, du_skip, bre, bim):
    S = u.shape[0]
    ts = TOK_TILE

    def body(lr_ref, li_ref, u_ref, dus_ref, br_ref, bi_ref, du_ref, dbr_ref, dbi_ref):
        @pl.when(pl.program_id(0) == 0)
        def _():
            dbr_ref[...] = jnp.zeros_like(dbr_ref)
            dbi_ref[...] = jnp.zeros_like(dbi_ref)

        lrb = lr_ref[...].astype(bf16)
        lib = li_ref[...].astype(bf16)
        du_ref[...] = dus_ref[...] + _dot_nt(lrb, br_ref[...].astype(bf16)) + _dot_nt(lib, bi_ref[...].astype(bf16))
        ub = u_ref[...].astype(bf16)
        dbr_ref[...] += _dot_tn(ub, lrb)
        dbi_ref[...] += _dot_tn(ub, lib)

    st = pl.BlockSpec((ts, D_STATE), lambda i: (i, 0))
    ch = pl.BlockSpec((ts, D_SSM), lambda i: (i, 0))
    b_full = _full((D_SSM, D_STATE))
    return pl.pallas_call(
        body, name="ssm_in_bwd", grid=(S // ts,),
        in_specs=[st, st, ch, ch, b_full, b_full],
        out_specs=[ch, b_full, b_full],
        out_shape=[jax.ShapeDtypeStruct((S, D_SSM), f32), jax.ShapeDtypeStruct((D_SSM, D_STATE), f32),
                   jax.ShapeDtypeStruct((D_SSM, D_STATE), f32)],
        compiler_params=_cp(("arbitrary",), 40),
    )(lr, li, u, du_skip, bre, bim)


def _ssm_da(lr, li, sr, si):
    S, N = lr.shape
    nst = S // SCAN_SEG

    def body(lr_ref, li_ref, sr_ref, si_ref, out_ref):
        row = lax.broadcasted_iota(jnp.int32, (SCAN_SEG, N), 0)
        last = pl.ds((nst - 1) * SCAN_SEG, SCAN_SEG)
        pr = jnp.where(row == 0, 0.0, pltpu.roll(sr_ref[last, :], 1, axis=0))
        pi = jnp.where(row == 0, 0.0, pltpu.roll(si_ref[last, :], 1, axis=0))
        first = pl.ds(0, SCAN_SEG)
        acc_r = lr_ref[first, :] * pr + li_ref[first, :] * pi
        acc_i = li_ref[first, :] * pr - lr_ref[first, :] * pi

        def step(t, c):
            acc_r, acc_i = c
            rows = pl.ds(pl.multiple_of(t * SCAN_SEG, SCAN_SEG), SCAN_SEG)
            prev = pl.ds(pl.multiple_of((t - 1) * SCAN_SEG, SCAN_SEG), SCAN_SEG)
            lrv, liv, srv, siv = lr_ref[rows, :], li_ref[rows, :], sr_ref[prev, :], si_ref[prev, :]
            return acc_r + lrv * srv + liv * siv, acc_i + liv * srv - lrv * siv

        acc_r, acc_i = lax.fori_loop(1, nst, step, (acc_r, acc_i))
        out_ref[0:1, :] = jnp.sum(acc_r, 0, keepdims=True)
        out_ref[1:2, :] = jnp.sum(acc_i, 0, keepdims=True)

    vm = pl.BlockSpec(memory_space=pltpu.VMEM)
    return pl.pallas_call(
        body, name="ssm_da", in_specs=[vm] * 4, out_specs=vm,
        out_shape=jax.ShapeDtypeStruct((2, N), f32),
        compiler_params=_cp(None, 48),
    )(lr, li, sr, si)


_POOL_TILE = 256


def _window_sums(xt, back):
    n = xt.shape[0]
    out = []
    ws = xt
    for k in (1, 2, 4, 8):
        ws = ws + pltpu.roll(ws, k if back else n - k, axis=0)
        out.append(ws)
    return out


def _pool_count(r0, w):
    t = r0 + lax.broadcasted_iota(jnp.int32, (_POOL_TILE, POOL_GROUP), 0)
    return jnp.minimum(t + 1, w).astype(f32)


def _pool_fwd(u_pad, pool_w, pool_scale):
    S = u_pad.shape[0] - POOL_HALO
    nt = S // _POOL_TILE

    def body(u_ref, w_ref, sc_ref, y_ref):
        def tile(t, c):
            r0 = pl.multiple_of(t * _POOL_TILE, _POOL_TILE)
            for g, w in enumerate(POOL_WINDOWS):
                cs = pl.ds(POOL_GROUP * g, POOL_GROUP)
                xt = u_ref[pl.ds(r0, _POOL_TILE + POOL_HALO), cs]
                ws = _window_sums(xt, True)[g][POOL_HALO:, :]
                pooled = ws / _pool_count(r0, w) - xt[POOL_HALO:, :]
                y_ref[pl.ds(r0, _POOL_TILE), cs] = _dot(pooled.astype(bf16), w_ref[g].astype(bf16)) * sc_ref[:, cs]
            return c
        lax.fori_loop(0, nt, tile, 0)

    vm = pl.BlockSpec(memory_space=pltpu.VMEM)
    return pl.pallas_call(
        body, name="pool_fwd", in_specs=[vm, vm, vm], out_specs=vm,
        out_shape=jax.ShapeDtypeStruct((S, D_POOL), f32),
    )(u_pad, pool_w, pool_scale)


def _pool_bwd(dy_pad, u_pad, pool_w, pool_scale):
    S = u_pad.shape[0] - POOL_HALO
    nt = S // _POOL_TILE
    n = _POOL_TILE + POOL_HALO

    def body(dy_ref, u_ref, w_ref, sc_ref, du_ref, dw_ref, dsc_ref):
        dw_ref[...] = jnp.zeros_like(dw_ref)
        dsc_ref[...] = jnp.zeros_like(dsc_ref)

        def tile(t, c):
            r0 = pl.multiple_of(t * _POOL_TILE, _POOL_TILE)
            for g, w in enumerate(POOL_WINDOWS):
                cs = pl.ds(POOL_GROUP * g, POOL_GROUP)
                wb = w_ref[g].astype(bf16)
                xt = u_ref[pl.ds(r0, n), cs]
                pooled = (_window_sums(xt, True)[g][POOL_HALO:, :] / _pool_count(r0, w) - xt[POOL_HALO:, :]).astype(bf16)
                dy = dy_ref[pl.ds(r0, _POOL_TILE), cs]
                dsc_ref[:, cs] += jnp.sum(dy * _dot(pooled, wb), 0, keepdims=True)
                dw_ref[g] += _dot_tn(pooled, (dy * sc_ref[:, cs]).astype(bf16))
                dyh = (dy_ref[pl.ds(r0, n), cs] * sc_ref[:, cs]).astype(bf16)
                dpl = _dot_nt(dyh, wb)
                cnt = jnp.minimum(r0 + lax.broadcasted_iota(jnp.int32, (n, POOL_GROUP), 0) + 1, w).astype(f32)
                lead = _window_sums(dpl / cnt, False)[g]
                du_ref[pl.ds(r0, _POOL_TILE), cs] = lead[:_POOL_TILE, :] - dpl[:_POOL_TILE, :]
            return c
        lax.fori_loop(0, nt, tile, 0)

    vm = pl.BlockSpec(memory_space=pltpu.VMEM)
    return pl.pallas_call(
        body, name="pool_bwd", in_specs=[vm, vm, vm, vm], out_specs=[vm, vm, vm],
        out_shape=[jax.ShapeDtypeStruct((S, D_POOL), f32), jax.ShapeDtypeStruct((4, POOL_GROUP, POOL_GROUP), f32),
                   jax.ShapeDtypeStruct((1, D_POOL), f32)],
    )(dy_pad, u_pad, pool_w, pool_scale)


def _loss_head(y, target):
    S, D = y.shape
    ts = TOK_TILE

    def body(y_ref, t_ref, loss_ref, dy_ref):
        @pl.when(pl.program_id(0) == 0)
        def _():
            loss_ref[...] = jnp.zeros_like(loss_ref)

        d = y_ref[...] - t_ref[...]
        dy_ref[...] = d * (1.0 / D)
        loss_ref[...] += 0.5 * jnp.sum(jnp.sum(d * d, -1, keepdims=True) * (1.0 / D), 0, keepdims=True)

    tok = pl.BlockSpec((ts, D), lambda i: (i, 0))
    return pl.pallas_call(
        body, name="loss_head", grid=(S // ts,),
        in_specs=[tok, tok], out_specs=[_full((1, 1)), tok],
        out_shape=[jax.ShapeDtypeStruct((1, 1), f32), jax.ShapeDtypeStruct((S, D), f32)],
        compiler_params=_cp(("arbitrary",)),
    )(y, target)


_ADA_COLS = 768


def _ada_fwd(c_all, ada_w, ada_b_cols):
    L, D, N = ada_w.shape
    B = c_all.shape[0]

    def body(c_ref, w_ref, b_ref, out_ref):
        cv = c_ref[...]
        cond = (cv * jax.nn.sigmoid(cv)).astype(bf16)
        out_ref[0] = _dot(cond, w_ref[0].astype(bf16)) + b_ref[0]

    return pl.pallas_call(
        body, name="ada_fwd", grid=(L, N // _ADA_COLS),
        in_specs=[_full((B, D)), pl.BlockSpec((1, D, _ADA_COLS), lambda l, j: (l, 0, j)),
                  pl.BlockSpec((1, 1, _ADA_COLS), lambda l, j: (l, 0, j))],
        out_specs=pl.BlockSpec((1, B, _ADA_COLS), lambda l, j: (l, 0, j)),
        out_shape=jax.ShapeDtypeStruct((L, B, N), f32),
        compiler_params=_cp(("parallel", "parallel")),
    )(c_all, ada_w, ada_b_cols)


def _ada_wgrad(c_all_t, dmod_cols):
    D, B = c_all_t.shape
    L, _, N = dmod_cols.shape

    def body(ct_ref, dm_ref, out_ref):
        cv = ct_ref[...]
        cond = cv * jax.nn.sigmoid(cv)
        acc = cond[:, 0:1] * dm_ref[0, 0:1, :]
        for b in range(1, B):
            acc = acc + cond[:, b:b + 1] * dm_ref[0, b:b + 1, :]
        out_ref[0] = acc

    return pl.pallas_call(
        body, name="ada_wgrad", grid=(L, N // _ADA_COLS),
        in_specs=[_full((D, B)), pl.BlockSpec((1, B, _ADA_COLS), lambda l, j: (l, 0, j))],
        out_specs=pl.BlockSpec((1, D, _ADA_COLS), lambda l, j: (l, 0, j)),
        out_shape=jax.ShapeDtypeStruct((L, D, N), f32),
        compiler_params=_cp(("parallel", "parallel")),
    )(c_all_t, dmod_cols)


def _adam_math(w, g, m, v):
    m = ADAM_B1 * m + (1.0 - ADAM_B1) * g
    v = ADAM_B2 * v + (1.0 - ADAM_B2) * (g * g)
    m_hat = m / (1.0 - ADAM_B1 ** ADAM_STEP)
    v_hat = v / (1.0 - ADAM_B2 ** ADAM_STEP)
    delta = -ADAM_LR * (m_hat / (jnp.sqrt(v_hat) + ADAM_EPS) + ADAM_WD * w)
    return delta, m, v


def _adamw(w, m, v, grads, row_tile):
    R, C = w.shape
    ng = len(grads)

    def body(*refs):
        w_ref, m_ref, v_ref = refs[:3]
        g_refs = refs[3:3 + ng]
        g_out, d_out, m_out, v_out = refs[3 + ng:]
        g = g_refs[0][...]
        for r in g_refs[1:]:
            g = g + r[...]
        delta, mn, vn = _adam_math(w_ref[...], g, m_ref[...], v_ref[...])
        g_out[...] = g
        d_out[...] = delta
        m_out[...] = mn
        v_out[...] = vn

    spec = pl.BlockSpec((row_tile, C), lambda i: (i, 0))
    shp = jax.ShapeDtypeStruct((R, C), f32)
    return pl.pallas_call(
        body, name="adamw", grid=(R // row_tile,),
        in_specs=[spec] * (3 + ng), out_specs=[spec] * 4, out_shape=[shp] * 4,
        compiler_params=_cp(("parallel",), 40),
    )(w, m, v, *grads)


def _pair_sum(g5, got, pc):
    _, LS, _, R2, C = g5.shape

    def body(pc_ref, own_ref, got_ref, out_ref):
        out_ref[0, 0] = (own_ref[0, 0, 0].astype(f32) + got_ref[0, 0].astype(f32)).astype(bf16)

    gs = pltpu.PrefetchScalarGridSpec(
        num_scalar_prefetch=1, grid=(N_CHIPS, LS),
        in_specs=[pl.BlockSpec((1, 1, 1, R2, C), lambda p, s, pc: (p, s, pc[1], 0, 0)),
                  pl.BlockSpec((1, 1, R2, C), lambda p, s, pc: (p, s, 0, 0))],
        out_specs=pl.BlockSpec((1, 1, R2, C), lambda p, s, pc: (p, s, 0, 0)),
    )
    return pl.pallas_call(
        body, name="pair_sum", grid_spec=gs, out_shape=jax.ShapeDtypeStruct((N_CHIPS, LS, R2, C), bf16),
        compiler_params=_cp(("parallel", "parallel")),
    )(pc, g5, got)


def _sum_shards(hsum, recv, pc):
    _, LS, R2, C = hsum.shape

    def body(pc_ref, own_ref, r_ref, out_ref):
        acc = own_ref[0, 0].astype(f32)
        for j in range(3):
            acc = acc + r_ref[j, 0].astype(f32)
        out_ref[0, 0] = acc

    gs = pltpu.PrefetchScalarGridSpec(
        num_scalar_prefetch=1, grid=(LS,),
        in_specs=[pl.BlockSpec((1, 1, R2, C), lambda s, pc: (pc[0], s, 0, 0)),
                  pl.BlockSpec((3, 1, R2, C), lambda s, pc: (0, s, 0, 0))],
        out_specs=pl.BlockSpec((1, 1, R2, C), lambda s, pc: (s, pc[1], 0, 0)),
    )
    return pl.pallas_call(
        body, name="sum_shards", grid_spec=gs, out_shape=jax.ShapeDtypeStruct((LS, 2, R2, C), f32),
        compiler_params=_cp(("parallel",)),
    )(pc, hsum, recv)


def _sum8(packs):
    _, R, C = packs.shape
    tr = R // 8 if R % 64 == 0 else R

    def body(p_ref, out_ref):
        acc = p_ref[0]
        for d in range(1, 8):
            acc = acc + p_ref[d]
        out_ref[...] = acc

    return pl.pallas_call(
        body, name="sum8", grid=(R // tr,),
        in_specs=[pl.BlockSpec((8, tr, C), lambda i: (0, i, 0))],
        out_specs=pl.BlockSpec((tr, C), lambda i: (i, 0)),
        out_shape=jax.ShapeDtypeStruct((R, C), f32),
        compiler_params=_cp(("parallel",)),
    )(packs)


def _allgather8(x_shard):
    m_per, n = x_shard.shape

    def body(x_ref, out_ref, send_sems, recv_sems, local_sem):
        x, y, c = lax.axis_index("x"), lax.axis_index("y"), lax.axis_index("c")
        me, sibling = (x, y, c), (x, y, 1 - c)
        chips = [(1 - x, y), (x, 1 - y), (1 - x, 1 - y)]

        def rows(px, py, pc):
            return out_ref.at[pl.ds((4 * px + 2 * py + pc) * m_per, m_per), :]

        def copy(k, block, to, src=None):
            return pltpu.make_async_remote_copy(
                src_ref=rows(*block) if src is None else src, dst_ref=rows(*block),
                send_sem=send_sems.at[k], recv_sem=recv_sems.at[k], device_id=to, device_id_type=MESH)

        mine = pltpu.make_async_copy(x_ref, rows(*me), local_sem)
        mine.start()
        first = [copy(0, me, sibling, src=x_ref)]
        first += [copy(1 + j, me, (*chip, c), src=x_ref) for j, chip in enumerate(chips)]
        for cp in first:
            cp.start()
        passed = [copy(4 + j, (*chip, c), sibling) for j, chip in enumerate(chips)]
        for j, chip in enumerate(chips):
            copy(1 + j, (*chip, c), me).wait_recv()
            passed[j].start()
        copy(0, sibling, me).wait_recv()
        for j, chip in enumerate(chips):
            copy(4 + j, (*chip, 1 - c), me).wait_recv()
        for cp in first + passed:
            cp.wait_send()
        mine.wait()

    return pl.pallas_call(
        body, name="allgather8",
        out_shape=jax.ShapeDtypeStruct((8 * m_per, n), x_shard.dtype),
        in_specs=[pl.BlockSpec(memory_space=pltpu.VMEM)],
        out_specs=pl.BlockSpec(memory_space=pltpu.VMEM),
        scratch_shapes=[pltpu.SemaphoreType.DMA((7,)), pltpu.SemaphoreType.DMA((7,)), pltpu.SemaphoreType.DMA],
        compiler_params=_cp(None, 48),
    )(x_shard)


def _other_chips():
    x, y = lax.axis_index("x"), lax.axis_index("y")
    return [(1 - x, y), (x, 1 - y), (1 - x, 1 - y)]


def _gather_weights(shards):
    n = len(shards)

    def body(*refs):
        ins, outs = refs[:n], refs[n:2 * n]
        ici_send, ici_recv, d2d_send, d2d_recv, local_sems = refs[2 * n:]
        x, y, c = lax.axis_index("x"), lax.axis_index("y"), lax.axis_index("c")
        p_me = 2 * x + y
        sibling = (x, y, 1 - c)
        chips = _other_chips()
        waits = []
        for a in range(n):
            loc = pltpu.make_async_copy(ins[a], outs[a].at[p_me], local_sems.at[a])
            loc.start()
            waits.append(loc.wait)
            for j, (cx, cy) in enumerate(chips):
                cp = pltpu.make_async_remote_copy(
                    src_ref=ins[a].at[:, c], dst_ref=outs[a].at[p_me, :, c], send_sem=ici_send.at[3 * a + j],
                    recv_sem=ici_recv.at[3 * a + j], device_id=(cx, cy, c), device_id_type=MESH)
                cp.start()
                waits.append(cp.wait_send)
        for a in range(n):
            for j, (cx, cy) in enumerate(chips):
                k = 3 * a + j
                landed = outs[a].at[2 * cx + cy, :, c]
                pltpu.make_async_remote_copy(src_ref=landed, dst_ref=landed, send_sem=ici_send.at[k],
                                             recv_sem=ici_recv.at[k], device_id=(cx, cy, c), device_id_type=MESH).wait_recv()
                fwd = pltpu.make_async_remote_copy(src_ref=landed, dst_ref=landed, send_sem=d2d_send.at[k],
                                                   recv_sem=d2d_recv.at[k], device_id=sibling, device_id_type=MESH)
                fwd.start()
                waits.append(fwd.wait_send)
        for a in range(n):
            for j, (cx, cy) in enumerate(chips):
                k = 3 * a + j
                theirs = outs[a].at[2 * cx + cy, :, 1 - c]
                pltpu.make_async_remote_copy(src_ref=theirs, dst_ref=theirs, send_sem=d2d_send.at[k],
                                             recv_sem=d2d_recv.at[k], device_id=sibling, device_id_type=MESH).wait_recv()
        for w in waits:
            w()

    hbm = pl.BlockSpec(memory_space=pl.ANY)
    return pl.pallas_call(
        body, name="gather_weights",
        out_shape=[jax.ShapeDtypeStruct((N_CHIPS,) + s.shape, s.dtype) for s in shards],
        in_specs=[hbm] * n, out_specs=[hbm] * n,
        scratch_shapes=[pltpu.SemaphoreType.DMA((3 * n,))] * 4 + [pltpu.SemaphoreType.DMA((n,))],
    )(*shards)


def _pair_exchange(g5s):
    n = len(g5s)

    def body(*refs):
        ins, outs = refs[:n], refs[n:2 * n]
        send_sems, recv_sems = refs[2 * n:]
        c = lax.axis_index("c")
        sibling = (lax.axis_index("x"), lax.axis_index("y"), 1 - c)
        copies = []
        for a in range(n):
            cp = pltpu.make_async_remote_copy(src_ref=ins[a].at[:, :, 1 - c], dst_ref=outs[a], send_sem=send_sems.at[a],
                                              recv_sem=recv_sems.at[a], device_id=sibling, device_id_type=MESH)
            cp.start()
            copies.append(cp)
        for cp in copies:
            cp.wait()

    hbm = pl.BlockSpec(memory_space=pl.ANY)
    return pl.pallas_call(
        body, name="pair_exchange",
        out_shape=[jax.ShapeDtypeStruct(g.shape[:2] + g.shape[3:], g.dtype) for g in g5s],
        in_specs=[hbm] * n, out_specs=[hbm] * n,
        scratch_shapes=[pltpu.SemaphoreType.DMA((n,)), pltpu.SemaphoreType.DMA((n,))],
    )(*g5s)


def _scatter_grads(hsums):
    n = len(hsums)

    def body(*refs):
        ins, outs = refs[:n], refs[n:2 * n]
        send_sems, recv_sems = refs[2 * n:]
        c = lax.axis_index("c")
        copies = []
        for a in range(n):
            for j, (cx, cy) in enumerate(_other_chips()):
                cp = pltpu.make_async_remote_copy(
                    src_ref=ins[a].at[2 * cx + cy], dst_ref=outs[a].at[j], send_sem=send_sems.at[3 * a + j],
                    recv_sem=recv_sems.at[3 * a + j], device_id=(cx, cy, c), device_id_type=MESH)
                cp.start()
                copies.append(cp)
        for cp in copies:
            cp.wait()

    hbm = pl.BlockSpec(memory_space=pl.ANY)
    return pl.pallas_call(
        body, name="scatter_grads",
        out_shape=[jax.ShapeDtypeStruct((3,) + g.shape[1:], g.dtype) for g in hsums],
        in_specs=[hbm] * n, out_specs=[hbm] * n,
        scratch_shapes=[pltpu.SemaphoreType.DMA((3 * n,)), pltpu.SemaphoreType.DMA((3 * n,))],
    )(*hsums)


def _swap_halves(fulls):
    n = len(fulls)

    def body(*refs):
        ins, outs = refs[:n], refs[n:2 * n]
        send_sems, recv_sems = refs[2 * n:]
        c = lax.axis_index("c")
        sibling = (lax.axis_index("x"), lax.axis_index("y"), 1 - c)
        copies = []
        for a in range(n):
            cp = pltpu.make_async_remote_copy(src_ref=outs[a].at[:, c], dst_ref=outs[a].at[:, c], send_sem=send_sems.at[a],
                                              recv_sem=recv_sems.at[a], device_id=sibling, device_id_type=MESH)
            cp.start()
            copies.append(cp)
        for a, cp in enumerate(copies):
            cp.wait_send()
            theirs = outs[a].at[:, 1 - c]
            pltpu.make_async_remote_copy(src_ref=theirs, dst_ref=theirs, send_sem=send_sems.at[a], recv_sem=recv_sems.at[a],
                                         device_id=sibling, device_id_type=MESH).wait_recv()

    hbm = pl.BlockSpec(memory_space=pl.ANY)
    return pl.pallas_call(
        body, name="swap_halves",
        out_shape=[jax.ShapeDtypeStruct(p.shape, p.dtype) for p in fulls],
        in_specs=[hbm] * n, out_specs=[hbm] * n,
        input_output_aliases={a: a for a in range(n)},
        scratch_shapes=[pltpu.SemaphoreType.DMA((n,)), pltpu.SemaphoreType.DMA((n,))],
    )(*fulls)


def _to_residue(t, d):
    s, c = t.shape
    return t.reshape(s // d, d, c).transpose(1, 0, 2).reshape(s, c)


def _from_residue(t, d):
    s, c = t.shape
    return t.reshape(d, s // d, c).transpose(1, 0, 2).reshape(s, c)


def _to_segments(t):
    s, c = t.shape
    return t.reshape(SCAN_SEG, s // SCAN_SEG, c).transpose(1, 0, 2).reshape(s, c)


def _from_segments(t):
    s, c = t.shape
    return t.reshape(s // SCAN_SEG, SCAN_SEG, c).transpose(1, 0, 2).reshape(s, c)


def _ssm_operators(a_re, a_im, log_dt, b_re, b_im, c_re, c_im):
    lam = lax.complex(a_re, a_im)
    dt = jnp.exp(log_dt)[:, None]
    a_bar = jnp.exp(lam * dt)
    b_bar = ((a_bar - 1.0) / lam)[:, :, None] * lax.complex(b_re, b_im)
    eye = jnp.eye(N_GROUPS, dtype=f32)

    def embed_b(t):
        return (jnp.transpose(t, (0, 2, 1))[:, :, None, :] * eye[:, None, :, None]).reshape(D_SSM, D_STATE)

    def embed_c(t):
        return (jnp.transpose(t, (0, 2, 1))[:, :, None, :] * eye[:, None, :, None]).reshape(D_STATE, D_SSM)

    a2 = jnp.stack([a_bar.real.reshape(D_STATE), a_bar.imag.reshape(D_STATE)])
    return a2, embed_b(b_bar.real), embed_b(b_bar.imag), embed_c(c_re), embed_c(c_im)


def _local_step(x, target, mod, W, small):
    table = jnp.asarray(_bucket_table())
    bias = _bias_fwd(small["rel_bias"], table)
    L = DEPTH
    saved = []
    ssm_ops = []
    for l in range(L):
        sv = {}
        m9 = mod[l]
        sv["x0"] = x
        x, sv["f0"], sv["g0"], sv["u0"] = _ffn_fwd(x, m9[0:3], W["gate"], W["up"], W["down"], 2 * l, small["ln_g"][l, 0:1],
                                                   small["ln_b"][l, 0:1])
        sv["x1"] = x
        z, sv["h1"] = _mix_in_fwd(x, m9[3:6], W["w_in"], l)
        q16 = (z[0] * HEAD_DIM ** -0.5).astype(bf16)
        k16 = z[1].astype(bf16)
        v16 = z[2].astype(bf16)
        pad = jnp.zeros((ATT_BLOCK, D_ATT), bf16)
        qp = jnp.stack([_to_residue(q16, d) for d in DILATIONS])
        kp = jnp.stack([jnp.concatenate([pad, _to_residue(k16, d)]) for d in DILATIONS])
        vp = jnp.stack([jnp.concatenate([pad, _to_residue(v16, d)]) for d in DILATIONS])
        op, lsep = _att_fwd(qp, kp, vp, bias)
        o3 = jnp.stack([_from_residue(op[b], d) for b, d in enumerate(DILATIONS)])
        lse3 = jnp.stack([_from_residue(lsep[b], d) for b, d in enumerate(DILATIONS)])
        y_att = _att_merge(o3, lse3)
        sv.update(qp=qp, kp=kp, vp=vp, lsep=lsep, lse3=lse3, y_att=y_att)

        prm = tuple(small[k][l] for k in ("ssm_a_re", "ssm_a_im", "ssm_log_dt", "ssm_b_re", "ssm_b_im", "ssm_c_re", "ssm_c_im"))
        (a2, bre, bim, cre, cim), ops_vjp = jax.vjp(_ssm_operators, *prm)
        ssm_ops.append(ops_vjp)
        u_ssm = _to_segments(z[3][:, :D_SSM])
        bur, bui = _ssm_in(u_ssm, bre, bim)
        sr, si = _ssm_scan(bur, bui, a2, False)
        dskip = small["ssm_d"][l][None, :]
        glu_b = small["glu_b"][l][None, :]
        out_seg, y_seg = _ssm_out(sr, si, u_ssm, cre, cim, dskip, small["glu_w"][l], glu_b)
        y_ssm = _from_segments(out_seg)
        sv.update(a2=a2, bre=bre, bim=bim, cre=cre, cim=cim, u_ssm=u_ssm, sr=sr, si=si, y_seg=y_seg, y_ssm=y_ssm)

        u_pool = jnp.concatenate([jnp.zeros((POOL_HALO, D_POOL), f32), z[3][:, D_SSM:]])
        y_pool = _pool_fwd(u_pool, small["pool_w"][l], small["pool_scale"][l][None, :])
        sv.update(u_pool=u_pool, y_pool=y_pool)

        x, sv["ymix"] = _mix_out_fwd(x, y_att, y_ssm, y_pool, m9[3:6], W["w_out"], l, small["ln_g"][l, 1:2], small["ln_b"][l, 1:2])
        sv["x2"] = x
        x, sv["f2"], sv["g2"], sv["u2"] = _ffn_fwd(x, m9[6:9], W["gate"], W["up"], W["down"], 2 * l + 1, small["ln_g"][l, 2:3],
                                                   small["ln_b"][l, 2:3])
        saved.append(sv)

    loss, dx = _loss_head(x, target)

    G = {k: lax.empty((N_CHIPS,) + W[k].shape[1:], bf16) for k in ("gate", "up", "down", "w_in", "w_out")}
    dmod = [None] * L
    dln_g = [None] * L
    dln_b = [None] * L
    sg = {k: [None] * L for k in ("ssm_a_re", "ssm_a_im", "ssm_log_dt", "ssm_b_re", "ssm_b_im", "ssm_c_re", "ssm_c_im",
                                  "ssm_d", "glu_w", "glu_b", "pool_w", "pool_scale")}
    dbias_tot = None
    for l in reversed(range(L)):
        sv = saved[l]
        m9 = mod[l]
        dx, dg, du, a, h, df, dm2, dlg2, dlb2 = _ffn_bwd(dx, sv["x2"], sv["f2"], sv["g2"], sv["u2"], m9[6:9], W["gate"], W["up"],
                                                        W["down"], 2 * l + 1, small["ln_g"][l, 2:3])
        G["gate"], G["up"], G["down"] = _ffn_wgrad(h, dg, du, a, df, G["gate"], G["up"], G["down"], 2 * l + 1)
        dxr, d_att, d_ssm, d_pool, dgate1, dlg1, dlb1, G["w_out"] = _mix_out_bwd(
            dx, sv["x1"], sv["ymix"], sv["y_att"], sv["y_ssm"], sv["y_pool"], m9[3:6], W["w_out"], l, small["ln_g"][l, 1:2], G["w_out"])
        do3, c3 = _att_merge_bwd(d_att, sv["y_att"], sv["lse3"])
        dop = jnp.stack([_to_residue(do3[b], d) for b, d in enumerate(DILATIONS)])
        cp = jnp.stack([_to_residue(c3[b], d) for b, d in enumerate(DILATIONS)])
        dqp, dkp, dvp, dbias = _att_bwd(sv["qp"], sv["kp"], sv["vp"], dop, sv["lsep"], cp, bias)
        dbias_tot = dbias if dbias_tot is None else dbias_tot + dbias

        def back(t3, padded):
            parts = [_from_residue(t3[b][ATT_BLOCK:] if padded else t3[b], d).astype(f32) for b, d in enumerate(DILATIONS)]
            return (parts[0] + parts[1] + parts[2]).astype(bf16)

        dq, dk, dv = back(dqp, False), back(dkp, True), back(dvp, True)
        d_seg = _to_segments(d_ssm)
        dskip = small["ssm_d"][l][None, :]
        glu_b = small["glu_b"][l][None, :]
        gsr, gsi, du_skip, dcre, dcim, dd, dglu_b, dglu_w = _ssm_out_bwd(
            d_seg, sv["y_seg"], sv["u_ssm"], sv["sr"], sv["si"], sv["cre"], sv["cim"], dskip, small["glu_w"][l], glu_b)
        lr, li = _ssm_scan(gsr, gsi, sv["a2"], True)
        du_seg, dbre, dbim = _ssm_in_bwd(lr, li, sv["u_ssm"], du_skip, sv["bre"], sv["bim"])
        da2 = _ssm_da(lr, li, sv["sr"], sv["si"])
        d_prm = ssm_ops[l]((da2, dbre, dbim, dcre, dcim))
        for k, v in zip(("ssm_a_re", "ssm_a_im", "ssm_log_dt", "ssm_b_re", "ssm_b_im", "ssm_c_re", "ssm_c_im"), d_prm):
            sg[k][l] = v
        sg["ssm_d"][l] = dd[0]
        sg["glu_b"][l] = dglu_b[0]
        sg["glu_w"][l] = dglu_w
        du_ssm = _from_segments(du_seg)
        dyp = jnp.concatenate([d_pool, jnp.zeros((POOL_HALO, D_POOL), f32)])
        du_pool, dpw, dps = _pool_bwd(dyp, sv["u_pool"], small["pool_w"][l], small["pool_scale"][l][None, :])
        sg["pool_w"][l] = dpw
        sg["pool_scale"][l] = dps[0]
        dz = jnp.stack([dq, dk, dv, jnp.concatenate([du_ssm, du_pool], axis=1).astype(bf16)])
        dx, dm1 = _mix_in_bwd(dz, dxr, sv["x1"], m9[3:6], W["w_in"], l)
        G["w_in"] = _mix_in_wgrad(sv["h1"], dz, G["w_in"], l)
        dm1 = jnp.concatenate([dm1[0:2], dgate1])
        dx, dg, du, a, h, df, dm0, dlg0, dlb0 = _ffn_bwd(dx, sv["x0"], sv["f0"], sv["g0"], sv["u0"], m9[0:3], W["gate"], W["up"],
                                                        W["down"], 2 * l, small["ln_g"][l, 0:1])
        G["gate"], G["up"], G["down"] = _ffn_wgrad(h, dg, du, a, df, G["gate"], G["up"], G["down"], 2 * l)
        dmod[l] = jnp.concatenate([dm0, dm1, dm2])
        dln_g[l] = jnp.concatenate([dlg0, dlg1, dlg2])
        dln_b[l] = jnp.concatenate([dlb0, dlb1, dlb2])

    small_grads = {k: jnp.stack(v) for k, v in sg.items()}
    small_grads["rel_bias"] = _bias_bwd(dbias_tot, table)
    small_grads["ln_g"] = jnp.stack(dln_g)
    small_grads["ln_b"] = jnp.stack(dln_b)
    return loss, dx, G, jnp.stack(dmod), small_grads


def _pack(arrs):
    flat = jnp.concatenate([a.reshape(-1).astype(f32) for a in arrs])
    n = flat.shape[0]
    npad = -(-n // 1024) * 1024
    return jnp.pad(flat, (0, npad - n)).reshape(npad // 128, 128)


def _unpack(buf, shapes):
    flat = buf.reshape(-1)
    out, off = [], 0
    for s in shapes:
        n = int(np.prod(s))
        out.append(flat[off:off + n].reshape(s))
        off += n
    return out


_REPL = ("rel_bias", "ada_b", "ssm_a_re", "ssm_a_im", "ssm_log_dt", "ssm_b_re", "ssm_b_im", "ssm_c_re", "ssm_c_im",
         "ssm_d", "glu_b", "pool_w", "pool_scale")
_SMALL_SHARDED = ("ln_g", "ln_b", "glu_w")
_BIG = ("ffn_w_gate", "ffn_w_up", "ffn_w_down", "w_in", "w_out")
_ORDER = ("rel_bias", "ada_w", "ada_b", "ln_g", "ln_b", "ffn_w_gate", "ffn_w_up", "ffn_w_down", "w_in", "w_out",
          "ssm_a_re", "ssm_a_im", "ssm_log_dt", "ssm_b_re", "ssm_b_im", "ssm_c_re", "ssm_c_im", "ssm_d", "glu_w",
          "glu_b", "pool_w", "pool_scale")


def kernel(x, c, rel_bias, ada_w, ada_b, ln_g, ln_b, ffn_w_gate, ffn_w_up, ffn_w_down, w_in, w_out, ssm_a_re, ssm_a_im, ssm_log_dt, ssm_b_re, ssm_b_im, ssm_c_re, ssm_c_im, ssm_d, glu_w, glu_b, pool_w, pool_scale, loss_target, m_rel_bias, m_ada_w, m_ada_b, m_ln_g, m_ln_b, m_ffn_w_gate, m_ffn_w_up, m_ffn_w_down, m_w_in, m_w_out, m_ssm_a_re, m_ssm_a_im, m_ssm_log_dt, m_ssm_b_re, m_ssm_b_im, m_ssm_c_re, m_ssm_c_im, m_ssm_d, m_glu_w, m_glu_b, m_pool_w, m_pool_scale, v_rel_bias, v_ada_w, v_ada_b, v_ln_g, v_ln_b, v_ffn_w_gate, v_ffn_w_up, v_ffn_w_down, v_w_in, v_w_out, v_ssm_a_re, v_ssm_a_im, v_ssm_log_dt, v_ssm_b_re, v_ssm_b_im, v_ssm_c_re, v_ssm_c_im, v_ssm_d, v_glu_w, v_glu_b, v_pool_w, v_pool_scale):
    args = dict(locals())
    w = {k: args[k] for k in _ORDER}
    m = {k: args["m_" + k] for k in _ORDER}
    v = {k: args["v_" + k] for k in _ORDER}
    L, D = DEPTH, D_MODEL
    ax, ay, ac = lax.axis_index("x"), lax.axis_index("y"), lax.axis_index("c")
    p_me = 2 * ax + ay
    dev = 4 * ax + 2 * ay + ac

    fwd_shapes = [(1, D), (L, 3, 256), (L, 3, 256), (L, 64, 256)]
    pack = _pack([c, ln_g, ln_b, glu_w])
    rows = pack.shape[0]
    allp = _allgather8(pack).reshape(8, rows, 128)
    per_dev = [_unpack(allp[d], fwd_shapes) for d in range(8)]
    c_all = jnp.concatenate([pd[0] for pd in per_dev])
    ln_g_full = jnp.concatenate([per_dev[2 * p][1] for p in range(N_CHIPS)], axis=-1)
    ln_b_full = jnp.concatenate([per_dev[2 * p][2] for p in range(N_CHIPS)], axis=-1)
    glu_w_full = jnp.concatenate([per_dev[2 * p][3] for p in range(N_CHIPS)], axis=1)

    ncol = ada_w.shape[-1]
    ada_b_cols = lax.dynamic_slice_in_dim(ada_b, p_me * ncol, ncol, axis=1)[:, None, :]
    mod_part = _ada_fwd(c_all, ada_w, ada_b_cols)
    mrows = L * 8 * ncol // 128
    mod_all = _allgather8(mod_part.reshape(mrows, 128)).reshape(8, L, 8, ncol)
    mod_mine = lax.dynamic_index_in_dim(mod_all, dev, axis=2, keepdims=False)
    mod = jnp.concatenate([mod_mine[2 * p] for p in range(N_CHIPS)], axis=-1).reshape(L, 9, D)

    shards = [ffn_w_gate.reshape(2 * L, D, FF_SHARD), ffn_w_up.reshape(2 * L, D, FF_SHARD),
              ffn_w_down.reshape(2 * L, FF_SHARD, D), w_in, w_out]
    halves = [s.astype(bf16).reshape(s.shape[0], 2, s.shape[1] // 2, s.shape[2]) for s in shards]
    gathered = [g.reshape((N_CHIPS,) + s.shape) for g, s in zip(_gather_weights(halves), shards)]
    W = dict(zip(("gate", "up", "down", "w_in", "w_out"), gathered))

    small = {k: w[k] for k in _REPL if k != "ada_b"}
    small.update(ln_g=ln_g_full, ln_b=ln_b_full, glu_w=glu_w_full)
    loss_dev, grad_x, G, dmod, sgrads = _local_step(x[0], loss_target[0], mod, W, small)
    loss = lax.psum(loss_dev[0, 0], ("x", "y", "c"))

    names = ("rel_bias", "ln_g", "ln_b", "ssm_a_re", "ssm_a_im", "ssm_log_dt", "ssm_b_re", "ssm_b_im", "ssm_c_re",
             "ssm_c_im", "ssm_d", "glu_w", "glu_b", "pool_w", "pool_scale")
    gpack = _pack([dmod] + [sgrads[k] for k in names])
    grows = gpack.shape[0]
    gall = _allgather8(gpack).reshape(8, grows, 128)
    gsum = _unpack(_sum8(gall), [(L, 9 * D)] + [sgrads[k].shape for k in names])
    red = dict(zip(("ada_b",) + names, gsum))
    red["ln_g"] = lax.dynamic_slice_in_dim(red["ln_g"], p_me * 256, 256, axis=2)
    red["ln_b"] = lax.dynamic_slice_in_dim(red["ln_b"], p_me * 256, 256, axis=2)
    red["glu_w"] = lax.dynamic_slice_in_dim(red["glu_w"], p_me * 64, 64, axis=1)

    dmod_all = gall[:, :L * 9 * D // 128].reshape(8, L, 9 * D)
    dmod_cols = jnp.transpose(lax.dynamic_slice_in_dim(dmod_all, p_me * ncol, ncol, axis=2), (1, 0, 2))
    g_ada_w = _ada_wgrad(jnp.transpose(c_all), dmod_cols)

    keys = ("gate", "up", "down", "w_in", "w_out")
    g5 = [G[k].reshape(G[k].shape[:2] + (2, G[k].shape[2] // 2, G[k].shape[3])) for k in keys]
    pc = jnp.stack([p_me, ac]).astype(jnp.int32)
    hsum = [_pair_sum(g, r, pc) for g, r in zip(g5, _pair_exchange(g5))]
    full = _swap_halves([_sum_shards(h, r, pc) for h, r in zip(hsum, _scatter_grads(hsum))])

    out_g, out_d, out_m, out_v = {}, {}, {}, {}
    for k, name in zip(keys, _BIG):
        shp = w[name].shape
        r2 = (int(np.prod(shp[:-1])), shp[-1])
        i = keys.index(k)
        res = _adamw(w[name].reshape(r2), m[name].reshape(r2), v[name].reshape(r2),
                     [full[i].reshape(r2)], 256)
        out_g[name], out_d[name], out_m[name], out_v[name] = [t.reshape(shp) for t in res]
    r2 = (L * D, ncol)
    res = _adamw(ada_w.reshape(r2), m["ada_w"].reshape(r2), v["ada_w"].reshape(r2), [g_ada_w.reshape(r2)], 128)
    out_g["ada_w"], out_d["ada_w"], out_m["ada_w"], out_v["ada_w"] = [t.reshape(ada_w.shape) for t in res]

    small_names = _REPL + _SMALL_SHARDED
    wp = _pack([w[k] for k in small_names])
    res = _adamw(wp, _pack([m[k] for k in small_names]), _pack([v[k] for k in small_names]),
                 [_pack([red[k] for k in small_names])], wp.shape[0])
    for t, dst in zip(res, (out_g, out_d, out_m, out_v)):
        for k, a in zip(small_names, _unpack(t, [w[k].shape for k in small_names])):
            dst[k] = a

    return (loss, grad_x[None], *[out_g[k] for k in _ORDER], *[out_d[k] for k in _ORDER],
            *[out_m[k] for k in _ORDER], *[out_v[k] for k in _ORDER])
```

```python
import functools
import math

import numpy as np
import jax
import jax.numpy as jnp
from jax import lax
from jax.experimental import pallas as pl
from jax.experimental.pallas import tpu as pltpu

f32 = jnp.float32
bf16 = jnp.bfloat16
MESH = pl.DeviceIdType.MESH

D_MODEL = 1024
SEQ = 2048
DEPTH = 2
HEAD_DIM = 64
N_HEADS = 8
D_ATT = 512
DILATIONS = (1, 4, 16)
BLOCKS_PER_RESIDUE = (16, 4, 1)
ATT_BLOCK = 128
N_UNITS = SEQ // ATT_BLOCK
N_GROUPS = 16
SSM_GROUP = 16
SSM_STATE = 64
D_SSM = 256
D_STATE = N_GROUPS * SSM_STATE
POOL_WINDOWS = (2, 4, 8, 16)
POOL_GROUP = 64
D_POOL = 256
POOL_HALO = 16
D_FF = 2816
N_BUCKETS = 32
MAX_DISTANCE = 2048
ALPHA = (2 * DEPTH) ** 0.25
FFN_RES = 0.5
LN_EPS = 1e-5
NEG = -1e30
N_CHIPS = 4
FF_SHARD = D_FF // N_CHIPS
SCAN_SEG = 8
SCAN_STEPS = SEQ // SCAN_SEG

ADAM_LR, ADAM_B1, ADAM_B2, ADAM_EPS, ADAM_WD, ADAM_STEP = 0.001, 0.9, 0.999, 1e-08, 0.01, 10

TOK_TILE = 512


def _cp(dims=None, vmem_mb=None):
    kw = {}
    if dims is not None:
        kw["dimension_semantics"] = dims
    if vmem_mb is not None:
        kw["vmem_limit_bytes"] = vmem_mb << 20
    return pltpu.CompilerParams(**kw)


def _dot(a, b):
    return jnp.dot(a, b, preferred_element_type=f32)


def _dot_nt(a, b):
    return lax.dot_general(a, b, (((1,), (1,)), ((), ())), preferred_element_type=f32)


def _dot_tn(a, b):
    return lax.dot_general(a, b, (((0,), (0,)), ((), ())), preferred_element_type=f32)


def _ln_stats(v):
    mu = jnp.mean(v, -1, keepdims=True)
    d = v - mu
    var = jnp.mean(d * d, -1, keepdims=True)
    rstd = lax.rsqrt(var + LN_EPS)
    return d * rstd, rstd


def _ln_bwd(dxh, xh, rstd):
    return rstd * (dxh - jnp.mean(dxh, -1, keepdims=True) - xh * jnp.mean(dxh * xh, -1, keepdims=True))


_GELU_C = math.sqrt(2.0 / math.pi)


def _gelu(y):
    return 0.5 * y * (1.0 + jnp.tanh(_GELU_C * (y + 0.044715 * y * y * y)))


def _gelu_grad(y):
    t = jnp.tanh(_GELU_C * (y + 0.044715 * y * y * y))
    return 0.5 * (1.0 + t) + 0.5 * y * (1.0 - t * t) * (_GELU_C * (1.0 + 3 * 0.044715 * y * y))


def _full(shape):
    return pl.BlockSpec(shape, lambda *_: (0,) * len(shape))


def _ffn_fwd(x, mod3, wg, wu, wd, ls, lng, lnb):
    S, D = x.shape
    Fs = wg.shape[-1]
    ts = TOK_TILE

    def body(x_ref, mod_ref, wg_ref, wu_ref, wd_ref, lng_ref, lnb_ref, xo_ref, f_ref, g_ref, u_ref, h_sc, acc_sc):
        j = pl.program_id(1)

        @pl.when(j == 0)
        def _():
            xh, _ = _ln_stats(x_ref[...])
            h_sc[...] = (xh * (1.0 + mod_ref[1:2, :]) + mod_ref[0:1, :]).astype(bf16)
            acc_sc[...] = jnp.zeros_like(acc_sc)

        h = h_sc[...]
        g = _dot(h, wg_ref[0, 0])
        u = _dot(h, wu_ref[0, 0])
        g_ref[0] = g.astype(bf16)
        u_ref[0] = u.astype(bf16)
        a = (g * jax.nn.sigmoid(g) * u).astype(bf16)
        acc_sc[...] += _dot(a, wd_ref[0, 0])

        @pl.when(j == N_CHIPS - 1)
        def _():
            f = acc_sc[...]
            f_ref[...] = f
            r = ALPHA * x_ref[...] + (FFN_RES * mod_ref[2:3, :]) * f
            rh, _ = _ln_stats(r)
            xo_ref[...] = rh * lng_ref[...] + lnb_ref[...]

    tok = pl.BlockSpec((ts, D), lambda i, j: (i, 0))
    wcol = pl.BlockSpec((1, 1, D, Fs), lambda i, j: (j, ls, 0, 0))
    wrow = pl.BlockSpec((1, 1, Fs, D), lambda i, j: (j, ls, 0, 0))
    hid = pl.BlockSpec((1, ts, Fs), lambda i, j: (j, i, 0))
    return pl.pallas_call(
        body, name="ffn_fwd", grid=(S // ts, N_CHIPS),
        in_specs=[tok, _full((3, D)), wcol, wcol, wrow, _full((1, D)), _full((1, D))],
        out_specs=[tok, tok, hid, hid],
        out_shape=[jax.ShapeDtypeStruct((S, D), f32), jax.ShapeDtypeStruct((S, D), f32),
                   jax.ShapeDtypeStruct((N_CHIPS, S, Fs), bf16), jax.ShapeDtypeStruct((N_CHIPS, S, Fs), bf16)],
        scratch_shapes=[pltpu.VMEM((ts, D), bf16), pltpu.VMEM((ts, D), f32)],
        compiler_params=_cp(("parallel", "arbitrary"), 56),
    )(x, mod3, wg, wu, wd, lng, lnb)


def _ffn_bwd(dxo, x, f, g, u, mod3, wg, wu, wd, ls, lng):
    S, D = x.shape
    Fs = wg.shape[-1]
    ts = TOK_TILE

    def body(dxo_ref, x_ref, f_ref, g_ref, u_ref, mod_ref, wg_ref, wu_ref, wd_ref, lng_ref,
             dx_ref, dg_ref, du_ref, a_ref, h_ref, df_ref, dmod_ref, dlng_ref, dlnb_ref,
             dr_sc, df_sc, acc_sc):
        i = pl.program_id(0)
        j = pl.program_id(1)

        @pl.when((i == 0) & (j == 0))
        def _():
            dmod_ref[...] = jnp.zeros_like(dmod_ref)
            dlng_ref[...] = jnp.zeros_like(dlng_ref)
            dlnb_ref[...] = jnp.zeros_like(dlnb_ref)

        @pl.when(j == 0)
        def _():
            xv = x_ref[...]
            fv = f_ref[...]
            gate = mod_ref[2:3, :]
            rh, rstd = _ln_stats(ALPHA * xv + (FFN_RES * gate) * fv)
            dy = dxo_ref[...]
            dlng_ref[...] += jnp.sum(dy * rh, 0, keepdims=True)
            dlnb_ref[...] += jnp.sum(dy, 0, keepdims=True)
            dr = _ln_bwd(dy * lng_ref[...], rh, rstd)
            dr_sc[...] = dr
            dmod_ref[2:3, :] += jnp.sum(FFN_RES * dr * fv, 0, keepdims=True)
            df = ((FFN_RES * gate) * dr).astype(bf16)
            df_sc[...] = df
            df_ref[...] = df
            xh, _ = _ln_stats(xv)
            h_ref[...] = (xh * (1.0 + mod_ref[1:2, :]) + mod_ref[0:1, :]).astype(bf16)
            acc_sc[...] = jnp.zeros_like(acc_sc)

        da = _dot_nt(df_sc[...], wd_ref[0, 0])
        gv = g_ref[0].astype(f32)
        uv = u_ref[0].astype(f32)
        sg = jax.nn.sigmoid(gv)
        si = gv * sg
        a_ref[0] = (si * uv).astype(bf16)
        dgv = (da * uv * (sg * (1.0 + gv * (1.0 - sg)))).astype(bf16)
        duv = (da * si).astype(bf16)
        dg_ref[0] = dgv
        du_ref[0] = duv
        acc_sc[...] += _dot_nt(dgv, wg_ref[0, 0]) + _dot_nt(duv, wu_ref[0, 0])

        @pl.when(j == N_CHIPS - 1)
        def _():
            dh = acc_sc[...]
            xh, rstd0 = _ln_stats(x_ref[...])
            dmod_ref[0:1, :] += jnp.sum(dh, 0, keepdims=True)
            dmod_ref[1:2, :] += jnp.sum(dh * xh, 0, keepdims=True)
            dx_ref[...] = _ln_bwd(dh * (1.0 + mod_ref[1:2, :]), xh, rstd0) + ALPHA * dr_sc[...]

    tok = pl.BlockSpec((ts, D), lambda i, j: (i, 0))
    wcol = pl.BlockSpec((1, 1, D, Fs), lambda i, j: (j, ls, 0, 0))
    wrow = pl.BlockSpec((1, 1, Fs, D), lambda i, j: (j, ls, 0, 0))
    hid = pl.BlockSpec((1, ts, Fs), lambda i, j: (j, i, 0))
    hid_shape = jax.ShapeDtypeStruct((N_CHIPS, S, Fs), bf16)
    return pl.pallas_call(
        body, name="ffn_bwd", grid=(S // ts, N_CHIPS),
        in_specs=[tok, tok, tok, hid, hid, _full((3, D)), wcol, wcol, wrow, _full((1, D))],
        out_specs=[tok, hid, hid, hid, tok, tok, _full((3, D)), _full((1, D)), _full((1, D))],
        out_shape=[jax.ShapeDtypeStruct((S, D), f32), hid_shape, hid_shape, hid_shape,
                   jax.ShapeDtypeStruct((S, D), bf16), jax.ShapeDtypeStruct((S, D), bf16),
                   jax.ShapeDtypeStruct((3, D), f32), jax.ShapeDtypeStruct((1, D), f32), jax.ShapeDtypeStruct((1, D), f32)],
        scratch_shapes=[pltpu.VMEM((ts, D), f32), pltpu.VMEM((ts, D), bf16), pltpu.VMEM((ts, D), f32)],
        compiler_params=_cp(("arbitrary", "arbitrary"), 56),
    )(dxo, x, f, g, u, mod3, wg, wu, wd, lng)


def _ffn_wgrad(h, dg, du, a, df, gwg, gwu, gwd, ls):
    S, D = h.shape
    Fs = dg.shape[-1]
    tk = TOK_TILE
    nk = S // tk

    def body(h_ref, dg_ref, du_ref, a_ref, df_ref, _g0, _g1, _g2, gwg_ref, gwu_ref, gwd_ref, ag_sc, au_sc, ad_sc):
        k = pl.program_id(1)

        @pl.when(k == 0)
        def _():
            ag_sc[...] = jnp.zeros_like(ag_sc)
            au_sc[...] = jnp.zeros_like(au_sc)
            ad_sc[...] = jnp.zeros_like(ad_sc)

        hv = h_ref[...]
        ag_sc[...] += _dot_tn(hv, dg_ref[0])
        au_sc[...] += _dot_tn(hv, du_ref[0])
        ad_sc[...] += _dot_tn(a_ref[0], df_ref[...])

        @pl.when(k == nk - 1)
        def _():
            gwg_ref[0, 0] = ag_sc[...].astype(bf16)
            gwu_ref[0, 0] = au_sc[...].astype(bf16)
            gwd_ref[0, 0] = ad_sc[...].astype(bf16)

    tok = pl.BlockSpec((tk, D), lambda p, k: (k, 0))
    hid = pl.BlockSpec((1, tk, Fs), lambda p, k: (p, k, 0))
    anyspec = pl.BlockSpec(memory_space=pl.ANY)
    ocol = pl.BlockSpec((1, 1, D, Fs), lambda p, k: (p, ls, 0, 0))
    orow = pl.BlockSpec((1, 1, Fs, D), lambda p, k: (p, ls, 0, 0))
    return pl.pallas_call(
        body, name="ffn_wgrad", grid=(N_CHIPS, nk),
        in_specs=[tok, hid, hid, hid, tok, anyspec, anyspec, anyspec],
        out_specs=[ocol, ocol, orow],
        out_shape=[jax.ShapeDtypeStruct(gwg.shape, bf16), jax.ShapeDtypeStruct(gwu.shape, bf16),
                   jax.ShapeDtypeStruct(gwd.shape, bf16)],
        scratch_shapes=[pltpu.VMEM((D, Fs), f32), pltpu.VMEM((D, Fs), f32), pltpu.VMEM((Fs, D), f32)],
        input_output_aliases={5: 0, 6: 1, 7: 2},
        compiler_params=_cp(("parallel", "arbitrary"), 48),
    )(h, dg, du, a, df, gwg, gwu, gwd)


def _mix_in_fwd(x, mod3, w_in, l):
    S, D = x.shape
    N = w_in.shape[-1]
    ts = TOK_TILE

    def body(x_ref, mod_ref, w_ref, z_ref, h_ref):
        @pl.when(pl.program_id(1) == 0)
        def _():
            xh, _ = _ln_stats(x_ref[...])
            h_ref[...] = (xh * (1.0 + mod_ref[1:2, :]) + mod_ref[0:1, :]).astype(bf16)

        z_ref[0] = _dot(h_ref[...], w_ref[0, 0])

    tok = pl.BlockSpec((ts, D), lambda i, j: (i, 0))
    return pl.pallas_call(
        body, name="mix_in_fwd", grid=(S // ts, N_CHIPS),
        in_specs=[tok, _full((3, D)), pl.BlockSpec((1, 1, D, N), lambda i, j: (j, l, 0, 0))],
        out_specs=[pl.BlockSpec((1, ts, N), lambda i, j: (j, i, 0)), tok],
        out_shape=[jax.ShapeDtypeStruct((N_CHIPS, S, N), f32), jax.ShapeDtypeStruct((S, D), bf16)],
        compiler_params=_cp(("parallel", "arbitrary"), 40),
    )(x, mod3, w_in)


def _mix_in_bwd(dz, dx_res, x, mod3, w_in, l):
    S, D = x.shape
    N = w_in.shape[-1]
    ts = TOK_TILE

    def body(dz_ref, dxr_ref, x_ref, mod_ref, w_ref, dx_ref, dmod_ref, acc_sc):
        i = pl.program_id(0)
        j = pl.program_id(1)

        @pl.when((i == 0) & (j == 0))
        def _():
            dmod_ref[...] = jnp.zeros_like(dmod_ref)

        @pl.when(j == 0)
        def _():
            acc_sc[...] = jnp.zeros_like(acc_sc)

        acc_sc[...] += _dot_nt(dz_ref[0], w_ref[0, 0])

        @pl.when(j == N_CHIPS - 1)
        def _():
            dh = acc_sc[...]
            xh, rstd0 = _ln_stats(x_ref[...])
            dmod_ref[0:1, :] += jnp.sum(dh, 0, keepdims=True)
            dmod_ref[1:2, :] += jnp.sum(dh * xh, 0, keepdims=True)
            dx_ref[...] = _ln_bwd(dh * (1.0 + mod_ref[1:2, :]), xh, rstd0) + dxr_ref[...]

    tok = pl.BlockSpec((ts, D), lambda i, j: (i, 0))
    return pl.pallas_call(
        body, name="mix_in_bwd", grid=(S // ts, N_CHIPS),
        in_specs=[pl.BlockSpec((1, ts, N), lambda i, j: (j, i, 0)), tok, tok, _full((3, D)),
                  pl.BlockSpec((1, 1, D, N), lambda i, j: (j, l, 0, 0))],
        out_specs=[tok, _full((3, D))],
        out_shape=[jax.ShapeDtypeStruct((S, D), f32), jax.ShapeDtypeStruct((3, D), f32)],
        scratch_shapes=[pltpu.VMEM((ts, D), f32)],
        compiler_params=_cp(("arbitrary", "arbitrary"), 40),
    )(dz, dx_res, x, mod3, w_in)


def _mix_in_wgrad(h, dz, gw, l):
    S, D = h.shape
    N = dz.shape[-1]
    tk = TOK_TILE
    nk = S // tk

    def body(h_ref, dz_ref, _g, gw_ref, acc_sc):
        k = pl.program_id(1)

        @pl.when(k == 0)
        def _():
            acc_sc[...] = jnp.zeros_like(acc_sc)

        acc_sc[...] += _dot_tn(h_ref[...], dz_ref[0])

        @pl.when(k == nk - 1)
        def _():
            gw_ref[0, 0] = acc_sc[...].astype(bf16)

    return pl.pallas_call(
        body, name="mix_in_wgrad", grid=(N_CHIPS, nk),
        in_specs=[pl.BlockSpec((tk, D), lambda p, k: (k, 0)), pl.BlockSpec((1, tk, N), lambda p, k: (p, k, 0)),
                  pl.BlockSpec(memory_space=pl.ANY)],
        out_specs=pl.BlockSpec((1, 1, D, N), lambda p, k: (p, l, 0, 0)),
        out_shape=jax.ShapeDtypeStruct(gw.shape, bf16),
        scratch_shapes=[pltpu.VMEM((D, N), f32)],
        input_output_aliases={2: 0},
        compiler_params=_cp(("parallel", "arbitrary"), 40),
    )(h, dz, gw)


def _mix_out_fwd(x, y_att, y_ssm, y_pool, mod3, w_out, l, lng, lnb):
    S, D = x.shape
    ts = TOK_TILE

    def body(x_ref, ya_ref, ys_ref, yp_ref, mod_ref, w_ref, lng_ref, lnb_ref, xo_ref, y_ref):
        ya = ya_ref[...].astype(bf16)
        y = (_dot(ya[:, 0:256], w_ref[0, 0]) + _dot(ya[:, 256:512], w_ref[1, 0])
             + _dot(ys_ref[...].astype(bf16), w_ref[2, 0]) + _dot(yp_ref[...].astype(bf16), w_ref[3, 0]))
        y_ref[...] = y
        rh, _ = _ln_stats(ALPHA * x_ref[...] + mod_ref[2:3, :] * y)
        xo_ref[...] = rh * lng_ref[...] + lnb_ref[...]

    tok = pl.BlockSpec((ts, D), lambda i: (i, 0))
    return pl.pallas_call(
        body, name="mix_out_fwd", grid=(S // ts,),
        in_specs=[tok, pl.BlockSpec((ts, D_ATT), lambda i: (i, 0)), pl.BlockSpec((ts, D_SSM), lambda i: (i, 0)),
                  pl.BlockSpec((ts, D_POOL), lambda i: (i, 0)), _full((3, D)),
                  pl.BlockSpec((N_CHIPS, 1, 256, D), lambda i: (0, l, 0, 0)), _full((1, D)), _full((1, D))],
        out_specs=[tok, tok],
        out_shape=[jax.ShapeDtypeStruct((S, D), f32), jax.ShapeDtypeStruct((S, D), f32)],
        compiler_params=_cp(("parallel",), 40),
    )(x, y_att, y_ssm, y_pool, mod3, w_out, lng, lnb)


def _mix_out_bwd(dxo, x, y, y_att, y_ssm, y_pool, mod3, w_out, l, lng, gw_out):
    S, D = x.shape
    ts = TOK_TILE
    nt = S // ts

    def body(dxo_ref, x_ref, y_ref, ya_ref, ys_ref, yp_ref, mod_ref, w_ref, lng_ref, _g,
             dxr_ref, da_ref, ds_ref, dp_ref, dgate_ref, dlng_ref, dlnb_ref, gw_ref, acc_sc):
        i = pl.program_id(0)

        @pl.when(i == 0)
        def _():
            dgate_ref[...] = jnp.zeros_like(dgate_ref)
            dlng_ref[...] = jnp.zeros_like(dlng_ref)
            dlnb_ref[...] = jnp.zeros_like(dlnb_ref)
            acc_sc[...] = jnp.zeros_like(acc_sc)

        gate = mod_ref[2:3, :]
        yv = y_ref[...]
        rh, rstd = _ln_stats(ALPHA * x_ref[...] + gate * yv)
        dy_out = dxo_ref[...]
        dlng_ref[...] += jnp.sum(dy_out * rh, 0, keepdims=True)
        dlnb_ref[...] += jnp.sum(dy_out, 0, keepdims=True)
        dr = _ln_bwd(dy_out * lng_ref[...], rh, rstd)
        dxr_ref[...] = ALPHA * dr
        dgate_ref[...] += jnp.sum(dr * yv, 0, keepdims=True)
        dy = (gate * dr).astype(bf16)
        da_ref[:, 0:256] = _dot_nt(dy, w_ref[0, 0])
        da_ref[:, 256:512] = _dot_nt(dy, w_ref[1, 0])
        ds_ref[...] = _dot_nt(dy, w_ref[2, 0])
        dp_ref[...] = _dot_nt(dy, w_ref[3, 0])
        ya = ya_ref[...].astype(bf16)
        acc_sc[0] += _dot_tn(ya[:, 0:256], dy)
        acc_sc[1] += _dot_tn(ya[:, 256:512], dy)
        acc_sc[2] += _dot_tn(ys_ref[...].astype(bf16), dy)
        acc_sc[3] += _dot_tn(yp_ref[...].astype(bf16), dy)

        @pl.when(i == nt - 1)
        def _():
            gw_ref[:, 0] = acc_sc[...].astype(bf16)

    tok = pl.BlockSpec((ts, D), lambda i: (i, 0))
    t512 = pl.BlockSpec((ts, D_ATT), lambda i: (i, 0))
    t256 = pl.BlockSpec((ts, 256), lambda i: (i, 0))
    wspec = pl.BlockSpec((N_CHIPS, 1, 256, D), lambda i: (0, l, 0, 0))
    return pl.pallas_call(
        body, name="mix_out_bwd", grid=(nt,),
        in_specs=[tok, tok, tok, t512, t256, t256, _full((3, D)), wspec, _full((1, D)), pl.BlockSpec(memory_space=pl.ANY)],
        out_specs=[tok, t512, t256, t256, _full((1, D)), _full((1, D)), _full((1, D)), wspec],
        out_shape=[jax.ShapeDtypeStruct((S, D), f32), jax.ShapeDtypeStruct((S, D_ATT), f32),
                   jax.ShapeDtypeStruct((S, D_SSM), f32), jax.ShapeDtypeStruct((S, D_POOL), f32),
                   jax.ShapeDtypeStruct((1, D), f32), jax.ShapeDtypeStruct((1, D), f32), jax.ShapeDtypeStruct((1, D), f32),
                   jax.ShapeDtypeStruct(gw_out.shape, bf16)],
        scratch_shapes=[pltpu.VMEM((N_CHIPS, 256, D), f32)],
        input_output_aliases={9: 7},
        compiler_params=_cp(("arbitrary",), 48),
    )(dxo, x, y, y_att, y_ssm, y_pool, mod3, w_out, lng, gw_out)


def _t5_bucket(dist):
    max_exact = N_BUCKETS // 2
    d = np.maximum(dist, 1).astype(np.float32)
    large = max_exact + (np.log(d / max_exact) / math.log(MAX_DISTANCE / max_exact)
                         * (N_BUCKETS - max_exact)).astype(np.int32)
    large = np.minimum(large, N_BUCKETS - 1)
    return np.where(dist < max_exact, dist, large).astype(np.int32)


def _bucket_table():
    q = ATT_BLOCK
    i = np.arange(q)[:, None]
    j = np.arange(2 * q)[None, :]
    r = i + q - j
    in_band = (r >= 0) & (r <= q)
    tabs = [np.where(in_band, _t5_bucket(np.clip(r, 0, None) * d), -1) for d in DILATIONS]
    return np.stack(tabs).astype(np.int32)


def _bias_fwd(rel_bias, table):
    def body(rb_ref, tab_ref, out_ref):
        for b in range(3):
            tb = tab_ref[b]
            for h in range(N_HEADS):
                def pick(k, acc):
                    return jnp.where(tb == k, rb_ref[k, h], acc)
                out_ref[b, h] = lax.fori_loop(0, N_BUCKETS, pick, jnp.where(tb < 0, NEG, 0.0).astype(f32))

    return pl.pallas_call(
        body, name="bias_fwd",
        in_specs=[pl.BlockSpec(memory_space=pltpu.SMEM), pl.BlockSpec(memory_space=pltpu.VMEM)],
        out_specs=pl.BlockSpec(memory_space=pltpu.VMEM),
        out_shape=jax.ShapeDtypeStruct((3, N_HEADS, ATT_BLOCK, 2 * ATT_BLOCK), f32),
    )(rel_bias, table)


def _bias_bwd(dbias, table):
    def body(db_ref, tab_ref, out_ref):
        def per_bucket(k, c):
            for h in range(N_HEADS):
                tot = jnp.zeros((), f32)
                for b in range(3):
                    tot = tot + jnp.sum(jnp.where(tab_ref[b] == k, db_ref[b, h], 0.0))
                out_ref[k, h] = tot
            return c
        lax.fori_loop(0, N_BUCKETS, per_bucket, 0)

    return pl.pallas_call(
        body, name="bias_bwd",
        in_specs=[pl.BlockSpec(memory_space=pltpu.VMEM), pl.BlockSpec(memory_space=pltpu.VMEM)],
        out_specs=pl.BlockSpec(memory_space=pltpu.SMEM),
        out_shape=jax.ShapeDtypeStruct((N_BUCKETS, N_HEADS), f32),
    )(dbias, table)


def _att_scores(q_ref, k_ref, b_ref, r0, h, valid_prev):
    cs = pl.ds(HEAD_DIM * h, HEAD_DIM)
    q = q_ref[0, pl.ds(r0, ATT_BLOCK), cs]
    kb = k_ref[0, pl.ds(r0, 2 * ATT_BLOCK), cs]
    s = _dot_nt(q, kb) + b_ref[0, h]
    col = lax.broadcasted_iota(jnp.int32, s.shape, 1)
    return q, kb, jnp.where((col >= ATT_BLOCK) | valid_prev, s, NEG)


def _blocks_per_residue():
    br = pl.program_id(0)
    return jnp.where(br == 0, BLOCKS_PER_RESIDUE[0], jnp.where(br == 1, BLOCKS_PER_RESIDUE[1], BLOCKS_PER_RESIDUE[2]))


def _att_fwd(qp, kp, vp, bias):
    S = qp.shape[1]

    def body(q_ref, k_ref, v_ref, b_ref, o_ref, lse_ref):
        nbr = _blocks_per_residue()

        def unit(u, c):
            r0 = pl.multiple_of(u * ATT_BLOCK, ATT_BLOCK)
            valid_prev = (u % nbr) != 0
            for h in range(N_HEADS):
                cs = pl.ds(HEAD_DIM * h, HEAD_DIM)
                _, _, s = _att_scores(q_ref, k_ref, b_ref, r0, h, valid_prev)
                m = jnp.max(s, -1, keepdims=True)
                p = jnp.exp(s - m)
                den = jnp.sum(p, -1, keepdims=True)
                vb = v_ref[0, pl.ds(r0, 2 * ATT_BLOCK), cs]
                o_ref[0, pl.ds(r0, ATT_BLOCK), cs] = _dot(p.astype(bf16), vb) / den
                lse_ref[0, pl.ds(r0, ATT_BLOCK), pl.ds(h, 1)] = m + jnp.log(den)
            return c

        lax.fori_loop(0, N_UNITS, unit, 0)

    qspec = pl.BlockSpec((1, S, D_ATT), lambda b: (b, 0, 0))
    kspec = pl.BlockSpec((1, S + ATT_BLOCK, D_ATT), lambda b: (b, 0, 0))
    return pl.pallas_call(
        body, name="att_fwd", grid=(3,),
        in_specs=[qspec, kspec, kspec, pl.BlockSpec((1, N_HEADS, ATT_BLOCK, 2 * ATT_BLOCK), lambda b: (b, 0, 0, 0))],
        out_specs=[qspec, pl.BlockSpec((1, S, N_HEADS), lambda b: (b, 0, 0))],
        out_shape=[jax.ShapeDtypeStruct((3, S, D_ATT), f32), jax.ShapeDtypeStruct((3, S, N_HEADS), f32)],
        compiler_params=_cp(("parallel",), 48),
    )(qp, kp, vp, bias)


def _att_bwd(qp, kp, vp, dop, lsep, cp, bias):
    S = qp.shape[1]

    def body(q_ref, k_ref, v_ref, do_ref, lse_ref, c_ref, b_ref, dq_ref, dk_ref, dv_ref, db_ref, dk_sc, dv_sc):
        nbr = _blocks_per_residue()
        dk_sc[...] = jnp.zeros_like(dk_sc)
        dv_sc[...] = jnp.zeros_like(dv_sc)
        db_ref[...] = jnp.zeros_like(db_ref)

        def unit(u, c):
            r0 = pl.multiple_of(u * ATT_BLOCK, ATT_BLOCK)
            valid_prev = (u % nbr) != 0
            rows = pl.ds(r0, ATT_BLOCK)
            band = pl.ds(r0, 2 * ATT_BLOCK)
            for h in range(N_HEADS):
                cs = pl.ds(HEAD_DIM * h, HEAD_DIM)
                q, kb, s = _att_scores(q_ref, k_ref, b_ref, r0, h, valid_prev)
                p = jnp.exp(s - lse_ref[0, rows, pl.ds(h, 1)])
                do = do_ref[0, rows, cs]
                vb = v_ref[0, band, cs]
                ds = p * (_dot_nt(do, vb) - c_ref[0, rows, pl.ds(h, 1)])
                db_ref[0, h] += ds
                dsb = ds.astype(bf16)
                dq_ref[0, rows, cs] = (HEAD_DIM ** -0.5 * _dot(dsb, kb)).astype(bf16)
                dk_sc[band, cs] += _dot_tn(dsb, q)
                dv_sc[band, cs] += _dot_tn(p.astype(bf16), do)
            return c

        lax.fori_loop(0, N_UNITS, unit, 0)
        dk_ref[0] = dk_sc[...].astype(bf16)
        dv_ref[0] = dv_sc[...].astype(bf16)

    qspec = pl.BlockSpec((1, S, D_ATT), lambda b: (b, 0, 0))
    kspec = pl.BlockSpec((1, S + ATT_BLOCK, D_ATT), lambda b: (b, 0, 0))
    hspec = pl.BlockSpec((1, S, N_HEADS), lambda b: (b, 0, 0))
    bspec = pl.BlockSpec((1, N_HEADS, ATT_BLOCK, 2 * ATT_BLOCK), lambda b: (b, 0, 0, 0))
    return pl.pallas_call(
        body, name="att_bwd", grid=(3,),
        in_specs=[qspec, kspec, kspec, qspec, hspec, hspec, bspec],
        out_specs=[qspec, kspec, kspec, bspec],
        out_shape=[jax.ShapeDtypeStruct((3, S, D_ATT), bf16), jax.ShapeDtypeStruct((3, S + ATT_BLOCK, D_ATT), bf16),
                   jax.ShapeDtypeStruct((3, S + ATT_BLOCK, D_ATT), bf16),
                   jax.ShapeDtypeStruct((3, N_HEADS, ATT_BLOCK, 2 * ATT_BLOCK), f32)],
        scratch_shapes=[pltpu.VMEM((S + ATT_BLOCK, D_ATT), f32), pltpu.VMEM((S + ATT_BLOCK, D_ATT), f32)],
        compiler_params=_cp(("arbitrary",), 56),
    )(qp, kp, vp, dop, lsep, cp, bias)


def _branch_weights(lse_ref):
    l0, l1, l2 = lse_ref[0], lse_ref[1], lse_ref[2]
    m = jnp.maximum(jnp.maximum(l0, l1), l2)
    e0, e1, e2 = jnp.exp(l0 - m), jnp.exp(l1 - m), jnp.exp(l2 - m)
    tot = e0 + e1 + e2
    return e0 / tot, e1 / tot, e2 / tot


def _att_merge(o3, lse3):
    S = o3.shape[1]
    ts = TOK_TILE

    def body(o_ref, lse_ref, y_ref):
        w = _branch_weights(lse_ref)
        for h in range(N_HEADS):
            cs = pl.ds(HEAD_DIM * h, HEAD_DIM)
            y_ref[:, cs] = (w[0][:, h:h + 1] * o_ref[0, :, cs] + w[1][:, h:h + 1] * o_ref[1, :, cs]
                            + w[2][:, h:h + 1] * o_ref[2, :, cs])

    return pl.pallas_call(
        body, name="att_merge", grid=(S // ts,),
        in_specs=[pl.BlockSpec((3, ts, D_ATT), lambda i: (0, i, 0)), pl.BlockSpec((3, ts, N_HEADS), lambda i: (0, i, 0))],
        out_specs=pl.BlockSpec((ts, D_ATT), lambda i: (i, 0)),
        out_shape=jax.ShapeDtypeStruct((S, D_ATT), f32),
        compiler_params=_cp(("parallel",)),
    )(o3, lse3)


def _att_merge_bwd(dy, y, lse3):
    S = dy.shape[0]
    ts = TOK_TILE

    def body(dy_ref, y_ref, lse_ref, do_ref, c_ref):
        w = _branch_weights(lse_ref)
        for h in range(N_HEADS):
            cs = pl.ds(HEAD_DIM * h, HEAD_DIM)
            dyh = dy_ref[:, cs]
            t = jnp.sum(dyh * y_ref[:, cs], -1, keepdims=True)
            for p in range(3):
                wp = w[p][:, h:h + 1]
                do_ref[p, :, cs] = (wp * dyh).astype(bf16)
                c_ref[p, :, pl.ds(h, 1)] = wp * t

    return pl.pallas_call(
        body, name="att_merge_bwd", grid=(S // ts,),
        in_specs=[pl.BlockSpec((ts, D_ATT), lambda i: (i, 0)), pl.BlockSpec((ts, D_ATT), lambda i: (i, 0)),
                  pl.BlockSpec((3, ts, N_HEADS), lambda i: (0, i, 0))],
        out_specs=[pl.BlockSpec((3, ts, D_ATT), lambda i: (0, i, 0)), pl.BlockSpec((3, ts, N_HEADS), lambda i: (0, i, 0))],
        out_shape=[jax.ShapeDtypeStruct((3, S, D_ATT), bf16), jax.ShapeDtypeStruct((3, S, N_HEADS), f32)],
        compiler_params=_cp(("parallel",)),
    )(dy, y, lse3)


def _ssm_scan(xr, xi, a2, reverse):
    S, N = xr.shape
    nst = S // SCAN_SEG

    def body(xr_ref, xi_ref, a_ref, sr_ref, si_ref):
        ar = jnp.broadcast_to(a_ref[0:1, :], (SCAN_SEG, N))
        ai = jnp.broadcast_to(a_ref[1:2, :], (SCAN_SEG, N))
        if reverse:
            ai = -ai
        row = lax.broadcasted_iota(jnp.int32, (SCAN_SEG, N), 0)
        zero = jnp.zeros((SCAN_SEG, N), f32)

        def tile(t):
            return pl.ds(pl.multiple_of((nst - 1 - t if reverse else t) * SCAN_SEG, SCAN_SEG), SCAN_SEG)

        def local(t, c):
            sr, si, pr, pi = c
            rows = tile(t)
            nsr = ar * sr - ai * si + xr_ref[rows, :]
            nsi = ar * si + ai * sr + xi_ref[rows, :]
            sr_ref[rows, :] = nsr
            si_ref[rows, :] = nsi
            return nsr, nsi, ar * pr - ai * pi, ar * pi + ai * pr

        fr, fi, apr, api = lax.fori_loop(0, nst, local, (zero, zero, zero + 1.0, zero))

        def shift(v):
            if reverse:
                return jnp.where(row == SCAN_SEG - 1, 0.0, pltpu.roll(v, SCAN_SEG - 1, axis=0))
            return jnp.where(row == 0, 0.0, pltpu.roll(v, 1, axis=0))

        cr, ci = zero, zero
        for _ in range(SCAN_SEG - 1):
            cr, ci = shift(fr + apr * cr - api * ci), shift(fi + apr * ci + api * cr)

        def fix(t, c):
            pr, pi = c
            npr, npi = ar * pr - ai * pi, ar * pi + ai * pr
            rows = tile(t)
            sr_ref[rows, :] += npr * cr - npi * ci
            si_ref[rows, :] += npr * ci + npi * cr
            return npr, npi

        lax.fori_loop(0, nst, fix, (zero + 1.0, zero))

    vm = pl.BlockSpec(memory_space=pltpu.VMEM)
    return pl.pallas_call(
        body, name="ssm_scan_rev" if reverse else "ssm_scan",
        in_specs=[vm, vm, vm], out_specs=[vm, vm],
        out_shape=[jax.ShapeDtypeStruct((S, N), f32), jax.ShapeDtypeStruct((S, N), f32)],
        compiler_params=_cp(None, 48),
    )(xr, xi, a2)


def _ssm_in(u, bre, bim):
    S = u.shape[0]
    ts = TOK_TILE

    def body(u_ref, br_ref, bi_ref, or_ref, oi_ref):
        ub = u_ref[...].astype(bf16)
        or_ref[...] = _dot(ub, br_ref[...].astype(bf16))
        oi_ref[...] = _dot(ub, bi_ref[...].astype(bf16))

    return pl.pallas_call(
        body, name="ssm_in", grid=(S // ts,),
        in_specs=[pl.BlockSpec((ts, D_SSM), lambda i: (i, 0)), _full((D_SSM, D_STATE)), _full((D_SSM, D_STATE))],
        out_specs=[pl.BlockSpec((ts, D_STATE), lambda i: (i, 0))] * 2,
        out_shape=[jax.ShapeDtypeStruct((S, D_STATE), f32)] * 2,
        compiler_params=_cp(("parallel",)),
    )(u, bre, bim)


def _ssm_out(sr, si, u, cre, cim, dskip, glu_w, glu_b):
    S = u.shape[0]
    ts = TOK_TILE

    def body(sr_ref, si_ref, u_ref, cr_ref, ci_ref, d_ref, w_ref, b_ref, out_ref, y_ref):
        y = (_dot(sr_ref[...].astype(bf16), cr_ref[...].astype(bf16))
             - _dot(si_ref[...].astype(bf16), ci_ref[...].astype(bf16)) + d_ref[...] * u_ref[...])
        y_ref[...] = y
        z = _dot(_gelu(y).astype(bf16), w_ref[...].astype(bf16)) + b_ref[...]
        out_ref[...] = y * jax.nn.sigmoid(z)

    st = pl.BlockSpec((ts, D_STATE), lambda i: (i, 0))
    ch = pl.BlockSpec((ts, D_SSM), lambda i: (i, 0))
    return pl.pallas_call(
        body, name="ssm_out", grid=(S // ts,),
        in_specs=[st, st, ch, _full((D_STATE, D_SSM)), _full((D_STATE, D_SSM)), _full((1, D_SSM)),
                  _full((D_SSM, D_SSM)), _full((1, D_SSM))],
        out_specs=[ch, ch],
        out_shape=[jax.ShapeDtypeStruct((S, D_SSM), f32)] * 2,
        compiler_params=_cp(("parallel",)),
    )(sr, si, u, cre, cim, dskip, glu_w, glu_b)


def _ssm_out_bwd(dout, y, u, sr, si, cre, cim, dskip, glu_w, glu_b):
    S = u.shape[0]
    ts = TOK_TILE

    def body(do_ref, y_ref, u_ref, sr_ref, si_ref, cr_ref, ci_ref, d_ref, w_ref, b_ref,
             gr_ref, gi_ref, du_ref, dcr_ref, dci_ref, dd_ref, dgb_ref, dgw_ref):
        @pl.when(pl.program_id(0) == 0)
        def _():
            for r in (dcr_ref, dci_ref, dd_ref, dgb_ref, dgw_ref):
                r[...] = jnp.zeros_like(r)

        y = y_ref[...]
        dout = do_ref[...]
        wb = w_ref[...].astype(bf16)
        ge = _gelu(y).astype(bf16)
        sz = jax.nn.sigmoid(_dot(ge, wb) + b_ref[...])
        dz = dout * y * sz * (1.0 - sz)
        dzb = dz.astype(bf16)
        dgb_ref[...] += jnp.sum(dz, 0, keepdims=True)
        dgw_ref[...] += _dot_tn(ge, dzb)
        dy = dout * sz + _gelu_grad(y) * _dot_nt(dzb, wb)
        uv = u_ref[...]
        dd_ref[...] += jnp.sum(dy * uv, 0, keepdims=True)
        du_ref[...] = dy * d_ref[...]
        dyb = dy.astype(bf16)
        gr_ref[...] = _dot_nt(dyb, cr_ref[...].astype(bf16))
        gi_ref[...] = -_dot_nt(dyb, ci_ref[...].astype(bf16))
        dcr_ref[...] += _dot_tn(sr_ref[...].astype(bf16), dyb)
        dci_ref[...] -= _dot_tn(si_ref[...].astype(bf16), dyb)

    st = pl.BlockSpec((ts, D_STATE), lambda i: (i, 0))
    ch = pl.BlockSpec((ts, D_SSM), lambda i: (i, 0))
    c_full = _full((D_STATE, D_SSM))
    return pl.pallas_call(
        body, name="ssm_out_bwd", grid=(S // ts,),
        in_specs=[ch, ch, ch, st, st, c_full, c_full, _full((1, D_SSM)), _full((D_SSM, D_SSM)), _full((1, D_SSM))],
        out_specs=[st, st, ch, c_full, c_full, _full((1, D_SSM)), _full((1, D_SSM)), _full((D_SSM, D_SSM))],
        out_shape=[jax.ShapeDtypeStruct((S, D_STATE), f32), jax.ShapeDtypeStruct((S, D_STATE), f32),
                   jax.ShapeDtypeStruct((S, D_SSM), f32), jax.ShapeDtypeStruct((D_STATE, D_SSM), f32),
                   jax.ShapeDtypeStruct((D_STATE, D_SSM), f32), jax.ShapeDtypeStruct((1, D_SSM), f32),
                   jax.ShapeDtypeStruct((1, D_SSM), f32), jax.ShapeDtypeStruct((D_SSM, D_SSM), f32)],
        compiler_params=_cp(("arbitrary",), 40),
    )(dout, y, u, sr, si, cre, cim, dskip, glu_w, glu_b)


def _ssm_in_bwd(lr, li, u, du_skip, bre, bim):
    S = u.shape[0]
    ts = TOK_TILE

    def body(lr_ref, li_ref, u_ref, dus_ref, br_ref, bi_ref, du_ref, dbr_ref, dbi_ref):
        @pl.when(pl.program_id(0) == 0)
        def _():
            dbr_ref[...] = jnp.zeros_like(dbr_ref)
            dbi_ref[...] = jnp.zeros_like(dbi_ref)

        lrb = lr_ref[...].astype(bf16)
        lib = li_ref[...].astype(bf16)
        du_ref[...] = dus_ref[...] + _dot_nt(lrb, br_ref[...].astype(bf16)) + _dot_nt(lib, bi_ref[...].astype(bf16))
        ub = u_ref[...].astype(bf16)
        dbr_ref[...] += _dot_tn(ub, lrb)
        dbi_ref[...] += _dot_tn(ub, lib)

    st = pl.BlockSpec((ts, D_STATE), lambda i: (i, 0))
    ch = pl.BlockSpec((ts, D_SSM), lambda i: (i, 0))
    b_full = _full((D_SSM, D_STATE))
    return pl.pallas_call(
        body, name="ssm_in_bwd", grid=(S // ts,),
        in_specs=[st, st, ch, ch, b_full, b_full],
        out_specs=[ch, b_full, b_full],
        out_shape=[jax.ShapeDtypeStruct((S, D_SSM), f32), jax.ShapeDtypeStruct((D_SSM, D_STATE), f32),
                   jax.ShapeDtypeStruct((D_SSM, D_STATE), f32)],
        compiler_params=_cp(("arbitrary",), 40),
    )(lr, li, u, du_skip, bre, bim)


def _ssm_da(lr, li, sr, si):
    S, N = lr.shape
    nst = S // SCAN_SEG

    def body(lr_ref, li_ref, sr_ref, si_ref, out_ref):
        row = lax.broadcasted_iota(jnp.int32, (SCAN_SEG, N), 0)
        last = pl.ds((nst - 1) * SCAN_SEG, SCAN_SEG)
        pr = jnp.where(row == 0, 0.0, pltpu.roll(sr_ref[last, :], 1, axis=0))
        pi = jnp.where(row == 0, 0.0, pltpu.roll(si_ref[last, :], 1, axis=0))
        first = pl.ds(0, SCAN_SEG)
        acc_r = lr_ref[first, :] * pr + li_ref[first, :] * pi
        acc_i = li_ref[first, :] * pr - lr_ref[first, :] * pi

        def step(t, c):
            acc_r, acc_i = c
            rows = pl.ds(pl.multiple_of(t * SCAN_SEG, SCAN_SEG), SCAN_SEG)
            prev = pl.ds(pl.multiple_of((t - 1) * SCAN_SEG, SCAN_SEG), SCAN_SEG)
            lrv, liv, srv, siv = lr_ref[rows, :], li_ref[rows, :], sr_ref[prev, :], si_ref[prev, :]
            return acc_r + lrv * srv + liv * siv, acc_i + liv * srv - lrv * siv

        acc_r, acc_i = lax.fori_loop(1, nst, step, (acc_r, acc_i))
        out_ref[0:1, :] = jnp.sum(acc_r, 0, keepdims=True)
        out_ref[1:2, :] = jnp.sum(acc_i, 0, keepdims=True)

    vm = pl.BlockSpec(memory_space=pltpu.VMEM)
    return pl.pallas_call(
        body, name="ssm_da", in_specs=[vm] * 4, out_specs=vm,
        out_shape=jax.ShapeDtypeStruct((2, N), f32),
        compiler_params=_cp(None, 48),
    )(lr, li, sr, si)


_POOL_TILE = 256


def _window_sums(xt, back):
    n = xt.shape[0]
    out = []
    ws = xt
    for k in (1, 2, 4, 8):
        ws = ws + pltpu.roll(ws, k if back else n - k, axis=0)
        out.append(ws)
    return out


def _pool_count(r0, w):
    t = r0 + lax.broadcasted_iota(jnp.int32, (_POOL_TILE, POOL_GROUP), 0)
    return jnp.minimum(t + 1, w).astype(f32)


def _pool_fwd(u_pad, pool_w, pool_scale):
    S = u_pad.shape[0] - POOL_HALO
    nt = S // _POOL_TILE

    def body(u_ref, w_ref, sc_ref, y_ref):
        def tile(t, c):
            r0 = pl.multiple_of(t * _POOL_TILE, _POOL_TILE)
            for g, w in enumerate(POOL_WINDOWS):
                cs = pl.ds(POOL_GROUP * g, POOL_GROUP)
                xt = u_ref[pl.ds(r0, _POOL_TILE + POOL_HALO), cs]
                ws = _window_sums(xt, True)[g][POOL_HALO:, :]
                pooled = ws / _pool_count(r0, w) - xt[POOL_HALO:, :]
                y_ref[pl.ds(r0, _POOL_TILE), cs] = _dot(pooled.astype(bf16), w_ref[g].astype(bf16)) * sc_ref[:, cs]
            return c
        lax.fori_loop(0, nt, tile, 0)

    vm = pl.BlockSpec(memory_space=pltpu.VMEM)
    return pl.pallas_call(
        body, name="pool_fwd", in_specs=[vm, vm, vm], out_specs=vm,
        out_shape=jax.ShapeDtypeStruct((S, D_POOL), f32),
    )(u_pad, pool_w, pool_scale)


def _pool_bwd(dy_pad, u_pad, pool_w, pool_scale):
    S = u_pad.shape[0] - POOL_HALO
    nt = S // _POOL_TILE
    n = _POOL_TILE + POOL_HALO

    def body(dy_ref, u_ref, w_ref, sc_ref, du_ref, dw_ref, dsc_ref):
        dw_ref[...] = jnp.zeros_like(dw_ref)
        dsc_ref[...] = jnp.zeros_like(dsc_ref)

        def tile(t, c):
            r0 = pl.multiple_of(t * _POOL_TILE, _POOL_TILE)
            for g, w in enumerate(POOL_WINDOWS):
                cs = pl.ds(POOL_GROUP * g, POOL_GROUP)
                wb = w_ref[g].astype(bf16)
                xt = u_ref[pl.ds(r0, n), cs]
                pooled = (_window_sums(xt, True)[g][POOL_HALO:, :] / _pool_count(r0, w) - xt[POOL_HALO:, :]).astype(bf16)
                dy = dy_ref[pl.ds(r0, _POOL_TILE), cs]
                dsc_ref[:, cs] += jnp.sum(dy * _dot(pooled, wb), 0, keepdims=True)
                dw_ref[g] += _dot_tn(pooled, (dy * sc_ref[:, cs]).astype(bf16))
                dyh = (dy_ref[pl.ds(r0, n), cs] * sc_ref[:, cs]).astype(bf16)
                dpl = _dot_nt(dyh, wb)
                cnt = jnp.minimum(r0 + lax.broadcasted_iota(jnp.int32, (n, POOL_GROUP), 0) + 1, w).astype(f32)
                lead = _window_sums(dpl / cnt, False)[g]
                du_ref[pl.ds(r0, _POOL_TILE), cs] = lead[:_POOL_TILE, :] - dpl[:_POOL_TILE, :]
            return c
        lax.fori_loop(0, nt, tile, 0)

    vm = pl.BlockSpec(memory_space=pltpu.VMEM)
    return pl.pallas_call(
        body, name="pool_bwd", in_specs=[vm, vm, vm, vm], out_specs=[vm, vm, vm],
        out_shape=[jax.ShapeDtypeStruct((S, D_POOL), f32), jax.ShapeDtypeStruct((4, POOL_GROUP, POOL_GROUP), f32),
                   jax.ShapeDtypeStruct((1, D_POOL), f32)],
    )(dy_pad, u_pad, pool_w, pool_scale)


def _loss_head(y, target):
    S, D = y.shape
    ts = TOK_TILE

    def body(y_ref, t_ref, loss_ref, dy_ref):
        @pl.when(pl.program_id(0) == 0)
        def _():
            loss_ref[...] = jnp.zeros_like(loss_ref)

        d = y_ref[...] - t_ref[...]
        dy_ref[...] = d * (1.0 / D)
        loss_ref[...] += 0.5 * jnp.sum(jnp.sum(d * d, -1, keepdims=True) * (1.0 / D), 0, keepdims=True)

    tok = pl.BlockSpec((ts, D), lambda i: (i, 0))
    return pl.pallas_call(
        body, name="loss_head", grid=(S // ts,),
        in_specs=[tok, tok], out_specs=[_full((1, 1)), tok],
        out_shape=[jax.ShapeDtypeStruct((1, 1), f32), jax.ShapeDtypeStruct((S, D), f32)],
        compiler_params=_cp(("arbitrary",)),
    )(y, target)


_ADA_COLS = 768


def _ada_fwd(c_all, ada_w, ada_b_cols):
    L, D, N = ada_w.shape
    B = c_all.shape[0]

    def body(c_ref, w_ref, b_ref, out_ref):
        cv = c_ref[...]
        cond = (cv * jax.nn.sigmoid(cv)).astype(bf16)
        out_ref[0] = _dot(cond, w_ref[0].astype(bf16)) + b_ref[0]

    return pl.pallas_call(
        body, name="ada_fwd", grid=(L, N // _ADA_COLS),
        in_specs=[_full((B, D)), pl.BlockSpec((1, D, _ADA_COLS), lambda l, j: (l, 0, j)),
                  pl.BlockSpec((1, 1, _ADA_COLS), lambda l, j: (l, 0, j))],
        out_specs=pl.BlockSpec((1, B, _ADA_COLS), lambda l, j: (l, 0, j)),
        out_shape=jax.ShapeDtypeStruct((L, B, N), f32),
        compiler_params=_cp(("parallel", "parallel")),
    )(c_all, ada_w, ada_b_cols)


def _ada_wgrad(c_all_t, dmod_cols):
    D, B = c_all_t.shape
    L, _, N = dmod_cols.shape

    def body(ct_ref, dm_ref, out_ref):
        cv = ct_ref[...]
        cond = cv * jax.nn.sigmoid(cv)
        acc = cond[:, 0:1] * dm_ref[0, 0:1, :]
        for b in range(1, B):
            acc = acc + cond[:, b:b + 1] * dm_ref[0, b:b + 1, :]
        out_ref[0] = acc

    return pl.pallas_call(
        body, name="ada_wgrad", grid=(L, N // _ADA_COLS),
        in_specs=[_full((D, B)), pl.BlockSpec((1, B, _ADA_COLS), lambda l, j: (l, 0, j))],
        out_specs=pl.BlockSpec((1, D, _ADA_COLS), lambda l, j: (l, 0, j)),
        out_shape=jax.ShapeDtypeStruct((L, D, N), f32),
        compiler_params=_cp(("parallel", "parallel")),
    )(c_all_t, dmod_cols)


def _adam_math(w, g, m, v):
    m = ADAM_B1 * m + (1.0 - ADAM_B1) * g
    v = ADAM_B2 * v + (1.0 - ADAM_B2) * (g * g)
    m_hat = m / (1.0 - ADAM_B1 ** ADAM_STEP)
    v_hat = v / (1.0 - ADAM_B2 ** ADAM_STEP)
    delta = -ADAM_LR * (m_hat / (jnp.sqrt(v_hat) + ADAM_EPS) + ADAM_WD * w)
    return delta, m, v


def _adamw(w, m, v, g, row_tile, row0=0, outs=None):
    R, C = w.shape
    b0 = row0 // row_tile

    def body(w_ref, m_ref, v_ref, g_ref, _0, _1, _2, _3, g_out, d_out, m_out, v_out):
        gv = g_ref[...]
        delta, mn, vn = _adam_math(w_ref[...], gv, m_ref[...], v_ref[...])
        g_out[...] = gv
        d_out[...] = delta
        m_out[...] = mn
        v_out[...] = vn

    pspec = pl.BlockSpec((row_tile, C), lambda i: (b0 + i, 0))
    gspec = pl.BlockSpec((row_tile, C), lambda i: (i, 0))
    anyspec = pl.BlockSpec(memory_space=pl.ANY)
    shp = jax.ShapeDtypeStruct((R, C), f32)
    if outs is None:
        outs = [lax.empty((R, C), f32) for _ in range(4)]
    return pl.pallas_call(
        body, name="adamw", grid=(g.shape[0] // row_tile,),
        in_specs=[pspec] * 3 + [gspec] + [anyspec] * 4, out_specs=[pspec] * 4, out_shape=[shp] * 4,
        input_output_aliases={4: 0, 5: 1, 6: 2, 7: 3},
        compiler_params=_cp(("parallel",), 40),
    )(w, m, v, g, *outs)


def _pair_sum(g5, got, pc):
    _, LS, _, R2, C = g5.shape

    def body(pc_ref, own_ref, got_ref, out_ref):
        out_ref[0, 0] = (own_ref[0, 0, 0].astype(f32) + got_ref[0, 0].astype(f32)).astype(bf16)

    gs = pltpu.PrefetchScalarGridSpec(
        num_scalar_prefetch=1, grid=(N_CHIPS, LS),
        in_specs=[pl.BlockSpec((1, 1, 1, R2, C), lambda p, s, pc: (p, s, pc[1], 0, 0)),
                  pl.BlockSpec((1, 1, R2, C), lambda p, s, pc: (p, s, 0, 0))],
        out_specs=pl.BlockSpec((1, 1, R2, C), lambda p, s, pc: (p, s, 0, 0)),
    )
    return pl.pallas_call(
        body, name="pair_sum", grid_spec=gs, out_shape=jax.ShapeDtypeStruct((N_CHIPS, LS, R2, C), bf16),
        compiler_params=_cp(("parallel", "parallel")),
    )(pc, g5, got)


def _sum_shards(hsum, recv, pc):
    _, LS, R2, C = hsum.shape

    def body(pc_ref, own_ref, r_ref, out_ref):
        acc = own_ref[0, 0].astype(f32)
        for j in range(3):
            acc = acc + r_ref[j, 0].astype(f32)
        out_ref[0, 0] = acc

    gs = pltpu.PrefetchScalarGridSpec(
        num_scalar_prefetch=1, grid=(LS,),
        in_specs=[pl.BlockSpec((1, 1, R2, C), lambda s, pc: (pc[0], s, 0, 0)),
                  pl.BlockSpec((3, 1, R2, C), lambda s, pc: (0, s, 0, 0))],
        out_specs=pl.BlockSpec((1, 1, R2, C), lambda s, pc: (s, pc[1], 0, 0)),
    )
    return pl.pallas_call(
        body, name="sum_shards", grid_spec=gs, out_shape=jax.ShapeDtypeStruct((LS, 2, R2, C), f32),
        compiler_params=_cp(("parallel",)),
    )(pc, hsum, recv)


def _sum8(packs):
    _, R, C = packs.shape
    tr = R // 8 if R % 64 == 0 else R

    def body(p_ref, out_ref):
        acc = p_ref[0]
        for d in range(1, 8):
            acc = acc + p_ref[d]
        out_ref[...] = acc

    return pl.pallas_call(
        body, name="sum8", grid=(R // tr,),
        in_specs=[pl.BlockSpec((8, tr, C), lambda i: (0, i, 0))],
        out_specs=pl.BlockSpec((tr, C), lambda i: (i, 0)),
        out_shape=jax.ShapeDtypeStruct((R, C), f32),
        compiler_params=_cp(("parallel",)),
    )(packs)


def _allgather8(x_shard):
    m_per, n = x_shard.shape

    def body(x_ref, out_ref, send_sems, recv_sems, local_sem):
        x, y, c = lax.axis_index("x"), lax.axis_index("y"), lax.axis_index("c")
        me, sibling = (x, y, c), (x, y, 1 - c)
        chips = [(1 - x, y), (x, 1 - y), (1 - x, 1 - y)]

        def rows(px, py, pc):
            return out_ref.at[pl.ds((4 * px + 2 * py + pc) * m_per, m_per), :]

        def copy(k, block, to, src=None):
            return pltpu.make_async_remote_copy(
                src_ref=rows(*block) if src is None else src, dst_ref=rows(*block),
                send_sem=send_sems.at[k], recv_sem=recv_sems.at[k], device_id=to, device_id_type=MESH)

        mine = pltpu.make_async_copy(x_ref, rows(*me), local_sem)
        mine.start()
        first = [copy(0, me, sibling, src=x_ref)]
        first += [copy(1 + j, me, (*chip, c), src=x_ref) for j, chip in enumerate(chips)]
        for cp in first:
            cp.start()
        passed = [copy(4 + j, (*chip, c), sibling) for j, chip in enumerate(chips)]
        for j, chip in enumerate(chips):
            copy(1 + j, (*chip, c), me).wait_recv()
            passed[j].start()
        copy(0, sibling, me).wait_recv()
        for j, chip in enumerate(chips):
            copy(4 + j, (*chip, 1 - c), me).wait_recv()
        for cp in first + passed:
            cp.wait_send()
        mine.wait()

    return pl.pallas_call(
        body, name="allgather8",
        out_shape=jax.ShapeDtypeStruct((8 * m_per, n), x_shard.dtype),
        in_specs=[pl.BlockSpec(memory_space=pltpu.VMEM)],
        out_specs=pl.BlockSpec(memory_space=pltpu.VMEM),
        scratch_shapes=[pltpu.SemaphoreType.DMA((7,)), pltpu.SemaphoreType.DMA((7,)), pltpu.SemaphoreType.DMA],
        compiler_params=_cp(None, 48),
    )(x_shard)


def _other_chips():
    x, y = lax.axis_index("x"), lax.axis_index("y")
    return [(1 - x, y), (x, 1 - y), (1 - x, 1 - y)]


_HBM = pl.BlockSpec(memory_space=pltpu.HBM)
_SEM = pl.BlockSpec(memory_space=pltpu.SEMAPHORE)
_EFFECT = pltpu.SideEffectType.DATAFLOW_SIDE_EFFECTING


def _gather_copies(srcs, lands, send_sems, recv_sems):
    x, y, c = lax.axis_index("x"), lax.axis_index("y"), lax.axis_index("c")
    return [pltpu.make_async_remote_copy(
        src_ref=srcs[a].at[:, c], dst_ref=lands[a].at[2 * x + y, :, c], send_sem=send_sems.at[3 * a + j],
        recv_sem=recv_sems.at[3 * a + j], device_id=(cx, cy, c), device_id_type=MESH)
        for a in range(len(srcs)) for j, (cx, cy) in enumerate(_other_chips())]


def _gather_start(chunks):
    sizes = [len(srcs) for srcs, _ in chunks]
    flat = [t for srcs, lands in chunks for t in list(srcs) + list(lands)]
    nflat = len(flat)
    nsem = 2 * len(chunks)

    def body(*refs):
        ins, sems, token = refs[:nflat], refs[nflat:nflat + nsem], refs[-1]
        off = 0
        for k, n in enumerate(sizes):
            for cp in _gather_copies(ins[off:off + n], ins[off + n:off + 2 * n], sems[2 * k], sems[2 * k + 1]):
                cp.start()
            off += 2 * n
        token[...] = jnp.zeros_like(token)

    res = pl.pallas_call(
        body, name="gather_start",
        out_shape=[pltpu.SemaphoreType.DMA((3 * n,)) for n in sizes for _ in range(2)]
        + [pltpu.HBM(t.shape, t.dtype) for t in flat] + [jax.ShapeDtypeStruct((8, 128), f32)],
        in_specs=[_HBM] * nflat,
        out_specs=[_SEM] * nsem + [_HBM] * nflat + [pl.BlockSpec(memory_space=pltpu.VMEM)],
        input_output_aliases={i: nsem + i for i in range(nflat)},
        compiler_params=pltpu.CompilerParams(has_side_effects=_EFFECT),
    )(*[pltpu.with_memory_space_constraint(t, pltpu.HBM) for t in flat])
    out, off = [], nsem
    for k, n in enumerate(sizes):
        out.append((res[2 * k], res[2 * k + 1], res[off:off + n], res[off + n:off + 2 * n]))
        off += 2 * n
    return out, res[-1]


def _gather_wait(send_sems, recv_sems, srcs, lands, after, name):
    n = len(srcs)

    def body(*refs):
        for cp in _gather_copies(refs[:n], refs[n:2 * n], refs[2 * n], refs[2 * n + 1]):
            cp.wait_send()
            cp.wait_recv()

    res = pl.pallas_call(
        body, name=name,
        out_shape=[pltpu.HBM(t.shape, t.dtype) for t in list(srcs) + list(lands)],
        in_specs=[_HBM] * (2 * n) + [_SEM, _SEM, pl.BlockSpec(memory_space=pl.ANY)],
        out_specs=[_HBM] * (2 * n),
        input_output_aliases={i: i for i in range(2 * n)},
        compiler_params=pltpu.CompilerParams(has_side_effects=_EFFECT),
    )(*srcs, *lands, send_sems, recv_sems, after)
    return res[n:]


def _gather_forward(lands):
    n = len(lands)

    def body(*refs):
        outs = refs[n:2 * n]
        send_sems, recv_sems = refs[2 * n:]
        x, y, c = lax.axis_index("x"), lax.axis_index("y"), lax.axis_index("c")
        sibling = (x, y, 1 - c)
        copies = []
        for a in range(n):
            for j, (cx, cy) in enumerate(_other_chips()):
                mine = outs[a].at[2 * cx + cy, :, c]
                cp = pltpu.make_async_remote_copy(src_ref=mine, dst_ref=mine, send_sem=send_sems.at[3 * a + j],
                                                  recv_sem=recv_sems.at[3 * a + j], device_id=sibling, device_id_type=MESH)
                cp.start()
                copies.append((cp, a, j, cx, cy))
        for cp, a, j, cx, cy in copies:
            cp.wait_send()
            theirs = outs[a].at[2 * cx + cy, :, 1 - c]
            pltpu.make_async_remote_copy(src_ref=theirs, dst_ref=theirs, send_sem=send_sems.at[3 * a + j],
                                         recv_sem=recv_sems.at[3 * a + j], device_id=sibling, device_id_type=MESH).wait_recv()

    hbm = pl.BlockSpec(memory_space=pl.ANY)
    return pl.pallas_call(
        body, name="gather_forward",
        out_shape=[jax.ShapeDtypeStruct(t.shape, t.dtype) for t in lands],
        in_specs=[hbm] * n, out_specs=[hbm] * n,
        input_output_aliases={a: a for a in range(n)},
        scratch_shapes=[pltpu.SemaphoreType.DMA((3 * n,)), pltpu.SemaphoreType.DMA((3 * n,))],
    )(*lands)


def _pair_exchange(g5s):
    n = len(g5s)

    def body(*refs):
        ins, outs = refs[:n], refs[n:2 * n]
        send_sems, recv_sems = refs[2 * n:]
        c = lax.axis_index("c")
        sibling = (lax.axis_index("x"), lax.axis_index("y"), 1 - c)
        copies = []
        for a in range(n):
            cp = pltpu.make_async_remote_copy(src_ref=ins[a].at[:, :, 1 - c], dst_ref=outs[a], send_sem=send_sems.at[a],
                                              recv_sem=recv_sems.at[a], device_id=sibling, device_id_type=MESH)
            cp.start()
            copies.append(cp)
        for cp in copies:
            cp.wait()

    hbm = pl.BlockSpec(memory_space=pl.ANY)
    return pl.pallas_call(
        body, name="pair_exchange",
        out_shape=[jax.ShapeDtypeStruct(g.shape[:2] + g.shape[3:], g.dtype) for g in g5s],
        in_specs=[hbm] * n, out_specs=[hbm] * n,
        scratch_shapes=[pltpu.SemaphoreType.DMA((n,)), pltpu.SemaphoreType.DMA((n,))],
    )(*g5s)


def _scatter_copies(srcs, lands, send_sems, recv_sems):
    c = lax.axis_index("c")
    return [pltpu.make_async_remote_copy(
        src_ref=srcs[a].at[2 * cx + cy], dst_ref=lands[a].at[j], send_sem=send_sems.at[3 * a + j],
        recv_sem=recv_sems.at[3 * a + j], device_id=(cx, cy, c), device_id_type=MESH)
        for a in range(len(srcs)) for j, (cx, cy) in enumerate(_other_chips())]


def _scatter_start(hsums, name):
    n = len(hsums)

    def body(*refs):
        srcs, lands = refs[:n], refs[n:2 * n]
        send_sems, recv_sems = refs[2 * n], refs[2 * n + 1]
        for cp in _scatter_copies(srcs, lands, send_sems, recv_sems):
            cp.start()
        refs[-1][...] = jnp.zeros_like(refs[-1])

    lands = [lax.empty((3,) + g.shape[1:], g.dtype) for g in hsums]
    res = pl.pallas_call(
        body, name=name,
        out_shape=[pltpu.SemaphoreType.DMA((3 * n,)), pltpu.SemaphoreType.DMA((3 * n,))]
        + [pltpu.HBM(g.shape, g.dtype) for g in hsums] + [pltpu.HBM(g.shape, g.dtype) for g in lands]
        + [jax.ShapeDtypeStruct((8, 128), f32)],
        in_specs=[_HBM] * (2 * n), out_specs=[_SEM, _SEM] + [_HBM] * (2 * n) + [pl.BlockSpec(memory_space=pltpu.VMEM)],
        input_output_aliases={i: i + 2 for i in range(2 * n)},
        compiler_params=pltpu.CompilerParams(has_side_effects=_EFFECT),
    )(*[pltpu.with_memory_space_constraint(t, pltpu.HBM) for t in list(hsums) + lands])
    return (res[0], res[1], res[2:2 + n], res[2 + n:2 + 2 * n]), res[-1]


def _scatter_wait(send_sems, recv_sems, srcs, lands, after, name):
    n = len(srcs)
    extra = list(after)

    def body(*refs):
        s_refs, l_refs = refs[:n], refs[n:2 * n]
        ss, rs = refs[2 * n], refs[2 * n + 1]
        for cp in _scatter_copies(s_refs, l_refs, ss, rs):
            cp.wait_send()
            cp.wait_recv()

    res = pl.pallas_call(
        body, name=name,
        out_shape=[pltpu.HBM(g.shape, g.dtype) for g in srcs] + [pltpu.HBM(g.shape, g.dtype) for g in lands],
        in_specs=[_HBM] * (2 * n) + [_SEM, _SEM] + [pl.BlockSpec(memory_space=pl.ANY)] * len(extra),
        out_specs=[_HBM] * (2 * n),
        input_output_aliases={i: i for i in range(2 * n)},
        compiler_params=pltpu.CompilerParams(has_side_effects=_EFFECT),
    )(*srcs, *lands, send_sems, recv_sems, *extra)
    return res[:n], res[n:]


def _swap_halves(fulls):
    n = len(fulls)

    def body(*refs):
        ins, outs = refs[:n], refs[n:2 * n]
        send_sems, recv_sems = refs[2 * n:]
        c = lax.axis_index("c")
        sibling = (lax.axis_index("x"), lax.axis_index("y"), 1 - c)
        copies = []
        for a in range(n):
            cp = pltpu.make_async_remote_copy(src_ref=outs[a].at[:, c], dst_ref=outs[a].at[:, c], send_sem=send_sems.at[a],
                                              recv_sem=recv_sems.at[a], device_id=sibling, device_id_type=MESH)
            cp.start()
            copies.append(cp)
        for a, cp in enumerate(copies):
            cp.wait_send()
            theirs = outs[a].at[:, 1 - c]
            pltpu.make_async_remote_copy(src_ref=theirs, dst_ref=theirs, send_sem=send_sems.at[a], recv_sem=recv_sems.at[a],
                                         device_id=sibling, device_id_type=MESH).wait_recv()

    hbm = pl.BlockSpec(memory_space=pl.ANY)
    return pl.pallas_call(
        body, name="swap_halves",
        out_shape=[jax.ShapeDtypeStruct(p.shape, p.dtype) for p in fulls],
        in_specs=[hbm] * n, out_specs=[hbm] * n,
        input_output_aliases={a: a for a in range(n)},
        scratch_shapes=[pltpu.SemaphoreType.DMA((n,)), pltpu.SemaphoreType.DMA((n,))],
    )(*fulls)


def _to_residue(t, d):
    s, c = t.shape
    return t.reshape(s // d, d, c).transpose(1, 0, 2).reshape(s, c)


def _from_residue(t, d):
    s, c = t.shape
    return t.reshape(d, s // d, c).transpose(1, 0, 2).reshape(s, c)


def _to_segments(t):
    s, c = t.shape
    return t.reshape(SCAN_SEG, s // SCAN_SEG, c).transpose(1, 0, 2).reshape(s, c)


def _from_segments(t):
    s, c = t.shape
    return t.reshape(s // SCAN_SEG, SCAN_SEG, c).transpose(1, 0, 2).reshape(s, c)


def _ssm_operators(a_re, a_im, log_dt, b_re, b_im, c_re, c_im):
    lam = lax.complex(a_re, a_im)
    dt = jnp.exp(log_dt)[:, None]
    a_bar = jnp.exp(lam * dt)
    b_bar = ((a_bar - 1.0) / lam)[:, :, None] * lax.complex(b_re, b_im)
    eye = jnp.eye(N_GROUPS, dtype=f32)

    def embed_b(t):
        return (jnp.transpose(t, (0, 2, 1))[:, :, None, :] * eye[:, None, :, None]).reshape(D_SSM, D_STATE)

    def embed_c(t):
        return (jnp.transpose(t, (0, 2, 1))[:, :, None, :] * eye[:, None, :, None]).reshape(D_STATE, D_SSM)

    a2 = jnp.stack([a_bar.real.reshape(D_STATE), a_bar.imag.reshape(D_STATE)])
    return a2, embed_b(b_bar.real), embed_b(b_bar.imag), embed_c(c_re), embed_c(c_im)


def _local_step(x, target, mod, small, ffn_weights, mix_weights, layer_done):
    table = jnp.asarray(_bucket_table())
    bias = _bias_fwd(small["rel_bias"], table)
    L = DEPTH
    saved = []
    ssm_ops = []
    for l in range(L):
        sv = {}
        m9 = mod[l]
        sv["x0"] = x
        sv["w0"] = ffn_weights(l, 0, x)
        x, sv["f0"], sv["g0"], sv["u0"] = _ffn_fwd(x, m9[0:3], *sv["w0"], 0, small["ln_g"][l, 0:1], small["ln_b"][l, 0:1])
        sv["x1"] = x
        sv["w1"] = mix_weights(l, x)
        z, sv["h1"] = _mix_in_fwd(x, m9[3:6], sv["w1"][0], 0)
        q16 = (z[0] * HEAD_DIM ** -0.5).astype(bf16)
        k16 = z[1].astype(bf16)
        v16 = z[2].astype(bf16)
        pad = jnp.zeros((ATT_BLOCK, D_ATT), bf16)
        qp = jnp.stack([_to_residue(q16, d) for d in DILATIONS])
        kp = jnp.stack([jnp.concatenate([pad, _to_residue(k16, d)]) for d in DILATIONS])
        vp = jnp.stack([jnp.concatenate([pad, _to_residue(v16, d)]) for d in DILATIONS])
        op, lsep = _att_fwd(qp, kp, vp, bias)
        o3 = jnp.stack([_from_residue(op[b], d) for b, d in enumerate(DILATIONS)])
        lse3 = jnp.stack([_from_residue(lsep[b], d) for b, d in enumerate(DILATIONS)])
        y_att = _att_merge(o3, lse3)
        sv.update(qp=qp, kp=kp, vp=vp, lsep=lsep, lse3=lse3, y_att=y_att)

        prm = tuple(small[k][l] for k in ("ssm_a_re", "ssm_a_im", "ssm_log_dt", "ssm_b_re", "ssm_b_im", "ssm_c_re", "ssm_c_im"))
        (a2, bre, bim, cre, cim), ops_vjp = jax.vjp(_ssm_operators, *prm)
        ssm_ops.append(ops_vjp)
        u_ssm = _to_segments(z[3][:, :D_SSM])
        bur, bui = _ssm_in(u_ssm, bre, bim)
        sr, si = _ssm_scan(bur, bui, a2, False)
        dskip = small["ssm_d"][l][None, :]
        glu_b = small["glu_b"][l][None, :]
        out_seg, y_seg = _ssm_out(sr, si, u_ssm, cre, cim, dskip, small["glu_w"][l], glu_b)
        y_ssm = _from_segments(out_seg)
        sv.update(a2=a2, bre=bre, bim=bim, cre=cre, cim=cim, u_ssm=u_ssm, sr=sr, si=si, y_seg=y_seg, y_ssm=y_ssm)

        u_pool = jnp.concatenate([jnp.zeros((POOL_HALO, D_POOL), f32), z[3][:, D_SSM:]])
        y_pool = _pool_fwd(u_pool, small["pool_w"][l], small["pool_scale"][l][None, :])
        sv.update(u_pool=u_pool, y_pool=y_pool)

        x, sv["ymix"] = _mix_out_fwd(x, y_att, y_ssm, y_pool, m9[3:6], sv["w1"][1], 0, small["ln_g"][l, 1:2], small["ln_b"][l, 1:2])
        sv["x2"] = x
        sv["w2"] = ffn_weights(l, 1, x)
        x, sv["f2"], sv["g2"], sv["u2"] = _ffn_fwd(x, m9[6:9], *sv["w2"], 0, small["ln_g"][l, 2:3], small["ln_b"][l, 2:3])
        saved.append(sv)

    loss, dx = _loss_head(x, target)

    dmod = [None] * L
    dln_g = [None] * L
    dln_b = [None] * L
    sg = {k: [None] * L for k in ("ssm_a_re", "ssm_a_im", "ssm_log_dt", "ssm_b_re", "ssm_b_im", "ssm_c_re", "ssm_c_im",
                                  "ssm_d", "glu_w", "glu_b", "pool_w", "pool_scale")}
    dbias_tot = None
    order_after = jnp.zeros((), f32)
    for l in reversed(range(L)):
        sv = saved[l]
        m9 = mod[l] + order_after
        G = {k: lax.empty((N_CHIPS, 2) + sv["w0"][i].shape[2:], bf16) for i, k in enumerate(("gate", "up", "down"))}
        G["w_in"] = lax.empty(sv["w1"][0].shape, bf16)
        G["w_out"] = lax.empty(sv["w1"][1].shape, bf16)
        dx, dg, du, a, h, df, dm2, dlg2, dlb2 = _ffn_bwd(dx, sv["x2"], sv["f2"], sv["g2"], sv["u2"], m9[6:9], *sv["w2"], 0,
                                                        small["ln_g"][l, 2:3])
        G["gate"], G["up"], G["down"] = _ffn_wgrad(h, dg, du, a, df, G["gate"], G["up"], G["down"], 1)
        dxr, d_att, d_ssm, d_pool, dgate1, dlg1, dlb1, G["w_out"] = _mix_out_bwd(
            dx, sv["x1"], sv["ymix"], sv["y_att"], sv["y_ssm"], sv["y_pool"], m9[3:6], sv["w1"][1], 0, small["ln_g"][l, 1:2], G["w_out"])
        do3, c3 = _att_merge_bwd(d_att, sv["y_att"], sv["lse3"])
        dop = jnp.stack([_to_residue(do3[b], d) for b, d in enumerate(DILATIONS)])
        cp = jnp.stack([_to_residue(c3[b], d) for b, d in enumerate(DILATIONS)])
        dqp, dkp, dvp, dbias = _att_bwd(sv["qp"], sv["kp"], sv["vp"], dop, sv["lsep"], cp, bias)
        dbias_tot = dbias if dbias_tot is None else dbias_tot + dbias

        def back(t3, padded):
            parts = [_from_residue(t3[b][ATT_BLOCK:] if padded else t3[b], d).astype(f32) for b, d in enumerate(DILATIONS)]
            return (parts[0] + parts[1] + parts[2]).astype(bf16)

        dq, dk, dv = back(dqp, False), back(dkp, True), back(dvp, True)
        d_seg = _to_segments(d_ssm)
        dskip = small["ssm_d"][l][None, :]
        glu_b = small["glu_b"][l][None, :]
        gsr, gsi, du_skip, dcre, dcim, dd, dglu_b, dglu_w = _ssm_out_bwd(
            d_seg, sv["y_seg"], sv["u_ssm"], sv["sr"], sv["si"], sv["cre"], sv["cim"], dskip, small["glu_w"][l], glu_b)
        lr, li = _ssm_scan(gsr, gsi, sv["a2"], True)
        du_seg, dbre, dbim = _ssm_in_bwd(lr, li, sv["u_ssm"], du_skip, sv["bre"], sv["bim"])
        da2 = _ssm_da(lr, li, sv["sr"], sv["si"])
        d_prm = ssm_ops[l]((da2, dbre, dbim, dcre, dcim))
        for k, v in zip(("ssm_a_re", "ssm_a_im", "ssm_log_dt", "ssm_b_re", "ssm_b_im", "ssm_c_re", "ssm_c_im"), d_prm):
            sg[k][l] = v
        sg["ssm_d"][l] = dd[0]
        sg["glu_b"][l] = dglu_b[0]
        sg["glu_w"][l] = dglu_w
        du_ssm = _from_segments(du_seg)
        dyp = jnp.concatenate([d_pool, jnp.zeros((POOL_HALO, D_POOL), f32)])
        du_pool, dpw, dps = _pool_bwd(dyp, sv["u_pool"], small["pool_w"][l], small["pool_scale"][l][None, :])
        sg["pool_w"][l] = dpw
        sg["pool_scale"][l] = dps[0]
        dz = jnp.stack([dq, dk, dv, jnp.concatenate([du_ssm, du_pool], axis=1).astype(bf16)])
        dx, dm1 = _mix_in_bwd(dz, dxr, sv["x1"], m9[3:6], sv["w1"][0], 0)
        G["w_in"] = _mix_in_wgrad(sv["h1"], dz, G["w_in"], 0)
        dm1 = jnp.concatenate([dm1[0:2], dgate1])
        dx, dg, du, a, h, df, dm0, dlg0, dlb0 = _ffn_bwd(dx, sv["x0"], sv["f0"], sv["g0"], sv["u0"], m9[0:3], *sv["w0"], 0,
                                                        small["ln_g"][l, 0:1])
        G["gate"], G["up"], G["down"] = _ffn_wgrad(h, dg, du, a, df, G["gate"], G["up"], G["down"], 0)
        order_after = layer_done(l, G)
        dmod[l] = jnp.concatenate([dm0, dm1, dm2])
        dln_g[l] = jnp.concatenate([dlg0, dlg1, dlg2])
        dln_b[l] = jnp.concatenate([dlb0, dlb1, dlb2])

    small_grads = {k: jnp.stack(v) for k, v in sg.items()}
    small_grads["rel_bias"] = _bias_bwd(dbias_tot, table)
    small_grads["ln_g"] = jnp.stack(dln_g)
    small_grads["ln_b"] = jnp.stack(dln_b)
    return loss, dx, jnp.stack(dmod), small_grads


def _pack(arrs):
    flat = jnp.concatenate([a.reshape(-1).astype(f32) for a in arrs])
    n = flat.shape[0]
    npad = -(-n // 1024) * 1024
    return jnp.pad(flat, (0, npad - n)).reshape(npad // 128, 128)


def _unpack(buf, shapes):
    flat = buf.reshape(-1)
    out, off = [], 0
    for s in shapes:
        n = int(np.prod(s))
        out.append(flat[off:off + n].reshape(s))
        off += n
    return out


_REPL = ("rel_bias", "ada_b", "ssm_a_re", "ssm_a_im", "ssm_log_dt", "ssm_b_re", "ssm_b_im", "ssm_c_re", "ssm_c_im",
         "ssm_d", "glu_b", "pool_w", "pool_scale")
_SMALL_SHARDED = ("ln_g", "ln_b", "glu_w")
_BIG = ("ffn_w_gate", "ffn_w_up", "ffn_w_down", "w_in", "w_out")
_ORDER = ("rel_bias", "ada_w", "ada_b", "ln_g", "ln_b", "ffn_w_gate", "ffn_w_up", "ffn_w_down", "w_in", "w_out",
          "ssm_a_re", "ssm_a_im", "ssm_log_dt", "ssm_b_re", "ssm_b_im", "ssm_c_re", "ssm_c_im", "ssm_d", "glu_w",
          "glu_b", "pool_w", "pool_scale")


def kernel(x, c, rel_bias, ada_w, ada_b, ln_g, ln_b, ffn_w_gate, ffn_w_up, ffn_w_down, w_in, w_out, ssm_a_re, ssm_a_im, ssm_log_dt, ssm_b_re, ssm_b_im, ssm_c_re, ssm_c_im, ssm_d, glu_w, glu_b, pool_w, pool_scale, loss_target, m_rel_bias, m_ada_w, m_ada_b, m_ln_g, m_ln_b, m_ffn_w_gate, m_ffn_w_up, m_ffn_w_down, m_w_in, m_w_out, m_ssm_a_re, m_ssm_a_im, m_ssm_log_dt, m_ssm_b_re, m_ssm_b_im, m_ssm_c_re, m_ssm_c_im, m_ssm_d, m_glu_w, m_glu_b, m_pool_w, m_pool_scale, v_rel_bias, v_ada_w, v_ada_b, v_ln_g, v_ln_b, v_ffn_w_gate, v_ffn_w_up, v_ffn_w_down, v_w_in, v_w_out, v_ssm_a_re, v_ssm_a_im, v_ssm_log_dt, v_ssm_b_re, v_ssm_b_im, v_ssm_c_re, v_ssm_c_im, v_ssm_d, v_glu_w, v_glu_b, v_pool_w, v_pool_scale):
    args = dict(locals())
    w = {k: args[k] for k in _ORDER}
    m = {k: args["m_" + k] for k in _ORDER}
    v = {k: args["v_" + k] for k in _ORDER}
    L, D = DEPTH, D_MODEL
    ax, ay, ac = lax.axis_index("x"), lax.axis_index("y"), lax.axis_index("c")
    p_me = 2 * ax + ay
    dev = 4 * ax + 2 * ay + ac

    def halves(t):
        return t.astype(bf16).reshape(1, 2, t.shape[0] // 2, t.shape[1])

    def landing(src):
        return lax.dynamic_update_slice(lax.empty((N_CHIPS,) + src.shape, bf16), src[None], (p_me, 0, 0, 0, 0))

    chunk_keys = [("ffn", 0, 0), ("mix", 0), ("ffn", 0, 1), ("ffn", 1, 0), ("mix", 1), ("ffn", 1, 1)]
    chunk_srcs = []
    for key in chunk_keys:
        if key[0] == "ffn":
            chunk_srcs.append([halves(t[key[1], key[2]]) for t in (ffn_w_gate, ffn_w_up, ffn_w_down)])
        else:
            chunk_srcs.append([halves(w_in[key[1]]), halves(w_out[key[1]])])
    in_flight, started = _gather_start([(srcs, [landing(t) for t in srcs]) for srcs in chunk_srcs])

    def gathered(key, after):
        k = chunk_keys.index(key)
        lands = _gather_forward(_gather_wait(*in_flight[k], after, "gather_wait_%d" % k))
        return [t.reshape(N_CHIPS, 1, 2 * t.shape[3], t.shape[4]) for t in lands]

    pack = _pack([c + started[0, 0], ln_g, ln_b, glu_w])
    rows = pack.shape[0]
    allp = _allgather8(pack).reshape(8, rows, 128)
    c_all = allp[:, :8].reshape(8, D)
    by_chip = allp[0::2]

    def sharded(row0, nrows, shape, axis):
        t = by_chip[:, row0:row0 + nrows].reshape((N_CHIPS,) + shape)
        return jnp.concatenate([t[p] for p in range(N_CHIPS)], axis=axis)

    ln_g_full = sharded(8, 12, (L, 3, 256), 2)
    ln_b_full = sharded(20, 12, (L, 3, 256), 2)
    glu_w_full = sharded(32, 256, (L, 64, 256), 1)

    ncol = ada_w.shape[-1]
    ada_b_cols = lax.dynamic_slice_in_dim(ada_b, p_me * ncol, ncol, axis=1)[:, None, :]
    mod_part = _ada_fwd(c_all, ada_w, ada_b_cols)
    mrows = L * 8 * ncol // 128
    mod_all = _allgather8(mod_part.reshape(mrows, 128)).reshape(8, L, 8, ncol)
    mod_mine = lax.dynamic_index_in_dim(mod_all, dev, axis=2, keepdims=False)
    mod = jnp.concatenate([mod_mine[2 * p] for p in range(N_CHIPS)], axis=-1).reshape(L, 9, D)

    keys = ("gate", "up", "down", "w_in", "w_out")
    pc = jnp.stack([p_me, ac]).astype(jnp.int32)
    scattering = {}

    def layer_done(l, G):
        g5 = [G[k].reshape(G[k].shape[:2] + (2, G[k].shape[2] // 2, G[k].shape[3])) for k in keys]
        hsum = [_pair_sum(g, r, pc) for g, r in zip(g5, _pair_exchange(g5))]
        scattering[l], begun = _scatter_start(hsum, "scatter_start_%d" % l)
        return begun[0, 0]

    small = {k: w[k] for k in _REPL if k != "ada_b"}
    small.update(ln_g=ln_g_full, ln_b=ln_b_full, glu_w=glu_w_full)
    loss_dev, grad_x, dmod, sgrads = _local_step(
        x[0], loss_target[0], mod, small, lambda l, s, after: gathered(("ffn", l, s), after),
        lambda l, after: gathered(("mix", l), after), layer_done)
    loss = lax.psum(loss_dev[0, 0], ("x", "y", "c"))

    names = ("rel_bias", "ln_g", "ln_b", "ssm_a_re", "ssm_a_im", "ssm_log_dt", "ssm_b_re", "ssm_b_im", "ssm_c_re",
             "ssm_c_im", "ssm_d", "glu_w", "glu_b", "pool_w", "pool_scale")
    gpack = _pack([dmod] + [sgrads[k] for k in names])
    grows = gpack.shape[0]
    gall = _allgather8(gpack).reshape(8, grows, 128)
    gsum = _unpack(_sum8(gall), [(L, 9 * D)] + [sgrads[k].shape for k in names])
    red = dict(zip(("ada_b",) + names, gsum))
    red["ln_g"] = lax.dynamic_slice_in_dim(red["ln_g"], p_me * 256, 256, axis=2)
    red["ln_b"] = lax.dynamic_slice_in_dim(red["ln_b"], p_me * 256, 256, axis=2)
    red["glu_w"] = lax.dynamic_slice_in_dim(red["glu_w"], p_me * 64, 64, axis=1)

    dmod_all = gall[:, :L * 9 * D // 128].reshape(8, L, 9 * D)
    dmod_cols = jnp.transpose(lax.dynamic_slice_in_dim(dmod_all, p_me * ncol, ncol, axis=2), (1, 0, 2))
    g_ada_w = _ada_wgrad(jnp.transpose(c_all), dmod_cols)

    out_g, out_d, out_m, out_v = {}, {}, {}, {}
    r2 = (L * D, ncol)
    res = _adamw(ada_w.reshape(r2), m["ada_w"].reshape(r2), v["ada_w"].reshape(r2), g_ada_w.reshape(r2), 128)
    out_g["ada_w"], out_d["ada_w"], out_m["ada_w"], out_v["ada_w"] = [t.reshape(ada_w.shape) for t in res]

    small_names = _REPL + _SMALL_SHARDED
    wp = _pack([w[k] for k in small_names])
    res_small = _adamw(wp, _pack([m[k] for k in small_names]), _pack([v[k] for k in small_names]),
                       _pack([red[k] for k in small_names]), wp.shape[0])
    for t, dst in zip(res_small, (out_g, out_d, out_m, out_v)):
        for k, a in zip(small_names, _unpack(t, [w[k].shape for k in small_names])):
            dst[k] = a

    row_tile = dict(zip(_BIG, (256, 256, 128, 256, 256)))
    big = {name: None for name in _BIG}
    after = [grad_x]
    for l in reversed(range(L)):
        hsum, recv = _scatter_wait(*scattering[l], after, "scatter_wait_%d" % l)
        full = _swap_halves([_sum_shards(h, r, pc) for h, r in zip(hsum, recv)])
        for i, name in enumerate(_BIG):
            shp = w[name].shape
            r2 = (int(np.prod(shp[:-1])), shp[-1])
            big[name] = _adamw(w[name].reshape(r2), m[name].reshape(r2), v[name].reshape(r2), full[i].reshape(-1, shp[-1]),
                               row_tile[name], l * (r2[0] // L), big[name])
        after = [big[_BIG[0]][1], res_small[1], res[1]]
    for name in _BIG:
        out_g[name], out_d[name], out_m[name], out_v[name] = [t.reshape(w[name].shape) for t in big[name]]

    return (loss, grad_x[None], *[out_g[k] for k in _ORDER], *[out_d[k] for k in _ORDER],
            *[out_m[k] for k in _ORDER], *[out_v[k] for k in _ORDER])
```

```python
import functools
import math

import numpy as np
import jax
import jax.numpy as jnp
from jax import lax
from jax.experimental import pallas as pl
from jax.experimental.pallas import tpu as pltpu

f32 = jnp.float32
bf16 = jnp.bfloat16
MESH = pl.DeviceIdType.MESH

D_MODEL = 1024
SEQ = 2048
DEPTH = 2
HEAD_DIM = 64
N_HEADS = 8
D_ATT = 512
DILATIONS = (1, 4, 16)
BLOCKS_PER_RESIDUE = (16, 4, 1)
ATT_BLOCK = 128
N_UNITS = SEQ // ATT_BLOCK
N_GROUPS = 16
SSM_GROUP = 16
SSM_STATE = 64
D_SSM = 256
D_STATE = N_GROUPS * SSM_STATE
POOL_WINDOWS = (2, 4, 8, 16)
POOL_GROUP = 64
D_POOL = 256
POOL_HALO = 16
D_FF = 2816
N_BUCKETS = 32
MAX_DISTANCE = 2048
ALPHA = (2 * DEPTH) ** 0.25
FFN_RES = 0.5
LN_EPS = 1e-5
NEG = -1e30
N_CHIPS = 4
FF_SHARD = D_FF // N_CHIPS
SCAN_SEG = 8
SCAN_STEPS = SEQ // SCAN_SEG

ADAM_LR, ADAM_B1, ADAM_B2, ADAM_EPS, ADAM_WD, ADAM_STEP = 0.001, 0.9, 0.999, 1e-08, 0.01, 10

TOK_TILE = 512


def _cp(dims=None, vmem_mb=None):
    kw = {}
    if dims is not None:
        kw["dimension_semantics"] = dims
    if vmem_mb is not None:
        kw["vmem_limit_bytes"] = vmem_mb << 20
    return pltpu.CompilerParams(**kw)


def _dot(a, b):
    return jnp.dot(a, b, preferred_element_type=f32)


def _dot_nt(a, b):
    return lax.dot_general(a, b, (((1,), (1,)), ((), ())), preferred_element_type=f32)


def _dot_tn(a, b):
    return lax.dot_general(a, b, (((0,), (0,)), ((), ())), preferred_element_type=f32)


def _ln_stats(v):
    mu = jnp.mean(v, -1, keepdims=True)
    d = v - mu
    var = jnp.mean(d * d, -1, keepdims=True)
    rstd = lax.rsqrt(var + LN_EPS)
    return d * rstd, rstd


def _ln_bwd(dxh, xh, rstd):
    return rstd * (dxh - jnp.mean(dxh, -1, keepdims=True) - xh * jnp.mean(dxh * xh, -1, keepdims=True))


_GELU_C = math.sqrt(2.0 / math.pi)


def _gelu(y):
    return 0.5 * y * (1.0 + jnp.tanh(_GELU_C * (y + 0.044715 * y * y * y)))


def _gelu_grad(y):
    t = jnp.tanh(_GELU_C * (y + 0.044715 * y * y * y))
    return 0.5 * (1.0 + t) + 0.5 * y * (1.0 - t * t) * (_GELU_C * (1.0 + 3 * 0.044715 * y * y))


def _full(shape):
    return pl.BlockSpec(shape, lambda *_: (0,) * len(shape))


def _ffn_fwd(x, mod3, wg, wu, wd, ls, lng, lnb):
    S, D = x.shape
    Fs = wg.shape[-1]
    ts = TOK_TILE

    def body(x_ref, mod_ref, wg_ref, wu_ref, wd_ref, lng_ref, lnb_ref, xo_ref, f_ref, g_ref, u_ref, h_sc, acc_sc):
        j = pl.program_id(1)

        @pl.when(j == 0)
        def _():
            xh, _ = _ln_stats(x_ref[...])
            h_sc[...] = (xh * (1.0 + mod_ref[1:2, :]) + mod_ref[0:1, :]).astype(bf16)
            acc_sc[...] = jnp.zeros_like(acc_sc)

        h = h_sc[...]
        g = _dot(h, wg_ref[0, 0])
        u = _dot(h, wu_ref[0, 0])
        g_ref[0] = g.astype(bf16)
        u_ref[0] = u.astype(bf16)
        a = (g * jax.nn.sigmoid(g) * u).astype(bf16)
        acc_sc[...] += _dot(a, wd_ref[0, 0])

        @pl.when(j == N_CHIPS - 1)
        def _():
            f = acc_sc[...]
            f_ref[...] = f
            r = ALPHA * x_ref[...] + (FFN_RES * mod_ref[2:3, :]) * f
            rh, _ = _ln_stats(r)
            xo_ref[...] = rh * lng_ref[...] + lnb_ref[...]

    tok = pl.BlockSpec((ts, D), lambda i, j: (i, 0))
    wcol = pl.BlockSpec((1, 1, D, Fs), lambda i, j: (j, ls, 0, 0))
    wrow = pl.BlockSpec((1, 1, Fs, D), lambda i, j: (j, ls, 0, 0))
    hid = pl.BlockSpec((1, ts, Fs), lambda i, j: (j, i, 0))
    return pl.pallas_call(
        body, name="ffn_fwd", grid=(S // ts, N_CHIPS),
        in_specs=[tok, _full((3, D)), wcol, wcol, wrow, _full((1, D)), _full((1, D))],
        out_specs=[tok, tok, hid, hid],
        out_shape=[jax.ShapeDtypeStruct((S, D), f32), jax.ShapeDtypeStruct((S, D), f32),
                   jax.ShapeDtypeStruct((N_CHIPS, S, Fs), bf16), jax.ShapeDtypeStruct((N_CHIPS, S, Fs), bf16)],
        scratch_shapes=[pltpu.VMEM((ts, D), bf16), pltpu.VMEM((ts, D), f32)],
        compiler_params=_cp(("parallel", "arbitrary"), 56),
    )(x, mod3, wg, wu, wd, lng, lnb)


def _ffn_bwd(dxo, x, f, g, u, mod3, wg, wu, wd, ls, lng):
    S, D = x.shape
    Fs = wg.shape[-1]
    ts = TOK_TILE

    def body(dxo_ref, x_ref, f_ref, g_ref, u_ref, mod_ref, wg_ref, wu_ref, wd_ref, lng_ref,
             dx_ref, dg_ref, du_ref, a_ref, h_ref, df_ref, dmod_ref, dlng_ref, dlnb_ref,
             dr_sc, df_sc, acc_sc):
        i = pl.program_id(0)
        j = pl.program_id(1)

        @pl.when((i == 0) & (j == 0))
        def _():
            dmod_ref[...] = jnp.zeros_like(dmod_ref)
            dlng_ref[...] = jnp.zeros_like(dlng_ref)
            dlnb_ref[...] = jnp.zeros_like(dlnb_ref)

        @pl.when(j == 0)
        def _():
            xv = x_ref[...]
            fv = f_ref[...]
            gate = mod_ref[2:3, :]
            rh, rstd = _ln_stats(ALPHA * xv + (FFN_RES * gate) * fv)
            dy = dxo_ref[...]
            dlng_ref[...] += jnp.sum(dy * rh, 0, keepdims=True)
            dlnb_ref[...] += jnp.sum(dy, 0, keepdims=True)
            dr = _ln_bwd(dy * lng_ref[...], rh, rstd)
            dr_sc[...] = dr
            dmod_ref[2:3, :] += jnp.sum(FFN_RES * dr * fv, 0, keepdims=True)
            df = ((FFN_RES * gate) * dr).astype(bf16)
            df_sc[...] = df
            df_ref[...] = df
            xh, _ = _ln_stats(xv)
            h_ref[...] = (xh * (1.0 + mod_ref[1:2, :]) + mod_ref[0:1, :]).astype(bf16)
            acc_sc[...] = jnp.zeros_like(acc_sc)

        da = _dot_nt(df_sc[...], wd_ref[0, 0])
        gv = g_ref[0].astype(f32)
        uv = u_ref[0].astype(f32)
        sg = jax.nn.sigmoid(gv)
        si = gv * sg
        a_ref[0] = (si * uv).astype(bf16)
        dgv = (da * uv * (sg * (1.0 + gv * (1.0 - sg)))).astype(bf16)
        duv = (da * si).astype(bf16)
        dg_ref[0] = dgv
        du_ref[0] = duv
        acc_sc[...] += _dot_nt(dgv, wg_ref[0, 0]) + _dot_nt(duv, wu_ref[0, 0])

        @pl.when(j == N_CHIPS - 1)
        def _():
            dh = acc_sc[...]
            xh, rstd0 = _ln_stats(x_ref[...])
            dmod_ref[0:1, :] += jnp.sum(dh, 0, keepdims=True)
            dmod_ref[1:2, :] += jnp.sum(dh * xh, 0, keepdims=True)
            dx_ref[...] = _ln_bwd(dh * (1.0 + mod_ref[1:2, :]), xh, rstd0) + ALPHA * dr_sc[...]

    tok = pl.BlockSpec((ts, D), lambda i, j: (i, 0))
    wcol = pl.BlockSpec((1, 1, D, Fs), lambda i, j: (j, ls, 0, 0))
    wrow = pl.BlockSpec((1, 1, Fs, D), lambda i, j: (j, ls, 0, 0))
    hid = pl.BlockSpec((1, ts, Fs), lambda i, j: (j, i, 0))
    hid_shape = jax.ShapeDtypeStruct((N_CHIPS, S, Fs), bf16)
    return pl.pallas_call(
        body, name="ffn_bwd", grid=(S // ts, N_CHIPS),
        in_specs=[tok, tok, tok, hid, hid, _full((3, D)), wcol, wcol, wrow, _full((1, D))],
        out_specs=[tok, hid, hid, hid, tok, tok, _full((3, D)), _full((1, D)), _full((1, D))],
        out_shape=[jax.ShapeDtypeStruct((S, D), f32), hid_shape, hid_shape, hid_shape,
                   jax.ShapeDtypeStruct((S, D), bf16), jax.ShapeDtypeStruct((S, D), bf16),
                   jax.ShapeDtypeStruct((3, D), f32), jax.ShapeDtypeStruct((1, D), f32), jax.ShapeDtypeStruct((1, D), f32)],
        scratch_shapes=[pltpu.VMEM((ts, D), f32), pltpu.VMEM((ts, D), bf16), pltpu.VMEM((ts, D), f32)],
        compiler_params=_cp(("arbitrary", "arbitrary"), 56),
    )(dxo, x, f, g, u, mod3, wg, wu, wd, lng)


def _ffn_wgrad(h, dg, du, a, df, gwg, gwu, gwd, ls):
    S, D = h.shape
    Fs = dg.shape[-1]
    tk = TOK_TILE
    nk = S // tk

    def body(h_ref, dg_ref, du_ref, a_ref, df_ref, _g0, _g1, _g2, gwg_ref, gwu_ref, gwd_ref, ag_sc, au_sc, ad_sc):
        k = pl.program_id(1)

        @pl.when(k == 0)
        def _():
            ag_sc[...] = jnp.zeros_like(ag_sc)
            au_sc[...] = jnp.zeros_like(au_sc)
            ad_sc[...] = jnp.zeros_like(ad_sc)

        hv = h_ref[...]
        ag_sc[...] += _dot_tn(hv, dg_ref[0])
        au_sc[...] += _dot_tn(hv, du_ref[0])
        ad_sc[...] += _dot_tn(a_ref[0], df_ref[...])

        @pl.when(k == nk - 1)
        def _():
            gwg_ref[0, 0] = ag_sc[...].astype(bf16)
            gwu_ref[0, 0] = au_sc[...].astype(bf16)
            gwd_ref[0, 0] = ad_sc[...].astype(bf16)

    tok = pl.BlockSpec((tk, D), lambda p, k: (k, 0))
    hid = pl.BlockSpec((1, tk, Fs), lambda p, k: (p, k, 0))
    anyspec = pl.BlockSpec(memory_space=pl.ANY)
    ocol = pl.BlockSpec((1, 1, D, Fs), lambda p, k: (p, ls, 0, 0))
    orow = pl.BlockSpec((1, 1, Fs, D), lambda p, k: (p, ls, 0, 0))
    return pl.pallas_call(
        body, name="ffn_wgrad", grid=(N_CHIPS, nk),
        in_specs=[tok, hid, hid, hid, tok, anyspec, anyspec, anyspec],
        out_specs=[ocol, ocol, orow],
        out_shape=[jax.ShapeDtypeStruct(gwg.shape, bf16), jax.ShapeDtypeStruct(gwu.shape, bf16),
                   jax.ShapeDtypeStruct(gwd.shape, bf16)],
        scratch_shapes=[pltpu.VMEM((D, Fs), f32), pltpu.VMEM((D, Fs), f32), pltpu.VMEM((Fs, D), f32)],
        input_output_aliases={5: 0, 6: 1, 7: 2},
        compiler_params=_cp(("parallel", "arbitrary"), 48),
    )(h, dg, du, a, df, gwg, gwu, gwd)


def _mix_in_fwd(x, mod3, w_in, l):
    S, D = x.shape
    N = w_in.shape[-1]
    ts = TOK_TILE

    def body(x_ref, mod_ref, w_ref, z_ref, h_ref):
        @pl.when(pl.program_id(1) == 0)
        def _():
            xh, _ = _ln_stats(x_ref[...])
            h_ref[...] = (xh * (1.0 + mod_ref[1:2, :]) + mod_ref[0:1, :]).astype(bf16)

        z_ref[0] = _dot(h_ref[...], w_ref[0, 0])

    tok = pl.BlockSpec((ts, D), lambda i, j: (i, 0))
    return pl.pallas_call(
        body, name="mix_in_fwd", grid=(S // ts, N_CHIPS),
        in_specs=[tok, _full((3, D)), pl.BlockSpec((1, 1, D, N), lambda i, j: (j, l, 0, 0))],
        out_specs=[pl.BlockSpec((1, ts, N), lambda i, j: (j, i, 0)), tok],
        out_shape=[jax.ShapeDtypeStruct((N_CHIPS, S, N), f32), jax.ShapeDtypeStruct((S, D), bf16)],
        compiler_params=_cp(("parallel", "arbitrary"), 40),
    )(x, mod3, w_in)


def _mix_in_bwd(dz, dx_res, x, mod3, w_in, l):
    S, D = x.shape
    N = w_in.shape[-1]
    ts = TOK_TILE

    def body(dz_ref, dxr_ref, x_ref, mod_ref, w_ref, dx_ref, dmod_ref, acc_sc):
        i = pl.program_id(0)
        j = pl.program_id(1)

        @pl.when((i == 0) & (j == 0))
        def _():
            dmod_ref[...] = jnp.zeros_like(dmod_ref)

        @pl.when(j == 0)
        def _():
            acc_sc[...] = jnp.zeros_like(acc_sc)

        acc_sc[...] += _dot_nt(dz_ref[0], w_ref[0, 0])

        @pl.when(j == N_CHIPS - 1)
        def _():
            dh = acc_sc[...]
            xh, rstd0 = _ln_stats(x_ref[...])
            dmod_ref[0:1, :] += jnp.sum(dh, 0, keepdims=True)
            dmod_ref[1:2, :] += jnp.sum(dh * xh, 0, keepdims=True)
            dx_ref[...] = _ln_bwd(dh * (1.0 + mod_ref[1:2, :]), xh, rstd0) + dxr_ref[...]

    tok = pl.BlockSpec((ts, D), lambda i, j: (i, 0))
    return pl.pallas_call(
        body, name="mix_in_bwd", grid=(S // ts, N_CHIPS),
        in_specs=[pl.BlockSpec((1, ts, N), lambda i, j: (j, i, 0)), tok, tok, _full((3, D)),
                  pl.BlockSpec((1, 1, D, N), lambda i, j: (j, l, 0, 0))],
        out_specs=[tok, _full((3, D))],
        out_shape=[jax.ShapeDtypeStruct((S, D), f32), jax.ShapeDtypeStruct((3, D), f32)],
        scratch_shapes=[pltpu.VMEM((ts, D), f32)],
        compiler_params=_cp(("arbitrary", "arbitrary"), 40),
    )(dz, dx_res, x, mod3, w_in)


def _mix_in_wgrad(h, dz, gw, l):
    S, D = h.shape
    N = dz.shape[-1]
    tk = TOK_TILE
    nk = S // tk

    def body(h_ref, dz_ref, _g, gw_ref, acc_sc):
        k = pl.program_id(1)

        @pl.when(k == 0)
        def _():
            acc_sc[...] = jnp.zeros_like(acc_sc)

        acc_sc[...] += _dot_tn(h_ref[...], dz_ref[0])

        @pl.when(k == nk - 1)
        def _():
            gw_ref[0, 0] = acc_sc[...].astype(bf16)

    return pl.pallas_call(
        body, name="mix_in_wgrad", grid=(N_CHIPS, nk),
        in_specs=[pl.BlockSpec((tk, D), lambda p, k: (k, 0)), pl.BlockSpec((1, tk, N), lambda p, k: (p, k, 0)),
                  pl.BlockSpec(memory_space=pl.ANY)],
        out_specs=pl.BlockSpec((1, 1, D, N), lambda p, k: (p, l, 0, 0)),
        out_shape=jax.ShapeDtypeStruct(gw.shape, bf16),
        scratch_shapes=[pltpu.VMEM((D, N), f32)],
        input_output_aliases={2: 0},
        compiler_params=_cp(("parallel", "arbitrary"), 40),
    )(h, dz, gw)


def _mix_out_fwd(x, y_att, y_ssm, y_pool, mod3, w_out, l, lng, lnb):
    S, D = x.shape
    ts = TOK_TILE

    def body(x_ref, ya_ref, ys_ref, yp_ref, mod_ref, w_ref, lng_ref, lnb_ref, xo_ref, y_ref):
        ya = ya_ref[...].astype(bf16)
        y = (_dot(ya[:, 0:256], w_ref[0, 0]) + _dot(ya[:, 256:512], w_ref[1, 0])
             + _dot(ys_ref[...].astype(bf16), w_ref[2, 0]) + _dot(yp_ref[...].astype(bf16), w_ref[3, 0]))
        y_ref[...] = y
        rh, _ = _ln_stats(ALPHA * x_ref[...] + mod_ref[2:3, :] * y)
        xo_ref[...] = rh * lng_ref[...] + lnb_ref[...]

    tok = pl.BlockSpec((ts, D), lambda i: (i, 0))
    return pl.pallas_call(
        body, name="mix_out_fwd", grid=(S // ts,),
        in_specs=[tok, pl.BlockSpec((ts, D_ATT), lambda i: (i, 0)), pl.BlockSpec((ts, D_SSM), lambda i: (i, 0)),
                  pl.BlockSpec((ts, D_POOL), lambda i: (i, 0)), _full((3, D)),
                  pl.BlockSpec((N_CHIPS, 1, 256, D), lambda i: (0, l, 0, 0)), _full((1, D)), _full((1, D))],
        out_specs=[tok, tok],
        out_shape=[jax.ShapeDtypeStruct((S, D), f32), jax.ShapeDtypeStruct((S, D), f32)],
        compiler_params=_cp(("parallel",), 40),
    )(x, y_att, y_ssm, y_pool, mod3, w_out, lng, lnb)


def _mix_out_bwd(dxo, x, y, y_att, y_ssm, y_pool, mod3, w_out, l, lng, gw_out):
    S, D = x.shape
    ts = TOK_TILE
    nt = S // ts

    def body(dxo_ref, x_ref, y_ref, ya_ref, ys_ref, yp_ref, mod_ref, w_ref, lng_ref, _g,
             dxr_ref, da_ref, ds_ref, dp_ref, dgate_ref, dlng_ref, dlnb_ref, gw_ref, acc_sc):
        i = pl.program_id(0)

        @pl.when(i == 0)
        def _():
            dgate_ref[...] = jnp.zeros_like(dgate_ref)
            dlng_ref[...] = jnp.zeros_like(dlng_ref)
            dlnb_ref[...] = jnp.zeros_like(dlnb_ref)
            acc_sc[...] = jnp.zeros_like(acc_sc)

        gate = mod_ref[2:3, :]
        yv = y_ref[...]
        rh, rstd = _ln_stats(ALPHA * x_ref[...] + gate * yv)
        dy_out = dxo_ref[...]
        dlng_ref[...] += jnp.sum(dy_out * rh, 0, keepdims=True)
        dlnb_ref[...] += jnp.sum(dy_out, 0, keepdims=True)
        dr = _ln_bwd(dy_out * lng_ref[...], rh, rstd)
        dxr_ref[...] = ALPHA * dr
        dgate_ref[...] += jnp.sum(dr * yv, 0, keepdims=True)
        dy = (gate * dr).astype(bf16)
        da_ref[:, 0:256] = _dot_nt(dy, w_ref[0, 0])
        da_ref[:, 256:512] = _dot_nt(dy, w_ref[1, 0])
        ds_ref[...] = _dot_nt(dy, w_ref[2, 0])
        dp_ref[...] = _dot_nt(dy, w_ref[3, 0])
        ya = ya_ref[...].astype(bf16)
        acc_sc[0] += _dot_tn(ya[:, 0:256], dy)
        acc_sc[1] += _dot_tn(ya[:, 256:512], dy)
        acc_sc[2] += _dot_tn(ys_ref[...].astype(bf16), dy)
        acc_sc[3] += _dot_tn(yp_ref[...].astype(bf16), dy)

        @pl.when(i == nt - 1)
        def _():
            gw_ref[:, 0] = acc_sc[...].astype(bf16)

    tok = pl.BlockSpec((ts, D), lambda i: (i, 0))
    t512 = pl.BlockSpec((ts, D_ATT), lambda i: (i, 0))
    t256 = pl.BlockSpec((ts, 256), lambda i: (i, 0))
    wspec = pl.BlockSpec((N_CHIPS, 1, 256, D), lambda i: (0, l, 0, 0))
    return pl.pallas_call(
        body, name="mix_out_bwd", grid=(nt,),
        in_specs=[tok, tok, tok, t512, t256, t256, _full((3, D)), wspec, _full((1, D)), pl.BlockSpec(memory_space=pl.ANY)],
        out_specs=[tok, t512, t256, t256, _full((1, D)), _full((1, D)), _full((1, D)), wspec],
        out_shape=[jax.ShapeDtypeStruct((S, D), f32), jax.ShapeDtypeStruct((S, D_ATT), f32),
                   jax.ShapeDtypeStruct((S, D_SSM), f32), jax.ShapeDtypeStruct((S, D_POOL), f32),
                   jax.ShapeDtypeStruct((1, D), f32), jax.ShapeDtypeStruct((1, D), f32), jax.ShapeDtypeStruct((1, D), f32),
                   jax.ShapeDtypeStruct(gw_out.shape, bf16)],
        scratch_shapes=[pltpu.VMEM((N_CHIPS, 256, D), f32)],
        input_output_aliases={9: 7},
        compiler_params=_cp(("arbitrary",), 48),
    )(dxo, x, y, y_att, y_ssm, y_pool, mod3, w_out, lng, gw_out)


def _t5_bucket(dist):
    max_exact = N_BUCKETS // 2
    d = np.maximum(dist, 1).astype(np.float32)
    large = max_exact + (np.log(d / max_exact) / math.log(MAX_DISTANCE / max_exact)
                         * (N_BUCKETS - max_exact)).astype(np.int32)
    large = np.minimum(large, N_BUCKETS - 1)
    return np.where(dist < max_exact, dist, large).astype(np.int32)


def _bucket_table():
    q = ATT_BLOCK
    i = np.arange(q)[:, None]
    j = np.arange(2 * q)[None, :]
    r = i + q - j
    in_band = (r >= 0) & (r <= q)
    tabs = [np.where(in_band, _t5_bucket(np.clip(r, 0, None) * d), -1) for d in DILATIONS]
    return np.stack(tabs).astype(np.int32)


def _bias_fwd(rel_bias, table):
    def body(rb_ref, tab_ref, out_ref):
        for b in range(3):
            tb = tab_ref[b]
            for h in range(N_HEADS):
                def pick(k, acc):
                    return jnp.where(tb == k, rb_ref[k, h], acc)
                out_ref[b, h] = lax.fori_loop(0, N_BUCKETS, pick, jnp.where(tb < 0, NEG, 0.0).astype(f32))

    return pl.pallas_call(
        body, name="bias_fwd",
        in_specs=[pl.BlockSpec(memory_space=pltpu.SMEM), pl.BlockSpec(memory_space=pltpu.VMEM)],
        out_specs=pl.BlockSpec(memory_space=pltpu.VMEM),
        out_shape=jax.ShapeDtypeStruct((3, N_HEADS, ATT_BLOCK, 2 * ATT_BLOCK), f32),
    )(rel_bias, table)


def _bias_bwd(dbias, table):
    def body(db_ref, tab_ref, out_ref):
        def per_bucket(k, c):
            for h in range(N_HEADS):
                tot = jnp.zeros((), f32)
                for b in range(3):
                    tot = tot + jnp.sum(jnp.where(tab_ref[b] == k, db_ref[b, h], 0.0))
                out_ref[k, h] = tot
            return c
        lax.fori_loop(0, N_BUCKETS, per_bucket, 0)

    return pl.pallas_call(
        body, name="bias_bwd",
        in_specs=[pl.BlockSpec(memory_space=pltpu.VMEM), pl.BlockSpec(memory_space=pltpu.VMEM)],
        out_specs=pl.BlockSpec(memory_space=pltpu.SMEM),
        out_shape=jax.ShapeDtypeStruct((N_BUCKETS, N_HEADS), f32),
    )(dbias, table)


def _att_scores(q_ref, k_ref, b_ref, r0, h, valid_prev):
    cs = pl.ds(HEAD_DIM * h, HEAD_DIM)
    q = q_ref[0, pl.ds(r0, ATT_BLOCK), cs]
    kb = k_ref[0, pl.ds(r0, 2 * ATT_BLOCK), cs]
    s = _dot_nt(q, kb) + b_ref[0, h]
    col = lax.broadcasted_iota(jnp.int32, s.shape, 1)
    return q, kb, jnp.where((col >= ATT_BLOCK) | valid_prev, s, NEG)


def _blocks_per_residue():
    br = pl.program_id(0)
    return jnp.where(br == 0, BLOCKS_PER_RESIDUE[0], jnp.where(br == 1, BLOCKS_PER_RESIDUE[1], BLOCKS_PER_RESIDUE[2]))


def _att_fwd(qp, kp, vp, bias):
    S = qp.shape[1]

    def body(q_ref, k_ref, v_ref, b_ref, o_ref, lse_ref):
        nbr = _blocks_per_residue()

        def unit(u, c):
            r0 = pl.multiple_of(u * ATT_BLOCK, ATT_BLOCK)
            valid_prev = (u % nbr) != 0
            for h in range(N_HEADS):
                cs = pl.ds(HEAD_DIM * h, HEAD_DIM)
                _, _, s = _att_scores(q_ref, k_ref, b_ref, r0, h, valid_prev)
                m = jnp.max(s, -1, keepdims=True)
                p = jnp.exp(s - m)
                den = jnp.sum(p, -1, keepdims=True)
                vb = v_ref[0, pl.ds(r0, 2 * ATT_BLOCK), cs]
                o_ref[0, pl.ds(r0, ATT_BLOCK), cs] = _dot(p.astype(bf16), vb) / den
                lse_ref[0, pl.ds(r0, ATT_BLOCK), pl.ds(h, 1)] = m + jnp.log(den)
            return c

        lax.fori_loop(0, N_UNITS, unit, 0)

    qspec = pl.BlockSpec((1, S, D_ATT), lambda b: (b, 0, 0))
    kspec = pl.BlockSpec((1, S + ATT_BLOCK, D_ATT), lambda b: (b, 0, 0))
    return pl.pallas_call(
        body, name="att_fwd", grid=(3,),
        in_specs=[qspec, kspec, kspec, pl.BlockSpec((1, N_HEADS, ATT_BLOCK, 2 * ATT_BLOCK), lambda b: (b, 0, 0, 0))],
        out_specs=[qspec, pl.BlockSpec((1, S, N_HEADS), lambda b: (b, 0, 0))],
        out_shape=[jax.ShapeDtypeStruct((3, S, D_ATT), f32), jax.ShapeDtypeStruct((3, S, N_HEADS), f32)],
        compiler_params=_cp(("parallel",), 48),
    )(qp, kp, vp, bias)


def _att_bwd(qp, kp, vp, dop, lsep, cp, bias):
    S = qp.shape[1]

    def body(q_ref, k_ref, v_ref, do_ref, lse_ref, c_ref, b_ref, dq_ref, dk_ref, dv_ref, db_ref, dk_sc, dv_sc):
        nbr = _blocks_per_residue()
        dk_sc[...] = jnp.zeros_like(dk_sc)
        dv_sc[...] = jnp.zeros_like(dv_sc)
        db_ref[...] = jnp.zeros_like(db_ref)

        def unit(u, c):
            r0 = pl.multiple_of(u * ATT_BLOCK, ATT_BLOCK)
            valid_prev = (u % nbr) != 0
            rows = pl.ds(r0, ATT_BLOCK)
            band = pl.ds(r0, 2 * ATT_BLOCK)
            for h in range(N_HEADS):
                cs = pl.ds(HEAD_DIM * h, HEAD_DIM)
                q, kb, s = _att_scores(q_ref, k_ref, b_ref, r0, h, valid_prev)
                p = jnp.exp(s - lse_ref[0, rows, pl.ds(h, 1)])
                do = do_ref[0, rows, cs]
                vb = v_ref[0, band, cs]
                ds = p * (_dot_nt(do, vb) - c_ref[0, rows, pl.ds(h, 1)])
                db_ref[0, h] += ds
                dsb = ds.astype(bf16)
                dq_ref[0, rows, cs] = (HEAD_DIM ** -0.5 * _dot(dsb, kb)).astype(bf16)
                dk_sc[band, cs] += _dot_tn(dsb, q)
                dv_sc[band, cs] += _dot_tn(p.astype(bf16), do)
            return c

        lax.fori_loop(0, N_UNITS, unit, 0)
        dk_ref[0] = dk_sc[...].astype(bf16)
        dv_ref[0] = dv_sc[...].astype(bf16)

    qspec = pl.BlockSpec((1, S, D_ATT), lambda b: (b, 0, 0))
    kspec = pl.BlockSpec((1, S + ATT_BLOCK, D_ATT), lambda b: (b, 0, 0))
    hspec = pl.BlockSpec((1, S, N_HEADS), lambda b: (b, 0, 0))
    bspec = pl.BlockSpec((1, N_HEADS, ATT_BLOCK, 2 * ATT_BLOCK), lambda b: (b, 0, 0, 0))
    return pl.pallas_call(
        body, name="att_bwd", grid=(3,),
        in_specs=[qspec, kspec, kspec, qspec, hspec, hspec, bspec],
        out_specs=[qspec, kspec, kspec, bspec],
        out_shape=[jax.ShapeDtypeStruct((3, S, D_ATT), bf16), jax.ShapeDtypeStruct((3, S + ATT_BLOCK, D_ATT), bf16),
                   jax.ShapeDtypeStruct((3, S + ATT_BLOCK, D_ATT), bf16),
                   jax.ShapeDtypeStruct((3, N_HEADS, ATT_BLOCK, 2 * ATT_BLOCK), f32)],
        scratch_shapes=[pltpu.VMEM((S + ATT_BLOCK, D_ATT), f32), pltpu.VMEM((S + ATT_BLOCK, D_ATT), f32)],
        compiler_params=_cp(("arbitrary",), 56),
    )(qp, kp, vp, dop, lsep, cp, bias)


def _branch_weights(lse_ref):
    l0, l1, l2 = lse_ref[0], lse_ref[1], lse_ref[2]
    m = jnp.maximum(jnp.maximum(l0, l1), l2)
    e0, e1, e2 = jnp.exp(l0 - m), jnp.exp(l1 - m), jnp.exp(l2 - m)
    tot = e0 + e1 + e2
    return e0 / tot, e1 / tot, e2 / tot


def _att_merge(o3, lse3):
    S = o3.shape[1]
    ts = TOK_TILE

    def body(o_ref, lse_ref, y_ref):
        w = _branch_weights(lse_ref)
        for h in range(N_HEADS):
            cs = pl.ds(HEAD_DIM * h, HEAD_DIM)
            y_ref[:, cs] = (w[0][:, h:h + 1] * o_ref[0, :, cs] + w[1][:, h:h + 1] * o_ref[1, :, cs]
                            + w[2][:, h:h + 1] * o_ref[2, :, cs])

    return pl.pallas_call(
        body, name="att_merge", grid=(S // ts,),
        in_specs=[pl.BlockSpec((3, ts, D_ATT), lambda i: (0, i, 0)), pl.BlockSpec((3, ts, N_HEADS), lambda i: (0, i, 0))],
        out_specs=pl.BlockSpec((ts, D_ATT), lambda i: (i, 0)),
        out_shape=jax.ShapeDtypeStruct((S, D_ATT), f32),
        compiler_params=_cp(("parallel",)),
    )(o3, lse3)


def _att_merge_bwd(dy, y, lse3):
    S = dy.shape[0]
    ts = TOK_TILE

    def body(dy_ref, y_ref, lse_ref, do_ref, c_ref):
        w = _branch_weights(lse_ref)
        for h in range(N_HEADS):
            cs = pl.ds(HEAD_DIM * h, HEAD_DIM)
            dyh = dy_ref[:, cs]
            t = jnp.sum(dyh * y_ref[:, cs], -1, keepdims=True)
            for p in range(3):
                wp = w[p][:, h:h + 1]
                do_ref[p, :, cs] = (wp * dyh).astype(bf16)
                c_ref[p, :, pl.ds(h, 1)] = wp * t

    return pl.pallas_call(
        body, name="att_merge_bwd", grid=(S // ts,),
        in_specs=[pl.BlockSpec((ts, D_ATT), lambda i: (i, 0)), pl.BlockSpec((ts, D_ATT), lambda i: (i, 0)),
                  pl.BlockSpec((3, ts, N_HEADS), lambda i: (0, i, 0))],
        out_specs=[pl.BlockSpec((3, ts, D_ATT), lambda i: (0, i, 0)), pl.BlockSpec((3, ts, N_HEADS), lambda i: (0, i, 0))],
        out_shape=[jax.ShapeDtypeStruct((3, S, D_ATT), bf16), jax.ShapeDtypeStruct((3, S, N_HEADS), f32)],
        compiler_params=_cp(("parallel",)),
    )(dy, y, lse3)


def _ssm_scan(xr, xi, a2, reverse):
    S, N = xr.shape
    nst = S // SCAN_SEG

    def body(xr_ref, xi_ref, a_ref, sr_ref, si_ref):
        ar = jnp.broadcast_to(a_ref[0:1, :], (SCAN_SEG, N))
        ai = jnp.broadcast_to(a_ref[1:2, :], (SCAN_SEG, N))
        if reverse:
            ai = -ai
        row = lax.broadcasted_iota(jnp.int32, (SCAN_SEG, N), 0)
        zero = jnp.zeros((SCAN_SEG, N), f32)

        def tile(t):
            return pl.ds(pl.multiple_of((nst - 1 - t if reverse else t) * SCAN_SEG, SCAN_SEG), SCAN_SEG)

        def local(t, c):
            sr, si, pr, pi = c
            rows = tile(t)
            nsr = ar * sr - ai * si + xr_ref[rows, :]
            nsi = ar * si + ai * sr + xi_ref[rows, :]
            sr_ref[rows, :] = nsr
            si_ref[rows, :] = nsi
            return nsr, nsi, ar * pr - ai * pi, ar * pi + ai * pr

        fr, fi, apr, api = lax.fori_loop(0, nst, local, (zero, zero, zero + 1.0, zero))

        def shift(v):
            if reverse:
                return jnp.where(row == SCAN_SEG - 1, 0.0, pltpu.roll(v, SCAN_SEG - 1, axis=0))
            return jnp.where(row == 0, 0.0, pltpu.roll(v, 1, axis=0))

        cr, ci = zero, zero
        for _ in range(SCAN_SEG - 1):
            cr, ci = shift(fr + apr * cr - api * ci), shift(fi + apr * ci + api * cr)

        def fix(t, c):
            pr, pi = c
            npr, npi = ar * pr - ai * pi, ar * pi + ai * pr
            rows = tile(t)
            sr_ref[rows, :] += npr * cr - npi * ci
            si_ref[rows, :] += npr * ci + npi * cr
            return npr, npi

        lax.fori_loop(0, nst, fix, (zero + 1.0, zero))

    vm = pl.BlockSpec(memory_space=pltpu.VMEM)
    return pl.pallas_call(
        body, name="ssm_scan_rev" if reverse else "ssm_scan",
        in_specs=[vm, vm, vm], out_specs=[vm, vm],
        out_shape=[jax.ShapeDtypeStruct((S, N), f32), jax.ShapeDtypeStruct((S, N), f32)],
        compiler_params=_cp(None, 48),
    )(xr, xi, a2)


def _ssm_in(u, bre, bim):
    S = u.shape[0]
    ts = TOK_TILE

    def body(u_ref, br_ref, bi_ref, or_ref, oi_ref):
        ub = u_ref[...].astype(bf16)
        or_ref[...] = _dot(ub, br_ref[...].astype(bf16))
        oi_ref[...] = _dot(ub, bi_ref[...].astype(bf16))

    return pl.pallas_call(
        body, name="ssm_in", grid=(S // ts,),
        in_specs=[pl.BlockSpec((ts, D_SSM), lambda i: (i, 0)), _full((D_SSM, D_STATE)), _full((D_SSM, D_STATE))],
        out_specs=[pl.BlockSpec((ts, D_STATE), lambda i: (i, 0))] * 2,
        out_shape=[jax.ShapeDtypeStruct((S, D_STATE), f32)] * 2,
        compiler_params=_cp(("parallel",)),
    )(u, bre, bim)


def _ssm_out(sr, si, u, cre, cim, dskip, glu_w, glu_b):
    S = u.shape[0]
    ts = TOK_TILE

    def body(sr_ref, si_ref, u_ref, cr_ref, ci_ref, d_ref, w_ref, b_ref, out_ref, y_ref):
        y = (_dot(sr_ref[...].astype(bf16), cr_ref[...].astype(bf16))
             - _dot(si_ref[...].astype(bf16), ci_ref[...].astype(bf16)) + d_ref[...] * u_ref[...])
        y_ref[...] = y
        z = _dot(_gelu(y).astype(bf16), w_ref[...].astype(bf16)) + b_ref[...]
        out_ref[...] = y * jax.nn.sigmoid(z)

    st = pl.BlockSpec((ts, D_STATE), lambda i: (i, 0))
    ch = pl.BlockSpec((ts, D_SSM), lambda i: (i, 0))
    return pl.pallas_call(
        body, name="ssm_out", grid=(S // ts,),
        in_specs=[st, st, ch, _full((D_STATE, D_SSM)), _full((D_STATE, D_SSM)), _full((1, D_SSM)),
                  _full((D_SSM, D_SSM)), _full((1, D_SSM))],
        out_specs=[ch, ch],
        out_shape=[jax.ShapeDtypeStruct((S, D_SSM), f32)] * 2,
        compiler_params=_cp(("parallel",)),
    )(sr, si, u, cre, cim, dskip, glu_w, glu_b)


def _ssm_out_bwd(dout, y, u, sr, si, cre, cim, dskip, glu_w, glu_b):
    S = u.shape[0]
    ts = TOK_TILE

    def body(do_ref, y_ref, u_ref, sr_ref, si_ref, cr_ref, ci_ref, d_ref, w_ref, b_ref,
             gr_ref, gi_ref, du_ref, dcr_ref, dci_ref, dd_ref, dgb_ref, dgw_ref):
        @pl.when(pl.program_id(0) == 0)
        def _():
            for r in (dcr_ref, dci_ref, dd_ref, dgb_ref, dgw_ref):
                r[...] = jnp.zeros_like(r)

        y = y_ref[...]
        dout = do_ref[...]
        wb = w_ref[...].astype(bf16)
        ge = _gelu(y).astype(bf16)
        sz = jax.nn.sigmoid(_dot(ge, wb) + b_ref[...])
        dz = dout * y * sz * (1.0 - sz)
        dzb = dz.astype(bf16)
        dgb_ref[...] += jnp.sum(dz, 0, keepdims=True)
        dgw_ref[...] += _dot_tn(ge, dzb)
        dy = dout * sz + _gelu_grad(y) * _dot_nt(dzb, wb)
        uv = u_ref[...]
        dd_ref[...] += jnp.sum(dy * uv, 0, keepdims=True)
        du_ref[...] = dy * d_ref[...]
        dyb = dy.astype(bf16)
        gr_ref[...] = _dot_nt(dyb, cr_ref[...].astype(bf16))
        gi_ref[...] = -_dot_nt(dyb, ci_ref[...].astype(bf16))
        dcr_ref[...] += _dot_tn(sr_ref[...].astype(bf16), dyb)
        dci_ref[...] -= _dot_tn(si_ref[...].astype(bf16), dyb)

    st = pl.BlockSpec((ts, D_STATE), lambda i: (i, 0))
    ch = pl.BlockSpec((ts, D_SSM), lambda i: (i, 0))
    c_full = _full((D_STATE, D_SSM))
    return pl.pallas_call(
        body, name="ssm_out_bwd", grid=(S // ts,),
        in_specs=[ch, ch, ch, st, st, c_full, c_full, _full((1, D_SSM)), _full((D_SSM, D_SSM)), _full((1, D_SSM))],
        out_specs=[st, st, ch, c_full, c_full, _full((1, D_SSM)), _full((1, D_SSM)), _full((D_SSM, D_SSM))],
        out_shape=[jax.ShapeDtypeStruct((S, D_STATE), f32), jax.ShapeDtypeStruct((S, D_STATE), f32),
                   jax.ShapeDtypeStruct((S, D_SSM), f32), jax.ShapeDtypeStruct((D_STATE, D_SSM), f32),
                   jax.ShapeDtypeStruct((D_STATE, D_SSM), f32), jax.ShapeDtypeStruct((1, D_SSM), f32),
                   jax.ShapeDtypeStruct((1, D_SSM), f32), jax.ShapeDtypeStruct((D_SSM, D_SSM), f32)],
        compiler_params=_cp(("arbitrary",), 40),
    )(dout, y, u, sr, si, cre, cim, dskip, glu_w, glu_b)


def _ssm_in_bwd(lr, li, u, du_skip, bre, bim):
    S = u.shape[0]
    ts = TOK_TILE

    def body(lr_ref, li_ref, u_ref, dus_ref, br_ref, bi_ref, du_ref, dbr_ref, dbi_ref):
        @pl.when(pl.program_id(0) == 0)
        def _():
            dbr_ref[...] = jnp.zeros_like(dbr_ref)
            dbi_ref[...] = jnp.zeros_like(dbi_ref)

        lrb = lr_ref[...].astype(bf16)
        lib = li_ref[...].astype(bf16)
        du_ref[...] = dus_ref[...] + _dot_nt(lrb, br_ref[...].astype(bf16)) + _dot_nt(lib, bi_ref[...].astype(bf16))
        ub = u_ref[...].astype(bf16)
        dbr_ref[...] += _dot_tn(ub, lrb)
        dbi_ref[...] += _dot_tn(ub, lib)

    st = pl.BlockSpec((ts, D_STATE), lambda i: (i, 0))
    ch = pl.BlockSpec((ts, D_SSM), lambda i: (i, 0))
    b_full = _full((D_SSM, D_STATE))
    return pl.pallas_call(
        body, name="ssm_in_bwd", grid=(S // ts,),
        in_specs=[st, st, ch, ch, b_full, b_full],
        out_specs=[ch, b_full, b_full],
        out_shape=[jax.ShapeDtypeStruct((S, D_SSM), f32), jax.ShapeDtypeStruct((D_SSM, D_STATE), f32),
                   jax.ShapeDtypeStruct((D_SSM, D_STATE), f32)],
        compiler_params=_cp(("arbitrary",), 40),
    )(lr, li, u, du_skip, bre, bim)


def _ssm_da(lr, li, sr, si):
    S, N = lr.shape
    nst = S // SCAN_SEG

    def body(lr_ref, li_ref, sr_ref, si_ref, out_ref):
        row = lax.broadcasted_iota(jnp.int32, (SCAN_SEG, N), 0)
        last = pl.ds((nst - 1) * SCAN_SEG, SCAN_SEG)
        pr = jnp.where(row == 0, 0.0, pltpu.roll(sr_ref[last, :], 1, axis=0))
        pi = jnp.where(row == 0, 0.0, pltpu.roll(si_ref[last, :], 1, axis=0))
        first = pl.ds(0, SCAN_SEG)
        acc_r = lr_ref[first, :] * pr + li_ref[first, :] * pi
        acc_i = li_ref[first, :] * pr - lr_ref[first, :] * pi

        def step(t, c):
            acc_r, acc_i = c
            rows = pl.ds(pl.multiple_of(t * SCAN_SEG, SCAN_SEG), SCAN_SEG)
            prev = pl.ds(pl.multiple_of((t - 1) * SCAN_SEG, SCAN_SEG), SCAN_SEG)
            lrv, liv, srv, siv = lr_ref[rows, :], li_ref[rows, :], sr_ref[prev, :], si_ref[prev, :]
            return acc_r + lrv * srv + liv * siv, acc_i + liv * srv - lrv * siv

        acc_r, acc_i = lax.fori_loop(1, nst, step, (acc_r, acc_i))
        out_ref[0:1, :] = jnp.sum(acc_r, 0, keepdims=True)
        out_ref[1:2, :] = jnp.sum(acc_i, 0, keepdims=True)

    vm = pl.BlockSpec(memory_space=pltpu.VMEM)
    return pl.pallas_call(
        body, name="ssm_da", in_specs=[vm] * 4, out_specs=vm,
        out_shape=jax.ShapeDtypeStruct((2, N), f32),
        compiler_params=_cp(None, 48),
    )(lr, li, sr, si)


_POOL_TILE = 256


def _window_sums(xt, back):
    n = xt.shape[0]
    out = []
    ws = xt
    for k in (1, 2, 4, 8):
        ws = ws + pltpu.roll(ws, k if back else n - k, axis=0)
        out.append(ws)
    return out


def _pool_count(r0, w):
    t = r0 + lax.broadcasted_iota(jnp.int32, (_POOL_TILE, POOL_GROUP), 0)
    return jnp.minimum(t + 1, w).astype(f32)


def _pool_fwd(u_pad, pool_w, pool_scale):
    S = u_pad.shape[0] - POOL_HALO
    nt = S // _POOL_TILE

    def body(u_ref, w_ref, sc_ref, y_ref):
        def tile(t, c):
            r0 = pl.multiple_of(t * _POOL_TILE, _POOL_TILE)
            for g, w in enumerate(POOL_WINDOWS):
                cs = pl.ds(POOL_GROUP * g, POOL_GROUP)
                xt = u_ref[pl.ds(r0, _POOL_TILE + POOL_HALO), cs]
                ws = _window_sums(xt, True)[g][POOL_HALO:, :]
                pooled = ws / _pool_count(r0, w) - xt[POOL_HALO:, :]
                y_ref[pl.ds(r0, _POOL_TILE), cs] = _dot(pooled.astype(bf16), w_ref[g].astype(bf16)) * sc_ref[:, cs]
            return c
        lax.fori_loop(0, nt, tile, 0)

    vm = pl.BlockSpec(memory_space=pltpu.VMEM)
    return pl.pallas_call(
        body, name="pool_fwd", in_specs=[vm, vm, vm], out_specs=vm,
        out_shape=jax.ShapeDtypeStruct((S, D_POOL), f32),
    )(u_pad, pool_w, pool_scale)


def _pool_bwd(dy_pad, u_pad, pool_w, pool_scale):
    S = u_pad.shape[0] - POOL_HALO
    nt = S // _POOL_TILE
    n = _POOL_TILE + POOL_HALO

    def body(dy_ref, u_ref, w_ref, sc_ref, du_ref, dw_ref, dsc_ref):
        dw_ref[...] = jnp.zeros_like(dw_ref)
        dsc_ref[...] = jnp.zeros_like(dsc_ref)

        def tile(t, c):
            r0 = pl.multiple_of(t * _POOL_TILE, _POOL_TILE)
            for g, w in enumerate(POOL_WINDOWS):
                cs = pl.ds(POOL_GROUP * g, POOL_GROUP)
                wb = w_ref[g].astype(bf16)
                xt = u_ref[pl.ds(r0, n), cs]
                pooled = (_window_sums(xt, True)[g][POOL_HALO:, :] / _pool_count(r0, w) - xt[POOL_HALO:, :]).astype(bf16)
                dy = dy_ref[pl.ds(r0, _POOL_TILE), cs]
                dsc_ref[:, cs] += jnp.sum(dy * _dot(pooled, wb), 0, keepdims=True)
                dw_ref[g] += _dot_tn(pooled, (dy * sc_ref[:, cs]).astype(bf16))
                dyh = (dy_ref[pl.ds(r0, n), cs] * sc_ref[:, cs]).astype(bf16)
                dpl = _dot_nt(dyh, wb)
                cnt = jnp.minimum(r0 + lax.broadcasted_iota(jnp.int32, (n, POOL_GROUP), 0) + 1, w).astype(f32)
                lead = _window_sums(dpl / cnt, False)[g]
                du_ref[pl.ds(r0, _POOL_TILE), cs] = lead[:_POOL_TILE, :] - dpl[:_POOL_TILE, :]
            return c
        lax.fori_loop(0, nt, tile, 0)

    vm = pl.BlockSpec(memory_space=pltpu.VMEM)
    return pl.pallas_call(
        body, name="pool_bwd", in_specs=[vm, vm, vm, vm], out_specs=[vm, vm, vm],
        out_shape=[jax.ShapeDtypeStruct((S, D_POOL), f32), jax.ShapeDtypeStruct((4, POOL_GROUP, POOL_GROUP), f32),
                   jax.ShapeDtypeStruct((1, D_POOL), f32)],
    )(dy_pad, u_pad, pool_w, pool_scale)


def _loss_head(y, target):
    S, D = y.shape
    ts = TOK_TILE

    def body(y_ref, t_ref, loss_ref, dy_ref):
        @pl.when(pl.program_id(0) == 0)
        def _():
            loss_ref[...] = jnp.zeros_like(loss_ref)

        d = y_ref[...] - t_ref[...]
        dy_ref[...] = d * (1.0 / D)
        loss_ref[...] += 0.5 * jnp.sum(jnp.sum(d * d, -1, keepdims=True) * (1.0 / D), 0, keepdims=True)

    tok = pl.BlockSpec((ts, D), lambda i: (i, 0))
    return pl.pallas_call(
        body, name="loss_head", grid=(S // ts,),
        in_specs=[tok, tok], out_specs=[_full((1, 1)), tok],
        out_shape=[jax.ShapeDtypeStruct((1, 1), f32), jax.ShapeDtypeStruct((S, D), f32)],
        compiler_params=_cp(("arbitrary",)),
    )(y, target)


_ADA_COLS = 768


def _ada_fwd(c_all, ada_w, ada_b_cols):
    L, D, N = ada_w.shape
    B = c_all.shape[0]

    def body(c_ref, w_ref, b_ref, out_ref):
        cv = c_ref[...]
        cond = (cv * jax.nn.sigmoid(cv)).astype(bf16)
        out_ref[0] = _dot(cond, w_ref[0].astype(bf16)) + b_ref[0]

    return pl.pallas_call(
        body, name="ada_fwd", grid=(L, N // _ADA_COLS),
        in_specs=[_full((B, D)), pl.BlockSpec((1, D, _ADA_COLS), lambda l, j: (l, 0, j)),
                  pl.BlockSpec((1, 1, _ADA_COLS), lambda l, j: (l, 0, j))],
        out_specs=pl.BlockSpec((1, B, _ADA_COLS), lambda l, j: (l, 0, j)),
        out_shape=jax.ShapeDtypeStruct((L, B, N), f32),
        compiler_params=_cp(("parallel", "parallel")),
    )(c_all, ada_w, ada_b_cols)


def _ada_wgrad(c_all_t, dmod_cols):
    D, B = c_all_t.shape
    L, _, N = dmod_cols.shape

    def body(ct_ref, dm_ref, out_ref):
        cv = ct_ref[...]
        cond = cv * jax.nn.sigmoid(cv)
        acc = cond[:, 0:1] * dm_ref[0, 0:1, :]
        for b in range(1, B):
            acc = acc + cond[:, b:b + 1] * dm_ref[0, b:b + 1, :]
        out_ref[0] = acc

    return pl.pallas_call(
        body, name="ada_wgrad", grid=(L, N // _ADA_COLS),
        in_specs=[_full((D, B)), pl.BlockSpec((1, B, _ADA_COLS), lambda l, j: (l, 0, j))],
        out_specs=pl.BlockSpec((1, D, _ADA_COLS), lambda l, j: (l, 0, j)),
        out_shape=jax.ShapeDtypeStruct((L, D, N), f32),
        compiler_params=_cp(("parallel", "parallel")),
    )(c_all_t, dmod_cols)


def _adam_math(w, g, m, v):
    m = ADAM_B1 * m + (1.0 - ADAM_B1) * g
    v = ADAM_B2 * v + (1.0 - ADAM_B2) * (g * g)
    m_hat = m / (1.0 - ADAM_B1 ** ADAM_STEP)
    v_hat = v / (1.0 - ADAM_B2 ** ADAM_STEP)
    delta = -ADAM_LR * (m_hat / (jnp.sqrt(v_hat) + ADAM_EPS) + ADAM_WD * w)
    return delta, m, v


def _adamw(w, m, v, g, row_tile, row0=0, outs=None):
    R, C = w.shape
    b0 = row0 // row_tile

    def body(w_ref, m_ref, v_ref, g_ref, _0, _1, _2, _3, g_out, d_out, m_out, v_out):
        gv = g_ref[...]
        delta, mn, vn = _adam_math(w_ref[...], gv, m_ref[...], v_ref[...])
        g_out[...] = gv
        d_out[...] = delta
        m_out[...] = mn
        v_out[...] = vn

    pspec = pl.BlockSpec((row_tile, C), lambda i: (b0 + i, 0))
    gspec = pl.BlockSpec((row_tile, C), lambda i: (i, 0))
    anyspec = pl.BlockSpec(memory_space=pl.ANY)
    shp = jax.ShapeDtypeStruct((R, C), f32)
    if outs is None:
        outs = [lax.empty((R, C), f32) for _ in range(4)]
    return pl.pallas_call(
        body, name="adamw", grid=(g.shape[0] // row_tile,),
        in_specs=[pspec] * 3 + [gspec] + [anyspec] * 4, out_specs=[pspec] * 4, out_shape=[shp] * 4,
        input_output_aliases={4: 0, 5: 1, 6: 2, 7: 3},
        compiler_params=_cp(("parallel",), 40),
    )(w, m, v, g, *outs)


def _pair_sum(g5, got, pc):
    _, LS, _, R2, C = g5.shape

    def body(pc_ref, own_ref, got_ref, out_ref):
        out_ref[0, 0] = (own_ref[0, 0, 0].astype(f32) + got_ref[0, 0].astype(f32)).astype(bf16)

    gs = pltpu.PrefetchScalarGridSpec(
        num_scalar_prefetch=1, grid=(N_CHIPS, LS),
        in_specs=[pl.BlockSpec((1, 1, 1, R2, C), lambda p, s, pc: (p, s, pc[1], 0, 0)),
                  pl.BlockSpec((1, 1, R2, C), lambda p, s, pc: (p, s, 0, 0))],
        out_specs=pl.BlockSpec((1, 1, R2, C), lambda p, s, pc: (p, s, 0, 0)),
    )
    return pl.pallas_call(
        body, name="pair_sum", grid_spec=gs, out_shape=jax.ShapeDtypeStruct((N_CHIPS, LS, R2, C), bf16),
        compiler_params=_cp(("parallel", "parallel")),
    )(pc, g5, got)


def _sum_shards(hsum, recv, pc):
    _, LS, R2, C = hsum.shape

    def body(pc_ref, own_ref, r_ref, out_ref):
        acc = own_ref[0, 0].astype(f32)
        for j in range(3):
            acc = acc + r_ref[j, 0].astype(f32)
        out_ref[0, 0] = acc

    gs = pltpu.PrefetchScalarGridSpec(
        num_scalar_prefetch=1, grid=(LS,),
        in_specs=[pl.BlockSpec((1, 1, R2, C), lambda s, pc: (pc[0], s, 0, 0)),
                  pl.BlockSpec((3, 1, R2, C), lambda s, pc: (0, s, 0, 0))],
        out_specs=pl.BlockSpec((1, 1, R2, C), lambda s, pc: (s, pc[1], 0, 0)),
    )
    return pl.pallas_call(
        body, name="sum_shards", grid_spec=gs, out_shape=jax.ShapeDtypeStruct((LS, 2, R2, C), f32),
        compiler_params=_cp(("parallel",)),
    )(pc, hsum, recv)


def _sum8(packs):
    _, R, C = packs.shape
    tr = R // 8 if R % 64 == 0 else R

    def body(p_ref, out_ref):
        acc = p_ref[0]
        for d in range(1, 8):
            acc = acc + p_ref[d]
        out_ref[...] = acc

    return pl.pallas_call(
        body, name="sum8", grid=(R // tr,),
        in_specs=[pl.BlockSpec((8, tr, C), lambda i: (0, i, 0))],
        out_specs=pl.BlockSpec((tr, C), lambda i: (i, 0)),
        out_shape=jax.ShapeDtypeStruct((R, C), f32),
        compiler_params=_cp(("parallel",)),
    )(packs)


def _allgather8(x_shard):
    m_per, n = x_shard.shape

    def body(x_ref, out_ref, send_sems, recv_sems, local_sem):
        x, y, c = lax.axis_index("x"), lax.axis_index("y"), lax.axis_index("c")
        me, sibling = (x, y, c), (x, y, 1 - c)
        chips = [(1 - x, y), (x, 1 - y), (1 - x, 1 - y)]

        def rows(px, py, pc):
            return out_ref.at[pl.ds((4 * px + 2 * py + pc) * m_per, m_per), :]

        def copy(k, block, to, src=None):
            return pltpu.make_async_remote_copy(
                src_ref=rows(*block) if src is None else src, dst_ref=rows(*block),
                send_sem=send_sems.at[k], recv_sem=recv_sems.at[k], device_id=to, device_id_type=MESH)

        mine = pltpu.make_async_copy(x_ref, rows(*me), local_sem)
        mine.start()
        first = [copy(0, me, sibling, src=x_ref)]
        first += [copy(1 + j, me, (*chip, c), src=x_ref) for j, chip in enumerate(chips)]
        for cp in first:
            cp.start()
        passed = [copy(4 + j, (*chip, c), sibling) for j, chip in enumerate(chips)]
        for j, chip in enumerate(chips):
            copy(1 + j, (*chip, c), me).wait_recv()
            passed[j].start()
        copy(0, sibling, me).wait_recv()
        for j, chip in enumerate(chips):
            copy(4 + j, (*chip, 1 - c), me).wait_recv()
        for cp in first + passed:
            cp.wait_send()
        mine.wait()

    return pl.pallas_call(
        body, name="allgather8",
        out_shape=jax.ShapeDtypeStruct((8 * m_per, n), x_shard.dtype),
        in_specs=[pl.BlockSpec(memory_space=pltpu.VMEM)],
        out_specs=pl.BlockSpec(memory_space=pltpu.VMEM),
        scratch_shapes=[pltpu.SemaphoreType.DMA((7,)), pltpu.SemaphoreType.DMA((7,)), pltpu.SemaphoreType.DMA],
        compiler_params=_cp(None, 48),
    )(x_shard)


def _other_chips():
    x, y = lax.axis_index("x"), lax.axis_index("y")
    return [(1 - x, y), (x, 1 - y), (1 - x, 1 - y)]


_HBM = pl.BlockSpec(memory_space=pltpu.HBM)
_SEM = pl.BlockSpec(memory_space=pltpu.SEMAPHORE)
_EFFECT = pltpu.SideEffectType.DATAFLOW_SIDE_EFFECTING


def _gather_copies(srcs, lands, send_sems, recv_sems):
    x, y, c = lax.axis_index("x"), lax.axis_index("y"), lax.axis_index("c")
    return [pltpu.make_async_remote_copy(
        src_ref=srcs[a].at[:, c], dst_ref=lands[a].at[2 * x + y, :, c], send_sem=send_sems.at[3 * a + j],
        recv_sem=recv_sems.at[3 * a + j], device_id=(cx, cy, c), device_id_type=MESH)
        for a in range(len(srcs)) for j, (cx, cy) in enumerate(_other_chips())]


def _gather_start(chunks, after):
    sizes = [len(srcs) for srcs, _ in chunks]
    flat = [t for srcs, lands in chunks for t in list(srcs) + list(lands)]
    nflat = len(flat)
    nsem = 2 * len(chunks)

    def body(*refs):
        ins, sems, token = refs[:nflat], refs[nflat + 1:nflat + 1 + nsem], refs[-1]
        off = 0
        for k, n in enumerate(sizes):
            for cp in _gather_copies(ins[off:off + n], ins[off + n:off + 2 * n], sems[2 * k], sems[2 * k + 1]):
                cp.start()
            off += 2 * n
        token[...] = jnp.zeros_like(token)

    res = pl.pallas_call(
        body, name="gather_start",
        out_shape=[pltpu.SemaphoreType.DMA((3 * n,)) for n in sizes for _ in range(2)]
        + [pltpu.HBM(t.shape, t.dtype) for t in flat] + [jax.ShapeDtypeStruct((8, 128), f32)],
        in_specs=[_HBM] * nflat + [pl.BlockSpec(memory_space=pl.ANY)],
        out_specs=[_SEM] * nsem + [_HBM] * nflat + [pl.BlockSpec(memory_space=pltpu.VMEM)],
        input_output_aliases={i: nsem + i for i in range(nflat)},
        compiler_params=pltpu.CompilerParams(has_side_effects=_EFFECT),
    )(*[pltpu.with_memory_space_constraint(t, pltpu.HBM) for t in flat], after)
    out, off = [], nsem
    for k, n in enumerate(sizes):
        out.append((res[2 * k], res[2 * k + 1], res[off:off + n], res[off + n:off + 2 * n]))
        off += 2 * n
    return out, res[-1]


def _gather_wait(send_sems, recv_sems, srcs, lands, after, name):
    n = len(srcs)

    def body(*refs):
        for cp in _gather_copies(refs[:n], refs[n:2 * n], refs[2 * n], refs[2 * n + 1]):
            cp.wait_send()
            cp.wait_recv()

    res = pl.pallas_call(
        body, name=name,
        out_shape=[pltpu.HBM(t.shape, t.dtype) for t in list(srcs) + list(lands)],
        in_specs=[_HBM] * (2 * n) + [_SEM, _SEM, pl.BlockSpec(memory_space=pl.ANY)],
        out_specs=[_HBM] * (2 * n),
        input_output_aliases={i: i for i in range(2 * n)},
        compiler_params=pltpu.CompilerParams(has_side_effects=_EFFECT),
    )(*srcs, *lands, send_sems, recv_sems, after)
    return res[n:]


def _gather_forward(lands):
    n = len(lands)

    def body(*refs):
        outs = refs[n:2 * n]
        send_sems, recv_sems = refs[2 * n:]
        x, y, c = lax.axis_index("x"), lax.axis_index("y"), lax.axis_index("c")
        sibling = (x, y, 1 - c)
        copies = []
        for a in range(n):
            for j, (cx, cy) in enumerate(_other_chips()):
                mine = outs[a].at[2 * cx + cy, :, c]
                cp = pltpu.make_async_remote_copy(src_ref=mine, dst_ref=mine, send_sem=send_sems.at[3 * a + j],
                                                  recv_sem=recv_sems.at[3 * a + j], device_id=sibling, device_id_type=MESH)
                cp.start()
                copies.append((cp, a, j, cx, cy))
        for cp, a, j, cx, cy in copies:
            cp.wait_send()
            theirs = outs[a].at[2 * cx + cy, :, 1 - c]
            pltpu.make_async_remote_copy(src_ref=theirs, dst_ref=theirs, send_sem=send_sems.at[3 * a + j],
                                         recv_sem=recv_sems.at[3 * a + j], device_id=sibling, device_id_type=MESH).wait_recv()

    hbm = pl.BlockSpec(memory_space=pl.ANY)
    return pl.pallas_call(
        body, name="gather_forward",
        out_shape=[jax.ShapeDtypeStruct(t.shape, t.dtype) for t in lands],
        in_specs=[hbm] * n, out_specs=[hbm] * n,
        input_output_aliases={a: a for a in range(n)},
        scratch_shapes=[pltpu.SemaphoreType.DMA((3 * n,)), pltpu.SemaphoreType.DMA((3 * n,))],
    )(*lands)


def _pair_exchange(g5s):
    n = len(g5s)

    def body(*refs):
        ins, outs = refs[:n], refs[n:2 * n]
        send_sems, recv_sems = refs[2 * n:]
        c = lax.axis_index("c")
        sibling = (lax.axis_index("x"), lax.axis_index("y"), 1 - c)
        copies = []
        for a in range(n):
            cp = pltpu.make_async_remote_copy(src_ref=ins[a].at[:, :, 1 - c], dst_ref=outs[a], send_sem=send_sems.at[a],
                                              recv_sem=recv_sems.at[a], device_id=sibling, device_id_type=MESH)
            cp.start()
            copies.append(cp)
        for cp in copies:
            cp.wait()

    hbm = pl.BlockSpec(memory_space=pl.ANY)
    return pl.pallas_call(
        body, name="pair_exchange",
        out_shape=[jax.ShapeDtypeStruct(g.shape[:2] + g.shape[3:], g.dtype) for g in g5s],
        in_specs=[hbm] * n, out_specs=[hbm] * n,
        scratch_shapes=[pltpu.SemaphoreType.DMA((n,)), pltpu.SemaphoreType.DMA((n,))],
    )(*g5s)


def _scatter_copies(srcs, lands, send_sems, recv_sems):
    c = lax.axis_index("c")
    return [pltpu.make_async_remote_copy(
        src_ref=srcs[a].at[2 * cx + cy], dst_ref=lands[a].at[j], send_sem=send_sems.at[3 * a + j],
        recv_sem=recv_sems.at[3 * a + j], device_id=(cx, cy, c), device_id_type=MESH)
        for a in range(len(srcs)) for j, (cx, cy) in enumerate(_other_chips())]


def _scatter_start(hsums, name, after=()):
    n = len(hsums)
    na = len(after)

    def body(*refs):
        srcs, lands = refs[:n], refs[n:2 * n]
        send_sems, recv_sems = refs[2 * n + na], refs[2 * n + na + 1]
        for cp in _scatter_copies(srcs, lands, send_sems, recv_sems):
            cp.start()
        refs[-1][...] = jnp.zeros_like(refs[-1])

    lands = [lax.empty((3,) + g.shape[1:], g.dtype) for g in hsums]
    res = pl.pallas_call(
        body, name=name,
        out_shape=[pltpu.SemaphoreType.DMA((3 * n,)), pltpu.SemaphoreType.DMA((3 * n,))]
        + [pltpu.HBM(g.shape, g.dtype) for g in hsums] + [pltpu.HBM(g.shape, g.dtype) for g in lands]
        + [jax.ShapeDtypeStruct((8, 128), f32)],
        in_specs=[_HBM] * (2 * n) + [pl.BlockSpec(memory_space=pl.ANY)] * na,
        out_specs=[_SEM, _SEM] + [_HBM] * (2 * n) + [pl.BlockSpec(memory_space=pltpu.VMEM)],
        input_output_aliases={i: i + 2 for i in range(2 * n)},
        compiler_params=pltpu.CompilerParams(has_side_effects=_EFFECT),
    )(*[pltpu.with_memory_space_constraint(t, pltpu.HBM) for t in list(hsums) + lands], *after)
    return (res[0], res[1], res[2:2 + n], res[2 + n:2 + 2 * n]), res[-1]


def _scatter_wait(send_sems, recv_sems, srcs, lands, after, name):
    n = len(srcs)
    extra = list(after)

    def body(*refs):
        s_refs, l_refs = refs[:n], refs[n:2 * n]
        ss, rs = refs[2 * n], refs[2 * n + 1]
        for cp in _scatter_copies(s_refs, l_refs, ss, rs):
            cp.wait_send()
            cp.wait_recv()

    res = pl.pallas_call(
        body, name=name,
        out_shape=[pltpu.HBM(g.shape, g.dtype) for g in srcs] + [pltpu.HBM(g.shape, g.dtype) for g in lands],
        in_specs=[_HBM] * (2 * n) + [_SEM, _SEM] + [pl.BlockSpec(memory_space=pl.ANY)] * len(extra),
        out_specs=[_HBM] * (2 * n),
        input_output_aliases={i: i for i in range(2 * n)},
        compiler_params=pltpu.CompilerParams(has_side_effects=_EFFECT),
    )(*srcs, *lands, send_sems, recv_sems, *extra)
    return res[:n], res[n:]


def _swap_halves(fulls):
    n = len(fulls)

    def body(*refs):
        ins, outs = refs[:n], refs[n:2 * n]
        send_sems, recv_sems = refs[2 * n:]
        c = lax.axis_index("c")
        sibling = (lax.axis_index("x"), lax.axis_index("y"), 1 - c)
        copies = []
        for a in range(n):
            cp = pltpu.make_async_remote_copy(src_ref=outs[a].at[:, c], dst_ref=outs[a].at[:, c], send_sem=send_sems.at[a],
                                              recv_sem=recv_sems.at[a], device_id=sibling, device_id_type=MESH)
            cp.start()
            copies.append(cp)
        for a, cp in enumerate(copies):
            cp.wait_send()
            theirs = outs[a].at[:, 1 - c]
            pltpu.make_async_remote_copy(src_ref=theirs, dst_ref=theirs, send_sem=send_sems.at[a], recv_sem=recv_sems.at[a],
                                         device_id=sibling, device_id_type=MESH).wait_recv()

    hbm = pl.BlockSpec(memory_space=pl.ANY)
    return pl.pallas_call(
        body, name="swap_halves",
        out_shape=[jax.ShapeDtypeStruct(p.shape, p.dtype) for p in fulls],
        in_specs=[hbm] * n, out_specs=[hbm] * n,
        input_output_aliases={a: a for a in range(n)},
        scratch_shapes=[pltpu.SemaphoreType.DMA((n,)), pltpu.SemaphoreType.DMA((n,))],
    )(*fulls)


def _to_residue(t, d):
    s, c = t.shape
    return t.reshape(s // d, d, c).transpose(1, 0, 2).reshape(s, c)


def _from_residue(t, d):
    s, c = t.shape
    return t.reshape(d, s // d, c).transpose(1, 0, 2).reshape(s, c)


def _to_segments(t):
    s, c = t.shape
    return t.reshape(SCAN_SEG, s // SCAN_SEG, c).transpose(1, 0, 2).reshape(s, c)


def _from_segments(t):
    s, c = t.shape
    return t.reshape(s // SCAN_SEG, SCAN_SEG, c).transpose(1, 0, 2).reshape(s, c)


def _ssm_operators(a_re, a_im, log_dt, b_re, b_im, c_re, c_im):
    lam = lax.complex(a_re, a_im)
    dt = jnp.exp(log_dt)[:, None]
    a_bar = jnp.exp(lam * dt)
    b_bar = ((a_bar - 1.0) / lam)[:, :, None] * lax.complex(b_re, b_im)
    eye = jnp.eye(N_GROUPS, dtype=f32)

    def embed_b(t):
        return (jnp.transpose(t, (0, 2, 1))[:, :, None, :] * eye[:, None, :, None]).reshape(D_SSM, D_STATE)

    def embed_c(t):
        return (jnp.transpose(t, (0, 2, 1))[:, :, None, :] * eye[:, None, :, None]).reshape(D_STATE, D_SSM)

    a2 = jnp.stack([a_bar.real.reshape(D_STATE), a_bar.imag.reshape(D_STATE)])
    return a2, embed_b(b_bar.real), embed_b(b_bar.imag), embed_c(c_re), embed_c(c_im)


def _local_step(x, target, mod, small, ffn_weights, mix_weights, grads_done):
    table = jnp.asarray(_bucket_table())
    bias = _bias_fwd(small["rel_bias"], table)
    L = DEPTH
    saved = []
    ssm_ops = []
    for l in range(L):
        sv = {}
        m9 = mod[l]
        sv["x0"] = x
        sv["w0"] = ffn_weights(l, 0, x)
        x, sv["f0"], sv["g0"], sv["u0"] = _ffn_fwd(x, m9[0:3], *sv["w0"], 0, small["ln_g"][l, 0:1], small["ln_b"][l, 0:1])
        sv["x1"] = x
        sv["w1"] = mix_weights(l, x)
        z, sv["h1"] = _mix_in_fwd(x, m9[3:6], sv["w1"][0], 0)
        q16 = (z[0] * HEAD_DIM ** -0.5).astype(bf16)
        k16 = z[1].astype(bf16)
        v16 = z[2].astype(bf16)
        pad = jnp.zeros((ATT_BLOCK, D_ATT), bf16)
        qp = jnp.stack([_to_residue(q16, d) for d in DILATIONS])
        kp = jnp.stack([jnp.concatenate([pad, _to_residue(k16, d)]) for d in DILATIONS])
        vp = jnp.stack([jnp.concatenate([pad, _to_residue(v16, d)]) for d in DILATIONS])
        op, lsep = _att_fwd(qp, kp, vp, bias)
        o3 = jnp.stack([_from_residue(op[b], d) for b, d in enumerate(DILATIONS)])
        lse3 = jnp.stack([_from_residue(lsep[b], d) for b, d in enumerate(DILATIONS)])
        y_att = _att_merge(o3, lse3)
        sv.update(qp=qp, kp=kp, vp=vp, lsep=lsep, lse3=lse3, y_att=y_att)

        prm = tuple(small[k][l] for k in ("ssm_a_re", "ssm_a_im", "ssm_log_dt", "ssm_b_re", "ssm_b_im", "ssm_c_re", "ssm_c_im"))
        (a2, bre, bim, cre, cim), ops_vjp = jax.vjp(_ssm_operators, *prm)
        ssm_ops.append(ops_vjp)
        u_ssm = _to_segments(z[3][:, :D_SSM])
        bur, bui = _ssm_in(u_ssm, bre, bim)
        sr, si = _ssm_scan(bur, bui, a2, False)
        dskip = small["ssm_d"][l][None, :]
        glu_b = small["glu_b"][l][None, :]
        out_seg, y_seg = _ssm_out(sr, si, u_ssm, cre, cim, dskip, small["glu_w"][l], glu_b)
        y_ssm = _from_segments(out_seg)
        sv.update(a2=a2, bre=bre, bim=bim, cre=cre, cim=cim, u_ssm=u_ssm, sr=sr, si=si, y_seg=y_seg, y_ssm=y_ssm)

        u_pool = jnp.concatenate([jnp.zeros((POOL_HALO, D_POOL), f32), z[3][:, D_SSM:]])
        y_pool = _pool_fwd(u_pool, small["pool_w"][l], small["pool_scale"][l][None, :])
        sv.update(u_pool=u_pool, y_pool=y_pool)

        x, sv["ymix"] = _mix_out_fwd(x, y_att, y_ssm, y_pool, m9[3:6], sv["w1"][1], 0, small["ln_g"][l, 1:2], small["ln_b"][l, 1:2])
        sv["x2"] = x
        sv["w2"] = ffn_weights(l, 1, x)
        x, sv["f2"], sv["g2"], sv["u2"] = _ffn_fwd(x, m9[6:9], *sv["w2"], 0, small["ln_g"][l, 2:3], small["ln_b"][l, 2:3])
        saved.append(sv)

    loss, dx = _loss_head(x, target)

    dmod = [None] * L
    dln_g = [None] * L
    dln_b = [None] * L
    sg = {k: [None] * L for k in ("ssm_a_re", "ssm_a_im", "ssm_log_dt", "ssm_b_re", "ssm_b_im", "ssm_c_re", "ssm_c_im",
                                  "ssm_d", "glu_w", "glu_b", "pool_w", "pool_scale")}
    dbias_tot = None
    order_after = jnp.zeros((), f32)
    for l in reversed(range(L)):
        sv = saved[l]
        m9 = mod[l] + order_after

        def fresh(like):
            return [lax.empty(t.shape, bf16) for t in like]

        dx, dg, du, a, h, df, dm2, dlg2, dlb2 = _ffn_bwd(dx, sv["x2"], sv["f2"], sv["g2"], sv["u2"], m9[6:9], *sv["w2"], 0,
                                                        small["ln_g"][l, 2:3])
        g_ffn1 = _ffn_wgrad(h, dg, du, a, df, *fresh(sv["w2"]), 0)
        dxr, d_att, d_ssm, d_pool, dgate1, dlg1, dlb1, g_w_out = _mix_out_bwd(
            dx, sv["x1"], sv["ymix"], sv["y_att"], sv["y_ssm"], sv["y_pool"], m9[3:6], sv["w1"][1], 0, small["ln_g"][l, 1:2],
            fresh(sv["w1"])[1])
        do3, c3 = _att_merge_bwd(d_att, sv["y_att"], sv["lse3"])
        dop = jnp.stack([_to_residue(do3[b], d) for b, d in enumerate(DILATIONS)])
        cp = jnp.stack([_to_residue(c3[b], d) for b, d in enumerate(DILATIONS)])
        dqp, dkp, dvp, dbias = _att_bwd(sv["qp"], sv["kp"], sv["vp"], dop, sv["lsep"], cp, bias)
        dbias_tot = dbias if dbias_tot is None else dbias_tot + dbias

        def back(t3, padded):
            parts = [_from_residue(t3[b][ATT_BLOCK:] if padded else t3[b], d).astype(f32) for b, d in enumerate(DILATIONS)]
            return (parts[0] + parts[1] + parts[2]).astype(bf16)

        dq, dk, dv = back(dqp, False), back(dkp, True), back(dvp, True)
        d_seg = _to_segments(d_ssm)
        dskip = small["ssm_d"][l][None, :]
        glu_b = small["glu_b"][l][None, :]
        gsr, gsi, du_skip, dcre, dcim, dd, dglu_b, dglu_w = _ssm_out_bwd(
            d_seg, sv["y_seg"], sv["u_ssm"], sv["sr"], sv["si"], sv["cre"], sv["cim"], dskip, small["glu_w"][l], glu_b)
        lr, li = _ssm_scan(gsr, gsi, sv["a2"], True)
        du_seg, dbre, dbim = _ssm_in_bwd(lr, li, sv["u_ssm"], du_skip, sv["bre"], sv["bim"])
        da2 = _ssm_da(lr, li, sv["sr"], sv["si"])
        d_prm = ssm_ops[l]((da2, dbre, dbim, dcre, dcim))
        for k, v in zip(("ssm_a_re", "ssm_a_im", "ssm_log_dt", "ssm_b_re", "ssm_b_im", "ssm_c_re", "ssm_c_im"), d_prm):
            sg[k][l] = v
        sg["ssm_d"][l] = dd[0]
        sg["glu_b"][l] = dglu_b[0]
        sg["glu_w"][l] = dglu_w
        du_ssm = _from_segments(du_seg)
        dyp = jnp.concatenate([d_pool, jnp.zeros((POOL_HALO, D_POOL), f32)])
        du_pool, dpw, dps = _pool_bwd(dyp, sv["u_pool"], small["pool_w"][l], small["pool_scale"][l][None, :])
        sg["pool_w"][l] = dpw
        sg["pool_scale"][l] = dps[0]
        dz = jnp.stack([dq, dk, dv, jnp.concatenate([du_ssm, du_pool], axis=1).astype(bf16)])
        dx, dm1 = _mix_in_bwd(dz, dxr, sv["x1"], m9[3:6], sv["w1"][0], 0)
        g_w_in = _mix_in_wgrad(sv["h1"], dz, fresh(sv["w1"])[0], 0)
        m9 = m9 + grads_done(l, 1, list(zip(("ffn_w_gate", "ffn_w_up", "ffn_w_down"), ((2 * l + 1) * r for r in (D_MODEL, D_MODEL, FF_SHARD)),
                                            g_ffn1)) + [("w_in", l * D_MODEL, g_w_in), ("w_out", l * 256, g_w_out)])
        dm1 = jnp.concatenate([dm1[0:2], dgate1])
        dx, dg, du, a, h, df, dm0, dlg0, dlb0 = _ffn_bwd(dx, sv["x0"], sv["f0"], sv["g0"], sv["u0"], m9[0:3], *sv["w0"], 0,
                                                        small["ln_g"][l, 0:1])
        g_ffn0 = _ffn_wgrad(h, dg, du, a, df, *fresh(sv["w0"]), 0)
        order_after = grads_done(l, 0, list(zip(("ffn_w_gate", "ffn_w_up", "ffn_w_down"),
                                                (2 * l * r for r in (D_MODEL, D_MODEL, FF_SHARD)), g_ffn0)))
        dmod[l] = jnp.concatenate([dm0, dm1, dm2])
        dln_g[l] = jnp.concatenate([dlg0, dlg1, dlg2])
        dln_b[l] = jnp.concatenate([dlb0, dlb1, dlb2])

    small_grads = {k: jnp.stack(v) for k, v in sg.items()}
    small_grads["rel_bias"] = _bias_bwd(dbias_tot, table)
    small_grads["ln_g"] = jnp.stack(dln_g)
    small_grads["ln_b"] = jnp.stack(dln_b)
    return loss, dx, jnp.stack(dmod), small_grads


def _pack(arrs):
    flat = jnp.concatenate([a.reshape(-1).astype(f32) for a in arrs])
    n = flat.shape[0]
    npad = -(-n // 1024) * 1024
    return jnp.pad(flat, (0, npad - n)).reshape(npad // 128, 128)


def _unpack(buf, shapes):
    flat = buf.reshape(-1)
    out, off = [], 0
    for s in shapes:
        n = int(np.prod(s))
        out.append(flat[off:off + n].reshape(s))
        off += n
    return out


_REPL = ("rel_bias", "ada_b", "ssm_a_re", "ssm_a_im", "ssm_log_dt", "ssm_b_re", "ssm_b_im", "ssm_c_re", "ssm_c_im",
         "ssm_d", "glu_b", "pool_w", "pool_scale")
_SMALL_SHARDED = ("ln_g", "ln_b", "glu_w")
_BIG = ("ffn_w_gate", "ffn_w_up", "ffn_w_down", "w_in", "w_out")
_ORDER = ("rel_bias", "ada_w", "ada_b", "ln_g", "ln_b", "ffn_w_gate", "ffn_w_up", "ffn_w_down", "w_in", "w_out",
          "ssm_a_re", "ssm_a_im", "ssm_log_dt", "ssm_b_re", "ssm_b_im", "ssm_c_re", "ssm_c_im", "ssm_d", "glu_w",
          "glu_b", "pool_w", "pool_scale")


def kernel(x, c, rel_bias, ada_w, ada_b, ln_g, ln_b, ffn_w_gate, ffn_w_up, ffn_w_down, w_in, w_out, ssm_a_re, ssm_a_im, ssm_log_dt, ssm_b_re, ssm_b_im, ssm_c_re, ssm_c_im, ssm_d, glu_w, glu_b, pool_w, pool_scale, loss_target, m_rel_bias, m_ada_w, m_ada_b, m_ln_g, m_ln_b, m_ffn_w_gate, m_ffn_w_up, m_ffn_w_down, m_w_in, m_w_out, m_ssm_a_re, m_ssm_a_im, m_ssm_log_dt, m_ssm_b_re, m_ssm_b_im, m_ssm_c_re, m_ssm_c_im, m_ssm_d, m_glu_w, m_glu_b, m_pool_w, m_pool_scale, v_rel_bias, v_ada_w, v_ada_b, v_ln_g, v_ln_b, v_ffn_w_gate, v_ffn_w_up, v_ffn_w_down, v_w_in, v_w_out, v_ssm_a_re, v_ssm_a_im, v_ssm_log_dt, v_ssm_b_re, v_ssm_b_im, v_ssm_c_re, v_ssm_c_im, v_ssm_d, v_glu_w, v_glu_b, v_pool_w, v_pool_scale):
    args = dict(locals())
    w = {k: args[k] for k in _ORDER}
    m = {k: args["m_" + k] for k in _ORDER}
    v = {k: args["v_" + k] for k in _ORDER}
    L, D = DEPTH, D_MODEL
    ax, ay, ac = lax.axis_index("x"), lax.axis_index("y"), lax.axis_index("c")
    p_me = 2 * ax + ay
    dev = 4 * ax + 2 * ay + ac

    def halves(t):
        return t.astype(bf16).reshape(1, 2, t.shape[0] // 2, t.shape[1])

    def landing(src):
        return lax.dynamic_update_slice(lax.empty((N_CHIPS,) + src.shape, bf16), src[None], (p_me, 0, 0, 0, 0))

    chunk_keys = [("ffn", 0, 0), ("mix", 0), ("ffn", 0, 1), ("ffn", 1, 0), ("mix", 1), ("ffn", 1, 1)]
    chunk_srcs = []
    for key in chunk_keys:
        if key[0] == "ffn":
            chunk_srcs.append([halves(t[key[1], key[2]]) for t in (ffn_w_gate, ffn_w_up, ffn_w_down)])
        else:
            chunk_srcs.append([halves(w_in[key[1]]), halves(w_out[key[1]])])

    pack = _pack([c, ln_g, ln_b, glu_w])
    rows = pack.shape[0]
    allp = _allgather8(pack).reshape(8, rows, 128)
    c_all = allp[:, :8].reshape(8, D)
    by_chip = allp[0::2]

    def sharded(row0, nrows, shape, axis):
        t = by_chip[:, row0:row0 + nrows].reshape((N_CHIPS,) + shape)
        return jnp.concatenate([t[p] for p in range(N_CHIPS)], axis=axis)

    ln_g_full = sharded(8, 12, (L, 3, 256), 2)
    ln_b_full = sharded(20, 12, (L, 3, 256), 2)
    glu_w_full = sharded(32, 256, (L, 64, 256), 1)

    ncol = ada_w.shape[-1]
    ada_b_cols = lax.dynamic_slice_in_dim(ada_b, p_me * ncol, ncol, axis=1)[:, None, :]
    mod_part = _ada_fwd(c_all, ada_w, ada_b_cols)
    mrows = L * 8 * ncol // 128
    mod_all = _allgather8(mod_part.reshape(mrows, 128)).reshape(8, L, 8, ncol)
    mod_mine = lax.dynamic_index_in_dim(mod_all, dev, axis=2, keepdims=False)
    mod = jnp.concatenate([mod_mine[2 * p] for p in range(N_CHIPS)], axis=-1).reshape(L, 9, D)

    in_flight, _ = _gather_start([(srcs, [landing(t) for t in srcs]) for srcs in chunk_srcs], mod)

    def gathered(key, after):
        k = chunk_keys.index(key)
        lands = _gather_forward(_gather_wait(*in_flight[k], after, "gather_wait_%d" % k))
        return [t.reshape(N_CHIPS, 1, 2 * t.shape[3], t.shape[4]) for t in lands]

    pc = jnp.stack([p_me, ac]).astype(jnp.int32)
    groups = {}
    scattering = {}

    def start_group(tag, after=()):
        g5 = [g.reshape(g.shape[:2] + (2, g.shape[2] // 2, g.shape[3])) for _, _, g in groups[tag]]
        hsum = [_pair_sum(g, r, pc) for g, r in zip(g5, _pair_exchange(g5))]
        scattering[tag], begun = _scatter_start(hsum, "scatter_start_%s" % tag, after)
        return begun

    def grads_done(l, s, grads):
        if l == 1:
            groups.setdefault("l1", []).extend(grads)
            return start_group("l1")[0, 0] if s == 0 else jnp.zeros((), f32)
        groups["l0a" if s == 1 else "l0b"] = grads
        return start_group("l0a")[0, 0] if s == 1 else jnp.zeros((), f32)

    small = {k: w[k] for k in _REPL if k != "ada_b"}
    small.update(ln_g=ln_g_full, ln_b=ln_b_full, glu_w=glu_w_full)
    loss_dev, grad_x, dmod, sgrads = _local_step(
        x[0], loss_target[0], mod, small, lambda l, s, after: gathered(("ffn", l, s), after),
        lambda l, after: gathered(("mix", l), after), grads_done)
    loss = lax.psum(loss_dev[0, 0], ("x", "y", "c"))

    names = ("rel_bias", "ln_g", "ln_b", "ssm_a_re", "ssm_a_im", "ssm_log_dt", "ssm_b_re", "ssm_b_im", "ssm_c_re",
             "ssm_c_im", "ssm_d", "glu_w", "glu_b", "pool_w", "pool_scale")
    gpack = _pack([dmod] + [sgrads[k] for k in names])
    grows = gpack.shape[0]
    gall = _allgather8(gpack).reshape(8, grows, 128)
    l0b_begun = start_group("l0b", (gall,))
    gsum = _unpack(_sum8(gall), [(L, 9 * D)] + [sgrads[k].shape for k in names])
    red = dict(zip(("ada_b",) + names, gsum))
    red["ln_g"] = lax.dynamic_slice_in_dim(red["ln_g"], p_me * 256, 256, axis=2)
    red["ln_b"] = lax.dynamic_slice_in_dim(red["ln_b"], p_me * 256, 256, axis=2)
    red["glu_w"] = lax.dynamic_slice_in_dim(red["glu_w"], p_me * 64, 64, axis=1)

    dmod_all = gall[:, :L * 9 * D // 128].reshape(8, L, 9 * D)
    dmod_cols = jnp.transpose(lax.dynamic_slice_in_dim(dmod_all, p_me * ncol, ncol, axis=2), (1, 0, 2))
    g_ada_w = _ada_wgrad(jnp.transpose(c_all), dmod_cols)

    out_g, out_d, out_m, out_v = {}, {}, {}, {}
    r2 = (L * D, ncol)
    res = _adamw(ada_w.reshape(r2), m["ada_w"].reshape(r2), v["ada_w"].reshape(r2), g_ada_w.reshape(r2), 128)
    out_g["ada_w"], out_d["ada_w"], out_m["ada_w"], out_v["ada_w"] = [t.reshape(ada_w.shape) for t in res]

    small_names = _REPL + _SMALL_SHARDED
    wp = _pack([w[k] for k in small_names])
    res_small = _adamw(wp, _pack([m[k] for k in small_names]), _pack([v[k] for k in small_names]),
                       _pack([red[k] for k in small_names]), wp.shape[0])
    for t, dst in zip(res_small, (out_g, out_d, out_m, out_v)):
        for k, a in zip(small_names, _unpack(t, [w[k].shape for k in small_names])):
            dst[k] = a

    row_tile = dict(zip(_BIG, (256, 256, 352, 256, 256)))
    big = {name: None for name in _BIG}
    after = [grad_x, res_small[1], res[1], l0b_begun]
    for tag in ("l1", "l0a", "l0b"):
        hsum, recv = _scatter_wait(*scattering[tag], after, "scatter_wait_%s" % tag)
        full = _swap_halves([_sum_shards(h, r, pc) for h, r in zip(hsum, recv)])
        for (name, row0, _), g in zip(groups[tag], full):
            shp = w[name].shape
            r2 = (int(np.prod(shp[:-1])), shp[-1])
            big[name] = _adamw(w[name].reshape(r2), m[name].reshape(r2), v[name].reshape(r2), g.reshape(-1, shp[-1]),
                               row_tile[name], row0, big[name])
        after = [big[name][1] for name, _, _ in groups[tag]]
    for name in _BIG:
        out_g[name], out_d[name], out_m[name], out_v[name] = [t.reshape(w[name].shape) for t in big[name]]

    return (loss, grad_x[None], *[out_g[k] for k in _ORDER], *[out_d[k] for k in _ORDER],
            *[out_m[k] for k in _ORDER], *[out_v[k] for k in _ORDER])
```

```python
import functools
import math

import numpy as np
import jax
import jax.numpy as jnp
from jax import lax
from jax.experimental import pallas as pl
from jax.experimental.pallas import tpu as pltpu

f32 = jnp.float32
bf16 = jnp.bfloat16
MESH = pl.DeviceIdType.MESH

D_MODEL = 1024
SEQ = 2048
DEPTH = 2
HEAD_DIM = 64
N_HEADS = 8
D_ATT = 512
DILATIONS = (1, 4, 16)
BLOCKS_PER_RESIDUE = (16, 4, 1)
ATT_BLOCK = 128
N_UNITS = SEQ // ATT_BLOCK
N_GROUPS = 16
SSM_GROUP = 16
SSM_STATE = 64
D_SSM = 256
D_STATE = N_GROUPS * SSM_STATE
POOL_WINDOWS = (2, 4, 8, 16)
POOL_GROUP = 64
D_POOL = 256
POOL_HALO = 16
D_FF = 2816
N_BUCKETS = 32
MAX_DISTANCE = 2048
ALPHA = (2 * DEPTH) ** 0.25
FFN_RES = 0.5
LN_EPS = 1e-5
NEG = -1e30
N_CHIPS = 4
FF_SHARD = D_FF // N_CHIPS
SCAN_SEG = 8
SCAN_STEPS = SEQ // SCAN_SEG

ADAM_LR, ADAM_B1, ADAM_B2, ADAM_EPS, ADAM_WD, ADAM_STEP = 0.001, 0.9, 0.999, 1e-08, 0.01, 10

TOK_TILE = 512


def _cp(dims=None, vmem_mb=None):
    kw = {}
    if dims is not None:
        kw["dimension_semantics"] = dims
    if vmem_mb is not None:
        kw["vmem_limit_bytes"] = vmem_mb << 20
    return pltpu.CompilerParams(**kw)


def _dot(a, b):
    return jnp.dot(a, b, preferred_element_type=f32)


def _dot_nt(a, b):
    return lax.dot_general(a, b, (((1,), (1,)), ((), ())), preferred_element_type=f32)


def _dot_tn(a, b):
    return lax.dot_general(a, b, (((0,), (0,)), ((), ())), preferred_element_type=f32)


def _ln_stats(v):
    mu = jnp.mean(v, -1, keepdims=True)
    d = v - mu
    var = jnp.mean(d * d, -1, keepdims=True)
    rstd = lax.rsqrt(var + LN_EPS)
    return d * rstd, rstd


def _ln_bwd(dxh, xh, rstd):
    return rstd * (dxh - jnp.mean(dxh, -1, keepdims=True) - xh * jnp.mean(dxh * xh, -1, keepdims=True))


_GELU_C = math.sqrt(2.0 / math.pi)


def _gelu(y):
    return 0.5 * y * (1.0 + jnp.tanh(_GELU_C * (y + 0.044715 * y * y * y)))


def _gelu_grad(y):
    t = jnp.tanh(_GELU_C * (y + 0.044715 * y * y * y))
    return 0.5 * (1.0 + t) + 0.5 * y * (1.0 - t * t) * (_GELU_C * (1.0 + 3 * 0.044715 * y * y))


def _full(shape):
    return pl.BlockSpec(shape, lambda *_: (0,) * len(shape))


def _ffn_fwd(x, mod3, wg, wu, wd, ls, lng, lnb):
    S, D = x.shape
    Fs = wg.shape[-2]
    ts = TOK_TILE

    def body(x_ref, mod_ref, wg_ref, wu_ref, wd_ref, lng_ref, lnb_ref, xo_ref, f_ref, g_ref, u_ref, h_sc, acc_sc):
        j = pl.program_id(1)

        @pl.when(j == 0)
        def _():
            xh, _ = _ln_stats(x_ref[...])
            h_sc[...] = (xh * (1.0 + mod_ref[1:2, :]) + mod_ref[0:1, :]).astype(bf16)
            acc_sc[...] = jnp.zeros_like(acc_sc)

        h = h_sc[...]
        g = _dot_nt(h, wg_ref[0, 0])
        u = _dot_nt(h, wu_ref[0, 0])
        g_ref[0] = g.astype(bf16)
        u_ref[0] = u.astype(bf16)
        a = (g * jax.nn.sigmoid(g) * u).astype(bf16)
        acc_sc[...] += _dot(a, wd_ref[0, 0])

        @pl.when(j == N_CHIPS - 1)
        def _():
            f = acc_sc[...]
            f_ref[...] = f
            r = ALPHA * x_ref[...] + (FFN_RES * mod_ref[2:3, :]) * f
            rh, _ = _ln_stats(r)
            xo_ref[...] = rh * lng_ref[...] + lnb_ref[...]

    tok = pl.BlockSpec((ts, D), lambda i, j: (i, 0))
    wrow = pl.BlockSpec((1, 1, Fs, D), lambda i, j: (j, ls, 0, 0))
    hid = pl.BlockSpec((1, ts, Fs), lambda i, j: (j, i, 0))
    return pl.pallas_call(
        body, name="ffn_fwd", grid=(S // ts, N_CHIPS),
        in_specs=[tok, _full((3, D)), wrow, wrow, wrow, _full((1, D)), _full((1, D))],
        out_specs=[tok, tok, hid, hid],
        out_shape=[jax.ShapeDtypeStruct((S, D), f32), jax.ShapeDtypeStruct((S, D), f32),
                   jax.ShapeDtypeStruct((N_CHIPS, S, Fs), bf16), jax.ShapeDtypeStruct((N_CHIPS, S, Fs), bf16)],
        scratch_shapes=[pltpu.VMEM((ts, D), bf16), pltpu.VMEM((ts, D), f32)],
        compiler_params=_cp(("parallel", "arbitrary"), 56),
    )(x, mod3, wg, wu, wd, lng, lnb)


def _ffn_bwd(dxo, x, f, g, u, mod3, wg, wu, wd, ls, lng):
    S, D = x.shape
    Fs = wg.shape[-2]
    ts = TOK_TILE

    def body(dxo_ref, x_ref, f_ref, g_ref, u_ref, mod_ref, wg_ref, wu_ref, wd_ref, lng_ref,
             dx_ref, dg_ref, du_ref, a_ref, h_ref, df_ref, dmod_ref, dlng_ref, dlnb_ref,
             dr_sc, df_sc, acc_sc):
        i = pl.program_id(0)
        j = pl.program_id(1)

        @pl.when((i == 0) & (j == 0))
        def _():
            dmod_ref[...] = jnp.zeros_like(dmod_ref)
            dlng_ref[...] = jnp.zeros_like(dlng_ref)
            dlnb_ref[...] = jnp.zeros_like(dlnb_ref)

        @pl.when(j == 0)
        def _():
            xv = x_ref[...]
            fv = f_ref[...]
            gate = mod_ref[2:3, :]
            rh, rstd = _ln_stats(ALPHA * xv + (FFN_RES * gate) * fv)
            dy = dxo_ref[...]
            dlng_ref[...] += jnp.sum(dy * rh, 0, keepdims=True)
            dlnb_ref[...] += jnp.sum(dy, 0, keepdims=True)
            dr = _ln_bwd(dy * lng_ref[...], rh, rstd)
            dr_sc[...] = dr
            dmod_ref[2:3, :] += jnp.sum(FFN_RES * dr * fv, 0, keepdims=True)
            df = ((FFN_RES * gate) * dr).astype(bf16)
            df_sc[...] = df
            df_ref[...] = df
            xh, _ = _ln_stats(xv)
            h_ref[...] = (xh * (1.0 + mod_ref[1:2, :]) + mod_ref[0:1, :]).astype(bf16)
            acc_sc[...] = jnp.zeros_like(acc_sc)

        da = _dot_nt(df_sc[...], wd_ref[0, 0])
        gv = g_ref[0].astype(f32)
        uv = u_ref[0].astype(f32)
        sg = jax.nn.sigmoid(gv)
        si = gv * sg
        a_ref[0] = (si * uv).astype(bf16)
        dgv = (da * uv * (sg * (1.0 + gv * (1.0 - sg)))).astype(bf16)
        duv = (da * si).astype(bf16)
        dg_ref[0] = dgv
        du_ref[0] = duv
        acc_sc[...] += _dot(dgv, wg_ref[0, 0]) + _dot(duv, wu_ref[0, 0])

        @pl.when(j == N_CHIPS - 1)
        def _():
            dh = acc_sc[...]
            xh, rstd0 = _ln_stats(x_ref[...])
            dmod_ref[0:1, :] += jnp.sum(dh, 0, keepdims=True)
            dmod_ref[1:2, :] += jnp.sum(dh * xh, 0, keepdims=True)
            dx_ref[...] = _ln_bwd(dh * (1.0 + mod_ref[1:2, :]), xh, rstd0) + ALPHA * dr_sc[...]

    tok = pl.BlockSpec((ts, D), lambda i, j: (i, 0))
    wrow = pl.BlockSpec((1, 1, Fs, D), lambda i, j: (j, ls, 0, 0))
    hid = pl.BlockSpec((1, ts, Fs), lambda i, j: (j, i, 0))
    hid_shape = jax.ShapeDtypeStruct((N_CHIPS, S, Fs), bf16)
    return pl.pallas_call(
        body, name="ffn_bwd", grid=(S // ts, N_CHIPS),
        in_specs=[tok, tok, tok, hid, hid, _full((3, D)), wrow, wrow, wrow, _full((1, D))],
        out_specs=[tok, hid, hid, hid, tok, tok, _full((3, D)), _full((1, D)), _full((1, D))],
        out_shape=[jax.ShapeDtypeStruct((S, D), f32), hid_shape, hid_shape, hid_shape,
                   jax.ShapeDtypeStruct((S, D), bf16), jax.ShapeDtypeStruct((S, D), bf16),
                   jax.ShapeDtypeStruct((3, D), f32), jax.ShapeDtypeStruct((1, D), f32), jax.ShapeDtypeStruct((1, D), f32)],
        scratch_shapes=[pltpu.VMEM((ts, D), f32), pltpu.VMEM((ts, D), bf16), pltpu.VMEM((ts, D), f32)],
        compiler_params=_cp(("arbitrary", "arbitrary"), 56),
    )(dxo, x, f, g, u, mod3, wg, wu, wd, lng)


def _ffn_wgrad(h, dg, du, a, df, gwg, gwu, gwd, ls):
    S, D = h.shape
    Fs = dg.shape[-1]
    tk = TOK_TILE
    nk = S // tk

    def body(h_ref, dg_ref, du_ref, a_ref, df_ref, _g0, _g1, _g2, gwg_ref, gwu_ref, gwd_ref, ag_sc, au_sc, ad_sc):
        k = pl.program_id(1)

        @pl.when(k == 0)
        def _():
            ag_sc[...] = jnp.zeros_like(ag_sc)
            au_sc[...] = jnp.zeros_like(au_sc)
            ad_sc[...] = jnp.zeros_like(ad_sc)

        hv = h_ref[...]
        ag_sc[...] += _dot_tn(dg_ref[0], hv)
        au_sc[...] += _dot_tn(du_ref[0], hv)
        ad_sc[...] += _dot_tn(a_ref[0], df_ref[...])

        @pl.when(k == nk - 1)
        def _():
            gwg_ref[0, 0] = ag_sc[...].astype(bf16)
            gwu_ref[0, 0] = au_sc[...].astype(bf16)
            gwd_ref[0, 0] = ad_sc[...].astype(bf16)

    tok = pl.BlockSpec((tk, D), lambda p, k: (k, 0))
    hid = pl.BlockSpec((1, tk, Fs), lambda p, k: (p, k, 0))
    anyspec = pl.BlockSpec(memory_space=pl.ANY)
    orow = pl.BlockSpec((1, 1, Fs, D), lambda p, k: (p, ls, 0, 0))
    return pl.pallas_call(
        body, name="ffn_wgrad", grid=(N_CHIPS, nk),
        in_specs=[tok, hid, hid, hid, tok, anyspec, anyspec, anyspec],
        out_specs=[orow, orow, orow],
        out_shape=[jax.ShapeDtypeStruct(gwg.shape, bf16), jax.ShapeDtypeStruct(gwu.shape, bf16),
                   jax.ShapeDtypeStruct(gwd.shape, bf16)],
        scratch_shapes=[pltpu.VMEM((Fs, D), f32), pltpu.VMEM((Fs, D), f32), pltpu.VMEM((Fs, D), f32)],
        input_output_aliases={5: 0, 6: 1, 7: 2},
        compiler_params=_cp(("parallel", "arbitrary"), 48),
    )(h, dg, du, a, df, gwg, gwu, gwd)


_LANES = 128
_QKV_BLOCKS = D_ATT // _LANES


def _res_spec(lead, d, width, index):
    return pl.BlockSpec((lead, d, TOK_TILE // d, width), index)


def _res_spec3(d, width):
    return pl.BlockSpec((d, TOK_TILE // d, width), lambda i: (0, i, 0))


def _rows_to_residues(tile_bufs, d, put):
    for r in range(d):
        for cb, buf in enumerate(tile_bufs):
            put(r, cb, buf[pl.ds(r, TOK_TILE // d, stride=d), :])


def _residues_to_rows(tile_bufs, d, get):
    for r in range(d):
        for cb, buf in enumerate(tile_bufs):
            buf[pl.ds(r, TOK_TILE // d, stride=d), :] = get(r, cb)


def _mix_in_fwd(x, mod3, w_in, l):
    S, D = x.shape
    N = w_in.shape[-1]
    ts = TOK_TILE

    def body(x_ref, mod_ref, w_ref, o1_ref, o4_ref, o16_ref, zr_ref, h_ref, *bufs):
        j = pl.program_id(1)

        @pl.when(j == 0)
        def _():
            xh, _ = _ln_stats(x_ref[...])
            h_ref[...] = (xh * (1.0 + mod_ref[1:2, :]) + mod_ref[0:1, :]).astype(bf16)

        z = _dot(h_ref[...], w_ref[0, 0])

        @pl.when(j == N_CHIPS - 1)
        def _():
            zr_ref[...] = z

        @pl.when(j < N_CHIPS - 1)
        def _():
            zz = z * jnp.where(j == 0, HEAD_DIM ** -0.5, 1.0)
            o1_ref[j, 0] = zz.astype(bf16)
            for cb, buf in enumerate(bufs):
                buf[...] = zz[:, _LANES * cb:_LANES * (cb + 1)]
            for d, o_ref in zip(DILATIONS[1:], (o4_ref, o16_ref)):
                def put(r, cb, piece, o_ref=o_ref):
                    o_ref[j, r, :, _LANES * cb:_LANES * (cb + 1)] = piece.astype(bf16)
                _rows_to_residues(bufs, d, put)

    tok = pl.BlockSpec((ts, D), lambda i, j: (i, 0))
    res = [_res_spec(3, d, N, lambda i, j: (0, 0, i, 0)) for d in DILATIONS]
    return pl.pallas_call(
        body, name="mix_in_fwd", grid=(S // ts, N_CHIPS),
        in_specs=[tok, _full((3, D)), pl.BlockSpec((1, 1, D, N), lambda i, j: (j, l, 0, 0))],
        out_specs=res + [pl.BlockSpec((ts, N), lambda i, j: (i, 0)), tok],
        out_shape=[jax.ShapeDtypeStruct((3, d, S // d, N), bf16) for d in DILATIONS]
        + [jax.ShapeDtypeStruct((S, N), f32), jax.ShapeDtypeStruct((S, D), bf16)],
        scratch_shapes=[pltpu.VMEM((ts, _LANES), f32)] * _QKV_BLOCKS,
        compiler_params=_cp(("parallel", "arbitrary"), 40),
    )(x, mod3, w_in)


def _mix_in_bwd(dqkv, d_rest, dx_res, x, mod3, w_in, l):
    S, D = x.shape
    N = w_in.shape[-1]
    ts = TOK_TILE

    def body(d1_ref, d4_ref, d16_ref, dr_ref, dxr_ref, x_ref, mod_ref, w_ref, dx_ref, dmod_ref, dz_ref, acc_sc, *bufs):
        i = pl.program_id(0)
        j = pl.program_id(1)

        @pl.when((i == 0) & (j == 0))
        def _():
            dmod_ref[...] = jnp.zeros_like(dmod_ref)

        @pl.when(j == 0)
        def _():
            acc_sc[...] = jnp.zeros_like(acc_sc)

        @pl.when(j == N_CHIPS - 1)
        def _():
            dz_ref[0] = dr_ref[...]

        @pl.when(j < N_CHIPS - 1)
        def _():
            for d, d_ref, tile_bufs in ((4, d4_ref, bufs[:_QKV_BLOCKS]), (16, d16_ref, bufs[_QKV_BLOCKS:])):
                _residues_to_rows(tile_bufs, d, lambda r, cb, d_ref=d_ref: d_ref[0, r, :, _LANES * cb:_LANES * (cb + 1)].astype(f32))
            for cb in range(_QKV_BLOCKS):
                cols = slice(_LANES * cb, _LANES * (cb + 1))
                dz_ref[0, :, cols] = (d1_ref[0, 0, :, cols].astype(f32) + bufs[cb][...] + bufs[_QKV_BLOCKS + cb][...]).astype(bf16)

        acc_sc[...] += _dot_nt(dz_ref[0], w_ref[0, 0])

        @pl.when(j == N_CHIPS - 1)
        def _():
            dh = acc_sc[...]
            xh, rstd0 = _ln_stats(x_ref[...])
            dmod_ref[0:1, :] += jnp.sum(dh, 0, keepdims=True)
            dmod_ref[1:2, :] += jnp.sum(dh * xh, 0, keepdims=True)
            dx_ref[...] = _ln_bwd(dh * (1.0 + mod_ref[1:2, :]), xh, rstd0) + dxr_ref[...]

    tok = pl.BlockSpec((ts, D), lambda i, j: (i, 0))
    res = [_res_spec(1, d, N, lambda i, j: (jnp.minimum(j, 2), 0, i, 0)) for d in DILATIONS]
    return pl.pallas_call(
        body, name="mix_in_bwd", grid=(S // ts, N_CHIPS),
        in_specs=res + [pl.BlockSpec((ts, N), lambda i, j: (i, 0)), tok, tok, _full((3, D)),
                        pl.BlockSpec((1, 1, D, N), lambda i, j: (j, l, 0, 0))],
        out_specs=[tok, _full((3, D)), pl.BlockSpec((1, ts, N), lambda i, j: (j, i, 0))],
        out_shape=[jax.ShapeDtypeStruct((S, D), f32), jax.ShapeDtypeStruct((3, D), f32),
                   jax.ShapeDtypeStruct((N_CHIPS, S, N), bf16)],
        scratch_shapes=[pltpu.VMEM((ts, D), f32)] + [pltpu.VMEM((ts, _LANES), f32)] * (2 * _QKV_BLOCKS),
        compiler_params=_cp(("arbitrary", "arbitrary"), 40),
    )(*dqkv, d_rest, dx_res, x, mod3, w_in)


def _mix_in_wgrad(h, dz, gw, l):
    S, D = h.shape
    N = dz.shape[-1]
    tk = TOK_TILE
    nk = S // tk

    def body(h_ref, dz_ref, _g, gw_ref, acc_sc):
        k = pl.program_id(1)

        @pl.when(k == 0)
        def _():
            acc_sc[...] = jnp.zeros_like(acc_sc)

        acc_sc[...] += _dot_tn(h_ref[...], dz_ref[0])

        @pl.when(k == nk - 1)
        def _():
            gw_ref[0, 0] = acc_sc[...].astype(bf16)

    return pl.pallas_call(
        body, name="mix_in_wgrad", grid=(N_CHIPS, nk),
        in_specs=[pl.BlockSpec((tk, D), lambda p, k: (k, 0)), pl.BlockSpec((1, tk, N), lambda p, k: (p, k, 0)),
                  pl.BlockSpec(memory_space=pl.ANY)],
        out_specs=pl.BlockSpec((1, 1, D, N), lambda p, k: (p, l, 0, 0)),
        out_shape=jax.ShapeDtypeStruct(gw.shape, bf16),
        scratch_shapes=[pltpu.VMEM((D, N), f32)],
        input_output_aliases={2: 0},
        compiler_params=_cp(("parallel", "arbitrary"), 40),
    )(h, dz, gw)


def _mix_out_fwd(x, y_att, y_ssm, y_pool, mod3, w_out, l, lng, lnb):
    S, D = x.shape
    ts = TOK_TILE

    def body(x_ref, ya_ref, ys_ref, yp_ref, mod_ref, w_ref, lng_ref, lnb_ref, xo_ref, y_ref):
        ya = ya_ref[...].astype(bf16)
        y = (_dot(ya[:, 0:256], w_ref[0, 0]) + _dot(ya[:, 256:512], w_ref[1, 0])
             + _dot(ys_ref[...].astype(bf16), w_ref[2, 0]) + _dot(yp_ref[...].astype(bf16), w_ref[3, 0]))
        y_ref[...] = y
        rh, _ = _ln_stats(ALPHA * x_ref[...] + mod_ref[2:3, :] * y)
        xo_ref[...] = rh * lng_ref[...] + lnb_ref[...]

    tok = pl.BlockSpec((ts, D), lambda i: (i, 0))
    return pl.pallas_call(
        body, name="mix_out_fwd", grid=(S // ts,),
        in_specs=[tok, pl.BlockSpec((ts, D_ATT), lambda i: (i, 0)), pl.BlockSpec((ts, D_SSM), lambda i: (i, 0)),
                  pl.BlockSpec((ts, D_POOL), lambda i: (i, 0)), _full((3, D)),
                  pl.BlockSpec((N_CHIPS, 1, 256, D), lambda i: (0, l, 0, 0)), _full((1, D)), _full((1, D))],
        out_specs=[tok, tok],
        out_shape=[jax.ShapeDtypeStruct((S, D), f32), jax.ShapeDtypeStruct((S, D), f32)],
        compiler_params=_cp(("parallel",), 40),
    )(x, y_att, y_ssm, y_pool, mod3, w_out, lng, lnb)


def _mix_out_bwd(dxo, x, y, y_att, y_ssm, y_pool, mod3, w_out, l, lng, gw_out):
    S, D = x.shape
    ts = TOK_TILE
    nt = S // ts

    def body(dxo_ref, x_ref, y_ref, ya_ref, ys_ref, yp_ref, mod_ref, w_ref, lng_ref, _g,
             dxr_ref, da_ref, ds_ref, dp_ref, dgate_ref, dlng_ref, dlnb_ref, gw_ref, acc_sc):
        i = pl.program_id(0)

        @pl.when(i == 0)
        def _():
            dgate_ref[...] = jnp.zeros_like(dgate_ref)
            dlng_ref[...] = jnp.zeros_like(dlng_ref)
            dlnb_ref[...] = jnp.zeros_like(dlnb_ref)
            acc_sc[...] = jnp.zeros_like(acc_sc)

        gate = mod_ref[2:3, :]
        yv = y_ref[...]
        rh, rstd = _ln_stats(ALPHA * x_ref[...] + gate * yv)
        dy_out = dxo_ref[...]
        dlng_ref[...] += jnp.sum(dy_out * rh, 0, keepdims=True)
        dlnb_ref[...] += jnp.sum(dy_out, 0, keepdims=True)
        dr = _ln_bwd(dy_out * lng_ref[...], rh, rstd)
        dxr_ref[...] = ALPHA * dr
        dgate_ref[...] += jnp.sum(dr * yv, 0, keepdims=True)
        dy = (gate * dr).astype(bf16)
        da_ref[:, 0:256] = _dot_nt(dy, w_ref[0, 0])
        da_ref[:, 256:512] = _dot_nt(dy, w_ref[1, 0])
        ds_ref[...] = _dot_nt(dy, w_ref[2, 0])
        dp_ref[...] = _dot_nt(dy, w_ref[3, 0])
        ya = ya_ref[...].astype(bf16)
        acc_sc[0] += _dot_tn(ya[:, 0:256], dy)
        acc_sc[1] += _dot_tn(ya[:, 256:512], dy)
        acc_sc[2] += _dot_tn(ys_ref[...].astype(bf16), dy)
        acc_sc[3] += _dot_tn(yp_ref[...].astype(bf16), dy)

        @pl.when(i == nt - 1)
        def _():
            gw_ref[:, 0] = acc_sc[...].astype(bf16)

    tok = pl.BlockSpec((ts, D), lambda i: (i, 0))
    t512 = pl.BlockSpec((ts, D_ATT), lambda i: (i, 0))
    t256 = pl.BlockSpec((ts, 256), lambda i: (i, 0))
    wspec = pl.BlockSpec((N_CHIPS, 1, 256, D), lambda i: (0, l, 0, 0))
    return pl.pallas_call(
        body, name="mix_out_bwd", grid=(nt,),
        in_specs=[tok, tok, tok, t512, t256, t256, _full((3, D)), wspec, _full((1, D)), pl.BlockSpec(memory_space=pl.ANY)],
        out_specs=[tok, t512, t256, t256, _full((1, D)), _full((1, D)), _full((1, D)), wspec],
        out_shape=[jax.ShapeDtypeStruct((S, D), f32), jax.ShapeDtypeStruct((S, D_ATT), f32),
                   jax.ShapeDtypeStruct((S, D_SSM), f32), jax.ShapeDtypeStruct((S, D_POOL), f32),
                   jax.ShapeDtypeStruct((1, D), f32), jax.ShapeDtypeStruct((1, D), f32), jax.ShapeDtypeStruct((1, D), f32),
                   jax.ShapeDtypeStruct(gw_out.shape, bf16)],
        scratch_shapes=[pltpu.VMEM((N_CHIPS, 256, D), f32)],
        input_output_aliases={9: 7},
        compiler_params=_cp(("arbitrary",), 48),
    )(dxo, x, y, y_att, y_ssm, y_pool, mod3, w_out, lng, gw_out)


def _t5_bucket(dist):
    max_exact = N_BUCKETS // 2
    d = np.maximum(dist, 1).astype(np.float32)
    large = max_exact + (np.log(d / max_exact) / math.log(MAX_DISTANCE / max_exact)
                         * (N_BUCKETS - max_exact)).astype(np.int32)
    large = np.minimum(large, N_BUCKETS - 1)
    return np.where(dist < max_exact, dist, large).astype(np.int32)


def _bucket_table():
    q = ATT_BLOCK
    i = np.arange(q)[:, None]
    j = np.arange(2 * q)[None, :]
    r = i + q - j
    in_band = (r >= 0) & (r <= q)
    tabs = [np.where(in_band, _t5_bucket(np.clip(r, 0, None) * d), -1) for d in DILATIONS]
    return np.stack(tabs).astype(np.int32)


def _bias_fwd(rel_bias, table):
    def body(rb_ref, tab_ref, out_ref):
        for b in range(3):
            tb = tab_ref[b]
            for h in range(N_HEADS):
                def pick(k, acc):
                    return jnp.where(tb == k, rb_ref[k, h], acc)
                out_ref[b, h] = lax.fori_loop(0, N_BUCKETS, pick, jnp.where(tb < 0, NEG, 0.0).astype(f32))

    return pl.pallas_call(
        body, name="bias_fwd",
        in_specs=[pl.BlockSpec(memory_space=pltpu.SMEM), pl.BlockSpec(memory_space=pltpu.VMEM)],
        out_specs=pl.BlockSpec(memory_space=pltpu.VMEM),
        out_shape=jax.ShapeDtypeStruct((3, N_HEADS, ATT_BLOCK, 2 * ATT_BLOCK), f32),
    )(rel_bias, table)


def _bias_bwd(dbias, table):
    def body(db_ref, tab_ref, out_ref):
        def per_bucket(k, c):
            for h in range(N_HEADS):
                tot = jnp.zeros((), f32)
                for b in range(3):
                    tot = tot + jnp.sum(jnp.where(tab_ref[b] == k, db_ref[b, h], 0.0))
                out_ref[k, h] = tot
            return c
        lax.fori_loop(0, N_BUCKETS, per_bucket, 0)

    return pl.pallas_call(
        body, name="bias_bwd",
        in_specs=[pl.BlockSpec(memory_space=pltpu.VMEM), pl.BlockSpec(memory_space=pltpu.VMEM)],
        out_specs=pl.BlockSpec(memory_space=pltpu.SMEM),
        out_shape=jax.ShapeDtypeStruct((N_BUCKETS, N_HEADS), f32),
    )(dbias, table)


def _att_unit(u, nbr):
    rows = pl.ds(pl.multiple_of(u * ATT_BLOCK, ATT_BLOCK), ATT_BLOCK)
    prev = pl.ds(pl.multiple_of(jnp.maximum(u - 1, 0) * ATT_BLOCK, ATT_BLOCK), ATT_BLOCK)
    return rows, prev, (u % nbr) != 0


def _att_scores(q, k_ref, b_ref, h, rows, prev, valid_prev, nbr):
    cs = pl.ds(HEAD_DIM * h, HEAD_DIM)
    s_cur = _dot_nt(q, k_ref[0, rows, cs]) + b_ref[0, h, :, ATT_BLOCK:]
    if nbr == 1:
        return s_cur, None
    s_prev = _dot_nt(q, k_ref[0, prev, cs]) + b_ref[0, h, :, :ATT_BLOCK]
    return s_cur, jnp.where(valid_prev, s_prev, NEG)


def _qkv_specs(S, branch):
    return ([pl.BlockSpec((1, S, D_ATT), lambda i, t=t: (t, 0, 0)) for t in range(3)],
            pl.BlockSpec((1, N_HEADS, ATT_BLOCK, 2 * ATT_BLOCK), lambda i: (branch, 0, 0, 0)))


def _att_fwd(qkv, bias, branch):
    S = qkv.shape[1]
    nbr = BLOCKS_PER_RESIDUE[branch]

    def body(q_ref, k_ref, v_ref, b_ref, o_ref, lse_ref):
        lse_ref[...] = jnp.zeros_like(lse_ref)

        def unit(u, c):
            rows, prev, valid_prev = _att_unit(u, nbr)
            for h in range(N_HEADS):
                cs = pl.ds(HEAD_DIM * h, HEAD_DIM)
                s_cur, s_prev = _att_scores(q_ref[0, rows, cs], k_ref, b_ref, h, rows, prev, valid_prev, nbr)
                m = jnp.max(s_cur, -1, keepdims=True)
                if s_prev is not None:
                    m = jnp.maximum(m, jnp.max(s_prev, -1, keepdims=True))
                p_cur = jnp.exp(s_cur - m)
                den = jnp.sum(p_cur, -1, keepdims=True)
                o = _dot(p_cur.astype(bf16), v_ref[0, rows, cs])
                if s_prev is not None:
                    p_prev = jnp.exp(s_prev - m)
                    den = den + jnp.sum(p_prev, -1, keepdims=True)
                    o = o + _dot(p_prev.astype(bf16), v_ref[0, prev, cs])
                o_ref[rows, cs] = o / den
                lse_ref[rows, pl.ds(h, 1)] = m + jnp.log(den)
            return c

        lax.fori_loop(0, N_UNITS, unit, 0)

    qkv_specs, bspec = _qkv_specs(S, branch)
    return pl.pallas_call(
        body, name="att_fwd", grid=(1,),
        in_specs=qkv_specs + [bspec],
        out_specs=[pl.BlockSpec((S, D_ATT), lambda i: (0, 0)), pl.BlockSpec((S, _LANES), lambda i: (0, 0))],
        out_shape=[jax.ShapeDtypeStruct((S, D_ATT), f32), jax.ShapeDtypeStruct((S, _LANES), f32)],
        compiler_params=_cp(("arbitrary",), 40),
    )(qkv, qkv, qkv, bias)


def _att_bwd(qkv, do, lse, crow, bias, branch):
    S = qkv.shape[1]
    nbr = BLOCKS_PER_RESIDUE[branch]

    def body(q_ref, k_ref, v_ref, do_ref, lse_ref, c_ref, b_ref, dqkv_ref, db_ref, dk_sc, dv_sc):
        dk_sc[...] = jnp.zeros_like(dk_sc)
        dv_sc[...] = jnp.zeros_like(dv_sc)
        db_ref[...] = jnp.zeros_like(db_ref)

        def unit(u, c):
            rows, prev, valid_prev = _att_unit(u, nbr)
            for h in range(N_HEADS):
                cs = pl.ds(HEAD_DIM * h, HEAD_DIM)
                q = q_ref[0, rows, cs]
                dov = do_ref[rows, cs]
                lse_h = lse_ref[rows, pl.ds(h, 1)]
                c_h = c_ref[rows, pl.ds(h, 1)]
                scores = _att_scores(q, k_ref, b_ref, h, rows, prev, valid_prev, nbr)
                dq = None
                for s, keys, lanes in zip(scores, (rows, prev), (slice(ATT_BLOCK, 2 * ATT_BLOCK), slice(0, ATT_BLOCK))):
                    if s is None:
                        continue
                    p = jnp.exp(s - lse_h)
                    ds = p * (_dot_nt(dov, v_ref[0, keys, cs]) - c_h)
                    db_ref[h, :, lanes] += ds
                    dsb = ds.astype(bf16)
                    part = _dot(dsb, k_ref[0, keys, cs])
                    dq = part if dq is None else dq + part
                    dk_sc[keys, cs] += _dot_tn(dsb, q)
                    dv_sc[keys, cs] += _dot_tn(p.astype(bf16), dov)
                dqkv_ref[0, rows, cs] = (HEAD_DIM ** -0.5 * dq).astype(bf16)
            return c

        lax.fori_loop(0, N_UNITS, unit, 0)
        dqkv_ref[1] = dk_sc[...].astype(bf16)
        dqkv_ref[2] = dv_sc[...].astype(bf16)

    qkv_specs, bspec = _qkv_specs(S, branch)
    row = pl.BlockSpec((S, _LANES), lambda i: (0, 0))
    return pl.pallas_call(
        body, name="att_bwd", grid=(1,),
        in_specs=qkv_specs + [pl.BlockSpec((S, D_ATT), lambda i: (0, 0)), row, row, bspec],
        out_specs=[pl.BlockSpec((3, S, D_ATT), lambda i: (0, 0, 0)),
                   pl.BlockSpec((N_HEADS, ATT_BLOCK, 2 * ATT_BLOCK), lambda i: (0, 0, 0))],
        out_shape=[jax.ShapeDtypeStruct((3, S, D_ATT), bf16), jax.ShapeDtypeStruct((N_HEADS, ATT_BLOCK, 2 * ATT_BLOCK), f32)],
        scratch_shapes=[pltpu.VMEM((S, D_ATT), f32), pltpu.VMEM((S, D_ATT), f32)],
        compiler_params=_cp(("arbitrary",), 48),
    )(qkv, qkv, qkv, do, lse, crow, bias)


def _branch_weights(lse_ref):
    l0, l1, l2 = lse_ref[0], lse_ref[1], lse_ref[2]
    m = jnp.maximum(jnp.maximum(l0, l1), l2)
    e0, e1, e2 = jnp.exp(l0 - m), jnp.exp(l1 - m), jnp.exp(l2 - m)
    tot = e0 + e1 + e2
    return e0 / tot, e1 / tot, e2 / tot


def _att_merge(os, lses):
    S = os[0].shape[0] * os[0].shape[1]
    ts = TOK_TILE

    def body(o1_ref, o4_ref, o16_ref, l1_ref, l4_ref, l16_ref, y_ref, lt_ref, *bufs):
        obufs = (bufs[:_QKV_BLOCKS], bufs[_QKV_BLOCKS:2 * _QKV_BLOCKS])
        lt_ref[0] = l1_ref[0]
        for k, (d, o_ref, l_ref) in enumerate(((4, o4_ref, l4_ref), (16, o16_ref, l16_ref))):
            _residues_to_rows(obufs[k], d, lambda r, cb, o_ref=o_ref: o_ref[r, :, _LANES * cb:_LANES * (cb + 1)])
            _residues_to_rows([bufs[2 * _QKV_BLOCKS + k]], d, lambda r, cb, l_ref=l_ref: l_ref[r])
            lt_ref[1 + k] = bufs[2 * _QKV_BLOCKS + k][...]
        w = _branch_weights(lt_ref)
        for h in range(N_HEADS):
            cs = slice(HEAD_DIM * h, HEAD_DIM * (h + 1))
            half = slice(HEAD_DIM * (h % 2), HEAD_DIM * (h % 2 + 1))
            y_ref[:, cs] = (w[0][:, h:h + 1] * o1_ref[0, :, cs] + w[1][:, h:h + 1] * obufs[0][h // 2][:, half]
                            + w[2][:, h:h + 1] * obufs[1][h // 2][:, half])

    return pl.pallas_call(
        body, name="att_merge", grid=(S // ts,),
        in_specs=[_res_spec3(d, D_ATT) for d in DILATIONS] + [_res_spec3(d, _LANES) for d in DILATIONS],
        out_specs=[pl.BlockSpec((ts, D_ATT), lambda i: (i, 0)), pl.BlockSpec((3, ts, _LANES), lambda i: (0, i, 0))],
        out_shape=[jax.ShapeDtypeStruct((S, D_ATT), f32), jax.ShapeDtypeStruct((3, S, _LANES), f32)],
        scratch_shapes=[pltpu.VMEM((ts, _LANES), f32)] * (2 * _QKV_BLOCKS + 2),
        compiler_params=_cp(("parallel",)),
    )(*os, *lses)


def _att_merge_bwd(dy, y, lse3):
    S = dy.shape[0]
    ts = TOK_TILE

    def body(dy_ref, y_ref, lse_ref, do1_ref, do4_ref, do16_ref, c1_ref, c4_ref, c16_ref, *bufs):
        dobufs = (bufs[:_QKV_BLOCKS], bufs[_QKV_BLOCKS:2 * _QKV_BLOCKS], bufs[2 * _QKV_BLOCKS:3 * _QKV_BLOCKS])
        cbufs = bufs[3 * _QKV_BLOCKS:]
        w = _branch_weights(lse_ref)
        for cb in cbufs:
            cb[...] = jnp.zeros_like(cb)
        for h in range(N_HEADS):
            cs = slice(HEAD_DIM * h, HEAD_DIM * (h + 1))
            half = slice(HEAD_DIM * (h % 2), HEAD_DIM * (h % 2 + 1))
            dyh = dy_ref[:, cs]
            t = jnp.sum(dyh * y_ref[:, cs], -1, keepdims=True)
            for p in range(3):
                wp = w[p][:, h:h + 1]
                dobufs[p][h // 2][:, half] = wp * dyh
                cbufs[p][:, h:h + 1] = wp * t
        for cb in range(_QKV_BLOCKS):
            do1_ref[0, :, _LANES * cb:_LANES * (cb + 1)] = dobufs[0][cb][...].astype(bf16)
        c1_ref[0] = cbufs[0][...]
        for k, (d, do_ref, c_ref) in enumerate(((4, do4_ref, c4_ref), (16, do16_ref, c16_ref))):
            def put_do(r, cb, piece, do_ref=do_ref):
                do_ref[r, :, _LANES * cb:_LANES * (cb + 1)] = piece.astype(bf16)

            def put_c(r, cb, piece, c_ref=c_ref):
                c_ref[r] = piece

            _rows_to_residues(dobufs[1 + k], d, put_do)
            _rows_to_residues([cbufs[1 + k]], d, put_c)

    return pl.pallas_call(
        body, name="att_merge_bwd", grid=(S // ts,),
        in_specs=[pl.BlockSpec((ts, D_ATT), lambda i: (i, 0)), pl.BlockSpec((ts, D_ATT), lambda i: (i, 0)),
                  pl.BlockSpec((3, ts, _LANES), lambda i: (0, i, 0))],
        out_specs=[_res_spec3(d, D_ATT) for d in DILATIONS] + [_res_spec3(d, _LANES) for d in DILATIONS],
        out_shape=[jax.ShapeDtypeStruct((d, S // d, D_ATT), bf16) for d in DILATIONS]
        + [jax.ShapeDtypeStruct((d, S // d, _LANES), f32) for d in DILATIONS],
        scratch_shapes=[pltpu.VMEM((ts, _LANES), f32)] * (3 * _QKV_BLOCKS + 3),
        compiler_params=_cp(("parallel",)),
    )(dy, y, lse3)


def _ssm_scan(xr, xi, a2, reverse):
    S, N = xr.shape
    nst = S // SCAN_SEG

    def body(xr_ref, xi_ref, a_ref, sr_ref, si_ref):
        ar = jnp.broadcast_to(a_ref[0:1, :], (SCAN_SEG, N))
        ai = jnp.broadcast_to(a_ref[1:2, :], (SCAN_SEG, N))
        if reverse:
            ai = -ai
        row = lax.broadcasted_iota(jnp.int32, (SCAN_SEG, N), 0)
        zero = jnp.zeros((SCAN_SEG, N), f32)

        def tile(t):
            return pl.ds(pl.multiple_of((nst - 1 - t if reverse else t) * SCAN_SEG, SCAN_SEG), SCAN_SEG)

        def local(t, c):
            sr, si, pr, pi = c
            rows = tile(t)
            nsr = ar * sr - ai * si + xr_ref[rows, :]
            nsi = ar * si + ai * sr + xi_ref[rows, :]
            sr_ref[rows, :] = nsr
            si_ref[rows, :] = nsi
            return nsr, nsi, ar * pr - ai * pi, ar * pi + ai * pr

        fr, fi, apr, api = lax.fori_loop(0, nst, local, (zero, zero, zero + 1.0, zero))

        def shift(v):
            if reverse:
                return jnp.where(row == SCAN_SEG - 1, 0.0, pltpu.roll(v, SCAN_SEG - 1, axis=0))
            return jnp.where(row == 0, 0.0, pltpu.roll(v, 1, axis=0))

        cr, ci = zero, zero
        for _ in range(SCAN_SEG - 1):
            cr, ci = shift(fr + apr * cr - api * ci), shift(fi + apr * ci + api * cr)

        def fix(t, c):
            pr, pi = c
            npr, npi = ar * pr - ai * pi, ar * pi + ai * pr
            rows = tile(t)
            sr_ref[rows, :] += npr * cr - npi * ci
            si_ref[rows, :] += npr * ci + npi * cr
            return npr, npi

        lax.fori_loop(0, nst, fix, (zero + 1.0, zero))

    vm = pl.BlockSpec(memory_space=pltpu.VMEM)
    return pl.pallas_call(
        body, name="ssm_scan_rev" if reverse else "ssm_scan",
        in_specs=[vm, vm, vm], out_specs=[vm, vm],
        out_shape=[jax.ShapeDtypeStruct((S, N), f32), jax.ShapeDtypeStruct((S, N), f32)],
        compiler_params=_cp(None, 48),
    )(xr, xi, a2)


def _ssm_in(u, bre, bim):
    S = u.shape[0]
    ts = TOK_TILE

    def body(u_ref, br_ref, bi_ref, or_ref, oi_ref):
        ub = u_ref[...].astype(bf16)
        or_ref[...] = _dot(ub, br_ref[...].astype(bf16))
        oi_ref[...] = _dot(ub, bi_ref[...].astype(bf16))

    return pl.pallas_call(
        body, name="ssm_in", grid=(S // ts,),
        in_specs=[pl.BlockSpec((ts, D_SSM), lambda i: (i, 0)), _full((D_SSM, D_STATE)), _full((D_SSM, D_STATE))],
        out_specs=[pl.BlockSpec((ts, D_STATE), lambda i: (i, 0))] * 2,
        out_shape=[jax.ShapeDtypeStruct((S, D_STATE), f32)] * 2,
        compiler_params=_cp(("parallel",)),
    )(u, bre, bim)


def _ssm_out(sr, si, u, cre, cim, dskip, glu_w, glu_b):
    S = u.shape[0]
    ts = TOK_TILE

    def body(sr_ref, si_ref, u_ref, cr_ref, ci_ref, d_ref, w_ref, b_ref, out_ref, y_ref):
        y = (_dot(sr_ref[...].astype(bf16), cr_ref[...].astype(bf16))
             - _dot(si_ref[...].astype(bf16), ci_ref[...].astype(bf16)) + d_ref[...] * u_ref[...])
        y_ref[...] = y
        z = _dot(_gelu(y).astype(bf16), w_ref[...].astype(bf16)) + b_ref[...]
        out_ref[...] = y * jax.nn.sigmoid(z)

    st = pl.BlockSpec((ts, D_STATE), lambda i: (i, 0))
    ch = pl.BlockSpec((ts, D_SSM), lambda i: (i, 0))
    return pl.pallas_call(
        body, name="ssm_out", grid=(S // ts,),
        in_specs=[st, st, ch, _full((D_STATE, D_SSM)), _full((D_STATE, D_SSM)), _full((1, D_SSM)),
                  _full((D_SSM, D_SSM)), _full((1, D_SSM))],
        out_specs=[ch, ch],
        out_shape=[jax.ShapeDtypeStruct((S, D_SSM), f32)] * 2,
        compiler_params=_cp(("parallel",)),
    )(sr, si, u, cre, cim, dskip, glu_w, glu_b)


def _ssm_out_bwd(dout, y, u, sr, si, cre, cim, dskip, glu_w, glu_b):
    S = u.shape[0]
    ts = TOK_TILE

    def body(do_ref, y_ref, u_ref, sr_ref, si_ref, cr_ref, ci_ref, d_ref, w_ref, b_ref,
             gr_ref, gi_ref, du_ref, dcr_ref, dci_ref, dd_ref, dgb_ref, dgw_ref):
        @pl.when(pl.program_id(0) == 0)
        def _():
            for r in (dcr_ref, dci_ref, dd_ref, dgb_ref, dgw_ref):
                r[...] = jnp.zeros_like(r)

        y = y_ref[...]
        dout = do_ref[...]
        wb = w_ref[...].astype(bf16)
        ge = _gelu(y).astype(bf16)
        sz = jax.nn.sigmoid(_dot(ge, wb) + b_ref[...])
        dz = dout * y * sz * (1.0 - sz)
        dzb = dz.astype(bf16)
        dgb_ref[...] += jnp.sum(dz, 0, keepdims=True)
        dgw_ref[...] += _dot_tn(ge, dzb)
        dy = dout * sz + _gelu_grad(y) * _dot_nt(dzb, wb)
        uv = u_ref[...]
        dd_ref[...] += jnp.sum(dy * uv, 0, keepdims=True)
        du_ref[...] = dy * d_ref[...]
        dyb = dy.astype(bf16)
        gr_ref[...] = _dot_nt(dyb, cr_ref[...].astype(bf16))
        gi_ref[...] = -_dot_nt(dyb, ci_ref[...].astype(bf16))
        dcr_ref[...] += _dot_tn(sr_ref[...].astype(bf16), dyb)
        dci_ref[...] -= _dot_tn(si_ref[...].astype(bf16), dyb)

    st = pl.BlockSpec((ts, D_STATE), lambda i: (i, 0))
    ch = pl.BlockSpec((ts, D_SSM), lambda i: (i, 0))
    c_full = _full((D_STATE, D_SSM))
    return pl.pallas_call(
        body, name="ssm_out_bwd", grid=(S // ts,),
        in_specs=[ch, ch, ch, st, st, c_full, c_full, _full((1, D_SSM)), _full((D_SSM, D_SSM)), _full((1, D_SSM))],
        out_specs=[st, st, ch, c_full, c_full, _full((1, D_SSM)), _full((1, D_SSM)), _full((D_SSM, D_SSM))],
        out_shape=[jax.ShapeDtypeStruct((S, D_STATE), f32), jax.ShapeDtypeStruct((S, D_STATE), f32),
                   jax.ShapeDtypeStruct((S, D_SSM), f32), jax.ShapeDtypeStruct((D_STATE, D_SSM), f32),
                   jax.ShapeDtypeStruct((D_STATE, D_SSM), f32), jax.ShapeDtypeStruct((1, D_SSM), f32),
                   jax.ShapeDtypeStruct((1, D_SSM), f32), jax.ShapeDtypeStruct((D_SSM, D_SSM), f32)],
        compiler_params=_cp(("arbitrary",), 40),
    )(dout, y, u, sr, si, cre, cim, dskip, glu_w, glu_b)


def _ssm_in_bwd(lr, li, u, du_skip, bre, bim):
    S = u.shape[0]
    ts = TOK_TILE

    def body(lr_ref, li_ref, u_ref, dus_ref, br_ref, bi_ref, du_ref, dbr_ref, dbi_ref):
        @pl.when(pl.program_id(0) == 0)
        def _():
            dbr_ref[...] = jnp.zeros_like(dbr_ref)
            dbi_ref[...] = jnp.zeros_like(dbi_ref)

        lrb = lr_ref[...].astype(bf16)
        lib = li_ref[...].astype(bf16)
        du_ref[...] = dus_ref[...] + _dot_nt(lrb, br_ref[...].astype(bf16)) + _dot_nt(lib, bi_ref[...].astype(bf16))
        ub = u_ref[...].astype(bf16)
        dbr_ref[...] += _dot_tn(ub, lrb)
        dbi_ref[...] += _dot_tn(ub, lib)

    st = pl.BlockSpec((ts, D_STATE), lambda i: (i, 0))
    ch = pl.BlockSpec((ts, D_SSM), lambda i: (i, 0))
    b_full = _full((D_SSM, D_STATE))
    return pl.pallas_call(
        body, name="ssm_in_bwd", grid=(S // ts,),
        in_specs=[st, st, ch, ch, b_full, b_full],
        out_specs=[ch, b_full, b_full],
        out_shape=[jax.ShapeDtypeStruct((S, D_SSM), f32), jax.ShapeDtypeStruct((D_SSM, D_STATE), f32),
                   jax.ShapeDtypeStruct((D_SSM, D_STATE), f32)],
        compiler_params=_cp(("arbitrary",), 40),
    )(lr, li, u, du_skip, bre, bim)


def _ssm_da(lr, li, sr, si):
    S, N = lr.shape
    nst = S // SCAN_SEG

    def body(lr_ref, li_ref, sr_ref, si_ref, out_ref):
        row = lax.broadcasted_iota(jnp.int32, (SCAN_SEG, N), 0)
        last = pl.ds((nst - 1) * SCAN_SEG, SCAN_SEG)
        pr = jnp.where(row == 0, 0.0, pltpu.roll(sr_ref[last, :], 1, axis=0))
        pi = jnp.where(row == 0, 0.0, pltpu.roll(si_ref[last, :], 1, axis=0))
        first = pl.ds(0, SCAN_SEG)
        acc_r = lr_ref[first, :] * pr + li_ref[first, :] * pi
        acc_i = li_ref[first, :] * pr - lr_ref[first, :] * pi

        def step(t, c):
            acc_r, acc_i = c
            rows = pl.ds(pl.multiple_of(t * SCAN_SEG, SCAN_SEG), SCAN_SEG)
            prev = pl.ds(pl.multiple_of((t - 1) * SCAN_SEG, SCAN_SEG), SCAN_SEG)
            lrv, liv, srv, siv = lr_ref[rows, :], li_ref[rows, :], sr_ref[prev, :], si_ref[prev, :]
            return acc_r + lrv * srv + liv * siv, acc_i + liv * srv - lrv * siv

        acc_r, acc_i = lax.fori_loop(1, nst, step, (acc_r, acc_i))
        out_ref[0:1, :] = jnp.sum(acc_r, 0, keepdims=True)
        out_ref[1:2, :] = jnp.sum(acc_i, 0, keepdims=True)

    vm = pl.BlockSpec(memory_space=pltpu.VMEM)
    return pl.pallas_call(
        body, name="ssm_da", in_specs=[vm] * 4, out_specs=vm,
        out_shape=jax.ShapeDtypeStruct((2, N), f32),
        compiler_params=_cp(None, 48),
    )(lr, li, sr, si)


_POOL_TILE = 256


def _window_sums(xt, back):
    n = xt.shape[0]
    out = []
    ws = xt
    for k in (1, 2, 4, 8):
        ws = ws + pltpu.roll(ws, k if back else n - k, axis=0)
        out.append(ws)
    return out


def _pool_count(r0, w):
    t = r0 + lax.broadcasted_iota(jnp.int32, (_POOL_TILE, POOL_GROUP), 0)
    return jnp.minimum(t + 1, w).astype(f32)


def _pool_fwd(u_pad, pool_w, pool_scale):
    S = u_pad.shape[0] - POOL_HALO
    nt = S // _POOL_TILE

    def body(u_ref, w_ref, sc_ref, y_ref):
        def tile(t, c):
            r0 = pl.multiple_of(t * _POOL_TILE, _POOL_TILE)
            for g, w in enumerate(POOL_WINDOWS):
                cs = pl.ds(POOL_GROUP * g, POOL_GROUP)
                xt = u_ref[pl.ds(r0, _POOL_TILE + POOL_HALO), cs]
                ws = _window_sums(xt, True)[g][POOL_HALO:, :]
                pooled = ws / _pool_count(r0, w) - xt[POOL_HALO:, :]
                y_ref[pl.ds(r0, _POOL_TILE), cs] = _dot(pooled.astype(bf16), w_ref[g].astype(bf16)) * sc_ref[:, cs]
            return c
        lax.fori_loop(0, nt, tile, 0)

    vm = pl.BlockSpec(memory_space=pltpu.VMEM)
    return pl.pallas_call(
        body, name="pool_fwd", in_specs=[vm, vm, vm], out_specs=vm,
        out_shape=jax.ShapeDtypeStruct((S, D_POOL), f32),
    )(u_pad, pool_w, pool_scale)


def _pool_bwd(dy_pad, u_pad, pool_w, pool_scale):
    S = u_pad.shape[0] - POOL_HALO
    nt = S // _POOL_TILE
    n = _POOL_TILE + POOL_HALO

    def body(dy_ref, u_ref, w_ref, sc_ref, du_ref, dw_ref, dsc_ref):
        dw_ref[...] = jnp.zeros_like(dw_ref)
        dsc_ref[...] = jnp.zeros_like(dsc_ref)

        def tile(t, c):
            r0 = pl.multiple_of(t * _POOL_TILE, _POOL_TILE)
            for g, w in enumerate(POOL_WINDOWS):
                cs = pl.ds(POOL_GROUP * g, POOL_GROUP)
                wb = w_ref[g].astype(bf16)
                xt = u_ref[pl.ds(r0, n), cs]
                pooled = (_window_sums(xt, True)[g][POOL_HALO:, :] / _pool_count(r0, w) - xt[POOL_HALO:, :]).astype(bf16)
                dy = dy_ref[pl.ds(r0, _POOL_TILE), cs]
                dsc_ref[:, cs] += jnp.sum(dy * _dot(pooled, wb), 0, keepdims=True)
                dw_ref[g] += _dot_tn(pooled, (dy * sc_ref[:, cs]).astype(bf16))
                dyh = (dy_ref[pl.ds(r0, n), cs] * sc_ref[:, cs]).astype(bf16)
                dpl = _dot_nt(dyh, wb)
                cnt = jnp.minimum(r0 + lax.broadcasted_iota(jnp.int32, (n, POOL_GROUP), 0) + 1, w).astype(f32)
                lead = _window_sums(dpl / cnt, False)[g]
                du_ref[pl.ds(r0, _POOL_TILE), cs] = lead[:_POOL_TILE, :] - dpl[:_POOL_TILE, :]
            return c
        lax.fori_loop(0, nt, tile, 0)

    vm = pl.BlockSpec(memory_space=pltpu.VMEM)
    return pl.pallas_call(
        body, name="pool_bwd", in_specs=[vm, vm, vm, vm], out_specs=[vm, vm, vm],
        out_shape=[jax.ShapeDtypeStruct((S, D_POOL), f32), jax.ShapeDtypeStruct((4, POOL_GROUP, POOL_GROUP), f32),
                   jax.ShapeDtypeStruct((1, D_POOL), f32)],
    )(dy_pad, u_pad, pool_w, pool_scale)


def _loss_head(y, target):
    S, D = y.shape
    ts = TOK_TILE

    def body(y_ref, t_ref, loss_ref, dy_ref):
        @pl.when(pl.program_id(0) == 0)
        def _():
            loss_ref[...] = jnp.zeros_like(loss_ref)

        d = y_ref[...] - t_ref[...]
        dy_ref[...] = d * (1.0 / D)
        loss_ref[...] += 0.5 * jnp.sum(jnp.sum(d * d, -1, keepdims=True) * (1.0 / D), 0, keepdims=True)

    tok = pl.BlockSpec((ts, D), lambda i: (i, 0))
    return pl.pallas_call(
        body, name="loss_head", grid=(S // ts,),
        in_specs=[tok, tok], out_specs=[_full((1, 1)), tok],
        out_shape=[jax.ShapeDtypeStruct((1, 1), f32), jax.ShapeDtypeStruct((S, D), f32)],
        compiler_params=_cp(("arbitrary",)),
    )(y, target)


_ADA_COLS = 768


def _ada_fwd(c_all, ada_w, ada_b_cols):
    L, D, N = ada_w.shape
    B = c_all.shape[0]

    def body(c_ref, w_ref, b_ref, out_ref):
        cv = c_ref[...]
        cond = (cv * jax.nn.sigmoid(cv)).astype(bf16)
        out_ref[0] = _dot(cond, w_ref[0].astype(bf16)) + b_ref[0]

    return pl.pallas_call(
        body, name="ada_fwd", grid=(L, N // _ADA_COLS),
        in_specs=[_full((B, D)), pl.BlockSpec((1, D, _ADA_COLS), lambda l, j: (l, 0, j)),
                  pl.BlockSpec((1, 1, _ADA_COLS), lambda l, j: (l, 0, j))],
        out_specs=pl.BlockSpec((1, B, _ADA_COLS), lambda l, j: (l, 0, j)),
        out_shape=jax.ShapeDtypeStruct((L, B, N), f32),
        compiler_params=_cp(("parallel", "parallel")),
    )(c_all, ada_w, ada_b_cols)


def _ada_wgrad(c_all_t, dmod_cols):
    D, B = c_all_t.shape
    L, _, N = dmod_cols.shape

    def body(ct_ref, dm_ref, out_ref):
        cv = ct_ref[...]
        cond = cv * jax.nn.sigmoid(cv)
        acc = cond[:, 0:1] * dm_ref[0, 0:1, :]
        for b in range(1, B):
            acc = acc + cond[:, b:b + 1] * dm_ref[0, b:b + 1, :]
        out_ref[0] = acc

    return pl.pallas_call(
        body, name="ada_wgrad", grid=(L, N // _ADA_COLS),
        in_specs=[_full((D, B)), pl.BlockSpec((1, B, _ADA_COLS), lambda l, j: (l, 0, j))],
        out_specs=pl.BlockSpec((1, D, _ADA_COLS), lambda l, j: (l, 0, j)),
        out_shape=jax.ShapeDtypeStruct((L, D, N), f32),
        compiler_params=_cp(("parallel", "parallel")),
    )(c_all_t, dmod_cols)


def _adam_math(w, g, m, v):
    m = ADAM_B1 * m + (1.0 - ADAM_B1) * g
    v = ADAM_B2 * v + (1.0 - ADAM_B2) * (g * g)
    m_hat = m / (1.0 - ADAM_B1 ** ADAM_STEP)
    v_hat = v / (1.0 - ADAM_B2 ** ADAM_STEP)
    delta = -ADAM_LR * (m_hat / (jnp.sqrt(v_hat) + ADAM_EPS) + ADAM_WD * w)
    return delta, m, v


def _adamw(w, m, v, g, row_tile, row0=0, outs=None):
    R, C = w.shape
    b0 = row0 // row_tile

    def body(w_ref, m_ref, v_ref, g_ref, _0, _1, _2, _3, g_out, d_out, m_out, v_out):
        gv = g_ref[...]
        delta, mn, vn = _adam_math(w_ref[...], gv, m_ref[...], v_ref[...])
        g_out[...] = gv
        d_out[...] = delta
        m_out[...] = mn
        v_out[...] = vn

    pspec = pl.BlockSpec((row_tile, C), lambda i: (b0 + i, 0))
    gspec = pl.BlockSpec((row_tile, C), lambda i: (i, 0))
    anyspec = pl.BlockSpec(memory_space=pl.ANY)
    shp = jax.ShapeDtypeStruct((R, C), f32)
    if outs is None:
        outs = [lax.empty((R, C), f32) for _ in range(4)]
    return pl.pallas_call(
        body, name="adamw", grid=(g.shape[0] // row_tile,),
        in_specs=[pspec] * 3 + [gspec] + [anyspec] * 4, out_specs=[pspec] * 4, out_shape=[shp] * 4,
        input_output_aliases={4: 0, 5: 1, 6: 2, 7: 3},
        compiler_params=_cp(("parallel",), 40),
    )(w, m, v, g, *outs)


def _pair_sum(g5, got, pc):
    _, LS, _, R2, C = g5.shape

    def body(pc_ref, own_ref, got_ref, out_ref):
        out_ref[0, 0] = (own_ref[0, 0, 0].astype(f32) + got_ref[0, 0].astype(f32)).astype(bf16)

    gs = pltpu.PrefetchScalarGridSpec(
        num_scalar_prefetch=1, grid=(N_CHIPS, LS),
        in_specs=[pl.BlockSpec((1, 1, 1, R2, C), lambda p, s, pc: (p, s, pc[1], 0, 0)),
                  pl.BlockSpec((1, 1, R2, C), lambda p, s, pc: (p, s, 0, 0))],
        out_specs=pl.BlockSpec((1, 1, R2, C), lambda p, s, pc: (p, s, 0, 0)),
    )
    return pl.pallas_call(
        body, name="pair_sum", grid_spec=gs, out_shape=jax.ShapeDtypeStruct((N_CHIPS, LS, R2, C), bf16),
        compiler_params=_cp(("parallel", "parallel")),
    )(pc, g5, got)


def _sum_shards(hsum, recv, pc):
    _, LS, R2, C = hsum.shape

    def body(pc_ref, own_ref, r_ref, out_ref):
        acc = own_ref[0, 0].astype(f32)
        for j in range(3):
            acc = acc + r_ref[j, 0].astype(f32)
        out_ref[0, 0] = acc

    gs = pltpu.PrefetchScalarGridSpec(
        num_scalar_prefetch=1, grid=(LS,),
        in_specs=[pl.BlockSpec((1, 1, R2, C), lambda s, pc: (pc[0], s, 0, 0)),
                  pl.BlockSpec((3, 1, R2, C), lambda s, pc: (0, s, 0, 0))],
        out_specs=pl.BlockSpec((1, 1, R2, C), lambda s, pc: (s, pc[1], 0, 0)),
    )
    return pl.pallas_call(
        body, name="sum_shards", grid_spec=gs, out_shape=jax.ShapeDtypeStruct((LS, 2, R2, C), f32),
        compiler_params=_cp(("parallel",)),
    )(pc, hsum, recv)


def _sum8(packs):
    _, R, C = packs.shape
    tr = R // 8 if R % 64 == 0 else R

    def body(p_ref, out_ref):
        acc = p_ref[0]
        for d in range(1, 8):
            acc = acc + p_ref[d]
        out_ref[...] = acc

    return pl.pallas_call(
        body, name="sum8", grid=(R // tr,),
        in_specs=[pl.BlockSpec((8, tr, C), lambda i: (0, i, 0))],
        out_specs=pl.BlockSpec((tr, C), lambda i: (i, 0)),
        out_shape=jax.ShapeDtypeStruct((R, C), f32),
        compiler_params=_cp(("parallel",)),
    )(packs)


def _allgather8(x_shard):
    m_per, n = x_shard.shape

    def body(x_ref, out_ref, send_sems, recv_sems, local_sem):
        x, y, c = lax.axis_index("x"), lax.axis_index("y"), lax.axis_index("c")
        me, sibling = (x, y, c), (x, y, 1 - c)
        chips = [(1 - x, y), (x, 1 - y), (1 - x, 1 - y)]

        def rows(px, py, pc):
            return out_ref.at[pl.ds((4 * px + 2 * py + pc) * m_per, m_per), :]

        def copy(k, block, to, src=None):
            return pltpu.make_async_remote_copy(
                src_ref=rows(*block) if src is None else src, dst_ref=rows(*block),
                send_sem=send_sems.at[k], recv_sem=recv_sems.at[k], device_id=to, device_id_type=MESH)

        mine = pltpu.make_async_copy(x_ref, rows(*me), local_sem)
        mine.start()
        first = [copy(0, me, sibling, src=x_ref)]
        first += [copy(1 + j, me, (*chip, c), src=x_ref) for j, chip in enumerate(chips)]
        for cp in first:
            cp.start()
        passed = [copy(4 + j, (*chip, c), sibling) for j, chip in enumerate(chips)]
        for j, chip in enumerate(chips):
            copy(1 + j, (*chip, c), me).wait_recv()
            passed[j].start()
        copy(0, sibling, me).wait_recv()
        for j, chip in enumerate(chips):
            copy(4 + j, (*chip, 1 - c), me).wait_recv()
        for cp in first + passed:
            cp.wait_send()
        mine.wait()

    return pl.pallas_call(
        body, name="allgather8",
        out_shape=jax.ShapeDtypeStruct((8 * m_per, n), x_shard.dtype),
        in_specs=[pl.BlockSpec(memory_space=pltpu.VMEM)],
        out_specs=pl.BlockSpec(memory_space=pltpu.VMEM),
        scratch_shapes=[pltpu.SemaphoreType.DMA((7,)), pltpu.SemaphoreType.DMA((7,)), pltpu.SemaphoreType.DMA],
        compiler_params=_cp(None, 48),
    )(x_shard)


def _other_chips():
    x, y = lax.axis_index("x"), lax.axis_index("y")
    return [(1 - x, y), (x, 1 - y), (1 - x, 1 - y)]


_HBM = pl.BlockSpec(memory_space=pltpu.HBM)
_SEM = pl.BlockSpec(memory_space=pltpu.SEMAPHORE)
_EFFECT = pltpu.SideEffectType.DATAFLOW_SIDE_EFFECTING


def _gather_copies(srcs, lands, send_sems, recv_sems):
    x, y, c = lax.axis_index("x"), lax.axis_index("y"), lax.axis_index("c")
    return [pltpu.make_async_remote_copy(
        src_ref=srcs[a].at[:, c], dst_ref=lands[a].at[2 * x + y, :, c], send_sem=send_sems.at[3 * a + j],
        recv_sem=recv_sems.at[3 * a + j], device_id=(cx, cy, c), device_id_type=MESH)
        for a in range(len(srcs)) for j, (cx, cy) in enumerate(_other_chips())]


def _gather_start(chunks, after):
    sizes = [len(srcs) for srcs, _ in chunks]
    flat = [t for srcs, lands in chunks for t in list(srcs) + list(lands)]
    nflat = len(flat)
    nsem = 2 * len(chunks)

    def body(*refs):
        ins, sems, token = refs[:nflat], refs[nflat + 1:nflat + 1 + nsem], refs[-1]
        off = 0
        for k, n in enumerate(sizes):
            for cp in _gather_copies(ins[off:off + n], ins[off + n:off + 2 * n], sems[2 * k], sems[2 * k + 1]):
                cp.start()
            off += 2 * n
        token[...] = jnp.zeros_like(token)

    res = pl.pallas_call(
        body, name="gather_start",
        out_shape=[pltpu.SemaphoreType.DMA((3 * n,)) for n in sizes for _ in range(2)]
        + [pltpu.HBM(t.shape, t.dtype) for t in flat] + [jax.ShapeDtypeStruct((8, 128), f32)],
        in_specs=[_HBM] * nflat + [pl.BlockSpec(memory_space=pl.ANY)],
        out_specs=[_SEM] * nsem + [_HBM] * nflat + [pl.BlockSpec(memory_space=pltpu.VMEM)],
        input_output_aliases={i: nsem + i for i in range(nflat)},
        compiler_params=pltpu.CompilerParams(has_side_effects=_EFFECT),
    )(*[pltpu.with_memory_space_constraint(t, pltpu.HBM) for t in flat], after)
    out, off = [], nsem
    for k, n in enumerate(sizes):
        out.append((res[2 * k], res[2 * k + 1], res[off:off + n], res[off + n:off + 2 * n]))
        off += 2 * n
    return out, res[-1]


def _gather_wait(send_sems, recv_sems, srcs, lands, after, name):
    n = len(srcs)

    def body(*refs):
        for cp in _gather_copies(refs[:n], refs[n:2 * n], refs[2 * n], refs[2 * n + 1]):
            cp.wait_send()
            cp.wait_recv()

    res = pl.pallas_call(
        body, name=name,
        out_shape=[pltpu.HBM(t.shape, t.dtype) for t in list(srcs) + list(lands)],
        in_specs=[_HBM] * (2 * n) + [_SEM, _SEM, pl.BlockSpec(memory_space=pl.ANY)],
        out_specs=[_HBM] * (2 * n),
        input_output_aliases={i: i for i in range(2 * n)},
        compiler_params=pltpu.CompilerParams(has_side_effects=_EFFECT),
    )(*srcs, *lands, send_sems, recv_sems, after)
    return res[n:]


def _gather_forward(lands):
    n = len(lands)

    def body(*refs):
        outs = refs[n:2 * n]
        send_sems, recv_sems = refs[2 * n:]
        x, y, c = lax.axis_index("x"), lax.axis_index("y"), lax.axis_index("c")
        sibling = (x, y, 1 - c)
        copies = []
        for a in range(n):
            for j, (cx, cy) in enumerate(_other_chips()):
                mine = outs[a].at[2 * cx + cy, :, c]
                cp = pltpu.make_async_remote_copy(src_ref=mine, dst_ref=mine, send_sem=send_sems.at[3 * a + j],
                                                  recv_sem=recv_sems.at[3 * a + j], device_id=sibling, device_id_type=MESH)
                cp.start()
                copies.append((cp, a, j, cx, cy))
        for cp, a, j, cx, cy in copies:
            cp.wait_send()
            theirs = outs[a].at[2 * cx + cy, :, 1 - c]
            pltpu.make_async_remote_copy(src_ref=theirs, dst_ref=theirs, send_sem=send_sems.at[3 * a + j],
                                         recv_sem=recv_sems.at[3 * a + j], device_id=sibling, device_id_type=MESH).wait_recv()

    hbm = pl.BlockSpec(memory_space=pl.ANY)
    return pl.pallas_call(
        body, name="gather_forward",
        out_shape=[jax.ShapeDtypeStruct(t.shape, t.dtype) for t in lands],
        in_specs=[hbm] * n, out_specs=[hbm] * n,
        input_output_aliases={a: a for a in range(n)},
        scratch_shapes=[pltpu.SemaphoreType.DMA((3 * n,)), pltpu.SemaphoreType.DMA((3 * n,))],
    )(*lands)


def _pair_exchange(g5s):
    n = len(g5s)

    def body(*refs):
        ins, outs = refs[:n], refs[n:2 * n]
        send_sems, recv_sems = refs[2 * n:]
        c = lax.axis_index("c")
        sibling = (lax.axis_index("x"), lax.axis_index("y"), 1 - c)
        copies = []
        for a in range(n):
            cp = pltpu.make_async_remote_copy(src_ref=ins[a].at[:, :, 1 - c], dst_ref=outs[a], send_sem=send_sems.at[a],
                                              recv_sem=recv_sems.at[a], device_id=sibling, device_id_type=MESH)
            cp.start()
            copies.append(cp)
        for cp in copies:
            cp.wait()

    hbm = pl.BlockSpec(memory_space=pl.ANY)
    return pl.pallas_call(
        body, name="pair_exchange",
        out_shape=[jax.ShapeDtypeStruct(g.shape[:2] + g.shape[3:], g.dtype) for g in g5s],
        in_specs=[hbm] * n, out_specs=[hbm] * n,
        scratch_shapes=[pltpu.SemaphoreType.DMA((n,)), pltpu.SemaphoreType.DMA((n,))],
    )(*g5s)


def _scatter_copies(srcs, lands, send_sems, recv_sems):
    c = lax.axis_index("c")
    return [pltpu.make_async_remote_copy(
        src_ref=srcs[a].at[2 * cx + cy], dst_ref=lands[a].at[j], send_sem=send_sems.at[3 * a + j],
        recv_sem=recv_sems.at[3 * a + j], device_id=(cx, cy, c), device_id_type=MESH)
        for a in range(len(srcs)) for j, (cx, cy) in enumerate(_other_chips())]


def _scatter_start(hsums, name, after=()):
    n = len(hsums)
    na = len(after)

    def body(*refs):
        srcs, lands = refs[:n], refs[n:2 * n]
        send_sems, recv_sems = refs[2 * n + na], refs[2 * n + na + 1]
        for cp in _scatter_copies(srcs, lands, send_sems, recv_sems):
            cp.start()
        refs[-1][...] = jnp.zeros_like(refs[-1])

    lands = [lax.empty((3,) + g.shape[1:], g.dtype) for g in hsums]
    res = pl.pallas_call(
        body, name=name,
        out_shape=[pltpu.SemaphoreType.DMA((3 * n,)), pltpu.SemaphoreType.DMA((3 * n,))]
        + [pltpu.HBM(g.shape, g.dtype) for g in hsums] + [pltpu.HBM(g.shape, g.dtype) for g in lands]
        + [jax.ShapeDtypeStruct((8, 128), f32)],
        in_specs=[_HBM] * (2 * n) + [pl.BlockSpec(memory_space=pl.ANY)] * na,
        out_specs=[_SEM, _SEM] + [_HBM] * (2 * n) + [pl.BlockSpec(memory_space=pltpu.VMEM)],
        input_output_aliases={i: i + 2 for i in range(2 * n)},
        compiler_params=pltpu.CompilerParams(has_side_effects=_EFFECT),
    )(*[pltpu.with_memory_space_constraint(t, pltpu.HBM) for t in list(hsums) + lands], *after)
    return (res[0], res[1], res[2:2 + n], res[2 + n:2 + 2 * n]), res[-1]


def _scatter_wait(send_sems, recv_sems, srcs, lands, after, name):
    n = len(srcs)
    extra = list(after)

    def body(*refs):
        s_refs, l_refs = refs[:n], refs[n:2 * n]
        ss, rs = refs[2 * n], refs[2 * n + 1]
        for cp in _scatter_copies(s_refs, l_refs, ss, rs):
            cp.wait_send()
            cp.wait_recv()

    res = pl.pallas_call(
        body, name=name,
        out_shape=[pltpu.HBM(g.shape, g.dtype) for g in srcs] + [pltpu.HBM(g.shape, g.dtype) for g in lands],
        in_specs=[_HBM] * (2 * n) + [_SEM, _SEM] + [pl.BlockSpec(memory_space=pl.ANY)] * len(extra),
        out_specs=[_HBM] * (2 * n),
        input_output_aliases={i: i for i in range(2 * n)},
        compiler_params=pltpu.CompilerParams(has_side_effects=_EFFECT),
    )(*srcs, *lands, send_sems, recv_sems, *extra)
    return res[:n], res[n:]


def _swap_halves(fulls):
    n = len(fulls)

    def body(*refs):
        ins, outs = refs[:n], refs[n:2 * n]
        send_sems, recv_sems = refs[2 * n:]
        c = lax.axis_index("c")
        sibling = (lax.axis_index("x"), lax.axis_index("y"), 1 - c)
        copies = []
        for a in range(n):
            cp = pltpu.make_async_remote_copy(src_ref=outs[a].at[:, c], dst_ref=outs[a].at[:, c], send_sem=send_sems.at[a],
                                              recv_sem=recv_sems.at[a], device_id=sibling, device_id_type=MESH)
            cp.start()
            copies.append(cp)
        for a, cp in enumerate(copies):
            cp.wait_send()
            theirs = outs[a].at[:, 1 - c]
            pltpu.make_async_remote_copy(src_ref=theirs, dst_ref=theirs, send_sem=send_sems.at[a], recv_sem=recv_sems.at[a],
                                         device_id=sibling, device_id_type=MESH).wait_recv()

    hbm = pl.BlockSpec(memory_space=pl.ANY)
    return pl.pallas_call(
        body, name="swap_halves",
        out_shape=[jax.ShapeDtypeStruct(p.shape, p.dtype) for p in fulls],
        in_specs=[hbm] * n, out_specs=[hbm] * n,
        input_output_aliases={a: a for a in range(n)},
        scratch_shapes=[pltpu.SemaphoreType.DMA((n,)), pltpu.SemaphoreType.DMA((n,))],
    )(*fulls)


def _to_segments(t):
    s, c = t.shape
    return t.reshape(SCAN_SEG, s // SCAN_SEG, c).transpose(1, 0, 2).reshape(s, c)


def _from_segments(t):
    s, c = t.shape
    return t.reshape(s // SCAN_SEG, SCAN_SEG, c).transpose(1, 0, 2).reshape(s, c)


def _ssm_operators(a_re, a_im, log_dt, b_re, b_im, c_re, c_im):
    lam = lax.complex(a_re, a_im)
    dt = jnp.exp(log_dt)[:, None]
    a_bar = jnp.exp(lam * dt)
    b_bar = ((a_bar - 1.0) / lam)[:, :, None] * lax.complex(b_re, b_im)
    eye = jnp.eye(N_GROUPS, dtype=f32)

    def embed_b(t):
        return (jnp.transpose(t, (0, 2, 1))[:, :, None, :] * eye[:, None, :, None]).reshape(D_SSM, D_STATE)

    def embed_c(t):
        return (jnp.transpose(t, (0, 2, 1))[:, :, None, :] * eye[:, None, :, None]).reshape(D_STATE, D_SSM)

    a2 = jnp.stack([a_bar.real.reshape(D_STATE), a_bar.imag.reshape(D_STATE)])
    return a2, embed_b(b_bar.real), embed_b(b_bar.imag), embed_c(c_re), embed_c(c_im)


def _local_step(x, target, mod, small, ffn_weights, mix_weights, grads_done):
    table = jnp.asarray(_bucket_table())
    bias = _bias_fwd(small["rel_bias"], table)
    L = DEPTH
    saved = []
    ssm_ops = []
    for l in range(L):
        sv = {}
        m9 = mod[l]
        sv["x0"] = x
        sv["w0"] = ffn_weights(l, 0, x)
        x, sv["f0"], sv["g0"], sv["u0"] = _ffn_fwd(x, m9[0:3], *sv["w0"], 0, small["ln_g"][l, 0:1], small["ln_b"][l, 0:1])
        sv["x1"] = x
        sv["w1"] = mix_weights(l, x)
        *qkv, z_rest, sv["h1"] = _mix_in_fwd(x, m9[3:6], sv["w1"][0], 0)
        S = x.shape[0]
        qkv = [t.reshape(3, S, D_ATT) for t in qkv]
        att = [_att_fwd(qkv[b], bias, b) for b in range(3)]
        y_att, lse3 = _att_merge([att[b][0].reshape(d, S // d, D_ATT) for b, d in enumerate(DILATIONS)],
                                 [att[b][1].reshape(d, S // d, _LANES) for b, d in enumerate(DILATIONS)])
        sv.update(qkv=qkv, lse=[a[1] for a in att], lse3=lse3, y_att=y_att)

        prm = tuple(small[k][l] for k in ("ssm_a_re", "ssm_a_im", "ssm_log_dt", "ssm_b_re", "ssm_b_im", "ssm_c_re", "ssm_c_im"))
        (a2, bre, bim, cre, cim), ops_vjp = jax.vjp(_ssm_operators, *prm)
        ssm_ops.append(ops_vjp)
        u_ssm = _to_segments(z_rest[:, :D_SSM])
        bur, bui = _ssm_in(u_ssm, bre, bim)
        sr, si = _ssm_scan(bur, bui, a2, False)
        dskip = small["ssm_d"][l][None, :]
        glu_b = small["glu_b"][l][None, :]
        out_seg, y_seg = _ssm_out(sr, si, u_ssm, cre, cim, dskip, small["glu_w"][l], glu_b)
        y_ssm = _from_segments(out_seg)
        sv.update(a2=a2, bre=bre, bim=bim, cre=cre, cim=cim, u_ssm=u_ssm, sr=sr, si=si, y_seg=y_seg, y_ssm=y_ssm)

        u_pool = jnp.concatenate([jnp.zeros((POOL_HALO, D_POOL), f32), z_rest[:, D_SSM:]])
        y_pool = _pool_fwd(u_pool, small["pool_w"][l], small["pool_scale"][l][None, :])
        sv.update(u_pool=u_pool, y_pool=y_pool)

        x, sv["ymix"] = _mix_out_fwd(x, y_att, y_ssm, y_pool, m9[3:6], sv["w1"][1], 0, small["ln_g"][l, 1:2], small["ln_b"][l, 1:2])
        sv["x2"] = x
        sv["w2"] = ffn_weights(l, 1, x)
        x, sv["f2"], sv["g2"], sv["u2"] = _ffn_fwd(x, m9[6:9], *sv["w2"], 0, small["ln_g"][l, 2:3], small["ln_b"][l, 2:3])
        saved.append(sv)

    loss, dx = _loss_head(x, target)

    dmod = [None] * L
    dln_g = [None] * L
    dln_b = [None] * L
    sg = {k: [None] * L for k in ("ssm_a_re", "ssm_a_im", "ssm_log_dt", "ssm_b_re", "ssm_b_im", "ssm_c_re", "ssm_c_im",
                                  "ssm_d", "glu_w", "glu_b", "pool_w", "pool_scale")}
    dbias_tot = None
    order_after = jnp.zeros((), f32)
    for l in reversed(range(L)):
        sv = saved[l]
        m9 = mod[l] + order_after

        def fresh(like):
            return [lax.empty(t.shape, bf16) for t in like]

        dx, dg, du, a, h, df, dm2, dlg2, dlb2 = _ffn_bwd(dx, sv["x2"], sv["f2"], sv["g2"], sv["u2"], m9[6:9], *sv["w2"], 0,
                                                        small["ln_g"][l, 2:3])
        g_ffn1 = _ffn_wgrad(h, dg, du, a, df, *fresh(sv["w2"]), 0)
        dxr, d_att, d_ssm, d_pool, dgate1, dlg1, dlb1, g_w_out = _mix_out_bwd(
            dx, sv["x1"], sv["ymix"], sv["y_att"], sv["y_ssm"], sv["y_pool"], m9[3:6], sv["w1"][1], 0, small["ln_g"][l, 1:2],
            fresh(sv["w1"])[1])
        S = d_att.shape[0]
        merged = _att_merge_bwd(d_att, sv["y_att"], sv["lse3"])
        dqkv, dbias = [], []
        for b, d in enumerate(DILATIONS):
            dq_b, db_b = _att_bwd(sv["qkv"][b], merged[b].reshape(S, D_ATT), sv["lse"][b], merged[3 + b].reshape(S, _LANES), bias, b)
            dqkv.append(dq_b.reshape(3, d, S // d, D_ATT))
            dbias.append(db_b)
        dbias = jnp.stack(dbias)
        dbias_tot = dbias if dbias_tot is None else dbias_tot + dbias
        d_seg = _to_segments(d_ssm)
        dskip = small["ssm_d"][l][None, :]
        glu_b = small["glu_b"][l][None, :]
        gsr, gsi, du_skip, dcre, dcim, dd, dglu_b, dglu_w = _ssm_out_bwd(
            d_seg, sv["y_seg"], sv["u_ssm"], sv["sr"], sv["si"], sv["cre"], sv["cim"], dskip, small["glu_w"][l], glu_b)
        lr, li = _ssm_scan(gsr, gsi, sv["a2"], True)
        du_seg, dbre, dbim = _ssm_in_bwd(lr, li, sv["u_ssm"], du_skip, sv["bre"], sv["bim"])
        da2 = _ssm_da(lr, li, sv["sr"], sv["si"])
        d_prm = ssm_ops[l]((da2, dbre, dbim, dcre, dcim))
        for k, v in zip(("ssm_a_re", "ssm_a_im", "ssm_log_dt", "ssm_b_re", "ssm_b_im", "ssm_c_re", "ssm_c_im"), d_prm):
            sg[k][l] = v
        sg["ssm_d"][l] = dd[0]
        sg["glu_b"][l] = dglu_b[0]
        sg["glu_w"][l] = dglu_w
        du_ssm = _from_segments(du_seg)
        dyp = jnp.concatenate([d_pool, jnp.zeros((POOL_HALO, D_POOL), f32)])
        du_pool, dpw, dps = _pool_bwd(dyp, sv["u_pool"], small["pool_w"][l], small["pool_scale"][l][None, :])
        sg["pool_w"][l] = dpw
        sg["pool_scale"][l] = dps[0]
        d_rest = jnp.concatenate([du_ssm, du_pool], axis=1).astype(bf16)
        dx, dm1, dz = _mix_in_bwd(dqkv, d_rest, dxr, sv["x1"], m9[3:6], sv["w1"][0], 0)
        g_w_in = _mix_in_wgrad(sv["h1"], dz, fresh(sv["w1"])[0], 0)
        ffn_names = ("ffn_w_gate", "ffn_w_up", "ffn_w_down")
        m9 = m9 + grads_done(l, 1, list(zip(ffn_names, [(2 * l + 1) * FF_SHARD] * 3, g_ffn1))
                             + [("w_in", l * D_MODEL, g_w_in), ("w_out", l * 256, g_w_out)])
        dm1 = jnp.concatenate([dm1[0:2], dgate1])
        dx, dg, du, a, h, df, dm0, dlg0, dlb0 = _ffn_bwd(dx, sv["x0"], sv["f0"], sv["g0"], sv["u0"], m9[0:3], *sv["w0"], 0,
                                                        small["ln_g"][l, 0:1])
        g_ffn0 = _ffn_wgrad(h, dg, du, a, df, *fresh(sv["w0"]), 0)
        order_after = grads_done(l, 0, list(zip(ffn_names, [2 * l * FF_SHARD] * 3, g_ffn0)))
        dmod[l] = jnp.concatenate([dm0, dm1, dm2])
        dln_g[l] = jnp.concatenate([dlg0, dlg1, dlg2])
        dln_b[l] = jnp.concatenate([dlb0, dlb1, dlb2])

    small_grads = {k: jnp.stack(v) for k, v in sg.items()}
    small_grads["rel_bias"] = _bias_bwd(dbias_tot, table)
    small_grads["ln_g"] = jnp.stack(dln_g)
    small_grads["ln_b"] = jnp.stack(dln_b)
    return loss, dx, jnp.stack(dmod), small_grads


def _pack(arrs):
    flat = jnp.concatenate([a.reshape(-1).astype(f32) for a in arrs])
    n = flat.shape[0]
    npad = -(-n // 1024) * 1024
    return jnp.pad(flat, (0, npad - n)).reshape(npad // 128, 128)


def _unpack(buf, shapes):
    flat = buf.reshape(-1)
    out, off = [], 0
    for s in shapes:
        n = int(np.prod(s))
        out.append(flat[off:off + n].reshape(s))
        off += n
    return out


_REPL = ("rel_bias", "ada_b", "ssm_a_re", "ssm_a_im", "ssm_log_dt", "ssm_b_re", "ssm_b_im", "ssm_c_re", "ssm_c_im",
         "ssm_d", "glu_b", "pool_w", "pool_scale")
_SMALL_SHARDED = ("ln_g", "ln_b", "glu_w")
_BIG = ("ffn_w_gate", "ffn_w_up", "ffn_w_down", "w_in", "w_out")
_ORDER = ("rel_bias", "ada_w", "ada_b", "ln_g", "ln_b", "ffn_w_gate", "ffn_w_up", "ffn_w_down", "w_in", "w_out",
          "ssm_a_re", "ssm_a_im", "ssm_log_dt", "ssm_b_re", "ssm_b_im", "ssm_c_re", "ssm_c_im", "ssm_d", "glu_w",
          "glu_b", "pool_w", "pool_scale")


def kernel(x, c, rel_bias, ada_w, ada_b, ln_g, ln_b, ffn_w_gate, ffn_w_up, ffn_w_down, w_in, w_out, ssm_a_re, ssm_a_im, ssm_log_dt, ssm_b_re, ssm_b_im, ssm_c_re, ssm_c_im, ssm_d, glu_w, glu_b, pool_w, pool_scale, loss_target, m_rel_bias, m_ada_w, m_ada_b, m_ln_g, m_ln_b, m_ffn_w_gate, m_ffn_w_up, m_ffn_w_down, m_w_in, m_w_out, m_ssm_a_re, m_ssm_a_im, m_ssm_log_dt, m_ssm_b_re, m_ssm_b_im, m_ssm_c_re, m_ssm_c_im, m_ssm_d, m_glu_w, m_glu_b, m_pool_w, m_pool_scale, v_rel_bias, v_ada_w, v_ada_b, v_ln_g, v_ln_b, v_ffn_w_gate, v_ffn_w_up, v_ffn_w_down, v_w_in, v_w_out, v_ssm_a_re, v_ssm_a_im, v_ssm_log_dt, v_ssm_b_re, v_ssm_b_im, v_ssm_c_re, v_ssm_c_im, v_ssm_d, v_glu_w, v_glu_b, v_pool_w, v_pool_scale):
    args = dict(locals())
    w = {k: args[k] for k in _ORDER}
    m = {k: args["m_" + k] for k in _ORDER}
    v = {k: args["v_" + k] for k in _ORDER}
    L, D = DEPTH, D_MODEL
    ax, ay, ac = lax.axis_index("x"), lax.axis_index("y"), lax.axis_index("c")
    p_me = 2 * ax + ay
    dev = 4 * ax + 2 * ay + ac

    transposed = ("ffn_w_gate", "ffn_w_up")
    for d in (w, m, v):
        for name in transposed:
            d[name] = jnp.swapaxes(d[name], 2, 3)

    def halves(t):
        return t.astype(bf16).reshape(1, 2, t.shape[0] // 2, t.shape[1])

    def landing(src):
        return lax.dynamic_update_slice(lax.empty((N_CHIPS,) + src.shape, bf16), src[None], (p_me, 0, 0, 0, 0))

    chunk_keys = [("ffn", 0, 0), ("mix", 0), ("ffn", 0, 1), ("ffn", 1, 0), ("mix", 1), ("ffn", 1, 1)]
    chunk_srcs = []
    for key in chunk_keys:
        if key[0] == "ffn":
            chunk_srcs.append([halves(w[name][key[1], key[2]]) for name in ("ffn_w_gate", "ffn_w_up", "ffn_w_down")])
        else:
            chunk_srcs.append([halves(w_in[key[1]]), halves(w_out[key[1]])])

    pack = _pack([c, ln_g, ln_b, glu_w])
    rows = pack.shape[0]
    allp = _allgather8(pack).reshape(8, rows, 128)
    c_all = allp[:, :8].reshape(8, D)
    by_chip = allp[0::2]

    def sharded(row0, nrows, shape, axis):
        t = by_chip[:, row0:row0 + nrows].reshape((N_CHIPS,) + shape)
        return jnp.concatenate([t[p] for p in range(N_CHIPS)], axis=axis)

    ln_g_full = sharded(8, 12, (L, 3, 256), 2)
    ln_b_full = sharded(20, 12, (L, 3, 256), 2)
    glu_w_full = sharded(32, 256, (L, 64, 256), 1)

    ncol = ada_w.shape[-1]
    ada_b_cols = lax.dynamic_slice_in_dim(ada_b, p_me * ncol, ncol, axis=1)[:, None, :]
    mod_part = _ada_fwd(c_all, ada_w, ada_b_cols)
    mrows = L * 8 * ncol // 128
    mod_all = _allgather8(mod_part.reshape(mrows, 128)).reshape(8, L, 8, ncol)
    mod_mine = lax.dynamic_index_in_dim(mod_all, dev, axis=2, keepdims=False)
    mod = jnp.concatenate([mod_mine[2 * p] for p in range(N_CHIPS)], axis=-1).reshape(L, 9, D)

    in_flight, _ = _gather_start([(srcs, [landing(t) for t in srcs]) for srcs in chunk_srcs], mod)

    def gathered(key, after):
        k = chunk_keys.index(key)
        lands = _gather_forward(_gather_wait(*in_flight[k], after, "gather_wait_%d" % k))
        return [t.reshape(N_CHIPS, 1, 2 * t.shape[3], t.shape[4]) for t in lands]

    pc = jnp.stack([p_me, ac]).astype(jnp.int32)
    groups = {}
    scattering = {}

    def start_group(tag, after=()):
        g5 = [g.reshape(g.shape[:2] + (2, g.shape[2] // 2, g.shape[3])) for _, _, g in groups[tag]]
        hsum = [_pair_sum(g, r, pc) for g, r in zip(g5, _pair_exchange(g5))]
        scattering[tag], begun = _scatter_start(hsum, "scatter_start_%s" % tag, after)
        return begun

    def grads_done(l, s, grads):
        if l == 1:
            groups.setdefault("l1", []).extend(grads)
            return start_group("l1")[0, 0] if s == 0 else jnp.zeros((), f32)
        groups["l0a" if s == 1 else "l0b"] = grads
        return start_group("l0a")[0, 0] if s == 1 else jnp.zeros((), f32)

    small = {k: w[k] for k in _REPL if k != "ada_b"}
    small.update(ln_g=ln_g_full, ln_b=ln_b_full, glu_w=glu_w_full)
    loss_dev, grad_x, dmod, sgrads = _local_step(
        x[0], loss_target[0], mod, small, lambda l, s, after: gathered(("ffn", l, s), after),
        lambda l, after: gathered(("mix", l), after), grads_done)
    loss = lax.psum(loss_dev[0, 0], ("x", "y", "c"))

    names = ("rel_bias", "ln_g", "ln_b", "ssm_a_re", "ssm_a_im", "ssm_log_dt", "ssm_b_re", "ssm_b_im", "ssm_c_re",
             "ssm_c_im", "ssm_d", "glu_w", "glu_b", "pool_w", "pool_scale")
    gpack = _pack([dmod] + [sgrads[k] for k in names])
    grows = gpack.shape[0]
    gall = _allgather8(gpack).reshape(8, grows, 128)
    l0b_begun = start_group("l0b", (gall,))
    gsum = _unpack(_sum8(gall), [(L, 9 * D)] + [sgrads[k].shape for k in names])
    red = dict(zip(("ada_b",) + names, gsum))
    red["ln_g"] = lax.dynamic_slice_in_dim(red["ln_g"], p_me * 256, 256, axis=2)
    red["ln_b"] = lax.dynamic_slice_in_dim(red["ln_b"], p_me * 256, 256, axis=2)
    red["glu_w"] = lax.dynamic_slice_in_dim(red["glu_w"], p_me * 64, 64, axis=1)

    dmod_all = gall[:, :L * 9 * D // 128].reshape(8, L, 9 * D)
    dmod_cols = jnp.transpose(lax.dynamic_slice_in_dim(dmod_all, p_me * ncol, ncol, axis=2), (1, 0, 2))
    g_ada_w = _ada_wgrad(jnp.transpose(c_all), dmod_cols)

    out_g, out_d, out_m, out_v = {}, {}, {}, {}
    r2 = (L * D, ncol)
    res = _adamw(ada_w.reshape(r2), m["ada_w"].reshape(r2), v["ada_w"].reshape(r2), g_ada_w.reshape(r2), 128)
    out_g["ada_w"], out_d["ada_w"], out_m["ada_w"], out_v["ada_w"] = [t.reshape(ada_w.shape) for t in res]

    small_names = _REPL + _SMALL_SHARDED
    wp = _pack([w[k] for k in small_names])
    res_small = _adamw(wp, _pack([m[k] for k in small_names]), _pack([v[k] for k in small_names]),
                       _pack([red[k] for k in small_names]), wp.shape[0])
    for t, dst in zip(res_small, (out_g, out_d, out_m, out_v)):
        for k, a in zip(small_names, _unpack(t, [w[k].shape for k in small_names])):
            dst[k] = a

    row_tile = dict(zip(_BIG, (352, 352, 352, 256, 256)))
    big = {name: None for name in _BIG}
    after = [grad_x, res_small[1], res[1], l0b_begun]
    for tag in ("l1", "l0a", "l0b"):
        hsum, recv = _scatter_wait(*scattering[tag], after, "scatter_wait_%s" % tag)
        full = _swap_halves([_sum_shards(h, r, pc) for h, r in zip(hsum, recv)])
        for (name, row0, _), g in zip(groups[tag], full):
            shp = w[name].shape
            r2 = (int(np.prod(shp[:-1])), shp[-1])
            big[name] = _adamw(w[name].reshape(r2), m[name].reshape(r2), v[name].reshape(r2), g.reshape(-1, shp[-1]),
                               row_tile[name], row0, big[name])
        after = [big[name][1] for name, _, _ in groups[tag]]
    for name in _BIG:
        res = [t.reshape(w[name].shape) for t in big[name]]
        out_g[name], out_d[name], out_m[name], out_v[name] = [jnp.swapaxes(t, 2, 3) for t in res] if name in transposed else res

    return (loss, grad_x[None], *[out_g[k] for k in _ORDER], *[out_d[k] for k in _ORDER],
            *[out_m[k] for k in _ORDER], *[out_v[k] for k in _ORDER])
```

```python
import functools
import math

import numpy as np
import jax
import jax.numpy as jnp
from jax import lax
from jax.experimental import pallas as pl
from jax.experimental.pallas import tpu as pltpu

f32 = jnp.float32
bf16 = jnp.bfloat16
MESH = pl.DeviceIdType.MESH

D_MODEL = 1024
SEQ = 2048
DEPTH = 2
HEAD_DIM = 64
N_HEADS = 8
D_ATT = 512
DILATIONS = (1, 4, 16)
BLOCKS_PER_RESIDUE = (16, 4, 1)
ATT_BLOCK = 128
N_UNITS = SEQ // ATT_BLOCK
N_GROUPS = 16
SSM_GROUP = 16
SSM_STATE = 64
D_SSM = 256
D_STATE = N_GROUPS * SSM_STATE
POOL_WINDOWS = (2, 4, 8, 16)
POOL_GROUP = 64
D_POOL = 256
POOL_HALO = 16
D_FF = 2816
N_BUCKETS = 32
MAX_DISTANCE = 2048
ALPHA = (2 * DEPTH) ** 0.25
FFN_RES = 0.5
LN_EPS = 1e-5
NEG = -1e30
N_CHIPS = 4
FF_SHARD = D_FF // N_CHIPS
SCAN_SEG = 8
SCAN_STEPS = SEQ // SCAN_SEG

ADAM_LR, ADAM_B1, ADAM_B2, ADAM_EPS, ADAM_WD, ADAM_STEP = 0.001, 0.9, 0.999, 1e-08, 0.01, 10

TOK_TILE = 512


def _cp(dims=None, vmem_mb=None):
    kw = {}
    if dims is not None:
        kw["dimension_semantics"] = dims
    if vmem_mb is not None:
        kw["vmem_limit_bytes"] = vmem_mb << 20
    return pltpu.CompilerParams(**kw)


def _dot(a, b):
    return jnp.dot(a, b, preferred_element_type=f32)


def _dot_nt(a, b):
    return lax.dot_general(a, b, (((1,), (1,)), ((), ())), preferred_element_type=f32)


def _dot_tn(a, b):
    return lax.dot_general(a, b, (((0,), (0,)), ((), ())), preferred_element_type=f32)


def _ln_stats(v):
    mu = jnp.mean(v, -1, keepdims=True)
    d = v - mu
    var = jnp.mean(d * d, -1, keepdims=True)
    rstd = lax.rsqrt(var + LN_EPS)
    return d * rstd, rstd


def _ln_bwd(dxh, xh, rstd):
    return rstd * (dxh - jnp.mean(dxh, -1, keepdims=True) - xh * jnp.mean(dxh * xh, -1, keepdims=True))


_GELU_C = math.sqrt(2.0 / math.pi)


def _gelu(y):
    return 0.5 * y * (1.0 + jnp.tanh(_GELU_C * (y + 0.044715 * y * y * y)))


def _gelu_grad(y):
    t = jnp.tanh(_GELU_C * (y + 0.044715 * y * y * y))
    return 0.5 * (1.0 + t) + 0.5 * y * (1.0 - t * t) * (_GELU_C * (1.0 + 3 * 0.044715 * y * y))


def _full(shape):
    return pl.BlockSpec(shape, lambda *_: (0,) * len(shape))


def _ffn_fwd(x, mod3, wg, wu, wd, ls, lng, lnb):
    S, D = x.shape
    Fs = wg.shape[-2]
    ts = TOK_TILE

    def body(x_ref, mod_ref, wg_ref, wu_ref, wd_ref, lng_ref, lnb_ref, xo_ref, f_ref, g_ref, u_ref, h_sc, acc_sc):
        j = pl.program_id(1)

        @pl.when(j == 0)
        def _():
            xh, _ = _ln_stats(x_ref[...])
            h_sc[...] = (xh * (1.0 + mod_ref[1:2, :]) + mod_ref[0:1, :]).astype(bf16)
            acc_sc[...] = jnp.zeros_like(acc_sc)

        h = h_sc[...]
        g = _dot_nt(h, wg_ref[0, 0])
        u = _dot_nt(h, wu_ref[0, 0])
        g_ref[0] = g.astype(bf16)
        u_ref[0] = u.astype(bf16)
        a = (g * jax.nn.sigmoid(g) * u).astype(bf16)
        acc_sc[...] += _dot(a, wd_ref[0, 0])

        @pl.when(j == N_CHIPS - 1)
        def _():
            f = acc_sc[...]
            f_ref[...] = f
            r = ALPHA * x_ref[...] + (FFN_RES * mod_ref[2:3, :]) * f
            rh, _ = _ln_stats(r)
            xo_ref[...] = rh * lng_ref[...] + lnb_ref[...]

    tok = pl.BlockSpec((ts, D), lambda i, j: (i, 0))
    wrow = pl.BlockSpec((1, 1, Fs, D), lambda i, j: (j, ls, 0, 0))
    hid = pl.BlockSpec((1, ts, Fs), lambda i, j: (j, i, 0))
    return pl.pallas_call(
        body, name="ffn_fwd", grid=(S // ts, N_CHIPS),
        in_specs=[tok, _full((3, D)), wrow, wrow, wrow, _full((1, D)), _full((1, D))],
        out_specs=[tok, tok, hid, hid],
        out_shape=[jax.ShapeDtypeStruct((S, D), f32), jax.ShapeDtypeStruct((S, D), f32),
                   jax.ShapeDtypeStruct((N_CHIPS, S, Fs), bf16), jax.ShapeDtypeStruct((N_CHIPS, S, Fs), bf16)],
        scratch_shapes=[pltpu.VMEM((ts, D), bf16), pltpu.VMEM((ts, D), f32)],
        compiler_params=_cp(("parallel", "arbitrary"), 56),
    )(x, mod3, wg, wu, wd, lng, lnb)


def _ffn_bwd(dxo, x, f, g, u, mod3, wg, wu, wd, ls, lng):
    S, D = x.shape
    Fs = wg.shape[-2]
    ts = TOK_TILE

    def body(dxo_ref, x_ref, f_ref, g_ref, u_ref, mod_ref, wg_ref, wu_ref, wd_ref, lng_ref,
             dx_ref, dg_ref, du_ref, a_ref, h_ref, df_ref, dmod_ref, dlng_ref, dlnb_ref,
             dr_sc, df_sc, acc_sc):
        i = pl.program_id(0)
        j = pl.program_id(1)

        @pl.when((i == 0) & (j == 0))
        def _():
            dmod_ref[...] = jnp.zeros_like(dmod_ref)
            dlng_ref[...] = jnp.zeros_like(dlng_ref)
            dlnb_ref[...] = jnp.zeros_like(dlnb_ref)

        @pl.when(j == 0)
        def _():
            xv = x_ref[...]
            fv = f_ref[...]
            gate = mod_ref[2:3, :]
            rh, rstd = _ln_stats(ALPHA * xv + (FFN_RES * gate) * fv)
            dy = dxo_ref[...]
            dlng_ref[...] += jnp.sum(dy * rh, 0, keepdims=True)
            dlnb_ref[...] += jnp.sum(dy, 0, keepdims=True)
            dr = _ln_bwd(dy * lng_ref[...], rh, rstd)
            dr_sc[...] = dr
            dmod_ref[2:3, :] += jnp.sum(FFN_RES * dr * fv, 0, keepdims=True)
            df = ((FFN_RES * gate) * dr).astype(bf16)
            df_sc[...] = df
            df_ref[...] = df
            xh, _ = _ln_stats(xv)
            h_ref[...] = (xh * (1.0 + mod_ref[1:2, :]) + mod_ref[0:1, :]).astype(bf16)
            acc_sc[...] = jnp.zeros_like(acc_sc)

        da = _dot_nt(df_sc[...], wd_ref[0, 0])
        gv = g_ref[0].astype(f32)
        uv = u_ref[0].astype(f32)
        sg = jax.nn.sigmoid(gv)
        si = gv * sg
        a_ref[0] = (si * uv).astype(bf16)
        dgv = (da * uv * (sg * (1.0 + gv * (1.0 - sg)))).astype(bf16)
        duv = (da * si).astype(bf16)
        dg_ref[0] = dgv
        du_ref[0] = duv
        acc_sc[...] += _dot(dgv, wg_ref[0, 0]) + _dot(duv, wu_ref[0, 0])

        @pl.when(j == N_CHIPS - 1)
        def _():
            dh = acc_sc[...]
            xh, rstd0 = _ln_stats(x_ref[...])
            dmod_ref[0:1, :] += jnp.sum(dh, 0, keepdims=True)
            dmod_ref[1:2, :] += jnp.sum(dh * xh, 0, keepdims=True)
            dx_ref[...] = _ln_bwd(dh * (1.0 + mod_ref[1:2, :]), xh, rstd0) + ALPHA * dr_sc[...]

    tok = pl.BlockSpec((ts, D), lambda i, j: (i, 0))
    wrow = pl.BlockSpec((1, 1, Fs, D), lambda i, j: (j, ls, 0, 0))
    hid = pl.BlockSpec((1, ts, Fs), lambda i, j: (j, i, 0))
    hid_shape = jax.ShapeDtypeStruct((N_CHIPS, S, Fs), bf16)
    return pl.pallas_call(
        body, name="ffn_bwd", grid=(S // ts, N_CHIPS),
        in_specs=[tok, tok, tok, hid, hid, _full((3, D)), wrow, wrow, wrow, _full((1, D))],
        out_specs=[tok, hid, hid, hid, tok, tok, _full((3, D)), _full((1, D)), _full((1, D))],
        out_shape=[jax.ShapeDtypeStruct((S, D), f32), hid_shape, hid_shape, hid_shape,
                   jax.ShapeDtypeStruct((S, D), bf16), jax.ShapeDtypeStruct((S, D), bf16),
                   jax.ShapeDtypeStruct((3, D), f32), jax.ShapeDtypeStruct((1, D), f32), jax.ShapeDtypeStruct((1, D), f32)],
        scratch_shapes=[pltpu.VMEM((ts, D), f32), pltpu.VMEM((ts, D), bf16), pltpu.VMEM((ts, D), f32)],
        compiler_params=_cp(("arbitrary", "arbitrary"), 56),
    )(dxo, x, f, g, u, mod3, wg, wu, wd, lng)


def _ffn_wgrad(h, dg, du, a, df, gwg, gwu, gwd, ls):
    S, D = h.shape
    Fs = dg.shape[-1]
    tk = TOK_TILE
    nk = S // tk

    def body(h_ref, dg_ref, du_ref, a_ref, df_ref, _g0, _g1, _g2, gwg_ref, gwu_ref, gwd_ref, ag_sc, au_sc, ad_sc):
        k = pl.program_id(1)

        @pl.when(k == 0)
        def _():
            ag_sc[...] = jnp.zeros_like(ag_sc)
            au_sc[...] = jnp.zeros_like(au_sc)
            ad_sc[...] = jnp.zeros_like(ad_sc)

        hv = h_ref[...]
        ag_sc[...] += _dot_tn(dg_ref[0], hv)
        au_sc[...] += _dot_tn(du_ref[0], hv)
        ad_sc[...] += _dot_tn(a_ref[0], df_ref[...])

        @pl.when(k == nk - 1)
        def _():
            gwg_ref[0, 0] = ag_sc[...].astype(bf16)
            gwu_ref[0, 0] = au_sc[...].astype(bf16)
            gwd_ref[0, 0] = ad_sc[...].astype(bf16)

    tok = pl.BlockSpec((tk, D), lambda p, k: (k, 0))
    hid = pl.BlockSpec((1, tk, Fs), lambda p, k: (p, k, 0))
    anyspec = pl.BlockSpec(memory_space=pl.ANY)
    orow = pl.BlockSpec((1, 1, Fs, D), lambda p, k: (p, ls, 0, 0))
    return pl.pallas_call(
        body, name="ffn_wgrad", grid=(N_CHIPS, nk),
        in_specs=[tok, hid, hid, hid, tok, anyspec, anyspec, anyspec],
        out_specs=[orow, orow, orow],
        out_shape=[jax.ShapeDtypeStruct(gwg.shape, bf16), jax.ShapeDtypeStruct(gwu.shape, bf16),
                   jax.ShapeDtypeStruct(gwd.shape, bf16)],
        scratch_shapes=[pltpu.VMEM((Fs, D), f32), pltpu.VMEM((Fs, D), f32), pltpu.VMEM((Fs, D), f32)],
        input_output_aliases={5: 0, 6: 1, 7: 2},
        compiler_params=_cp(("parallel", "arbitrary"), 48),
    )(h, dg, du, a, df, gwg, gwu, gwd)


_LANES = 128
_QKV_BLOCKS = D_ATT // _LANES


def _res_spec(lead, d, width, index):
    return pl.BlockSpec((lead, d, TOK_TILE // d, width), index)


def _res_spec3(d, width):
    return pl.BlockSpec((d, TOK_TILE // d, width), lambda i: (0, i, 0))


def _rows_to_residues(tile_bufs, d, put):
    for r in range(d):
        for cb, buf in enumerate(tile_bufs):
            put(r, cb, buf[pl.ds(r, TOK_TILE // d, stride=d), :])


def _residues_to_rows(tile_bufs, d, get):
    for r in range(d):
        for cb, buf in enumerate(tile_bufs):
            buf[pl.ds(r, TOK_TILE // d, stride=d), :] = get(r, cb)


def _mix_in_fwd(x, mod3, w_in, l):
    S, D = x.shape
    N = w_in.shape[-1]
    ts = TOK_TILE

    def body(x_ref, mod_ref, w_ref, o1_ref, o4_ref, o16_ref, zr_ref, h_ref, *bufs):
        j = pl.program_id(1)

        @pl.when(j == 0)
        def _():
            xh, _ = _ln_stats(x_ref[...])
            h_ref[...] = (xh * (1.0 + mod_ref[1:2, :]) + mod_ref[0:1, :]).astype(bf16)

        z = _dot(h_ref[...], w_ref[0, 0])

        @pl.when(j == N_CHIPS - 1)
        def _():
            zr_ref[...] = z

        @pl.when(j < N_CHIPS - 1)
        def _():
            zz = z * jnp.where(j == 0, HEAD_DIM ** -0.5, 1.0)
            o1_ref[j, 0] = zz.astype(bf16)
            for cb, buf in enumerate(bufs):
                buf[...] = zz[:, _LANES * cb:_LANES * (cb + 1)]
            for d, o_ref in zip(DILATIONS[1:], (o4_ref, o16_ref)):
                def put(r, cb, piece, o_ref=o_ref):
                    o_ref[j, r, :, _LANES * cb:_LANES * (cb + 1)] = piece.astype(bf16)
                _rows_to_residues(bufs, d, put)

    tok = pl.BlockSpec((ts, D), lambda i, j: (i, 0))
    res = [_res_spec(3, d, N, lambda i, j: (0, 0, i, 0)) for d in DILATIONS]
    return pl.pallas_call(
        body, name="mix_in_fwd", grid=(S // ts, N_CHIPS),
        in_specs=[tok, _full((3, D)), pl.BlockSpec((1, 1, D, N), lambda i, j: (j, l, 0, 0))],
        out_specs=res + [pl.BlockSpec((ts, N), lambda i, j: (i, 0)), tok],
        out_shape=[jax.ShapeDtypeStruct((3, d, S // d, N), bf16) for d in DILATIONS]
        + [jax.ShapeDtypeStruct((S, N), f32), jax.ShapeDtypeStruct((S, D), bf16)],
        scratch_shapes=[pltpu.VMEM((ts, _LANES), f32)] * _QKV_BLOCKS,
        compiler_params=_cp(("parallel", "arbitrary"), 40),
    )(x, mod3, w_in)


def _mix_in_bwd(dqkv, d_rest, dx_res, x, mod3, w_in, l):
    S, D = x.shape
    N = w_in.shape[-1]
    ts = TOK_TILE

    def body(d1_ref, d4_ref, d16_ref, dr_ref, dxr_ref, x_ref, mod_ref, w_ref, dx_ref, dmod_ref, dz_ref, acc_sc, *bufs):
        i = pl.program_id(0)
        j = pl.program_id(1)

        @pl.when((i == 0) & (j == 0))
        def _():
            dmod_ref[...] = jnp.zeros_like(dmod_ref)

        @pl.when(j == 0)
        def _():
            acc_sc[...] = jnp.zeros_like(acc_sc)

        @pl.when(j == N_CHIPS - 1)
        def _():
            dz_ref[0] = dr_ref[...]

        @pl.when(j < N_CHIPS - 1)
        def _():
            for d, d_ref, tile_bufs in ((4, d4_ref, bufs[:_QKV_BLOCKS]), (16, d16_ref, bufs[_QKV_BLOCKS:])):
                _residues_to_rows(tile_bufs, d, lambda r, cb, d_ref=d_ref: d_ref[0, r, :, _LANES * cb:_LANES * (cb + 1)].astype(f32))
            for cb in range(_QKV_BLOCKS):
                cols = slice(_LANES * cb, _LANES * (cb + 1))
                dz_ref[0, :, cols] = (d1_ref[0, 0, :, cols].astype(f32) + bufs[cb][...] + bufs[_QKV_BLOCKS + cb][...]).astype(bf16)

        acc_sc[...] += _dot_nt(dz_ref[0], w_ref[0, 0])

        @pl.when(j == N_CHIPS - 1)
        def _():
            dh = acc_sc[...]
            xh, rstd0 = _ln_stats(x_ref[...])
            dmod_ref[0:1, :] += jnp.sum(dh, 0, keepdims=True)
            dmod_ref[1:2, :] += jnp.sum(dh * xh, 0, keepdims=True)
            dx_ref[...] = _ln_bwd(dh * (1.0 + mod_ref[1:2, :]), xh, rstd0) + dxr_ref[...]

    tok = pl.BlockSpec((ts, D), lambda i, j: (i, 0))
    res = [_res_spec(1, d, N, lambda i, j: (jnp.minimum(j, 2), 0, i, 0)) for d in DILATIONS]
    return pl.pallas_call(
        body, name="mix_in_bwd", grid=(S // ts, N_CHIPS),
        in_specs=res + [pl.BlockSpec((ts, N), lambda i, j: (i, 0)), tok, tok, _full((3, D)),
                        pl.BlockSpec((1, 1, D, N), lambda i, j: (j, l, 0, 0))],
        out_specs=[tok, _full((3, D)), pl.BlockSpec((1, ts, N), lambda i, j: (j, i, 0))],
        out_shape=[jax.ShapeDtypeStruct((S, D), f32), jax.ShapeDtypeStruct((3, D), f32),
                   jax.ShapeDtypeStruct((N_CHIPS, S, N), bf16)],
        scratch_shapes=[pltpu.VMEM((ts, D), f32)] + [pltpu.VMEM((ts, _LANES), f32)] * (2 * _QKV_BLOCKS),
        compiler_params=_cp(("arbitrary", "arbitrary"), 40),
    )(*dqkv, d_rest, dx_res, x, mod3, w_in)


def _mix_in_wgrad(h, dz, gw, l):
    S, D = h.shape
    N = dz.shape[-1]
    tk = TOK_TILE
    nk = S // tk

    def body(h_ref, dz_ref, _g, gw_ref, acc_sc):
        k = pl.program_id(1)

        @pl.when(k == 0)
        def _():
            acc_sc[...] = jnp.zeros_like(acc_sc)

        acc_sc[...] += _dot_tn(h_ref[...], dz_ref[0])

        @pl.when(k == nk - 1)
        def _():
            gw_ref[0, 0] = acc_sc[...].astype(bf16)

    return pl.pallas_call(
        body, name="mix_in_wgrad", grid=(N_CHIPS, nk),
        in_specs=[pl.BlockSpec((tk, D), lambda p, k: (k, 0)), pl.BlockSpec((1, tk, N), lambda p, k: (p, k, 0)),
                  pl.BlockSpec(memory_space=pl.ANY)],
        out_specs=pl.BlockSpec((1, 1, D, N), lambda p, k: (p, l, 0, 0)),
        out_shape=jax.ShapeDtypeStruct(gw.shape, bf16),
        scratch_shapes=[pltpu.VMEM((D, N), f32)],
        input_output_aliases={2: 0},
        compiler_params=_cp(("parallel", "arbitrary"), 40),
    )(h, dz, gw)


def _mix_out_fwd(x, y_att, y_ssm, y_pool, mod3, w_out, l, lng, lnb):
    S, D = x.shape
    ts = TOK_TILE

    def body(x_ref, ya_ref, ys_ref, yp_ref, mod_ref, w_ref, lng_ref, lnb_ref, xo_ref, y_ref):
        ya = ya_ref[...].astype(bf16)
        y = (_dot(ya[:, 0:256], w_ref[0, 0]) + _dot(ya[:, 256:512], w_ref[1, 0])
             + _dot(ys_ref[...].astype(bf16), w_ref[2, 0]) + _dot(yp_ref[...].astype(bf16), w_ref[3, 0]))
        y_ref[...] = y
        rh, _ = _ln_stats(ALPHA * x_ref[...] + mod_ref[2:3, :] * y)
        xo_ref[...] = rh * lng_ref[...] + lnb_ref[...]

    tok = pl.BlockSpec((ts, D), lambda i: (i, 0))
    return pl.pallas_call(
        body, name="mix_out_fwd", grid=(S // ts,),
        in_specs=[tok, pl.BlockSpec((ts, D_ATT), lambda i: (i, 0)), pl.BlockSpec((ts, D_SSM), lambda i: (i, 0)),
                  pl.BlockSpec((ts, D_POOL), lambda i: (i, 0)), _full((3, D)),
                  pl.BlockSpec((N_CHIPS, 1, 256, D), lambda i: (0, l, 0, 0)), _full((1, D)), _full((1, D))],
        out_specs=[tok, tok],
        out_shape=[jax.ShapeDtypeStruct((S, D), f32), jax.ShapeDtypeStruct((S, D), f32)],
        compiler_params=_cp(("parallel",), 40),
    )(x, y_att, y_ssm, y_pool, mod3, w_out, lng, lnb)


def _mix_out_bwd(dxo, x, y, y_att, y_ssm, y_pool, mod3, w_out, l, lng, gw_out):
    S, D = x.shape
    ts = TOK_TILE
    nt = S // ts

    def body(dxo_ref, x_ref, y_ref, ya_ref, ys_ref, yp_ref, mod_ref, w_ref, lng_ref, _g,
             dxr_ref, da_ref, ds_ref, dp_ref, dgate_ref, dlng_ref, dlnb_ref, gw_ref, acc_sc):
        i = pl.program_id(0)

        @pl.when(i == 0)
        def _():
            dgate_ref[...] = jnp.zeros_like(dgate_ref)
            dlng_ref[...] = jnp.zeros_like(dlng_ref)
            dlnb_ref[...] = jnp.zeros_like(dlnb_ref)
            acc_sc[...] = jnp.zeros_like(acc_sc)

        gate = mod_ref[2:3, :]
        yv = y_ref[...]
        rh, rstd = _ln_stats(ALPHA * x_ref[...] + gate * yv)
        dy_out = dxo_ref[...]
        dlng_ref[...] += jnp.sum(dy_out * rh, 0, keepdims=True)
        dlnb_ref[...] += jnp.sum(dy_out, 0, keepdims=True)
        dr = _ln_bwd(dy_out * lng_ref[...], rh, rstd)
        dxr_ref[...] = ALPHA * dr
        dgate_ref[...] += jnp.sum(dr * yv, 0, keepdims=True)
        dy = (gate * dr).astype(bf16)
        da_ref[:, 0:256] = _dot_nt(dy, w_ref[0, 0])
        da_ref[:, 256:512] = _dot_nt(dy, w_ref[1, 0])
        ds_ref[...] = _dot_nt(dy, w_ref[2, 0])
        dp_ref[...] = _dot_nt(dy, w_ref[3, 0])
        ya = ya_ref[...].astype(bf16)
        acc_sc[0] += _dot_tn(ya[:, 0:256], dy)
        acc_sc[1] += _dot_tn(ya[:, 256:512], dy)
        acc_sc[2] += _dot_tn(ys_ref[...].astype(bf16), dy)
        acc_sc[3] += _dot_tn(yp_ref[...].astype(bf16), dy)

        @pl.when(i == nt - 1)
        def _():
            gw_ref[:, 0] = acc_sc[...].astype(bf16)

    tok = pl.BlockSpec((ts, D), lambda i: (i, 0))
    t512 = pl.BlockSpec((ts, D_ATT), lambda i: (i, 0))
    t256 = pl.BlockSpec((ts, 256), lambda i: (i, 0))
    wspec = pl.BlockSpec((N_CHIPS, 1, 256, D), lambda i: (0, l, 0, 0))
    return pl.pallas_call(
        body, name="mix_out_bwd", grid=(nt,),
        in_specs=[tok, tok, tok, t512, t256, t256, _full((3, D)), wspec, _full((1, D)), pl.BlockSpec(memory_space=pl.ANY)],
        out_specs=[tok, t512, t256, t256, _full((1, D)), _full((1, D)), _full((1, D)), wspec],
        out_shape=[jax.ShapeDtypeStruct((S, D), f32), jax.ShapeDtypeStruct((S, D_ATT), f32),
                   jax.ShapeDtypeStruct((S, D_SSM), f32), jax.ShapeDtypeStruct((S, D_POOL), f32),
                   jax.ShapeDtypeStruct((1, D), f32), jax.ShapeDtypeStruct((1, D), f32), jax.ShapeDtypeStruct((1, D), f32),
                   jax.ShapeDtypeStruct(gw_out.shape, bf16)],
        scratch_shapes=[pltpu.VMEM((N_CHIPS, 256, D), f32)],
        input_output_aliases={9: 7},
        compiler_params=_cp(("arbitrary",), 48),
    )(dxo, x, y, y_att, y_ssm, y_pool, mod3, w_out, lng, gw_out)


def _t5_bucket(dist):
    max_exact = N_BUCKETS // 2
    d = np.maximum(dist, 1).astype(np.float32)
    large = max_exact + (np.log(d / max_exact) / math.log(MAX_DISTANCE / max_exact)
                         * (N_BUCKETS - max_exact)).astype(np.int32)
    large = np.minimum(large, N_BUCKETS - 1)
    return np.where(dist < max_exact, dist, large).astype(np.int32)


def _bucket_table():
    q = ATT_BLOCK
    i = np.arange(q)[:, None]
    j = np.arange(2 * q)[None, :]
    r = i + q - j
    in_band = (r >= 0) & (r <= q)
    tabs = [np.where(in_band, _t5_bucket(np.clip(r, 0, None) * d), -1) for d in DILATIONS]
    return np.stack(tabs).astype(np.int32)


def _bias_fwd(rel_bias, table):
    def body(rb_ref, tab_ref, out_ref):
        for b in range(3):
            tb = tab_ref[b]
            for h in range(N_HEADS):
                def pick(k, acc):
                    return jnp.where(tb == k, rb_ref[k, h], acc)
                out_ref[b, h] = lax.fori_loop(0, N_BUCKETS, pick, jnp.where(tb < 0, NEG, 0.0).astype(f32))

    return pl.pallas_call(
        body, name="bias_fwd",
        in_specs=[pl.BlockSpec(memory_space=pltpu.SMEM), pl.BlockSpec(memory_space=pltpu.VMEM)],
        out_specs=pl.BlockSpec(memory_space=pltpu.VMEM),
        out_shape=jax.ShapeDtypeStruct((3, N_HEADS, ATT_BLOCK, 2 * ATT_BLOCK), f32),
    )(rel_bias, table)


def _bias_bwd(dbias, table):
    def body(db_ref, tab_ref, out_ref):
        def per_bucket(k, c):
            for h in range(N_HEADS):
                tot = jnp.zeros((), f32)
                for b in range(3):
                    tot = tot + jnp.sum(jnp.where(tab_ref[b] == k, db_ref[b, h], 0.0))
                out_ref[k, h] = tot
            return c
        lax.fori_loop(0, N_BUCKETS, per_bucket, 0)

    return pl.pallas_call(
        body, name="bias_bwd",
        in_specs=[pl.BlockSpec(memory_space=pltpu.VMEM), pl.BlockSpec(memory_space=pltpu.VMEM)],
        out_specs=pl.BlockSpec(memory_space=pltpu.SMEM),
        out_shape=jax.ShapeDtypeStruct((N_BUCKETS, N_HEADS), f32),
    )(dbias, table)


def _att_unit(u, nbr):
    rows = pl.ds(pl.multiple_of(u * ATT_BLOCK, ATT_BLOCK), ATT_BLOCK)
    prev = pl.ds(pl.multiple_of(jnp.maximum(u - 1, 0) * ATT_BLOCK, ATT_BLOCK), ATT_BLOCK)
    return rows, prev, (u % nbr) != 0


_HEAD_ROWS = N_HEADS * ATT_BLOCK


def _head_rows(t):
    lane = lax.broadcasted_iota(jnp.int32, t.shape, 1)
    return jnp.concatenate([jnp.where((lane >= HEAD_DIM * h) & (lane < HEAD_DIM * (h + 1)), t, jnp.zeros_like(t))
                            for h in range(N_HEADS)], axis=0)


def _head_cols(big):
    lane = lax.broadcasted_iota(jnp.int32, (ATT_BLOCK, D_ATT), 1)
    out = big[0:ATT_BLOCK]
    for h in range(1, N_HEADS):
        out = jnp.where(lane >= HEAD_DIM * h, big[ATT_BLOCK * h:ATT_BLOCK * (h + 1)], out)
    return out


def _head_column(ref, rows):
    t = ref[rows, :]
    return jnp.concatenate([t[:, h:h + 1] for h in range(N_HEADS)], axis=0)


def _att_band(ref, rows, prev, nbr):
    cur = ref[0, rows, :]
    return cur if nbr == 1 else jnp.concatenate([ref[0, prev, :], cur], axis=0)


def _att_scores(q_ref, k_ref, b_ref, rows, prev, valid_prev, nbr):
    qbd = _head_rows(q_ref[0, rows, :])
    kb = _att_band(k_ref, rows, prev, nbr)
    bias = b_ref[0].reshape(_HEAD_ROWS, 2 * ATT_BLOCK)
    if nbr == 1:
        return qbd, kb, _dot_nt(qbd, kb) + bias[:, ATT_BLOCK:]
    s = _dot_nt(qbd, kb) + bias
    col = lax.broadcasted_iota(jnp.int32, s.shape, 1)
    return qbd, kb, jnp.where((col >= ATT_BLOCK) | valid_prev, s, NEG)


def _qkv_specs(S, branch):
    return ([pl.BlockSpec((1, S, D_ATT), lambda i, t=t: (t, 0, 0)) for t in range(3)],
            pl.BlockSpec((1, N_HEADS, ATT_BLOCK, 2 * ATT_BLOCK), lambda i: (branch, 0, 0, 0)))


def _att_fwd(qkv, bias, branch):
    S = qkv.shape[1]
    nbr = BLOCKS_PER_RESIDUE[branch]

    def body(q_ref, k_ref, v_ref, b_ref, o_ref, lse_ref):
        lse_ref[...] = jnp.zeros_like(lse_ref)

        def unit(u, c):
            rows, prev, valid_prev = _att_unit(u, nbr)
            _, _, s = _att_scores(q_ref, k_ref, b_ref, rows, prev, valid_prev, nbr)
            m = jnp.max(s, -1, keepdims=True)
            p = jnp.exp(s - m)
            den = jnp.sum(p, -1, keepdims=True)
            big = _dot(p.astype(bf16), _att_band(v_ref, rows, prev, nbr))
            o_ref[rows, :] = _head_cols(big / den)
            lse = m + jnp.log(den)
            for h in range(N_HEADS):
                lse_ref[rows, pl.ds(h, 1)] = lse[ATT_BLOCK * h:ATT_BLOCK * (h + 1)]
            return c

        lax.fori_loop(0, N_UNITS, unit, 0)

    qkv_specs, bspec = _qkv_specs(S, branch)
    return pl.pallas_call(
        body, name="att_fwd", grid=(1,),
        in_specs=qkv_specs + [bspec],
        out_specs=[pl.BlockSpec((S, D_ATT), lambda i: (0, 0)), pl.BlockSpec((S, _LANES), lambda i: (0, 0))],
        out_shape=[jax.ShapeDtypeStruct((S, D_ATT), f32), jax.ShapeDtypeStruct((S, _LANES), f32)],
        compiler_params=_cp(("arbitrary",), 40),
    )(qkv, qkv, qkv, bias)


def _att_bwd(qkv, do, lse, crow, bias, branch):
    S = qkv.shape[1]
    nbr = BLOCKS_PER_RESIDUE[branch]

    def body(q_ref, k_ref, v_ref, do_ref, lse_ref, c_ref, b_ref, dqkv_ref, db_ref, dk_sc, dv_sc):
        dk_sc[...] = jnp.zeros_like(dk_sc)
        dv_sc[...] = jnp.zeros_like(dv_sc)
        db_ref[...] = jnp.zeros_like(db_ref)

        def unit(u, c):
            rows, prev, valid_prev = _att_unit(u, nbr)
            qbd, kb, s = _att_scores(q_ref, k_ref, b_ref, rows, prev, valid_prev, nbr)
            p = jnp.exp(s - _head_column(lse_ref, rows))
            dobd = _head_rows(do_ref[rows, :])
            ds = p * (_dot_nt(dobd, _att_band(v_ref, rows, prev, nbr)) - _head_column(c_ref, rows))
            if nbr == 1:
                db_ref[:, :, ATT_BLOCK:] += ds.reshape(N_HEADS, ATT_BLOCK, ATT_BLOCK)
            else:
                db_ref[...] += ds.reshape(N_HEADS, ATT_BLOCK, 2 * ATT_BLOCK)
            dsb = ds.astype(bf16)
            dqkv_ref[0, rows, :] = (HEAD_DIM ** -0.5 * _head_cols(_dot(dsb, kb))).astype(bf16)
            dkb = _dot_tn(dsb, qbd)
            dvb = _dot_tn(p.astype(bf16), dobd)
            if nbr == 1:
                dk_sc[rows, :] += dkb
                dv_sc[rows, :] += dvb
            else:
                dk_sc[prev, :] += dkb[:ATT_BLOCK]
                dv_sc[prev, :] += dvb[:ATT_BLOCK]
                dk_sc[rows, :] += dkb[ATT_BLOCK:]
                dv_sc[rows, :] += dvb[ATT_BLOCK:]
            return c

        lax.fori_loop(0, N_UNITS, unit, 0)
        dqkv_ref[1] = dk_sc[...].astype(bf16)
        dqkv_ref[2] = dv_sc[...].astype(bf16)

    qkv_specs, bspec = _qkv_specs(S, branch)
    row = pl.BlockSpec((S, _LANES), lambda i: (0, 0))
    return pl.pallas_call(
        body, name="att_bwd", grid=(1,),
        in_specs=qkv_specs + [pl.BlockSpec((S, D_ATT), lambda i: (0, 0)), row, row, bspec],
        out_specs=[pl.BlockSpec((3, S, D_ATT), lambda i: (0, 0, 0)),
                   pl.BlockSpec((N_HEADS, ATT_BLOCK, 2 * ATT_BLOCK), lambda i: (0, 0, 0))],
        out_shape=[jax.ShapeDtypeStruct((3, S, D_ATT), bf16), jax.ShapeDtypeStruct((N_HEADS, ATT_BLOCK, 2 * ATT_BLOCK), f32)],
        scratch_shapes=[pltpu.VMEM((S, D_ATT), f32), pltpu.VMEM((S, D_ATT), f32)],
        compiler_params=_cp(("arbitrary",), 48),
    )(qkv, qkv, qkv, do, lse, crow, bias)


def _branch_weights(lse_ref):
    l0, l1, l2 = lse_ref[0], lse_ref[1], lse_ref[2]
    m = jnp.maximum(jnp.maximum(l0, l1), l2)
    e0, e1, e2 = jnp.exp(l0 - m), jnp.exp(l1 - m), jnp.exp(l2 - m)
    tot = e0 + e1 + e2
    return e0 / tot, e1 / tot, e2 / tot


def _att_merge(os, lses):
    S = os[0].shape[0] * os[0].shape[1]
    ts = TOK_TILE

    def body(o1_ref, o4_ref, o16_ref, l1_ref, l4_ref, l16_ref, y_ref, lt_ref, *bufs):
        obufs = (bufs[:_QKV_BLOCKS], bufs[_QKV_BLOCKS:2 * _QKV_BLOCKS])
        lt_ref[0] = l1_ref[0]
        for k, (d, o_ref, l_ref) in enumerate(((4, o4_ref, l4_ref), (16, o16_ref, l16_ref))):
            _residues_to_rows(obufs[k], d, lambda r, cb, o_ref=o_ref: o_ref[r, :, _LANES * cb:_LANES * (cb + 1)])
            _residues_to_rows([bufs[2 * _QKV_BLOCKS + k]], d, lambda r, cb, l_ref=l_ref: l_ref[r])
            lt_ref[1 + k] = bufs[2 * _QKV_BLOCKS + k][...]
        w = _branch_weights(lt_ref)
        for h in range(N_HEADS):
            cs = slice(HEAD_DIM * h, HEAD_DIM * (h + 1))
            half = slice(HEAD_DIM * (h % 2), HEAD_DIM * (h % 2 + 1))
            y_ref[:, cs] = (w[0][:, h:h + 1] * o1_ref[0, :, cs] + w[1][:, h:h + 1] * obufs[0][h // 2][:, half]
                            + w[2][:, h:h + 1] * obufs[1][h // 2][:, half])

    return pl.pallas_call(
        body, name="att_merge", grid=(S // ts,),
        in_specs=[_res_spec3(d, D_ATT) for d in DILATIONS] + [_res_spec3(d, _LANES) for d in DILATIONS],
        out_specs=[pl.BlockSpec((ts, D_ATT), lambda i: (i, 0)), pl.BlockSpec((3, ts, _LANES), lambda i: (0, i, 0))],
        out_shape=[jax.ShapeDtypeStruct((S, D_ATT), f32), jax.ShapeDtypeStruct((3, S, _LANES), f32)],
        scratch_shapes=[pltpu.VMEM((ts, _LANES), f32)] * (2 * _QKV_BLOCKS + 2),
        compiler_params=_cp(("parallel",)),
    )(*os, *lses)


def _att_merge_bwd(dy, y, lse3):
    S = dy.shape[0]
    ts = TOK_TILE

    def body(dy_ref, y_ref, lse_ref, do1_ref, do4_ref, do16_ref, c1_ref, c4_ref, c16_ref, *bufs):
        dobufs = (bufs[:_QKV_BLOCKS], bufs[_QKV_BLOCKS:2 * _QKV_BLOCKS], bufs[2 * _QKV_BLOCKS:3 * _QKV_BLOCKS])
        cbufs = bufs[3 * _QKV_BLOCKS:]
        w = _branch_weights(lse_ref)
        for cb in cbufs:
            cb[...] = jnp.zeros_like(cb)
        for h in range(N_HEADS):
            cs = slice(HEAD_DIM * h, HEAD_DIM * (h + 1))
            half = slice(HEAD_DIM * (h % 2), HEAD_DIM * (h % 2 + 1))
            dyh = dy_ref[:, cs]
            t = jnp.sum(dyh * y_ref[:, cs], -1, keepdims=True)
            for p in range(3):
                wp = w[p][:, h:h + 1]
                dobufs[p][h // 2][:, half] = wp * dyh
                cbufs[p][:, h:h + 1] = wp * t
        for cb in range(_QKV_BLOCKS):
            do1_ref[0, :, _LANES * cb:_LANES * (cb + 1)] = dobufs[0][cb][...].astype(bf16)
        c1_ref[0] = cbufs[0][...]
        for k, (d, do_ref, c_ref) in enumerate(((4, do4_ref, c4_ref), (16, do16_ref, c16_ref))):
            def put_do(r, cb, piece, do_ref=do_ref):
                do_ref[r, :, _LANES * cb:_LANES * (cb + 1)] = piece.astype(bf16)

            def put_c(r, cb, piece, c_ref=c_ref):
                c_ref[r] = piece

            _rows_to_residues(dobufs[1 + k], d, put_do)
            _rows_to_residues([cbufs[1 + k]], d, put_c)

    return pl.pallas_call(
        body, name="att_merge_bwd", grid=(S // ts,),
        in_specs=[pl.BlockSpec((ts, D_ATT), lambda i: (i, 0)), pl.BlockSpec((ts, D_ATT), lambda i: (i, 0)),
                  pl.BlockSpec((3, ts, _LANES), lambda i: (0, i, 0))],
        out_specs=[_res_spec3(d, D_ATT) for d in DILATIONS] + [_res_spec3(d, _LANES) for d in DILATIONS],
        out_shape=[jax.ShapeDtypeStruct((d, S // d, D_ATT), bf16) for d in DILATIONS]
        + [jax.ShapeDtypeStruct((d, S // d, _LANES), f32) for d in DILATIONS],
        scratch_shapes=[pltpu.VMEM((ts, _LANES), f32)] * (3 * _QKV_BLOCKS + 3),
        compiler_params=_cp(("parallel",)),
    )(dy, y, lse3)


def _ssm_scan(xr, xi, a2, reverse):
    S, N = xr.shape
    nst = S // SCAN_SEG

    def body(xr_ref, xi_ref, a_ref, sr_ref, si_ref):
        ar = jnp.broadcast_to(a_ref[0:1, :], (SCAN_SEG, N))
        ai = jnp.broadcast_to(a_ref[1:2, :], (SCAN_SEG, N))
        if reverse:
            ai = -ai
        row = lax.broadcasted_iota(jnp.int32, (SCAN_SEG, N), 0)
        zero = jnp.zeros((SCAN_SEG, N), f32)

        def tile(t):
            return pl.ds(pl.multiple_of((nst - 1 - t if reverse else t) * SCAN_SEG, SCAN_SEG), SCAN_SEG)

        def local(t, c):
            sr, si, pr, pi = c
            rows = tile(t)
            nsr = ar * sr - ai * si + xr_ref[rows, :]
            nsi = ar * si + ai * sr + xi_ref[rows, :]
            sr_ref[rows, :] = nsr
            si_ref[rows, :] = nsi
            return nsr, nsi, ar * pr - ai * pi, ar * pi + ai * pr

        fr, fi, apr, api = lax.fori_loop(0, nst, local, (zero, zero, zero + 1.0, zero))

        def shift(v):
            if reverse:
                return jnp.where(row == SCAN_SEG - 1, 0.0, pltpu.roll(v, SCAN_SEG - 1, axis=0))
            return jnp.where(row == 0, 0.0, pltpu.roll(v, 1, axis=0))

        cr, ci = zero, zero
        for _ in range(SCAN_SEG - 1):
            cr, ci = shift(fr + apr * cr - api * ci), shift(fi + apr * ci + api * cr)

        def fix(t, c):
            pr, pi = c
            npr, npi = ar * pr - ai * pi, ar * pi + ai * pr
            rows = tile(t)
            sr_ref[rows, :] += npr * cr - npi * ci
            si_ref[rows, :] += npr * ci + npi * cr
            return npr, npi

        lax.fori_loop(0, nst, fix, (zero + 1.0, zero))

    vm = pl.BlockSpec(memory_space=pltpu.VMEM)
    return pl.pallas_call(
        body, name="ssm_scan_rev" if reverse else "ssm_scan",
        in_specs=[vm, vm, vm], out_specs=[vm, vm],
        out_shape=[jax.ShapeDtypeStruct((S, N), f32), jax.ShapeDtypeStruct((S, N), f32)],
        compiler_params=_cp(None, 48),
    )(xr, xi, a2)


def _ssm_in(u, bre, bim):
    S = u.shape[0]
    ts = TOK_TILE

    def body(u_ref, br_ref, bi_ref, or_ref, oi_ref):
        ub = u_ref[...].astype(bf16)
        or_ref[...] = _dot(ub, br_ref[...].astype(bf16))
        oi_ref[...] = _dot(ub, bi_ref[...].astype(bf16))

    return pl.pallas_call(
        body, name="ssm_in", grid=(S // ts,),
        in_specs=[pl.BlockSpec((ts, D_SSM), lambda i: (i, 0)), _full((D_SSM, D_STATE)), _full((D_SSM, D_STATE))],
        out_specs=[pl.BlockSpec((ts, D_STATE), lambda i: (i, 0))] * 2,
        out_shape=[jax.ShapeDtypeStruct((S, D_STATE), f32)] * 2,
        compiler_params=_cp(("parallel",)),
    )(u, bre, bim)


def _ssm_out(sr, si, u, cre, cim, dskip, glu_w, glu_b):
    S = u.shape[0]
    ts = TOK_TILE

    def body(sr_ref, si_ref, u_ref, cr_ref, ci_ref, d_ref, w_ref, b_ref, out_ref, y_ref):
        y = (_dot(sr_ref[...].astype(bf16), cr_ref[...].astype(bf16))
             - _dot(si_ref[...].astype(bf16), ci_ref[...].astype(bf16)) + d_ref[...] * u_ref[...])
        y_ref[...] = y
        z = _dot(_gelu(y).astype(bf16), w_ref[...].astype(bf16)) + b_ref[...]
        out_ref[...] = y * jax.nn.sigmoid(z)

    st = pl.BlockSpec((ts, D_STATE), lambda i: (i, 0))
    ch = pl.BlockSpec((ts, D_SSM), lambda i: (i, 0))
    return pl.pallas_call(
        body, name="ssm_out", grid=(S // ts,),
        in_specs=[st, st, ch, _full((D_STATE, D_SSM)), _full((D_STATE, D_SSM)), _full((1, D_SSM)),
                  _full((D_SSM, D_SSM)), _full((1, D_SSM))],
        out_specs=[ch, ch],
        out_shape=[jax.ShapeDtypeStruct((S, D_SSM), f32)] * 2,
        compiler_params=_cp(("parallel",)),
    )(sr, si, u, cre, cim, dskip, glu_w, glu_b)


def _ssm_out_bwd(dout, y, u, sr, si, cre, cim, dskip, glu_w, glu_b):
    S = u.shape[0]
    ts = TOK_TILE

    def body(do_ref, y_ref, u_ref, sr_ref, si_ref, cr_ref, ci_ref, d_ref, w_ref, b_ref,
             gr_ref, gi_ref, du_ref, dcr_ref, dci_ref, dd_ref, dgb_ref, dgw_ref):
        @pl.when(pl.program_id(0) == 0)
        def _():
            for r in (dcr_ref, dci_ref, dd_ref, dgb_ref, dgw_ref):
                r[...] = jnp.zeros_like(r)

        y = y_ref[...]
        dout = do_ref[...]
        wb = w_ref[...].astype(bf16)
        ge = _gelu(y).astype(bf16)
        sz = jax.nn.sigmoid(_dot(ge, wb) + b_ref[...])
        dz = dout * y * sz * (1.0 - sz)
        dzb = dz.astype(bf16)
        dgb_ref[...] += jnp.sum(dz, 0, keepdims=True)
        dgw_ref[...] += _dot_tn(ge, dzb)
        dy = dout * sz + _gelu_grad(y) * _dot_nt(dzb, wb)
        uv = u_ref[...]
        dd_ref[...] += jnp.sum(dy * uv, 0, keepdims=True)
        du_ref[...] = dy * d_ref[...]
        dyb = dy.astype(bf16)
        gr_ref[...] = _dot_nt(dyb, cr_ref[...].astype(bf16))
        gi_ref[...] = -_dot_nt(dyb, ci_ref[...].astype(bf16))
        dcr_ref[...] += _dot_tn(sr_ref[...].astype(bf16), dyb)
        dci_ref[...] -= _dot_tn(si_ref[...].astype(bf16), dyb)

    st = pl.BlockSpec((ts, D_STATE), lambda i: (i, 0))
    ch = pl.BlockSpec((ts, D_SSM), lambda i: (i, 0))
    c_full = _full((D_STATE, D_SSM))
    return pl.pallas_call(
        body, name="ssm_out_bwd", grid=(S // ts,),
        in_specs=[ch, ch, ch, st, st, c_full, c_full, _full((1, D_SSM)), _full((D_SSM, D_SSM)), _full((1, D_SSM))],
        out_specs=[st, st, ch, c_full, c_full, _full((1, D_SSM)), _full((1, D_SSM)), _full((D_SSM, D_SSM))],
        out_shape=[jax.ShapeDtypeStruct((S, D_STATE), f32), jax.ShapeDtypeStruct((S, D_STATE), f32),
                   jax.ShapeDtypeStruct((S, D_SSM), f32), jax.ShapeDtypeStruct((D_STATE, D_SSM), f32),
                   jax.ShapeDtypeStruct((D_STATE, D_SSM), f32), jax.ShapeDtypeStruct((1, D_SSM), f32),
                   jax.ShapeDtypeStruct((1, D_SSM), f32), jax.ShapeDtypeStruct((D_SSM, D_SSM), f32)],
        compiler_params=_cp(("arbitrary",), 40),
    )(dout, y, u, sr, si, cre, cim, dskip, glu_w, glu_b)


def _ssm_in_bwd(lr, li, u, du_skip, bre, bim):
    S = u.shape[0]
    ts = TOK_TILE

    def body(lr_ref, li_ref, u_ref, dus_ref, br_ref, bi_ref, du_ref, dbr_ref, dbi_ref):
        @pl.when(pl.program_id(0) == 0)
        def _():
            dbr_ref[...] = jnp.zeros_like(dbr_ref)
            dbi_ref[...] = jnp.zeros_like(dbi_ref)

        lrb = lr_ref[...].astype(bf16)
        lib = li_ref[...].astype(bf16)
        du_ref[...] = dus_ref[...] + _dot_nt(lrb, br_ref[...].astype(bf16)) + _dot_nt(lib, bi_ref[...].astype(bf16))
        ub = u_ref[...].astype(bf16)
        dbr_ref[...] += _dot_tn(ub, lrb)
        dbi_ref[...] += _dot_tn(ub, lib)

    st = pl.BlockSpec((ts, D_STATE), lambda i: (i, 0))
    ch = pl.BlockSpec((ts, D_SSM), lambda i: (i, 0))
    b_full = _full((D_SSM, D_STATE))
    return pl.pallas_call(
        body, name="ssm_in_bwd", grid=(S // ts,),
        in_specs=[st, st, ch, ch, b_full, b_full],
        out_specs=[ch, b_full, b_full],
        out_shape=[jax.ShapeDtypeStruct((S, D_SSM), f32), jax.ShapeDtypeStruct((D_SSM, D_STATE), f32),
                   jax.ShapeDtypeStruct((D_SSM, D_STATE), f32)],
        compiler_params=_cp(("arbitrary",), 40),
    )(lr, li, u, du_skip, bre, bim)


def _ssm_da(lr, li, sr, si):
    S, N = lr.shape
    nst = S // SCAN_SEG

    def body(lr_ref, li_ref, sr_ref, si_ref, out_ref):
        row = lax.broadcasted_iota(jnp.int32, (SCAN_SEG, N), 0)
        last = pl.ds((nst - 1) * SCAN_SEG, SCAN_SEG)
        pr = jnp.where(row == 0, 0.0, pltpu.roll(sr_ref[last, :], 1, axis=0))
        pi = jnp.where(row == 0, 0.0, pltpu.roll(si_ref[last, :], 1, axis=0))
        first = pl.ds(0, SCAN_SEG)
        acc_r = lr_ref[first, :] * pr + li_ref[first, :] * pi
        acc_i = li_ref[first, :] * pr - lr_ref[first, :] * pi

        def step(t, c):
            acc_r, acc_i = c
            rows = pl.ds(pl.multiple_of(t * SCAN_SEG, SCAN_SEG), SCAN_SEG)
            prev = pl.ds(pl.multiple_of((t - 1) * SCAN_SEG, SCAN_SEG), SCAN_SEG)
            lrv, liv, srv, siv = lr_ref[rows, :], li_ref[rows, :], sr_ref[prev, :], si_ref[prev, :]
            return acc_r + lrv * srv + liv * siv, acc_i + liv * srv - lrv * siv

        acc_r, acc_i = lax.fori_loop(1, nst, step, (acc_r, acc_i))
        out_ref[0:1, :] = jnp.sum(acc_r, 0, keepdims=True)
        out_ref[1:2, :] = jnp.sum(acc_i, 0, keepdims=True)

    vm = pl.BlockSpec(memory_space=pltpu.VMEM)
    return pl.pallas_call(
        body, name="ssm_da", in_specs=[vm] * 4, out_specs=vm,
        out_shape=jax.ShapeDtypeStruct((2, N), f32),
        compiler_params=_cp(None, 48),
    )(lr, li, sr, si)


_POOL_TILE = 256


def _window_sums(xt, back):
    n = xt.shape[0]
    out = []
    ws = xt
    for k in (1, 2, 4, 8):
        ws = ws + pltpu.roll(ws, k if back else n - k, axis=0)
        out.append(ws)
    return out


def _pool_count(r0, w):
    t = r0 + lax.broadcasted_iota(jnp.int32, (_POOL_TILE, POOL_GROUP), 0)
    return jnp.minimum(t + 1, w).astype(f32)


def _pool_fwd(u_pad, pool_w, pool_scale):
    S = u_pad.shape[0] - POOL_HALO
    nt = S // _POOL_TILE

    def body(u_ref, w_ref, sc_ref, y_ref):
        def tile(t, c):
            r0 = pl.multiple_of(t * _POOL_TILE, _POOL_TILE)
            for g, w in enumerate(POOL_WINDOWS):
                cs = pl.ds(POOL_GROUP * g, POOL_GROUP)
                xt = u_ref[pl.ds(r0, _POOL_TILE + POOL_HALO), cs]
                ws = _window_sums(xt, True)[g][POOL_HALO:, :]
                pooled = ws / _pool_count(r0, w) - xt[POOL_HALO:, :]
                y_ref[pl.ds(r0, _POOL_TILE), cs] = _dot(pooled.astype(bf16), w_ref[g].astype(bf16)) * sc_ref[:, cs]
            return c
        lax.fori_loop(0, nt, tile, 0)

    vm = pl.BlockSpec(memory_space=pltpu.VMEM)
    return pl.pallas_call(
        body, name="pool_fwd", in_specs=[vm, vm, vm], out_specs=vm,
        out_shape=jax.ShapeDtypeStruct((S, D_POOL), f32),
    )(u_pad, pool_w, pool_scale)


def _pool_bwd(dy_pad, u_pad, pool_w, pool_scale):
    S = u_pad.shape[0] - POOL_HALO
    nt = S // _POOL_TILE
    n = _POOL_TILE + POOL_HALO

    def body(dy_ref, u_ref, w_ref, sc_ref, du_ref, dw_ref, dsc_ref):
        dw_ref[...] = jnp.zeros_like(dw_ref)
        dsc_ref[...] = jnp.zeros_like(dsc_ref)

        def tile(t, c):
            r0 = pl.multiple_of(t * _POOL_TILE, _POOL_TILE)
            for g, w in enumerate(POOL_WINDOWS):
                cs = pl.ds(POOL_GROUP * g, POOL_GROUP)
                wb = w_ref[g].astype(bf16)
                xt = u_ref[pl.ds(r0, n), cs]
                pooled = (_window_sums(xt, True)[g][POOL_HALO:, :] / _pool_count(r0, w) - xt[POOL_HALO:, :]).astype(bf16)
                dy = dy_ref[pl.ds(r0, _POOL_TILE), cs]
                dsc_ref[:, cs] += jnp.sum(dy * _dot(pooled, wb), 0, keepdims=True)
                dw_ref[g] += _dot_tn(pooled, (dy * sc_ref[:, cs]).astype(bf16))
                dyh = (dy_ref[pl.ds(r0, n), cs] * sc_ref[:, cs]).astype(bf16)
                dpl = _dot_nt(dyh, wb)
                cnt = jnp.minimum(r0 + lax.broadcasted_iota(jnp.int32, (n, POOL_GROUP), 0) + 1, w).astype(f32)
                lead = _window_sums(dpl / cnt, False)[g]
                du_ref[pl.ds(r0, _POOL_TILE), cs] = lead[:_POOL_TILE, :] - dpl[:_POOL_TILE, :]
            return c
        lax.fori_loop(0, nt, tile, 0)

    vm = pl.BlockSpec(memory_space=pltpu.VMEM)
    return pl.pallas_call(
        body, name="pool_bwd", in_specs=[vm, vm, vm, vm], out_specs=[vm, vm, vm],
        out_shape=[jax.ShapeDtypeStruct((S, D_POOL), f32), jax.ShapeDtypeStruct((4, POOL_GROUP, POOL_GROUP), f32),
                   jax.ShapeDtypeStruct((1, D_POOL), f32)],
    )(dy_pad, u_pad, pool_w, pool_scale)


def _loss_head(y, target):
    S, D = y.shape
    ts = TOK_TILE

    def body(y_ref, t_ref, loss_ref, dy_ref):
        @pl.when(pl.program_id(0) == 0)
        def _():
            loss_ref[...] = jnp.zeros_like(loss_ref)

        d = y_ref[...] - t_ref[...]
        dy_ref[...] = d * (1.0 / D)
        loss_ref[...] += 0.5 * jnp.sum(jnp.sum(d * d, -1, keepdims=True) * (1.0 / D), 0, keepdims=True)

    tok = pl.BlockSpec((ts, D), lambda i: (i, 0))
    return pl.pallas_call(
        body, name="loss_head", grid=(S // ts,),
        in_specs=[tok, tok], out_specs=[_full((1, 1)), tok],
        out_shape=[jax.ShapeDtypeStruct((1, 1), f32), jax.ShapeDtypeStruct((S, D), f32)],
        compiler_params=_cp(("arbitrary",)),
    )(y, target)


_ADA_COLS = 768


def _ada_fwd(c_all, ada_w, ada_b_cols):
    L, D, N = ada_w.shape
    B = c_all.shape[0]

    def body(c_ref, w_ref, b_ref, out_ref):
        cv = c_ref[...]
        cond = (cv * jax.nn.sigmoid(cv)).astype(bf16)
        out_ref[0] = _dot(cond, w_ref[0].astype(bf16)) + b_ref[0]

    return pl.pallas_call(
        body, name="ada_fwd", grid=(L, N // _ADA_COLS),
        in_specs=[_full((B, D)), pl.BlockSpec((1, D, _ADA_COLS), lambda l, j: (l, 0, j)),
                  pl.BlockSpec((1, 1, _ADA_COLS), lambda l, j: (l, 0, j))],
        out_specs=pl.BlockSpec((1, B, _ADA_COLS), lambda l, j: (l, 0, j)),
        out_shape=jax.ShapeDtypeStruct((L, B, N), f32),
        compiler_params=_cp(("parallel", "parallel")),
    )(c_all, ada_w, ada_b_cols)


def _ada_wgrad(c_all_t, dmod_cols):
    D, B = c_all_t.shape
    L, _, N = dmod_cols.shape

    def body(ct_ref, dm_ref, out_ref):
        cv = ct_ref[...]
        cond = cv * jax.nn.sigmoid(cv)
        acc = cond[:, 0:1] * dm_ref[0, 0:1, :]
        for b in range(1, B):
            acc = acc + cond[:, b:b + 1] * dm_ref[0, b:b + 1, :]
        out_ref[0] = acc

    return pl.pallas_call(
        body, name="ada_wgrad", grid=(L, N // _ADA_COLS),
        in_specs=[_full((D, B)), pl.BlockSpec((1, B, _ADA_COLS), lambda l, j: (l, 0, j))],
        out_specs=pl.BlockSpec((1, D, _ADA_COLS), lambda l, j: (l, 0, j)),
        out_shape=jax.ShapeDtypeStruct((L, D, N), f32),
        compiler_params=_cp(("parallel", "parallel")),
    )(c_all_t, dmod_cols)


def _adam_math(w, g, m, v):
    m = ADAM_B1 * m + (1.0 - ADAM_B1) * g
    v = ADAM_B2 * v + (1.0 - ADAM_B2) * (g * g)
    m_hat = m / (1.0 - ADAM_B1 ** ADAM_STEP)
    v_hat = v / (1.0 - ADAM_B2 ** ADAM_STEP)
    delta = -ADAM_LR * (m_hat / (jnp.sqrt(v_hat) + ADAM_EPS) + ADAM_WD * w)
    return delta, m, v


def _adamw(w, m, v, g, row_tile, row0=0, outs=None):
    R, C = w.shape
    b0 = row0 // row_tile

    def body(w_ref, m_ref, v_ref, g_ref, _0, _1, _2, _3, g_out, d_out, m_out, v_out):
        gv = g_ref[...]
        delta, mn, vn = _adam_math(w_ref[...], gv, m_ref[...], v_ref[...])
        g_out[...] = gv
        d_out[...] = delta
        m_out[...] = mn
        v_out[...] = vn

    pspec = pl.BlockSpec((row_tile, C), lambda i: (b0 + i, 0))
    gspec = pl.BlockSpec((row_tile, C), lambda i: (i, 0))
    anyspec = pl.BlockSpec(memory_space=pl.ANY)
    shp = jax.ShapeDtypeStruct((R, C), f32)
    if outs is None:
        outs = [lax.empty((R, C), f32) for _ in range(4)]
    return pl.pallas_call(
        body, name="adamw", grid=(g.shape[0] // row_tile,),
        in_specs=[pspec] * 3 + [gspec] + [anyspec] * 4, out_specs=[pspec] * 4, out_shape=[shp] * 4,
        input_output_aliases={4: 0, 5: 1, 6: 2, 7: 3},
        compiler_params=_cp(("parallel",), 40),
    )(w, m, v, g, *outs)


def _pair_sum(g5, got, pc):
    _, LS, _, R2, C = g5.shape

    def body(pc_ref, own_ref, got_ref, out_ref):
        out_ref[0, 0] = (own_ref[0, 0, 0].astype(f32) + got_ref[0, 0].astype(f32)).astype(bf16)

    gs = pltpu.PrefetchScalarGridSpec(
        num_scalar_prefetch=1, grid=(N_CHIPS, LS),
        in_specs=[pl.BlockSpec((1, 1, 1, R2, C), lambda p, s, pc: (p, s, pc[1], 0, 0)),
                  pl.BlockSpec((1, 1, R2, C), lambda p, s, pc: (p, s, 0, 0))],
        out_specs=pl.BlockSpec((1, 1, R2, C), lambda p, s, pc: (p, s, 0, 0)),
    )
    return pl.pallas_call(
        body, name="pair_sum", grid_spec=gs, out_shape=jax.ShapeDtypeStruct((N_CHIPS, LS, R2, C), bf16),
        compiler_params=_cp(("parallel", "parallel")),
    )(pc, g5, got)


def _sum_shards(hsum, recv, pc):
    _, LS, R2, C = hsum.shape

    def body(pc_ref, own_ref, r_ref, out_ref):
        acc = own_ref[0, 0].astype(f32)
        for j in range(3):
            acc = acc + r_ref[j, 0].astype(f32)
        out_ref[0, 0] = acc

    gs = pltpu.PrefetchScalarGridSpec(
        num_scalar_prefetch=1, grid=(LS,),
        in_specs=[pl.BlockSpec((1, 1, R2, C), lambda s, pc: (pc[0], s, 0, 0)),
                  pl.BlockSpec((3, 1, R2, C), lambda s, pc: (0, s, 0, 0))],
        out_specs=pl.BlockSpec((1, 1, R2, C), lambda s, pc: (s, pc[1], 0, 0)),
    )
    return pl.pallas_call(
        body, name="sum_shards", grid_spec=gs, out_shape=jax.ShapeDtypeStruct((LS, 2, R2, C), f32),
        compiler_params=_cp(("parallel",)),
    )(pc, hsum, recv)


def _sum8(packs):
    _, R, C = packs.shape
    tr = R // 8 if R % 64 == 0 else R

    def body(p_ref, out_ref):
        acc = p_ref[0]
        for d in range(1, 8):
            acc = acc + p_ref[d]
        out_ref[...] = acc

    return pl.pallas_call(
        body, name="sum8", grid=(R // tr,),
        in_specs=[pl.BlockSpec((8, tr, C), lambda i: (0, i, 0))],
        out_specs=pl.BlockSpec((tr, C), lambda i: (i, 0)),
        out_shape=jax.ShapeDtypeStruct((R, C), f32),
        compiler_params=_cp(("parallel",)),
    )(packs)


def _allgather8(x_shard):
    m_per, n = x_shard.shape

    def body(x_ref, out_ref, send_sems, recv_sems, local_sem):
        x, y, c = lax.axis_index("x"), lax.axis_index("y"), lax.axis_index("c")
        me, sibling = (x, y, c), (x, y, 1 - c)
        chips = [(1 - x, y), (x, 1 - y), (1 - x, 1 - y)]

        def rows(px, py, pc):
            return out_ref.at[pl.ds((4 * px + 2 * py + pc) * m_per, m_per), :]

        def copy(k, block, to, src=None):
            return pltpu.make_async_remote_copy(
                src_ref=rows(*block) if src is None else src, dst_ref=rows(*block),
                send_sem=send_sems.at[k], recv_sem=recv_sems.at[k], device_id=to, device_id_type=MESH)

        mine = pltpu.make_async_copy(x_ref, rows(*me), local_sem)
        mine.start()
        first = [copy(0, me, sibling, src=x_ref)]
        first += [copy(1 + j, me, (*chip, c), src=x_ref) for j, chip in enumerate(chips)]
        for cp in first:
            cp.start()
        passed = [copy(4 + j, (*chip, c), sibling) for j, chip in enumerate(chips)]
        for j, chip in enumerate(chips):
            copy(1 + j, (*chip, c), me).wait_recv()
            passed[j].start()
        copy(0, sibling, me).wait_recv()
        for j, chip in enumerate(chips):
            copy(4 + j, (*chip, 1 - c), me).wait_recv()
        for cp in first + passed:
            cp.wait_send()
        mine.wait()

    return pl.pallas_call(
        body, name="allgather8",
        out_shape=jax.ShapeDtypeStruct((8 * m_per, n), x_shard.dtype),
        in_specs=[pl.BlockSpec(memory_space=pltpu.VMEM)],
        out_specs=pl.BlockSpec(memory_space=pltpu.VMEM),
        scratch_shapes=[pltpu.SemaphoreType.DMA((7,)), pltpu.SemaphoreType.DMA((7,)), pltpu.SemaphoreType.DMA],
        compiler_params=_cp(None, 48),
    )(x_shard)


def _other_chips():
    x, y = lax.axis_index("x"), lax.axis_index("y")
    return [(1 - x, y), (x, 1 - y), (1 - x, 1 - y)]


_HBM = pl.BlockSpec(memory_space=pltpu.HBM)
_SEM = pl.BlockSpec(memory_space=pltpu.SEMAPHORE)
_EFFECT = pltpu.SideEffectType.DATAFLOW_SIDE_EFFECTING


def _gather_copies(srcs, lands, send_sems, recv_sems):
    x, y, c = lax.axis_index("x"), lax.axis_index("y"), lax.axis_index("c")
    return [pltpu.make_async_remote_copy(
        src_ref=srcs[a].at[:, c], dst_ref=lands[a].at[2 * x + y, :, c], send_sem=send_sems.at[3 * a + j],
        recv_sem=recv_sems.at[3 * a + j], device_id=(cx, cy, c), device_id_type=MESH)
        for a in range(len(srcs)) for j, (cx, cy) in enumerate(_other_chips())]


def _gather_start(chunks, after, name):
    sizes = [len(srcs) for srcs, _ in chunks]
    flat = [t for srcs, lands in chunks for t in list(srcs) + list(lands)]
    nflat = len(flat)
    nsem = 2 * len(chunks)

    def body(*refs):
        ins, sems, token = refs[:nflat], refs[nflat + 1:nflat + 1 + nsem], refs[-1]
        off = 0
        for k, n in enumerate(sizes):
            for cp in _gather_copies(ins[off:off + n], ins[off + n:off + 2 * n], sems[2 * k], sems[2 * k + 1]):
                cp.start()
            off += 2 * n
        token[...] = jnp.zeros_like(token)

    res = pl.pallas_call(
        body, name=name,
        out_shape=[pltpu.SemaphoreType.DMA((3 * n,)) for n in sizes for _ in range(2)]
        + [pltpu.HBM(t.shape, t.dtype) for t in flat] + [jax.ShapeDtypeStruct((8, 128), f32)],
        in_specs=[_HBM] * nflat + [pl.BlockSpec(memory_space=pl.ANY)],
        out_specs=[_SEM] * nsem + [_HBM] * nflat + [pl.BlockSpec(memory_space=pltpu.VMEM)],
        input_output_aliases={i: nsem + i for i in range(nflat)},
        compiler_params=pltpu.CompilerParams(has_side_effects=_EFFECT),
    )(*[pltpu.with_memory_space_constraint(t, pltpu.HBM) for t in flat], after)
    out, off = [], nsem
    for k, n in enumerate(sizes):
        out.append((res[2 * k], res[2 * k + 1], res[off:off + n], res[off + n:off + 2 * n]))
        off += 2 * n
    return out, res[-1]


def _gather_wait(send_sems, recv_sems, srcs, lands, after, name):
    n = len(srcs)

    def body(*refs):
        for cp in _gather_copies(refs[:n], refs[n:2 * n], refs[2 * n], refs[2 * n + 1]):
            cp.wait_send()
            cp.wait_recv()

    res = pl.pallas_call(
        body, name=name,
        out_shape=[pltpu.HBM(t.shape, t.dtype) for t in list(srcs) + list(lands)],
        in_specs=[_HBM] * (2 * n) + [_SEM, _SEM, pl.BlockSpec(memory_space=pl.ANY)],
        out_specs=[_HBM] * (2 * n),
        input_output_aliases={i: i for i in range(2 * n)},
        compiler_params=pltpu.CompilerParams(has_side_effects=_EFFECT),
    )(*srcs, *lands, send_sems, recv_sems, after)
    return res[n:]


def _gather_forward(lands):
    n = len(lands)

    def body(*refs):
        outs = refs[n:2 * n]
        send_sems, recv_sems = refs[2 * n:]
        x, y, c = lax.axis_index("x"), lax.axis_index("y"), lax.axis_index("c")
        sibling = (x, y, 1 - c)
        copies = []
        for a in range(n):
            for j, (cx, cy) in enumerate(_other_chips()):
                mine = outs[a].at[2 * cx + cy, :, c]
                cp = pltpu.make_async_remote_copy(src_ref=mine, dst_ref=mine, send_sem=send_sems.at[3 * a + j],
                                                  recv_sem=recv_sems.at[3 * a + j], device_id=sibling, device_id_type=MESH)
                cp.start()
                copies.append((cp, a, j, cx, cy))
        for cp, a, j, cx, cy in copies:
            cp.wait_send()
            theirs = outs[a].at[2 * cx + cy, :, 1 - c]
            pltpu.make_async_remote_copy(src_ref=theirs, dst_ref=theirs, send_sem=send_sems.at[3 * a + j],
                                         recv_sem=recv_sems.at[3 * a + j], device_id=sibling, device_id_type=MESH).wait_recv()

    hbm = pl.BlockSpec(memory_space=pl.ANY)
    return pl.pallas_call(
        body, name="gather_forward",
        out_shape=[jax.ShapeDtypeStruct(t.shape, t.dtype) for t in lands],
        in_specs=[hbm] * n, out_specs=[hbm] * n,
        input_output_aliases={a: a for a in range(n)},
        scratch_shapes=[pltpu.SemaphoreType.DMA((3 * n,)), pltpu.SemaphoreType.DMA((3 * n,))],
    )(*lands)


def _pair_exchange(g5s):
    n = len(g5s)

    def body(*refs):
        ins, outs = refs[:n], refs[n:2 * n]
        send_sems, recv_sems = refs[2 * n:]
        c = lax.axis_index("c")
        sibling = (lax.axis_index("x"), lax.axis_index("y"), 1 - c)
        copies = []
        for a in range(n):
            cp = pltpu.make_async_remote_copy(src_ref=ins[a].at[:, :, 1 - c], dst_ref=outs[a], send_sem=send_sems.at[a],
                                              recv_sem=recv_sems.at[a], device_id=sibling, device_id_type=MESH)
            cp.start()
            copies.append(cp)
        for cp in copies:
            cp.wait()

    hbm = pl.BlockSpec(memory_space=pl.ANY)
    return pl.pallas_call(
        body, name="pair_exchange",
        out_shape=[jax.ShapeDtypeStruct(g.shape[:2] + g.shape[3:], g.dtype) for g in g5s],
        in_specs=[hbm] * n, out_specs=[hbm] * n,
        scratch_shapes=[pltpu.SemaphoreType.DMA((n,)), pltpu.SemaphoreType.DMA((n,))],
    )(*g5s)


def _scatter_copies(srcs, lands, send_sems, recv_sems):
    c = lax.axis_index("c")
    return [pltpu.make_async_remote_copy(
        src_ref=srcs[a].at[2 * cx + cy], dst_ref=lands[a].at[j], send_sem=send_sems.at[3 * a + j],
        recv_sem=recv_sems.at[3 * a + j], device_id=(cx, cy, c), device_id_type=MESH)
        for a in range(len(srcs)) for j, (cx, cy) in enumerate(_other_chips())]


def _scatter_start(hsums, name, after=()):
    n = len(hsums)
    na = len(after)

    def body(*refs):
        srcs, lands = refs[:n], refs[n:2 * n]
        send_sems, recv_sems = refs[2 * n + na], refs[2 * n + na + 1]
        for cp in _scatter_copies(srcs, lands, send_sems, recv_sems):
            cp.start()
        refs[-1][...] = jnp.zeros_like(refs[-1])

    lands = [lax.empty((3,) + g.shape[1:], g.dtype) for g in hsums]
    res = pl.pallas_call(
        body, name=name,
        out_shape=[pltpu.SemaphoreType.DMA((3 * n,)), pltpu.SemaphoreType.DMA((3 * n,))]
        + [pltpu.HBM(g.shape, g.dtype) for g in hsums] + [pltpu.HBM(g.shape, g.dtype) for g in lands]
        + [jax.ShapeDtypeStruct((8, 128), f32)],
        in_specs=[_HBM] * (2 * n) + [pl.BlockSpec(memory_space=pl.ANY)] * na,
        out_specs=[_SEM, _SEM] + [_HBM] * (2 * n) + [pl.BlockSpec(memory_space=pltpu.VMEM)],
        input_output_aliases={i: i + 2 for i in range(2 * n)},
        compiler_params=pltpu.CompilerParams(has_side_effects=_EFFECT),
    )(*[pltpu.with_memory_space_constraint(t, pltpu.HBM) for t in list(hsums) + lands], *after)
    return (res[0], res[1], res[2:2 + n], res[2 + n:2 + 2 * n]), res[-1]


def _scatter_wait(send_sems, recv_sems, srcs, lands, after, name):
    n = len(srcs)
    extra = list(after)

    def body(*refs):
        s_refs, l_refs = refs[:n], refs[n:2 * n]
        ss, rs = refs[2 * n], refs[2 * n + 1]
        for cp in _scatter_copies(s_refs, l_refs, ss, rs):
            cp.wait_send()
            cp.wait_recv()

    res = pl.pallas_call(
        body, name=name,
        out_shape=[pltpu.HBM(g.shape, g.dtype) for g in srcs] + [pltpu.HBM(g.shape, g.dtype) for g in lands],
        in_specs=[_HBM] * (2 * n) + [_SEM, _SEM] + [pl.BlockSpec(memory_space=pl.ANY)] * len(extra),
        out_specs=[_HBM] * (2 * n),
        input_output_aliases={i: i for i in range(2 * n)},
        compiler_params=pltpu.CompilerParams(has_side_effects=_EFFECT),
    )(*srcs, *lands, send_sems, recv_sems, *extra)
    return res[:n], res[n:]


def _swap_halves(fulls):
    n = len(fulls)

    def body(*refs):
        ins, outs = refs[:n], refs[n:2 * n]
        send_sems, recv_sems = refs[2 * n:]
        c = lax.axis_index("c")
        sibling = (lax.axis_index("x"), lax.axis_index("y"), 1 - c)
        copies = []
        for a in range(n):
            cp = pltpu.make_async_remote_copy(src_ref=outs[a].at[:, c], dst_ref=outs[a].at[:, c], send_sem=send_sems.at[a],
                                              recv_sem=recv_sems.at[a], device_id=sibling, device_id_type=MESH)
            cp.start()
            copies.append(cp)
        for a, cp in enumerate(copies):
            cp.wait_send()
            theirs = outs[a].at[:, 1 - c]
            pltpu.make_async_remote_copy(src_ref=theirs, dst_ref=theirs, send_sem=send_sems.at[a], recv_sem=recv_sems.at[a],
                                         device_id=sibling, device_id_type=MESH).wait_recv()

    hbm = pl.BlockSpec(memory_space=pl.ANY)
    return pl.pallas_call(
        body, name="swap_halves",
        out_shape=[jax.ShapeDtypeStruct(p.shape, p.dtype) for p in fulls],
        in_specs=[hbm] * n, out_specs=[hbm] * n,
        input_output_aliases={a: a for a in range(n)},
        scratch_shapes=[pltpu.SemaphoreType.DMA((n,)), pltpu.SemaphoreType.DMA((n,))],
    )(*fulls)


def _to_segments(t):
    s, c = t.shape
    return t.reshape(SCAN_SEG, s // SCAN_SEG, c).transpose(1, 0, 2).reshape(s, c)


def _from_segments(t):
    s, c = t.shape
    return t.reshape(s // SCAN_SEG, SCAN_SEG, c).transpose(1, 0, 2).reshape(s, c)


def _ssm_operators(a_re, a_im, log_dt, b_re, b_im, c_re, c_im):
    lam = lax.complex(a_re, a_im)
    dt = jnp.exp(log_dt)[:, None]
    a_bar = jnp.exp(lam * dt)
    b_bar = ((a_bar - 1.0) / lam)[:, :, None] * lax.complex(b_re, b_im)
    eye = jnp.eye(N_GROUPS, dtype=f32)

    def embed_b(t):
        return (jnp.transpose(t, (0, 2, 1))[:, :, None, :] * eye[:, None, :, None]).reshape(D_SSM, D_STATE)

    def embed_c(t):
        return (jnp.transpose(t, (0, 2, 1))[:, :, None, :] * eye[:, None, :, None]).reshape(D_STATE, D_SSM)

    a2 = jnp.stack([a_bar.real.reshape(D_STATE), a_bar.imag.reshape(D_STATE)])
    return a2, embed_b(b_bar.real), embed_b(b_bar.imag), embed_c(c_re), embed_c(c_im)


def _local_step(x, target, mod, small, ffn_weights, mix_weights, grads_done):
    table = jnp.asarray(_bucket_table())
    bias = _bias_fwd(small["rel_bias"], table)
    L = DEPTH
    saved = []
    ssm_ops = []
    for l in range(L):
        sv = {}
        m9 = mod[l]
        sv["x0"] = x
        sv["w0"] = ffn_weights(l, 0, x)
        x, sv["f0"], sv["g0"], sv["u0"] = _ffn_fwd(x, m9[0:3], *sv["w0"], 0, small["ln_g"][l, 0:1], small["ln_b"][l, 0:1])
        sv["x1"] = x
        sv["w1"] = mix_weights(l, x)
        *qkv, z_rest, sv["h1"] = _mix_in_fwd(x, m9[3:6], sv["w1"][0], 0)
        S = x.shape[0]
        qkv = [t.reshape(3, S, D_ATT) for t in qkv]
        att = [_att_fwd(qkv[b], bias, b) for b in range(3)]
        y_att, lse3 = _att_merge([att[b][0].reshape(d, S // d, D_ATT) for b, d in enumerate(DILATIONS)],
                                 [att[b][1].reshape(d, S // d, _LANES) for b, d in enumerate(DILATIONS)])
        sv.update(qkv=qkv, lse=[a[1] for a in att], lse3=lse3, y_att=y_att)

        prm = tuple(small[k][l] for k in ("ssm_a_re", "ssm_a_im", "ssm_log_dt", "ssm_b_re", "ssm_b_im", "ssm_c_re", "ssm_c_im"))
        (a2, bre, bim, cre, cim), ops_vjp = jax.vjp(_ssm_operators, *prm)
        ssm_ops.append(ops_vjp)
        u_ssm = _to_segments(z_rest[:, :D_SSM])
        bur, bui = _ssm_in(u_ssm, bre, bim)
        sr, si = _ssm_scan(bur, bui, a2, False)
        dskip = small["ssm_d"][l][None, :]
        glu_b = small["glu_b"][l][None, :]
        out_seg, y_seg = _ssm_out(sr, si, u_ssm, cre, cim, dskip, small["glu_w"][l], glu_b)
        y_ssm = _from_segments(out_seg)
        sv.update(a2=a2, bre=bre, bim=bim, cre=cre, cim=cim, u_ssm=u_ssm, sr=sr, si=si, y_seg=y_seg, y_ssm=y_ssm)

        u_pool = jnp.concatenate([jnp.zeros((POOL_HALO, D_POOL), f32), z_rest[:, D_SSM:]])
        y_pool = _pool_fwd(u_pool, small["pool_w"][l], small["pool_scale"][l][None, :])
        sv.update(u_pool=u_pool, y_pool=y_pool)

        x, sv["ymix"] = _mix_out_fwd(x, y_att, y_ssm, y_pool, m9[3:6], sv["w1"][1], 0, small["ln_g"][l, 1:2], small["ln_b"][l, 1:2])
        sv["x2"] = x
        sv["w2"] = ffn_weights(l, 1, x)
        x, sv["f2"], sv["g2"], sv["u2"] = _ffn_fwd(x, m9[6:9], *sv["w2"], 0, small["ln_g"][l, 2:3], small["ln_b"][l, 2:3])
        saved.append(sv)

    loss, dx = _loss_head(x, target)

    dmod = [None] * L
    dln_g = [None] * L
    dln_b = [None] * L
    sg = {k: [None] * L for k in ("ssm_a_re", "ssm_a_im", "ssm_log_dt", "ssm_b_re", "ssm_b_im", "ssm_c_re", "ssm_c_im",
                                  "ssm_d", "glu_w", "glu_b", "pool_w", "pool_scale")}
    dbias_tot = None
    order_after = jnp.zeros((), f32)
    for l in reversed(range(L)):
        sv = saved[l]
        m9 = mod[l] + order_after

        def fresh(like):
            return [lax.empty(t.shape, bf16) for t in like]

        dx, dg, du, a, h, df, dm2, dlg2, dlb2 = _ffn_bwd(dx, sv["x2"], sv["f2"], sv["g2"], sv["u2"], m9[6:9], *sv["w2"], 0,
                                                        small["ln_g"][l, 2:3])
        g_ffn1 = _ffn_wgrad(h, dg, du, a, df, *fresh(sv["w2"]), 0)
        dxr, d_att, d_ssm, d_pool, dgate1, dlg1, dlb1, g_w_out = _mix_out_bwd(
            dx, sv["x1"], sv["ymix"], sv["y_att"], sv["y_ssm"], sv["y_pool"], m9[3:6], sv["w1"][1], 0, small["ln_g"][l, 1:2],
            fresh(sv["w1"])[1])
        S = d_att.shape[0]
        merged = _att_merge_bwd(d_att, sv["y_att"], sv["lse3"])
        dqkv, dbias = [], []
        for b, d in enumerate(DILATIONS):
            dq_b, db_b = _att_bwd(sv["qkv"][b], merged[b].reshape(S, D_ATT), sv["lse"][b], merged[3 + b].reshape(S, _LANES), bias, b)
            dqkv.append(dq_b.reshape(3, d, S // d, D_ATT))
            dbias.append(db_b)
        dbias = jnp.stack(dbias)
        dbias_tot = dbias if dbias_tot is None else dbias_tot + dbias
        d_seg = _to_segments(d_ssm)
        dskip = small["ssm_d"][l][None, :]
        glu_b = small["glu_b"][l][None, :]
        gsr, gsi, du_skip, dcre, dcim, dd, dglu_b, dglu_w = _ssm_out_bwd(
            d_seg, sv["y_seg"], sv["u_ssm"], sv["sr"], sv["si"], sv["cre"], sv["cim"], dskip, small["glu_w"][l], glu_b)
        lr, li = _ssm_scan(gsr, gsi, sv["a2"], True)
        du_seg, dbre, dbim = _ssm_in_bwd(lr, li, sv["u_ssm"], du_skip, sv["bre"], sv["bim"])
        da2 = _ssm_da(lr, li, sv["sr"], sv["si"])
        d_prm = ssm_ops[l]((da2, dbre, dbim, dcre, dcim))
        for k, v in zip(("ssm_a_re", "ssm_a_im", "ssm_log_dt", "ssm_b_re", "ssm_b_im", "ssm_c_re", "ssm_c_im"), d_prm):
            sg[k][l] = v
        sg["ssm_d"][l] = dd[0]
        sg["glu_b"][l] = dglu_b[0]
        sg["glu_w"][l] = dglu_w
        du_ssm = _from_segments(du_seg)
        dyp = jnp.concatenate([d_pool, jnp.zeros((POOL_HALO, D_POOL), f32)])
        du_pool, dpw, dps = _pool_bwd(dyp, sv["u_pool"], small["pool_w"][l], small["pool_scale"][l][None, :])
        sg["pool_w"][l] = dpw
        sg["pool_scale"][l] = dps[0]
        d_rest = jnp.concatenate([du_ssm, du_pool], axis=1).astype(bf16)
        dx, dm1, dz = _mix_in_bwd(dqkv, d_rest, dxr, sv["x1"], m9[3:6], sv["w1"][0], 0)
        g_w_in = _mix_in_wgrad(sv["h1"], dz, fresh(sv["w1"])[0], 0)
        ffn_names = ("ffn_w_gate", "ffn_w_up", "ffn_w_down")
        m9 = m9 + grads_done(l, 1, list(zip(ffn_names, [(2 * l + 1) * FF_SHARD] * 3, g_ffn1))
                             + [("w_in", l * D_MODEL, g_w_in), ("w_out", l * 256, g_w_out)])
        dm1 = jnp.concatenate([dm1[0:2], dgate1])
        dx, dg, du, a, h, df, dm0, dlg0, dlb0 = _ffn_bwd(dx, sv["x0"], sv["f0"], sv["g0"], sv["u0"], m9[0:3], *sv["w0"], 0,
                                                        small["ln_g"][l, 0:1])
        g_ffn0 = _ffn_wgrad(h, dg, du, a, df, *fresh(sv["w0"]), 0)
        order_after = grads_done(l, 0, list(zip(ffn_names, [2 * l * FF_SHARD] * 3, g_ffn0)))
        dmod[l] = jnp.concatenate([dm0, dm1, dm2])
        dln_g[l] = jnp.concatenate([dlg0, dlg1, dlg2])
        dln_b[l] = jnp.concatenate([dlb0, dlb1, dlb2])

    small_grads = {k: jnp.stack(v) for k, v in sg.items()}
    small_grads["rel_bias"] = _bias_bwd(dbias_tot, table)
    small_grads["ln_g"] = jnp.stack(dln_g)
    small_grads["ln_b"] = jnp.stack(dln_b)
    return loss, dx, jnp.stack(dmod), small_grads


def _pack(arrs):
    flat = jnp.concatenate([a.reshape(-1).astype(f32) for a in arrs])
    n = flat.shape[0]
    npad = -(-n // 1024) * 1024
    return jnp.pad(flat, (0, npad - n)).reshape(npad // 128, 128)


def _unpack(buf, shapes):
    flat = buf.reshape(-1)
    out, off = [], 0
    for s in shapes:
        n = int(np.prod(s))
        out.append(flat[off:off + n].reshape(s))
        off += n
    return out


_REPL = ("rel_bias", "ada_b", "ssm_a_re", "ssm_a_im", "ssm_log_dt", "ssm_b_re", "ssm_b_im", "ssm_c_re", "ssm_c_im",
         "ssm_d", "glu_b", "pool_w", "pool_scale")
_SMALL_SHARDED = ("ln_g", "ln_b", "glu_w")
_BIG = ("ffn_w_gate", "ffn_w_up", "ffn_w_down", "w_in", "w_out")
_ORDER = ("rel_bias", "ada_w", "ada_b", "ln_g", "ln_b", "ffn_w_gate", "ffn_w_up", "ffn_w_down", "w_in", "w_out",
          "ssm_a_re", "ssm_a_im", "ssm_log_dt", "ssm_b_re", "ssm_b_im", "ssm_c_re", "ssm_c_im", "ssm_d", "glu_w",
          "glu_b", "pool_w", "pool_scale")


def kernel(x, c, rel_bias, ada_w, ada_b, ln_g, ln_b, ffn_w_gate, ffn_w_up, ffn_w_down, w_in, w_out, ssm_a_re, ssm_a_im, ssm_log_dt, ssm_b_re, ssm_b_im, ssm_c_re, ssm_c_im, ssm_d, glu_w, glu_b, pool_w, pool_scale, loss_target, m_rel_bias, m_ada_w, m_ada_b, m_ln_g, m_ln_b, m_ffn_w_gate, m_ffn_w_up, m_ffn_w_down, m_w_in, m_w_out, m_ssm_a_re, m_ssm_a_im, m_ssm_log_dt, m_ssm_b_re, m_ssm_b_im, m_ssm_c_re, m_ssm_c_im, m_ssm_d, m_glu_w, m_glu_b, m_pool_w, m_pool_scale, v_rel_bias, v_ada_w, v_ada_b, v_ln_g, v_ln_b, v_ffn_w_gate, v_ffn_w_up, v_ffn_w_down, v_w_in, v_w_out, v_ssm_a_re, v_ssm_a_im, v_ssm_log_dt, v_ssm_b_re, v_ssm_b_im, v_ssm_c_re, v_ssm_c_im, v_ssm_d, v_glu_w, v_glu_b, v_pool_w, v_pool_scale):
    args = dict(locals())
    w = {k: args[k] for k in _ORDER}
    m = {k: args["m_" + k] for k in _ORDER}
    v = {k: args["v_" + k] for k in _ORDER}
    L, D = DEPTH, D_MODEL
    ax, ay, ac = lax.axis_index("x"), lax.axis_index("y"), lax.axis_index("c")
    p_me = 2 * ax + ay
    dev = 4 * ax + 2 * ay + ac

    transposed = ("ffn_w_gate", "ffn_w_up")
    for d in (w, m, v):
        for name in transposed:
            d[name] = jnp.swapaxes(d[name], 2, 3)

    def halves(t):
        return t.astype(bf16).reshape(1, 2, t.shape[0] // 2, t.shape[1])

    def landing(src):
        return lax.dynamic_update_slice(lax.empty((N_CHIPS,) + src.shape, bf16), src[None], (p_me, 0, 0, 0, 0))

    chunk_keys = [("ffn", 0, 0), ("mix", 0), ("ffn", 0, 1), ("ffn", 1, 0), ("mix", 1), ("ffn", 1, 1)]
    chunk_srcs = []
    for key in chunk_keys:
        if key[0] == "ffn":
            chunk_srcs.append([halves(w[name][key[1], key[2]]) for name in ("ffn_w_gate", "ffn_w_up", "ffn_w_down")])
        else:
            chunk_srcs.append([halves(w_in[key[1]]), halves(w_out[key[1]])])

    chunks = [(srcs, [landing(t) for t in srcs]) for srcs in chunk_srcs]
    first_in_flight, first_begun = _gather_start(chunks[:1], c, "gather_start_first")
    pack = _pack([c + first_begun[0, 0], ln_g, ln_b, glu_w])
    rows = pack.shape[0]
    allp = _allgather8(pack).reshape(8, rows, 128)
    c_all = allp[:, :8].reshape(8, D)
    by_chip = allp[0::2]

    def sharded(row0, nrows, shape, axis):
        t = by_chip[:, row0:row0 + nrows].reshape((N_CHIPS,) + shape)
        return jnp.concatenate([t[p] for p in range(N_CHIPS)], axis=axis)

    ln_g_full = sharded(8, 12, (L, 3, 256), 2)
    ln_b_full = sharded(20, 12, (L, 3, 256), 2)
    glu_w_full = sharded(32, 256, (L, 64, 256), 1)

    ncol = ada_w.shape[-1]
    ada_b_cols = lax.dynamic_slice_in_dim(ada_b, p_me * ncol, ncol, axis=1)[:, None, :]
    mod_part = _ada_fwd(c_all, ada_w, ada_b_cols)
    mrows = L * 8 * ncol // 128
    mod_all = _allgather8(mod_part.reshape(mrows, 128)).reshape(8, L, 8, ncol)
    mod_mine = lax.dynamic_index_in_dim(mod_all, dev, axis=2, keepdims=False)
    mod = jnp.concatenate([mod_mine[2 * p] for p in range(N_CHIPS)], axis=-1).reshape(L, 9, D)

    in_flight = first_in_flight + _gather_start(chunks[1:], mod, "gather_start_rest")[0]

    def gathered(key, after):
        k = chunk_keys.index(key)
        lands = _gather_forward(_gather_wait(*in_flight[k], after, "gather_wait_%d" % k))
        return [t.reshape(N_CHIPS, 1, 2 * t.shape[3], t.shape[4]) for t in lands]

    pc = jnp.stack([p_me, ac]).astype(jnp.int32)
    groups = {}
    scattering = {}

    def start_group(tag, after=()):
        g5 = [g.reshape(g.shape[:2] + (2, g.shape[2] // 2, g.shape[3])) for _, _, g in groups[tag]]
        hsum = [_pair_sum(g, r, pc) for g, r in zip(g5, _pair_exchange(g5))]
        scattering[tag], begun = _scatter_start(hsum, "scatter_start_%s" % tag, after)
        return begun

    def grads_done(l, s, grads):
        if l == 1:
            groups.setdefault("l1", []).extend(grads)
            return start_group("l1")[0, 0] if s == 0 else jnp.zeros((), f32)
        groups["l0a" if s == 1 else "l0b"] = grads
        return start_group("l0a")[0, 0] if s == 1 else jnp.zeros((), f32)

    small = {k: w[k] for k in _REPL if k != "ada_b"}
    small.update(ln_g=ln_g_full, ln_b=ln_b_full, glu_w=glu_w_full)
    loss_dev, grad_x, dmod, sgrads = _local_step(
        x[0], loss_target[0], mod, small, lambda l, s, after: gathered(("ffn", l, s), after),
        lambda l, after: gathered(("mix", l), after), grads_done)
    loss = lax.psum(loss_dev[0, 0], ("x", "y", "c"))

    names = ("rel_bias", "ln_g", "ln_b", "ssm_a_re", "ssm_a_im", "ssm_log_dt", "ssm_b_re", "ssm_b_im", "ssm_c_re",
             "ssm_c_im", "ssm_d", "glu_w", "glu_b", "pool_w", "pool_scale")
    gpack = _pack([dmod] + [sgrads[k] for k in names])
    grows = gpack.shape[0]
    gall = _allgather8(gpack).reshape(8, grows, 128)
    l0b_begun = start_group("l0b", (gall,))
    gsum = _unpack(_sum8(gall), [(L, 9 * D)] + [sgrads[k].shape for k in names])
    red = dict(zip(("ada_b",) + names, gsum))
    red["ln_g"] = lax.dynamic_slice_in_dim(red["ln_g"], p_me * 256, 256, axis=2)
    red["ln_b"] = lax.dynamic_slice_in_dim(red["ln_b"], p_me * 256, 256, axis=2)
    red["glu_w"] = lax.dynamic_slice_in_dim(red["glu_w"], p_me * 64, 64, axis=1)

    dmod_all = gall[:, :L * 9 * D // 128].reshape(8, L, 9 * D)
    dmod_cols = jnp.transpose(lax.dynamic_slice_in_dim(dmod_all, p_me * ncol, ncol, axis=2), (1, 0, 2))
    g_ada_w = _ada_wgrad(jnp.transpose(c_all), dmod_cols)

    out_g, out_d, out_m, out_v = {}, {}, {}, {}
    r2 = (L * D, ncol)
    res = _adamw(ada_w.reshape(r2), m["ada_w"].reshape(r2), v["ada_w"].reshape(r2), g_ada_w.reshape(r2), 128)
    out_g["ada_w"], out_d["ada_w"], out_m["ada_w"], out_v["ada_w"] = [t.reshape(ada_w.shape) for t in res]

    small_names = _REPL + _SMALL_SHARDED
    wp = _pack([w[k] for k in small_names])
    res_small = _adamw(wp, _pack([m[k] for k in small_names]), _pack([v[k] for k in small_names]),
                       _pack([red[k] for k in small_names]), wp.shape[0])
    for t, dst in zip(res_small, (out_g, out_d, out_m, out_v)):
        for k, a in zip(small_names, _unpack(t, [w[k].shape for k in small_names])):
            dst[k] = a

    row_tile = dict(zip(_BIG, (352, 352, 352, 256, 256)))
    big = {name: None for name in _BIG}
    after = [grad_x, res_small[1], res[1], l0b_begun]
    for tag in ("l1", "l0a", "l0b"):
        hsum, recv = _scatter_wait(*scattering[tag], after, "scatter_wait_%s" % tag)
        full = _swap_halves([_sum_shards(h, r, pc) for h, r in zip(hsum, recv)])
        for (name, row0, _), g in zip(groups[tag], full):
            shp = w[name].shape
            r2 = (int(np.prod(shp[:-1])), shp[-1])
            big[name] = _adamw(w[name].reshape(r2), m[name].reshape(r2), v[name].reshape(r2), g.reshape(-1, shp[-1]),
                               row_tile[name], row0, big[name])
        after = [big[name][1] for name, _, _ in groups[tag]]
    for name in _BIG:
        res = [t.reshape(w[name].shape) for t in big[name]]
        out_g[name], out_d[name], out_m[name], out_v[name] = [jnp.swapaxes(t, 2, 3) for t in res] if name in transposed else res

    return (loss, grad_x[None], *[out_g[k] for k in _ORDER], *[out_d[k] for k in _ORDER],
            *[out_m[k] for k in _ORDER], *[out_v[k] for k in _ORDER])
```

```python
import functools
import math

import numpy as np
import jax
import jax.numpy as jnp
from jax import lax
from jax.experimental import pallas as pl
from jax.experimental.pallas import tpu as pltpu

f32 = jnp.float32
bf16 = jnp.bfloat16
MESH = pl.DeviceIdType.MESH

D_MODEL = 1024
SEQ = 2048
DEPTH = 2
HEAD_DIM = 64
N_HEADS = 8
D_ATT = 512
DILATIONS = (1, 4, 16)
BLOCKS_PER_RESIDUE = (16, 4, 1)
ATT_BLOCK = 128
N_UNITS = SEQ // ATT_BLOCK
N_GROUPS = 16
SSM_GROUP = 16
SSM_STATE = 64
D_SSM = 256
D_STATE = N_GROUPS * SSM_STATE
POOL_WINDOWS = (2, 4, 8, 16)
POOL_GROUP = 64
D_POOL = 256
POOL_HALO = 16
D_FF = 2816
N_BUCKETS = 32
MAX_DISTANCE = 2048
ALPHA = (2 * DEPTH) ** 0.25
FFN_RES = 0.5
LN_EPS = 1e-5
NEG = -1e30
N_CHIPS = 4
FF_SHARD = D_FF // N_CHIPS
SCAN_SEG = 8
SCAN_STEPS = SEQ // SCAN_SEG

ADAM_LR, ADAM_B1, ADAM_B2, ADAM_EPS, ADAM_WD, ADAM_STEP = 0.001, 0.9, 0.999, 1e-08, 0.01, 10

TOK_TILE = 512


def _cp(dims=None, vmem_mb=None):
    kw = {}
    if dims is not None:
        kw["dimension_semantics"] = dims
    if vmem_mb is not None:
        kw["vmem_limit_bytes"] = vmem_mb << 20
    return pltpu.CompilerParams(**kw)


def _dot(a, b):
    return jnp.dot(a, b, preferred_element_type=f32)


def _dot_nt(a, b):
    return lax.dot_general(a, b, (((1,), (1,)), ((), ())), preferred_element_type=f32)


def _dot_tn(a, b):
    return lax.dot_general(a, b, (((0,), (0,)), ((), ())), preferred_element_type=f32)


def _ln_stats(v):
    mu = jnp.mean(v, -1, keepdims=True)
    d = v - mu
    var = jnp.mean(d * d, -1, keepdims=True)
    rstd = lax.rsqrt(var + LN_EPS)
    return d * rstd, rstd


def _ln_bwd(dxh, xh, rstd):
    return rstd * (dxh - jnp.mean(dxh, -1, keepdims=True) - xh * jnp.mean(dxh * xh, -1, keepdims=True))


_GELU_C = math.sqrt(2.0 / math.pi)


def _gelu(y):
    return 0.5 * y * (1.0 + jnp.tanh(_GELU_C * (y + 0.044715 * y * y * y)))


def _gelu_grad(y):
    t = jnp.tanh(_GELU_C * (y + 0.044715 * y * y * y))
    return 0.5 * (1.0 + t) + 0.5 * y * (1.0 - t * t) * (_GELU_C * (1.0 + 3 * 0.044715 * y * y))


def _full(shape):
    return pl.BlockSpec(shape, lambda *_: (0,) * len(shape))


def _ffn_fwd(x, mod3, wg, wu, wd, ls, lng, lnb):
    S, D = x.shape
    Fs = wg.shape[-2]
    ts = TOK_TILE

    def body(x_ref, mod_ref, wg_ref, wu_ref, wd_ref, lng_ref, lnb_ref, xo_ref, f_ref, g_ref, u_ref, h_ref, acc_sc):
        j = pl.program_id(1)

        @pl.when(j == 0)
        def _():
            xh, _ = _ln_stats(x_ref[...])
            h_ref[...] = (xh * (1.0 + mod_ref[1:2, :]) + mod_ref[0:1, :]).astype(bf16)
            acc_sc[...] = jnp.zeros_like(acc_sc)

        h = h_ref[...]
        g = _dot_nt(h, wg_ref[0, 0])
        u = _dot_nt(h, wu_ref[0, 0])
        g_ref[0] = g.astype(bf16)
        u_ref[0] = u.astype(bf16)
        a = (g * jax.nn.sigmoid(g) * u).astype(bf16)
        acc_sc[...] += _dot(a, wd_ref[0, 0])

        @pl.when(j == N_CHIPS - 1)
        def _():
            f = acc_sc[...]
            f_ref[...] = f
            r = ALPHA * x_ref[...] + (FFN_RES * mod_ref[2:3, :]) * f
            rh, _ = _ln_stats(r)
            xo_ref[...] = rh * lng_ref[...] + lnb_ref[...]

    tok = pl.BlockSpec((ts, D), lambda i, j: (i, 0))
    wrow = pl.BlockSpec((1, 1, Fs, D), lambda i, j: (j, ls, 0, 0))
    hid = pl.BlockSpec((1, ts, Fs), lambda i, j: (j, i, 0))
    return pl.pallas_call(
        body, name="ffn_fwd", grid=(S // ts, N_CHIPS),
        in_specs=[tok, _full((3, D)), wrow, wrow, wrow, _full((1, D)), _full((1, D))],
        out_specs=[tok, tok, hid, hid, tok],
        out_shape=[jax.ShapeDtypeStruct((S, D), f32), jax.ShapeDtypeStruct((S, D), f32),
                   jax.ShapeDtypeStruct((N_CHIPS, S, Fs), bf16), jax.ShapeDtypeStruct((N_CHIPS, S, Fs), bf16),
                   jax.ShapeDtypeStruct((S, D), bf16)],
        scratch_shapes=[pltpu.VMEM((ts, D), f32)],
        compiler_params=_cp(("parallel", "arbitrary"), 56),
    )(x, mod3, wg, wu, wd, lng, lnb)


def _ffn_bwd(dxo, x, f, g, u, mod3, wg, wu, wd, ls, lng):
    S, D = x.shape
    Fs = wg.shape[-2]
    ts = TOK_TILE

    def body(dxo_ref, x_ref, f_ref, g_ref, u_ref, mod_ref, wg_ref, wu_ref, wd_ref, lng_ref,
             dx_ref, dg_ref, du_ref, a_ref, df_ref, dmod_ref, dlng_ref, dlnb_ref,
             dr_sc, df_sc, acc_sc):
        i = pl.program_id(0)
        j = pl.program_id(1)

        @pl.when((i == 0) & (j == 0))
        def _():
            dmod_ref[...] = jnp.zeros_like(dmod_ref)
            dlng_ref[...] = jnp.zeros_like(dlng_ref)
            dlnb_ref[...] = jnp.zeros_like(dlnb_ref)

        @pl.when(j == 0)
        def _():
            xv = x_ref[...]
            fv = f_ref[...]
            gate = mod_ref[2:3, :]
            rh, rstd = _ln_stats(ALPHA * xv + (FFN_RES * gate) * fv)
            dy = dxo_ref[...]
            dlng_ref[...] += jnp.sum(dy * rh, 0, keepdims=True)
            dlnb_ref[...] += jnp.sum(dy, 0, keepdims=True)
            dr = _ln_bwd(dy * lng_ref[...], rh, rstd)
            dr_sc[...] = dr
            dmod_ref[2:3, :] += jnp.sum(FFN_RES * dr * fv, 0, keepdims=True)
            df = ((FFN_RES * gate) * dr).astype(bf16)
            df_sc[...] = df
            df_ref[...] = df
            acc_sc[...] = jnp.zeros_like(acc_sc)

        da = _dot_nt(df_sc[...], wd_ref[0, 0])
        gv = g_ref[0].astype(f32)
        uv = u_ref[0].astype(f32)
        sg = jax.nn.sigmoid(gv)
        si = gv * sg
        a_ref[0] = (si * uv).astype(bf16)
        dgv = (da * uv * (sg * (1.0 + gv * (1.0 - sg)))).astype(bf16)
        duv = (da * si).astype(bf16)
        dg_ref[0] = dgv
        du_ref[0] = duv
        acc_sc[...] += _dot(dgv, wg_ref[0, 0]) + _dot(duv, wu_ref[0, 0])

        @pl.when(j == N_CHIPS - 1)
        def _():
            dh = acc_sc[...]
            xh, rstd0 = _ln_stats(x_ref[...])
            dmod_ref[0:1, :] += jnp.sum(dh, 0, keepdims=True)
            dmod_ref[1:2, :] += jnp.sum(dh * xh, 0, keepdims=True)
            dx_ref[...] = _ln_bwd(dh * (1.0 + mod_ref[1:2, :]), xh, rstd0) + ALPHA * dr_sc[...]

    tok = pl.BlockSpec((ts, D), lambda i, j: (i, 0))
    wrow = pl.BlockSpec((1, 1, Fs, D), lambda i, j: (j, ls, 0, 0))
    hid = pl.BlockSpec((1, ts, Fs), lambda i, j: (j, i, 0))
    hid_shape = jax.ShapeDtypeStruct((N_CHIPS, S, Fs), bf16)
    return pl.pallas_call(
        body, name="ffn_bwd", grid=(S // ts, N_CHIPS),
        in_specs=[tok, tok, tok, hid, hid, _full((3, D)), wrow, wrow, wrow, _full((1, D))],
        out_specs=[tok, hid, hid, hid, tok, _full((3, D)), _full((1, D)), _full((1, D))],
        out_shape=[jax.ShapeDtypeStruct((S, D), f32), hid_shape, hid_shape, hid_shape,
                   jax.ShapeDtypeStruct((S, D), bf16),
                   jax.ShapeDtypeStruct((3, D), f32), jax.ShapeDtypeStruct((1, D), f32), jax.ShapeDtypeStruct((1, D), f32)],
        scratch_shapes=[pltpu.VMEM((ts, D), f32), pltpu.VMEM((ts, D), bf16), pltpu.VMEM((ts, D), f32)],
        compiler_params=_cp(("arbitrary", "arbitrary"), 56),
    )(dxo, x, f, g, u, mod3, wg, wu, wd, lng)


def _ffn_wgrad(h, dg, du, a, df, gwg, gwu, gwd, ls):
    S, D = h.shape
    Fs = dg.shape[-1]
    tk = TOK_TILE
    nk = S // tk

    def body(h_ref, dg_ref, du_ref, a_ref, df_ref, _g0, _g1, _g2, gwg_ref, gwu_ref, gwd_ref, ag_sc, au_sc, ad_sc):
        k = pl.program_id(1)

        @pl.when(k == 0)
        def _():
            ag_sc[...] = jnp.zeros_like(ag_sc)
            au_sc[...] = jnp.zeros_like(au_sc)
            ad_sc[...] = jnp.zeros_like(ad_sc)

        hv = h_ref[...]
        ag_sc[...] += _dot_tn(dg_ref[0], hv)
        au_sc[...] += _dot_tn(du_ref[0], hv)
        ad_sc[...] += _dot_tn(a_ref[0], df_ref[...])

        @pl.when(k == nk - 1)
        def _():
            gwg_ref[0, 0] = ag_sc[...].astype(bf16)
            gwu_ref[0, 0] = au_sc[...].astype(bf16)
            gwd_ref[0, 0] = ad_sc[...].astype(bf16)

    tok = pl.BlockSpec((tk, D), lambda p, k: (k, 0))
    hid = pl.BlockSpec((1, tk, Fs), lambda p, k: (p, k, 0))
    anyspec = pl.BlockSpec(memory_space=pl.ANY)
    orow = pl.BlockSpec((1, 1, Fs, D), lambda p, k: (p, ls, 0, 0))
    return pl.pallas_call(
        body, name="ffn_wgrad", grid=(N_CHIPS, nk),
        in_specs=[tok, hid, hid, hid, tok, anyspec, anyspec, anyspec],
        out_specs=[orow, orow, orow],
        out_shape=[jax.ShapeDtypeStruct(gwg.shape, bf16), jax.ShapeDtypeStruct(gwu.shape, bf16),
                   jax.ShapeDtypeStruct(gwd.shape, bf16)],
        scratch_shapes=[pltpu.VMEM((Fs, D), f32), pltpu.VMEM((Fs, D), f32), pltpu.VMEM((Fs, D), f32)],
        input_output_aliases={5: 0, 6: 1, 7: 2},
        compiler_params=_cp(("parallel", "arbitrary"), 48),
    )(h, dg, du, a, df, gwg, gwu, gwd)


_LANES = 128
_QKV_BLOCKS = D_ATT // _LANES


def _res_spec(lead, d, width, index):
    return pl.BlockSpec((lead, d, TOK_TILE // d, width), index)


def _res_spec3(d, width):
    return pl.BlockSpec((d, TOK_TILE // d, width), lambda i: (0, i, 0))


def _rows_to_residues(tile_bufs, d, put):
    for r in range(d):
        for cb, buf in enumerate(tile_bufs):
            put(r, cb, buf[pl.ds(r, TOK_TILE // d, stride=d), :])


def _residues_to_rows(tile_bufs, d, get):
    for r in range(d):
        for cb, buf in enumerate(tile_bufs):
            buf[pl.ds(r, TOK_TILE // d, stride=d), :] = get(r, cb)


def _mix_in_fwd(x, mod3, w_in, l):
    S, D = x.shape
    N = w_in.shape[-1]
    ts = TOK_TILE

    def body(x_ref, mod_ref, w_ref, o1_ref, o4_ref, o16_ref, zr_ref, h_ref, *bufs):
        j = pl.program_id(1)

        @pl.when(j == 0)
        def _():
            xh, _ = _ln_stats(x_ref[...])
            h_ref[...] = (xh * (1.0 + mod_ref[1:2, :]) + mod_ref[0:1, :]).astype(bf16)

        z = _dot(h_ref[...], w_ref[0, 0])

        @pl.when(j == N_CHIPS - 1)
        def _():
            zr_ref[...] = z

        @pl.when(j < N_CHIPS - 1)
        def _():
            zz = z * jnp.where(j == 0, HEAD_DIM ** -0.5, 1.0)
            o1_ref[j, 0] = zz.astype(bf16)
            for cb, buf in enumerate(bufs):
                buf[...] = zz[:, _LANES * cb:_LANES * (cb + 1)]
            for d, o_ref in zip(DILATIONS[1:], (o4_ref, o16_ref)):
                def put(r, cb, piece, o_ref=o_ref):
                    o_ref[j, r, :, _LANES * cb:_LANES * (cb + 1)] = piece.astype(bf16)
                _rows_to_residues(bufs, d, put)

    tok = pl.BlockSpec((ts, D), lambda i, j: (i, 0))
    res = [_res_spec(3, d, N, lambda i, j: (0, 0, i, 0)) for d in DILATIONS]
    return pl.pallas_call(
        body, name="mix_in_fwd", grid=(S // ts, N_CHIPS),
        in_specs=[tok, _full((3, D)), pl.BlockSpec((1, 1, D, N), lambda i, j: (j, l, 0, 0))],
        out_specs=res + [pl.BlockSpec((ts, N), lambda i, j: (i, 0)), tok],
        out_shape=[jax.ShapeDtypeStruct((3, d, S // d, N), bf16) for d in DILATIONS]
        + [jax.ShapeDtypeStruct((S, N), f32), jax.ShapeDtypeStruct((S, D), bf16)],
        scratch_shapes=[pltpu.VMEM((ts, _LANES), f32)] * _QKV_BLOCKS,
        compiler_params=_cp(("parallel", "arbitrary"), 40),
    )(x, mod3, w_in)


def _mix_in_bwd(dqkv, d_rest, dx_res, x, mod3, w_in, l):
    S, D = x.shape
    N = w_in.shape[-1]
    ts = TOK_TILE

    def body(d1_ref, d4_ref, d16_ref, dr_ref, dxr_ref, x_ref, mod_ref, w_ref, dx_ref, dmod_ref, dz_ref, acc_sc, *bufs):
        i = pl.program_id(0)
        j = pl.program_id(1)

        @pl.when((i == 0) & (j == 0))
        def _():
            dmod_ref[...] = jnp.zeros_like(dmod_ref)

        @pl.when(j == 0)
        def _():
            acc_sc[...] = jnp.zeros_like(acc_sc)

        @pl.when(j == N_CHIPS - 1)
        def _():
            dz_ref[0] = dr_ref[...]

        @pl.when(j < N_CHIPS - 1)
        def _():
            for d, d_ref, tile_bufs in ((4, d4_ref, bufs[:_QKV_BLOCKS]), (16, d16_ref, bufs[_QKV_BLOCKS:])):
                _residues_to_rows(tile_bufs, d, lambda r, cb, d_ref=d_ref: d_ref[0, r, :, _LANES * cb:_LANES * (cb + 1)].astype(f32))
            for cb in range(_QKV_BLOCKS):
                cols = slice(_LANES * cb, _LANES * (cb + 1))
                dz_ref[0, :, cols] = (d1_ref[0, 0, :, cols].astype(f32) + bufs[cb][...] + bufs[_QKV_BLOCKS + cb][...]).astype(bf16)

        acc_sc[...] += _dot_nt(dz_ref[0], w_ref[0, 0])

        @pl.when(j == N_CHIPS - 1)
        def _():
            dh = acc_sc[...]
            xh, rstd0 = _ln_stats(x_ref[...])
            dmod_ref[0:1, :] += jnp.sum(dh, 0, keepdims=True)
            dmod_ref[1:2, :] += jnp.sum(dh * xh, 0, keepdims=True)
            dx_ref[...] = _ln_bwd(dh * (1.0 + mod_ref[1:2, :]), xh, rstd0) + dxr_ref[...]

    tok = pl.BlockSpec((ts, D), lambda i, j: (i, 0))
    res = [_res_spec(1, d, N, lambda i, j: (jnp.minimum(j, 2), 0, i, 0)) for d in DILATIONS]
    return pl.pallas_call(
        body, name="mix_in_bwd", grid=(S // ts, N_CHIPS),
        in_specs=res + [pl.BlockSpec((ts, N), lambda i, j: (i, 0)), tok, tok, _full((3, D)),
                        pl.BlockSpec((1, 1, D, N), lambda i, j: (j, l, 0, 0))],
        out_specs=[tok, _full((3, D)), pl.BlockSpec((1, ts, N), lambda i, j: (j, i, 0))],
        out_shape=[jax.ShapeDtypeStruct((S, D), f32), jax.ShapeDtypeStruct((3, D), f32),
                   jax.ShapeDtypeStruct((N_CHIPS, S, N), bf16)],
        scratch_shapes=[pltpu.VMEM((ts, D), f32)] + [pltpu.VMEM((ts, _LANES), f32)] * (2 * _QKV_BLOCKS),
        compiler_params=_cp(("arbitrary", "arbitrary"), 40),
    )(*dqkv, d_rest, dx_res, x, mod3, w_in)


def _mix_in_wgrad(h, dz, gw, l):
    S, D = h.shape
    N = dz.shape[-1]
    tk = TOK_TILE
    nk = S // tk

    def body(h_ref, dz_ref, _g, gw_ref, acc_sc):
        k = pl.program_id(1)

        @pl.when(k == 0)
        def _():
            acc_sc[...] = jnp.zeros_like(acc_sc)

        acc_sc[...] += _dot_tn(h_ref[...], dz_ref[0])

        @pl.when(k == nk - 1)
        def _():
            gw_ref[0, 0] = acc_sc[...].astype(bf16)

    return pl.pallas_call(
        body, name="mix_in_wgrad", grid=(N_CHIPS, nk),
        in_specs=[pl.BlockSpec((tk, D), lambda p, k: (k, 0)), pl.BlockSpec((1, tk, N), lambda p, k: (p, k, 0)),
                  pl.BlockSpec(memory_space=pl.ANY)],
        out_specs=pl.BlockSpec((1, 1, D, N), lambda p, k: (p, l, 0, 0)),
        out_shape=jax.ShapeDtypeStruct(gw.shape, bf16),
        scratch_shapes=[pltpu.VMEM((D, N), f32)],
        input_output_aliases={2: 0},
        compiler_params=_cp(("parallel", "arbitrary"), 40),
    )(h, dz, gw)


def _mix_out_fwd(x, y_att, y_ssm, y_pool, mod3, w_out, l, lng, lnb):
    S, D = x.shape
    ts = TOK_TILE

    def body(x_ref, ya_ref, ys_ref, yp_ref, mod_ref, w_ref, lng_ref, lnb_ref, xo_ref, y_ref):
        ya = ya_ref[...].astype(bf16)
        y = (_dot(ya[:, 0:256], w_ref[0, 0]) + _dot(ya[:, 256:512], w_ref[1, 0])
             + _dot(ys_ref[...].astype(bf16), w_ref[2, 0]) + _dot(yp_ref[...].astype(bf16), w_ref[3, 0]))
        y_ref[...] = y
        rh, _ = _ln_stats(ALPHA * x_ref[...] + mod_ref[2:3, :] * y)
        xo_ref[...] = rh * lng_ref[...] + lnb_ref[...]

    tok = pl.BlockSpec((ts, D), lambda i: (i, 0))
    return pl.pallas_call(
        body, name="mix_out_fwd", grid=(S // ts,),
        in_specs=[tok, pl.BlockSpec((ts, D_ATT), lambda i: (i, 0)), pl.BlockSpec((ts, D_SSM), lambda i: (i, 0)),
                  pl.BlockSpec((ts, D_POOL), lambda i: (i, 0)), _full((3, D)),
                  pl.BlockSpec((N_CHIPS, 1, 256, D), lambda i: (0, l, 0, 0)), _full((1, D)), _full((1, D))],
        out_specs=[tok, tok],
        out_shape=[jax.ShapeDtypeStruct((S, D), f32), jax.ShapeDtypeStruct((S, D), f32)],
        compiler_params=_cp(("parallel",), 40),
    )(x, y_att, y_ssm, y_pool, mod3, w_out, lng, lnb)


def _mix_out_bwd(dxo, x, y, y_att, y_ssm, y_pool, mod3, w_out, l, lng, gw_out):
    S, D = x.shape
    ts = TOK_TILE
    nt = S // ts

    def body(dxo_ref, x_ref, y_ref, ya_ref, ys_ref, yp_ref, mod_ref, w_ref, lng_ref, _g,
             dxr_ref, da_ref, ds_ref, dp_ref, dgate_ref, dlng_ref, dlnb_ref, gw_ref, acc_sc):
        i = pl.program_id(0)

        @pl.when(i == 0)
        def _():
            dgate_ref[...] = jnp.zeros_like(dgate_ref)
            dlng_ref[...] = jnp.zeros_like(dlng_ref)
            dlnb_ref[...] = jnp.zeros_like(dlnb_ref)
            acc_sc[...] = jnp.zeros_like(acc_sc)

        gate = mod_ref[2:3, :]
        yv = y_ref[...]
        rh, rstd = _ln_stats(ALPHA * x_ref[...] + gate * yv)
        dy_out = dxo_ref[...]
        dlng_ref[...] += jnp.sum(dy_out * rh, 0, keepdims=True)
        dlnb_ref[...] += jnp.sum(dy_out, 0, keepdims=True)
        dr = _ln_bwd(dy_out * lng_ref[...], rh, rstd)
        dxr_ref[...] = ALPHA * dr
        dgate_ref[...] += jnp.sum(dr * yv, 0, keepdims=True)
        dy = (gate * dr).astype(bf16)
        da_ref[:, 0:256] = _dot_nt(dy, w_ref[0, 0])
        da_ref[:, 256:512] = _dot_nt(dy, w_ref[1, 0])
        ds_ref[...] = _dot_nt(dy, w_ref[2, 0])
        dp_ref[...] = _dot_nt(dy, w_ref[3, 0])
        ya = ya_ref[...].astype(bf16)
        acc_sc[0] += _dot_tn(ya[:, 0:256], dy)
        acc_sc[1] += _dot_tn(ya[:, 256:512], dy)
        acc_sc[2] += _dot_tn(ys_ref[...].astype(bf16), dy)
        acc_sc[3] += _dot_tn(yp_ref[...].astype(bf16), dy)

        @pl.when(i == nt - 1)
        def _():
            gw_ref[:, 0] = acc_sc[...].astype(bf16)

    tok = pl.BlockSpec((ts, D), lambda i: (i, 0))
    t512 = pl.BlockSpec((ts, D_ATT), lambda i: (i, 0))
    t256 = pl.BlockSpec((ts, 256), lambda i: (i, 0))
    wspec = pl.BlockSpec((N_CHIPS, 1, 256, D), lambda i: (0, l, 0, 0))
    return pl.pallas_call(
        body, name="mix_out_bwd", grid=(nt,),
        in_specs=[tok, tok, tok, t512, t256, t256, _full((3, D)), wspec, _full((1, D)), pl.BlockSpec(memory_space=pl.ANY)],
        out_specs=[tok, t512, t256, t256, _full((1, D)), _full((1, D)), _full((1, D)), wspec],
        out_shape=[jax.ShapeDtypeStruct((S, D), f32), jax.ShapeDtypeStruct((S, D_ATT), f32),
                   jax.ShapeDtypeStruct((S, D_SSM), f32), jax.ShapeDtypeStruct((S, D_POOL), f32),
                   jax.ShapeDtypeStruct((1, D), f32), jax.ShapeDtypeStruct((1, D), f32), jax.ShapeDtypeStruct((1, D), f32),
                   jax.ShapeDtypeStruct(gw_out.shape, bf16)],
        scratch_shapes=[pltpu.VMEM((N_CHIPS, 256, D), f32)],
        input_output_aliases={9: 7},
        compiler_params=_cp(("arbitrary",), 48),
    )(dxo, x, y, y_att, y_ssm, y_pool, mod3, w_out, lng, gw_out)


def _t5_bucket(dist):
    max_exact = N_BUCKETS // 2
    d = np.maximum(dist, 1).astype(np.float32)
    large = max_exact + (np.log(d / max_exact) / math.log(MAX_DISTANCE / max_exact)
                         * (N_BUCKETS - max_exact)).astype(np.int32)
    large = np.minimum(large, N_BUCKETS - 1)
    return np.where(dist < max_exact, dist, large).astype(np.int32)


def _bucket_table():
    q = ATT_BLOCK
    i = np.arange(q)[:, None]
    j = np.arange(2 * q)[None, :]
    r = i + q - j
    in_band = (r >= 0) & (r <= q)
    tabs = [np.where(in_band, _t5_bucket(np.clip(r, 0, None) * d), -1) for d in DILATIONS]
    return np.stack(tabs).astype(np.int32)


def _bias_fwd(rel_bias, table):
    def body(rb_ref, tab_ref, out_ref):
        for b in range(3):
            tb = tab_ref[b]
            for h in range(N_HEADS):
                def pick(k, acc):
                    return jnp.where(tb == k, rb_ref[k, h], acc)
                out_ref[b, h] = lax.fori_loop(0, N_BUCKETS, pick, jnp.where(tb < 0, NEG, 0.0).astype(f32))

    return pl.pallas_call(
        body, name="bias_fwd",
        in_specs=[pl.BlockSpec(memory_space=pltpu.SMEM), pl.BlockSpec(memory_space=pltpu.VMEM)],
        out_specs=pl.BlockSpec(memory_space=pltpu.VMEM),
        out_shape=jax.ShapeDtypeStruct((3, N_HEADS, ATT_BLOCK, 2 * ATT_BLOCK), f32),
    )(rel_bias, table)


def _bias_bwd(dbias, table):
    def body(db_ref, tab_ref, out_ref):
        def per_bucket(k, c):
            for h in range(N_HEADS):
                tot = jnp.zeros((), f32)
                for b in range(3):
                    tot = tot + jnp.sum(jnp.where(tab_ref[b] == k, db_ref[b, h], 0.0))
                out_ref[k, h] = tot
            return c
        lax.fori_loop(0, N_BUCKETS, per_bucket, 0)

    return pl.pallas_call(
        body, name="bias_bwd",
        in_specs=[pl.BlockSpec(memory_space=pltpu.VMEM), pl.BlockSpec(memory_space=pltpu.VMEM)],
        out_specs=pl.BlockSpec(memory_space=pltpu.SMEM),
        out_shape=jax.ShapeDtypeStruct((N_BUCKETS, N_HEADS), f32),
    )(dbias, table)


def _att_unit(u, nbr):
    rows = pl.ds(pl.multiple_of(u * ATT_BLOCK, ATT_BLOCK), ATT_BLOCK)
    prev = pl.ds(pl.multiple_of(jnp.maximum(u - 1, 0) * ATT_BLOCK, ATT_BLOCK), ATT_BLOCK)
    return rows, prev, (u % nbr) != 0


_HEAD_ROWS = N_HEADS * ATT_BLOCK


def _head_rows(t):
    lane = lax.broadcasted_iota(jnp.int32, t.shape, 1)
    return jnp.concatenate([jnp.where((lane >= HEAD_DIM * h) & (lane < HEAD_DIM * (h + 1)), t, jnp.zeros_like(t))
                            for h in range(N_HEADS)], axis=0)


def _head_cols(big):
    lane = lax.broadcasted_iota(jnp.int32, (ATT_BLOCK, D_ATT), 1)
    out = big[0:ATT_BLOCK]
    for h in range(1, N_HEADS):
        out = jnp.where(lane >= HEAD_DIM * h, big[ATT_BLOCK * h:ATT_BLOCK * (h + 1)], out)
    return out


def _head_column(ref, rows):
    t = ref[rows, :]
    return jnp.concatenate([t[:, h:h + 1] for h in range(N_HEADS)], axis=0)


def _att_band(ref, rows, prev, nbr):
    cur = ref[0, rows, :]
    return cur if nbr == 1 else jnp.concatenate([ref[0, prev, :], cur], axis=0)


def _att_scores(q_ref, k_ref, b_ref, rows, prev, valid_prev, nbr):
    qbd = _head_rows(q_ref[0, rows, :])
    kb = _att_band(k_ref, rows, prev, nbr)
    bias = b_ref[0].reshape(_HEAD_ROWS, 2 * ATT_BLOCK)
    if nbr == 1:
        return qbd, kb, _dot_nt(qbd, kb) + bias[:, ATT_BLOCK:]
    s = _dot_nt(qbd, kb) + bias
    col = lax.broadcasted_iota(jnp.int32, s.shape, 1)
    return qbd, kb, jnp.where((col >= ATT_BLOCK) | valid_prev, s, NEG)


def _qkv_specs(S, branch):
    return ([pl.BlockSpec((1, S, D_ATT), lambda i, t=t: (t, 0, 0)) for t in range(3)],
            pl.BlockSpec((1, N_HEADS, ATT_BLOCK, 2 * ATT_BLOCK), lambda i: (branch, 0, 0, 0)))


def _att_fwd(qkv, bias, branch):
    S = qkv.shape[1]
    nbr = BLOCKS_PER_RESIDUE[branch]

    def body(q_ref, k_ref, v_ref, b_ref, o_ref, lse_ref):
        lse_ref[...] = jnp.zeros_like(lse_ref)

        def unit(u, c):
            rows, prev, valid_prev = _att_unit(u, nbr)
            _, _, s = _att_scores(q_ref, k_ref, b_ref, rows, prev, valid_prev, nbr)
            m = jnp.max(s, -1, keepdims=True)
            p = jnp.exp(s - m)
            den = jnp.sum(p, -1, keepdims=True)
            big = _dot(p.astype(bf16), _att_band(v_ref, rows, prev, nbr))
            o_ref[rows, :] = _head_cols(big / den)
            lse = m + jnp.log(den)
            for h in range(N_HEADS):
                lse_ref[rows, pl.ds(h, 1)] = lse[ATT_BLOCK * h:ATT_BLOCK * (h + 1)]
            return c

        lax.fori_loop(0, N_UNITS, unit, 0)

    qkv_specs, bspec = _qkv_specs(S, branch)
    return pl.pallas_call(
        body, name="att_fwd", grid=(1,),
        in_specs=qkv_specs + [bspec],
        out_specs=[pl.BlockSpec((S, D_ATT), lambda i: (0, 0)), pl.BlockSpec((S, _LANES), lambda i: (0, 0))],
        out_shape=[jax.ShapeDtypeStruct((S, D_ATT), f32), jax.ShapeDtypeStruct((S, _LANES), f32)],
        compiler_params=_cp(("arbitrary",), 40),
    )(qkv, qkv, qkv, bias)


def _att_bwd(qkv, do, lse, crow, bias, branch):
    S = qkv.shape[1]
    nbr = BLOCKS_PER_RESIDUE[branch]

    def body(q_ref, k_ref, v_ref, do_ref, lse_ref, c_ref, b_ref, dqkv_ref, db_ref, dk_sc, dv_sc):
        dk_sc[...] = jnp.zeros_like(dk_sc)
        dv_sc[...] = jnp.zeros_like(dv_sc)
        db_ref[...] = jnp.zeros_like(db_ref)

        def unit(u, c):
            rows, prev, valid_prev = _att_unit(u, nbr)
            qbd, kb, s = _att_scores(q_ref, k_ref, b_ref, rows, prev, valid_prev, nbr)
            p = jnp.exp(s - _head_column(lse_ref, rows))
            dobd = _head_rows(do_ref[rows, :])
            ds = p * (_dot_nt(dobd, _att_band(v_ref, rows, prev, nbr)) - _head_column(c_ref, rows))
            if nbr == 1:
                db_ref[:, :, ATT_BLOCK:] += ds.reshape(N_HEADS, ATT_BLOCK, ATT_BLOCK)
            else:
                db_ref[...] += ds.reshape(N_HEADS, ATT_BLOCK, 2 * ATT_BLOCK)
            dsb = ds.astype(bf16)
            dqkv_ref[0, rows, :] = (HEAD_DIM ** -0.5 * _head_cols(_dot(dsb, kb))).astype(bf16)
            dkb = _dot_tn(dsb, qbd)
            dvb = _dot_tn(p.astype(bf16), dobd)
            if nbr == 1:
                dk_sc[rows, :] += dkb
                dv_sc[rows, :] += dvb
            else:
                dk_sc[prev, :] += dkb[:ATT_BLOCK]
                dv_sc[prev, :] += dvb[:ATT_BLOCK]
                dk_sc[rows, :] += dkb[ATT_BLOCK:]
                dv_sc[rows, :] += dvb[ATT_BLOCK:]
            return c

        lax.fori_loop(0, N_UNITS, unit, 0)
        dqkv_ref[1] = dk_sc[...].astype(bf16)
        dqkv_ref[2] = dv_sc[...].astype(bf16)

    qkv_specs, bspec = _qkv_specs(S, branch)
    row = pl.BlockSpec((S, _LANES), lambda i: (0, 0))
    return pl.pallas_call(
        body, name="att_bwd", grid=(1,),
        in_specs=qkv_specs + [pl.BlockSpec((S, D_ATT), lambda i: (0, 0)), row, row, bspec],
        out_specs=[pl.BlockSpec((3, S, D_ATT), lambda i: (0, 0, 0)),
                   pl.BlockSpec((N_HEADS, ATT_BLOCK, 2 * ATT_BLOCK), lambda i: (0, 0, 0))],
        out_shape=[jax.ShapeDtypeStruct((3, S, D_ATT), bf16), jax.ShapeDtypeStruct((N_HEADS, ATT_BLOCK, 2 * ATT_BLOCK), f32)],
        scratch_shapes=[pltpu.VMEM((S, D_ATT), f32), pltpu.VMEM((S, D_ATT), f32)],
        compiler_params=_cp(("arbitrary",), 48),
    )(qkv, qkv, qkv, do, lse, crow, bias)


def _branch_weights(lse_ref):
    l0, l1, l2 = lse_ref[0], lse_ref[1], lse_ref[2]
    m = jnp.maximum(jnp.maximum(l0, l1), l2)
    e0, e1, e2 = jnp.exp(l0 - m), jnp.exp(l1 - m), jnp.exp(l2 - m)
    tot = e0 + e1 + e2
    return e0 / tot, e1 / tot, e2 / tot


def _att_merge(os, lses):
    S = os[0].shape[0] * os[0].shape[1]
    ts = TOK_TILE

    def body(o1_ref, o4_ref, o16_ref, l1_ref, l4_ref, l16_ref, y_ref, lt_ref, *bufs):
        obufs = (bufs[:_QKV_BLOCKS], bufs[_QKV_BLOCKS:2 * _QKV_BLOCKS])
        lt_ref[0] = l1_ref[0]
        for k, (d, o_ref, l_ref) in enumerate(((4, o4_ref, l4_ref), (16, o16_ref, l16_ref))):
            _residues_to_rows(obufs[k], d, lambda r, cb, o_ref=o_ref: o_ref[r, :, _LANES * cb:_LANES * (cb + 1)])
            _residues_to_rows([bufs[2 * _QKV_BLOCKS + k]], d, lambda r, cb, l_ref=l_ref: l_ref[r])
            lt_ref[1 + k] = bufs[2 * _QKV_BLOCKS + k][...]
        w = _branch_weights(lt_ref)
        for h in range(N_HEADS):
            cs = slice(HEAD_DIM * h, HEAD_DIM * (h + 1))
            half = slice(HEAD_DIM * (h % 2), HEAD_DIM * (h % 2 + 1))
            y_ref[:, cs] = (w[0][:, h:h + 1] * o1_ref[0, :, cs] + w[1][:, h:h + 1] * obufs[0][h // 2][:, half]
                            + w[2][:, h:h + 1] * obufs[1][h // 2][:, half])

    return pl.pallas_call(
        body, name="att_merge", grid=(S // ts,),
        in_specs=[_res_spec3(d, D_ATT) for d in DILATIONS] + [_res_spec3(d, _LANES) for d in DILATIONS],
        out_specs=[pl.BlockSpec((ts, D_ATT), lambda i: (i, 0)), pl.BlockSpec((3, ts, _LANES), lambda i: (0, i, 0))],
        out_shape=[jax.ShapeDtypeStruct((S, D_ATT), f32), jax.ShapeDtypeStruct((3, S, _LANES), f32)],
        scratch_shapes=[pltpu.VMEM((ts, _LANES), f32)] * (2 * _QKV_BLOCKS + 2),
        compiler_params=_cp(("parallel",)),
    )(*os, *lses)


def _att_merge_bwd(dy, y, lse3):
    S = dy.shape[0]
    ts = TOK_TILE

    def body(dy_ref, y_ref, lse_ref, do1_ref, do4_ref, do16_ref, c1_ref, c4_ref, c16_ref, *bufs):
        dobufs = (bufs[:_QKV_BLOCKS], bufs[_QKV_BLOCKS:2 * _QKV_BLOCKS], bufs[2 * _QKV_BLOCKS:3 * _QKV_BLOCKS])
        cbufs = bufs[3 * _QKV_BLOCKS:]
        w = _branch_weights(lse_ref)
        for cb in cbufs:
            cb[...] = jnp.zeros_like(cb)
        for h in range(N_HEADS):
            cs = slice(HEAD_DIM * h, HEAD_DIM * (h + 1))
            half = slice(HEAD_DIM * (h % 2), HEAD_DIM * (h % 2 + 1))
            dyh = dy_ref[:, cs]
            t = jnp.sum(dyh * y_ref[:, cs], -1, keepdims=True)
            for p in range(3):
                wp = w[p][:, h:h + 1]
                dobufs[p][h // 2][:, half] = wp * dyh
                cbufs[p][:, h:h + 1] = wp * t
        for cb in range(_QKV_BLOCKS):
            do1_ref[0, :, _LANES * cb:_LANES * (cb + 1)] = dobufs[0][cb][...].astype(bf16)
        c1_ref[0] = cbufs[0][...]
        for k, (d, do_ref, c_ref) in enumerate(((4, do4_ref, c4_ref), (16, do16_ref, c16_ref))):
            def put_do(r, cb, piece, do_ref=do_ref):
                do_ref[r, :, _LANES * cb:_LANES * (cb + 1)] = piece.astype(bf16)

            def put_c(r, cb, piece, c_ref=c_ref):
                c_ref[r] = piece

            _rows_to_residues(dobufs[1 + k], d, put_do)
            _rows_to_residues([cbufs[1 + k]], d, put_c)

    return pl.pallas_call(
        body, name="att_merge_bwd", grid=(S // ts,),
        in_specs=[pl.BlockSpec((ts, D_ATT), lambda i: (i, 0)), pl.BlockSpec((ts, D_ATT), lambda i: (i, 0)),
                  pl.BlockSpec((3, ts, _LANES), lambda i: (0, i, 0))],
        out_specs=[_res_spec3(d, D_ATT) for d in DILATIONS] + [_res_spec3(d, _LANES) for d in DILATIONS],
        out_shape=[jax.ShapeDtypeStruct((d, S // d, D_ATT), bf16) for d in DILATIONS]
        + [jax.ShapeDtypeStruct((d, S // d, _LANES), f32) for d in DILATIONS],
        scratch_shapes=[pltpu.VMEM((ts, _LANES), f32)] * (3 * _QKV_BLOCKS + 3),
        compiler_params=_cp(("parallel",)),
    )(dy, y, lse3)


_SSM_ROWS = 256


def _scan_in_place(sr_ref, si_ref, a_ref, reverse):
    S, N = sr_ref.shape
    nst = S // SCAN_SEG
    ar = jnp.broadcast_to(a_ref[0:1, :], (SCAN_SEG, N))
    ai = jnp.broadcast_to(a_ref[1:2, :], (SCAN_SEG, N))
    if reverse:
        ai = -ai
    row = lax.broadcasted_iota(jnp.int32, (SCAN_SEG, N), 0)
    zero = jnp.zeros((SCAN_SEG, N), f32)

    def tile(t):
        return pl.ds(pl.multiple_of((nst - 1 - t if reverse else t) * SCAN_SEG, SCAN_SEG), SCAN_SEG)

    def local(t, c):
        sr, si, pr, pi = c
        rows = tile(t)
        nsr = ar * sr - ai * si + sr_ref[rows, :]
        nsi = ar * si + ai * sr + si_ref[rows, :]
        sr_ref[rows, :] = nsr
        si_ref[rows, :] = nsi
        return nsr, nsi, ar * pr - ai * pi, ar * pi + ai * pr

    fr, fi, apr, api = lax.fori_loop(0, nst, local, (zero, zero, zero + 1.0, zero))

    def shift(v):
        if reverse:
            return jnp.where(row == SCAN_SEG - 1, 0.0, pltpu.roll(v, SCAN_SEG - 1, axis=0))
        return jnp.where(row == 0, 0.0, pltpu.roll(v, 1, axis=0))

    cr, ci = zero, zero
    for _ in range(SCAN_SEG - 1):
        cr, ci = shift(fr + apr * cr - api * ci), shift(fi + apr * ci + api * cr)

    def fix(t, c):
        pr, pi = c
        npr, npi = ar * pr - ai * pi, ar * pi + ai * pr
        rows = tile(t)
        sr_ref[rows, :] += npr * cr - npi * ci
        si_ref[rows, :] += npr * ci + npi * cr
        return npr, npi

    lax.fori_loop(0, nst, fix, (zero + 1.0, zero))


def _ssm_states(u, bre, bim, a2):
    S = u.shape[0]

    def body(u_ref, br_ref, bi_ref, a_ref, sr_ref, si_ref):
        brb = br_ref[...].astype(bf16)
        bib = bi_ref[...].astype(bf16)

        def project(t, c):
            rows = pl.ds(pl.multiple_of(t * _SSM_ROWS, _SSM_ROWS), _SSM_ROWS)
            ub = u_ref[rows, :].astype(bf16)
            sr_ref[rows, :] = _dot(ub, brb)
            si_ref[rows, :] = _dot(ub, bib)
            return c

        lax.fori_loop(0, S // _SSM_ROWS, project, 0)
        _scan_in_place(sr_ref, si_ref, a_ref, False)

    vm = pl.BlockSpec(memory_space=pltpu.VMEM)
    return pl.pallas_call(
        body, name="ssm_states", in_specs=[vm] * 4, out_specs=[vm, vm],
        out_shape=[jax.ShapeDtypeStruct((S, D_STATE), f32)] * 2,
        compiler_params=_cp(None, 48),
    )(u, bre, bim, a2)


def _ssm_out(sr, si, u, cre, cim, dskip, glu_w, glu_b):
    S = u.shape[0]
    ts = TOK_TILE

    def body(sr_ref, si_ref, u_ref, cr_ref, ci_ref, d_ref, w_ref, b_ref, out_ref, y_ref):
        y = (_dot(sr_ref[...].astype(bf16), cr_ref[...].astype(bf16))
             - _dot(si_ref[...].astype(bf16), ci_ref[...].astype(bf16)) + d_ref[...] * u_ref[...])
        y_ref[...] = y
        z = _dot(_gelu(y).astype(bf16), w_ref[...].astype(bf16)) + b_ref[...]
        out_ref[...] = y * jax.nn.sigmoid(z)

    st = pl.BlockSpec((ts, D_STATE), lambda i: (i, 0))
    ch = pl.BlockSpec((ts, D_SSM), lambda i: (i, 0))
    return pl.pallas_call(
        body, name="ssm_out", grid=(S // ts,),
        in_specs=[st, st, ch, _full((D_STATE, D_SSM)), _full((D_STATE, D_SSM)), _full((1, D_SSM)),
                  _full((D_SSM, D_SSM)), _full((1, D_SSM))],
        out_specs=[ch, ch],
        out_shape=[jax.ShapeDtypeStruct((S, D_SSM), f32)] * 2,
        compiler_params=_cp(("parallel",)),
    )(sr, si, u, cre, cim, dskip, glu_w, glu_b)


def _ssm_out_bwd(dout, y, u, sr, si, dskip, glu_w, glu_b):
    S = u.shape[0]
    ts = TOK_TILE

    def body(do_ref, y_ref, u_ref, sr_ref, si_ref, d_ref, w_ref, b_ref,
             dy_ref, du_ref, dcr_ref, dci_ref, dd_ref, dgb_ref, dgw_ref):
        @pl.when(pl.program_id(0) == 0)
        def _():
            for r in (dcr_ref, dci_ref, dd_ref, dgb_ref, dgw_ref):
                r[...] = jnp.zeros_like(r)

        y = y_ref[...]
        dout = do_ref[...]
        wb = w_ref[...].astype(bf16)
        ge = _gelu(y).astype(bf16)
        sz = jax.nn.sigmoid(_dot(ge, wb) + b_ref[...])
        dz = dout * y * sz * (1.0 - sz)
        dzb = dz.astype(bf16)
        dgb_ref[...] += jnp.sum(dz, 0, keepdims=True)
        dgw_ref[...] += _dot_tn(ge, dzb)
        dy = dout * sz + _gelu_grad(y) * _dot_nt(dzb, wb)
        uv = u_ref[...]
        dd_ref[...] += jnp.sum(dy * uv, 0, keepdims=True)
        du_ref[...] = dy * d_ref[...]
        dy_ref[...] = dy
        dyb = dy.astype(bf16)
        dcr_ref[...] += _dot_tn(sr_ref[...].astype(bf16), dyb)
        dci_ref[...] -= _dot_tn(si_ref[...].astype(bf16), dyb)

    st = pl.BlockSpec((ts, D_STATE), lambda i: (i, 0))
    ch = pl.BlockSpec((ts, D_SSM), lambda i: (i, 0))
    c_full = _full((D_STATE, D_SSM))
    return pl.pallas_call(
        body, name="ssm_out_bwd", grid=(S // ts,),
        in_specs=[ch, ch, ch, st, st, _full((1, D_SSM)), _full((D_SSM, D_SSM)), _full((1, D_SSM))],
        out_specs=[ch, ch, c_full, c_full, _full((1, D_SSM)), _full((1, D_SSM)), _full((D_SSM, D_SSM))],
        out_shape=[jax.ShapeDtypeStruct((S, D_SSM), f32), jax.ShapeDtypeStruct((S, D_SSM), f32),
                   jax.ShapeDtypeStruct((D_STATE, D_SSM), f32), jax.ShapeDtypeStruct((D_STATE, D_SSM), f32),
                   jax.ShapeDtypeStruct((1, D_SSM), f32), jax.ShapeDtypeStruct((1, D_SSM), f32),
                   jax.ShapeDtypeStruct((D_SSM, D_SSM), f32)],
        compiler_params=_cp(("arbitrary",), 40),
    )(dout, y, u, sr, si, dskip, glu_w, glu_b)


def _ssm_states_bwd(dy, du_skip, u, sr, si, cre, cim, bre, bim, a2):
    S = u.shape[0]
    N = D_STATE
    nst = S // SCAN_SEG
    nproj = S // _SSM_ROWS

    def body(dy_ref, dus_ref, u_ref, sr_ref, si_ref, cr_ref, ci_ref, br_ref, bi_ref, a_ref,
             du_ref, dbr_ref, dbi_ref, da_ref, lr_ref, li_ref):
        crb = cr_ref[...].astype(bf16)
        cib = ci_ref[...].astype(bf16)

        def project(t, c):
            rows = pl.ds(pl.multiple_of(t * _SSM_ROWS, _SSM_ROWS), _SSM_ROWS)
            dyb = dy_ref[rows, :].astype(bf16)
            lr_ref[rows, :] = _dot_nt(dyb, crb)
            li_ref[rows, :] = -_dot_nt(dyb, cib)
            return c

        lax.fori_loop(0, nproj, project, 0)
        _scan_in_place(lr_ref, li_ref, a_ref, True)

        row = lax.broadcasted_iota(jnp.int32, (SCAN_SEG, N), 0)
        last = pl.ds((nst - 1) * SCAN_SEG, SCAN_SEG)
        pr = jnp.where(row == 0, 0.0, pltpu.roll(sr_ref[last, :], 1, axis=0))
        pi = jnp.where(row == 0, 0.0, pltpu.roll(si_ref[last, :], 1, axis=0))
        first = pl.ds(0, SCAN_SEG)
        acc_r = lr_ref[first, :] * pr + li_ref[first, :] * pi
        acc_i = li_ref[first, :] * pr - lr_ref[first, :] * pi

        def step(t, c):
            acc_r, acc_i = c
            rows = pl.ds(pl.multiple_of(t * SCAN_SEG, SCAN_SEG), SCAN_SEG)
            prev = pl.ds(pl.multiple_of((t - 1) * SCAN_SEG, SCAN_SEG), SCAN_SEG)
            lrv, liv, srv, siv = lr_ref[rows, :], li_ref[rows, :], sr_ref[prev, :], si_ref[prev, :]
            return acc_r + lrv * srv + liv * siv, acc_i + liv * srv - lrv * siv

        acc_r, acc_i = lax.fori_loop(1, nst, step, (acc_r, acc_i))
        da_ref[0:1, :] = jnp.sum(acc_r, 0, keepdims=True)
        da_ref[1:2, :] = jnp.sum(acc_i, 0, keepdims=True)

        brb = br_ref[...].astype(bf16)
        bib = bi_ref[...].astype(bf16)
        dbr_ref[...] = jnp.zeros_like(dbr_ref)
        dbi_ref[...] = jnp.zeros_like(dbi_ref)

        def back(t, c):
            rows = pl.ds(pl.multiple_of(t * _SSM_ROWS, _SSM_ROWS), _SSM_ROWS)
            lrb = lr_ref[rows, :].astype(bf16)
            lib = li_ref[rows, :].astype(bf16)
            du_ref[rows, :] = dus_ref[rows, :] + _dot_nt(lrb, brb) + _dot_nt(lib, bib)
            ub = u_ref[rows, :].astype(bf16)
            dbr_ref[...] += _dot_tn(ub, lrb)
            dbi_ref[...] += _dot_tn(ub, lib)
            return c

        lax.fori_loop(0, nproj, back, 0)

    vm = pl.BlockSpec(memory_space=pltpu.VMEM)
    return pl.pallas_call(
        body, name="ssm_states_bwd", in_specs=[vm] * 10, out_specs=[vm] * 4,
        out_shape=[jax.ShapeDtypeStruct((S, D_SSM), f32), jax.ShapeDtypeStruct((D_SSM, D_STATE), f32),
                   jax.ShapeDtypeStruct((D_SSM, D_STATE), f32), jax.ShapeDtypeStruct((2, D_STATE), f32)],
        scratch_shapes=[pltpu.VMEM((S, D_STATE), f32), pltpu.VMEM((S, D_STATE), f32)],
        compiler_params=_cp(None, 56),
    )(dy, du_skip, u, sr, si, cre, cim, bre, bim, a2)


_POOL_TILE = 256


def _window_sums(xt, back):
    n = xt.shape[0]
    out = []
    ws = xt
    for k in (1, 2, 4, 8):
        ws = ws + pltpu.roll(ws, k if back else n - k, axis=0)
        out.append(ws)
    return out


def _pool_count(r0, w):
    t = r0 + lax.broadcasted_iota(jnp.int32, (_POOL_TILE, POOL_GROUP), 0)
    return jnp.minimum(t + 1, w).astype(f32)


def _pool_fwd(u_pad, pool_w, pool_scale):
    S = u_pad.shape[0] - POOL_HALO
    nt = S // _POOL_TILE

    def body(u_ref, w_ref, sc_ref, y_ref):
        def tile(t, c):
            r0 = pl.multiple_of(t * _POOL_TILE, _POOL_TILE)
            for g, w in enumerate(POOL_WINDOWS):
                cs = pl.ds(POOL_GROUP * g, POOL_GROUP)
                xt = u_ref[pl.ds(r0, _POOL_TILE + POOL_HALO), cs]
                ws = _window_sums(xt, True)[g][POOL_HALO:, :]
                pooled = ws / _pool_count(r0, w) - xt[POOL_HALO:, :]
                y_ref[pl.ds(r0, _POOL_TILE), cs] = _dot(pooled.astype(bf16), w_ref[g].astype(bf16)) * sc_ref[:, cs]
            return c
        lax.fori_loop(0, nt, tile, 0)

    vm = pl.BlockSpec(memory_space=pltpu.VMEM)
    return pl.pallas_call(
        body, name="pool_fwd", in_specs=[vm, vm, vm], out_specs=vm,
        out_shape=jax.ShapeDtypeStruct((S, D_POOL), f32),
    )(u_pad, pool_w, pool_scale)


def _pool_bwd(dy_pad, u_pad, pool_w, pool_scale):
    S = u_pad.shape[0] - POOL_HALO
    nt = S // _POOL_TILE
    n = _POOL_TILE + POOL_HALO

    def body(dy_ref, u_ref, w_ref, sc_ref, du_ref, dw_ref, dsc_ref):
        dw_ref[...] = jnp.zeros_like(dw_ref)
        dsc_ref[...] = jnp.zeros_like(dsc_ref)

        def tile(t, c):
            r0 = pl.multiple_of(t * _POOL_TILE, _POOL_TILE)
            for g, w in enumerate(POOL_WINDOWS):
                cs = pl.ds(POOL_GROUP * g, POOL_GROUP)
                wb = w_ref[g].astype(bf16)
                xt = u_ref[pl.ds(r0, n), cs]
                pooled = (_window_sums(xt, True)[g][POOL_HALO:, :] / _pool_count(r0, w) - xt[POOL_HALO:, :]).astype(bf16)
                dy = dy_ref[pl.ds(r0, _POOL_TILE), cs]
                dsc_ref[:, cs] += jnp.sum(dy * _dot(pooled, wb), 0, keepdims=True)
                dw_ref[g] += _dot_tn(pooled, (dy * sc_ref[:, cs]).astype(bf16))
                dyh = (dy_ref[pl.ds(r0, n), cs] * sc_ref[:, cs]).astype(bf16)
                dpl = _dot_nt(dyh, wb)
                cnt = jnp.minimum(r0 + lax.broadcasted_iota(jnp.int32, (n, POOL_GROUP), 0) + 1, w).astype(f32)
                lead = _window_sums(dpl / cnt, False)[g]
                du_ref[pl.ds(r0, _POOL_TILE), cs] = lead[:_POOL_TILE, :] - dpl[:_POOL_TILE, :]
            return c
        lax.fori_loop(0, nt, tile, 0)

    vm = pl.BlockSpec(memory_space=pltpu.VMEM)
    return pl.pallas_call(
        body, name="pool_bwd", in_specs=[vm, vm, vm, vm], out_specs=[vm, vm, vm],
        out_shape=[jax.ShapeDtypeStruct((S, D_POOL), f32), jax.ShapeDtypeStruct((4, POOL_GROUP, POOL_GROUP), f32),
                   jax.ShapeDtypeStruct((1, D_POOL), f32)],
    )(dy_pad, u_pad, pool_w, pool_scale)


def _loss_head(y, target):
    S, D = y.shape
    ts = TOK_TILE

    def body(y_ref, t_ref, loss_ref, dy_ref):
        @pl.when(pl.program_id(0) == 0)
        def _():
            loss_ref[...] = jnp.zeros_like(loss_ref)

        d = y_ref[...] - t_ref[...]
        dy_ref[...] = d * (1.0 / D)
        loss_ref[...] += 0.5 * jnp.sum(jnp.sum(d * d, -1, keepdims=True) * (1.0 / D), 0, keepdims=True)

    tok = pl.BlockSpec((ts, D), lambda i: (i, 0))
    return pl.pallas_call(
        body, name="loss_head", grid=(S // ts,),
        in_specs=[tok, tok], out_specs=[_full((1, 1)), tok],
        out_shape=[jax.ShapeDtypeStruct((1, 1), f32), jax.ShapeDtypeStruct((S, D), f32)],
        compiler_params=_cp(("arbitrary",)),
    )(y, target)


_ADA_COLS = 768


def _ada_fwd(c_all, ada_w, ada_b_cols):
    L, D, N = ada_w.shape
    B = c_all.shape[0]

    def body(c_ref, w_ref, b_ref, out_ref):
        cv = c_ref[...]
        cond = (cv * jax.nn.sigmoid(cv)).astype(bf16)
        out_ref[0] = _dot(cond, w_ref[0].astype(bf16)) + b_ref[0]

    return pl.pallas_call(
        body, name="ada_fwd", grid=(L, N // _ADA_COLS),
        in_specs=[_full((B, D)), pl.BlockSpec((1, D, _ADA_COLS), lambda l, j: (l, 0, j)),
                  pl.BlockSpec((1, 1, _ADA_COLS), lambda l, j: (l, 0, j))],
        out_specs=pl.BlockSpec((1, B, _ADA_COLS), lambda l, j: (l, 0, j)),
        out_shape=jax.ShapeDtypeStruct((L, B, N), f32),
        compiler_params=_cp(("parallel", "parallel")),
    )(c_all, ada_w, ada_b_cols)


def _ada_wgrad(c_all_t, dmod_cols):
    D, B = c_all_t.shape
    L, _, N = dmod_cols.shape

    def body(ct_ref, dm_ref, out_ref):
        cv = ct_ref[...]
        cond = cv * jax.nn.sigmoid(cv)
        acc = cond[:, 0:1] * dm_ref[0, 0:1, :]
        for b in range(1, B):
            acc = acc + cond[:, b:b + 1] * dm_ref[0, b:b + 1, :]
        out_ref[0] = acc

    return pl.pallas_call(
        body, name="ada_wgrad", grid=(L, N // _ADA_COLS),
        in_specs=[_full((D, B)), pl.BlockSpec((1, B, _ADA_COLS), lambda l, j: (l, 0, j))],
        out_specs=pl.BlockSpec((1, D, _ADA_COLS), lambda l, j: (l, 0, j)),
        out_shape=jax.ShapeDtypeStruct((L, D, N), f32),
        compiler_params=_cp(("parallel", "parallel")),
    )(c_all_t, dmod_cols)


def _adam_math(w, g, m, v):
    m = ADAM_B1 * m + (1.0 - ADAM_B1) * g
    v = ADAM_B2 * v + (1.0 - ADAM_B2) * (g * g)
    m_hat = m / (1.0 - ADAM_B1 ** ADAM_STEP)
    v_hat = v / (1.0 - ADAM_B2 ** ADAM_STEP)
    delta = -ADAM_LR * (m_hat / (jnp.sqrt(v_hat) + ADAM_EPS) + ADAM_WD * w)
    return delta, m, v


def _adamw(w, m, v, g, row_tile, row0=0, outs=None):
    R, C = w.shape
    b0 = row0 // row_tile

    def body(w_ref, m_ref, v_ref, g_ref, _0, _1, _2, _3, g_out, d_out, m_out, v_out):
        gv = g_ref[...]
        delta, mn, vn = _adam_math(w_ref[...], gv, m_ref[...], v_ref[...])
        g_out[...] = gv
        d_out[...] = delta
        m_out[...] = mn
        v_out[...] = vn

    pspec = pl.BlockSpec((row_tile, C), lambda i: (b0 + i, 0))
    gspec = pl.BlockSpec((row_tile, C), lambda i: (i, 0))
    anyspec = pl.BlockSpec(memory_space=pl.ANY)
    shp = jax.ShapeDtypeStruct((R, C), f32)
    if outs is None:
        outs = [lax.empty((R, C), f32) for _ in range(4)]
    return pl.pallas_call(
        body, name="adamw", grid=(g.shape[0] // row_tile,),
        in_specs=[pspec] * 3 + [gspec] + [anyspec] * 4, out_specs=[pspec] * 4, out_shape=[shp] * 4,
        input_output_aliases={4: 0, 5: 1, 6: 2, 7: 3},
        compiler_params=_cp(("parallel",), 40),
    )(w, m, v, g, *outs)


def _pair_sum(g5, got, pc):
    _, LS, _, R2, C = g5.shape

    def body(pc_ref, own_ref, got_ref, out_ref):
        out_ref[0, 0] = (own_ref[0, 0, 0].astype(f32) + got_ref[0, 0].astype(f32)).astype(bf16)

    gs = pltpu.PrefetchScalarGridSpec(
        num_scalar_prefetch=1, grid=(N_CHIPS, LS),
        in_specs=[pl.BlockSpec((1, 1, 1, R2, C), lambda p, s, pc: (p, s, pc[1], 0, 0)),
                  pl.BlockSpec((1, 1, R2, C), lambda p, s, pc: (p, s, 0, 0))],
        out_specs=pl.BlockSpec((1, 1, R2, C), lambda p, s, pc: (p, s, 0, 0)),
    )
    return pl.pallas_call(
        body, name="pair_sum", grid_spec=gs, out_shape=jax.ShapeDtypeStruct((N_CHIPS, LS, R2, C), bf16),
        compiler_params=_cp(("parallel", "parallel")),
    )(pc, g5, got)


def _sum_shards(hsum, recv, pc):
    _, LS, R2, C = hsum.shape

    def body(pc_ref, own_ref, r_ref, out_ref):
        acc = own_ref[0, 0].astype(f32)
        for j in range(3):
            acc = acc + r_ref[j, 0].astype(f32)
        out_ref[0, 0] = acc

    gs = pltpu.PrefetchScalarGridSpec(
        num_scalar_prefetch=1, grid=(LS,),
        in_specs=[pl.BlockSpec((1, 1, R2, C), lambda s, pc: (pc[0], s, 0, 0)),
                  pl.BlockSpec((3, 1, R2, C), lambda s, pc: (0, s, 0, 0))],
        out_specs=pl.BlockSpec((1, 1, R2, C), lambda s, pc: (s, pc[1], 0, 0)),
    )
    return pl.pallas_call(
        body, name="sum_shards", grid_spec=gs, out_shape=jax.ShapeDtypeStruct((LS, 2, R2, C), f32),
        compiler_params=_cp(("parallel",)),
    )(pc, hsum, recv)


def _sum8(packs):
    _, R, C = packs.shape
    tr = R // 8 if R % 64 == 0 else R

    def body(p_ref, out_ref):
        acc = p_ref[0]
        for d in range(1, 8):
            acc = acc + p_ref[d]
        out_ref[...] = acc

    return pl.pallas_call(
        body, name="sum8", grid=(R // tr,),
        in_specs=[pl.BlockSpec((8, tr, C), lambda i: (0, i, 0))],
        out_specs=pl.BlockSpec((tr, C), lambda i: (i, 0)),
        out_shape=jax.ShapeDtypeStruct((R, C), f32),
        compiler_params=_cp(("parallel",)),
    )(packs)


def _allgather8(x_shard):
    m_per, n = x_shard.shape

    def body(x_ref, out_ref, send_sems, recv_sems, local_sem):
        x, y, c = lax.axis_index("x"), lax.axis_index("y"), lax.axis_index("c")
        me, sibling = (x, y, c), (x, y, 1 - c)
        chips = [(1 - x, y), (x, 1 - y), (1 - x, 1 - y)]

        def rows(px, py, pc):
            return out_ref.at[pl.ds((4 * px + 2 * py + pc) * m_per, m_per), :]

        def copy(k, block, to, src=None):
            return pltpu.make_async_remote_copy(
                src_ref=rows(*block) if src is None else src, dst_ref=rows(*block),
                send_sem=send_sems.at[k], recv_sem=recv_sems.at[k], device_id=to, device_id_type=MESH)

        mine = pltpu.make_async_copy(x_ref, rows(*me), local_sem)
        mine.start()
        first = [copy(0, me, sibling, src=x_ref)]
        first += [copy(1 + j, me, (*chip, c), src=x_ref) for j, chip in enumerate(chips)]
        for cp in first:
            cp.start()
        passed = [copy(4 + j, (*chip, c), sibling) for j, chip in enumerate(chips)]
        for j, chip in enumerate(chips):
            copy(1 + j, (*chip, c), me).wait_recv()
            passed[j].start()
        copy(0, sibling, me).wait_recv()
        for j, chip in enumerate(chips):
            copy(4 + j, (*chip, 1 - c), me).wait_recv()
        for cp in first + passed:
            cp.wait_send()
        mine.wait()

    return pl.pallas_call(
        body, name="allgather8",
        out_shape=jax.ShapeDtypeStruct((8 * m_per, n), x_shard.dtype),
        in_specs=[pl.BlockSpec(memory_space=pltpu.VMEM)],
        out_specs=pl.BlockSpec(memory_space=pltpu.VMEM),
        scratch_shapes=[pltpu.SemaphoreType.DMA((7,)), pltpu.SemaphoreType.DMA((7,)), pltpu.SemaphoreType.DMA],
        compiler_params=_cp(None, 48),
    )(x_shard)


def _other_chips():
    x, y = lax.axis_index("x"), lax.axis_index("y")
    return [(1 - x, y), (x, 1 - y), (1 - x, 1 - y)]


_HBM = pl.BlockSpec(memory_space=pltpu.HBM)
_SEM = pl.BlockSpec(memory_space=pltpu.SEMAPHORE)
_EFFECT = pltpu.SideEffectType.DATAFLOW_SIDE_EFFECTING


def _gather_copies(srcs, lands, send_sems, recv_sems):
    x, y, c = lax.axis_index("x"), lax.axis_index("y"), lax.axis_index("c")
    return [pltpu.make_async_remote_copy(
        src_ref=srcs[a].at[:, c], dst_ref=lands[a].at[2 * x + y, :, c], send_sem=send_sems.at[3 * a + j],
        recv_sem=recv_sems.at[3 * a + j], device_id=(cx, cy, c), device_id_type=MESH)
        for a in range(len(srcs)) for j, (cx, cy) in enumerate(_other_chips())]


def _gather_start(chunks, after, name):
    sizes = [len(srcs) for srcs, _ in chunks]
    flat = [t for srcs, lands in chunks for t in list(srcs) + list(lands)]
    nflat = len(flat)
    nsem = 2 * len(chunks)

    def body(*refs):
        ins, sems, token = refs[:nflat], refs[nflat + 1:nflat + 1 + nsem], refs[-1]
        off = 0
        for k, n in enumerate(sizes):
            for cp in _gather_copies(ins[off:off + n], ins[off + n:off + 2 * n], sems[2 * k], sems[2 * k + 1]):
                cp.start()
            off += 2 * n
        token[...] = jnp.zeros_like(token)

    res = pl.pallas_call(
        body, name=name,
        out_shape=[pltpu.SemaphoreType.DMA((3 * n,)) for n in sizes for _ in range(2)]
        + [pltpu.HBM(t.shape, t.dtype) for t in flat] + [jax.ShapeDtypeStruct((8, 128), f32)],
        in_specs=[_HBM] * nflat + [pl.BlockSpec(memory_space=pl.ANY)],
        out_specs=[_SEM] * nsem + [_HBM] * nflat + [pl.BlockSpec(memory_space=pltpu.VMEM)],
        input_output_aliases={i: nsem + i for i in range(nflat)},
        compiler_params=pltpu.CompilerParams(has_side_effects=_EFFECT),
    )(*[pltpu.with_memory_space_constraint(t, pltpu.HBM) for t in flat], after)
    out, off = [], nsem
    for k, n in enumerate(sizes):
        out.append((res[2 * k], res[2 * k + 1], res[off:off + n], res[off + n:off + 2 * n]))
        off += 2 * n
    return out, res[-1]


def _gather_wait(send_sems, recv_sems, srcs, lands, after, name):
    n = len(srcs)

    def body(*refs):
        for cp in _gather_copies(refs[:n], refs[n:2 * n], refs[2 * n], refs[2 * n + 1]):
            cp.wait_send()
            cp.wait_recv()

    res = pl.pallas_call(
        body, name=name,
        out_shape=[pltpu.HBM(t.shape, t.dtype) for t in list(srcs) + list(lands)],
        in_specs=[_HBM] * (2 * n) + [_SEM, _SEM] + [pl.BlockSpec(memory_space=pl.ANY)] * len(after),
        out_specs=[_HBM] * (2 * n),
        input_output_aliases={i: i for i in range(2 * n)},
        compiler_params=pltpu.CompilerParams(has_side_effects=_EFFECT),
    )(*srcs, *lands, send_sems, recv_sems, *after)
    return res[n:]


def _gather_forward(lands):
    n = len(lands)

    def body(*refs):
        outs = refs[n:2 * n]
        send_sems, recv_sems = refs[2 * n:]
        x, y, c = lax.axis_index("x"), lax.axis_index("y"), lax.axis_index("c")
        sibling = (x, y, 1 - c)
        copies = []
        for a in range(n):
            for j, (cx, cy) in enumerate(_other_chips()):
                mine = outs[a].at[2 * cx + cy, :, c]
                cp = pltpu.make_async_remote_copy(src_ref=mine, dst_ref=mine, send_sem=send_sems.at[3 * a + j],
                                                  recv_sem=recv_sems.at[3 * a + j], device_id=sibling, device_id_type=MESH)
                cp.start()
                copies.append((cp, a, j, cx, cy))
        for cp, a, j, cx, cy in copies:
            cp.wait_send()
            theirs = outs[a].at[2 * cx + cy, :, 1 - c]
            pltpu.make_async_remote_copy(src_ref=theirs, dst_ref=theirs, send_sem=send_sems.at[3 * a + j],
                                         recv_sem=recv_sems.at[3 * a + j], device_id=sibling, device_id_type=MESH).wait_recv()

    hbm = pl.BlockSpec(memory_space=pl.ANY)
    return pl.pallas_call(
        body, name="gather_forward",
        out_shape=[jax.ShapeDtypeStruct(t.shape, t.dtype) for t in lands],
        in_specs=[hbm] * n, out_specs=[hbm] * n,
        input_output_aliases={a: a for a in range(n)},
        scratch_shapes=[pltpu.SemaphoreType.DMA((3 * n,)), pltpu.SemaphoreType.DMA((3 * n,))],
    )(*lands)


def _pair_exchange(g5s):
    n = len(g5s)

    def body(*refs):
        ins, outs = refs[:n], refs[n:2 * n]
        send_sems, recv_sems = refs[2 * n:]
        c = lax.axis_index("c")
        sibling = (lax.axis_index("x"), lax.axis_index("y"), 1 - c)
        copies = []
        for a in range(n):
            cp = pltpu.make_async_remote_copy(src_ref=ins[a].at[:, :, 1 - c], dst_ref=outs[a], send_sem=send_sems.at[a],
                                              recv_sem=recv_sems.at[a], device_id=sibling, device_id_type=MESH)
            cp.start()
            copies.append(cp)
        for cp in copies:
            cp.wait()

    hbm = pl.BlockSpec(memory_space=pl.ANY)
    return pl.pallas_call(
        body, name="pair_exchange",
        out_shape=[jax.ShapeDtypeStruct(g.shape[:2] + g.shape[3:], g.dtype) for g in g5s],
        in_specs=[hbm] * n, out_specs=[hbm] * n,
        scratch_shapes=[pltpu.SemaphoreType.DMA((n,)), pltpu.SemaphoreType.DMA((n,))],
    )(*g5s)


def _scatter_copies(srcs, lands, send_sems, recv_sems):
    c = lax.axis_index("c")
    return [pltpu.make_async_remote_copy(
        src_ref=srcs[a].at[2 * cx + cy], dst_ref=lands[a].at[j], send_sem=send_sems.at[3 * a + j],
        recv_sem=recv_sems.at[3 * a + j], device_id=(cx, cy, c), device_id_type=MESH)
        for a in range(len(srcs)) for j, (cx, cy) in enumerate(_other_chips())]


def _scatter_start(hsums, name, after=()):
    n = len(hsums)
    na = len(after)

    def body(*refs):
        srcs, lands = refs[:n], refs[n:2 * n]
        send_sems, recv_sems = refs[2 * n + na], refs[2 * n + na + 1]
        for cp in _scatter_copies(srcs, lands, send_sems, recv_sems):
            cp.start()
        refs[-1][...] = jnp.zeros_like(refs[-1])

    lands = [lax.empty((3,) + g.shape[1:], g.dtype) for g in hsums]
    res = pl.pallas_call(
        body, name=name,
        out_shape=[pltpu.SemaphoreType.DMA((3 * n,)), pltpu.SemaphoreType.DMA((3 * n,))]
        + [pltpu.HBM(g.shape, g.dtype) for g in hsums] + [pltpu.HBM(g.shape, g.dtype) for g in lands]
        + [jax.ShapeDtypeStruct((8, 128), f32)],
        in_specs=[_HBM] * (2 * n) + [pl.BlockSpec(memory_space=pl.ANY)] * na,
        out_specs=[_SEM, _SEM] + [_HBM] * (2 * n) + [pl.BlockSpec(memory_space=pltpu.VMEM)],
        input_output_aliases={i: i + 2 for i in range(2 * n)},
        compiler_params=pltpu.CompilerParams(has_side_effects=_EFFECT),
    )(*[pltpu.with_memory_space_constraint(t, pltpu.HBM) for t in list(hsums) + lands], *after)
    return (res[0], res[1], res[2:2 + n], res[2 + n:2 + 2 * n]), res[-1]


def _scatter_wait(send_sems, recv_sems, srcs, lands, after, name):
    n = len(srcs)
    extra = list(after)

    def body(*refs):
        s_refs, l_refs = refs[:n], refs[n:2 * n]
        ss, rs = refs[2 * n], refs[2 * n + 1]
        for cp in _scatter_copies(s_refs, l_refs, ss, rs):
            cp.wait_send()
            cp.wait_recv()

    res = pl.pallas_call(
        body, name=name,
        out_shape=[pltpu.HBM(g.shape, g.dtype) for g in srcs] + [pltpu.HBM(g.shape, g.dtype) for g in lands],
        in_specs=[_HBM] * (2 * n) + [_SEM, _SEM] + [pl.BlockSpec(memory_space=pl.ANY)] * len(extra),
        out_specs=[_HBM] * (2 * n),
        input_output_aliases={i: i for i in range(2 * n)},
        compiler_params=pltpu.CompilerParams(has_side_effects=_EFFECT),
    )(*srcs, *lands, send_sems, recv_sems, *extra)
    return res[:n], res[n:]


def _swap_halves(fulls):
    n = len(fulls)

    def body(*refs):
        ins, outs = refs[:n], refs[n:2 * n]
        send_sems, recv_sems = refs[2 * n:]
        c = lax.axis_index("c")
        sibling = (lax.axis_index("x"), lax.axis_index("y"), 1 - c)
        copies = []
        for a in range(n):
            cp = pltpu.make_async_remote_copy(src_ref=outs[a].at[:, c], dst_ref=outs[a].at[:, c], send_sem=send_sems.at[a],
                                              recv_sem=recv_sems.at[a], device_id=sibling, device_id_type=MESH)
            cp.start()
            copies.append(cp)
        for a, cp in enumerate(copies):
            cp.wait_send()
            theirs = outs[a].at[:, 1 - c]
            pltpu.make_async_remote_copy(src_ref=theirs, dst_ref=theirs, send_sem=send_sems.at[a], recv_sem=recv_sems.at[a],
                                         device_id=sibling, device_id_type=MESH).wait_recv()

    hbm = pl.BlockSpec(memory_space=pl.ANY)
    return pl.pallas_call(
        body, name="swap_halves",
        out_shape=[jax.ShapeDtypeStruct(p.shape, p.dtype) for p in fulls],
        in_specs=[hbm] * n, out_specs=[hbm] * n,
        input_output_aliases={a: a for a in range(n)},
        scratch_shapes=[pltpu.SemaphoreType.DMA((n,)), pltpu.SemaphoreType.DMA((n,))],
    )(*fulls)


def _to_segments(t):
    s, c = t.shape
    return t.reshape(SCAN_SEG, s // SCAN_SEG, c).transpose(1, 0, 2).reshape(s, c)


def _from_segments(t):
    s, c = t.shape
    return t.reshape(s // SCAN_SEG, SCAN_SEG, c).transpose(1, 0, 2).reshape(s, c)


def _ssm_operators(a_re, a_im, log_dt, b_re, b_im, c_re, c_im):
    lam = lax.complex(a_re, a_im)
    dt = jnp.exp(log_dt)[:, None]
    a_bar = jnp.exp(lam * dt)
    b_bar = ((a_bar - 1.0) / lam)[:, :, None] * lax.complex(b_re, b_im)
    eye = jnp.eye(N_GROUPS, dtype=f32)

    def embed_b(t):
        return (jnp.transpose(t, (0, 2, 1))[:, :, None, :] * eye[:, None, :, None]).reshape(D_SSM, D_STATE)

    def embed_c(t):
        return (jnp.transpose(t, (0, 2, 1))[:, :, None, :] * eye[:, None, :, None]).reshape(D_STATE, D_SSM)

    a2 = jnp.stack([a_bar.real.reshape(D_STATE), a_bar.imag.reshape(D_STATE)])
    return a2, embed_b(b_bar.real), embed_b(b_bar.imag), embed_c(c_re), embed_c(c_im)


def _local_step(x, target, mod, small, ffn_weights, mix_weights, grads_done):
    table = jnp.asarray(_bucket_table())
    bias = _bias_fwd(small["rel_bias"], table)
    L = DEPTH
    saved = []
    ssm_ops = []
    for l in range(L):
        sv = {}
        m9 = mod[l]
        sv["x0"] = x
        sv["w0"] = ffn_weights(l, 0, x)
        x, sv["f0"], sv["g0"], sv["u0"], sv["h0"] = _ffn_fwd(x, m9[0:3], *sv["w0"], 0, small["ln_g"][l, 0:1], small["ln_b"][l, 0:1])
        sv["x1"] = x
        sv["w1"] = mix_weights(l, x)
        *qkv, z_rest, sv["h1"] = _mix_in_fwd(x, m9[3:6], sv["w1"][0], 0)
        S = x.shape[0]
        qkv = [t.reshape(3, S, D_ATT) for t in qkv]
        att = [_att_fwd(qkv[b], bias, b) for b in range(3)]
        y_att, lse3 = _att_merge([att[b][0].reshape(d, S // d, D_ATT) for b, d in enumerate(DILATIONS)],
                                 [att[b][1].reshape(d, S // d, _LANES) for b, d in enumerate(DILATIONS)])
        sv.update(qkv=qkv, lse=[a[1] for a in att], lse3=lse3, y_att=y_att)

        prm = tuple(small[k][l] for k in ("ssm_a_re", "ssm_a_im", "ssm_log_dt", "ssm_b_re", "ssm_b_im", "ssm_c_re", "ssm_c_im"))
        (a2, bre, bim, cre, cim), ops_vjp = jax.vjp(_ssm_operators, *prm)
        ssm_ops.append(ops_vjp)
        u_ssm = _to_segments(z_rest[:, :D_SSM])
        sr, si = _ssm_states(u_ssm, bre, bim, a2)
        dskip = small["ssm_d"][l][None, :]
        glu_b = small["glu_b"][l][None, :]
        out_seg, y_seg = _ssm_out(sr, si, u_ssm, cre, cim, dskip, small["glu_w"][l], glu_b)
        y_ssm = _from_segments(out_seg)
        sv.update(a2=a2, bre=bre, bim=bim, cre=cre, cim=cim, u_ssm=u_ssm, sr=sr, si=si, y_seg=y_seg, y_ssm=y_ssm)

        u_pool = jnp.concatenate([jnp.zeros((POOL_HALO, D_POOL), f32), z_rest[:, D_SSM:]])
        y_pool = _pool_fwd(u_pool, small["pool_w"][l], small["pool_scale"][l][None, :])
        sv.update(u_pool=u_pool, y_pool=y_pool)

        x, sv["ymix"] = _mix_out_fwd(x, y_att, y_ssm, y_pool, m9[3:6], sv["w1"][1], 0, small["ln_g"][l, 1:2], small["ln_b"][l, 1:2])
        sv["x2"] = x
        sv["w2"] = ffn_weights(l, 1, x)
        x, sv["f2"], sv["g2"], sv["u2"], sv["h2"] = _ffn_fwd(x, m9[6:9], *sv["w2"], 0, small["ln_g"][l, 2:3], small["ln_b"][l, 2:3])
        saved.append(sv)

    loss, dx = _loss_head(x, target)

    dmod = [None] * L
    dln_g = [None] * L
    dln_b = [None] * L
    sg = {k: [None] * L for k in ("ssm_a_re", "ssm_a_im", "ssm_log_dt", "ssm_b_re", "ssm_b_im", "ssm_c_re", "ssm_c_im",
                                  "ssm_d", "glu_w", "glu_b", "pool_w", "pool_scale")}
    dbias_tot = None
    order_after = jnp.zeros((), f32)
    for l in reversed(range(L)):
        sv = saved[l]
        m9 = mod[l] + order_after

        def fresh(like):
            return [lax.empty(t.shape, bf16) for t in like]

        dx, dg, du, a, df, dm2, dlg2, dlb2 = _ffn_bwd(dx, sv["x2"], sv["f2"], sv["g2"], sv["u2"], m9[6:9], *sv["w2"], 0,
                                                     small["ln_g"][l, 2:3])
        g_ffn1 = _ffn_wgrad(sv["h2"], dg, du, a, df, *fresh(sv["w2"]), 0)
        dxr, d_att, d_ssm, d_pool, dgate1, dlg1, dlb1, g_w_out = _mix_out_bwd(
            dx, sv["x1"], sv["ymix"], sv["y_att"], sv["y_ssm"], sv["y_pool"], m9[3:6], sv["w1"][1], 0, small["ln_g"][l, 1:2],
            fresh(sv["w1"])[1])
        S = d_att.shape[0]
        merged = _att_merge_bwd(d_att, sv["y_att"], sv["lse3"])
        dqkv, dbias = [], []
        for b, d in enumerate(DILATIONS):
            dq_b, db_b = _att_bwd(sv["qkv"][b], merged[b].reshape(S, D_ATT), sv["lse"][b], merged[3 + b].reshape(S, _LANES), bias, b)
            dqkv.append(dq_b.reshape(3, d, S // d, D_ATT))
            dbias.append(db_b)
        dbias = jnp.stack(dbias)
        dbias_tot = dbias if dbias_tot is None else dbias_tot + dbias
        d_seg = _to_segments(d_ssm)
        dskip = small["ssm_d"][l][None, :]
        glu_b = small["glu_b"][l][None, :]
        dy_seg, du_skip, dcre, dcim, dd, dglu_b, dglu_w = _ssm_out_bwd(
            d_seg, sv["y_seg"], sv["u_ssm"], sv["sr"], sv["si"], dskip, small["glu_w"][l], glu_b)
        du_seg, dbre, dbim, da2 = _ssm_states_bwd(dy_seg, du_skip, sv["u_ssm"], sv["sr"], sv["si"], sv["cre"], sv["cim"],
                                                  sv["bre"], sv["bim"], sv["a2"])
        d_prm = ssm_ops[l]((da2, dbre, dbim, dcre, dcim))
        for k, v in zip(("ssm_a_re", "ssm_a_im", "ssm_log_dt", "ssm_b_re", "ssm_b_im", "ssm_c_re", "ssm_c_im"), d_prm):
            sg[k][l] = v
        sg["ssm_d"][l] = dd[0]
        sg["glu_b"][l] = dglu_b[0]
        sg["glu_w"][l] = dglu_w
        du_ssm = _from_segments(du_seg)
        dyp = jnp.concatenate([d_pool, jnp.zeros((POOL_HALO, D_POOL), f32)])
        du_pool, dpw, dps = _pool_bwd(dyp, sv["u_pool"], small["pool_w"][l], small["pool_scale"][l][None, :])
        sg["pool_w"][l] = dpw
        sg["pool_scale"][l] = dps[0]
        d_rest = jnp.concatenate([du_ssm, du_pool], axis=1).astype(bf16)
        dx, dm1, dz = _mix_in_bwd(dqkv, d_rest, dxr, sv["x1"], m9[3:6], sv["w1"][0], 0)
        g_w_in = _mix_in_wgrad(sv["h1"], dz, fresh(sv["w1"])[0], 0)
        ffn_names = ("ffn_w_gate", "ffn_w_up", "ffn_w_down")
        m9 = m9 + grads_done(l, 1, list(zip(ffn_names, [(2 * l + 1) * FF_SHARD] * 3, g_ffn1))
                             + [("w_in", l * D_MODEL, g_w_in), ("w_out", l * 256, g_w_out)])
        dm1 = jnp.concatenate([dm1[0:2], dgate1])
        dx, dg, du, a, df, dm0, dlg0, dlb0 = _ffn_bwd(dx, sv["x0"], sv["f0"], sv["g0"], sv["u0"], m9[0:3], *sv["w0"], 0,
                                                     small["ln_g"][l, 0:1])
        g_ffn0 = _ffn_wgrad(sv["h0"], dg, du, a, df, *fresh(sv["w0"]), 0)
        order_after = grads_done(l, 0, list(zip(ffn_names, [2 * l * FF_SHARD] * 3, g_ffn0)))
        dmod[l] = jnp.concatenate([dm0, dm1, dm2])
        dln_g[l] = jnp.concatenate([dlg0, dlg1, dlg2])
        dln_b[l] = jnp.concatenate([dlb0, dlb1, dlb2])

    small_grads = {k: jnp.stack(v) for k, v in sg.items()}
    small_grads["rel_bias"] = _bias_bwd(dbias_tot, table)
    small_grads["ln_g"] = jnp.stack(dln_g)
    small_grads["ln_b"] = jnp.stack(dln_b)
    return loss, dx, jnp.stack(dmod), small_grads


_TILE_ELEMS = 8 * 128


def _pack_rows(shapes):
    out, row = [], 0
    for s in shapes:
        nr = -(-int(np.prod(s)) // _TILE_ELEMS) * 8
        out.append((row, nr))
        row += nr
    return out


def _pack(arrs):
    parts = []
    for a in arrs:
        flat = a.reshape(-1).astype(f32)
        npad = -(-flat.shape[0] // _TILE_ELEMS) * _TILE_ELEMS
        parts.append(jnp.pad(flat, (0, npad - flat.shape[0])).reshape(npad // 128, 128))
    return jnp.concatenate(parts, axis=0)


def _unpack(buf, shapes):
    return [buf[row:row + nr].reshape(-1)[:int(np.prod(s))].reshape(s) for s, (row, nr) in zip(shapes, _pack_rows(shapes))]


_REPL = ("rel_bias", "ada_b", "ssm_a_re", "ssm_a_im", "ssm_log_dt", "ssm_b_re", "ssm_b_im", "ssm_c_re", "ssm_c_im",
         "ssm_d", "glu_b", "pool_w", "pool_scale")
_SMALL_SHARDED = ("ln_g", "ln_b", "glu_w")
_BIG = ("ffn_w_gate", "ffn_w_up", "ffn_w_down", "w_in", "w_out")
_ORDER = ("rel_bias", "ada_w", "ada_b", "ln_g", "ln_b", "ffn_w_gate", "ffn_w_up", "ffn_w_down", "w_in", "w_out",
          "ssm_a_re", "ssm_a_im", "ssm_log_dt", "ssm_b_re", "ssm_b_im", "ssm_c_re", "ssm_c_im", "ssm_d", "glu_w",
          "glu_b", "pool_w", "pool_scale")


def kernel(x, c, rel_bias, ada_w, ada_b, ln_g, ln_b, ffn_w_gate, ffn_w_up, ffn_w_down, w_in, w_out, ssm_a_re, ssm_a_im, ssm_log_dt, ssm_b_re, ssm_b_im, ssm_c_re, ssm_c_im, ssm_d, glu_w, glu_b, pool_w, pool_scale, loss_target, m_rel_bias, m_ada_w, m_ada_b, m_ln_g, m_ln_b, m_ffn_w_gate, m_ffn_w_up, m_ffn_w_down, m_w_in, m_w_out, m_ssm_a_re, m_ssm_a_im, m_ssm_log_dt, m_ssm_b_re, m_ssm_b_im, m_ssm_c_re, m_ssm_c_im, m_ssm_d, m_glu_w, m_glu_b, m_pool_w, m_pool_scale, v_rel_bias, v_ada_w, v_ada_b, v_ln_g, v_ln_b, v_ffn_w_gate, v_ffn_w_up, v_ffn_w_down, v_w_in, v_w_out, v_ssm_a_re, v_ssm_a_im, v_ssm_log_dt, v_ssm_b_re, v_ssm_b_im, v_ssm_c_re, v_ssm_c_im, v_ssm_d, v_glu_w, v_glu_b, v_pool_w, v_pool_scale):
    args = dict(locals())
    w = {k: args[k] for k in _ORDER}
    m = {k: args["m_" + k] for k in _ORDER}
    v = {k: args["v_" + k] for k in _ORDER}
    L, D = DEPTH, D_MODEL
    ax, ay, ac = lax.axis_index("x"), lax.axis_index("y"), lax.axis_index("c")
    p_me = 2 * ax + ay
    dev = 4 * ax + 2 * ay + ac

    transposed = ("ffn_w_gate", "ffn_w_up")
    for d in (w, m, v):
        for name in transposed:
            d[name] = jnp.swapaxes(d[name], 2, 3)

    def halves(t):
        return t.astype(bf16).reshape(1, 2, t.shape[0] // 2, t.shape[1])

    def landing(src):
        return lax.dynamic_update_slice(lax.empty((N_CHIPS,) + src.shape, bf16), src[None], (p_me, 0, 0, 0, 0))

    chunk_keys = [("ffn", 0, 0), ("mix", 0), ("ffn", 0, 1), ("ffn", 1, 0), ("mix", 1), ("ffn", 1, 1)]
    chunk_srcs = []
    for key in chunk_keys:
        if key[0] == "ffn":
            chunk_srcs.append([halves(w[name][key[1], key[2]]) for name in ("ffn_w_gate", "ffn_w_up", "ffn_w_down")])
        else:
            chunk_srcs.append([halves(w_in[key[1]]), halves(w_out[key[1]])])

    pack = _pack([c, ln_g, ln_b, glu_w])
    rows = pack.shape[0]
    allp = _allgather8(pack).reshape(8, rows, 128)
    chunks = [(srcs, [landing(t) for t in srcs]) for srcs in chunk_srcs]
    first_in_flight, first_begun = _gather_start(chunks[:1], allp, "gather_start_first")
    c_all = allp[:, :8].reshape(8, D) + first_begun[0, 0]
    by_chip = allp[0::2]

    fwd_rows = _pack_rows([c.shape, ln_g.shape, ln_b.shape, glu_w.shape])

    def sharded(part, shape, axis):
        row0, nrows = fwd_rows[part]
        t = by_chip[:, row0:row0 + nrows].reshape(N_CHIPS, -1)[:, :int(np.prod(shape))].reshape((N_CHIPS,) + shape)
        return jnp.concatenate([t[p] for p in range(N_CHIPS)], axis=axis)

    ln_g_full = sharded(1, ln_g.shape, 2)
    ln_b_full = sharded(2, ln_b.shape, 2)
    glu_w_full = sharded(3, glu_w.shape, 1)

    ncol = ada_w.shape[-1]
    ada_b_cols = lax.dynamic_slice_in_dim(ada_b, p_me * ncol, ncol, axis=1)[:, None, :]
    mod_part = _ada_fwd(c_all, ada_w, ada_b_cols)
    mrows = L * 8 * ncol // 128
    mod_all = _allgather8(mod_part.reshape(mrows, 128)).reshape(8, L, 8, ncol)
    mod_mine = lax.dynamic_index_in_dim(mod_all, dev, axis=2, keepdims=False)
    mod = jnp.concatenate([mod_mine[2 * p] for p in range(N_CHIPS)], axis=-1).reshape(L, 9, D)

    rest_in_flight, rest_begun = _gather_start(chunks[1:], mod, "gather_start_rest")
    in_flight = first_in_flight + rest_in_flight

    def gathered(key, after):
        k = chunk_keys.index(key)
        lands = _gather_forward(_gather_wait(*in_flight[k], [after, rest_begun], "gather_wait_%d" % k))
        return [t.reshape(N_CHIPS, 1, 2 * t.shape[3], t.shape[4]) for t in lands]

    pc = jnp.stack([p_me, ac]).astype(jnp.int32)
    groups = {}
    scattering = {}

    def start_group(tag, after=()):
        g5 = [g.reshape(g.shape[:2] + (2, g.shape[2] // 2, g.shape[3])) for _, _, g in groups[tag]]
        hsum = [_pair_sum(g, r, pc) for g, r in zip(g5, _pair_exchange(g5))]
        scattering[tag], begun = _scatter_start(hsum, "scatter_start_%s" % tag, after)
        return begun

    def grads_done(l, s, grads):
        if l == 1:
            groups.setdefault("l1", []).extend(grads)
            return start_group("l1")[0, 0] if s == 0 else jnp.zeros((), f32)
        groups["l0a" if s == 1 else "l0b"] = grads
        return start_group("l0a")[0, 0] if s == 1 else jnp.zeros((), f32)

    small = {k: w[k] for k in _REPL if k != "ada_b"}
    small.update(ln_g=ln_g_full, ln_b=ln_b_full, glu_w=glu_w_full)
    loss_dev, grad_x, dmod, sgrads = _local_step(
        x[0], loss_target[0], mod, small, lambda l, s, after: gathered(("ffn", l, s), after),
        lambda l, after: gathered(("mix", l), after), grads_done)
    loss = lax.psum(loss_dev[0, 0], ("x", "y", "c"))

    names = ("rel_bias", "ln_g", "ln_b", "ssm_a_re", "ssm_a_im", "ssm_log_dt", "ssm_b_re", "ssm_b_im", "ssm_c_re",
             "ssm_c_im", "ssm_d", "glu_w", "glu_b", "pool_w", "pool_scale")
    gpack = _pack([dmod] + [sgrads[k] for k in names])
    grows = gpack.shape[0]
    gall = _allgather8(gpack).reshape(8, grows, 128)
    l0b_begun = start_group("l0b", (gall,))
    gsum = _unpack(_sum8(gall), [(L, 9 * D)] + [sgrads[k].shape for k in names])
    red = dict(zip(("ada_b",) + names, gsum))
    red["ln_g"] = lax.dynamic_slice_in_dim(red["ln_g"], p_me * 256, 256, axis=2)
    red["ln_b"] = lax.dynamic_slice_in_dim(red["ln_b"], p_me * 256, 256, axis=2)
    red["glu_w"] = lax.dynamic_slice_in_dim(red["glu_w"], p_me * 64, 64, axis=1)

    dmod_all = gall[:, :L * 9 * D // 128].reshape(8, L, 9 * D)
    dmod_cols = jnp.transpose(lax.dynamic_slice_in_dim(dmod_all, p_me * ncol, ncol, axis=2), (1, 0, 2))
    g_ada_w = _ada_wgrad(jnp.transpose(c_all), dmod_cols)

    out_g, out_d, out_m, out_v = {}, {}, {}, {}
    r2 = (L * D, ncol)
    res = _adamw(ada_w.reshape(r2), m["ada_w"].reshape(r2), v["ada_w"].reshape(r2), g_ada_w.reshape(r2), 128)
    out_g["ada_w"], out_d["ada_w"], out_m["ada_w"], out_v["ada_w"] = [t.reshape(ada_w.shape) for t in res]

    small_names = _REPL + _SMALL_SHARDED
    wp = _pack([w[k] for k in small_names])
    res_small = _adamw(wp, _pack([m[k] for k in small_names]), _pack([v[k] for k in small_names]),
                       _pack([red[k] for k in small_names]), wp.shape[0])
    for t, dst in zip(res_small, (out_g, out_d, out_m, out_v)):
        for k, a in zip(small_names, _unpack(t, [w[k].shape for k in small_names])):
            dst[k] = a

    row_tile = dict(zip(_BIG, (352, 352, 352, 256, 256)))
    big = {name: None for name in _BIG}
    after = [grad_x, res_small[1], res[1], l0b_begun]
    for tag in ("l1", "l0a", "l0b"):
        hsum, recv = _scatter_wait(*scattering[tag], after, "scatter_wait_%s" % tag)
        full = _swap_halves([_sum_shards(h, r, pc) for h, r in zip(hsum, recv)])
        for (name, row0, _), g in zip(groups[tag], full):
            shp = w[name].shape
            r2 = (int(np.prod(shp[:-1])), shp[-1])
            big[name] = _adamw(w[name].reshape(r2), m[name].reshape(r2), v[name].reshape(r2), g.reshape(-1, shp[-1]),
                               row_tile[name], row0, big[name])
        after = [big[name][1] for name, _, _ in groups[tag]]
    for name in _BIG:
        res = [t.reshape(w[name].shape) for t in big[name]]
        out_g[name], out_d[name], out_m[name], out_v[name] = [jnp.swapaxes(t, 2, 3) for t in res] if name in transposed else res

    return (loss, grad_x[None], *[out_g[k] for k in _ORDER], *[out_d[k] for k in _ORDER],
            *[out_m[k] for k in _ORDER], *[out_v[k] for k in _ORDER])
```

```python
import functools
import math

import numpy as np
import jax
import jax.numpy as jnp
from jax import lax
from jax.experimental import pallas as pl
from jax.experimental.pallas import tpu as pltpu

f32 = jnp.float32
bf16 = jnp.bfloat16
MESH = pl.DeviceIdType.MESH

D_MODEL = 1024
SEQ = 2048
DEPTH = 2
HEAD_DIM = 64
N_HEADS = 8
D_ATT = 512
DILATIONS = (1, 4, 16)
BLOCKS_PER_RESIDUE = (16, 4, 1)
ATT_BLOCK = 128
N_UNITS = SEQ // ATT_BLOCK
N_GROUPS = 16
SSM_GROUP = 16
SSM_STATE = 64
D_SSM = 256
D_STATE = N_GROUPS * SSM_STATE
POOL_WINDOWS = (2, 4, 8, 16)
POOL_GROUP = 64
D_POOL = 256
POOL_HALO = 16
D_FF = 2816
N_BUCKETS = 32
MAX_DISTANCE = 2048
ALPHA = (2 * DEPTH) ** 0.25
FFN_RES = 0.5
LN_EPS = 1e-5
NEG = -1e30
N_CHIPS = 4
FF_SHARD = D_FF // N_CHIPS
SCAN_SEG = 8
SCAN_STEPS = SEQ // SCAN_SEG

ADAM_LR, ADAM_B1, ADAM_B2, ADAM_EPS, ADAM_WD, ADAM_STEP = 0.001, 0.9, 0.999, 1e-08, 0.01, 10

TOK_TILE = 512


def _cp(dims=None, vmem_mb=None):
    kw = {}
    if dims is not None:
        kw["dimension_semantics"] = dims
    if vmem_mb is not None:
        kw["vmem_limit_bytes"] = vmem_mb << 20
    return pltpu.CompilerParams(**kw)


def _dot(a, b):
    return jnp.dot(a, b, preferred_element_type=f32)


def _dot_nt(a, b):
    return lax.dot_general(a, b, (((1,), (1,)), ((), ())), preferred_element_type=f32)


def _dot_tn(a, b):
    return lax.dot_general(a, b, (((0,), (0,)), ((), ())), preferred_element_type=f32)


def _ln_stats(v):
    mu = jnp.mean(v, -1, keepdims=True)
    d = v - mu
    var = jnp.mean(d * d, -1, keepdims=True)
    rstd = lax.rsqrt(var + LN_EPS)
    return d * rstd, rstd


def _ln_bwd(dxh, xh, rstd):
    return rstd * (dxh - jnp.mean(dxh, -1, keepdims=True) - xh * jnp.mean(dxh * xh, -1, keepdims=True))


_GELU_C = math.sqrt(2.0 / math.pi)


def _gelu(y):
    return 0.5 * y * (1.0 + jnp.tanh(_GELU_C * (y + 0.044715 * y * y * y)))


def _gelu_grad(y):
    t = jnp.tanh(_GELU_C * (y + 0.044715 * y * y * y))
    return 0.5 * (1.0 + t) + 0.5 * y * (1.0 - t * t) * (_GELU_C * (1.0 + 3 * 0.044715 * y * y))


def _full(shape):
    return pl.BlockSpec(shape, lambda *_: (0,) * len(shape))


def _ffn_fwd(x, mod3, wg, wu, wd, ls, lng, lnb):
    S, D = x.shape
    Fs = wg.shape[-2]
    ts = TOK_TILE

    def body(x_ref, mod_ref, wg_ref, wu_ref, wd_ref, lng_ref, lnb_ref, xo_ref, f_ref, g_ref, u_ref, h_ref, acc_sc):
        j = pl.program_id(1)

        @pl.when(j == 0)
        def _():
            xh, _ = _ln_stats(x_ref[...])
            h_ref[...] = (xh * (1.0 + mod_ref[1:2, :]) + mod_ref[0:1, :]).astype(bf16)
            acc_sc[...] = jnp.zeros_like(acc_sc)

        h = h_ref[...]
        g = _dot_nt(h, wg_ref[0, 0])
        u = _dot_nt(h, wu_ref[0, 0])
        g_ref[0] = g.astype(bf16)
        u_ref[0] = u.astype(bf16)
        a = (g * jax.nn.sigmoid(g) * u).astype(bf16)
        acc_sc[...] += _dot(a, wd_ref[0, 0])

        @pl.when(j == N_CHIPS - 1)
        def _():
            f = acc_sc[...]
            f_ref[...] = f
            r = ALPHA * x_ref[...] + (FFN_RES * mod_ref[2:3, :]) * f
            rh, _ = _ln_stats(r)
            xo_ref[...] = rh * lng_ref[...] + lnb_ref[...]

    tok = pl.BlockSpec((ts, D), lambda i, j: (i, 0))
    wrow = pl.BlockSpec((1, 1, Fs, D), lambda i, j: (j, ls, 0, 0))
    hid = pl.BlockSpec((1, ts, Fs), lambda i, j: (j, i, 0))
    return pl.pallas_call(
        body, name="ffn_fwd", grid=(S // ts, N_CHIPS),
        in_specs=[tok, _full((3, D)), wrow, wrow, wrow, _full((1, D)), _full((1, D))],
        out_specs=[tok, tok, hid, hid, tok],
        out_shape=[jax.ShapeDtypeStruct((S, D), f32), jax.ShapeDtypeStruct((S, D), f32),
                   jax.ShapeDtypeStruct((N_CHIPS, S, Fs), bf16), jax.ShapeDtypeStruct((N_CHIPS, S, Fs), bf16),
                   jax.ShapeDtypeStruct((S, D), bf16)],
        scratch_shapes=[pltpu.VMEM((ts, D), f32)],
        compiler_params=_cp(("parallel", "arbitrary"), 56),
    )(x, mod3, wg, wu, wd, lng, lnb)


def _ffn_bwd(dxo, x, f, g, u, mod3, wg, wu, wd, ls, lng):
    S, D = x.shape
    Fs = wg.shape[-2]
    ts = TOK_TILE

    def body(dxo_ref, x_ref, f_ref, g_ref, u_ref, mod_ref, wg_ref, wu_ref, wd_ref, lng_ref,
             dx_ref, dg_ref, du_ref, a_ref, df_ref, dmod_ref, dlng_ref, dlnb_ref,
             dr_sc, df_sc, acc_sc):
        i = pl.program_id(0)
        j = pl.program_id(1)

        @pl.when((i == 0) & (j == 0))
        def _():
            dmod_ref[...] = jnp.zeros_like(dmod_ref)
            dlng_ref[...] = jnp.zeros_like(dlng_ref)
            dlnb_ref[...] = jnp.zeros_like(dlnb_ref)

        @pl.when(j == 0)
        def _():
            xv = x_ref[...]
            fv = f_ref[...]
            gate = mod_ref[2:3, :]
            rh, rstd = _ln_stats(ALPHA * xv + (FFN_RES * gate) * fv)
            dy = dxo_ref[...]
            dlng_ref[...] += jnp.sum(dy * rh, 0, keepdims=True)
            dlnb_ref[...] += jnp.sum(dy, 0, keepdims=True)
            dr = _ln_bwd(dy * lng_ref[...], rh, rstd)
            dr_sc[...] = dr
            dmod_ref[2:3, :] += jnp.sum(FFN_RES * dr * fv, 0, keepdims=True)
            df = ((FFN_RES * gate) * dr).astype(bf16)
            df_sc[...] = df
            df_ref[...] = df
            acc_sc[...] = jnp.zeros_like(acc_sc)

        da = _dot_nt(df_sc[...], wd_ref[0, 0])
        gv = g_ref[0].astype(f32)
        uv = u_ref[0].astype(f32)
        sg = jax.nn.sigmoid(gv)
        si = gv * sg
        a_ref[0] = (si * uv).astype(bf16)
        dgv = (da * uv * (sg * (1.0 + gv * (1.0 - sg)))).astype(bf16)
        duv = (da * si).astype(bf16)
        dg_ref[0] = dgv
        du_ref[0] = duv
        acc_sc[...] += _dot(dgv, wg_ref[0, 0]) + _dot(duv, wu_ref[0, 0])

        @pl.when(j == N_CHIPS - 1)
        def _():
            dh = acc_sc[...]
            xh, rstd0 = _ln_stats(x_ref[...])
            dmod_ref[0:1, :] += jnp.sum(dh, 0, keepdims=True)
            dmod_ref[1:2, :] += jnp.sum(dh * xh, 0, keepdims=True)
            dx_ref[...] = _ln_bwd(dh * (1.0 + mod_ref[1:2, :]), xh, rstd0) + ALPHA * dr_sc[...]

    tok = pl.BlockSpec((ts, D), lambda i, j: (i, 0))
    wrow = pl.BlockSpec((1, 1, Fs, D), lambda i, j: (j, ls, 0, 0))
    hid = pl.BlockSpec((1, ts, Fs), lambda i, j: (j, i, 0))
    hid_shape = jax.ShapeDtypeStruct((N_CHIPS, S, Fs), bf16)
    return pl.pallas_call(
        body, name="ffn_bwd", grid=(S // ts, N_CHIPS),
        in_specs=[tok, tok, tok, hid, hid, _full((3, D)), wrow, wrow, wrow, _full((1, D))],
        out_specs=[tok, hid, hid, hid, tok, _full((3, D)), _full((1, D)), _full((1, D))],
        out_shape=[jax.ShapeDtypeStruct((S, D), f32), hid_shape, hid_shape, hid_shape,
                   jax.ShapeDtypeStruct((S, D), bf16),
                   jax.ShapeDtypeStruct((3, D), f32), jax.ShapeDtypeStruct((1, D), f32), jax.ShapeDtypeStruct((1, D), f32)],
        scratch_shapes=[pltpu.VMEM((ts, D), f32), pltpu.VMEM((ts, D), bf16), pltpu.VMEM((ts, D), f32)],
        compiler_params=_cp(("arbitrary", "arbitrary"), 56),
    )(dxo, x, f, g, u, mod3, wg, wu, wd, lng)


def _ffn_wgrad(h, dg, du, a, df, gwg, gwu, gwd, ls):
    S, D = h.shape
    Fs = dg.shape[-1]
    tk = TOK_TILE
    nk = S // tk

    def body(h_ref, dg_ref, du_ref, a_ref, df_ref, _g0, _g1, _g2, gwg_ref, gwu_ref, gwd_ref, ag_sc, au_sc, ad_sc):
        k = pl.program_id(1)

        @pl.when(k == 0)
        def _():
            ag_sc[...] = jnp.zeros_like(ag_sc)
            au_sc[...] = jnp.zeros_like(au_sc)
            ad_sc[...] = jnp.zeros_like(ad_sc)

        hv = h_ref[...]
        ag_sc[...] += _dot_tn(dg_ref[0], hv)
        au_sc[...] += _dot_tn(du_ref[0], hv)
        ad_sc[...] += _dot_tn(a_ref[0], df_ref[...])

        @pl.when(k == nk - 1)
        def _():
            gwg_ref[0, 0] = ag_sc[...].astype(bf16)
            gwu_ref[0, 0] = au_sc[...].astype(bf16)
            gwd_ref[0, 0] = ad_sc[...].astype(bf16)

    tok = pl.BlockSpec((tk, D), lambda p, k: (k, 0))
    hid = pl.BlockSpec((1, tk, Fs), lambda p, k: (p, k, 0))
    anyspec = pl.BlockSpec(memory_space=pl.ANY)
    orow = pl.BlockSpec((1, 1, Fs, D), lambda p, k: (p, ls, 0, 0))
    return pl.pallas_call(
        body, name="ffn_wgrad", grid=(N_CHIPS, nk),
        in_specs=[tok, hid, hid, hid, tok, anyspec, anyspec, anyspec],
        out_specs=[orow, orow, orow],
        out_shape=[jax.ShapeDtypeStruct(gwg.shape, bf16), jax.ShapeDtypeStruct(gwu.shape, bf16),
                   jax.ShapeDtypeStruct(gwd.shape, bf16)],
        scratch_shapes=[pltpu.VMEM((Fs, D), f32), pltpu.VMEM((Fs, D), f32), pltpu.VMEM((Fs, D), f32)],
        input_output_aliases={5: 0, 6: 1, 7: 2},
        compiler_params=_cp(("parallel", "arbitrary"), 48),
    )(h, dg, du, a, df, gwg, gwu, gwd)


_LANES = 128
_QKV_BLOCKS = D_ATT // _LANES


def _res_spec(lead, d, width, index):
    return pl.BlockSpec((lead, d, TOK_TILE // d, width), index)


def _res_spec3(d, width):
    return pl.BlockSpec((d, TOK_TILE // d, width), lambda i: (0, i, 0))


def _rows_to_residues(tile_bufs, d, put):
    for r in range(d):
        for cb, buf in enumerate(tile_bufs):
            put(r, cb, buf[pl.ds(r, TOK_TILE // d, stride=d), :])


def _residues_to_rows(tile_bufs, d, get):
    for r in range(d):
        for cb, buf in enumerate(tile_bufs):
            buf[pl.ds(r, TOK_TILE // d, stride=d), :] = get(r, cb)


def _mix_in_fwd(x, mod3, w_in, l):
    S, D = x.shape
    N = w_in.shape[-1]
    ts = TOK_TILE

    def body(x_ref, mod_ref, w_ref, o1_ref, o4_ref, o16_ref, zr_ref, h_ref, *bufs):
        j = pl.program_id(1)

        @pl.when(j == 0)
        def _():
            xh, _ = _ln_stats(x_ref[...])
            h_ref[...] = (xh * (1.0 + mod_ref[1:2, :]) + mod_ref[0:1, :]).astype(bf16)

        z = _dot(h_ref[...], w_ref[0, 0])

        @pl.when(j == N_CHIPS - 1)
        def _():
            zr_ref[...] = z

        @pl.when(j < N_CHIPS - 1)
        def _():
            zz = z * jnp.where(j == 0, HEAD_DIM ** -0.5, 1.0)
            o1_ref[j, 0] = zz.astype(bf16)
            for cb, buf in enumerate(bufs):
                buf[...] = zz[:, _LANES * cb:_LANES * (cb + 1)]
            for d, o_ref in zip(DILATIONS[1:], (o4_ref, o16_ref)):
                def put(r, cb, piece, o_ref=o_ref):
                    o_ref[j, r, :, _LANES * cb:_LANES * (cb + 1)] = piece.astype(bf16)
                _rows_to_residues(bufs, d, put)

    tok = pl.BlockSpec((ts, D), lambda i, j: (i, 0))
    res = [_res_spec(3, d, N, lambda i, j: (0, 0, i, 0)) for d in DILATIONS]
    return pl.pallas_call(
        body, name="mix_in_fwd", grid=(S // ts, N_CHIPS),
        in_specs=[tok, _full((3, D)), pl.BlockSpec((1, 1, D, N), lambda i, j: (j, l, 0, 0))],
        out_specs=res + [pl.BlockSpec((ts, N), lambda i, j: (i, 0)), tok],
        out_shape=[jax.ShapeDtypeStruct((3, d, S // d, N), bf16) for d in DILATIONS]
        + [jax.ShapeDtypeStruct((S, N), f32), jax.ShapeDtypeStruct((S, D), bf16)],
        scratch_shapes=[pltpu.VMEM((ts, _LANES), f32)] * _QKV_BLOCKS,
        compiler_params=_cp(("parallel", "arbitrary"), 40),
    )(x, mod3, w_in)


def _mix_in_bwd(dqkv, d_rest, dx_res, x, mod3, w_in, l):
    S, D = x.shape
    N = w_in.shape[-1]
    ts = TOK_TILE

    def body(d1_ref, d4_ref, d16_ref, dr_ref, dxr_ref, x_ref, mod_ref, w_ref, dx_ref, dmod_ref, dz_ref, acc_sc, *bufs):
        i = pl.program_id(0)
        j = pl.program_id(1)

        @pl.when((i == 0) & (j == 0))
        def _():
            dmod_ref[...] = jnp.zeros_like(dmod_ref)

        @pl.when(j == 0)
        def _():
            acc_sc[...] = jnp.zeros_like(acc_sc)

        @pl.when(j == N_CHIPS - 1)
        def _():
            dz_ref[0] = dr_ref[...]

        @pl.when(j < N_CHIPS - 1)
        def _():
            for d, d_ref, tile_bufs in ((4, d4_ref, bufs[:_QKV_BLOCKS]), (16, d16_ref, bufs[_QKV_BLOCKS:])):
                _residues_to_rows(tile_bufs, d, lambda r, cb, d_ref=d_ref: d_ref[0, r, :, _LANES * cb:_LANES * (cb + 1)].astype(f32))
            for cb in range(_QKV_BLOCKS):
                cols = slice(_LANES * cb, _LANES * (cb + 1))
                dz_ref[0, :, cols] = (d1_ref[0, 0, :, cols].astype(f32) + bufs[cb][...] + bufs[_QKV_BLOCKS + cb][...]).astype(bf16)

        acc_sc[...] += _dot_nt(dz_ref[0], w_ref[0, 0])

        @pl.when(j == N_CHIPS - 1)
        def _():
            dh = acc_sc[...]
            xh, rstd0 = _ln_stats(x_ref[...])
            dmod_ref[0:1, :] += jnp.sum(dh, 0, keepdims=True)
            dmod_ref[1:2, :] += jnp.sum(dh * xh, 0, keepdims=True)
            dx_ref[...] = _ln_bwd(dh * (1.0 + mod_ref[1:2, :]), xh, rstd0) + dxr_ref[...]

    tok = pl.BlockSpec((ts, D), lambda i, j: (i, 0))
    res = [_res_spec(1, d, N, lambda i, j: (jnp.minimum(j, 2), 0, i, 0)) for d in DILATIONS]
    return pl.pallas_call(
        body, name="mix_in_bwd", grid=(S // ts, N_CHIPS),
        in_specs=res + [pl.BlockSpec((ts, N), lambda i, j: (i, 0)), tok, tok, _full((3, D)),
                        pl.BlockSpec((1, 1, D, N), lambda i, j: (j, l, 0, 0))],
        out_specs=[tok, _full((3, D)), pl.BlockSpec((1, ts, N), lambda i, j: (j, i, 0))],
        out_shape=[jax.ShapeDtypeStruct((S, D), f32), jax.ShapeDtypeStruct((3, D), f32),
                   jax.ShapeDtypeStruct((N_CHIPS, S, N), bf16)],
        scratch_shapes=[pltpu.VMEM((ts, D), f32)] + [pltpu.VMEM((ts, _LANES), f32)] * (2 * _QKV_BLOCKS),
        compiler_params=_cp(("arbitrary", "arbitrary"), 40),
    )(*dqkv, d_rest, dx_res, x, mod3, w_in)


def _mix_in_wgrad(h, dz, gw, l):
    S, D = h.shape
    N = dz.shape[-1]
    tk = TOK_TILE
    nk = S // tk

    def body(h_ref, dz_ref, _g, gw_ref, acc_sc):
        k = pl.program_id(1)

        @pl.when(k == 0)
        def _():
            acc_sc[...] = jnp.zeros_like(acc_sc)

        acc_sc[...] += _dot_tn(h_ref[...], dz_ref[0])

        @pl.when(k == nk - 1)
        def _():
            gw_ref[0, 0] = acc_sc[...].astype(bf16)

    return pl.pallas_call(
        body, name="mix_in_wgrad", grid=(N_CHIPS, nk),
        in_specs=[pl.BlockSpec((tk, D), lambda p, k: (k, 0)), pl.BlockSpec((1, tk, N), lambda p, k: (p, k, 0)),
                  pl.BlockSpec(memory_space=pl.ANY)],
        out_specs=pl.BlockSpec((1, 1, D, N), lambda p, k: (p, l, 0, 0)),
        out_shape=jax.ShapeDtypeStruct(gw.shape, bf16),
        scratch_shapes=[pltpu.VMEM((D, N), f32)],
        input_output_aliases={2: 0},
        compiler_params=_cp(("parallel", "arbitrary"), 40),
    )(h, dz, gw)


def _mix_out_fwd(x, y_att, y_ssm, y_pool, mod3, w_out, l, lng, lnb):
    S, D = x.shape
    ts = TOK_TILE

    def body(x_ref, ya_ref, ys_ref, yp_ref, mod_ref, w_ref, lng_ref, lnb_ref, xo_ref, y_ref):
        ya = ya_ref[...].astype(bf16)
        y = (_dot(ya[:, 0:256], w_ref[0, 0]) + _dot(ya[:, 256:512], w_ref[1, 0])
             + _dot(ys_ref[...].astype(bf16), w_ref[2, 0]) + _dot(yp_ref[...].astype(bf16), w_ref[3, 0]))
        y_ref[...] = y
        rh, _ = _ln_stats(ALPHA * x_ref[...] + mod_ref[2:3, :] * y)
        xo_ref[...] = rh * lng_ref[...] + lnb_ref[...]

    tok = pl.BlockSpec((ts, D), lambda i: (i, 0))
    return pl.pallas_call(
        body, name="mix_out_fwd", grid=(S // ts,),
        in_specs=[tok, pl.BlockSpec((ts, D_ATT), lambda i: (i, 0)), pl.BlockSpec((ts, D_SSM), lambda i: (i, 0)),
                  pl.BlockSpec((ts, D_POOL), lambda i: (i, 0)), _full((3, D)),
                  pl.BlockSpec((N_CHIPS, 1, 256, D), lambda i: (0, l, 0, 0)), _full((1, D)), _full((1, D))],
        out_specs=[tok, tok],
        out_shape=[jax.ShapeDtypeStruct((S, D), f32), jax.ShapeDtypeStruct((S, D), f32)],
        compiler_params=_cp(("parallel",), 40),
    )(x, y_att, y_ssm, y_pool, mod3, w_out, lng, lnb)


def _mix_out_bwd(dxo, x, y, y_att, y_ssm, y_pool, mod3, w_out, l, lng, gw_out):
    S, D = x.shape
    ts = TOK_TILE
    nt = S // ts

    def body(dxo_ref, x_ref, y_ref, ya_ref, ys_ref, yp_ref, mod_ref, w_ref, lng_ref, _g,
             dxr_ref, da_ref, ds_ref, dp_ref, dgate_ref, dlng_ref, dlnb_ref, gw_ref, acc_sc):
        i = pl.program_id(0)

        @pl.when(i == 0)
        def _():
            dgate_ref[...] = jnp.zeros_like(dgate_ref)
            dlng_ref[...] = jnp.zeros_like(dlng_ref)
            dlnb_ref[...] = jnp.zeros_like(dlnb_ref)
            acc_sc[...] = jnp.zeros_like(acc_sc)

        gate = mod_ref[2:3, :]
        yv = y_ref[...]
        rh, rstd = _ln_stats(ALPHA * x_ref[...] + gate * yv)
        dy_out = dxo_ref[...]
        dlng_ref[...] += jnp.sum(dy_out * rh, 0, keepdims=True)
        dlnb_ref[...] += jnp.sum(dy_out, 0, keepdims=True)
        dr = _ln_bwd(dy_out * lng_ref[...], rh, rstd)
        dxr_ref[...] = ALPHA * dr
        dgate_ref[...] += jnp.sum(dr * yv, 0, keepdims=True)
        dy = (gate * dr).astype(bf16)
        da_ref[:, 0:256] = _dot_nt(dy, w_ref[0, 0])
        da_ref[:, 256:512] = _dot_nt(dy, w_ref[1, 0])
        ds_ref[...] = _dot_nt(dy, w_ref[2, 0])
        dp_ref[...] = _dot_nt(dy, w_ref[3, 0])
        ya = ya_ref[...].astype(bf16)
        acc_sc[0] += _dot_tn(ya[:, 0:256], dy)
        acc_sc[1] += _dot_tn(ya[:, 256:512], dy)
        acc_sc[2] += _dot_tn(ys_ref[...].astype(bf16), dy)
        acc_sc[3] += _dot_tn(yp_ref[...].astype(bf16), dy)

        @pl.when(i == nt - 1)
        def _():
            gw_ref[:, 0] = acc_sc[...].astype(bf16)

    tok = pl.BlockSpec((ts, D), lambda i: (i, 0))
    t512 = pl.BlockSpec((ts, D_ATT), lambda i: (i, 0))
    t256 = pl.BlockSpec((ts, 256), lambda i: (i, 0))
    wspec = pl.BlockSpec((N_CHIPS, 1, 256, D), lambda i: (0, l, 0, 0))
    return pl.pallas_call(
        body, name="mix_out_bwd", grid=(nt,),
        in_specs=[tok, tok, tok, t512, t256, t256, _full((3, D)), wspec, _full((1, D)), pl.BlockSpec(memory_space=pl.ANY)],
        out_specs=[tok, t512, t256, t256, _full((1, D)), _full((1, D)), _full((1, D)), wspec],
        out_shape=[jax.ShapeDtypeStruct((S, D), f32), jax.ShapeDtypeStruct((S, D_ATT), f32),
                   jax.ShapeDtypeStruct((S, D_SSM), f32), jax.ShapeDtypeStruct((S, D_POOL), f32),
                   jax.ShapeDtypeStruct((1, D), f32), jax.ShapeDtypeStruct((1, D), f32), jax.ShapeDtypeStruct((1, D), f32),
                   jax.ShapeDtypeStruct(gw_out.shape, bf16)],
        scratch_shapes=[pltpu.VMEM((N_CHIPS, 256, D), f32)],
        input_output_aliases={9: 7},
        compiler_params=_cp(("arbitrary",), 48),
    )(dxo, x, y, y_att, y_ssm, y_pool, mod3, w_out, lng, gw_out)


def _t5_bucket(dist):
    max_exact = N_BUCKETS // 2
    d = np.maximum(dist, 1).astype(np.float32)
    large = max_exact + (np.log(d / max_exact) / math.log(MAX_DISTANCE / max_exact)
                         * (N_BUCKETS - max_exact)).astype(np.int32)
    large = np.minimum(large, N_BUCKETS - 1)
    return np.where(dist < max_exact, dist, large).astype(np.int32)


def _bucket_table():
    q = ATT_BLOCK
    i = np.arange(q)[:, None]
    j = np.arange(2 * q)[None, :]
    r = i + q - j
    in_band = (r >= 0) & (r <= q)
    tabs = [np.where(in_band, _t5_bucket(np.clip(r, 0, None) * d), -1) for d in DILATIONS]
    return np.stack(tabs).astype(np.int32)


def _bias_fwd(rel_bias, table):
    def body(rb_ref, tab_ref, out_ref):
        for b in range(3):
            tb = tab_ref[b]
            for h in range(N_HEADS):
                def pick(k, acc):
                    return jnp.where(tb == k, rb_ref[k, h], acc)
                out_ref[b, h] = lax.fori_loop(0, N_BUCKETS, pick, jnp.where(tb < 0, NEG, 0.0).astype(f32))

    return pl.pallas_call(
        body, name="bias_fwd",
        in_specs=[pl.BlockSpec(memory_space=pltpu.SMEM), pl.BlockSpec(memory_space=pltpu.VMEM)],
        out_specs=pl.BlockSpec(memory_space=pltpu.VMEM),
        out_shape=jax.ShapeDtypeStruct((3, N_HEADS, ATT_BLOCK, 2 * ATT_BLOCK), f32),
    )(rel_bias, table)


def _bias_bwd(dbias, table):
    def body(db_ref, tab_ref, out_ref):
        def per_bucket(k, c):
            for h in range(N_HEADS):
                tot = jnp.zeros((), f32)
                for b in range(3):
                    tot = tot + jnp.sum(jnp.where(tab_ref[b] == k, db_ref[b, h], 0.0))
                out_ref[k, h] = tot
            return c
        lax.fori_loop(0, N_BUCKETS, per_bucket, 0)

    return pl.pallas_call(
        body, name="bias_bwd",
        in_specs=[pl.BlockSpec(memory_space=pltpu.VMEM), pl.BlockSpec(memory_space=pltpu.VMEM)],
        out_specs=pl.BlockSpec(memory_space=pltpu.SMEM),
        out_shape=jax.ShapeDtypeStruct((N_BUCKETS, N_HEADS), f32),
    )(dbias, table)


def _att_unit(u, nbr):
    rows = pl.ds(pl.multiple_of(u * ATT_BLOCK, ATT_BLOCK), ATT_BLOCK)
    prev = pl.ds(pl.multiple_of(jnp.maximum(u - 1, 0) * ATT_BLOCK, ATT_BLOCK), ATT_BLOCK)
    return rows, prev, (u % nbr) != 0


_HEAD_ROWS = N_HEADS * ATT_BLOCK


def _head_rows(t):
    lane = lax.broadcasted_iota(jnp.int32, t.shape, 1)
    return jnp.concatenate([jnp.where((lane >= HEAD_DIM * h) & (lane < HEAD_DIM * (h + 1)), t, jnp.zeros_like(t))
                            for h in range(N_HEADS)], axis=0)


def _head_cols(big):
    lane = lax.broadcasted_iota(jnp.int32, (ATT_BLOCK, D_ATT), 1)
    out = big[0:ATT_BLOCK]
    for h in range(1, N_HEADS):
        out = jnp.where(lane >= HEAD_DIM * h, big[ATT_BLOCK * h:ATT_BLOCK * (h + 1)], out)
    return out


def _head_column(ref, rows):
    t = ref[rows, :]
    return jnp.concatenate([t[:, h:h + 1] for h in range(N_HEADS)], axis=0)


def _att_band(ref, rows, prev, nbr):
    cur = ref[0, rows, :]
    return cur if nbr == 1 else jnp.concatenate([ref[0, prev, :], cur], axis=0)


def _att_scores(q_ref, k_ref, b_ref, rows, prev, valid_prev, nbr):
    qbd = _head_rows(q_ref[0, rows, :])
    kb = _att_band(k_ref, rows, prev, nbr)
    bias = b_ref[0].reshape(_HEAD_ROWS, 2 * ATT_BLOCK)
    if nbr == 1:
        return qbd, kb, _dot_nt(qbd, kb) + bias[:, ATT_BLOCK:]
    s = _dot_nt(qbd, kb) + bias
    col = lax.broadcasted_iota(jnp.int32, s.shape, 1)
    return qbd, kb, jnp.where((col >= ATT_BLOCK) | valid_prev, s, NEG)


def _qkv_specs(S, branch):
    return ([pl.BlockSpec((1, S, D_ATT), lambda i, t=t: (t, 0, 0)) for t in range(3)],
            pl.BlockSpec((1, N_HEADS, ATT_BLOCK, 2 * ATT_BLOCK), lambda i: (branch, 0, 0, 0)))


def _att_fwd(qkv, bias, branch):
    S = qkv.shape[1]
    nbr = BLOCKS_PER_RESIDUE[branch]

    def body(q_ref, k_ref, v_ref, b_ref, o_ref, lse_ref):
        lse_ref[...] = jnp.zeros_like(lse_ref)

        def unit(u, c):
            rows, prev, valid_prev = _att_unit(u, nbr)
            _, _, s = _att_scores(q_ref, k_ref, b_ref, rows, prev, valid_prev, nbr)
            m = jnp.max(s, -1, keepdims=True)
            p = jnp.exp(s - m)
            den = jnp.sum(p, -1, keepdims=True)
            big = _dot(p.astype(bf16), _att_band(v_ref, rows, prev, nbr))
            o_ref[rows, :] = _head_cols(big / den)
            lse = m + jnp.log(den)
            for h in range(N_HEADS):
                lse_ref[rows, pl.ds(h, 1)] = lse[ATT_BLOCK * h:ATT_BLOCK * (h + 1)]
            return c

        lax.fori_loop(0, N_UNITS, unit, 0)

    qkv_specs, bspec = _qkv_specs(S, branch)
    return pl.pallas_call(
        body, name="att_fwd", grid=(1,),
        in_specs=qkv_specs + [bspec],
        out_specs=[pl.BlockSpec((S, D_ATT), lambda i: (0, 0)), pl.BlockSpec((S, _LANES), lambda i: (0, 0))],
        out_shape=[jax.ShapeDtypeStruct((S, D_ATT), f32), jax.ShapeDtypeStruct((S, _LANES), f32)],
        compiler_params=_cp(("arbitrary",), 40),
    )(qkv, qkv, qkv, bias)


def _att_bwd(qkv, do, lse, crow, bias, branch):
    S = qkv.shape[1]
    nbr = BLOCKS_PER_RESIDUE[branch]

    def body(q_ref, k_ref, v_ref, do_ref, lse_ref, c_ref, b_ref, dqkv_ref, db_ref, dk_sc, dv_sc):
        dk_sc[...] = jnp.zeros_like(dk_sc)
        dv_sc[...] = jnp.zeros_like(dv_sc)
        db_ref[...] = jnp.zeros_like(db_ref)

        def unit(u, c):
            rows, prev, valid_prev = _att_unit(u, nbr)
            qbd, kb, s = _att_scores(q_ref, k_ref, b_ref, rows, prev, valid_prev, nbr)
            p = jnp.exp(s - _head_column(lse_ref, rows))
            dobd = _head_rows(do_ref[rows, :])
            ds = p * (_dot_nt(dobd, _att_band(v_ref, rows, prev, nbr)) - _head_column(c_ref, rows))
            if nbr == 1:
                db_ref[:, :, ATT_BLOCK:] += ds.reshape(N_HEADS, ATT_BLOCK, ATT_BLOCK)
            else:
                db_ref[...] += ds.reshape(N_HEADS, ATT_BLOCK, 2 * ATT_BLOCK)
            dsb = ds.astype(bf16)
            dqkv_ref[0, rows, :] = (HEAD_DIM ** -0.5 * _head_cols(_dot(dsb, kb))).astype(bf16)
            dkb = _dot_tn(dsb, qbd)
            dvb = _dot_tn(p.astype(bf16), dobd)
            if nbr == 1:
                dk_sc[rows, :] += dkb
                dv_sc[rows, :] += dvb
            else:
                dk_sc[prev, :] += dkb[:ATT_BLOCK]
                dv_sc[prev, :] += dvb[:ATT_BLOCK]
                dk_sc[rows, :] += dkb[ATT_BLOCK:]
                dv_sc[rows, :] += dvb[ATT_BLOCK:]
            return c

        lax.fori_loop(0, N_UNITS, unit, 0)
        dqkv_ref[1] = dk_sc[...].astype(bf16)
        dqkv_ref[2] = dv_sc[...].astype(bf16)

    qkv_specs, bspec = _qkv_specs(S, branch)
    row = pl.BlockSpec((S, _LANES), lambda i: (0, 0))
    return pl.pallas_call(
        body, name="att_bwd", grid=(1,),
        in_specs=qkv_specs + [pl.BlockSpec((S, D_ATT), lambda i: (0, 0)), row, row, bspec],
        out_specs=[pl.BlockSpec((3, S, D_ATT), lambda i: (0, 0, 0)),
                   pl.BlockSpec((N_HEADS, ATT_BLOCK, 2 * ATT_BLOCK), lambda i: (0, 0, 0))],
        out_shape=[jax.ShapeDtypeStruct((3, S, D_ATT), bf16), jax.ShapeDtypeStruct((N_HEADS, ATT_BLOCK, 2 * ATT_BLOCK), f32)],
        scratch_shapes=[pltpu.VMEM((S, D_ATT), f32), pltpu.VMEM((S, D_ATT), f32)],
        compiler_params=_cp(("arbitrary",), 48),
    )(qkv, qkv, qkv, do, lse, crow, bias)


def _branch_weights(lse_ref):
    l0, l1, l2 = lse_ref[0], lse_ref[1], lse_ref[2]
    m = jnp.maximum(jnp.maximum(l0, l1), l2)
    e0, e1, e2 = jnp.exp(l0 - m), jnp.exp(l1 - m), jnp.exp(l2 - m)
    tot = e0 + e1 + e2
    return e0 / tot, e1 / tot, e2 / tot


def _att_merge(os, lses):
    S = os[0].shape[0] * os[0].shape[1]
    ts = TOK_TILE

    def body(o1_ref, o4_ref, o16_ref, l1_ref, l4_ref, l16_ref, y_ref, lt_ref, *bufs):
        obufs = (bufs[:_QKV_BLOCKS], bufs[_QKV_BLOCKS:2 * _QKV_BLOCKS])
        lt_ref[0] = l1_ref[0]
        for k, (d, o_ref, l_ref) in enumerate(((4, o4_ref, l4_ref), (16, o16_ref, l16_ref))):
            _residues_to_rows(obufs[k], d, lambda r, cb, o_ref=o_ref: o_ref[r, :, _LANES * cb:_LANES * (cb + 1)])
            _residues_to_rows([bufs[2 * _QKV_BLOCKS + k]], d, lambda r, cb, l_ref=l_ref: l_ref[r])
            lt_ref[1 + k] = bufs[2 * _QKV_BLOCKS + k][...]
        w = _branch_weights(lt_ref)
        for h in range(N_HEADS):
            cs = slice(HEAD_DIM * h, HEAD_DIM * (h + 1))
            half = slice(HEAD_DIM * (h % 2), HEAD_DIM * (h % 2 + 1))
            y_ref[:, cs] = (w[0][:, h:h + 1] * o1_ref[0, :, cs] + w[1][:, h:h + 1] * obufs[0][h // 2][:, half]
                            + w[2][:, h:h + 1] * obufs[1][h // 2][:, half])

    return pl.pallas_call(
        body, name="att_merge", grid=(S // ts,),
        in_specs=[_res_spec3(d, D_ATT) for d in DILATIONS] + [_res_spec3(d, _LANES) for d in DILATIONS],
        out_specs=[pl.BlockSpec((ts, D_ATT), lambda i: (i, 0)), pl.BlockSpec((3, ts, _LANES), lambda i: (0, i, 0))],
        out_shape=[jax.ShapeDtypeStruct((S, D_ATT), f32), jax.ShapeDtypeStruct((3, S, _LANES), f32)],
        scratch_shapes=[pltpu.VMEM((ts, _LANES), f32)] * (2 * _QKV_BLOCKS + 2),
        compiler_params=_cp(("parallel",)),
    )(*os, *lses)


def _att_merge_bwd(dy, y, lse3):
    S = dy.shape[0]
    ts = TOK_TILE

    def body(dy_ref, y_ref, lse_ref, do1_ref, do4_ref, do16_ref, c1_ref, c4_ref, c16_ref, *bufs):
        dobufs = (bufs[:_QKV_BLOCKS], bufs[_QKV_BLOCKS:2 * _QKV_BLOCKS], bufs[2 * _QKV_BLOCKS:3 * _QKV_BLOCKS])
        cbufs = bufs[3 * _QKV_BLOCKS:]
        w = _branch_weights(lse_ref)
        for cb in cbufs:
            cb[...] = jnp.zeros_like(cb)
        for h in range(N_HEADS):
            cs = slice(HEAD_DIM * h, HEAD_DIM * (h + 1))
            half = slice(HEAD_DIM * (h % 2), HEAD_DIM * (h % 2 + 1))
            dyh = dy_ref[:, cs]
            t = jnp.sum(dyh * y_ref[:, cs], -1, keepdims=True)
            for p in range(3):
                wp = w[p][:, h:h + 1]
                dobufs[p][h // 2][:, half] = wp * dyh
                cbufs[p][:, h:h + 1] = wp * t
        for cb in range(_QKV_BLOCKS):
            do1_ref[0, :, _LANES * cb:_LANES * (cb + 1)] = dobufs[0][cb][...].astype(bf16)
        c1_ref[0] = cbufs[0][...]
        for k, (d, do_ref, c_ref) in enumerate(((4, do4_ref, c4_ref), (16, do16_ref, c16_ref))):
            def put_do(r, cb, piece, do_ref=do_ref):
                do_ref[r, :, _LANES * cb:_LANES * (cb + 1)] = piece.astype(bf16)

            def put_c(r, cb, piece, c_ref=c_ref):
                c_ref[r] = piece

            _rows_to_residues(dobufs[1 + k], d, put_do)
            _rows_to_residues([cbufs[1 + k]], d, put_c)

    return pl.pallas_call(
        body, name="att_merge_bwd", grid=(S // ts,),
        in_specs=[pl.BlockSpec((ts, D_ATT), lambda i: (i, 0)), pl.BlockSpec((ts, D_ATT), lambda i: (i, 0)),
                  pl.BlockSpec((3, ts, _LANES), lambda i: (0, i, 0))],
        out_specs=[_res_spec3(d, D_ATT) for d in DILATIONS] + [_res_spec3(d, _LANES) for d in DILATIONS],
        out_shape=[jax.ShapeDtypeStruct((d, S // d, D_ATT), bf16) for d in DILATIONS]
        + [jax.ShapeDtypeStruct((d, S // d, _LANES), f32) for d in DILATIONS],
        scratch_shapes=[pltpu.VMEM((ts, _LANES), f32)] * (3 * _QKV_BLOCKS + 3),
        compiler_params=_cp(("parallel",)),
    )(dy, y, lse3)


_SSM_ROWS = 256


def _scan_in_place(sr_ref, si_ref, a_ref, reverse):
    S, N = sr_ref.shape
    nst = S // SCAN_SEG
    ar = jnp.broadcast_to(a_ref[0:1, :], (SCAN_SEG, N))
    ai = jnp.broadcast_to(a_ref[1:2, :], (SCAN_SEG, N))
    if reverse:
        ai = -ai
    row = lax.broadcasted_iota(jnp.int32, (SCAN_SEG, N), 0)
    zero = jnp.zeros((SCAN_SEG, N), f32)

    def tile(t):
        return pl.ds(pl.multiple_of((nst - 1 - t if reverse else t) * SCAN_SEG, SCAN_SEG), SCAN_SEG)

    def local(t, c):
        sr, si, pr, pi = c
        rows = tile(t)
        nsr = ar * sr - ai * si + sr_ref[rows, :]
        nsi = ar * si + ai * sr + si_ref[rows, :]
        sr_ref[rows, :] = nsr
        si_ref[rows, :] = nsi
        return nsr, nsi, ar * pr - ai * pi, ar * pi + ai * pr

    fr, fi, apr, api = lax.fori_loop(0, nst, local, (zero, zero, zero + 1.0, zero))

    def shift(v):
        if reverse:
            return jnp.where(row == SCAN_SEG - 1, 0.0, pltpu.roll(v, SCAN_SEG - 1, axis=0))
        return jnp.where(row == 0, 0.0, pltpu.roll(v, 1, axis=0))

    cr, ci = zero, zero
    for _ in range(SCAN_SEG - 1):
        cr, ci = shift(fr + apr * cr - api * ci), shift(fi + apr * ci + api * cr)

    def fix(t, c):
        pr, pi = c
        npr, npi = ar * pr - ai * pi, ar * pi + ai * pr
        rows = tile(t)
        sr_ref[rows, :] += npr * cr - npi * ci
        si_ref[rows, :] += npr * ci + npi * cr
        return npr, npi

    lax.fori_loop(0, nst, fix, (zero + 1.0, zero))


def _ssm_states(u, bre, bim, a2):
    S = u.shape[0]

    def body(u_ref, br_ref, bi_ref, a_ref, sr_ref, si_ref):
        brb = br_ref[...].astype(bf16)
        bib = bi_ref[...].astype(bf16)

        def project(t, c):
            rows = pl.ds(pl.multiple_of(t * _SSM_ROWS, _SSM_ROWS), _SSM_ROWS)
            ub = u_ref[rows, :].astype(bf16)
            sr_ref[rows, :] = _dot(ub, brb)
            si_ref[rows, :] = _dot(ub, bib)
            return c

        lax.fori_loop(0, S // _SSM_ROWS, project, 0)
        _scan_in_place(sr_ref, si_ref, a_ref, False)

    vm = pl.BlockSpec(memory_space=pltpu.VMEM)
    return pl.pallas_call(
        body, name="ssm_states", in_specs=[vm] * 4, out_specs=[vm, vm],
        out_shape=[jax.ShapeDtypeStruct((S, D_STATE), f32)] * 2,
        compiler_params=_cp(None, 48),
    )(u, bre, bim, a2)


def _ssm_out(sr, si, u, cre, cim, dskip, glu_w, glu_b):
    S = u.shape[0]
    ts = TOK_TILE

    def body(sr_ref, si_ref, u_ref, cr_ref, ci_ref, d_ref, w_ref, b_ref, out_ref, y_ref):
        y = (_dot(sr_ref[...].astype(bf16), cr_ref[...].astype(bf16))
             - _dot(si_ref[...].astype(bf16), ci_ref[...].astype(bf16)) + d_ref[...] * u_ref[...])
        y_ref[...] = y
        z = _dot(_gelu(y).astype(bf16), w_ref[...].astype(bf16)) + b_ref[...]
        out_ref[...] = y * jax.nn.sigmoid(z)

    st = pl.BlockSpec((ts, D_STATE), lambda i: (i, 0))
    ch = pl.BlockSpec((ts, D_SSM), lambda i: (i, 0))
    return pl.pallas_call(
        body, name="ssm_out", grid=(S // ts,),
        in_specs=[st, st, ch, _full((D_STATE, D_SSM)), _full((D_STATE, D_SSM)), _full((1, D_SSM)),
                  _full((D_SSM, D_SSM)), _full((1, D_SSM))],
        out_specs=[ch, ch],
        out_shape=[jax.ShapeDtypeStruct((S, D_SSM), f32)] * 2,
        compiler_params=_cp(("parallel",)),
    )(sr, si, u, cre, cim, dskip, glu_w, glu_b)


def _ssm_out_bwd(dout, y, u, sr, si, dskip, glu_w, glu_b):
    S = u.shape[0]
    ts = TOK_TILE

    def body(do_ref, y_ref, u_ref, sr_ref, si_ref, d_ref, w_ref, b_ref,
             dy_ref, du_ref, dcr_ref, dci_ref, dd_ref, dgb_ref, dgw_ref):
        @pl.when(pl.program_id(0) == 0)
        def _():
            for r in (dcr_ref, dci_ref, dd_ref, dgb_ref, dgw_ref):
                r[...] = jnp.zeros_like(r)

        y = y_ref[...]
        dout = do_ref[...]
        wb = w_ref[...].astype(bf16)
        ge = _gelu(y).astype(bf16)
        sz = jax.nn.sigmoid(_dot(ge, wb) + b_ref[...])
        dz = dout * y * sz * (1.0 - sz)
        dzb = dz.astype(bf16)
        dgb_ref[...] += jnp.sum(dz, 0, keepdims=True)
        dgw_ref[...] += _dot_tn(ge, dzb)
        dy = dout * sz + _gelu_grad(y) * _dot_nt(dzb, wb)
        uv = u_ref[...]
        dd_ref[...] += jnp.sum(dy * uv, 0, keepdims=True)
        du_ref[...] = dy * d_ref[...]
        dy_ref[...] = dy
        dyb = dy.astype(bf16)
        dcr_ref[...] += _dot_tn(sr_ref[...].astype(bf16), dyb)
        dci_ref[...] -= _dot_tn(si_ref[...].astype(bf16), dyb)

    st = pl.BlockSpec((ts, D_STATE), lambda i: (i, 0))
    ch = pl.BlockSpec((ts, D_SSM), lambda i: (i, 0))
    c_full = _full((D_STATE, D_SSM))
    return pl.pallas_call(
        body, name="ssm_out_bwd", grid=(S // ts,),
        in_specs=[ch, ch, ch, st, st, _full((1, D_SSM)), _full((D_SSM, D_SSM)), _full((1, D_SSM))],
        out_specs=[ch, ch, c_full, c_full, _full((1, D_SSM)), _full((1, D_SSM)), _full((D_SSM, D_SSM))],
        out_shape=[jax.ShapeDtypeStruct((S, D_SSM), f32), jax.ShapeDtypeStruct((S, D_SSM), f32),
                   jax.ShapeDtypeStruct((D_STATE, D_SSM), f32), jax.ShapeDtypeStruct((D_STATE, D_SSM), f32),
                   jax.ShapeDtypeStruct((1, D_SSM), f32), jax.ShapeDtypeStruct((1, D_SSM), f32),
                   jax.ShapeDtypeStruct((D_SSM, D_SSM), f32)],
        compiler_params=_cp(("arbitrary",), 40),
    )(dout, y, u, sr, si, dskip, glu_w, glu_b)


def _ssm_states_bwd(dy, du_skip, u, sr, si, cre, cim, bre, bim, a2):
    S = u.shape[0]
    N = D_STATE
    nst = S // SCAN_SEG
    nproj = S // _SSM_ROWS

    def body(dy_ref, dus_ref, u_ref, sr_ref, si_ref, cr_ref, ci_ref, br_ref, bi_ref, a_ref,
             du_ref, dbr_ref, dbi_ref, da_ref, lr_ref, li_ref):
        crb = cr_ref[...].astype(bf16)
        cib = ci_ref[...].astype(bf16)

        def project(t, c):
            rows = pl.ds(pl.multiple_of(t * _SSM_ROWS, _SSM_ROWS), _SSM_ROWS)
            dyb = dy_ref[rows, :].astype(bf16)
            lr_ref[rows, :] = _dot_nt(dyb, crb)
            li_ref[rows, :] = -_dot_nt(dyb, cib)
            return c

        lax.fori_loop(0, nproj, project, 0)
        _scan_in_place(lr_ref, li_ref, a_ref, True)

        row = lax.broadcasted_iota(jnp.int32, (SCAN_SEG, N), 0)
        last = pl.ds((nst - 1) * SCAN_SEG, SCAN_SEG)
        pr = jnp.where(row == 0, 0.0, pltpu.roll(sr_ref[last, :], 1, axis=0))
        pi = jnp.where(row == 0, 0.0, pltpu.roll(si_ref[last, :], 1, axis=0))
        first = pl.ds(0, SCAN_SEG)
        acc_r = lr_ref[first, :] * pr + li_ref[first, :] * pi
        acc_i = li_ref[first, :] * pr - lr_ref[first, :] * pi

        def step(t, c):
            acc_r, acc_i = c
            rows = pl.ds(pl.multiple_of(t * SCAN_SEG, SCAN_SEG), SCAN_SEG)
            prev = pl.ds(pl.multiple_of((t - 1) * SCAN_SEG, SCAN_SEG), SCAN_SEG)
            lrv, liv, srv, siv = lr_ref[rows, :], li_ref[rows, :], sr_ref[prev, :], si_ref[prev, :]
            return acc_r + lrv * srv + liv * siv, acc_i + liv * srv - lrv * siv

        acc_r, acc_i = lax.fori_loop(1, nst, step, (acc_r, acc_i))
        da_ref[0:1, :] = jnp.sum(acc_r, 0, keepdims=True)
        da_ref[1:2, :] = jnp.sum(acc_i, 0, keepdims=True)

        brb = br_ref[...].astype(bf16)
        bib = bi_ref[...].astype(bf16)
        dbr_ref[...] = jnp.zeros_like(dbr_ref)
        dbi_ref[...] = jnp.zeros_like(dbi_ref)

        def back(t, c):
            rows = pl.ds(pl.multiple_of(t * _SSM_ROWS, _SSM_ROWS), _SSM_ROWS)
            lrb = lr_ref[rows, :].astype(bf16)
            lib = li_ref[rows, :].astype(bf16)
            du_ref[rows, :] = dus_ref[rows, :] + _dot_nt(lrb, brb) + _dot_nt(lib, bib)
            ub = u_ref[rows, :].astype(bf16)
            dbr_ref[...] += _dot_tn(ub, lrb)
            dbi_ref[...] += _dot_tn(ub, lib)
            return c

        lax.fori_loop(0, nproj, back, 0)

    vm = pl.BlockSpec(memory_space=pltpu.VMEM)
    return pl.pallas_call(
        body, name="ssm_states_bwd", in_specs=[vm] * 10, out_specs=[vm] * 4,
        out_shape=[jax.ShapeDtypeStruct((S, D_SSM), f32), jax.ShapeDtypeStruct((D_SSM, D_STATE), f32),
                   jax.ShapeDtypeStruct((D_SSM, D_STATE), f32), jax.ShapeDtypeStruct((2, D_STATE), f32)],
        scratch_shapes=[pltpu.VMEM((S, D_STATE), f32), pltpu.VMEM((S, D_STATE), f32)],
        compiler_params=_cp(None, 56),
    )(dy, du_skip, u, sr, si, cre, cim, bre, bim, a2)


_POOL_TILE = 256


def _window_sums(xt, back):
    n = xt.shape[0]
    out = []
    ws = xt
    for k in (1, 2, 4, 8):
        ws = ws + pltpu.roll(ws, k if back else n - k, axis=0)
        out.append(ws)
    return out


def _pool_count(r0, w):
    t = r0 + lax.broadcasted_iota(jnp.int32, (_POOL_TILE, POOL_GROUP), 0)
    return jnp.minimum(t + 1, w).astype(f32)


def _pool_fwd(u_pad, pool_w, pool_scale):
    S = u_pad.shape[0] - POOL_HALO
    nt = S // _POOL_TILE

    def body(u_ref, w_ref, sc_ref, y_ref):
        def tile(t, c):
            r0 = pl.multiple_of(t * _POOL_TILE, _POOL_TILE)
            for g, w in enumerate(POOL_WINDOWS):
                cs = pl.ds(POOL_GROUP * g, POOL_GROUP)
                xt = u_ref[pl.ds(r0, _POOL_TILE + POOL_HALO), cs]
                ws = _window_sums(xt, True)[g][POOL_HALO:, :]
                pooled = ws / _pool_count(r0, w) - xt[POOL_HALO:, :]
                y_ref[pl.ds(r0, _POOL_TILE), cs] = _dot(pooled.astype(bf16), w_ref[g].astype(bf16)) * sc_ref[:, cs]
            return c
        lax.fori_loop(0, nt, tile, 0)

    vm = pl.BlockSpec(memory_space=pltpu.VMEM)
    return pl.pallas_call(
        body, name="pool_fwd", in_specs=[vm, vm, vm], out_specs=vm,
        out_shape=jax.ShapeDtypeStruct((S, D_POOL), f32),
    )(u_pad, pool_w, pool_scale)


def _pool_bwd(dy_pad, u_pad, pool_w, pool_scale):
    S = u_pad.shape[0] - POOL_HALO
    nt = S // _POOL_TILE
    n = _POOL_TILE + POOL_HALO

    def body(dy_ref, u_ref, w_ref, sc_ref, du_ref, dw_ref, dsc_ref):
        dw_ref[...] = jnp.zeros_like(dw_ref)
        dsc_ref[...] = jnp.zeros_like(dsc_ref)

        def tile(t, c):
            r0 = pl.multiple_of(t * _POOL_TILE, _POOL_TILE)
            for g, w in enumerate(POOL_WINDOWS):
                cs = pl.ds(POOL_GROUP * g, POOL_GROUP)
                wb = w_ref[g].astype(bf16)
                xt = u_ref[pl.ds(r0, n), cs]
                pooled = (_window_sums(xt, True)[g][POOL_HALO:, :] / _pool_count(r0, w) - xt[POOL_HALO:, :]).astype(bf16)
                dy = dy_ref[pl.ds(r0, _POOL_TILE), cs]
                dsc_ref[:, cs] += jnp.sum(dy * _dot(pooled, wb), 0, keepdims=True)
                dw_ref[g] += _dot_tn(pooled, (dy * sc_ref[:, cs]).astype(bf16))
                dyh = (dy_ref[pl.ds(r0, n), cs] * sc_ref[:, cs]).astype(bf16)
                dpl = _dot_nt(dyh, wb)
                cnt = jnp.minimum(r0 + lax.broadcasted_iota(jnp.int32, (n, POOL_GROUP), 0) + 1, w).astype(f32)
                lead = _window_sums(dpl / cnt, False)[g]
                du_ref[pl.ds(r0, _POOL_TILE), cs] = lead[:_POOL_TILE, :] - dpl[:_POOL_TILE, :]
            return c
        lax.fori_loop(0, nt, tile, 0)

    vm = pl.BlockSpec(memory_space=pltpu.VMEM)
    return pl.pallas_call(
        body, name="pool_bwd", in_specs=[vm, vm, vm, vm], out_specs=[vm, vm, vm],
        out_shape=[jax.ShapeDtypeStruct((S, D_POOL), f32), jax.ShapeDtypeStruct((4, POOL_GROUP, POOL_GROUP), f32),
                   jax.ShapeDtypeStruct((1, D_POOL), f32)],
    )(dy_pad, u_pad, pool_w, pool_scale)


def _loss_head(y, target):
    S, D = y.shape
    ts = TOK_TILE

    def body(y_ref, t_ref, loss_ref, dy_ref):
        @pl.when(pl.program_id(0) == 0)
        def _():
            loss_ref[...] = jnp.zeros_like(loss_ref)

        d = y_ref[...] - t_ref[...]
        dy_ref[...] = d * (1.0 / D)
        loss_ref[...] += 0.5 * jnp.sum(jnp.sum(d * d, -1, keepdims=True) * (1.0 / D), 0, keepdims=True)

    tok = pl.BlockSpec((ts, D), lambda i: (i, 0))
    return pl.pallas_call(
        body, name="loss_head", grid=(S // ts,),
        in_specs=[tok, tok], out_specs=[_full((1, 1)), tok],
        out_shape=[jax.ShapeDtypeStruct((1, 1), f32), jax.ShapeDtypeStruct((S, D), f32)],
        compiler_params=_cp(("arbitrary",)),
    )(y, target)


_ADA_COLS = 768


def _ada_fwd(c_all, ada_w, ada_b_cols):
    L, D, N = ada_w.shape
    B = c_all.shape[0]

    def body(c_ref, w_ref, b_ref, out_ref):
        cv = c_ref[...]
        cond = (cv * jax.nn.sigmoid(cv)).astype(bf16)
        out_ref[0] = _dot(cond, w_ref[0].astype(bf16)) + b_ref[0]

    return pl.pallas_call(
        body, name="ada_fwd", grid=(L, N // _ADA_COLS),
        in_specs=[_full((B, D)), pl.BlockSpec((1, D, _ADA_COLS), lambda l, j: (l, 0, j)),
                  pl.BlockSpec((1, 1, _ADA_COLS), lambda l, j: (l, 0, j))],
        out_specs=pl.BlockSpec((1, B, _ADA_COLS), lambda l, j: (l, 0, j)),
        out_shape=jax.ShapeDtypeStruct((L, B, N), f32),
        compiler_params=_cp(("parallel", "parallel")),
    )(c_all, ada_w, ada_b_cols)


def _ada_wgrad(c_all_t, dmod_cols):
    D, B = c_all_t.shape
    L, _, N = dmod_cols.shape

    def body(ct_ref, dm_ref, out_ref):
        cv = ct_ref[...]
        cond = cv * jax.nn.sigmoid(cv)
        acc = cond[:, 0:1] * dm_ref[0, 0:1, :]
        for b in range(1, B):
            acc = acc + cond[:, b:b + 1] * dm_ref[0, b:b + 1, :]
        out_ref[0] = acc

    return pl.pallas_call(
        body, name="ada_wgrad", grid=(L, N // _ADA_COLS),
        in_specs=[_full((D, B)), pl.BlockSpec((1, B, _ADA_COLS), lambda l, j: (l, 0, j))],
        out_specs=pl.BlockSpec((1, D, _ADA_COLS), lambda l, j: (l, 0, j)),
        out_shape=jax.ShapeDtypeStruct((L, D, N), f32),
        compiler_params=_cp(("parallel", "parallel")),
    )(c_all_t, dmod_cols)


def _adam_math(w, g, m, v):
    m = ADAM_B1 * m + (1.0 - ADAM_B1) * g
    v = ADAM_B2 * v + (1.0 - ADAM_B2) * (g * g)
    m_hat = m / (1.0 - ADAM_B1 ** ADAM_STEP)
    v_hat = v / (1.0 - ADAM_B2 ** ADAM_STEP)
    delta = -ADAM_LR * (m_hat / (jnp.sqrt(v_hat) + ADAM_EPS) + ADAM_WD * w)
    return delta, m, v


def _adamw(w, m, v, g, row_tile, row0=0, outs=None):
    R, C = w.shape
    b0 = row0 // row_tile

    def body(w_ref, m_ref, v_ref, g_ref, _0, _1, _2, _3, g_out, d_out, m_out, v_out):
        gv = g_ref[...]
        delta, mn, vn = _adam_math(w_ref[...], gv, m_ref[...], v_ref[...])
        g_out[...] = gv
        d_out[...] = delta
        m_out[...] = mn
        v_out[...] = vn

    pspec = pl.BlockSpec((row_tile, C), lambda i: (b0 + i, 0))
    gspec = pl.BlockSpec((row_tile, C), lambda i: (i, 0))
    anyspec = pl.BlockSpec(memory_space=pl.ANY)
    shp = jax.ShapeDtypeStruct((R, C), f32)
    if outs is None:
        outs = [lax.empty((R, C), f32) for _ in range(4)]
    return pl.pallas_call(
        body, name="adamw", grid=(g.shape[0] // row_tile,),
        in_specs=[pspec] * 3 + [gspec] + [anyspec] * 4, out_specs=[pspec] * 4, out_shape=[shp] * 4,
        input_output_aliases={4: 0, 5: 1, 6: 2, 7: 3},
        compiler_params=_cp(("parallel",), 40),
    )(w, m, v, g, *outs)


def _pair_sum(g5s, gots, pc):
    n = len(g5s)

    def body(pc_ref, *refs):
        for own, got, out in zip(refs[:n], refs[n:2 * n], refs[2 * n:]):
            out[0, 0] = (own[0, 0, 0].astype(f32) + got[0, 0].astype(f32)).astype(bf16)

    def half(g):
        return pl.BlockSpec((1, 1) + g.shape[-2:], lambda p, pc: (p, 0, 0, 0))

    gs = pltpu.PrefetchScalarGridSpec(
        num_scalar_prefetch=1, grid=(N_CHIPS,),
        in_specs=[pl.BlockSpec((1, 1, 1) + g.shape[-2:], lambda p, pc: (p, 0, pc[1], 0, 0)) for g in g5s]
        + [half(g) for g in gots],
        out_specs=[half(g) for g in gots],
    )
    return pl.pallas_call(
        body, name="pair_sum", grid_spec=gs, out_shape=[jax.ShapeDtypeStruct(g.shape, bf16) for g in gots],
        compiler_params=_cp(("parallel",), 48),
    )(pc, *g5s, *gots)


_SUM_STEPS = 2


def _sum_shards(hsums, recvs, pc):
    n = len(hsums)

    def body(pc_ref, *refs):
        for own, got, out in zip(refs[:n], refs[n:2 * n], refs[2 * n:]):
            acc = own[0, 0].astype(f32)
            for j in range(3):
                acc = acc + got[j, 0].astype(f32)
            out[0, 0] = acc

    def rows(h):
        return (h.shape[2] // _SUM_STEPS, h.shape[3])

    gs = pltpu.PrefetchScalarGridSpec(
        num_scalar_prefetch=1, grid=(_SUM_STEPS,),
        in_specs=[pl.BlockSpec((1, 1) + rows(h), lambda i, pc: (pc[0], 0, i, 0)) for h in hsums]
        + [pl.BlockSpec((3, 1) + rows(h), lambda i, pc: (0, 0, i, 0)) for h in hsums],
        out_specs=[pl.BlockSpec((1, 1) + rows(h), lambda i, pc: (0, pc[1], i, 0)) for h in hsums],
    )
    return pl.pallas_call(
        body, name="sum_shards", grid_spec=gs,
        out_shape=[jax.ShapeDtypeStruct((1, 2) + h.shape[2:], f32) for h in hsums],
        compiler_params=_cp(("parallel",), 48),
    )(pc, *hsums, *recvs)


def _sum8(packs):
    _, R, C = packs.shape
    tr = R // 8 if R % 64 == 0 else R

    def body(p_ref, out_ref):
        acc = p_ref[0]
        for d in range(1, 8):
            acc = acc + p_ref[d]
        out_ref[...] = acc

    return pl.pallas_call(
        body, name="sum8", grid=(R // tr,),
        in_specs=[pl.BlockSpec((8, tr, C), lambda i: (0, i, 0))],
        out_specs=pl.BlockSpec((tr, C), lambda i: (i, 0)),
        out_shape=jax.ShapeDtypeStruct((R, C), f32),
        compiler_params=_cp(("parallel",)),
    )(packs)


def _allgather8(x_shard):
    m_per, n = x_shard.shape

    def body(x_ref, out_ref, send_sems, recv_sems, local_sem):
        x, y, c = lax.axis_index("x"), lax.axis_index("y"), lax.axis_index("c")
        me, sibling = (x, y, c), (x, y, 1 - c)
        chips = [(1 - x, y), (x, 1 - y), (1 - x, 1 - y)]

        def rows(px, py, pc):
            return out_ref.at[pl.ds((4 * px + 2 * py + pc) * m_per, m_per), :]

        def copy(k, block, to, src=None):
            return pltpu.make_async_remote_copy(
                src_ref=rows(*block) if src is None else src, dst_ref=rows(*block),
                send_sem=send_sems.at[k], recv_sem=recv_sems.at[k], device_id=to, device_id_type=MESH)

        mine = pltpu.make_async_copy(x_ref, rows(*me), local_sem)
        mine.start()
        first = [copy(0, me, sibling, src=x_ref)]
        first += [copy(1 + j, me, (*chip, c), src=x_ref) for j, chip in enumerate(chips)]
        for cp in first:
            cp.start()
        passed = [copy(4 + j, (*chip, c), sibling) for j, chip in enumerate(chips)]
        for j, chip in enumerate(chips):
            copy(1 + j, (*chip, c), me).wait_recv()
            passed[j].start()
        copy(0, sibling, me).wait_recv()
        for j, chip in enumerate(chips):
            copy(4 + j, (*chip, 1 - c), me).wait_recv()
        for cp in first + passed:
            cp.wait_send()
        mine.wait()

    return pl.pallas_call(
        body, name="allgather8",
        out_shape=jax.ShapeDtypeStruct((8 * m_per, n), x_shard.dtype),
        in_specs=[pl.BlockSpec(memory_space=pltpu.VMEM)],
        out_specs=pl.BlockSpec(memory_space=pltpu.VMEM),
        scratch_shapes=[pltpu.SemaphoreType.DMA((7,)), pltpu.SemaphoreType.DMA((7,)), pltpu.SemaphoreType.DMA],
        compiler_params=_cp(None, 48),
    )(x_shard)


def _other_chips():
    x, y = lax.axis_index("x"), lax.axis_index("y")
    return [(1 - x, y), (x, 1 - y), (1 - x, 1 - y)]


_HBM = pl.BlockSpec(memory_space=pltpu.HBM)
_SEM = pl.BlockSpec(memory_space=pltpu.SEMAPHORE)
_EFFECT = pltpu.SideEffectType.DATAFLOW_SIDE_EFFECTING


def _gather_copies(srcs, lands, send_sems, recv_sems):
    x, y, c = lax.axis_index("x"), lax.axis_index("y"), lax.axis_index("c")
    return [pltpu.make_async_remote_copy(
        src_ref=srcs[a].at[:, c], dst_ref=lands[a].at[2 * x + y, :, c], send_sem=send_sems.at[3 * a + j],
        recv_sem=recv_sems.at[3 * a + j], device_id=(cx, cy, c), device_id_type=MESH)
        for a in range(len(srcs)) for j, (cx, cy) in enumerate(_other_chips())]


def _gather_start(chunks, after, name):
    sizes = [len(srcs) for srcs, _ in chunks]
    flat = [t for srcs, lands in chunks for t in list(srcs) + list(lands)]
    nflat = len(flat)
    nsem = 2 * len(chunks)

    def body(*refs):
        ins, sems, token = refs[:nflat], refs[nflat + 1:nflat + 1 + nsem], refs[-1]
        off = 0
        for k, n in enumerate(sizes):
            for cp in _gather_copies(ins[off:off + n], ins[off + n:off + 2 * n], sems[2 * k], sems[2 * k + 1]):
                cp.start()
            off += 2 * n
        token[...] = jnp.zeros_like(token)

    res = pl.pallas_call(
        body, name=name,
        out_shape=[pltpu.SemaphoreType.DMA((3 * n,)) for n in sizes for _ in range(2)]
        + [pltpu.HBM(t.shape, t.dtype) for t in flat] + [jax.ShapeDtypeStruct((8, 128), f32)],
        in_specs=[_HBM] * nflat + [pl.BlockSpec(memory_space=pl.ANY)],
        out_specs=[_SEM] * nsem + [_HBM] * nflat + [pl.BlockSpec(memory_space=pltpu.VMEM)],
        input_output_aliases={i: nsem + i for i in range(nflat)},
        compiler_params=pltpu.CompilerParams(has_side_effects=_EFFECT),
    )(*[pltpu.with_memory_space_constraint(t, pltpu.HBM) for t in flat], after)
    out, off = [], nsem
    for k, n in enumerate(sizes):
        out.append((res[2 * k], res[2 * k + 1], res[off:off + n], res[off + n:off + 2 * n]))
        off += 2 * n
    return out, res[-1]


def _gather_wait(send_sems, recv_sems, srcs, lands, after, name):
    n = len(srcs)

    def body(*refs):
        for cp in _gather_copies(refs[:n], refs[n:2 * n], refs[2 * n], refs[2 * n + 1]):
            cp.wait_send()
            cp.wait_recv()

    res = pl.pallas_call(
        body, name=name,
        out_shape=[pltpu.HBM(t.shape, t.dtype) for t in list(srcs) + list(lands)],
        in_specs=[_HBM] * (2 * n) + [_SEM, _SEM] + [pl.BlockSpec(memory_space=pl.ANY)] * len(after),
        out_specs=[_HBM] * (2 * n),
        input_output_aliases={i: i for i in range(2 * n)},
        compiler_params=pltpu.CompilerParams(has_side_effects=_EFFECT),
    )(*srcs, *lands, send_sems, recv_sems, *after)
    return res[:n], res[n:]


def _gather_forward(srcs, lands):
    n = len(lands)

    def body(*refs):
        own, outs = refs[:n], refs[2 * n:3 * n]
        send_sems, recv_sems, local_sems = refs[3 * n:]
        x, y, c = lax.axis_index("x"), lax.axis_index("y"), lax.axis_index("c")
        sibling = (x, y, 1 - c)
        local = [pltpu.make_async_copy(own[a], outs[a].at[2 * x + y], local_sems.at[a]) for a in range(n)]
        for cp in local:
            cp.start()
        copies = []
        for a in range(n):
            for j, (cx, cy) in enumerate(_other_chips()):
                mine = outs[a].at[2 * cx + cy, :, c]
                cp = pltpu.make_async_remote_copy(src_ref=mine, dst_ref=mine, send_sem=send_sems.at[3 * a + j],
                                                  recv_sem=recv_sems.at[3 * a + j], device_id=sibling, device_id_type=MESH)
                cp.start()
                copies.append((cp, a, j, cx, cy))
        for cp, a, j, cx, cy in copies:
            cp.wait_send()
            theirs = outs[a].at[2 * cx + cy, :, 1 - c]
            pltpu.make_async_remote_copy(src_ref=theirs, dst_ref=theirs, send_sem=send_sems.at[3 * a + j],
                                         recv_sem=recv_sems.at[3 * a + j], device_id=sibling, device_id_type=MESH).wait_recv()
        for cp in local:
            cp.wait()

    hbm = pl.BlockSpec(memory_space=pl.ANY)
    return pl.pallas_call(
        body, name="gather_forward",
        out_shape=[jax.ShapeDtypeStruct(t.shape, t.dtype) for t in lands],
        in_specs=[hbm] * (2 * n), out_specs=[hbm] * n,
        input_output_aliases={n + a: a for a in range(n)},
        scratch_shapes=[pltpu.SemaphoreType.DMA((3 * n,)), pltpu.SemaphoreType.DMA((3 * n,)), pltpu.SemaphoreType.DMA((n,))],
    )(*srcs, *lands)


def _pair_exchange(g5s):
    n = len(g5s)

    def body(*refs):
        ins, outs = refs[:n], refs[n:2 * n]
        send_sems, recv_sems = refs[2 * n:]
        c = lax.axis_index("c")
        sibling = (lax.axis_index("x"), lax.axis_index("y"), 1 - c)
        copies = []
        for a in range(n):
            cp = pltpu.make_async_remote_copy(src_ref=ins[a].at[:, :, 1 - c], dst_ref=outs[a], send_sem=send_sems.at[a],
                                              recv_sem=recv_sems.at[a], device_id=sibling, device_id_type=MESH)
            cp.start()
            copies.append(cp)
        for cp in copies:
            cp.wait()

    hbm = pl.BlockSpec(memory_space=pl.ANY)
    return pl.pallas_call(
        body, name="pair_exchange",
        out_shape=[jax.ShapeDtypeStruct(g.shape[:2] + g.shape[3:], g.dtype) for g in g5s],
        in_specs=[hbm] * n, out_specs=[hbm] * n,
        scratch_shapes=[pltpu.SemaphoreType.DMA((n,)), pltpu.SemaphoreType.DMA((n,))],
    )(*g5s)


def _scatter_copies(srcs, lands, send_sems, recv_sems):
    c = lax.axis_index("c")
    return [pltpu.make_async_remote_copy(
        src_ref=srcs[a].at[2 * cx + cy], dst_ref=lands[a].at[j], send_sem=send_sems.at[3 * a + j],
        recv_sem=recv_sems.at[3 * a + j], device_id=(cx, cy, c), device_id_type=MESH)
        for a in range(len(srcs)) for j, (cx, cy) in enumerate(_other_chips())]


def _scatter_start(hsums, name, after=()):
    n = len(hsums)
    na = len(after)

    def body(*refs):
        srcs, lands = refs[:n], refs[n:2 * n]
        send_sems, recv_sems = refs[2 * n + na], refs[2 * n + na + 1]
        for cp in _scatter_copies(srcs, lands, send_sems, recv_sems):
            cp.start()
        refs[-1][...] = jnp.zeros_like(refs[-1])

    lands = [lax.empty((3,) + g.shape[1:], g.dtype) for g in hsums]
    res = pl.pallas_call(
        body, name=name,
        out_shape=[pltpu.SemaphoreType.DMA((3 * n,)), pltpu.SemaphoreType.DMA((3 * n,))]
        + [pltpu.HBM(g.shape, g.dtype) for g in hsums] + [pltpu.HBM(g.shape, g.dtype) for g in lands]
        + [jax.ShapeDtypeStruct((8, 128), f32)],
        in_specs=[_HBM] * (2 * n) + [pl.BlockSpec(memory_space=pl.ANY)] * na,
        out_specs=[_SEM, _SEM] + [_HBM] * (2 * n) + [pl.BlockSpec(memory_space=pltpu.VMEM)],
        input_output_aliases={i: i + 2 for i in range(2 * n)},
        compiler_params=pltpu.CompilerParams(has_side_effects=_EFFECT),
    )(*[pltpu.with_memory_space_constraint(t, pltpu.HBM) for t in list(hsums) + lands], *after)
    return (res[0], res[1], res[2:2 + n], res[2 + n:2 + 2 * n]), res[-1]


def _scatter_wait(send_sems, recv_sems, srcs, lands, after, name):
    n = len(srcs)
    extra = list(after)

    def body(*refs):
        s_refs, l_refs = refs[:n], refs[n:2 * n]
        ss, rs = refs[2 * n], refs[2 * n + 1]
        for cp in _scatter_copies(s_refs, l_refs, ss, rs):
            cp.wait_send()
            cp.wait_recv()

    res = pl.pallas_call(
        body, name=name,
        out_shape=[pltpu.HBM(g.shape, g.dtype) for g in srcs] + [pltpu.HBM(g.shape, g.dtype) for g in lands],
        in_specs=[_HBM] * (2 * n) + [_SEM, _SEM] + [pl.BlockSpec(memory_space=pl.ANY)] * len(extra),
        out_specs=[_HBM] * (2 * n),
        input_output_aliases={i: i for i in range(2 * n)},
        compiler_params=pltpu.CompilerParams(has_side_effects=_EFFECT),
    )(*srcs, *lands, send_sems, recv_sems, *extra)
    return res[:n], res[n:]


def _swap_halves(fulls):
    n = len(fulls)

    def body(*refs):
        ins, outs = refs[:n], refs[n:2 * n]
        send_sems, recv_sems = refs[2 * n:]
        c = lax.axis_index("c")
        sibling = (lax.axis_index("x"), lax.axis_index("y"), 1 - c)
        copies = []
        for a in range(n):
            cp = pltpu.make_async_remote_copy(src_ref=outs[a].at[:, c], dst_ref=outs[a].at[:, c], send_sem=send_sems.at[a],
                                              recv_sem=recv_sems.at[a], device_id=sibling, device_id_type=MESH)
            cp.start()
            copies.append(cp)
        for a, cp in enumerate(copies):
            cp.wait_send()
            theirs = outs[a].at[:, 1 - c]
            pltpu.make_async_remote_copy(src_ref=theirs, dst_ref=theirs, send_sem=send_sems.at[a], recv_sem=recv_sems.at[a],
                                         device_id=sibling, device_id_type=MESH).wait_recv()

    hbm = pl.BlockSpec(memory_space=pl.ANY)
    return pl.pallas_call(
        body, name="swap_halves",
        out_shape=[jax.ShapeDtypeStruct(p.shape, p.dtype) for p in fulls],
        in_specs=[hbm] * n, out_specs=[hbm] * n,
        input_output_aliases={a: a for a in range(n)},
        scratch_shapes=[pltpu.SemaphoreType.DMA((n,)), pltpu.SemaphoreType.DMA((n,))],
    )(*fulls)


def _to_segments(t):
    s, c = t.shape
    return t.reshape(SCAN_SEG, s // SCAN_SEG, c).transpose(1, 0, 2).reshape(s, c)


def _from_segments(t):
    s, c = t.shape
    return t.reshape(s // SCAN_SEG, SCAN_SEG, c).transpose(1, 0, 2).reshape(s, c)


def _ssm_operators(a_re, a_im, log_dt, b_re, b_im, c_re, c_im):
    lam = lax.complex(a_re, a_im)
    dt = jnp.exp(log_dt)[:, None]
    a_bar = jnp.exp(lam * dt)
    b_bar = ((a_bar - 1.0) / lam)[:, :, None] * lax.complex(b_re, b_im)
    eye = jnp.eye(N_GROUPS, dtype=f32)

    def embed_b(t):
        return (jnp.transpose(t, (0, 2, 1))[:, :, None, :] * eye[:, None, :, None]).reshape(D_SSM, D_STATE)

    def embed_c(t):
        return (jnp.transpose(t, (0, 2, 1))[:, :, None, :] * eye[:, None, :, None]).reshape(D_STATE, D_SSM)

    a2 = jnp.stack([a_bar.real.reshape(D_STATE), a_bar.imag.reshape(D_STATE)])
    return a2, embed_b(b_bar.real), embed_b(b_bar.imag), embed_c(c_re), embed_c(c_im)


def _local_step(x, target, mod, small, ffn_weights, mix_weights, grads_done):
    table = jnp.asarray(_bucket_table())
    bias = _bias_fwd(small["rel_bias"], table)
    L = DEPTH
    saved = []
    ssm_ops = []
    for l in range(L):
        sv = {}
        m9 = mod[l]
        sv["x0"] = x
        sv["w0"] = ffn_weights(l, 0, x)
        x, sv["f0"], sv["g0"], sv["u0"], sv["h0"] = _ffn_fwd(x, m9[0:3], *sv["w0"], 0, small["ln_g"][l, 0:1], small["ln_b"][l, 0:1])
        sv["x1"] = x
        sv["w1"] = mix_weights(l, x)
        *qkv, z_rest, sv["h1"] = _mix_in_fwd(x, m9[3:6], sv["w1"][0], 0)
        S = x.shape[0]
        qkv = [t.reshape(3, S, D_ATT) for t in qkv]
        att = [_att_fwd(qkv[b], bias, b) for b in range(3)]
        y_att, lse3 = _att_merge([att[b][0].reshape(d, S // d, D_ATT) for b, d in enumerate(DILATIONS)],
                                 [att[b][1].reshape(d, S // d, _LANES) for b, d in enumerate(DILATIONS)])
        sv.update(qkv=qkv, lse=[a[1] for a in att], lse3=lse3, y_att=y_att)

        prm = tuple(small[k][l] for k in ("ssm_a_re", "ssm_a_im", "ssm_log_dt", "ssm_b_re", "ssm_b_im", "ssm_c_re", "ssm_c_im"))
        (a2, bre, bim, cre, cim), ops_vjp = jax.vjp(_ssm_operators, *prm)
        ssm_ops.append(ops_vjp)
        u_ssm = _to_segments(z_rest[:, :D_SSM])
        sr, si = _ssm_states(u_ssm, bre, bim, a2)
        dskip = small["ssm_d"][l][None, :]
        glu_b = small["glu_b"][l][None, :]
        out_seg, y_seg = _ssm_out(sr, si, u_ssm, cre, cim, dskip, small["glu_w"][l], glu_b)
        y_ssm = _from_segments(out_seg)
        sv.update(a2=a2, bre=bre, bim=bim, cre=cre, cim=cim, u_ssm=u_ssm, sr=sr, si=si, y_seg=y_seg, y_ssm=y_ssm)

        u_pool = jnp.concatenate([jnp.zeros((POOL_HALO, D_POOL), f32), z_rest[:, D_SSM:]])
        y_pool = _pool_fwd(u_pool, small["pool_w"][l], small["pool_scale"][l][None, :])
        sv.update(u_pool=u_pool, y_pool=y_pool)

        x, sv["ymix"] = _mix_out_fwd(x, y_att, y_ssm, y_pool, m9[3:6], sv["w1"][1], 0, small["ln_g"][l, 1:2], small["ln_b"][l, 1:2])
        sv["x2"] = x
        sv["w2"] = ffn_weights(l, 1, x)
        x, sv["f2"], sv["g2"], sv["u2"], sv["h2"] = _ffn_fwd(x, m9[6:9], *sv["w2"], 0, small["ln_g"][l, 2:3], small["ln_b"][l, 2:3])
        saved.append(sv)

    loss, dx = _loss_head(x, target)

    dmod = [None] * L
    dln_g = [None] * L
    dln_b = [None] * L
    sg = {k: [None] * L for k in ("ssm_a_re", "ssm_a_im", "ssm_log_dt", "ssm_b_re", "ssm_b_im", "ssm_c_re", "ssm_c_im",
                                  "ssm_d", "glu_w", "glu_b", "pool_w", "pool_scale")}
    dbias_tot = None
    order_after = jnp.zeros((), f32)
    for l in reversed(range(L)):
        sv = saved[l]
        m9 = mod[l] + order_after

        def fresh(like):
            return [lax.empty(t.shape, bf16) for t in like]

        dx, dg, du, a, df, dm2, dlg2, dlb2 = _ffn_bwd(dx, sv["x2"], sv["f2"], sv["g2"], sv["u2"], m9[6:9], *sv["w2"], 0,
                                                     small["ln_g"][l, 2:3])
        g_ffn1 = _ffn_wgrad(sv["h2"], dg, du, a, df, *fresh(sv["w2"]), 0)
        dxr, d_att, d_ssm, d_pool, dgate1, dlg1, dlb1, g_w_out = _mix_out_bwd(
            dx, sv["x1"], sv["ymix"], sv["y_att"], sv["y_ssm"], sv["y_pool"], m9[3:6], sv["w1"][1], 0, small["ln_g"][l, 1:2],
            fresh(sv["w1"])[1])
        S = d_att.shape[0]
        merged = _att_merge_bwd(d_att, sv["y_att"], sv["lse3"])
        dqkv, dbias = [], []
        for b, d in enumerate(DILATIONS):
            dq_b, db_b = _att_bwd(sv["qkv"][b], merged[b].reshape(S, D_ATT), sv["lse"][b], merged[3 + b].reshape(S, _LANES), bias, b)
            dqkv.append(dq_b.reshape(3, d, S // d, D_ATT))
            dbias.append(db_b)
        dbias = jnp.stack(dbias)
        dbias_tot = dbias if dbias_tot is None else dbias_tot + dbias
        d_seg = _to_segments(d_ssm)
        dskip = small["ssm_d"][l][None, :]
        glu_b = small["glu_b"][l][None, :]
        dy_seg, du_skip, dcre, dcim, dd, dglu_b, dglu_w = _ssm_out_bwd(
            d_seg, sv["y_seg"], sv["u_ssm"], sv["sr"], sv["si"], dskip, small["glu_w"][l], glu_b)
        du_seg, dbre, dbim, da2 = _ssm_states_bwd(dy_seg, du_skip, sv["u_ssm"], sv["sr"], sv["si"], sv["cre"], sv["cim"],
                                                  sv["bre"], sv["bim"], sv["a2"])
        d_prm = ssm_ops[l]((da2, dbre, dbim, dcre, dcim))
        for k, v in zip(("ssm_a_re", "ssm_a_im", "ssm_log_dt", "ssm_b_re", "ssm_b_im", "ssm_c_re", "ssm_c_im"), d_prm):
            sg[k][l] = v
        sg["ssm_d"][l] = dd[0]
        sg["glu_b"][l] = dglu_b[0]
        sg["glu_w"][l] = dglu_w
        du_ssm = _from_segments(du_seg)
        dyp = jnp.concatenate([d_pool, jnp.zeros((POOL_HALO, D_POOL), f32)])
        du_pool, dpw, dps = _pool_bwd(dyp, sv["u_pool"], small["pool_w"][l], small["pool_scale"][l][None, :])
        sg["pool_w"][l] = dpw
        sg["pool_scale"][l] = dps[0]
        d_rest = jnp.concatenate([du_ssm, du_pool], axis=1).astype(bf16)
        dx, dm1, dz = _mix_in_bwd(dqkv, d_rest, dxr, sv["x1"], m9[3:6], sv["w1"][0], 0)
        g_w_in = _mix_in_wgrad(sv["h1"], dz, fresh(sv["w1"])[0], 0)
        ffn_names = ("ffn_w_gate", "ffn_w_up", "ffn_w_down")
        m9 = m9 + grads_done(l, 1, list(zip(ffn_names, [(2 * l + 1) * FF_SHARD] * 3, g_ffn1))
                             + [("w_in", l * D_MODEL, g_w_in), ("w_out", l * 256, g_w_out)])
        dm1 = jnp.concatenate([dm1[0:2], dgate1])
        dx, dg, du, a, df, dm0, dlg0, dlb0 = _ffn_bwd(dx, sv["x0"], sv["f0"], sv["g0"], sv["u0"], m9[0:3], *sv["w0"], 0,
                                                     small["ln_g"][l, 0:1])
        g_ffn0 = _ffn_wgrad(sv["h0"], dg, du, a, df, *fresh(sv["w0"]), 0)
        order_after = grads_done(l, 0, list(zip(ffn_names, [2 * l * FF_SHARD] * 3, g_ffn0)))
        dmod[l] = jnp.concatenate([dm0, dm1, dm2])
        dln_g[l] = jnp.concatenate([dlg0, dlg1, dlg2])
        dln_b[l] = jnp.concatenate([dlb0, dlb1, dlb2])

    small_grads = {k: jnp.stack(v) for k, v in sg.items()}
    small_grads["rel_bias"] = _bias_bwd(dbias_tot, table)
    small_grads["ln_g"] = jnp.stack(dln_g)
    small_grads["ln_b"] = jnp.stack(dln_b)
    return loss, dx, jnp.stack(dmod), small_grads


_TILE_ELEMS = 8 * 128


def _pack_rows(shapes):
    out, row = [], 0
    for s in shapes:
        nr = -(-int(np.prod(s)) // _TILE_ELEMS) * 8
        out.append((row, nr))
        row += nr
    return out


def _pack(arrs):
    parts = []
    for a in arrs:
        flat = a.reshape(-1).astype(f32)
        npad = -(-flat.shape[0] // _TILE_ELEMS) * _TILE_ELEMS
        parts.append(jnp.pad(flat, (0, npad - flat.shape[0])).reshape(npad // 128, 128))
    return jnp.concatenate(parts, axis=0)


def _unpack(buf, shapes):
    return [buf[row:row + nr].reshape(-1)[:int(np.prod(s))].reshape(s) for s, (row, nr) in zip(shapes, _pack_rows(shapes))]


_REPL = ("rel_bias", "ada_b", "ssm_a_re", "ssm_a_im", "ssm_log_dt", "ssm_b_re", "ssm_b_im", "ssm_c_re", "ssm_c_im",
         "ssm_d", "glu_b", "pool_w", "pool_scale")
_SMALL_SHARDED = ("ln_g", "ln_b", "glu_w")
_BIG = ("ffn_w_gate", "ffn_w_up", "ffn_w_down", "w_in", "w_out")
_ORDER = ("rel_bias", "ada_w", "ada_b", "ln_g", "ln_b", "ffn_w_gate", "ffn_w_up", "ffn_w_down", "w_in", "w_out",
          "ssm_a_re", "ssm_a_im", "ssm_log_dt", "ssm_b_re", "ssm_b_im", "ssm_c_re", "ssm_c_im", "ssm_d", "glu_w",
          "glu_b", "pool_w", "pool_scale")


def kernel(x, c, rel_bias, ada_w, ada_b, ln_g, ln_b, ffn_w_gate, ffn_w_up, ffn_w_down, w_in, w_out, ssm_a_re, ssm_a_im, ssm_log_dt, ssm_b_re, ssm_b_im, ssm_c_re, ssm_c_im, ssm_d, glu_w, glu_b, pool_w, pool_scale, loss_target, m_rel_bias, m_ada_w, m_ada_b, m_ln_g, m_ln_b, m_ffn_w_gate, m_ffn_w_up, m_ffn_w_down, m_w_in, m_w_out, m_ssm_a_re, m_ssm_a_im, m_ssm_log_dt, m_ssm_b_re, m_ssm_b_im, m_ssm_c_re, m_ssm_c_im, m_ssm_d, m_glu_w, m_glu_b, m_pool_w, m_pool_scale, v_rel_bias, v_ada_w, v_ada_b, v_ln_g, v_ln_b, v_ffn_w_gate, v_ffn_w_up, v_ffn_w_down, v_w_in, v_w_out, v_ssm_a_re, v_ssm_a_im, v_ssm_log_dt, v_ssm_b_re, v_ssm_b_im, v_ssm_c_re, v_ssm_c_im, v_ssm_d, v_glu_w, v_glu_b, v_pool_w, v_pool_scale):
    args = dict(locals())
    w = {k: args[k] for k in _ORDER}
    m = {k: args["m_" + k] for k in _ORDER}
    v = {k: args["v_" + k] for k in _ORDER}
    L, D = DEPTH, D_MODEL
    ax, ay, ac = lax.axis_index("x"), lax.axis_index("y"), lax.axis_index("c")
    p_me = 2 * ax + ay
    dev = 4 * ax + 2 * ay + ac

    transposed = ("ffn_w_gate", "ffn_w_up")
    for d in (w, m, v):
        for name in transposed:
            d[name] = jnp.swapaxes(d[name], 2, 3)

    def halves(t):
        return t.astype(bf16).reshape(1, 2, t.shape[0] // 2, t.shape[1])

    def landing(src):
        return lax.empty((N_CHIPS,) + src.shape, bf16)

    chunk_keys = [("ffn", 0, 0), ("mix", 0), ("ffn", 0, 1), ("ffn", 1, 0), ("mix", 1), ("ffn", 1, 1)]
    chunk_srcs = []
    for key in chunk_keys:
        if key[0] == "ffn":
            chunk_srcs.append([halves(w[name][key[1], key[2]]) for name in ("ffn_w_gate", "ffn_w_up", "ffn_w_down")])
        else:
            chunk_srcs.append([halves(w_in[key[1]]), halves(w_out[key[1]])])

    pack = _pack([c, ln_g, ln_b, glu_w])
    rows = pack.shape[0]
    allp = _allgather8(pack).reshape(8, rows, 128)
    chunks = [(srcs, [landing(t) for t in srcs]) for srcs in chunk_srcs]
    first_in_flight, first_begun = _gather_start(chunks[:1], allp, "gather_start_first")
    c_all = allp[:, :8].reshape(8, D) + first_begun[0, 0]
    by_chip = allp[0::2]

    fwd_rows = _pack_rows([c.shape, ln_g.shape, ln_b.shape, glu_w.shape])

    def sharded(part, shape, axis):
        row0, nrows = fwd_rows[part]
        t = by_chip[:, row0:row0 + nrows].reshape(N_CHIPS, -1)[:, :int(np.prod(shape))].reshape((N_CHIPS,) + shape)
        return jnp.concatenate([t[p] for p in range(N_CHIPS)], axis=axis)

    ln_g_full = sharded(1, ln_g.shape, 2)
    ln_b_full = sharded(2, ln_b.shape, 2)
    glu_w_full = sharded(3, glu_w.shape, 1)

    ncol = ada_w.shape[-1]
    ada_b_cols = lax.dynamic_slice_in_dim(ada_b, p_me * ncol, ncol, axis=1)[:, None, :]
    mod_part = _ada_fwd(c_all, ada_w, ada_b_cols)
    mrows = L * 8 * ncol // 128
    mod_all = _allgather8(mod_part.reshape(mrows, 128)).reshape(8, L, 8, ncol)
    mod_mine = lax.dynamic_index_in_dim(mod_all, dev, axis=2, keepdims=False)
    mod = jnp.concatenate([mod_mine[2 * p] for p in range(N_CHIPS)], axis=-1).reshape(L, 9, D)

    rest_in_flight, rest_begun = _gather_start(chunks[1:], mod, "gather_start_rest")
    in_flight = first_in_flight + rest_in_flight

    def gathered(key, after):
        k = chunk_keys.index(key)
        lands = _gather_forward(*_gather_wait(*in_flight[k], [after, rest_begun], "gather_wait_%d" % k))
        return [t.reshape(N_CHIPS, 1, 2 * t.shape[3], t.shape[4]) for t in lands]

    pc = jnp.stack([p_me, ac]).astype(jnp.int32)
    groups = {}
    scattering = {}

    def start_group(tag, after=()):
        g5 = [g.reshape(g.shape[:2] + (2, g.shape[2] // 2, g.shape[3])) for _, _, g in groups[tag]]
        hsum = _pair_sum(g5, _pair_exchange(g5), pc)
        scattering[tag], begun = _scatter_start(hsum, "scatter_start_%s" % tag, after)
        return begun

    def grads_done(l, s, grads):
        if l == 1:
            groups.setdefault("l1", []).extend(grads)
            return start_group("l1")[0, 0] if s == 0 else jnp.zeros((), f32)
        groups["l0a" if s == 1 else "l0b"] = grads
        return start_group("l0a")[0, 0] if s == 1 else jnp.zeros((), f32)

    small = {k: w[k] for k in _REPL if k != "ada_b"}
    small.update(ln_g=ln_g_full, ln_b=ln_b_full, glu_w=glu_w_full)
    loss_dev, grad_x, dmod, sgrads = _local_step(
        x[0], loss_target[0], mod, small, lambda l, s, after: gathered(("ffn", l, s), after),
        lambda l, after: gathered(("mix", l), after), grads_done)
    loss = lax.psum(loss_dev[0, 0], ("x", "y", "c"))

    names = ("rel_bias", "ln_g", "ln_b", "ssm_a_re", "ssm_a_im", "ssm_log_dt", "ssm_b_re", "ssm_b_im", "ssm_c_re",
             "ssm_c_im", "ssm_d", "glu_w", "glu_b", "pool_w", "pool_scale")
    gpack = _pack([dmod] + [sgrads[k] for k in names])
    grows = gpack.shape[0]
    gall = _allgather8(gpack).reshape(8, grows, 128)
    l0b_begun = start_group("l0b", (gall,))
    gsum = _unpack(_sum8(gall), [(L, 9 * D)] + [sgrads[k].shape for k in names])
    red = dict(zip(("ada_b",) + names, gsum))
    red["ln_g"] = lax.dynamic_slice_in_dim(red["ln_g"], p_me * 256, 256, axis=2)
    red["ln_b"] = lax.dynamic_slice_in_dim(red["ln_b"], p_me * 256, 256, axis=2)
    red["glu_w"] = lax.dynamic_slice_in_dim(red["glu_w"], p_me * 64, 64, axis=1)

    dmod_all = gall[:, :L * 9 * D // 128].reshape(8, L, 9 * D)
    dmod_cols = jnp.transpose(lax.dynamic_slice_in_dim(dmod_all, p_me * ncol, ncol, axis=2), (1, 0, 2))
    g_ada_w = _ada_wgrad(jnp.transpose(c_all), dmod_cols)

    out_g, out_d, out_m, out_v = {}, {}, {}, {}
    r2 = (L * D, ncol)
    res = _adamw(ada_w.reshape(r2), m["ada_w"].reshape(r2), v["ada_w"].reshape(r2), g_ada_w.reshape(r2), 128)
    out_g["ada_w"], out_d["ada_w"], out_m["ada_w"], out_v["ada_w"] = [t.reshape(ada_w.shape) for t in res]

    small_names = _REPL + _SMALL_SHARDED
    wp = _pack([w[k] for k in small_names])
    res_small = _adamw(wp, _pack([m[k] for k in small_names]), _pack([v[k] for k in small_names]),
                       _pack([red[k] for k in small_names]), wp.shape[0])
    for t, dst in zip(res_small, (out_g, out_d, out_m, out_v)):
        for k, a in zip(small_names, _unpack(t, [w[k].shape for k in small_names])):
            dst[k] = a

    row_tile = dict(zip(_BIG, (352, 352, 352, 256, 256)))
    big = {name: None for name in _BIG}
    after = [grad_x, res_small[1], res[1], l0b_begun]
    for tag in ("l1", "l0a", "l0b"):
        hsum, recv = _scatter_wait(*scattering[tag], after, "scatter_wait_%s" % tag)
        full = _swap_halves(_sum_shards(hsum, recv, pc))
        for (name, row0, _), g in zip(groups[tag], full):
            shp = w[name].shape
            r2 = (int(np.prod(shp[:-1])), shp[-1])
            big[name] = _adamw(w[name].reshape(r2), m[name].reshape(r2), v[name].reshape(r2), g.reshape(-1, shp[-1]),
                               row_tile[name], row0, big[name])
        after = [big[name][1] for name, _, _ in groups[tag]]
    for name in _BIG:
        res = [t.reshape(w[name].shape) for t in big[name]]
        out_g[name], out_d[name], out_m[name], out_v[name] = [jnp.swapaxes(t, 2, 3) for t in res] if name in transposed else res

    return (loss, grad_x[None], *[out_g[k] for k in _ORDER], *[out_d[k] for k in _ORDER],
            *[out_m[k] for k in _ORDER], *[out_v[k] for k in _ORDER])
```

```python
import functools
import math

import numpy as np
import jax
import jax.numpy as jnp
from jax import lax
from jax.experimental import pallas as pl
from jax.experimental.pallas import tpu as pltpu

f32 = jnp.float32
bf16 = jnp.bfloat16
MESH = pl.DeviceIdType.MESH

D_MODEL = 1024
SEQ = 2048
DEPTH = 2
HEAD_DIM = 64
N_HEADS = 8
D_ATT = 512
DILATIONS = (1, 4, 16)
BLOCKS_PER_RESIDUE = (16, 4, 1)
ATT_BLOCK = 128
N_UNITS = SEQ // ATT_BLOCK
N_GROUPS = 16
SSM_GROUP = 16
SSM_STATE = 64
D_SSM = 256
D_STATE = N_GROUPS * SSM_STATE
POOL_WINDOWS = (2, 4, 8, 16)
POOL_GROUP = 64
D_POOL = 256
POOL_HALO = 16
D_FF = 2816
N_BUCKETS = 32
MAX_DISTANCE = 2048
ALPHA = (2 * DEPTH) ** 0.25
FFN_RES = 0.5
LN_EPS = 1e-5
NEG = -1e30
N_CHIPS = 4
FF_SHARD = D_FF // N_CHIPS
SCAN_SEG = 8
SCAN_STEPS = SEQ // SCAN_SEG

ADAM_LR, ADAM_B1, ADAM_B2, ADAM_EPS, ADAM_WD, ADAM_STEP = 0.001, 0.9, 0.999, 1e-08, 0.01, 10

TOK_TILE = 512


def _cp(dims=None, vmem_mb=None):
    kw = {}
    if dims is not None:
        kw["dimension_semantics"] = dims
    if vmem_mb is not None:
        kw["vmem_limit_bytes"] = vmem_mb << 20
    return pltpu.CompilerParams(**kw)


def _dot(a, b):
    return jnp.dot(a, b, preferred_element_type=f32)


def _dot_nt(a, b):
    return lax.dot_general(a, b, (((1,), (1,)), ((), ())), preferred_element_type=f32)


def _dot_tn(a, b):
    return lax.dot_general(a, b, (((0,), (0,)), ((), ())), preferred_element_type=f32)


def _ln_stats(v):
    mu = jnp.mean(v, -1, keepdims=True)
    d = v - mu
    var = jnp.mean(d * d, -1, keepdims=True)
    rstd = lax.rsqrt(var + LN_EPS)
    return d * rstd, rstd


def _ln_bwd(dxh, xh, rstd):
    return rstd * (dxh - jnp.mean(dxh, -1, keepdims=True) - xh * jnp.mean(dxh * xh, -1, keepdims=True))


_GELU_C = math.sqrt(2.0 / math.pi)


def _gelu(y):
    return 0.5 * y * (1.0 + jnp.tanh(_GELU_C * (y + 0.044715 * y * y * y)))


def _gelu_grad(y):
    t = jnp.tanh(_GELU_C * (y + 0.044715 * y * y * y))
    return 0.5 * (1.0 + t) + 0.5 * y * (1.0 - t * t) * (_GELU_C * (1.0 + 3 * 0.044715 * y * y))


def _full(shape):
    return pl.BlockSpec(shape, lambda *_: (0,) * len(shape))


def _ffn_fwd(x, mod3, wg, wu, wd, ls, lng, lnb):
    S, D = x.shape
    Fs = wg.shape[-2]
    ts = TOK_TILE

    def body(x_ref, mod_ref, wg_ref, wu_ref, wd_ref, lng_ref, lnb_ref, xo_ref, f_ref, g_ref, u_ref, h_ref, acc_sc):
        j = pl.program_id(1)

        @pl.when(j == 0)
        def _():
            xh, _ = _ln_stats(x_ref[...])
            h_ref[...] = (xh * (1.0 + mod_ref[1:2, :]) + mod_ref[0:1, :]).astype(bf16)
            acc_sc[...] = jnp.zeros_like(acc_sc)

        h = h_ref[...]
        g = _dot_nt(h, wg_ref[0, 0])
        u = _dot_nt(h, wu_ref[0, 0])
        g_ref[0] = g.astype(bf16)
        u_ref[0] = u.astype(bf16)
        a = (g * jax.nn.sigmoid(g) * u).astype(bf16)
        acc_sc[...] += _dot(a, wd_ref[0, 0])

        @pl.when(j == N_CHIPS - 1)
        def _():
            f = acc_sc[...]
            f_ref[...] = f
            r = ALPHA * x_ref[...] + (FFN_RES * mod_ref[2:3, :]) * f
            rh, _ = _ln_stats(r)
            xo_ref[...] = rh * lng_ref[...] + lnb_ref[...]

    tok = pl.BlockSpec((ts, D), lambda i, j: (i, 0))
    wrow = pl.BlockSpec((1, 1, Fs, D), lambda i, j: (j, ls, 0, 0))
    hid = pl.BlockSpec((1, ts, Fs), lambda i, j: (j, i, 0))
    return pl.pallas_call(
        body, name="ffn_fwd", grid=(S // ts, N_CHIPS),
        in_specs=[tok, _full((3, D)), wrow, wrow, wrow, _full((1, D)), _full((1, D))],
        out_specs=[tok, tok, hid, hid, tok],
        out_shape=[jax.ShapeDtypeStruct((S, D), f32), jax.ShapeDtypeStruct((S, D), f32),
                   jax.ShapeDtypeStruct((N_CHIPS, S, Fs), bf16), jax.ShapeDtypeStruct((N_CHIPS, S, Fs), bf16),
                   jax.ShapeDtypeStruct((S, D), bf16)],
        scratch_shapes=[pltpu.VMEM((ts, D), f32)],
        compiler_params=_cp(("parallel", "arbitrary"), 56),
    )(x, mod3, wg, wu, wd, lng, lnb)


def _ffn_bwd(dxo, x, f, g, u, mod3, wg, wu, wd, ls, lng):
    S, D = x.shape
    Fs = wg.shape[-2]
    ts = TOK_TILE

    def body(dxo_ref, x_ref, f_ref, g_ref, u_ref, mod_ref, wg_ref, wu_ref, wd_ref, lng_ref,
             dx_ref, dg_ref, du_ref, a_ref, df_ref, dmod_ref, dlng_ref, dlnb_ref,
             dr_sc, df_sc, acc_sc):
        i = pl.program_id(0)
        j = pl.program_id(1)

        @pl.when((i == 0) & (j == 0))
        def _():
            dmod_ref[...] = jnp.zeros_like(dmod_ref)
            dlng_ref[...] = jnp.zeros_like(dlng_ref)
            dlnb_ref[...] = jnp.zeros_like(dlnb_ref)

        @pl.when(j == 0)
        def _():
            xv = x_ref[...]
            fv = f_ref[...]
            gate = mod_ref[2:3, :]
            rh, rstd = _ln_stats(ALPHA * xv + (FFN_RES * gate) * fv)
            dy = dxo_ref[...]
            dlng_ref[...] += jnp.sum(dy * rh, 0, keepdims=True)
            dlnb_ref[...] += jnp.sum(dy, 0, keepdims=True)
            dr = _ln_bwd(dy * lng_ref[...], rh, rstd)
            dr_sc[...] = dr
            dmod_ref[2:3, :] += jnp.sum(FFN_RES * dr * fv, 0, keepdims=True)
            df = ((FFN_RES * gate) * dr).astype(bf16)
            df_sc[...] = df
            df_ref[...] = df
            acc_sc[...] = jnp.zeros_like(acc_sc)

        da = _dot_nt(df_sc[...], wd_ref[0, 0])
        gv = g_ref[0].astype(f32)
        uv = u_ref[0].astype(f32)
        sg = jax.nn.sigmoid(gv)
        si = gv * sg
        a_ref[0] = (si * uv).astype(bf16)
        dgv = (da * uv * (sg * (1.0 + gv * (1.0 - sg)))).astype(bf16)
        duv = (da * si).astype(bf16)
        dg_ref[0] = dgv
        du_ref[0] = duv
        acc_sc[...] += _dot(dgv, wg_ref[0, 0]) + _dot(duv, wu_ref[0, 0])

        @pl.when(j == N_CHIPS - 1)
        def _():
            dh = acc_sc[...]
            xh, rstd0 = _ln_stats(x_ref[...])
            dmod_ref[0:1, :] += jnp.sum(dh, 0, keepdims=True)
            dmod_ref[1:2, :] += jnp.sum(dh * xh, 0, keepdims=True)
            dx_ref[...] = _ln_bwd(dh * (1.0 + mod_ref[1:2, :]), xh, rstd0) + ALPHA * dr_sc[...]

    tok = pl.BlockSpec((ts, D), lambda i, j: (i, 0))
    wrow = pl.BlockSpec((1, 1, Fs, D), lambda i, j: (j, ls, 0, 0))
    hid = pl.BlockSpec((1, ts, Fs), lambda i, j: (j, i, 0))
    hid_shape = jax.ShapeDtypeStruct((N_CHIPS, S, Fs), bf16)
    return pl.pallas_call(
        body, name="ffn_bwd", grid=(S // ts, N_CHIPS),
        in_specs=[tok, tok, tok, hid, hid, _full((3, D)), wrow, wrow, wrow, _full((1, D))],
        out_specs=[tok, hid, hid, hid, tok, _full((3, D)), _full((1, D)), _full((1, D))],
        out_shape=[jax.ShapeDtypeStruct((S, D), f32), hid_shape, hid_shape, hid_shape,
                   jax.ShapeDtypeStruct((S, D), bf16),
                   jax.ShapeDtypeStruct((3, D), f32), jax.ShapeDtypeStruct((1, D), f32), jax.ShapeDtypeStruct((1, D), f32)],
        scratch_shapes=[pltpu.VMEM((ts, D), f32), pltpu.VMEM((ts, D), bf16), pltpu.VMEM((ts, D), f32)],
        compiler_params=_cp(("arbitrary", "arbitrary"), 56),
    )(dxo, x, f, g, u, mod3, wg, wu, wd, lng)


def _ffn_wgrad(h, dg, du, a, df, gwg, gwu, gwd, ls):
    S, D = h.shape
    Fs = dg.shape[-1]
    tk = TOK_TILE
    nk = S // tk

    def body(h_ref, dg_ref, du_ref, a_ref, df_ref, _g0, _g1, _g2, gwg_ref, gwu_ref, gwd_ref, ag_sc, au_sc, ad_sc):
        k = pl.program_id(1)

        @pl.when(k == 0)
        def _():
            ag_sc[...] = jnp.zeros_like(ag_sc)
            au_sc[...] = jnp.zeros_like(au_sc)
            ad_sc[...] = jnp.zeros_like(ad_sc)

        hv = h_ref[...]
        ag_sc[...] += _dot_tn(dg_ref[0], hv)
        au_sc[...] += _dot_tn(du_ref[0], hv)
        ad_sc[...] += _dot_tn(a_ref[0], df_ref[...])

        @pl.when(k == nk - 1)
        def _():
            gwg_ref[0, 0] = ag_sc[...].astype(bf16)
            gwu_ref[0, 0] = au_sc[...].astype(bf16)
            gwd_ref[0, 0] = ad_sc[...].astype(bf16)

    tok = pl.BlockSpec((tk, D), lambda p, k: (k, 0))
    hid = pl.BlockSpec((1, tk, Fs), lambda p, k: (p, k, 0))
    anyspec = pl.BlockSpec(memory_space=pl.ANY)
    orow = pl.BlockSpec((1, 1, Fs, D), lambda p, k: (p, ls, 0, 0))
    return pl.pallas_call(
        body, name="ffn_wgrad", grid=(N_CHIPS, nk),
        in_specs=[tok, hid, hid, hid, tok, anyspec, anyspec, anyspec],
        out_specs=[orow, orow, orow],
        out_shape=[jax.ShapeDtypeStruct(gwg.shape, bf16), jax.ShapeDtypeStruct(gwu.shape, bf16),
                   jax.ShapeDtypeStruct(gwd.shape, bf16)],
        scratch_shapes=[pltpu.VMEM((Fs, D), f32), pltpu.VMEM((Fs, D), f32), pltpu.VMEM((Fs, D), f32)],
        input_output_aliases={5: 0, 6: 1, 7: 2},
        compiler_params=_cp(("parallel", "arbitrary"), 48),
    )(h, dg, du, a, df, gwg, gwu, gwd)


_LANES = 128
_QKV_BLOCKS = D_ATT // _LANES


def _res_spec(lead, d, width, index):
    return pl.BlockSpec((lead, d, TOK_TILE // d, width), index)


def _res_spec3(d, width):
    return pl.BlockSpec((d, TOK_TILE // d, width), lambda i: (0, i, 0))


def _rows_to_residues(tile_bufs, d, put):
    for r in range(d):
        for cb, buf in enumerate(tile_bufs):
            put(r, cb, buf[pl.ds(r, TOK_TILE // d, stride=d), :])


def _residues_to_rows(tile_bufs, d, get):
    for r in range(d):
        for cb, buf in enumerate(tile_bufs):
            buf[pl.ds(r, TOK_TILE // d, stride=d), :] = get(r, cb)


def _mix_in_fwd(x, mod3, w_in, l):
    S, D = x.shape
    N = w_in.shape[-1]
    ts = TOK_TILE

    def body(x_ref, mod_ref, w_ref, o1_ref, o4_ref, o16_ref, zr_ref, h_ref, *bufs):
        j = pl.program_id(1)

        @pl.when(j == 0)
        def _():
            xh, _ = _ln_stats(x_ref[...])
            h_ref[...] = (xh * (1.0 + mod_ref[1:2, :]) + mod_ref[0:1, :]).astype(bf16)

        z = _dot(h_ref[...], w_ref[0, 0])

        @pl.when(j == N_CHIPS - 1)
        def _():
            zr_ref[...] = z

        @pl.when(j < N_CHIPS - 1)
        def _():
            zz = z * jnp.where(j == 0, HEAD_DIM ** -0.5, 1.0)
            o1_ref[j, 0] = zz.astype(bf16)
            for cb, buf in enumerate(bufs):
                buf[...] = zz[:, _LANES * cb:_LANES * (cb + 1)]
            for d, o_ref in zip(DILATIONS[1:], (o4_ref, o16_ref)):
                def put(r, cb, piece, o_ref=o_ref):
                    o_ref[j, r, :, _LANES * cb:_LANES * (cb + 1)] = piece.astype(bf16)
                _rows_to_residues(bufs, d, put)

    tok = pl.BlockSpec((ts, D), lambda i, j: (i, 0))
    res = [_res_spec(3, d, N, lambda i, j: (0, 0, i, 0)) for d in DILATIONS]
    return pl.pallas_call(
        body, name="mix_in_fwd", grid=(S // ts, N_CHIPS),
        in_specs=[tok, _full((3, D)), pl.BlockSpec((1, 1, D, N), lambda i, j: (j, l, 0, 0))],
        out_specs=res + [pl.BlockSpec((ts, N), lambda i, j: (i, 0)), tok],
        out_shape=[jax.ShapeDtypeStruct((3, d, S // d, N), bf16) for d in DILATIONS]
        + [jax.ShapeDtypeStruct((S, N), f32), jax.ShapeDtypeStruct((S, D), bf16)],
        scratch_shapes=[pltpu.VMEM((ts, _LANES), f32)] * _QKV_BLOCKS,
        compiler_params=_cp(("parallel", "arbitrary"), 40),
    )(x, mod3, w_in)


def _mix_in_bwd(dqkv, d_rest, dx_res, x, mod3, w_in, l):
    S, D = x.shape
    N = w_in.shape[-1]
    ts = TOK_TILE

    def body(d1_ref, d4_ref, d16_ref, dr_ref, dxr_ref, x_ref, mod_ref, w_ref, dx_ref, dmod_ref, dz_ref, acc_sc, *bufs):
        i = pl.program_id(0)
        j = pl.program_id(1)

        @pl.when((i == 0) & (j == 0))
        def _():
            dmod_ref[...] = jnp.zeros_like(dmod_ref)

        @pl.when(j == 0)
        def _():
            acc_sc[...] = jnp.zeros_like(acc_sc)

        @pl.when(j == N_CHIPS - 1)
        def _():
            dz_ref[0] = dr_ref[...]

        @pl.when(j < N_CHIPS - 1)
        def _():
            for d, d_ref, tile_bufs in ((4, d4_ref, bufs[:_QKV_BLOCKS]), (16, d16_ref, bufs[_QKV_BLOCKS:])):
                _residues_to_rows(tile_bufs, d, lambda r, cb, d_ref=d_ref: d_ref[0, r, :, _LANES * cb:_LANES * (cb + 1)].astype(f32))
            for cb in range(_QKV_BLOCKS):
                cols = slice(_LANES * cb, _LANES * (cb + 1))
                dz_ref[0, :, cols] = (d1_ref[0, 0, :, cols].astype(f32) + bufs[cb][...] + bufs[_QKV_BLOCKS + cb][...]).astype(bf16)

        acc_sc[...] += _dot_nt(dz_ref[0], w_ref[0, 0])

        @pl.when(j == N_CHIPS - 1)
        def _():
            dh = acc_sc[...]
            xh, rstd0 = _ln_stats(x_ref[...])
            dmod_ref[0:1, :] += jnp.sum(dh, 0, keepdims=True)
            dmod_ref[1:2, :] += jnp.sum(dh * xh, 0, keepdims=True)
            dx_ref[...] = _ln_bwd(dh * (1.0 + mod_ref[1:2, :]), xh, rstd0) + dxr_ref[...]

    tok = pl.BlockSpec((ts, D), lambda i, j: (i, 0))
    res = [_res_spec(1, d, N, lambda i, j: (jnp.minimum(j, 2), 0, i, 0)) for d in DILATIONS]
    return pl.pallas_call(
        body, name="mix_in_bwd", grid=(S // ts, N_CHIPS),
        in_specs=res + [pl.BlockSpec((ts, N), lambda i, j: (i, 0)), tok, tok, _full((3, D)),
                        pl.BlockSpec((1, 1, D, N), lambda i, j: (j, l, 0, 0))],
        out_specs=[tok, _full((3, D)), pl.BlockSpec((1, ts, N), lambda i, j: (j, i, 0))],
        out_shape=[jax.ShapeDtypeStruct((S, D), f32), jax.ShapeDtypeStruct((3, D), f32),
                   jax.ShapeDtypeStruct((N_CHIPS, S, N), bf16)],
        scratch_shapes=[pltpu.VMEM((ts, D), f32)] + [pltpu.VMEM((ts, _LANES), f32)] * (2 * _QKV_BLOCKS),
        compiler_params=_cp(("arbitrary", "arbitrary"), 40),
    )(*dqkv, d_rest, dx_res, x, mod3, w_in)


def _mix_in_wgrad(h, dz, gw, l):
    S, D = h.shape
    N = dz.shape[-1]
    tk = TOK_TILE
    nk = S // tk

    def body(h_ref, dz_ref, _g, gw_ref, acc_sc):
        k = pl.program_id(1)

        @pl.when(k == 0)
        def _():
            acc_sc[...] = jnp.zeros_like(acc_sc)

        acc_sc[...] += _dot_tn(h_ref[...], dz_ref[0])

        @pl.when(k == nk - 1)
        def _():
            gw_ref[0, 0] = acc_sc[...].astype(bf16)

    return pl.pallas_call(
        body, name="mix_in_wgrad", grid=(N_CHIPS, nk),
        in_specs=[pl.BlockSpec((tk, D), lambda p, k: (k, 0)), pl.BlockSpec((1, tk, N), lambda p, k: (p, k, 0)),
                  pl.BlockSpec(memory_space=pl.ANY)],
        out_specs=pl.BlockSpec((1, 1, D, N), lambda p, k: (p, l, 0, 0)),
        out_shape=jax.ShapeDtypeStruct(gw.shape, bf16),
        scratch_shapes=[pltpu.VMEM((D, N), f32)],
        input_output_aliases={2: 0},
        compiler_params=_cp(("parallel", "arbitrary"), 40),
    )(h, dz, gw)


def _mix_out_fwd(x, y_att, y_ssm, y_pool, mod3, w_out, l, lng, lnb):
    S, D = x.shape
    ts = TOK_TILE

    def body(x_ref, ya_ref, ys_ref, yp_ref, mod_ref, w_ref, lng_ref, lnb_ref, xo_ref, y_ref):
        ya = ya_ref[...].astype(bf16)
        y = (_dot(ya[:, 0:256], w_ref[0, 0]) + _dot(ya[:, 256:512], w_ref[1, 0])
             + _dot(ys_ref[...].astype(bf16), w_ref[2, 0]) + _dot(yp_ref[...].astype(bf16), w_ref[3, 0]))
        y_ref[...] = y
        rh, _ = _ln_stats(ALPHA * x_ref[...] + mod_ref[2:3, :] * y)
        xo_ref[...] = rh * lng_ref[...] + lnb_ref[...]

    tok = pl.BlockSpec((ts, D), lambda i: (i, 0))
    return pl.pallas_call(
        body, name="mix_out_fwd", grid=(S // ts,),
        in_specs=[tok, pl.BlockSpec((ts, D_ATT), lambda i: (i, 0)), pl.BlockSpec((ts, D_SSM), lambda i: (i, 0)),
                  pl.BlockSpec((ts, D_POOL), lambda i: (i, 0)), _full((3, D)),
                  pl.BlockSpec((N_CHIPS, 1, 256, D), lambda i: (0, l, 0, 0)), _full((1, D)), _full((1, D))],
        out_specs=[tok, tok],
        out_shape=[jax.ShapeDtypeStruct((S, D), f32), jax.ShapeDtypeStruct((S, D), f32)],
        compiler_params=_cp(("parallel",), 40),
    )(x, y_att, y_ssm, y_pool, mod3, w_out, lng, lnb)


def _mix_out_bwd(dxo, x, y, y_att, y_ssm, y_pool, mod3, w_out, l, lng, gw_out):
    S, D = x.shape
    ts = TOK_TILE
    nt = S // ts

    def body(dxo_ref, x_ref, y_ref, ya_ref, ys_ref, yp_ref, mod_ref, w_ref, lng_ref, _g,
             dxr_ref, da_ref, ds_ref, dp_ref, dgate_ref, dlng_ref, dlnb_ref, gw_ref, acc_sc):
        i = pl.program_id(0)

        @pl.when(i == 0)
        def _():
            dgate_ref[...] = jnp.zeros_like(dgate_ref)
            dlng_ref[...] = jnp.zeros_like(dlng_ref)
            dlnb_ref[...] = jnp.zeros_like(dlnb_ref)
            acc_sc[...] = jnp.zeros_like(acc_sc)

        gate = mod_ref[2:3, :]
        yv = y_ref[...]
        rh, rstd = _ln_stats(ALPHA * x_ref[...] + gate * yv)
        dy_out = dxo_ref[...]
        dlng_ref[...] += jnp.sum(dy_out * rh, 0, keepdims=True)
        dlnb_ref[...] += jnp.sum(dy_out, 0, keepdims=True)
        dr = _ln_bwd(dy_out * lng_ref[...], rh, rstd)
        dxr_ref[...] = ALPHA * dr
        dgate_ref[...] += jnp.sum(dr * yv, 0, keepdims=True)
        dy = (gate * dr).astype(bf16)
        da_ref[:, 0:256] = _dot_nt(dy, w_ref[0, 0])
        da_ref[:, 256:512] = _dot_nt(dy, w_ref[1, 0])
        ds_ref[...] = _dot_nt(dy, w_ref[2, 0])
        dp_ref[...] = _dot_nt(dy, w_ref[3, 0])
        ya = ya_ref[...].astype(bf16)
        acc_sc[0] += _dot_tn(ya[:, 0:256], dy)
        acc_sc[1] += _dot_tn(ya[:, 256:512], dy)
        acc_sc[2] += _dot_tn(ys_ref[...].astype(bf16), dy)
        acc_sc[3] += _dot_tn(yp_ref[...].astype(bf16), dy)

        @pl.when(i == nt - 1)
        def _():
            gw_ref[:, 0] = acc_sc[...].astype(bf16)

    tok = pl.BlockSpec((ts, D), lambda i: (i, 0))
    t512 = pl.BlockSpec((ts, D_ATT), lambda i: (i, 0))
    t256 = pl.BlockSpec((ts, 256), lambda i: (i, 0))
    wspec = pl.BlockSpec((N_CHIPS, 1, 256, D), lambda i: (0, l, 0, 0))
    return pl.pallas_call(
        body, name="mix_out_bwd", grid=(nt,),
        in_specs=[tok, tok, tok, t512, t256, t256, _full((3, D)), wspec, _full((1, D)), pl.BlockSpec(memory_space=pl.ANY)],
        out_specs=[tok, t512, t256, t256, _full((1, D)), _full((1, D)), _full((1, D)), wspec],
        out_shape=[jax.ShapeDtypeStruct((S, D), f32), jax.ShapeDtypeStruct((S, D_ATT), f32),
                   jax.ShapeDtypeStruct((S, D_SSM), f32), jax.ShapeDtypeStruct((S, D_POOL), f32),
                   jax.ShapeDtypeStruct((1, D), f32), jax.ShapeDtypeStruct((1, D), f32), jax.ShapeDtypeStruct((1, D), f32),
                   jax.ShapeDtypeStruct(gw_out.shape, bf16)],
        scratch_shapes=[pltpu.VMEM((N_CHIPS, 256, D), f32)],
        input_output_aliases={9: 7},
        compiler_params=_cp(("arbitrary",), 48),
    )(dxo, x, y, y_att, y_ssm, y_pool, mod3, w_out, lng, gw_out)


def _t5_bucket(dist):
    max_exact = N_BUCKETS // 2
    d = np.maximum(dist, 1).astype(np.float32)
    large = max_exact + (np.log(d / max_exact) / math.log(MAX_DISTANCE / max_exact)
                         * (N_BUCKETS - max_exact)).astype(np.int32)
    large = np.minimum(large, N_BUCKETS - 1)
    return np.where(dist < max_exact, dist, large).astype(np.int32)


def _bucket_table():
    q = ATT_BLOCK
    i = np.arange(q)[:, None]
    j = np.arange(2 * q)[None, :]
    r = i + q - j
    in_band = (r >= 0) & (r <= q)
    tabs = [np.where(in_band, _t5_bucket(np.clip(r, 0, None) * d), -1) for d in DILATIONS]
    return np.stack(tabs).astype(np.int32)


def _bias_fwd(rel_bias, table):
    def body(rb_ref, tab_ref, out_ref):
        for b in range(3):
            tb = tab_ref[b]
            for h in range(N_HEADS):
                def pick(k, acc):
                    return jnp.where(tb == k, rb_ref[k, h], acc)
                out_ref[b, h] = lax.fori_loop(0, N_BUCKETS, pick, jnp.where(tb < 0, NEG, 0.0).astype(f32))

    return pl.pallas_call(
        body, name="bias_fwd",
        in_specs=[pl.BlockSpec(memory_space=pltpu.SMEM), pl.BlockSpec(memory_space=pltpu.VMEM)],
        out_specs=pl.BlockSpec(memory_space=pltpu.VMEM),
        out_shape=jax.ShapeDtypeStruct((3, N_HEADS, ATT_BLOCK, 2 * ATT_BLOCK), f32),
    )(rel_bias, table)


def _bias_bwd(dbias, table):
    def body(db_ref, tab_ref, out_ref):
        def per_bucket(k, c):
            for h in range(N_HEADS):
                tot = jnp.zeros((), f32)
                for b in range(3):
                    tot = tot + jnp.sum(jnp.where(tab_ref[b] == k, db_ref[b, h], 0.0))
                out_ref[k, h] = tot
            return c
        lax.fori_loop(0, N_BUCKETS, per_bucket, 0)

    return pl.pallas_call(
        body, name="bias_bwd",
        in_specs=[pl.BlockSpec(memory_space=pltpu.VMEM), pl.BlockSpec(memory_space=pltpu.VMEM)],
        out_specs=pl.BlockSpec(memory_space=pltpu.SMEM),
        out_shape=jax.ShapeDtypeStruct((N_BUCKETS, N_HEADS), f32),
    )(dbias, table)


def _att_unit(u, nbr):
    rows = pl.ds(pl.multiple_of(u * ATT_BLOCK, ATT_BLOCK), ATT_BLOCK)
    prev = pl.ds(pl.multiple_of(jnp.maximum(u - 1, 0) * ATT_BLOCK, ATT_BLOCK), ATT_BLOCK)
    return rows, prev, (u % nbr) != 0


_HEAD_ROWS = N_HEADS * ATT_BLOCK


def _head_rows(t):
    lane = lax.broadcasted_iota(jnp.int32, t.shape, 1)
    return jnp.concatenate([jnp.where((lane >= HEAD_DIM * h) & (lane < HEAD_DIM * (h + 1)), t, jnp.zeros_like(t))
                            for h in range(N_HEADS)], axis=0)


def _head_cols(big):
    lane = lax.broadcasted_iota(jnp.int32, (ATT_BLOCK, D_ATT), 1)
    out = big[0:ATT_BLOCK]
    for h in range(1, N_HEADS):
        out = jnp.where(lane >= HEAD_DIM * h, big[ATT_BLOCK * h:ATT_BLOCK * (h + 1)], out)
    return out


def _head_column(ref, rows):
    t = ref[rows, :]
    return jnp.concatenate([t[:, h:h + 1] for h in range(N_HEADS)], axis=0)


def _att_band(ref, rows, prev, nbr):
    cur = ref[0, rows, :]
    return cur if nbr == 1 else jnp.concatenate([ref[0, prev, :], cur], axis=0)


def _att_scores(q_ref, k_ref, b_ref, rows, prev, valid_prev, nbr):
    qbd = _head_rows(q_ref[0, rows, :])
    kb = _att_band(k_ref, rows, prev, nbr)
    bias = b_ref[0].reshape(_HEAD_ROWS, 2 * ATT_BLOCK)
    if nbr == 1:
        return qbd, kb, _dot_nt(qbd, kb) + bias[:, ATT_BLOCK:]
    s = _dot_nt(qbd, kb) + bias
    col = lax.broadcasted_iota(jnp.int32, s.shape, 1)
    return qbd, kb, jnp.where((col >= ATT_BLOCK) | valid_prev, s, NEG)


def _qkv_specs(S, branch):
    return ([pl.BlockSpec((1, S, D_ATT), lambda i, t=t: (t, 0, 0)) for t in range(3)],
            pl.BlockSpec((1, N_HEADS, ATT_BLOCK, 2 * ATT_BLOCK), lambda i: (branch, 0, 0, 0)))


def _att_fwd(qkv, bias, branch):
    S = qkv.shape[1]
    nbr = BLOCKS_PER_RESIDUE[branch]

    def body(q_ref, k_ref, v_ref, b_ref, o_ref, lse_ref):
        lse_ref[...] = jnp.zeros_like(lse_ref)

        def unit(u, c):
            rows, prev, valid_prev = _att_unit(u, nbr)
            _, _, s = _att_scores(q_ref, k_ref, b_ref, rows, prev, valid_prev, nbr)
            m = jnp.max(s, -1, keepdims=True)
            p = jnp.exp(s - m)
            den = jnp.sum(p, -1, keepdims=True)
            big = _dot(p.astype(bf16), _att_band(v_ref, rows, prev, nbr))
            o_ref[rows, :] = _head_cols(big / den)
            lse = m + jnp.log(den)
            for h in range(N_HEADS):
                lse_ref[rows, pl.ds(h, 1)] = lse[ATT_BLOCK * h:ATT_BLOCK * (h + 1)]
            return c

        lax.fori_loop(0, N_UNITS, unit, 0)

    qkv_specs, bspec = _qkv_specs(S, branch)
    return pl.pallas_call(
        body, name="att_fwd", grid=(1,),
        in_specs=qkv_specs + [bspec],
        out_specs=[pl.BlockSpec((S, D_ATT), lambda i: (0, 0)), pl.BlockSpec((S, _LANES), lambda i: (0, 0))],
        out_shape=[jax.ShapeDtypeStruct((S, D_ATT), f32), jax.ShapeDtypeStruct((S, _LANES), f32)],
        compiler_params=_cp(("arbitrary",), 40),
    )(qkv, qkv, qkv, bias)


def _att_bwd(qkv, do, lse, crow, bias, branch):
    S = qkv.shape[1]
    nbr = BLOCKS_PER_RESIDUE[branch]

    def body(q_ref, k_ref, v_ref, do_ref, lse_ref, c_ref, b_ref, dqkv_ref, db_ref, dk_sc, dv_sc):
        dk_sc[...] = jnp.zeros_like(dk_sc)
        dv_sc[...] = jnp.zeros_like(dv_sc)
        db_ref[...] = jnp.zeros_like(db_ref)

        def unit(u, c):
            rows, prev, valid_prev = _att_unit(u, nbr)
            qbd, kb, s = _att_scores(q_ref, k_ref, b_ref, rows, prev, valid_prev, nbr)
            p = jnp.exp(s - _head_column(lse_ref, rows))
            dobd = _head_rows(do_ref[rows, :])
            ds = p * (_dot_nt(dobd, _att_band(v_ref, rows, prev, nbr)) - _head_column(c_ref, rows))
            if nbr == 1:
                db_ref[:, :, ATT_BLOCK:] += ds.reshape(N_HEADS, ATT_BLOCK, ATT_BLOCK)
            else:
                db_ref[...] += ds.reshape(N_HEADS, ATT_BLOCK, 2 * ATT_BLOCK)
            dsb = ds.astype(bf16)
            dqkv_ref[0, rows, :] = (HEAD_DIM ** -0.5 * _head_cols(_dot(dsb, kb))).astype(bf16)
            dkb = _dot_tn(dsb, qbd)
            dvb = _dot_tn(p.astype(bf16), dobd)
            if nbr == 1:
                dk_sc[rows, :] += dkb
                dv_sc[rows, :] += dvb
            else:
                dk_sc[prev, :] += dkb[:ATT_BLOCK]
                dv_sc[prev, :] += dvb[:ATT_BLOCK]
                dk_sc[rows, :] += dkb[ATT_BLOCK:]
                dv_sc[rows, :] += dvb[ATT_BLOCK:]
            return c

        lax.fori_loop(0, N_UNITS, unit, 0)
        dqkv_ref[1] = dk_sc[...].astype(bf16)
        dqkv_ref[2] = dv_sc[...].astype(bf16)

    qkv_specs, bspec = _qkv_specs(S, branch)
    row = pl.BlockSpec((S, _LANES), lambda i: (0, 0))
    return pl.pallas_call(
        body, name="att_bwd", grid=(1,),
        in_specs=qkv_specs + [pl.BlockSpec((S, D_ATT), lambda i: (0, 0)), row, row, bspec],
        out_specs=[pl.BlockSpec((3, S, D_ATT), lambda i: (0, 0, 0)),
                   pl.BlockSpec((N_HEADS, ATT_BLOCK, 2 * ATT_BLOCK), lambda i: (0, 0, 0))],
        out_shape=[jax.ShapeDtypeStruct((3, S, D_ATT), bf16), jax.ShapeDtypeStruct((N_HEADS, ATT_BLOCK, 2 * ATT_BLOCK), f32)],
        scratch_shapes=[pltpu.VMEM((S, D_ATT), f32), pltpu.VMEM((S, D_ATT), f32)],
        compiler_params=_cp(("arbitrary",), 48),
    )(qkv, qkv, qkv, do, lse, crow, bias)


def _branch_weights(lse_ref):
    l0, l1, l2 = lse_ref[0], lse_ref[1], lse_ref[2]
    m = jnp.maximum(jnp.maximum(l0, l1), l2)
    e0, e1, e2 = jnp.exp(l0 - m), jnp.exp(l1 - m), jnp.exp(l2 - m)
    tot = e0 + e1 + e2
    return e0 / tot, e1 / tot, e2 / tot


def _att_merge(os, lses):
    S = os[0].shape[0] * os[0].shape[1]
    ts = TOK_TILE

    def body(o1_ref, o4_ref, o16_ref, l1_ref, l4_ref, l16_ref, y_ref, lt_ref, *bufs):
        obufs = (bufs[:_QKV_BLOCKS], bufs[_QKV_BLOCKS:2 * _QKV_BLOCKS])
        lt_ref[0] = l1_ref[0]
        for k, (d, o_ref, l_ref) in enumerate(((4, o4_ref, l4_ref), (16, o16_ref, l16_ref))):
            _residues_to_rows(obufs[k], d, lambda r, cb, o_ref=o_ref: o_ref[r, :, _LANES * cb:_LANES * (cb + 1)])
            _residues_to_rows([bufs[2 * _QKV_BLOCKS + k]], d, lambda r, cb, l_ref=l_ref: l_ref[r])
            lt_ref[1 + k] = bufs[2 * _QKV_BLOCKS + k][...]
        w = _branch_weights(lt_ref)
        for h in range(N_HEADS):
            cs = slice(HEAD_DIM * h, HEAD_DIM * (h + 1))
            half = slice(HEAD_DIM * (h % 2), HEAD_DIM * (h % 2 + 1))
            y_ref[:, cs] = (w[0][:, h:h + 1] * o1_ref[0, :, cs] + w[1][:, h:h + 1] * obufs[0][h // 2][:, half]
                            + w[2][:, h:h + 1] * obufs[1][h // 2][:, half])

    return pl.pallas_call(
        body, name="att_merge", grid=(S // ts,),
        in_specs=[_res_spec3(d, D_ATT) for d in DILATIONS] + [_res_spec3(d, _LANES) for d in DILATIONS],
        out_specs=[pl.BlockSpec((ts, D_ATT), lambda i: (i, 0)), pl.BlockSpec((3, ts, _LANES), lambda i: (0, i, 0))],
        out_shape=[jax.ShapeDtypeStruct((S, D_ATT), f32), jax.ShapeDtypeStruct((3, S, _LANES), f32)],
        scratch_shapes=[pltpu.VMEM((ts, _LANES), f32)] * (2 * _QKV_BLOCKS + 2),
        compiler_params=_cp(("parallel",)),
    )(*os, *lses)


def _att_merge_bwd(dy, y, lse3):
    S = dy.shape[0]
    ts = TOK_TILE

    def body(dy_ref, y_ref, lse_ref, do1_ref, do4_ref, do16_ref, c1_ref, c4_ref, c16_ref, *bufs):
        dobufs = (bufs[:_QKV_BLOCKS], bufs[_QKV_BLOCKS:2 * _QKV_BLOCKS], bufs[2 * _QKV_BLOCKS:3 * _QKV_BLOCKS])
        cbufs = bufs[3 * _QKV_BLOCKS:]
        w = _branch_weights(lse_ref)
        for cb in cbufs:
            cb[...] = jnp.zeros_like(cb)
        for h in range(N_HEADS):
            cs = slice(HEAD_DIM * h, HEAD_DIM * (h + 1))
            half = slice(HEAD_DIM * (h % 2), HEAD_DIM * (h % 2 + 1))
            dyh = dy_ref[:, cs]
            t = jnp.sum(dyh * y_ref[:, cs], -1, keepdims=True)
            for p in range(3):
                wp = w[p][:, h:h + 1]
                dobufs[p][h // 2][:, half] = wp * dyh
                cbufs[p][:, h:h + 1] = wp * t
        for cb in range(_QKV_BLOCKS):
            do1_ref[0, :, _LANES * cb:_LANES * (cb + 1)] = dobufs[0][cb][...].astype(bf16)
        c1_ref[0] = cbufs[0][...]
        for k, (d, do_ref, c_ref) in enumerate(((4, do4_ref, c4_ref), (16, do16_ref, c16_ref))):
            def put_do(r, cb, piece, do_ref=do_ref):
                do_ref[r, :, _LANES * cb:_LANES * (cb + 1)] = piece.astype(bf16)

            def put_c(r, cb, piece, c_ref=c_ref):
                c_ref[r] = piece

            _rows_to_residues(dobufs[1 + k], d, put_do)
            _rows_to_residues([cbufs[1 + k]], d, put_c)

    return pl.pallas_call(
        body, name="att_merge_bwd", grid=(S // ts,),
        in_specs=[pl.BlockSpec((ts, D_ATT), lambda i: (i, 0)), pl.BlockSpec((ts, D_ATT), lambda i: (i, 0)),
                  pl.BlockSpec((3, ts, _LANES), lambda i: (0, i, 0))],
        out_specs=[_res_spec3(d, D_ATT) for d in DILATIONS] + [_res_spec3(d, _LANES) for d in DILATIONS],
        out_shape=[jax.ShapeDtypeStruct((d, S // d, D_ATT), bf16) for d in DILATIONS]
        + [jax.ShapeDtypeStruct((d, S // d, _LANES), f32) for d in DILATIONS],
        scratch_shapes=[pltpu.VMEM((ts, _LANES), f32)] * (3 * _QKV_BLOCKS + 3),
        compiler_params=_cp(("parallel",)),
    )(dy, y, lse3)


_SSM_ROWS = 256


def _scan_in_place(sr_ref, si_ref, a_ref, reverse):
    S, N = sr_ref.shape
    nst = S // SCAN_SEG
    ar = jnp.broadcast_to(a_ref[0:1, :], (SCAN_SEG, N))
    ai = jnp.broadcast_to(a_ref[1:2, :], (SCAN_SEG, N))
    if reverse:
        ai = -ai
    row = lax.broadcasted_iota(jnp.int32, (SCAN_SEG, N), 0)
    zero = jnp.zeros((SCAN_SEG, N), f32)

    def tile(t):
        return pl.ds(pl.multiple_of((nst - 1 - t if reverse else t) * SCAN_SEG, SCAN_SEG), SCAN_SEG)

    def local(t, c):
        sr, si, pr, pi = c
        rows = tile(t)
        nsr = ar * sr - ai * si + sr_ref[rows, :]
        nsi = ar * si + ai * sr + si_ref[rows, :]
        sr_ref[rows, :] = nsr
        si_ref[rows, :] = nsi
        return nsr, nsi, ar * pr - ai * pi, ar * pi + ai * pr

    fr, fi, apr, api = lax.fori_loop(0, nst, local, (zero, zero, zero + 1.0, zero))

    def shift(v):
        if reverse:
            return jnp.where(row == SCAN_SEG - 1, 0.0, pltpu.roll(v, SCAN_SEG - 1, axis=0))
        return jnp.where(row == 0, 0.0, pltpu.roll(v, 1, axis=0))

    cr, ci = zero, zero
    for _ in range(SCAN_SEG - 1):
        cr, ci = shift(fr + apr * cr - api * ci), shift(fi + apr * ci + api * cr)

    def fix(t, c):
        pr, pi = c
        npr, npi = ar * pr - ai * pi, ar * pi + ai * pr
        rows = tile(t)
        sr_ref[rows, :] += npr * cr - npi * ci
        si_ref[rows, :] += npr * ci + npi * cr
        return npr, npi

    lax.fori_loop(0, nst, fix, (zero + 1.0, zero))


def _ssm_states(u, bre, bim, a2):
    S = u.shape[0]

    def body(u_ref, br_ref, bi_ref, a_ref, sr_ref, si_ref):
        brb = br_ref[...].astype(bf16)
        bib = bi_ref[...].astype(bf16)

        def project(t, c):
            rows = pl.ds(pl.multiple_of(t * _SSM_ROWS, _SSM_ROWS), _SSM_ROWS)
            ub = u_ref[rows, :].astype(bf16)
            sr_ref[rows, :] = _dot(ub, brb)
            si_ref[rows, :] = _dot(ub, bib)
            return c

        lax.fori_loop(0, S // _SSM_ROWS, project, 0)
        _scan_in_place(sr_ref, si_ref, a_ref, False)

    vm = pl.BlockSpec(memory_space=pltpu.VMEM)
    return pl.pallas_call(
        body, name="ssm_states", in_specs=[vm] * 4, out_specs=[vm, vm],
        out_shape=[jax.ShapeDtypeStruct((S, D_STATE), f32)] * 2,
        compiler_params=_cp(None, 48),
    )(u, bre, bim, a2)


def _ssm_out(sr, si, u, cre, cim, dskip, glu_w, glu_b):
    S = u.shape[0]
    ts = TOK_TILE

    def body(sr_ref, si_ref, u_ref, cr_ref, ci_ref, d_ref, w_ref, b_ref, out_ref, y_ref):
        y = (_dot(sr_ref[...].astype(bf16), cr_ref[...].astype(bf16))
             - _dot(si_ref[...].astype(bf16), ci_ref[...].astype(bf16)) + d_ref[...] * u_ref[...])
        y_ref[...] = y
        z = _dot(_gelu(y).astype(bf16), w_ref[...].astype(bf16)) + b_ref[...]
        out_ref[...] = y * jax.nn.sigmoid(z)

    st = pl.BlockSpec((ts, D_STATE), lambda i: (i, 0))
    ch = pl.BlockSpec((ts, D_SSM), lambda i: (i, 0))
    return pl.pallas_call(
        body, name="ssm_out", grid=(S // ts,),
        in_specs=[st, st, ch, _full((D_STATE, D_SSM)), _full((D_STATE, D_SSM)), _full((1, D_SSM)),
                  _full((D_SSM, D_SSM)), _full((1, D_SSM))],
        out_specs=[ch, ch],
        out_shape=[jax.ShapeDtypeStruct((S, D_SSM), f32)] * 2,
        compiler_params=_cp(("parallel",)),
    )(sr, si, u, cre, cim, dskip, glu_w, glu_b)


def _ssm_out_bwd(dout, y, u, sr, si, dskip, glu_w, glu_b):
    S = u.shape[0]
    ts = TOK_TILE

    def body(do_ref, y_ref, u_ref, sr_ref, si_ref, d_ref, w_ref, b_ref,
             dy_ref, du_ref, dcr_ref, dci_ref, dd_ref, dgb_ref, dgw_ref):
        @pl.when(pl.program_id(0) == 0)
        def _():
            for r in (dcr_ref, dci_ref, dd_ref, dgb_ref, dgw_ref):
                r[...] = jnp.zeros_like(r)

        y = y_ref[...]
        dout = do_ref[...]
        wb = w_ref[...].astype(bf16)
        ge = _gelu(y).astype(bf16)
        sz = jax.nn.sigmoid(_dot(ge, wb) + b_ref[...])
        dz = dout * y * sz * (1.0 - sz)
        dzb = dz.astype(bf16)
        dgb_ref[...] += jnp.sum(dz, 0, keepdims=True)
        dgw_ref[...] += _dot_tn(ge, dzb)
        dy = dout * sz + _gelu_grad(y) * _dot_nt(dzb, wb)
        uv = u_ref[...]
        dd_ref[...] += jnp.sum(dy * uv, 0, keepdims=True)
        du_ref[...] = dy * d_ref[...]
        dy_ref[...] = dy
        dyb = dy.astype(bf16)
        dcr_ref[...] += _dot_tn(sr_ref[...].astype(bf16), dyb)
        dci_ref[...] -= _dot_tn(si_ref[...].astype(bf16), dyb)

    st = pl.BlockSpec((ts, D_STATE), lambda i: (i, 0))
    ch = pl.BlockSpec((ts, D_SSM), lambda i: (i, 0))
    c_full = _full((D_STATE, D_SSM))
    return pl.pallas_call(
        body, name="ssm_out_bwd", grid=(S // ts,),
        in_specs=[ch, ch, ch, st, st, _full((1, D_SSM)), _full((D_SSM, D_SSM)), _full((1, D_SSM))],
        out_specs=[ch, ch, c_full, c_full, _full((1, D_SSM)), _full((1, D_SSM)), _full((D_SSM, D_SSM))],
        out_shape=[jax.ShapeDtypeStruct((S, D_SSM), f32), jax.ShapeDtypeStruct((S, D_SSM), f32),
                   jax.ShapeDtypeStruct((D_STATE, D_SSM), f32), jax.ShapeDtypeStruct((D_STATE, D_SSM), f32),
                   jax.ShapeDtypeStruct((1, D_SSM), f32), jax.ShapeDtypeStruct((1, D_SSM), f32),
                   jax.ShapeDtypeStruct((D_SSM, D_SSM), f32)],
        compiler_params=_cp(("arbitrary",), 40),
    )(dout, y, u, sr, si, dskip, glu_w, glu_b)


def _ssm_states_bwd(dy, du_skip, u, sr, si, cre, cim, bre, bim, a2):
    S = u.shape[0]
    N = D_STATE
    nst = S // SCAN_SEG
    nproj = S // _SSM_ROWS

    def body(dy_ref, dus_ref, u_ref, sr_ref, si_ref, cr_ref, ci_ref, br_ref, bi_ref, a_ref,
             du_ref, dbr_ref, dbi_ref, da_ref, lr_ref, li_ref):
        crb = cr_ref[...].astype(bf16)
        cib = ci_ref[...].astype(bf16)

        def project(t, c):
            rows = pl.ds(pl.multiple_of(t * _SSM_ROWS, _SSM_ROWS), _SSM_ROWS)
            dyb = dy_ref[rows, :].astype(bf16)
            lr_ref[rows, :] = _dot_nt(dyb, crb)
            li_ref[rows, :] = -_dot_nt(dyb, cib)
            return c

        lax.fori_loop(0, nproj, project, 0)
        _scan_in_place(lr_ref, li_ref, a_ref, True)

        row = lax.broadcasted_iota(jnp.int32, (SCAN_SEG, N), 0)
        last = pl.ds((nst - 1) * SCAN_SEG, SCAN_SEG)
        pr = jnp.where(row == 0, 0.0, pltpu.roll(sr_ref[last, :], 1, axis=0))
        pi = jnp.where(row == 0, 0.0, pltpu.roll(si_ref[last, :], 1, axis=0))
        first = pl.ds(0, SCAN_SEG)
        acc_r = lr_ref[first, :] * pr + li_ref[first, :] * pi
        acc_i = li_ref[first, :] * pr - lr_ref[first, :] * pi

        def step(t, c):
            acc_r, acc_i = c
            rows = pl.ds(pl.multiple_of(t * SCAN_SEG, SCAN_SEG), SCAN_SEG)
            prev = pl.ds(pl.multiple_of((t - 1) * SCAN_SEG, SCAN_SEG), SCAN_SEG)
            lrv, liv, srv, siv = lr_ref[rows, :], li_ref[rows, :], sr_ref[prev, :], si_ref[prev, :]
            return acc_r + lrv * srv + liv * siv, acc_i + liv * srv - lrv * siv

        acc_r, acc_i = lax.fori_loop(1, nst, step, (acc_r, acc_i))
        da_ref[0:1, :] = jnp.sum(acc_r, 0, keepdims=True)
        da_ref[1:2, :] = jnp.sum(acc_i, 0, keepdims=True)

        brb = br_ref[...].astype(bf16)
        bib = bi_ref[...].astype(bf16)
        dbr_ref[...] = jnp.zeros_like(dbr_ref)
        dbi_ref[...] = jnp.zeros_like(dbi_ref)

        def back(t, c):
            rows = pl.ds(pl.multiple_of(t * _SSM_ROWS, _SSM_ROWS), _SSM_ROWS)
            lrb = lr_ref[rows, :].astype(bf16)
            lib = li_ref[rows, :].astype(bf16)
            du_ref[rows, :] = dus_ref[rows, :] + _dot_nt(lrb, brb) + _dot_nt(lib, bib)
            ub = u_ref[rows, :].astype(bf16)
            dbr_ref[...] += _dot_tn(ub, lrb)
            dbi_ref[...] += _dot_tn(ub, lib)
            return c

        lax.fori_loop(0, nproj, back, 0)

    vm = pl.BlockSpec(memory_space=pltpu.VMEM)
    return pl.pallas_call(
        body, name="ssm_states_bwd", in_specs=[vm] * 10, out_specs=[vm] * 4,
        out_shape=[jax.ShapeDtypeStruct((S, D_SSM), f32), jax.ShapeDtypeStruct((D_SSM, D_STATE), f32),
                   jax.ShapeDtypeStruct((D_SSM, D_STATE), f32), jax.ShapeDtypeStruct((2, D_STATE), f32)],
        scratch_shapes=[pltpu.VMEM((S, D_STATE), f32), pltpu.VMEM((S, D_STATE), f32)],
        compiler_params=_cp(None, 56),
    )(dy, du_skip, u, sr, si, cre, cim, bre, bim, a2)


_POOL_TILE = 256


def _window_sums(xt, back):
    n = xt.shape[0]
    out = []
    ws = xt
    for k in (1, 2, 4, 8):
        ws = ws + pltpu.roll(ws, k if back else n - k, axis=0)
        out.append(ws)
    return out


def _pool_count(r0, w):
    t = r0 + lax.broadcasted_iota(jnp.int32, (_POOL_TILE, POOL_GROUP), 0)
    return jnp.minimum(t + 1, w).astype(f32)


def _pool_fwd(u_pad, pool_w, pool_scale):
    S = u_pad.shape[0] - POOL_HALO
    nt = S // _POOL_TILE

    def body(u_ref, w_ref, sc_ref, y_ref):
        def tile(t, c):
            r0 = pl.multiple_of(t * _POOL_TILE, _POOL_TILE)
            for g, w in enumerate(POOL_WINDOWS):
                cs = pl.ds(POOL_GROUP * g, POOL_GROUP)
                xt = u_ref[pl.ds(r0, _POOL_TILE + POOL_HALO), cs]
                ws = _window_sums(xt, True)[g][POOL_HALO:, :]
                pooled = ws / _pool_count(r0, w) - xt[POOL_HALO:, :]
                y_ref[pl.ds(r0, _POOL_TILE), cs] = _dot(pooled.astype(bf16), w_ref[g].astype(bf16)) * sc_ref[:, cs]
            return c
        lax.fori_loop(0, nt, tile, 0)

    vm = pl.BlockSpec(memory_space=pltpu.VMEM)
    return pl.pallas_call(
        body, name="pool_fwd", in_specs=[vm, vm, vm], out_specs=vm,
        out_shape=jax.ShapeDtypeStruct((S, D_POOL), f32),
    )(u_pad, pool_w, pool_scale)


def _pool_bwd(dy_pad, u_pad, pool_w, pool_scale):
    S = u_pad.shape[0] - POOL_HALO
    nt = S // _POOL_TILE
    n = _POOL_TILE + POOL_HALO

    def body(dy_ref, u_ref, w_ref, sc_ref, du_ref, dw_ref, dsc_ref):
        dw_ref[...] = jnp.zeros_like(dw_ref)
        dsc_ref[...] = jnp.zeros_like(dsc_ref)

        def tile(t, c):
            r0 = pl.multiple_of(t * _POOL_TILE, _POOL_TILE)
            for g, w in enumerate(POOL_WINDOWS):
                cs = pl.ds(POOL_GROUP * g, POOL_GROUP)
                wb = w_ref[g].astype(bf16)
                xt = u_ref[pl.ds(r0, n), cs]
                pooled = (_window_sums(xt, True)[g][POOL_HALO:, :] / _pool_count(r0, w) - xt[POOL_HALO:, :]).astype(bf16)
                dy = dy_ref[pl.ds(r0, _POOL_TILE), cs]
                dsc_ref[:, cs] += jnp.sum(dy * _dot(pooled, wb), 0, keepdims=True)
                dw_ref[g] += _dot_tn(pooled, (dy * sc_ref[:, cs]).astype(bf16))
                dyh = (dy_ref[pl.ds(r0, n), cs] * sc_ref[:, cs]).astype(bf16)
                dpl = _dot_nt(dyh, wb)
                cnt = jnp.minimum(r0 + lax.broadcasted_iota(jnp.int32, (n, POOL_GROUP), 0) + 1, w).astype(f32)
                lead = _window_sums(dpl / cnt, False)[g]
                du_ref[pl.ds(r0, _POOL_TILE), cs] = lead[:_POOL_TILE, :] - dpl[:_POOL_TILE, :]
            return c
        lax.fori_loop(0, nt, tile, 0)

    vm = pl.BlockSpec(memory_space=pltpu.VMEM)
    return pl.pallas_call(
        body, name="pool_bwd", in_specs=[vm, vm, vm, vm], out_specs=[vm, vm, vm],
        out_shape=[jax.ShapeDtypeStruct((S, D_POOL), f32), jax.ShapeDtypeStruct((4, POOL_GROUP, POOL_GROUP), f32),
                   jax.ShapeDtypeStruct((1, D_POOL), f32)],
    )(dy_pad, u_pad, pool_w, pool_scale)


def _loss_head(y, target):
    S, D = y.shape
    ts = TOK_TILE

    def body(y_ref, t_ref, loss_ref, dy_ref):
        @pl.when(pl.program_id(0) == 0)
        def _():
            loss_ref[...] = jnp.zeros_like(loss_ref)

        d = y_ref[...] - t_ref[...]
        dy_ref[...] = d * (1.0 / D)
        loss_ref[...] += 0.5 * jnp.sum(jnp.sum(d * d, -1, keepdims=True) * (1.0 / D), 0, keepdims=True)

    tok = pl.BlockSpec((ts, D), lambda i: (i, 0))
    return pl.pallas_call(
        body, name="loss_head", grid=(S // ts,),
        in_specs=[tok, tok], out_specs=[_full((1, 1)), tok],
        out_shape=[jax.ShapeDtypeStruct((1, 1), f32), jax.ShapeDtypeStruct((S, D), f32)],
        compiler_params=_cp(("arbitrary",)),
    )(y, target)


_ADA_COLS = 768


def _ada_fwd(c_all, ada_w, ada_b_cols):
    L, D, N = ada_w.shape
    B = c_all.shape[0]

    def body(c_ref, w_ref, b_ref, out_ref):
        cv = c_ref[...]
        cond = (cv * jax.nn.sigmoid(cv)).astype(bf16)
        out_ref[0] = _dot(cond, w_ref[0].astype(bf16)) + b_ref[0]

    return pl.pallas_call(
        body, name="ada_fwd", grid=(L, N // _ADA_COLS),
        in_specs=[_full((B, D)), pl.BlockSpec((1, D, _ADA_COLS), lambda l, j: (l, 0, j)),
                  pl.BlockSpec((1, 1, _ADA_COLS), lambda l, j: (l, 0, j))],
        out_specs=pl.BlockSpec((1, B, _ADA_COLS), lambda l, j: (l, 0, j)),
        out_shape=jax.ShapeDtypeStruct((L, B, N), f32),
        compiler_params=_cp(("parallel", "parallel")),
    )(c_all, ada_w, ada_b_cols)


def _ada_wgrad(c_all_t, dmod_cols):
    D, B = c_all_t.shape
    L, _, N = dmod_cols.shape

    def body(ct_ref, dm_ref, out_ref):
        cv = ct_ref[...]
        cond = cv * jax.nn.sigmoid(cv)
        acc = cond[:, 0:1] * dm_ref[0, 0:1, :]
        for b in range(1, B):
            acc = acc + cond[:, b:b + 1] * dm_ref[0, b:b + 1, :]
        out_ref[0] = acc

    return pl.pallas_call(
        body, name="ada_wgrad", grid=(L, N // _ADA_COLS),
        in_specs=[_full((D, B)), pl.BlockSpec((1, B, _ADA_COLS), lambda l, j: (l, 0, j))],
        out_specs=pl.BlockSpec((1, D, _ADA_COLS), lambda l, j: (l, 0, j)),
        out_shape=jax.ShapeDtypeStruct((L, D, N), f32),
        compiler_params=_cp(("parallel", "parallel")),
    )(c_all_t, dmod_cols)


def _adam_math(w, g, m, v):
    m = ADAM_B1 * m + (1.0 - ADAM_B1) * g
    v = ADAM_B2 * v + (1.0 - ADAM_B2) * (g * g)
    m_hat = m / (1.0 - ADAM_B1 ** ADAM_STEP)
    v_hat = v / (1.0 - ADAM_B2 ** ADAM_STEP)
    delta = -ADAM_LR * (m_hat / (jnp.sqrt(v_hat) + ADAM_EPS) + ADAM_WD * w)
    return delta, m, v


def _adamw(w, m, v, g, row_tile, row0=0, outs=None):
    R, C = w.shape
    b0 = row0 // row_tile

    def body(w_ref, m_ref, v_ref, g_ref, _0, _1, _2, _3, g_out, d_out, m_out, v_out):
        gv = g_ref[...]
        delta, mn, vn = _adam_math(w_ref[...], gv, m_ref[...], v_ref[...])
        g_out[...] = gv
        d_out[...] = delta
        m_out[...] = mn
        v_out[...] = vn

    pspec = pl.BlockSpec((row_tile, C), lambda i: (b0 + i, 0))
    gspec = pl.BlockSpec((row_tile, C), lambda i: (i, 0))
    anyspec = pl.BlockSpec(memory_space=pl.ANY)
    shp = jax.ShapeDtypeStruct((R, C), f32)
    if outs is None:
        outs = [lax.empty((R, C), f32) for _ in range(4)]
    return pl.pallas_call(
        body, name="adamw", grid=(g.shape[0] // row_tile,),
        in_specs=[pspec] * 3 + [gspec] + [anyspec] * 4, out_specs=[pspec] * 4, out_shape=[shp] * 4,
        input_output_aliases={4: 0, 5: 1, 6: 2, 7: 3},
        compiler_params=_cp(("parallel",), 40),
    )(w, m, v, g, *outs)


def _pair_sum(g5s, gots, pc):
    n = len(g5s)

    def body(pc_ref, *refs):
        for own, got, out in zip(refs[:n], refs[n:2 * n], refs[2 * n:]):
            out[0, 0] = (own[0, 0, 0].astype(f32) + got[0, 0].astype(f32)).astype(bf16)

    def half(g):
        return pl.BlockSpec((1, 1) + g.shape[-2:], lambda p, pc: (p, 0, 0, 0))

    gs = pltpu.PrefetchScalarGridSpec(
        num_scalar_prefetch=1, grid=(N_CHIPS,),
        in_specs=[pl.BlockSpec((1, 1, 1) + g.shape[-2:], lambda p, pc: (p, 0, pc[1], 0, 0)) for g in g5s]
        + [half(g) for g in gots],
        out_specs=[half(g) for g in gots],
    )
    return pl.pallas_call(
        body, name="pair_sum", grid_spec=gs, out_shape=[jax.ShapeDtypeStruct(g.shape, bf16) for g in gots],
        compiler_params=_cp(("parallel",), 48),
    )(pc, *g5s, *gots)


_SUM_STEPS = 2


def _sum_shards(hsums, recvs, pc):
    n = len(hsums)

    def body(pc_ref, *refs):
        for own, got, out in zip(refs[:n], refs[n:2 * n], refs[2 * n:]):
            acc = own[0, 0].astype(f32)
            for j in range(3):
                acc = acc + got[j, 0].astype(f32)
            out[0, 0] = acc

    def rows(h):
        return (h.shape[2] // _SUM_STEPS, h.shape[3])

    gs = pltpu.PrefetchScalarGridSpec(
        num_scalar_prefetch=1, grid=(_SUM_STEPS,),
        in_specs=[pl.BlockSpec((1, 1) + rows(h), lambda i, pc: (pc[0], 0, i, 0)) for h in hsums]
        + [pl.BlockSpec((3, 1) + rows(h), lambda i, pc: (0, 0, i, 0)) for h in hsums],
        out_specs=[pl.BlockSpec((1, 1) + rows(h), lambda i, pc: (0, pc[1], i, 0)) for h in hsums],
    )
    return pl.pallas_call(
        body, name="sum_shards", grid_spec=gs,
        out_shape=[jax.ShapeDtypeStruct((1, 2) + h.shape[2:], f32) for h in hsums],
        compiler_params=_cp(("parallel",), 48),
    )(pc, *hsums, *recvs)


def _sum8(packs):
    _, R, C = packs.shape
    tr = R // 8 if R % 64 == 0 else R

    def body(p_ref, out_ref):
        acc = p_ref[0]
        for d in range(1, 8):
            acc = acc + p_ref[d]
        out_ref[...] = acc

    return pl.pallas_call(
        body, name="sum8", grid=(R // tr,),
        in_specs=[pl.BlockSpec((8, tr, C), lambda i: (0, i, 0))],
        out_specs=pl.BlockSpec((tr, C), lambda i: (i, 0)),
        out_shape=jax.ShapeDtypeStruct((R, C), f32),
        compiler_params=_cp(("parallel",)),
    )(packs)


def _allgather8(x_shard):
    m_per, n = x_shard.shape

    def body(x_ref, out_ref, send_sems, recv_sems, local_sem):
        x, y, c = lax.axis_index("x"), lax.axis_index("y"), lax.axis_index("c")
        me, sibling = (x, y, c), (x, y, 1 - c)
        chips = [(1 - x, y), (x, 1 - y), (1 - x, 1 - y)]

        def rows(px, py, pc):
            return out_ref.at[pl.ds((4 * px + 2 * py + pc) * m_per, m_per), :]

        def copy(k, block, to, src=None):
            return pltpu.make_async_remote_copy(
                src_ref=rows(*block) if src is None else src, dst_ref=rows(*block),
                send_sem=send_sems.at[k], recv_sem=recv_sems.at[k], device_id=to, device_id_type=MESH)

        mine = pltpu.make_async_copy(x_ref, rows(*me), local_sem)
        mine.start()
        first = [copy(0, me, sibling, src=x_ref)]
        first += [copy(1 + j, me, (*chip, c), src=x_ref) for j, chip in enumerate(chips)]
        for cp in first:
            cp.start()
        passed = [copy(4 + j, (*chip, c), sibling) for j, chip in enumerate(chips)]
        for j, chip in enumerate(chips):
            copy(1 + j, (*chip, c), me).wait_recv()
            passed[j].start()
        copy(0, sibling, me).wait_recv()
        for j, chip in enumerate(chips):
            copy(4 + j, (*chip, 1 - c), me).wait_recv()
        for cp in first + passed:
            cp.wait_send()
        mine.wait()

    return pl.pallas_call(
        body, name="allgather8",
        out_shape=jax.ShapeDtypeStruct((8 * m_per, n), x_shard.dtype),
        in_specs=[pl.BlockSpec(memory_space=pltpu.VMEM)],
        out_specs=pl.BlockSpec(memory_space=pltpu.VMEM),
        scratch_shapes=[pltpu.SemaphoreType.DMA((7,)), pltpu.SemaphoreType.DMA((7,)), pltpu.SemaphoreType.DMA],
        compiler_params=_cp(None, 48),
    )(x_shard)


def _other_chips():
    x, y = lax.axis_index("x"), lax.axis_index("y")
    return [(1 - x, y), (x, 1 - y), (1 - x, 1 - y)]


_HBM = pl.BlockSpec(memory_space=pltpu.HBM)
_SEM = pl.BlockSpec(memory_space=pltpu.SEMAPHORE)
_EFFECT = pltpu.SideEffectType.DATAFLOW_SIDE_EFFECTING


def _gather_copies(srcs, lands, send_sems, recv_sems):
    x, y, c = lax.axis_index("x"), lax.axis_index("y"), lax.axis_index("c")
    return [pltpu.make_async_remote_copy(
        src_ref=srcs[a].at[:, c], dst_ref=lands[a].at[2 * x + y, :, c], send_sem=send_sems.at[3 * a + j],
        recv_sem=recv_sems.at[3 * a + j], device_id=(cx, cy, c), device_id_type=MESH)
        for a in range(len(srcs)) for j, (cx, cy) in enumerate(_other_chips())]


def _gather_start(chunks, after, name):
    sizes = [len(srcs) for srcs, _ in chunks]
    flat = [t for srcs, lands in chunks for t in list(srcs) + list(lands)]
    nflat = len(flat)
    nsem = 2 * len(chunks)

    def body(*refs):
        ins, sems, token = refs[:nflat], refs[nflat + 1:nflat + 1 + nsem], refs[-1]
        off = 0
        for k, n in enumerate(sizes):
            for cp in _gather_copies(ins[off:off + n], ins[off + n:off + 2 * n], sems[2 * k], sems[2 * k + 1]):
                cp.start()
            off += 2 * n
        token[...] = jnp.zeros_like(token)

    res = pl.pallas_call(
        body, name=name,
        out_shape=[pltpu.SemaphoreType.DMA((3 * n,)) for n in sizes for _ in range(2)]
        + [pltpu.HBM(t.shape, t.dtype) for t in flat] + [jax.ShapeDtypeStruct((8, 128), f32)],
        in_specs=[_HBM] * nflat + [pl.BlockSpec(memory_space=pl.ANY)],
        out_specs=[_SEM] * nsem + [_HBM] * nflat + [pl.BlockSpec(memory_space=pltpu.VMEM)],
        input_output_aliases={i: nsem + i for i in range(nflat)},
        compiler_params=pltpu.CompilerParams(has_side_effects=_EFFECT),
    )(*[pltpu.with_memory_space_constraint(t, pltpu.HBM) for t in flat], after)
    out, off = [], nsem
    for k, n in enumerate(sizes):
        out.append((res[2 * k], res[2 * k + 1], res[off:off + n], res[off + n:off + 2 * n]))
        off += 2 * n
    return out, res[-1]


def _gather_wait(send_sems, recv_sems, srcs, lands, after, name):
    n = len(srcs)

    def body(*refs):
        for cp in _gather_copies(refs[:n], refs[n:2 * n], refs[2 * n], refs[2 * n + 1]):
            cp.wait_send()
            cp.wait_recv()

    res = pl.pallas_call(
        body, name=name,
        out_shape=[pltpu.HBM(t.shape, t.dtype) for t in list(srcs) + list(lands)],
        in_specs=[_HBM] * (2 * n) + [_SEM, _SEM] + [pl.BlockSpec(memory_space=pl.ANY)] * len(after),
        out_specs=[_HBM] * (2 * n),
        input_output_aliases={i: i for i in range(2 * n)},
        compiler_params=pltpu.CompilerParams(has_side_effects=_EFFECT),
    )(*srcs, *lands, send_sems, recv_sems, *after)
    return res[n:]


def _gather_forward(lands):
    n = len(lands)

    def body(*refs):
        outs = refs[n:2 * n]
        send_sems, recv_sems = refs[2 * n:]
        x, y, c = lax.axis_index("x"), lax.axis_index("y"), lax.axis_index("c")
        sibling = (x, y, 1 - c)
        copies = []
        for a in range(n):
            for j, (cx, cy) in enumerate(_other_chips()):
                mine = outs[a].at[2 * cx + cy, :, c]
                cp = pltpu.make_async_remote_copy(src_ref=mine, dst_ref=mine, send_sem=send_sems.at[3 * a + j],
                                                  recv_sem=recv_sems.at[3 * a + j], device_id=sibling, device_id_type=MESH)
                cp.start()
                copies.append((cp, a, j, cx, cy))
        for cp, a, j, cx, cy in copies:
            cp.wait_send()
            theirs = outs[a].at[2 * cx + cy, :, 1 - c]
            pltpu.make_async_remote_copy(src_ref=theirs, dst_ref=theirs, send_sem=send_sems.at[3 * a + j],
                                         recv_sem=recv_sems.at[3 * a + j], device_id=sibling, device_id_type=MESH).wait_recv()

    hbm = pl.BlockSpec(memory_space=pl.ANY)
    return pl.pallas_call(
        body, name="gather_forward",
        out_shape=[jax.ShapeDtypeStruct(t.shape, t.dtype) for t in lands],
        in_specs=[hbm] * n, out_specs=[hbm] * n,
        input_output_aliases={a: a for a in range(n)},
        scratch_shapes=[pltpu.SemaphoreType.DMA((3 * n,)), pltpu.SemaphoreType.DMA((3 * n,))],
    )(*lands)


def _pair_exchange(g5s):
    n = len(g5s)

    def body(*refs):
        ins, outs = refs[:n], refs[n:2 * n]
        send_sems, recv_sems = refs[2 * n:]
        c = lax.axis_index("c")
        sibling = (lax.axis_index("x"), lax.axis_index("y"), 1 - c)
        copies = []
        for a in range(n):
            cp = pltpu.make_async_remote_copy(src_ref=ins[a].at[:, :, 1 - c], dst_ref=outs[a], send_sem=send_sems.at[a],
                                              recv_sem=recv_sems.at[a], device_id=sibling, device_id_type=MESH)
            cp.start()
            copies.append(cp)
        for cp in copies:
            cp.wait()

    hbm = pl.BlockSpec(memory_space=pl.ANY)
    return pl.pallas_call(
        body, name="pair_exchange",
        out_shape=[jax.ShapeDtypeStruct(g.shape[:2] + g.shape[3:], g.dtype) for g in g5s],
        in_specs=[hbm] * n, out_specs=[hbm] * n,
        scratch_shapes=[pltpu.SemaphoreType.DMA((n,)), pltpu.SemaphoreType.DMA((n,))],
    )(*g5s)


def _scatter_copies(srcs, lands, send_sems, recv_sems):
    c = lax.axis_index("c")
    return [pltpu.make_async_remote_copy(
        src_ref=srcs[a].at[2 * cx + cy], dst_ref=lands[a].at[j], send_sem=send_sems.at[3 * a + j],
        recv_sem=recv_sems.at[3 * a + j], device_id=(cx, cy, c), device_id_type=MESH)
        for a in range(len(srcs)) for j, (cx, cy) in enumerate(_other_chips())]


def _scatter_start(hsums, name, after=()):
    n = len(hsums)
    na = len(after)

    def body(*refs):
        srcs, lands = refs[:n], refs[n:2 * n]
        send_sems, recv_sems = refs[2 * n + na], refs[2 * n + na + 1]
        for cp in _scatter_copies(srcs, lands, send_sems, recv_sems):
            cp.start()
        refs[-1][...] = jnp.zeros_like(refs[-1])

    lands = [lax.empty((3,) + g.shape[1:], g.dtype) for g in hsums]
    res = pl.pallas_call(
        body, name=name,
        out_shape=[pltpu.SemaphoreType.DMA((3 * n,)), pltpu.SemaphoreType.DMA((3 * n,))]
        + [pltpu.HBM(g.shape, g.dtype) for g in hsums] + [pltpu.HBM(g.shape, g.dtype) for g in lands]
        + [jax.ShapeDtypeStruct((8, 128), f32)],
        in_specs=[_HBM] * (2 * n) + [pl.BlockSpec(memory_space=pl.ANY)] * na,
        out_specs=[_SEM, _SEM] + [_HBM] * (2 * n) + [pl.BlockSpec(memory_space=pltpu.VMEM)],
        input_output_aliases={i: i + 2 for i in range(2 * n)},
        compiler_params=pltpu.CompilerParams(has_side_effects=_EFFECT),
    )(*[pltpu.with_memory_space_constraint(t, pltpu.HBM) for t in list(hsums) + lands], *after)
    return (res[0], res[1], res[2:2 + n], res[2 + n:2 + 2 * n]), res[-1]


def _scatter_wait(send_sems, recv_sems, srcs, lands, after, name):
    n = len(srcs)
    extra = list(after)

    def body(*refs):
        s_refs, l_refs = refs[:n], refs[n:2 * n]
        ss, rs = refs[2 * n], refs[2 * n + 1]
        for cp in _scatter_copies(s_refs, l_refs, ss, rs):
            cp.wait_send()
            cp.wait_recv()

    res = pl.pallas_call(
        body, name=name,
        out_shape=[pltpu.HBM(g.shape, g.dtype) for g in srcs] + [pltpu.HBM(g.shape, g.dtype) for g in lands],
        in_specs=[_HBM] * (2 * n) + [_SEM, _SEM] + [pl.BlockSpec(memory_space=pl.ANY)] * len(extra),
        out_specs=[_HBM] * (2 * n),
        input_output_aliases={i: i for i in range(2 * n)},
        compiler_params=pltpu.CompilerParams(has_side_effects=_EFFECT),
    )(*srcs, *lands, send_sems, recv_sems, *extra)
    return res[:n], res[n:]


def _swap_halves(fulls):
    n = len(fulls)

    def body(*refs):
        ins, outs = refs[:n], refs[n:2 * n]
        send_sems, recv_sems = refs[2 * n:]
        c = lax.axis_index("c")
        sibling = (lax.axis_index("x"), lax.axis_index("y"), 1 - c)
        copies = []
        for a in range(n):
            cp = pltpu.make_async_remote_copy(src_ref=outs[a].at[:, c], dst_ref=outs[a].at[:, c], send_sem=send_sems.at[a],
                                              recv_sem=recv_sems.at[a], device_id=sibling, device_id_type=MESH)
            cp.start()
            copies.append(cp)
        for a, cp in enumerate(copies):
            cp.wait_send()
            theirs = outs[a].at[:, 1 - c]
            pltpu.make_async_remote_copy(src_ref=theirs, dst_ref=theirs, send_sem=send_sems.at[a], recv_sem=recv_sems.at[a],
                                         device_id=sibling, device_id_type=MESH).wait_recv()

    hbm = pl.BlockSpec(memory_space=pl.ANY)
    return pl.pallas_call(
        body, name="swap_halves",
        out_shape=[jax.ShapeDtypeStruct(p.shape, p.dtype) for p in fulls],
        in_specs=[hbm] * n, out_specs=[hbm] * n,
        input_output_aliases={a: a for a in range(n)},
        scratch_shapes=[pltpu.SemaphoreType.DMA((n,)), pltpu.SemaphoreType.DMA((n,))],
    )(*fulls)


def _to_segments(t):
    s, c = t.shape
    return t.reshape(SCAN_SEG, s // SCAN_SEG, c).transpose(1, 0, 2).reshape(s, c)


def _from_segments(t):
    s, c = t.shape
    return t.reshape(s // SCAN_SEG, SCAN_SEG, c).transpose(1, 0, 2).reshape(s, c)


def _ssm_operators(a_re, a_im, log_dt, b_re, b_im, c_re, c_im):
    lam = lax.complex(a_re, a_im)
    dt = jnp.exp(log_dt)[:, None]
    a_bar = jnp.exp(lam * dt)
    b_bar = ((a_bar - 1.0) / lam)[:, :, None] * lax.complex(b_re, b_im)
    eye = jnp.eye(N_GROUPS, dtype=f32)

    def embed_b(t):
        return (jnp.transpose(t, (0, 2, 1))[:, :, None, :] * eye[:, None, :, None]).reshape(D_SSM, D_STATE)

    def embed_c(t):
        return (jnp.transpose(t, (0, 2, 1))[:, :, None, :] * eye[:, None, :, None]).reshape(D_STATE, D_SSM)

    a2 = jnp.stack([a_bar.real.reshape(D_STATE), a_bar.imag.reshape(D_STATE)])
    return a2, embed_b(b_bar.real), embed_b(b_bar.imag), embed_c(c_re), embed_c(c_im)


def _local_step(x, target, mod, small, ffn_weights, mix_weights, grads_done):
    table = jnp.asarray(_bucket_table())
    bias = _bias_fwd(small["rel_bias"], table)
    L = DEPTH
    saved = []
    ssm_ops = []
    for l in range(L):
        sv = {}
        m9 = mod[l]
        sv["x0"] = x
        sv["w0"] = ffn_weights(l, 0, x)
        x, sv["f0"], sv["g0"], sv["u0"], sv["h0"] = _ffn_fwd(x, m9[0:3], *sv["w0"], 0, small["ln_g"][l, 0:1], small["ln_b"][l, 0:1])
        sv["x1"] = x
        sv["w1"] = mix_weights(l, x)
        *qkv, z_rest, sv["h1"] = _mix_in_fwd(x, m9[3:6], sv["w1"][0], 0)
        S = x.shape[0]
        qkv = [t.reshape(3, S, D_ATT) for t in qkv]
        att = [_att_fwd(qkv[b], bias, b) for b in range(3)]
        y_att, lse3 = _att_merge([att[b][0].reshape(d, S // d, D_ATT) for b, d in enumerate(DILATIONS)],
                                 [att[b][1].reshape(d, S // d, _LANES) for b, d in enumerate(DILATIONS)])
        sv.update(qkv=qkv, lse=[a[1] for a in att], lse3=lse3, y_att=y_att)

        prm = tuple(small[k][l] for k in ("ssm_a_re", "ssm_a_im", "ssm_log_dt", "ssm_b_re", "ssm_b_im", "ssm_c_re", "ssm_c_im"))
        (a2, bre, bim, cre, cim), ops_vjp = jax.vjp(_ssm_operators, *prm)
        ssm_ops.append(ops_vjp)
        u_ssm = _to_segments(z_rest[:, :D_SSM])
        sr, si = _ssm_states(u_ssm, bre, bim, a2)
        dskip = small["ssm_d"][l][None, :]
        glu_b = small["glu_b"][l][None, :]
        out_seg, y_seg = _ssm_out(sr, si, u_ssm, cre, cim, dskip, small["glu_w"][l], glu_b)
        y_ssm = _from_segments(out_seg)
        sv.update(a2=a2, bre=bre, bim=bim, cre=cre, cim=cim, u_ssm=u_ssm, sr=sr, si=si, y_seg=y_seg, y_ssm=y_ssm)

        u_pool = jnp.concatenate([jnp.zeros((POOL_HALO, D_POOL), f32), z_rest[:, D_SSM:]])
        y_pool = _pool_fwd(u_pool, small["pool_w"][l], small["pool_scale"][l][None, :])
        sv.update(u_pool=u_pool, y_pool=y_pool)

        x, sv["ymix"] = _mix_out_fwd(x, y_att, y_ssm, y_pool, m9[3:6], sv["w1"][1], 0, small["ln_g"][l, 1:2], small["ln_b"][l, 1:2])
        sv["x2"] = x
        sv["w2"] = ffn_weights(l, 1, x)
        x, sv["f2"], sv["g2"], sv["u2"], sv["h2"] = _ffn_fwd(x, m9[6:9], *sv["w2"], 0, small["ln_g"][l, 2:3], small["ln_b"][l, 2:3])
        saved.append(sv)

    loss, dx = _loss_head(x, target)

    dmod = [None] * L
    dln_g = [None] * L
    dln_b = [None] * L
    sg = {k: [None] * L for k in ("ssm_a_re", "ssm_a_im", "ssm_log_dt", "ssm_b_re", "ssm_b_im", "ssm_c_re", "ssm_c_im",
                                  "ssm_d", "glu_w", "glu_b", "pool_w", "pool_scale")}
    dbias_tot = None
    order_after = jnp.zeros((), f32)
    for l in reversed(range(L)):
        sv = saved[l]
        m9 = mod[l] + order_after

        def fresh(like):
            return [lax.empty(t.shape, bf16) for t in like]

        dx, dg, du, a, df, dm2, dlg2, dlb2 = _ffn_bwd(dx, sv["x2"], sv["f2"], sv["g2"], sv["u2"], m9[6:9], *sv["w2"], 0,
                                                     small["ln_g"][l, 2:3])
        g_ffn1 = _ffn_wgrad(sv["h2"], dg, du, a, df, *fresh(sv["w2"]), 0)
        dxr, d_att, d_ssm, d_pool, dgate1, dlg1, dlb1, g_w_out = _mix_out_bwd(
            dx, sv["x1"], sv["ymix"], sv["y_att"], sv["y_ssm"], sv["y_pool"], m9[3:6], sv["w1"][1], 0, small["ln_g"][l, 1:2],
            fresh(sv["w1"])[1])
        S = d_att.shape[0]
        merged = _att_merge_bwd(d_att, sv["y_att"], sv["lse3"])
        dqkv, dbias = [], []
        for b, d in enumerate(DILATIONS):
            dq_b, db_b = _att_bwd(sv["qkv"][b], merged[b].reshape(S, D_ATT), sv["lse"][b], merged[3 + b].reshape(S, _LANES), bias, b)
            dqkv.append(dq_b.reshape(3, d, S // d, D_ATT))
            dbias.append(db_b)
        dbias = jnp.stack(dbias)
        dbias_tot = dbias if dbias_tot is None else dbias_tot + dbias
        d_seg = _to_segments(d_ssm)
        dskip = small["ssm_d"][l][None, :]
        glu_b = small["glu_b"][l][None, :]
        dy_seg, du_skip, dcre, dcim, dd, dglu_b, dglu_w = _ssm_out_bwd(
            d_seg, sv["y_seg"], sv["u_ssm"], sv["sr"], sv["si"], dskip, small["glu_w"][l], glu_b)
        du_seg, dbre, dbim, da2 = _ssm_states_bwd(dy_seg, du_skip, sv["u_ssm"], sv["sr"], sv["si"], sv["cre"], sv["cim"],
                                                  sv["bre"], sv["bim"], sv["a2"])
        d_prm = ssm_ops[l]((da2, dbre, dbim, dcre, dcim))
        for k, v in zip(("ssm_a_re", "ssm_a_im", "ssm_log_dt", "ssm_b_re", "ssm_b_im", "ssm_c_re", "ssm_c_im"), d_prm):
            sg[k][l] = v
        sg["ssm_d"][l] = dd[0]
        sg["glu_b"][l] = dglu_b[0]
        sg["glu_w"][l] = dglu_w
        du_ssm = _from_segments(du_seg)
        dyp = jnp.concatenate([d_pool, jnp.zeros((POOL_HALO, D_POOL), f32)])
        du_pool, dpw, dps = _pool_bwd(dyp, sv["u_pool"], small["pool_w"][l], small["pool_scale"][l][None, :])
        sg["pool_w"][l] = dpw
        sg["pool_scale"][l] = dps[0]
        d_rest = jnp.concatenate([du_ssm, du_pool], axis=1).astype(bf16)
        dx, dm1, dz = _mix_in_bwd(dqkv, d_rest, dxr, sv["x1"], m9[3:6], sv["w1"][0], 0)
        g_w_in = _mix_in_wgrad(sv["h1"], dz, fresh(sv["w1"])[0], 0)
        ffn_names = ("ffn_w_gate", "ffn_w_up", "ffn_w_down")
        m9 = m9 + grads_done(l, 1, list(zip(ffn_names, [(2 * l + 1) * FF_SHARD] * 3, g_ffn1))
                             + [("w_in", l * D_MODEL, g_w_in), ("w_out", l * 256, g_w_out)])
        dm1 = jnp.concatenate([dm1[0:2], dgate1])
        dx, dg, du, a, df, dm0, dlg0, dlb0 = _ffn_bwd(dx, sv["x0"], sv["f0"], sv["g0"], sv["u0"], m9[0:3], *sv["w0"], 0,
                                                     small["ln_g"][l, 0:1])
        g_ffn0 = _ffn_wgrad(sv["h0"], dg, du, a, df, *fresh(sv["w0"]), 0)
        order_after = grads_done(l, 0, list(zip(ffn_names, [2 * l * FF_SHARD] * 3, g_ffn0)))
        dmod[l] = jnp.concatenate([dm0, dm1, dm2])
        dln_g[l] = jnp.concatenate([dlg0, dlg1, dlg2])
        dln_b[l] = jnp.concatenate([dlb0, dlb1, dlb2])

    small_grads = {k: jnp.stack(v) for k, v in sg.items()}
    small_grads["rel_bias"] = _bias_bwd(dbias_tot, table)
    small_grads["ln_g"] = jnp.stack(dln_g)
    small_grads["ln_b"] = jnp.stack(dln_b)
    return loss, dx, jnp.stack(dmod), small_grads


_TILE_ELEMS = 8 * 128


def _pack_rows(shapes):
    out, row = [], 0
    for s in shapes:
        nr = -(-int(np.prod(s)) // _TILE_ELEMS) * 8
        out.append((row, nr))
        row += nr
    return out


def _pack(arrs):
    parts = []
    for a in arrs:
        flat = a.reshape(-1).astype(f32)
        npad = -(-flat.shape[0] // _TILE_ELEMS) * _TILE_ELEMS
        parts.append(jnp.pad(flat, (0, npad - flat.shape[0])).reshape(npad // 128, 128))
    return jnp.concatenate(parts, axis=0)


def _unpack(buf, shapes):
    return [buf[row:row + nr].reshape(-1)[:int(np.prod(s))].reshape(s) for s, (row, nr) in zip(shapes, _pack_rows(shapes))]


_REPL = ("rel_bias", "ada_b", "ssm_a_re", "ssm_a_im", "ssm_log_dt", "ssm_b_re", "ssm_b_im", "ssm_c_re", "ssm_c_im",
         "ssm_d", "glu_b", "pool_w", "pool_scale")
_SMALL_SHARDED = ("ln_g", "ln_b", "glu_w")
_BIG = ("ffn_w_gate", "ffn_w_up", "ffn_w_down", "w_in", "w_out")
_ORDER = ("rel_bias", "ada_w", "ada_b", "ln_g", "ln_b", "ffn_w_gate", "ffn_w_up", "ffn_w_down", "w_in", "w_out",
          "ssm_a_re", "ssm_a_im", "ssm_log_dt", "ssm_b_re", "ssm_b_im", "ssm_c_re", "ssm_c_im", "ssm_d", "glu_w",
          "glu_b", "pool_w", "pool_scale")


def kernel(x, c, rel_bias, ada_w, ada_b, ln_g, ln_b, ffn_w_gate, ffn_w_up, ffn_w_down, w_in, w_out, ssm_a_re, ssm_a_im, ssm_log_dt, ssm_b_re, ssm_b_im, ssm_c_re, ssm_c_im, ssm_d, glu_w, glu_b, pool_w, pool_scale, loss_target, m_rel_bias, m_ada_w, m_ada_b, m_ln_g, m_ln_b, m_ffn_w_gate, m_ffn_w_up, m_ffn_w_down, m_w_in, m_w_out, m_ssm_a_re, m_ssm_a_im, m_ssm_log_dt, m_ssm_b_re, m_ssm_b_im, m_ssm_c_re, m_ssm_c_im, m_ssm_d, m_glu_w, m_glu_b, m_pool_w, m_pool_scale, v_rel_bias, v_ada_w, v_ada_b, v_ln_g, v_ln_b, v_ffn_w_gate, v_ffn_w_up, v_ffn_w_down, v_w_in, v_w_out, v_ssm_a_re, v_ssm_a_im, v_ssm_log_dt, v_ssm_b_re, v_ssm_b_im, v_ssm_c_re, v_ssm_c_im, v_ssm_d, v_glu_w, v_glu_b, v_pool_w, v_pool_scale):
    args = dict(locals())
    w = {k: args[k] for k in _ORDER}
    m = {k: args["m_" + k] for k in _ORDER}
    v = {k: args["v_" + k] for k in _ORDER}
    L, D = DEPTH, D_MODEL
    ax, ay, ac = lax.axis_index("x"), lax.axis_index("y"), lax.axis_index("c")
    p_me = 2 * ax + ay
    dev = 4 * ax + 2 * ay + ac

    transposed = ("ffn_w_gate", "ffn_w_up")
    for d in (w, m, v):
        for name in transposed:
            d[name] = jnp.swapaxes(d[name], 2, 3)

    def halves(t):
        return t.astype(bf16).reshape(1, 2, t.shape[0] // 2, t.shape[1])

    def landing(src):
        return lax.dynamic_update_slice(lax.empty((N_CHIPS,) + src.shape, bf16), src[None], (p_me, 0, 0, 0, 0))

    chunk_keys = [("ffn", 0, 0), ("mix", 0), ("ffn", 0, 1), ("ffn", 1, 0), ("mix", 1), ("ffn", 1, 1)]
    chunk_srcs = []
    for key in chunk_keys:
        if key[0] == "ffn":
            chunk_srcs.append([halves(w[name][key[1], key[2]]) for name in ("ffn_w_gate", "ffn_w_up", "ffn_w_down")])
        else:
            chunk_srcs.append([halves(w_in[key[1]]), halves(w_out[key[1]])])

    pack = _pack([c, ln_g, ln_b, glu_w])
    rows = pack.shape[0]
    allp = _allgather8(pack).reshape(8, rows, 128)
    chunks = [(srcs, [landing(t) for t in srcs]) for srcs in chunk_srcs]
    first_in_flight, first_begun = _gather_start(chunks[:1], allp, "gather_start_first")
    c_all = allp[:, :8].reshape(8, D) + first_begun[0, 0]
    by_chip = allp[0::2]

    fwd_rows = _pack_rows([c.shape, ln_g.shape, ln_b.shape, glu_w.shape])

    def sharded(part, shape, axis):
        row0, nrows = fwd_rows[part]
        t = by_chip[:, row0:row0 + nrows].reshape(N_CHIPS, -1)[:, :int(np.prod(shape))].reshape((N_CHIPS,) + shape)
        return jnp.concatenate([t[p] for p in range(N_CHIPS)], axis=axis)

    ln_g_full = sharded(1, ln_g.shape, 2)
    ln_b_full = sharded(2, ln_b.shape, 2)
    glu_w_full = sharded(3, glu_w.shape, 1)

    ncol = ada_w.shape[-1]
    ada_b_cols = lax.dynamic_slice_in_dim(ada_b, p_me * ncol, ncol, axis=1)[:, None, :]
    mod_part = _ada_fwd(c_all, ada_w, ada_b_cols)
    mrows = L * 8 * ncol // 128
    mod_all = _allgather8(mod_part.reshape(mrows, 128)).reshape(8, L, 8, ncol)
    mod_mine = lax.dynamic_index_in_dim(mod_all, dev, axis=2, keepdims=False)
    mod = jnp.concatenate([mod_mine[2 * p] for p in range(N_CHIPS)], axis=-1).reshape(L, 9, D)

    rest_in_flight, rest_begun = _gather_start(chunks[1:], mod, "gather_start_rest")
    in_flight = first_in_flight + rest_in_flight

    def gathered(key, after):
        k = chunk_keys.index(key)
        lands = _gather_forward(_gather_wait(*in_flight[k], [after, rest_begun], "gather_wait_%d" % k))
        return [t.reshape(N_CHIPS, 1, 2 * t.shape[3], t.shape[4]) for t in lands]

    pc = jnp.stack([p_me, ac]).astype(jnp.int32)
    groups = {}
    scattering = {}

    def start_group(tag, after=()):
        g5 = [g.reshape(g.shape[:2] + (2, g.shape[2] // 2, g.shape[3])) for _, _, g in groups[tag]]
        hsum = _pair_sum(g5, _pair_exchange(g5), pc)
        scattering[tag], begun = _scatter_start(hsum, "scatter_start_%s" % tag, after)
        return begun

    def grads_done(l, s, grads):
        if l == 1:
            groups.setdefault("l1", []).extend(grads)
            return start_group("l1")[0, 0] if s == 0 else jnp.zeros((), f32)
        groups["l0a" if s == 1 else "l0b"] = grads
        return start_group("l0a")[0, 0] if s == 1 else jnp.zeros((), f32)

    small = {k: w[k] for k in _REPL if k != "ada_b"}
    small.update(ln_g=ln_g_full, ln_b=ln_b_full, glu_w=glu_w_full)
    loss_dev, grad_x, dmod, sgrads = _local_step(
        x[0], loss_target[0], mod, small, lambda l, s, after: gathered(("ffn", l, s), after),
        lambda l, after: gathered(("mix", l), after), grads_done)
    loss = lax.psum(loss_dev[0, 0], ("x", "y", "c"))

    names = ("rel_bias", "ln_g", "ln_b", "ssm_a_re", "ssm_a_im", "ssm_log_dt", "ssm_b_re", "ssm_b_im", "ssm_c_re",
             "ssm_c_im", "ssm_d", "glu_w", "glu_b", "pool_w", "pool_scale")
    gpack = _pack([dmod] + [sgrads[k] for k in names])
    grows = gpack.shape[0]
    gall = _allgather8(gpack).reshape(8, grows, 128)
    l0b_begun = start_group("l0b", (gall,))
    gsum = _unpack(_sum8(gall), [(L, 9 * D)] + [sgrads[k].shape for k in names])
    red = dict(zip(("ada_b",) + names, gsum))
    red["ln_g"] = lax.dynamic_slice_in_dim(red["ln_g"], p_me * 256, 256, axis=2)
    red["ln_b"] = lax.dynamic_slice_in_dim(red["ln_b"], p_me * 256, 256, axis=2)
    red["glu_w"] = lax.dynamic_slice_in_dim(red["glu_w"], p_me * 64, 64, axis=1)

    dmod_all = gall[:, :L * 9 * D // 128].reshape(8, L, 9 * D)
    dmod_cols = jnp.transpose(lax.dynamic_slice_in_dim(dmod_all, p_me * ncol, ncol, axis=2), (1, 0, 2))
    g_ada_w = _ada_wgrad(jnp.transpose(c_all), dmod_cols)

    out_g, out_d, out_m, out_v = {}, {}, {}, {}
    r2 = (L * D, ncol)
    res = _adamw(ada_w.reshape(r2), m["ada_w"].reshape(r2), v["ada_w"].reshape(r2), g_ada_w.reshape(r2), 128)
    out_g["ada_w"], out_d["ada_w"], out_m["ada_w"], out_v["ada_w"] = [t.reshape(ada_w.shape) for t in res]

    small_names = _REPL + _SMALL_SHARDED
    wp = _pack([w[k] for k in small_names])
    res_small = _adamw(wp, _pack([m[k] for k in small_names]), _pack([v[k] for k in small_names]),
                       _pack([red[k] for k in small_names]), wp.shape[0])
    for t, dst in zip(res_small, (out_g, out_d, out_m, out_v)):
        for k, a in zip(small_names, _unpack(t, [w[k].shape for k in small_names])):
            dst[k] = a

    row_tile = dict(zip(_BIG, (352, 352, 352, 256, 256)))
    big = {name: None for name in _BIG}
    after = [grad_x, res_small[1], res[1], l0b_begun]
    for tag in ("l1", "l0a", "l0b"):
        hsum, recv = _scatter_wait(*scattering[tag], after, "scatter_wait_%s" % tag)
        full = _swap_halves(_sum_shards(hsum, recv, pc))
        for (name, row0, _), g in zip(groups[tag], full):
            shp = w[name].shape
            r2 = (int(np.prod(shp[:-1])), shp[-1])
            big[name] = _adamw(w[name].reshape(r2), m[name].reshape(r2), v[name].reshape(r2), g.reshape(-1, shp[-1]),
                               row_tile[name], row0, big[name])
        after = [big[name][1] for name, _, _ in groups[tag]]
    for name in _BIG:
        res = [t.reshape(w[name].shape) for t in big[name]]
        out_g[name], out_d[name], out_m[name], out_v[name] = [jnp.swapaxes(t, 2, 3) for t in res] if name in transposed else res

    return (loss, grad_x[None], *[out_g[k] for k in _ORDER], *[out_d[k] for k in _ORDER],
            *[out_m[k] for k in _ORDER], *[out_v[k] for k in _ORDER])
```

```python
import functools
import math

import numpy as np
import jax
import jax.numpy as jnp
from jax import lax
from jax.experimental import pallas as pl
from jax.experimental.pallas import tpu as pltpu

f32 = jnp.float32
bf16 = jnp.bfloat16
MESH = pl.DeviceIdType.MESH

D_MODEL = 1024
SEQ = 2048
DEPTH = 2
HEAD_DIM = 64
N_HEADS = 8
D_ATT = 512
DILATIONS = (1, 4, 16)
BLOCKS_PER_RESIDUE = (16, 4, 1)
ATT_BLOCK = 128
N_UNITS = SEQ // ATT_BLOCK
N_GROUPS = 16
SSM_GROUP = 16
SSM_STATE = 64
D_SSM = 256
D_STATE = N_GROUPS * SSM_STATE
POOL_WINDOWS = (2, 4, 8, 16)
POOL_GROUP = 64
D_POOL = 256
POOL_HALO = 16
D_FF = 2816
N_BUCKETS = 32
MAX_DISTANCE = 2048
ALPHA = (2 * DEPTH) ** 0.25
FFN_RES = 0.5
LN_EPS = 1e-5
NEG = -1e30
N_CHIPS = 4
FF_SHARD = D_FF // N_CHIPS
SCAN_SEG = 8
SCAN_STEPS = SEQ // SCAN_SEG

ADAM_LR, ADAM_B1, ADAM_B2, ADAM_EPS, ADAM_WD, ADAM_STEP = 0.001, 0.9, 0.999, 1e-08, 0.01, 10

TOK_TILE = 512


def _cp(dims=None, vmem_mb=None):
    kw = {}
    if dims is not None:
        kw["dimension_semantics"] = dims
    if vmem_mb is not None:
        kw["vmem_limit_bytes"] = vmem_mb << 20
    return pltpu.CompilerParams(**kw)


def _dot(a, b):
    return jnp.dot(a, b, preferred_element_type=f32)


def _dot_nt(a, b):
    return lax.dot_general(a, b, (((1,), (1,)), ((), ())), preferred_element_type=f32)


def _dot_tn(a, b):
    return lax.dot_general(a, b, (((0,), (0,)), ((), ())), preferred_element_type=f32)


def _ln_stats(v):
    mu = jnp.mean(v, -1, keepdims=True)
    d = v - mu
    var = jnp.mean(d * d, -1, keepdims=True)
    rstd = lax.rsqrt(var + LN_EPS)
    return d * rstd, rstd


def _ln_bwd(dxh, xh, rstd):
    return rstd * (dxh - jnp.mean(dxh, -1, keepdims=True) - xh * jnp.mean(dxh * xh, -1, keepdims=True))


_GELU_C = math.sqrt(2.0 / math.pi)


def _gelu(y):
    return 0.5 * y * (1.0 + jnp.tanh(_GELU_C * (y + 0.044715 * y * y * y)))


def _gelu_grad(y):
    t = jnp.tanh(_GELU_C * (y + 0.044715 * y * y * y))
    return 0.5 * (1.0 + t) + 0.5 * y * (1.0 - t * t) * (_GELU_C * (1.0 + 3 * 0.044715 * y * y))


def _full(shape):
    return pl.BlockSpec(shape, lambda *_: (0,) * len(shape))


def _hbm(*args):
    return [pltpu.with_memory_space_constraint(a, pltpu.HBM) if getattr(a, "ndim", 0) >= 2 else a for a in args]


def _ffn_fwd(x, mod3, wg, wu, wd, ls, lng, lnb):
    S, D = x.shape
    Fs = wg.shape[-2]
    ts = TOK_TILE

    def body(x_ref, mod_ref, wg_ref, wu_ref, wd_ref, lng_ref, lnb_ref, xo_ref, f_ref, g_ref, u_ref, h_ref, acc_sc):
        j = pl.program_id(1)

        @pl.when(j == 0)
        def _():
            xh, _ = _ln_stats(x_ref[...])
            h_ref[...] = (xh * (1.0 + mod_ref[1:2, :]) + mod_ref[0:1, :]).astype(bf16)
            acc_sc[...] = jnp.zeros_like(acc_sc)

        h = h_ref[...]
        g = _dot_nt(h, wg_ref[0, 0])
        u = _dot_nt(h, wu_ref[0, 0])
        g_ref[0] = g.astype(bf16)
        u_ref[0] = u.astype(bf16)
        a = (g * jax.nn.sigmoid(g) * u).astype(bf16)
        acc_sc[...] += _dot(a, wd_ref[0, 0])

        @pl.when(j == N_CHIPS - 1)
        def _():
            f = acc_sc[...]
            f_ref[...] = f
            r = ALPHA * x_ref[...] + (FFN_RES * mod_ref[2:3, :]) * f
            rh, _ = _ln_stats(r)
            xo_ref[...] = rh * lng_ref[...] + lnb_ref[...]

    tok = pl.BlockSpec((ts, D), lambda i, j: (i, 0))
    wrow = pl.BlockSpec((1, 1, Fs, D), lambda i, j: (j, ls, 0, 0))
    hid = pl.BlockSpec((1, ts, Fs), lambda i, j: (j, i, 0))
    return pl.pallas_call(
        body, name="ffn_fwd", grid=(S // ts, N_CHIPS),
        in_specs=[tok, _full((3, D)), wrow, wrow, wrow, _full((1, D)), _full((1, D))],
        out_specs=[tok, tok, hid, hid, tok],
        out_shape=[jax.ShapeDtypeStruct((S, D), f32), jax.ShapeDtypeStruct((S, D), f32),
                   jax.ShapeDtypeStruct((N_CHIPS, S, Fs), bf16), jax.ShapeDtypeStruct((N_CHIPS, S, Fs), bf16),
                   jax.ShapeDtypeStruct((S, D), bf16)],
        scratch_shapes=[pltpu.VMEM((ts, D), f32)],
        compiler_params=_cp(("parallel", "arbitrary"), 56),
    )(*_hbm(x, mod3, wg, wu, wd, lng, lnb))


def _ffn_bwd(dxo, x, f, g, u, mod3, wg, wu, wd, ls, lng):
    S, D = x.shape
    Fs = wg.shape[-2]
    ts = TOK_TILE

    def body(dxo_ref, x_ref, f_ref, g_ref, u_ref, mod_ref, wg_ref, wu_ref, wd_ref, lng_ref,
             dx_ref, dg_ref, du_ref, a_ref, df_ref, dmod_ref, dlng_ref, dlnb_ref,
             dr_sc, df_sc, acc_sc):
        i = pl.program_id(0)
        j = pl.program_id(1)

        @pl.when((i == 0) & (j == 0))
        def _():
            dmod_ref[...] = jnp.zeros_like(dmod_ref)
            dlng_ref[...] = jnp.zeros_like(dlng_ref)
            dlnb_ref[...] = jnp.zeros_like(dlnb_ref)

        @pl.when(j == 0)
        def _():
            xv = x_ref[...]
            fv = f_ref[...]
            gate = mod_ref[2:3, :]
            rh, rstd = _ln_stats(ALPHA * xv + (FFN_RES * gate) * fv)
            dy = dxo_ref[...]
            dlng_ref[...] += jnp.sum(dy * rh, 0, keepdims=True)
            dlnb_ref[...] += jnp.sum(dy, 0, keepdims=True)
            dr = _ln_bwd(dy * lng_ref[...], rh, rstd)
            dr_sc[...] = dr
            dmod_ref[2:3, :] += jnp.sum(FFN_RES * dr * fv, 0, keepdims=True)
            df = ((FFN_RES * gate) * dr).astype(bf16)
            df_sc[...] = df
            df_ref[...] = df
            acc_sc[...] = jnp.zeros_like(acc_sc)

        da = _dot_nt(df_sc[...], wd_ref[0, 0])
        gv = g_ref[0].astype(f32)
        uv = u_ref[0].astype(f32)
        sg = jax.nn.sigmoid(gv)
        si = gv * sg
        a_ref[0] = (si * uv).astype(bf16)
        dgv = (da * uv * (sg * (1.0 + gv * (1.0 - sg)))).astype(bf16)
        duv = (da * si).astype(bf16)
        dg_ref[0] = dgv
        du_ref[0] = duv
        acc_sc[...] += _dot(dgv, wg_ref[0, 0]) + _dot(duv, wu_ref[0, 0])

        @pl.when(j == N_CHIPS - 1)
        def _():
            dh = acc_sc[...]
            xh, rstd0 = _ln_stats(x_ref[...])
            dmod_ref[0:1, :] += jnp.sum(dh, 0, keepdims=True)
            dmod_ref[1:2, :] += jnp.sum(dh * xh, 0, keepdims=True)
            dx_ref[...] = _ln_bwd(dh * (1.0 + mod_ref[1:2, :]), xh, rstd0) + ALPHA * dr_sc[...]

    tok = pl.BlockSpec((ts, D), lambda i, j: (i, 0))
    wrow = pl.BlockSpec((1, 1, Fs, D), lambda i, j: (j, ls, 0, 0))
    hid = pl.BlockSpec((1, ts, Fs), lambda i, j: (j, i, 0))
    hid_shape = jax.ShapeDtypeStruct((N_CHIPS, S, Fs), bf16)
    return pl.pallas_call(
        body, name="ffn_bwd", grid=(S // ts, N_CHIPS),
        in_specs=[tok, tok, tok, hid, hid, _full((3, D)), wrow, wrow, wrow, _full((1, D))],
        out_specs=[tok, hid, hid, hid, tok, _full((3, D)), _full((1, D)), _full((1, D))],
        out_shape=[jax.ShapeDtypeStruct((S, D), f32), hid_shape, hid_shape, hid_shape,
                   jax.ShapeDtypeStruct((S, D), bf16),
                   jax.ShapeDtypeStruct((3, D), f32), jax.ShapeDtypeStruct((1, D), f32), jax.ShapeDtypeStruct((1, D), f32)],
        scratch_shapes=[pltpu.VMEM((ts, D), f32), pltpu.VMEM((ts, D), bf16), pltpu.VMEM((ts, D), f32)],
        compiler_params=_cp(("arbitrary", "arbitrary"), 56),
    )(*_hbm(dxo, x, f, g, u, mod3, wg, wu, wd, lng))


def _ffn_wgrad(h, dg, du, a, df, gwg, gwu, gwd, ls):
    S, D = h.shape
    Fs = dg.shape[-1]
    tk = TOK_TILE
    nk = S // tk

    def body(h_ref, dg_ref, du_ref, a_ref, df_ref, _g0, _g1, _g2, gwg_ref, gwu_ref, gwd_ref, ag_sc, au_sc, ad_sc):
        k = pl.program_id(1)

        @pl.when(k == 0)
        def _():
            ag_sc[...] = jnp.zeros_like(ag_sc)
            au_sc[...] = jnp.zeros_like(au_sc)
            ad_sc[...] = jnp.zeros_like(ad_sc)

        hv = h_ref[...]
        ag_sc[...] += _dot_tn(dg_ref[0], hv)
        au_sc[...] += _dot_tn(du_ref[0], hv)
        ad_sc[...] += _dot_tn(a_ref[0], df_ref[...])

        @pl.when(k == nk - 1)
        def _():
            gwg_ref[0, 0] = ag_sc[...].astype(bf16)
            gwu_ref[0, 0] = au_sc[...].astype(bf16)
            gwd_ref[0, 0] = ad_sc[...].astype(bf16)

    tok = pl.BlockSpec((tk, D), lambda p, k: (k, 0))
    hid = pl.BlockSpec((1, tk, Fs), lambda p, k: (p, k, 0))
    anyspec = pl.BlockSpec(memory_space=pl.ANY)
    orow = pl.BlockSpec((1, 1, Fs, D), lambda p, k: (p, ls, 0, 0))
    return pl.pallas_call(
        body, name="ffn_wgrad", grid=(N_CHIPS, nk),
        in_specs=[tok, hid, hid, hid, tok, anyspec, anyspec, anyspec],
        out_specs=[orow, orow, orow],
        out_shape=[jax.ShapeDtypeStruct(gwg.shape, bf16), jax.ShapeDtypeStruct(gwu.shape, bf16),
                   jax.ShapeDtypeStruct(gwd.shape, bf16)],
        scratch_shapes=[pltpu.VMEM((Fs, D), f32), pltpu.VMEM((Fs, D), f32), pltpu.VMEM((Fs, D), f32)],
        input_output_aliases={5: 0, 6: 1, 7: 2},
        compiler_params=_cp(("parallel", "arbitrary"), 48),
    )(*_hbm(h, dg, du, a, df, gwg, gwu, gwd))


_LANES = 128
_QKV_BLOCKS = D_ATT // _LANES


def _res_spec(lead, d, width, index):
    return pl.BlockSpec((lead, d, TOK_TILE // d, width), index)


def _res_spec3(d, width):
    return pl.BlockSpec((d, TOK_TILE // d, width), lambda i: (0, i, 0))


def _rows_to_residues(tile_bufs, d, put):
    for r in range(d):
        for cb, buf in enumerate(tile_bufs):
            put(r, cb, buf[pl.ds(r, TOK_TILE // d, stride=d), :])


def _residues_to_rows(tile_bufs, d, get):
    for r in range(d):
        for cb, buf in enumerate(tile_bufs):
            buf[pl.ds(r, TOK_TILE // d, stride=d), :] = get(r, cb)


def _mix_in_fwd(x, mod3, w_in, l):
    S, D = x.shape
    N = w_in.shape[-1]
    ts = TOK_TILE

    def body(x_ref, mod_ref, w_ref, o1_ref, o4_ref, o16_ref, zr_ref, h_ref, *bufs):
        j = pl.program_id(1)

        @pl.when(j == 0)
        def _():
            xh, _ = _ln_stats(x_ref[...])
            h_ref[...] = (xh * (1.0 + mod_ref[1:2, :]) + mod_ref[0:1, :]).astype(bf16)

        z = _dot(h_ref[...], w_ref[0, 0])

        @pl.when(j == N_CHIPS - 1)
        def _():
            zr_ref[...] = z

        @pl.when(j < N_CHIPS - 1)
        def _():
            zz = z * jnp.where(j == 0, HEAD_DIM ** -0.5, 1.0)
            o1_ref[j, 0] = zz.astype(bf16)
            for cb, buf in enumerate(bufs):
                buf[...] = zz[:, _LANES * cb:_LANES * (cb + 1)]
            for d, o_ref in zip(DILATIONS[1:], (o4_ref, o16_ref)):
                def put(r, cb, piece, o_ref=o_ref):
                    o_ref[j, r, :, _LANES * cb:_LANES * (cb + 1)] = piece.astype(bf16)
                _rows_to_residues(bufs, d, put)

    tok = pl.BlockSpec((ts, D), lambda i, j: (i, 0))
    res = [_res_spec(3, d, N, lambda i, j: (0, 0, i, 0)) for d in DILATIONS]
    return pl.pallas_call(
        body, name="mix_in_fwd", grid=(S // ts, N_CHIPS),
        in_specs=[tok, _full((3, D)), pl.BlockSpec((1, 1, D, N), lambda i, j: (j, l, 0, 0))],
        out_specs=res + [pl.BlockSpec((ts, N), lambda i, j: (i, 0)), tok],
        out_shape=[jax.ShapeDtypeStruct((3, d, S // d, N), bf16) for d in DILATIONS]
        + [jax.ShapeDtypeStruct((S, N), f32), jax.ShapeDtypeStruct((S, D), bf16)],
        scratch_shapes=[pltpu.VMEM((ts, _LANES), f32)] * _QKV_BLOCKS,
        compiler_params=_cp(("parallel", "arbitrary"), 40),
    )(*_hbm(x, mod3, w_in))


def _mix_in_bwd(dqkv, d_rest, dx_res, x, mod3, w_in, l):
    S, D = x.shape
    N = w_in.shape[-1]
    ts = TOK_TILE

    def body(d1_ref, d4_ref, d16_ref, dr_ref, dxr_ref, x_ref, mod_ref, w_ref, dx_ref, dmod_ref, dz_ref, acc_sc, *bufs):
        i = pl.program_id(0)
        j = pl.program_id(1)

        @pl.when((i == 0) & (j == 0))
        def _():
            dmod_ref[...] = jnp.zeros_like(dmod_ref)

        @pl.when(j == 0)
        def _():
            acc_sc[...] = jnp.zeros_like(acc_sc)

        @pl.when(j == N_CHIPS - 1)
        def _():
            dz_ref[0] = dr_ref[...]

        @pl.when(j < N_CHIPS - 1)
        def _():
            for d, d_ref, tile_bufs in ((4, d4_ref, bufs[:_QKV_BLOCKS]), (16, d16_ref, bufs[_QKV_BLOCKS:])):
                _residues_to_rows(tile_bufs, d, lambda r, cb, d_ref=d_ref: d_ref[0, r, :, _LANES * cb:_LANES * (cb + 1)].astype(f32))
            for cb in range(_QKV_BLOCKS):
                cols = slice(_LANES * cb, _LANES * (cb + 1))
                dz_ref[0, :, cols] = (d1_ref[0, 0, :, cols].astype(f32) + bufs[cb][...] + bufs[_QKV_BLOCKS + cb][...]).astype(bf16)

        acc_sc[...] += _dot_nt(dz_ref[0], w_ref[0, 0])

        @pl.when(j == N_CHIPS - 1)
        def _():
            dh = acc_sc[...]
            xh, rstd0 = _ln_stats(x_ref[...])
            dmod_ref[0:1, :] += jnp.sum(dh, 0, keepdims=True)
            dmod_ref[1:2, :] += jnp.sum(dh * xh, 0, keepdims=True)
            dx_ref[...] = _ln_bwd(dh * (1.0 + mod_ref[1:2, :]), xh, rstd0) + dxr_ref[...]

    tok = pl.BlockSpec((ts, D), lambda i, j: (i, 0))
    res = [_res_spec(1, d, N, lambda i, j: (jnp.minimum(j, 2), 0, i, 0)) for d in DILATIONS]
    return pl.pallas_call(
        body, name="mix_in_bwd", grid=(S // ts, N_CHIPS),
        in_specs=res + [pl.BlockSpec((ts, N), lambda i, j: (i, 0)), tok, tok, _full((3, D)),
                        pl.BlockSpec((1, 1, D, N), lambda i, j: (j, l, 0, 0))],
        out_specs=[tok, _full((3, D)), pl.BlockSpec((1, ts, N), lambda i, j: (j, i, 0))],
        out_shape=[jax.ShapeDtypeStruct((S, D), f32), jax.ShapeDtypeStruct((3, D), f32),
                   jax.ShapeDtypeStruct((N_CHIPS, S, N), bf16)],
        scratch_shapes=[pltpu.VMEM((ts, D), f32)] + [pltpu.VMEM((ts, _LANES), f32)] * (2 * _QKV_BLOCKS),
        compiler_params=_cp(("arbitrary", "arbitrary"), 40),
    )(*_hbm(*dqkv, d_rest, dx_res, x, mod3, w_in))


def _mix_in_wgrad(h, dz, gw, l):
    S, D = h.shape
    N = dz.shape[-1]
    tk = TOK_TILE
    nk = S // tk

    def body(h_ref, dz_ref, _g, gw_ref, acc_sc):
        k = pl.program_id(1)

        @pl.when(k == 0)
        def _():
            acc_sc[...] = jnp.zeros_like(acc_sc)

        acc_sc[...] += _dot_tn(h_ref[...], dz_ref[0])

        @pl.when(k == nk - 1)
        def _():
            gw_ref[0, 0] = acc_sc[...].astype(bf16)

    return pl.pallas_call(
        body, name="mix_in_wgrad", grid=(N_CHIPS, nk),
        in_specs=[pl.BlockSpec((tk, D), lambda p, k: (k, 0)), pl.BlockSpec((1, tk, N), lambda p, k: (p, k, 0)),
                  pl.BlockSpec(memory_space=pl.ANY)],
        out_specs=pl.BlockSpec((1, 1, D, N), lambda p, k: (p, l, 0, 0)),
        out_shape=jax.ShapeDtypeStruct(gw.shape, bf16),
        scratch_shapes=[pltpu.VMEM((D, N), f32)],
        input_output_aliases={2: 0},
        compiler_params=_cp(("parallel", "arbitrary"), 40),
    )(*_hbm(h, dz, gw))


def _mix_out_fwd(x, y_att, y_ssm, y_pool, mod3, w_out, l, lng, lnb):
    S, D = x.shape
    ts = TOK_TILE

    def body(x_ref, ya_ref, ys_ref, yp_ref, mod_ref, w_ref, lng_ref, lnb_ref, xo_ref, y_ref):
        ya = ya_ref[...].astype(bf16)
        y = (_dot(ya[:, 0:256], w_ref[0, 0]) + _dot(ya[:, 256:512], w_ref[1, 0])
             + _dot(ys_ref[...].astype(bf16), w_ref[2, 0]) + _dot(yp_ref[...].astype(bf16), w_ref[3, 0]))
        y_ref[...] = y
        rh, _ = _ln_stats(ALPHA * x_ref[...] + mod_ref[2:3, :] * y)
        xo_ref[...] = rh * lng_ref[...] + lnb_ref[...]

    tok = pl.BlockSpec((ts, D), lambda i: (i, 0))
    return pl.pallas_call(
        body, name="mix_out_fwd", grid=(S // ts,),
        in_specs=[tok, pl.BlockSpec((ts, D_ATT), lambda i: (i, 0)), pl.BlockSpec((ts, D_SSM), lambda i: (i, 0)),
                  pl.BlockSpec((ts, D_POOL), lambda i: (i, 0)), _full((3, D)),
                  pl.BlockSpec((N_CHIPS, 1, 256, D), lambda i: (0, l, 0, 0)), _full((1, D)), _full((1, D))],
        out_specs=[tok, tok],
        out_shape=[jax.ShapeDtypeStruct((S, D), f32), jax.ShapeDtypeStruct((S, D), f32)],
        compiler_params=_cp(("parallel",), 40),
    )(*_hbm(x, y_att, y_ssm, y_pool, mod3, w_out, lng, lnb))


def _mix_out_bwd(dxo, x, y, y_att, y_ssm, y_pool, mod3, w_out, l, lng, gw_out):
    S, D = x.shape
    ts = TOK_TILE
    nt = S // ts

    def body(dxo_ref, x_ref, y_ref, ya_ref, ys_ref, yp_ref, mod_ref, w_ref, lng_ref, _g,
             dxr_ref, da_ref, ds_ref, dp_ref, dgate_ref, dlng_ref, dlnb_ref, gw_ref, acc_sc):
        i = pl.program_id(0)

        @pl.when(i == 0)
        def _():
            dgate_ref[...] = jnp.zeros_like(dgate_ref)
            dlng_ref[...] = jnp.zeros_like(dlng_ref)
            dlnb_ref[...] = jnp.zeros_like(dlnb_ref)
            acc_sc[...] = jnp.zeros_like(acc_sc)

        gate = mod_ref[2:3, :]
        yv = y_ref[...]
        rh, rstd = _ln_stats(ALPHA * x_ref[...] + gate * yv)
        dy_out = dxo_ref[...]
        dlng_ref[...] += jnp.sum(dy_out * rh, 0, keepdims=True)
        dlnb_ref[...] += jnp.sum(dy_out, 0, keepdims=True)
        dr = _ln_bwd(dy_out * lng_ref[...], rh, rstd)
        dxr_ref[...] = ALPHA * dr
        dgate_ref[...] += jnp.sum(dr * yv, 0, keepdims=True)
        dy = (gate * dr).astype(bf16)
        da_ref[:, 0:256] = _dot_nt(dy, w_ref[0, 0])
        da_ref[:, 256:512] = _dot_nt(dy, w_ref[1, 0])
        ds_ref[...] = _dot_nt(dy, w_ref[2, 0])
        dp_ref[...] = _dot_nt(dy, w_ref[3, 0])
        ya = ya_ref[...].astype(bf16)
        acc_sc[0] += _dot_tn(ya[:, 0:256], dy)
        acc_sc[1] += _dot_tn(ya[:, 256:512], dy)
        acc_sc[2] += _dot_tn(ys_ref[...].astype(bf16), dy)
        acc_sc[3] += _dot_tn(yp_ref[...].astype(bf16), dy)

        @pl.when(i == nt - 1)
        def _():
            gw_ref[:, 0] = acc_sc[...].astype(bf16)

    tok = pl.BlockSpec((ts, D), lambda i: (i, 0))
    t512 = pl.BlockSpec((ts, D_ATT), lambda i: (i, 0))
    t256 = pl.BlockSpec((ts, 256), lambda i: (i, 0))
    wspec = pl.BlockSpec((N_CHIPS, 1, 256, D), lambda i: (0, l, 0, 0))
    return pl.pallas_call(
        body, name="mix_out_bwd", grid=(nt,),
        in_specs=[tok, tok, tok, t512, t256, t256, _full((3, D)), wspec, _full((1, D)), pl.BlockSpec(memory_space=pl.ANY)],
        out_specs=[tok, t512, t256, t256, _full((1, D)), _full((1, D)), _full((1, D)), wspec],
        out_shape=[jax.ShapeDtypeStruct((S, D), f32), jax.ShapeDtypeStruct((S, D_ATT), f32),
                   jax.ShapeDtypeStruct((S, D_SSM), f32), jax.ShapeDtypeStruct((S, D_POOL), f32),
                   jax.ShapeDtypeStruct((1, D), f32), jax.ShapeDtypeStruct((1, D), f32), jax.ShapeDtypeStruct((1, D), f32),
                   jax.ShapeDtypeStruct(gw_out.shape, bf16)],
        scratch_shapes=[pltpu.VMEM((N_CHIPS, 256, D), f32)],
        input_output_aliases={9: 7},
        compiler_params=_cp(("arbitrary",), 48),
    )(*_hbm(dxo, x, y, y_att, y_ssm, y_pool, mod3, w_out, lng, gw_out))


def _t5_bucket(dist):
    max_exact = N_BUCKETS // 2
    d = np.maximum(dist, 1).astype(np.float32)
    large = max_exact + (np.log(d / max_exact) / math.log(MAX_DISTANCE / max_exact)
                         * (N_BUCKETS - max_exact)).astype(np.int32)
    large = np.minimum(large, N_BUCKETS - 1)
    return np.where(dist < max_exact, dist, large).astype(np.int32)


def _bucket_table():
    q = ATT_BLOCK
    i = np.arange(q)[:, None]
    j = np.arange(2 * q)[None, :]
    r = i + q - j
    in_band = (r >= 0) & (r <= q)
    tabs = [np.where(in_band, _t5_bucket(np.clip(r, 0, None) * d), -1) for d in DILATIONS]
    return np.stack(tabs).astype(np.int32)


def _bias_fwd(rel_bias, table):
    def body(rb_ref, tab_ref, out_ref):
        for b in range(3):
            tb = tab_ref[b]
            for h in range(N_HEADS):
                def pick(k, acc):
                    return jnp.where(tb == k, rb_ref[k, h], acc)
                out_ref[b, h] = lax.fori_loop(0, N_BUCKETS, pick, jnp.where(tb < 0, NEG, 0.0).astype(f32))

    return pl.pallas_call(
        body, name="bias_fwd",
        in_specs=[pl.BlockSpec(memory_space=pltpu.SMEM), pl.BlockSpec(memory_space=pltpu.VMEM)],
        out_specs=pl.BlockSpec(memory_space=pltpu.VMEM),
        out_shape=jax.ShapeDtypeStruct((3, N_HEADS, ATT_BLOCK, 2 * ATT_BLOCK), f32),
    )(rel_bias, table)


def _bias_bwd(dbias, table):
    def body(db_ref, tab_ref, out_ref):
        def per_bucket(k, c):
            for h in range(N_HEADS):
                tot = jnp.zeros((), f32)
                for b in range(3):
                    tot = tot + jnp.sum(jnp.where(tab_ref[b] == k, db_ref[b, h], 0.0))
                out_ref[k, h] = tot
            return c
        lax.fori_loop(0, N_BUCKETS, per_bucket, 0)

    return pl.pallas_call(
        body, name="bias_bwd",
        in_specs=[pl.BlockSpec(memory_space=pltpu.VMEM), pl.BlockSpec(memory_space=pltpu.VMEM)],
        out_specs=pl.BlockSpec(memory_space=pltpu.SMEM),
        out_shape=jax.ShapeDtypeStruct((N_BUCKETS, N_HEADS), f32),
    )(dbias, table)


def _att_unit(u, nbr):
    rows = pl.ds(pl.multiple_of(u * ATT_BLOCK, ATT_BLOCK), ATT_BLOCK)
    prev = pl.ds(pl.multiple_of(jnp.maximum(u - 1, 0) * ATT_BLOCK, ATT_BLOCK), ATT_BLOCK)
    return rows, prev, (u % nbr) != 0


_HEAD_ROWS = N_HEADS * ATT_BLOCK


def _head_rows(t):
    lane = lax.broadcasted_iota(jnp.int32, t.shape, 1)
    return jnp.concatenate([jnp.where((lane >= HEAD_DIM * h) & (lane < HEAD_DIM * (h + 1)), t, jnp.zeros_like(t))
                            for h in range(N_HEADS)], axis=0)


def _head_cols(big):
    lane = lax.broadcasted_iota(jnp.int32, (ATT_BLOCK, D_ATT), 1)
    out = big[0:ATT_BLOCK]
    for h in range(1, N_HEADS):
        out = jnp.where(lane >= HEAD_DIM * h, big[ATT_BLOCK * h:ATT_BLOCK * (h + 1)], out)
    return out


def _head_column(ref, rows):
    t = ref[rows, :]
    return jnp.concatenate([t[:, h:h + 1] for h in range(N_HEADS)], axis=0)


def _att_band(ref, rows, prev, nbr):
    cur = ref[0, rows, :]
    return cur if nbr == 1 else jnp.concatenate([ref[0, prev, :], cur], axis=0)


def _att_scores(q_ref, k_ref, b_ref, rows, prev, valid_prev, nbr):
    qbd = _head_rows(q_ref[0, rows, :])
    kb = _att_band(k_ref, rows, prev, nbr)
    bias = b_ref[0].reshape(_HEAD_ROWS, 2 * ATT_BLOCK)
    if nbr == 1:
        return qbd, kb, _dot_nt(qbd, kb) + bias[:, ATT_BLOCK:]
    s = _dot_nt(qbd, kb) + bias
    col = lax.broadcasted_iota(jnp.int32, s.shape, 1)
    return qbd, kb, jnp.where((col >= ATT_BLOCK) | valid_prev, s, NEG)


def _qkv_specs(S, branch):
    return ([pl.BlockSpec((1, S, D_ATT), lambda i, t=t: (t, 0, 0)) for t in range(3)],
            pl.BlockSpec((1, N_HEADS, ATT_BLOCK, 2 * ATT_BLOCK), lambda i: (branch, 0, 0, 0)))


def _att_fwd(qkv, bias, branch):
    S = qkv.shape[1]
    nbr = BLOCKS_PER_RESIDUE[branch]

    def body(q_ref, k_ref, v_ref, b_ref, o_ref, lse_ref):
        lse_ref[...] = jnp.zeros_like(lse_ref)

        def unit(u, c):
            rows, prev, valid_prev = _att_unit(u, nbr)
            _, _, s = _att_scores(q_ref, k_ref, b_ref, rows, prev, valid_prev, nbr)
            m = jnp.max(s, -1, keepdims=True)
            p = jnp.exp(s - m)
            den = jnp.sum(p, -1, keepdims=True)
            big = _dot(p.astype(bf16), _att_band(v_ref, rows, prev, nbr))
            o_ref[rows, :] = _head_cols(big / den)
            lse = m + jnp.log(den)
            for h in range(N_HEADS):
                lse_ref[rows, pl.ds(h, 1)] = lse[ATT_BLOCK * h:ATT_BLOCK * (h + 1)]
            return c

        lax.fori_loop(0, N_UNITS, unit, 0)

    qkv_specs, bspec = _qkv_specs(S, branch)
    return pl.pallas_call(
        body, name="att_fwd", grid=(1,),
        in_specs=qkv_specs + [bspec],
        out_specs=[pl.BlockSpec((S, D_ATT), lambda i: (0, 0)), pl.BlockSpec((S, _LANES), lambda i: (0, 0))],
        out_shape=[jax.ShapeDtypeStruct((S, D_ATT), f32), jax.ShapeDtypeStruct((S, _LANES), f32)],
        compiler_params=_cp(("arbitrary",), 40),
    )(*_hbm(qkv, qkv, qkv, bias))


def _att_bwd(qkv, do, lse, crow, bias, branch):
    S = qkv.shape[1]
    nbr = BLOCKS_PER_RESIDUE[branch]

    def body(q_ref, k_ref, v_ref, do_ref, lse_ref, c_ref, b_ref, dqkv_ref, db_ref, dk_sc, dv_sc):
        dk_sc[...] = jnp.zeros_like(dk_sc)
        dv_sc[...] = jnp.zeros_like(dv_sc)
        db_ref[...] = jnp.zeros_like(db_ref)

        def unit(u, c):
            rows, prev, valid_prev = _att_unit(u, nbr)
            qbd, kb, s = _att_scores(q_ref, k_ref, b_ref, rows, prev, valid_prev, nbr)
            p = jnp.exp(s - _head_column(lse_ref, rows))
            dobd = _head_rows(do_ref[rows, :])
            ds = p * (_dot_nt(dobd, _att_band(v_ref, rows, prev, nbr)) - _head_column(c_ref, rows))
            if nbr == 1:
                db_ref[:, :, ATT_BLOCK:] += ds.reshape(N_HEADS, ATT_BLOCK, ATT_BLOCK)
            else:
                db_ref[...] += ds.reshape(N_HEADS, ATT_BLOCK, 2 * ATT_BLOCK)
            dsb = ds.astype(bf16)
            dqkv_ref[0, rows, :] = (HEAD_DIM ** -0.5 * _head_cols(_dot(dsb, kb))).astype(bf16)
            dkb = _dot_tn(dsb, qbd)
            dvb = _dot_tn(p.astype(bf16), dobd)
            if nbr == 1:
                dk_sc[rows, :] += dkb
                dv_sc[rows, :] += dvb
            else:
                dk_sc[prev, :] += dkb[:ATT_BLOCK]
                dv_sc[prev, :] += dvb[:ATT_BLOCK]
                dk_sc[rows, :] += dkb[ATT_BLOCK:]
                dv_sc[rows, :] += dvb[ATT_BLOCK:]
            return c

        lax.fori_loop(0, N_UNITS, unit, 0)
        dqkv_ref[1] = dk_sc[...].astype(bf16)
        dqkv_ref[2] = dv_sc[...].astype(bf16)

    qkv_specs, bspec = _qkv_specs(S, branch)
    row = pl.BlockSpec((S, _LANES), lambda i: (0, 0))
    return pl.pallas_call(
        body, name="att_bwd", grid=(1,),
        in_specs=qkv_specs + [pl.BlockSpec((S, D_ATT), lambda i: (0, 0)), row, row, bspec],
        out_specs=[pl.BlockSpec((3, S, D_ATT), lambda i: (0, 0, 0)),
                   pl.BlockSpec((N_HEADS, ATT_BLOCK, 2 * ATT_BLOCK), lambda i: (0, 0, 0))],
        out_shape=[jax.ShapeDtypeStruct((3, S, D_ATT), bf16), jax.ShapeDtypeStruct((N_HEADS, ATT_BLOCK, 2 * ATT_BLOCK), f32)],
        scratch_shapes=[pltpu.VMEM((S, D_ATT), f32), pltpu.VMEM((S, D_ATT), f32)],
        compiler_params=_cp(("arbitrary",), 48),
    )(*_hbm(qkv, qkv, qkv, do, lse, crow, bias))


def _branch_weights(lse_ref):
    l0, l1, l2 = lse_ref[0], lse_ref[1], lse_ref[2]
    m = jnp.maximum(jnp.maximum(l0, l1), l2)
    e0, e1, e2 = jnp.exp(l0 - m), jnp.exp(l1 - m), jnp.exp(l2 - m)
    tot = e0 + e1 + e2
    return e0 / tot, e1 / tot, e2 / tot


def _att_merge(os, lses):
    S = os[0].shape[0] * os[0].shape[1]
    ts = TOK_TILE

    def body(o1_ref, o4_ref, o16_ref, l1_ref, l4_ref, l16_ref, y_ref, lt_ref, *bufs):
        obufs = (bufs[:_QKV_BLOCKS], bufs[_QKV_BLOCKS:2 * _QKV_BLOCKS])
        lt_ref[0] = l1_ref[0]
        for k, (d, o_ref, l_ref) in enumerate(((4, o4_ref, l4_ref), (16, o16_ref, l16_ref))):
            _residues_to_rows(obufs[k], d, lambda r, cb, o_ref=o_ref: o_ref[r, :, _LANES * cb:_LANES * (cb + 1)])
            _residues_to_rows([bufs[2 * _QKV_BLOCKS + k]], d, lambda r, cb, l_ref=l_ref: l_ref[r])
            lt_ref[1 + k] = bufs[2 * _QKV_BLOCKS + k][...]
        w = _branch_weights(lt_ref)
        for h in range(N_HEADS):
            cs = slice(HEAD_DIM * h, HEAD_DIM * (h + 1))
            half = slice(HEAD_DIM * (h % 2), HEAD_DIM * (h % 2 + 1))
            y_ref[:, cs] = (w[0][:, h:h + 1] * o1_ref[0, :, cs] + w[1][:, h:h + 1] * obufs[0][h // 2][:, half]
                            + w[2][:, h:h + 1] * obufs[1][h // 2][:, half])

    return pl.pallas_call(
        body, name="att_merge", grid=(S // ts,),
        in_specs=[_res_spec3(d, D_ATT) for d in DILATIONS] + [_res_spec3(d, _LANES) for d in DILATIONS],
        out_specs=[pl.BlockSpec((ts, D_ATT), lambda i: (i, 0)), pl.BlockSpec((3, ts, _LANES), lambda i: (0, i, 0))],
        out_shape=[jax.ShapeDtypeStruct((S, D_ATT), f32), jax.ShapeDtypeStruct((3, S, _LANES), f32)],
        scratch_shapes=[pltpu.VMEM((ts, _LANES), f32)] * (2 * _QKV_BLOCKS + 2),
        compiler_params=_cp(("parallel",)),
    )(*_hbm(*os, *lses))


def _att_merge_bwd(dy, y, lse3):
    S = dy.shape[0]
    ts = TOK_TILE

    def body(dy_ref, y_ref, lse_ref, do1_ref, do4_ref, do16_ref, c1_ref, c4_ref, c16_ref, *bufs):
        dobufs = (bufs[:_QKV_BLOCKS], bufs[_QKV_BLOCKS:2 * _QKV_BLOCKS], bufs[2 * _QKV_BLOCKS:3 * _QKV_BLOCKS])
        cbufs = bufs[3 * _QKV_BLOCKS:]
        w = _branch_weights(lse_ref)
        for cb in cbufs:
            cb[...] = jnp.zeros_like(cb)
        for h in range(N_HEADS):
            cs = slice(HEAD_DIM * h, HEAD_DIM * (h + 1))
            half = slice(HEAD_DIM * (h % 2), HEAD_DIM * (h % 2 + 1))
            dyh = dy_ref[:, cs]
            t = jnp.sum(dyh * y_ref[:, cs], -1, keepdims=True)
            for p in range(3):
                wp = w[p][:, h:h + 1]
                dobufs[p][h // 2][:, half] = wp * dyh
                cbufs[p][:, h:h + 1] = wp * t
        for cb in range(_QKV_BLOCKS):
            do1_ref[0, :, _LANES * cb:_LANES * (cb + 1)] = dobufs[0][cb][...].astype(bf16)
        c1_ref[0] = cbufs[0][...]
        for k, (d, do_ref, c_ref) in enumerate(((4, do4_ref, c4_ref), (16, do16_ref, c16_ref))):
            def put_do(r, cb, piece, do_ref=do_ref):
                do_ref[r, :, _LANES * cb:_LANES * (cb + 1)] = piece.astype(bf16)

            def put_c(r, cb, piece, c_ref=c_ref):
                c_ref[r] = piece

            _rows_to_residues(dobufs[1 + k], d, put_do)
            _rows_to_residues([cbufs[1 + k]], d, put_c)

    return pl.pallas_call(
        body, name="att_merge_bwd", grid=(S // ts,),
        in_specs=[pl.BlockSpec((ts, D_ATT), lambda i: (i, 0)), pl.BlockSpec((ts, D_ATT), lambda i: (i, 0)),
                  pl.BlockSpec((3, ts, _LANES), lambda i: (0, i, 0))],
        out_specs=[_res_spec3(d, D_ATT) for d in DILATIONS] + [_res_spec3(d, _LANES) for d in DILATIONS],
        out_shape=[jax.ShapeDtypeStruct((d, S // d, D_ATT), bf16) for d in DILATIONS]
        + [jax.ShapeDtypeStruct((d, S // d, _LANES), f32) for d in DILATIONS],
        scratch_shapes=[pltpu.VMEM((ts, _LANES), f32)] * (3 * _QKV_BLOCKS + 3),
        compiler_params=_cp(("parallel",)),
    )(*_hbm(dy, y, lse3))


_SSM_ROWS = 256


def _scan_in_place(sr_ref, si_ref, a_ref, reverse):
    S, N = sr_ref.shape
    nst = S // SCAN_SEG
    ar = jnp.broadcast_to(a_ref[0:1, :], (SCAN_SEG, N))
    ai = jnp.broadcast_to(a_ref[1:2, :], (SCAN_SEG, N))
    if reverse:
        ai = -ai
    row = lax.broadcasted_iota(jnp.int32, (SCAN_SEG, N), 0)
    zero = jnp.zeros((SCAN_SEG, N), f32)

    def tile(t):
        return pl.ds(pl.multiple_of((nst - 1 - t if reverse else t) * SCAN_SEG, SCAN_SEG), SCAN_SEG)

    def local(t, c):
        sr, si, pr, pi = c
        rows = tile(t)
        nsr = ar * sr - ai * si + sr_ref[rows, :]
        nsi = ar * si + ai * sr + si_ref[rows, :]
        sr_ref[rows, :] = nsr
        si_ref[rows, :] = nsi
        return nsr, nsi, ar * pr - ai * pi, ar * pi + ai * pr

    fr, fi, apr, api = lax.fori_loop(0, nst, local, (zero, zero, zero + 1.0, zero))

    def shift(v):
        if reverse:
            return jnp.where(row == SCAN_SEG - 1, 0.0, pltpu.roll(v, SCAN_SEG - 1, axis=0))
        return jnp.where(row == 0, 0.0, pltpu.roll(v, 1, axis=0))

    cr, ci = zero, zero
    for _ in range(SCAN_SEG - 1):
        cr, ci = shift(fr + apr * cr - api * ci), shift(fi + apr * ci + api * cr)

    def fix(t, c):
        pr, pi = c
        npr, npi = ar * pr - ai * pi, ar * pi + ai * pr
        rows = tile(t)
        sr_ref[rows, :] += npr * cr - npi * ci
        si_ref[rows, :] += npr * ci + npi * cr
        return npr, npi

    lax.fori_loop(0, nst, fix, (zero + 1.0, zero))


def _ssm_states(u, bre, bim, a2):
    S = u.shape[0]

    def body(u_ref, br_ref, bi_ref, a_ref, sr_ref, si_ref):
        brb = br_ref[...].astype(bf16)
        bib = bi_ref[...].astype(bf16)

        def project(t, c):
            rows = pl.ds(pl.multiple_of(t * _SSM_ROWS, _SSM_ROWS), _SSM_ROWS)
            ub = u_ref[rows, :].astype(bf16)
            sr_ref[rows, :] = _dot(ub, brb)
            si_ref[rows, :] = _dot(ub, bib)
            return c

        lax.fori_loop(0, S // _SSM_ROWS, project, 0)
        _scan_in_place(sr_ref, si_ref, a_ref, False)

    vm = pl.BlockSpec(memory_space=pltpu.VMEM)
    return pl.pallas_call(
        body, name="ssm_states", in_specs=[vm] * 4, out_specs=[vm, vm],
        out_shape=[jax.ShapeDtypeStruct((S, D_STATE), f32)] * 2,
        compiler_params=_cp(None, 48),
    )(u, bre, bim, a2)


def _ssm_out(sr, si, u, cre, cim, dskip, glu_w, glu_b):
    S = u.shape[0]
    ts = TOK_TILE

    def body(sr_ref, si_ref, u_ref, cr_ref, ci_ref, d_ref, w_ref, b_ref, out_ref, y_ref):
        y = (_dot(sr_ref[...].astype(bf16), cr_ref[...].astype(bf16))
             - _dot(si_ref[...].astype(bf16), ci_ref[...].astype(bf16)) + d_ref[...] * u_ref[...])
        y_ref[...] = y
        z = _dot(_gelu(y).astype(bf16), w_ref[...].astype(bf16)) + b_ref[...]
        out_ref[...] = y * jax.nn.sigmoid(z)

    st = pl.BlockSpec((ts, D_STATE), lambda i: (i, 0))
    ch = pl.BlockSpec((ts, D_SSM), lambda i: (i, 0))
    return pl.pallas_call(
        body, name="ssm_out", grid=(S // ts,),
        in_specs=[st, st, ch, _full((D_STATE, D_SSM)), _full((D_STATE, D_SSM)), _full((1, D_SSM)),
                  _full((D_SSM, D_SSM)), _full((1, D_SSM))],
        out_specs=[ch, ch],
        out_shape=[jax.ShapeDtypeStruct((S, D_SSM), f32)] * 2,
        compiler_params=_cp(("parallel",)),
    )(*_hbm(sr, si, u, cre, cim, dskip, glu_w, glu_b))


def _ssm_out_bwd(dout, y, u, sr, si, dskip, glu_w, glu_b):
    S = u.shape[0]
    ts = TOK_TILE

    def body(do_ref, y_ref, u_ref, sr_ref, si_ref, d_ref, w_ref, b_ref,
             dy_ref, du_ref, dcr_ref, dci_ref, dd_ref, dgb_ref, dgw_ref):
        @pl.when(pl.program_id(0) == 0)
        def _():
            for r in (dcr_ref, dci_ref, dd_ref, dgb_ref, dgw_ref):
                r[...] = jnp.zeros_like(r)

        y = y_ref[...]
        dout = do_ref[...]
        wb = w_ref[...].astype(bf16)
        ge = _gelu(y).astype(bf16)
        sz = jax.nn.sigmoid(_dot(ge, wb) + b_ref[...])
        dz = dout * y * sz * (1.0 - sz)
        dzb = dz.astype(bf16)
        dgb_ref[...] += jnp.sum(dz, 0, keepdims=True)
        dgw_ref[...] += _dot_tn(ge, dzb)
        dy = dout * sz + _gelu_grad(y) * _dot_nt(dzb, wb)
        uv = u_ref[...]
        dd_ref[...] += jnp.sum(dy * uv, 0, keepdims=True)
        du_ref[...] = dy * d_ref[...]
        dy_ref[...] = dy
        dyb = dy.astype(bf16)
        dcr_ref[...] += _dot_tn(sr_ref[...].astype(bf16), dyb)
        dci_ref[...] -= _dot_tn(si_ref[...].astype(bf16), dyb)

    st = pl.BlockSpec((ts, D_STATE), lambda i: (i, 0))
    ch = pl.BlockSpec((ts, D_SSM), lambda i: (i, 0))
    c_full = _full((D_STATE, D_SSM))
    return pl.pallas_call(
        body, name="ssm_out_bwd", grid=(S // ts,),
        in_specs=[ch, ch, ch, st, st, _full((1, D_SSM)), _full((D_SSM, D_SSM)), _full((1, D_SSM))],
        out_specs=[ch, ch, c_full, c_full, _full((1, D_SSM)), _full((1, D_SSM)), _full((D_SSM, D_SSM))],
        out_shape=[jax.ShapeDtypeStruct((S, D_SSM), f32), jax.ShapeDtypeStruct((S, D_SSM), f32),
                   jax.ShapeDtypeStruct((D_STATE, D_SSM), f32), jax.ShapeDtypeStruct((D_STATE, D_SSM), f32),
                   jax.ShapeDtypeStruct((1, D_SSM), f32), jax.ShapeDtypeStruct((1, D_SSM), f32),
                   jax.ShapeDtypeStruct((D_SSM, D_SSM), f32)],
        compiler_params=_cp(("arbitrary",), 40),
    )(*_hbm(dout, y, u, sr, si, dskip, glu_w, glu_b))


def _ssm_states_bwd(dy, du_skip, u, sr, si, cre, cim, bre, bim, a2):
    S = u.shape[0]
    N = D_STATE
    nst = S // SCAN_SEG
    nproj = S // _SSM_ROWS

    def body(dy_ref, dus_ref, u_ref, sr_ref, si_ref, cr_ref, ci_ref, br_ref, bi_ref, a_ref,
             du_ref, dbr_ref, dbi_ref, da_ref, lr_ref, li_ref):
        crb = cr_ref[...].astype(bf16)
        cib = ci_ref[...].astype(bf16)

        def project(t, c):
            rows = pl.ds(pl.multiple_of(t * _SSM_ROWS, _SSM_ROWS), _SSM_ROWS)
            dyb = dy_ref[rows, :].astype(bf16)
            lr_ref[rows, :] = _dot_nt(dyb, crb)
            li_ref[rows, :] = -_dot_nt(dyb, cib)
            return c

        lax.fori_loop(0, nproj, project, 0)
        _scan_in_place(lr_ref, li_ref, a_ref, True)

        row = lax.broadcasted_iota(jnp.int32, (SCAN_SEG, N), 0)
        last = pl.ds((nst - 1) * SCAN_SEG, SCAN_SEG)
        pr = jnp.where(row == 0, 0.0, pltpu.roll(sr_ref[last, :], 1, axis=0))
        pi = jnp.where(row == 0, 0.0, pltpu.roll(si_ref[last, :], 1, axis=0))
        first = pl.ds(0, SCAN_SEG)
        acc_r = lr_ref[first, :] * pr + li_ref[first, :] * pi
        acc_i = li_ref[first, :] * pr - lr_ref[first, :] * pi

        def step(t, c):
            acc_r, acc_i = c
            rows = pl.ds(pl.multiple_of(t * SCAN_SEG, SCAN_SEG), SCAN_SEG)
            prev = pl.ds(pl.multiple_of((t - 1) * SCAN_SEG, SCAN_SEG), SCAN_SEG)
            lrv, liv, srv, siv = lr_ref[rows, :], li_ref[rows, :], sr_ref[prev, :], si_ref[prev, :]
            return acc_r + lrv * srv + liv * siv, acc_i + liv * srv - lrv * siv

        acc_r, acc_i = lax.fori_loop(1, nst, step, (acc_r, acc_i))
        da_ref[0:1, :] = jnp.sum(acc_r, 0, keepdims=True)
        da_ref[1:2, :] = jnp.sum(acc_i, 0, keepdims=True)

        brb = br_ref[...].astype(bf16)
        bib = bi_ref[...].astype(bf16)
        dbr_ref[...] = jnp.zeros_like(dbr_ref)
        dbi_ref[...] = jnp.zeros_like(dbi_ref)

        def back(t, c):
            rows = pl.ds(pl.multiple_of(t * _SSM_ROWS, _SSM_ROWS), _SSM_ROWS)
            lrb = lr_ref[rows, :].astype(bf16)
            lib = li_ref[rows, :].astype(bf16)
            du_ref[rows, :] = dus_ref[rows, :] + _dot_nt(lrb, brb) + _dot_nt(lib, bib)
            ub = u_ref[rows, :].astype(bf16)
            dbr_ref[...] += _dot_tn(ub, lrb)
            dbi_ref[...] += _dot_tn(ub, lib)
            return c

        lax.fori_loop(0, nproj, back, 0)

    vm = pl.BlockSpec(memory_space=pltpu.VMEM)
    return pl.pallas_call(
        body, name="ssm_states_bwd", in_specs=[vm] * 10, out_specs=[vm] * 4,
        out_shape=[jax.ShapeDtypeStruct((S, D_SSM), f32), jax.ShapeDtypeStruct((D_SSM, D_STATE), f32),
                   jax.ShapeDtypeStruct((D_SSM, D_STATE), f32), jax.ShapeDtypeStruct((2, D_STATE), f32)],
        scratch_shapes=[pltpu.VMEM((S, D_STATE), f32), pltpu.VMEM((S, D_STATE), f32)],
        compiler_params=_cp(None, 56),
    )(dy, du_skip, u, sr, si, cre, cim, bre, bim, a2)


_POOL_TILE = 256


def _window_sums(xt, back):
    n = xt.shape[0]
    out = []
    ws = xt
    for k in (1, 2, 4, 8):
        ws = ws + pltpu.roll(ws, k if back else n - k, axis=0)
        out.append(ws)
    return out


def _pool_count(r0, w):
    t = r0 + lax.broadcasted_iota(jnp.int32, (_POOL_TILE, POOL_GROUP), 0)
    return jnp.minimum(t + 1, w).astype(f32)


def _pool_fwd(u_pad, pool_w, pool_scale):
    S = u_pad.shape[0] - POOL_HALO
    nt = S // _POOL_TILE

    def body(u_ref, w_ref, sc_ref, y_ref):
        def tile(t, c):
            r0 = pl.multiple_of(t * _POOL_TILE, _POOL_TILE)
            for g, w in enumerate(POOL_WINDOWS):
                cs = pl.ds(POOL_GROUP * g, POOL_GROUP)
                xt = u_ref[pl.ds(r0, _POOL_TILE + POOL_HALO), cs]
                ws = _window_sums(xt, True)[g][POOL_HALO:, :]
                pooled = ws / _pool_count(r0, w) - xt[POOL_HALO:, :]
                y_ref[pl.ds(r0, _POOL_TILE), cs] = _dot(pooled.astype(bf16), w_ref[g].astype(bf16)) * sc_ref[:, cs]
            return c
        lax.fori_loop(0, nt, tile, 0)

    vm = pl.BlockSpec(memory_space=pltpu.VMEM)
    return pl.pallas_call(
        body, name="pool_fwd", in_specs=[vm, vm, vm], out_specs=vm,
        out_shape=jax.ShapeDtypeStruct((S, D_POOL), f32),
    )(u_pad, pool_w, pool_scale)


def _pool_bwd(dy_pad, u_pad, pool_w, pool_scale):
    S = u_pad.shape[0] - POOL_HALO
    nt = S // _POOL_TILE
    n = _POOL_TILE + POOL_HALO

    def body(dy_ref, u_ref, w_ref, sc_ref, du_ref, dw_ref, dsc_ref):
        dw_ref[...] = jnp.zeros_like(dw_ref)
        dsc_ref[...] = jnp.zeros_like(dsc_ref)

        def tile(t, c):
            r0 = pl.multiple_of(t * _POOL_TILE, _POOL_TILE)
            for g, w in enumerate(POOL_WINDOWS):
                cs = pl.ds(POOL_GROUP * g, POOL_GROUP)
                wb = w_ref[g].astype(bf16)
                xt = u_ref[pl.ds(r0, n), cs]
                pooled = (_window_sums(xt, True)[g][POOL_HALO:, :] / _pool_count(r0, w) - xt[POOL_HALO:, :]).astype(bf16)
                dy = dy_ref[pl.ds(r0, _POOL_TILE), cs]
                dsc_ref[:, cs] += jnp.sum(dy * _dot(pooled, wb), 0, keepdims=True)
                dw_ref[g] += _dot_tn(pooled, (dy * sc_ref[:, cs]).astype(bf16))
                dyh = (dy_ref[pl.ds(r0, n), cs] * sc_ref[:, cs]).astype(bf16)
                dpl = _dot_nt(dyh, wb)
                cnt = jnp.minimum(r0 + lax.broadcasted_iota(jnp.int32, (n, POOL_GROUP), 0) + 1, w).astype(f32)
                lead = _window_sums(dpl / cnt, False)[g]
                du_ref[pl.ds(r0, _POOL_TILE), cs] = lead[:_POOL_TILE, :] - dpl[:_POOL_TILE, :]
            return c
        lax.fori_loop(0, nt, tile, 0)

    vm = pl.BlockSpec(memory_space=pltpu.VMEM)
    return pl.pallas_call(
        body, name="pool_bwd", in_specs=[vm, vm, vm, vm], out_specs=[vm, vm, vm],
        out_shape=[jax.ShapeDtypeStruct((S, D_POOL), f32), jax.ShapeDtypeStruct((4, POOL_GROUP, POOL_GROUP), f32),
                   jax.ShapeDtypeStruct((1, D_POOL), f32)],
    )(dy_pad, u_pad, pool_w, pool_scale)


def _loss_head(y, target):
    S, D = y.shape
    ts = TOK_TILE

    def body(y_ref, t_ref, loss_ref, dy_ref):
        @pl.when(pl.program_id(0) == 0)
        def _():
            loss_ref[...] = jnp.zeros_like(loss_ref)

        d = y_ref[...] - t_ref[...]
        dy_ref[...] = d * (1.0 / D)
        loss_ref[...] += 0.5 * jnp.sum(jnp.sum(d * d, -1, keepdims=True) * (1.0 / D), 0, keepdims=True)

    tok = pl.BlockSpec((ts, D), lambda i: (i, 0))
    return pl.pallas_call(
        body, name="loss_head", grid=(S // ts,),
        in_specs=[tok, tok], out_specs=[_full((1, 1)), tok],
        out_shape=[jax.ShapeDtypeStruct((1, 1), f32), jax.ShapeDtypeStruct((S, D), f32)],
        compiler_params=_cp(("arbitrary",)),
    )(*_hbm(y, target))


_ADA_COLS = 768


def _ada_fwd(c_all, ada_w, ada_b_cols):
    L, D, N = ada_w.shape
    B = c_all.shape[0]

    def body(c_ref, w_ref, b_ref, out_ref):
        cv = c_ref[...]
        cond = (cv * jax.nn.sigmoid(cv)).astype(bf16)
        out_ref[0] = _dot(cond, w_ref[0].astype(bf16)) + b_ref[0]

    return pl.pallas_call(
        body, name="ada_fwd", grid=(L, N // _ADA_COLS),
        in_specs=[_full((B, D)), pl.BlockSpec((1, D, _ADA_COLS), lambda l, j: (l, 0, j)),
                  pl.BlockSpec((1, 1, _ADA_COLS), lambda l, j: (l, 0, j))],
        out_specs=pl.BlockSpec((1, B, _ADA_COLS), lambda l, j: (l, 0, j)),
        out_shape=jax.ShapeDtypeStruct((L, B, N), f32),
        compiler_params=_cp(("parallel", "parallel")),
    )(c_all, ada_w, ada_b_cols)


def _ada_wgrad(c_all_t, dmod_cols):
    D, B = c_all_t.shape
    L, _, N = dmod_cols.shape

    def body(ct_ref, dm_ref, out_ref):
        cv = ct_ref[...]
        cond = cv * jax.nn.sigmoid(cv)
        acc = cond[:, 0:1] * dm_ref[0, 0:1, :]
        for b in range(1, B):
            acc = acc + cond[:, b:b + 1] * dm_ref[0, b:b + 1, :]
        out_ref[0] = acc

    return pl.pallas_call(
        body, name="ada_wgrad", grid=(L, N // _ADA_COLS),
        in_specs=[_full((D, B)), pl.BlockSpec((1, B, _ADA_COLS), lambda l, j: (l, 0, j))],
        out_specs=pl.BlockSpec((1, D, _ADA_COLS), lambda l, j: (l, 0, j)),
        out_shape=jax.ShapeDtypeStruct((L, D, N), f32),
        compiler_params=_cp(("parallel", "parallel")),
    )(c_all_t, dmod_cols)


def _adam_math(w, g, m, v):
    m = ADAM_B1 * m + (1.0 - ADAM_B1) * g
    v = ADAM_B2 * v + (1.0 - ADAM_B2) * (g * g)
    m_hat = m / (1.0 - ADAM_B1 ** ADAM_STEP)
    v_hat = v / (1.0 - ADAM_B2 ** ADAM_STEP)
    delta = -ADAM_LR * (m_hat / (jnp.sqrt(v_hat) + ADAM_EPS) + ADAM_WD * w)
    return delta, m, v


def _adamw(w, m, v, g, row_tile, row0=0, outs=None):
    R, C = w.shape
    b0 = row0 // row_tile

    def body(w_ref, m_ref, v_ref, g_ref, _0, _1, _2, _3, g_out, d_out, m_out, v_out):
        gv = g_ref[...]
        delta, mn, vn = _adam_math(w_ref[...], gv, m_ref[...], v_ref[...])
        g_out[...] = gv
        d_out[...] = delta
        m_out[...] = mn
        v_out[...] = vn

    pspec = pl.BlockSpec((row_tile, C), lambda i: (b0 + i, 0))
    gspec = pl.BlockSpec((row_tile, C), lambda i: (i, 0))
    anyspec = pl.BlockSpec(memory_space=pl.ANY)
    shp = jax.ShapeDtypeStruct((R, C), f32)
    if outs is None:
        outs = [lax.empty((R, C), f32) for _ in range(4)]
    return pl.pallas_call(
        body, name="adamw", grid=(g.shape[0] // row_tile,),
        in_specs=[pspec] * 3 + [gspec] + [anyspec] * 4, out_specs=[pspec] * 4, out_shape=[shp] * 4,
        input_output_aliases={4: 0, 5: 1, 6: 2, 7: 3},
        compiler_params=_cp(("parallel",), 40),
    )(*_hbm(w, m, v, g, *outs))


def _pair_sum(g5s, gots, pc):
    n = len(g5s)

    def body(pc_ref, *refs):
        for own, got, out in zip(refs[:n], refs[n:2 * n], refs[2 * n:]):
            out[0, 0] = (own[0, 0, 0].astype(f32) + got[0, 0].astype(f32)).astype(bf16)

    def half(g):
        return pl.BlockSpec((1, 1) + g.shape[-2:], lambda p, pc: (p, 0, 0, 0))

    gs = pltpu.PrefetchScalarGridSpec(
        num_scalar_prefetch=1, grid=(N_CHIPS,),
        in_specs=[pl.BlockSpec((1, 1, 1) + g.shape[-2:], lambda p, pc: (p, 0, pc[1], 0, 0)) for g in g5s]
        + [half(g) for g in gots],
        out_specs=[half(g) for g in gots],
    )
    return pl.pallas_call(
        body, name="pair_sum", grid_spec=gs, out_shape=[jax.ShapeDtypeStruct(g.shape, bf16) for g in gots],
        compiler_params=_cp(("parallel",), 48),
    )(pc, *_hbm(*g5s, *gots))


_SUM_STEPS = 2


def _sum_shards(hsums, recvs, pc):
    n = len(hsums)

    def body(pc_ref, *refs):
        for own, got, out in zip(refs[:n], refs[n:2 * n], refs[2 * n:]):
            acc = own[0, 0].astype(f32)
            for j in range(3):
                acc = acc + got[j, 0].astype(f32)
            out[0, 0] = acc

    def rows(h):
        return (h.shape[2] // _SUM_STEPS, h.shape[3])

    gs = pltpu.PrefetchScalarGridSpec(
        num_scalar_prefetch=1, grid=(_SUM_STEPS,),
        in_specs=[pl.BlockSpec((1, 1) + rows(h), lambda i, pc: (pc[0], 0, i, 0)) for h in hsums]
        + [pl.BlockSpec((3, 1) + rows(h), lambda i, pc: (0, 0, i, 0)) for h in hsums],
        out_specs=[pl.BlockSpec((1, 1) + rows(h), lambda i, pc: (0, pc[1], i, 0)) for h in hsums],
    )
    return pl.pallas_call(
        body, name="sum_shards", grid_spec=gs,
        out_shape=[jax.ShapeDtypeStruct((1, 2) + h.shape[2:], f32) for h in hsums],
        compiler_params=_cp(("parallel",), 48),
    )(pc, *_hbm(*hsums, *recvs))


def _sum8(packs):
    _, R, C = packs.shape
    tr = R // 8 if R % 64 == 0 else R

    def body(p_ref, out_ref):
        acc = p_ref[0]
        for d in range(1, 8):
            acc = acc + p_ref[d]
        out_ref[...] = acc

    return pl.pallas_call(
        body, name="sum8", grid=(R // tr,),
        in_specs=[pl.BlockSpec((8, tr, C), lambda i: (0, i, 0))],
        out_specs=pl.BlockSpec((tr, C), lambda i: (i, 0)),
        out_shape=jax.ShapeDtypeStruct((R, C), f32),
        compiler_params=_cp(("parallel",)),
    )(packs)


def _allgather8(x_shard):
    m_per, n = x_shard.shape

    def body(x_ref, out_ref, send_sems, recv_sems, local_sem):
        x, y, c = lax.axis_index("x"), lax.axis_index("y"), lax.axis_index("c")
        me, sibling = (x, y, c), (x, y, 1 - c)
        chips = [(1 - x, y), (x, 1 - y), (1 - x, 1 - y)]

        def rows(px, py, pc):
            return out_ref.at[pl.ds((4 * px + 2 * py + pc) * m_per, m_per), :]

        def copy(k, block, to, src=None):
            return pltpu.make_async_remote_copy(
                src_ref=rows(*block) if src is None else src, dst_ref=rows(*block),
                send_sem=send_sems.at[k], recv_sem=recv_sems.at[k], device_id=to, device_id_type=MESH)

        mine = pltpu.make_async_copy(x_ref, rows(*me), local_sem)
        mine.start()
        first = [copy(0, me, sibling, src=x_ref)]
        first += [copy(1 + j, me, (*chip, c), src=x_ref) for j, chip in enumerate(chips)]
        for cp in first:
            cp.start()
        passed = [copy(4 + j, (*chip, c), sibling) for j, chip in enumerate(chips)]
        for j, chip in enumerate(chips):
            copy(1 + j, (*chip, c), me).wait_recv()
            passed[j].start()
        copy(0, sibling, me).wait_recv()
        for j, chip in enumerate(chips):
            copy(4 + j, (*chip, 1 - c), me).wait_recv()
        for cp in first + passed:
            cp.wait_send()
        mine.wait()

    return pl.pallas_call(
        body, name="allgather8",
        out_shape=jax.ShapeDtypeStruct((8 * m_per, n), x_shard.dtype),
        in_specs=[pl.BlockSpec(memory_space=pltpu.VMEM)],
        out_specs=pl.BlockSpec(memory_space=pltpu.VMEM),
        scratch_shapes=[pltpu.SemaphoreType.DMA((7,)), pltpu.SemaphoreType.DMA((7,)), pltpu.SemaphoreType.DMA],
        compiler_params=_cp(None, 48),
    )(x_shard)


def _other_chips():
    x, y = lax.axis_index("x"), lax.axis_index("y")
    return [(1 - x, y), (x, 1 - y), (1 - x, 1 - y)]


_HBM = pl.BlockSpec(memory_space=pltpu.HBM)
_SEM = pl.BlockSpec(memory_space=pltpu.SEMAPHORE)
_EFFECT = pltpu.SideEffectType.DATAFLOW_SIDE_EFFECTING


def _gather_copies(srcs, lands, send_sems, recv_sems):
    x, y, c = lax.axis_index("x"), lax.axis_index("y"), lax.axis_index("c")
    return [pltpu.make_async_remote_copy(
        src_ref=srcs[a].at[:, c], dst_ref=lands[a].at[2 * x + y, :, c], send_sem=send_sems.at[3 * a + j],
        recv_sem=recv_sems.at[3 * a + j], device_id=(cx, cy, c), device_id_type=MESH)
        for a in range(len(srcs)) for j, (cx, cy) in enumerate(_other_chips())]


def _gather_start(chunks, after, name):
    sizes = [len(srcs) for srcs, _ in chunks]
    flat = [t for srcs, lands in chunks for t in list(srcs) + list(lands)]
    nflat = len(flat)
    nsem = 2 * len(chunks)

    def body(*refs):
        ins, sems, token = refs[:nflat], refs[nflat + 1:nflat + 1 + nsem], refs[-1]
        off = 0
        for k, n in enumerate(sizes):
            for cp in _gather_copies(ins[off:off + n], ins[off + n:off + 2 * n], sems[2 * k], sems[2 * k + 1]):
                cp.start()
            off += 2 * n
        token[...] = jnp.zeros_like(token)

    res = pl.pallas_call(
        body, name=name,
        out_shape=[pltpu.SemaphoreType.DMA((3 * n,)) for n in sizes for _ in range(2)]
        + [pltpu.HBM(t.shape, t.dtype) for t in flat] + [jax.ShapeDtypeStruct((8, 128), f32)],
        in_specs=[_HBM] * nflat + [pl.BlockSpec(memory_space=pl.ANY)],
        out_specs=[_SEM] * nsem + [_HBM] * nflat + [pl.BlockSpec(memory_space=pltpu.VMEM)],
        input_output_aliases={i: nsem + i for i in range(nflat)},
        compiler_params=pltpu.CompilerParams(has_side_effects=_EFFECT),
    )(*[pltpu.with_memory_space_constraint(t, pltpu.HBM) for t in flat], after)
    out, off = [], nsem
    for k, n in enumerate(sizes):
        out.append((res[2 * k], res[2 * k + 1], res[off:off + n], res[off + n:off + 2 * n]))
        off += 2 * n
    return out, res[-1]


def _gather_wait(send_sems, recv_sems, srcs, lands, after, name):
    n = len(srcs)

    def body(*refs):
        for cp in _gather_copies(refs[:n], refs[n:2 * n], refs[2 * n], refs[2 * n + 1]):
            cp.wait_send()
            cp.wait_recv()

    res = pl.pallas_call(
        body, name=name,
        out_shape=[pltpu.HBM(t.shape, t.dtype) for t in list(srcs) + list(lands)],
        in_specs=[_HBM] * (2 * n) + [_SEM, _SEM] + [pl.BlockSpec(memory_space=pl.ANY)] * len(after),
        out_specs=[_HBM] * (2 * n),
        input_output_aliases={i: i for i in range(2 * n)},
        compiler_params=pltpu.CompilerParams(has_side_effects=_EFFECT),
    )(*srcs, *lands, send_sems, recv_sems, *after)
    return res[n:]


def _gather_forward(lands):
    n = len(lands)

    def body(*refs):
        outs = refs[n:2 * n]
        send_sems, recv_sems = refs[2 * n:]
        x, y, c = lax.axis_index("x"), lax.axis_index("y"), lax.axis_index("c")
        sibling = (x, y, 1 - c)
        copies = []
        for a in range(n):
            for j, (cx, cy) in enumerate(_other_chips()):
                mine = outs[a].at[2 * cx + cy, :, c]
                cp = pltpu.make_async_remote_copy(src_ref=mine, dst_ref=mine, send_sem=send_sems.at[3 * a + j],
                                                  recv_sem=recv_sems.at[3 * a + j], device_id=sibling, device_id_type=MESH)
                cp.start()
                copies.append((cp, a, j, cx, cy))
        for cp, a, j, cx, cy in copies:
            cp.wait_send()
            theirs = outs[a].at[2 * cx + cy, :, 1 - c]
            pltpu.make_async_remote_copy(src_ref=theirs, dst_ref=theirs, send_sem=send_sems.at[3 * a + j],
                                         recv_sem=recv_sems.at[3 * a + j], device_id=sibling, device_id_type=MESH).wait_recv()

    hbm = pl.BlockSpec(memory_space=pl.ANY)
    return pl.pallas_call(
        body, name="gather_forward",
        out_shape=[jax.ShapeDtypeStruct(t.shape, t.dtype) for t in lands],
        in_specs=[hbm] * n, out_specs=[hbm] * n,
        input_output_aliases={a: a for a in range(n)},
        scratch_shapes=[pltpu.SemaphoreType.DMA((3 * n,)), pltpu.SemaphoreType.DMA((3 * n,))],
    )(*lands)


def _pair_exchange(g5s):
    n = len(g5s)

    def body(*refs):
        ins, outs = refs[:n], refs[n:2 * n]
        send_sems, recv_sems = refs[2 * n:]
        c = lax.axis_index("c")
        sibling = (lax.axis_index("x"), lax.axis_index("y"), 1 - c)
        copies = []
        for a in range(n):
            cp = pltpu.make_async_remote_copy(src_ref=ins[a].at[:, :, 1 - c], dst_ref=outs[a], send_sem=send_sems.at[a],
                                              recv_sem=recv_sems.at[a], device_id=sibling, device_id_type=MESH)
            cp.start()
            copies.append(cp)
        for cp in copies:
            cp.wait()

    hbm = pl.BlockSpec(memory_space=pl.ANY)
    return pl.pallas_call(
        body, name="pair_exchange",
        out_shape=[jax.ShapeDtypeStruct(g.shape[:2] + g.shape[3:], g.dtype) for g in g5s],
        in_specs=[hbm] * n, out_specs=[hbm] * n,
        scratch_shapes=[pltpu.SemaphoreType.DMA((n,)), pltpu.SemaphoreType.DMA((n,))],
    )(*g5s)


def _scatter_copies(srcs, lands, send_sems, recv_sems):
    c = lax.axis_index("c")
    return [pltpu.make_async_remote_copy(
        src_ref=srcs[a].at[2 * cx + cy], dst_ref=lands[a].at[j], send_sem=send_sems.at[3 * a + j],
        recv_sem=recv_sems.at[3 * a + j], device_id=(cx, cy, c), device_id_type=MESH)
        for a in range(len(srcs)) for j, (cx, cy) in enumerate(_other_chips())]


def _scatter_start(hsums, name, after=()):
    n = len(hsums)
    na = len(after)

    def body(*refs):
        srcs, lands = refs[:n], refs[n:2 * n]
        send_sems, recv_sems = refs[2 * n + na], refs[2 * n + na + 1]
        for cp in _scatter_copies(srcs, lands, send_sems, recv_sems):
            cp.start()
        refs[-1][...] = jnp.zeros_like(refs[-1])

    lands = [lax.empty((3,) + g.shape[1:], g.dtype) for g in hsums]
    res = pl.pallas_call(
        body, name=name,
        out_shape=[pltpu.SemaphoreType.DMA((3 * n,)), pltpu.SemaphoreType.DMA((3 * n,))]
        + [pltpu.HBM(g.shape, g.dtype) for g in hsums] + [pltpu.HBM(g.shape, g.dtype) for g in lands]
        + [jax.ShapeDtypeStruct((8, 128), f32)],
        in_specs=[_HBM] * (2 * n) + [pl.BlockSpec(memory_space=pl.ANY)] * na,
        out_specs=[_SEM, _SEM] + [_HBM] * (2 * n) + [pl.BlockSpec(memory_space=pltpu.VMEM)],
        input_output_aliases={i: i + 2 for i in range(2 * n)},
        compiler_params=pltpu.CompilerParams(has_side_effects=_EFFECT),
    )(*[pltpu.with_memory_space_constraint(t, pltpu.HBM) for t in list(hsums) + lands], *after)
    return (res[0], res[1], res[2:2 + n], res[2 + n:2 + 2 * n]), res[-1]


def _scatter_wait(send_sems, recv_sems, srcs, lands, after, name):
    n = len(srcs)
    extra = list(after)

    def body(*refs):
        s_refs, l_refs = refs[:n], refs[n:2 * n]
        ss, rs = refs[2 * n], refs[2 * n + 1]
        for cp in _scatter_copies(s_refs, l_refs, ss, rs):
            cp.wait_send()
            cp.wait_recv()

    res = pl.pallas_call(
        body, name=name,
        out_shape=[pltpu.HBM(g.shape, g.dtype) for g in srcs] + [pltpu.HBM(g.shape, g.dtype) for g in lands],
        in_specs=[_HBM] * (2 * n) + [_SEM, _SEM] + [pl.BlockSpec(memory_space=pl.ANY)] * len(extra),
        out_specs=[_HBM] * (2 * n),
        input_output_aliases={i: i for i in range(2 * n)},
        compiler_params=pltpu.CompilerParams(has_side_effects=_EFFECT),
    )(*srcs, *lands, send_sems, recv_sems, *extra)
    return res[:n], res[n:]


def _swap_halves(fulls):
    n = len(fulls)

    def body(*refs):
        ins, outs = refs[:n], refs[n:2 * n]
        send_sems, recv_sems = refs[2 * n:]
        c = lax.axis_index("c")
        sibling = (lax.axis_index("x"), lax.axis_index("y"), 1 - c)
        copies = []
        for a in range(n):
            cp = pltpu.make_async_remote_copy(src_ref=outs[a].at[:, c], dst_ref=outs[a].at[:, c], send_sem=send_sems.at[a],
                                              recv_sem=recv_sems.at[a], device_id=sibling, device_id_type=MESH)
            cp.start()
            copies.append(cp)
        for a, cp in enumerate(copies):
            cp.wait_send()
            theirs = outs[a].at[:, 1 - c]
            pltpu.make_async_remote_copy(src_ref=theirs, dst_ref=theirs, send_sem=send_sems.at[a], recv_sem=recv_sems.at[a],
                                         device_id=sibling, device_id_type=MESH).wait_recv()

    hbm = pl.BlockSpec(memory_space=pl.ANY)
    return pl.pallas_call(
        body, name="swap_halves",
        out_shape=[jax.ShapeDtypeStruct(p.shape, p.dtype) for p in fulls],
        in_specs=[hbm] * n, out_specs=[hbm] * n,
        input_output_aliases={a: a for a in range(n)},
        scratch_shapes=[pltpu.SemaphoreType.DMA((n,)), pltpu.SemaphoreType.DMA((n,))],
    )(*fulls)


def _to_segments(t):
    s, c = t.shape
    return t.reshape(SCAN_SEG, s // SCAN_SEG, c).transpose(1, 0, 2).reshape(s, c)


def _from_segments(t):
    s, c = t.shape
    return t.reshape(s // SCAN_SEG, SCAN_SEG, c).transpose(1, 0, 2).reshape(s, c)


def _ssm_operators(a_re, a_im, log_dt, b_re, b_im, c_re, c_im):
    lam = lax.complex(a_re, a_im)
    dt = jnp.exp(log_dt)[:, None]
    a_bar = jnp.exp(lam * dt)
    b_bar = ((a_bar - 1.0) / lam)[:, :, None] * lax.complex(b_re, b_im)
    eye = jnp.eye(N_GROUPS, dtype=f32)

    def embed_b(t):
        return (jnp.transpose(t, (0, 2, 1))[:, :, None, :] * eye[:, None, :, None]).reshape(D_SSM, D_STATE)

    def embed_c(t):
        return (jnp.transpose(t, (0, 2, 1))[:, :, None, :] * eye[:, None, :, None]).reshape(D_STATE, D_SSM)

    a2 = jnp.stack([a_bar.real.reshape(D_STATE), a_bar.imag.reshape(D_STATE)])
    return a2, embed_b(b_bar.real), embed_b(b_bar.imag), embed_c(c_re), embed_c(c_im)


def _local_step(x, target, mod, small, ffn_weights, mix_weights, grads_done):
    table = jnp.asarray(_bucket_table())
    bias = _bias_fwd(small["rel_bias"], table)
    L = DEPTH
    saved = []
    ssm_ops = []
    for l in range(L):
        sv = {}
        m9 = mod[l]
        sv["x0"] = x
        sv["w0"] = ffn_weights(l, 0, x)
        x, sv["f0"], sv["g0"], sv["u0"], sv["h0"] = _ffn_fwd(x, m9[0:3], *sv["w0"], 0, small["ln_g"][l, 0:1], small["ln_b"][l, 0:1])
        sv["x1"] = x
        sv["w1"] = mix_weights(l, x)
        *qkv, z_rest, sv["h1"] = _mix_in_fwd(x, m9[3:6], sv["w1"][0], 0)
        S = x.shape[0]
        qkv = [t.reshape(3, S, D_ATT) for t in qkv]
        att = [_att_fwd(qkv[b], bias, b) for b in range(3)]
        y_att, lse3 = _att_merge([att[b][0].reshape(d, S // d, D_ATT) for b, d in enumerate(DILATIONS)],
                                 [att[b][1].reshape(d, S // d, _LANES) for b, d in enumerate(DILATIONS)])
        sv.update(qkv=qkv, lse=[a[1] for a in att], lse3=lse3, y_att=y_att)

        prm = tuple(small[k][l] for k in ("ssm_a_re", "ssm_a_im", "ssm_log_dt", "ssm_b_re", "ssm_b_im", "ssm_c_re", "ssm_c_im"))
        (a2, bre, bim, cre, cim), ops_vjp = jax.vjp(_ssm_operators, *prm)
        ssm_ops.append(ops_vjp)
        u_ssm = _to_segments(z_rest[:, :D_SSM])
        sr, si = _ssm_states(u_ssm, bre, bim, a2)
        dskip = small["ssm_d"][l][None, :]
        glu_b = small["glu_b"][l][None, :]
        out_seg, y_seg = _ssm_out(sr, si, u_ssm, cre, cim, dskip, small["glu_w"][l], glu_b)
        y_ssm = _from_segments(out_seg)
        sv.update(a2=a2, bre=bre, bim=bim, cre=cre, cim=cim, u_ssm=u_ssm, sr=sr, si=si, y_seg=y_seg, y_ssm=y_ssm)

        u_pool = jnp.concatenate([jnp.zeros((POOL_HALO, D_POOL), f32), z_rest[:, D_SSM:]])
        y_pool = _pool_fwd(u_pool, small["pool_w"][l], small["pool_scale"][l][None, :])
        sv.update(u_pool=u_pool, y_pool=y_pool)

        x, sv["ymix"] = _mix_out_fwd(x, y_att, y_ssm, y_pool, m9[3:6], sv["w1"][1], 0, small["ln_g"][l, 1:2], small["ln_b"][l, 1:2])
        sv["x2"] = x
        sv["w2"] = ffn_weights(l, 1, x)
        x, sv["f2"], sv["g2"], sv["u2"], sv["h2"] = _ffn_fwd(x, m9[6:9], *sv["w2"], 0, small["ln_g"][l, 2:3], small["ln_b"][l, 2:3])
        saved.append(sv)

    loss, dx = _loss_head(x, target)

    dmod = [None] * L
    dln_g = [None] * L
    dln_b = [None] * L
    sg = {k: [None] * L for k in ("ssm_a_re", "ssm_a_im", "ssm_log_dt", "ssm_b_re", "ssm_b_im", "ssm_c_re", "ssm_c_im",
                                  "ssm_d", "glu_w", "glu_b", "pool_w", "pool_scale")}
    dbias_tot = None
    order_after = jnp.zeros((), f32)
    for l in reversed(range(L)):
        sv = saved[l]
        m9 = mod[l] + order_after

        def fresh(like):
            return [lax.empty(t.shape, bf16) for t in like]

        dx, dg, du, a, df, dm2, dlg2, dlb2 = _ffn_bwd(dx, sv["x2"], sv["f2"], sv["g2"], sv["u2"], m9[6:9], *sv["w2"], 0,
                                                     small["ln_g"][l, 2:3])
        g_ffn1 = _ffn_wgrad(sv["h2"], dg, du, a, df, *fresh(sv["w2"]), 0)
        dxr, d_att, d_ssm, d_pool, dgate1, dlg1, dlb1, g_w_out = _mix_out_bwd(
            dx, sv["x1"], sv["ymix"], sv["y_att"], sv["y_ssm"], sv["y_pool"], m9[3:6], sv["w1"][1], 0, small["ln_g"][l, 1:2],
            fresh(sv["w1"])[1])
        S = d_att.shape[0]
        merged = _att_merge_bwd(d_att, sv["y_att"], sv["lse3"])
        dqkv, dbias = [], []
        for b, d in enumerate(DILATIONS):
            dq_b, db_b = _att_bwd(sv["qkv"][b], merged[b].reshape(S, D_ATT), sv["lse"][b], merged[3 + b].reshape(S, _LANES), bias, b)
            dqkv.append(dq_b.reshape(3, d, S // d, D_ATT))
            dbias.append(db_b)
        dbias = jnp.stack(dbias)
        dbias_tot = dbias if dbias_tot is None else dbias_tot + dbias
        d_seg = _to_segments(d_ssm)
        dskip = small["ssm_d"][l][None, :]
        glu_b = small["glu_b"][l][None, :]
        dy_seg, du_skip, dcre, dcim, dd, dglu_b, dglu_w = _ssm_out_bwd(
            d_seg, sv["y_seg"], sv["u_ssm"], sv["sr"], sv["si"], dskip, small["glu_w"][l], glu_b)
        du_seg, dbre, dbim, da2 = _ssm_states_bwd(dy_seg, du_skip, sv["u_ssm"], sv["sr"], sv["si"], sv["cre"], sv["cim"],
                                                  sv["bre"], sv["bim"], sv["a2"])
        d_prm = ssm_ops[l]((da2, dbre, dbim, dcre, dcim))
        for k, v in zip(("ssm_a_re", "ssm_a_im", "ssm_log_dt", "ssm_b_re", "ssm_b_im", "ssm_c_re", "ssm_c_im"), d_prm):
            sg[k][l] = v
        sg["ssm_d"][l] = dd[0]
        sg["glu_b"][l] = dglu_b[0]
        sg["glu_w"][l] = dglu_w
        du_ssm = _from_segments(du_seg)
        dyp = jnp.concatenate([d_pool, jnp.zeros((POOL_HALO, D_POOL), f32)])
        du_pool, dpw, dps = _pool_bwd(dyp, sv["u_pool"], small["pool_w"][l], small["pool_scale"][l][None, :])
        sg["pool_w"][l] = dpw
        sg["pool_scale"][l] = dps[0]
        d_rest = jnp.concatenate([du_ssm, du_pool], axis=1).astype(bf16)
        dx, dm1, dz = _mix_in_bwd(dqkv, d_rest, dxr, sv["x1"], m9[3:6], sv["w1"][0], 0)
        g_w_in = _mix_in_wgrad(sv["h1"], dz, fresh(sv["w1"])[0], 0)
        ffn_names = ("ffn_w_gate", "ffn_w_up", "ffn_w_down")
        m9 = m9 + grads_done(l, 1, list(zip(ffn_names, [(2 * l + 1) * FF_SHARD] * 3, g_ffn1))
                             + [("w_in", l * D_MODEL, g_w_in), ("w_out", l * 256, g_w_out)])
        dm1 = jnp.concatenate([dm1[0:2], dgate1])
        dx, dg, du, a, df, dm0, dlg0, dlb0 = _ffn_bwd(dx, sv["x0"], sv["f0"], sv["g0"], sv["u0"], m9[0:3], *sv["w0"], 0,
                                                     small["ln_g"][l, 0:1])
        g_ffn0 = _ffn_wgrad(sv["h0"], dg, du, a, df, *fresh(sv["w0"]), 0)
        order_after = grads_done(l, 0, list(zip(ffn_names, [2 * l * FF_SHARD] * 3, g_ffn0)))
        dmod[l] = jnp.concatenate([dm0, dm1, dm2])
        dln_g[l] = jnp.concatenate([dlg0, dlg1, dlg2])
        dln_b[l] = jnp.concatenate([dlb0, dlb1, dlb2])

    small_grads = {k: jnp.stack(v) for k, v in sg.items()}
    small_grads["rel_bias"] = _bias_bwd(dbias_tot, table)
    small_grads["ln_g"] = jnp.stack(dln_g)
    small_grads["ln_b"] = jnp.stack(dln_b)
    return loss, dx, jnp.stack(dmod), small_grads


_TILE_ELEMS = 8 * 128


def _pack_rows(shapes):
    out, row = [], 0
    for s in shapes:
        nr = -(-int(np.prod(s)) // _TILE_ELEMS) * 8
        out.append((row, nr))
        row += nr
    return out


def _pack(arrs):
    parts = []
    for a in arrs:
        flat = a.reshape(-1).astype(f32)
        npad = -(-flat.shape[0] // _TILE_ELEMS) * _TILE_ELEMS
        parts.append(jnp.pad(flat, (0, npad - flat.shape[0])).reshape(npad // 128, 128))
    return jnp.concatenate(parts, axis=0)


def _unpack(buf, shapes):
    return [buf[row:row + nr].reshape(-1)[:int(np.prod(s))].reshape(s) for s, (row, nr) in zip(shapes, _pack_rows(shapes))]


_REPL = ("rel_bias", "ada_b", "ssm_a_re", "ssm_a_im", "ssm_log_dt", "ssm_b_re", "ssm_b_im", "ssm_c_re", "ssm_c_im",
         "ssm_d", "glu_b", "pool_w", "pool_scale")
_SMALL_SHARDED = ("ln_g", "ln_b", "glu_w")
_BIG = ("ffn_w_gate", "ffn_w_up", "ffn_w_down", "w_in", "w_out")
_ORDER = ("rel_bias", "ada_w", "ada_b", "ln_g", "ln_b", "ffn_w_gate", "ffn_w_up", "ffn_w_down", "w_in", "w_out",
          "ssm_a_re", "ssm_a_im", "ssm_log_dt", "ssm_b_re", "ssm_b_im", "ssm_c_re", "ssm_c_im", "ssm_d", "glu_w",
          "glu_b", "pool_w", "pool_scale")


def kernel(x, c, rel_bias, ada_w, ada_b, ln_g, ln_b, ffn_w_gate, ffn_w_up, ffn_w_down, w_in, w_out, ssm_a_re, ssm_a_im, ssm_log_dt, ssm_b_re, ssm_b_im, ssm_c_re, ssm_c_im, ssm_d, glu_w, glu_b, pool_w, pool_scale, loss_target, m_rel_bias, m_ada_w, m_ada_b, m_ln_g, m_ln_b, m_ffn_w_gate, m_ffn_w_up, m_ffn_w_down, m_w_in, m_w_out, m_ssm_a_re, m_ssm_a_im, m_ssm_log_dt, m_ssm_b_re, m_ssm_b_im, m_ssm_c_re, m_ssm_c_im, m_ssm_d, m_glu_w, m_glu_b, m_pool_w, m_pool_scale, v_rel_bias, v_ada_w, v_ada_b, v_ln_g, v_ln_b, v_ffn_w_gate, v_ffn_w_up, v_ffn_w_down, v_w_in, v_w_out, v_ssm_a_re, v_ssm_a_im, v_ssm_log_dt, v_ssm_b_re, v_ssm_b_im, v_ssm_c_re, v_ssm_c_im, v_ssm_d, v_glu_w, v_glu_b, v_pool_w, v_pool_scale):
    args = dict(locals())
    w = {k: args[k] for k in _ORDER}
    m = {k: args["m_" + k] for k in _ORDER}
    v = {k: args["v_" + k] for k in _ORDER}
    L, D = DEPTH, D_MODEL
    ax, ay, ac = lax.axis_index("x"), lax.axis_index("y"), lax.axis_index("c")
    p_me = 2 * ax + ay
    dev = 4 * ax + 2 * ay + ac

    transposed = ("ffn_w_gate", "ffn_w_up")
    for d in (w, m, v):
        for name in transposed:
            d[name] = jnp.swapaxes(d[name], 2, 3)

    def halves(t):
        return t.astype(bf16).reshape(1, 2, t.shape[0] // 2, t.shape[1])

    def landing(src):
        return lax.dynamic_update_slice(lax.empty((N_CHIPS,) + src.shape, bf16), src[None], (p_me, 0, 0, 0, 0))

    chunk_keys = [("ffn", 0, 0), ("mix", 0), ("ffn", 0, 1), ("ffn", 1, 0), ("mix", 1), ("ffn", 1, 1)]
    chunk_srcs = []
    for key in chunk_keys:
        if key[0] == "ffn":
            chunk_srcs.append([halves(w[name][key[1], key[2]]) for name in ("ffn_w_gate", "ffn_w_up", "ffn_w_down")])
        else:
            chunk_srcs.append([halves(w_in[key[1]]), halves(w_out[key[1]])])

    pack = _pack([c, ln_g, ln_b, glu_w])
    rows = pack.shape[0]
    allp = _allgather8(pack).reshape(8, rows, 128)
    chunks = [(srcs, [landing(t) for t in srcs]) for srcs in chunk_srcs]
    first_in_flight, first_begun = _gather_start(chunks[:1], allp, "gather_start_first")
    c_all = allp[:, :8].reshape(8, D) + first_begun[0, 0]
    by_chip = allp[0::2]

    fwd_rows = _pack_rows([c.shape, ln_g.shape, ln_b.shape, glu_w.shape])

    def sharded(part, shape, axis):
        row0, nrows = fwd_rows[part]
        t = by_chip[:, row0:row0 + nrows].reshape(N_CHIPS, -1)[:, :int(np.prod(shape))].reshape((N_CHIPS,) + shape)
        return jnp.concatenate([t[p] for p in range(N_CHIPS)], axis=axis)

    ln_g_full = sharded(1, ln_g.shape, 2)
    ln_b_full = sharded(2, ln_b.shape, 2)
    glu_w_full = sharded(3, glu_w.shape, 1)

    ncol = ada_w.shape[-1]
    ada_b_cols = lax.dynamic_slice_in_dim(ada_b, p_me * ncol, ncol, axis=1)[:, None, :]
    mod_part = _ada_fwd(c_all, ada_w, ada_b_cols)
    mrows = L * 8 * ncol // 128
    mod_all = _allgather8(mod_part.reshape(mrows, 128)).reshape(8, L, 8, ncol)
    mod_mine = lax.dynamic_index_in_dim(mod_all, dev, axis=2, keepdims=False)
    mod = jnp.concatenate([mod_mine[2 * p] for p in range(N_CHIPS)], axis=-1).reshape(L, 9, D)

    rest_in_flight, rest_begun = _gather_start(chunks[1:], mod, "gather_start_rest")
    in_flight = first_in_flight + rest_in_flight

    def gathered(key, after):
        k = chunk_keys.index(key)
        lands = _gather_forward(_gather_wait(*in_flight[k], [after, rest_begun], "gather_wait_%d" % k))
        return [t.reshape(N_CHIPS, 1, 2 * t.shape[3], t.shape[4]) for t in lands]

    pc = jnp.stack([p_me, ac]).astype(jnp.int32)
    groups = {}
    scattering = {}

    def start_group(tag, after=()):
        g5 = [g.reshape(g.shape[:2] + (2, g.shape[2] // 2, g.shape[3])) for _, _, g in groups[tag]]
        hsum = _pair_sum(g5, _pair_exchange(g5), pc)
        scattering[tag], begun = _scatter_start(hsum, "scatter_start_%s" % tag, after)
        return begun

    def grads_done(l, s, grads):
        if l == 1:
            groups.setdefault("l1", []).extend(grads)
            return start_group("l1")[0, 0] if s == 0 else jnp.zeros((), f32)
        groups["l0a" if s == 1 else "l0b"] = grads
        return start_group("l0a")[0, 0] if s == 1 else jnp.zeros((), f32)

    small = {k: w[k] for k in _REPL if k != "ada_b"}
    small.update(ln_g=ln_g_full, ln_b=ln_b_full, glu_w=glu_w_full)
    loss_dev, grad_x, dmod, sgrads = _local_step(
        x[0], loss_target[0], mod, small, lambda l, s, after: gathered(("ffn", l, s), after),
        lambda l, after: gathered(("mix", l), after), grads_done)
    loss = lax.psum(loss_dev[0, 0], ("x", "y", "c"))

    names = ("rel_bias", "ln_g", "ln_b", "ssm_a_re", "ssm_a_im", "ssm_log_dt", "ssm_b_re", "ssm_b_im", "ssm_c_re",
             "ssm_c_im", "ssm_d", "glu_w", "glu_b", "pool_w", "pool_scale")
    gpack = _pack([dmod] + [sgrads[k] for k in names])
    grows = gpack.shape[0]
    gall = _allgather8(gpack).reshape(8, grows, 128)
    l0b_begun = start_group("l0b", (gall,))
    gsum = _unpack(_sum8(gall), [(L, 9 * D)] + [sgrads[k].shape for k in names])
    red = dict(zip(("ada_b",) + names, gsum))
    red["ln_g"] = lax.dynamic_slice_in_dim(red["ln_g"], p_me * 256, 256, axis=2)
    red["ln_b"] = lax.dynamic_slice_in_dim(red["ln_b"], p_me * 256, 256, axis=2)
    red["glu_w"] = lax.dynamic_slice_in_dim(red["glu_w"], p_me * 64, 64, axis=1)

    dmod_all = gall[:, :L * 9 * D // 128].reshape(8, L, 9 * D)
    dmod_cols = jnp.transpose(lax.dynamic_slice_in_dim(dmod_all, p_me * ncol, ncol, axis=2), (1, 0, 2))
    g_ada_w = _ada_wgrad(jnp.transpose(c_all), dmod_cols)

    out_g, out_d, out_m, out_v = {}, {}, {}, {}
    r2 = (L * D, ncol)
    res = _adamw(ada_w.reshape(r2), m["ada_w"].reshape(r2), v["ada_w"].reshape(r2), g_ada_w.reshape(r2), 128)
    out_g["ada_w"], out_d["ada_w"], out_m["ada_w"], out_v["ada_w"] = [t.reshape(ada_w.shape) for t in res]

    small_names = _REPL + _SMALL_SHARDED
    wp = _pack([w[k] for k in small_names])
    res_small = _adamw(wp, _pack([m[k] for k in small_names]), _pack([v[k] for k in small_names]),
                       _pack([red[k] for k in small_names]), wp.shape[0])
    for t, dst in zip(res_small, (out_g, out_d, out_m, out_v)):
        for k, a in zip(small_names, _unpack(t, [w[k].shape for k in small_names])):
            dst[k] = a

    row_tile = dict(zip(_BIG, (352, 352, 352, 256, 256)))
    big = {name: None for name in _BIG}
    after = [grad_x, res_small[1], res[1], l0b_begun]
    for tag in ("l1", "l0a", "l0b"):
        hsum, recv = _scatter_wait(*scattering[tag], after, "scatter_wait_%s" % tag)
        full = _swap_halves(_sum_shards(hsum, recv, pc))
        for (name, row0, _), g in zip(groups[tag], full):
            shp = w[name].shape
            r2 = (int(np.prod(shp[:-1])), shp[-1])
            big[name] = _adamw(w[name].reshape(r2), m[name].reshape(r2), v[name].reshape(r2), g.reshape(-1, shp[-1]),
                               row_tile[name], row0, big[name])
        after = [big[name][1] for name, _, _ in groups[tag]]
    for name in _BIG:
        res = [t.reshape(w[name].shape) for t in big[name]]
        out_g[name], out_d[name], out_m[name], out_v[name] = [jnp.swapaxes(t, 2, 3) for t in res] if name in transposed else res

    return (loss, grad_x[None], *[out_g[k] for k in _ORDER], *[out_d[k] for k in _ORDER],
            *[out_m[k] for k in _ORDER], *[out_v[k] for k in _ORDER])
```

```python
import functools
import math

import numpy as np
import jax
import jax.numpy as jnp
from jax import lax
from jax.experimental import pallas as pl
from jax.experimental.pallas import tpu as pltpu

f32 = jnp.float32
bf16 = jnp.bfloat16
MESH = pl.DeviceIdType.MESH

D_MODEL = 1024
SEQ = 2048
DEPTH = 2
HEAD_DIM = 64
N_HEADS = 8
D_ATT = 512
DILATIONS = (1, 4, 16)
BLOCKS_PER_RESIDUE = (16, 4, 1)
ATT_BLOCK = 128
N_UNITS = SEQ // ATT_BLOCK
N_GROUPS = 16
SSM_GROUP = 16
SSM_STATE = 64
D_SSM = 256
D_STATE = N_GROUPS * SSM_STATE
POOL_WINDOWS = (2, 4, 8, 16)
POOL_GROUP = 64
D_POOL = 256
POOL_HALO = 16
D_FF = 2816
N_BUCKETS = 32
MAX_DISTANCE = 2048
ALPHA = (2 * DEPTH) ** 0.25
FFN_RES = 0.5
LN_EPS = 1e-5
NEG = -1e30
N_CHIPS = 4
FF_SHARD = D_FF // N_CHIPS
SCAN_SEG = 8
SCAN_STEPS = SEQ // SCAN_SEG

ADAM_LR, ADAM_B1, ADAM_B2, ADAM_EPS, ADAM_WD, ADAM_STEP = 0.001, 0.9, 0.999, 1e-08, 0.01, 10

TOK_TILE = 512


def _cp(dims=None, vmem_mb=None):
    kw = {}
    if dims is not None:
        kw["dimension_semantics"] = dims
    if vmem_mb is not None:
        kw["vmem_limit_bytes"] = vmem_mb << 20
    return pltpu.CompilerParams(**kw)


def _dot(a, b):
    return jnp.dot(a, b, preferred_element_type=f32)


def _dot_nt(a, b):
    return lax.dot_general(a, b, (((1,), (1,)), ((), ())), preferred_element_type=f32)


def _dot_tn(a, b):
    return lax.dot_general(a, b, (((0,), (0,)), ((), ())), preferred_element_type=f32)


def _ln_stats(v):
    mu = jnp.mean(v, -1, keepdims=True)
    d = v - mu
    var = jnp.mean(d * d, -1, keepdims=True)
    rstd = lax.rsqrt(var + LN_EPS)
    return d * rstd, rstd


def _ln_bwd(dxh, xh, rstd):
    return rstd * (dxh - jnp.mean(dxh, -1, keepdims=True) - xh * jnp.mean(dxh * xh, -1, keepdims=True))


_GELU_C = math.sqrt(2.0 / math.pi)


def _gelu(y):
    return 0.5 * y * (1.0 + jnp.tanh(_GELU_C * (y + 0.044715 * y * y * y)))


def _gelu_grad(y):
    t = jnp.tanh(_GELU_C * (y + 0.044715 * y * y * y))
    return 0.5 * (1.0 + t) + 0.5 * y * (1.0 - t * t) * (_GELU_C * (1.0 + 3 * 0.044715 * y * y))


def _full(shape):
    return pl.BlockSpec(shape, lambda *_: (0,) * len(shape))


def _hbm(*args):
    return [pltpu.with_memory_space_constraint(a, pltpu.HBM) if getattr(a, "ndim", 0) >= 2 else a for a in args]


def _ffn_fwd(x, mod3, wg, wu, wd, ls, lng, lnb):
    S, D = x.shape
    Fs = wg.shape[-2]
    ts = TOK_TILE

    def body(x_ref, mod_ref, wg_ref, wu_ref, wd_ref, lng_ref, lnb_ref, xo_ref, f_ref, g_ref, u_ref, h_ref, acc_sc):
        j = pl.program_id(1)

        @pl.when(j == 0)
        def _():
            xh, _ = _ln_stats(x_ref[...])
            h_ref[...] = (xh * (1.0 + mod_ref[1:2, :]) + mod_ref[0:1, :]).astype(bf16)
            acc_sc[...] = jnp.zeros_like(acc_sc)

        h = h_ref[...]
        g = _dot_nt(h, wg_ref[0, 0])
        u = _dot_nt(h, wu_ref[0, 0])
        g_ref[0] = g.astype(bf16)
        u_ref[0] = u.astype(bf16)
        a = (g * jax.nn.sigmoid(g) * u).astype(bf16)
        acc_sc[...] += _dot(a, wd_ref[0, 0])

        @pl.when(j == N_CHIPS - 1)
        def _():
            f = acc_sc[...]
            f_ref[...] = f
            r = ALPHA * x_ref[...] + (FFN_RES * mod_ref[2:3, :]) * f
            rh, _ = _ln_stats(r)
            xo_ref[...] = rh * lng_ref[...] + lnb_ref[...]

    tok = pl.BlockSpec((ts, D), lambda i, j: (i, 0))
    wrow = pl.BlockSpec((1, 1, Fs, D), lambda i, j: (j, ls, 0, 0))
    hid = pl.BlockSpec((1, ts, Fs), lambda i, j: (j, i, 0))
    return pl.pallas_call(
        body, name="ffn_fwd", grid=(S // ts, N_CHIPS),
        in_specs=[tok, _full((3, D)), wrow, wrow, wrow, _full((1, D)), _full((1, D))],
        out_specs=[tok, tok, hid, hid, tok],
        out_shape=[jax.ShapeDtypeStruct((S, D), f32), jax.ShapeDtypeStruct((S, D), f32),
                   jax.ShapeDtypeStruct((N_CHIPS, S, Fs), bf16), jax.ShapeDtypeStruct((N_CHIPS, S, Fs), bf16),
                   jax.ShapeDtypeStruct((S, D), bf16)],
        scratch_shapes=[pltpu.VMEM((ts, D), f32)],
        compiler_params=_cp(("parallel", "arbitrary"), 56),
    )(*_hbm(x, mod3, wg, wu, wd, lng, lnb))


def _ffn_bwd(dxo, x, f, g, u, mod3, wg, wu, wd, ls, lng):
    S, D = x.shape
    Fs = wg.shape[-2]
    ts = TOK_TILE

    def body(dxo_ref, x_ref, f_ref, g_ref, u_ref, mod_ref, wg_ref, wu_ref, wd_ref, lng_ref,
             dx_ref, dg_ref, du_ref, a_ref, df_ref, dmod_ref, dlng_ref, dlnb_ref,
             dr_sc, df_sc, acc_sc):
        i = pl.program_id(0)
        j = pl.program_id(1)

        @pl.when((i == 0) & (j == 0))
        def _():
            dmod_ref[...] = jnp.zeros_like(dmod_ref)
            dlng_ref[...] = jnp.zeros_like(dlng_ref)
            dlnb_ref[...] = jnp.zeros_like(dlnb_ref)

        @pl.when(j == 0)
        def _():
            xv = x_ref[...]
            fv = f_ref[...]
            gate = mod_ref[2:3, :]
            rh, rstd = _ln_stats(ALPHA * xv + (FFN_RES * gate) * fv)
            dy = dxo_ref[...]
            dlng_ref[...] += jnp.sum(dy * rh, 0, keepdims=True)
            dlnb_ref[...] += jnp.sum(dy, 0, keepdims=True)
            dr = _ln_bwd(dy * lng_ref[...], rh, rstd)
            dr_sc[...] = dr
            dmod_ref[2:3, :] += jnp.sum(FFN_RES * dr * fv, 0, keepdims=True)
            df = ((FFN_RES * gate) * dr).astype(bf16)
            df_sc[...] = df
            df_ref[...] = df
            acc_sc[...] = jnp.zeros_like(acc_sc)

        da = _dot_nt(df_sc[...], wd_ref[0, 0])
        gv = g_ref[0].astype(f32)
        uv = u_ref[0].astype(f32)
        sg = jax.nn.sigmoid(gv)
        si = gv * sg
        a_ref[0] = (si * uv).astype(bf16)
        dgv = (da * uv * (sg * (1.0 + gv * (1.0 - sg)))).astype(bf16)
        duv = (da * si).astype(bf16)
        dg_ref[0] = dgv
        du_ref[0] = duv
        acc_sc[...] += _dot(dgv, wg_ref[0, 0]) + _dot(duv, wu_ref[0, 0])

        @pl.when(j == N_CHIPS - 1)
        def _():
            dh = acc_sc[...]
            xh, rstd0 = _ln_stats(x_ref[...])
            dmod_ref[0:1, :] += jnp.sum(dh, 0, keepdims=True)
            dmod_ref[1:2, :] += jnp.sum(dh * xh, 0, keepdims=True)
            dx_ref[...] = _ln_bwd(dh * (1.0 + mod_ref[1:2, :]), xh, rstd0) + ALPHA * dr_sc[...]

    tok = pl.BlockSpec((ts, D), lambda i, j: (i, 0))
    wrow = pl.BlockSpec((1, 1, Fs, D), lambda i, j: (j, ls, 0, 0))
    hid = pl.BlockSpec((1, ts, Fs), lambda i, j: (j, i, 0))
    hid_shape = jax.ShapeDtypeStruct((N_CHIPS, S, Fs), bf16)
    return pl.pallas_call(
        body, name="ffn_bwd", grid=(S // ts, N_CHIPS),
        in_specs=[tok, tok, tok, hid, hid, _full((3, D)), wrow, wrow, wrow, _full((1, D))],
        out_specs=[tok, hid, hid, hid, tok, _full((3, D)), _full((1, D)), _full((1, D))],
        out_shape=[jax.ShapeDtypeStruct((S, D), f32), hid_shape, hid_shape, hid_shape,
                   jax.ShapeDtypeStruct((S, D), bf16),
                   jax.ShapeDtypeStruct((3, D), f32), jax.ShapeDtypeStruct((1, D), f32), jax.ShapeDtypeStruct((1, D), f32)],
        scratch_shapes=[pltpu.VMEM((ts, D), f32), pltpu.VMEM((ts, D), bf16), pltpu.VMEM((ts, D), f32)],
        compiler_params=_cp(("arbitrary", "arbitrary"), 56),
    )(*_hbm(dxo, x, f, g, u, mod3, wg, wu, wd, lng))


def _ffn_wgrad(h, dg, du, a, df, gwg, gwu, gwd, ls):
    S, D = h.shape
    Fs = dg.shape[-1]
    tk = TOK_TILE
    nk = S // tk

    def body(h_ref, dg_ref, du_ref, a_ref, df_ref, _g0, _g1, _g2, gwg_ref, gwu_ref, gwd_ref, ag_sc, au_sc, ad_sc):
        k = pl.program_id(1)

        @pl.when(k == 0)
        def _():
            ag_sc[...] = jnp.zeros_like(ag_sc)
            au_sc[...] = jnp.zeros_like(au_sc)
            ad_sc[...] = jnp.zeros_like(ad_sc)

        hv = h_ref[...]
        ag_sc[...] += _dot_tn(dg_ref[0], hv)
        au_sc[...] += _dot_tn(du_ref[0], hv)
        ad_sc[...] += _dot_tn(a_ref[0], df_ref[...])

        @pl.when(k == nk - 1)
        def _():
            gwg_ref[0, 0] = ag_sc[...].astype(bf16)
            gwu_ref[0, 0] = au_sc[...].astype(bf16)
            gwd_ref[0, 0] = ad_sc[...].astype(bf16)

    tok = pl.BlockSpec((tk, D), lambda p, k: (k, 0))
    hid = pl.BlockSpec((1, tk, Fs), lambda p, k: (p, k, 0))
    anyspec = pl.BlockSpec(memory_space=pl.ANY)
    orow = pl.BlockSpec((1, 1, Fs, D), lambda p, k: (p, ls, 0, 0))
    return pl.pallas_call(
        body, name="ffn_wgrad", grid=(N_CHIPS, nk),
        in_specs=[tok, hid, hid, hid, tok, anyspec, anyspec, anyspec],
        out_specs=[orow, orow, orow],
        out_shape=[jax.ShapeDtypeStruct(gwg.shape, bf16), jax.ShapeDtypeStruct(gwu.shape, bf16),
                   jax.ShapeDtypeStruct(gwd.shape, bf16)],
        scratch_shapes=[pltpu.VMEM((Fs, D), f32), pltpu.VMEM((Fs, D), f32), pltpu.VMEM((Fs, D), f32)],
        input_output_aliases={5: 0, 6: 1, 7: 2},
        compiler_params=_cp(("parallel", "arbitrary"), 48),
    )(*_hbm(h, dg, du, a, df, gwg, gwu, gwd))


_LANES = 128
_QKV_BLOCKS = D_ATT // _LANES


def _res_spec(lead, d, width, index):
    return pl.BlockSpec((lead, d, TOK_TILE // d, width), index)


def _res_spec3(d, width):
    return pl.BlockSpec((d, TOK_TILE // d, width), lambda i: (0, i, 0))


def _rows_to_residues(tile_bufs, d, put):
    for r in range(d):
        for cb, buf in enumerate(tile_bufs):
            put(r, cb, buf[pl.ds(r, TOK_TILE // d, stride=d), :])


def _residues_to_rows(tile_bufs, d, get):
    for r in range(d):
        for cb, buf in enumerate(tile_bufs):
            buf[pl.ds(r, TOK_TILE // d, stride=d), :] = get(r, cb)


def _mix_in_fwd(x, mod3, w_in, l):
    S, D = x.shape
    N = w_in.shape[-1]
    ts = TOK_TILE

    def body(x_ref, mod_ref, w_ref, o1_ref, o4_ref, o16_ref, zr_ref, h_ref, *bufs):
        j = pl.program_id(1)

        @pl.when(j == 0)
        def _():
            xh, _ = _ln_stats(x_ref[...])
            h_ref[...] = (xh * (1.0 + mod_ref[1:2, :]) + mod_ref[0:1, :]).astype(bf16)

        z = _dot(h_ref[...], w_ref[0, 0])

        @pl.when(j == N_CHIPS - 1)
        def _():
            zr_ref[...] = z

        @pl.when(j < N_CHIPS - 1)
        def _():
            zz = z * jnp.where(j == 0, HEAD_DIM ** -0.5, 1.0)
            o1_ref[j, 0] = zz.astype(bf16)
            for cb, buf in enumerate(bufs):
                buf[...] = zz[:, _LANES * cb:_LANES * (cb + 1)]
            for d, o_ref in zip(DILATIONS[1:], (o4_ref, o16_ref)):
                def put(r, cb, piece, o_ref=o_ref):
                    o_ref[j, r, :, _LANES * cb:_LANES * (cb + 1)] = piece.astype(bf16)
                _rows_to_residues(bufs, d, put)

    tok = pl.BlockSpec((ts, D), lambda i, j: (i, 0))
    res = [_res_spec(3, d, N, lambda i, j: (0, 0, i, 0)) for d in DILATIONS]
    return pl.pallas_call(
        body, name="mix_in_fwd", grid=(S // ts, N_CHIPS),
        in_specs=[tok, _full((3, D)), pl.BlockSpec((1, 1, D, N), lambda i, j: (j, l, 0, 0))],
        out_specs=res + [pl.BlockSpec((ts, N), lambda i, j: (i, 0)), tok],
        out_shape=[jax.ShapeDtypeStruct((3, d, S // d, N), bf16) for d in DILATIONS]
        + [jax.ShapeDtypeStruct((S, N), f32), jax.ShapeDtypeStruct((S, D), bf16)],
        scratch_shapes=[pltpu.VMEM((ts, _LANES), f32)] * _QKV_BLOCKS,
        compiler_params=_cp(("parallel", "arbitrary"), 40),
    )(*_hbm(x, mod3, w_in))


def _mix_in_bwd(dqkv, d_rest, dx_res, x, mod3, w_in, l):
    S, D = x.shape
    N = w_in.shape[-1]
    ts = TOK_TILE

    def body(d1_ref, d4_ref, d16_ref, dr_ref, dxr_ref, x_ref, mod_ref, w_ref, dx_ref, dmod_ref, dz_ref, acc_sc, *bufs):
        i = pl.program_id(0)
        j = pl.program_id(1)

        @pl.when((i == 0) & (j == 0))
        def _():
            dmod_ref[...] = jnp.zeros_like(dmod_ref)

        @pl.when(j == 0)
        def _():
            acc_sc[...] = jnp.zeros_like(acc_sc)

        @pl.when(j == N_CHIPS - 1)
        def _():
            dz_ref[0] = dr_ref[...]

        @pl.when(j < N_CHIPS - 1)
        def _():
            for d, d_ref, tile_bufs in ((4, d4_ref, bufs[:_QKV_BLOCKS]), (16, d16_ref, bufs[_QKV_BLOCKS:])):
                _residues_to_rows(tile_bufs, d, lambda r, cb, d_ref=d_ref: d_ref[0, r, :, _LANES * cb:_LANES * (cb + 1)].astype(f32))
            for cb in range(_QKV_BLOCKS):
                cols = slice(_LANES * cb, _LANES * (cb + 1))
                dz_ref[0, :, cols] = (d1_ref[0, 0, :, cols].astype(f32) + bufs[cb][...] + bufs[_QKV_BLOCKS + cb][...]).astype(bf16)

        acc_sc[...] += _dot_nt(dz_ref[0], w_ref[0, 0])

        @pl.when(j == N_CHIPS - 1)
        def _():
            dh = acc_sc[...]
            xh, rstd0 = _ln_stats(x_ref[...])
            dmod_ref[0:1, :] += jnp.sum(dh, 0, keepdims=True)
            dmod_ref[1:2, :] += jnp.sum(dh * xh, 0, keepdims=True)
            dx_ref[...] = _ln_bwd(dh * (1.0 + mod_ref[1:2, :]), xh, rstd0) + dxr_ref[...]

    tok = pl.BlockSpec((ts, D), lambda i, j: (i, 0))
    res = [_res_spec(1, d, N, lambda i, j: (jnp.minimum(j, 2), 0, i, 0)) for d in DILATIONS]
    return pl.pallas_call(
        body, name="mix_in_bwd", grid=(S // ts, N_CHIPS),
        in_specs=res + [pl.BlockSpec((ts, N), lambda i, j: (i, 0)), tok, tok, _full((3, D)),
                        pl.BlockSpec((1, 1, D, N), lambda i, j: (j, l, 0, 0))],
        out_specs=[tok, _full((3, D)), pl.BlockSpec((1, ts, N), lambda i, j: (j, i, 0))],
        out_shape=[jax.ShapeDtypeStruct((S, D), f32), jax.ShapeDtypeStruct((3, D), f32),
                   jax.ShapeDtypeStruct((N_CHIPS, S, N), bf16)],
        scratch_shapes=[pltpu.VMEM((ts, D), f32)] + [pltpu.VMEM((ts, _LANES), f32)] * (2 * _QKV_BLOCKS),
        compiler_params=_cp(("arbitrary", "arbitrary"), 40),
    )(*_hbm(*dqkv, d_rest, dx_res, x, mod3, w_in))


def _mix_in_wgrad(h, dz, gw, l):
    S, D = h.shape
    N = dz.shape[-1]
    tk = TOK_TILE
    nk = S // tk

    def body(h_ref, dz_ref, _g, gw_ref, acc_sc):
        k = pl.program_id(1)

        @pl.when(k == 0)
        def _():
            acc_sc[...] = jnp.zeros_like(acc_sc)

        acc_sc[...] += _dot_tn(h_ref[...], dz_ref[0])

        @pl.when(k == nk - 1)
        def _():
            gw_ref[0, 0] = acc_sc[...].astype(bf16)

    return pl.pallas_call(
        body, name="mix_in_wgrad", grid=(N_CHIPS, nk),
        in_specs=[pl.BlockSpec((tk, D), lambda p, k: (k, 0)), pl.BlockSpec((1, tk, N), lambda p, k: (p, k, 0)),
                  pl.BlockSpec(memory_space=pl.ANY)],
        out_specs=pl.BlockSpec((1, 1, D, N), lambda p, k: (p, l, 0, 0)),
        out_shape=jax.ShapeDtypeStruct(gw.shape, bf16),
        scratch_shapes=[pltpu.VMEM((D, N), f32)],
        input_output_aliases={2: 0},
        compiler_params=_cp(("parallel", "arbitrary"), 40),
    )(*_hbm(h, dz, gw))


def _mix_out_fwd(x, y_att, y_ssm, y_pool, mod3, w_out, l, lng, lnb):
    S, D = x.shape
    ts = TOK_TILE

    def body(x_ref, ya_ref, ys_ref, yp_ref, mod_ref, w_ref, lng_ref, lnb_ref, xo_ref, y_ref):
        ya = ya_ref[...].astype(bf16)
        y = (_dot(ya[:, 0:256], w_ref[0, 0]) + _dot(ya[:, 256:512], w_ref[1, 0])
             + _dot(ys_ref[...].astype(bf16), w_ref[2, 0]) + _dot(yp_ref[...].astype(bf16), w_ref[3, 0]))
        y_ref[...] = y
        rh, _ = _ln_stats(ALPHA * x_ref[...] + mod_ref[2:3, :] * y)
        xo_ref[...] = rh * lng_ref[...] + lnb_ref[...]

    tok = pl.BlockSpec((ts, D), lambda i: (i, 0))
    return pl.pallas_call(
        body, name="mix_out_fwd", grid=(S // ts,),
        in_specs=[tok, pl.BlockSpec((ts, D_ATT), lambda i: (i, 0)), pl.BlockSpec((ts, D_SSM), lambda i: (i, 0)),
                  pl.BlockSpec((ts, D_POOL), lambda i: (i, 0)), _full((3, D)),
                  pl.BlockSpec((N_CHIPS, 1, 256, D), lambda i: (0, l, 0, 0)), _full((1, D)), _full((1, D))],
        out_specs=[tok, tok],
        out_shape=[jax.ShapeDtypeStruct((S, D), f32), jax.ShapeDtypeStruct((S, D), f32)],
        compiler_params=_cp(("parallel",), 40),
    )(*_hbm(x, y_att, y_ssm, y_pool, mod3, w_out, lng, lnb))


def _mix_out_bwd(dxo, x, y, y_att, y_ssm, y_pool, mod3, w_out, l, lng, gw_out):
    S, D = x.shape
    ts = TOK_TILE
    nt = S // ts

    def body(dxo_ref, x_ref, y_ref, ya_ref, ys_ref, yp_ref, mod_ref, w_ref, lng_ref, _g,
             dxr_ref, da_ref, ds_ref, dp_ref, dgate_ref, dlng_ref, dlnb_ref, gw_ref, acc_sc):
        i = pl.program_id(0)

        @pl.when(i == 0)
        def _():
            dgate_ref[...] = jnp.zeros_like(dgate_ref)
            dlng_ref[...] = jnp.zeros_like(dlng_ref)
            dlnb_ref[...] = jnp.zeros_like(dlnb_ref)
            acc_sc[...] = jnp.zeros_like(acc_sc)

        gate = mod_ref[2:3, :]
        yv = y_ref[...]
        rh, rstd = _ln_stats(ALPHA * x_ref[...] + gate * yv)
        dy_out = dxo_ref[...]
        dlng_ref[...] += jnp.sum(dy_out * rh, 0, keepdims=True)
        dlnb_ref[...] += jnp.sum(dy_out, 0, keepdims=True)
        dr = _ln_bwd(dy_out * lng_ref[...], rh, rstd)
        dxr_ref[...] = ALPHA * dr
        dgate_ref[...] += jnp.sum(dr * yv, 0, keepdims=True)
        dy = (gate * dr).astype(bf16)
        da_ref[:, 0:256] = _dot_nt(dy, w_ref[0, 0])
        da_ref[:, 256:512] = _dot_nt(dy, w_ref[1, 0])
        ds_ref[...] = _dot_nt(dy, w_ref[2, 0])
        dp_ref[...] = _dot_nt(dy, w_ref[3, 0])
        ya = ya_ref[...].astype(bf16)
        acc_sc[0] += _dot_tn(ya[:, 0:256], dy)
        acc_sc[1] += _dot_tn(ya[:, 256:512], dy)
        acc_sc[2] += _dot_tn(ys_ref[...].astype(bf16), dy)
        acc_sc[3] += _dot_tn(yp_ref[...].astype(bf16), dy)

        @pl.when(i == nt - 1)
        def _():
            gw_ref[:, 0] = acc_sc[...].astype(bf16)

    tok = pl.BlockSpec((ts, D), lambda i: (i, 0))
    t512 = pl.BlockSpec((ts, D_ATT), lambda i: (i, 0))
    t256 = pl.BlockSpec((ts, 256), lambda i: (i, 0))
    wspec = pl.BlockSpec((N_CHIPS, 1, 256, D), lambda i: (0, l, 0, 0))
    return pl.pallas_call(
        body, name="mix_out_bwd", grid=(nt,),
        in_specs=[tok, tok, tok, t512, t256, t256, _full((3, D)), wspec, _full((1, D)), pl.BlockSpec(memory_space=pl.ANY)],
        out_specs=[tok, t512, t256, t256, _full((1, D)), _full((1, D)), _full((1, D)), wspec],
        out_shape=[jax.ShapeDtypeStruct((S, D), f32), jax.ShapeDtypeStruct((S, D_ATT), f32),
                   jax.ShapeDtypeStruct((S, D_SSM), f32), jax.ShapeDtypeStruct((S, D_POOL), f32),
                   jax.ShapeDtypeStruct((1, D), f32), jax.ShapeDtypeStruct((1, D), f32), jax.ShapeDtypeStruct((1, D), f32),
                   jax.ShapeDtypeStruct(gw_out.shape, bf16)],
        scratch_shapes=[pltpu.VMEM((N_CHIPS, 256, D), f32)],
        input_output_aliases={9: 7},
        compiler_params=_cp(("arbitrary",), 48),
    )(*_hbm(dxo, x, y, y_att, y_ssm, y_pool, mod3, w_out, lng, gw_out))


def _t5_bucket(dist):
    max_exact = N_BUCKETS // 2
    d = np.maximum(dist, 1).astype(np.float32)
    large = max_exact + (np.log(d / max_exact) / math.log(MAX_DISTANCE / max_exact)
                         * (N_BUCKETS - max_exact)).astype(np.int32)
    large = np.minimum(large, N_BUCKETS - 1)
    return np.where(dist < max_exact, dist, large).astype(np.int32)


def _bucket_table():
    q = ATT_BLOCK
    i = np.arange(q)[:, None]
    j = np.arange(2 * q)[None, :]
    r = i + q - j
    in_band = (r >= 0) & (r <= q)
    tabs = [np.where(in_band, _t5_bucket(np.clip(r, 0, None) * d), -1) for d in DILATIONS]
    return np.stack(tabs).astype(np.int32)


def _bias_fwd(rel_bias, table):
    def body(rb_ref, tab_ref, out_ref):
        for b in range(3):
            tb = tab_ref[b]
            for h in range(N_HEADS):
                def pick(k, acc):
                    return jnp.where(tb == k, rb_ref[k, h], acc)
                out_ref[b, h] = lax.fori_loop(0, N_BUCKETS, pick, jnp.where(tb < 0, NEG, 0.0).astype(f32))

    return pl.pallas_call(
        body, name="bias_fwd",
        in_specs=[pl.BlockSpec(memory_space=pltpu.SMEM), pl.BlockSpec(memory_space=pltpu.VMEM)],
        out_specs=pl.BlockSpec(memory_space=pltpu.VMEM),
        out_shape=jax.ShapeDtypeStruct((3, N_HEADS, ATT_BLOCK, 2 * ATT_BLOCK), f32),
    )(rel_bias, table)


def _bias_bwd(dbias, table):
    def body(db_ref, tab_ref, out_ref):
        def per_bucket(k, c):
            for h in range(N_HEADS):
                tot = jnp.zeros((), f32)
                for b in range(3):
                    tot = tot + jnp.sum(jnp.where(tab_ref[b] == k, db_ref[b, h], 0.0))
                out_ref[k, h] = tot
            return c
        lax.fori_loop(0, N_BUCKETS, per_bucket, 0)

    return pl.pallas_call(
        body, name="bias_bwd",
        in_specs=[pl.BlockSpec(memory_space=pltpu.VMEM), pl.BlockSpec(memory_space=pltpu.VMEM)],
        out_specs=pl.BlockSpec(memory_space=pltpu.SMEM),
        out_shape=jax.ShapeDtypeStruct((N_BUCKETS, N_HEADS), f32),
    )(dbias, table)


def _att_unit(u, nbr):
    rows = pl.ds(pl.multiple_of(u * ATT_BLOCK, ATT_BLOCK), ATT_BLOCK)
    prev = pl.ds(pl.multiple_of(jnp.maximum(u - 1, 0) * ATT_BLOCK, ATT_BLOCK), ATT_BLOCK)
    return rows, prev, (u % nbr) != 0


_HEAD_ROWS = N_HEADS * ATT_BLOCK


def _head_rows(t):
    lane = lax.broadcasted_iota(jnp.int32, t.shape, 1)
    return jnp.concatenate([jnp.where((lane >= HEAD_DIM * h) & (lane < HEAD_DIM * (h + 1)), t, jnp.zeros_like(t))
                            for h in range(N_HEADS)], axis=0)


def _head_cols(big):
    lane = lax.broadcasted_iota(jnp.int32, (ATT_BLOCK, D_ATT), 1)
    out = big[0:ATT_BLOCK]
    for h in range(1, N_HEADS):
        out = jnp.where(lane >= HEAD_DIM * h, big[ATT_BLOCK * h:ATT_BLOCK * (h + 1)], out)
    return out


def _head_column(ref, rows):
    t = ref[rows, :]
    return jnp.concatenate([t[:, h:h + 1] for h in range(N_HEADS)], axis=0)


def _att_band(ref, rows, prev, nbr):
    cur = ref[0, rows, :]
    return cur if nbr == 1 else jnp.concatenate([ref[0, prev, :], cur], axis=0)


def _att_scores(q_ref, k_ref, b_ref, rows, prev, valid_prev, nbr):
    qbd = _head_rows(q_ref[0, rows, :])
    kb = _att_band(k_ref, rows, prev, nbr)
    bias = b_ref[0].reshape(_HEAD_ROWS, 2 * ATT_BLOCK)
    if nbr == 1:
        return qbd, kb, _dot_nt(qbd, kb) + bias[:, ATT_BLOCK:]
    s = _dot_nt(qbd, kb) + bias
    col = lax.broadcasted_iota(jnp.int32, s.shape, 1)
    return qbd, kb, jnp.where((col >= ATT_BLOCK) | valid_prev, s, NEG)


def _qkv_specs(S, branch):
    return ([pl.BlockSpec((1, S, D_ATT), lambda i, t=t: (t, 0, 0)) for t in range(3)],
            pl.BlockSpec((1, N_HEADS, ATT_BLOCK, 2 * ATT_BLOCK), lambda i: (branch, 0, 0, 0)))


def _att_fwd(qkv, bias, branch):
    S = qkv.shape[1]
    nbr = BLOCKS_PER_RESIDUE[branch]

    def body(q_ref, k_ref, v_ref, b_ref, o_ref, lse_ref):
        lse_ref[...] = jnp.zeros_like(lse_ref)

        def unit(u, c):
            rows, prev, valid_prev = _att_unit(u, nbr)
            _, _, s = _att_scores(q_ref, k_ref, b_ref, rows, prev, valid_prev, nbr)
            m = jnp.max(s, -1, keepdims=True)
            p = jnp.exp(s - m)
            den = jnp.sum(p, -1, keepdims=True)
            big = _dot(p.astype(bf16), _att_band(v_ref, rows, prev, nbr))
            o_ref[rows, :] = _head_cols(big / den)
            lse = m + jnp.log(den)
            for h in range(N_HEADS):
                lse_ref[rows, pl.ds(h, 1)] = lse[ATT_BLOCK * h:ATT_BLOCK * (h + 1)]
            return c

        lax.fori_loop(0, N_UNITS, unit, 0)

    qkv_specs, bspec = _qkv_specs(S, branch)
    return pl.pallas_call(
        body, name="att_fwd", grid=(1,),
        in_specs=qkv_specs + [bspec],
        out_specs=[pl.BlockSpec((S, D_ATT), lambda i: (0, 0)), pl.BlockSpec((S, _LANES), lambda i: (0, 0))],
        out_shape=[jax.ShapeDtypeStruct((S, D_ATT), f32), jax.ShapeDtypeStruct((S, _LANES), f32)],
        compiler_params=_cp(("arbitrary",), 40),
    )(*_hbm(qkv, qkv, qkv, bias))


def _att_bwd(qkv, do, lse, crow, bias, branch):
    S = qkv.shape[1]
    nbr = BLOCKS_PER_RESIDUE[branch]

    def body(q_ref, k_ref, v_ref, do_ref, lse_ref, c_ref, b_ref, dqkv_ref, db_ref, dk_sc, dv_sc):
        dk_sc[...] = jnp.zeros_like(dk_sc)
        dv_sc[...] = jnp.zeros_like(dv_sc)
        db_ref[...] = jnp.zeros_like(db_ref)

        def unit(u, c):
            rows, prev, valid_prev = _att_unit(u, nbr)
            qbd, kb, s = _att_scores(q_ref, k_ref, b_ref, rows, prev, valid_prev, nbr)
            p = jnp.exp(s - _head_column(lse_ref, rows))
            dobd = _head_rows(do_ref[rows, :])
            ds = p * (_dot_nt(dobd, _att_band(v_ref, rows, prev, nbr)) - _head_column(c_ref, rows))
            if nbr == 1:
                db_ref[:, :, ATT_BLOCK:] += ds.reshape(N_HEADS, ATT_BLOCK, ATT_BLOCK)
            else:
                db_ref[...] += ds.reshape(N_HEADS, ATT_BLOCK, 2 * ATT_BLOCK)
            dsb = ds.astype(bf16)
            dqkv_ref[0, rows, :] = (HEAD_DIM ** -0.5 * _head_cols(_dot(dsb, kb))).astype(bf16)
            dkb = _dot_tn(dsb, qbd)
            dvb = _dot_tn(p.astype(bf16), dobd)
            if nbr == 1:
                dk_sc[rows, :] += dkb
                dv_sc[rows, :] += dvb
            else:
                dk_sc[prev, :] += dkb[:ATT_BLOCK]
                dv_sc[prev, :] += dvb[:ATT_BLOCK]
                dk_sc[rows, :] += dkb[ATT_BLOCK:]
                dv_sc[rows, :] += dvb[ATT_BLOCK:]
            return c

        lax.fori_loop(0, N_UNITS, unit, 0)
        dqkv_ref[1] = dk_sc[...].astype(bf16)
        dqkv_ref[2] = dv_sc[...].astype(bf16)

    qkv_specs, bspec = _qkv_specs(S, branch)
    row = pl.BlockSpec((S, _LANES), lambda i: (0, 0))
    return pl.pallas_call(
        body, name="att_bwd", grid=(1,),
        in_specs=qkv_specs + [pl.BlockSpec((S, D_ATT), lambda i: (0, 0)), row, row, bspec],
        out_specs=[pl.BlockSpec((3, S, D_ATT), lambda i: (0, 0, 0)),
                   pl.BlockSpec((N_HEADS, ATT_BLOCK, 2 * ATT_BLOCK), lambda i: (0, 0, 0))],
        out_shape=[jax.ShapeDtypeStruct((3, S, D_ATT), bf16), jax.ShapeDtypeStruct((N_HEADS, ATT_BLOCK, 2 * ATT_BLOCK), f32)],
        scratch_shapes=[pltpu.VMEM((S, D_ATT), f32), pltpu.VMEM((S, D_ATT), f32)],
        compiler_params=_cp(("arbitrary",), 48),
    )(*_hbm(qkv, qkv, qkv, do, lse, crow, bias))


def _branch_weights(lse_ref):
    l0, l1, l2 = lse_ref[0], lse_ref[1], lse_ref[2]
    m = jnp.maximum(jnp.maximum(l0, l1), l2)
    e0, e1, e2 = jnp.exp(l0 - m), jnp.exp(l1 - m), jnp.exp(l2 - m)
    tot = e0 + e1 + e2
    return e0 / tot, e1 / tot, e2 / tot


def _att_merge(os, lses):
    S = os[0].shape[0] * os[0].shape[1]
    ts = TOK_TILE

    def body(o1_ref, o4_ref, o16_ref, l1_ref, l4_ref, l16_ref, y_ref, lt_ref, *bufs):
        obufs = (bufs[:_QKV_BLOCKS], bufs[_QKV_BLOCKS:2 * _QKV_BLOCKS])
        lt_ref[0] = l1_ref[0]
        for k, (d, o_ref, l_ref) in enumerate(((4, o4_ref, l4_ref), (16, o16_ref, l16_ref))):
            _residues_to_rows(obufs[k], d, lambda r, cb, o_ref=o_ref: o_ref[r, :, _LANES * cb:_LANES * (cb + 1)])
            _residues_to_rows([bufs[2 * _QKV_BLOCKS + k]], d, lambda r, cb, l_ref=l_ref: l_ref[r])
            lt_ref[1 + k] = bufs[2 * _QKV_BLOCKS + k][...]
        w = _branch_weights(lt_ref)
        for h in range(N_HEADS):
            cs = slice(HEAD_DIM * h, HEAD_DIM * (h + 1))
            half = slice(HEAD_DIM * (h % 2), HEAD_DIM * (h % 2 + 1))
            y_ref[:, cs] = (w[0][:, h:h + 1] * o1_ref[0, :, cs] + w[1][:, h:h + 1] * obufs[0][h // 2][:, half]
                            + w[2][:, h:h + 1] * obufs[1][h // 2][:, half])

    return pl.pallas_call(
        body, name="att_merge", grid=(S // ts,),
        in_specs=[_res_spec3(d, D_ATT) for d in DILATIONS] + [_res_spec3(d, _LANES) for d in DILATIONS],
        out_specs=[pl.BlockSpec((ts, D_ATT), lambda i: (i, 0)), pl.BlockSpec((3, ts, _LANES), lambda i: (0, i, 0))],
        out_shape=[jax.ShapeDtypeStruct((S, D_ATT), f32), jax.ShapeDtypeStruct((3, S, _LANES), f32)],
        scratch_shapes=[pltpu.VMEM((ts, _LANES), f32)] * (2 * _QKV_BLOCKS + 2),
        compiler_params=_cp(("parallel",)),
    )(*_hbm(*os, *lses))


def _att_merge_bwd(dy, y, lse3):
    S = dy.shape[0]
    ts = TOK_TILE

    def body(dy_ref, y_ref, lse_ref, do1_ref, do4_ref, do16_ref, c1_ref, c4_ref, c16_ref, *bufs):
        dobufs = (bufs[:_QKV_BLOCKS], bufs[_QKV_BLOCKS:2 * _QKV_BLOCKS], bufs[2 * _QKV_BLOCKS:3 * _QKV_BLOCKS])
        cbufs = bufs[3 * _QKV_BLOCKS:]
        w = _branch_weights(lse_ref)
        for cb in cbufs:
            cb[...] = jnp.zeros_like(cb)
        for h in range(N_HEADS):
            cs = slice(HEAD_DIM * h, HEAD_DIM * (h + 1))
            half = slice(HEAD_DIM * (h % 2), HEAD_DIM * (h % 2 + 1))
            dyh = dy_ref[:, cs]
            t = jnp.sum(dyh * y_ref[:, cs], -1, keepdims=True)
            for p in range(3):
                wp = w[p][:, h:h + 1]
                dobufs[p][h // 2][:, half] = wp * dyh
                cbufs[p][:, h:h + 1] = wp * t
        for cb in range(_QKV_BLOCKS):
            do1_ref[0, :, _LANES * cb:_LANES * (cb + 1)] = dobufs[0][cb][...].astype(bf16)
        c1_ref[0] = cbufs[0][...]
        for k, (d, do_ref, c_ref) in enumerate(((4, do4_ref, c4_ref), (16, do16_ref, c16_ref))):
            def put_do(r, cb, piece, do_ref=do_ref):
                do_ref[r, :, _LANES * cb:_LANES * (cb + 1)] = piece.astype(bf16)

            def put_c(r, cb, piece, c_ref=c_ref):
                c_ref[r] = piece

            _rows_to_residues(dobufs[1 + k], d, put_do)
            _rows_to_residues([cbufs[1 + k]], d, put_c)

    return pl.pallas_call(
        body, name="att_merge_bwd", grid=(S // ts,),
        in_specs=[pl.BlockSpec((ts, D_ATT), lambda i: (i, 0)), pl.BlockSpec((ts, D_ATT), lambda i: (i, 0)),
                  pl.BlockSpec((3, ts, _LANES), lambda i: (0, i, 0))],
        out_specs=[_res_spec3(d, D_ATT) for d in DILATIONS] + [_res_spec3(d, _LANES) for d in DILATIONS],
        out_shape=[jax.ShapeDtypeStruct((d, S // d, D_ATT), bf16) for d in DILATIONS]
        + [jax.ShapeDtypeStruct((d, S // d, _LANES), f32) for d in DILATIONS],
        scratch_shapes=[pltpu.VMEM((ts, _LANES), f32)] * (3 * _QKV_BLOCKS + 3),
        compiler_params=_cp(("parallel",)),
    )(*_hbm(dy, y, lse3))


_SSM_ROWS = 256


def _scan_in_place(sr_ref, si_ref, a_ref, reverse):
    S, N = sr_ref.shape
    nst = S // SCAN_SEG
    ar = jnp.broadcast_to(a_ref[0:1, :], (SCAN_SEG, N))
    ai = jnp.broadcast_to(a_ref[1:2, :], (SCAN_SEG, N))
    if reverse:
        ai = -ai
    row = lax.broadcasted_iota(jnp.int32, (SCAN_SEG, N), 0)
    zero = jnp.zeros((SCAN_SEG, N), f32)

    def tile(t):
        return pl.ds(pl.multiple_of((nst - 1 - t if reverse else t) * SCAN_SEG, SCAN_SEG), SCAN_SEG)

    def local(t, c):
        sr, si, pr, pi = c
        rows = tile(t)
        nsr = ar * sr - ai * si + sr_ref[rows, :]
        nsi = ar * si + ai * sr + si_ref[rows, :]
        sr_ref[rows, :] = nsr
        si_ref[rows, :] = nsi
        return nsr, nsi, ar * pr - ai * pi, ar * pi + ai * pr

    fr, fi, apr, api = lax.fori_loop(0, nst, local, (zero, zero, zero + 1.0, zero))

    def shift(v):
        if reverse:
            return jnp.where(row == SCAN_SEG - 1, 0.0, pltpu.roll(v, SCAN_SEG - 1, axis=0))
        return jnp.where(row == 0, 0.0, pltpu.roll(v, 1, axis=0))

    cr, ci = zero, zero
    for _ in range(SCAN_SEG - 1):
        cr, ci = shift(fr + apr * cr - api * ci), shift(fi + apr * ci + api * cr)

    def fix(t, c):
        pr, pi = c
        npr, npi = ar * pr - ai * pi, ar * pi + ai * pr
        rows = tile(t)
        sr_ref[rows, :] += npr * cr - npi * ci
        si_ref[rows, :] += npr * ci + npi * cr
        return npr, npi

    lax.fori_loop(0, nst, fix, (zero + 1.0, zero))


def _ssm_states(u, bre, bim, a2):
    S = u.shape[0]

    def body(u_ref, br_ref, bi_ref, a_ref, sr_ref, si_ref):
        brb = br_ref[...].astype(bf16)
        bib = bi_ref[...].astype(bf16)

        def project(t, c):
            rows = pl.ds(pl.multiple_of(t * _SSM_ROWS, _SSM_ROWS), _SSM_ROWS)
            ub = u_ref[rows, :].astype(bf16)
            sr_ref[rows, :] = _dot(ub, brb)
            si_ref[rows, :] = _dot(ub, bib)
            return c

        lax.fori_loop(0, S // _SSM_ROWS, project, 0)
        _scan_in_place(sr_ref, si_ref, a_ref, False)

    vm = pl.BlockSpec(memory_space=pltpu.VMEM)
    return pl.pallas_call(
        body, name="ssm_states", in_specs=[vm] * 4, out_specs=[vm, vm],
        out_shape=[jax.ShapeDtypeStruct((S, D_STATE), f32)] * 2,
        compiler_params=_cp(None, 48),
    )(u, bre, bim, a2)


def _ssm_out(sr, si, u, cre, cim, dskip, glu_w, glu_b):
    S = u.shape[0]
    ts = TOK_TILE

    def body(sr_ref, si_ref, u_ref, cr_ref, ci_ref, d_ref, w_ref, b_ref, out_ref, y_ref):
        y = (_dot(sr_ref[...].astype(bf16), cr_ref[...].astype(bf16))
             - _dot(si_ref[...].astype(bf16), ci_ref[...].astype(bf16)) + d_ref[...] * u_ref[...])
        y_ref[...] = y
        z = _dot(_gelu(y).astype(bf16), w_ref[...].astype(bf16)) + b_ref[...]
        out_ref[...] = y * jax.nn.sigmoid(z)

    st = pl.BlockSpec((ts, D_STATE), lambda i: (i, 0))
    ch = pl.BlockSpec((ts, D_SSM), lambda i: (i, 0))
    return pl.pallas_call(
        body, name="ssm_out", grid=(S // ts,),
        in_specs=[st, st, ch, _full((D_STATE, D_SSM)), _full((D_STATE, D_SSM)), _full((1, D_SSM)),
                  _full((D_SSM, D_SSM)), _full((1, D_SSM))],
        out_specs=[ch, ch],
        out_shape=[jax.ShapeDtypeStruct((S, D_SSM), f32)] * 2,
        compiler_params=_cp(("parallel",)),
    )(*_hbm(sr, si, u, cre, cim, dskip, glu_w, glu_b))


def _ssm_out_bwd(dout, y, u, sr, si, dskip, glu_w, glu_b):
    S = u.shape[0]
    ts = TOK_TILE

    def body(do_ref, y_ref, u_ref, sr_ref, si_ref, d_ref, w_ref, b_ref,
             dy_ref, du_ref, dcr_ref, dci_ref, dd_ref, dgb_ref, dgw_ref):
        @pl.when(pl.program_id(0) == 0)
        def _():
            for r in (dcr_ref, dci_ref, dd_ref, dgb_ref, dgw_ref):
                r[...] = jnp.zeros_like(r)

        y = y_ref[...]
        dout = do_ref[...]
        wb = w_ref[...].astype(bf16)
        ge = _gelu(y).astype(bf16)
        sz = jax.nn.sigmoid(_dot(ge, wb) + b_ref[...])
        dz = dout * y * sz * (1.0 - sz)
        dzb = dz.astype(bf16)
        dgb_ref[...] += jnp.sum(dz, 0, keepdims=True)
        dgw_ref[...] += _dot_tn(ge, dzb)
        dy = dout * sz + _gelu_grad(y) * _dot_nt(dzb, wb)
        uv = u_ref[...]
        dd_ref[...] += jnp.sum(dy * uv, 0, keepdims=True)
        du_ref[...] = dy * d_ref[...]
        dy_ref[...] = dy
        dyb = dy.astype(bf16)
        dcr_ref[...] += _dot_tn(sr_ref[...].astype(bf16), dyb)
        dci_ref[...] -= _dot_tn(si_ref[...].astype(bf16), dyb)

    st = pl.BlockSpec((ts, D_STATE), lambda i: (i, 0))
    ch = pl.BlockSpec((ts, D_SSM), lambda i: (i, 0))
    c_full = _full((D_STATE, D_SSM))
    return pl.pallas_call(
        body, name="ssm_out_bwd", grid=(S // ts,),
        in_specs=[ch, ch, ch, st, st, _full((1, D_SSM)), _full((D_SSM, D_SSM)), _full((1, D_SSM))],
        out_specs=[ch, ch, c_full, c_full, _full((1, D_SSM)), _full((1, D_SSM)), _full((D_SSM, D_SSM))],
        out_shape=[jax.ShapeDtypeStruct((S, D_SSM), f32), jax.ShapeDtypeStruct((S, D_SSM), f32),
                   jax.ShapeDtypeStruct((D_STATE, D_SSM), f32), jax.ShapeDtypeStruct((D_STATE, D_SSM), f32),
                   jax.ShapeDtypeStruct((1, D_SSM), f32), jax.ShapeDtypeStruct((1, D_SSM), f32),
                   jax.ShapeDtypeStruct((D_SSM, D_SSM), f32)],
        compiler_params=_cp(("arbitrary",), 40),
    )(*_hbm(dout, y, u, sr, si, dskip, glu_w, glu_b))


def _ssm_states_bwd(dy, du_skip, u, sr, si, cre, cim, bre, bim, a2):
    S = u.shape[0]
    N = D_STATE
    nst = S // SCAN_SEG
    nproj = S // _SSM_ROWS

    def body(dy_ref, dus_ref, u_ref, sr_ref, si_ref, cr_ref, ci_ref, br_ref, bi_ref, a_ref,
             du_ref, dbr_ref, dbi_ref, da_ref, lr_ref, li_ref):
        crb = cr_ref[...].astype(bf16)
        cib = ci_ref[...].astype(bf16)

        def project(t, c):
            rows = pl.ds(pl.multiple_of(t * _SSM_ROWS, _SSM_ROWS), _SSM_ROWS)
            dyb = dy_ref[rows, :].astype(bf16)
            lr_ref[rows, :] = _dot_nt(dyb, crb)
            li_ref[rows, :] = -_dot_nt(dyb, cib)
            return c

        lax.fori_loop(0, nproj, project, 0)
        _scan_in_place(lr_ref, li_ref, a_ref, True)

        row = lax.broadcasted_iota(jnp.int32, (SCAN_SEG, N), 0)
        last = pl.ds((nst - 1) * SCAN_SEG, SCAN_SEG)
        pr = jnp.where(row == 0, 0.0, pltpu.roll(sr_ref[last, :], 1, axis=0))
        pi = jnp.where(row == 0, 0.0, pltpu.roll(si_ref[last, :], 1, axis=0))
        first = pl.ds(0, SCAN_SEG)
        acc_r = lr_ref[first, :] * pr + li_ref[first, :] * pi
        acc_i = li_ref[first, :] * pr - lr_ref[first, :] * pi

        def step(t, c):
            acc_r, acc_i = c
            rows = pl.ds(pl.multiple_of(t * SCAN_SEG, SCAN_SEG), SCAN_SEG)
            prev = pl.ds(pl.multiple_of((t - 1) * SCAN_SEG, SCAN_SEG), SCAN_SEG)
            lrv, liv, srv, siv = lr_ref[rows, :], li_ref[rows, :], sr_ref[prev, :], si_ref[prev, :]
            return acc_r + lrv * srv + liv * siv, acc_i + liv * srv - lrv * siv

        acc_r, acc_i = lax.fori_loop(1, nst, step, (acc_r, acc_i))
        da_ref[0:1, :] = jnp.sum(acc_r, 0, keepdims=True)
        da_ref[1:2, :] = jnp.sum(acc_i, 0, keepdims=True)

        brb = br_ref[...].astype(bf16)
        bib = bi_ref[...].astype(bf16)
        dbr_ref[...] = jnp.zeros_like(dbr_ref)
        dbi_ref[...] = jnp.zeros_like(dbi_ref)

        def back(t, c):
            rows = pl.ds(pl.multiple_of(t * _SSM_ROWS, _SSM_ROWS), _SSM_ROWS)
            lrb = lr_ref[rows, :].astype(bf16)
            lib = li_ref[rows, :].astype(bf16)
            du_ref[rows, :] = dus_ref[rows, :] + _dot_nt(lrb, brb) + _dot_nt(lib, bib)
            ub = u_ref[rows, :].astype(bf16)
            dbr_ref[...] += _dot_tn(ub, lrb)
            dbi_ref[...] += _dot_tn(ub, lib)
            return c

        lax.fori_loop(0, nproj, back, 0)

    vm = pl.BlockSpec(memory_space=pltpu.VMEM)
    return pl.pallas_call(
        body, name="ssm_states_bwd", in_specs=[vm] * 10, out_specs=[vm] * 4,
        out_shape=[jax.ShapeDtypeStruct((S, D_SSM), f32), jax.ShapeDtypeStruct((D_SSM, D_STATE), f32),
                   jax.ShapeDtypeStruct((D_SSM, D_STATE), f32), jax.ShapeDtypeStruct((2, D_STATE), f32)],
        scratch_shapes=[pltpu.VMEM((S, D_STATE), f32), pltpu.VMEM((S, D_STATE), f32)],
        compiler_params=_cp(None, 56),
    )(dy, du_skip, u, sr, si, cre, cim, bre, bim, a2)


_POOL_TILE = 256


def _window_sums(xt, back):
    n = xt.shape[0]
    out = []
    ws = xt
    for k in (1, 2, 4, 8):
        ws = ws + pltpu.roll(ws, k if back else n - k, axis=0)
        out.append(ws)
    return out


def _pool_count(r0, w):
    t = r0 + lax.broadcasted_iota(jnp.int32, (_POOL_TILE, POOL_GROUP), 0)
    return jnp.minimum(t + 1, w).astype(f32)


def _pool_fwd(u_pad, pool_w, pool_scale):
    S = u_pad.shape[0] - POOL_HALO
    nt = S // _POOL_TILE

    def body(u_ref, w_ref, sc_ref, y_ref):
        def tile(t, c):
            r0 = pl.multiple_of(t * _POOL_TILE, _POOL_TILE)
            for g, w in enumerate(POOL_WINDOWS):
                cs = pl.ds(POOL_GROUP * g, POOL_GROUP)
                xt = u_ref[pl.ds(r0, _POOL_TILE + POOL_HALO), cs]
                ws = _window_sums(xt, True)[g][POOL_HALO:, :]
                pooled = ws / _pool_count(r0, w) - xt[POOL_HALO:, :]
                y_ref[pl.ds(r0, _POOL_TILE), cs] = _dot(pooled.astype(bf16), w_ref[g].astype(bf16)) * sc_ref[:, cs]
            return c
        lax.fori_loop(0, nt, tile, 0)

    vm = pl.BlockSpec(memory_space=pltpu.VMEM)
    return pl.pallas_call(
        body, name="pool_fwd", in_specs=[vm, vm, vm], out_specs=vm,
        out_shape=jax.ShapeDtypeStruct((S, D_POOL), f32),
    )(u_pad, pool_w, pool_scale)


def _pool_bwd(dy_pad, u_pad, pool_w, pool_scale):
    S = u_pad.shape[0] - POOL_HALO
    nt = S // _POOL_TILE
    n = _POOL_TILE + POOL_HALO

    def body(dy_ref, u_ref, w_ref, sc_ref, du_ref, dw_ref, dsc_ref):
        dw_ref[...] = jnp.zeros_like(dw_ref)
        dsc_ref[...] = jnp.zeros_like(dsc_ref)

        def tile(t, c):
            r0 = pl.multiple_of(t * _POOL_TILE, _POOL_TILE)
            for g, w in enumerate(POOL_WINDOWS):
                cs = pl.ds(POOL_GROUP * g, POOL_GROUP)
                wb = w_ref[g].astype(bf16)
                xt = u_ref[pl.ds(r0, n), cs]
                pooled = (_window_sums(xt, True)[g][POOL_HALO:, :] / _pool_count(r0, w) - xt[POOL_HALO:, :]).astype(bf16)
                dy = dy_ref[pl.ds(r0, _POOL_TILE), cs]
                dsc_ref[:, cs] += jnp.sum(dy * _dot(pooled, wb), 0, keepdims=True)
                dw_ref[g] += _dot_tn(pooled, (dy * sc_ref[:, cs]).astype(bf16))
                dyh = (dy_ref[pl.ds(r0, n), cs] * sc_ref[:, cs]).astype(bf16)
                dpl = _dot_nt(dyh, wb)
                cnt = jnp.minimum(r0 + lax.broadcasted_iota(jnp.int32, (n, POOL_GROUP), 0) + 1, w).astype(f32)
                lead = _window_sums(dpl / cnt, False)[g]
                du_ref[pl.ds(r0, _POOL_TILE), cs] = lead[:_POOL_TILE, :] - dpl[:_POOL_TILE, :]
            return c
        lax.fori_loop(0, nt, tile, 0)

    vm = pl.BlockSpec(memory_space=pltpu.VMEM)
    return pl.pallas_call(
        body, name="pool_bwd", in_specs=[vm, vm, vm, vm], out_specs=[vm, vm, vm],
        out_shape=[jax.ShapeDtypeStruct((S, D_POOL), f32), jax.ShapeDtypeStruct((4, POOL_GROUP, POOL_GROUP), f32),
                   jax.ShapeDtypeStruct((1, D_POOL), f32)],
    )(dy_pad, u_pad, pool_w, pool_scale)


def _loss_head(y, target):
    S, D = y.shape
    ts = TOK_TILE

    def body(y_ref, t_ref, loss_ref, dy_ref):
        @pl.when(pl.program_id(0) == 0)
        def _():
            loss_ref[...] = jnp.zeros_like(loss_ref)

        d = y_ref[...] - t_ref[...]
        dy_ref[...] = d * (1.0 / D)
        loss_ref[...] += 0.5 * jnp.sum(jnp.sum(d * d, -1, keepdims=True) * (1.0 / D), 0, keepdims=True)

    tok = pl.BlockSpec((ts, D), lambda i: (i, 0))
    return pl.pallas_call(
        body, name="loss_head", grid=(S // ts,),
        in_specs=[tok, tok], out_specs=[_full((1, 1)), tok],
        out_shape=[jax.ShapeDtypeStruct((1, 1), f32), jax.ShapeDtypeStruct((S, D), f32)],
        compiler_params=_cp(("arbitrary",)),
    )(*_hbm(y, target))


_ADA_COLS = 768


def _ada_fwd(c_all, ada_w, ada_b_cols):
    L, D, N = ada_w.shape
    B = c_all.shape[0]

    def body(c_ref, w_ref, b_ref, out_ref):
        cv = c_ref[...]
        cond = (cv * jax.nn.sigmoid(cv)).astype(bf16)
        out_ref[0] = _dot(cond, w_ref[0].astype(bf16)) + b_ref[0]

    return pl.pallas_call(
        body, name="ada_fwd", grid=(L, N // _ADA_COLS),
        in_specs=[_full((B, D)), pl.BlockSpec((1, D, _ADA_COLS), lambda l, j: (l, 0, j)),
                  pl.BlockSpec((1, 1, _ADA_COLS), lambda l, j: (l, 0, j))],
        out_specs=pl.BlockSpec((1, B, _ADA_COLS), lambda l, j: (l, 0, j)),
        out_shape=jax.ShapeDtypeStruct((L, B, N), f32),
        compiler_params=_cp(("parallel", "parallel")),
    )(c_all, ada_w, ada_b_cols)


def _ada_wgrad(c_all_t, dmod_cols):
    D, B = c_all_t.shape
    L, _, N = dmod_cols.shape

    def body(ct_ref, dm_ref, out_ref):
        cv = ct_ref[...]
        cond = cv * jax.nn.sigmoid(cv)
        acc = cond[:, 0:1] * dm_ref[0, 0:1, :]
        for b in range(1, B):
            acc = acc + cond[:, b:b + 1] * dm_ref[0, b:b + 1, :]
        out_ref[0] = acc

    return pl.pallas_call(
        body, name="ada_wgrad", grid=(L, N // _ADA_COLS),
        in_specs=[_full((D, B)), pl.BlockSpec((1, B, _ADA_COLS), lambda l, j: (l, 0, j))],
        out_specs=pl.BlockSpec((1, D, _ADA_COLS), lambda l, j: (l, 0, j)),
        out_shape=jax.ShapeDtypeStruct((L, D, N), f32),
        compiler_params=_cp(("parallel", "parallel")),
    )(c_all_t, dmod_cols)


def _adam_math(w, g, m, v):
    m = ADAM_B1 * m + (1.0 - ADAM_B1) * g
    v = ADAM_B2 * v + (1.0 - ADAM_B2) * (g * g)
    m_hat = m / (1.0 - ADAM_B1 ** ADAM_STEP)
    v_hat = v / (1.0 - ADAM_B2 ** ADAM_STEP)
    delta = -ADAM_LR * (m_hat / (jnp.sqrt(v_hat) + ADAM_EPS) + ADAM_WD * w)
    return delta, m, v


def _adamw(w, m, v, g, row_tile, row0=0, outs=None):
    R, C = w.shape
    b0 = row0 // row_tile

    def body(w_ref, m_ref, v_ref, g_ref, _0, _1, _2, _3, g_out, d_out, m_out, v_out):
        gv = g_ref[...]
        delta, mn, vn = _adam_math(w_ref[...], gv, m_ref[...], v_ref[...])
        g_out[...] = gv
        d_out[...] = delta
        m_out[...] = mn
        v_out[...] = vn

    pspec = pl.BlockSpec((row_tile, C), lambda i: (b0 + i, 0))
    gspec = pl.BlockSpec((row_tile, C), lambda i: (i, 0))
    anyspec = pl.BlockSpec(memory_space=pl.ANY)
    shp = jax.ShapeDtypeStruct((R, C), f32)
    if outs is None:
        outs = [lax.empty((R, C), f32) for _ in range(4)]
    return pl.pallas_call(
        body, name="adamw", grid=(g.shape[0] // row_tile,),
        in_specs=[pspec] * 3 + [gspec] + [anyspec] * 4, out_specs=[pspec] * 4, out_shape=[shp] * 4,
        input_output_aliases={4: 0, 5: 1, 6: 2, 7: 3},
        compiler_params=_cp(("parallel",), 40),
    )(*_hbm(w, m, v, g, *outs))


def _pair_sum(g5s, gots, pc):
    n = len(g5s)

    def body(pc_ref, *refs):
        for own, got, out in zip(refs[:n], refs[n:2 * n], refs[2 * n:]):
            out[0, 0] = (own[0, 0, 0].astype(f32) + got[0, 0].astype(f32)).astype(bf16)

    def half(g):
        return pl.BlockSpec((1, 1) + g.shape[-2:], lambda p, pc: (p, 0, 0, 0))

    gs = pltpu.PrefetchScalarGridSpec(
        num_scalar_prefetch=1, grid=(N_CHIPS,),
        in_specs=[pl.BlockSpec((1, 1, 1) + g.shape[-2:], lambda p, pc: (p, 0, pc[1], 0, 0)) for g in g5s]
        + [half(g) for g in gots],
        out_specs=[half(g) for g in gots],
    )
    return pl.pallas_call(
        body, name="pair_sum", grid_spec=gs, out_shape=[jax.ShapeDtypeStruct(g.shape, bf16) for g in gots],
        compiler_params=_cp(("parallel",), 48),
    )(pc, *_hbm(*g5s, *gots))


_SUM_STEPS = 2


def _sum_shards(hsums, recvs, pc):
    n = len(hsums)

    def body(pc_ref, *refs):
        for own, got, out in zip(refs[:n], refs[n:2 * n], refs[2 * n:]):
            acc = own[0, 0].astype(f32)
            for j in range(3):
                acc = acc + got[j, 0].astype(f32)
            out[0, 0] = acc

    def rows(h):
        return (h.shape[2] // _SUM_STEPS, h.shape[3])

    gs = pltpu.PrefetchScalarGridSpec(
        num_scalar_prefetch=1, grid=(_SUM_STEPS,),
        in_specs=[pl.BlockSpec((1, 1) + rows(h), lambda i, pc: (pc[0], 0, i, 0)) for h in hsums]
        + [pl.BlockSpec((3, 1) + rows(h), lambda i, pc: (0, 0, i, 0)) for h in hsums],
        out_specs=[pl.BlockSpec((1, 1) + rows(h), lambda i, pc: (0, pc[1], i, 0)) for h in hsums],
    )
    return pl.pallas_call(
        body, name="sum_shards", grid_spec=gs,
        out_shape=[jax.ShapeDtypeStruct((1, 2) + h.shape[2:], f32) for h in hsums],
        compiler_params=_cp(("parallel",), 48),
    )(pc, *_hbm(*hsums, *recvs))


def _sum8(packs):
    _, R, C = packs.shape
    tr = R // 8 if R % 64 == 0 else R

    def body(p_ref, out_ref):
        acc = p_ref[0]
        for d in range(1, 8):
            acc = acc + p_ref[d]
        out_ref[...] = acc

    return pl.pallas_call(
        body, name="sum8", grid=(R // tr,),
        in_specs=[pl.BlockSpec((8, tr, C), lambda i: (0, i, 0))],
        out_specs=pl.BlockSpec((tr, C), lambda i: (i, 0)),
        out_shape=jax.ShapeDtypeStruct((R, C), f32),
        compiler_params=_cp(("parallel",)),
    )(packs)


def _allgather8(x_shard):
    m_per, n = x_shard.shape

    def body(x_ref, out_ref, send_sems, recv_sems, local_sem):
        x, y, c = lax.axis_index("x"), lax.axis_index("y"), lax.axis_index("c")
        me, sibling = (x, y, c), (x, y, 1 - c)
        chips = [(1 - x, y), (x, 1 - y), (1 - x, 1 - y)]

        def rows(px, py, pc):
            return out_ref.at[pl.ds((4 * px + 2 * py + pc) * m_per, m_per), :]

        def copy(k, block, to, src=None):
            return pltpu.make_async_remote_copy(
                src_ref=rows(*block) if src is None else src, dst_ref=rows(*block),
                send_sem=send_sems.at[k], recv_sem=recv_sems.at[k], device_id=to, device_id_type=MESH)

        mine = pltpu.make_async_copy(x_ref, rows(*me), local_sem)
        mine.start()
        first = [copy(0, me, sibling, src=x_ref)]
        first += [copy(1 + j, me, (*chip, c), src=x_ref) for j, chip in enumerate(chips)]
        for cp in first:
            cp.start()
        passed = [copy(4 + j, (*chip, c), sibling) for j, chip in enumerate(chips)]
        for j, chip in enumerate(chips):
            copy(1 + j, (*chip, c), me).wait_recv()
            passed[j].start()
        copy(0, sibling, me).wait_recv()
        for j, chip in enumerate(chips):
            copy(4 + j, (*chip, 1 - c), me).wait_recv()
        for cp in first + passed:
            cp.wait_send()
        mine.wait()

    return pl.pallas_call(
        body, name="allgather8",
        out_shape=jax.ShapeDtypeStruct((8 * m_per, n), x_shard.dtype),
        in_specs=[pl.BlockSpec(memory_space=pltpu.VMEM)],
        out_specs=pl.BlockSpec(memory_space=pltpu.VMEM),
        scratch_shapes=[pltpu.SemaphoreType.DMA((7,)), pltpu.SemaphoreType.DMA((7,)), pltpu.SemaphoreType.DMA],
        compiler_params=_cp(None, 48),
    )(x_shard)


def _other_chips():
    x, y = lax.axis_index("x"), lax.axis_index("y")
    return [(1 - x, y), (x, 1 - y), (1 - x, 1 - y)]


_HBM = pl.BlockSpec(memory_space=pltpu.HBM)
_SEM = pl.BlockSpec(memory_space=pltpu.SEMAPHORE)
_EFFECT = pltpu.SideEffectType.DATAFLOW_SIDE_EFFECTING


def _gather_copies(srcs, lands, send_sems, recv_sems):
    x, y, c = lax.axis_index("x"), lax.axis_index("y"), lax.axis_index("c")
    return [pltpu.make_async_remote_copy(
        src_ref=srcs[a].at[:, c], dst_ref=lands[a].at[2 * x + y, :, c], send_sem=send_sems.at[3 * a + j],
        recv_sem=recv_sems.at[3 * a + j], device_id=(cx, cy, c), device_id_type=MESH)
        for a in range(len(srcs)) for j, (cx, cy) in enumerate(_other_chips())]


def _gather_start(chunks, after, name):
    sizes = [len(srcs) for srcs, _ in chunks]
    flat = [t for srcs, lands in chunks for t in list(srcs) + list(lands)]
    nflat = len(flat)
    nsem = 2 * len(chunks)

    def body(*refs):
        ins, sems, token = refs[:nflat], refs[nflat + 1:nflat + 1 + nsem], refs[-1]
        off = 0
        for k, n in enumerate(sizes):
            for cp in _gather_copies(ins[off:off + n], ins[off + n:off + 2 * n], sems[2 * k], sems[2 * k + 1]):
                cp.start()
            off += 2 * n
        token[...] = jnp.zeros_like(token)

    res = pl.pallas_call(
        body, name=name,
        out_shape=[pltpu.SemaphoreType.DMA((3 * n,)) for n in sizes for _ in range(2)]
        + [pltpu.HBM(t.shape, t.dtype) for t in flat] + [jax.ShapeDtypeStruct((8, 128), f32)],
        in_specs=[_HBM] * nflat + [pl.BlockSpec(memory_space=pl.ANY)],
        out_specs=[_SEM] * nsem + [_HBM] * nflat + [pl.BlockSpec(memory_space=pltpu.VMEM)],
        input_output_aliases={i: nsem + i for i in range(nflat)},
        compiler_params=pltpu.CompilerParams(has_side_effects=_EFFECT),
    )(*[pltpu.with_memory_space_constraint(t, pltpu.HBM) for t in flat], after)
    out, off = [], nsem
    for k, n in enumerate(sizes):
        out.append((res[2 * k], res[2 * k + 1], res[off:off + n], res[off + n:off + 2 * n]))
        off += 2 * n
    return out, res[-1]


def _gather_wait(send_sems, recv_sems, srcs, lands, after, name):
    n = len(srcs)

    def body(*refs):
        for cp in _gather_copies(refs[:n], refs[n:2 * n], refs[2 * n], refs[2 * n + 1]):
            cp.wait_send()
            cp.wait_recv()

    res = pl.pallas_call(
        body, name=name,
        out_shape=[pltpu.HBM(t.shape, t.dtype) for t in list(srcs) + list(lands)],
        in_specs=[_HBM] * (2 * n) + [_SEM, _SEM] + [pl.BlockSpec(memory_space=pl.ANY)] * len(after),
        out_specs=[_HBM] * (2 * n),
        input_output_aliases={i: i for i in range(2 * n)},
        compiler_params=pltpu.CompilerParams(has_side_effects=_EFFECT),
    )(*srcs, *lands, send_sems, recv_sems, *after)
    return res[n:]


def _gather_forward(lands):
    n = len(lands)

    def body(*refs):
        outs = refs[n:2 * n]
        send_sems, recv_sems = refs[2 * n:]
        x, y, c = lax.axis_index("x"), lax.axis_index("y"), lax.axis_index("c")
        sibling = (x, y, 1 - c)
        copies = []
        for a in range(n):
            for j, (cx, cy) in enumerate(_other_chips()):
                mine = outs[a].at[2 * cx + cy, :, c]
                cp = pltpu.make_async_remote_copy(src_ref=mine, dst_ref=mine, send_sem=send_sems.at[3 * a + j],
                                                  recv_sem=recv_sems.at[3 * a + j], device_id=sibling, device_id_type=MESH)
                cp.start()
                copies.append((cp, a, j, cx, cy))
        for cp, a, j, cx, cy in copies:
            cp.wait_send()
            theirs = outs[a].at[2 * cx + cy, :, 1 - c]
            pltpu.make_async_remote_copy(src_ref=theirs, dst_ref=theirs, send_sem=send_sems.at[3 * a + j],
                                         recv_sem=recv_sems.at[3 * a + j], device_id=sibling, device_id_type=MESH).wait_recv()

    hbm = pl.BlockSpec(memory_space=pl.ANY)
    return pl.pallas_call(
        body, name="gather_forward",
        out_shape=[jax.ShapeDtypeStruct(t.shape, t.dtype) for t in lands],
        in_specs=[hbm] * n, out_specs=[hbm] * n,
        input_output_aliases={a: a for a in range(n)},
        scratch_shapes=[pltpu.SemaphoreType.DMA((3 * n,)), pltpu.SemaphoreType.DMA((3 * n,))],
    )(*lands)


def _pair_exchange(g5s):
    n = len(g5s)

    def body(*refs):
        ins, outs = refs[:n], refs[n:2 * n]
        send_sems, recv_sems = refs[2 * n:]
        c = lax.axis_index("c")
        sibling = (lax.axis_index("x"), lax.axis_index("y"), 1 - c)
        copies = []
        for a in range(n):
            cp = pltpu.make_async_remote_copy(src_ref=ins[a].at[:, :, 1 - c], dst_ref=outs[a], send_sem=send_sems.at[a],
                                              recv_sem=recv_sems.at[a], device_id=sibling, device_id_type=MESH)
            cp.start()
            copies.append(cp)
        for cp in copies:
            cp.wait()

    hbm = pl.BlockSpec(memory_space=pl.ANY)
    return pl.pallas_call(
        body, name="pair_exchange",
        out_shape=[jax.ShapeDtypeStruct(g.shape[:2] + g.shape[3:], g.dtype) for g in g5s],
        in_specs=[hbm] * n, out_specs=[hbm] * n,
        scratch_shapes=[pltpu.SemaphoreType.DMA((n,)), pltpu.SemaphoreType.DMA((n,))],
    )(*g5s)


def _scatter_copies(srcs, lands, send_sems, recv_sems):
    c = lax.axis_index("c")
    return [pltpu.make_async_remote_copy(
        src_ref=srcs[a].at[2 * cx + cy], dst_ref=lands[a].at[j], send_sem=send_sems.at[3 * a + j],
        recv_sem=recv_sems.at[3 * a + j], device_id=(cx, cy, c), device_id_type=MESH)
        for a in range(len(srcs)) for j, (cx, cy) in enumerate(_other_chips())]


def _scatter_start(hsums, name, after=()):
    n = len(hsums)
    na = len(after)

    def body(*refs):
        srcs, lands = refs[:n], refs[n:2 * n]
        send_sems, recv_sems = refs[2 * n + na], refs[2 * n + na + 1]
        for cp in _scatter_copies(srcs, lands, send_sems, recv_sems):
            cp.start()
        refs[-1][...] = jnp.zeros_like(refs[-1])

    lands = [lax.empty((3,) + g.shape[1:], g.dtype) for g in hsums]
    res = pl.pallas_call(
        body, name=name,
        out_shape=[pltpu.SemaphoreType.DMA((3 * n,)), pltpu.SemaphoreType.DMA((3 * n,))]
        + [pltpu.HBM(g.shape, g.dtype) for g in hsums] + [pltpu.HBM(g.shape, g.dtype) for g in lands]
        + [jax.ShapeDtypeStruct((8, 128), f32)],
        in_specs=[_HBM] * (2 * n) + [pl.BlockSpec(memory_space=pl.ANY)] * na,
        out_specs=[_SEM, _SEM] + [_HBM] * (2 * n) + [pl.BlockSpec(memory_space=pltpu.VMEM)],
        input_output_aliases={i: i + 2 for i in range(2 * n)},
        compiler_params=pltpu.CompilerParams(has_side_effects=_EFFECT),
    )(*[pltpu.with_memory_space_constraint(t, pltpu.HBM) for t in list(hsums) + lands], *after)
    return (res[0], res[1], res[2:2 + n], res[2 + n:2 + 2 * n]), res[-1]


def _scatter_wait(send_sems, recv_sems, srcs, lands, after, name):
    n = len(srcs)
    extra = list(after)

    def body(*refs):
        s_refs, l_refs = refs[:n], refs[n:2 * n]
        ss, rs = refs[2 * n], refs[2 * n + 1]
        for cp in _scatter_copies(s_refs, l_refs, ss, rs):
            cp.wait_send()
            cp.wait_recv()

    res = pl.pallas_call(
        body, name=name,
        out_shape=[pltpu.HBM(g.shape, g.dtype) for g in srcs] + [pltpu.HBM(g.shape, g.dtype) for g in lands],
        in_specs=[_HBM] * (2 * n) + [_SEM, _SEM] + [pl.BlockSpec(memory_space=pl.ANY)] * len(extra),
        out_specs=[_HBM] * (2 * n),
        input_output_aliases={i: i for i in range(2 * n)},
        compiler_params=pltpu.CompilerParams(has_side_effects=_EFFECT),
    )(*srcs, *lands, send_sems, recv_sems, *extra)
    return res[:n], res[n:]


def _plane_copies(src, land, send_sems, recv_sems):
    x, y, c = lax.axis_index("x"), lax.axis_index("y"), lax.axis_index("c")
    return [pltpu.make_async_remote_copy(src_ref=src, dst_ref=land.at[2 * x + y, c], send_sem=send_sems.at[j],
                                         recv_sem=recv_sems.at[j], device_id=(cx, cy, c), device_id_type=MESH)
            for j, (cx, cy) in enumerate(_other_chips())]


def _plane_start(pack, land, name):
    def body(src, lnd, send_sems, recv_sems, _s, _l, token):
        for cp in _plane_copies(src, lnd, send_sems, recv_sems):
            cp.start()
        token[...] = jnp.zeros_like(token)

    res = pl.pallas_call(
        body, name=name,
        out_shape=[pltpu.SemaphoreType.DMA((3,)), pltpu.SemaphoreType.DMA((3,)), pltpu.HBM(pack.shape, pack.dtype),
                   pltpu.HBM(land.shape, land.dtype), jax.ShapeDtypeStruct((8, 128), f32)],
        in_specs=[_HBM, _HBM], out_specs=[_SEM, _SEM, _HBM, _HBM, pl.BlockSpec(memory_space=pltpu.VMEM)],
        input_output_aliases={0: 2, 1: 3},
        compiler_params=pltpu.CompilerParams(has_side_effects=_EFFECT),
    )(pltpu.with_memory_space_constraint(pack, pltpu.HBM), pltpu.with_memory_space_constraint(land, pltpu.HBM))
    return res[:4], res[4]


def _plane_wait(send_sems, recv_sems, pack, land, after, name):
    def body(src, lnd, ss, rs, *_):
        for cp in _plane_copies(src, lnd, ss, rs):
            cp.wait_send()
            cp.wait_recv()

    return pl.pallas_call(
        body, name=name,
        out_shape=[pltpu.HBM(pack.shape, pack.dtype), pltpu.HBM(land.shape, land.dtype)],
        in_specs=[_HBM, _HBM, _SEM, _SEM] + [pl.BlockSpec(memory_space=pl.ANY)] * len(after),
        out_specs=[_HBM, _HBM], input_output_aliases={0: 0, 1: 1},
        compiler_params=pltpu.CompilerParams(has_side_effects=_EFFECT),
    )(pack, land, send_sems, recv_sems, *after)[1]


def _swap_halves(fulls):
    n = len(fulls)

    def body(*refs):
        ins, outs = refs[:n], refs[n:2 * n]
        send_sems, recv_sems = refs[2 * n:]
        c = lax.axis_index("c")
        sibling = (lax.axis_index("x"), lax.axis_index("y"), 1 - c)
        copies = []
        for a in range(n):
            cp = pltpu.make_async_remote_copy(src_ref=outs[a].at[:, c], dst_ref=outs[a].at[:, c], send_sem=send_sems.at[a],
                                              recv_sem=recv_sems.at[a], device_id=sibling, device_id_type=MESH)
            cp.start()
            copies.append(cp)
        for a, cp in enumerate(copies):
            cp.wait_send()
            theirs = outs[a].at[:, 1 - c]
            pltpu.make_async_remote_copy(src_ref=theirs, dst_ref=theirs, send_sem=send_sems.at[a], recv_sem=recv_sems.at[a],
                                         device_id=sibling, device_id_type=MESH).wait_recv()

    hbm = pl.BlockSpec(memory_space=pl.ANY)
    return pl.pallas_call(
        body, name="swap_halves",
        out_shape=[jax.ShapeDtypeStruct(p.shape, p.dtype) for p in fulls],
        in_specs=[hbm] * n, out_specs=[hbm] * n,
        input_output_aliases={a: a for a in range(n)},
        scratch_shapes=[pltpu.SemaphoreType.DMA((n,)), pltpu.SemaphoreType.DMA((n,))],
    )(*fulls)


def _to_segments(t):
    s, c = t.shape
    return t.reshape(SCAN_SEG, s // SCAN_SEG, c).transpose(1, 0, 2).reshape(s, c)


def _from_segments(t):
    s, c = t.shape
    return t.reshape(s // SCAN_SEG, SCAN_SEG, c).transpose(1, 0, 2).reshape(s, c)


def _ssm_operators(a_re, a_im, log_dt, b_re, b_im, c_re, c_im):
    lam = lax.complex(a_re, a_im)
    dt = jnp.exp(log_dt)[:, None]
    a_bar = jnp.exp(lam * dt)
    b_bar = ((a_bar - 1.0) / lam)[:, :, None] * lax.complex(b_re, b_im)
    eye = jnp.eye(N_GROUPS, dtype=f32)

    def embed_b(t):
        return (jnp.transpose(t, (0, 2, 1))[:, :, None, :] * eye[:, None, :, None]).reshape(D_SSM, D_STATE)

    def embed_c(t):
        return (jnp.transpose(t, (0, 2, 1))[:, :, None, :] * eye[:, None, :, None]).reshape(D_STATE, D_SSM)

    a2 = jnp.stack([a_bar.real.reshape(D_STATE), a_bar.imag.reshape(D_STATE)])
    return a2, embed_b(b_bar.real), embed_b(b_bar.imag), embed_c(c_re), embed_c(c_im)


def _local_step(x, target, mod, small, ffn_weights, mix_weights, grads_done):
    table = jnp.asarray(_bucket_table())
    bias = _bias_fwd(small["rel_bias"], table)
    L = DEPTH
    saved = []
    ssm_ops = []
    for l in range(L):
        sv = {}
        m9 = mod[l]
        sv["x0"] = x
        sv["w0"] = ffn_weights(l, 0, x)
        x, sv["f0"], sv["g0"], sv["u0"], sv["h0"] = _ffn_fwd(x, m9[0:3], *sv["w0"], 0, small["ln_g"][l, 0:1], small["ln_b"][l, 0:1])
        sv["x1"] = x
        sv["w1"] = mix_weights(l, x)
        *qkv, z_rest, sv["h1"] = _mix_in_fwd(x, m9[3:6], sv["w1"][0], 0)
        S = x.shape[0]
        qkv = [t.reshape(3, S, D_ATT) for t in qkv]
        att = [_att_fwd(qkv[b], bias, b) for b in range(3)]
        y_att, lse3 = _att_merge([att[b][0].reshape(d, S // d, D_ATT) for b, d in enumerate(DILATIONS)],
                                 [att[b][1].reshape(d, S // d, _LANES) for b, d in enumerate(DILATIONS)])
        sv.update(qkv=qkv, lse=[a[1] for a in att], lse3=lse3, y_att=y_att)

        prm = tuple(small[k][l] for k in ("ssm_a_re", "ssm_a_im", "ssm_log_dt", "ssm_b_re", "ssm_b_im", "ssm_c_re", "ssm_c_im"))
        (a2, bre, bim, cre, cim), ops_vjp = jax.vjp(_ssm_operators, *prm)
        ssm_ops.append(ops_vjp)
        u_ssm = _to_segments(z_rest[:, :D_SSM])
        sr, si = _ssm_states(u_ssm, bre, bim, a2)
        dskip = small["ssm_d"][l][None, :]
        glu_b = small["glu_b"][l][None, :]
        out_seg, y_seg = _ssm_out(sr, si, u_ssm, cre, cim, dskip, small["glu_w"][l], glu_b)
        y_ssm = _from_segments(out_seg)
        sv.update(a2=a2, bre=bre, bim=bim, cre=cre, cim=cim, u_ssm=u_ssm, sr=sr, si=si, y_seg=y_seg, y_ssm=y_ssm)

        u_pool = jnp.concatenate([jnp.zeros((POOL_HALO, D_POOL), f32), z_rest[:, D_SSM:]])
        y_pool = _pool_fwd(u_pool, small["pool_w"][l], small["pool_scale"][l][None, :])
        sv.update(u_pool=u_pool, y_pool=y_pool)

        x, sv["ymix"] = _mix_out_fwd(x, y_att, y_ssm, y_pool, m9[3:6], sv["w1"][1], 0, small["ln_g"][l, 1:2], small["ln_b"][l, 1:2])
        sv["x2"] = x
        sv["w2"] = ffn_weights(l, 1, x)
        x, sv["f2"], sv["g2"], sv["u2"], sv["h2"] = _ffn_fwd(x, m9[6:9], *sv["w2"], 0, small["ln_g"][l, 2:3], small["ln_b"][l, 2:3])
        saved.append(sv)

    loss, dx = _loss_head(x, target)

    dmod = [None] * L
    dln_g = [None] * L
    dln_b = [None] * L
    sg = {k: [None] * L for k in ("ssm_a_re", "ssm_a_im", "ssm_log_dt", "ssm_b_re", "ssm_b_im", "ssm_c_re", "ssm_c_im",
                                  "ssm_d", "glu_w", "glu_b", "pool_w", "pool_scale")}
    dbias_tot = None
    order_after = jnp.zeros((), f32)
    for l in reversed(range(L)):
        sv = saved[l]
        m9 = mod[l] + order_after

        def fresh(like):
            return [lax.empty(t.shape, bf16) for t in like]

        dx, dg, du, a, df, dm2, dlg2, dlb2 = _ffn_bwd(dx, sv["x2"], sv["f2"], sv["g2"], sv["u2"], m9[6:9], *sv["w2"], 0,
                                                     small["ln_g"][l, 2:3])
        g_ffn1 = _ffn_wgrad(sv["h2"], dg, du, a, df, *fresh(sv["w2"]), 0)
        dxr, d_att, d_ssm, d_pool, dgate1, dlg1, dlb1, g_w_out = _mix_out_bwd(
            dx, sv["x1"], sv["ymix"], sv["y_att"], sv["y_ssm"], sv["y_pool"], m9[3:6], sv["w1"][1], 0, small["ln_g"][l, 1:2],
            fresh(sv["w1"])[1])
        S = d_att.shape[0]
        merged = _att_merge_bwd(d_att, sv["y_att"], sv["lse3"])
        dqkv, dbias = [], []
        for b, d in enumerate(DILATIONS):
            dq_b, db_b = _att_bwd(sv["qkv"][b], merged[b].reshape(S, D_ATT), sv["lse"][b], merged[3 + b].reshape(S, _LANES), bias, b)
            dqkv.append(dq_b.reshape(3, d, S // d, D_ATT))
            dbias.append(db_b)
        dbias = jnp.stack(dbias)
        dbias_tot = dbias if dbias_tot is None else dbias_tot + dbias
        d_seg = _to_segments(d_ssm)
        dskip = small["ssm_d"][l][None, :]
        glu_b = small["glu_b"][l][None, :]
        dy_seg, du_skip, dcre, dcim, dd, dglu_b, dglu_w = _ssm_out_bwd(
            d_seg, sv["y_seg"], sv["u_ssm"], sv["sr"], sv["si"], dskip, small["glu_w"][l], glu_b)
        du_seg, dbre, dbim, da2 = _ssm_states_bwd(dy_seg, du_skip, sv["u_ssm"], sv["sr"], sv["si"], sv["cre"], sv["cim"],
                                                  sv["bre"], sv["bim"], sv["a2"])
        d_prm = ssm_ops[l]((da2, dbre, dbim, dcre, dcim))
        for k, v in zip(("ssm_a_re", "ssm_a_im", "ssm_log_dt", "ssm_b_re", "ssm_b_im", "ssm_c_re", "ssm_c_im"), d_prm):
            sg[k][l] = v
        sg["ssm_d"][l] = dd[0]
        sg["glu_b"][l] = dglu_b[0]
        sg["glu_w"][l] = dglu_w
        du_ssm = _from_segments(du_seg)
        dyp = jnp.concatenate([d_pool, jnp.zeros((POOL_HALO, D_POOL), f32)])
        du_pool, dpw, dps = _pool_bwd(dyp, sv["u_pool"], small["pool_w"][l], small["pool_scale"][l][None, :])
        sg["pool_w"][l] = dpw
        sg["pool_scale"][l] = dps[0]
        d_rest = jnp.concatenate([du_ssm, du_pool], axis=1).astype(bf16)
        dx, dm1, dz = _mix_in_bwd(dqkv, d_rest, dxr, sv["x1"], m9[3:6], sv["w1"][0], 0)
        g_w_in = _mix_in_wgrad(sv["h1"], dz, fresh(sv["w1"])[0], 0)
        ffn_names = ("ffn_w_gate", "ffn_w_up", "ffn_w_down")
        m9 = m9 + grads_done(l, 1, list(zip(ffn_names, [(2 * l + 1) * FF_SHARD] * 3, g_ffn1))
                             + [("w_in", l * D_MODEL, g_w_in), ("w_out", l * 256, g_w_out)])
        dm1 = jnp.concatenate([dm1[0:2], dgate1])
        dx, dg, du, a, df, dm0, dlg0, dlb0 = _ffn_bwd(dx, sv["x0"], sv["f0"], sv["g0"], sv["u0"], m9[0:3], *sv["w0"], 0,
                                                     small["ln_g"][l, 0:1])
        g_ffn0 = _ffn_wgrad(sv["h0"], dg, du, a, df, *fresh(sv["w0"]), 0)
        order_after = grads_done(l, 0, list(zip(ffn_names, [2 * l * FF_SHARD] * 3, g_ffn0)))
        dmod[l] = jnp.concatenate([dm0, dm1, dm2])
        dln_g[l] = jnp.concatenate([dlg0, dlg1, dlg2])
        dln_b[l] = jnp.concatenate([dlb0, dlb1, dlb2])

    small_grads = {k: jnp.stack(v) for k, v in sg.items()}
    small_grads["rel_bias"] = _bias_bwd(dbias_tot, table)
    small_grads["ln_g"] = jnp.stack(dln_g)
    small_grads["ln_b"] = jnp.stack(dln_b)
    return loss, dx, jnp.stack(dmod), small_grads


_TILE_ELEMS = 8 * 128


def _pack_rows(shapes):
    out, row = [], 0
    for s in shapes:
        nr = -(-int(np.prod(s)) // _TILE_ELEMS) * 8
        out.append((row, nr))
        row += nr
    return out


def _pack(arrs):
    parts = []
    for a in arrs:
        flat = a.reshape(-1).astype(f32)
        npad = -(-flat.shape[0] // _TILE_ELEMS) * _TILE_ELEMS
        parts.append(jnp.pad(flat, (0, npad - flat.shape[0])).reshape(npad // 128, 128))
    return jnp.concatenate(parts, axis=0)


def _unpack(buf, shapes):
    return [buf[row:row + nr].reshape(-1)[:int(np.prod(s))].reshape(s) for s, (row, nr) in zip(shapes, _pack_rows(shapes))]


_REPL = ("rel_bias", "ada_b", "ssm_a_re", "ssm_a_im", "ssm_log_dt", "ssm_b_re", "ssm_b_im", "ssm_c_re", "ssm_c_im",
         "ssm_d", "glu_b", "pool_w", "pool_scale")
_SMALL_SHARDED = ("ln_g", "ln_b", "glu_w")
_BIG = ("ffn_w_gate", "ffn_w_up", "ffn_w_down", "w_in", "w_out")
_ORDER = ("rel_bias", "ada_w", "ada_b", "ln_g", "ln_b", "ffn_w_gate", "ffn_w_up", "ffn_w_down", "w_in", "w_out",
          "ssm_a_re", "ssm_a_im", "ssm_log_dt", "ssm_b_re", "ssm_b_im", "ssm_c_re", "ssm_c_im", "ssm_d", "glu_w",
          "glu_b", "pool_w", "pool_scale")


def kernel(x, c, rel_bias, ada_w, ada_b, ln_g, ln_b, ffn_w_gate, ffn_w_up, ffn_w_down, w_in, w_out, ssm_a_re, ssm_a_im, ssm_log_dt, ssm_b_re, ssm_b_im, ssm_c_re, ssm_c_im, ssm_d, glu_w, glu_b, pool_w, pool_scale, loss_target, m_rel_bias, m_ada_w, m_ada_b, m_ln_g, m_ln_b, m_ffn_w_gate, m_ffn_w_up, m_ffn_w_down, m_w_in, m_w_out, m_ssm_a_re, m_ssm_a_im, m_ssm_log_dt, m_ssm_b_re, m_ssm_b_im, m_ssm_c_re, m_ssm_c_im, m_ssm_d, m_glu_w, m_glu_b, m_pool_w, m_pool_scale, v_rel_bias, v_ada_w, v_ada_b, v_ln_g, v_ln_b, v_ffn_w_gate, v_ffn_w_up, v_ffn_w_down, v_w_in, v_w_out, v_ssm_a_re, v_ssm_a_im, v_ssm_log_dt, v_ssm_b_re, v_ssm_b_im, v_ssm_c_re, v_ssm_c_im, v_ssm_d, v_glu_w, v_glu_b, v_pool_w, v_pool_scale):
    args = dict(locals())
    w = {k: args[k] for k in _ORDER}
    m = {k: args["m_" + k] for k in _ORDER}
    v = {k: args["v_" + k] for k in _ORDER}
    L, D = DEPTH, D_MODEL
    ax, ay, ac = lax.axis_index("x"), lax.axis_index("y"), lax.axis_index("c")
    p_me = 2 * ax + ay
    dev = 4 * ax + 2 * ay + ac

    transposed = ("ffn_w_gate", "ffn_w_up")
    for d in (w, m, v):
        for name in transposed:
            d[name] = jnp.swapaxes(d[name], 2, 3)

    def halves(t):
        return t.astype(bf16).reshape(1, 2, t.shape[0] // 2, t.shape[1])

    def landing(src):
        return lax.dynamic_update_slice(lax.empty((N_CHIPS,) + src.shape, bf16), src[None], (p_me, 0, 0, 0, 0))

    chunk_keys = [("ffn", 0, 0), ("mix", 0), ("ffn", 0, 1), ("ffn", 1, 0), ("mix", 1), ("ffn", 1, 1)]
    chunk_srcs = []
    for key in chunk_keys:
        if key[0] == "ffn":
            chunk_srcs.append([halves(w[name][key[1], key[2]]) for name in ("ffn_w_gate", "ffn_w_up", "ffn_w_down")])
        else:
            chunk_srcs.append([halves(w_in[key[1]]), halves(w_out[key[1]])])

    pack = _pack([c, ln_g, ln_b, glu_w])
    rows = pack.shape[0]
    allp = _allgather8(pack).reshape(8, rows, 128)
    chunks = [(srcs, [landing(t) for t in srcs]) for srcs in chunk_srcs]
    first_in_flight, first_begun = _gather_start(chunks[:1], allp, "gather_start_first")
    c_all = allp[:, :8].reshape(8, D) + first_begun[0, 0]
    by_chip = allp[0::2]

    fwd_rows = _pack_rows([c.shape, ln_g.shape, ln_b.shape, glu_w.shape])

    def sharded(part, shape, axis):
        row0, nrows = fwd_rows[part]
        t = by_chip[:, row0:row0 + nrows].reshape(N_CHIPS, -1)[:, :int(np.prod(shape))].reshape((N_CHIPS,) + shape)
        return jnp.concatenate([t[p] for p in range(N_CHIPS)], axis=axis)

    ln_g_full = sharded(1, ln_g.shape, 2)
    ln_b_full = sharded(2, ln_b.shape, 2)
    glu_w_full = sharded(3, glu_w.shape, 1)

    ncol = ada_w.shape[-1]
    ada_b_cols = lax.dynamic_slice_in_dim(ada_b, p_me * ncol, ncol, axis=1)[:, None, :]
    mod_part = _ada_fwd(c_all, ada_w, ada_b_cols)
    mrows = L * 8 * ncol // 128
    mod_all = _allgather8(mod_part.reshape(mrows, 128)).reshape(8, L, 8, ncol)
    mod_mine = lax.dynamic_index_in_dim(mod_all, dev, axis=2, keepdims=False)
    mod = jnp.concatenate([mod_mine[2 * p] for p in range(N_CHIPS)], axis=-1).reshape(L, 9, D)

    rest_in_flight, rest_begun = _gather_start(chunks[1:], mod, "gather_start_rest")
    in_flight = first_in_flight + rest_in_flight

    def gathered(key, after):
        k = chunk_keys.index(key)
        lands = _gather_forward(_gather_wait(*in_flight[k], [after, rest_begun], "gather_wait_%d" % k))
        return [t.reshape(N_CHIPS, 1, 2 * t.shape[3], t.shape[4]) for t in lands]

    pc = jnp.stack([p_me, ac]).astype(jnp.int32)
    groups = {}
    scattering = {}

    def start_group(tag, after=()):
        g5 = [g.reshape(g.shape[:2] + (2, g.shape[2] // 2, g.shape[3])) for _, _, g in groups[tag]]
        hsum = _pair_sum(g5, _pair_exchange(g5), pc)
        scattering[tag], begun = _scatter_start(hsum, "scatter_start_%s" % tag, after)
        return begun

    def grads_done(l, s, grads):
        if l == 1:
            groups.setdefault("l1", []).extend(grads)
            return start_group("l1")[0, 0] if s == 0 else jnp.zeros((), f32)
        groups["l0a" if s == 1 else "l0b"] = grads
        return start_group("l0a")[0, 0] if s == 1 else jnp.zeros((), f32)

    small = {k: w[k] for k in _REPL if k != "ada_b"}
    small.update(ln_g=ln_g_full, ln_b=ln_b_full, glu_w=glu_w_full)
    loss_dev, grad_x, dmod, sgrads = _local_step(
        x[0], loss_target[0], mod, small, lambda l, s, after: gathered(("ffn", l, s), after),
        lambda l, after: gathered(("mix", l), after), grads_done)
    loss = lax.psum(loss_dev[0, 0], ("x", "y", "c"))

    names = ("rel_bias", "ln_g", "ln_b", "ssm_a_re", "ssm_a_im", "ssm_log_dt", "ssm_b_re", "ssm_b_im", "ssm_c_re",
             "ssm_c_im", "ssm_d", "glu_w", "glu_b", "pool_w", "pool_scale")
    gpack = _pack([dmod] + [sgrads[k] for k in names])
    grows = gpack.shape[0]
    land = lax.dynamic_update_slice(lax.empty((N_CHIPS, 2, grows, 128), f32), gpack[None, None], (p_me, ac, 0, 0))
    small_in_flight, small_begun = _plane_start(gpack, land, "small_grads_start")
    l0b_begun = start_group("l0b", (small_begun,))

    out_g, out_d, out_m, out_v = {}, {}, {}, {}
    row_tile = dict(zip(_BIG, (352, 352, 352, 256, 256)))
    big = {name: None for name in _BIG}

    def finish_group(tag, after):
        hsum, recv = _scatter_wait(*scattering[tag], after, "scatter_wait_%s" % tag)
        full = _swap_halves(_sum_shards(hsum, recv, pc))
        for (name, row0, _), g in zip(groups[tag], full):
            shp = w[name].shape
            r2 = (int(np.prod(shp[:-1])), shp[-1])
            big[name] = _adamw(w[name].reshape(r2), m[name].reshape(r2), v[name].reshape(r2), g.reshape(-1, shp[-1]),
                               row_tile[name], row0, big[name])
        return [big[name][1] for name, _, _ in groups[tag]]

    after = finish_group("l1", [grad_x, l0b_begun])

    land = _plane_wait(*small_in_flight, after, "small_grads_wait")
    gall = _swap_halves([land])[0].reshape(8, grows, 128)
    gsum = _unpack(_sum8(gall), [(L, 9 * D)] + [sgrads[k].shape for k in names])
    red = dict(zip(("ada_b",) + names, gsum))
    red["ln_g"] = lax.dynamic_slice_in_dim(red["ln_g"], p_me * 256, 256, axis=2)
    red["ln_b"] = lax.dynamic_slice_in_dim(red["ln_b"], p_me * 256, 256, axis=2)
    red["glu_w"] = lax.dynamic_slice_in_dim(red["glu_w"], p_me * 64, 64, axis=1)

    dmod_all = gall[:, :L * 9 * D // 128].reshape(8, L, 9 * D)
    dmod_cols = jnp.transpose(lax.dynamic_slice_in_dim(dmod_all, p_me * ncol, ncol, axis=2), (1, 0, 2))
    g_ada_w = _ada_wgrad(jnp.transpose(c_all), dmod_cols)

    r2 = (L * D, ncol)
    res = _adamw(ada_w.reshape(r2), m["ada_w"].reshape(r2), v["ada_w"].reshape(r2), g_ada_w.reshape(r2), 128)
    out_g["ada_w"], out_d["ada_w"], out_m["ada_w"], out_v["ada_w"] = [t.reshape(ada_w.shape) for t in res]

    small_names = _REPL + _SMALL_SHARDED
    wp = _pack([w[k] for k in small_names])
    res_small = _adamw(wp, _pack([m[k] for k in small_names]), _pack([v[k] for k in small_names]),
                       _pack([red[k] for k in small_names]), wp.shape[0])
    for t, dst in zip(res_small, (out_g, out_d, out_m, out_v)):
        for k, a in zip(small_names, _unpack(t, [w[k].shape for k in small_names])):
            dst[k] = a

    finish_group("l0b", finish_group("l0a", [res_small[1], res[1]]))
    for name in _BIG:
        res = [t.reshape(w[name].shape) for t in big[name]]
        out_g[name], out_d[name], out_m[name], out_v[name] = [jnp.swapaxes(t, 2, 3) for t in res] if name in transposed else res

    return (loss, grad_x[None], *[out_g[k] for k in _ORDER], *[out_d[k] for k in _ORDER],
            *[out_m[k] for k in _ORDER], *[out_v[k] for k in _ORDER])
```

```python
import functools
import math

import numpy as np
import jax
import jax.numpy as jnp
from jax import lax
from jax.experimental import pallas as pl
from jax.experimental.pallas import tpu as pltpu

f32 = jnp.float32
bf16 = jnp.bfloat16
MESH = pl.DeviceIdType.MESH

D_MODEL = 1024
SEQ = 2048
DEPTH = 2
HEAD_DIM = 64
N_HEADS = 8
D_ATT = 512
DILATIONS = (1, 4, 16)
BLOCKS_PER_RESIDUE = (16, 4, 1)
ATT_BLOCK = 128
N_UNITS = SEQ // ATT_BLOCK
N_GROUPS = 16
SSM_GROUP = 16
SSM_STATE = 64
D_SSM = 256
D_STATE = N_GROUPS * SSM_STATE
POOL_WINDOWS = (2, 4, 8, 16)
POOL_GROUP = 64
D_POOL = 256
POOL_HALO = 16
D_FF = 2816
N_BUCKETS = 32
MAX_DISTANCE = 2048
ALPHA = (2 * DEPTH) ** 0.25
FFN_RES = 0.5
LN_EPS = 1e-5
NEG = -1e30
N_CHIPS = 4
FF_SHARD = D_FF // N_CHIPS
SCAN_SEG = 8
SCAN_STEPS = SEQ // SCAN_SEG

ADAM_LR, ADAM_B1, ADAM_B2, ADAM_EPS, ADAM_WD, ADAM_STEP = 0.001, 0.9, 0.999, 1e-08, 0.01, 10

TOK_TILE = 512


def _cp(dims=None, vmem_mb=None):
    kw = {}
    if dims is not None:
        kw["dimension_semantics"] = dims
    if vmem_mb is not None:
        kw["vmem_limit_bytes"] = vmem_mb << 20
    return pltpu.CompilerParams(**kw)


def _dot(a, b):
    return jnp.dot(a, b, preferred_element_type=f32)


def _dot_nt(a, b):
    return lax.dot_general(a, b, (((1,), (1,)), ((), ())), preferred_element_type=f32)


def _dot_tn(a, b):
    return lax.dot_general(a, b, (((0,), (0,)), ((), ())), preferred_element_type=f32)


def _ln_stats(v):
    mu = jnp.mean(v, -1, keepdims=True)
    d = v - mu
    var = jnp.mean(d * d, -1, keepdims=True)
    rstd = lax.rsqrt(var + LN_EPS)
    return d * rstd, rstd


def _ln_bwd(dxh, xh, rstd):
    return rstd * (dxh - jnp.mean(dxh, -1, keepdims=True) - xh * jnp.mean(dxh * xh, -1, keepdims=True))


_GELU_C = math.sqrt(2.0 / math.pi)


def _gelu(y):
    return 0.5 * y * (1.0 + jnp.tanh(_GELU_C * (y + 0.044715 * y * y * y)))


def _gelu_grad(y):
    t = jnp.tanh(_GELU_C * (y + 0.044715 * y * y * y))
    return 0.5 * (1.0 + t) + 0.5 * y * (1.0 - t * t) * (_GELU_C * (1.0 + 3 * 0.044715 * y * y))


def _full(shape):
    return pl.BlockSpec(shape, lambda *_: (0,) * len(shape))


def _hbm(*args):
    return [pltpu.with_memory_space_constraint(a, pltpu.HBM) if getattr(a, "ndim", 0) >= 2 else a for a in args]


def _ffn_fwd(x, mod3, wg, wu, wd, ls, lng, lnb):
    S, D = x.shape
    Fs = wg.shape[-2]
    ts = TOK_TILE

    def body(x_ref, mod_ref, wg_ref, wu_ref, wd_ref, lng_ref, lnb_ref, xo_ref, f_ref, g_ref, u_ref, h_ref, acc_sc):
        j = pl.program_id(1)

        @pl.when(j == 0)
        def _():
            xh, _ = _ln_stats(x_ref[...])
            h_ref[...] = (xh * (1.0 + mod_ref[1:2, :]) + mod_ref[0:1, :]).astype(bf16)
            acc_sc[...] = jnp.zeros_like(acc_sc)

        h = h_ref[...]
        g = _dot_nt(h, wg_ref[0, 0])
        u = _dot_nt(h, wu_ref[0, 0])
        g_ref[0] = g.astype(bf16)
        u_ref[0] = u.astype(bf16)
        a = (g * jax.nn.sigmoid(g) * u).astype(bf16)
        acc_sc[...] += _dot(a, wd_ref[0, 0])

        @pl.when(j == N_CHIPS - 1)
        def _():
            f = acc_sc[...]
            f_ref[...] = f
            r = ALPHA * x_ref[...] + (FFN_RES * mod_ref[2:3, :]) * f
            rh, _ = _ln_stats(r)
            xo_ref[...] = rh * lng_ref[...] + lnb_ref[...]

    tok = pl.BlockSpec((ts, D), lambda i, j: (i, 0))
    wrow = pl.BlockSpec((1, 1, Fs, D), lambda i, j: (j, ls, 0, 0))
    hid = pl.BlockSpec((1, ts, Fs), lambda i, j: (j, i, 0))
    return pl.pallas_call(
        body, name="ffn_fwd", grid=(S // ts, N_CHIPS),
        in_specs=[tok, _full((3, D)), wrow, wrow, wrow, _full((1, D)), _full((1, D))],
        out_specs=[tok, tok, hid, hid, tok],
        out_shape=[jax.ShapeDtypeStruct((S, D), f32), jax.ShapeDtypeStruct((S, D), f32),
                   jax.ShapeDtypeStruct((N_CHIPS, S, Fs), bf16), jax.ShapeDtypeStruct((N_CHIPS, S, Fs), bf16),
                   jax.ShapeDtypeStruct((S, D), bf16)],
        scratch_shapes=[pltpu.VMEM((ts, D), f32)],
        compiler_params=_cp(("parallel", "arbitrary"), 56),
    )(*_hbm(x, mod3, wg, wu, wd, lng, lnb))


def _ffn_bwd(dxo, x, f, g, u, mod3, wg, wu, wd, ls, lng):
    S, D = x.shape
    Fs = wg.shape[-2]
    ts = TOK_TILE

    def body(dxo_ref, x_ref, f_ref, g_ref, u_ref, mod_ref, wg_ref, wu_ref, wd_ref, lng_ref,
             dx_ref, dg_ref, du_ref, a_ref, df_ref, dmod_ref, dlng_ref, dlnb_ref,
             dr_sc, df_sc, acc_sc):
        i = pl.program_id(0)
        j = pl.program_id(1)

        @pl.when((i == 0) & (j == 0))
        def _():
            dmod_ref[...] = jnp.zeros_like(dmod_ref)
            dlng_ref[...] = jnp.zeros_like(dlng_ref)
            dlnb_ref[...] = jnp.zeros_like(dlnb_ref)

        @pl.when(j == 0)
        def _():
            xv = x_ref[...]
            fv = f_ref[...]
            gate = mod_ref[2:3, :]
            rh, rstd = _ln_stats(ALPHA * xv + (FFN_RES * gate) * fv)
            dy = dxo_ref[...]
            dlng_ref[...] += jnp.sum(dy * rh, 0, keepdims=True)
            dlnb_ref[...] += jnp.sum(dy, 0, keepdims=True)
            dr = _ln_bwd(dy * lng_ref[...], rh, rstd)
            dr_sc[...] = dr
            dmod_ref[2:3, :] += jnp.sum(FFN_RES * dr * fv, 0, keepdims=True)
            df = ((FFN_RES * gate) * dr).astype(bf16)
            df_sc[...] = df
            df_ref[...] = df
            acc_sc[...] = jnp.zeros_like(acc_sc)

        da = _dot_nt(df_sc[...], wd_ref[0, 0])
        gv = g_ref[0].astype(f32)
        uv = u_ref[0].astype(f32)
        sg = jax.nn.sigmoid(gv)
        si = gv * sg
        a_ref[0] = (si * uv).astype(bf16)
        dgv = (da * uv * (sg * (1.0 + gv * (1.0 - sg)))).astype(bf16)
        duv = (da * si).astype(bf16)
        dg_ref[0] = dgv
        du_ref[0] = duv
        acc_sc[...] += _dot(dgv, wg_ref[0, 0]) + _dot(duv, wu_ref[0, 0])

        @pl.when(j == N_CHIPS - 1)
        def _():
            dh = acc_sc[...]
            xh, rstd0 = _ln_stats(x_ref[...])
            dmod_ref[0:1, :] += jnp.sum(dh, 0, keepdims=True)
            dmod_ref[1:2, :] += jnp.sum(dh * xh, 0, keepdims=True)
            dx_ref[...] = _ln_bwd(dh * (1.0 + mod_ref[1:2, :]), xh, rstd0) + ALPHA * dr_sc[...]

    tok = pl.BlockSpec((ts, D), lambda i, j: (i, 0))
    wrow = pl.BlockSpec((1, 1, Fs, D), lambda i, j: (j, ls, 0, 0))
    hid = pl.BlockSpec((1, ts, Fs), lambda i, j: (j, i, 0))
    hid_shape = jax.ShapeDtypeStruct((N_CHIPS, S, Fs), bf16)
    return pl.pallas_call(
        body, name="ffn_bwd", grid=(S // ts, N_CHIPS),
        in_specs=[tok, tok, tok, hid, hid, _full((3, D)), wrow, wrow, wrow, _full((1, D))],
        out_specs=[tok, hid, hid, hid, tok, _full((3, D)), _full((1, D)), _full((1, D))],
        out_shape=[jax.ShapeDtypeStruct((S, D), f32), hid_shape, hid_shape, hid_shape,
                   jax.ShapeDtypeStruct((S, D), bf16),
                   jax.ShapeDtypeStruct((3, D), f32), jax.ShapeDtypeStruct((1, D), f32), jax.ShapeDtypeStruct((1, D), f32)],
        scratch_shapes=[pltpu.VMEM((ts, D), f32), pltpu.VMEM((ts, D), bf16), pltpu.VMEM((ts, D), f32)],
        compiler_params=_cp(("arbitrary", "arbitrary"), 56),
    )(*_hbm(dxo, x, f, g, u, mod3, wg, wu, wd, lng))


def _ffn_wgrad(h, dg, du, a, df, gwg, gwu, gwd, ls):
    S, D = h.shape
    Fs = dg.shape[-1]
    tk = TOK_TILE
    nk = S // tk

    def body(h_ref, dg_ref, du_ref, a_ref, df_ref, _g0, _g1, _g2, gwg_ref, gwu_ref, gwd_ref, ag_sc, au_sc, ad_sc):
        k = pl.program_id(1)

        @pl.when(k == 0)
        def _():
            ag_sc[...] = jnp.zeros_like(ag_sc)
            au_sc[...] = jnp.zeros_like(au_sc)
            ad_sc[...] = jnp.zeros_like(ad_sc)

        hv = h_ref[...]
        ag_sc[...] += _dot_tn(dg_ref[0], hv)
        au_sc[...] += _dot_tn(du_ref[0], hv)
        ad_sc[...] += _dot_tn(a_ref[0], df_ref[...])

        @pl.when(k == nk - 1)
        def _():
            gwg_ref[0, 0] = ag_sc[...].astype(bf16)
            gwu_ref[0, 0] = au_sc[...].astype(bf16)
            gwd_ref[0, 0] = ad_sc[...].astype(bf16)

    tok = pl.BlockSpec((tk, D), lambda p, k: (k, 0))
    hid = pl.BlockSpec((1, tk, Fs), lambda p, k: (p, k, 0))
    anyspec = pl.BlockSpec(memory_space=pl.ANY)
    orow = pl.BlockSpec((1, 1, Fs, D), lambda p, k: (p, ls, 0, 0))
    return pl.pallas_call(
        body, name="ffn_wgrad", grid=(N_CHIPS, nk),
        in_specs=[tok, hid, hid, hid, tok, anyspec, anyspec, anyspec],
        out_specs=[orow, orow, orow],
        out_shape=[jax.ShapeDtypeStruct(gwg.shape, bf16), jax.ShapeDtypeStruct(gwu.shape, bf16),
                   jax.ShapeDtypeStruct(gwd.shape, bf16)],
        scratch_shapes=[pltpu.VMEM((Fs, D), f32), pltpu.VMEM((Fs, D), f32), pltpu.VMEM((Fs, D), f32)],
        input_output_aliases={5: 0, 6: 1, 7: 2},
        compiler_params=_cp(("parallel", "arbitrary"), 48),
    )(*_hbm(h, dg, du, a, df, gwg, gwu, gwd))


_LANES = 128
_QKV_BLOCKS = D_ATT // _LANES


def _res_spec(lead, d, width, index):
    return pl.BlockSpec((lead, d, TOK_TILE // d, width), index)


def _res_spec3(d, width):
    return pl.BlockSpec((d, TOK_TILE // d, width), lambda i: (0, i, 0))


def _rows_to_residues(tile_bufs, d, put):
    for r in range(d):
        for cb, buf in enumerate(tile_bufs):
            put(r, cb, buf[pl.ds(r, TOK_TILE // d, stride=d), :])


def _residues_to_rows(tile_bufs, d, get):
    for r in range(d):
        for cb, buf in enumerate(tile_bufs):
            buf[pl.ds(r, TOK_TILE // d, stride=d), :] = get(r, cb)


def _mix_in_fwd(x, mod3, w_in, l):
    S, D = x.shape
    N = w_in.shape[-1]
    ts = TOK_TILE

    def body(x_ref, mod_ref, w_ref, o1_ref, o4_ref, o16_ref, zr_ref, h_ref, *bufs):
        j = pl.program_id(1)

        @pl.when(j == 0)
        def _():
            xh, _ = _ln_stats(x_ref[...])
            h_ref[...] = (xh * (1.0 + mod_ref[1:2, :]) + mod_ref[0:1, :]).astype(bf16)

        z = _dot(h_ref[...], w_ref[0, 0])

        @pl.when(j == N_CHIPS - 1)
        def _():
            zr_ref[...] = z

        @pl.when(j < N_CHIPS - 1)
        def _():
            zz = z * jnp.where(j == 0, HEAD_DIM ** -0.5, 1.0)
            o1_ref[j, 0] = zz.astype(bf16)
            for cb, buf in enumerate(bufs):
                buf[...] = zz[:, _LANES * cb:_LANES * (cb + 1)]
            for d, o_ref in zip(DILATIONS[1:], (o4_ref, o16_ref)):
                def put(r, cb, piece, o_ref=o_ref):
                    o_ref[j, r, :, _LANES * cb:_LANES * (cb + 1)] = piece.astype(bf16)
                _rows_to_residues(bufs, d, put)

    tok = pl.BlockSpec((ts, D), lambda i, j: (i, 0))
    res = [_res_spec(3, d, N, lambda i, j: (0, 0, i, 0)) for d in DILATIONS]
    return pl.pallas_call(
        body, name="mix_in_fwd", grid=(S // ts, N_CHIPS),
        in_specs=[tok, _full((3, D)), pl.BlockSpec((1, 1, D, N), lambda i, j: (j, l, 0, 0))],
        out_specs=res + [pl.BlockSpec((ts, N), lambda i, j: (i, 0)), tok],
        out_shape=[jax.ShapeDtypeStruct((3, d, S // d, N), bf16) for d in DILATIONS]
        + [jax.ShapeDtypeStruct((S, N), f32), jax.ShapeDtypeStruct((S, D), bf16)],
        scratch_shapes=[pltpu.VMEM((ts, _LANES), f32)] * _QKV_BLOCKS,
        compiler_params=_cp(("parallel", "arbitrary"), 40),
    )(*_hbm(x, mod3, w_in))


def _mix_in_bwd(dqkv, d_rest, dx_res, x, mod3, w_in, l):
    S, D = x.shape
    N = w_in.shape[-1]
    ts = TOK_TILE

    def body(d1_ref, d4_ref, d16_ref, dr_ref, dxr_ref, x_ref, mod_ref, w_ref, dx_ref, dmod_ref, dz_ref, acc_sc, *bufs):
        i = pl.program_id(0)
        j = pl.program_id(1)

        @pl.when((i == 0) & (j == 0))
        def _():
            dmod_ref[...] = jnp.zeros_like(dmod_ref)

        @pl.when(j == 0)
        def _():
            acc_sc[...] = jnp.zeros_like(acc_sc)

        @pl.when(j == N_CHIPS - 1)
        def _():
            dz_ref[0] = dr_ref[...]

        @pl.when(j < N_CHIPS - 1)
        def _():
            for d, d_ref, tile_bufs in ((4, d4_ref, bufs[:_QKV_BLOCKS]), (16, d16_ref, bufs[_QKV_BLOCKS:])):
                _residues_to_rows(tile_bufs, d, lambda r, cb, d_ref=d_ref: d_ref[0, r, :, _LANES * cb:_LANES * (cb + 1)].astype(f32))
            for cb in range(_QKV_BLOCKS):
                cols = slice(_LANES * cb, _LANES * (cb + 1))
                dz_ref[0, :, cols] = (d1_ref[0, 0, :, cols].astype(f32) + bufs[cb][...] + bufs[_QKV_BLOCKS + cb][...]).astype(bf16)

        acc_sc[...] += _dot_nt(dz_ref[0], w_ref[0, 0])

        @pl.when(j == N_CHIPS - 1)
        def _():
            dh = acc_sc[...]
            xh, rstd0 = _ln_stats(x_ref[...])
            dmod_ref[0:1, :] += jnp.sum(dh, 0, keepdims=True)
            dmod_ref[1:2, :] += jnp.sum(dh * xh, 0, keepdims=True)
            dx_ref[...] = _ln_bwd(dh * (1.0 + mod_ref[1:2, :]), xh, rstd0) + dxr_ref[...]

    tok = pl.BlockSpec((ts, D), lambda i, j: (i, 0))
    res = [_res_spec(1, d, N, lambda i, j: (jnp.minimum(j, 2), 0, i, 0)) for d in DILATIONS]
    return pl.pallas_call(
        body, name="mix_in_bwd", grid=(S // ts, N_CHIPS),
        in_specs=res + [pl.BlockSpec((ts, N), lambda i, j: (i, 0)), tok, tok, _full((3, D)),
                        pl.BlockSpec((1, 1, D, N), lambda i, j: (j, l, 0, 0))],
        out_specs=[tok, _full((3, D)), pl.BlockSpec((1, ts, N), lambda i, j: (j, i, 0))],
        out_shape=[jax.ShapeDtypeStruct((S, D), f32), jax.ShapeDtypeStruct((3, D), f32),
                   jax.ShapeDtypeStruct((N_CHIPS, S, N), bf16)],
        scratch_shapes=[pltpu.VMEM((ts, D), f32)] + [pltpu.VMEM((ts, _LANES), f32)] * (2 * _QKV_BLOCKS),
        compiler_params=_cp(("arbitrary", "arbitrary"), 40),
    )(*_hbm(*dqkv, d_rest, dx_res, x, mod3, w_in))


def _mix_in_wgrad(h, dz, gw, l):
    S, D = h.shape
    N = dz.shape[-1]
    tk = TOK_TILE
    nk = S // tk

    def body(h_ref, dz_ref, _g, gw_ref, acc_sc):
        k = pl.program_id(1)

        @pl.when(k == 0)
        def _():
            acc_sc[...] = jnp.zeros_like(acc_sc)

        acc_sc[...] += _dot_tn(h_ref[...], dz_ref[0])

        @pl.when(k == nk - 1)
        def _():
            gw_ref[0, 0] = acc_sc[...].astype(bf16)

    return pl.pallas_call(
        body, name="mix_in_wgrad", grid=(N_CHIPS, nk),
        in_specs=[pl.BlockSpec((tk, D), lambda p, k: (k, 0)), pl.BlockSpec((1, tk, N), lambda p, k: (p, k, 0)),
                  pl.BlockSpec(memory_space=pl.ANY)],
        out_specs=pl.BlockSpec((1, 1, D, N), lambda p, k: (p, l, 0, 0)),
        out_shape=jax.ShapeDtypeStruct(gw.shape, bf16),
        scratch_shapes=[pltpu.VMEM((D, N), f32)],
        input_output_aliases={2: 0},
        compiler_params=_cp(("parallel", "arbitrary"), 40),
    )(*_hbm(h, dz, gw))


def _mix_out_fwd(x, y_att, y_ssm, y_pool, mod3, w_out, l, lng, lnb):
    S, D = x.shape
    ts = TOK_TILE

    def body(x_ref, ya_ref, ys_ref, yp_ref, mod_ref, w_ref, lng_ref, lnb_ref, xo_ref, y_ref):
        ya = ya_ref[...].astype(bf16)
        y = (_dot(ya[:, 0:256], w_ref[0, 0]) + _dot(ya[:, 256:512], w_ref[1, 0])
             + _dot(ys_ref[...].astype(bf16), w_ref[2, 0]) + _dot(yp_ref[...].astype(bf16), w_ref[3, 0]))
        y_ref[...] = y
        rh, _ = _ln_stats(ALPHA * x_ref[...] + mod_ref[2:3, :] * y)
        xo_ref[...] = rh * lng_ref[...] + lnb_ref[...]

    tok = pl.BlockSpec((ts, D), lambda i: (i, 0))
    return pl.pallas_call(
        body, name="mix_out_fwd", grid=(S // ts,),
        in_specs=[tok, pl.BlockSpec((ts, D_ATT), lambda i: (i, 0)), pl.BlockSpec((ts, D_SSM), lambda i: (i, 0)),
                  pl.BlockSpec((ts, D_POOL), lambda i: (i, 0)), _full((3, D)),
                  pl.BlockSpec((N_CHIPS, 1, 256, D), lambda i: (0, l, 0, 0)), _full((1, D)), _full((1, D))],
        out_specs=[tok, tok],
        out_shape=[jax.ShapeDtypeStruct((S, D), f32), jax.ShapeDtypeStruct((S, D), f32)],
        compiler_params=_cp(("parallel",), 40),
    )(*_hbm(x, y_att, y_ssm, y_pool, mod3, w_out, lng, lnb))


def _mix_out_bwd(dxo, x, y, y_att, y_ssm, y_pool, mod3, w_out, l, lng, gw_out):
    S, D = x.shape
    ts = TOK_TILE
    nt = S // ts

    def body(dxo_ref, x_ref, y_ref, ya_ref, ys_ref, yp_ref, mod_ref, w_ref, lng_ref, _g,
             dxr_ref, da_ref, ds_ref, dp_ref, dgate_ref, dlng_ref, dlnb_ref, gw_ref, acc_sc):
        i = pl.program_id(0)

        @pl.when(i == 0)
        def _():
            dgate_ref[...] = jnp.zeros_like(dgate_ref)
            dlng_ref[...] = jnp.zeros_like(dlng_ref)
            dlnb_ref[...] = jnp.zeros_like(dlnb_ref)
            acc_sc[...] = jnp.zeros_like(acc_sc)

        gate = mod_ref[2:3, :]
        yv = y_ref[...]
        rh, rstd = _ln_stats(ALPHA * x_ref[...] + gate * yv)
        dy_out = dxo_ref[...]
        dlng_ref[...] += jnp.sum(dy_out * rh, 0, keepdims=True)
        dlnb_ref[...] += jnp.sum(dy_out, 0, keepdims=True)
        dr = _ln_bwd(dy_out * lng_ref[...], rh, rstd)
        dxr_ref[...] = ALPHA * dr
        dgate_ref[...] += jnp.sum(dr * yv, 0, keepdims=True)
        dy = (gate * dr).astype(bf16)
        da_ref[:, 0:256] = _dot_nt(dy, w_ref[0, 0])
        da_ref[:, 256:512] = _dot_nt(dy, w_ref[1, 0])
        ds_ref[...] = _dot_nt(dy, w_ref[2, 0])
        dp_ref[...] = _dot_nt(dy, w_ref[3, 0])
        ya = ya_ref[...].astype(bf16)
        acc_sc[0] += _dot_tn(ya[:, 0:256], dy)
        acc_sc[1] += _dot_tn(ya[:, 256:512], dy)
        acc_sc[2] += _dot_tn(ys_ref[...].astype(bf16), dy)
        acc_sc[3] += _dot_tn(yp_ref[...].astype(bf16), dy)

        @pl.when(i == nt - 1)
        def _():
            gw_ref[:, 0] = acc_sc[...].astype(bf16)

    tok = pl.BlockSpec((ts, D), lambda i: (i, 0))
    t512 = pl.BlockSpec((ts, D_ATT), lambda i: (i, 0))
    t256 = pl.BlockSpec((ts, 256), lambda i: (i, 0))
    wspec = pl.BlockSpec((N_CHIPS, 1, 256, D), lambda i: (0, l, 0, 0))
    return pl.pallas_call(
        body, name="mix_out_bwd", grid=(nt,),
        in_specs=[tok, tok, tok, t512, t256, t256, _full((3, D)), wspec, _full((1, D)), pl.BlockSpec(memory_space=pl.ANY)],
        out_specs=[tok, t512, t256, t256, _full((1, D)), _full((1, D)), _full((1, D)), wspec],
        out_shape=[jax.ShapeDtypeStruct((S, D), f32), jax.ShapeDtypeStruct((S, D_ATT), f32),
                   jax.ShapeDtypeStruct((S, D_SSM), f32), jax.ShapeDtypeStruct((S, D_POOL), f32),
                   jax.ShapeDtypeStruct((1, D), f32), jax.ShapeDtypeStruct((1, D), f32), jax.ShapeDtypeStruct((1, D), f32),
                   jax.ShapeDtypeStruct(gw_out.shape, bf16)],
        scratch_shapes=[pltpu.VMEM((N_CHIPS, 256, D), f32)],
        input_output_aliases={9: 7},
        compiler_params=_cp(("arbitrary",), 48),
    )(*_hbm(dxo, x, y, y_att, y_ssm, y_pool, mod3, w_out, lng, gw_out))


def _t5_bucket(dist):
    max_exact = N_BUCKETS // 2
    d = np.maximum(dist, 1).astype(np.float32)
    large = max_exact + (np.log(d / max_exact) / math.log(MAX_DISTANCE / max_exact)
                         * (N_BUCKETS - max_exact)).astype(np.int32)
    large = np.minimum(large, N_BUCKETS - 1)
    return np.where(dist < max_exact, dist, large).astype(np.int32)


def _bucket_table():
    q = ATT_BLOCK
    i = np.arange(q)[:, None]
    j = np.arange(2 * q)[None, :]
    r = i + q - j
    in_band = (r >= 0) & (r <= q)
    tabs = [np.where(in_band, _t5_bucket(np.clip(r, 0, None) * d), -1) for d in DILATIONS]
    return np.stack(tabs).astype(np.int32)


def _bias_fwd(rel_bias, table):
    def body(rb_ref, tab_ref, out_ref):
        for b in range(3):
            tb = tab_ref[b]
            for h in range(N_HEADS):
                def pick(k, acc):
                    return jnp.where(tb == k, rb_ref[k, h], acc)
                out_ref[b, h] = lax.fori_loop(0, N_BUCKETS, pick, jnp.where(tb < 0, NEG, 0.0).astype(f32))

    return pl.pallas_call(
        body, name="bias_fwd",
        in_specs=[pl.BlockSpec(memory_space=pltpu.SMEM), pl.BlockSpec(memory_space=pltpu.VMEM)],
        out_specs=pl.BlockSpec(memory_space=pltpu.VMEM),
        out_shape=jax.ShapeDtypeStruct((3, N_HEADS, ATT_BLOCK, 2 * ATT_BLOCK), f32),
    )(rel_bias, table)


def _bias_bwd(dbias, table):
    def body(db_ref, tab_ref, out_ref):
        def per_bucket(k, c):
            for h in range(N_HEADS):
                tot = jnp.zeros((), f32)
                for b in range(3):
                    tot = tot + jnp.sum(jnp.where(tab_ref[b] == k, db_ref[b, h], 0.0))
                out_ref[k, h] = tot
            return c
        lax.fori_loop(0, N_BUCKETS, per_bucket, 0)

    return pl.pallas_call(
        body, name="bias_bwd",
        in_specs=[pl.BlockSpec(memory_space=pltpu.VMEM), pl.BlockSpec(memory_space=pltpu.VMEM)],
        out_specs=pl.BlockSpec(memory_space=pltpu.SMEM),
        out_shape=jax.ShapeDtypeStruct((N_BUCKETS, N_HEADS), f32),
    )(dbias, table)


def _att_unit(u, nbr):
    rows = pl.ds(pl.multiple_of(u * ATT_BLOCK, ATT_BLOCK), ATT_BLOCK)
    prev = pl.ds(pl.multiple_of(jnp.maximum(u - 1, 0) * ATT_BLOCK, ATT_BLOCK), ATT_BLOCK)
    return rows, prev, (u % nbr) != 0


_N_PAIRS = N_HEADS // 2


def _pair_rows(t):
    lane = lax.broadcasted_iota(jnp.int32, t.shape, 1)
    zero = jnp.zeros_like(t)
    return jnp.concatenate([jnp.where(lane < HEAD_DIM, t, zero), jnp.where(lane >= HEAD_DIM, t, zero)], axis=0)


def _pair_cols(big):
    lane = lax.broadcasted_iota(jnp.int32, (ATT_BLOCK, _LANES), 1)
    return jnp.where(lane < HEAD_DIM, big[:ATT_BLOCK], big[ATT_BLOCK:])


def _pair_column(ref, rows, hp):
    t = ref[rows, :]
    return jnp.concatenate([t[:, 2 * hp:2 * hp + 1], t[:, 2 * hp + 1:2 * hp + 2]], axis=0)


def _pair_band(ref, rows, prev, nbr, hp):
    lanes = pl.ds(_LANES * hp, _LANES)
    cur = ref[0, rows, lanes]
    return cur if nbr == 1 else jnp.concatenate([ref[0, prev, lanes], cur], axis=0)


def _pair_scores(q_ref, k_ref, b_ref, rows, prev, valid_prev, nbr, hp):
    qbd = _pair_rows(q_ref[0, rows, pl.ds(_LANES * hp, _LANES)])
    kb = _pair_band(k_ref, rows, prev, nbr, hp)
    bias = b_ref[0, 2 * hp:2 * hp + 2].reshape(2 * ATT_BLOCK, 2 * ATT_BLOCK)
    if nbr == 1:
        return qbd, kb, _dot_nt(qbd, kb) + bias[:, ATT_BLOCK:]
    s = _dot_nt(qbd, kb) + bias
    col = lax.broadcasted_iota(jnp.int32, s.shape, 1)
    return qbd, kb, jnp.where((col >= ATT_BLOCK) | valid_prev, s, NEG)


def _qkv_specs(S, branch):
    return ([pl.BlockSpec((1, S, D_ATT), lambda i, t=t: (t, 0, 0)) for t in range(3)],
            pl.BlockSpec((1, N_HEADS, ATT_BLOCK, 2 * ATT_BLOCK), lambda i: (branch, 0, 0, 0)))


def _att_fwd(qkv, bias, branch):
    S = qkv.shape[1]
    nbr = BLOCKS_PER_RESIDUE[branch]

    def body(q_ref, k_ref, v_ref, b_ref, o_ref, lse_ref):
        lse_ref[...] = jnp.zeros_like(lse_ref)

        def unit(u, c):
            rows, prev, valid_prev = _att_unit(u, nbr)
            for hp in range(_N_PAIRS):
                _, _, s = _pair_scores(q_ref, k_ref, b_ref, rows, prev, valid_prev, nbr, hp)
                m = jnp.max(s, -1, keepdims=True)
                p = jnp.exp(s - m)
                den = jnp.sum(p, -1, keepdims=True)
                big = _dot(p.astype(bf16), _pair_band(v_ref, rows, prev, nbr, hp))
                o_ref[rows, pl.ds(_LANES * hp, _LANES)] = _pair_cols(big / den)
                lse = m + jnp.log(den)
                lse_ref[rows, pl.ds(2 * hp, 1)] = lse[:ATT_BLOCK]
                lse_ref[rows, pl.ds(2 * hp + 1, 1)] = lse[ATT_BLOCK:]
            return c

        lax.fori_loop(0, N_UNITS, unit, 0)

    qkv_specs, bspec = _qkv_specs(S, branch)
    return pl.pallas_call(
        body, name="att_fwd", grid=(1,),
        in_specs=qkv_specs + [bspec],
        out_specs=[pl.BlockSpec((S, D_ATT), lambda i: (0, 0)), pl.BlockSpec((S, _LANES), lambda i: (0, 0))],
        out_shape=[jax.ShapeDtypeStruct((S, D_ATT), f32), jax.ShapeDtypeStruct((S, _LANES), f32)],
        compiler_params=_cp(("arbitrary",), 40),
    )(*_hbm(qkv, qkv, qkv, bias))


def _att_bwd(qkv, do, lse, crow, bias, branch):
    S = qkv.shape[1]
    nbr = BLOCKS_PER_RESIDUE[branch]

    def body(q_ref, k_ref, v_ref, do_ref, lse_ref, c_ref, b_ref, dqkv_ref, db_ref, dk_sc, dv_sc):
        dk_sc[...] = jnp.zeros_like(dk_sc)
        dv_sc[...] = jnp.zeros_like(dv_sc)
        db_ref[...] = jnp.zeros_like(db_ref)

        def unit(u, c):
            rows, prev, valid_prev = _att_unit(u, nbr)
            for hp in range(_N_PAIRS):
                lanes = pl.ds(_LANES * hp, _LANES)
                qbd, kb, s = _pair_scores(q_ref, k_ref, b_ref, rows, prev, valid_prev, nbr, hp)
                p = jnp.exp(s - _pair_column(lse_ref, rows, hp))
                dobd = _pair_rows(do_ref[rows, lanes])
                ds = p * (_dot_nt(dobd, _pair_band(v_ref, rows, prev, nbr, hp)) - _pair_column(c_ref, rows, hp))
                if nbr == 1:
                    db_ref[2 * hp:2 * hp + 2, :, ATT_BLOCK:] += ds.reshape(2, ATT_BLOCK, ATT_BLOCK)
                else:
                    db_ref[2 * hp:2 * hp + 2] += ds.reshape(2, ATT_BLOCK, 2 * ATT_BLOCK)
                dsb = ds.astype(bf16)
                dqkv_ref[0, rows, lanes] = (HEAD_DIM ** -0.5 * _pair_cols(_dot(dsb, kb))).astype(bf16)
                dkb = _dot_tn(dsb, qbd)
                dvb = _dot_tn(p.astype(bf16), dobd)
                if nbr == 1:
                    dk_sc[rows, lanes] += dkb
                    dv_sc[rows, lanes] += dvb
                else:
                    dk_sc[prev, lanes] += dkb[:ATT_BLOCK]
                    dv_sc[prev, lanes] += dvb[:ATT_BLOCK]
                    dk_sc[rows, lanes] += dkb[ATT_BLOCK:]
                    dv_sc[rows, lanes] += dvb[ATT_BLOCK:]
            return c

        lax.fori_loop(0, N_UNITS, unit, 0)
        dqkv_ref[1] = dk_sc[...].astype(bf16)
        dqkv_ref[2] = dv_sc[...].astype(bf16)

    qkv_specs, bspec = _qkv_specs(S, branch)
    row = pl.BlockSpec((S, _LANES), lambda i: (0, 0))
    return pl.pallas_call(
        body, name="att_bwd", grid=(1,),
        in_specs=qkv_specs + [pl.BlockSpec((S, D_ATT), lambda i: (0, 0)), row, row, bspec],
        out_specs=[pl.BlockSpec((3, S, D_ATT), lambda i: (0, 0, 0)),
                   pl.BlockSpec((N_HEADS, ATT_BLOCK, 2 * ATT_BLOCK), lambda i: (0, 0, 0))],
        out_shape=[jax.ShapeDtypeStruct((3, S, D_ATT), bf16), jax.ShapeDtypeStruct((N_HEADS, ATT_BLOCK, 2 * ATT_BLOCK), f32)],
        scratch_shapes=[pltpu.VMEM((S, D_ATT), f32), pltpu.VMEM((S, D_ATT), f32)],
        compiler_params=_cp(("arbitrary",), 48),
    )(*_hbm(qkv, qkv, qkv, do, lse, crow, bias))


def _branch_weights(lse_ref):
    l0, l1, l2 = lse_ref[0], lse_ref[1], lse_ref[2]
    m = jnp.maximum(jnp.maximum(l0, l1), l2)
    e0, e1, e2 = jnp.exp(l0 - m), jnp.exp(l1 - m), jnp.exp(l2 - m)
    tot = e0 + e1 + e2
    return e0 / tot, e1 / tot, e2 / tot


def _att_merge(os, lses):
    S = os[0].shape[0] * os[0].shape[1]
    ts = TOK_TILE

    def body(o1_ref, o4_ref, o16_ref, l1_ref, l4_ref, l16_ref, y_ref, lt_ref, *bufs):
        obufs = (bufs[:_QKV_BLOCKS], bufs[_QKV_BLOCKS:2 * _QKV_BLOCKS])
        lt_ref[0] = l1_ref[0]
        for k, (d, o_ref, l_ref) in enumerate(((4, o4_ref, l4_ref), (16, o16_ref, l16_ref))):
            _residues_to_rows(obufs[k], d, lambda r, cb, o_ref=o_ref: o_ref[r, :, _LANES * cb:_LANES * (cb + 1)])
            _residues_to_rows([bufs[2 * _QKV_BLOCKS + k]], d, lambda r, cb, l_ref=l_ref: l_ref[r])
            lt_ref[1 + k] = bufs[2 * _QKV_BLOCKS + k][...]
        w = _branch_weights(lt_ref)
        for h in range(N_HEADS):
            cs = slice(HEAD_DIM * h, HEAD_DIM * (h + 1))
            half = slice(HEAD_DIM * (h % 2), HEAD_DIM * (h % 2 + 1))
            y_ref[:, cs] = (w[0][:, h:h + 1] * o1_ref[0, :, cs] + w[1][:, h:h + 1] * obufs[0][h // 2][:, half]
                            + w[2][:, h:h + 1] * obufs[1][h // 2][:, half])

    return pl.pallas_call(
        body, name="att_merge", grid=(S // ts,),
        in_specs=[_res_spec3(d, D_ATT) for d in DILATIONS] + [_res_spec3(d, _LANES) for d in DILATIONS],
        out_specs=[pl.BlockSpec((ts, D_ATT), lambda i: (i, 0)), pl.BlockSpec((3, ts, _LANES), lambda i: (0, i, 0))],
        out_shape=[jax.ShapeDtypeStruct((S, D_ATT), f32), jax.ShapeDtypeStruct((3, S, _LANES), f32)],
        scratch_shapes=[pltpu.VMEM((ts, _LANES), f32)] * (2 * _QKV_BLOCKS + 2),
        compiler_params=_cp(("parallel",)),
    )(*_hbm(*os, *lses))


def _att_merge_bwd(dy, y, lse3):
    S = dy.shape[0]
    ts = TOK_TILE

    def body(dy_ref, y_ref, lse_ref, do1_ref, do4_ref, do16_ref, c1_ref, c4_ref, c16_ref, *bufs):
        dobufs = (bufs[:_QKV_BLOCKS], bufs[_QKV_BLOCKS:2 * _QKV_BLOCKS], bufs[2 * _QKV_BLOCKS:3 * _QKV_BLOCKS])
        cbufs = bufs[3 * _QKV_BLOCKS:]
        w = _branch_weights(lse_ref)
        for cb in cbufs:
            cb[...] = jnp.zeros_like(cb)
        for h in range(N_HEADS):
            cs = slice(HEAD_DIM * h, HEAD_DIM * (h + 1))
            half = slice(HEAD_DIM * (h % 2), HEAD_DIM * (h % 2 + 1))
            dyh = dy_ref[:, cs]
            t = jnp.sum(dyh * y_ref[:, cs], -1, keepdims=True)
            for p in range(3):
                wp = w[p][:, h:h + 1]
                dobufs[p][h // 2][:, half] = wp * dyh
                cbufs[p][:, h:h + 1] = wp * t
        for cb in range(_QKV_BLOCKS):
            do1_ref[0, :, _LANES * cb:_LANES * (cb + 1)] = dobufs[0][cb][...].astype(bf16)
        c1_ref[0] = cbufs[0][...]
        for k, (d, do_ref, c_ref) in enumerate(((4, do4_ref, c4_ref), (16, do16_ref, c16_ref))):
            def put_do(r, cb, piece, do_ref=do_ref):
                do_ref[r, :, _LANES * cb:_LANES * (cb + 1)] = piece.astype(bf16)

            def put_c(r, cb, piece, c_ref=c_ref):
                c_ref[r] = piece

            _rows_to_residues(dobufs[1 + k], d, put_do)
            _rows_to_residues([cbufs[1 + k]], d, put_c)

    return pl.pallas_call(
        body, name="att_merge_bwd", grid=(S // ts,),
        in_specs=[pl.BlockSpec((ts, D_ATT), lambda i: (i, 0)), pl.BlockSpec((ts, D_ATT), lambda i: (i, 0)),
                  pl.BlockSpec((3, ts, _LANES), lambda i: (0, i, 0))],
        out_specs=[_res_spec3(d, D_ATT) for d in DILATIONS] + [_res_spec3(d, _LANES) for d in DILATIONS],
        out_shape=[jax.ShapeDtypeStruct((d, S // d, D_ATT), bf16) for d in DILATIONS]
        + [jax.ShapeDtypeStruct((d, S // d, _LANES), f32) for d in DILATIONS],
        scratch_shapes=[pltpu.VMEM((ts, _LANES), f32)] * (3 * _QKV_BLOCKS + 3),
        compiler_params=_cp(("parallel",)),
    )(*_hbm(dy, y, lse3))


_SSM_ROWS = 256


def _scan_in_place(sr_ref, si_ref, a_ref, reverse):
    S, N = sr_ref.shape
    nst = S // SCAN_SEG
    ar = jnp.broadcast_to(a_ref[0:1, :], (SCAN_SEG, N))
    ai = jnp.broadcast_to(a_ref[1:2, :], (SCAN_SEG, N))
    if reverse:
        ai = -ai
    row = lax.broadcasted_iota(jnp.int32, (SCAN_SEG, N), 0)
    zero = jnp.zeros((SCAN_SEG, N), f32)

    def tile(t):
        return pl.ds(pl.multiple_of((nst - 1 - t if reverse else t) * SCAN_SEG, SCAN_SEG), SCAN_SEG)

    def local(t, c):
        sr, si, pr, pi = c
        rows = tile(t)
        nsr = ar * sr - ai * si + sr_ref[rows, :]
        nsi = ar * si + ai * sr + si_ref[rows, :]
        sr_ref[rows, :] = nsr
        si_ref[rows, :] = nsi
        return nsr, nsi, ar * pr - ai * pi, ar * pi + ai * pr

    fr, fi, apr, api = lax.fori_loop(0, nst, local, (zero, zero, zero + 1.0, zero))

    def shift(v):
        if reverse:
            return jnp.where(row == SCAN_SEG - 1, 0.0, pltpu.roll(v, SCAN_SEG - 1, axis=0))
        return jnp.where(row == 0, 0.0, pltpu.roll(v, 1, axis=0))

    cr, ci = zero, zero
    for _ in range(SCAN_SEG - 1):
        cr, ci = shift(fr + apr * cr - api * ci), shift(fi + apr * ci + api * cr)

    def fix(t, c):
        pr, pi = c
        npr, npi = ar * pr - ai * pi, ar * pi + ai * pr
        rows = tile(t)
        sr_ref[rows, :] += npr * cr - npi * ci
        si_ref[rows, :] += npr * ci + npi * cr
        return npr, npi

    lax.fori_loop(0, nst, fix, (zero + 1.0, zero))


def _ssm_states(u, bre, bim, a2):
    S = u.shape[0]

    def body(u_ref, br_ref, bi_ref, a_ref, sr_ref, si_ref):
        brb = br_ref[...].astype(bf16)
        bib = bi_ref[...].astype(bf16)

        def project(t, c):
            rows = pl.ds(pl.multiple_of(t * _SSM_ROWS, _SSM_ROWS), _SSM_ROWS)
            ub = u_ref[rows, :].astype(bf16)
            sr_ref[rows, :] = _dot(ub, brb)
            si_ref[rows, :] = _dot(ub, bib)
            return c

        lax.fori_loop(0, S // _SSM_ROWS, project, 0)
        _scan_in_place(sr_ref, si_ref, a_ref, False)

    vm = pl.BlockSpec(memory_space=pltpu.VMEM)
    return pl.pallas_call(
        body, name="ssm_states", in_specs=[vm] * 4, out_specs=[vm, vm],
        out_shape=[jax.ShapeDtypeStruct((S, D_STATE), f32)] * 2,
        compiler_params=_cp(None, 48),
    )(u, bre, bim, a2)


def _ssm_out(sr, si, u, cre, cim, dskip, glu_w, glu_b):
    S = u.shape[0]
    ts = TOK_TILE

    def body(sr_ref, si_ref, u_ref, cr_ref, ci_ref, d_ref, w_ref, b_ref, out_ref, y_ref):
        y = (_dot(sr_ref[...].astype(bf16), cr_ref[...].astype(bf16))
             - _dot(si_ref[...].astype(bf16), ci_ref[...].astype(bf16)) + d_ref[...] * u_ref[...])
        y_ref[...] = y
        z = _dot(_gelu(y).astype(bf16), w_ref[...].astype(bf16)) + b_ref[...]
        out_ref[...] = y * jax.nn.sigmoid(z)

    st = pl.BlockSpec((ts, D_STATE), lambda i: (i, 0))
    ch = pl.BlockSpec((ts, D_SSM), lambda i: (i, 0))
    return pl.pallas_call(
        body, name="ssm_out", grid=(S // ts,),
        in_specs=[st, st, ch, _full((D_STATE, D_SSM)), _full((D_STATE, D_SSM)), _full((1, D_SSM)),
                  _full((D_SSM, D_SSM)), _full((1, D_SSM))],
        out_specs=[ch, ch],
        out_shape=[jax.ShapeDtypeStruct((S, D_SSM), f32)] * 2,
        compiler_params=_cp(("parallel",)),
    )(*_hbm(sr, si, u, cre, cim, dskip, glu_w, glu_b))


def _ssm_out_bwd(dout, y, u, sr, si, dskip, glu_w, glu_b):
    S = u.shape[0]
    ts = TOK_TILE

    def body(do_ref, y_ref, u_ref, sr_ref, si_ref, d_ref, w_ref, b_ref,
             dy_ref, du_ref, dcr_ref, dci_ref, dd_ref, dgb_ref, dgw_ref):
        @pl.when(pl.program_id(0) == 0)
        def _():
            for r in (dcr_ref, dci_ref, dd_ref, dgb_ref, dgw_ref):
                r[...] = jnp.zeros_like(r)

        y = y_ref[...]
        dout = do_ref[...]
        wb = w_ref[...].astype(bf16)
        ge = _gelu(y).astype(bf16)
        sz = jax.nn.sigmoid(_dot(ge, wb) + b_ref[...])
        dz = dout * y * sz * (1.0 - sz)
        dzb = dz.astype(bf16)
        dgb_ref[...] += jnp.sum(dz, 0, keepdims=True)
        dgw_ref[...] += _dot_tn(ge, dzb)
        dy = dout * sz + _gelu_grad(y) * _dot_nt(dzb, wb)
        uv = u_ref[...]
        dd_ref[...] += jnp.sum(dy * uv, 0, keepdims=True)
        du_ref[...] = dy * d_ref[...]
        dy_ref[...] = dy
        dyb = dy.astype(bf16)
        dcr_ref[...] += _dot_tn(sr_ref[...].astype(bf16), dyb)
        dci_ref[...] -= _dot_tn(si_ref[...].astype(bf16), dyb)

    st = pl.BlockSpec((ts, D_STATE), lambda i: (i, 0))
    ch = pl.BlockSpec((ts, D_SSM), lambda i: (i, 0))
    c_full = _full((D_STATE, D_SSM))
    return pl.pallas_call(
        body, name="ssm_out_bwd", grid=(S // ts,),
        in_specs=[ch, ch, ch, st, st, _full((1, D_SSM)), _full((D_SSM, D_SSM)), _full((1, D_SSM))],
        out_specs=[ch, ch, c_full, c_full, _full((1, D_SSM)), _full((1, D_SSM)), _full((D_SSM, D_SSM))],
        out_shape=[jax.ShapeDtypeStruct((S, D_SSM), f32), jax.ShapeDtypeStruct((S, D_SSM), f32),
                   jax.ShapeDtypeStruct((D_STATE, D_SSM), f32), jax.ShapeDtypeStruct((D_STATE, D_SSM), f32),
                   jax.ShapeDtypeStruct((1, D_SSM), f32), jax.ShapeDtypeStruct((1, D_SSM), f32),
                   jax.ShapeDtypeStruct((D_SSM, D_SSM), f32)],
        compiler_params=_cp(("arbitrary",), 40),
    )(*_hbm(dout, y, u, sr, si, dskip, glu_w, glu_b))


def _ssm_states_bwd(dy, du_skip, u, sr, si, cre, cim, bre, bim, a2):
    S = u.shape[0]
    N = D_STATE
    nst = S // SCAN_SEG
    nproj = S // _SSM_ROWS

    def body(dy_ref, dus_ref, u_ref, sr_ref, si_ref, cr_ref, ci_ref, br_ref, bi_ref, a_ref,
             du_ref, dbr_ref, dbi_ref, da_ref, lr_ref, li_ref):
        crb = cr_ref[...].astype(bf16)
        cib = ci_ref[...].astype(bf16)

        def project(t, c):
            rows = pl.ds(pl.multiple_of(t * _SSM_ROWS, _SSM_ROWS), _SSM_ROWS)
            dyb = dy_ref[rows, :].astype(bf16)
            lr_ref[rows, :] = _dot_nt(dyb, crb)
            li_ref[rows, :] = -_dot_nt(dyb, cib)
            return c

        lax.fori_loop(0, nproj, project, 0)
        _scan_in_place(lr_ref, li_ref, a_ref, True)

        row = lax.broadcasted_iota(jnp.int32, (SCAN_SEG, N), 0)
        last = pl.ds((nst - 1) * SCAN_SEG, SCAN_SEG)
        pr = jnp.where(row == 0, 0.0, pltpu.roll(sr_ref[last, :], 1, axis=0))
        pi = jnp.where(row == 0, 0.0, pltpu.roll(si_ref[last, :], 1, axis=0))
        first = pl.ds(0, SCAN_SEG)
        acc_r = lr_ref[first, :] * pr + li_ref[first, :] * pi
        acc_i = li_ref[first, :] * pr - lr_ref[first, :] * pi

        def step(t, c):
            acc_r, acc_i = c
            rows = pl.ds(pl.multiple_of(t * SCAN_SEG, SCAN_SEG), SCAN_SEG)
            prev = pl.ds(pl.multiple_of((t - 1) * SCAN_SEG, SCAN_SEG), SCAN_SEG)
            lrv, liv, srv, siv = lr_ref[rows, :], li_ref[rows, :], sr_ref[prev, :], si_ref[prev, :]
            return acc_r + lrv * srv + liv * siv, acc_i + liv * srv - lrv * siv

        acc_r, acc_i = lax.fori_loop(1, nst, step, (acc_r, acc_i))
        da_ref[0:1, :] = jnp.sum(acc_r, 0, keepdims=True)
        da_ref[1:2, :] = jnp.sum(acc_i, 0, keepdims=True)

        brb = br_ref[...].astype(bf16)
        bib = bi_ref[...].astype(bf16)
        dbr_ref[...] = jnp.zeros_like(dbr_ref)
        dbi_ref[...] = jnp.zeros_like(dbi_ref)

        def back(t, c):
            rows = pl.ds(pl.multiple_of(t * _SSM_ROWS, _SSM_ROWS), _SSM_ROWS)
            lrb = lr_ref[rows, :].astype(bf16)
            lib = li_ref[rows, :].astype(bf16)
            du_ref[rows, :] = dus_ref[rows, :] + _dot_nt(lrb, brb) + _dot_nt(lib, bib)
            ub = u_ref[rows, :].astype(bf16)
            dbr_ref[...] += _dot_tn(ub, lrb)
            dbi_ref[...] += _dot_tn(ub, lib)
            return c

        lax.fori_loop(0, nproj, back, 0)

    vm = pl.BlockSpec(memory_space=pltpu.VMEM)
    return pl.pallas_call(
        body, name="ssm_states_bwd", in_specs=[vm] * 10, out_specs=[vm] * 4,
        out_shape=[jax.ShapeDtypeStruct((S, D_SSM), f32), jax.ShapeDtypeStruct((D_SSM, D_STATE), f32),
                   jax.ShapeDtypeStruct((D_SSM, D_STATE), f32), jax.ShapeDtypeStruct((2, D_STATE), f32)],
        scratch_shapes=[pltpu.VMEM((S, D_STATE), f32), pltpu.VMEM((S, D_STATE), f32)],
        compiler_params=_cp(None, 56),
    )(dy, du_skip, u, sr, si, cre, cim, bre, bim, a2)


_POOL_TILE = 256


def _window_sums(xt, back):
    n = xt.shape[0]
    out = []
    ws = xt
    for k in (1, 2, 4, 8):
        ws = ws + pltpu.roll(ws, k if back else n - k, axis=0)
        out.append(ws)
    return out


def _pool_count(r0, w):
    t = r0 + lax.broadcasted_iota(jnp.int32, (_POOL_TILE, POOL_GROUP), 0)
    return jnp.minimum(t + 1, w).astype(f32)


def _pool_fwd(u_pad, pool_w, pool_scale):
    S = u_pad.shape[0] - POOL_HALO
    nt = S // _POOL_TILE

    def body(u_ref, w_ref, sc_ref, y_ref):
        def tile(t, c):
            r0 = pl.multiple_of(t * _POOL_TILE, _POOL_TILE)
            for g, w in enumerate(POOL_WINDOWS):
                cs = pl.ds(POOL_GROUP * g, POOL_GROUP)
                xt = u_ref[pl.ds(r0, _POOL_TILE + POOL_HALO), cs]
                ws = _window_sums(xt, True)[g][POOL_HALO:, :]
                pooled = ws / _pool_count(r0, w) - xt[POOL_HALO:, :]
                y_ref[pl.ds(r0, _POOL_TILE), cs] = _dot(pooled.astype(bf16), w_ref[g].astype(bf16)) * sc_ref[:, cs]
            return c
        lax.fori_loop(0, nt, tile, 0)

    vm = pl.BlockSpec(memory_space=pltpu.VMEM)
    return pl.pallas_call(
        body, name="pool_fwd", in_specs=[vm, vm, vm], out_specs=vm,
        out_shape=jax.ShapeDtypeStruct((S, D_POOL), f32),
    )(u_pad, pool_w, pool_scale)


def _pool_bwd(dy_pad, u_pad, pool_w, pool_scale):
    S = u_pad.shape[0] - POOL_HALO
    nt = S // _POOL_TILE
    n = _POOL_TILE + POOL_HALO

    def body(dy_ref, u_ref, w_ref, sc_ref, du_ref, dw_ref, dsc_ref):
        dw_ref[...] = jnp.zeros_like(dw_ref)
        dsc_ref[...] = jnp.zeros_like(dsc_ref)

        def tile(t, c):
            r0 = pl.multiple_of(t * _POOL_TILE, _POOL_TILE)
            for g, w in enumerate(POOL_WINDOWS):
                cs = pl.ds(POOL_GROUP * g, POOL_GROUP)
                wb = w_ref[g].astype(bf16)
                xt = u_ref[pl.ds(r0, n), cs]
                pooled = (_window_sums(xt, True)[g][POOL_HALO:, :] / _pool_count(r0, w) - xt[POOL_HALO:, :]).astype(bf16)
                dy = dy_ref[pl.ds(r0, _POOL_TILE), cs]
                dsc_ref[:, cs] += jnp.sum(dy * _dot(pooled, wb), 0, keepdims=True)
                dw_ref[g] += _dot_tn(pooled, (dy * sc_ref[:, cs]).astype(bf16))
                dyh = (dy_ref[pl.ds(r0, n), cs] * sc_ref[:, cs]).astype(bf16)
                dpl = _dot_nt(dyh, wb)
                cnt = jnp.minimum(r0 + lax.broadcasted_iota(jnp.int32, (n, POOL_GROUP), 0) + 1, w).astype(f32)
                lead = _window_sums(dpl / cnt, False)[g]
                du_ref[pl.ds(r0, _POOL_TILE), cs] = lead[:_POOL_TILE, :] - dpl[:_POOL_TILE, :]
            return c
        lax.fori_loop(0, nt, tile, 0)

    vm = pl.BlockSpec(memory_space=pltpu.VMEM)
    return pl.pallas_call(
        body, name="pool_bwd", in_specs=[vm, vm, vm, vm], out_specs=[vm, vm, vm],
        out_shape=[jax.ShapeDtypeStruct((S, D_POOL), f32), jax.ShapeDtypeStruct((4, POOL_GROUP, POOL_GROUP), f32),
                   jax.ShapeDtypeStruct((1, D_POOL), f32)],
    )(dy_pad, u_pad, pool_w, pool_scale)


def _loss_head(y, target):
    S, D = y.shape
    ts = TOK_TILE

    def body(y_ref, t_ref, loss_ref, dy_ref):
        @pl.when(pl.program_id(0) == 0)
        def _():
            loss_ref[...] = jnp.zeros_like(loss_ref)

        d = y_ref[...] - t_ref[...]
        dy_ref[...] = d * (1.0 / D)
        loss_ref[...] += 0.5 * jnp.sum(jnp.sum(d * d, -1, keepdims=True) * (1.0 / D), 0, keepdims=True)

    tok = pl.BlockSpec((ts, D), lambda i: (i, 0))
    return pl.pallas_call(
        body, name="loss_head", grid=(S // ts,),
        in_specs=[tok, tok], out_specs=[_full((1, 1)), tok],
        out_shape=[jax.ShapeDtypeStruct((1, 1), f32), jax.ShapeDtypeStruct((S, D), f32)],
        compiler_params=_cp(("arbitrary",)),
    )(*_hbm(y, target))


_ADA_COLS = 768


def _ada_fwd(c_all, ada_w, ada_b_cols):
    L, D, N = ada_w.shape
    B = c_all.shape[0]

    def body(c_ref, w_ref, b_ref, out_ref):
        cv = c_ref[...]
        cond = (cv * jax.nn.sigmoid(cv)).astype(bf16)
        out_ref[0] = _dot(cond, w_ref[0].astype(bf16)) + b_ref[0]

    return pl.pallas_call(
        body, name="ada_fwd", grid=(L, N // _ADA_COLS),
        in_specs=[_full((B, D)), pl.BlockSpec((1, D, _ADA_COLS), lambda l, j: (l, 0, j)),
                  pl.BlockSpec((1, 1, _ADA_COLS), lambda l, j: (l, 0, j))],
        out_specs=pl.BlockSpec((1, B, _ADA_COLS), lambda l, j: (l, 0, j)),
        out_shape=jax.ShapeDtypeStruct((L, B, N), f32),
        compiler_params=_cp(("parallel", "parallel")),
    )(c_all, ada_w, ada_b_cols)


def _ada_wgrad(c_all_t, dmod_cols):
    D, B = c_all_t.shape
    L, _, N = dmod_cols.shape

    def body(ct_ref, dm_ref, out_ref):
        cv = ct_ref[...]
        cond = cv * jax.nn.sigmoid(cv)
        acc = cond[:, 0:1] * dm_ref[0, 0:1, :]
        for b in range(1, B):
            acc = acc + cond[:, b:b + 1] * dm_ref[0, b:b + 1, :]
        out_ref[0] = acc

    return pl.pallas_call(
        body, name="ada_wgrad", grid=(L, N // _ADA_COLS),
        in_specs=[_full((D, B)), pl.BlockSpec((1, B, _ADA_COLS), lambda l, j: (l, 0, j))],
        out_specs=pl.BlockSpec((1, D, _ADA_COLS), lambda l, j: (l, 0, j)),
        out_shape=jax.ShapeDtypeStruct((L, D, N), f32),
        compiler_params=_cp(("parallel", "parallel")),
    )(c_all_t, dmod_cols)


def _adam_math(w, g, m, v):
    m = ADAM_B1 * m + (1.0 - ADAM_B1) * g
    v = ADAM_B2 * v + (1.0 - ADAM_B2) * (g * g)
    m_hat = m / (1.0 - ADAM_B1 ** ADAM_STEP)
    v_hat = v / (1.0 - ADAM_B2 ** ADAM_STEP)
    delta = -ADAM_LR * (m_hat / (jnp.sqrt(v_hat) + ADAM_EPS) + ADAM_WD * w)
    return delta, m, v


def _adamw(w, m, v, g, row_tile, row0=0, outs=None):
    R, C = w.shape
    b0 = row0 // row_tile

    def body(w_ref, m_ref, v_ref, g_ref, _0, _1, _2, _3, g_out, d_out, m_out, v_out):
        gv = g_ref[...]
        delta, mn, vn = _adam_math(w_ref[...], gv, m_ref[...], v_ref[...])
        g_out[...] = gv
        d_out[...] = delta
        m_out[...] = mn
        v_out[...] = vn

    pspec = pl.BlockSpec((row_tile, C), lambda i: (b0 + i, 0))
    gspec = pl.BlockSpec((row_tile, C), lambda i: (i, 0))
    anyspec = pl.BlockSpec(memory_space=pl.ANY)
    shp = jax.ShapeDtypeStruct((R, C), f32)
    if outs is None:
        outs = [lax.empty((R, C), f32) for _ in range(4)]
    return pl.pallas_call(
        body, name="adamw", grid=(g.shape[0] // row_tile,),
        in_specs=[pspec] * 3 + [gspec] + [anyspec] * 4, out_specs=[pspec] * 4, out_shape=[shp] * 4,
        input_output_aliases={4: 0, 5: 1, 6: 2, 7: 3},
        compiler_params=_cp(("parallel",), 40),
    )(w, m, v, g, *outs)


def _pair_sum(g5s, gots, pc):
    n = len(g5s)

    def body(pc_ref, *refs):
        for own, got, out in zip(refs[:n], refs[n:2 * n], refs[2 * n:]):
            out[0, 0] = (own[0, 0, 0].astype(f32) + got[0, 0].astype(f32)).astype(bf16)

    def half(g):
        return pl.BlockSpec((1, 1) + g.shape[-2:], lambda p, pc: (p, 0, 0, 0))

    gs = pltpu.PrefetchScalarGridSpec(
        num_scalar_prefetch=1, grid=(N_CHIPS,),
        in_specs=[pl.BlockSpec((1, 1, 1) + g.shape[-2:], lambda p, pc: (p, 0, pc[1], 0, 0)) for g in g5s]
        + [half(g) for g in gots],
        out_specs=[half(g) for g in gots],
    )
    return pl.pallas_call(
        body, name="pair_sum", grid_spec=gs, out_shape=[jax.ShapeDtypeStruct(g.shape, bf16) for g in gots],
        compiler_params=_cp(("parallel",), 48),
    )(pc, *_hbm(*g5s, *gots))


_SUM_STEPS = 2


def _sum_shards(hsums, recvs, pc):
    n = len(hsums)

    def body(pc_ref, *refs):
        for own, got, out in zip(refs[:n], refs[n:2 * n], refs[2 * n:]):
            acc = own[0, 0].astype(f32)
            for j in range(3):
                acc = acc + got[j, 0].astype(f32)
            out[0, 0] = acc

    def rows(h):
        return (h.shape[2] // _SUM_STEPS, h.shape[3])

    gs = pltpu.PrefetchScalarGridSpec(
        num_scalar_prefetch=1, grid=(_SUM_STEPS,),
        in_specs=[pl.BlockSpec((1, 1) + rows(h), lambda i, pc: (pc[0], 0, i, 0)) for h in hsums]
        + [pl.BlockSpec((3, 1) + rows(h), lambda i, pc: (0, 0, i, 0)) for h in hsums],
        out_specs=[pl.BlockSpec((1, 1) + rows(h), lambda i, pc: (0, pc[1], i, 0)) for h in hsums],
    )
    return pl.pallas_call(
        body, name="sum_shards", grid_spec=gs,
        out_shape=[jax.ShapeDtypeStruct((1, 2) + h.shape[2:], f32) for h in hsums],
        compiler_params=_cp(("parallel",), 48),
    )(pc, *_hbm(*hsums, *recvs))


def _sum8(packs):
    _, R, C = packs.shape
    tr = R // 8 if R % 64 == 0 else R

    def body(p_ref, out_ref):
        acc = p_ref[0]
        for d in range(1, 8):
            acc = acc + p_ref[d]
        out_ref[...] = acc

    return pl.pallas_call(
        body, name="sum8", grid=(R // tr,),
        in_specs=[pl.BlockSpec((8, tr, C), lambda i: (0, i, 0))],
        out_specs=pl.BlockSpec((tr, C), lambda i: (i, 0)),
        out_shape=jax.ShapeDtypeStruct((R, C), f32),
        compiler_params=_cp(("parallel",)),
    )(packs)


def _allgather8(x_shard):
    m_per, n = x_shard.shape

    def body(x_ref, out_ref, send_sems, recv_sems, local_sem):
        x, y, c = lax.axis_index("x"), lax.axis_index("y"), lax.axis_index("c")
        me, sibling = (x, y, c), (x, y, 1 - c)
        chips = [(1 - x, y), (x, 1 - y), (1 - x, 1 - y)]

        def rows(px, py, pc):
            return out_ref.at[pl.ds((4 * px + 2 * py + pc) * m_per, m_per), :]

        def copy(k, block, to, src=None):
            return pltpu.make_async_remote_copy(
                src_ref=rows(*block) if src is None else src, dst_ref=rows(*block),
                send_sem=send_sems.at[k], recv_sem=recv_sems.at[k], device_id=to, device_id_type=MESH)

        mine = pltpu.make_async_copy(x_ref, rows(*me), local_sem)
        mine.start()
        first = [copy(0, me, sibling, src=x_ref)]
        first += [copy(1 + j, me, (*chip, c), src=x_ref) for j, chip in enumerate(chips)]
        for cp in first:
            cp.start()
        passed = [copy(4 + j, (*chip, c), sibling) for j, chip in enumerate(chips)]
        for j, chip in enumerate(chips):
            copy(1 + j, (*chip, c), me).wait_recv()
            passed[j].start()
        copy(0, sibling, me).wait_recv()
        for j, chip in enumerate(chips):
            copy(4 + j, (*chip, 1 - c), me).wait_recv()
        for cp in first + passed:
            cp.wait_send()
        mine.wait()

    return pl.pallas_call(
        body, name="allgather8",
        out_shape=jax.ShapeDtypeStruct((8 * m_per, n), x_shard.dtype),
        in_specs=[pl.BlockSpec(memory_space=pltpu.VMEM)],
        out_specs=pl.BlockSpec(memory_space=pltpu.VMEM),
        scratch_shapes=[pltpu.SemaphoreType.DMA((7,)), pltpu.SemaphoreType.DMA((7,)), pltpu.SemaphoreType.DMA],
        compiler_params=_cp(None, 48),
    )(x_shard)


def _other_chips():
    x, y = lax.axis_index("x"), lax.axis_index("y")
    return [(1 - x, y), (x, 1 - y), (1 - x, 1 - y)]


_HBM = pl.BlockSpec(memory_space=pltpu.HBM)
_SEM = pl.BlockSpec(memory_space=pltpu.SEMAPHORE)
_EFFECT = pltpu.SideEffectType.DATAFLOW_SIDE_EFFECTING


def _gather_copies(srcs, lands, send_sems, recv_sems):
    x, y, c = lax.axis_index("x"), lax.axis_index("y"), lax.axis_index("c")
    return [pltpu.make_async_remote_copy(
        src_ref=srcs[a].at[:, c], dst_ref=lands[a].at[2 * x + y, :, c], send_sem=send_sems.at[3 * a + j],
        recv_sem=recv_sems.at[3 * a + j], device_id=(cx, cy, c), device_id_type=MESH)
        for a in range(len(srcs)) for j, (cx, cy) in enumerate(_other_chips())]


def _gather_start(chunks, after, name):
    sizes = [len(srcs) for srcs, _ in chunks]
    flat = [t for srcs, lands in chunks for t in list(srcs) + list(lands)]
    nflat = len(flat)
    nsem = 2 * len(chunks)

    def body(*refs):
        ins, sems, token = refs[:nflat], refs[nflat + 1:nflat + 1 + nsem], refs[-1]
        off = 0
        for k, n in enumerate(sizes):
            for cp in _gather_copies(ins[off:off + n], ins[off + n:off + 2 * n], sems[2 * k], sems[2 * k + 1]):
                cp.start()
            off += 2 * n
        token[...] = jnp.zeros_like(token)

    res = pl.pallas_call(
        body, name=name,
        out_shape=[pltpu.SemaphoreType.DMA((3 * n,)) for n in sizes for _ in range(2)]
        + [pltpu.HBM(t.shape, t.dtype) for t in flat] + [jax.ShapeDtypeStruct((8, 128), f32)],
        in_specs=[_HBM] * nflat + [pl.BlockSpec(memory_space=pl.ANY)],
        out_specs=[_SEM] * nsem + [_HBM] * nflat + [pl.BlockSpec(memory_space=pltpu.VMEM)],
        input_output_aliases={i: nsem + i for i in range(nflat)},
        compiler_params=pltpu.CompilerParams(has_side_effects=_EFFECT),
    )(*[pltpu.with_memory_space_constraint(t, pltpu.HBM) for t in flat], after)
    out, off = [], nsem
    for k, n in enumerate(sizes):
        out.append((res[2 * k], res[2 * k + 1], res[off:off + n], res[off + n:off + 2 * n]))
        off += 2 * n
    return out, res[-1]


def _gather_wait(send_sems, recv_sems, srcs, lands, after, name):
    n = len(srcs)

    def body(*refs):
        for cp in _gather_copies(refs[:n], refs[n:2 * n], refs[2 * n], refs[2 * n + 1]):
            cp.wait_send()
            cp.wait_recv()

    res = pl.pallas_call(
        body, name=name,
        out_shape=[pltpu.HBM(t.shape, t.dtype) for t in list(srcs) + list(lands)],
        in_specs=[_HBM] * (2 * n) + [_SEM, _SEM] + [pl.BlockSpec(memory_space=pl.ANY)] * len(after),
        out_specs=[_HBM] * (2 * n),
        input_output_aliases={i: i for i in range(2 * n)},
        compiler_params=pltpu.CompilerParams(has_side_effects=_EFFECT),
    )(*srcs, *lands, send_sems, recv_sems, *after)
    return res[n:]


def _gather_forward(lands):
    n = len(lands)

    def body(*refs):
        outs = refs[n:2 * n]
        send_sems, recv_sems = refs[2 * n:]
        x, y, c = lax.axis_index("x"), lax.axis_index("y"), lax.axis_index("c")
        sibling = (x, y, 1 - c)
        copies = []
        for a in range(n):
            for j, (cx, cy) in enumerate(_other_chips()):
                mine = outs[a].at[2 * cx + cy, :, c]
                cp = pltpu.make_async_remote_copy(src_ref=mine, dst_ref=mine, send_sem=send_sems.at[3 * a + j],
                                                  recv_sem=recv_sems.at[3 * a + j], device_id=sibling, device_id_type=MESH)
                cp.start()
                copies.append((cp, a, j, cx, cy))
        for cp, a, j, cx, cy in copies:
            cp.wait_send()
            theirs = outs[a].at[2 * cx + cy, :, 1 - c]
            pltpu.make_async_remote_copy(src_ref=theirs, dst_ref=theirs, send_sem=send_sems.at[3 * a + j],
                                         recv_sem=recv_sems.at[3 * a + j], device_id=sibling, device_id_type=MESH).wait_recv()

    hbm = pl.BlockSpec(memory_space=pl.ANY)
    return pl.pallas_call(
        body, name="gather_forward",
        out_shape=[jax.ShapeDtypeStruct(t.shape, t.dtype) for t in lands],
        in_specs=[hbm] * n, out_specs=[hbm] * n,
        input_output_aliases={a: a for a in range(n)},
        scratch_shapes=[pltpu.SemaphoreType.DMA((3 * n,)), pltpu.SemaphoreType.DMA((3 * n,))],
    )(*lands)


def _pair_exchange(g5s):
    n = len(g5s)

    def body(*refs):
        ins, outs = refs[:n], refs[n:2 * n]
        send_sems, recv_sems = refs[2 * n:]
        c = lax.axis_index("c")
        sibling = (lax.axis_index("x"), lax.axis_index("y"), 1 - c)
        copies = []
        for a in range(n):
            cp = pltpu.make_async_remote_copy(src_ref=ins[a].at[:, :, 1 - c], dst_ref=outs[a], send_sem=send_sems.at[a],
                                              recv_sem=recv_sems.at[a], device_id=sibling, device_id_type=MESH)
            cp.start()
            copies.append(cp)
        for cp in copies:
            cp.wait()

    hbm = pl.BlockSpec(memory_space=pl.ANY)
    return pl.pallas_call(
        body, name="pair_exchange",
        out_shape=[jax.ShapeDtypeStruct(g.shape[:2] + g.shape[3:], g.dtype) for g in g5s],
        in_specs=[hbm] * n, out_specs=[hbm] * n,
        scratch_shapes=[pltpu.SemaphoreType.DMA((n,)), pltpu.SemaphoreType.DMA((n,))],
    )(*g5s)


def _scatter_copies(srcs, lands, send_sems, recv_sems):
    c = lax.axis_index("c")
    return [pltpu.make_async_remote_copy(
        src_ref=srcs[a].at[2 * cx + cy], dst_ref=lands[a].at[j], send_sem=send_sems.at[3 * a + j],
        recv_sem=recv_sems.at[3 * a + j], device_id=(cx, cy, c), device_id_type=MESH)
        for a in range(len(srcs)) for j, (cx, cy) in enumerate(_other_chips())]


def _scatter_start(hsums, name, after=()):
    n = len(hsums)
    na = len(after)

    def body(*refs):
        srcs, lands = refs[:n], refs[n:2 * n]
        send_sems, recv_sems = refs[2 * n + na], refs[2 * n + na + 1]
        for cp in _scatter_copies(srcs, lands, send_sems, recv_sems):
            cp.start()
        refs[-1][...] = jnp.zeros_like(refs[-1])

    lands = [lax.empty((3,) + g.shape[1:], g.dtype) for g in hsums]
    res = pl.pallas_call(
        body, name=name,
        out_shape=[pltpu.SemaphoreType.DMA((3 * n,)), pltpu.SemaphoreType.DMA((3 * n,))]
        + [pltpu.HBM(g.shape, g.dtype) for g in hsums] + [pltpu.HBM(g.shape, g.dtype) for g in lands]
        + [jax.ShapeDtypeStruct((8, 128), f32)],
        in_specs=[_HBM] * (2 * n) + [pl.BlockSpec(memory_space=pl.ANY)] * na,
        out_specs=[_SEM, _SEM] + [_HBM] * (2 * n) + [pl.BlockSpec(memory_space=pltpu.VMEM)],
        input_output_aliases={i: i + 2 for i in range(2 * n)},
        compiler_params=pltpu.CompilerParams(has_side_effects=_EFFECT),
    )(*[pltpu.with_memory_space_constraint(t, pltpu.HBM) for t in list(hsums) + lands], *after)
    return (res[0], res[1], res[2:2 + n], res[2 + n:2 + 2 * n]), res[-1]


def _scatter_wait(send_sems, recv_sems, srcs, lands, after, name):
    n = len(srcs)
    extra = list(after)

    def body(*refs):
        s_refs, l_refs = refs[:n], refs[n:2 * n]
        ss, rs = refs[2 * n], refs[2 * n + 1]
        for cp in _scatter_copies(s_refs, l_refs, ss, rs):
            cp.wait_send()
            cp.wait_recv()

    res = pl.pallas_call(
        body, name=name,
        out_shape=[pltpu.HBM(g.shape, g.dtype) for g in srcs] + [pltpu.HBM(g.shape, g.dtype) for g in lands],
        in_specs=[_HBM] * (2 * n) + [_SEM, _SEM] + [pl.BlockSpec(memory_space=pl.ANY)] * len(extra),
        out_specs=[_HBM] * (2 * n),
        input_output_aliases={i: i for i in range(2 * n)},
        compiler_params=pltpu.CompilerParams(has_side_effects=_EFFECT),
    )(*srcs, *lands, send_sems, recv_sems, *extra)
    return res[:n], res[n:]


def _plane_copies(src, land, send_sems, recv_sems):
    x, y, c = lax.axis_index("x"), lax.axis_index("y"), lax.axis_index("c")
    return [pltpu.make_async_remote_copy(src_ref=src, dst_ref=land.at[2 * x + y, c], send_sem=send_sems.at[j],
                                         recv_sem=recv_sems.at[j], device_id=(cx, cy, c), device_id_type=MESH)
            for j, (cx, cy) in enumerate(_other_chips())]


def _plane_start(pack, land, name):
    def body(src, lnd, send_sems, recv_sems, _s, _l, token):
        for cp in _plane_copies(src, lnd, send_sems, recv_sems):
            cp.start()
        token[...] = jnp.zeros_like(token)

    res = pl.pallas_call(
        body, name=name,
        out_shape=[pltpu.SemaphoreType.DMA((3,)), pltpu.SemaphoreType.DMA((3,)), pltpu.HBM(pack.shape, pack.dtype),
                   pltpu.HBM(land.shape, land.dtype), jax.ShapeDtypeStruct((8, 128), f32)],
        in_specs=[_HBM, _HBM], out_specs=[_SEM, _SEM, _HBM, _HBM, pl.BlockSpec(memory_space=pltpu.VMEM)],
        input_output_aliases={0: 2, 1: 3},
        compiler_params=pltpu.CompilerParams(has_side_effects=_EFFECT),
    )(pltpu.with_memory_space_constraint(pack, pltpu.HBM), pltpu.with_memory_space_constraint(land, pltpu.HBM))
    return res[:4], res[4]


def _plane_wait(send_sems, recv_sems, pack, land, after, name):
    def body(src, lnd, ss, rs, *_):
        for cp in _plane_copies(src, lnd, ss, rs):
            cp.wait_send()
            cp.wait_recv()

    return pl.pallas_call(
        body, name=name,
        out_shape=[pltpu.HBM(pack.shape, pack.dtype), pltpu.HBM(land.shape, land.dtype)],
        in_specs=[_HBM, _HBM, _SEM, _SEM] + [pl.BlockSpec(memory_space=pl.ANY)] * len(after),
        out_specs=[_HBM, _HBM], input_output_aliases={0: 0, 1: 1},
        compiler_params=pltpu.CompilerParams(has_side_effects=_EFFECT),
    )(pack, land, send_sems, recv_sems, *after)[1]


def _swap_halves(fulls):
    n = len(fulls)

    def body(*refs):
        ins, outs = refs[:n], refs[n:2 * n]
        send_sems, recv_sems = refs[2 * n:]
        c = lax.axis_index("c")
        sibling = (lax.axis_index("x"), lax.axis_index("y"), 1 - c)
        copies = []
        for a in range(n):
            cp = pltpu.make_async_remote_copy(src_ref=outs[a].at[:, c], dst_ref=outs[a].at[:, c], send_sem=send_sems.at[a],
                                              recv_sem=recv_sems.at[a], device_id=sibling, device_id_type=MESH)
            cp.start()
            copies.append(cp)
        for a, cp in enumerate(copies):
            cp.wait_send()
            theirs = outs[a].at[:, 1 - c]
            pltpu.make_async_remote_copy(src_ref=theirs, dst_ref=theirs, send_sem=send_sems.at[a], recv_sem=recv_sems.at[a],
                                         device_id=sibling, device_id_type=MESH).wait_recv()

    hbm = pl.BlockSpec(memory_space=pl.ANY)
    return pl.pallas_call(
        body, name="swap_halves",
        out_shape=[jax.ShapeDtypeStruct(p.shape, p.dtype) for p in fulls],
        in_specs=[hbm] * n, out_specs=[hbm] * n,
        input_output_aliases={a: a for a in range(n)},
        scratch_shapes=[pltpu.SemaphoreType.DMA((n,)), pltpu.SemaphoreType.DMA((n,))],
    )(*fulls)


def _to_segments(t):
    s, c = t.shape
    return t.reshape(SCAN_SEG, s // SCAN_SEG, c).transpose(1, 0, 2).reshape(s, c)


def _from_segments(t):
    s, c = t.shape
    return t.reshape(s // SCAN_SEG, SCAN_SEG, c).transpose(1, 0, 2).reshape(s, c)


def _ssm_operators(a_re, a_im, log_dt, b_re, b_im, c_re, c_im):
    lam = lax.complex(a_re, a_im)
    dt = jnp.exp(log_dt)[:, None]
    a_bar = jnp.exp(lam * dt)
    b_bar = ((a_bar - 1.0) / lam)[:, :, None] * lax.complex(b_re, b_im)
    eye = jnp.eye(N_GROUPS, dtype=f32)

    def embed_b(t):
        return (jnp.transpose(t, (0, 2, 1))[:, :, None, :] * eye[:, None, :, None]).reshape(D_SSM, D_STATE)

    def embed_c(t):
        return (jnp.transpose(t, (0, 2, 1))[:, :, None, :] * eye[:, None, :, None]).reshape(D_STATE, D_SSM)

    a2 = jnp.stack([a_bar.real.reshape(D_STATE), a_bar.imag.reshape(D_STATE)])
    return a2, embed_b(b_bar.real), embed_b(b_bar.imag), embed_c(c_re), embed_c(c_im)


def _local_step(x, target, mod, small, ffn_weights, mix_weights, grads_done):
    table = jnp.asarray(_bucket_table())
    bias = _bias_fwd(small["rel_bias"], table)
    L = DEPTH
    saved = []
    ssm_ops = []
    for l in range(L):
        sv = {}
        m9 = mod[l]
        sv["x0"] = x
        sv["w0"] = ffn_weights(l, 0, x)
        x, sv["f0"], sv["g0"], sv["u0"], sv["h0"] = _ffn_fwd(x, m9[0:3], *sv["w0"], 0, small["ln_g"][l, 0:1], small["ln_b"][l, 0:1])
        sv["x1"] = x
        sv["w1"] = mix_weights(l, x)
        *qkv, z_rest, sv["h1"] = _mix_in_fwd(x, m9[3:6], sv["w1"][0], 0)
        S = x.shape[0]
        qkv = [t.reshape(3, S, D_ATT) for t in qkv]
        att = [_att_fwd(qkv[b], bias, b) for b in range(3)]
        y_att, lse3 = _att_merge([att[b][0].reshape(d, S // d, D_ATT) for b, d in enumerate(DILATIONS)],
                                 [att[b][1].reshape(d, S // d, _LANES) for b, d in enumerate(DILATIONS)])
        sv.update(qkv=qkv, lse=[a[1] for a in att], lse3=lse3, y_att=y_att)

        prm = tuple(small[k][l] for k in ("ssm_a_re", "ssm_a_im", "ssm_log_dt", "ssm_b_re", "ssm_b_im", "ssm_c_re", "ssm_c_im"))
        (a2, bre, bim, cre, cim), ops_vjp = jax.vjp(_ssm_operators, *prm)
        ssm_ops.append(ops_vjp)
        u_ssm = _to_segments(z_rest[:, :D_SSM])
        sr, si = _ssm_states(u_ssm, bre, bim, a2)
        dskip = small["ssm_d"][l][None, :]
        glu_b = small["glu_b"][l][None, :]
        out_seg, y_seg = _ssm_out(sr, si, u_ssm, cre, cim, dskip, small["glu_w"][l], glu_b)
        y_ssm = _from_segments(out_seg)
        sv.update(a2=a2, bre=bre, bim=bim, cre=cre, cim=cim, u_ssm=u_ssm, sr=sr, si=si, y_seg=y_seg, y_ssm=y_ssm)

        u_pool = jnp.concatenate([jnp.zeros((POOL_HALO, D_POOL), f32), z_rest[:, D_SSM:]])
        y_pool = _pool_fwd(u_pool, small["pool_w"][l], small["pool_scale"][l][None, :])
        sv.update(u_pool=u_pool, y_pool=y_pool)

        x, sv["ymix"] = _mix_out_fwd(x, y_att, y_ssm, y_pool, m9[3:6], sv["w1"][1], 0, small["ln_g"][l, 1:2], small["ln_b"][l, 1:2])
        sv["x2"] = x
        sv["w2"] = ffn_weights(l, 1, x)
        x, sv["f2"], sv["g2"], sv["u2"], sv["h2"] = _ffn_fwd(x, m9[6:9], *sv["w2"], 0, small["ln_g"][l, 2:3], small["ln_b"][l, 2:3])
        saved.append(sv)

    loss, dx = _loss_head(x, target)

    dmod = [None] * L
    dln_g = [None] * L
    dln_b = [None] * L
    sg = {k: [None] * L for k in ("ssm_a_re", "ssm_a_im", "ssm_log_dt", "ssm_b_re", "ssm_b_im", "ssm_c_re", "ssm_c_im",
                                  "ssm_d", "glu_w", "glu_b", "pool_w", "pool_scale")}
    dbias_tot = None
    order_after = jnp.zeros((), f32)
    for l in reversed(range(L)):
        sv = saved[l]
        m9 = mod[l] + order_after

        def fresh(like):
            return [lax.empty(t.shape, bf16) for t in like]

        dx, dg, du, a, df, dm2, dlg2, dlb2 = _ffn_bwd(dx, sv["x2"], sv["f2"], sv["g2"], sv["u2"], m9[6:9], *sv["w2"], 0,
                                                     small["ln_g"][l, 2:3])
        g_ffn1 = _ffn_wgrad(sv["h2"], dg, du, a, df, *fresh(sv["w2"]), 0)
        dxr, d_att, d_ssm, d_pool, dgate1, dlg1, dlb1, g_w_out = _mix_out_bwd(
            dx, sv["x1"], sv["ymix"], sv["y_att"], sv["y_ssm"], sv["y_pool"], m9[3:6], sv["w1"][1], 0, small["ln_g"][l, 1:2],
            fresh(sv["w1"])[1])
        S = d_att.shape[0]
        merged = _att_merge_bwd(d_att, sv["y_att"], sv["lse3"])
        dqkv, dbias = [], []
        for b, d in enumerate(DILATIONS):
            dq_b, db_b = _att_bwd(sv["qkv"][b], merged[b].reshape(S, D_ATT), sv["lse"][b], merged[3 + b].reshape(S, _LANES), bias, b)
            dqkv.append(dq_b.reshape(3, d, S // d, D_ATT))
            dbias.append(db_b)
        dbias = jnp.stack(dbias)
        dbias_tot = dbias if dbias_tot is None else dbias_tot + dbias
        d_seg = _to_segments(d_ssm)
        dskip = small["ssm_d"][l][None, :]
        glu_b = small["glu_b"][l][None, :]
        dy_seg, du_skip, dcre, dcim, dd, dglu_b, dglu_w = _ssm_out_bwd(
            d_seg, sv["y_seg"], sv["u_ssm"], sv["sr"], sv["si"], dskip, small["glu_w"][l], glu_b)
        du_seg, dbre, dbim, da2 = _ssm_states_bwd(dy_seg, du_skip, sv["u_ssm"], sv["sr"], sv["si"], sv["cre"], sv["cim"],
                                                  sv["bre"], sv["bim"], sv["a2"])
        d_prm = ssm_ops[l]((da2, dbre, dbim, dcre, dcim))
        for k, v in zip(("ssm_a_re", "ssm_a_im", "ssm_log_dt", "ssm_b_re", "ssm_b_im", "ssm_c_re", "ssm_c_im"), d_prm):
            sg[k][l] = v
        sg["ssm_d"][l] = dd[0]
        sg["glu_b"][l] = dglu_b[0]
        sg["glu_w"][l] = dglu_w
        du_ssm = _from_segments(du_seg)
        dyp = jnp.concatenate([d_pool, jnp.zeros((POOL_HALO, D_POOL), f32)])
        du_pool, dpw, dps = _pool_bwd(dyp, sv["u_pool"], small["pool_w"][l], small["pool_scale"][l][None, :])
        sg["pool_w"][l] = dpw
        sg["pool_scale"][l] = dps[0]
        d_rest = jnp.concatenate([du_ssm, du_pool], axis=1).astype(bf16)
        dx, dm1, dz = _mix_in_bwd(dqkv, d_rest, dxr, sv["x1"], m9[3:6], sv["w1"][0], 0)
        g_w_in = _mix_in_wgrad(sv["h1"], dz, fresh(sv["w1"])[0], 0)
        ffn_names = ("ffn_w_gate", "ffn_w_up", "ffn_w_down")
        m9 = m9 + grads_done(l, 1, list(zip(ffn_names, [(2 * l + 1) * FF_SHARD] * 3, g_ffn1))
                             + [("w_in", l * D_MODEL, g_w_in), ("w_out", l * 256, g_w_out)])
        dm1 = jnp.concatenate([dm1[0:2], dgate1])
        dx, dg, du, a, df, dm0, dlg0, dlb0 = _ffn_bwd(dx, sv["x0"], sv["f0"], sv["g0"], sv["u0"], m9[0:3], *sv["w0"], 0,
                                                     small["ln_g"][l, 0:1])
        g_ffn0 = _ffn_wgrad(sv["h0"], dg, du, a, df, *fresh(sv["w0"]), 0)
        order_after = grads_done(l, 0, list(zip(ffn_names, [2 * l * FF_SHARD] * 3, g_ffn0)))
        dmod[l] = jnp.concatenate([dm0, dm1, dm2])
        dln_g[l] = jnp.concatenate([dlg0, dlg1, dlg2])
        dln_b[l] = jnp.concatenate([dlb0, dlb1, dlb2])

    small_grads = {k: jnp.stack(v) for k, v in sg.items()}
    small_grads["rel_bias"] = _bias_bwd(dbias_tot, table)
    small_grads["ln_g"] = jnp.stack(dln_g)
    small_grads["ln_b"] = jnp.stack(dln_b)
    return loss, dx, jnp.stack(dmod), small_grads


_TILE_ELEMS = 8 * 128


def _pack_rows(shapes):
    out, row = [], 0
    for s in shapes:
        nr = -(-int(np.prod(s)) // _TILE_ELEMS) * 8
        out.append((row, nr))
        row += nr
    return out


def _pack(arrs):
    parts = []
    for a in arrs:
        flat = a.reshape(-1).astype(f32)
        npad = -(-flat.shape[0] // _TILE_ELEMS) * _TILE_ELEMS
        parts.append(jnp.pad(flat, (0, npad - flat.shape[0])).reshape(npad // 128, 128))
    return jnp.concatenate(parts, axis=0)


def _unpack(buf, shapes):
    return [buf[row:row + nr].reshape(-1)[:int(np.prod(s))].reshape(s) for s, (row, nr) in zip(shapes, _pack_rows(shapes))]


_REPL = ("rel_bias", "ada_b", "ssm_a_re", "ssm_a_im", "ssm_log_dt", "ssm_b_re", "ssm_b_im", "ssm_c_re", "ssm_c_im",
         "ssm_d", "glu_b", "pool_w", "pool_scale")
_SMALL_SHARDED = ("ln_g", "ln_b", "glu_w")
_BIG = ("ffn_w_gate", "ffn_w_up", "ffn_w_down", "w_in", "w_out")
_ORDER = ("rel_bias", "ada_w", "ada_b", "ln_g", "ln_b", "ffn_w_gate", "ffn_w_up", "ffn_w_down", "w_in", "w_out",
          "ssm_a_re", "ssm_a_im", "ssm_log_dt", "ssm_b_re", "ssm_b_im", "ssm_c_re", "ssm_c_im", "ssm_d", "glu_w",
          "glu_b", "pool_w", "pool_scale")


def kernel(x, c, rel_bias, ada_w, ada_b, ln_g, ln_b, ffn_w_gate, ffn_w_up, ffn_w_down, w_in, w_out, ssm_a_re, ssm_a_im, ssm_log_dt, ssm_b_re, ssm_b_im, ssm_c_re, ssm_c_im, ssm_d, glu_w, glu_b, pool_w, pool_scale, loss_target, m_rel_bias, m_ada_w, m_ada_b, m_ln_g, m_ln_b, m_ffn_w_gate, m_ffn_w_up, m_ffn_w_down, m_w_in, m_w_out, m_ssm_a_re, m_ssm_a_im, m_ssm_log_dt, m_ssm_b_re, m_ssm_b_im, m_ssm_c_re, m_ssm_c_im, m_ssm_d, m_glu_w, m_glu_b, m_pool_w, m_pool_scale, v_rel_bias, v_ada_w, v_ada_b, v_ln_g, v_ln_b, v_ffn_w_gate, v_ffn_w_up, v_ffn_w_down, v_w_in, v_w_out, v_ssm_a_re, v_ssm_a_im, v_ssm_log_dt, v_ssm_b_re, v_ssm_b_im, v_ssm_c_re, v_ssm_c_im, v_ssm_d, v_glu_w, v_glu_b, v_pool_w, v_pool_scale):
    args = dict(locals())
    w = {k: args[k] for k in _ORDER}
    m = {k: args["m_" + k] for k in _ORDER}
    v = {k: args["v_" + k] for k in _ORDER}
    L, D = DEPTH, D_MODEL
    ax, ay, ac = lax.axis_index("x"), lax.axis_index("y"), lax.axis_index("c")
    p_me = 2 * ax + ay
    dev = 4 * ax + 2 * ay + ac

    transposed = ("ffn_w_gate", "ffn_w_up")
    for d in (w, m, v):
        for name in transposed:
            d[name] = jnp.swapaxes(d[name], 2, 3)

    def halves(t):
        return t.astype(bf16).reshape(1, 2, t.shape[0] // 2, t.shape[1])

    def landing(src):
        return lax.dynamic_update_slice(lax.empty((N_CHIPS,) + src.shape, bf16), src[None], (p_me, 0, 0, 0, 0))

    chunk_keys = [("ffn", 0, 0), ("mix", 0), ("ffn", 0, 1), ("ffn", 1, 0), ("mix", 1), ("ffn", 1, 1)]
    chunk_srcs = []
    for key in chunk_keys:
        if key[0] == "ffn":
            chunk_srcs.append([halves(w[name][key[1], key[2]]) for name in ("ffn_w_gate", "ffn_w_up", "ffn_w_down")])
        else:
            chunk_srcs.append([halves(w_in[key[1]]), halves(w_out[key[1]])])

    pack = _pack([c, ln_g, ln_b, glu_w])
    rows = pack.shape[0]
    allp = _allgather8(pack).reshape(8, rows, 128)
    chunks = [(srcs, [landing(t) for t in srcs]) for srcs in chunk_srcs]
    first_in_flight, first_begun = _gather_start(chunks[:1], allp, "gather_start_first")
    c_all = allp[:, :8].reshape(8, D) + first_begun[0, 0]
    by_chip = allp[0::2]

    fwd_rows = _pack_rows([c.shape, ln_g.shape, ln_b.shape, glu_w.shape])

    def sharded(part, shape, axis):
        row0, nrows = fwd_rows[part]
        t = by_chip[:, row0:row0 + nrows].reshape(N_CHIPS, -1)[:, :int(np.prod(shape))].reshape((N_CHIPS,) + shape)
        return jnp.concatenate([t[p] for p in range(N_CHIPS)], axis=axis)

    ln_g_full = sharded(1, ln_g.shape, 2)
    ln_b_full = sharded(2, ln_b.shape, 2)
    glu_w_full = sharded(3, glu_w.shape, 1)

    ncol = ada_w.shape[-1]
    ada_b_cols = lax.dynamic_slice_in_dim(ada_b, p_me * ncol, ncol, axis=1)[:, None, :]
    mod_part = _ada_fwd(c_all, ada_w, ada_b_cols)
    mrows = L * 8 * ncol // 128
    mod_all = _allgather8(mod_part.reshape(mrows, 128)).reshape(8, L, 8, ncol)
    mod_mine = lax.dynamic_index_in_dim(mod_all, dev, axis=2, keepdims=False)
    mod = jnp.concatenate([mod_mine[2 * p] for p in range(N_CHIPS)], axis=-1).reshape(L, 9, D)

    rest_in_flight, rest_begun = _gather_start(chunks[1:], mod, "gather_start_rest")
    in_flight = first_in_flight + rest_in_flight

    def gathered(key, after):
        k = chunk_keys.index(key)
        lands = _gather_forward(_gather_wait(*in_flight[k], [after, rest_begun], "gather_wait_%d" % k))
        return [t.reshape(N_CHIPS, 1, 2 * t.shape[3], t.shape[4]) for t in lands]

    pc = jnp.stack([p_me, ac]).astype(jnp.int32)
    groups = {}
    scattering = {}

    def start_group(tag, after=()):
        g5 = [g.reshape(g.shape[:2] + (2, g.shape[2] // 2, g.shape[3])) for _, _, g in groups[tag]]
        hsum = _pair_sum(g5, _pair_exchange(g5), pc)
        scattering[tag], begun = _scatter_start(hsum, "scatter_start_%s" % tag, after)
        return begun

    def grads_done(l, s, grads):
        if l == 1:
            groups.setdefault("l1", []).extend(grads)
            return start_group("l1")[0, 0] if s == 0 else jnp.zeros((), f32)
        groups["l0a" if s == 1 else "l0b"] = grads
        return start_group("l0a")[0, 0] if s == 1 else jnp.zeros((), f32)

    small = {k: w[k] for k in _REPL if k != "ada_b"}
    small.update(ln_g=ln_g_full, ln_b=ln_b_full, glu_w=glu_w_full)
    loss_dev, grad_x, dmod, sgrads = _local_step(
        x[0], loss_target[0], mod, small, lambda l, s, after: gathered(("ffn", l, s), after),
        lambda l, after: gathered(("mix", l), after), grads_done)
    loss = lax.psum(loss_dev[0, 0], ("x", "y", "c"))

    names = ("rel_bias", "ln_g", "ln_b", "ssm_a_re", "ssm_a_im", "ssm_log_dt", "ssm_b_re", "ssm_b_im", "ssm_c_re",
             "ssm_c_im", "ssm_d", "glu_w", "glu_b", "pool_w", "pool_scale")
    gpack = _pack([dmod] + [sgrads[k] for k in names])
    grows = gpack.shape[0]
    land = lax.dynamic_update_slice(lax.empty((N_CHIPS, 2, grows, 128), f32), gpack[None, None], (p_me, ac, 0, 0))
    small_in_flight, small_begun = _plane_start(gpack, land, "small_grads_start")
    l0b_begun = start_group("l0b", (small_begun,))

    out_g, out_d, out_m, out_v = {}, {}, {}, {}
    row_tile = dict(zip(_BIG, (352, 352, 352, 256, 256)))
    big = {name: None for name in _BIG}

    def finish_group(tag, after):
        hsum, recv = _scatter_wait(*scattering[tag], after, "scatter_wait_%s" % tag)
        full = _swap_halves(_sum_shards(hsum, recv, pc))
        for (name, row0, _), g in zip(groups[tag], full):
            shp = w[name].shape
            r2 = (int(np.prod(shp[:-1])), shp[-1])
            big[name] = _adamw(w[name].reshape(r2), m[name].reshape(r2), v[name].reshape(r2), g.reshape(-1, shp[-1]),
                               row_tile[name], row0, big[name])
        return [big[name][1] for name, _, _ in groups[tag]]

    after = finish_group("l1", [grad_x, l0b_begun])

    land = _plane_wait(*small_in_flight, after, "small_grads_wait")
    gall = _swap_halves([land])[0].reshape(8, grows, 128)
    gsum = _unpack(_sum8(gall), [(L, 9 * D)] + [sgrads[k].shape for k in names])
    red = dict(zip(("ada_b",) + names, gsum))
    red["ln_g"] = lax.dynamic_slice_in_dim(red["ln_g"], p_me * 256, 256, axis=2)
    red["ln_b"] = lax.dynamic_slice_in_dim(red["ln_b"], p_me * 256, 256, axis=2)
    red["glu_w"] = lax.dynamic_slice_in_dim(red["glu_w"], p_me * 64, 64, axis=1)

    dmod_all = gall[:, :L * 9 * D // 128].reshape(8, L, 9 * D)
    dmod_cols = jnp.transpose(lax.dynamic_slice_in_dim(dmod_all, p_me * ncol, ncol, axis=2), (1, 0, 2))
    g_ada_w = _ada_wgrad(jnp.transpose(c_all), dmod_cols)

    r2 = (L * D, ncol)
    res = _adamw(ada_w.reshape(r2), m["ada_w"].reshape(r2), v["ada_w"].reshape(r2), g_ada_w.reshape(r2), 128)
    out_g["ada_w"], out_d["ada_w"], out_m["ada_w"], out_v["ada_w"] = [t.reshape(ada_w.shape) for t in res]

    small_names = _REPL + _SMALL_SHARDED
    wp = _pack([w[k] for k in small_names])
    res_small = _adamw(wp, _pack([m[k] for k in small_names]), _pack([v[k] for k in small_names]),
                       _pack([red[k] for k in small_names]), wp.shape[0])
    for t, dst in zip(res_small, (out_g, out_d, out_m, out_v)):
        for k, a in zip(small_names, _unpack(t, [w[k].shape for k in small_names])):
            dst[k] = a

    finish_group("l0b", finish_group("l0a", [res_small[1], res[1]]))
    for name in _BIG:
        res = [t.reshape(w[name].shape) for t in big[name]]
        out_g[name], out_d[name], out_m[name], out_v[name] = [jnp.swapaxes(t, 2, 3) for t in res] if name in transposed else res

    return (loss, grad_x[None], *[out_g[k] for k in _ORDER], *[out_d[k] for k in _ORDER],
            *[out_m[k] for k in _ORDER], *[out_v[k] for k in _ORDER])
```

```python
import functools
import math

import numpy as np
import jax
import jax.numpy as jnp
from jax import lax
from jax.experimental import pallas as pl
from jax.experimental.pallas import tpu as pltpu

f32 = jnp.float32
bf16 = jnp.bfloat16
MESH = pl.DeviceIdType.MESH

D_MODEL = 1024
SEQ = 2048
DEPTH = 2
HEAD_DIM = 64
N_HEADS = 8
D_ATT = 512
DILATIONS = (1, 4, 16)
BLOCKS_PER_RESIDUE = (16, 4, 1)
ATT_BLOCK = 128
N_UNITS = SEQ // ATT_BLOCK
N_GROUPS = 16
SSM_GROUP = 16
SSM_STATE = 64
D_SSM = 256
D_STATE = N_GROUPS * SSM_STATE
POOL_WINDOWS = (2, 4, 8, 16)
POOL_GROUP = 64
D_POOL = 256
POOL_HALO = 16
D_FF = 2816
N_BUCKETS = 32
MAX_DISTANCE = 2048
ALPHA = (2 * DEPTH) ** 0.25
FFN_RES = 0.5
LN_EPS = 1e-5
NEG = -1e30
N_CHIPS = 4
FF_SHARD = D_FF // N_CHIPS
SCAN_SEG = 8
SCAN_STEPS = SEQ // SCAN_SEG

ADAM_LR, ADAM_B1, ADAM_B2, ADAM_EPS, ADAM_WD, ADAM_STEP = 0.001, 0.9, 0.999, 1e-08, 0.01, 10

TOK_TILE = 512


def _cp(dims=None, vmem_mb=None):
    kw = {}
    if dims is not None:
        kw["dimension_semantics"] = dims
    if vmem_mb is not None:
        kw["vmem_limit_bytes"] = vmem_mb << 20
    return pltpu.CompilerParams(**kw)


def _dot(a, b):
    return jnp.dot(a, b, preferred_element_type=f32)


def _dot_nt(a, b):
    return lax.dot_general(a, b, (((1,), (1,)), ((), ())), preferred_element_type=f32)


def _dot_tn(a, b):
    return lax.dot_general(a, b, (((0,), (0,)), ((), ())), preferred_element_type=f32)


def _ln_stats(v):
    mu = jnp.mean(v, -1, keepdims=True)
    d = v - mu
    var = jnp.mean(d * d, -1, keepdims=True)
    rstd = lax.rsqrt(var + LN_EPS)
    return d * rstd, rstd


def _ln_bwd(dxh, xh, rstd):
    return rstd * (dxh - jnp.mean(dxh, -1, keepdims=True) - xh * jnp.mean(dxh * xh, -1, keepdims=True))


_GELU_C = math.sqrt(2.0 / math.pi)


def _gelu(y):
    return 0.5 * y * (1.0 + jnp.tanh(_GELU_C * (y + 0.044715 * y * y * y)))


def _gelu_grad(y):
    t = jnp.tanh(_GELU_C * (y + 0.044715 * y * y * y))
    return 0.5 * (1.0 + t) + 0.5 * y * (1.0 - t * t) * (_GELU_C * (1.0 + 3 * 0.044715 * y * y))


def _full(shape):
    return pl.BlockSpec(shape, lambda *_: (0,) * len(shape))


def _hbm(*args):
    return [pltpu.with_memory_space_constraint(a, pltpu.HBM) if getattr(a, "ndim", 0) >= 2 else a for a in args]


def _ffn_fwd(x, mod3, wg, wu, wd, ls, lng, lnb):
    S, D = x.shape
    Fs = wg.shape[-2]
    ts = TOK_TILE

    def body(x_ref, mod_ref, wg_ref, wu_ref, wd_ref, lng_ref, lnb_ref, xo_ref, f_ref, g_ref, u_ref, h_ref, acc_sc):
        j = pl.program_id(1)

        @pl.when(j == 0)
        def _():
            xh, _ = _ln_stats(x_ref[...])
            h_ref[...] = (xh * (1.0 + mod_ref[1:2, :]) + mod_ref[0:1, :]).astype(bf16)
            acc_sc[...] = jnp.zeros_like(acc_sc)

        h = h_ref[...]
        g = _dot_nt(h, wg_ref[0, 0])
        u = _dot_nt(h, wu_ref[0, 0])
        g_ref[0] = g.astype(bf16)
        u_ref[0] = u.astype(bf16)
        a = (g * jax.nn.sigmoid(g) * u).astype(bf16)
        acc_sc[...] += _dot(a, wd_ref[0, 0])

        @pl.when(j == N_CHIPS - 1)
        def _():
            f = acc_sc[...]
            f_ref[...] = f
            r = ALPHA * x_ref[...] + (FFN_RES * mod_ref[2:3, :]) * f
            rh, _ = _ln_stats(r)
            xo_ref[...] = rh * lng_ref[...] + lnb_ref[...]

    tok = pl.BlockSpec((ts, D), lambda i, j: (i, 0))
    wrow = pl.BlockSpec((1, 1, Fs, D), lambda i, j: (j, ls, 0, 0))
    hid = pl.BlockSpec((1, ts, Fs), lambda i, j: (j, i, 0))
    return pl.pallas_call(
        body, name="ffn_fwd", grid=(S // ts, N_CHIPS),
        in_specs=[tok, _full((3, D)), wrow, wrow, wrow, _full((1, D)), _full((1, D))],
        out_specs=[tok, tok, hid, hid, tok],
        out_shape=[jax.ShapeDtypeStruct((S, D), f32), jax.ShapeDtypeStruct((S, D), f32),
                   jax.ShapeDtypeStruct((N_CHIPS, S, Fs), bf16), jax.ShapeDtypeStruct((N_CHIPS, S, Fs), bf16),
                   jax.ShapeDtypeStruct((S, D), bf16)],
        scratch_shapes=[pltpu.VMEM((ts, D), f32)],
        compiler_params=_cp(("parallel", "arbitrary"), 56),
    )(*_hbm(x, mod3, wg, wu, wd, lng, lnb))


def _ffn_bwd(dxo, x, f, g, u, mod3, wg, wu, wd, ls, lng):
    S, D = x.shape
    Fs = wg.shape[-2]
    ts = TOK_TILE

    def body(dxo_ref, x_ref, f_ref, g_ref, u_ref, mod_ref, wg_ref, wu_ref, wd_ref, lng_ref,
             dx_ref, dg_ref, du_ref, a_ref, df_ref, dmod_ref, dlng_ref, dlnb_ref,
             dr_sc, df_sc, acc_sc):
        i = pl.program_id(0)
        j = pl.program_id(1)

        @pl.when((i == 0) & (j == 0))
        def _():
            dmod_ref[...] = jnp.zeros_like(dmod_ref)
            dlng_ref[...] = jnp.zeros_like(dlng_ref)
            dlnb_ref[...] = jnp.zeros_like(dlnb_ref)

        @pl.when(j == 0)
        def _():
            xv = x_ref[...]
            fv = f_ref[...]
            gate = mod_ref[2:3, :]
            rh, rstd = _ln_stats(ALPHA * xv + (FFN_RES * gate) * fv)
            dy = dxo_ref[...]
            dlng_ref[...] += jnp.sum(dy * rh, 0, keepdims=True)
            dlnb_ref[...] += jnp.sum(dy, 0, keepdims=True)
            dr = _ln_bwd(dy * lng_ref[...], rh, rstd)
            dr_sc[...] = dr
            dmod_ref[2:3, :] += jnp.sum(FFN_RES * dr * fv, 0, keepdims=True)
            df = ((FFN_RES * gate) * dr).astype(bf16)
            df_sc[...] = df
            df_ref[...] = df
            acc_sc[...] = jnp.zeros_like(acc_sc)

        da = _dot_nt(df_sc[...], wd_ref[0, 0])
        gv = g_ref[0].astype(f32)
        uv = u_ref[0].astype(f32)
        sg = jax.nn.sigmoid(gv)
        si = gv * sg
        a_ref[0] = (si * uv).astype(bf16)
        dgv = (da * uv * (sg * (1.0 + gv * (1.0 - sg)))).astype(bf16)
        duv = (da * si).astype(bf16)
        dg_ref[0] = dgv
        du_ref[0] = duv
        acc_sc[...] += _dot(dgv, wg_ref[0, 0]) + _dot(duv, wu_ref[0, 0])

        @pl.when(j == N_CHIPS - 1)
        def _():
            dh = acc_sc[...]
            xh, rstd0 = _ln_stats(x_ref[...])
            dmod_ref[0:1, :] += jnp.sum(dh, 0, keepdims=True)
            dmod_ref[1:2, :] += jnp.sum(dh * xh, 0, keepdims=True)
            dx_ref[...] = _ln_bwd(dh * (1.0 + mod_ref[1:2, :]), xh, rstd0) + ALPHA * dr_sc[...]

    tok = pl.BlockSpec((ts, D), lambda i, j: (i, 0))
    wrow = pl.BlockSpec((1, 1, Fs, D), lambda i, j: (j, ls, 0, 0))
    hid = pl.BlockSpec((1, ts, Fs), lambda i, j: (j, i, 0))
    hid_shape = jax.ShapeDtypeStruct((N_CHIPS, S, Fs), bf16)
    return pl.pallas_call(
        body, name="ffn_bwd", grid=(S // ts, N_CHIPS),
        in_specs=[tok, tok, tok, hid, hid, _full((3, D)), wrow, wrow, wrow, _full((1, D))],
        out_specs=[tok, hid, hid, hid, tok, _full((3, D)), _full((1, D)), _full((1, D))],
        out_shape=[jax.ShapeDtypeStruct((S, D), f32), hid_shape, hid_shape, hid_shape,
                   jax.ShapeDtypeStruct((S, D), bf16),
                   jax.ShapeDtypeStruct((3, D), f32), jax.ShapeDtypeStruct((1, D), f32), jax.ShapeDtypeStruct((1, D), f32)],
        scratch_shapes=[pltpu.VMEM((ts, D), f32), pltpu.VMEM((ts, D), bf16), pltpu.VMEM((ts, D), f32)],
        compiler_params=_cp(("arbitrary", "arbitrary"), 56),
    )(*_hbm(dxo, x, f, g, u, mod3, wg, wu, wd, lng))


def _ffn_wgrad(h, dg, du, a, df, gwg, gwu, gwd, ls):
    S, D = h.shape
    Fs = dg.shape[-1]
    tk = TOK_TILE
    nk = S // tk

    def body(h_ref, dg_ref, du_ref, a_ref, df_ref, _g0, _g1, _g2, gwg_ref, gwu_ref, gwd_ref, ag_sc, au_sc, ad_sc):
        k = pl.program_id(1)

        @pl.when(k == 0)
        def _():
            ag_sc[...] = jnp.zeros_like(ag_sc)
            au_sc[...] = jnp.zeros_like(au_sc)
            ad_sc[...] = jnp.zeros_like(ad_sc)

        hv = h_ref[...]
        ag_sc[...] += _dot_tn(dg_ref[0], hv)
        au_sc[...] += _dot_tn(du_ref[0], hv)
        ad_sc[...] += _dot_tn(a_ref[0], df_ref[...])

        @pl.when(k == nk - 1)
        def _():
            gwg_ref[0, 0] = ag_sc[...].astype(bf16)
            gwu_ref[0, 0] = au_sc[...].astype(bf16)
            gwd_ref[0, 0] = ad_sc[...].astype(bf16)

    tok = pl.BlockSpec((tk, D), lambda p, k: (k, 0))
    hid = pl.BlockSpec((1, tk, Fs), lambda p, k: (p, k, 0))
    anyspec = pl.BlockSpec(memory_space=pl.ANY)
    orow = pl.BlockSpec((1, 1, Fs, D), lambda p, k: (p, ls, 0, 0))
    return pl.pallas_call(
        body, name="ffn_wgrad", grid=(N_CHIPS, nk),
        in_specs=[tok, hid, hid, hid, tok, anyspec, anyspec, anyspec],
        out_specs=[orow, orow, orow],
        out_shape=[jax.ShapeDtypeStruct(gwg.shape, bf16), jax.ShapeDtypeStruct(gwu.shape, bf16),
                   jax.ShapeDtypeStruct(gwd.shape, bf16)],
        scratch_shapes=[pltpu.VMEM((Fs, D), f32), pltpu.VMEM((Fs, D), f32), pltpu.VMEM((Fs, D), f32)],
        input_output_aliases={5: 0, 6: 1, 7: 2},
        compiler_params=_cp(("parallel", "arbitrary"), 48),
    )(*_hbm(h, dg, du, a, df, gwg, gwu, gwd))


_LANES = 128
_QKV_BLOCKS = D_ATT // _LANES


def _res_spec(lead, d, width, index):
    return pl.BlockSpec((lead, d, TOK_TILE // d, width), index)


def _res_spec3(d, width):
    return pl.BlockSpec((d, TOK_TILE // d, width), lambda i: (0, i, 0))


def _rows_to_residues(tile_bufs, d, put):
    for r in range(d):
        for cb, buf in enumerate(tile_bufs):
            put(r, cb, buf[pl.ds(r, TOK_TILE // d, stride=d), :])


def _residues_to_rows(tile_bufs, d, get):
    for r in range(d):
        for cb, buf in enumerate(tile_bufs):
            buf[pl.ds(r, TOK_TILE // d, stride=d), :] = get(r, cb)


def _mix_in_fwd(x, mod3, w_in, l):
    S, D = x.shape
    N = w_in.shape[-1]
    ts = TOK_TILE

    def body(x_ref, mod_ref, w_ref, o1_ref, o4_ref, o16_ref, zr_ref, h_ref, *bufs):
        j = pl.program_id(1)

        @pl.when(j == 0)
        def _():
            xh, _ = _ln_stats(x_ref[...])
            h_ref[...] = (xh * (1.0 + mod_ref[1:2, :]) + mod_ref[0:1, :]).astype(bf16)

        z = _dot(h_ref[...], w_ref[0, 0])

        @pl.when(j == N_CHIPS - 1)
        def _():
            zr_ref[...] = z

        @pl.when(j < N_CHIPS - 1)
        def _():
            zz = z * jnp.where(j == 0, HEAD_DIM ** -0.5, 1.0)
            o1_ref[j, 0] = zz.astype(bf16)
            for cb, buf in enumerate(bufs):
                buf[...] = zz[:, _LANES * cb:_LANES * (cb + 1)]
            for d, o_ref in zip(DILATIONS[1:], (o4_ref, o16_ref)):
                def put(r, cb, piece, o_ref=o_ref):
                    o_ref[j, r, :, _LANES * cb:_LANES * (cb + 1)] = piece.astype(bf16)
                _rows_to_residues(bufs, d, put)

    tok = pl.BlockSpec((ts, D), lambda i, j: (i, 0))
    res = [_res_spec(3, d, N, lambda i, j: (0, 0, i, 0)) for d in DILATIONS]
    return pl.pallas_call(
        body, name="mix_in_fwd", grid=(S // ts, N_CHIPS),
        in_specs=[tok, _full((3, D)), pl.BlockSpec((1, 1, D, N), lambda i, j: (j, l, 0, 0))],
        out_specs=res + [pl.BlockSpec((ts, N), lambda i, j: (i, 0)), tok],
        out_shape=[jax.ShapeDtypeStruct((3, d, S // d, N), bf16) for d in DILATIONS]
        + [jax.ShapeDtypeStruct((S, N), f32), jax.ShapeDtypeStruct((S, D), bf16)],
        scratch_shapes=[pltpu.VMEM((ts, _LANES), f32)] * _QKV_BLOCKS,
        compiler_params=_cp(("parallel", "arbitrary"), 40),
    )(*_hbm(x, mod3, w_in))


def _mix_in_bwd(dqkv, d_rest, dx_res, x, mod3, w_in, l):
    S, D = x.shape
    N = w_in.shape[-1]
    ts = TOK_TILE

    def body(d1_ref, d4_ref, d16_ref, dr_ref, dxr_ref, x_ref, mod_ref, w_ref, dx_ref, dmod_ref, dz_ref, acc_sc, *bufs):
        i = pl.program_id(0)
        j = pl.program_id(1)

        @pl.when((i == 0) & (j == 0))
        def _():
            dmod_ref[...] = jnp.zeros_like(dmod_ref)

        @pl.when(j == 0)
        def _():
            acc_sc[...] = jnp.zeros_like(acc_sc)

        @pl.when(j == N_CHIPS - 1)
        def _():
            dz_ref[0] = dr_ref[...]

        @pl.when(j < N_CHIPS - 1)
        def _():
            for d, d_ref, tile_bufs in ((4, d4_ref, bufs[:_QKV_BLOCKS]), (16, d16_ref, bufs[_QKV_BLOCKS:])):
                _residues_to_rows(tile_bufs, d, lambda r, cb, d_ref=d_ref: d_ref[0, r, :, _LANES * cb:_LANES * (cb + 1)].astype(f32))
            for cb in range(_QKV_BLOCKS):
                cols = slice(_LANES * cb, _LANES * (cb + 1))
                dz_ref[0, :, cols] = (d1_ref[0, 0, :, cols].astype(f32) + bufs[cb][...] + bufs[_QKV_BLOCKS + cb][...]).astype(bf16)

        acc_sc[...] += _dot_nt(dz_ref[0], w_ref[0, 0])

        @pl.when(j == N_CHIPS - 1)
        def _():
            dh = acc_sc[...]
            xh, rstd0 = _ln_stats(x_ref[...])
            dmod_ref[0:1, :] += jnp.sum(dh, 0, keepdims=True)
            dmod_ref[1:2, :] += jnp.sum(dh * xh, 0, keepdims=True)
            dx_ref[...] = _ln_bwd(dh * (1.0 + mod_ref[1:2, :]), xh, rstd0) + dxr_ref[...]

    tok = pl.BlockSpec((ts, D), lambda i, j: (i, 0))
    res = [_res_spec(1, d, N, lambda i, j: (jnp.minimum(j, 2), 0, i, 0)) for d in DILATIONS]
    return pl.pallas_call(
        body, name="mix_in_bwd", grid=(S // ts, N_CHIPS),
        in_specs=res + [pl.BlockSpec((ts, N), lambda i, j: (i, 0)), tok, tok, _full((3, D)),
                        pl.BlockSpec((1, 1, D, N), lambda i, j: (j, l, 0, 0))],
        out_specs=[tok, _full((3, D)), pl.BlockSpec((1, ts, N), lambda i, j: (j, i, 0))],
        out_shape=[jax.ShapeDtypeStruct((S, D), f32), jax.ShapeDtypeStruct((3, D), f32),
                   jax.ShapeDtypeStruct((N_CHIPS, S, N), bf16)],
        scratch_shapes=[pltpu.VMEM((ts, D), f32)] + [pltpu.VMEM((ts, _LANES), f32)] * (2 * _QKV_BLOCKS),
        compiler_params=_cp(("arbitrary", "arbitrary"), 40),
    )(*_hbm(*dqkv, d_rest, dx_res, x, mod3, w_in))


def _mix_in_wgrad(h, dz, gw, l):
    S, D = h.shape
    N = dz.shape[-1]
    tk = TOK_TILE
    nk = S // tk

    def body(h_ref, dz_ref, _g, gw_ref, acc_sc):
        k = pl.program_id(1)

        @pl.when(k == 0)
        def _():
            acc_sc[...] = jnp.zeros_like(acc_sc)

        acc_sc[...] += _dot_tn(h_ref[...], dz_ref[0])

        @pl.when(k == nk - 1)
        def _():
            gw_ref[0, 0] = acc_sc[...].astype(bf16)

    return pl.pallas_call(
        body, name="mix_in_wgrad", grid=(N_CHIPS, nk),
        in_specs=[pl.BlockSpec((tk, D), lambda p, k: (k, 0)), pl.BlockSpec((1, tk, N), lambda p, k: (p, k, 0)),
                  pl.BlockSpec(memory_space=pl.ANY)],
        out_specs=pl.BlockSpec((1, 1, D, N), lambda p, k: (p, l, 0, 0)),
        out_shape=jax.ShapeDtypeStruct(gw.shape, bf16),
        scratch_shapes=[pltpu.VMEM((D, N), f32)],
        input_output_aliases={2: 0},
        compiler_params=_cp(("parallel", "arbitrary"), 40),
    )(*_hbm(h, dz, gw))


def _mix_out_fwd(x, y_att, y_ssm, y_pool, mod3, w_out, l, lng, lnb):
    S, D = x.shape
    ts = TOK_TILE

    def body(x_ref, ya_ref, ys_ref, yp_ref, mod_ref, w_ref, lng_ref, lnb_ref, xo_ref, y_ref):
        ya = ya_ref[...].astype(bf16)
        y = (_dot(ya[:, 0:256], w_ref[0, 0]) + _dot(ya[:, 256:512], w_ref[1, 0])
             + _dot(ys_ref[...].astype(bf16), w_ref[2, 0]) + _dot(yp_ref[...].astype(bf16), w_ref[3, 0]))
        y_ref[...] = y
        rh, _ = _ln_stats(ALPHA * x_ref[...] + mod_ref[2:3, :] * y)
        xo_ref[...] = rh * lng_ref[...] + lnb_ref[...]

    tok = pl.BlockSpec((ts, D), lambda i: (i, 0))
    return pl.pallas_call(
        body, name="mix_out_fwd", grid=(S // ts,),
        in_specs=[tok, pl.BlockSpec((ts, D_ATT), lambda i: (i, 0)), pl.BlockSpec((ts, D_SSM), lambda i: (i, 0)),
                  pl.BlockSpec((ts, D_POOL), lambda i: (i, 0)), _full((3, D)),
                  pl.BlockSpec((N_CHIPS, 1, 256, D), lambda i: (0, l, 0, 0)), _full((1, D)), _full((1, D))],
        out_specs=[tok, tok],
        out_shape=[jax.ShapeDtypeStruct((S, D), f32), jax.ShapeDtypeStruct((S, D), f32)],
        compiler_params=_cp(("parallel",), 40),
    )(*_hbm(x, y_att, y_ssm, y_pool, mod3, w_out, lng, lnb))


def _mix_out_bwd(dxo, x, y, y_att, y_ssm, y_pool, mod3, w_out, l, lng, gw_out):
    S, D = x.shape
    ts = TOK_TILE
    nt = S // ts

    def body(dxo_ref, x_ref, y_ref, ya_ref, ys_ref, yp_ref, mod_ref, w_ref, lng_ref, _g,
             dxr_ref, da_ref, ds_ref, dp_ref, dgate_ref, dlng_ref, dlnb_ref, gw_ref, acc_sc):
        i = pl.program_id(0)

        @pl.when(i == 0)
        def _():
            dgate_ref[...] = jnp.zeros_like(dgate_ref)
            dlng_ref[...] = jnp.zeros_like(dlng_ref)
            dlnb_ref[...] = jnp.zeros_like(dlnb_ref)
            acc_sc[...] = jnp.zeros_like(acc_sc)

        gate = mod_ref[2:3, :]
        yv = y_ref[...]
        rh, rstd = _ln_stats(ALPHA * x_ref[...] + gate * yv)
        dy_out = dxo_ref[...]
        dlng_ref[...] += jnp.sum(dy_out * rh, 0, keepdims=True)
        dlnb_ref[...] += jnp.sum(dy_out, 0, keepdims=True)
        dr = _ln_bwd(dy_out * lng_ref[...], rh, rstd)
        dxr_ref[...] = ALPHA * dr
        dgate_ref[...] += jnp.sum(dr * yv, 0, keepdims=True)
        dy = (gate * dr).astype(bf16)
        da_ref[:, 0:256] = _dot_nt(dy, w_ref[0, 0])
        da_ref[:, 256:512] = _dot_nt(dy, w_ref[1, 0])
        ds_ref[...] = _dot_nt(dy, w_ref[2, 0])
        dp_ref[...] = _dot_nt(dy, w_ref[3, 0])
        ya = ya_ref[...].astype(bf16)
        acc_sc[0] += _dot_tn(ya[:, 0:256], dy)
        acc_sc[1] += _dot_tn(ya[:, 256:512], dy)
        acc_sc[2] += _dot_tn(ys_ref[...].astype(bf16), dy)
        acc_sc[3] += _dot_tn(yp_ref[...].astype(bf16), dy)

        @pl.when(i == nt - 1)
        def _():
            gw_ref[:, 0] = acc_sc[...].astype(bf16)

    tok = pl.BlockSpec((ts, D), lambda i: (i, 0))
    t512 = pl.BlockSpec((ts, D_ATT), lambda i: (i, 0))
    t256 = pl.BlockSpec((ts, 256), lambda i: (i, 0))
    wspec = pl.BlockSpec((N_CHIPS, 1, 256, D), lambda i: (0, l, 0, 0))
    return pl.pallas_call(
        body, name="mix_out_bwd", grid=(nt,),
        in_specs=[tok, tok, tok, t512, t256, t256, _full((3, D)), wspec, _full((1, D)), pl.BlockSpec(memory_space=pl.ANY)],
        out_specs=[tok, t512, t256, t256, _full((1, D)), _full((1, D)), _full((1, D)), wspec],
        out_shape=[jax.ShapeDtypeStruct((S, D), f32), jax.ShapeDtypeStruct((S, D_ATT), f32),
                   jax.ShapeDtypeStruct((S, D_SSM), f32), jax.ShapeDtypeStruct((S, D_POOL), f32),
                   jax.ShapeDtypeStruct((1, D), f32), jax.ShapeDtypeStruct((1, D), f32), jax.ShapeDtypeStruct((1, D), f32),
                   jax.ShapeDtypeStruct(gw_out.shape, bf16)],
        scratch_shapes=[pltpu.VMEM((N_CHIPS, 256, D), f32)],
        input_output_aliases={9: 7},
        compiler_params=_cp(("arbitrary",), 48),
    )(*_hbm(dxo, x, y, y_att, y_ssm, y_pool, mod3, w_out, lng, gw_out))


def _t5_bucket(dist):
    max_exact = N_BUCKETS // 2
    d = np.maximum(dist, 1).astype(np.float32)
    large = max_exact + (np.log(d / max_exact) / math.log(MAX_DISTANCE / max_exact)
                         * (N_BUCKETS - max_exact)).astype(np.int32)
    large = np.minimum(large, N_BUCKETS - 1)
    return np.where(dist < max_exact, dist, large).astype(np.int32)


def _bucket_table():
    q = ATT_BLOCK
    i = np.arange(q)[:, None]
    j = np.arange(2 * q)[None, :]
    r = i + q - j
    in_band = (r >= 0) & (r <= q)
    tabs = [np.where(in_band, _t5_bucket(np.clip(r, 0, None) * d), -1) for d in DILATIONS]
    return np.stack(tabs).astype(np.int32)


def _bias_fwd(rel_bias, table):
    def body(rb_ref, tab_ref, out_ref):
        for b in range(3):
            tb = tab_ref[b]
            for h in range(N_HEADS):
                def pick(k, acc):
                    return jnp.where(tb == k, rb_ref[k, h], acc)
                out_ref[b, h] = lax.fori_loop(0, N_BUCKETS, pick, jnp.where(tb < 0, NEG, 0.0).astype(f32))

    return pl.pallas_call(
        body, name="bias_fwd",
        in_specs=[pl.BlockSpec(memory_space=pltpu.SMEM), pl.BlockSpec(memory_space=pltpu.VMEM)],
        out_specs=pl.BlockSpec(memory_space=pltpu.VMEM),
        out_shape=jax.ShapeDtypeStruct((3, N_HEADS, ATT_BLOCK, 2 * ATT_BLOCK), f32),
    )(rel_bias, table)


def _bias_bwd(dbias, table):
    def body(db_ref, tab_ref, out_ref):
        def per_bucket(k, c):
            for h in range(N_HEADS):
                tot = jnp.zeros((), f32)
                for b in range(3):
                    tot = tot + jnp.sum(jnp.where(tab_ref[b] == k, db_ref[b, h], 0.0))
                out_ref[k, h] = tot
            return c
        lax.fori_loop(0, N_BUCKETS, per_bucket, 0)

    return pl.pallas_call(
        body, name="bias_bwd",
        in_specs=[pl.BlockSpec(memory_space=pltpu.VMEM), pl.BlockSpec(memory_space=pltpu.VMEM)],
        out_specs=pl.BlockSpec(memory_space=pltpu.SMEM),
        out_shape=jax.ShapeDtypeStruct((N_BUCKETS, N_HEADS), f32),
    )(dbias, table)


def _att_unit(u, nbr):
    rows = pl.ds(pl.multiple_of(u * ATT_BLOCK, ATT_BLOCK), ATT_BLOCK)
    prev = pl.ds(pl.multiple_of(jnp.maximum(u - 1, 0) * ATT_BLOCK, ATT_BLOCK), ATT_BLOCK)
    return rows, prev, (u % nbr) != 0


_N_PAIRS = N_HEADS // 2


def _pair_rows(t):
    lane = lax.broadcasted_iota(jnp.int32, t.shape, 1)
    zero = jnp.zeros_like(t)
    return jnp.concatenate([jnp.where(lane < HEAD_DIM, t, zero), jnp.where(lane >= HEAD_DIM, t, zero)], axis=0)


def _pair_cols(big):
    lane = lax.broadcasted_iota(jnp.int32, (ATT_BLOCK, _LANES), 1)
    return jnp.where(lane < HEAD_DIM, big[:ATT_BLOCK], big[ATT_BLOCK:])


def _pair_column(ref, rows, hp):
    t = ref[rows, :]
    return jnp.concatenate([t[:, 2 * hp:2 * hp + 1], t[:, 2 * hp + 1:2 * hp + 2]], axis=0)


def _pair_band(ref, rows, prev, nbr, hp):
    lanes = pl.ds(_LANES * hp, _LANES)
    cur = ref[0, rows, lanes]
    return cur if nbr == 1 else jnp.concatenate([ref[0, prev, lanes], cur], axis=0)


def _pair_scores(q_ref, k_ref, b_ref, rows, prev, valid_prev, nbr, hp):
    qbd = _pair_rows(q_ref[0, rows, pl.ds(_LANES * hp, _LANES)])
    kb = _pair_band(k_ref, rows, prev, nbr, hp)
    bias = b_ref[0, 2 * hp:2 * hp + 2].reshape(2 * ATT_BLOCK, 2 * ATT_BLOCK)
    if nbr == 1:
        return qbd, kb, _dot_nt(qbd, kb) + bias[:, ATT_BLOCK:]
    s = _dot_nt(qbd, kb) + bias
    col = lax.broadcasted_iota(jnp.int32, s.shape, 1)
    return qbd, kb, jnp.where((col >= ATT_BLOCK) | valid_prev, s, NEG)


def _qkv_specs(S, branch):
    return ([pl.BlockSpec((1, S, D_ATT), lambda i, t=t: (t, 0, 0)) for t in range(3)],
            pl.BlockSpec((1, N_HEADS, ATT_BLOCK, 2 * ATT_BLOCK), lambda i: (branch, 0, 0, 0)))


def _att_fwd(qkv, bias, branch):
    S = qkv.shape[1]
    nbr = BLOCKS_PER_RESIDUE[branch]

    def body(q_ref, k_ref, v_ref, b_ref, o_ref, lse_ref):
        lse_ref[...] = jnp.zeros_like(lse_ref)

        def unit(u, c):
            rows, prev, valid_prev = _att_unit(u, nbr)
            for hp in range(_N_PAIRS):
                _, _, s = _pair_scores(q_ref, k_ref, b_ref, rows, prev, valid_prev, nbr, hp)
                m = jnp.max(s, -1, keepdims=True)
                p = jnp.exp(s - m)
                den = jnp.sum(p, -1, keepdims=True)
                big = _dot(p.astype(bf16), _pair_band(v_ref, rows, prev, nbr, hp))
                o_ref[rows, pl.ds(_LANES * hp, _LANES)] = _pair_cols(big / den)
                lse = m + jnp.log(den)
                lse_ref[rows, pl.ds(2 * hp, 1)] = lse[:ATT_BLOCK]
                lse_ref[rows, pl.ds(2 * hp + 1, 1)] = lse[ATT_BLOCK:]
            return c

        lax.fori_loop(0, N_UNITS, unit, 0)

    qkv_specs, bspec = _qkv_specs(S, branch)
    return pl.pallas_call(
        body, name="att_fwd", grid=(1,),
        in_specs=qkv_specs + [bspec],
        out_specs=[pl.BlockSpec((S, D_ATT), lambda i: (0, 0)), pl.BlockSpec((S, _LANES), lambda i: (0, 0))],
        out_shape=[jax.ShapeDtypeStruct((S, D_ATT), f32), jax.ShapeDtypeStruct((S, _LANES), f32)],
        compiler_params=_cp(("arbitrary",), 40),
    )(*_hbm(qkv, qkv, qkv, bias))


def _att_bwd(qkv, do, lse, crow, bias, branch):
    S = qkv.shape[1]
    nbr = BLOCKS_PER_RESIDUE[branch]

    def body(q_ref, k_ref, v_ref, do_ref, lse_ref, c_ref, b_ref, dqkv_ref, db_ref, dk_sc, dv_sc):
        dk_sc[...] = jnp.zeros_like(dk_sc)
        dv_sc[...] = jnp.zeros_like(dv_sc)
        db_ref[...] = jnp.zeros_like(db_ref)

        def unit(u, c):
            rows, prev, valid_prev = _att_unit(u, nbr)
            for hp in range(_N_PAIRS):
                lanes = pl.ds(_LANES * hp, _LANES)
                qbd, kb, s = _pair_scores(q_ref, k_ref, b_ref, rows, prev, valid_prev, nbr, hp)
                p = jnp.exp(s - _pair_column(lse_ref, rows, hp))
                dobd = _pair_rows(do_ref[rows, lanes])
                ds = p * (_dot_nt(dobd, _pair_band(v_ref, rows, prev, nbr, hp)) - _pair_column(c_ref, rows, hp))
                if nbr == 1:
                    db_ref[2 * hp:2 * hp + 2, :, ATT_BLOCK:] += ds.reshape(2, ATT_BLOCK, ATT_BLOCK)
                else:
                    db_ref[2 * hp:2 * hp + 2] += ds.reshape(2, ATT_BLOCK, 2 * ATT_BLOCK)
                dsb = ds.astype(bf16)
                dqkv_ref[0, rows, lanes] = (HEAD_DIM ** -0.5 * _pair_cols(_dot(dsb, kb))).astype(bf16)
                dkb = _dot_tn(dsb, qbd)
                dvb = _dot_tn(p.astype(bf16), dobd)
                if nbr == 1:
                    dk_sc[rows, lanes] += dkb
                    dv_sc[rows, lanes] += dvb
                else:
                    dk_sc[prev, lanes] += dkb[:ATT_BLOCK]
                    dv_sc[prev, lanes] += dvb[:ATT_BLOCK]
                    dk_sc[rows, lanes] += dkb[ATT_BLOCK:]
                    dv_sc[rows, lanes] += dvb[ATT_BLOCK:]
            return c

        lax.fori_loop(0, N_UNITS, unit, 0)
        dqkv_ref[1] = dk_sc[...].astype(bf16)
        dqkv_ref[2] = dv_sc[...].astype(bf16)

    qkv_specs, bspec = _qkv_specs(S, branch)
    row = pl.BlockSpec((S, _LANES), lambda i: (0, 0))
    return pl.pallas_call(
        body, name="att_bwd", grid=(1,),
        in_specs=qkv_specs + [pl.BlockSpec((S, D_ATT), lambda i: (0, 0)), row, row, bspec],
        out_specs=[pl.BlockSpec((3, S, D_ATT), lambda i: (0, 0, 0)),
                   pl.BlockSpec((N_HEADS, ATT_BLOCK, 2 * ATT_BLOCK), lambda i: (0, 0, 0))],
        out_shape=[jax.ShapeDtypeStruct((3, S, D_ATT), bf16), jax.ShapeDtypeStruct((N_HEADS, ATT_BLOCK, 2 * ATT_BLOCK), f32)],
        scratch_shapes=[pltpu.VMEM((S, D_ATT), f32), pltpu.VMEM((S, D_ATT), f32)],
        compiler_params=_cp(("arbitrary",), 48),
    )(*_hbm(qkv, qkv, qkv, do, lse, crow, bias))


def _branch_weights(lse_ref):
    l0, l1, l2 = lse_ref[0], lse_ref[1], lse_ref[2]
    m = jnp.maximum(jnp.maximum(l0, l1), l2)
    e0, e1, e2 = jnp.exp(l0 - m), jnp.exp(l1 - m), jnp.exp(l2 - m)
    tot = e0 + e1 + e2
    return e0 / tot, e1 / tot, e2 / tot


def _att_merge(os, lses):
    S = os[0].shape[0] * os[0].shape[1]
    ts = TOK_TILE

    def body(o1_ref, o4_ref, o16_ref, l1_ref, l4_ref, l16_ref, y_ref, lt_ref, *bufs):
        obufs = (bufs[:_QKV_BLOCKS], bufs[_QKV_BLOCKS:2 * _QKV_BLOCKS])
        lt_ref[0] = l1_ref[0]
        for k, (d, o_ref, l_ref) in enumerate(((4, o4_ref, l4_ref), (16, o16_ref, l16_ref))):
            _residues_to_rows(obufs[k], d, lambda r, cb, o_ref=o_ref: o_ref[r, :, _LANES * cb:_LANES * (cb + 1)])
            _residues_to_rows([bufs[2 * _QKV_BLOCKS + k]], d, lambda r, cb, l_ref=l_ref: l_ref[r])
            lt_ref[1 + k] = bufs[2 * _QKV_BLOCKS + k][...]
        w = _branch_weights(lt_ref)
        for h in range(N_HEADS):
            cs = slice(HEAD_DIM * h, HEAD_DIM * (h + 1))
            half = slice(HEAD_DIM * (h % 2), HEAD_DIM * (h % 2 + 1))
            y_ref[:, cs] = (w[0][:, h:h + 1] * o1_ref[0, :, cs] + w[1][:, h:h + 1] * obufs[0][h // 2][:, half]
                            + w[2][:, h:h + 1] * obufs[1][h // 2][:, half])

    return pl.pallas_call(
        body, name="att_merge", grid=(S // ts,),
        in_specs=[_res_spec3(d, D_ATT) for d in DILATIONS] + [_res_spec3(d, _LANES) for d in DILATIONS],
        out_specs=[pl.BlockSpec((ts, D_ATT), lambda i: (i, 0)), pl.BlockSpec((3, ts, _LANES), lambda i: (0, i, 0))],
        out_shape=[jax.ShapeDtypeStruct((S, D_ATT), f32), jax.ShapeDtypeStruct((3, S, _LANES), f32)],
        scratch_shapes=[pltpu.VMEM((ts, _LANES), f32)] * (2 * _QKV_BLOCKS + 2),
        compiler_params=_cp(("parallel",)),
    )(*_hbm(*os, *lses))


def _att_merge_bwd(dy, y, lse3):
    S = dy.shape[0]
    ts = TOK_TILE

    def body(dy_ref, y_ref, lse_ref, do1_ref, do4_ref, do16_ref, c1_ref, c4_ref, c16_ref, *bufs):
        dobufs = (bufs[:_QKV_BLOCKS], bufs[_QKV_BLOCKS:2 * _QKV_BLOCKS], bufs[2 * _QKV_BLOCKS:3 * _QKV_BLOCKS])
        cbufs = bufs[3 * _QKV_BLOCKS:]
        w = _branch_weights(lse_ref)
        for cb in cbufs:
            cb[...] = jnp.zeros_like(cb)
        for h in range(N_HEADS):
            cs = slice(HEAD_DIM * h, HEAD_DIM * (h + 1))
            half = slice(HEAD_DIM * (h % 2), HEAD_DIM * (h % 2 + 1))
            dyh = dy_ref[:, cs]
            t = jnp.sum(dyh * y_ref[:, cs], -1, keepdims=True)
            for p in range(3):
                wp = w[p][:, h:h + 1]
                dobufs[p][h // 2][:, half] = wp * dyh
                cbufs[p][:, h:h + 1] = wp * t
        for cb in range(_QKV_BLOCKS):
            do1_ref[0, :, _LANES * cb:_LANES * (cb + 1)] = dobufs[0][cb][...].astype(bf16)
        c1_ref[0] = cbufs[0][...]
        for k, (d, do_ref, c_ref) in enumerate(((4, do4_ref, c4_ref), (16, do16_ref, c16_ref))):
            def put_do(r, cb, piece, do_ref=do_ref):
                do_ref[r, :, _LANES * cb:_LANES * (cb + 1)] = piece.astype(bf16)

            def put_c(r, cb, piece, c_ref=c_ref):
                c_ref[r] = piece

            _rows_to_residues(dobufs[1 + k], d, put_do)
            _rows_to_residues([cbufs[1 + k]], d, put_c)

    return pl.pallas_call(
        body, name="att_merge_bwd", grid=(S // ts,),
        in_specs=[pl.BlockSpec((ts, D_ATT), lambda i: (i, 0)), pl.BlockSpec((ts, D_ATT), lambda i: (i, 0)),
                  pl.BlockSpec((3, ts, _LANES), lambda i: (0, i, 0))],
        out_specs=[_res_spec3(d, D_ATT) for d in DILATIONS] + [_res_spec3(d, _LANES) for d in DILATIONS],
        out_shape=[jax.ShapeDtypeStruct((d, S // d, D_ATT), bf16) for d in DILATIONS]
        + [jax.ShapeDtypeStruct((d, S // d, _LANES), f32) for d in DILATIONS],
        scratch_shapes=[pltpu.VMEM((ts, _LANES), f32)] * (3 * _QKV_BLOCKS + 3),
        compiler_params=_cp(("parallel",)),
    )(*_hbm(dy, y, lse3))


_SSM_ROWS = 256


def _scan_in_place(sr_ref, si_ref, a_ref, reverse):
    S, N = sr_ref.shape
    nst = S // SCAN_SEG
    ar = jnp.broadcast_to(a_ref[0:1, :], (SCAN_SEG, N))
    ai = jnp.broadcast_to(a_ref[1:2, :], (SCAN_SEG, N))
    if reverse:
        ai = -ai
    row = lax.broadcasted_iota(jnp.int32, (SCAN_SEG, N), 0)
    zero = jnp.zeros((SCAN_SEG, N), f32)

    def tile(t):
        return pl.ds(pl.multiple_of((nst - 1 - t if reverse else t) * SCAN_SEG, SCAN_SEG), SCAN_SEG)

    def local(t, c):
        sr, si, pr, pi = c
        rows = tile(t)
        nsr = ar * sr - ai * si + sr_ref[rows, :]
        nsi = ar * si + ai * sr + si_ref[rows, :]
        sr_ref[rows, :] = nsr
        si_ref[rows, :] = nsi
        return nsr, nsi, ar * pr - ai * pi, ar * pi + ai * pr

    fr, fi, apr, api = lax.fori_loop(0, nst, local, (zero, zero, zero + 1.0, zero))

    def shift(v):
        if reverse:
            return jnp.where(row == SCAN_SEG - 1, 0.0, pltpu.roll(v, SCAN_SEG - 1, axis=0))
        return jnp.where(row == 0, 0.0, pltpu.roll(v, 1, axis=0))

    cr, ci = zero, zero
    for _ in range(SCAN_SEG - 1):
        cr, ci = shift(fr + apr * cr - api * ci), shift(fi + apr * ci + api * cr)

    def fix(t, c):
        pr, pi = c
        npr, npi = ar * pr - ai * pi, ar * pi + ai * pr
        rows = tile(t)
        sr_ref[rows, :] += npr * cr - npi * ci
        si_ref[rows, :] += npr * ci + npi * cr
        return npr, npi

    lax.fori_loop(0, nst, fix, (zero + 1.0, zero))


def _ssm_states(u, bre, bim, a2):
    S = u.shape[0]

    def body(u_ref, br_ref, bi_ref, a_ref, sr_ref, si_ref):
        brb = br_ref[...].astype(bf16)
        bib = bi_ref[...].astype(bf16)

        def project(t, c):
            rows = pl.ds(pl.multiple_of(t * _SSM_ROWS, _SSM_ROWS), _SSM_ROWS)
            ub = u_ref[rows, :].astype(bf16)
            sr_ref[rows, :] = _dot(ub, brb)
            si_ref[rows, :] = _dot(ub, bib)
            return c

        lax.fori_loop(0, S // _SSM_ROWS, project, 0)
        _scan_in_place(sr_ref, si_ref, a_ref, False)

    vm = pl.BlockSpec(memory_space=pltpu.VMEM)
    return pl.pallas_call(
        body, name="ssm_states", in_specs=[vm] * 4, out_specs=[vm, vm],
        out_shape=[jax.ShapeDtypeStruct((S, D_STATE), f32)] * 2,
        compiler_params=_cp(None, 48),
    )(u, bre, bim, a2)


def _ssm_out(sr, si, u, cre, cim, dskip, glu_w, glu_b):
    S = u.shape[0]
    ts = TOK_TILE

    def body(sr_ref, si_ref, u_ref, cr_ref, ci_ref, d_ref, w_ref, b_ref, out_ref, y_ref):
        y = (_dot(sr_ref[...].astype(bf16), cr_ref[...].astype(bf16))
             - _dot(si_ref[...].astype(bf16), ci_ref[...].astype(bf16)) + d_ref[...] * u_ref[...])
        y_ref[...] = y
        z = _dot(_gelu(y).astype(bf16), w_ref[...].astype(bf16)) + b_ref[...]
        out_ref[...] = y * jax.nn.sigmoid(z)

    st = pl.BlockSpec((ts, D_STATE), lambda i: (i, 0))
    ch = pl.BlockSpec((ts, D_SSM), lambda i: (i, 0))
    return pl.pallas_call(
        body, name="ssm_out", grid=(S // ts,),
        in_specs=[st, st, ch, _full((D_STATE, D_SSM)), _full((D_STATE, D_SSM)), _full((1, D_SSM)),
                  _full((D_SSM, D_SSM)), _full((1, D_SSM))],
        out_specs=[ch, ch],
        out_shape=[jax.ShapeDtypeStruct((S, D_SSM), f32)] * 2,
        compiler_params=_cp(("parallel",)),
    )(*_hbm(sr, si, u, cre, cim, dskip, glu_w, glu_b))


def _ssm_out_bwd(dout, y, u, sr, si, dskip, glu_w, glu_b):
    S = u.shape[0]
    ts = TOK_TILE

    def body(do_ref, y_ref, u_ref, sr_ref, si_ref, d_ref, w_ref, b_ref,
             dy_ref, du_ref, dcr_ref, dci_ref, dd_ref, dgb_ref, dgw_ref):
        @pl.when(pl.program_id(0) == 0)
        def _():
            for r in (dcr_ref, dci_ref, dd_ref, dgb_ref, dgw_ref):
                r[...] = jnp.zeros_like(r)

        y = y_ref[...]
        dout = do_ref[...]
        wb = w_ref[...].astype(bf16)
        ge = _gelu(y).astype(bf16)
        sz = jax.nn.sigmoid(_dot(ge, wb) + b_ref[...])
        dz = dout * y * sz * (1.0 - sz)
        dzb = dz.astype(bf16)
        dgb_ref[...] += jnp.sum(dz, 0, keepdims=True)
        dgw_ref[...] += _dot_tn(ge, dzb)
        dy = dout * sz + _gelu_grad(y) * _dot_nt(dzb, wb)
        uv = u_ref[...]
        dd_ref[...] += jnp.sum(dy * uv, 0, keepdims=True)
        du_ref[...] = dy * d_ref[...]
        dy_ref[...] = dy
        dyb = dy.astype(bf16)
        dcr_ref[...] += _dot_tn(sr_ref[...].astype(bf16), dyb)
        dci_ref[...] -= _dot_tn(si_ref[...].astype(bf16), dyb)

    st = pl.BlockSpec((ts, D_STATE), lambda i: (i, 0))
    ch = pl.BlockSpec((ts, D_SSM), lambda i: (i, 0))
    c_full = _full((D_STATE, D_SSM))
    return pl.pallas_call(
        body, name="ssm_out_bwd", grid=(S // ts,),
        in_specs=[ch, ch, ch, st, st, _full((1, D_SSM)), _full((D_SSM, D_SSM)), _full((1, D_SSM))],
        out_specs=[ch, ch, c_full, c_full, _full((1, D_SSM)), _full((1, D_SSM)), _full((D_SSM, D_SSM))],
        out_shape=[jax.ShapeDtypeStruct((S, D_SSM), f32), jax.ShapeDtypeStruct((S, D_SSM), f32),
                   jax.ShapeDtypeStruct((D_STATE, D_SSM), f32), jax.ShapeDtypeStruct((D_STATE, D_SSM), f32),
                   jax.ShapeDtypeStruct((1, D_SSM), f32), jax.ShapeDtypeStruct((1, D_SSM), f32),
                   jax.ShapeDtypeStruct((D_SSM, D_SSM), f32)],
        compiler_params=_cp(("arbitrary",), 40),
    )(*_hbm(dout, y, u, sr, si, dskip, glu_w, glu_b))


def _ssm_states_bwd(dy, du_skip, u, sr, si, cre, cim, bre, bim, a2):
    S = u.shape[0]
    N = D_STATE
    nst = S // SCAN_SEG
    nproj = S // _SSM_ROWS

    def body(dy_ref, dus_ref, u_ref, sr_ref, si_ref, cr_ref, ci_ref, br_ref, bi_ref, a_ref,
             du_ref, dbr_ref, dbi_ref, da_ref, lr_ref, li_ref):
        crb = cr_ref[...].astype(bf16)
        cib = ci_ref[...].astype(bf16)

        def project(t, c):
            rows = pl.ds(pl.multiple_of(t * _SSM_ROWS, _SSM_ROWS), _SSM_ROWS)
            dyb = dy_ref[rows, :].astype(bf16)
            lr_ref[rows, :] = _dot_nt(dyb, crb)
            li_ref[rows, :] = -_dot_nt(dyb, cib)
            return c

        lax.fori_loop(0, nproj, project, 0)
        _scan_in_place(lr_ref, li_ref, a_ref, True)

        row = lax.broadcasted_iota(jnp.int32, (SCAN_SEG, N), 0)
        last = pl.ds((nst - 1) * SCAN_SEG, SCAN_SEG)
        pr = jnp.where(row == 0, 0.0, pltpu.roll(sr_ref[last, :], 1, axis=0))
        pi = jnp.where(row == 0, 0.0, pltpu.roll(si_ref[last, :], 1, axis=0))
        first = pl.ds(0, SCAN_SEG)
        acc_r = lr_ref[first, :] * pr + li_ref[first, :] * pi
        acc_i = li_ref[first, :] * pr - lr_ref[first, :] * pi

        def step(t, c):
            acc_r, acc_i = c
            rows = pl.ds(pl.multiple_of(t * SCAN_SEG, SCAN_SEG), SCAN_SEG)
            prev = pl.ds(pl.multiple_of((t - 1) * SCAN_SEG, SCAN_SEG), SCAN_SEG)
            lrv, liv, srv, siv = lr_ref[rows, :], li_ref[rows, :], sr_ref[prev, :], si_ref[prev, :]
            return acc_r + lrv * srv + liv * siv, acc_i + liv * srv - lrv * siv

        acc_r, acc_i = lax.fori_loop(1, nst, step, (acc_r, acc_i))
        da_ref[0:1, :] = jnp.sum(acc_r, 0, keepdims=True)
        da_ref[1:2, :] = jnp.sum(acc_i, 0, keepdims=True)

        brb = br_ref[...].astype(bf16)
        bib = bi_ref[...].astype(bf16)
        dbr_ref[...] = jnp.zeros_like(dbr_ref)
        dbi_ref[...] = jnp.zeros_like(dbi_ref)

        def back(t, c):
            rows = pl.ds(pl.multiple_of(t * _SSM_ROWS, _SSM_ROWS), _SSM_ROWS)
            lrb = lr_ref[rows, :].astype(bf16)
            lib = li_ref[rows, :].astype(bf16)
            du_ref[rows, :] = dus_ref[rows, :] + _dot_nt(lrb, brb) + _dot_nt(lib, bib)
            ub = u_ref[rows, :].astype(bf16)
            dbr_ref[...] += _dot_tn(ub, lrb)
            dbi_ref[...] += _dot_tn(ub, lib)
            return c

        lax.fori_loop(0, nproj, back, 0)

    vm = pl.BlockSpec(memory_space=pltpu.VMEM)
    return pl.pallas_call(
        body, name="ssm_states_bwd", in_specs=[vm] * 10, out_specs=[vm] * 4,
        out_shape=[jax.ShapeDtypeStruct((S, D_SSM), f32), jax.ShapeDtypeStruct((D_SSM, D_STATE), f32),
                   jax.ShapeDtypeStruct((D_SSM, D_STATE), f32), jax.ShapeDtypeStruct((2, D_STATE), f32)],
        scratch_shapes=[pltpu.VMEM((S, D_STATE), f32), pltpu.VMEM((S, D_STATE), f32)],
        compiler_params=_cp(None, 56),
    )(dy, du_skip, u, sr, si, cre, cim, bre, bim, a2)


_POOL_TILE = 256


def _window_sums(xt, back):
    n = xt.shape[0]
    out = []
    ws = xt
    for k in (1, 2, 4, 8):
        ws = ws + pltpu.roll(ws, k if back else n - k, axis=0)
        out.append(ws)
    return out


def _pool_count(r0, w):
    t = r0 + lax.broadcasted_iota(jnp.int32, (_POOL_TILE, POOL_GROUP), 0)
    return jnp.minimum(t + 1, w).astype(f32)


def _pool_fwd(u_pad, pool_w, pool_scale):
    S = u_pad.shape[0] - POOL_HALO
    nt = S // _POOL_TILE

    def body(u_ref, w_ref, sc_ref, y_ref):
        def tile(t, c):
            r0 = pl.multiple_of(t * _POOL_TILE, _POOL_TILE)
            for g, w in enumerate(POOL_WINDOWS):
                cs = pl.ds(POOL_GROUP * g, POOL_GROUP)
                xt = u_ref[pl.ds(r0, _POOL_TILE + POOL_HALO), cs]
                ws = _window_sums(xt, True)[g][POOL_HALO:, :]
                pooled = ws / _pool_count(r0, w) - xt[POOL_HALO:, :]
                y_ref[pl.ds(r0, _POOL_TILE), cs] = _dot(pooled.astype(bf16), w_ref[g].astype(bf16)) * sc_ref[:, cs]
            return c
        lax.fori_loop(0, nt, tile, 0)

    vm = pl.BlockSpec(memory_space=pltpu.VMEM)
    return pl.pallas_call(
        body, name="pool_fwd", in_specs=[vm, vm, vm], out_specs=vm,
        out_shape=jax.ShapeDtypeStruct((S, D_POOL), f32),
    )(u_pad, pool_w, pool_scale)


def _pool_bwd(dy_pad, u_pad, pool_w, pool_scale):
    S = u_pad.shape[0] - POOL_HALO
    nt = S // _POOL_TILE
    n = _POOL_TILE + POOL_HALO

    def body(dy_ref, u_ref, w_ref, sc_ref, du_ref, dw_ref, dsc_ref):
        dw_ref[...] = jnp.zeros_like(dw_ref)
        dsc_ref[...] = jnp.zeros_like(dsc_ref)

        def tile(t, c):
            r0 = pl.multiple_of(t * _POOL_TILE, _POOL_TILE)
            for g, w in enumerate(POOL_WINDOWS):
                cs = pl.ds(POOL_GROUP * g, POOL_GROUP)
                wb = w_ref[g].astype(bf16)
                xt = u_ref[pl.ds(r0, n), cs]
                pooled = (_window_sums(xt, True)[g][POOL_HALO:, :] / _pool_count(r0, w) - xt[POOL_HALO:, :]).astype(bf16)
                dy = dy_ref[pl.ds(r0, _POOL_TILE), cs]
                dsc_ref[:, cs] += jnp.sum(dy * _dot(pooled, wb), 0, keepdims=True)
                dw_ref[g] += _dot_tn(pooled, (dy * sc_ref[:, cs]).astype(bf16))
                dyh = (dy_ref[pl.ds(r0, n), cs] * sc_ref[:, cs]).astype(bf16)
                dpl = _dot_nt(dyh, wb)
                cnt = jnp.minimum(r0 + lax.broadcasted_iota(jnp.int32, (n, POOL_GROUP), 0) + 1, w).astype(f32)
                lead = _window_sums(dpl / cnt, False)[g]
                du_ref[pl.ds(r0, _POOL_TILE), cs] = lead[:_POOL_TILE, :] - dpl[:_POOL_TILE, :]
            return c
        lax.fori_loop(0, nt, tile, 0)

    vm = pl.BlockSpec(memory_space=pltpu.VMEM)
    return pl.pallas_call(
        body, name="pool_bwd", in_specs=[vm, vm, vm, vm], out_specs=[vm, vm, vm],
        out_shape=[jax.ShapeDtypeStruct((S, D_POOL), f32), jax.ShapeDtypeStruct((4, POOL_GROUP, POOL_GROUP), f32),
                   jax.ShapeDtypeStruct((1, D_POOL), f32)],
    )(dy_pad, u_pad, pool_w, pool_scale)


def _loss_head(y, target):
    S, D = y.shape
    ts = TOK_TILE

    def body(y_ref, t_ref, loss_ref, dy_ref):
        @pl.when(pl.program_id(0) == 0)
        def _():
            loss_ref[...] = jnp.zeros_like(loss_ref)

        d = y_ref[...] - t_ref[...]
        dy_ref[...] = d * (1.0 / D)
        loss_ref[...] += 0.5 * jnp.sum(jnp.sum(d * d, -1, keepdims=True) * (1.0 / D), 0, keepdims=True)

    tok = pl.BlockSpec((ts, D), lambda i: (i, 0))
    return pl.pallas_call(
        body, name="loss_head", grid=(S // ts,),
        in_specs=[tok, tok], out_specs=[_full((1, 1)), tok],
        out_shape=[jax.ShapeDtypeStruct((1, 1), f32), jax.ShapeDtypeStruct((S, D), f32)],
        compiler_params=_cp(("arbitrary",)),
    )(*_hbm(y, target))


_ADA_COLS = 768


def _ada_fwd(c_all, ada_w, ada_b_cols):
    L, D, N = ada_w.shape
    B = c_all.shape[0]

    def body(c_ref, w_ref, b_ref, out_ref):
        cv = c_ref[...]
        cond = (cv * jax.nn.sigmoid(cv)).astype(bf16)
        out_ref[0] = _dot(cond, w_ref[0].astype(bf16)) + b_ref[0]

    return pl.pallas_call(
        body, name="ada_fwd", grid=(L, N // _ADA_COLS),
        in_specs=[_full((B, D)), pl.BlockSpec((1, D, _ADA_COLS), lambda l, j: (l, 0, j)),
                  pl.BlockSpec((1, 1, _ADA_COLS), lambda l, j: (l, 0, j))],
        out_specs=pl.BlockSpec((1, B, _ADA_COLS), lambda l, j: (l, 0, j)),
        out_shape=jax.ShapeDtypeStruct((L, B, N), f32),
        compiler_params=_cp(("parallel", "parallel")),
    )(c_all, ada_w, ada_b_cols)


def _ada_wgrad(c_all_t, dmod_cols):
    D, B = c_all_t.shape
    L, _, N = dmod_cols.shape

    def body(ct_ref, dm_ref, out_ref):
        cv = ct_ref[...]
        cond = cv * jax.nn.sigmoid(cv)
        acc = cond[:, 0:1] * dm_ref[0, 0:1, :]
        for b in range(1, B):
            acc = acc + cond[:, b:b + 1] * dm_ref[0, b:b + 1, :]
        out_ref[0] = acc

    return pl.pallas_call(
        body, name="ada_wgrad", grid=(L, N // _ADA_COLS),
        in_specs=[_full((D, B)), pl.BlockSpec((1, B, _ADA_COLS), lambda l, j: (l, 0, j))],
        out_specs=pl.BlockSpec((1, D, _ADA_COLS), lambda l, j: (l, 0, j)),
        out_shape=jax.ShapeDtypeStruct((L, D, N), f32),
        compiler_params=_cp(("parallel", "parallel")),
    )(c_all_t, dmod_cols)


def _adam_math(w, g, m, v):
    m = ADAM_B1 * m + (1.0 - ADAM_B1) * g
    v = ADAM_B2 * v + (1.0 - ADAM_B2) * (g * g)
    m_hat = m / (1.0 - ADAM_B1 ** ADAM_STEP)
    v_hat = v / (1.0 - ADAM_B2 ** ADAM_STEP)
    delta = -ADAM_LR * (m_hat / (jnp.sqrt(v_hat) + ADAM_EPS) + ADAM_WD * w)
    return delta, m, v


def _adamw(w, m, v, g, row_tile, row0=0, outs=None):
    R, C = w.shape
    b0 = row0 // row_tile

    def body(w_ref, m_ref, v_ref, g_ref, _0, _1, _2, _3, g_out, d_out, m_out, v_out):
        gv = g_ref[...]
        delta, mn, vn = _adam_math(w_ref[...], gv, m_ref[...], v_ref[...])
        g_out[...] = gv
        d_out[...] = delta
        m_out[...] = mn
        v_out[...] = vn

    pspec = pl.BlockSpec((row_tile, C), lambda i: (b0 + i, 0))
    gspec = pl.BlockSpec((row_tile, C), lambda i: (i, 0))
    anyspec = pl.BlockSpec(memory_space=pl.ANY)
    shp = jax.ShapeDtypeStruct((R, C), f32)
    if outs is None:
        outs = [lax.empty((R, C), f32) for _ in range(4)]
    return pl.pallas_call(
        body, name="adamw", grid=(g.shape[0] // row_tile,),
        in_specs=[pspec] * 3 + [gspec] + [anyspec] * 4, out_specs=[pspec] * 4, out_shape=[shp] * 4,
        input_output_aliases={4: 0, 5: 1, 6: 2, 7: 3},
        compiler_params=_cp(("parallel",), 40),
    )(*_hbm(w, m, v, g, *outs))


def _pair_sum(g5s, gots, pc):
    n = len(g5s)

    def body(pc_ref, *refs):
        for own, got, out in zip(refs[:n], refs[n:2 * n], refs[2 * n:]):
            out[0, 0] = (own[0, 0, 0].astype(f32) + got[0, 0].astype(f32)).astype(bf16)

    def half(g):
        return pl.BlockSpec((1, 1) + g.shape[-2:], lambda p, pc: (p, 0, 0, 0))

    gs = pltpu.PrefetchScalarGridSpec(
        num_scalar_prefetch=1, grid=(N_CHIPS,),
        in_specs=[pl.BlockSpec((1, 1, 1) + g.shape[-2:], lambda p, pc: (p, 0, pc[1], 0, 0)) for g in g5s]
        + [half(g) for g in gots],
        out_specs=[half(g) for g in gots],
    )
    return pl.pallas_call(
        body, name="pair_sum", grid_spec=gs, out_shape=[jax.ShapeDtypeStruct(g.shape, bf16) for g in gots],
        compiler_params=_cp(("parallel",), 48),
    )(pc, *_hbm(*g5s, *gots))


_SUM_STEPS = 2


def _sum_shards(hsums, recvs, pc):
    n = len(hsums)

    def body(pc_ref, *refs):
        for own, got, out in zip(refs[:n], refs[n:2 * n], refs[2 * n:]):
            acc = own[0, 0].astype(f32)
            for j in range(3):
                acc = acc + got[j, 0].astype(f32)
            out[0, 0] = acc

    def rows(h):
        return (h.shape[2] // _SUM_STEPS, h.shape[3])

    gs = pltpu.PrefetchScalarGridSpec(
        num_scalar_prefetch=1, grid=(_SUM_STEPS,),
        in_specs=[pl.BlockSpec((1, 1) + rows(h), lambda i, pc: (pc[0], 0, i, 0)) for h in hsums]
        + [pl.BlockSpec((3, 1) + rows(h), lambda i, pc: (0, 0, i, 0)) for h in hsums],
        out_specs=[pl.BlockSpec((1, 1) + rows(h), lambda i, pc: (0, pc[1], i, 0)) for h in hsums],
    )
    return pl.pallas_call(
        body, name="sum_shards", grid_spec=gs,
        out_shape=[jax.ShapeDtypeStruct((1, 2) + h.shape[2:], f32) for h in hsums],
        compiler_params=_cp(("parallel",), 48),
    )(pc, *_hbm(*hsums, *recvs))


def _sum8(packs):
    _, R, C = packs.shape
    tr = R // 8 if R % 64 == 0 else R

    def body(p_ref, out_ref):
        acc = p_ref[0]
        for d in range(1, 8):
            acc = acc + p_ref[d]
        out_ref[...] = acc

    return pl.pallas_call(
        body, name="sum8", grid=(R // tr,),
        in_specs=[pl.BlockSpec((8, tr, C), lambda i: (0, i, 0))],
        out_specs=pl.BlockSpec((tr, C), lambda i: (i, 0)),
        out_shape=jax.ShapeDtypeStruct((R, C), f32),
        compiler_params=_cp(("parallel",)),
    )(packs)


def _allgather8(x_shard):
    m_per, n = x_shard.shape

    def body(x_ref, out_ref, send_sems, recv_sems, local_sem):
        x, y, c = lax.axis_index("x"), lax.axis_index("y"), lax.axis_index("c")
        me, sibling = (x, y, c), (x, y, 1 - c)
        chips = [(1 - x, y), (x, 1 - y), (1 - x, 1 - y)]

        def rows(px, py, pc):
            return out_ref.at[pl.ds((4 * px + 2 * py + pc) * m_per, m_per), :]

        def copy(k, block, to, src=None):
            return pltpu.make_async_remote_copy(
                src_ref=rows(*block) if src is None else src, dst_ref=rows(*block),
                send_sem=send_sems.at[k], recv_sem=recv_sems.at[k], device_id=to, device_id_type=MESH)

        mine = pltpu.make_async_copy(x_ref, rows(*me), local_sem)
        mine.start()
        first = [copy(0, me, sibling, src=x_ref)]
        first += [copy(1 + j, me, (*chip, c), src=x_ref) for j, chip in enumerate(chips)]
        for cp in first:
            cp.start()
        passed = [copy(4 + j, (*chip, c), sibling) for j, chip in enumerate(chips)]
        for j, chip in enumerate(chips):
            copy(1 + j, (*chip, c), me).wait_recv()
            passed[j].start()
        copy(0, sibling, me).wait_recv()
        for j, chip in enumerate(chips):
            copy(4 + j, (*chip, 1 - c), me).wait_recv()
        for cp in first + passed:
            cp.wait_send()
        mine.wait()

    return pl.pallas_call(
        body, name="allgather8",
        out_shape=jax.ShapeDtypeStruct((8 * m_per, n), x_shard.dtype),
        in_specs=[pl.BlockSpec(memory_space=pltpu.VMEM)],
        out_specs=pl.BlockSpec(memory_space=pltpu.VMEM),
        scratch_shapes=[pltpu.SemaphoreType.DMA((7,)), pltpu.SemaphoreType.DMA((7,)), pltpu.SemaphoreType.DMA],
        compiler_params=_cp(None, 48),
    )(x_shard)


def _other_chips():
    x, y = lax.axis_index("x"), lax.axis_index("y")
    return [(1 - x, y), (x, 1 - y), (1 - x, 1 - y)]


_HBM = pl.BlockSpec(memory_space=pltpu.HBM)
_SEM = pl.BlockSpec(memory_space=pltpu.SEMAPHORE)
_EFFECT = pltpu.SideEffectType.DATAFLOW_SIDE_EFFECTING


def _gather_copies(srcs, lands, send_sems, recv_sems):
    x, y, c = lax.axis_index("x"), lax.axis_index("y"), lax.axis_index("c")
    return [pltpu.make_async_remote_copy(
        src_ref=srcs[a].at[:, c], dst_ref=lands[a].at[2 * x + y, :, c], send_sem=send_sems.at[3 * a + j],
        recv_sem=recv_sems.at[3 * a + j], device_id=(cx, cy, c), device_id_type=MESH)
        for a in range(len(srcs)) for j, (cx, cy) in enumerate(_other_chips())]


def _gather_start(chunks, after, name):
    sizes = [len(srcs) for srcs, _ in chunks]
    flat = [t for srcs, lands in chunks for t in list(srcs) + list(lands)]
    nflat = len(flat)
    nsem = 2 * len(chunks)

    def body(*refs):
        ins, sems, token = refs[:nflat], refs[nflat + 1:nflat + 1 + nsem], refs[-1]
        off = 0
        for k, n in enumerate(sizes):
            for cp in _gather_copies(ins[off:off + n], ins[off + n:off + 2 * n], sems[2 * k], sems[2 * k + 1]):
                cp.start()
            off += 2 * n
        token[...] = jnp.zeros_like(token)

    res = pl.pallas_call(
        body, name=name,
        out_shape=[pltpu.SemaphoreType.DMA((3 * n,)) for n in sizes for _ in range(2)]
        + [pltpu.HBM(t.shape, t.dtype) for t in flat] + [jax.ShapeDtypeStruct((8, 128), f32)],
        in_specs=[_HBM] * nflat + [pl.BlockSpec(memory_space=pl.ANY)],
        out_specs=[_SEM] * nsem + [_HBM] * nflat + [pl.BlockSpec(memory_space=pltpu.VMEM)],
        input_output_aliases={i: nsem + i for i in range(nflat)},
        compiler_params=pltpu.CompilerParams(has_side_effects=_EFFECT),
    )(*[pltpu.with_memory_space_constraint(t, pltpu.HBM) for t in flat], after)
    out, off = [], nsem
    for k, n in enumerate(sizes):
        out.append((res[2 * k], res[2 * k + 1], res[off:off + n], res[off + n:off + 2 * n]))
        off += 2 * n
    return out, res[-1]


def _gather_wait(send_sems, recv_sems, srcs, lands, after, name):
    n = len(srcs)

    def body(*refs):
        for cp in _gather_copies(refs[:n], refs[n:2 * n], refs[2 * n], refs[2 * n + 1]):
            cp.wait_send()
            cp.wait_recv()

    res = pl.pallas_call(
        body, name=name,
        out_shape=[pltpu.HBM(t.shape, t.dtype) for t in list(srcs) + list(lands)],
        in_specs=[_HBM] * (2 * n) + [_SEM, _SEM] + [pl.BlockSpec(memory_space=pl.ANY)] * len(after),
        out_specs=[_HBM] * (2 * n),
        input_output_aliases={i: i for i in range(2 * n)},
        compiler_params=pltpu.CompilerParams(has_side_effects=_EFFECT),
    )(*srcs, *lands, send_sems, recv_sems, *after)
    return res[n:]


def _split_start(make_copies, arrays, nsem, name, after=()):
    n, na = len(arrays), len(after)

    def body(*refs):
        for cp in make_copies(refs[:n], refs[n + na], refs[n + na + 1]):
            cp.start()
        refs[-1][...] = jnp.zeros_like(refs[-1])

    res = pl.pallas_call(
        body, name=name,
        out_shape=[pltpu.SemaphoreType.DMA((nsem,)), pltpu.SemaphoreType.DMA((nsem,))]
        + [pltpu.HBM(t.shape, t.dtype) for t in arrays] + [jax.ShapeDtypeStruct((8, 128), f32)],
        in_specs=[_HBM] * n + [pl.BlockSpec(memory_space=pl.ANY)] * na,
        out_specs=[_SEM, _SEM] + [_HBM] * n + [pl.BlockSpec(memory_space=pltpu.VMEM)],
        input_output_aliases={i: i + 2 for i in range(n)},
        compiler_params=pltpu.CompilerParams(has_side_effects=_EFFECT),
    )(*[pltpu.with_memory_space_constraint(t, pltpu.HBM) for t in arrays], *after)
    return (res[0], res[1], res[2:2 + n]), res[-1]


def _split_wait(make_copies, send_sems, recv_sems, arrays, after, name):
    n = len(arrays)

    def body(*refs):
        for cp in make_copies(refs[:n], refs[n], refs[n + 1]):
            cp.wait_send()
            cp.wait_recv()

    return pl.pallas_call(
        body, name=name,
        out_shape=[pltpu.HBM(t.shape, t.dtype) for t in arrays],
        in_specs=[_HBM] * n + [_SEM, _SEM] + [pl.BlockSpec(memory_space=pl.ANY)] * len(after),
        out_specs=[_HBM] * n, input_output_aliases={i: i for i in range(n)},
        compiler_params=pltpu.CompilerParams(has_side_effects=_EFFECT),
    )(*arrays, send_sems, recv_sems, *after)


def _sibling():
    return lax.axis_index("x"), lax.axis_index("y"), 1 - lax.axis_index("c")


def _forward_copies(lands, send_sems, recv_sems):
    c = lax.axis_index("c")
    return [pltpu.make_async_remote_copy(
        src_ref=lands[a].at[2 * cx + cy, :, c], dst_ref=lands[a].at[2 * cx + cy, :, c], send_sem=send_sems.at[3 * a + j],
        recv_sem=recv_sems.at[3 * a + j], device_id=_sibling(), device_id_type=MESH)
        for a in range(len(lands)) for j, (cx, cy) in enumerate(_other_chips())]


def _pair_copies(refs, send_sems, recv_sems):
    n = len(refs) // 2
    c = lax.axis_index("c")
    return [pltpu.make_async_remote_copy(src_ref=refs[a].at[:, :, 1 - c], dst_ref=refs[n + a], send_sem=send_sems.at[a],
                                         recv_sem=recv_sems.at[a], device_id=_sibling(), device_id_type=MESH)
            for a in range(n)]


def _scatter_copies(srcs, lands, send_sems, recv_sems):
    c = lax.axis_index("c")
    return [pltpu.make_async_remote_copy(
        src_ref=srcs[a].at[2 * cx + cy], dst_ref=lands[a].at[j], send_sem=send_sems.at[3 * a + j],
        recv_sem=recv_sems.at[3 * a + j], device_id=(cx, cy, c), device_id_type=MESH)
        for a in range(len(srcs)) for j, (cx, cy) in enumerate(_other_chips())]


def _scatter_start(hsums, name, after=()):
    n = len(hsums)
    na = len(after)

    def body(*refs):
        srcs, lands = refs[:n], refs[n:2 * n]
        send_sems, recv_sems = refs[2 * n + na], refs[2 * n + na + 1]
        for cp in _scatter_copies(srcs, lands, send_sems, recv_sems):
            cp.start()
        refs[-1][...] = jnp.zeros_like(refs[-1])

    lands = [lax.empty((3,) + g.shape[1:], g.dtype) for g in hsums]
    res = pl.pallas_call(
        body, name=name,
        out_shape=[pltpu.SemaphoreType.DMA((3 * n,)), pltpu.SemaphoreType.DMA((3 * n,))]
        + [pltpu.HBM(g.shape, g.dtype) for g in hsums] + [pltpu.HBM(g.shape, g.dtype) for g in lands]
        + [jax.ShapeDtypeStruct((8, 128), f32)],
        in_specs=[_HBM] * (2 * n) + [pl.BlockSpec(memory_space=pl.ANY)] * na,
        out_specs=[_SEM, _SEM] + [_HBM] * (2 * n) + [pl.BlockSpec(memory_space=pltpu.VMEM)],
        input_output_aliases={i: i + 2 for i in range(2 * n)},
        compiler_params=pltpu.CompilerParams(has_side_effects=_EFFECT),
    )(*[pltpu.with_memory_space_constraint(t, pltpu.HBM) for t in list(hsums) + lands], *after)
    return (res[0], res[1], res[2:2 + n], res[2 + n:2 + 2 * n]), res[-1]


def _scatter_wait(send_sems, recv_sems, srcs, lands, after, name):
    n = len(srcs)
    extra = list(after)

    def body(*refs):
        s_refs, l_refs = refs[:n], refs[n:2 * n]
        ss, rs = refs[2 * n], refs[2 * n + 1]
        for cp in _scatter_copies(s_refs, l_refs, ss, rs):
            cp.wait_send()
            cp.wait_recv()

    res = pl.pallas_call(
        body, name=name,
        out_shape=[pltpu.HBM(g.shape, g.dtype) for g in srcs] + [pltpu.HBM(g.shape, g.dtype) for g in lands],
        in_specs=[_HBM] * (2 * n) + [_SEM, _SEM] + [pl.BlockSpec(memory_space=pl.ANY)] * len(extra),
        out_specs=[_HBM] * (2 * n),
        input_output_aliases={i: i for i in range(2 * n)},
        compiler_params=pltpu.CompilerParams(has_side_effects=_EFFECT),
    )(*srcs, *lands, send_sems, recv_sems, *extra)
    return res[:n], res[n:]


def _plane_copies(src, land, send_sems, recv_sems):
    x, y, c = lax.axis_index("x"), lax.axis_index("y"), lax.axis_index("c")
    return [pltpu.make_async_remote_copy(src_ref=src, dst_ref=land.at[2 * x + y, c], send_sem=send_sems.at[j],
                                         recv_sem=recv_sems.at[j], device_id=(cx, cy, c), device_id_type=MESH)
            for j, (cx, cy) in enumerate(_other_chips())]


def _plane_start(pack, land, name):
    def body(src, lnd, send_sems, recv_sems, _s, _l, token):
        for cp in _plane_copies(src, lnd, send_sems, recv_sems):
            cp.start()
        token[...] = jnp.zeros_like(token)

    res = pl.pallas_call(
        body, name=name,
        out_shape=[pltpu.SemaphoreType.DMA((3,)), pltpu.SemaphoreType.DMA((3,)), pltpu.HBM(pack.shape, pack.dtype),
                   pltpu.HBM(land.shape, land.dtype), jax.ShapeDtypeStruct((8, 128), f32)],
        in_specs=[_HBM, _HBM], out_specs=[_SEM, _SEM, _HBM, _HBM, pl.BlockSpec(memory_space=pltpu.VMEM)],
        input_output_aliases={0: 2, 1: 3},
        compiler_params=pltpu.CompilerParams(has_side_effects=_EFFECT),
    )(pltpu.with_memory_space_constraint(pack, pltpu.HBM), pltpu.with_memory_space_constraint(land, pltpu.HBM))
    return res[:4], res[4]


def _plane_wait(send_sems, recv_sems, pack, land, after, name):
    def body(src, lnd, ss, rs, *_):
        for cp in _plane_copies(src, lnd, ss, rs):
            cp.wait_send()
            cp.wait_recv()

    return pl.pallas_call(
        body, name=name,
        out_shape=[pltpu.HBM(pack.shape, pack.dtype), pltpu.HBM(land.shape, land.dtype)],
        in_specs=[_HBM, _HBM, _SEM, _SEM] + [pl.BlockSpec(memory_space=pl.ANY)] * len(after),
        out_specs=[_HBM, _HBM], input_output_aliases={0: 0, 1: 1},
        compiler_params=pltpu.CompilerParams(has_side_effects=_EFFECT),
    )(pack, land, send_sems, recv_sems, *after)[1]


def _swap_halves(fulls):
    n = len(fulls)

    def body(*refs):
        ins, outs = refs[:n], refs[n:2 * n]
        send_sems, recv_sems = refs[2 * n:]
        c = lax.axis_index("c")
        sibling = (lax.axis_index("x"), lax.axis_index("y"), 1 - c)
        copies = []
        for a in range(n):
            cp = pltpu.make_async_remote_copy(src_ref=outs[a].at[:, c], dst_ref=outs[a].at[:, c], send_sem=send_sems.at[a],
                                              recv_sem=recv_sems.at[a], device_id=sibling, device_id_type=MESH)
            cp.start()
            copies.append(cp)
        for a, cp in enumerate(copies):
            cp.wait_send()
            theirs = outs[a].at[:, 1 - c]
            pltpu.make_async_remote_copy(src_ref=theirs, dst_ref=theirs, send_sem=send_sems.at[a], recv_sem=recv_sems.at[a],
                                         device_id=sibling, device_id_type=MESH).wait_recv()

    hbm = pl.BlockSpec(memory_space=pl.ANY)
    return pl.pallas_call(
        body, name="swap_halves",
        out_shape=[jax.ShapeDtypeStruct(p.shape, p.dtype) for p in fulls],
        in_specs=[hbm] * n, out_specs=[hbm] * n,
        input_output_aliases={a: a for a in range(n)},
        scratch_shapes=[pltpu.SemaphoreType.DMA((n,)), pltpu.SemaphoreType.DMA((n,))],
    )(*fulls)


def _to_segments(t):
    s, c = t.shape
    return t.reshape(SCAN_SEG, s // SCAN_SEG, c).transpose(1, 0, 2).reshape(s, c)


def _from_segments(t):
    s, c = t.shape
    return t.reshape(s // SCAN_SEG, SCAN_SEG, c).transpose(1, 0, 2).reshape(s, c)


def _ssm_operators(a_re, a_im, log_dt, b_re, b_im, c_re, c_im):
    lam = lax.complex(a_re, a_im)
    dt = jnp.exp(log_dt)[:, None]
    a_bar = jnp.exp(lam * dt)
    b_bar = ((a_bar - 1.0) / lam)[:, :, None] * lax.complex(b_re, b_im)
    eye = jnp.eye(N_GROUPS, dtype=f32)

    def embed_b(t):
        return (jnp.transpose(t, (0, 2, 1))[:, :, None, :] * eye[:, None, :, None]).reshape(D_SSM, D_STATE)

    def embed_c(t):
        return (jnp.transpose(t, (0, 2, 1))[:, :, None, :] * eye[:, None, :, None]).reshape(D_STATE, D_SSM)

    a2 = jnp.stack([a_bar.real.reshape(D_STATE), a_bar.imag.reshape(D_STATE)])
    return a2, embed_b(b_bar.real), embed_b(b_bar.imag), embed_c(c_re), embed_c(c_im)


def _local_step(x, target, mod, small, ffn_weights, mix_weights, grads_done, ffn_bwd_issued):
    table = jnp.asarray(_bucket_table())
    bias = _bias_fwd(small["rel_bias"], table)
    L = DEPTH
    saved = []
    ssm_ops = []
    for l in range(L):
        sv = {}
        m9 = mod[l]
        sv["x0"] = x
        sv["w0"] = ffn_weights(l, 0, x)
        x, sv["f0"], sv["g0"], sv["u0"], sv["h0"] = _ffn_fwd(x, m9[0:3], *sv["w0"], 0, small["ln_g"][l, 0:1], small["ln_b"][l, 0:1])
        sv["x1"] = x
        sv["w1"] = mix_weights(l, x)
        *qkv, z_rest, sv["h1"] = _mix_in_fwd(x, m9[3:6], sv["w1"][0], 0)
        S = x.shape[0]
        qkv = [t.reshape(3, S, D_ATT) for t in qkv]
        att = [_att_fwd(qkv[b], bias, b) for b in range(3)]
        y_att, lse3 = _att_merge([att[b][0].reshape(d, S // d, D_ATT) for b, d in enumerate(DILATIONS)],
                                 [att[b][1].reshape(d, S // d, _LANES) for b, d in enumerate(DILATIONS)])
        sv.update(qkv=qkv, lse=[a[1] for a in att], lse3=lse3, y_att=y_att)

        prm = tuple(small[k][l] for k in ("ssm_a_re", "ssm_a_im", "ssm_log_dt", "ssm_b_re", "ssm_b_im", "ssm_c_re", "ssm_c_im"))
        (a2, bre, bim, cre, cim), ops_vjp = jax.vjp(_ssm_operators, *prm)
        ssm_ops.append(ops_vjp)
        u_ssm = _to_segments(z_rest[:, :D_SSM])
        sr, si = _ssm_states(u_ssm, bre, bim, a2)
        dskip = small["ssm_d"][l][None, :]
        glu_b = small["glu_b"][l][None, :]
        out_seg, y_seg = _ssm_out(sr, si, u_ssm, cre, cim, dskip, small["glu_w"][l], glu_b)
        y_ssm = _from_segments(out_seg)
        sv.update(a2=a2, bre=bre, bim=bim, cre=cre, cim=cim, u_ssm=u_ssm, sr=sr, si=si, y_seg=y_seg, y_ssm=y_ssm)

        u_pool = jnp.concatenate([jnp.zeros((POOL_HALO, D_POOL), f32), z_rest[:, D_SSM:]])
        y_pool = _pool_fwd(u_pool, small["pool_w"][l], small["pool_scale"][l][None, :])
        sv.update(u_pool=u_pool, y_pool=y_pool)

        x, sv["ymix"] = _mix_out_fwd(x, y_att, y_ssm, y_pool, m9[3:6], sv["w1"][1], 0, small["ln_g"][l, 1:2], small["ln_b"][l, 1:2])
        sv["x2"] = x
        sv["w2"] = ffn_weights(l, 1, x)
        x, sv["f2"], sv["g2"], sv["u2"], sv["h2"] = _ffn_fwd(x, m9[6:9], *sv["w2"], 0, small["ln_g"][l, 2:3], small["ln_b"][l, 2:3])
        saved.append(sv)

    loss, dx = _loss_head(x, target)

    dmod = [None] * L
    dln_g = [None] * L
    dln_b = [None] * L
    sg = {k: [None] * L for k in ("ssm_a_re", "ssm_a_im", "ssm_log_dt", "ssm_b_re", "ssm_b_im", "ssm_c_re", "ssm_c_im",
                                  "ssm_d", "glu_w", "glu_b", "pool_w", "pool_scale")}
    dbias_tot = None
    order_after = jnp.zeros((), f32)
    for l in reversed(range(L)):
        sv = saved[l]
        m9 = mod[l] + order_after

        def fresh(like):
            return [lax.empty(t.shape, bf16) for t in like]

        dx, dg, du, a, df, dm2, dlg2, dlb2 = _ffn_bwd(dx, sv["x2"], sv["f2"], sv["g2"], sv["u2"], m9[6:9], *sv["w2"], 0,
                                                     small["ln_g"][l, 2:3])
        m9 = m9 + ffn_bwd_issued(l, 1, dx)
        g_ffn1 = _ffn_wgrad(sv["h2"], dg, du, a, df, *fresh(sv["w2"]), 0)
        dxr, d_att, d_ssm, d_pool, dgate1, dlg1, dlb1, g_w_out = _mix_out_bwd(
            dx, sv["x1"], sv["ymix"], sv["y_att"], sv["y_ssm"], sv["y_pool"], m9[3:6], sv["w1"][1], 0, small["ln_g"][l, 1:2],
            fresh(sv["w1"])[1])
        S = d_att.shape[0]
        merged = _att_merge_bwd(d_att, sv["y_att"], sv["lse3"])
        dqkv, dbias = [], []
        for b, d in enumerate(DILATIONS):
            dq_b, db_b = _att_bwd(sv["qkv"][b], merged[b].reshape(S, D_ATT), sv["lse"][b], merged[3 + b].reshape(S, _LANES), bias, b)
            dqkv.append(dq_b.reshape(3, d, S // d, D_ATT))
            dbias.append(db_b)
        dbias = jnp.stack(dbias)
        dbias_tot = dbias if dbias_tot is None else dbias_tot + dbias
        d_seg = _to_segments(d_ssm)
        dskip = small["ssm_d"][l][None, :]
        glu_b = small["glu_b"][l][None, :]
        dy_seg, du_skip, dcre, dcim, dd, dglu_b, dglu_w = _ssm_out_bwd(
            d_seg, sv["y_seg"], sv["u_ssm"], sv["sr"], sv["si"], dskip, small["glu_w"][l], glu_b)
        du_seg, dbre, dbim, da2 = _ssm_states_bwd(dy_seg, du_skip, sv["u_ssm"], sv["sr"], sv["si"], sv["cre"], sv["cim"],
                                                  sv["bre"], sv["bim"], sv["a2"])
        d_prm = ssm_ops[l]((da2, dbre, dbim, dcre, dcim))
        for k, v in zip(("ssm_a_re", "ssm_a_im", "ssm_log_dt", "ssm_b_re", "ssm_b_im", "ssm_c_re", "ssm_c_im"), d_prm):
            sg[k][l] = v
        sg["ssm_d"][l] = dd[0]
        sg["glu_b"][l] = dglu_b[0]
        sg["glu_w"][l] = dglu_w
        du_ssm = _from_segments(du_seg)
        dyp = jnp.concatenate([d_pool, jnp.zeros((POOL_HALO, D_POOL), f32)])
        du_pool, dpw, dps = _pool_bwd(dyp, sv["u_pool"], small["pool_w"][l], small["pool_scale"][l][None, :])
        sg["pool_w"][l] = dpw
        sg["pool_scale"][l] = dps[0]
        d_rest = jnp.concatenate([du_ssm, du_pool], axis=1).astype(bf16)
        dx, dm1, dz = _mix_in_bwd(dqkv, d_rest, dxr, sv["x1"], m9[3:6], sv["w1"][0], 0)
        g_w_in = _mix_in_wgrad(sv["h1"], dz, fresh(sv["w1"])[0], 0)
        ffn_names = ("ffn_w_gate", "ffn_w_up", "ffn_w_down")
        m9 = m9 + grads_done(l, 1, list(zip(ffn_names, [(2 * l + 1) * FF_SHARD] * 3, g_ffn1))
                             + [("w_in", l * D_MODEL, g_w_in), ("w_out", l * 256, g_w_out)])
        dm1 = jnp.concatenate([dm1[0:2], dgate1])
        dx, dg, du, a, df, dm0, dlg0, dlb0 = _ffn_bwd(dx, sv["x0"], sv["f0"], sv["g0"], sv["u0"], m9[0:3], *sv["w0"], 0,
                                                     small["ln_g"][l, 0:1])
        issued = ffn_bwd_issued(l, 0, dx)
        g_ffn0 = _ffn_wgrad(sv["h0"], dg, du, a, df, *fresh(sv["w0"]), 0)
        order_after = grads_done(l, 0, list(zip(ffn_names, [2 * l * FF_SHARD] * 3, g_ffn0))) + issued
        dmod[l] = jnp.concatenate([dm0 + issued, dm1, dm2])
        dln_g[l] = jnp.concatenate([dlg0, dlg1, dlg2])
        dln_b[l] = jnp.concatenate([dlb0, dlb1, dlb2])

    small_grads = {k: jnp.stack(v) for k, v in sg.items()}
    small_grads["rel_bias"] = _bias_bwd(dbias_tot, table)
    small_grads["ln_g"] = jnp.stack(dln_g)
    small_grads["ln_b"] = jnp.stack(dln_b)
    return loss, dx, jnp.stack(dmod), small_grads


_TILE_ELEMS = 8 * 128


def _pack_rows(shapes):
    out, row = [], 0
    for s in shapes:
        nr = -(-int(np.prod(s)) // _TILE_ELEMS) * 8
        out.append((row, nr))
        row += nr
    return out


def _pack(arrs):
    parts = []
    for a in arrs:
        flat = a.reshape(-1).astype(f32)
        npad = -(-flat.shape[0] // _TILE_ELEMS) * _TILE_ELEMS
        parts.append(jnp.pad(flat, (0, npad - flat.shape[0])).reshape(npad // 128, 128))
    return jnp.concatenate(parts, axis=0)


def _unpack(buf, shapes):
    return [buf[row:row + nr].reshape(-1)[:int(np.prod(s))].reshape(s) for s, (row, nr) in zip(shapes, _pack_rows(shapes))]


_REPL = ("rel_bias", "ada_b", "ssm_a_re", "ssm_a_im", "ssm_log_dt", "ssm_b_re", "ssm_b_im", "ssm_c_re", "ssm_c_im",
         "ssm_d", "glu_b", "pool_w", "pool_scale")
_SMALL_SHARDED = ("ln_g", "ln_b", "glu_w")
_BIG = ("ffn_w_gate", "ffn_w_up", "ffn_w_down", "w_in", "w_out")
_ORDER = ("rel_bias", "ada_w", "ada_b", "ln_g", "ln_b", "ffn_w_gate", "ffn_w_up", "ffn_w_down", "w_in", "w_out",
          "ssm_a_re", "ssm_a_im", "ssm_log_dt", "ssm_b_re", "ssm_b_im", "ssm_c_re", "ssm_c_im", "ssm_d", "glu_w",
          "glu_b", "pool_w", "pool_scale")


def kernel(x, c, rel_bias, ada_w, ada_b, ln_g, ln_b, ffn_w_gate, ffn_w_up, ffn_w_down, w_in, w_out, ssm_a_re, ssm_a_im, ssm_log_dt, ssm_b_re, ssm_b_im, ssm_c_re, ssm_c_im, ssm_d, glu_w, glu_b, pool_w, pool_scale, loss_target, m_rel_bias, m_ada_w, m_ada_b, m_ln_g, m_ln_b, m_ffn_w_gate, m_ffn_w_up, m_ffn_w_down, m_w_in, m_w_out, m_ssm_a_re, m_ssm_a_im, m_ssm_log_dt, m_ssm_b_re, m_ssm_b_im, m_ssm_c_re, m_ssm_c_im, m_ssm_d, m_glu_w, m_glu_b, m_pool_w, m_pool_scale, v_rel_bias, v_ada_w, v_ada_b, v_ln_g, v_ln_b, v_ffn_w_gate, v_ffn_w_up, v_ffn_w_down, v_w_in, v_w_out, v_ssm_a_re, v_ssm_a_im, v_ssm_log_dt, v_ssm_b_re, v_ssm_b_im, v_ssm_c_re, v_ssm_c_im, v_ssm_d, v_glu_w, v_glu_b, v_pool_w, v_pool_scale):
    args = dict(locals())
    w = {k: args[k] for k in _ORDER}
    m = {k: args["m_" + k] for k in _ORDER}
    v = {k: args["v_" + k] for k in _ORDER}
    L, D = DEPTH, D_MODEL
    ax, ay, ac = lax.axis_index("x"), lax.axis_index("y"), lax.axis_index("c")
    p_me = 2 * ax + ay
    dev = 4 * ax + 2 * ay + ac

    transposed = ("ffn_w_gate", "ffn_w_up")
    for d in (w, m, v):
        for name in transposed:
            d[name] = jnp.swapaxes(d[name], 2, 3)

    def halves(t):
        return t.astype(bf16).reshape(1, 2, t.shape[0] // 2, t.shape[1])

    def landing(src):
        return lax.dynamic_update_slice(lax.empty((N_CHIPS,) + src.shape, bf16), src[None], (p_me, 0, 0, 0, 0))

    chunk_keys = [("ffn", 0, 0), ("mix", 0), ("ffn", 0, 1), ("ffn", 1, 0), ("mix", 1), ("ffn", 1, 1)]
    chunk_srcs = []
    for key in chunk_keys:
        if key[0] == "ffn":
            chunk_srcs.append([halves(w[name][key[1], key[2]]) for name in ("ffn_w_gate", "ffn_w_up", "ffn_w_down")])
        else:
            chunk_srcs.append([halves(w_in[key[1]]), halves(w_out[key[1]])])

    pack = _pack([c, ln_g, ln_b, glu_w])
    rows = pack.shape[0]
    allp = _allgather8(pack).reshape(8, rows, 128)
    chunks = [(srcs, [landing(t) for t in srcs]) for srcs in chunk_srcs]
    first_in_flight, first_begun = _gather_start(chunks[:1], allp, "gather_start_first")
    c_all = allp[:, :8].reshape(8, D) + first_begun[0, 0]
    by_chip = allp[0::2]

    fwd_rows = _pack_rows([c.shape, ln_g.shape, ln_b.shape, glu_w.shape])

    def sharded(part, shape, axis):
        row0, nrows = fwd_rows[part]
        t = by_chip[:, row0:row0 + nrows].reshape(N_CHIPS, -1)[:, :int(np.prod(shape))].reshape((N_CHIPS,) + shape)
        return jnp.concatenate([t[p] for p in range(N_CHIPS)], axis=axis)

    ln_g_full = sharded(1, ln_g.shape, 2)
    ln_b_full = sharded(2, ln_b.shape, 2)
    glu_w_full = sharded(3, glu_w.shape, 1)

    ncol = ada_w.shape[-1]
    ada_b_cols = lax.dynamic_slice_in_dim(ada_b, p_me * ncol, ncol, axis=1)[:, None, :]
    mod_part = _ada_fwd(c_all, ada_w, ada_b_cols)
    mrows = L * 8 * ncol // 128
    mod_all = _allgather8(mod_part.reshape(mrows, 128)).reshape(8, L, 8, ncol)
    mod_mine = lax.dynamic_index_in_dim(mod_all, dev, axis=2, keepdims=False)
    mod = jnp.concatenate([mod_mine[2 * p] for p in range(N_CHIPS)], axis=-1).reshape(L, 9, D)

    rest_in_flight, rest_begun = _gather_start(chunks[1:], mod, "gather_start_rest")
    in_flight = first_in_flight + rest_in_flight

    forwarding = {}

    def forward(k, after):
        lands = _gather_wait(*in_flight[k], [after, rest_begun], "gather_wait_%d" % k)
        forwarding[k], begun = _split_start(_forward_copies, lands, 3 * len(lands), "gather_forward_start_%d" % k)
        return begun

    def gathered(key, after):
        k = chunk_keys.index(key)
        order = [after]
        if k not in forwarding:
            order.append(forward(k, after))
        if k + 1 < len(chunk_keys):
            order.append(forward(k + 1, after))
        lands = _split_wait(_forward_copies, *forwarding[k], order, "gather_forward_wait_%d" % k)
        return [t.reshape(N_CHIPS, 1, 2 * t.shape[3], t.shape[4]) for t in lands]

    pc = jnp.stack([p_me, ac]).astype(jnp.int32)
    groups = {}
    scattering = {}

    pairing = {}

    def start_pairs(tag, after=()):
        g5 = [g.reshape(g.shape[:2] + (2, g.shape[2] // 2, g.shape[3])) for _, _, g in groups[tag]]
        gots = [lax.empty(g.shape[:2] + g.shape[3:], bf16) for g in g5]
        pairing[tag], begun = _split_start(_pair_copies, g5 + gots, len(g5), "pair_exchange_start_%s" % tag, after)
        return begun

    def start_group(tag, after):
        arrays = _split_wait(_pair_copies, *pairing[tag], after, "pair_exchange_wait_%s" % tag)
        n = len(arrays) // 2
        hsum = _pair_sum(arrays[:n], arrays[n:], pc)
        scattering[tag], begun = _scatter_start(hsum, "scatter_start_%s" % tag)
        return begun

    def grads_done(l, s, grads):
        if l == 1:
            groups.setdefault("l1", []).extend(grads)
            return start_pairs("l1")[0, 0] if s == 0 else jnp.zeros((), f32)
        groups["l0a" if s == 1 else "l0b"] = grads
        return start_pairs("l0a")[0, 0] if s == 1 else jnp.zeros((), f32)

    def ffn_bwd_issued(l, s, dx):
        if l == 0:
            return start_group("l1" if s == 1 else "l0a", [dx])[0, 0]
        return jnp.zeros((), f32)

    small = {k: w[k] for k in _REPL if k != "ada_b"}
    small.update(ln_g=ln_g_full, ln_b=ln_b_full, glu_w=glu_w_full)
    loss_dev, grad_x, dmod, sgrads = _local_step(
        x[0], loss_target[0], mod, small, lambda l, s, after: gathered(("ffn", l, s), after),
        lambda l, after: gathered(("mix", l), after), grads_done, ffn_bwd_issued)
    loss = lax.psum(loss_dev[0, 0], ("x", "y", "c"))

    names = ("rel_bias", "ln_g", "ln_b", "ssm_a_re", "ssm_a_im", "ssm_log_dt", "ssm_b_re", "ssm_b_im", "ssm_c_re",
             "ssm_c_im", "ssm_d", "glu_w", "glu_b", "pool_w", "pool_scale")
    gpack = _pack([dmod] + [sgrads[k] for k in names])
    grows = gpack.shape[0]
    land = lax.dynamic_update_slice(lax.empty((N_CHIPS, 2, grows, 128), f32), gpack[None, None], (p_me, ac, 0, 0))
    small_in_flight, small_begun = _plane_start(gpack, land, "small_grads_start")
    l0b_begun = start_group("l0b", [start_pairs("l0b", (small_begun,))])

    out_g, out_d, out_m, out_v = {}, {}, {}, {}
    row_tile = dict(zip(_BIG, (352, 352, 352, 256, 256)))
    big = {name: None for name in _BIG}

    def finish_group(tag, after):
        hsum, recv = _scatter_wait(*scattering[tag], after, "scatter_wait_%s" % tag)
        full = _swap_halves(_sum_shards(hsum, recv, pc))
        for (name, row0, _), g in zip(groups[tag], full):
            shp = w[name].shape
            r2 = (int(np.prod(shp[:-1])), shp[-1])
            big[name] = _adamw(w[name].reshape(r2), m[name].reshape(r2), v[name].reshape(r2), g.reshape(-1, shp[-1]),
                               row_tile[name], row0, big[name])
        return [big[name][1] for name, _, _ in groups[tag]]

    after = finish_group("l1", [grad_x, l0b_begun])

    land = _plane_wait(*small_in_flight, after, "small_grads_wait")
    gall = _swap_halves([land])[0].reshape(8, grows, 128)
    gsum = _unpack(_sum8(gall), [(L, 9 * D)] + [sgrads[k].shape for k in names])
    red = dict(zip(("ada_b",) + names, gsum))
    red["ln_g"] = lax.dynamic_slice_in_dim(red["ln_g"], p_me * 256, 256, axis=2)
    red["ln_b"] = lax.dynamic_slice_in_dim(red["ln_b"], p_me * 256, 256, axis=2)
    red["glu_w"] = lax.dynamic_slice_in_dim(red["glu_w"], p_me * 64, 64, axis=1)

    dmod_all = gall[:, :L * 9 * D // 128].reshape(8, L, 9 * D)
    dmod_cols = jnp.transpose(lax.dynamic_slice_in_dim(dmod_all, p_me * ncol, ncol, axis=2), (1, 0, 2))
    g_ada_w = _ada_wgrad(jnp.transpose(c_all), dmod_cols)

    r2 = (L * D, ncol)
    res = _adamw(ada_w.reshape(r2), m["ada_w"].reshape(r2), v["ada_w"].reshape(r2), g_ada_w.reshape(r2), 128)
    out_g["ada_w"], out_d["ada_w"], out_m["ada_w"], out_v["ada_w"] = [t.reshape(ada_w.shape) for t in res]

    small_names = _REPL + _SMALL_SHARDED
    wp = _pack([w[k] for k in small_names])
    res_small = _adamw(wp, _pack([m[k] for k in small_names]), _pack([v[k] for k in small_names]),
                       _pack([red[k] for k in small_names]), wp.shape[0])
    for t, dst in zip(res_small, (out_g, out_d, out_m, out_v)):
        for k, a in zip(small_names, _unpack(t, [w[k].shape for k in small_names])):
            dst[k] = a

    finish_group("l0b", finish_group("l0a", [res_small[1], res[1]]))
    for name in _BIG:
        res = [t.reshape(w[name].shape) for t in big[name]]
        out_g[name], out_d[name], out_m[name], out_v[name] = [jnp.swapaxes(t, 2, 3) for t in res] if name in transposed else res

    return (loss, grad_x[None], *[out_g[k] for k in _ORDER], *[out_d[k] for k in _ORDER],
            *[out_m[k] for k in _ORDER], *[out_v[k] for k in _ORDER])
```

```python
import functools
import math

import numpy as np
import jax
import jax.numpy as jnp
from jax import lax
from jax.experimental import pallas as pl
from jax.experimental.pallas import tpu as pltpu

f32 = jnp.float32
bf16 = jnp.bfloat16
MESH = pl.DeviceIdType.MESH

D_MODEL = 1024
SEQ = 2048
DEPTH = 2
HEAD_DIM = 64
N_HEADS = 8
D_ATT = 512
DILATIONS = (1, 4, 16)
BLOCKS_PER_RESIDUE = (16, 4, 1)
ATT_BLOCK = 128
N_UNITS = SEQ // ATT_BLOCK
N_GROUPS = 16
SSM_GROUP = 16
SSM_STATE = 64
D_SSM = 256
D_STATE = N_GROUPS * SSM_STATE
POOL_WINDOWS = (2, 4, 8, 16)
POOL_GROUP = 64
D_POOL = 256
POOL_HALO = 16
D_FF = 2816
N_BUCKETS = 32
MAX_DISTANCE = 2048
ALPHA = (2 * DEPTH) ** 0.25
FFN_RES = 0.5
LN_EPS = 1e-5
NEG = -1e30
N_CHIPS = 4
FF_SHARD = D_FF // N_CHIPS
SCAN_SEG = 8
SCAN_STEPS = SEQ // SCAN_SEG

ADAM_LR, ADAM_B1, ADAM_B2, ADAM_EPS, ADAM_WD, ADAM_STEP = 0.001, 0.9, 0.999, 1e-08, 0.01, 10

TOK_TILE = 512


def _cp(dims=None, vmem_mb=None):
    kw = {}
    if dims is not None:
        kw["dimension_semantics"] = dims
    if vmem_mb is not None:
        kw["vmem_limit_bytes"] = vmem_mb << 20
    return pltpu.CompilerParams(**kw)


def _dot(a, b):
    return jnp.dot(a, b, preferred_element_type=f32)


def _dot_nt(a, b):
    return lax.dot_general(a, b, (((1,), (1,)), ((), ())), preferred_element_type=f32)


def _dot_tn(a, b):
    return lax.dot_general(a, b, (((0,), (0,)), ((), ())), preferred_element_type=f32)


def _ln_stats(v):
    mu = jnp.mean(v, -1, keepdims=True)
    d = v - mu
    var = jnp.mean(d * d, -1, keepdims=True)
    rstd = lax.rsqrt(var + LN_EPS)
    return d * rstd, rstd


def _ln_bwd(dxh, xh, rstd):
    return rstd * (dxh - jnp.mean(dxh, -1, keepdims=True) - xh * jnp.mean(dxh * xh, -1, keepdims=True))


_GELU_C = math.sqrt(2.0 / math.pi)


def _gelu(y):
    return 0.5 * y * (1.0 + jnp.tanh(_GELU_C * (y + 0.044715 * y * y * y)))


def _gelu_grad(y):
    t = jnp.tanh(_GELU_C * (y + 0.044715 * y * y * y))
    return 0.5 * (1.0 + t) + 0.5 * y * (1.0 - t * t) * (_GELU_C * (1.0 + 3 * 0.044715 * y * y))


def _full(shape):
    return pl.BlockSpec(shape, lambda *_: (0,) * len(shape))


def _hbm(*args):
    return [pltpu.with_memory_space_constraint(a, pltpu.HBM) if getattr(a, "ndim", 0) >= 2 else a for a in args]


def _ffn_fwd(x, mod3, wg, wu, wd, ls, lng, lnb):
    S, D = x.shape
    Fs = wg.shape[-2]
    ts = TOK_TILE

    def body(x_ref, mod_ref, wg_ref, wu_ref, wd_ref, lng_ref, lnb_ref, xo_ref, f_ref, g_ref, u_ref, h_ref, acc_sc):
        j = pl.program_id(1)

        @pl.when(j == 0)
        def _():
            xh, _ = _ln_stats(x_ref[...])
            h_ref[...] = (xh * (1.0 + mod_ref[1:2, :]) + mod_ref[0:1, :]).astype(bf16)
            acc_sc[...] = jnp.zeros_like(acc_sc)

        h = h_ref[...]
        g = _dot_nt(h, wg_ref[0, 0])
        u = _dot_nt(h, wu_ref[0, 0])
        g_ref[0] = g.astype(bf16)
        u_ref[0] = u.astype(bf16)
        a = (g * jax.nn.sigmoid(g) * u).astype(bf16)
        acc_sc[...] += _dot(a, wd_ref[0, 0])

        @pl.when(j == N_CHIPS - 1)
        def _():
            f = acc_sc[...]
            f_ref[...] = f
            r = ALPHA * x_ref[...] + (FFN_RES * mod_ref[2:3, :]) * f
            rh, _ = _ln_stats(r)
            xo_ref[...] = rh * lng_ref[...] + lnb_ref[...]

    tok = pl.BlockSpec((ts, D), lambda i, j: (i, 0))
    wrow = pl.BlockSpec((1, 1, Fs, D), lambda i, j: (j, ls, 0, 0))
    hid = pl.BlockSpec((1, ts, Fs), lambda i, j: (j, i, 0))
    return pl.pallas_call(
        body, name="ffn_fwd", grid=(S // ts, N_CHIPS),
        in_specs=[tok, _full((3, D)), wrow, wrow, wrow, _full((1, D)), _full((1, D))],
        out_specs=[tok, tok, hid, hid, tok],
        out_shape=[jax.ShapeDtypeStruct((S, D), f32), jax.ShapeDtypeStruct((S, D), f32),
                   jax.ShapeDtypeStruct((N_CHIPS, S, Fs), bf16), jax.ShapeDtypeStruct((N_CHIPS, S, Fs), bf16),
                   jax.ShapeDtypeStruct((S, D), bf16)],
        scratch_shapes=[pltpu.VMEM((ts, D), f32)],
        compiler_params=_cp(("parallel", "arbitrary"), 56),
    )(*_hbm(x, mod3, wg, wu, wd, lng, lnb))


def _ffn_bwd(dxo, x, f, g, u, mod3, wg, wu, wd, ls, lng):
    S, D = x.shape
    Fs = wg.shape[-2]
    ts = TOK_TILE

    def body(dxo_ref, x_ref, f_ref, g_ref, u_ref, mod_ref, wg_ref, wu_ref, wd_ref, lng_ref,
             dx_ref, dg_ref, du_ref, a_ref, df_ref, dmod_ref, dlng_ref, dlnb_ref,
             dr_sc, df_sc, acc_sc):
        i = pl.program_id(0)
        j = pl.program_id(1)

        @pl.when((i == 0) & (j == 0))
        def _():
            dmod_ref[...] = jnp.zeros_like(dmod_ref)
            dlng_ref[...] = jnp.zeros_like(dlng_ref)
            dlnb_ref[...] = jnp.zeros_like(dlnb_ref)

        @pl.when(j == 0)
        def _():
            xv = x_ref[...]
            fv = f_ref[...]
            gate = mod_ref[2:3, :]
            rh, rstd = _ln_stats(ALPHA * xv + (FFN_RES * gate) * fv)
            dy = dxo_ref[...]
            dlng_ref[...] += jnp.sum(dy * rh, 0, keepdims=True)
            dlnb_ref[...] += jnp.sum(dy, 0, keepdims=True)
            dr = _ln_bwd(dy * lng_ref[...], rh, rstd)
            dr_sc[...] = dr
            dmod_ref[2:3, :] += jnp.sum(FFN_RES * dr * fv, 0, keepdims=True)
            df = ((FFN_RES * gate) * dr).astype(bf16)
            df_sc[...] = df
            df_ref[...] = df
            acc_sc[...] = jnp.zeros_like(acc_sc)

        da = _dot_nt(df_sc[...], wd_ref[0, 0])
        gv = g_ref[0].astype(f32)
        uv = u_ref[0].astype(f32)
        sg = jax.nn.sigmoid(gv)
        si = gv * sg
        a_ref[0] = (si * uv).astype(bf16)
        dgv = (da * uv * (sg * (1.0 + gv * (1.0 - sg)))).astype(bf16)
        duv = (da * si).astype(bf16)
        dg_ref[0] = dgv
        du_ref[0] = duv
        acc_sc[...] += _dot(dgv, wg_ref[0, 0]) + _dot(duv, wu_ref[0, 0])

        @pl.when(j == N_CHIPS - 1)
        def _():
            dh = acc_sc[...]
            xh, rstd0 = _ln_stats(x_ref[...])
            dmod_ref[0:1, :] += jnp.sum(dh, 0, keepdims=True)
            dmod_ref[1:2, :] += jnp.sum(dh * xh, 0, keepdims=True)
            dx_ref[...] = _ln_bwd(dh * (1.0 + mod_ref[1:2, :]), xh, rstd0) + ALPHA * dr_sc[...]

    tok = pl.BlockSpec((ts, D), lambda i, j: (i, 0))
    wrow = pl.BlockSpec((1, 1, Fs, D), lambda i, j: (j, ls, 0, 0))
    hid = pl.BlockSpec((1, ts, Fs), lambda i, j: (j, i, 0))
    hid_shape = jax.ShapeDtypeStruct((N_CHIPS, S, Fs), bf16)
    return pl.pallas_call(
        body, name="ffn_bwd", grid=(S // ts, N_CHIPS),
        in_specs=[tok, tok, tok, hid, hid, _full((3, D)), wrow, wrow, wrow, _full((1, D))],
        out_specs=[tok, hid, hid, hid, tok, _full((3, D)), _full((1, D)), _full((1, D))],
        out_shape=[jax.ShapeDtypeStruct((S, D), f32), hid_shape, hid_shape, hid_shape,
                   jax.ShapeDtypeStruct((S, D), bf16),
                   jax.ShapeDtypeStruct((3, D), f32), jax.ShapeDtypeStruct((1, D), f32), jax.ShapeDtypeStruct((1, D), f32)],
        scratch_shapes=[pltpu.VMEM((ts, D), f32), pltpu.VMEM((ts, D), bf16), pltpu.VMEM((ts, D), f32)],
        compiler_params=_cp(("arbitrary", "arbitrary"), 56),
    )(*_hbm(dxo, x, f, g, u, mod3, wg, wu, wd, lng))


def _ffn_wgrad(h, dg, du, a, df, gwg, gwu, gwd, ls):
    S, D = h.shape
    Fs = dg.shape[-1]
    tk = TOK_TILE
    nk = S // tk

    def body(h_ref, dg_ref, du_ref, a_ref, df_ref, _g0, _g1, _g2, gwg_ref, gwu_ref, gwd_ref, ag_sc, au_sc, ad_sc):
        k = pl.program_id(1)

        @pl.when(k == 0)
        def _():
            ag_sc[...] = jnp.zeros_like(ag_sc)
            au_sc[...] = jnp.zeros_like(au_sc)
            ad_sc[...] = jnp.zeros_like(ad_sc)

        hv = h_ref[...]
        ag_sc[...] += _dot_tn(dg_ref[0], hv)
        au_sc[...] += _dot_tn(du_ref[0], hv)
        ad_sc[...] += _dot_tn(a_ref[0], df_ref[...])

        @pl.when(k == nk - 1)
        def _():
            gwg_ref[0, 0] = ag_sc[...].astype(bf16)
            gwu_ref[0, 0] = au_sc[...].astype(bf16)
            gwd_ref[0, 0] = ad_sc[...].astype(bf16)

    tok = pl.BlockSpec((tk, D), lambda p, k: (k, 0))
    hid = pl.BlockSpec((1, tk, Fs), lambda p, k: (p, k, 0))
    anyspec = pl.BlockSpec(memory_space=pl.ANY)
    orow = pl.BlockSpec((1, 1, Fs, D), lambda p, k: (p, ls, 0, 0))
    return pl.pallas_call(
        body, name="ffn_wgrad", grid=(N_CHIPS, nk),
        in_specs=[tok, hid, hid, hid, tok, anyspec, anyspec, anyspec],
        out_specs=[orow, orow, orow],
        out_shape=[jax.ShapeDtypeStruct(gwg.shape, bf16), jax.ShapeDtypeStruct(gwu.shape, bf16),
                   jax.ShapeDtypeStruct(gwd.shape, bf16)],
        scratch_shapes=[pltpu.VMEM((Fs, D), f32), pltpu.VMEM((Fs, D), f32), pltpu.VMEM((Fs, D), f32)],
        input_output_aliases={5: 0, 6: 1, 7: 2},
        compiler_params=_cp(("parallel", "arbitrary"), 48),
    )(*_hbm(h, dg, du, a, df, gwg, gwu, gwd))


_LANES = 128
_QKV_BLOCKS = D_ATT // _LANES


def _res_spec(lead, d, width, index):
    return pl.BlockSpec((lead, d, TOK_TILE // d, width), index)


def _res_spec3(d, width):
    return pl.BlockSpec((d, TOK_TILE // d, width), lambda i: (0, i, 0))


def _rows_to_residues(tile_bufs, d, put):
    for r in range(d):
        for cb, buf in enumerate(tile_bufs):
            put(r, cb, buf[pl.ds(r, TOK_TILE // d, stride=d), :])


def _residues_to_rows(tile_bufs, d, get):
    for r in range(d):
        for cb, buf in enumerate(tile_bufs):
            buf[pl.ds(r, TOK_TILE // d, stride=d), :] = get(r, cb)


def _mix_in_fwd(x, mod3, w_in, l):
    S, D = x.shape
    N = w_in.shape[-1]
    ts = TOK_TILE

    def body(x_ref, mod_ref, w_ref, o1_ref, o4_ref, o16_ref, zr_ref, h_ref, *bufs):
        j = pl.program_id(1)

        @pl.when(j == 0)
        def _():
            xh, _ = _ln_stats(x_ref[...])
            h_ref[...] = (xh * (1.0 + mod_ref[1:2, :]) + mod_ref[0:1, :]).astype(bf16)

        z = _dot(h_ref[...], w_ref[0, 0])

        @pl.when(j == N_CHIPS - 1)
        def _():
            zr_ref[...] = z

        @pl.when(j < N_CHIPS - 1)
        def _():
            zz = z * jnp.where(j == 0, HEAD_DIM ** -0.5, 1.0)
            o1_ref[j, 0] = zz.astype(bf16)
            for cb, buf in enumerate(bufs):
                buf[...] = zz[:, _LANES * cb:_LANES * (cb + 1)]
            for d, o_ref in zip(DILATIONS[1:], (o4_ref, o16_ref)):
                def put(r, cb, piece, o_ref=o_ref):
                    o_ref[j, r, :, _LANES * cb:_LANES * (cb + 1)] = piece.astype(bf16)
                _rows_to_residues(bufs, d, put)

    tok = pl.BlockSpec((ts, D), lambda i, j: (i, 0))
    res = [_res_spec(3, d, N, lambda i, j: (0, 0, i, 0)) for d in DILATIONS]
    return pl.pallas_call(
        body, name="mix_in_fwd", grid=(S // ts, N_CHIPS),
        in_specs=[tok, _full((3, D)), pl.BlockSpec((1, 1, D, N), lambda i, j: (j, l, 0, 0))],
        out_specs=res + [pl.BlockSpec((ts, N), lambda i, j: (i, 0)), tok],
        out_shape=[jax.ShapeDtypeStruct((3, d, S // d, N), bf16) for d in DILATIONS]
        + [jax.ShapeDtypeStruct((S, N), f32), jax.ShapeDtypeStruct((S, D), bf16)],
        scratch_shapes=[pltpu.VMEM((ts, _LANES), f32)] * _QKV_BLOCKS,
        compiler_params=_cp(("parallel", "arbitrary"), 40),
    )(*_hbm(x, mod3, w_in))


def _mix_in_bwd(dqkv, d_rest, dx_res, x, mod3, w_in, l):
    S, D = x.shape
    N = w_in.shape[-1]
    ts = TOK_TILE

    def body(d1_ref, d4_ref, d16_ref, dr_ref, dxr_ref, x_ref, mod_ref, w_ref, dx_ref, dmod_ref, dz_ref, acc_sc, *bufs):
        i = pl.program_id(0)
        j = pl.program_id(1)

        @pl.when((i == 0) & (j == 0))
        def _():
            dmod_ref[...] = jnp.zeros_like(dmod_ref)

        @pl.when(j == 0)
        def _():
            acc_sc[...] = jnp.zeros_like(acc_sc)

        @pl.when(j == N_CHIPS - 1)
        def _():
            dz_ref[0] = dr_ref[...]

        @pl.when(j < N_CHIPS - 1)
        def _():
            for d, d_ref, tile_bufs in ((4, d4_ref, bufs[:_QKV_BLOCKS]), (16, d16_ref, bufs[_QKV_BLOCKS:])):
                _residues_to_rows(tile_bufs, d, lambda r, cb, d_ref=d_ref: d_ref[0, r, :, _LANES * cb:_LANES * (cb + 1)].astype(f32))
            for cb in range(_QKV_BLOCKS):
                cols = slice(_LANES * cb, _LANES * (cb + 1))
                dz_ref[0, :, cols] = (d1_ref[0, 0, :, cols].astype(f32) + bufs[cb][...] + bufs[_QKV_BLOCKS + cb][...]).astype(bf16)

        acc_sc[...] += _dot_nt(dz_ref[0], w_ref[0, 0])

        @pl.when(j == N_CHIPS - 1)
        def _():
            dh = acc_sc[...]
            xh, rstd0 = _ln_stats(x_ref[...])
            dmod_ref[0:1, :] += jnp.sum(dh, 0, keepdims=True)
            dmod_ref[1:2, :] += jnp.sum(dh * xh, 0, keepdims=True)
            dx_ref[...] = _ln_bwd(dh * (1.0 + mod_ref[1:2, :]), xh, rstd0) + dxr_ref[...]

    tok = pl.BlockSpec((ts, D), lambda i, j: (i, 0))
    res = [_res_spec(1, d, N, lambda i, j: (jnp.minimum(j, 2), 0, i, 0)) for d in DILATIONS]
    return pl.pallas_call(
        body, name="mix_in_bwd", grid=(S // ts, N_CHIPS),
        in_specs=res + [pl.BlockSpec((ts, N), lambda i, j: (i, 0)), tok, tok, _full((3, D)),
                        pl.BlockSpec((1, 1, D, N), lambda i, j: (j, l, 0, 0))],
        out_specs=[tok, _full((3, D)), pl.BlockSpec((1, ts, N), lambda i, j: (j, i, 0))],
        out_shape=[jax.ShapeDtypeStruct((S, D), f32), jax.ShapeDtypeStruct((3, D), f32),
                   jax.ShapeDtypeStruct((N_CHIPS, S, N), bf16)],
        scratch_shapes=[pltpu.VMEM((ts, D), f32)] + [pltpu.VMEM((ts, _LANES), f32)] * (2 * _QKV_BLOCKS),
        compiler_params=_cp(("arbitrary", "arbitrary"), 40),
    )(*_hbm(*dqkv, d_rest, dx_res, x, mod3, w_in))


def _mix_in_wgrad(h, dz, gw, l):
    S, D = h.shape
    N = dz.shape[-1]
    tk = TOK_TILE
    nk = S // tk

    def body(h_ref, dz_ref, _g, gw_ref, acc_sc):
        k = pl.program_id(1)

        @pl.when(k == 0)
        def _():
            acc_sc[...] = jnp.zeros_like(acc_sc)

        acc_sc[...] += _dot_tn(h_ref[...], dz_ref[0])

        @pl.when(k == nk - 1)
        def _():
            gw_ref[0, 0] = acc_sc[...].astype(bf16)

    return pl.pallas_call(
        body, name="mix_in_wgrad", grid=(N_CHIPS, nk),
        in_specs=[pl.BlockSpec((tk, D), lambda p, k: (k, 0)), pl.BlockSpec((1, tk, N), lambda p, k: (p, k, 0)),
                  pl.BlockSpec(memory_space=pl.ANY)],
        out_specs=pl.BlockSpec((1, 1, D, N), lambda p, k: (p, l, 0, 0)),
        out_shape=jax.ShapeDtypeStruct(gw.shape, bf16),
        scratch_shapes=[pltpu.VMEM((D, N), f32)],
        input_output_aliases={2: 0},
        compiler_params=_cp(("parallel", "arbitrary"), 40),
    )(*_hbm(h, dz, gw))


def _mix_out_fwd(x, y_att, y_ssm, y_pool, mod3, w_out, l, lng, lnb):
    S, D = x.shape
    ts = TOK_TILE

    def body(x_ref, ya_ref, ys_ref, yp_ref, mod_ref, w_ref, lng_ref, lnb_ref, xo_ref, y_ref):
        ya = ya_ref[...].astype(bf16)
        y = (_dot(ya[:, 0:256], w_ref[0, 0]) + _dot(ya[:, 256:512], w_ref[1, 0])
             + _dot(ys_ref[...].astype(bf16), w_ref[2, 0]) + _dot(yp_ref[...].astype(bf16), w_ref[3, 0]))
        y_ref[...] = y
        rh, _ = _ln_stats(ALPHA * x_ref[...] + mod_ref[2:3, :] * y)
        xo_ref[...] = rh * lng_ref[...] + lnb_ref[...]

    tok = pl.BlockSpec((ts, D), lambda i: (i, 0))
    return pl.pallas_call(
        body, name="mix_out_fwd", grid=(S // ts,),
        in_specs=[tok, pl.BlockSpec((ts, D_ATT), lambda i: (i, 0)), pl.BlockSpec((ts, D_SSM), lambda i: (i, 0)),
                  pl.BlockSpec((ts, D_POOL), lambda i: (i, 0)), _full((3, D)),
                  pl.BlockSpec((N_CHIPS, 1, 256, D), lambda i: (0, l, 0, 0)), _full((1, D)), _full((1, D))],
        out_specs=[tok, tok],
        out_shape=[jax.ShapeDtypeStruct((S, D), f32), jax.ShapeDtypeStruct((S, D), f32)],
        compiler_params=_cp(("parallel",), 40),
    )(*_hbm(x, y_att, y_ssm, y_pool, mod3, w_out, lng, lnb))


def _mix_out_bwd(dxo, x, y, y_att, y_ssm, y_pool, mod3, w_out, l, lng, gw_out):
    S, D = x.shape
    ts = TOK_TILE
    nt = S // ts

    def body(dxo_ref, x_ref, y_ref, ya_ref, ys_ref, yp_ref, mod_ref, w_ref, lng_ref, _g,
             dxr_ref, da_ref, ds_ref, dp_ref, dgate_ref, dlng_ref, dlnb_ref, gw_ref, acc_sc):
        i = pl.program_id(0)

        @pl.when(i == 0)
        def _():
            dgate_ref[...] = jnp.zeros_like(dgate_ref)
            dlng_ref[...] = jnp.zeros_like(dlng_ref)
            dlnb_ref[...] = jnp.zeros_like(dlnb_ref)
            acc_sc[...] = jnp.zeros_like(acc_sc)

        gate = mod_ref[2:3, :]
        yv = y_ref[...]
        rh, rstd = _ln_stats(ALPHA * x_ref[...] + gate * yv)
        dy_out = dxo_ref[...]
        dlng_ref[...] += jnp.sum(dy_out * rh, 0, keepdims=True)
        dlnb_ref[...] += jnp.sum(dy_out, 0, keepdims=True)
        dr = _ln_bwd(dy_out * lng_ref[...], rh, rstd)
        dxr_ref[...] = ALPHA * dr
        dgate_ref[...] += jnp.sum(dr * yv, 0, keepdims=True)
        dy = (gate * dr).astype(bf16)
        da_ref[:, 0:256] = _dot_nt(dy, w_ref[0, 0])
        da_ref[:, 256:512] = _dot_nt(dy, w_ref[1, 0])
        ds_ref[...] = _dot_nt(dy, w_ref[2, 0])
        dp_ref[...] = _dot_nt(dy, w_ref[3, 0])
        ya = ya_ref[...].astype(bf16)
        acc_sc[0] += _dot_tn(ya[:, 0:256], dy)
        acc_sc[1] += _dot_tn(ya[:, 256:512], dy)
        acc_sc[2] += _dot_tn(ys_ref[...].astype(bf16), dy)
        acc_sc[3] += _dot_tn(yp_ref[...].astype(bf16), dy)

        @pl.when(i == nt - 1)
        def _():
            gw_ref[:, 0] = acc_sc[...].astype(bf16)

    tok = pl.BlockSpec((ts, D), lambda i: (i, 0))
    t512 = pl.BlockSpec((ts, D_ATT), lambda i: (i, 0))
    t256 = pl.BlockSpec((ts, 256), lambda i: (i, 0))
    wspec = pl.BlockSpec((N_CHIPS, 1, 256, D), lambda i: (0, l, 0, 0))
    return pl.pallas_call(
        body, name="mix_out_bwd", grid=(nt,),
        in_specs=[tok, tok, tok, t512, t256, t256, _full((3, D)), wspec, _full((1, D)), pl.BlockSpec(memory_space=pl.ANY)],
        out_specs=[tok, t512, t256, t256, _full((1, D)), _full((1, D)), _full((1, D)), wspec],
        out_shape=[jax.ShapeDtypeStruct((S, D), f32), jax.ShapeDtypeStruct((S, D_ATT), f32),
                   jax.ShapeDtypeStruct((S, D_SSM), f32), jax.ShapeDtypeStruct((S, D_POOL), f32),
                   jax.ShapeDtypeStruct((1, D), f32), jax.ShapeDtypeStruct((1, D), f32), jax.ShapeDtypeStruct((1, D), f32),
                   jax.ShapeDtypeStruct(gw_out.shape, bf16)],
        scratch_shapes=[pltpu.VMEM((N_CHIPS, 256, D), f32)],
        input_output_aliases={9: 7},
        compiler_params=_cp(("arbitrary",), 48),
    )(*_hbm(dxo, x, y, y_att, y_ssm, y_pool, mod3, w_out, lng, gw_out))


def _t5_bucket(dist):
    max_exact = N_BUCKETS // 2
    d = np.maximum(dist, 1).astype(np.float32)
    large = max_exact + (np.log(d / max_exact) / math.log(MAX_DISTANCE / max_exact)
                         * (N_BUCKETS - max_exact)).astype(np.int32)
    large = np.minimum(large, N_BUCKETS - 1)
    return np.where(dist < max_exact, dist, large).astype(np.int32)


def _bucket_table():
    q = ATT_BLOCK
    i = np.arange(q)[:, None]
    j = np.arange(2 * q)[None, :]
    r = i + q - j
    in_band = (r >= 0) & (r <= q)
    tabs = [np.where(in_band, _t5_bucket(np.clip(r, 0, None) * d), -1) for d in DILATIONS]
    return np.stack(tabs).astype(np.int32)


def _bias_fwd(rel_bias, table):
    def body(rb_ref, tab_ref, out_ref):
        for b in range(3):
            tb = tab_ref[b]
            for h in range(N_HEADS):
                def pick(k, acc):
                    return jnp.where(tb == k, rb_ref[k, h], acc)
                out_ref[b, h] = lax.fori_loop(0, N_BUCKETS, pick, jnp.where(tb < 0, NEG, 0.0).astype(f32))

    return pl.pallas_call(
        body, name="bias_fwd",
        in_specs=[pl.BlockSpec(memory_space=pltpu.SMEM), pl.BlockSpec(memory_space=pltpu.VMEM)],
        out_specs=pl.BlockSpec(memory_space=pltpu.VMEM),
        out_shape=jax.ShapeDtypeStruct((3, N_HEADS, ATT_BLOCK, 2 * ATT_BLOCK), f32),
    )(rel_bias, table)


def _bias_bwd(dbias, table):
    def body(db_ref, tab_ref, out_ref):
        def per_bucket(k, c):
            for h in range(N_HEADS):
                tot = jnp.zeros((), f32)
                for b in range(3):
                    tot = tot + jnp.sum(jnp.where(tab_ref[b] == k, db_ref[b, h], 0.0))
                out_ref[k, h] = tot
            return c
        lax.fori_loop(0, N_BUCKETS, per_bucket, 0)

    return pl.pallas_call(
        body, name="bias_bwd",
        in_specs=[pl.BlockSpec(memory_space=pltpu.VMEM), pl.BlockSpec(memory_space=pltpu.VMEM)],
        out_specs=pl.BlockSpec(memory_space=pltpu.SMEM),
        out_shape=jax.ShapeDtypeStruct((N_BUCKETS, N_HEADS), f32),
    )(dbias, table)


def _att_unit(u, nbr):
    rows = pl.ds(pl.multiple_of(u * ATT_BLOCK, ATT_BLOCK), ATT_BLOCK)
    prev = pl.ds(pl.multiple_of(jnp.maximum(u - 1, 0) * ATT_BLOCK, ATT_BLOCK), ATT_BLOCK)
    return rows, prev, (u % nbr) != 0


_N_PAIRS = N_HEADS // 2


def _pair_rows(t):
    lane = lax.broadcasted_iota(jnp.int32, t.shape, 1)
    zero = jnp.zeros_like(t)
    return jnp.concatenate([jnp.where(lane < HEAD_DIM, t, zero), jnp.where(lane >= HEAD_DIM, t, zero)], axis=0)


def _pair_cols(big):
    lane = lax.broadcasted_iota(jnp.int32, (ATT_BLOCK, _LANES), 1)
    return jnp.where(lane < HEAD_DIM, big[:ATT_BLOCK], big[ATT_BLOCK:])


def _pair_column(ref, rows, hp):
    t = ref[rows, :]
    return jnp.concatenate([t[:, 2 * hp:2 * hp + 1], t[:, 2 * hp + 1:2 * hp + 2]], axis=0)


def _pair_band(ref, rows, prev, nbr, hp):
    lanes = pl.ds(_LANES * hp, _LANES)
    cur = ref[0, rows, lanes]
    return cur if nbr == 1 else jnp.concatenate([ref[0, prev, lanes], cur], axis=0)


def _pair_scores(q_ref, k_ref, b_ref, rows, prev, valid_prev, nbr, hp):
    qbd = _pair_rows(q_ref[0, rows, pl.ds(_LANES * hp, _LANES)])
    kb = _pair_band(k_ref, rows, prev, nbr, hp)
    bias = b_ref[0, 2 * hp:2 * hp + 2].reshape(2 * ATT_BLOCK, 2 * ATT_BLOCK)
    if nbr == 1:
        return qbd, kb, _dot_nt(qbd, kb) + bias[:, ATT_BLOCK:]
    s = _dot_nt(qbd, kb) + bias
    col = lax.broadcasted_iota(jnp.int32, s.shape, 1)
    return qbd, kb, jnp.where((col >= ATT_BLOCK) | valid_prev, s, NEG)


def _qkv_specs(S, branch):
    return ([pl.BlockSpec((1, S, D_ATT), lambda i, t=t: (t, 0, 0)) for t in range(3)],
            pl.BlockSpec((1, N_HEADS, ATT_BLOCK, 2 * ATT_BLOCK), lambda i: (branch, 0, 0, 0)))


def _att_fwd(qkv, bias, branch):
    S = qkv.shape[1]
    nbr = BLOCKS_PER_RESIDUE[branch]

    def body(q_ref, k_ref, v_ref, b_ref, o_ref, lse_ref):
        lse_ref[...] = jnp.zeros_like(lse_ref)

        def unit(u, c):
            rows, prev, valid_prev = _att_unit(u, nbr)
            for hp in range(_N_PAIRS):
                _, _, s = _pair_scores(q_ref, k_ref, b_ref, rows, prev, valid_prev, nbr, hp)
                m = jnp.max(s, -1, keepdims=True)
                p = jnp.exp(s - m)
                den = jnp.sum(p, -1, keepdims=True)
                big = _dot(p.astype(bf16), _pair_band(v_ref, rows, prev, nbr, hp))
                o_ref[rows, pl.ds(_LANES * hp, _LANES)] = _pair_cols(big / den)
                lse = m + jnp.log(den)
                lse_ref[rows, pl.ds(2 * hp, 1)] = lse[:ATT_BLOCK]
                lse_ref[rows, pl.ds(2 * hp + 1, 1)] = lse[ATT_BLOCK:]
            return c

        lax.fori_loop(0, N_UNITS, unit, 0)

    qkv_specs, bspec = _qkv_specs(S, branch)
    return pl.pallas_call(
        body, name="att_fwd", grid=(1,),
        in_specs=qkv_specs + [bspec],
        out_specs=[pl.BlockSpec((S, D_ATT), lambda i: (0, 0)), pl.BlockSpec((S, _LANES), lambda i: (0, 0))],
        out_shape=[jax.ShapeDtypeStruct((S, D_ATT), f32), jax.ShapeDtypeStruct((S, _LANES), f32)],
        compiler_params=_cp(("arbitrary",), 40),
    )(*_hbm(qkv, qkv, qkv, bias))


def _att_bwd(qkv, do, lse, crow, bias, branch):
    S = qkv.shape[1]
    nbr = BLOCKS_PER_RESIDUE[branch]

    def body(q_ref, k_ref, v_ref, do_ref, lse_ref, c_ref, b_ref, dqkv_ref, db_ref, dk_sc, dv_sc):
        dk_sc[...] = jnp.zeros_like(dk_sc)
        dv_sc[...] = jnp.zeros_like(dv_sc)
        db_ref[...] = jnp.zeros_like(db_ref)

        def unit(u, c):
            rows, prev, valid_prev = _att_unit(u, nbr)
            for hp in range(_N_PAIRS):
                lanes = pl.ds(_LANES * hp, _LANES)
                qbd, kb, s = _pair_scores(q_ref, k_ref, b_ref, rows, prev, valid_prev, nbr, hp)
                p = jnp.exp(s - _pair_column(lse_ref, rows, hp))
                dobd = _pair_rows(do_ref[rows, lanes])
                ds = p * (_dot_nt(dobd, _pair_band(v_ref, rows, prev, nbr, hp)) - _pair_column(c_ref, rows, hp))
                if nbr == 1:
                    db_ref[2 * hp:2 * hp + 2, :, ATT_BLOCK:] += ds.reshape(2, ATT_BLOCK, ATT_BLOCK)
                else:
                    db_ref[2 * hp:2 * hp + 2] += ds.reshape(2, ATT_BLOCK, 2 * ATT_BLOCK)
                dsb = ds.astype(bf16)
                dqkv_ref[0, rows, lanes] = (HEAD_DIM ** -0.5 * _pair_cols(_dot(dsb, kb))).astype(bf16)
                dkb = _dot_tn(dsb, qbd)
                dvb = _dot_tn(p.astype(bf16), dobd)
                if nbr == 1:
                    dk_sc[rows, lanes] += dkb
                    dv_sc[rows, lanes] += dvb
                else:
                    dk_sc[prev, lanes] += dkb[:ATT_BLOCK]
                    dv_sc[prev, lanes] += dvb[:ATT_BLOCK]
                    dk_sc[rows, lanes] += dkb[ATT_BLOCK:]
                    dv_sc[rows, lanes] += dvb[ATT_BLOCK:]
            return c

        lax.fori_loop(0, N_UNITS, unit, 0)
        dqkv_ref[1] = dk_sc[...].astype(bf16)
        dqkv_ref[2] = dv_sc[...].astype(bf16)

    qkv_specs, bspec = _qkv_specs(S, branch)
    row = pl.BlockSpec((S, _LANES), lambda i: (0, 0))
    return pl.pallas_call(
        body, name="att_bwd", grid=(1,),
        in_specs=qkv_specs + [pl.BlockSpec((S, D_ATT), lambda i: (0, 0)), row, row, bspec],
        out_specs=[pl.BlockSpec((3, S, D_ATT), lambda i: (0, 0, 0)),
                   pl.BlockSpec((N_HEADS, ATT_BLOCK, 2 * ATT_BLOCK), lambda i: (0, 0, 0))],
        out_shape=[jax.ShapeDtypeStruct((3, S, D_ATT), bf16), jax.ShapeDtypeStruct((N_HEADS, ATT_BLOCK, 2 * ATT_BLOCK), f32)],
        scratch_shapes=[pltpu.VMEM((S, D_ATT), f32), pltpu.VMEM((S, D_ATT), f32)],
        compiler_params=_cp(("arbitrary",), 48),
    )(*_hbm(qkv, qkv, qkv, do, lse, crow, bias))


def _branch_weights(lse_ref):
    l0, l1, l2 = lse_ref[0], lse_ref[1], lse_ref[2]
    m = jnp.maximum(jnp.maximum(l0, l1), l2)
    e0, e1, e2 = jnp.exp(l0 - m), jnp.exp(l1 - m), jnp.exp(l2 - m)
    tot = e0 + e1 + e2
    return e0 / tot, e1 / tot, e2 / tot


def _att_merge(os, lses):
    S = os[0].shape[0] * os[0].shape[1]
    ts = TOK_TILE

    def body(o1_ref, o4_ref, o16_ref, l1_ref, l4_ref, l16_ref, y_ref, lt_ref, *bufs):
        obufs = (bufs[:_QKV_BLOCKS], bufs[_QKV_BLOCKS:2 * _QKV_BLOCKS])
        lt_ref[0] = l1_ref[0]
        for k, (d, o_ref, l_ref) in enumerate(((4, o4_ref, l4_ref), (16, o16_ref, l16_ref))):
            _residues_to_rows(obufs[k], d, lambda r, cb, o_ref=o_ref: o_ref[r, :, _LANES * cb:_LANES * (cb + 1)])
            _residues_to_rows([bufs[2 * _QKV_BLOCKS + k]], d, lambda r, cb, l_ref=l_ref: l_ref[r])
            lt_ref[1 + k] = bufs[2 * _QKV_BLOCKS + k][...]
        w = _branch_weights(lt_ref)
        for h in range(N_HEADS):
            cs = slice(HEAD_DIM * h, HEAD_DIM * (h + 1))
            half = slice(HEAD_DIM * (h % 2), HEAD_DIM * (h % 2 + 1))
            y_ref[:, cs] = (w[0][:, h:h + 1] * o1_ref[0, :, cs] + w[1][:, h:h + 1] * obufs[0][h // 2][:, half]
                            + w[2][:, h:h + 1] * obufs[1][h // 2][:, half])

    return pl.pallas_call(
        body, name="att_merge", grid=(S // ts,),
        in_specs=[_res_spec3(d, D_ATT) for d in DILATIONS] + [_res_spec3(d, _LANES) for d in DILATIONS],
        out_specs=[pl.BlockSpec((ts, D_ATT), lambda i: (i, 0)), pl.BlockSpec((3, ts, _LANES), lambda i: (0, i, 0))],
        out_shape=[jax.ShapeDtypeStruct((S, D_ATT), f32), jax.ShapeDtypeStruct((3, S, _LANES), f32)],
        scratch_shapes=[pltpu.VMEM((ts, _LANES), f32)] * (2 * _QKV_BLOCKS + 2),
        compiler_params=_cp(("parallel",)),
    )(*_hbm(*os, *lses))


def _att_merge_bwd(dy, y, lse3):
    S = dy.shape[0]
    ts = TOK_TILE

    def body(dy_ref, y_ref, lse_ref, do1_ref, do4_ref, do16_ref, c1_ref, c4_ref, c16_ref, *bufs):
        dobufs = (bufs[:_QKV_BLOCKS], bufs[_QKV_BLOCKS:2 * _QKV_BLOCKS], bufs[2 * _QKV_BLOCKS:3 * _QKV_BLOCKS])
        cbufs = bufs[3 * _QKV_BLOCKS:]
        w = _branch_weights(lse_ref)
        for cb in cbufs:
            cb[...] = jnp.zeros_like(cb)
        for h in range(N_HEADS):
            cs = slice(HEAD_DIM * h, HEAD_DIM * (h + 1))
            half = slice(HEAD_DIM * (h % 2), HEAD_DIM * (h % 2 + 1))
            dyh = dy_ref[:, cs]
            t = jnp.sum(dyh * y_ref[:, cs], -1, keepdims=True)
            for p in range(3):
                wp = w[p][:, h:h + 1]
                dobufs[p][h // 2][:, half] = wp * dyh
                cbufs[p][:, h:h + 1] = wp * t
        for cb in range(_QKV_BLOCKS):
            do1_ref[0, :, _LANES * cb:_LANES * (cb + 1)] = dobufs[0][cb][...].astype(bf16)
        c1_ref[0] = cbufs[0][...]
        for k, (d, do_ref, c_ref) in enumerate(((4, do4_ref, c4_ref), (16, do16_ref, c16_ref))):
            def put_do(r, cb, piece, do_ref=do_ref):
                do_ref[r, :, _LANES * cb:_LANES * (cb + 1)] = piece.astype(bf16)

            def put_c(r, cb, piece, c_ref=c_ref):
                c_ref[r] = piece

            _rows_to_residues(dobufs[1 + k], d, put_do)
            _rows_to_residues([cbufs[1 + k]], d, put_c)

    return pl.pallas_call(
        body, name="att_merge_bwd", grid=(S // ts,),
        in_specs=[pl.BlockSpec((ts, D_ATT), lambda i: (i, 0)), pl.BlockSpec((ts, D_ATT), lambda i: (i, 0)),
                  pl.BlockSpec((3, ts, _LANES), lambda i: (0, i, 0))],
        out_specs=[_res_spec3(d, D_ATT) for d in DILATIONS] + [_res_spec3(d, _LANES) for d in DILATIONS],
        out_shape=[jax.ShapeDtypeStruct((d, S // d, D_ATT), bf16) for d in DILATIONS]
        + [jax.ShapeDtypeStruct((d, S // d, _LANES), f32) for d in DILATIONS],
        scratch_shapes=[pltpu.VMEM((ts, _LANES), f32)] * (3 * _QKV_BLOCKS + 3),
        compiler_params=_cp(("parallel",)),
    )(*_hbm(dy, y, lse3))


_SSM_ROWS = 256


def _scan_in_place(sr_ref, si_ref, a_ref, reverse):
    S, N = sr_ref.shape
    nst = S // SCAN_SEG
    ar = jnp.broadcast_to(a_ref[0:1, :], (SCAN_SEG, N))
    ai = jnp.broadcast_to(a_ref[1:2, :], (SCAN_SEG, N))
    if reverse:
        ai = -ai
    row = lax.broadcasted_iota(jnp.int32, (SCAN_SEG, N), 0)
    zero = jnp.zeros((SCAN_SEG, N), f32)

    def tile(t):
        return pl.ds(pl.multiple_of((nst - 1 - t if reverse else t) * SCAN_SEG, SCAN_SEG), SCAN_SEG)

    def local(t, c):
        sr, si, pr, pi = c
        rows = tile(t)
        nsr = ar * sr - ai * si + sr_ref[rows, :]
        nsi = ar * si + ai * sr + si_ref[rows, :]
        sr_ref[rows, :] = nsr
        si_ref[rows, :] = nsi
        return nsr, nsi, ar * pr - ai * pi, ar * pi + ai * pr

    fr, fi, apr, api = lax.fori_loop(0, nst, local, (zero, zero, zero + 1.0, zero))

    def shift(v):
        if reverse:
            return jnp.where(row == SCAN_SEG - 1, 0.0, pltpu.roll(v, SCAN_SEG - 1, axis=0))
        return jnp.where(row == 0, 0.0, pltpu.roll(v, 1, axis=0))

    cr, ci = zero, zero
    for _ in range(SCAN_SEG - 1):
        cr, ci = shift(fr + apr * cr - api * ci), shift(fi + apr * ci + api * cr)

    def fix(t, c):
        pr, pi = c
        npr, npi = ar * pr - ai * pi, ar * pi + ai * pr
        rows = tile(t)
        sr_ref[rows, :] += npr * cr - npi * ci
        si_ref[rows, :] += npr * ci + npi * cr
        return npr, npi

    lax.fori_loop(0, nst, fix, (zero + 1.0, zero))


def _ssm_states(u, bre, bim, a2):
    S = u.shape[0]

    def body(u_ref, br_ref, bi_ref, a_ref, sr_ref, si_ref):
        brb = br_ref[...].astype(bf16)
        bib = bi_ref[...].astype(bf16)

        def project(t, c):
            rows = pl.ds(pl.multiple_of(t * _SSM_ROWS, _SSM_ROWS), _SSM_ROWS)
            ub = u_ref[rows, :].astype(bf16)
            sr_ref[rows, :] = _dot(ub, brb)
            si_ref[rows, :] = _dot(ub, bib)
            return c

        lax.fori_loop(0, S // _SSM_ROWS, project, 0)
        _scan_in_place(sr_ref, si_ref, a_ref, False)

    vm = pl.BlockSpec(memory_space=pltpu.VMEM)
    return pl.pallas_call(
        body, name="ssm_states", in_specs=[vm] * 4, out_specs=[vm, vm],
        out_shape=[jax.ShapeDtypeStruct((S, D_STATE), f32)] * 2,
        compiler_params=_cp(None, 48),
    )(u, bre, bim, a2)


def _ssm_out(sr, si, u, cre, cim, dskip, glu_w, glu_b):
    S = u.shape[0]
    ts = TOK_TILE

    def body(sr_ref, si_ref, u_ref, cr_ref, ci_ref, d_ref, w_ref, b_ref, out_ref, y_ref):
        y = (_dot(sr_ref[...].astype(bf16), cr_ref[...].astype(bf16))
             - _dot(si_ref[...].astype(bf16), ci_ref[...].astype(bf16)) + d_ref[...] * u_ref[...])
        y_ref[...] = y
        z = _dot(_gelu(y).astype(bf16), w_ref[...].astype(bf16)) + b_ref[...]
        out_ref[...] = y * jax.nn.sigmoid(z)

    st = pl.BlockSpec((ts, D_STATE), lambda i: (i, 0))
    ch = pl.BlockSpec((ts, D_SSM), lambda i: (i, 0))
    return pl.pallas_call(
        body, name="ssm_out", grid=(S // ts,),
        in_specs=[st, st, ch, _full((D_STATE, D_SSM)), _full((D_STATE, D_SSM)), _full((1, D_SSM)),
                  _full((D_SSM, D_SSM)), _full((1, D_SSM))],
        out_specs=[ch, ch],
        out_shape=[jax.ShapeDtypeStruct((S, D_SSM), f32)] * 2,
        compiler_params=_cp(("parallel",)),
    )(*_hbm(sr, si, u, cre, cim, dskip, glu_w, glu_b))


def _ssm_out_bwd(dout, y, u, sr, si, dskip, glu_w, glu_b):
    S = u.shape[0]
    ts = TOK_TILE

    def body(do_ref, y_ref, u_ref, sr_ref, si_ref, d_ref, w_ref, b_ref,
             dy_ref, du_ref, dcr_ref, dci_ref, dd_ref, dgb_ref, dgw_ref):
        @pl.when(pl.program_id(0) == 0)
        def _():
            for r in (dcr_ref, dci_ref, dd_ref, dgb_ref, dgw_ref):
                r[...] = jnp.zeros_like(r)

        y = y_ref[...]
        dout = do_ref[...]
        wb = w_ref[...].astype(bf16)
        ge = _gelu(y).astype(bf16)
        sz = jax.nn.sigmoid(_dot(ge, wb) + b_ref[...])
        dz = dout * y * sz * (1.0 - sz)
        dzb = dz.astype(bf16)
        dgb_ref[...] += jnp.sum(dz, 0, keepdims=True)
        dgw_ref[...] += _dot_tn(ge, dzb)
        dy = dout * sz + _gelu_grad(y) * _dot_nt(dzb, wb)
        uv = u_ref[...]
        dd_ref[...] += jnp.sum(dy * uv, 0, keepdims=True)
        du_ref[...] = dy * d_ref[...]
        dy_ref[...] = dy
        dyb = dy.astype(bf16)
        dcr_ref[...] += _dot_tn(sr_ref[...].astype(bf16), dyb)
        dci_ref[...] -= _dot_tn(si_ref[...].astype(bf16), dyb)

    st = pl.BlockSpec((ts, D_STATE), lambda i: (i, 0))
    ch = pl.BlockSpec((ts, D_SSM), lambda i: (i, 0))
    c_full = _full((D_STATE, D_SSM))
    return pl.pallas_call(
        body, name="ssm_out_bwd", grid=(S // ts,),
        in_specs=[ch, ch, ch, st, st, _full((1, D_SSM)), _full((D_SSM, D_SSM)), _full((1, D_SSM))],
        out_specs=[ch, ch, c_full, c_full, _full((1, D_SSM)), _full((1, D_SSM)), _full((D_SSM, D_SSM))],
        out_shape=[jax.ShapeDtypeStruct((S, D_SSM), f32), jax.ShapeDtypeStruct((S, D_SSM), f32),
                   jax.ShapeDtypeStruct((D_STATE, D_SSM), f32), jax.ShapeDtypeStruct((D_STATE, D_SSM), f32),
                   jax.ShapeDtypeStruct((1, D_SSM), f32), jax.ShapeDtypeStruct((1, D_SSM), f32),
                   jax.ShapeDtypeStruct((D_SSM, D_SSM), f32)],
        compiler_params=_cp(("arbitrary",), 40),
    )(*_hbm(dout, y, u, sr, si, dskip, glu_w, glu_b))


def _ssm_states_bwd(dy, du_skip, u, sr, si, cre, cim, bre, bim, a2):
    S = u.shape[0]
    N = D_STATE
    nst = S // SCAN_SEG
    nproj = S // _SSM_ROWS

    def body(dy_ref, dus_ref, u_ref, sr_ref, si_ref, cr_ref, ci_ref, br_ref, bi_ref, a_ref,
             du_ref, dbr_ref, dbi_ref, da_ref, lr_ref, li_ref):
        crb = cr_ref[...].astype(bf16)
        cib = ci_ref[...].astype(bf16)

        def project(t, c):
            rows = pl.ds(pl.multiple_of(t * _SSM_ROWS, _SSM_ROWS), _SSM_ROWS)
            dyb = dy_ref[rows, :].astype(bf16)
            lr_ref[rows, :] = _dot_nt(dyb, crb)
            li_ref[rows, :] = -_dot_nt(dyb, cib)
            return c

        lax.fori_loop(0, nproj, project, 0)
        _scan_in_place(lr_ref, li_ref, a_ref, True)

        row = lax.broadcasted_iota(jnp.int32, (SCAN_SEG, N), 0)
        last = pl.ds((nst - 1) * SCAN_SEG, SCAN_SEG)
        pr = jnp.where(row == 0, 0.0, pltpu.roll(sr_ref[last, :], 1, axis=0))
        pi = jnp.where(row == 0, 0.0, pltpu.roll(si_ref[last, :], 1, axis=0))
        first = pl.ds(0, SCAN_SEG)
        acc_r = lr_ref[first, :] * pr + li_ref[first, :] * pi
        acc_i = li_ref[first, :] * pr - lr_ref[first, :] * pi

        def step(t, c):
            acc_r, acc_i = c
            rows = pl.ds(pl.multiple_of(t * SCAN_SEG, SCAN_SEG), SCAN_SEG)
            prev = pl.ds(pl.multiple_of((t - 1) * SCAN_SEG, SCAN_SEG), SCAN_SEG)
            lrv, liv, srv, siv = lr_ref[rows, :], li_ref[rows, :], sr_ref[prev, :], si_ref[prev, :]
            return acc_r + lrv * srv + liv * siv, acc_i + liv * srv - lrv * siv

        acc_r, acc_i = lax.fori_loop(1, nst, step, (acc_r, acc_i))
        da_ref[0:1, :] = jnp.sum(acc_r, 0, keepdims=True)
        da_ref[1:2, :] = jnp.sum(acc_i, 0, keepdims=True)

        brb = br_ref[...].astype(bf16)
        bib = bi_ref[...].astype(bf16)
        dbr_ref[...] = jnp.zeros_like(dbr_ref)
        dbi_ref[...] = jnp.zeros_like(dbi_ref)

        def back(t, c):
            rows = pl.ds(pl.multiple_of(t * _SSM_ROWS, _SSM_ROWS), _SSM_ROWS)
            lrb = lr_ref[rows, :].astype(bf16)
            lib = li_ref[rows, :].astype(bf16)
            du_ref[rows, :] = dus_ref[rows, :] + _dot_nt(lrb, brb) + _dot_nt(lib, bib)
            ub = u_ref[rows, :].astype(bf16)
            dbr_ref[...] += _dot_tn(ub, lrb)
            dbi_ref[...] += _dot_tn(ub, lib)
            return c

        lax.fori_loop(0, nproj, back, 0)

    vm = pl.BlockSpec(memory_space=pltpu.VMEM)
    return pl.pallas_call(
        body, name="ssm_states_bwd", in_specs=[vm] * 10, out_specs=[vm] * 4,
        out_shape=[jax.ShapeDtypeStruct((S, D_SSM), f32), jax.ShapeDtypeStruct((D_SSM, D_STATE), f32),
                   jax.ShapeDtypeStruct((D_SSM, D_STATE), f32), jax.ShapeDtypeStruct((2, D_STATE), f32)],
        scratch_shapes=[pltpu.VMEM((S, D_STATE), f32), pltpu.VMEM((S, D_STATE), f32)],
        compiler_params=_cp(None, 56),
    )(dy, du_skip, u, sr, si, cre, cim, bre, bim, a2)


_POOL_TILE = 256


def _window_sums(xt, back):
    n = xt.shape[0]
    out = []
    ws = xt
    for k in (1, 2, 4, 8):
        ws = ws + pltpu.roll(ws, k if back else n - k, axis=0)
        out.append(ws)
    return out


def _pool_count(r0, w):
    t = r0 + lax.broadcasted_iota(jnp.int32, (_POOL_TILE, POOL_GROUP), 0)
    return jnp.minimum(t + 1, w).astype(f32)


def _pool_fwd(u_pad, pool_w, pool_scale):
    S = u_pad.shape[0] - POOL_HALO
    nt = S // _POOL_TILE

    def body(u_ref, w_ref, sc_ref, y_ref):
        def tile(t, c):
            r0 = pl.multiple_of(t * _POOL_TILE, _POOL_TILE)
            for g, w in enumerate(POOL_WINDOWS):
                cs = pl.ds(POOL_GROUP * g, POOL_GROUP)
                xt = u_ref[pl.ds(r0, _POOL_TILE + POOL_HALO), cs]
                ws = _window_sums(xt, True)[g][POOL_HALO:, :]
                pooled = ws / _pool_count(r0, w) - xt[POOL_HALO:, :]
                y_ref[pl.ds(r0, _POOL_TILE), cs] = _dot(pooled.astype(bf16), w_ref[g].astype(bf16)) * sc_ref[:, cs]
            return c
        lax.fori_loop(0, nt, tile, 0)

    vm = pl.BlockSpec(memory_space=pltpu.VMEM)
    return pl.pallas_call(
        body, name="pool_fwd", in_specs=[vm, vm, vm], out_specs=vm,
        out_shape=jax.ShapeDtypeStruct((S, D_POOL), f32),
    )(u_pad, pool_w, pool_scale)


def _pool_bwd(dy_pad, u_pad, pool_w, pool_scale):
    S = u_pad.shape[0] - POOL_HALO
    nt = S // _POOL_TILE
    n = _POOL_TILE + POOL_HALO

    def body(dy_ref, u_ref, w_ref, sc_ref, du_ref, dw_ref, dsc_ref):
        dw_ref[...] = jnp.zeros_like(dw_ref)
        dsc_ref[...] = jnp.zeros_like(dsc_ref)

        def tile(t, c):
            r0 = pl.multiple_of(t * _POOL_TILE, _POOL_TILE)
            for g, w in enumerate(POOL_WINDOWS):
                cs = pl.ds(POOL_GROUP * g, POOL_GROUP)
                wb = w_ref[g].astype(bf16)
                xt = u_ref[pl.ds(r0, n), cs]
                pooled = (_window_sums(xt, True)[g][POOL_HALO:, :] / _pool_count(r0, w) - xt[POOL_HALO:, :]).astype(bf16)
                dy = dy_ref[pl.ds(r0, _POOL_TILE), cs]
                dsc_ref[:, cs] += jnp.sum(dy * _dot(pooled, wb), 0, keepdims=True)
                dw_ref[g] += _dot_tn(pooled, (dy * sc_ref[:, cs]).astype(bf16))
                dyh = (dy_ref[pl.ds(r0, n), cs] * sc_ref[:, cs]).astype(bf16)
                dpl = _dot_nt(dyh, wb)
                cnt = jnp.minimum(r0 + lax.broadcasted_iota(jnp.int32, (n, POOL_GROUP), 0) + 1, w).astype(f32)
                lead = _window_sums(dpl / cnt, False)[g]
                du_ref[pl.ds(r0, _POOL_TILE), cs] = lead[:_POOL_TILE, :] - dpl[:_POOL_TILE, :]
            return c
        lax.fori_loop(0, nt, tile, 0)

    vm = pl.BlockSpec(memory_space=pltpu.VMEM)
    return pl.pallas_call(
        body, name="pool_bwd", in_specs=[vm, vm, vm, vm], out_specs=[vm, vm, vm],
        out_shape=[jax.ShapeDtypeStruct((S, D_POOL), f32), jax.ShapeDtypeStruct((4, POOL_GROUP, POOL_GROUP), f32),
                   jax.ShapeDtypeStruct((1, D_POOL), f32)],
    )(dy_pad, u_pad, pool_w, pool_scale)


def _loss_head(y, target):
    S, D = y.shape
    ts = TOK_TILE

    def body(y_ref, t_ref, loss_ref, dy_ref):
        @pl.when(pl.program_id(0) == 0)
        def _():
            loss_ref[...] = jnp.zeros_like(loss_ref)

        d = y_ref[...] - t_ref[...]
        dy_ref[...] = d * (1.0 / D)
        loss_ref[...] += 0.5 * jnp.sum(jnp.sum(d * d, -1, keepdims=True) * (1.0 / D), 0, keepdims=True)

    tok = pl.BlockSpec((ts, D), lambda i: (i, 0))
    return pl.pallas_call(
        body, name="loss_head", grid=(S // ts,),
        in_specs=[tok, tok], out_specs=[_full((1, 1)), tok],
        out_shape=[jax.ShapeDtypeStruct((1, 1), f32), jax.ShapeDtypeStruct((S, D), f32)],
        compiler_params=_cp(("arbitrary",)),
    )(*_hbm(y, target))


_ADA_COLS = 768


def _ada_fwd(c_all, ada_w, ada_b_cols):
    L, D, N = ada_w.shape
    B = c_all.shape[0]

    def body(c_ref, w_ref, b_ref, out_ref):
        cv = c_ref[...]
        cond = (cv * jax.nn.sigmoid(cv)).astype(bf16)
        out_ref[0] = _dot(cond, w_ref[0].astype(bf16)) + b_ref[0]

    return pl.pallas_call(
        body, name="ada_fwd", grid=(L, N // _ADA_COLS),
        in_specs=[_full((B, D)), pl.BlockSpec((1, D, _ADA_COLS), lambda l, j: (l, 0, j)),
                  pl.BlockSpec((1, 1, _ADA_COLS), lambda l, j: (l, 0, j))],
        out_specs=pl.BlockSpec((1, B, _ADA_COLS), lambda l, j: (l, 0, j)),
        out_shape=jax.ShapeDtypeStruct((L, B, N), f32),
        compiler_params=_cp(("parallel", "parallel")),
    )(c_all, ada_w, ada_b_cols)


def _ada_wgrad(c_all_t, dmod_cols):
    D, B = c_all_t.shape
    L, _, N = dmod_cols.shape

    def body(ct_ref, dm_ref, out_ref):
        cv = ct_ref[...]
        cond = cv * jax.nn.sigmoid(cv)
        acc = cond[:, 0:1] * dm_ref[0, 0:1, :]
        for b in range(1, B):
            acc = acc + cond[:, b:b + 1] * dm_ref[0, b:b + 1, :]
        out_ref[0] = acc

    return pl.pallas_call(
        body, name="ada_wgrad", grid=(L, N // _ADA_COLS),
        in_specs=[_full((D, B)), pl.BlockSpec((1, B, _ADA_COLS), lambda l, j: (l, 0, j))],
        out_specs=pl.BlockSpec((1, D, _ADA_COLS), lambda l, j: (l, 0, j)),
        out_shape=jax.ShapeDtypeStruct((L, D, N), f32),
        compiler_params=_cp(("parallel", "parallel")),
    )(c_all_t, dmod_cols)


def _adam_math(w, g, m, v):
    m = ADAM_B1 * m + (1.0 - ADAM_B1) * g
    v = ADAM_B2 * v + (1.0 - ADAM_B2) * (g * g)
    m_hat = m / (1.0 - ADAM_B1 ** ADAM_STEP)
    v_hat = v / (1.0 - ADAM_B2 ** ADAM_STEP)
    delta = -ADAM_LR * (m_hat / (jnp.sqrt(v_hat) + ADAM_EPS) + ADAM_WD * w)
    return delta, m, v


def _adamw(w, m, v, g, row_tile, row0=0, outs=None):
    R, C = w.shape
    b0 = row0 // row_tile

    def body(w_ref, m_ref, v_ref, g_ref, _0, _1, _2, _3, g_out, d_out, m_out, v_out):
        gv = g_ref[...]
        delta, mn, vn = _adam_math(w_ref[...], gv, m_ref[...], v_ref[...])
        g_out[...] = gv
        d_out[...] = delta
        m_out[...] = mn
        v_out[...] = vn

    pspec = pl.BlockSpec((row_tile, C), lambda i: (b0 + i, 0))
    gspec = pl.BlockSpec((row_tile, C), lambda i: (i, 0))
    anyspec = pl.BlockSpec(memory_space=pl.ANY)
    shp = jax.ShapeDtypeStruct((R, C), f32)
    if outs is None:
        outs = [lax.empty((R, C), f32) for _ in range(4)]
    return pl.pallas_call(
        body, name="adamw", grid=(g.shape[0] // row_tile,),
        in_specs=[pspec] * 3 + [gspec] + [anyspec] * 4, out_specs=[pspec] * 4, out_shape=[shp] * 4,
        input_output_aliases={4: 0, 5: 1, 6: 2, 7: 3},
        compiler_params=_cp(("parallel",), 40),
    )(*_hbm(w, m, v, g, *outs))


def _pair_sum(g5s, gots, pc):
    n = len(g5s)

    def body(pc_ref, *refs):
        for own, got, out in zip(refs[:n], refs[n:2 * n], refs[2 * n:]):
            out[0, 0] = (own[0, 0, 0].astype(f32) + got[0, 0].astype(f32)).astype(bf16)

    def half(g):
        return pl.BlockSpec((1, 1) + g.shape[-2:], lambda p, pc: (p, 0, 0, 0))

    gs = pltpu.PrefetchScalarGridSpec(
        num_scalar_prefetch=1, grid=(N_CHIPS,),
        in_specs=[pl.BlockSpec((1, 1, 1) + g.shape[-2:], lambda p, pc: (p, 0, pc[1], 0, 0)) for g in g5s]
        + [half(g) for g in gots],
        out_specs=[half(g) for g in gots],
    )
    return pl.pallas_call(
        body, name="pair_sum", grid_spec=gs, out_shape=[jax.ShapeDtypeStruct(g.shape, bf16) for g in gots],
        compiler_params=_cp(("parallel",), 48),
    )(pc, *_hbm(*g5s, *gots))


_SUM_STEPS = 2


def _sum_shards(hsums, recvs, pc):
    n = len(hsums)

    def body(pc_ref, *refs):
        for own, got, out in zip(refs[:n], refs[n:2 * n], refs[2 * n:]):
            acc = own[0, 0].astype(f32)
            for j in range(3):
                acc = acc + got[j, 0].astype(f32)
            out[0, 0] = acc

    def rows(h):
        return (h.shape[2] // _SUM_STEPS, h.shape[3])

    gs = pltpu.PrefetchScalarGridSpec(
        num_scalar_prefetch=1, grid=(_SUM_STEPS,),
        in_specs=[pl.BlockSpec((1, 1) + rows(h), lambda i, pc: (pc[0], 0, i, 0)) for h in hsums]
        + [pl.BlockSpec((3, 1) + rows(h), lambda i, pc: (0, 0, i, 0)) for h in hsums],
        out_specs=[pl.BlockSpec((1, 1) + rows(h), lambda i, pc: (0, pc[1], i, 0)) for h in hsums],
    )
    return pl.pallas_call(
        body, name="sum_shards", grid_spec=gs,
        out_shape=[jax.ShapeDtypeStruct((1, 2) + h.shape[2:], f32) for h in hsums],
        compiler_params=_cp(("parallel",), 48),
    )(pc, *_hbm(*hsums, *recvs))


def _sum8(packs):
    _, R, C = packs.shape
    tr = R // 8 if R % 64 == 0 else R

    def body(p_ref, out_ref):
        acc = p_ref[0]
        for d in range(1, 8):
            acc = acc + p_ref[d]
        out_ref[...] = acc

    return pl.pallas_call(
        body, name="sum8", grid=(R // tr,),
        in_specs=[pl.BlockSpec((8, tr, C), lambda i: (0, i, 0))],
        out_specs=pl.BlockSpec((tr, C), lambda i: (i, 0)),
        out_shape=jax.ShapeDtypeStruct((R, C), f32),
        compiler_params=_cp(("parallel",)),
    )(packs)


def _allgather8(x_shard):
    m_per, n = x_shard.shape

    def body(x_ref, out_ref, send_sems, recv_sems, local_sem):
        x, y, c = lax.axis_index("x"), lax.axis_index("y"), lax.axis_index("c")
        me, sibling = (x, y, c), (x, y, 1 - c)
        chips = [(1 - x, y), (x, 1 - y), (1 - x, 1 - y)]

        def rows(px, py, pc):
            return out_ref.at[pl.ds((4 * px + 2 * py + pc) * m_per, m_per), :]

        def copy(k, block, to, src=None):
            return pltpu.make_async_remote_copy(
                src_ref=rows(*block) if src is None else src, dst_ref=rows(*block),
                send_sem=send_sems.at[k], recv_sem=recv_sems.at[k], device_id=to, device_id_type=MESH)

        mine = pltpu.make_async_copy(x_ref, rows(*me), local_sem)
        mine.start()
        first = [copy(0, me, sibling, src=x_ref)]
        first += [copy(1 + j, me, (*chip, c), src=x_ref) for j, chip in enumerate(chips)]
        for cp in first:
            cp.start()
        passed = [copy(4 + j, (*chip, c), sibling) for j, chip in enumerate(chips)]
        for j, chip in enumerate(chips):
            copy(1 + j, (*chip, c), me).wait_recv()
            passed[j].start()
        copy(0, sibling, me).wait_recv()
        for j, chip in enumerate(chips):
            copy(4 + j, (*chip, 1 - c), me).wait_recv()
        for cp in first + passed:
            cp.wait_send()
        mine.wait()

    return pl.pallas_call(
        body, name="allgather8",
        out_shape=jax.ShapeDtypeStruct((8 * m_per, n), x_shard.dtype),
        in_specs=[pl.BlockSpec(memory_space=pltpu.VMEM)],
        out_specs=pl.BlockSpec(memory_space=pltpu.VMEM),
        scratch_shapes=[pltpu.SemaphoreType.DMA((7,)), pltpu.SemaphoreType.DMA((7,)), pltpu.SemaphoreType.DMA],
        compiler_params=_cp(None, 48),
    )(x_shard)


def _other_chips():
    x, y = lax.axis_index("x"), lax.axis_index("y")
    return [(1 - x, y), (x, 1 - y), (1 - x, 1 - y)]


_HBM = pl.BlockSpec(memory_space=pltpu.HBM)
_SEM = pl.BlockSpec(memory_space=pltpu.SEMAPHORE)
_EFFECT = pltpu.SideEffectType.DATAFLOW_SIDE_EFFECTING


def _gather_copies(srcs, lands, send_sems, recv_sems):
    x, y, c = lax.axis_index("x"), lax.axis_index("y"), lax.axis_index("c")
    return [pltpu.make_async_remote_copy(
        src_ref=srcs[a].at[:, c], dst_ref=lands[a].at[2 * x + y, :, c], send_sem=send_sems.at[3 * a + j],
        recv_sem=recv_sems.at[3 * a + j], device_id=(cx, cy, c), device_id_type=MESH)
        for a in range(len(srcs)) for j, (cx, cy) in enumerate(_other_chips())]


def _gather_start(chunks, after, name):
    sizes = [len(srcs) for srcs, _ in chunks]
    flat = [t for srcs, lands in chunks for t in list(srcs) + list(lands)]
    nflat = len(flat)
    nsem = 2 * len(chunks)

    def body(*refs):
        ins, sems, token = refs[:nflat], refs[nflat + 1:nflat + 1 + nsem], refs[-1]
        off = 0
        for k, n in enumerate(sizes):
            for cp in _gather_copies(ins[off:off + n], ins[off + n:off + 2 * n], sems[2 * k], sems[2 * k + 1]):
                cp.start()
            off += 2 * n
        token[...] = jnp.zeros_like(token)

    res = pl.pallas_call(
        body, name=name,
        out_shape=[pltpu.SemaphoreType.DMA((3 * n,)) for n in sizes for _ in range(2)]
        + [pltpu.HBM(t.shape, t.dtype) for t in flat] + [jax.ShapeDtypeStruct((8, 128), f32)],
        in_specs=[_HBM] * nflat + [pl.BlockSpec(memory_space=pl.ANY)],
        out_specs=[_SEM] * nsem + [_HBM] * nflat + [pl.BlockSpec(memory_space=pltpu.VMEM)],
        input_output_aliases={i: nsem + i for i in range(nflat)},
        compiler_params=pltpu.CompilerParams(has_side_effects=_EFFECT),
    )(*[pltpu.with_memory_space_constraint(t, pltpu.HBM) for t in flat], after)
    out, off = [], nsem
    for k, n in enumerate(sizes):
        out.append((res[2 * k], res[2 * k + 1], res[off:off + n], res[off + n:off + 2 * n]))
        off += 2 * n
    return out, res[-1]


def _gather_wait(send_sems, recv_sems, srcs, lands, after, name):
    n = len(srcs)

    def body(*refs):
        for cp in _gather_copies(refs[:n], refs[n:2 * n], refs[2 * n], refs[2 * n + 1]):
            cp.wait_send()
            cp.wait_recv()

    res = pl.pallas_call(
        body, name=name,
        out_shape=[pltpu.HBM(t.shape, t.dtype) for t in list(srcs) + list(lands)],
        in_specs=[_HBM] * (2 * n) + [_SEM, _SEM] + [pl.BlockSpec(memory_space=pl.ANY)] * len(after),
        out_specs=[_HBM] * (2 * n),
        input_output_aliases={i: i for i in range(2 * n)},
        compiler_params=pltpu.CompilerParams(has_side_effects=_EFFECT),
    )(*srcs, *lands, send_sems, recv_sems, *after)
    return res[n:]


def _split_start(make_copies, arrays, nsem, name, after=()):
    n, na = len(arrays), len(after)

    def body(*refs):
        for cp in make_copies(refs[:n], refs[n + na], refs[n + na + 1]):
            cp.start()
        refs[-1][...] = jnp.zeros_like(refs[-1])

    res = pl.pallas_call(
        body, name=name,
        out_shape=[pltpu.SemaphoreType.DMA((nsem,)), pltpu.SemaphoreType.DMA((nsem,))]
        + [pltpu.HBM(t.shape, t.dtype) for t in arrays] + [jax.ShapeDtypeStruct((8, 128), f32)],
        in_specs=[_HBM] * n + [pl.BlockSpec(memory_space=pl.ANY)] * na,
        out_specs=[_SEM, _SEM] + [_HBM] * n + [pl.BlockSpec(memory_space=pltpu.VMEM)],
        input_output_aliases={i: i + 2 for i in range(n)},
        compiler_params=pltpu.CompilerParams(has_side_effects=_EFFECT),
    )(*[pltpu.with_memory_space_constraint(t, pltpu.HBM) for t in arrays], *after)
    return (res[0], res[1], res[2:2 + n]), res[-1]


def _split_wait(make_copies, send_sems, recv_sems, arrays, after, name):
    n = len(arrays)

    def body(*refs):
        for cp in make_copies(refs[:n], refs[n], refs[n + 1]):
            cp.wait_send()
            cp.wait_recv()

    return pl.pallas_call(
        body, name=name,
        out_shape=[pltpu.HBM(t.shape, t.dtype) for t in arrays],
        in_specs=[_HBM] * n + [_SEM, _SEM] + [pl.BlockSpec(memory_space=pl.ANY)] * len(after),
        out_specs=[_HBM] * n, input_output_aliases={i: i for i in range(n)},
        compiler_params=pltpu.CompilerParams(has_side_effects=_EFFECT),
    )(*arrays, send_sems, recv_sems, *after)


def _sibling():
    return lax.axis_index("x"), lax.axis_index("y"), 1 - lax.axis_index("c")


def _forward_copies(lands, send_sems, recv_sems):
    c = lax.axis_index("c")
    return [pltpu.make_async_remote_copy(
        src_ref=lands[a].at[2 * cx + cy, :, c], dst_ref=lands[a].at[2 * cx + cy, :, c], send_sem=send_sems.at[3 * a + j],
        recv_sem=recv_sems.at[3 * a + j], device_id=_sibling(), device_id_type=MESH)
        for a in range(len(lands)) for j, (cx, cy) in enumerate(_other_chips())]


def _swap_copies(fulls, send_sems, recv_sems):
    c = lax.axis_index("c")
    return [pltpu.make_async_remote_copy(src_ref=t.at[:, c], dst_ref=t.at[:, c], send_sem=send_sems.at[a],
                                         recv_sem=recv_sems.at[a], device_id=_sibling(), device_id_type=MESH)
            for a, t in enumerate(fulls)]


def _pair_copies(refs, send_sems, recv_sems):
    n = len(refs) // 2
    c = lax.axis_index("c")
    return [pltpu.make_async_remote_copy(src_ref=refs[a].at[:, :, 1 - c], dst_ref=refs[n + a], send_sem=send_sems.at[a],
                                         recv_sem=recv_sems.at[a], device_id=_sibling(), device_id_type=MESH)
            for a in range(n)]


def _scatter_copies(srcs, lands, send_sems, recv_sems):
    c = lax.axis_index("c")
    return [pltpu.make_async_remote_copy(
        src_ref=srcs[a].at[2 * cx + cy], dst_ref=lands[a].at[j], send_sem=send_sems.at[3 * a + j],
        recv_sem=recv_sems.at[3 * a + j], device_id=(cx, cy, c), device_id_type=MESH)
        for a in range(len(srcs)) for j, (cx, cy) in enumerate(_other_chips())]


def _scatter_start(hsums, name, after=()):
    n = len(hsums)
    na = len(after)

    def body(*refs):
        srcs, lands = refs[:n], refs[n:2 * n]
        send_sems, recv_sems = refs[2 * n + na], refs[2 * n + na + 1]
        for cp in _scatter_copies(srcs, lands, send_sems, recv_sems):
            cp.start()
        refs[-1][...] = jnp.zeros_like(refs[-1])

    lands = [lax.empty((3,) + g.shape[1:], g.dtype) for g in hsums]
    res = pl.pallas_call(
        body, name=name,
        out_shape=[pltpu.SemaphoreType.DMA((3 * n,)), pltpu.SemaphoreType.DMA((3 * n,))]
        + [pltpu.HBM(g.shape, g.dtype) for g in hsums] + [pltpu.HBM(g.shape, g.dtype) for g in lands]
        + [jax.ShapeDtypeStruct((8, 128), f32)],
        in_specs=[_HBM] * (2 * n) + [pl.BlockSpec(memory_space=pl.ANY)] * na,
        out_specs=[_SEM, _SEM] + [_HBM] * (2 * n) + [pl.BlockSpec(memory_space=pltpu.VMEM)],
        input_output_aliases={i: i + 2 for i in range(2 * n)},
        compiler_params=pltpu.CompilerParams(has_side_effects=_EFFECT),
    )(*[pltpu.with_memory_space_constraint(t, pltpu.HBM) for t in list(hsums) + lands], *after)
    return (res[0], res[1], res[2:2 + n], res[2 + n:2 + 2 * n]), res[-1]


def _scatter_wait(send_sems, recv_sems, srcs, lands, after, name):
    n = len(srcs)
    extra = list(after)

    def body(*refs):
        s_refs, l_refs = refs[:n], refs[n:2 * n]
        ss, rs = refs[2 * n], refs[2 * n + 1]
        for cp in _scatter_copies(s_refs, l_refs, ss, rs):
            cp.wait_send()
            cp.wait_recv()

    res = pl.pallas_call(
        body, name=name,
        out_shape=[pltpu.HBM(g.shape, g.dtype) for g in srcs] + [pltpu.HBM(g.shape, g.dtype) for g in lands],
        in_specs=[_HBM] * (2 * n) + [_SEM, _SEM] + [pl.BlockSpec(memory_space=pl.ANY)] * len(extra),
        out_specs=[_HBM] * (2 * n),
        input_output_aliases={i: i for i in range(2 * n)},
        compiler_params=pltpu.CompilerParams(has_side_effects=_EFFECT),
    )(*srcs, *lands, send_sems, recv_sems, *extra)
    return res[:n], res[n:]


def _plane_copies(src, land, send_sems, recv_sems):
    x, y, c = lax.axis_index("x"), lax.axis_index("y"), lax.axis_index("c")
    return [pltpu.make_async_remote_copy(src_ref=src, dst_ref=land.at[2 * x + y, c], send_sem=send_sems.at[j],
                                         recv_sem=recv_sems.at[j], device_id=(cx, cy, c), device_id_type=MESH)
            for j, (cx, cy) in enumerate(_other_chips())]


def _plane_start(pack, land, name):
    def body(src, lnd, send_sems, recv_sems, _s, _l, token):
        for cp in _plane_copies(src, lnd, send_sems, recv_sems):
            cp.start()
        token[...] = jnp.zeros_like(token)

    res = pl.pallas_call(
        body, name=name,
        out_shape=[pltpu.SemaphoreType.DMA((3,)), pltpu.SemaphoreType.DMA((3,)), pltpu.HBM(pack.shape, pack.dtype),
                   pltpu.HBM(land.shape, land.dtype), jax.ShapeDtypeStruct((8, 128), f32)],
        in_specs=[_HBM, _HBM], out_specs=[_SEM, _SEM, _HBM, _HBM, pl.BlockSpec(memory_space=pltpu.VMEM)],
        input_output_aliases={0: 2, 1: 3},
        compiler_params=pltpu.CompilerParams(has_side_effects=_EFFECT),
    )(pltpu.with_memory_space_constraint(pack, pltpu.HBM), pltpu.with_memory_space_constraint(land, pltpu.HBM))
    return res[:4], res[4]


def _plane_wait(send_sems, recv_sems, pack, land, after, name):
    def body(src, lnd, ss, rs, *_):
        for cp in _plane_copies(src, lnd, ss, rs):
            cp.wait_send()
            cp.wait_recv()

    return pl.pallas_call(
        body, name=name,
        out_shape=[pltpu.HBM(pack.shape, pack.dtype), pltpu.HBM(land.shape, land.dtype)],
        in_specs=[_HBM, _HBM, _SEM, _SEM] + [pl.BlockSpec(memory_space=pl.ANY)] * len(after),
        out_specs=[_HBM, _HBM], input_output_aliases={0: 0, 1: 1},
        compiler_params=pltpu.CompilerParams(has_side_effects=_EFFECT),
    )(pack, land, send_sems, recv_sems, *after)[1]


def _swap_halves(fulls):
    n = len(fulls)

    def body(*refs):
        ins, outs = refs[:n], refs[n:2 * n]
        send_sems, recv_sems = refs[2 * n:]
        c = lax.axis_index("c")
        sibling = (lax.axis_index("x"), lax.axis_index("y"), 1 - c)
        copies = []
        for a in range(n):
            cp = pltpu.make_async_remote_copy(src_ref=outs[a].at[:, c], dst_ref=outs[a].at[:, c], send_sem=send_sems.at[a],
                                              recv_sem=recv_sems.at[a], device_id=sibling, device_id_type=MESH)
            cp.start()
            copies.append(cp)
        for a, cp in enumerate(copies):
            cp.wait_send()
            theirs = outs[a].at[:, 1 - c]
            pltpu.make_async_remote_copy(src_ref=theirs, dst_ref=theirs, send_sem=send_sems.at[a], recv_sem=recv_sems.at[a],
                                         device_id=sibling, device_id_type=MESH).wait_recv()

    hbm = pl.BlockSpec(memory_space=pl.ANY)
    return pl.pallas_call(
        body, name="swap_halves",
        out_shape=[jax.ShapeDtypeStruct(p.shape, p.dtype) for p in fulls],
        in_specs=[hbm] * n, out_specs=[hbm] * n,
        input_output_aliases={a: a for a in range(n)},
        scratch_shapes=[pltpu.SemaphoreType.DMA((n,)), pltpu.SemaphoreType.DMA((n,))],
    )(*fulls)


def _to_segments(t):
    s, c = t.shape
    return t.reshape(SCAN_SEG, s // SCAN_SEG, c).transpose(1, 0, 2).reshape(s, c)


def _from_segments(t):
    s, c = t.shape
    return t.reshape(s // SCAN_SEG, SCAN_SEG, c).transpose(1, 0, 2).reshape(s, c)


def _ssm_operators(a_re, a_im, log_dt, b_re, b_im, c_re, c_im):
    lam = lax.complex(a_re, a_im)
    dt = jnp.exp(log_dt)[:, None]
    a_bar = jnp.exp(lam * dt)
    b_bar = ((a_bar - 1.0) / lam)[:, :, None] * lax.complex(b_re, b_im)
    eye = jnp.eye(N_GROUPS, dtype=f32)

    def embed_b(t):
        return (jnp.transpose(t, (0, 2, 1))[:, :, None, :] * eye[:, None, :, None]).reshape(D_SSM, D_STATE)

    def embed_c(t):
        return (jnp.transpose(t, (0, 2, 1))[:, :, None, :] * eye[:, None, :, None]).reshape(D_STATE, D_SSM)

    a2 = jnp.stack([a_bar.real.reshape(D_STATE), a_bar.imag.reshape(D_STATE)])
    return a2, embed_b(b_bar.real), embed_b(b_bar.imag), embed_c(c_re), embed_c(c_im)


def _local_step(x, target, mod, small, ffn_weights, mix_weights, grads_done, ffn_bwd_issued):
    table = jnp.asarray(_bucket_table())
    bias = _bias_fwd(small["rel_bias"], table)
    L = DEPTH
    saved = []
    ssm_ops = []
    for l in range(L):
        sv = {}
        m9 = mod[l]
        sv["x0"] = x
        sv["w0"] = ffn_weights(l, 0, x)
        x, sv["f0"], sv["g0"], sv["u0"], sv["h0"] = _ffn_fwd(x, m9[0:3], *sv["w0"], 0, small["ln_g"][l, 0:1], small["ln_b"][l, 0:1])
        sv["x1"] = x
        sv["w1"] = mix_weights(l, x)
        *qkv, z_rest, sv["h1"] = _mix_in_fwd(x, m9[3:6], sv["w1"][0], 0)
        S = x.shape[0]
        qkv = [t.reshape(3, S, D_ATT) for t in qkv]
        att = [_att_fwd(qkv[b], bias, b) for b in range(3)]
        y_att, lse3 = _att_merge([att[b][0].reshape(d, S // d, D_ATT) for b, d in enumerate(DILATIONS)],
                                 [att[b][1].reshape(d, S // d, _LANES) for b, d in enumerate(DILATIONS)])
        sv.update(qkv=qkv, lse=[a[1] for a in att], lse3=lse3, y_att=y_att)

        prm = tuple(small[k][l] for k in ("ssm_a_re", "ssm_a_im", "ssm_log_dt", "ssm_b_re", "ssm_b_im", "ssm_c_re", "ssm_c_im"))
        (a2, bre, bim, cre, cim), ops_vjp = jax.vjp(_ssm_operators, *prm)
        ssm_ops.append(ops_vjp)
        u_ssm = _to_segments(z_rest[:, :D_SSM])
        sr, si = _ssm_states(u_ssm, bre, bim, a2)
        dskip = small["ssm_d"][l][None, :]
        glu_b = small["glu_b"][l][None, :]
        out_seg, y_seg = _ssm_out(sr, si, u_ssm, cre, cim, dskip, small["glu_w"][l], glu_b)
        y_ssm = _from_segments(out_seg)
        sv.update(a2=a2, bre=bre, bim=bim, cre=cre, cim=cim, u_ssm=u_ssm, sr=sr, si=si, y_seg=y_seg, y_ssm=y_ssm)

        u_pool = jnp.concatenate([jnp.zeros((POOL_HALO, D_POOL), f32), z_rest[:, D_SSM:]])
        y_pool = _pool_fwd(u_pool, small["pool_w"][l], small["pool_scale"][l][None, :])
        sv.update(u_pool=u_pool, y_pool=y_pool)

        x, sv["ymix"] = _mix_out_fwd(x, y_att, y_ssm, y_pool, m9[3:6], sv["w1"][1], 0, small["ln_g"][l, 1:2], small["ln_b"][l, 1:2])
        sv["x2"] = x
        sv["w2"] = ffn_weights(l, 1, x)
        x, sv["f2"], sv["g2"], sv["u2"], sv["h2"] = _ffn_fwd(x, m9[6:9], *sv["w2"], 0, small["ln_g"][l, 2:3], small["ln_b"][l, 2:3])
        saved.append(sv)

    loss, dx = _loss_head(x, target)

    dmod = [None] * L
    dln_g = [None] * L
    dln_b = [None] * L
    sg = {k: [None] * L for k in ("ssm_a_re", "ssm_a_im", "ssm_log_dt", "ssm_b_re", "ssm_b_im", "ssm_c_re", "ssm_c_im",
                                  "ssm_d", "glu_w", "glu_b", "pool_w", "pool_scale")}
    dbias_tot = None
    order_after = jnp.zeros((), f32)
    for l in reversed(range(L)):
        sv = saved[l]
        m9 = mod[l] + order_after

        def fresh(like):
            return [lax.empty(t.shape, bf16) for t in like]

        dx, dg, du, a, df, dm2, dlg2, dlb2 = _ffn_bwd(dx, sv["x2"], sv["f2"], sv["g2"], sv["u2"], m9[6:9], *sv["w2"], 0,
                                                     small["ln_g"][l, 2:3])
        m9 = m9 + ffn_bwd_issued(l, 1, dx)
        g_ffn1 = _ffn_wgrad(sv["h2"], dg, du, a, df, *fresh(sv["w2"]), 0)
        dxr, d_att, d_ssm, d_pool, dgate1, dlg1, dlb1, g_w_out = _mix_out_bwd(
            dx, sv["x1"], sv["ymix"], sv["y_att"], sv["y_ssm"], sv["y_pool"], m9[3:6], sv["w1"][1], 0, small["ln_g"][l, 1:2],
            fresh(sv["w1"])[1])
        S = d_att.shape[0]
        merged = _att_merge_bwd(d_att, sv["y_att"], sv["lse3"])
        dqkv, dbias = [], []
        for b, d in enumerate(DILATIONS):
            dq_b, db_b = _att_bwd(sv["qkv"][b], merged[b].reshape(S, D_ATT), sv["lse"][b], merged[3 + b].reshape(S, _LANES), bias, b)
            dqkv.append(dq_b.reshape(3, d, S // d, D_ATT))
            dbias.append(db_b)
        dbias = jnp.stack(dbias)
        dbias_tot = dbias if dbias_tot is None else dbias_tot + dbias
        d_seg = _to_segments(d_ssm)
        dskip = small["ssm_d"][l][None, :]
        glu_b = small["glu_b"][l][None, :]
        dy_seg, du_skip, dcre, dcim, dd, dglu_b, dglu_w = _ssm_out_bwd(
            d_seg, sv["y_seg"], sv["u_ssm"], sv["sr"], sv["si"], dskip, small["glu_w"][l], glu_b)
        du_seg, dbre, dbim, da2 = _ssm_states_bwd(dy_seg, du_skip, sv["u_ssm"], sv["sr"], sv["si"], sv["cre"], sv["cim"],
                                                  sv["bre"], sv["bim"], sv["a2"])
        d_prm = ssm_ops[l]((da2, dbre, dbim, dcre, dcim))
        for k, v in zip(("ssm_a_re", "ssm_a_im", "ssm_log_dt", "ssm_b_re", "ssm_b_im", "ssm_c_re", "ssm_c_im"), d_prm):
            sg[k][l] = v
        sg["ssm_d"][l] = dd[0]
        sg["glu_b"][l] = dglu_b[0]
        sg["glu_w"][l] = dglu_w
        du_ssm = _from_segments(du_seg)
        dyp = jnp.concatenate([d_pool, jnp.zeros((POOL_HALO, D_POOL), f32)])
        du_pool, dpw, dps = _pool_bwd(dyp, sv["u_pool"], small["pool_w"][l], small["pool_scale"][l][None, :])
        sg["pool_w"][l] = dpw
        sg["pool_scale"][l] = dps[0]
        d_rest = jnp.concatenate([du_ssm, du_pool], axis=1).astype(bf16)
        dx, dm1, dz = _mix_in_bwd(dqkv, d_rest, dxr, sv["x1"], m9[3:6], sv["w1"][0], 0)
        g_w_in = _mix_in_wgrad(sv["h1"], dz, fresh(sv["w1"])[0], 0)
        ffn_names = ("ffn_w_gate", "ffn_w_up", "ffn_w_down")
        m9 = m9 + grads_done(l, 1, list(zip(ffn_names, [(2 * l + 1) * FF_SHARD] * 3, g_ffn1))
                             + [("w_in", l * D_MODEL, g_w_in), ("w_out", l * 256, g_w_out)])
        dm1 = jnp.concatenate([dm1[0:2], dgate1])
        dx, dg, du, a, df, dm0, dlg0, dlb0 = _ffn_bwd(dx, sv["x0"], sv["f0"], sv["g0"], sv["u0"], m9[0:3], *sv["w0"], 0,
                                                     small["ln_g"][l, 0:1])
        issued = ffn_bwd_issued(l, 0, dx)
        g_ffn0 = _ffn_wgrad(sv["h0"], dg, du, a, df, *fresh(sv["w0"]), 0)
        order_after = grads_done(l, 0, list(zip(ffn_names, [2 * l * FF_SHARD] * 3, g_ffn0))) + issued
        dmod[l] = jnp.concatenate([dm0 + issued, dm1, dm2])
        dln_g[l] = jnp.concatenate([dlg0, dlg1, dlg2])
        dln_b[l] = jnp.concatenate([dlb0, dlb1, dlb2])

    small_grads = {k: jnp.stack(v) for k, v in sg.items()}
    small_grads["rel_bias"] = _bias_bwd(dbias_tot, table)
    small_grads["ln_g"] = jnp.stack(dln_g)
    small_grads["ln_b"] = jnp.stack(dln_b)
    return loss, dx, jnp.stack(dmod), small_grads


_TILE_ELEMS = 8 * 128


def _pack_rows(shapes):
    out, row = [], 0
    for s in shapes:
        nr = -(-int(np.prod(s)) // _TILE_ELEMS) * 8
        out.append((row, nr))
        row += nr
    return out


def _pack(arrs):
    parts = []
    for a in arrs:
        flat = a.reshape(-1).astype(f32)
        npad = -(-flat.shape[0] // _TILE_ELEMS) * _TILE_ELEMS
        parts.append(jnp.pad(flat, (0, npad - flat.shape[0])).reshape(npad // 128, 128))
    return jnp.concatenate(parts, axis=0)


def _unpack(buf, shapes):
    return [buf[row:row + nr].reshape(-1)[:int(np.prod(s))].reshape(s) for s, (row, nr) in zip(shapes, _pack_rows(shapes))]


_REPL = ("rel_bias", "ada_b", "ssm_a_re", "ssm_a_im", "ssm_log_dt", "ssm_b_re", "ssm_b_im", "ssm_c_re", "ssm_c_im",
         "ssm_d", "glu_b", "pool_w", "pool_scale")
_SMALL_SHARDED = ("ln_g", "ln_b", "glu_w")
_BIG = ("ffn_w_gate", "ffn_w_up", "ffn_w_down", "w_in", "w_out")
_ORDER = ("rel_bias", "ada_w", "ada_b", "ln_g", "ln_b", "ffn_w_gate", "ffn_w_up", "ffn_w_down", "w_in", "w_out",
          "ssm_a_re", "ssm_a_im", "ssm_log_dt", "ssm_b_re", "ssm_b_im", "ssm_c_re", "ssm_c_im", "ssm_d", "glu_w",
          "glu_b", "pool_w", "pool_scale")


def kernel(x, c, rel_bias, ada_w, ada_b, ln_g, ln_b, ffn_w_gate, ffn_w_up, ffn_w_down, w_in, w_out, ssm_a_re, ssm_a_im, ssm_log_dt, ssm_b_re, ssm_b_im, ssm_c_re, ssm_c_im, ssm_d, glu_w, glu_b, pool_w, pool_scale, loss_target, m_rel_bias, m_ada_w, m_ada_b, m_ln_g, m_ln_b, m_ffn_w_gate, m_ffn_w_up, m_ffn_w_down, m_w_in, m_w_out, m_ssm_a_re, m_ssm_a_im, m_ssm_log_dt, m_ssm_b_re, m_ssm_b_im, m_ssm_c_re, m_ssm_c_im, m_ssm_d, m_glu_w, m_glu_b, m_pool_w, m_pool_scale, v_rel_bias, v_ada_w, v_ada_b, v_ln_g, v_ln_b, v_ffn_w_gate, v_ffn_w_up, v_ffn_w_down, v_w_in, v_w_out, v_ssm_a_re, v_ssm_a_im, v_ssm_log_dt, v_ssm_b_re, v_ssm_b_im, v_ssm_c_re, v_ssm_c_im, v_ssm_d, v_glu_w, v_glu_b, v_pool_w, v_pool_scale):
    args = dict(locals())
    w = {k: args[k] for k in _ORDER}
    m = {k: args["m_" + k] for k in _ORDER}
    v = {k: args["v_" + k] for k in _ORDER}
    L, D = DEPTH, D_MODEL
    ax, ay, ac = lax.axis_index("x"), lax.axis_index("y"), lax.axis_index("c")
    p_me = 2 * ax + ay
    dev = 4 * ax + 2 * ay + ac

    transposed = ("ffn_w_gate", "ffn_w_up")
    for d in (w, m, v):
        for name in transposed:
            d[name] = jnp.swapaxes(d[name], 2, 3)

    def halves(t):
        return t.astype(bf16).reshape(1, 2, t.shape[0] // 2, t.shape[1])

    def landing(src):
        return lax.dynamic_update_slice(lax.empty((N_CHIPS,) + src.shape, bf16), src[None], (p_me, 0, 0, 0, 0))

    chunk_keys = [("ffn", 0, 0), ("mix", 0), ("ffn", 0, 1), ("ffn", 1, 0), ("mix", 1), ("ffn", 1, 1)]
    chunk_srcs = []
    for key in chunk_keys:
        if key[0] == "ffn":
            chunk_srcs.append([halves(w[name][key[1], key[2]]) for name in ("ffn_w_gate", "ffn_w_up", "ffn_w_down")])
        else:
            chunk_srcs.append([halves(w_in[key[1]]), halves(w_out[key[1]])])

    pack = _pack([c, ln_g, ln_b, glu_w])
    rows = pack.shape[0]
    allp = _allgather8(pack).reshape(8, rows, 128)
    chunks = [(srcs, [landing(t) for t in srcs]) for srcs in chunk_srcs]
    first_in_flight, first_begun = _gather_start(chunks[:1], allp, "gather_start_first")
    c_all = allp[:, :8].reshape(8, D) + first_begun[0, 0]
    by_chip = allp[0::2]

    fwd_rows = _pack_rows([c.shape, ln_g.shape, ln_b.shape, glu_w.shape])

    def sharded(part, shape, axis):
        row0, nrows = fwd_rows[part]
        t = by_chip[:, row0:row0 + nrows].reshape(N_CHIPS, -1)[:, :int(np.prod(shape))].reshape((N_CHIPS,) + shape)
        return jnp.concatenate([t[p] for p in range(N_CHIPS)], axis=axis)

    ln_g_full = sharded(1, ln_g.shape, 2)
    ln_b_full = sharded(2, ln_b.shape, 2)
    glu_w_full = sharded(3, glu_w.shape, 1)

    ncol = ada_w.shape[-1]
    ada_b_cols = lax.dynamic_slice_in_dim(ada_b, p_me * ncol, ncol, axis=1)[:, None, :]
    mod_part = _ada_fwd(c_all, ada_w, ada_b_cols)
    mrows = L * 8 * ncol // 128
    mod_all = _allgather8(mod_part.reshape(mrows, 128)).reshape(8, L, 8, ncol)
    mod_mine = lax.dynamic_index_in_dim(mod_all, dev, axis=2, keepdims=False)
    mod = jnp.concatenate([mod_mine[2 * p] for p in range(N_CHIPS)], axis=-1).reshape(L, 9, D)

    rest_in_flight, rest_begun = _gather_start(chunks[1:], mod, "gather_start_rest")
    in_flight = first_in_flight + rest_in_flight

    forwarding = {}

    def forward(k, after):
        lands = _gather_wait(*in_flight[k], [after, rest_begun], "gather_wait_%d" % k)
        forwarding[k], begun = _split_start(_forward_copies, lands, 3 * len(lands), "gather_forward_start_%d" % k)
        return begun

    def gathered(key, after):
        k = chunk_keys.index(key)
        order = [after]
        if k not in forwarding:
            order.append(forward(k, after))
        if 3 <= k + 1 < len(chunk_keys):
            order.append(forward(k + 1, after))
        lands = _split_wait(_forward_copies, *forwarding[k], order, "gather_forward_wait_%d" % k)
        return [t.reshape(N_CHIPS, 1, 2 * t.shape[3], t.shape[4]) for t in lands]

    pc = jnp.stack([p_me, ac]).astype(jnp.int32)
    groups = {}
    scattering = {}

    pairing = {}

    def start_pairs(tag, after=()):
        g5 = [g.reshape(g.shape[:2] + (2, g.shape[2] // 2, g.shape[3])) for _, _, g in groups[tag]]
        gots = [lax.empty(g.shape[:2] + g.shape[3:], bf16) for g in g5]
        pairing[tag], begun = _split_start(_pair_copies, g5 + gots, len(g5), "pair_exchange_start_%s" % tag, after)
        return begun

    def start_group(tag, after):
        arrays = _split_wait(_pair_copies, *pairing[tag], after, "pair_exchange_wait_%s" % tag)
        n = len(arrays) // 2
        hsum = _pair_sum(arrays[:n], arrays[n:], pc)
        scattering[tag], begun = _scatter_start(hsum, "scatter_start_%s" % tag)
        return begun

    def grads_done(l, s, grads):
        if l == 1:
            groups.setdefault("l1", []).extend(grads)
            return start_pairs("l1")[0, 0] if s == 0 else jnp.zeros((), f32)
        groups["l0a" if s == 1 else "l0b"] = grads
        return start_pairs("l0a")[0, 0] if s == 1 else jnp.zeros((), f32)

    def ffn_bwd_issued(l, s, dx):
        if l == 0:
            return start_group("l1" if s == 1 else "l0a", [dx])[0, 0]
        return jnp.zeros((), f32)

    small = {k: w[k] for k in _REPL if k != "ada_b"}
    small.update(ln_g=ln_g_full, ln_b=ln_b_full, glu_w=glu_w_full)
    loss_dev, grad_x, dmod, sgrads = _local_step(
        x[0], loss_target[0], mod, small, lambda l, s, after: gathered(("ffn", l, s), after),
        lambda l, after: gathered(("mix", l), after), grads_done, ffn_bwd_issued)
    loss = lax.psum(loss_dev[0, 0], ("x", "y", "c"))

    names = ("rel_bias", "ln_g", "ln_b", "ssm_a_re", "ssm_a_im", "ssm_log_dt", "ssm_b_re", "ssm_b_im", "ssm_c_re",
             "ssm_c_im", "ssm_d", "glu_w", "glu_b", "pool_w", "pool_scale")
    gpack = _pack([dmod] + [sgrads[k] for k in names])
    grows = gpack.shape[0]
    land = lax.dynamic_update_slice(lax.empty((N_CHIPS, 2, grows, 128), f32), gpack[None, None], (p_me, ac, 0, 0))
    small_in_flight, small_begun = _plane_start(gpack, land, "small_grads_start")
    l0b_begun = start_group("l0b", [start_pairs("l0b", (small_begun,))])

    out_g, out_d, out_m, out_v = {}, {}, {}, {}
    row_tile = dict(zip(_BIG, (352, 352, 352, 256, 256)))
    big = {name: None for name in _BIG}

    swapping = {}

    def reduce_group(tag, after):
        hsum, recv = _scatter_wait(*scattering[tag], after, "scatter_wait_%s" % tag)
        full = _sum_shards(hsum, recv, pc)
        swapping[tag], begun = _split_start(_swap_copies, full, len(full), "swap_halves_start_%s" % tag)
        return [begun]

    def update_group(tag, after):
        full = _split_wait(_swap_copies, *swapping[tag], after, "swap_halves_wait_%s" % tag)
        for (name, row0, _), g in zip(groups[tag], full):
            shp = w[name].shape
            r2 = (int(np.prod(shp[:-1])), shp[-1])
            big[name] = _adamw(w[name].reshape(r2), m[name].reshape(r2), v[name].reshape(r2), g.reshape(-1, shp[-1]),
                               row_tile[name], row0, big[name])
        return [big[name][1] for name, _, _ in groups[tag]]

    after = reduce_group("l0a", reduce_group("l1", [grad_x, l0b_begun]))
    after = update_group("l0a", update_group("l1", after))

    land = _plane_wait(*small_in_flight, after, "small_grads_wait")
    gall = _swap_halves([land])[0].reshape(8, grows, 128)
    gsum = _unpack(_sum8(gall), [(L, 9 * D)] + [sgrads[k].shape for k in names])
    red = dict(zip(("ada_b",) + names, gsum))
    red["ln_g"] = lax.dynamic_slice_in_dim(red["ln_g"], p_me * 256, 256, axis=2)
    red["ln_b"] = lax.dynamic_slice_in_dim(red["ln_b"], p_me * 256, 256, axis=2)
    red["glu_w"] = lax.dynamic_slice_in_dim(red["glu_w"], p_me * 64, 64, axis=1)

    dmod_all = gall[:, :L * 9 * D // 128].reshape(8, L, 9 * D)
    dmod_cols = jnp.transpose(lax.dynamic_slice_in_dim(dmod_all, p_me * ncol, ncol, axis=2), (1, 0, 2))
    g_ada_w = _ada_wgrad(jnp.transpose(c_all), dmod_cols)

    r2 = (L * D, ncol)
    res = _adamw(ada_w.reshape(r2), m["ada_w"].reshape(r2), v["ada_w"].reshape(r2), g_ada_w.reshape(r2), 128)
    out_g["ada_w"], out_d["ada_w"], out_m["ada_w"], out_v["ada_w"] = [t.reshape(ada_w.shape) for t in res]

    small_names = _REPL + _SMALL_SHARDED
    wp = _pack([w[k] for k in small_names])
    res_small = _adamw(wp, _pack([m[k] for k in small_names]), _pack([v[k] for k in small_names]),
                       _pack([red[k] for k in small_names]), wp.shape[0])
    for t, dst in zip(res_small, (out_g, out_d, out_m, out_v)):
        for k, a in zip(small_names, _unpack(t, [w[k].shape for k in small_names])):
            dst[k] = a

    update_group("l0b", reduce_group("l0b", [res_small[1], res[1]]))
    for name in _BIG:
        res = [t.reshape(w[name].shape) for t in big[name]]
        out_g[name], out_d[name], out_m[name], out_v[name] = [jnp.swapaxes(t, 2, 3) for t in res] if name in transposed else res

    return (loss, grad_x[None], *[out_g[k] for k in _ORDER], *[out_d[k] for k in _ORDER],
            *[out_m[k] for k in _ORDER], *[out_v[k] for k in _ORDER])
```

```python
import functools
import math

import numpy as np
import jax
import jax.numpy as jnp
from jax import lax
from jax.experimental import pallas as pl
from jax.experimental.pallas import tpu as pltpu

f32 = jnp.float32
bf16 = jnp.bfloat16
MESH = pl.DeviceIdType.MESH

D_MODEL = 1024
SEQ = 2048
DEPTH = 2
HEAD_DIM = 64
N_HEADS = 8
D_ATT = 512
DILATIONS = (1, 4, 16)
BLOCKS_PER_RESIDUE = (16, 4, 1)
ATT_BLOCK = 128
N_UNITS = SEQ // ATT_BLOCK
N_GROUPS = 16
SSM_GROUP = 16
SSM_STATE = 64
D_SSM = 256
D_STATE = N_GROUPS * SSM_STATE
POOL_WINDOWS = (2, 4, 8, 16)
POOL_GROUP = 64
D_POOL = 256
POOL_HALO = 16
D_FF = 2816
N_BUCKETS = 32
MAX_DISTANCE = 2048
ALPHA = (2 * DEPTH) ** 0.25
FFN_RES = 0.5
LN_EPS = 1e-5
NEG = -1e30
N_CHIPS = 4
FF_SHARD = D_FF // N_CHIPS
SCAN_SEG = 8
SCAN_STEPS = SEQ // SCAN_SEG

ADAM_LR, ADAM_B1, ADAM_B2, ADAM_EPS, ADAM_WD, ADAM_STEP = 0.001, 0.9, 0.999, 1e-08, 0.01, 10

TOK_TILE = 512


def _cp(dims=None, vmem_mb=None):
    kw = {}
    if dims is not None:
        kw["dimension_semantics"] = dims
    if vmem_mb is not None:
        kw["vmem_limit_bytes"] = vmem_mb << 20
    return pltpu.CompilerParams(**kw)


def _dot(a, b):
    return jnp.dot(a, b, preferred_element_type=f32)


def _dot_nt(a, b):
    return lax.dot_general(a, b, (((1,), (1,)), ((), ())), preferred_element_type=f32)


def _dot_tn(a, b):
    return lax.dot_general(a, b, (((0,), (0,)), ((), ())), preferred_element_type=f32)


def _ln_stats(v):
    mu = jnp.mean(v, -1, keepdims=True)
    d = v - mu
    var = jnp.mean(d * d, -1, keepdims=True)
    rstd = lax.rsqrt(var + LN_EPS)
    return d * rstd, rstd


def _ln_bwd(dxh, xh, rstd):
    return rstd * (dxh - jnp.mean(dxh, -1, keepdims=True) - xh * jnp.mean(dxh * xh, -1, keepdims=True))


_GELU_C = math.sqrt(2.0 / math.pi)


def _gelu(y):
    return 0.5 * y * (1.0 + jnp.tanh(_GELU_C * (y + 0.044715 * y * y * y)))


def _gelu_grad(y):
    t = jnp.tanh(_GELU_C * (y + 0.044715 * y * y * y))
    return 0.5 * (1.0 + t) + 0.5 * y * (1.0 - t * t) * (_GELU_C * (1.0 + 3 * 0.044715 * y * y))


def _full(shape):
    return pl.BlockSpec(shape, lambda *_: (0,) * len(shape))


def _hbm(*args):
    return [pltpu.with_memory_space_constraint(a, pltpu.HBM) if getattr(a, "ndim", 0) >= 2 else a for a in args]


def _ffn_fwd(x, mod3, wg, wu, wd, ls, lng, lnb):
    S, D = x.shape
    Fs = wg.shape[-2]
    ts = TOK_TILE

    def body(x_ref, mod_ref, wg_ref, wu_ref, wd_ref, lng_ref, lnb_ref, xo_ref, f_ref, g_ref, u_ref, h_ref, acc_sc):
        j = pl.program_id(1)

        @pl.when(j == 0)
        def _():
            xh, _ = _ln_stats(x_ref[...])
            h_ref[...] = (xh * (1.0 + mod_ref[1:2, :]) + mod_ref[0:1, :]).astype(bf16)
            acc_sc[...] = jnp.zeros_like(acc_sc)

        h = h_ref[...]
        g = _dot_nt(h, wg_ref[0, 0])
        u = _dot_nt(h, wu_ref[0, 0])
        g_ref[0] = g.astype(bf16)
        u_ref[0] = u.astype(bf16)
        a = (g * jax.nn.sigmoid(g) * u).astype(bf16)
        acc_sc[...] += _dot(a, wd_ref[0, 0])

        @pl.when(j == N_CHIPS - 1)
        def _():
            f = acc_sc[...]
            f_ref[...] = f
            r = ALPHA * x_ref[...] + (FFN_RES * mod_ref[2:3, :]) * f
            rh, _ = _ln_stats(r)
            xo_ref[...] = rh * lng_ref[...] + lnb_ref[...]

    tok = pl.BlockSpec((ts, D), lambda i, j: (i, 0))
    wrow = pl.BlockSpec((1, 1, Fs, D), lambda i, j: (j, ls, 0, 0))
    hid = pl.BlockSpec((1, ts, Fs), lambda i, j: (j, i, 0))
    return pl.pallas_call(
        body, name="ffn_fwd", grid=(S // ts, N_CHIPS),
        in_specs=[tok, _full((3, D)), wrow, wrow, wrow, _full((1, D)), _full((1, D))],
        out_specs=[tok, tok, hid, hid, tok],
        out_shape=[jax.ShapeDtypeStruct((S, D), f32), jax.ShapeDtypeStruct((S, D), f32),
                   jax.ShapeDtypeStruct((N_CHIPS, S, Fs), bf16), jax.ShapeDtypeStruct((N_CHIPS, S, Fs), bf16),
                   jax.ShapeDtypeStruct((S, D), bf16)],
        scratch_shapes=[pltpu.VMEM((ts, D), f32)],
        compiler_params=_cp(("parallel", "arbitrary"), 56),
    )(*_hbm(x, mod3, wg, wu, wd, lng, lnb))


def _ffn_bwd(dxo, x, f, g, u, mod3, wg, wu, wd, ls, lng):
    S, D = x.shape
    Fs = wg.shape[-2]
    ts = TOK_TILE

    def body(dxo_ref, x_ref, f_ref, g_ref, u_ref, mod_ref, wg_ref, wu_ref, wd_ref, lng_ref,
             dx_ref, dg_ref, du_ref, a_ref, df_ref, dmod_ref, dlng_ref, dlnb_ref,
             dr_sc, df_sc, acc_sc):
        i = pl.program_id(0)
        j = pl.program_id(1)

        @pl.when((i == 0) & (j == 0))
        def _():
            dmod_ref[...] = jnp.zeros_like(dmod_ref)
            dlng_ref[...] = jnp.zeros_like(dlng_ref)
            dlnb_ref[...] = jnp.zeros_like(dlnb_ref)

        @pl.when(j == 0)
        def _():
            xv = x_ref[...]
            fv = f_ref[...]
            gate = mod_ref[2:3, :]
            rh, rstd = _ln_stats(ALPHA * xv + (FFN_RES * gate) * fv)
            dy = dxo_ref[...]
            dlng_ref[...] += jnp.sum(dy * rh, 0, keepdims=True)
            dlnb_ref[...] += jnp.sum(dy, 0, keepdims=True)
            dr = _ln_bwd(dy * lng_ref[...], rh, rstd)
            dr_sc[...] = dr
            dmod_ref[2:3, :] += jnp.sum(FFN_RES * dr * fv, 0, keepdims=True)
            df = ((FFN_RES * gate) * dr).astype(bf16)
            df_sc[...] = df
            df_ref[...] = df
            acc_sc[...] = jnp.zeros_like(acc_sc)

        da = _dot_nt(df_sc[...], wd_ref[0, 0])
        gv = g_ref[0].astype(f32)
        uv = u_ref[0].astype(f32)
        sg = jax.nn.sigmoid(gv)
        si = gv * sg
        a_ref[0] = (si * uv).astype(bf16)
        dgv = (da * uv * (sg * (1.0 + gv * (1.0 - sg)))).astype(bf16)
        duv = (da * si).astype(bf16)
        dg_ref[0] = dgv
        du_ref[0] = duv
        acc_sc[...] += _dot(dgv, wg_ref[0, 0]) + _dot(duv, wu_ref[0, 0])

        @pl.when(j == N_CHIPS - 1)
        def _():
            dh = acc_sc[...]
            xh, rstd0 = _ln_stats(x_ref[...])
            dmod_ref[0:1, :] += jnp.sum(dh, 0, keepdims=True)
            dmod_ref[1:2, :] += jnp.sum(dh * xh, 0, keepdims=True)
            dx_ref[...] = _ln_bwd(dh * (1.0 + mod_ref[1:2, :]), xh, rstd0) + ALPHA * dr_sc[...]

    tok = pl.BlockSpec((ts, D), lambda i, j: (i, 0))
    wrow = pl.BlockSpec((1, 1, Fs, D), lambda i, j: (j, ls, 0, 0))
    hid = pl.BlockSpec((1, ts, Fs), lambda i, j: (j, i, 0))
    hid_shape = jax.ShapeDtypeStruct((N_CHIPS, S, Fs), bf16)
    return pl.pallas_call(
        body, name="ffn_bwd", grid=(S // ts, N_CHIPS),
        in_specs=[tok, tok, tok, hid, hid, _full((3, D)), wrow, wrow, wrow, _full((1, D))],
        out_specs=[tok, hid, hid, hid, tok, _full((3, D)), _full((1, D)), _full((1, D))],
        out_shape=[jax.ShapeDtypeStruct((S, D), f32), hid_shape, hid_shape, hid_shape,
                   jax.ShapeDtypeStruct((S, D), bf16),
                   jax.ShapeDtypeStruct((3, D), f32), jax.ShapeDtypeStruct((1, D), f32), jax.ShapeDtypeStruct((1, D), f32)],
        scratch_shapes=[pltpu.VMEM((ts, D), f32), pltpu.VMEM((ts, D), bf16), pltpu.VMEM((ts, D), f32)],
        compiler_params=_cp(("arbitrary", "arbitrary"), 56),
    )(*_hbm(dxo, x, f, g, u, mod3, wg, wu, wd, lng))


def _ffn_wgrad(h, dg, du, a, df, gwg, gwu, gwd, ls):
    S, D = h.shape
    Fs = dg.shape[-1]
    tk = TOK_TILE
    nk = S // tk

    def body(h_ref, dg_ref, du_ref, a_ref, df_ref, _g0, _g1, _g2, gwg_ref, gwu_ref, gwd_ref, ag_sc, au_sc, ad_sc):
        k = pl.program_id(1)

        @pl.when(k == 0)
        def _():
            ag_sc[...] = jnp.zeros_like(ag_sc)
            au_sc[...] = jnp.zeros_like(au_sc)
            ad_sc[...] = jnp.zeros_like(ad_sc)

        hv = h_ref[...]
        ag_sc[...] += _dot_tn(dg_ref[0], hv)
        au_sc[...] += _dot_tn(du_ref[0], hv)
        ad_sc[...] += _dot_tn(a_ref[0], df_ref[...])

        @pl.when(k == nk - 1)
        def _():
            gwg_ref[0, 0] = ag_sc[...].astype(bf16)
            gwu_ref[0, 0] = au_sc[...].astype(bf16)
            gwd_ref[0, 0] = ad_sc[...].astype(bf16)

    tok = pl.BlockSpec((tk, D), lambda p, k: (k, 0))
    hid = pl.BlockSpec((1, tk, Fs), lambda p, k: (p, k, 0))
    anyspec = pl.BlockSpec(memory_space=pl.ANY)
    orow = pl.BlockSpec((1, 1, Fs, D), lambda p, k: (p, ls, 0, 0))
    return pl.pallas_call(
        body, name="ffn_wgrad", grid=(N_CHIPS, nk),
        in_specs=[tok, hid, hid, hid, tok, anyspec, anyspec, anyspec],
        out_specs=[orow, orow, orow],
        out_shape=[jax.ShapeDtypeStruct(gwg.shape, bf16), jax.ShapeDtypeStruct(gwu.shape, bf16),
                   jax.ShapeDtypeStruct(gwd.shape, bf16)],
        scratch_shapes=[pltpu.VMEM((Fs, D), f32), pltpu.VMEM((Fs, D), f32), pltpu.VMEM((Fs, D), f32)],
        input_output_aliases={5: 0, 6: 1, 7: 2},
        compiler_params=_cp(("parallel", "arbitrary"), 48),
    )(*_hbm(h, dg, du, a, df, gwg, gwu, gwd))


_LANES = 128
_QKV_BLOCKS = D_ATT // _LANES


def _res_spec(lead, d, width, index):
    return pl.BlockSpec((lead, d, TOK_TILE // d, width), index)


def _res_spec3(d, width):
    return pl.BlockSpec((d, TOK_TILE // d, width), lambda i: (0, i, 0))


def _rows_to_residues(tile_bufs, d, put):
    for r in range(d):
        for cb, buf in enumerate(tile_bufs):
            put(r, cb, buf[pl.ds(r, TOK_TILE // d, stride=d), :])


def _residues_to_rows(tile_bufs, d, get):
    for r in range(d):
        for cb, buf in enumerate(tile_bufs):
            buf[pl.ds(r, TOK_TILE // d, stride=d), :] = get(r, cb)


def _mix_in_fwd(x, mod3, w_in, l):
    S, D = x.shape
    N = w_in.shape[-1]
    ts = TOK_TILE

    def body(x_ref, mod_ref, w_ref, o1_ref, o4_ref, o16_ref, zr_ref, h_ref, *bufs):
        j = pl.program_id(1)

        @pl.when(j == 0)
        def _():
            xh, _ = _ln_stats(x_ref[...])
            h_ref[...] = (xh * (1.0 + mod_ref[1:2, :]) + mod_ref[0:1, :]).astype(bf16)

        z = _dot(h_ref[...], w_ref[0, 0])

        @pl.when(j == N_CHIPS - 1)
        def _():
            zr_ref[...] = z

        @pl.when(j < N_CHIPS - 1)
        def _():
            zz = z * jnp.where(j == 0, HEAD_DIM ** -0.5, 1.0)
            o1_ref[j, 0] = zz.astype(bf16)
            for cb, buf in enumerate(bufs):
                buf[...] = zz[:, _LANES * cb:_LANES * (cb + 1)]
            for d, o_ref in zip(DILATIONS[1:], (o4_ref, o16_ref)):
                def put(r, cb, piece, o_ref=o_ref):
                    o_ref[j, r, :, _LANES * cb:_LANES * (cb + 1)] = piece.astype(bf16)
                _rows_to_residues(bufs, d, put)

    tok = pl.BlockSpec((ts, D), lambda i, j: (i, 0))
    res = [_res_spec(3, d, N, lambda i, j: (0, 0, i, 0)) for d in DILATIONS]
    return pl.pallas_call(
        body, name="mix_in_fwd", grid=(S // ts, N_CHIPS),
        in_specs=[tok, _full((3, D)), pl.BlockSpec((1, 1, D, N), lambda i, j: (j, l, 0, 0))],
        out_specs=res + [pl.BlockSpec((ts, N), lambda i, j: (i, 0)), tok],
        out_shape=[jax.ShapeDtypeStruct((3, d, S // d, N), bf16) for d in DILATIONS]
        + [jax.ShapeDtypeStruct((S, N), f32), jax.ShapeDtypeStruct((S, D), bf16)],
        scratch_shapes=[pltpu.VMEM((ts, _LANES), f32)] * _QKV_BLOCKS,
        compiler_params=_cp(("parallel", "arbitrary"), 40),
    )(*_hbm(x, mod3, w_in))


def _mix_in_bwd(dqkv, d_rest, dx_res, x, mod3, w_in, l):
    S, D = x.shape
    N = w_in.shape[-1]
    ts = TOK_TILE

    def body(d1_ref, d4_ref, d16_ref, dr_ref, dxr_ref, x_ref, mod_ref, w_ref, dx_ref, dmod_ref, dz_ref, acc_sc, *bufs):
        i = pl.program_id(0)
        j = pl.program_id(1)

        @pl.when((i == 0) & (j == 0))
        def _():
            dmod_ref[...] = jnp.zeros_like(dmod_ref)

        @pl.when(j == 0)
        def _():
            acc_sc[...] = jnp.zeros_like(acc_sc)

        @pl.when(j == N_CHIPS - 1)
        def _():
            dz_ref[0] = dr_ref[...]

        @pl.when(j < N_CHIPS - 1)
        def _():
            for d, d_ref, tile_bufs in ((4, d4_ref, bufs[:_QKV_BLOCKS]), (16, d16_ref, bufs[_QKV_BLOCKS:])):
                _residues_to_rows(tile_bufs, d, lambda r, cb, d_ref=d_ref: d_ref[0, r, :, _LANES * cb:_LANES * (cb + 1)].astype(f32))
            for cb in range(_QKV_BLOCKS):
                cols = slice(_LANES * cb, _LANES * (cb + 1))
                dz_ref[0, :, cols] = (d1_ref[0, 0, :, cols].astype(f32) + bufs[cb][...] + bufs[_QKV_BLOCKS + cb][...]).astype(bf16)

        acc_sc[...] += _dot_nt(dz_ref[0], w_ref[0, 0])

        @pl.when(j == N_CHIPS - 1)
        def _():
            dh = acc_sc[...]
            xh, rstd0 = _ln_stats(x_ref[...])
            dmod_ref[0:1, :] += jnp.sum(dh, 0, keepdims=True)
            dmod_ref[1:2, :] += jnp.sum(dh * xh, 0, keepdims=True)
            dx_ref[...] = _ln_bwd(dh * (1.0 + mod_ref[1:2, :]), xh, rstd0) + dxr_ref[...]

    tok = pl.BlockSpec((ts, D), lambda i, j: (i, 0))
    res = [_res_spec(1, d, N, lambda i, j: (jnp.minimum(j, 2), 0, i, 0)) for d in DILATIONS]
    return pl.pallas_call(
        body, name="mix_in_bwd", grid=(S // ts, N_CHIPS),
        in_specs=res + [pl.BlockSpec((ts, N), lambda i, j: (i, 0)), tok, tok, _full((3, D)),
                        pl.BlockSpec((1, 1, D, N), lambda i, j: (j, l, 0, 0))],
        out_specs=[tok, _full((3, D)), pl.BlockSpec((1, ts, N), lambda i, j: (j, i, 0))],
        out_shape=[jax.ShapeDtypeStruct((S, D), f32), jax.ShapeDtypeStruct((3, D), f32),
                   jax.ShapeDtypeStruct((N_CHIPS, S, N), bf16)],
        scratch_shapes=[pltpu.VMEM((ts, D), f32)] + [pltpu.VMEM((ts, _LANES), f32)] * (2 * _QKV_BLOCKS),
        compiler_params=_cp(("arbitrary", "arbitrary"), 40),
    )(*_hbm(*dqkv, d_rest, dx_res, x, mod3, w_in))


def _mix_in_wgrad(h, dz, gw, l):
    S, D = h.shape
    N = dz.shape[-1]
    tk = TOK_TILE
    nk = S // tk

    def body(h_ref, dz_ref, _g, gw_ref, acc_sc):
        k = pl.program_id(1)

        @pl.when(k == 0)
        def _():
            acc_sc[...] = jnp.zeros_like(acc_sc)

        acc_sc[...] += _dot_tn(h_ref[...], dz_ref[0])

        @pl.when(k == nk - 1)
        def _():
            gw_ref[0, 0] = acc_sc[...].astype(bf16)

    return pl.pallas_call(
        body, name="mix_in_wgrad", grid=(N_CHIPS, nk),
        in_specs=[pl.BlockSpec((tk, D), lambda p, k: (k, 0)), pl.BlockSpec((1, tk, N), lambda p, k: (p, k, 0)),
                  pl.BlockSpec(memory_space=pl.ANY)],
        out_specs=pl.BlockSpec((1, 1, D, N), lambda p, k: (p, l, 0, 0)),
        out_shape=jax.ShapeDtypeStruct(gw.shape, bf16),
        scratch_shapes=[pltpu.VMEM((D, N), f32)],
        input_output_aliases={2: 0},
        compiler_params=_cp(("parallel", "arbitrary"), 40),
    )(*_hbm(h, dz, gw))


def _mix_out_fwd(x, y_att, y_ssm, y_pool, mod3, w_out, l, lng, lnb):
    S, D = x.shape
    ts = TOK_TILE

    def body(x_ref, ya_ref, ys_ref, yp_ref, mod_ref, w_ref, lng_ref, lnb_ref, xo_ref, y_ref):
        ya = ya_ref[...].astype(bf16)
        y = (_dot(ya[:, 0:256], w_ref[0, 0]) + _dot(ya[:, 256:512], w_ref[1, 0])
             + _dot(ys_ref[...].astype(bf16), w_ref[2, 0]) + _dot(yp_ref[...].astype(bf16), w_ref[3, 0]))
        y_ref[...] = y
        rh, _ = _ln_stats(ALPHA * x_ref[...] + mod_ref[2:3, :] * y)
        xo_ref[...] = rh * lng_ref[...] + lnb_ref[...]

    tok = pl.BlockSpec((ts, D), lambda i: (i, 0))
    return pl.pallas_call(
        body, name="mix_out_fwd", grid=(S // ts,),
        in_specs=[tok, pl.BlockSpec((ts, D_ATT), lambda i: (i, 0)), pl.BlockSpec((ts, D_SSM), lambda i: (i, 0)),
                  pl.BlockSpec((ts, D_POOL), lambda i: (i, 0)), _full((3, D)),
                  pl.BlockSpec((N_CHIPS, 1, 256, D), lambda i: (0, l, 0, 0)), _full((1, D)), _full((1, D))],
        out_specs=[tok, tok],
        out_shape=[jax.ShapeDtypeStruct((S, D), f32), jax.ShapeDtypeStruct((S, D), f32)],
        compiler_params=_cp(("parallel",), 40),
    )(*_hbm(x, y_att, y_ssm, y_pool, mod3, w_out, lng, lnb))


def _mix_out_bwd(dxo, x, y, y_att, y_ssm, y_pool, mod3, w_out, l, lng, gw_out):
    S, D = x.shape
    ts = TOK_TILE
    nt = S // ts

    def body(dxo_ref, x_ref, y_ref, ya_ref, ys_ref, yp_ref, mod_ref, w_ref, lng_ref, _g,
             dxr_ref, da_ref, ds_ref, dp_ref, dgate_ref, dlng_ref, dlnb_ref, gw_ref, acc_sc):
        i = pl.program_id(0)

        @pl.when(i == 0)
        def _():
            dgate_ref[...] = jnp.zeros_like(dgate_ref)
            dlng_ref[...] = jnp.zeros_like(dlng_ref)
            dlnb_ref[...] = jnp.zeros_like(dlnb_ref)
            acc_sc[...] = jnp.zeros_like(acc_sc)

        gate = mod_ref[2:3, :]
        yv = y_ref[...]
        rh, rstd = _ln_stats(ALPHA * x_ref[...] + gate * yv)
        dy_out = dxo_ref[...]
        dlng_ref[...] += jnp.sum(dy_out * rh, 0, keepdims=True)
        dlnb_ref[...] += jnp.sum(dy_out, 0, keepdims=True)
        dr = _ln_bwd(dy_out * lng_ref[...], rh, rstd)
        dxr_ref[...] = ALPHA * dr
        dgate_ref[...] += jnp.sum(dr * yv, 0, keepdims=True)
        dy = (gate * dr).astype(bf16)
        da_ref[:, 0:256] = _dot_nt(dy, w_ref[0, 0])
        da_ref[:, 256:512] = _dot_nt(dy, w_ref[1, 0])
        ds_ref[...] = _dot_nt(dy, w_ref[2, 0])
        dp_ref[...] = _dot_nt(dy, w_ref[3, 0])
        ya = ya_ref[...].astype(bf16)
        acc_sc[0] += _dot_tn(ya[:, 0:256], dy)
        acc_sc[1] += _dot_tn(ya[:, 256:512], dy)
        acc_sc[2] += _dot_tn(ys_ref[...].astype(bf16), dy)
        acc_sc[3] += _dot_tn(yp_ref[...].astype(bf16), dy)

        @pl.when(i == nt - 1)
        def _():
            gw_ref[:, 0] = acc_sc[...].astype(bf16)

    tok = pl.BlockSpec((ts, D), lambda i: (i, 0))
    t512 = pl.BlockSpec((ts, D_ATT), lambda i: (i, 0))
    t256 = pl.BlockSpec((ts, 256), lambda i: (i, 0))
    wspec = pl.BlockSpec((N_CHIPS, 1, 256, D), lambda i: (0, l, 0, 0))
    return pl.pallas_call(
        body, name="mix_out_bwd", grid=(nt,),
        in_specs=[tok, tok, tok, t512, t256, t256, _full((3, D)), wspec, _full((1, D)), pl.BlockSpec(memory_space=pl.ANY)],
        out_specs=[tok, t512, t256, t256, _full((1, D)), _full((1, D)), _full((1, D)), wspec],
        out_shape=[jax.ShapeDtypeStruct((S, D), f32), jax.ShapeDtypeStruct((S, D_ATT), f32),
                   jax.ShapeDtypeStruct((S, D_SSM), f32), jax.ShapeDtypeStruct((S, D_POOL), f32),
                   jax.ShapeDtypeStruct((1, D), f32), jax.ShapeDtypeStruct((1, D), f32), jax.ShapeDtypeStruct((1, D), f32),
                   jax.ShapeDtypeStruct(gw_out.shape, bf16)],
        scratch_shapes=[pltpu.VMEM((N_CHIPS, 256, D), f32)],
        input_output_aliases={9: 7},
        compiler_params=_cp(("arbitrary",), 48),
    )(*_hbm(dxo, x, y, y_att, y_ssm, y_pool, mod3, w_out, lng, gw_out))


def _t5_bucket(dist):
    max_exact = N_BUCKETS // 2
    d = np.maximum(dist, 1).astype(np.float32)
    large = max_exact + (np.log(d / max_exact) / math.log(MAX_DISTANCE / max_exact)
                         * (N_BUCKETS - max_exact)).astype(np.int32)
    large = np.minimum(large, N_BUCKETS - 1)
    return np.where(dist < max_exact, dist, large).astype(np.int32)


def _bucket_table():
    q = ATT_BLOCK
    i = np.arange(q)[:, None]
    j = np.arange(2 * q)[None, :]
    r = i + q - j
    in_band = (r >= 0) & (r <= q)
    tabs = [np.where(in_band, _t5_bucket(np.clip(r, 0, None) * d), -1) for d in DILATIONS]
    return np.stack(tabs).astype(np.int32)


def _bias_fwd(rel_bias, table):
    def body(rb_ref, tab_ref, out_ref):
        for b in range(3):
            tb = tab_ref[b]
            for h in range(N_HEADS):
                def pick(k, acc):
                    return jnp.where(tb == k, rb_ref[k, h], acc)
                out_ref[b, h] = lax.fori_loop(0, N_BUCKETS, pick, jnp.where(tb < 0, NEG, 0.0).astype(f32))

    return pl.pallas_call(
        body, name="bias_fwd",
        in_specs=[pl.BlockSpec(memory_space=pltpu.SMEM), pl.BlockSpec(memory_space=pltpu.VMEM)],
        out_specs=pl.BlockSpec(memory_space=pltpu.VMEM),
        out_shape=jax.ShapeDtypeStruct((3, N_HEADS, ATT_BLOCK, 2 * ATT_BLOCK), f32),
    )(rel_bias, table)


def _bias_bwd(dbias, table):
    def body(db_ref, tab_ref, out_ref):
        def per_bucket(k, c):
            for h in range(N_HEADS):
                tot = jnp.zeros((), f32)
                for b in range(3):
                    tot = tot + jnp.sum(jnp.where(tab_ref[b] == k, db_ref[b, h], 0.0))
                out_ref[k, h] = tot
            return c
        lax.fori_loop(0, N_BUCKETS, per_bucket, 0)

    return pl.pallas_call(
        body, name="bias_bwd",
        in_specs=[pl.BlockSpec(memory_space=pltpu.VMEM), pl.BlockSpec(memory_space=pltpu.VMEM)],
        out_specs=pl.BlockSpec(memory_space=pltpu.SMEM),
        out_shape=jax.ShapeDtypeStruct((N_BUCKETS, N_HEADS), f32),
    )(dbias, table)


def _att_unit(u, nbr):
    rows = pl.ds(pl.multiple_of(u * ATT_BLOCK, ATT_BLOCK), ATT_BLOCK)
    prev = pl.ds(pl.multiple_of(jnp.maximum(u - 1, 0) * ATT_BLOCK, ATT_BLOCK), ATT_BLOCK)
    return rows, prev, (u % nbr) != 0


_N_PAIRS = N_HEADS // 2


def _pair_rows(t):
    lane = lax.broadcasted_iota(jnp.int32, t.shape, 1)
    zero = jnp.zeros_like(t)
    return jnp.concatenate([jnp.where(lane < HEAD_DIM, t, zero), jnp.where(lane >= HEAD_DIM, t, zero)], axis=0)


def _pair_cols(big):
    lane = lax.broadcasted_iota(jnp.int32, (ATT_BLOCK, _LANES), 1)
    return jnp.where(lane < HEAD_DIM, big[:ATT_BLOCK], big[ATT_BLOCK:])


def _pair_column(ref, rows, hp):
    t = ref[rows, :]
    return jnp.concatenate([t[:, 2 * hp:2 * hp + 1], t[:, 2 * hp + 1:2 * hp + 2]], axis=0)


def _pair_band(ref, rows, prev, nbr, hp):
    lanes = pl.ds(_LANES * hp, _LANES)
    cur = ref[0, rows, lanes]
    return cur if nbr == 1 else jnp.concatenate([ref[0, prev, lanes], cur], axis=0)


def _pair_scores(q_ref, k_ref, b_ref, rows, prev, valid_prev, nbr, hp):
    qbd = _pair_rows(q_ref[0, rows, pl.ds(_LANES * hp, _LANES)])
    kb = _pair_band(k_ref, rows, prev, nbr, hp)
    bias = b_ref[0, 2 * hp:2 * hp + 2].reshape(2 * ATT_BLOCK, 2 * ATT_BLOCK)
    if nbr == 1:
        return qbd, kb, _dot_nt(qbd, kb) + bias[:, ATT_BLOCK:]
    s = _dot_nt(qbd, kb) + bias
    col = lax.broadcasted_iota(jnp.int32, s.shape, 1)
    return qbd, kb, jnp.where((col >= ATT_BLOCK) | valid_prev, s, NEG)


def _qkv_specs(S, branch):
    return ([pl.BlockSpec((1, S, D_ATT), lambda i, t=t: (t, 0, 0)) for t in range(3)],
            pl.BlockSpec((1, N_HEADS, ATT_BLOCK, 2 * ATT_BLOCK), lambda i: (branch, 0, 0, 0)))


def _att_fwd(qkv, bias, branch):
    S = qkv.shape[1]
    nbr = BLOCKS_PER_RESIDUE[branch]

    def body(q_ref, k_ref, v_ref, b_ref, o_ref, lse_ref):
        lse_ref[...] = jnp.zeros_like(lse_ref)

        def unit(u, c):
            rows, prev, valid_prev = _att_unit(u, nbr)
            for hp in range(_N_PAIRS):
                _, _, s = _pair_scores(q_ref, k_ref, b_ref, rows, prev, valid_prev, nbr, hp)
                m = jnp.max(s, -1, keepdims=True)
                p = jnp.exp(s - m)
                den = jnp.sum(p, -1, keepdims=True)
                big = _dot(p.astype(bf16), _pair_band(v_ref, rows, prev, nbr, hp))
                o_ref[rows, pl.ds(_LANES * hp, _LANES)] = _pair_cols(big / den)
                lse = m + jnp.log(den)
                lse_ref[rows, pl.ds(2 * hp, 1)] = lse[:ATT_BLOCK]
                lse_ref[rows, pl.ds(2 * hp + 1, 1)] = lse[ATT_BLOCK:]
            return c

        lax.fori_loop(0, N_UNITS, unit, 0)

    qkv_specs, bspec = _qkv_specs(S, branch)
    return pl.pallas_call(
        body, name="att_fwd", grid=(1,),
        in_specs=qkv_specs + [bspec],
        out_specs=[pl.BlockSpec((S, D_ATT), lambda i: (0, 0)), pl.BlockSpec((S, _LANES), lambda i: (0, 0))],
        out_shape=[jax.ShapeDtypeStruct((S, D_ATT), f32), jax.ShapeDtypeStruct((S, _LANES), f32)],
        compiler_params=_cp(("arbitrary",), 40),
    )(*_hbm(qkv, qkv, qkv, bias))


def _att_bwd(qkv, do, lse, crow, bias, branch):
    S = qkv.shape[1]
    nbr = BLOCKS_PER_RESIDUE[branch]

    def body(q_ref, k_ref, v_ref, do_ref, lse_ref, c_ref, b_ref, dqkv_ref, db_ref, dk_sc, dv_sc):
        dk_sc[...] = jnp.zeros_like(dk_sc)
        dv_sc[...] = jnp.zeros_like(dv_sc)
        db_ref[...] = jnp.zeros_like(db_ref)

        def unit(u, c):
            rows, prev, valid_prev = _att_unit(u, nbr)
            for hp in range(_N_PAIRS):
                lanes = pl.ds(_LANES * hp, _LANES)
                qbd, kb, s = _pair_scores(q_ref, k_ref, b_ref, rows, prev, valid_prev, nbr, hp)
                p = jnp.exp(s - _pair_column(lse_ref, rows, hp))
                dobd = _pair_rows(do_ref[rows, lanes])
                ds = p * (_dot_nt(dobd, _pair_band(v_ref, rows, prev, nbr, hp)) - _pair_column(c_ref, rows, hp))
                if nbr == 1:
                    db_ref[2 * hp:2 * hp + 2, :, ATT_BLOCK:] += ds.reshape(2, ATT_BLOCK, ATT_BLOCK)
                else:
                    db_ref[2 * hp:2 * hp + 2] += ds.reshape(2, ATT_BLOCK, 2 * ATT_BLOCK)
                dsb = ds.astype(bf16)
                dqkv_ref[0, rows, lanes] = (HEAD_DIM ** -0.5 * _pair_cols(_dot(dsb, kb))).astype(bf16)
                dkb = _dot_tn(dsb, qbd)
                dvb = _dot_tn(p.astype(bf16), dobd)
                if nbr == 1:
                    dk_sc[rows, lanes] += dkb
                    dv_sc[rows, lanes] += dvb
                else:
                    dk_sc[prev, lanes] += dkb[:ATT_BLOCK]
                    dv_sc[prev, lanes] += dvb[:ATT_BLOCK]
                    dk_sc[rows, lanes] += dkb[ATT_BLOCK:]
                    dv_sc[rows, lanes] += dvb[ATT_BLOCK:]
            return c

        lax.fori_loop(0, N_UNITS, unit, 0)
        dqkv_ref[1] = dk_sc[...].astype(bf16)
        dqkv_ref[2] = dv_sc[...].astype(bf16)

    qkv_specs, bspec = _qkv_specs(S, branch)
    row = pl.BlockSpec((S, _LANES), lambda i: (0, 0))
    return pl.pallas_call(
        body, name="att_bwd", grid=(1,),
        in_specs=qkv_specs + [pl.BlockSpec((S, D_ATT), lambda i: (0, 0)), row, row, bspec],
        out_specs=[pl.BlockSpec((3, S, D_ATT), lambda i: (0, 0, 0)),
                   pl.BlockSpec((N_HEADS, ATT_BLOCK, 2 * ATT_BLOCK), lambda i: (0, 0, 0))],
        out_shape=[jax.ShapeDtypeStruct((3, S, D_ATT), bf16), jax.ShapeDtypeStruct((N_HEADS, ATT_BLOCK, 2 * ATT_BLOCK), f32)],
        scratch_shapes=[pltpu.VMEM((S, D_ATT), f32), pltpu.VMEM((S, D_ATT), f32)],
        compiler_params=_cp(("arbitrary",), 48),
    )(*_hbm(qkv, qkv, qkv, do, lse, crow, bias))


def _branch_weights(lse_ref):
    l0, l1, l2 = lse_ref[0], lse_ref[1], lse_ref[2]
    m = jnp.maximum(jnp.maximum(l0, l1), l2)
    e0, e1, e2 = jnp.exp(l0 - m), jnp.exp(l1 - m), jnp.exp(l2 - m)
    tot = e0 + e1 + e2
    return e0 / tot, e1 / tot, e2 / tot


def _att_merge(os, lses):
    S = os[0].shape[0] * os[0].shape[1]
    ts = TOK_TILE

    def body(o1_ref, o4_ref, o16_ref, l1_ref, l4_ref, l16_ref, y_ref, lt_ref, *bufs):
        obufs = (bufs[:_QKV_BLOCKS], bufs[_QKV_BLOCKS:2 * _QKV_BLOCKS])
        lt_ref[0] = l1_ref[0]
        for k, (d, o_ref, l_ref) in enumerate(((4, o4_ref, l4_ref), (16, o16_ref, l16_ref))):
            _residues_to_rows(obufs[k], d, lambda r, cb, o_ref=o_ref: o_ref[r, :, _LANES * cb:_LANES * (cb + 1)])
            _residues_to_rows([bufs[2 * _QKV_BLOCKS + k]], d, lambda r, cb, l_ref=l_ref: l_ref[r])
            lt_ref[1 + k] = bufs[2 * _QKV_BLOCKS + k][...]
        w = _branch_weights(lt_ref)
        for h in range(N_HEADS):
            cs = slice(HEAD_DIM * h, HEAD_DIM * (h + 1))
            half = slice(HEAD_DIM * (h % 2), HEAD_DIM * (h % 2 + 1))
            y_ref[:, cs] = (w[0][:, h:h + 1] * o1_ref[0, :, cs] + w[1][:, h:h + 1] * obufs[0][h // 2][:, half]
                            + w[2][:, h:h + 1] * obufs[1][h // 2][:, half])

    return pl.pallas_call(
        body, name="att_merge", grid=(S // ts,),
        in_specs=[_res_spec3(d, D_ATT) for d in DILATIONS] + [_res_spec3(d, _LANES) for d in DILATIONS],
        out_specs=[pl.BlockSpec((ts, D_ATT), lambda i: (i, 0)), pl.BlockSpec((3, ts, _LANES), lambda i: (0, i, 0))],
        out_shape=[jax.ShapeDtypeStruct((S, D_ATT), f32), jax.ShapeDtypeStruct((3, S, _LANES), f32)],
        scratch_shapes=[pltpu.VMEM((ts, _LANES), f32)] * (2 * _QKV_BLOCKS + 2),
        compiler_params=_cp(("parallel",)),
    )(*_hbm(*os, *lses))


def _att_merge_bwd(dy, y, lse3):
    S = dy.shape[0]
    ts = TOK_TILE

    def body(dy_ref, y_ref, lse_ref, do1_ref, do4_ref, do16_ref, c1_ref, c4_ref, c16_ref, *bufs):
        dobufs = (bufs[:_QKV_BLOCKS], bufs[_QKV_BLOCKS:2 * _QKV_BLOCKS], bufs[2 * _QKV_BLOCKS:3 * _QKV_BLOCKS])
        cbufs = bufs[3 * _QKV_BLOCKS:]
        w = _branch_weights(lse_ref)
        for cb in cbufs:
            cb[...] = jnp.zeros_like(cb)
        for h in range(N_HEADS):
            cs = slice(HEAD_DIM * h, HEAD_DIM * (h + 1))
            half = slice(HEAD_DIM * (h % 2), HEAD_DIM * (h % 2 + 1))
            dyh = dy_ref[:, cs]
            t = jnp.sum(dyh * y_ref[:, cs], -1, keepdims=True)
            for p in range(3):
                wp = w[p][:, h:h + 1]
                dobufs[p][h // 2][:, half] = wp * dyh
                cbufs[p][:, h:h + 1] = wp * t
        for cb in range(_QKV_BLOCKS):
            do1_ref[0, :, _LANES * cb:_LANES * (cb + 1)] = dobufs[0][cb][...].astype(bf16)
        c1_ref[0] = cbufs[0][...]
        for k, (d, do_ref, c_ref) in enumerate(((4, do4_ref, c4_ref), (16, do16_ref, c16_ref))):
            def put_do(r, cb, piece, do_ref=do_ref):
                do_ref[r, :, _LANES * cb:_LANES * (cb + 1)] = piece.astype(bf16)

            def put_c(r, cb, piece, c_ref=c_ref):
                c_ref[r] = piece

            _rows_to_residues(dobufs[1 + k], d, put_do)
            _rows_to_residues([cbufs[1 + k]], d, put_c)

    return pl.pallas_call(
        body, name="att_merge_bwd", grid=(S // ts,),
        in_specs=[pl.BlockSpec((ts, D_ATT), lambda i: (i, 0)), pl.BlockSpec((ts, D_ATT), lambda i: (i, 0)),
                  pl.BlockSpec((3, ts, _LANES), lambda i: (0, i, 0))],
        out_specs=[_res_spec3(d, D_ATT) for d in DILATIONS] + [_res_spec3(d, _LANES) for d in DILATIONS],
        out_shape=[jax.ShapeDtypeStruct((d, S // d, D_ATT), bf16) for d in DILATIONS]
        + [jax.ShapeDtypeStruct((d, S // d, _LANES), f32) for d in DILATIONS],
        scratch_shapes=[pltpu.VMEM((ts, _LANES), f32)] * (3 * _QKV_BLOCKS + 3),
        compiler_params=_cp(("parallel",)),
    )(*_hbm(dy, y, lse3))


_SSM_ROWS = 256


def _scan_in_place(sr_ref, si_ref, a_ref, reverse):
    S, N = sr_ref.shape
    nst = S // SCAN_SEG
    ar = jnp.broadcast_to(a_ref[0:1, :], (SCAN_SEG, N))
    ai = jnp.broadcast_to(a_ref[1:2, :], (SCAN_SEG, N))
    if reverse:
        ai = -ai
    row = lax.broadcasted_iota(jnp.int32, (SCAN_SEG, N), 0)
    zero = jnp.zeros((SCAN_SEG, N), f32)

    def tile(t):
        return pl.ds(pl.multiple_of((nst - 1 - t if reverse else t) * SCAN_SEG, SCAN_SEG), SCAN_SEG)

    def local(t, c):
        sr, si, pr, pi = c
        rows = tile(t)
        nsr = ar * sr - ai * si + sr_ref[rows, :]
        nsi = ar * si + ai * sr + si_ref[rows, :]
        sr_ref[rows, :] = nsr
        si_ref[rows, :] = nsi
        return nsr, nsi, ar * pr - ai * pi, ar * pi + ai * pr

    fr, fi, apr, api = lax.fori_loop(0, nst, local, (zero, zero, zero + 1.0, zero))

    def shift(v):
        if reverse:
            return jnp.where(row == SCAN_SEG - 1, 0.0, pltpu.roll(v, SCAN_SEG - 1, axis=0))
        return jnp.where(row == 0, 0.0, pltpu.roll(v, 1, axis=0))

    cr, ci = zero, zero
    for _ in range(SCAN_SEG - 1):
        cr, ci = shift(fr + apr * cr - api * ci), shift(fi + apr * ci + api * cr)

    def fix(t, c):
        pr, pi = c
        npr, npi = ar * pr - ai * pi, ar * pi + ai * pr
        rows = tile(t)
        sr_ref[rows, :] += npr * cr - npi * ci
        si_ref[rows, :] += npr * ci + npi * cr
        return npr, npi

    lax.fori_loop(0, nst, fix, (zero + 1.0, zero))


def _ssm_states(u, bre, bim, a2):
    S = u.shape[0]

    def body(u_ref, br_ref, bi_ref, a_ref, sr_ref, si_ref):
        brb = br_ref[...].astype(bf16)
        bib = bi_ref[...].astype(bf16)

        def project(t, c):
            rows = pl.ds(pl.multiple_of(t * _SSM_ROWS, _SSM_ROWS), _SSM_ROWS)
            ub = u_ref[rows, :].astype(bf16)
            sr_ref[rows, :] = _dot(ub, brb)
            si_ref[rows, :] = _dot(ub, bib)
            return c

        lax.fori_loop(0, S // _SSM_ROWS, project, 0)
        _scan_in_place(sr_ref, si_ref, a_ref, False)

    vm = pl.BlockSpec(memory_space=pltpu.VMEM)
    return pl.pallas_call(
        body, name="ssm_states", in_specs=[vm] * 4, out_specs=[vm, vm],
        out_shape=[jax.ShapeDtypeStruct((S, D_STATE), f32)] * 2,
        compiler_params=_cp(None, 48),
    )(u, bre, bim, a2)


def _ssm_out(sr, si, u, cre, cim, dskip, glu_w, glu_b):
    S = u.shape[0]
    ts = TOK_TILE

    def body(sr_ref, si_ref, u_ref, cr_ref, ci_ref, d_ref, w_ref, b_ref, out_ref, y_ref):
        y = (_dot(sr_ref[...].astype(bf16), cr_ref[...].astype(bf16))
             - _dot(si_ref[...].astype(bf16), ci_ref[...].astype(bf16)) + d_ref[...] * u_ref[...])
        y_ref[...] = y
        z = _dot(_gelu(y).astype(bf16), w_ref[...].astype(bf16)) + b_ref[...]
        out_ref[...] = y * jax.nn.sigmoid(z)

    st = pl.BlockSpec((ts, D_STATE), lambda i: (i, 0))
    ch = pl.BlockSpec((ts, D_SSM), lambda i: (i, 0))
    return pl.pallas_call(
        body, name="ssm_out", grid=(S // ts,),
        in_specs=[st, st, ch, _full((D_STATE, D_SSM)), _full((D_STATE, D_SSM)), _full((1, D_SSM)),
                  _full((D_SSM, D_SSM)), _full((1, D_SSM))],
        out_specs=[ch, ch],
        out_shape=[jax.ShapeDtypeStruct((S, D_SSM), f32)] * 2,
        compiler_params=_cp(("parallel",)),
    )(*_hbm(sr, si, u, cre, cim, dskip, glu_w, glu_b))


def _ssm_out_bwd(dout, y, u, sr, si, dskip, glu_w, glu_b):
    S = u.shape[0]
    ts = TOK_TILE

    def body(do_ref, y_ref, u_ref, sr_ref, si_ref, d_ref, w_ref, b_ref,
             dy_ref, du_ref, dcr_ref, dci_ref, dd_ref, dgb_ref, dgw_ref):
        @pl.when(pl.program_id(0) == 0)
        def _():
            for r in (dcr_ref, dci_ref, dd_ref, dgb_ref, dgw_ref):
                r[...] = jnp.zeros_like(r)

        y = y_ref[...]
        dout = do_ref[...]
        wb = w_ref[...].astype(bf16)
        ge = _gelu(y).astype(bf16)
        sz = jax.nn.sigmoid(_dot(ge, wb) + b_ref[...])
        dz = dout * y * sz * (1.0 - sz)
        dzb = dz.astype(bf16)
        dgb_ref[...] += jnp.sum(dz, 0, keepdims=True)
        dgw_ref[...] += _dot_tn(ge, dzb)
        dy = dout * sz + _gelu_grad(y) * _dot_nt(dzb, wb)
        uv = u_ref[...]
        dd_ref[...] += jnp.sum(dy * uv, 0, keepdims=True)
        du_ref[...] = dy * d_ref[...]
        dy_ref[...] = dy
        dyb = dy.astype(bf16)
        dcr_ref[...] += _dot_tn(sr_ref[...].astype(bf16), dyb)
        dci_ref[...] -= _dot_tn(si_ref[...].astype(bf16), dyb)

    st = pl.BlockSpec((ts, D_STATE), lambda i: (i, 0))
    ch = pl.BlockSpec((ts, D_SSM), lambda i: (i, 0))
    c_full = _full((D_STATE, D_SSM))
    return pl.pallas_call(
        body, name="ssm_out_bwd", grid=(S // ts,),
        in_specs=[ch, ch, ch, st, st, _full((1, D_SSM)), _full((D_SSM, D_SSM)), _full((1, D_SSM))],
        out_specs=[ch, ch, c_full, c_full, _full((1, D_SSM)), _full((1, D_SSM)), _full((D_SSM, D_SSM))],
        out_shape=[jax.ShapeDtypeStruct((S, D_SSM), f32), jax.ShapeDtypeStruct((S, D_SSM), f32),
                   jax.ShapeDtypeStruct((D_STATE, D_SSM), f32), jax.ShapeDtypeStruct((D_STATE, D_SSM), f32),
                   jax.ShapeDtypeStruct((1, D_SSM), f32), jax.ShapeDtypeStruct((1, D_SSM), f32),
                   jax.ShapeDtypeStruct((D_SSM, D_SSM), f32)],
        compiler_params=_cp(("arbitrary",), 40),
    )(*_hbm(dout, y, u, sr, si, dskip, glu_w, glu_b))


def _ssm_states_bwd(dy, du_skip, u, sr, si, cre, cim, bre, bim, a2):
    S = u.shape[0]
    N = D_STATE
    nst = S // SCAN_SEG
    nproj = S // _SSM_ROWS

    def body(dy_ref, dus_ref, u_ref, sr_ref, si_ref, cr_ref, ci_ref, br_ref, bi_ref, a_ref,
             du_ref, dbr_ref, dbi_ref, da_ref, lr_ref, li_ref):
        crb = cr_ref[...].astype(bf16)
        cib = ci_ref[...].astype(bf16)

        def project(t, c):
            rows = pl.ds(pl.multiple_of(t * _SSM_ROWS, _SSM_ROWS), _SSM_ROWS)
            dyb = dy_ref[rows, :].astype(bf16)
            lr_ref[rows, :] = _dot_nt(dyb, crb)
            li_ref[rows, :] = -_dot_nt(dyb, cib)
            return c

        lax.fori_loop(0, nproj, project, 0)
        _scan_in_place(lr_ref, li_ref, a_ref, True)

        row = lax.broadcasted_iota(jnp.int32, (SCAN_SEG, N), 0)
        last = pl.ds((nst - 1) * SCAN_SEG, SCAN_SEG)
        pr = jnp.where(row == 0, 0.0, pltpu.roll(sr_ref[last, :], 1, axis=0))
        pi = jnp.where(row == 0, 0.0, pltpu.roll(si_ref[last, :], 1, axis=0))
        first = pl.ds(0, SCAN_SEG)
        acc_r = lr_ref[first, :] * pr + li_ref[first, :] * pi
        acc_i = li_ref[first, :] * pr - lr_ref[first, :] * pi

        def step(t, c):
            acc_r, acc_i = c
            rows = pl.ds(pl.multiple_of(t * SCAN_SEG, SCAN_SEG), SCAN_SEG)
            prev = pl.ds(pl.multiple_of((t - 1) * SCAN_SEG, SCAN_SEG), SCAN_SEG)
            lrv, liv, srv, siv = lr_ref[rows, :], li_ref[rows, :], sr_ref[prev, :], si_ref[prev, :]
            return acc_r + lrv * srv + liv * siv, acc_i + liv * srv - lrv * siv

        acc_r, acc_i = lax.fori_loop(1, nst, step, (acc_r, acc_i))
        da_ref[0:1, :] = jnp.sum(acc_r, 0, keepdims=True)
        da_ref[1:2, :] = jnp.sum(acc_i, 0, keepdims=True)

        brb = br_ref[...].astype(bf16)
        bib = bi_ref[...].astype(bf16)
        dbr_ref[...] = jnp.zeros_like(dbr_ref)
        dbi_ref[...] = jnp.zeros_like(dbi_ref)

        def back(t, c):
            rows = pl.ds(pl.multiple_of(t * _SSM_ROWS, _SSM_ROWS), _SSM_ROWS)
            lrb = lr_ref[rows, :].astype(bf16)
            lib = li_ref[rows, :].astype(bf16)
            du_ref[rows, :] = dus_ref[rows, :] + _dot_nt(lrb, brb) + _dot_nt(lib, bib)
            ub = u_ref[rows, :].astype(bf16)
            dbr_ref[...] += _dot_tn(ub, lrb)
            dbi_ref[...] += _dot_tn(ub, lib)
            return c

        lax.fori_loop(0, nproj, back, 0)

    vm = pl.BlockSpec(memory_space=pltpu.VMEM)
    return pl.pallas_call(
        body, name="ssm_states_bwd", in_specs=[vm] * 10, out_specs=[vm] * 4,
        out_shape=[jax.ShapeDtypeStruct((S, D_SSM), f32), jax.ShapeDtypeStruct((D_SSM, D_STATE), f32),
                   jax.ShapeDtypeStruct((D_SSM, D_STATE), f32), jax.ShapeDtypeStruct((2, D_STATE), f32)],
        scratch_shapes=[pltpu.VMEM((S, D_STATE), f32), pltpu.VMEM((S, D_STATE), f32)],
        compiler_params=_cp(None, 56),
    )(dy, du_skip, u, sr, si, cre, cim, bre, bim, a2)


_POOL_TILE = 256


def _window_sums(xt, back):
    n = xt.shape[0]
    out = []
    ws = xt
    for k in (1, 2, 4, 8):
        ws = ws + pltpu.roll(ws, k if back else n - k, axis=0)
        out.append(ws)
    return out


def _pool_count(r0, w):
    t = r0 + lax.broadcasted_iota(jnp.int32, (_POOL_TILE, POOL_GROUP), 0)
    return jnp.minimum(t + 1, w).astype(f32)


def _pool_fwd(u_pad, pool_w, pool_scale):
    S = u_pad.shape[0] - POOL_HALO
    nt = S // _POOL_TILE

    def body(u_ref, w_ref, sc_ref, y_ref):
        def tile(t, c):
            r0 = pl.multiple_of(t * _POOL_TILE, _POOL_TILE)
            for g, w in enumerate(POOL_WINDOWS):
                cs = pl.ds(POOL_GROUP * g, POOL_GROUP)
                xt = u_ref[pl.ds(r0, _POOL_TILE + POOL_HALO), cs]
                ws = _window_sums(xt, True)[g][POOL_HALO:, :]
                pooled = ws / _pool_count(r0, w) - xt[POOL_HALO:, :]
                y_ref[pl.ds(r0, _POOL_TILE), cs] = _dot(pooled.astype(bf16), w_ref[g].astype(bf16)) * sc_ref[:, cs]
            return c
        lax.fori_loop(0, nt, tile, 0)

    vm = pl.BlockSpec(memory_space=pltpu.VMEM)
    return pl.pallas_call(
        body, name="pool_fwd", in_specs=[vm, vm, vm], out_specs=vm,
        out_shape=jax.ShapeDtypeStruct((S, D_POOL), f32),
    )(u_pad, pool_w, pool_scale)


def _pool_bwd(dy_pad, u_pad, pool_w, pool_scale):
    S = u_pad.shape[0] - POOL_HALO
    nt = S // _POOL_TILE
    n = _POOL_TILE + POOL_HALO

    def body(dy_ref, u_ref, w_ref, sc_ref, du_ref, dw_ref, dsc_ref):
        dw_ref[...] = jnp.zeros_like(dw_ref)
        dsc_ref[...] = jnp.zeros_like(dsc_ref)

        def tile(t, c):
            r0 = pl.multiple_of(t * _POOL_TILE, _POOL_TILE)
            for g, w in enumerate(POOL_WINDOWS):
                cs = pl.ds(POOL_GROUP * g, POOL_GROUP)
                wb = w_ref[g].astype(bf16)
                xt = u_ref[pl.ds(r0, n), cs]
                pooled = (_window_sums(xt, True)[g][POOL_HALO:, :] / _pool_count(r0, w) - xt[POOL_HALO:, :]).astype(bf16)
                dy = dy_ref[pl.ds(r0, _POOL_TILE), cs]
                dsc_ref[:, cs] += jnp.sum(dy * _dot(pooled, wb), 0, keepdims=True)
                dw_ref[g] += _dot_tn(pooled, (dy * sc_ref[:, cs]).astype(bf16))
                dyh = (dy_ref[pl.ds(r0, n), cs] * sc_ref[:, cs]).astype(bf16)
                dpl = _dot_nt(dyh, wb)
                cnt = jnp.minimum(r0 + lax.broadcasted_iota(jnp.int32, (n, POOL_GROUP), 0) + 1, w).astype(f32)
                lead = _window_sums(dpl / cnt, False)[g]
                du_ref[pl.ds(r0, _POOL_TILE), cs] = lead[:_POOL_TILE, :] - dpl[:_POOL_TILE, :]
            return c
        lax.fori_loop(0, nt, tile, 0)

    vm = pl.BlockSpec(memory_space=pltpu.VMEM)
    return pl.pallas_call(
        body, name="pool_bwd", in_specs=[vm, vm, vm, vm], out_specs=[vm, vm, vm],
        out_shape=[jax.ShapeDtypeStruct((S, D_POOL), f32), jax.ShapeDtypeStruct((4, POOL_GROUP, POOL_GROUP), f32),
                   jax.ShapeDtypeStruct((1, D_POOL), f32)],
    )(dy_pad, u_pad, pool_w, pool_scale)


def _loss_head(y, target):
    S, D = y.shape
    ts = TOK_TILE

    def body(y_ref, t_ref, loss_ref, dy_ref):
        @pl.when(pl.program_id(0) == 0)
        def _():
            loss_ref[...] = jnp.zeros_like(loss_ref)

        d = y_ref[...] - t_ref[...]
        dy_ref[...] = d * (1.0 / D)
        loss_ref[...] += 0.5 * jnp.sum(jnp.sum(d * d, -1, keepdims=True) * (1.0 / D), 0, keepdims=True)

    tok = pl.BlockSpec((ts, D), lambda i: (i, 0))
    return pl.pallas_call(
        body, name="loss_head", grid=(S // ts,),
        in_specs=[tok, tok], out_specs=[_full((1, 1)), tok],
        out_shape=[jax.ShapeDtypeStruct((1, 1), f32), jax.ShapeDtypeStruct((S, D), f32)],
        compiler_params=_cp(("arbitrary",)),
    )(*_hbm(y, target))


_ADA_COLS = 768


def _ada_fwd(c_all, ada_w, ada_b_cols):
    L, D, N = ada_w.shape
    B = c_all.shape[0]

    def body(c_ref, w_ref, b_ref, out_ref):
        cv = c_ref[...]
        cond = (cv * jax.nn.sigmoid(cv)).astype(bf16)
        out_ref[0] = _dot(cond, w_ref[0].astype(bf16)) + b_ref[0]

    return pl.pallas_call(
        body, name="ada_fwd", grid=(L, N // _ADA_COLS),
        in_specs=[_full((B, D)), pl.BlockSpec((1, D, _ADA_COLS), lambda l, j: (l, 0, j)),
                  pl.BlockSpec((1, 1, _ADA_COLS), lambda l, j: (l, 0, j))],
        out_specs=pl.BlockSpec((1, B, _ADA_COLS), lambda l, j: (l, 0, j)),
        out_shape=jax.ShapeDtypeStruct((L, B, N), f32),
        compiler_params=_cp(("parallel", "parallel")),
    )(c_all, ada_w, ada_b_cols)


def _ada_wgrad(c_all_t, dmod_cols):
    D, B = c_all_t.shape
    L, _, N = dmod_cols.shape

    def body(ct_ref, dm_ref, out_ref):
        cv = ct_ref[...]
        cond = cv * jax.nn.sigmoid(cv)
        acc = cond[:, 0:1] * dm_ref[0, 0:1, :]
        for b in range(1, B):
            acc = acc + cond[:, b:b + 1] * dm_ref[0, b:b + 1, :]
        out_ref[0] = acc

    return pl.pallas_call(
        body, name="ada_wgrad", grid=(L, N // _ADA_COLS),
        in_specs=[_full((D, B)), pl.BlockSpec((1, B, _ADA_COLS), lambda l, j: (l, 0, j))],
        out_specs=pl.BlockSpec((1, D, _ADA_COLS), lambda l, j: (l, 0, j)),
        out_shape=jax.ShapeDtypeStruct((L, D, N), f32),
        compiler_params=_cp(("parallel", "parallel")),
    )(c_all_t, dmod_cols)


def _adam_math(w, g, m, v):
    m = ADAM_B1 * m + (1.0 - ADAM_B1) * g
    v = ADAM_B2 * v + (1.0 - ADAM_B2) * (g * g)
    m_hat = m / (1.0 - ADAM_B1 ** ADAM_STEP)
    v_hat = v / (1.0 - ADAM_B2 ** ADAM_STEP)
    delta = -ADAM_LR * (m_hat / (jnp.sqrt(v_hat) + ADAM_EPS) + ADAM_WD * w)
    return delta, m, v


def _adamw(w, m, v, g, row_tile, row0=0, outs=None):
    R, C = w.shape
    b0 = row0 // row_tile

    def body(w_ref, m_ref, v_ref, g_ref, _0, _1, _2, _3, g_out, d_out, m_out, v_out):
        gv = g_ref[...]
        delta, mn, vn = _adam_math(w_ref[...], gv, m_ref[...], v_ref[...])
        g_out[...] = gv
        d_out[...] = delta
        m_out[...] = mn
        v_out[...] = vn

    pspec = pl.BlockSpec((row_tile, C), lambda i: (b0 + i, 0))
    gspec = pl.BlockSpec((row_tile, C), lambda i: (i, 0))
    anyspec = pl.BlockSpec(memory_space=pl.ANY)
    shp = jax.ShapeDtypeStruct((R, C), f32)
    if outs is None:
        outs = [lax.empty((R, C), f32) for _ in range(4)]
    return pl.pallas_call(
        body, name="adamw", grid=(g.shape[0] // row_tile,),
        in_specs=[pspec] * 3 + [gspec] + [anyspec] * 4, out_specs=[pspec] * 4, out_shape=[shp] * 4,
        input_output_aliases={4: 0, 5: 1, 6: 2, 7: 3},
        compiler_params=_cp(("parallel",), 40),
    )(*_hbm(w, m, v, g, *outs))


def _pair_sum(g5s, gots, pc):
    n = len(g5s)

    def body(pc_ref, *refs):
        for own, got, out in zip(refs[:n], refs[n:2 * n], refs[2 * n:]):
            out[0, 0] = (own[0, 0, 0].astype(f32) + got[0, 0].astype(f32)).astype(bf16)

    def half(g):
        return pl.BlockSpec((1, 1) + g.shape[-2:], lambda p, pc: (p, 0, 0, 0))

    gs = pltpu.PrefetchScalarGridSpec(
        num_scalar_prefetch=1, grid=(N_CHIPS,),
        in_specs=[pl.BlockSpec((1, 1, 1) + g.shape[-2:], lambda p, pc: (p, 0, pc[1], 0, 0)) for g in g5s]
        + [half(g) for g in gots],
        out_specs=[half(g) for g in gots],
    )
    return pl.pallas_call(
        body, name="pair_sum", grid_spec=gs, out_shape=[jax.ShapeDtypeStruct(g.shape, bf16) for g in gots],
        compiler_params=_cp(("parallel",), 48),
    )(pc, *_hbm(*g5s, *gots))


_SUM_STEPS = 2


def _sum_shards(hsums, recvs, pc):
    n = len(hsums)

    def body(pc_ref, *refs):
        for own, got, out in zip(refs[:n], refs[n:2 * n], refs[2 * n:]):
            acc = own[0, 0].astype(f32)
            for j in range(3):
                acc = acc + got[j, 0].astype(f32)
            out[0, 0] = acc

    def rows(h):
        return (h.shape[2] // _SUM_STEPS, h.shape[3])

    gs = pltpu.PrefetchScalarGridSpec(
        num_scalar_prefetch=1, grid=(_SUM_STEPS,),
        in_specs=[pl.BlockSpec((1, 1) + rows(h), lambda i, pc: (pc[0], 0, i, 0)) for h in hsums]
        + [pl.BlockSpec((3, 1) + rows(h), lambda i, pc: (0, 0, i, 0)) for h in hsums],
        out_specs=[pl.BlockSpec((1, 1) + rows(h), lambda i, pc: (0, pc[1], i, 0)) for h in hsums],
    )
    return pl.pallas_call(
        body, name="sum_shards", grid_spec=gs,
        out_shape=[jax.ShapeDtypeStruct((1, 2) + h.shape[2:], f32) for h in hsums],
        compiler_params=_cp(("parallel",), 48),
    )(pc, *_hbm(*hsums, *recvs))


def _sum8(packs):
    _, R, C = packs.shape
    tr = R // 8 if R % 64 == 0 else R

    def body(p_ref, out_ref):
        acc = p_ref[0]
        for d in range(1, 8):
            acc = acc + p_ref[d]
        out_ref[...] = acc

    return pl.pallas_call(
        body, name="sum8", grid=(R // tr,),
        in_specs=[pl.BlockSpec((8, tr, C), lambda i: (0, i, 0))],
        out_specs=pl.BlockSpec((tr, C), lambda i: (i, 0)),
        out_shape=jax.ShapeDtypeStruct((R, C), f32),
        compiler_params=_cp(("parallel",)),
    )(packs)


def _allgather8(x_shard):
    m_per, n = x_shard.shape

    def body(x_ref, out_ref, send_sems, recv_sems, local_sem):
        x, y, c = lax.axis_index("x"), lax.axis_index("y"), lax.axis_index("c")
        me, sibling = (x, y, c), (x, y, 1 - c)
        chips = [(1 - x, y), (x, 1 - y), (1 - x, 1 - y)]

        def rows(px, py, pc):
            return out_ref.at[pl.ds((4 * px + 2 * py + pc) * m_per, m_per), :]

        def copy(k, block, to, src=None):
            return pltpu.make_async_remote_copy(
                src_ref=rows(*block) if src is None else src, dst_ref=rows(*block),
                send_sem=send_sems.at[k], recv_sem=recv_sems.at[k], device_id=to, device_id_type=MESH)

        mine = pltpu.make_async_copy(x_ref, rows(*me), local_sem)
        mine.start()
        first = [copy(0, me, sibling, src=x_ref)]
        first += [copy(1 + j, me, (*chip, c), src=x_ref) for j, chip in enumerate(chips)]
        for cp in first:
            cp.start()
        passed = [copy(4 + j, (*chip, c), sibling) for j, chip in enumerate(chips)]
        for j, chip in enumerate(chips):
            copy(1 + j, (*chip, c), me).wait_recv()
            passed[j].start()
        copy(0, sibling, me).wait_recv()
        for j, chip in enumerate(chips):
            copy(4 + j, (*chip, 1 - c), me).wait_recv()
        for cp in first + passed:
            cp.wait_send()
        mine.wait()

    return pl.pallas_call(
        body, name="allgather8",
        out_shape=jax.ShapeDtypeStruct((8 * m_per, n), x_shard.dtype),
        in_specs=[pl.BlockSpec(memory_space=pltpu.VMEM)],
        out_specs=pl.BlockSpec(memory_space=pltpu.VMEM),
        scratch_shapes=[pltpu.SemaphoreType.DMA((7,)), pltpu.SemaphoreType.DMA((7,)), pltpu.SemaphoreType.DMA],
        compiler_params=_cp(None, 48),
    )(x_shard)


def _other_chips():
    x, y = lax.axis_index("x"), lax.axis_index("y")
    return [(1 - x, y), (x, 1 - y), (1 - x, 1 - y)]


_HBM = pl.BlockSpec(memory_space=pltpu.HBM)
_SEM = pl.BlockSpec(memory_space=pltpu.SEMAPHORE)
_EFFECT = pltpu.SideEffectType.DATAFLOW_SIDE_EFFECTING


def _gather_copies(srcs, lands, send_sems, recv_sems):
    x, y, c = lax.axis_index("x"), lax.axis_index("y"), lax.axis_index("c")
    return [pltpu.make_async_remote_copy(
        src_ref=srcs[a].at[:, c], dst_ref=lands[a].at[2 * x + y, :, c], send_sem=send_sems.at[3 * a + j],
        recv_sem=recv_sems.at[3 * a + j], device_id=(cx, cy, c), device_id_type=MESH)
        for a in range(len(srcs)) for j, (cx, cy) in enumerate(_other_chips())]


def _gather_start(chunks, after, name):
    sizes = [len(srcs) for srcs, _ in chunks]
    flat = [t for srcs, lands in chunks for t in list(srcs) + list(lands)]
    nflat = len(flat)
    nsem = 2 * len(chunks)

    def body(*refs):
        ins, sems, token = refs[:nflat], refs[nflat + 1:nflat + 1 + nsem], refs[-1]
        off = 0
        for k, n in enumerate(sizes):
            for cp in _gather_copies(ins[off:off + n], ins[off + n:off + 2 * n], sems[2 * k], sems[2 * k + 1]):
                cp.start()
            off += 2 * n
        token[...] = jnp.zeros_like(token)

    res = pl.pallas_call(
        body, name=name,
        out_shape=[pltpu.SemaphoreType.DMA((3 * n,)) for n in sizes for _ in range(2)]
        + [pltpu.HBM(t.shape, t.dtype) for t in flat] + [jax.ShapeDtypeStruct((8, 128), f32)],
        in_specs=[_HBM] * nflat + [pl.BlockSpec(memory_space=pl.ANY)],
        out_specs=[_SEM] * nsem + [_HBM] * nflat + [pl.BlockSpec(memory_space=pltpu.VMEM)],
        input_output_aliases={i: nsem + i for i in range(nflat)},
        compiler_params=pltpu.CompilerParams(has_side_effects=_EFFECT),
    )(*[pltpu.with_memory_space_constraint(t, pltpu.HBM) for t in flat], after)
    out, off = [], nsem
    for k, n in enumerate(sizes):
        out.append((res[2 * k], res[2 * k + 1], res[off:off + n], res[off + n:off + 2 * n]))
        off += 2 * n
    return out, res[-1]


def _gather_wait(send_sems, recv_sems, srcs, lands, after, name):
    n = len(srcs)

    def body(*refs):
        for cp in _gather_copies(refs[:n], refs[n:2 * n], refs[2 * n], refs[2 * n + 1]):
            cp.wait_send()
            cp.wait_recv()

    res = pl.pallas_call(
        body, name=name,
        out_shape=[pltpu.HBM(t.shape, t.dtype) for t in list(srcs) + list(lands)],
        in_specs=[_HBM] * (2 * n) + [_SEM, _SEM] + [pl.BlockSpec(memory_space=pl.ANY)] * len(after),
        out_specs=[_HBM] * (2 * n),
        input_output_aliases={i: i for i in range(2 * n)},
        compiler_params=pltpu.CompilerParams(has_side_effects=_EFFECT),
    )(*srcs, *lands, send_sems, recv_sems, *after)
    return res[n:]


def _split_start(make_copies, arrays, nsem, name, after=()):
    n, na = len(arrays), len(after)

    def body(*refs):
        for cp in make_copies(refs[:n], refs[n + na], refs[n + na + 1]):
            cp.start()
        refs[-1][...] = jnp.zeros_like(refs[-1])

    res = pl.pallas_call(
        body, name=name,
        out_shape=[pltpu.SemaphoreType.DMA((nsem,)), pltpu.SemaphoreType.DMA((nsem,))]
        + [pltpu.HBM(t.shape, t.dtype) for t in arrays] + [jax.ShapeDtypeStruct((8, 128), f32)],
        in_specs=[_HBM] * n + [pl.BlockSpec(memory_space=pl.ANY)] * na,
        out_specs=[_SEM, _SEM] + [_HBM] * n + [pl.BlockSpec(memory_space=pltpu.VMEM)],
        input_output_aliases={i: i + 2 for i in range(n)},
        compiler_params=pltpu.CompilerParams(has_side_effects=_EFFECT),
    )(*[pltpu.with_memory_space_constraint(t, pltpu.HBM) for t in arrays], *after)
    return (res[0], res[1], res[2:2 + n]), res[-1]


def _split_wait(make_copies, send_sems, recv_sems, arrays, after, name):
    n = len(arrays)

    def body(*refs):
        for cp in make_copies(refs[:n], refs[n], refs[n + 1]):
            cp.wait_send()
            cp.wait_recv()

    return pl.pallas_call(
        body, name=name,
        out_shape=[pltpu.HBM(t.shape, t.dtype) for t in arrays],
        in_specs=[_HBM] * n + [_SEM, _SEM] + [pl.BlockSpec(memory_space=pl.ANY)] * len(after),
        out_specs=[_HBM] * n, input_output_aliases={i: i for i in range(n)},
        compiler_params=pltpu.CompilerParams(has_side_effects=_EFFECT),
    )(*arrays, send_sems, recv_sems, *after)


def _sibling():
    return lax.axis_index("x"), lax.axis_index("y"), 1 - lax.axis_index("c")


def _forward_copies(lands, send_sems, recv_sems):
    c = lax.axis_index("c")
    return [pltpu.make_async_remote_copy(
        src_ref=lands[a].at[2 * cx + cy, :, c], dst_ref=lands[a].at[2 * cx + cy, :, c], send_sem=send_sems.at[3 * a + j],
        recv_sem=recv_sems.at[3 * a + j], device_id=_sibling(), device_id_type=MESH)
        for a in range(len(lands)) for j, (cx, cy) in enumerate(_other_chips())]


def _swap_copies(fulls, send_sems, recv_sems):
    c = lax.axis_index("c")
    return [pltpu.make_async_remote_copy(src_ref=t.at[:, c], dst_ref=t.at[:, c], send_sem=send_sems.at[a],
                                         recv_sem=recv_sems.at[a], device_id=_sibling(), device_id_type=MESH)
            for a, t in enumerate(fulls)]


def _pair_copies(refs, send_sems, recv_sems):
    n = len(refs) // 2
    c = lax.axis_index("c")
    return [pltpu.make_async_remote_copy(src_ref=refs[a].at[:, :, 1 - c], dst_ref=refs[n + a], send_sem=send_sems.at[a],
                                         recv_sem=recv_sems.at[a], device_id=_sibling(), device_id_type=MESH)
            for a in range(n)]


def _scatter_copies(srcs, lands, send_sems, recv_sems):
    c = lax.axis_index("c")
    return [pltpu.make_async_remote_copy(
        src_ref=srcs[a].at[2 * cx + cy], dst_ref=lands[a].at[j], send_sem=send_sems.at[3 * a + j],
        recv_sem=recv_sems.at[3 * a + j], device_id=(cx, cy, c), device_id_type=MESH)
        for a in range(len(srcs)) for j, (cx, cy) in enumerate(_other_chips())]


def _scatter_start(hsums, name, after=()):
    n = len(hsums)
    na = len(after)

    def body(*refs):
        srcs, lands = refs[:n], refs[n:2 * n]
        send_sems, recv_sems = refs[2 * n + na], refs[2 * n + na + 1]
        for cp in _scatter_copies(srcs, lands, send_sems, recv_sems):
            cp.start()
        refs[-1][...] = jnp.zeros_like(refs[-1])

    lands = [lax.empty((3,) + g.shape[1:], g.dtype) for g in hsums]
    res = pl.pallas_call(
        body, name=name,
        out_shape=[pltpu.SemaphoreType.DMA((3 * n,)), pltpu.SemaphoreType.DMA((3 * n,))]
        + [pltpu.HBM(g.shape, g.dtype) for g in hsums] + [pltpu.HBM(g.shape, g.dtype) for g in lands]
        + [jax.ShapeDtypeStruct((8, 128), f32)],
        in_specs=[_HBM] * (2 * n) + [pl.BlockSpec(memory_space=pl.ANY)] * na,
        out_specs=[_SEM, _SEM] + [_HBM] * (2 * n) + [pl.BlockSpec(memory_space=pltpu.VMEM)],
        input_output_aliases={i: i + 2 for i in range(2 * n)},
        compiler_params=pltpu.CompilerParams(has_side_effects=_EFFECT),
    )(*[pltpu.with_memory_space_constraint(t, pltpu.HBM) for t in list(hsums) + lands], *after)
    return (res[0], res[1], res[2:2 + n], res[2 + n:2 + 2 * n]), res[-1]


def _scatter_wait(send_sems, recv_sems, srcs, lands, after, name):
    n = len(srcs)
    extra = list(after)

    def body(*refs):
        s_refs, l_refs = refs[:n], refs[n:2 * n]
        ss, rs = refs[2 * n], refs[2 * n + 1]
        for cp in _scatter_copies(s_refs, l_refs, ss, rs):
            cp.wait_send()
            cp.wait_recv()

    res = pl.pallas_call(
        body, name=name,
        out_shape=[pltpu.HBM(g.shape, g.dtype) for g in srcs] + [pltpu.HBM(g.shape, g.dtype) for g in lands],
        in_specs=[_HBM] * (2 * n) + [_SEM, _SEM] + [pl.BlockSpec(memory_space=pl.ANY)] * len(extra),
        out_specs=[_HBM] * (2 * n),
        input_output_aliases={i: i for i in range(2 * n)},
        compiler_params=pltpu.CompilerParams(has_side_effects=_EFFECT),
    )(*srcs, *lands, send_sems, recv_sems, *extra)
    return res[:n], res[n:]


def _plane_copies(src, land, send_sems, recv_sems):
    x, y, c = lax.axis_index("x"), lax.axis_index("y"), lax.axis_index("c")
    return [pltpu.make_async_remote_copy(src_ref=src, dst_ref=land.at[2 * x + y, c], send_sem=send_sems.at[j],
                                         recv_sem=recv_sems.at[j], device_id=(cx, cy, c), device_id_type=MESH)
            for j, (cx, cy) in enumerate(_other_chips())]


def _plane_start(pack, land, name):
    def body(src, lnd, send_sems, recv_sems, _s, _l, token):
        for cp in _plane_copies(src, lnd, send_sems, recv_sems):
            cp.start()
        token[...] = jnp.zeros_like(token)

    res = pl.pallas_call(
        body, name=name,
        out_shape=[pltpu.SemaphoreType.DMA((3,)), pltpu.SemaphoreType.DMA((3,)), pltpu.HBM(pack.shape, pack.dtype),
                   pltpu.HBM(land.shape, land.dtype), jax.ShapeDtypeStruct((8, 128), f32)],
        in_specs=[_HBM, _HBM], out_specs=[_SEM, _SEM, _HBM, _HBM, pl.BlockSpec(memory_space=pltpu.VMEM)],
        input_output_aliases={0: 2, 1: 3},
        compiler_params=pltpu.CompilerParams(has_side_effects=_EFFECT),
    )(pltpu.with_memory_space_constraint(pack, pltpu.HBM), pltpu.with_memory_space_constraint(land, pltpu.HBM))
    return res[:4], res[4]


def _plane_wait(send_sems, recv_sems, pack, land, after, name):
    def body(src, lnd, ss, rs, *_):
        for cp in _plane_copies(src, lnd, ss, rs):
            cp.wait_send()
            cp.wait_recv()

    return pl.pallas_call(
        body, name=name,
        out_shape=[pltpu.HBM(pack.shape, pack.dtype), pltpu.HBM(land.shape, land.dtype)],
        in_specs=[_HBM, _HBM, _SEM, _SEM] + [pl.BlockSpec(memory_space=pl.ANY)] * len(after),
        out_specs=[_HBM, _HBM], input_output_aliases={0: 0, 1: 1},
        compiler_params=pltpu.CompilerParams(has_side_effects=_EFFECT),
    )(pack, land, send_sems, recv_sems, *after)[1]


def _swap_halves(fulls):
    n = len(fulls)

    def body(*refs):
        ins, outs = refs[:n], refs[n:2 * n]
        send_sems, recv_sems = refs[2 * n:]
        c = lax.axis_index("c")
        sibling = (lax.axis_index("x"), lax.axis_index("y"), 1 - c)
        copies = []
        for a in range(n):
            cp = pltpu.make_async_remote_copy(src_ref=outs[a].at[:, c], dst_ref=outs[a].at[:, c], send_sem=send_sems.at[a],
                                              recv_sem=recv_sems.at[a], device_id=sibling, device_id_type=MESH)
            cp.start()
            copies.append(cp)
        for a, cp in enumerate(copies):
            cp.wait_send()
            theirs = outs[a].at[:, 1 - c]
            pltpu.make_async_remote_copy(src_ref=theirs, dst_ref=theirs, send_sem=send_sems.at[a], recv_sem=recv_sems.at[a],
                                         device_id=sibling, device_id_type=MESH).wait_recv()

    hbm = pl.BlockSpec(memory_space=pl.ANY)
    return pl.pallas_call(
        body, name="swap_halves",
        out_shape=[jax.ShapeDtypeStruct(p.shape, p.dtype) for p in fulls],
        in_specs=[hbm] * n, out_specs=[hbm] * n,
        input_output_aliases={a: a for a in range(n)},
        scratch_shapes=[pltpu.SemaphoreType.DMA((n,)), pltpu.SemaphoreType.DMA((n,))],
    )(*fulls)


def _to_segments(t):
    s, c = t.shape
    return t.reshape(SCAN_SEG, s // SCAN_SEG, c).transpose(1, 0, 2).reshape(s, c)


def _from_segments(t):
    s, c = t.shape
    return t.reshape(s // SCAN_SEG, SCAN_SEG, c).transpose(1, 0, 2).reshape(s, c)


def _ssm_operators(a_re, a_im, log_dt, b_re, b_im, c_re, c_im):
    lam = lax.complex(a_re, a_im)
    dt = jnp.exp(log_dt)[:, None]
    a_bar = jnp.exp(lam * dt)
    b_bar = ((a_bar - 1.0) / lam)[:, :, None] * lax.complex(b_re, b_im)
    eye = jnp.eye(N_GROUPS, dtype=f32)

    def embed_b(t):
        return (jnp.transpose(t, (0, 2, 1))[:, :, None, :] * eye[:, None, :, None]).reshape(D_SSM, D_STATE)

    def embed_c(t):
        return (jnp.transpose(t, (0, 2, 1))[:, :, None, :] * eye[:, None, :, None]).reshape(D_STATE, D_SSM)

    a2 = jnp.stack([a_bar.real.reshape(D_STATE), a_bar.imag.reshape(D_STATE)])
    return a2, embed_b(b_bar.real), embed_b(b_bar.imag), embed_c(c_re), embed_c(c_im)


def _local_step(x, target, mod, small, ffn_weights, mix_weights, grads_done, ffn_bwd_issued):
    table = jnp.asarray(_bucket_table())
    bias = small["att_bias"]
    L = DEPTH
    saved = []
    ssm_ops = []
    for l in range(L):
        sv = {}
        m9 = mod[l]
        sv["x0"] = x
        sv["w0"] = ffn_weights(l, 0, x)
        x, sv["f0"], sv["g0"], sv["u0"], sv["h0"] = _ffn_fwd(x, m9[0:3], *sv["w0"], 0, small["ln_g"][l, 0:1], small["ln_b"][l, 0:1])
        sv["x1"] = x
        sv["w1"] = mix_weights(l, x)
        *qkv, z_rest, sv["h1"] = _mix_in_fwd(x, m9[3:6], sv["w1"][0], 0)
        S = x.shape[0]
        qkv = [t.reshape(3, S, D_ATT) for t in qkv]
        att = [_att_fwd(qkv[b], bias, b) for b in range(3)]
        y_att, lse3 = _att_merge([att[b][0].reshape(d, S // d, D_ATT) for b, d in enumerate(DILATIONS)],
                                 [att[b][1].reshape(d, S // d, _LANES) for b, d in enumerate(DILATIONS)])
        sv.update(qkv=qkv, lse=[a[1] for a in att], lse3=lse3, y_att=y_att)

        prm = tuple(small[k][l] for k in ("ssm_a_re", "ssm_a_im", "ssm_log_dt", "ssm_b_re", "ssm_b_im", "ssm_c_re", "ssm_c_im"))
        (a2, bre, bim, cre, cim), ops_vjp = jax.vjp(_ssm_operators, *prm)
        ssm_ops.append(ops_vjp)
        u_ssm = _to_segments(z_rest[:, :D_SSM])
        sr, si = _ssm_states(u_ssm, bre, bim, a2)
        dskip = small["ssm_d"][l][None, :]
        glu_b = small["glu_b"][l][None, :]
        out_seg, y_seg = _ssm_out(sr, si, u_ssm, cre, cim, dskip, small["glu_w"][l], glu_b)
        y_ssm = _from_segments(out_seg)
        sv.update(a2=a2, bre=bre, bim=bim, cre=cre, cim=cim, u_ssm=u_ssm, sr=sr, si=si, y_seg=y_seg, y_ssm=y_ssm)

        u_pool = jnp.concatenate([jnp.zeros((POOL_HALO, D_POOL), f32), z_rest[:, D_SSM:]])
        y_pool = _pool_fwd(u_pool, small["pool_w"][l], small["pool_scale"][l][None, :])
        sv.update(u_pool=u_pool, y_pool=y_pool)

        x, sv["ymix"] = _mix_out_fwd(x, y_att, y_ssm, y_pool, m9[3:6], sv["w1"][1], 0, small["ln_g"][l, 1:2], small["ln_b"][l, 1:2])
        sv["x2"] = x
        sv["w2"] = ffn_weights(l, 1, x)
        x, sv["f2"], sv["g2"], sv["u2"], sv["h2"] = _ffn_fwd(x, m9[6:9], *sv["w2"], 0, small["ln_g"][l, 2:3], small["ln_b"][l, 2:3])
        saved.append(sv)

    loss, dx = _loss_head(x, target)

    dmod = [None] * L
    dln_g = [None] * L
    dln_b = [None] * L
    sg = {k: [None] * L for k in ("ssm_a_re", "ssm_a_im", "ssm_log_dt", "ssm_b_re", "ssm_b_im", "ssm_c_re", "ssm_c_im",
                                  "ssm_d", "glu_w", "glu_b", "pool_w", "pool_scale")}
    dbias_tot = None
    order_after = jnp.zeros((), f32)
    for l in reversed(range(L)):
        sv = saved[l]
        m9 = mod[l] + order_after

        def fresh(like):
            return [lax.empty(t.shape, bf16) for t in like]

        dx, dg, du, a, df, dm2, dlg2, dlb2 = _ffn_bwd(dx, sv["x2"], sv["f2"], sv["g2"], sv["u2"], m9[6:9], *sv["w2"], 0,
                                                     small["ln_g"][l, 2:3])
        m9 = m9 + ffn_bwd_issued(l, 1, dx)
        g_ffn1 = _ffn_wgrad(sv["h2"], dg, du, a, df, *fresh(sv["w2"]), 0)
        dxr, d_att, d_ssm, d_pool, dgate1, dlg1, dlb1, g_w_out = _mix_out_bwd(
            dx, sv["x1"], sv["ymix"], sv["y_att"], sv["y_ssm"], sv["y_pool"], m9[3:6], sv["w1"][1], 0, small["ln_g"][l, 1:2],
            fresh(sv["w1"])[1])
        S = d_att.shape[0]
        merged = _att_merge_bwd(d_att, sv["y_att"], sv["lse3"])
        dqkv, dbias = [], []
        for b, d in enumerate(DILATIONS):
            dq_b, db_b = _att_bwd(sv["qkv"][b], merged[b].reshape(S, D_ATT), sv["lse"][b], merged[3 + b].reshape(S, _LANES), bias, b)
            dqkv.append(dq_b.reshape(3, d, S // d, D_ATT))
            dbias.append(db_b)
        dbias = jnp.stack(dbias)
        dbias_tot = dbias if dbias_tot is None else dbias_tot + dbias
        d_seg = _to_segments(d_ssm)
        dskip = small["ssm_d"][l][None, :]
        glu_b = small["glu_b"][l][None, :]
        dy_seg, du_skip, dcre, dcim, dd, dglu_b, dglu_w = _ssm_out_bwd(
            d_seg, sv["y_seg"], sv["u_ssm"], sv["sr"], sv["si"], dskip, small["glu_w"][l], glu_b)
        du_seg, dbre, dbim, da2 = _ssm_states_bwd(dy_seg, du_skip, sv["u_ssm"], sv["sr"], sv["si"], sv["cre"], sv["cim"],
                                                  sv["bre"], sv["bim"], sv["a2"])
        d_prm = ssm_ops[l]((da2, dbre, dbim, dcre, dcim))
        for k, v in zip(("ssm_a_re", "ssm_a_im", "ssm_log_dt", "ssm_b_re", "ssm_b_im", "ssm_c_re", "ssm_c_im"), d_prm):
            sg[k][l] = v
        sg["ssm_d"][l] = dd[0]
        sg["glu_b"][l] = dglu_b[0]
        sg["glu_w"][l] = dglu_w
        du_ssm = _from_segments(du_seg)
        dyp = jnp.concatenate([d_pool, jnp.zeros((POOL_HALO, D_POOL), f32)])
        du_pool, dpw, dps = _pool_bwd(dyp, sv["u_pool"], small["pool_w"][l], small["pool_scale"][l][None, :])
        sg["pool_w"][l] = dpw
        sg["pool_scale"][l] = dps[0]
        d_rest = jnp.concatenate([du_ssm, du_pool], axis=1).astype(bf16)
        dx, dm1, dz = _mix_in_bwd(dqkv, d_rest, dxr, sv["x1"], m9[3:6], sv["w1"][0], 0)
        g_w_in = _mix_in_wgrad(sv["h1"], dz, fresh(sv["w1"])[0], 0)
        ffn_names = ("ffn_w_gate", "ffn_w_up", "ffn_w_down")
        m9 = m9 + grads_done(l, 1, list(zip(ffn_names, [(2 * l + 1) * FF_SHARD] * 3, g_ffn1))
                             + [("w_in", l * D_MODEL, g_w_in), ("w_out", l * 256, g_w_out)])
        dm1 = jnp.concatenate([dm1[0:2], dgate1])
        dx, dg, du, a, df, dm0, dlg0, dlb0 = _ffn_bwd(dx, sv["x0"], sv["f0"], sv["g0"], sv["u0"], m9[0:3], *sv["w0"], 0,
                                                     small["ln_g"][l, 0:1])
        issued = ffn_bwd_issued(l, 0, dx)
        g_ffn0 = _ffn_wgrad(sv["h0"], dg, du, a, df, *fresh(sv["w0"]), 0)
        order_after = grads_done(l, 0, list(zip(ffn_names, [2 * l * FF_SHARD] * 3, g_ffn0))) + issued
        dmod[l] = jnp.concatenate([dm0 + issued, dm1, dm2])
        dln_g[l] = jnp.concatenate([dlg0, dlg1, dlg2])
        dln_b[l] = jnp.concatenate([dlb0, dlb1, dlb2])

    small_grads = {k: jnp.stack(v) for k, v in sg.items()}
    small_grads["rel_bias"] = _bias_bwd(dbias_tot, table)
    small_grads["ln_g"] = jnp.stack(dln_g)
    small_grads["ln_b"] = jnp.stack(dln_b)
    return loss, dx, jnp.stack(dmod), small_grads


_TILE_ELEMS = 8 * 128


def _pack_rows(shapes):
    out, row = [], 0
    for s in shapes:
        nr = -(-int(np.prod(s)) // _TILE_ELEMS) * 8
        out.append((row, nr))
        row += nr
    return out


def _pack(arrs):
    parts = []
    for a in arrs:
        flat = a.reshape(-1).astype(f32)
        npad = -(-flat.shape[0] // _TILE_ELEMS) * _TILE_ELEMS
        parts.append(jnp.pad(flat, (0, npad - flat.shape[0])).reshape(npad // 128, 128))
    return jnp.concatenate(parts, axis=0)


def _unpack(buf, shapes):
    return [buf[row:row + nr].reshape(-1)[:int(np.prod(s))].reshape(s) for s, (row, nr) in zip(shapes, _pack_rows(shapes))]


_REPL = ("rel_bias", "ada_b", "ssm_a_re", "ssm_a_im", "ssm_log_dt", "ssm_b_re", "ssm_b_im", "ssm_c_re", "ssm_c_im",
         "ssm_d", "glu_b", "pool_w", "pool_scale")
_SMALL_SHARDED = ("ln_g", "ln_b", "glu_w")
_BIG = ("ffn_w_gate", "ffn_w_up", "ffn_w_down", "w_in", "w_out")
_ORDER = ("rel_bias", "ada_w", "ada_b", "ln_g", "ln_b", "ffn_w_gate", "ffn_w_up", "ffn_w_down", "w_in", "w_out",
          "ssm_a_re", "ssm_a_im", "ssm_log_dt", "ssm_b_re", "ssm_b_im", "ssm_c_re", "ssm_c_im", "ssm_d", "glu_w",
          "glu_b", "pool_w", "pool_scale")


def kernel(x, c, rel_bias, ada_w, ada_b, ln_g, ln_b, ffn_w_gate, ffn_w_up, ffn_w_down, w_in, w_out, ssm_a_re, ssm_a_im, ssm_log_dt, ssm_b_re, ssm_b_im, ssm_c_re, ssm_c_im, ssm_d, glu_w, glu_b, pool_w, pool_scale, loss_target, m_rel_bias, m_ada_w, m_ada_b, m_ln_g, m_ln_b, m_ffn_w_gate, m_ffn_w_up, m_ffn_w_down, m_w_in, m_w_out, m_ssm_a_re, m_ssm_a_im, m_ssm_log_dt, m_ssm_b_re, m_ssm_b_im, m_ssm_c_re, m_ssm_c_im, m_ssm_d, m_glu_w, m_glu_b, m_pool_w, m_pool_scale, v_rel_bias, v_ada_w, v_ada_b, v_ln_g, v_ln_b, v_ffn_w_gate, v_ffn_w_up, v_ffn_w_down, v_w_in, v_w_out, v_ssm_a_re, v_ssm_a_im, v_ssm_log_dt, v_ssm_b_re, v_ssm_b_im, v_ssm_c_re, v_ssm_c_im, v_ssm_d, v_glu_w, v_glu_b, v_pool_w, v_pool_scale):
    args = dict(locals())
    w = {k: args[k] for k in _ORDER}
    m = {k: args["m_" + k] for k in _ORDER}
    v = {k: args["v_" + k] for k in _ORDER}
    L, D = DEPTH, D_MODEL
    ax, ay, ac = lax.axis_index("x"), lax.axis_index("y"), lax.axis_index("c")
    p_me = 2 * ax + ay
    dev = 4 * ax + 2 * ay + ac

    transposed = ("ffn_w_gate", "ffn_w_up")
    for d in (w, m, v):
        for name in transposed:
            d[name] = jnp.swapaxes(d[name], 2, 3)

    def halves(t):
        return t.astype(bf16).reshape(1, 2, t.shape[0] // 2, t.shape[1])

    def landing(src):
        return lax.dynamic_update_slice(lax.empty((N_CHIPS,) + src.shape, bf16), src[None], (p_me, 0, 0, 0, 0))

    chunk_keys = [("ffn", 0, 0), ("mix", 0), ("ffn", 0, 1), ("ffn", 1, 0), ("mix", 1), ("ffn", 1, 1)]
    chunk_srcs = []
    for key in chunk_keys:
        if key[0] == "ffn":
            chunk_srcs.append([halves(w[name][key[1], key[2]]) for name in ("ffn_w_gate", "ffn_w_up", "ffn_w_down")])
        else:
            chunk_srcs.append([halves(w_in[key[1]]), halves(w_out[key[1]])])

    pack = _pack([c, ln_g, ln_b, glu_w])
    rows = pack.shape[0]
    allp = _allgather8(pack).reshape(8, rows, 128)
    chunks = [(srcs, [landing(t) for t in srcs]) for srcs in chunk_srcs]
    first_in_flight, first_begun = _gather_start(chunks[:1], allp, "gather_start_first")
    c_all = allp[:, :8].reshape(8, D) + first_begun[0, 0]
    by_chip = allp[0::2]

    fwd_rows = _pack_rows([c.shape, ln_g.shape, ln_b.shape, glu_w.shape])

    def sharded(part, shape, axis):
        row0, nrows = fwd_rows[part]
        t = by_chip[:, row0:row0 + nrows].reshape(N_CHIPS, -1)[:, :int(np.prod(shape))].reshape((N_CHIPS,) + shape)
        return jnp.concatenate([t[p] for p in range(N_CHIPS)], axis=axis)

    ln_g_full = sharded(1, ln_g.shape, 2)
    ln_b_full = sharded(2, ln_b.shape, 2)
    glu_w_full = sharded(3, glu_w.shape, 1)

    ncol = ada_w.shape[-1]
    ada_b_cols = lax.dynamic_slice_in_dim(ada_b, p_me * ncol, ncol, axis=1)[:, None, :]
    mod_part = _ada_fwd(c_all, ada_w, ada_b_cols)
    mrows = L * 8 * ncol // 128
    mod_pack = mod_part.reshape(mrows, 128)
    mod_land = lax.dynamic_update_slice(lax.empty((N_CHIPS, 2, mrows, 128), f32), mod_pack[None, None], (p_me, ac, 0, 0))
    mod_in_flight, _ = _plane_start(mod_pack, mod_land, "mod_start")
    att_bias = _bias_fwd(rel_bias, jnp.asarray(_bucket_table()))
    small_names = _REPL + _SMALL_SHARDED
    small_packs = [_pack([d[k] for k in small_names]) for d in (w, m, v)]
    mod_land = _plane_wait(*mod_in_flight, [att_bias] + small_packs + [t for _, lands in chunks[1:] for t in lands], "mod_wait")
    mod_all = _swap_halves([mod_land])[0].reshape(8, L, 8, ncol)
    mod_mine = lax.dynamic_index_in_dim(mod_all, dev, axis=2, keepdims=False)
    mod = jnp.concatenate([mod_mine[2 * p] for p in range(N_CHIPS)], axis=-1).reshape(L, 9, D)

    rest_in_flight, rest_begun = _gather_start(chunks[1:], mod, "gather_start_rest")
    in_flight = first_in_flight + rest_in_flight

    forwarding = {}

    def forward(k, after):
        lands = _gather_wait(*in_flight[k], [after, rest_begun], "gather_wait_%d" % k)
        forwarding[k], begun = _split_start(_forward_copies, lands, 3 * len(lands), "gather_forward_start_%d" % k)
        return begun

    def gathered(key, after):
        k = chunk_keys.index(key)
        order = [after]
        if k not in forwarding:
            order.append(forward(k, after))
        if 3 <= k + 1 < len(chunk_keys):
            order.append(forward(k + 1, after))
        lands = _split_wait(_forward_copies, *forwarding[k], order, "gather_forward_wait_%d" % k)
        return [t.reshape(N_CHIPS, 1, 2 * t.shape[3], t.shape[4]) for t in lands]

    pc = jnp.stack([p_me, ac]).astype(jnp.int32)
    groups = {}
    scattering = {}

    pairing = {}

    def start_pairs(tag, after=()):
        g5 = [g.reshape(g.shape[:2] + (2, g.shape[2] // 2, g.shape[3])) for _, _, g in groups[tag]]
        gots = [lax.empty(g.shape[:2] + g.shape[3:], bf16) for g in g5]
        pairing[tag], begun = _split_start(_pair_copies, g5 + gots, len(g5), "pair_exchange_start_%s" % tag, after)
        return begun

    def start_group(tag, after):
        arrays = _split_wait(_pair_copies, *pairing[tag], after, "pair_exchange_wait_%s" % tag)
        n = len(arrays) // 2
        hsum = _pair_sum(arrays[:n], arrays[n:], pc)
        scattering[tag], begun = _scatter_start(hsum, "scatter_start_%s" % tag)
        return begun

    def grads_done(l, s, grads):
        if l == 1:
            groups.setdefault("l1", []).extend(grads)
            return start_pairs("l1")[0, 0] if s == 0 else jnp.zeros((), f32)
        groups["l0a" if s == 1 else "l0b"] = grads
        return start_pairs("l0a")[0, 0] if s == 1 else jnp.zeros((), f32)

    def ffn_bwd_issued(l, s, dx):
        if l == 0:
            return start_group("l1" if s == 1 else "l0a", [dx])[0, 0]
        return jnp.zeros((), f32)

    small = {k: w[k] for k in _REPL if k != "ada_b"}
    small.update(ln_g=ln_g_full, ln_b=ln_b_full, glu_w=glu_w_full, att_bias=att_bias)
    loss_dev, grad_x, dmod, sgrads = _local_step(
        x[0], loss_target[0], mod, small, lambda l, s, after: gathered(("ffn", l, s), after),
        lambda l, after: gathered(("mix", l), after), grads_done, ffn_bwd_issued)
    loss = lax.psum(loss_dev[0, 0], ("x", "y", "c"))

    names = ("rel_bias", "ln_g", "ln_b", "ssm_a_re", "ssm_a_im", "ssm_log_dt", "ssm_b_re", "ssm_b_im", "ssm_c_re",
             "ssm_c_im", "ssm_d", "glu_w", "glu_b", "pool_w", "pool_scale")
    gpack = _pack([dmod] + [sgrads[k] for k in names])
    grows = gpack.shape[0]
    land = lax.dynamic_update_slice(lax.empty((N_CHIPS, 2, grows, 128), f32), gpack[None, None], (p_me, ac, 0, 0))
    small_in_flight, small_begun = _plane_start(gpack, land, "small_grads_start")
    l0b_begun = start_group("l0b", [start_pairs("l0b", (small_begun,))])

    out_g, out_d, out_m, out_v = {}, {}, {}, {}
    row_tile = dict(zip(_BIG, (352, 352, 352, 256, 256)))
    big = {name: None for name in _BIG}

    swapping = {}

    def reduce_group(tag, after):
        hsum, recv = _scatter_wait(*scattering[tag], after, "scatter_wait_%s" % tag)
        full = _sum_shards(hsum, recv, pc)
        swapping[tag], begun = _split_start(_swap_copies, full, len(full), "swap_halves_start_%s" % tag)
        return [begun]

    def update_group(tag, after):
        full = _split_wait(_swap_copies, *swapping[tag], after, "swap_halves_wait_%s" % tag)
        for (name, row0, _), g in zip(groups[tag], full):
            shp = w[name].shape
            r2 = (int(np.prod(shp[:-1])), shp[-1])
            big[name] = _adamw(w[name].reshape(r2), m[name].reshape(r2), v[name].reshape(r2), g.reshape(-1, shp[-1]),
                               row_tile[name], row0, big[name])
        return [big[name][1] for name, _, _ in groups[tag]]

    after = reduce_group("l0a", reduce_group("l1", [grad_x, l0b_begun]))
    after = update_group("l0a", update_group("l1", after))

    land = _plane_wait(*small_in_flight, after, "small_grads_wait")
    gall = _swap_halves([land])[0].reshape(8, grows, 128)
    gsum = _unpack(_sum8(gall), [(L, 9 * D)] + [sgrads[k].shape for k in names])
    red = dict(zip(("ada_b",) + names, gsum))
    red["ln_g"] = lax.dynamic_slice_in_dim(red["ln_g"], p_me * 256, 256, axis=2)
    red["ln_b"] = lax.dynamic_slice_in_dim(red["ln_b"], p_me * 256, 256, axis=2)
    red["glu_w"] = lax.dynamic_slice_in_dim(red["glu_w"], p_me * 64, 64, axis=1)

    dmod_all = gall[:, :L * 9 * D // 128].reshape(8, L, 9 * D)
    dmod_cols = jnp.transpose(lax.dynamic_slice_in_dim(dmod_all, p_me * ncol, ncol, axis=2), (1, 0, 2))
    g_ada_w = _ada_wgrad(jnp.transpose(c_all), dmod_cols)

    r2 = (L * D, ncol)
    res = _adamw(ada_w.reshape(r2), m["ada_w"].reshape(r2), v["ada_w"].reshape(r2), g_ada_w.reshape(r2), 128)
    out_g["ada_w"], out_d["ada_w"], out_m["ada_w"], out_v["ada_w"] = [t.reshape(ada_w.shape) for t in res]

    res_small = _adamw(*small_packs, _pack([red[k] for k in small_names]), small_packs[0].shape[0])
    for t, dst in zip(res_small, (out_g, out_d, out_m, out_v)):
        for k, a in zip(small_names, _unpack(t, [w[k].shape for k in small_names])):
            dst[k] = a

    update_group("l0b", reduce_group("l0b", [res_small[1], res[1]]))
    for name in _BIG:
        res = [t.reshape(w[name].shape) for t in big[name]]
        out_g[name], out_d[name], out_m[name], out_v[name] = [jnp.swapaxes(t, 2, 3) for t in res] if name in transposed else res

    return (loss, grad_x[None], *[out_g[k] for k in _ORDER], *[out_d[k] for k in _ORDER],
            *[out_m[k] for k in _ORDER], *[out_v[k] for k in _ORDER])
```

```python
import math

import numpy as np
import jax
import jax.numpy as jnp
from jax import lax
from jax.experimental import pallas as pl
from jax.experimental.pallas import tpu as pltpu

f32 = jnp.float32
bf16 = jnp.bfloat16
MESH = pl.DeviceIdType.MESH

D_MODEL = 1024
SEQ = 2048
DEPTH = 2
HEAD_DIM = 64
N_HEADS = 8
D_ATT = 512
DILATIONS = (1, 4, 16)
BLOCKS_PER_RESIDUE = (16, 4, 1)
ATT_BLOCK = 128
N_UNITS = SEQ // ATT_BLOCK
N_GROUPS = 16
SSM_STATE = 64
D_SSM = 256
D_STATE = N_GROUPS * SSM_STATE
POOL_WINDOWS = (2, 4, 8, 16)
POOL_GROUP = 64
D_POOL = 256
POOL_HALO = 16
D_FF = 2816
N_BUCKETS = 32
MAX_DISTANCE = 2048
ALPHA = (2 * DEPTH) ** 0.25
FFN_RES = 0.5
LN_EPS = 1e-5
NEG = -1e30
N_CHIPS = 4
FF_SHARD = D_FF // N_CHIPS
SCAN_SEG = 8

ADAM_LR, ADAM_B1, ADAM_B2, ADAM_EPS, ADAM_WD, ADAM_STEP = 0.001, 0.9, 0.999, 1e-08, 0.01, 10

TOK_TILE = 512


def _cp(dims=None, vmem_mb=None):
    kw = {}
    if dims is not None:
        kw["dimension_semantics"] = dims
    if vmem_mb is not None:
        kw["vmem_limit_bytes"] = vmem_mb << 20
    return pltpu.CompilerParams(**kw)


def _dot(a, b):
    return jnp.dot(a, b, preferred_element_type=f32)


def _dot_nt(a, b):
    return lax.dot_general(a, b, (((1,), (1,)), ((), ())), preferred_element_type=f32)


def _dot_tn(a, b):
    return lax.dot_general(a, b, (((0,), (0,)), ((), ())), preferred_element_type=f32)


def _ln_stats(v):
    mu = jnp.mean(v, -1, keepdims=True)
    d = v - mu
    var = jnp.mean(d * d, -1, keepdims=True)
    rstd = lax.rsqrt(var + LN_EPS)
    return d * rstd, rstd


def _ln_bwd(dxh, xh, rstd):
    return rstd * (dxh - jnp.mean(dxh, -1, keepdims=True) - xh * jnp.mean(dxh * xh, -1, keepdims=True))


_GELU_C = math.sqrt(2.0 / math.pi)


def _gelu(y):
    return 0.5 * y * (1.0 + jnp.tanh(_GELU_C * (y + 0.044715 * y * y * y)))


def _gelu_grad(y):
    t = jnp.tanh(_GELU_C * (y + 0.044715 * y * y * y))
    return 0.5 * (1.0 + t) + 0.5 * y * (1.0 - t * t) * (_GELU_C * (1.0 + 3 * 0.044715 * y * y))


def _full(shape):
    return pl.BlockSpec(shape, lambda *_: (0,) * len(shape))


def _hbm(*args):
    return [pltpu.with_memory_space_constraint(a, pltpu.HBM) if getattr(a, "ndim", 0) >= 2 else a for a in args]


def _ffn_fwd(x, mod3, wg, wu, wd, ls, lng, lnb):
    S, D = x.shape
    Fs = wg.shape[-2]
    ts = TOK_TILE

    def body(x_ref, mod_ref, wg_ref, wu_ref, wd_ref, lng_ref, lnb_ref, xo_ref, f_ref, g_ref, u_ref, h_ref, acc_sc):
        j = pl.program_id(1)

        @pl.when(j == 0)
        def _():
            xh, _ = _ln_stats(x_ref[...])
            h_ref[...] = (xh * (1.0 + mod_ref[1:2, :]) + mod_ref[0:1, :]).astype(bf16)
            acc_sc[...] = jnp.zeros_like(acc_sc)

        h = h_ref[...]
        g = _dot_nt(h, wg_ref[0, 0])
        u = _dot_nt(h, wu_ref[0, 0])
        g_ref[0] = g.astype(bf16)
        u_ref[0] = u.astype(bf16)
        a = (g * jax.nn.sigmoid(g) * u).astype(bf16)
        acc_sc[...] += _dot(a, wd_ref[0, 0])

        @pl.when(j == N_CHIPS - 1)
        def _():
            f = acc_sc[...]
            f_ref[...] = f
            r = ALPHA * x_ref[...] + (FFN_RES * mod_ref[2:3, :]) * f
            rh, _ = _ln_stats(r)
            xo_ref[...] = rh * lng_ref[...] + lnb_ref[...]

    tok = pl.BlockSpec((ts, D), lambda i, j: (i, 0))
    wrow = pl.BlockSpec((1, 1, Fs, D), lambda i, j: (j, ls, 0, 0))
    hid = pl.BlockSpec((1, ts, Fs), lambda i, j: (j, i, 0))
    return pl.pallas_call(
        body, name="ffn_fwd", grid=(S // ts, N_CHIPS),
        in_specs=[tok, _full((3, D)), wrow, wrow, wrow, _full((1, D)), _full((1, D))],
        out_specs=[tok, tok, hid, hid, tok],
        out_shape=[jax.ShapeDtypeStruct((S, D), f32), jax.ShapeDtypeStruct((S, D), f32),
                   jax.ShapeDtypeStruct((N_CHIPS, S, Fs), bf16), jax.ShapeDtypeStruct((N_CHIPS, S, Fs), bf16),
                   jax.ShapeDtypeStruct((S, D), bf16)],
        scratch_shapes=[pltpu.VMEM((ts, D), f32)],
        compiler_params=_cp(("parallel", "arbitrary"), 56),
    )(*_hbm(x, mod3, wg, wu, wd, lng, lnb))


def _ffn_bwd(dxo, x, f, g, u, mod3, wg, wu, wd, ls, lng):
    S, D = x.shape
    Fs = wg.shape[-2]
    ts = TOK_TILE

    def body(dxo_ref, x_ref, f_ref, g_ref, u_ref, mod_ref, wg_ref, wu_ref, wd_ref, lng_ref,
             dx_ref, dg_ref, du_ref, a_ref, df_ref, dmod_ref, dlng_ref, dlnb_ref,
             dr_sc, df_sc, acc_sc):
        i = pl.program_id(0)
        j = pl.program_id(1)

        @pl.when((i == 0) & (j == 0))
        def _():
            dmod_ref[...] = jnp.zeros_like(dmod_ref)
            dlng_ref[...] = jnp.zeros_like(dlng_ref)
            dlnb_ref[...] = jnp.zeros_like(dlnb_ref)

        @pl.when(j == 0)
        def _():
            xv = x_ref[...]
            fv = f_ref[...]
            gate = mod_ref[2:3, :]
            rh, rstd = _ln_stats(ALPHA * xv + (FFN_RES * gate) * fv)
            dy = dxo_ref[...]
            dlng_ref[...] += jnp.sum(dy * rh, 0, keepdims=True)
            dlnb_ref[...] += jnp.sum(dy, 0, keepdims=True)
            dr = _ln_bwd(dy * lng_ref[...], rh, rstd)
            dr_sc[...] = dr
            dmod_ref[2:3, :] += jnp.sum(FFN_RES * dr * fv, 0, keepdims=True)
            df = ((FFN_RES * gate) * dr).astype(bf16)
            df_sc[...] = df
            df_ref[...] = df
            acc_sc[...] = jnp.zeros_like(acc_sc)

        da = _dot_nt(df_sc[...], wd_ref[0, 0])
        gv = g_ref[0].astype(f32)
        uv = u_ref[0].astype(f32)
        sg = jax.nn.sigmoid(gv)
        si = gv * sg
        a_ref[0] = (si * uv).astype(bf16)
        dgv = (da * uv * (sg * (1.0 + gv * (1.0 - sg)))).astype(bf16)
        duv = (da * si).astype(bf16)
        dg_ref[0] = dgv
        du_ref[0] = duv
        acc_sc[...] += _dot(dgv, wg_ref[0, 0]) + _dot(duv, wu_ref[0, 0])

        @pl.when(j == N_CHIPS - 1)
        def _():
            dh = acc_sc[...]
            xh, rstd0 = _ln_stats(x_ref[...])
            dmod_ref[0:1, :] += jnp.sum(dh, 0, keepdims=True)
            dmod_ref[1:2, :] += jnp.sum(dh * xh, 0, keepdims=True)
            dx_ref[...] = _ln_bwd(dh * (1.0 + mod_ref[1:2, :]), xh, rstd0) + ALPHA * dr_sc[...]

    tok = pl.BlockSpec((ts, D), lambda i, j: (i, 0))
    wrow = pl.BlockSpec((1, 1, Fs, D), lambda i, j: (j, ls, 0, 0))
    hid = pl.BlockSpec((1, ts, Fs), lambda i, j: (j, i, 0))
    hid_shape = jax.ShapeDtypeStruct((N_CHIPS, S, Fs), bf16)
    return pl.pallas_call(
        body, name="ffn_bwd", grid=(S // ts, N_CHIPS),
        in_specs=[tok, tok, tok, hid, hid, _full((3, D)), wrow, wrow, wrow, _full((1, D))],
        out_specs=[tok, hid, hid, hid, tok, _full((3, D)), _full((1, D)), _full((1, D))],
        out_shape=[jax.ShapeDtypeStruct((S, D), f32), hid_shape, hid_shape, hid_shape,
                   jax.ShapeDtypeStruct((S, D), bf16),
                   jax.ShapeDtypeStruct((3, D), f32), jax.ShapeDtypeStruct((1, D), f32), jax.ShapeDtypeStruct((1, D), f32)],
        scratch_shapes=[pltpu.VMEM((ts, D), f32), pltpu.VMEM((ts, D), bf16), pltpu.VMEM((ts, D), f32)],
        compiler_params=_cp(("arbitrary", "arbitrary"), 56),
    )(*_hbm(dxo, x, f, g, u, mod3, wg, wu, wd, lng))


def _ffn_wgrad(h, dg, du, a, df, gwg, gwu, gwd, ls):
    S, D = h.shape
    Fs = dg.shape[-1]
    tk = TOK_TILE
    nk = S // tk

    def body(h_ref, dg_ref, du_ref, a_ref, df_ref, _g0, _g1, _g2, gwg_ref, gwu_ref, gwd_ref, ag_sc, au_sc, ad_sc):
        k = pl.program_id(1)

        @pl.when(k == 0)
        def _():
            ag_sc[...] = jnp.zeros_like(ag_sc)
            au_sc[...] = jnp.zeros_like(au_sc)
            ad_sc[...] = jnp.zeros_like(ad_sc)

        hv = h_ref[...]
        ag_sc[...] += _dot_tn(dg_ref[0], hv)
        au_sc[...] += _dot_tn(du_ref[0], hv)
        ad_sc[...] += _dot_tn(a_ref[0], df_ref[...])

        @pl.when(k == nk - 1)
        def _():
            gwg_ref[0, 0] = ag_sc[...].astype(bf16)
            gwu_ref[0, 0] = au_sc[...].astype(bf16)
            gwd_ref[0, 0] = ad_sc[...].astype(bf16)

    tok = pl.BlockSpec((tk, D), lambda p, k: (k, 0))
    hid = pl.BlockSpec((1, tk, Fs), lambda p, k: (p, k, 0))
    anyspec = pl.BlockSpec(memory_space=pl.ANY)
    orow = pl.BlockSpec((1, 1, Fs, D), lambda p, k: (p, ls, 0, 0))
    return pl.pallas_call(
        body, name="ffn_wgrad", grid=(N_CHIPS, nk),
        in_specs=[tok, hid, hid, hid, tok, anyspec, anyspec, anyspec],
        out_specs=[orow, orow, orow],
        out_shape=[jax.ShapeDtypeStruct(gwg.shape, bf16), jax.ShapeDtypeStruct(gwu.shape, bf16),
                   jax.ShapeDtypeStruct(gwd.shape, bf16)],
        scratch_shapes=[pltpu.VMEM((Fs, D), f32), pltpu.VMEM((Fs, D), f32), pltpu.VMEM((Fs, D), f32)],
        input_output_aliases={5: 0, 6: 1, 7: 2},
        compiler_params=_cp(("parallel", "arbitrary"), 48),
    )(*_hbm(h, dg, du, a, df, gwg, gwu, gwd))


_LANES = 128
_QKV_BLOCKS = D_ATT // _LANES


def _res_spec(lead, d, width, index):
    return pl.BlockSpec((lead, d, TOK_TILE // d, width), index)


def _res_spec3(d, width):
    return pl.BlockSpec((d, TOK_TILE // d, width), lambda i: (0, i, 0))


def _rows_to_residues(tile_bufs, d, put):
    for r in range(d):
        for cb, buf in enumerate(tile_bufs):
            put(r, cb, buf[pl.ds(r, TOK_TILE // d, stride=d), :])


def _residues_to_rows(tile_bufs, d, get):
    for r in range(d):
        for cb, buf in enumerate(tile_bufs):
            buf[pl.ds(r, TOK_TILE // d, stride=d), :] = get(r, cb)


def _mix_in_fwd(x, mod3, w_in, l):
    S, D = x.shape
    N = w_in.shape[-1]
    ts = TOK_TILE

    def body(x_ref, mod_ref, w_ref, o1_ref, o4_ref, o16_ref, zr_ref, h_ref, *bufs):
        j = pl.program_id(1)

        @pl.when(j == 0)
        def _():
            xh, _ = _ln_stats(x_ref[...])
            h_ref[...] = (xh * (1.0 + mod_ref[1:2, :]) + mod_ref[0:1, :]).astype(bf16)

        z = _dot(h_ref[...], w_ref[0, 0])

        @pl.when(j == N_CHIPS - 1)
        def _():
            zr_ref[...] = z

        @pl.when(j < N_CHIPS - 1)
        def _():
            zz = z * jnp.where(j == 0, HEAD_DIM ** -0.5, 1.0)
            o1_ref[j, 0] = zz.astype(bf16)
            for cb, buf in enumerate(bufs):
                buf[...] = zz[:, _LANES * cb:_LANES * (cb + 1)]
            for d, o_ref in zip(DILATIONS[1:], (o4_ref, o16_ref)):
                def put(r, cb, piece, o_ref=o_ref):
                    o_ref[j, r, :, _LANES * cb:_LANES * (cb + 1)] = piece.astype(bf16)
                _rows_to_residues(bufs, d, put)

    tok = pl.BlockSpec((ts, D), lambda i, j: (i, 0))
    res = [_res_spec(3, d, N, lambda i, j: (0, 0, i, 0)) for d in DILATIONS]
    return pl.pallas_call(
        body, name="mix_in_fwd", grid=(S // ts, N_CHIPS),
        in_specs=[tok, _full((3, D)), pl.BlockSpec((1, 1, D, N), lambda i, j: (j, l, 0, 0))],
        out_specs=res + [pl.BlockSpec((ts, N), lambda i, j: (i, 0)), tok],
        out_shape=[jax.ShapeDtypeStruct((3, d, S // d, N), bf16) for d in DILATIONS]
        + [jax.ShapeDtypeStruct((S, N), f32), jax.ShapeDtypeStruct((S, D), bf16)],
        scratch_shapes=[pltpu.VMEM((ts, _LANES), f32)] * _QKV_BLOCKS,
        compiler_params=_cp(("parallel", "arbitrary"), 40),
    )(*_hbm(x, mod3, w_in))


def _mix_in_bwd(dqkv, d_rest, dx_res, x, mod3, w_in, l):
    S, D = x.shape
    N = w_in.shape[-1]
    ts = TOK_TILE

    def body(d1_ref, d4_ref, d16_ref, dr_ref, dxr_ref, x_ref, mod_ref, w_ref, dx_ref, dmod_ref, dz_ref, acc_sc, *bufs):
        i = pl.program_id(0)
        j = pl.program_id(1)

        @pl.when((i == 0) & (j == 0))
        def _():
            dmod_ref[...] = jnp.zeros_like(dmod_ref)

        @pl.when(j == 0)
        def _():
            acc_sc[...] = jnp.zeros_like(acc_sc)

        @pl.when(j == N_CHIPS - 1)
        def _():
            dz_ref[0] = dr_ref[...]

        @pl.when(j < N_CHIPS - 1)
        def _():
            for d, d_ref, tile_bufs in ((4, d4_ref, bufs[:_QKV_BLOCKS]), (16, d16_ref, bufs[_QKV_BLOCKS:])):
                _residues_to_rows(tile_bufs, d, lambda r, cb, d_ref=d_ref: d_ref[0, r, :, _LANES * cb:_LANES * (cb + 1)].astype(f32))
            for cb in range(_QKV_BLOCKS):
                cols = slice(_LANES * cb, _LANES * (cb + 1))
                dz_ref[0, :, cols] = (d1_ref[0, 0, :, cols].astype(f32) + bufs[cb][...] + bufs[_QKV_BLOCKS + cb][...]).astype(bf16)

        acc_sc[...] += _dot_nt(dz_ref[0], w_ref[0, 0])

        @pl.when(j == N_CHIPS - 1)
        def _():
            dh = acc_sc[...]
            xh, rstd0 = _ln_stats(x_ref[...])
            dmod_ref[0:1, :] += jnp.sum(dh, 0, keepdims=True)
            dmod_ref[1:2, :] += jnp.sum(dh * xh, 0, keepdims=True)
            dx_ref[...] = _ln_bwd(dh * (1.0 + mod_ref[1:2, :]), xh, rstd0) + dxr_ref[...]

    tok = pl.BlockSpec((ts, D), lambda i, j: (i, 0))
    res = [_res_spec(1, d, N, lambda i, j: (jnp.minimum(j, 2), 0, i, 0)) for d in DILATIONS]
    return pl.pallas_call(
        body, name="mix_in_bwd", grid=(S // ts, N_CHIPS),
        in_specs=res + [pl.BlockSpec((ts, N), lambda i, j: (i, 0)), tok, tok, _full((3, D)),
                        pl.BlockSpec((1, 1, D, N), lambda i, j: (j, l, 0, 0))],
        out_specs=[tok, _full((3, D)), pl.BlockSpec((1, ts, N), lambda i, j: (j, i, 0))],
        out_shape=[jax.ShapeDtypeStruct((S, D), f32), jax.ShapeDtypeStruct((3, D), f32),
                   jax.ShapeDtypeStruct((N_CHIPS, S, N), bf16)],
        scratch_shapes=[pltpu.VMEM((ts, D), f32)] + [pltpu.VMEM((ts, _LANES), f32)] * (2 * _QKV_BLOCKS),
        compiler_params=_cp(("arbitrary", "arbitrary"), 40),
    )(*_hbm(*dqkv, d_rest, dx_res, x, mod3, w_in))


def _mix_in_wgrad(h, dz, gw, l):
    S, D = h.shape
    N = dz.shape[-1]
    tk = TOK_TILE
    nk = S // tk

    def body(h_ref, dz_ref, _g, gw_ref, acc_sc):
        k = pl.program_id(1)

        @pl.when(k == 0)
        def _():
            acc_sc[...] = jnp.zeros_like(acc_sc)

        acc_sc[...] += _dot_tn(h_ref[...], dz_ref[0])

        @pl.when(k == nk - 1)
        def _():
            gw_ref[0, 0] = acc_sc[...].astype(bf16)

    return pl.pallas_call(
        body, name="mix_in_wgrad", grid=(N_CHIPS, nk),
        in_specs=[pl.BlockSpec((tk, D), lambda p, k: (k, 0)), pl.BlockSpec((1, tk, N), lambda p, k: (p, k, 0)),
                  pl.BlockSpec(memory_space=pl.ANY)],
        out_specs=pl.BlockSpec((1, 1, D, N), lambda p, k: (p, l, 0, 0)),
        out_shape=jax.ShapeDtypeStruct(gw.shape, bf16),
        scratch_shapes=[pltpu.VMEM((D, N), f32)],
        input_output_aliases={2: 0},
        compiler_params=_cp(("parallel", "arbitrary"), 40),
    )(*_hbm(h, dz, gw))


def _mix_out_fwd(x, y_att, y_ssm, y_pool, mod3, w_out, l, lng, lnb):
    S, D = x.shape
    ts = TOK_TILE

    def body(x_ref, ya_ref, ys_ref, yp_ref, mod_ref, w_ref, lng_ref, lnb_ref, xo_ref, y_ref):
        ya = ya_ref[...].astype(bf16)
        y = (_dot(ya[:, 0:256], w_ref[0, 0]) + _dot(ya[:, 256:512], w_ref[1, 0])
             + _dot(ys_ref[...].astype(bf16), w_ref[2, 0]) + _dot(yp_ref[...].astype(bf16), w_ref[3, 0]))
        y_ref[...] = y
        rh, _ = _ln_stats(ALPHA * x_ref[...] + mod_ref[2:3, :] * y)
        xo_ref[...] = rh * lng_ref[...] + lnb_ref[...]

    tok = pl.BlockSpec((ts, D), lambda i: (i, 0))
    return pl.pallas_call(
        body, name="mix_out_fwd", grid=(S // ts,),
        in_specs=[tok, pl.BlockSpec((ts, D_ATT), lambda i: (i, 0)), pl.BlockSpec((ts, D_SSM), lambda i: (i, 0)),
                  pl.BlockSpec((ts, D_POOL), lambda i: (i, 0)), _full((3, D)),
                  pl.BlockSpec((N_CHIPS, 1, 256, D), lambda i: (0, l, 0, 0)), _full((1, D)), _full((1, D))],
        out_specs=[tok, tok],
        out_shape=[jax.ShapeDtypeStruct((S, D), f32), jax.ShapeDtypeStruct((S, D), f32)],
        compiler_params=_cp(("parallel",), 40),
    )(*_hbm(x, y_att, y_ssm, y_pool, mod3, w_out, lng, lnb))


def _mix_out_bwd(dxo, x, y, y_att, y_ssm, y_pool, mod3, w_out, l, lng, gw_out):
    S, D = x.shape
    ts = TOK_TILE
    nt = S // ts

    def body(dxo_ref, x_ref, y_ref, ya_ref, ys_ref, yp_ref, mod_ref, w_ref, lng_ref, _g,
             dxr_ref, da_ref, ds_ref, dp_ref, dgate_ref, dlng_ref, dlnb_ref, gw_ref, acc_sc):
        i = pl.program_id(0)

        @pl.when(i == 0)
        def _():
            dgate_ref[...] = jnp.zeros_like(dgate_ref)
            dlng_ref[...] = jnp.zeros_like(dlng_ref)
            dlnb_ref[...] = jnp.zeros_like(dlnb_ref)
            acc_sc[...] = jnp.zeros_like(acc_sc)

        gate = mod_ref[2:3, :]
        yv = y_ref[...]
        rh, rstd = _ln_stats(ALPHA * x_ref[...] + gate * yv)
        dy_out = dxo_ref[...]
        dlng_ref[...] += jnp.sum(dy_out * rh, 0, keepdims=True)
        dlnb_ref[...] += jnp.sum(dy_out, 0, keepdims=True)
        dr = _ln_bwd(dy_out * lng_ref[...], rh, rstd)
        dxr_ref[...] = ALPHA * dr
        dgate_ref[...] += jnp.sum(dr * yv, 0, keepdims=True)
        dy = (gate * dr).astype(bf16)
        da_ref[:, 0:256] = _dot_nt(dy, w_ref[0, 0])
        da_ref[:, 256:512] = _dot_nt(dy, w_ref[1, 0])
        ds_ref[...] = _dot_nt(dy, w_ref[2, 0])
        dp_ref[...] = _dot_nt(dy, w_ref[3, 0])
        ya = ya_ref[...].astype(bf16)
        acc_sc[0] += _dot_tn(ya[:, 0:256], dy)
        acc_sc[1] += _dot_tn(ya[:, 256:512], dy)
        acc_sc[2] += _dot_tn(ys_ref[...].astype(bf16), dy)
        acc_sc[3] += _dot_tn(yp_ref[...].astype(bf16), dy)

        @pl.when(i == nt - 1)
        def _():
            gw_ref[:, 0] = acc_sc[...].astype(bf16)

    tok = pl.BlockSpec((ts, D), lambda i: (i, 0))
    t512 = pl.BlockSpec((ts, D_ATT), lambda i: (i, 0))
    t256 = pl.BlockSpec((ts, 256), lambda i: (i, 0))
    wspec = pl.BlockSpec((N_CHIPS, 1, 256, D), lambda i: (0, l, 0, 0))
    return pl.pallas_call(
        body, name="mix_out_bwd", grid=(nt,),
        in_specs=[tok, tok, tok, t512, t256, t256, _full((3, D)), wspec, _full((1, D)), pl.BlockSpec(memory_space=pl.ANY)],
        out_specs=[tok, t512, t256, t256, _full((1, D)), _full((1, D)), _full((1, D)), wspec],
        out_shape=[jax.ShapeDtypeStruct((S, D), f32), jax.ShapeDtypeStruct((S, D_ATT), f32),
                   jax.ShapeDtypeStruct((S, D_SSM), f32), jax.ShapeDtypeStruct((S, D_POOL), f32),
                   jax.ShapeDtypeStruct((1, D), f32), jax.ShapeDtypeStruct((1, D), f32), jax.ShapeDtypeStruct((1, D), f32),
                   jax.ShapeDtypeStruct(gw_out.shape, bf16)],
        scratch_shapes=[pltpu.VMEM((N_CHIPS, 256, D), f32)],
        input_output_aliases={9: 7},
        compiler_params=_cp(("arbitrary",), 48),
    )(*_hbm(dxo, x, y, y_att, y_ssm, y_pool, mod3, w_out, lng, gw_out))


def _t5_bucket(dist):
    max_exact = N_BUCKETS // 2
    d = np.maximum(dist, 1).astype(np.float32)
    large = max_exact + (np.log(d / max_exact) / math.log(MAX_DISTANCE / max_exact)
                         * (N_BUCKETS - max_exact)).astype(np.int32)
    large = np.minimum(large, N_BUCKETS - 1)
    return np.where(dist < max_exact, dist, large).astype(np.int32)


def _bucket_table():
    q = ATT_BLOCK
    i = np.arange(q)[:, None]
    j = np.arange(2 * q)[None, :]
    r = i + q - j
    in_band = (r >= 0) & (r <= q)
    tabs = [np.where(in_band, _t5_bucket(np.clip(r, 0, None) * d), -1) for d in DILATIONS]
    return np.stack(tabs).astype(np.int32)


def _bias_fwd(rel_bias, table):
    def body(rb_ref, tab_ref, out_ref):
        for b in range(3):
            tb = tab_ref[b]
            for h in range(N_HEADS):
                def pick(k, acc):
                    return jnp.where(tb == k, rb_ref[k, h], acc)
                out_ref[b, h] = lax.fori_loop(0, N_BUCKETS, pick, jnp.where(tb < 0, NEG, 0.0).astype(f32))

    return pl.pallas_call(
        body, name="bias_fwd",
        in_specs=[pl.BlockSpec(memory_space=pltpu.SMEM), pl.BlockSpec(memory_space=pltpu.VMEM)],
        out_specs=pl.BlockSpec(memory_space=pltpu.VMEM),
        out_shape=jax.ShapeDtypeStruct((3, N_HEADS, ATT_BLOCK, 2 * ATT_BLOCK), f32),
    )(rel_bias, table)


def _bias_bwd(dbias, table):
    def body(db_ref, tab_ref, out_ref):
        def per_bucket(k, c):
            for h in range(N_HEADS):
                tot = jnp.zeros((), f32)
                for b in range(3):
                    tot = tot + jnp.sum(jnp.where(tab_ref[b] == k, db_ref[b, h], 0.0))
                out_ref[k, h] = tot
            return c
        lax.fori_loop(0, N_BUCKETS, per_bucket, 0)

    return pl.pallas_call(
        body, name="bias_bwd",
        in_specs=[pl.BlockSpec(memory_space=pltpu.VMEM), pl.BlockSpec(memory_space=pltpu.VMEM)],
        out_specs=pl.BlockSpec(memory_space=pltpu.SMEM),
        out_shape=jax.ShapeDtypeStruct((N_BUCKETS, N_HEADS), f32),
    )(dbias, table)


def _att_unit(u, nbr):
    rows = pl.ds(pl.multiple_of(u * ATT_BLOCK, ATT_BLOCK), ATT_BLOCK)
    prev = pl.ds(pl.multiple_of(jnp.maximum(u - 1, 0) * ATT_BLOCK, ATT_BLOCK), ATT_BLOCK)
    return rows, prev, (u % nbr) != 0


_N_PAIRS = N_HEADS // 2


def _pair_rows(t):
    lane = lax.broadcasted_iota(jnp.int32, t.shape, 1)
    zero = jnp.zeros_like(t)
    return jnp.concatenate([jnp.where(lane < HEAD_DIM, t, zero), jnp.where(lane >= HEAD_DIM, t, zero)], axis=0)


def _pair_cols(big):
    lane = lax.broadcasted_iota(jnp.int32, (ATT_BLOCK, _LANES), 1)
    return jnp.where(lane < HEAD_DIM, big[:ATT_BLOCK], big[ATT_BLOCK:])


def _pair_column(ref, rows, hp):
    t = ref[rows, :]
    return jnp.concatenate([t[:, 2 * hp:2 * hp + 1], t[:, 2 * hp + 1:2 * hp + 2]], axis=0)


def _pair_band(ref, rows, prev, nbr, hp):
    lanes = pl.ds(_LANES * hp, _LANES)
    cur = ref[0, rows, lanes]
    return cur if nbr == 1 else jnp.concatenate([ref[0, prev, lanes], cur], axis=0)


def _pair_scores(q_ref, k_ref, b_ref, rows, prev, valid_prev, nbr, hp):
    qbd = _pair_rows(q_ref[0, rows, pl.ds(_LANES * hp, _LANES)])
    kb = _pair_band(k_ref, rows, prev, nbr, hp)
    bias = b_ref[0, 2 * hp:2 * hp + 2].reshape(2 * ATT_BLOCK, 2 * ATT_BLOCK)
    if nbr == 1:
        return qbd, kb, _dot_nt(qbd, kb) + bias[:, ATT_BLOCK:]
    s = _dot_nt(qbd, kb) + bias
    col = lax.broadcasted_iota(jnp.int32, s.shape, 1)
    return qbd, kb, jnp.where((col >= ATT_BLOCK) | valid_prev, s, NEG)


def _qkv_specs(S, branch):
    return ([pl.BlockSpec((1, S, D_ATT), lambda i, t=t: (t, 0, 0)) for t in range(3)],
            pl.BlockSpec((1, N_HEADS, ATT_BLOCK, 2 * ATT_BLOCK), lambda i: (branch, 0, 0, 0)))


def _att_fwd(qkv, bias, branch):
    S = qkv.shape[1]
    nbr = BLOCKS_PER_RESIDUE[branch]

    def body(q_ref, k_ref, v_ref, b_ref, o_ref, lse_ref):
        lse_ref[...] = jnp.zeros_like(lse_ref)

        def unit(u, c):
            rows, prev, valid_prev = _att_unit(u, nbr)
            for hp in range(_N_PAIRS):
                _, _, s = _pair_scores(q_ref, k_ref, b_ref, rows, prev, valid_prev, nbr, hp)
                m = jnp.max(s, -1, keepdims=True)
                p = jnp.exp(s - m)
                den = jnp.sum(p, -1, keepdims=True)
                big = _dot(p.astype(bf16), _pair_band(v_ref, rows, prev, nbr, hp))
                o_ref[rows, pl.ds(_LANES * hp, _LANES)] = _pair_cols(big / den)
                lse = m + jnp.log(den)
                lse_ref[rows, pl.ds(2 * hp, 1)] = lse[:ATT_BLOCK]
                lse_ref[rows, pl.ds(2 * hp + 1, 1)] = lse[ATT_BLOCK:]
            return c

        lax.fori_loop(0, N_UNITS, unit, 0)

    qkv_specs, bspec = _qkv_specs(S, branch)
    return pl.pallas_call(
        body, name="att_fwd", grid=(1,),
        in_specs=qkv_specs + [bspec],
        out_specs=[pl.BlockSpec((S, D_ATT), lambda i: (0, 0)), pl.BlockSpec((S, _LANES), lambda i: (0, 0))],
        out_shape=[jax.ShapeDtypeStruct((S, D_ATT), f32), jax.ShapeDtypeStruct((S, _LANES), f32)],
        compiler_params=_cp(("arbitrary",), 40),
    )(*_hbm(qkv, qkv, qkv, bias))


def _att_bwd(qkv, do, lse, crow, bias, branch):
    S = qkv.shape[1]
    nbr = BLOCKS_PER_RESIDUE[branch]

    def body(q_ref, k_ref, v_ref, do_ref, lse_ref, c_ref, b_ref, dqkv_ref, db_ref, dk_sc, dv_sc):
        dk_sc[...] = jnp.zeros_like(dk_sc)
        dv_sc[...] = jnp.zeros_like(dv_sc)
        db_ref[...] = jnp.zeros_like(db_ref)

        def unit(u, c):
            rows, prev, valid_prev = _att_unit(u, nbr)
            for hp in range(_N_PAIRS):
                lanes = pl.ds(_LANES * hp, _LANES)
                qbd, kb, s = _pair_scores(q_ref, k_ref, b_ref, rows, prev, valid_prev, nbr, hp)
                p = jnp.exp(s - _pair_column(lse_ref, rows, hp))
                dobd = _pair_rows(do_ref[rows, lanes])
                ds = p * (_dot_nt(dobd, _pair_band(v_ref, rows, prev, nbr, hp)) - _pair_column(c_ref, rows, hp))
                if nbr == 1:
                    db_ref[2 * hp:2 * hp + 2, :, ATT_BLOCK:] += ds.reshape(2, ATT_BLOCK, ATT_BLOCK)
                else:
                    db_ref[2 * hp:2 * hp + 2] += ds.reshape(2, ATT_BLOCK, 2 * ATT_BLOCK)
                dsb = ds.astype(bf16)
                dqkv_ref[0, rows, lanes] = (HEAD_DIM ** -0.5 * _pair_cols(_dot(dsb, kb))).astype(bf16)
                dkb = _dot_tn(dsb, qbd)
                dvb = _dot_tn(p.astype(bf16), dobd)
                if nbr == 1:
                    dk_sc[rows, lanes] += dkb
                    dv_sc[rows, lanes] += dvb
                else:
                    dk_sc[prev, lanes] += dkb[:ATT_BLOCK]
                    dv_sc[prev, lanes] += dvb[:ATT_BLOCK]
                    dk_sc[rows, lanes] += dkb[ATT_BLOCK:]
                    dv_sc[rows, lanes] += dvb[ATT_BLOCK:]
            return c

        lax.fori_loop(0, N_UNITS, unit, 0)
        dqkv_ref[1] = dk_sc[...].astype(bf16)
        dqkv_ref[2] = dv_sc[...].astype(bf16)

    qkv_specs, bspec = _qkv_specs(S, branch)
    row = pl.BlockSpec((S, _LANES), lambda i: (0, 0))
    return pl.pallas_call(
        body, name="att_bwd", grid=(1,),
        in_specs=qkv_specs + [pl.BlockSpec((S, D_ATT), lambda i: (0, 0)), row, row, bspec],
        out_specs=[pl.BlockSpec((3, S, D_ATT), lambda i: (0, 0, 0)),
                   pl.BlockSpec((N_HEADS, ATT_BLOCK, 2 * ATT_BLOCK), lambda i: (0, 0, 0))],
        out_shape=[jax.ShapeDtypeStruct((3, S, D_ATT), bf16), jax.ShapeDtypeStruct((N_HEADS, ATT_BLOCK, 2 * ATT_BLOCK), f32)],
        scratch_shapes=[pltpu.VMEM((S, D_ATT), f32), pltpu.VMEM((S, D_ATT), f32)],
        compiler_params=_cp(("arbitrary",), 48),
    )(*_hbm(qkv, qkv, qkv, do, lse, crow, bias))


def _branch_weights(lse_ref):
    l0, l1, l2 = lse_ref[0], lse_ref[1], lse_ref[2]
    m = jnp.maximum(jnp.maximum(l0, l1), l2)
    e0, e1, e2 = jnp.exp(l0 - m), jnp.exp(l1 - m), jnp.exp(l2 - m)
    tot = e0 + e1 + e2
    return e0 / tot, e1 / tot, e2 / tot


def _att_merge(os, lses):
    S = os[0].shape[0] * os[0].shape[1]
    ts = TOK_TILE

    def body(o1_ref, o4_ref, o16_ref, l1_ref, l4_ref, l16_ref, y_ref, lt_ref, *bufs):
        obufs = (bufs[:_QKV_BLOCKS], bufs[_QKV_BLOCKS:2 * _QKV_BLOCKS])
        lt_ref[0] = l1_ref[0]
        for k, (d, o_ref, l_ref) in enumerate(((4, o4_ref, l4_ref), (16, o16_ref, l16_ref))):
            _residues_to_rows(obufs[k], d, lambda r, cb, o_ref=o_ref: o_ref[r, :, _LANES * cb:_LANES * (cb + 1)])
            _residues_to_rows([bufs[2 * _QKV_BLOCKS + k]], d, lambda r, cb, l_ref=l_ref: l_ref[r])
            lt_ref[1 + k] = bufs[2 * _QKV_BLOCKS + k][...]
        w = _branch_weights(lt_ref)
        for h in range(N_HEADS):
            cs = slice(HEAD_DIM * h, HEAD_DIM * (h + 1))
            half = slice(HEAD_DIM * (h % 2), HEAD_DIM * (h % 2 + 1))
            y_ref[:, cs] = (w[0][:, h:h + 1] * o1_ref[0, :, cs] + w[1][:, h:h + 1] * obufs[0][h // 2][:, half]
                            + w[2][:, h:h + 1] * obufs[1][h // 2][:, half])

    return pl.pallas_call(
        body, name="att_merge", grid=(S // ts,),
        in_specs=[_res_spec3(d, D_ATT) for d in DILATIONS] + [_res_spec3(d, _LANES) for d in DILATIONS],
        out_specs=[pl.BlockSpec((ts, D_ATT), lambda i: (i, 0)), pl.BlockSpec((3, ts, _LANES), lambda i: (0, i, 0))],
        out_shape=[jax.ShapeDtypeStruct((S, D_ATT), f32), jax.ShapeDtypeStruct((3, S, _LANES), f32)],
        scratch_shapes=[pltpu.VMEM((ts, _LANES), f32)] * (2 * _QKV_BLOCKS + 2),
        compiler_params=_cp(("parallel",)),
    )(*_hbm(*os, *lses))


def _att_merge_bwd(dy, y, lse3):
    S = dy.shape[0]
    ts = TOK_TILE

    def body(dy_ref, y_ref, lse_ref, do1_ref, do4_ref, do16_ref, c1_ref, c4_ref, c16_ref, *bufs):
        dobufs = (bufs[:_QKV_BLOCKS], bufs[_QKV_BLOCKS:2 * _QKV_BLOCKS], bufs[2 * _QKV_BLOCKS:3 * _QKV_BLOCKS])
        cbufs = bufs[3 * _QKV_BLOCKS:]
        w = _branch_weights(lse_ref)
        for cb in cbufs:
            cb[...] = jnp.zeros_like(cb)
        for h in range(N_HEADS):
            cs = slice(HEAD_DIM * h, HEAD_DIM * (h + 1))
            half = slice(HEAD_DIM * (h % 2), HEAD_DIM * (h % 2 + 1))
            dyh = dy_ref[:, cs]
            t = jnp.sum(dyh * y_ref[:, cs], -1, keepdims=True)
            for p in range(3):
                wp = w[p][:, h:h + 1]
                dobufs[p][h // 2][:, half] = wp * dyh
                cbufs[p][:, h:h + 1] = wp * t
        for cb in range(_QKV_BLOCKS):
            do1_ref[0, :, _LANES * cb:_LANES * (cb + 1)] = dobufs[0][cb][...].astype(bf16)
        c1_ref[0] = cbufs[0][...]
        for k, (d, do_ref, c_ref) in enumerate(((4, do4_ref, c4_ref), (16, do16_ref, c16_ref))):
            def put_do(r, cb, piece, do_ref=do_ref):
                do_ref[r, :, _LANES * cb:_LANES * (cb + 1)] = piece.astype(bf16)

            def put_c(r, cb, piece, c_ref=c_ref):
                c_ref[r] = piece

            _rows_to_residues(dobufs[1 + k], d, put_do)
            _rows_to_residues([cbufs[1 + k]], d, put_c)

    return pl.pallas_call(
        body, name="att_merge_bwd", grid=(S // ts,),
        in_specs=[pl.BlockSpec((ts, D_ATT), lambda i: (i, 0)), pl.BlockSpec((ts, D_ATT), lambda i: (i, 0)),
                  pl.BlockSpec((3, ts, _LANES), lambda i: (0, i, 0))],
        out_specs=[_res_spec3(d, D_ATT) for d in DILATIONS] + [_res_spec3(d, _LANES) for d in DILATIONS],
        out_shape=[jax.ShapeDtypeStruct((d, S // d, D_ATT), bf16) for d in DILATIONS]
        + [jax.ShapeDtypeStruct((d, S // d, _LANES), f32) for d in DILATIONS],
        scratch_shapes=[pltpu.VMEM((ts, _LANES), f32)] * (3 * _QKV_BLOCKS + 3),
        compiler_params=_cp(("parallel",)),
    )(*_hbm(dy, y, lse3))


_SSM_ROWS = 256


def _scan_in_place(sr_ref, si_ref, a_ref, reverse):
    S, N = sr_ref.shape
    nst = S // SCAN_SEG
    ar = jnp.broadcast_to(a_ref[0:1, :], (SCAN_SEG, N))
    ai = jnp.broadcast_to(a_ref[1:2, :], (SCAN_SEG, N))
    if reverse:
        ai = -ai
    row = lax.broadcasted_iota(jnp.int32, (SCAN_SEG, N), 0)
    zero = jnp.zeros((SCAN_SEG, N), f32)

    def tile(t):
        return pl.ds(pl.multiple_of((nst - 1 - t if reverse else t) * SCAN_SEG, SCAN_SEG), SCAN_SEG)

    def local(t, c):
        sr, si, pr, pi = c
        rows = tile(t)
        nsr = ar * sr - ai * si + sr_ref[rows, :]
        nsi = ar * si + ai * sr + si_ref[rows, :]
        sr_ref[rows, :] = nsr
        si_ref[rows, :] = nsi
        return nsr, nsi, ar * pr - ai * pi, ar * pi + ai * pr

    fr, fi, apr, api = lax.fori_loop(0, nst, local, (zero, zero, zero + 1.0, zero))

    def shift(v):
        if reverse:
            return jnp.where(row == SCAN_SEG - 1, 0.0, pltpu.roll(v, SCAN_SEG - 1, axis=0))
        return jnp.where(row == 0, 0.0, pltpu.roll(v, 1, axis=0))

    cr, ci = zero, zero
    for _ in range(SCAN_SEG - 1):
        cr, ci = shift(fr + apr * cr - api * ci), shift(fi + apr * ci + api * cr)

    def fix(t, c):
        pr, pi = c
        npr, npi = ar * pr - ai * pi, ar * pi + ai * pr
        rows = tile(t)
        sr_ref[rows, :] += npr * cr - npi * ci
        si_ref[rows, :] += npr * ci + npi * cr
        return npr, npi

    lax.fori_loop(0, nst, fix, (zero + 1.0, zero))


def _ssm_states(u, bre, bim, a2, l):
    S = u.shape[0]

    def body(u_ref, br_ref, bi_ref, a2_ref, sr_ref, si_ref):
        a_ref = a2_ref.at[l]
        brb = br_ref[l].astype(bf16)
        bib = bi_ref[l].astype(bf16)

        def project(t, c):
            rows = pl.ds(pl.multiple_of(t * _SSM_ROWS, _SSM_ROWS), _SSM_ROWS)
            ub = u_ref[rows, :].astype(bf16)
            sr_ref[rows, :] = _dot(ub, brb)
            si_ref[rows, :] = _dot(ub, bib)
            return c

        lax.fori_loop(0, S // _SSM_ROWS, project, 0)
        _scan_in_place(sr_ref, si_ref, a_ref, False)

    vm = pl.BlockSpec(memory_space=pltpu.VMEM)
    return pl.pallas_call(
        body, name="ssm_states", in_specs=[vm] * 4, out_specs=[vm, vm],
        out_shape=[jax.ShapeDtypeStruct((S, D_STATE), f32)] * 2,
        compiler_params=_cp(None, 48),
    )(u, bre, bim, a2)


def _ssm_out(sr, si, u, cre, cim, l, dskip, glu_w, glu_b):
    S = u.shape[0]
    ts = TOK_TILE

    def body(sr_ref, si_ref, u_ref, cr_ref, ci_ref, d_ref, w_ref, b_ref, out_ref, y_ref):
        y = (_dot(sr_ref[...].astype(bf16), cr_ref[0].astype(bf16))
             - _dot(si_ref[...].astype(bf16), ci_ref[0].astype(bf16)) + d_ref[...] * u_ref[...])
        y_ref[...] = y
        z = _dot(_gelu(y).astype(bf16), w_ref[...].astype(bf16)) + b_ref[...]
        out_ref[...] = y * jax.nn.sigmoid(z)

    st = pl.BlockSpec((ts, D_STATE), lambda i: (i, 0))
    ch = pl.BlockSpec((ts, D_SSM), lambda i: (i, 0))
    c_l = pl.BlockSpec((1, D_STATE, D_SSM), lambda i: (l, 0, 0))
    return pl.pallas_call(
        body, name="ssm_out", grid=(S // ts,),
        in_specs=[st, st, ch, c_l, c_l, _full((1, D_SSM)), _full((D_SSM, D_SSM)), _full((1, D_SSM))],
        out_specs=[ch, ch],
        out_shape=[jax.ShapeDtypeStruct((S, D_SSM), f32)] * 2,
        compiler_params=_cp(("parallel",)),
    )(*_hbm(sr, si, u, cre, cim, dskip, glu_w, glu_b))


def _ssm_out_bwd(dout, y, u, sr, si, dskip, glu_w, glu_b):
    S = u.shape[0]
    ts = TOK_TILE

    def body(do_ref, y_ref, u_ref, sr_ref, si_ref, d_ref, w_ref, b_ref,
             dy_ref, du_ref, dcr_ref, dci_ref, dd_ref, dgb_ref, dgw_ref):
        @pl.when(pl.program_id(0) == 0)
        def _():
            for r in (dcr_ref, dci_ref, dd_ref, dgb_ref, dgw_ref):
                r[...] = jnp.zeros_like(r)

        y = y_ref[...]
        dout = do_ref[...]
        wb = w_ref[...].astype(bf16)
        ge = _gelu(y).astype(bf16)
        sz = jax.nn.sigmoid(_dot(ge, wb) + b_ref[...])
        dz = dout * y * sz * (1.0 - sz)
        dzb = dz.astype(bf16)
        dgb_ref[...] += jnp.sum(dz, 0, keepdims=True)
        dgw_ref[...] += _dot_tn(ge, dzb)
        dy = dout * sz + _gelu_grad(y) * _dot_nt(dzb, wb)
        uv = u_ref[...]
        dd_ref[...] += jnp.sum(dy * uv, 0, keepdims=True)
        du_ref[...] = dy * d_ref[...]
        dy_ref[...] = dy
        dyb = dy.astype(bf16)
        dcr_ref[...] += _dot_tn(sr_ref[...].astype(bf16), dyb)
        dci_ref[...] -= _dot_tn(si_ref[...].astype(bf16), dyb)

    st = pl.BlockSpec((ts, D_STATE), lambda i: (i, 0))
    ch = pl.BlockSpec((ts, D_SSM), lambda i: (i, 0))
    c_full = _full((D_STATE, D_SSM))
    return pl.pallas_call(
        body, name="ssm_out_bwd", grid=(S // ts,),
        in_specs=[ch, ch, ch, st, st, _full((1, D_SSM)), _full((D_SSM, D_SSM)), _full((1, D_SSM))],
        out_specs=[ch, ch, c_full, c_full, _full((1, D_SSM)), _full((1, D_SSM)), _full((D_SSM, D_SSM))],
        out_shape=[jax.ShapeDtypeStruct((S, D_SSM), f32), jax.ShapeDtypeStruct((S, D_SSM), f32),
                   jax.ShapeDtypeStruct((D_STATE, D_SSM), f32), jax.ShapeDtypeStruct((D_STATE, D_SSM), f32),
                   jax.ShapeDtypeStruct((1, D_SSM), f32), jax.ShapeDtypeStruct((1, D_SSM), f32),
                   jax.ShapeDtypeStruct((D_SSM, D_SSM), f32)],
        compiler_params=_cp(("arbitrary",), 40),
    )(*_hbm(dout, y, u, sr, si, dskip, glu_w, glu_b))


def _ssm_states_bwd(dy, du_skip, u, sr, si, cre, cim, bre, bim, a2, l):
    S = u.shape[0]
    N = D_STATE
    nst = S // SCAN_SEG
    nproj = S // _SSM_ROWS

    def body(dy_ref, dus_ref, u_ref, sr_ref, si_ref, cr_ref, ci_ref, br_ref, bi_ref, a2_ref,
             du_ref, dbr_ref, dbi_ref, da_ref, lr_ref, li_ref):
        a_ref = a2_ref.at[l]
        crb = cr_ref[l].astype(bf16)
        cib = ci_ref[l].astype(bf16)

        def project(t, c):
            rows = pl.ds(pl.multiple_of(t * _SSM_ROWS, _SSM_ROWS), _SSM_ROWS)
            dyb = dy_ref[rows, :].astype(bf16)
            lr_ref[rows, :] = _dot_nt(dyb, crb)
            li_ref[rows, :] = -_dot_nt(dyb, cib)
            return c

        lax.fori_loop(0, nproj, project, 0)
        _scan_in_place(lr_ref, li_ref, a_ref, True)

        row = lax.broadcasted_iota(jnp.int32, (SCAN_SEG, N), 0)
        last = pl.ds((nst - 1) * SCAN_SEG, SCAN_SEG)
        pr = jnp.where(row == 0, 0.0, pltpu.roll(sr_ref[last, :], 1, axis=0))
        pi = jnp.where(row == 0, 0.0, pltpu.roll(si_ref[last, :], 1, axis=0))
        first = pl.ds(0, SCAN_SEG)
        acc_r = lr_ref[first, :] * pr + li_ref[first, :] * pi
        acc_i = li_ref[first, :] * pr - lr_ref[first, :] * pi

        def step(t, c):
            acc_r, acc_i = c
            rows = pl.ds(pl.multiple_of(t * SCAN_SEG, SCAN_SEG), SCAN_SEG)
            prev = pl.ds(pl.multiple_of((t - 1) * SCAN_SEG, SCAN_SEG), SCAN_SEG)
            lrv, liv, srv, siv = lr_ref[rows, :], li_ref[rows, :], sr_ref[prev, :], si_ref[prev, :]
            return acc_r + lrv * srv + liv * siv, acc_i + liv * srv - lrv * siv

        acc_r, acc_i = lax.fori_loop(1, nst, step, (acc_r, acc_i))
        da_ref[0:1, :] = jnp.sum(acc_r, 0, keepdims=True)
        da_ref[1:2, :] = jnp.sum(acc_i, 0, keepdims=True)

        brb = br_ref[l].astype(bf16)
        bib = bi_ref[l].astype(bf16)
        dbr_ref[...] = jnp.zeros_like(dbr_ref)
        dbi_ref[...] = jnp.zeros_like(dbi_ref)

        def back(t, c):
            rows = pl.ds(pl.multiple_of(t * _SSM_ROWS, _SSM_ROWS), _SSM_ROWS)
            lrb = lr_ref[rows, :].astype(bf16)
            lib = li_ref[rows, :].astype(bf16)
            du_ref[rows, :] = dus_ref[rows, :] + _dot_nt(lrb, brb) + _dot_nt(lib, bib)
            ub = u_ref[rows, :].astype(bf16)
            dbr_ref[...] += _dot_tn(ub, lrb)
            dbi_ref[...] += _dot_tn(ub, lib)
            return c

        lax.fori_loop(0, nproj, back, 0)

    vm = pl.BlockSpec(memory_space=pltpu.VMEM)
    return pl.pallas_call(
        body, name="ssm_states_bwd", in_specs=[vm] * 10, out_specs=[vm] * 4,
        out_shape=[jax.ShapeDtypeStruct((S, D_SSM), f32), jax.ShapeDtypeStruct((D_SSM, D_STATE), f32),
                   jax.ShapeDtypeStruct((D_SSM, D_STATE), f32), jax.ShapeDtypeStruct((2, D_STATE), f32)],
        scratch_shapes=[pltpu.VMEM((S, D_STATE), f32), pltpu.VMEM((S, D_STATE), f32)],
        compiler_params=_cp(None, 56),
    )(dy, du_skip, u, sr, si, cre, cim, bre, bim, a2)


_POOL_TILE = 256


def _window_sums(xt, back):
    n = xt.shape[0]
    out = []
    ws = xt
    for k in (1, 2, 4, 8):
        ws = ws + pltpu.roll(ws, k if back else n - k, axis=0)
        out.append(ws)
    return out


def _pool_count(r0, w):
    t = r0 + lax.broadcasted_iota(jnp.int32, (_POOL_TILE, POOL_GROUP), 0)
    return jnp.minimum(t + 1, w).astype(f32)


def _pool_fwd(u_pad, pool_w, pool_scale):
    S = u_pad.shape[0] - POOL_HALO
    nt = S // _POOL_TILE

    def body(u_ref, w_ref, sc_ref, y_ref):
        def tile(t, c):
            r0 = pl.multiple_of(t * _POOL_TILE, _POOL_TILE)
            for g, w in enumerate(POOL_WINDOWS):
                cs = pl.ds(POOL_GROUP * g, POOL_GROUP)
                xt = u_ref[pl.ds(r0, _POOL_TILE + POOL_HALO), cs]
                ws = _window_sums(xt, True)[g][POOL_HALO:, :]
                pooled = ws / _pool_count(r0, w) - xt[POOL_HALO:, :]
                y_ref[pl.ds(r0, _POOL_TILE), cs] = _dot(pooled.astype(bf16), w_ref[g].astype(bf16)) * sc_ref[:, cs]
            return c
        lax.fori_loop(0, nt, tile, 0)

    vm = pl.BlockSpec(memory_space=pltpu.VMEM)
    return pl.pallas_call(
        body, name="pool_fwd", in_specs=[vm, vm, vm], out_specs=vm,
        out_shape=jax.ShapeDtypeStruct((S, D_POOL), f32),
    )(u_pad, pool_w, pool_scale)


def _pool_bwd(dy_pad, u_pad, pool_w, pool_scale):
    S = u_pad.shape[0] - POOL_HALO
    nt = S // _POOL_TILE
    n = _POOL_TILE + POOL_HALO

    def body(dy_ref, u_ref, w_ref, sc_ref, du_ref, dw_ref, dsc_ref):
        dw_ref[...] = jnp.zeros_like(dw_ref)
        dsc_ref[...] = jnp.zeros_like(dsc_ref)

        def tile(t, c):
            r0 = pl.multiple_of(t * _POOL_TILE, _POOL_TILE)
            for g, w in enumerate(POOL_WINDOWS):
                cs = pl.ds(POOL_GROUP * g, POOL_GROUP)
                wb = w_ref[g].astype(bf16)
                xt = u_ref[pl.ds(r0, n), cs]
                pooled = (_window_sums(xt, True)[g][POOL_HALO:, :] / _pool_count(r0, w) - xt[POOL_HALO:, :]).astype(bf16)
                dy = dy_ref[pl.ds(r0, _POOL_TILE), cs]
                dsc_ref[:, cs] += jnp.sum(dy * _dot(pooled, wb), 0, keepdims=True)
                dw_ref[g] += _dot_tn(pooled, (dy * sc_ref[:, cs]).astype(bf16))
                dyh = (dy_ref[pl.ds(r0, n), cs] * sc_ref[:, cs]).astype(bf16)
                dpl = _dot_nt(dyh, wb)
                cnt = jnp.minimum(r0 + lax.broadcasted_iota(jnp.int32, (n, POOL_GROUP), 0) + 1, w).astype(f32)
                lead = _window_sums(dpl / cnt, False)[g]
                du_ref[pl.ds(r0, _POOL_TILE), cs] = lead[:_POOL_TILE, :] - dpl[:_POOL_TILE, :]
            return c
        lax.fori_loop(0, nt, tile, 0)

    vm = pl.BlockSpec(memory_space=pltpu.VMEM)
    return pl.pallas_call(
        body, name="pool_bwd", in_specs=[vm, vm, vm, vm], out_specs=[vm, vm, vm],
        out_shape=[jax.ShapeDtypeStruct((S, D_POOL), f32), jax.ShapeDtypeStruct((4, POOL_GROUP, POOL_GROUP), f32),
                   jax.ShapeDtypeStruct((1, D_POOL), f32)],
    )(dy_pad, u_pad, pool_w, pool_scale)


def _loss_head(y, target):
    S, D = y.shape
    ts = TOK_TILE

    def body(y_ref, t_ref, loss_ref, dy_ref):
        @pl.when(pl.program_id(0) == 0)
        def _():
            loss_ref[...] = jnp.zeros_like(loss_ref)

        d = y_ref[...] - t_ref[...]
        dy_ref[...] = d * (1.0 / D)
        loss_ref[...] += 0.5 * jnp.sum(jnp.sum(d * d, -1, keepdims=True) * (1.0 / D), 0, keepdims=True)

    tok = pl.BlockSpec((ts, D), lambda i: (i, 0))
    return pl.pallas_call(
        body, name="loss_head", grid=(S // ts,),
        in_specs=[tok, tok], out_specs=[_full((1, 1)), tok],
        out_shape=[jax.ShapeDtypeStruct((1, 1), f32), jax.ShapeDtypeStruct((S, D), f32)],
        compiler_params=_cp(("arbitrary",)),
    )(*_hbm(y, target))


_ADA_COLS = 768


def _ada_fwd(c_all, ada_w, ada_b_cols):
    L, D, N = ada_w.shape
    B = c_all.shape[0]

    def body(c_ref, w_ref, b_ref, out_ref):
        cv = c_ref[...]
        cond = (cv * jax.nn.sigmoid(cv)).astype(bf16)
        out_ref[0] = _dot(cond, w_ref[0].astype(bf16)) + b_ref[0]

    return pl.pallas_call(
        body, name="ada_fwd", grid=(L, N // _ADA_COLS),
        in_specs=[_full((B, D)), pl.BlockSpec((1, D, _ADA_COLS), lambda l, j: (l, 0, j)),
                  pl.BlockSpec((1, 1, _ADA_COLS), lambda l, j: (l, 0, j))],
        out_specs=pl.BlockSpec((1, B, _ADA_COLS), lambda l, j: (l, 0, j)),
        out_shape=jax.ShapeDtypeStruct((L, B, N), f32),
        compiler_params=_cp(("parallel", "parallel")),
    )(c_all, ada_w, ada_b_cols)


def _ada_wgrad(c_all_t, dmod_cols):
    D, B = c_all_t.shape
    L, _, N = dmod_cols.shape

    def body(ct_ref, dm_ref, out_ref):
        cv = ct_ref[...]
        cond = cv * jax.nn.sigmoid(cv)
        acc = cond[:, 0:1] * dm_ref[0, 0:1, :]
        for b in range(1, B):
            acc = acc + cond[:, b:b + 1] * dm_ref[0, b:b + 1, :]
        out_ref[0] = acc

    return pl.pallas_call(
        body, name="ada_wgrad", grid=(L, N // _ADA_COLS),
        in_specs=[_full((D, B)), pl.BlockSpec((1, B, _ADA_COLS), lambda l, j: (l, 0, j))],
        out_specs=pl.BlockSpec((1, D, _ADA_COLS), lambda l, j: (l, 0, j)),
        out_shape=jax.ShapeDtypeStruct((L, D, N), f32),
        compiler_params=_cp(("parallel", "parallel")),
    )(c_all_t, dmod_cols)


def _adam_math(w, g, m, v):
    m = ADAM_B1 * m + (1.0 - ADAM_B1) * g
    v = ADAM_B2 * v + (1.0 - ADAM_B2) * (g * g)
    m_hat = m / (1.0 - ADAM_B1 ** ADAM_STEP)
    v_hat = v / (1.0 - ADAM_B2 ** ADAM_STEP)
    delta = -ADAM_LR * (m_hat / (jnp.sqrt(v_hat) + ADAM_EPS) + ADAM_WD * w)
    return delta, m, v


def _adamw(w, m, v, g, row_tile, row0=0, outs=None):
    R, C = w.shape
    b0 = row0 // row_tile

    def body(w_ref, m_ref, v_ref, g_ref, _0, _1, _2, _3, g_out, d_out, m_out, v_out):
        gv = g_ref[...]
        delta, mn, vn = _adam_math(w_ref[...], gv, m_ref[...], v_ref[...])
        g_out[...] = gv
        d_out[...] = delta
        m_out[...] = mn
        v_out[...] = vn

    pspec = pl.BlockSpec((row_tile, C), lambda i: (b0 + i, 0))
    gspec = pl.BlockSpec((row_tile, C), lambda i: (i, 0))
    anyspec = pl.BlockSpec(memory_space=pl.ANY)
    shp = jax.ShapeDtypeStruct((R, C), f32)
    if outs is None:
        outs = [lax.empty((R, C), f32) for _ in range(4)]
    return pl.pallas_call(
        body, name="adamw", grid=(g.shape[0] // row_tile,),
        in_specs=[pspec] * 3 + [gspec] + [anyspec] * 4, out_specs=[pspec] * 4, out_shape=[shp] * 4,
        input_output_aliases={4: 0, 5: 1, 6: 2, 7: 3},
        compiler_params=_cp(("parallel",), 40),
    )(*_hbm(w, m, v, g, *outs))


def _pair_sum(g5s, gots, pc):
    n = len(g5s)

    def body(pc_ref, *refs):
        for own, got, out in zip(refs[:n], refs[n:2 * n], refs[2 * n:]):
            out[0, 0] = (own[0, 0, 0].astype(f32) + got[0, 0].astype(f32)).astype(bf16)

    def half(g):
        return pl.BlockSpec((1, 1) + g.shape[-2:], lambda p, pc: (p, 0, 0, 0))

    gs = pltpu.PrefetchScalarGridSpec(
        num_scalar_prefetch=1, grid=(N_CHIPS,),
        in_specs=[pl.BlockSpec((1, 1, 1) + g.shape[-2:], lambda p, pc: (p, 0, pc[1], 0, 0)) for g in g5s]
        + [half(g) for g in gots],
        out_specs=[half(g) for g in gots],
    )
    return pl.pallas_call(
        body, name="pair_sum", grid_spec=gs, out_shape=[jax.ShapeDtypeStruct(g.shape, bf16) for g in gots],
        compiler_params=_cp(("parallel",), 48),
    )(pc, *_hbm(*g5s, *gots))


_SUM_STEPS = 2


def _sum_shards(hsums, recvs, pc):
    n = len(hsums)

    def body(pc_ref, *refs):
        for own, got, out in zip(refs[:n], refs[n:2 * n], refs[2 * n:]):
            acc = own[0, 0].astype(f32)
            for j in range(3):
                acc = acc + got[j, 0].astype(f32)
            out[0, 0] = acc

    def rows(h):
        return (h.shape[2] // _SUM_STEPS, h.shape[3])

    gs = pltpu.PrefetchScalarGridSpec(
        num_scalar_prefetch=1, grid=(_SUM_STEPS,),
        in_specs=[pl.BlockSpec((1, 1) + rows(h), lambda i, pc: (pc[0], 0, i, 0)) for h in hsums]
        + [pl.BlockSpec((3, 1) + rows(h), lambda i, pc: (0, 0, i, 0)) for h in hsums],
        out_specs=[pl.BlockSpec((1, 1) + rows(h), lambda i, pc: (0, pc[1], i, 0)) for h in hsums],
    )
    return pl.pallas_call(
        body, name="sum_shards", grid_spec=gs,
        out_shape=[jax.ShapeDtypeStruct((1, 2) + h.shape[2:], f32) for h in hsums],
        compiler_params=_cp(("parallel",), 48),
    )(pc, *_hbm(*hsums, *recvs))


def _sum8(packs):
    _, R, C = packs.shape
    tr = R // 8 if R % 64 == 0 else R

    def body(p_ref, out_ref):
        acc = p_ref[0]
        for d in range(1, 8):
            acc = acc + p_ref[d]
        out_ref[...] = acc

    return pl.pallas_call(
        body, name="sum8", grid=(R // tr,),
        in_specs=[pl.BlockSpec((8, tr, C), lambda i: (0, i, 0))],
        out_specs=pl.BlockSpec((tr, C), lambda i: (i, 0)),
        out_shape=jax.ShapeDtypeStruct((R, C), f32),
        compiler_params=_cp(("parallel",)),
    )(packs)


def _allgather8(x_shard):
    m_per, n = x_shard.shape

    def body(x_ref, out_ref, send_sems, recv_sems, local_sem):
        x, y, c = lax.axis_index("x"), lax.axis_index("y"), lax.axis_index("c")
        me, sibling = (x, y, c), (x, y, 1 - c)
        chips = [(1 - x, y), (x, 1 - y), (1 - x, 1 - y)]

        def rows(px, py, pc):
            return out_ref.at[pl.ds((4 * px + 2 * py + pc) * m_per, m_per), :]

        def copy(k, block, to, src=None):
            return pltpu.make_async_remote_copy(
                src_ref=rows(*block) if src is None else src, dst_ref=rows(*block),
                send_sem=send_sems.at[k], recv_sem=recv_sems.at[k], device_id=to, device_id_type=MESH)

        mine = pltpu.make_async_copy(x_ref, rows(*me), local_sem)
        mine.start()
        first = [copy(0, me, sibling, src=x_ref)]
        first += [copy(1 + j, me, (*chip, c), src=x_ref) for j, chip in enumerate(chips)]
        for cp in first:
            cp.start()
        passed = [copy(4 + j, (*chip, c), sibling) for j, chip in enumerate(chips)]
        for j, chip in enumerate(chips):
            copy(1 + j, (*chip, c), me).wait_recv()
            passed[j].start()
        copy(0, sibling, me).wait_recv()
        for j, chip in enumerate(chips):
            copy(4 + j, (*chip, 1 - c), me).wait_recv()
        for cp in first + passed:
            cp.wait_send()
        mine.wait()

    return pl.pallas_call(
        body, name="allgather8",
        out_shape=jax.ShapeDtypeStruct((8 * m_per, n), x_shard.dtype),
        in_specs=[pl.BlockSpec(memory_space=pltpu.VMEM)],
        out_specs=pl.BlockSpec(memory_space=pltpu.VMEM),
        scratch_shapes=[pltpu.SemaphoreType.DMA((7,)), pltpu.SemaphoreType.DMA((7,)), pltpu.SemaphoreType.DMA],
        compiler_params=_cp(None, 48),
    )(x_shard)


def _other_chips():
    x, y = lax.axis_index("x"), lax.axis_index("y")
    return [(1 - x, y), (x, 1 - y), (1 - x, 1 - y)]


_HBM = pl.BlockSpec(memory_space=pltpu.HBM)
_SEM = pl.BlockSpec(memory_space=pltpu.SEMAPHORE)
_EFFECT = pltpu.SideEffectType.DATAFLOW_SIDE_EFFECTING


def _gather_copies(srcs, lands, send_sems, recv_sems):
    x, y, c = lax.axis_index("x"), lax.axis_index("y"), lax.axis_index("c")
    return [pltpu.make_async_remote_copy(
        src_ref=srcs[a].at[:, c], dst_ref=lands[a].at[2 * x + y, :, c], send_sem=send_sems.at[3 * a + j],
        recv_sem=recv_sems.at[3 * a + j], device_id=(cx, cy, c), device_id_type=MESH)
        for a in range(len(srcs)) for j, (cx, cy) in enumerate(_other_chips())]


def _gather_start(chunks, after, name):
    sizes = [len(srcs) for srcs, _ in chunks]
    flat = [t for srcs, lands in chunks for t in list(srcs) + list(lands)]
    nflat = len(flat)
    nsem = 2 * len(chunks)

    def body(*refs):
        ins, sems, token = refs[:nflat], refs[nflat + 1:nflat + 1 + nsem], refs[-1]
        off = 0
        for k, n in enumerate(sizes):
            for cp in _gather_copies(ins[off:off + n], ins[off + n:off + 2 * n], sems[2 * k], sems[2 * k + 1]):
                cp.start()
            off += 2 * n
        token[...] = jnp.zeros_like(token)

    res = pl.pallas_call(
        body, name=name,
        out_shape=[pltpu.SemaphoreType.DMA((3 * n,)) for n in sizes for _ in range(2)]
        + [pltpu.HBM(t.shape, t.dtype) for t in flat] + [jax.ShapeDtypeStruct((8, 128), f32)],
        in_specs=[_HBM] * nflat + [pl.BlockSpec(memory_space=pl.ANY)],
        out_specs=[_SEM] * nsem + [_HBM] * nflat + [pl.BlockSpec(memory_space=pltpu.VMEM)],
        input_output_aliases={i: nsem + i for i in range(nflat)},
        compiler_params=pltpu.CompilerParams(has_side_effects=_EFFECT),
    )(*[pltpu.with_memory_space_constraint(t, pltpu.HBM) for t in flat], after)
    out, off = [], nsem
    for k, n in enumerate(sizes):
        out.append((res[2 * k], res[2 * k + 1], res[off:off + n], res[off + n:off + 2 * n]))
        off += 2 * n
    return out, res[-1]


def _gather_wait(send_sems, recv_sems, srcs, lands, after, name):
    n = len(srcs)

    def body(*refs):
        for cp in _gather_copies(refs[:n], refs[n:2 * n], refs[2 * n], refs[2 * n + 1]):
            cp.wait_send()
            cp.wait_recv()

    res = pl.pallas_call(
        body, name=name,
        out_shape=[pltpu.HBM(t.shape, t.dtype) for t in list(srcs) + list(lands)],
        in_specs=[_HBM] * (2 * n) + [_SEM, _SEM] + [pl.BlockSpec(memory_space=pl.ANY)] * len(after),
        out_specs=[_HBM] * (2 * n),
        input_output_aliases={i: i for i in range(2 * n)},
        compiler_params=pltpu.CompilerParams(has_side_effects=_EFFECT),
    )(*srcs, *lands, send_sems, recv_sems, *after)
    return res[n:]


def _split_start(make_copies, arrays, nsem, name, after=()):
    n, na = len(arrays), len(after)

    def body(*refs):
        for cp in make_copies(refs[:n], refs[n + na], refs[n + na + 1]):
            cp.start()
        refs[-1][...] = jnp.zeros_like(refs[-1])

    res = pl.pallas_call(
        body, name=name,
        out_shape=[pltpu.SemaphoreType.DMA((nsem,)), pltpu.SemaphoreType.DMA((nsem,))]
        + [pltpu.HBM(t.shape, t.dtype) for t in arrays] + [jax.ShapeDtypeStruct((8, 128), f32)],
        in_specs=[_HBM] * n + [pl.BlockSpec(memory_space=pl.ANY)] * na,
        out_specs=[_SEM, _SEM] + [_HBM] * n + [pl.BlockSpec(memory_space=pltpu.VMEM)],
        input_output_aliases={i: i + 2 for i in range(n)},
        compiler_params=pltpu.CompilerParams(has_side_effects=_EFFECT),
    )(*[pltpu.with_memory_space_constraint(t, pltpu.HBM) for t in arrays], *after)
    return (res[0], res[1], res[2:2 + n]), res[-1]


def _split_wait(make_copies, send_sems, recv_sems, arrays, after, name):
    n = len(arrays)

    def body(*refs):
        for cp in make_copies(refs[:n], refs[n], refs[n + 1]):
            cp.wait_send()
            cp.wait_recv()

    return pl.pallas_call(
        body, name=name,
        out_shape=[pltpu.HBM(t.shape, t.dtype) for t in arrays],
        in_specs=[_HBM] * n + [_SEM, _SEM] + [pl.BlockSpec(memory_space=pl.ANY)] * len(after),
        out_specs=[_HBM] * n, input_output_aliases={i: i for i in range(n)},
        compiler_params=pltpu.CompilerParams(has_side_effects=_EFFECT),
    )(*arrays, send_sems, recv_sems, *after)


def _sibling():
    return lax.axis_index("x"), lax.axis_index("y"), 1 - lax.axis_index("c")


def _forward_copies(lands, send_sems, recv_sems):
    c = lax.axis_index("c")
    return [pltpu.make_async_remote_copy(
        src_ref=lands[a].at[2 * cx + cy, :, c], dst_ref=lands[a].at[2 * cx + cy, :, c], send_sem=send_sems.at[3 * a + j],
        recv_sem=recv_sems.at[3 * a + j], device_id=_sibling(), device_id_type=MESH)
        for a in range(len(lands)) for j, (cx, cy) in enumerate(_other_chips())]


def _swap_copies(fulls, send_sems, recv_sems):
    c = lax.axis_index("c")
    return [pltpu.make_async_remote_copy(src_ref=t.at[:, c], dst_ref=t.at[:, c], send_sem=send_sems.at[a],
                                         recv_sem=recv_sems.at[a], device_id=_sibling(), device_id_type=MESH)
            for a, t in enumerate(fulls)]


def _pair_copies(refs, send_sems, recv_sems):
    n = len(refs) // 2
    c = lax.axis_index("c")
    return [pltpu.make_async_remote_copy(src_ref=refs[a].at[:, :, 1 - c], dst_ref=refs[n + a], send_sem=send_sems.at[a],
                                         recv_sem=recv_sems.at[a], device_id=_sibling(), device_id_type=MESH)
            for a in range(n)]


def _scatter_copies(srcs, lands, send_sems, recv_sems):
    c = lax.axis_index("c")
    return [pltpu.make_async_remote_copy(
        src_ref=srcs[a].at[2 * cx + cy], dst_ref=lands[a].at[j], send_sem=send_sems.at[3 * a + j],
        recv_sem=recv_sems.at[3 * a + j], device_id=(cx, cy, c), device_id_type=MESH)
        for a in range(len(srcs)) for j, (cx, cy) in enumerate(_other_chips())]


def _scatter_start(hsums, name, after=()):
    n = len(hsums)
    na = len(after)

    def body(*refs):
        srcs, lands = refs[:n], refs[n:2 * n]
        send_sems, recv_sems = refs[2 * n + na], refs[2 * n + na + 1]
        for cp in _scatter_copies(srcs, lands, send_sems, recv_sems):
            cp.start()
        refs[-1][...] = jnp.zeros_like(refs[-1])

    lands = [lax.empty((3,) + g.shape[1:], g.dtype) for g in hsums]
    res = pl.pallas_call(
        body, name=name,
        out_shape=[pltpu.SemaphoreType.DMA((3 * n,)), pltpu.SemaphoreType.DMA((3 * n,))]
        + [pltpu.HBM(g.shape, g.dtype) for g in hsums] + [pltpu.HBM(g.shape, g.dtype) for g in lands]
        + [jax.ShapeDtypeStruct((8, 128), f32)],
        in_specs=[_HBM] * (2 * n) + [pl.BlockSpec(memory_space=pl.ANY)] * na,
        out_specs=[_SEM, _SEM] + [_HBM] * (2 * n) + [pl.BlockSpec(memory_space=pltpu.VMEM)],
        input_output_aliases={i: i + 2 for i in range(2 * n)},
        compiler_params=pltpu.CompilerParams(has_side_effects=_EFFECT),
    )(*[pltpu.with_memory_space_constraint(t, pltpu.HBM) for t in list(hsums) + lands], *after)
    return (res[0], res[1], res[2:2 + n], res[2 + n:2 + 2 * n]), res[-1]


def _scatter_wait(send_sems, recv_sems, srcs, lands, after, name):
    n = len(srcs)
    extra = list(after)

    def body(*refs):
        s_refs, l_refs = refs[:n], refs[n:2 * n]
        ss, rs = refs[2 * n], refs[2 * n + 1]
        for cp in _scatter_copies(s_refs, l_refs, ss, rs):
            cp.wait_send()
            cp.wait_recv()

    res = pl.pallas_call(
        body, name=name,
        out_shape=[pltpu.HBM(g.shape, g.dtype) for g in srcs] + [pltpu.HBM(g.shape, g.dtype) for g in lands],
        in_specs=[_HBM] * (2 * n) + [_SEM, _SEM] + [pl.BlockSpec(memory_space=pl.ANY)] * len(extra),
        out_specs=[_HBM] * (2 * n),
        input_output_aliases={i: i for i in range(2 * n)},
        compiler_params=pltpu.CompilerParams(has_side_effects=_EFFECT),
    )(*srcs, *lands, send_sems, recv_sems, *extra)
    return res[:n], res[n:]


def _plane_copies(src, land, send_sems, recv_sems):
    x, y, c = lax.axis_index("x"), lax.axis_index("y"), lax.axis_index("c")
    return [pltpu.make_async_remote_copy(src_ref=src, dst_ref=land.at[2 * x + y, c], send_sem=send_sems.at[j],
                                         recv_sem=recv_sems.at[j], device_id=(cx, cy, c), device_id_type=MESH)
            for j, (cx, cy) in enumerate(_other_chips())]


def _plane_start(pack, land, name):
    def body(src, lnd, send_sems, recv_sems, _s, _l, token):
        for cp in _plane_copies(src, lnd, send_sems, recv_sems):
            cp.start()
        token[...] = jnp.zeros_like(token)

    res = pl.pallas_call(
        body, name=name,
        out_shape=[pltpu.SemaphoreType.DMA((3,)), pltpu.SemaphoreType.DMA((3,)), pltpu.HBM(pack.shape, pack.dtype),
                   pltpu.HBM(land.shape, land.dtype), jax.ShapeDtypeStruct((8, 128), f32)],
        in_specs=[_HBM, _HBM], out_specs=[_SEM, _SEM, _HBM, _HBM, pl.BlockSpec(memory_space=pltpu.VMEM)],
        input_output_aliases={0: 2, 1: 3},
        compiler_params=pltpu.CompilerParams(has_side_effects=_EFFECT),
    )(pltpu.with_memory_space_constraint(pack, pltpu.HBM), pltpu.with_memory_space_constraint(land, pltpu.HBM))
    return res[:4], res[4]


def _plane_wait(send_sems, recv_sems, pack, land, after, name):
    def body(src, lnd, ss, rs, *_):
        for cp in _plane_copies(src, lnd, ss, rs):
            cp.wait_send()
            cp.wait_recv()

    return pl.pallas_call(
        body, name=name,
        out_shape=[pltpu.HBM(pack.shape, pack.dtype), pltpu.HBM(land.shape, land.dtype)],
        in_specs=[_HBM, _HBM, _SEM, _SEM] + [pl.BlockSpec(memory_space=pl.ANY)] * len(after),
        out_specs=[_HBM, _HBM], input_output_aliases={0: 0, 1: 1},
        compiler_params=pltpu.CompilerParams(has_side_effects=_EFFECT),
    )(pack, land, send_sems, recv_sems, *after)[1]


def _swap_halves(fulls):
    n = len(fulls)

    def body(*refs):
        ins, outs = refs[:n], refs[n:2 * n]
        send_sems, recv_sems = refs[2 * n:]
        c = lax.axis_index("c")
        sibling = (lax.axis_index("x"), lax.axis_index("y"), 1 - c)
        copies = []
        for a in range(n):
            cp = pltpu.make_async_remote_copy(src_ref=outs[a].at[:, c], dst_ref=outs[a].at[:, c], send_sem=send_sems.at[a],
                                              recv_sem=recv_sems.at[a], device_id=sibling, device_id_type=MESH)
            cp.start()
            copies.append(cp)
        for a, cp in enumerate(copies):
            cp.wait_send()
            theirs = outs[a].at[:, 1 - c]
            pltpu.make_async_remote_copy(src_ref=theirs, dst_ref=theirs, send_sem=send_sems.at[a], recv_sem=recv_sems.at[a],
                                         device_id=sibling, device_id_type=MESH).wait_recv()

    hbm = pl.BlockSpec(memory_space=pl.ANY)
    return pl.pallas_call(
        body, name="swap_halves",
        out_shape=[jax.ShapeDtypeStruct(p.shape, p.dtype) for p in fulls],
        in_specs=[hbm] * n, out_specs=[hbm] * n,
        input_output_aliases={a: a for a in range(n)},
        scratch_shapes=[pltpu.SemaphoreType.DMA((n,)), pltpu.SemaphoreType.DMA((n,))],
    )(*fulls)


def _to_segments(t):
    s, c = t.shape
    return t.reshape(SCAN_SEG, s // SCAN_SEG, c).transpose(1, 0, 2).reshape(s, c)


def _from_segments(t):
    s, c = t.shape
    return t.reshape(s // SCAN_SEG, SCAN_SEG, c).transpose(1, 0, 2).reshape(s, c)


def _ssm_operators(a_re, a_im, log_dt, b_re, b_im, c_re, c_im):
    lam = lax.complex(a_re, a_im)
    dt = jnp.exp(log_dt)[:, None]
    a_bar = jnp.exp(lam * dt)
    b_bar = ((a_bar - 1.0) / lam)[:, :, None] * lax.complex(b_re, b_im)
    eye = jnp.eye(N_GROUPS, dtype=f32)

    def embed_b(t):
        return (jnp.transpose(t, (0, 2, 1))[:, :, None, :] * eye[:, None, :, None]).reshape(D_SSM, D_STATE)

    def embed_c(t):
        return (jnp.transpose(t, (0, 2, 1))[:, :, None, :] * eye[:, None, :, None]).reshape(D_STATE, D_SSM)

    a2 = jnp.stack([a_bar.real.reshape(D_STATE), a_bar.imag.reshape(D_STATE)])
    return a2, embed_b(b_bar.real), embed_b(b_bar.imag), embed_c(c_re), embed_c(c_im)


def _local_step(x, target, mod, small, ffn_weights, mix_weights, grads_done, ffn_bwd_issued):
    table = jnp.asarray(_bucket_table())
    bias = small["att_bias"]
    L = DEPTH
    saved = []
    ssm_names = ("ssm_a_re", "ssm_a_im", "ssm_log_dt", "ssm_b_re", "ssm_b_im", "ssm_c_re", "ssm_c_im")
    (a2, bre, bim, cre, cim), ssm_ops_vjp = jax.vjp(jax.vmap(_ssm_operators), *[small[k] for k in ssm_names])
    for l in range(L):
        sv = {}
        m9 = mod[l]
        sv["x0"] = x
        sv["w0"] = ffn_weights(l, 0, x)
        x, sv["f0"], sv["g0"], sv["u0"], sv["h0"] = _ffn_fwd(x, m9[0:3], *sv["w0"], 0, small["ln_g"][l, 0:1], small["ln_b"][l, 0:1])
        sv["x1"] = x
        sv["w1"] = mix_weights(l, x)
        *qkv, z_rest, sv["h1"] = _mix_in_fwd(x, m9[3:6], sv["w1"][0], 0)
        S = x.shape[0]
        qkv = [t.reshape(3, S, D_ATT) for t in qkv]
        att = [_att_fwd(qkv[b], bias, b) for b in range(3)]
        y_att, lse3 = _att_merge([att[b][0].reshape(d, S // d, D_ATT) for b, d in enumerate(DILATIONS)],
                                 [att[b][1].reshape(d, S // d, _LANES) for b, d in enumerate(DILATIONS)])
        sv.update(qkv=qkv, lse=[a[1] for a in att], lse3=lse3, y_att=y_att)

        u_ssm = _to_segments(z_rest[:, :D_SSM])
        sr, si = _ssm_states(u_ssm, bre, bim, a2, l)
        dskip = small["ssm_d"][l][None, :]
        glu_b = small["glu_b"][l][None, :]
        out_seg, y_seg = _ssm_out(sr, si, u_ssm, cre, cim, l, dskip, small["glu_w"][l], glu_b)
        y_ssm = _from_segments(out_seg)
        sv.update(u_ssm=u_ssm, sr=sr, si=si, y_seg=y_seg, y_ssm=y_ssm)

        u_pool = jnp.concatenate([jnp.zeros((POOL_HALO, D_POOL), f32), z_rest[:, D_SSM:]])
        y_pool = _pool_fwd(u_pool, small["pool_w"][l], small["pool_scale"][l][None, :])
        sv.update(u_pool=u_pool, y_pool=y_pool)

        x, sv["ymix"] = _mix_out_fwd(x, y_att, y_ssm, y_pool, m9[3:6], sv["w1"][1], 0, small["ln_g"][l, 1:2], small["ln_b"][l, 1:2])
        sv["x2"] = x
        sv["w2"] = ffn_weights(l, 1, x)
        x, sv["f2"], sv["g2"], sv["u2"], sv["h2"] = _ffn_fwd(x, m9[6:9], *sv["w2"], 0, small["ln_g"][l, 2:3], small["ln_b"][l, 2:3])
        saved.append(sv)

    loss, dx = _loss_head(x, target)

    dmod = [None] * L
    dln_g = [None] * L
    dln_b = [None] * L
    sg = {k: [None] * L for k in ("ssm_d", "glu_w", "glu_b", "pool_w", "pool_scale")}
    d_ssm_ops = [[None] * L for _ in range(5)]
    dbias_tot = None
    order_after = jnp.zeros((), f32)
    for l in reversed(range(L)):
        sv = saved[l]
        m9 = mod[l] + order_after

        def fresh(like):
            return [lax.empty(t.shape, bf16) for t in like]

        dx, dg, du, a, df, dm2, dlg2, dlb2 = _ffn_bwd(dx, sv["x2"], sv["f2"], sv["g2"], sv["u2"], m9[6:9], *sv["w2"], 0,
                                                     small["ln_g"][l, 2:3])
        m9 = m9 + ffn_bwd_issued(l, 1, dx)
        g_ffn1 = _ffn_wgrad(sv["h2"], dg, du, a, df, *fresh(sv["w2"]), 0)
        dxr, d_att, d_ssm, d_pool, dgate1, dlg1, dlb1, g_w_out = _mix_out_bwd(
            dx, sv["x1"], sv["ymix"], sv["y_att"], sv["y_ssm"], sv["y_pool"], m9[3:6], sv["w1"][1], 0, small["ln_g"][l, 1:2],
            fresh(sv["w1"])[1])
        S = d_att.shape[0]
        merged = _att_merge_bwd(d_att, sv["y_att"], sv["lse3"])
        dqkv, dbias = [], []
        for b, d in enumerate(DILATIONS):
            dq_b, db_b = _att_bwd(sv["qkv"][b], merged[b].reshape(S, D_ATT), sv["lse"][b], merged[3 + b].reshape(S, _LANES), bias, b)
            dqkv.append(dq_b.reshape(3, d, S // d, D_ATT))
            dbias.append(db_b)
        dbias = jnp.stack(dbias)
        dbias_tot = dbias if dbias_tot is None else dbias_tot + dbias
        d_seg = _to_segments(d_ssm)
        dskip = small["ssm_d"][l][None, :]
        glu_b = small["glu_b"][l][None, :]
        dy_seg, du_skip, dcre, dcim, dd, dglu_b, dglu_w = _ssm_out_bwd(
            d_seg, sv["y_seg"], sv["u_ssm"], sv["sr"], sv["si"], dskip, small["glu_w"][l], glu_b)
        du_seg, dbre, dbim, da2 = _ssm_states_bwd(dy_seg, du_skip, sv["u_ssm"], sv["sr"], sv["si"], cre, cim, bre, bim, a2, l)
        for slot, t in zip(d_ssm_ops, (da2, dbre, dbim, dcre, dcim)):
            slot[l] = t
        sg["ssm_d"][l] = dd[0]
        sg["glu_b"][l] = dglu_b[0]
        sg["glu_w"][l] = dglu_w
        du_ssm = _from_segments(du_seg)
        dyp = jnp.concatenate([d_pool, jnp.zeros((POOL_HALO, D_POOL), f32)])
        du_pool, dpw, dps = _pool_bwd(dyp, sv["u_pool"], small["pool_w"][l], small["pool_scale"][l][None, :])
        sg["pool_w"][l] = dpw
        sg["pool_scale"][l] = dps[0]
        d_rest = jnp.concatenate([du_ssm, du_pool], axis=1).astype(bf16)
        dx, dm1, dz = _mix_in_bwd(dqkv, d_rest, dxr, sv["x1"], m9[3:6], sv["w1"][0], 0)
        g_w_in = _mix_in_wgrad(sv["h1"], dz, fresh(sv["w1"])[0], 0)
        ffn_names = ("ffn_w_gate", "ffn_w_up", "ffn_w_down")
        m9 = m9 + grads_done(l, 1, list(zip(ffn_names, [(2 * l + 1) * FF_SHARD] * 3, g_ffn1))
                             + [("w_in", l * D_MODEL, g_w_in), ("w_out", l * 256, g_w_out)])
        dm1 = jnp.concatenate([dm1[0:2], dgate1])
        dx, dg, du, a, df, dm0, dlg0, dlb0 = _ffn_bwd(dx, sv["x0"], sv["f0"], sv["g0"], sv["u0"], m9[0:3], *sv["w0"], 0,
                                                     small["ln_g"][l, 0:1])
        issued = ffn_bwd_issued(l, 0, dx)
        g_ffn0 = _ffn_wgrad(sv["h0"], dg, du, a, df, *fresh(sv["w0"]), 0)
        order_after = grads_done(l, 0, list(zip(ffn_names, [2 * l * FF_SHARD] * 3, g_ffn0))) + issued
        dmod[l] = jnp.concatenate([dm0 + issued, dm1, dm2])
        dln_g[l] = jnp.concatenate([dlg0, dlg1, dlg2])
        dln_b[l] = jnp.concatenate([dlb0, dlb1, dlb2])

    small_grads = {k: jnp.stack(v) for k, v in sg.items()}
    small_grads.update(zip(ssm_names, ssm_ops_vjp(tuple(jnp.stack(slot) for slot in d_ssm_ops))))
    small_grads["rel_bias"] = _bias_bwd(dbias_tot, table)
    small_grads["ln_g"] = jnp.stack(dln_g)
    small_grads["ln_b"] = jnp.stack(dln_b)
    return loss, dx, jnp.stack(dmod), small_grads


_TILE_ELEMS = 8 * 128


def _pack_rows(shapes):
    out, row = [], 0
    for s in shapes:
        nr = -(-int(np.prod(s)) // _TILE_ELEMS) * 8
        out.append((row, nr))
        row += nr
    return out


def _pack(arrs):
    parts = []
    for a in arrs:
        flat = a.reshape(-1).astype(f32)
        npad = -(-flat.shape[0] // _TILE_ELEMS) * _TILE_ELEMS
        parts.append(jnp.pad(flat, (0, npad - flat.shape[0])).reshape(npad // 128, 128))
    return jnp.concatenate(parts, axis=0)


def _unpack(buf, shapes):
    return [buf[row:row + nr].reshape(-1)[:int(np.prod(s))].reshape(s) for s, (row, nr) in zip(shapes, _pack_rows(shapes))]


_REPL = ("rel_bias", "ada_b", "ssm_a_re", "ssm_a_im", "ssm_log_dt", "ssm_b_re", "ssm_b_im", "ssm_c_re", "ssm_c_im",
         "ssm_d", "glu_b", "pool_w", "pool_scale")
_SMALL_SHARDED = ("ln_g", "ln_b", "glu_w")
_BIG = ("ffn_w_gate", "ffn_w_up", "ffn_w_down", "w_in", "w_out")
_ORDER = ("rel_bias", "ada_w", "ada_b", "ln_g", "ln_b", "ffn_w_gate", "ffn_w_up", "ffn_w_down", "w_in", "w_out",
          "ssm_a_re", "ssm_a_im", "ssm_log_dt", "ssm_b_re", "ssm_b_im", "ssm_c_re", "ssm_c_im", "ssm_d", "glu_w",
          "glu_b", "pool_w", "pool_scale")


def kernel(x, c, rel_bias, ada_w, ada_b, ln_g, ln_b, ffn_w_gate, ffn_w_up, ffn_w_down, w_in, w_out, ssm_a_re, ssm_a_im, ssm_log_dt, ssm_b_re, ssm_b_im, ssm_c_re, ssm_c_im, ssm_d, glu_w, glu_b, pool_w, pool_scale, loss_target, m_rel_bias, m_ada_w, m_ada_b, m_ln_g, m_ln_b, m_ffn_w_gate, m_ffn_w_up, m_ffn_w_down, m_w_in, m_w_out, m_ssm_a_re, m_ssm_a_im, m_ssm_log_dt, m_ssm_b_re, m_ssm_b_im, m_ssm_c_re, m_ssm_c_im, m_ssm_d, m_glu_w, m_glu_b, m_pool_w, m_pool_scale, v_rel_bias, v_ada_w, v_ada_b, v_ln_g, v_ln_b, v_ffn_w_gate, v_ffn_w_up, v_ffn_w_down, v_w_in, v_w_out, v_ssm_a_re, v_ssm_a_im, v_ssm_log_dt, v_ssm_b_re, v_ssm_b_im, v_ssm_c_re, v_ssm_c_im, v_ssm_d, v_glu_w, v_glu_b, v_pool_w, v_pool_scale):
    args = dict(locals())
    w = {k: args[k] for k in _ORDER}
    m = {k: args["m_" + k] for k in _ORDER}
    v = {k: args["v_" + k] for k in _ORDER}
    L, D = DEPTH, D_MODEL
    ax, ay, ac = lax.axis_index("x"), lax.axis_index("y"), lax.axis_index("c")
    p_me = 2 * ax + ay
    dev = 4 * ax + 2 * ay + ac

    transposed = ("ffn_w_gate", "ffn_w_up")
    for d in (w, m, v):
        for name in transposed:
            d[name] = jnp.swapaxes(d[name], 2, 3)

    def halves(t):
        return t.astype(bf16).reshape(1, 2, t.shape[0] // 2, t.shape[1])

    def landing(src):
        return lax.dynamic_update_slice(lax.empty((N_CHIPS,) + src.shape, bf16), src[None], (p_me, 0, 0, 0, 0))

    chunk_keys = [("ffn", 0, 0), ("mix", 0), ("ffn", 0, 1), ("ffn", 1, 0), ("mix", 1), ("ffn", 1, 1)]
    chunk_srcs = []
    for key in chunk_keys:
        if key[0] == "ffn":
            chunk_srcs.append([halves(w[name][key[1], key[2]]) for name in ("ffn_w_gate", "ffn_w_up", "ffn_w_down")])
        else:
            chunk_srcs.append([halves(w_in[key[1]]), halves(w_out[key[1]])])

    pack = _pack([c, ln_g, ln_b, glu_w])
    rows = pack.shape[0]
    allp = _allgather8(pack).reshape(8, rows, 128)
    chunks = [(srcs, [landing(t) for t in srcs]) for srcs in chunk_srcs]
    first_in_flight, first_begun = _gather_start(chunks[:1], allp, "gather_start_first")
    c_all = allp[:, :8].reshape(8, D) + first_begun[0, 0]
    by_chip = allp[0::2]

    fwd_rows = _pack_rows([c.shape, ln_g.shape, ln_b.shape, glu_w.shape])

    def sharded(part, shape, axis):
        row0, nrows = fwd_rows[part]
        t = by_chip[:, row0:row0 + nrows].reshape(N_CHIPS, -1)[:, :int(np.prod(shape))].reshape((N_CHIPS,) + shape)
        return jnp.concatenate([t[p] for p in range(N_CHIPS)], axis=axis)

    ln_g_full = sharded(1, ln_g.shape, 2)
    ln_b_full = sharded(2, ln_b.shape, 2)
    glu_w_full = sharded(3, glu_w.shape, 1)

    ncol = ada_w.shape[-1]
    ada_b_cols = lax.dynamic_slice_in_dim(ada_b, p_me * ncol, ncol, axis=1)[:, None, :]
    mod_part = _ada_fwd(c_all, ada_w, ada_b_cols)
    mrows = L * 8 * ncol // 128
    mod_pack = mod_part.reshape(mrows, 128)
    mod_land = lax.dynamic_update_slice(lax.empty((N_CHIPS, 2, mrows, 128), f32), mod_pack[None, None], (p_me, ac, 0, 0))
    mod_in_flight, _ = _plane_start(mod_pack, mod_land, "mod_start")
    att_bias = _bias_fwd(rel_bias, jnp.asarray(_bucket_table()))
    small_names = _REPL + _SMALL_SHARDED
    small_packs = [_pack([d[k] for k in small_names]) for d in (w, m, v)]
    mod_land = _plane_wait(*mod_in_flight, [att_bias] + small_packs + [t for _, lands in chunks[1:] for t in lands], "mod_wait")
    mod_all = _swap_halves([mod_land])[0].reshape(8, L, 8, ncol)
    mod_mine = lax.dynamic_index_in_dim(mod_all, dev, axis=2, keepdims=False)
    mod = jnp.concatenate([mod_mine[2 * p] for p in range(N_CHIPS)], axis=-1).reshape(L, 9, D)

    rest_in_flight, rest_begun = _gather_start(chunks[1:], mod, "gather_start_rest")
    in_flight = first_in_flight + rest_in_flight

    forwarding = {}

    def forward(k, after):
        lands = _gather_wait(*in_flight[k], [after, rest_begun], "gather_wait_%d" % k)
        forwarding[k], begun = _split_start(_forward_copies, lands, 3 * len(lands), "gather_forward_start_%d" % k)
        return begun

    def gathered(key, after):
        k = chunk_keys.index(key)
        order = [after]
        if k not in forwarding:
            order.append(forward(k, after))
        if 3 <= k + 1 < len(chunk_keys):
            order.append(forward(k + 1, after))
        lands = _split_wait(_forward_copies, *forwarding[k], order, "gather_forward_wait_%d" % k)
        return [t.reshape(N_CHIPS, 1, 2 * t.shape[3], t.shape[4]) for t in lands]

    pc = jnp.stack([p_me, ac]).astype(jnp.int32)
    groups = {}
    scattering = {}

    pairing = {}

    def start_pairs(tag, after=()):
        g5 = [g.reshape(g.shape[:2] + (2, g.shape[2] // 2, g.shape[3])) for _, _, g in groups[tag]]
        gots = [lax.empty(g.shape[:2] + g.shape[3:], bf16) for g in g5]
        pairing[tag], begun = _split_start(_pair_copies, g5 + gots, len(g5), "pair_exchange_start_%s" % tag, after)
        return begun

    def start_group(tag, after):
        arrays = _split_wait(_pair_copies, *pairing[tag], after, "pair_exchange_wait_%s" % tag)
        n = len(arrays) // 2
        hsum = _pair_sum(arrays[:n], arrays[n:], pc)
        scattering[tag], begun = _scatter_start(hsum, "scatter_start_%s" % tag)
        return begun

    def grads_done(l, s, grads):
        if l == 1:
            groups.setdefault("l1", []).extend(grads)
            return start_pairs("l1")[0, 0] if s == 0 else jnp.zeros((), f32)
        groups["l0a" if s == 1 else "l0b"] = grads
        return start_pairs("l0a")[0, 0] if s == 1 else jnp.zeros((), f32)

    def ffn_bwd_issued(l, s, dx):
        if l == 0:
            return start_group("l1" if s == 1 else "l0a", [dx])[0, 0]
        return jnp.zeros((), f32)

    small = {k: w[k] for k in _REPL if k != "ada_b"}
    small.update(ln_g=ln_g_full, ln_b=ln_b_full, glu_w=glu_w_full, att_bias=att_bias)
    loss_dev, grad_x, dmod, sgrads = _local_step(
        x[0], loss_target[0], mod, small, lambda l, s, after: gathered(("ffn", l, s), after),
        lambda l, after: gathered(("mix", l), after), grads_done, ffn_bwd_issued)
    loss = lax.psum(loss_dev[0, 0], ("x", "y", "c"))

    names = ("rel_bias", "ln_g", "ln_b", "ssm_a_re", "ssm_a_im", "ssm_log_dt", "ssm_b_re", "ssm_b_im", "ssm_c_re",
             "ssm_c_im", "ssm_d", "glu_w", "glu_b", "pool_w", "pool_scale")
    gpack = _pack([dmod] + [sgrads[k] for k in names])
    grows = gpack.shape[0]
    land = lax.dynamic_update_slice(lax.empty((N_CHIPS, 2, grows, 128), f32), gpack[None, None], (p_me, ac, 0, 0))
    small_in_flight, small_begun = _plane_start(gpack, land, "small_grads_start")
    l0b_begun = start_group("l0b", [start_pairs("l0b", (small_begun,))])

    out_g, out_d, out_m, out_v = {}, {}, {}, {}
    row_tile = dict(zip(_BIG, (352, 352, 352, 256, 256)))
    big = {name: None for name in _BIG}

    swapping = {}

    def reduce_group(tag, after):
        hsum, recv = _scatter_wait(*scattering[tag], after, "scatter_wait_%s" % tag)
        full = _sum_shards(hsum, recv, pc)
        swapping[tag], begun = _split_start(_swap_copies, full, len(full), "swap_halves_start_%s" % tag)
        return [begun]

    def update_group(tag, after):
        full = _split_wait(_swap_copies, *swapping[tag], after, "swap_halves_wait_%s" % tag)
        for (name, row0, _), g in zip(groups[tag], full):
            shp = w[name].shape
            r2 = (int(np.prod(shp[:-1])), shp[-1])
            big[name] = _adamw(w[name].reshape(r2), m[name].reshape(r2), v[name].reshape(r2), g.reshape(-1, shp[-1]),
                               row_tile[name], row0, big[name])
        return [big[name][1] for name, _, _ in groups[tag]]

    after = reduce_group("l0a", reduce_group("l1", [grad_x, l0b_begun]))
    after = update_group("l0a", update_group("l1", after))

    land = _plane_wait(*small_in_flight, after, "small_grads_wait")
    gall = _swap_halves([land])[0].reshape(8, grows, 128)
    gsum = _unpack(_sum8(gall), [(L, 9 * D)] + [sgrads[k].shape for k in names])
    red = dict(zip(("ada_b",) + names, gsum))
    red["ln_g"] = lax.dynamic_slice_in_dim(red["ln_g"], p_me * 256, 256, axis=2)
    red["ln_b"] = lax.dynamic_slice_in_dim(red["ln_b"], p_me * 256, 256, axis=2)
    red["glu_w"] = lax.dynamic_slice_in_dim(red["glu_w"], p_me * 64, 64, axis=1)

    dmod_all = gall[:, :L * 9 * D // 128].reshape(8, L, 9 * D)
    dmod_cols = jnp.transpose(lax.dynamic_slice_in_dim(dmod_all, p_me * ncol, ncol, axis=2), (1, 0, 2))
    g_ada_w = _ada_wgrad(jnp.transpose(c_all), dmod_cols)

    r2 = (L * D, ncol)
    res = _adamw(ada_w.reshape(r2), m["ada_w"].reshape(r2), v["ada_w"].reshape(r2), g_ada_w.reshape(r2), 128)
    out_g["ada_w"], out_d["ada_w"], out_m["ada_w"], out_v["ada_w"] = [t.reshape(ada_w.shape) for t in res]

    res_small = _adamw(*small_packs, _pack([red[k] for k in small_names]), small_packs[0].shape[0])
    for t, dst in zip(res_small, (out_g, out_d, out_m, out_v)):
        for k, a in zip(small_names, _unpack(t, [w[k].shape for k in small_names])):
            dst[k] = a

    update_group("l0b", reduce_group("l0b", [res_small[1], res[1]]))
    for name in _BIG:
        res = [t.reshape(w[name].shape) for t in big[name]]
        out_g[name], out_d[name], out_m[name], out_v[name] = [jnp.swapaxes(t, 2, 3) for t in res] if name in transposed else res

    return (loss, grad_x[None], *[out_g[k] for k in _ORDER], *[out_d[k] for k in _ORDER],
            *[out_m[k] for k in _ORDER], *[out_v[k] for k in _ORDER])
```

```python
import math

import numpy as np
import jax
import jax.numpy as jnp
from jax import lax
from jax.experimental import pallas as pl
from jax.experimental.pallas import tpu as pltpu

f32 = jnp.float32
bf16 = jnp.bfloat16
MESH = pl.DeviceIdType.MESH

D_MODEL = 1024
SEQ = 2048
DEPTH = 2
HEAD_DIM = 64
N_HEADS = 8
D_ATT = 512
DILATIONS = (1, 4, 16)
BLOCKS_PER_RESIDUE = (16, 4, 1)
ATT_BLOCK = 128
N_UNITS = SEQ // ATT_BLOCK
N_GROUPS = 16
SSM_STATE = 64
D_SSM = 256
D_STATE = N_GROUPS * SSM_STATE
POOL_WINDOWS = (2, 4, 8, 16)
POOL_GROUP = 64
D_POOL = 256
POOL_HALO = 16
D_FF = 2816
N_BUCKETS = 32
MAX_DISTANCE = 2048
ALPHA = (2 * DEPTH) ** 0.25
FFN_RES = 0.5
LN_EPS = 1e-5
NEG = -1e30
N_CHIPS = 4
FF_SHARD = D_FF // N_CHIPS
SCAN_SEG = 8

ADAM_LR, ADAM_B1, ADAM_B2, ADAM_EPS, ADAM_WD, ADAM_STEP = 0.001, 0.9, 0.999, 1e-08, 0.01, 10

TOK_TILE = 512


def _cp(dims=None, vmem_mb=None):
    kw = {}
    if dims is not None:
        kw["dimension_semantics"] = dims
    if vmem_mb is not None:
        kw["vmem_limit_bytes"] = vmem_mb << 20
    return pltpu.CompilerParams(**kw)


def _dot(a, b):
    return jnp.dot(a, b, preferred_element_type=f32)


def _dot_nt(a, b):
    return lax.dot_general(a, b, (((1,), (1,)), ((), ())), preferred_element_type=f32)


def _dot_tn(a, b):
    return lax.dot_general(a, b, (((0,), (0,)), ((), ())), preferred_element_type=f32)


def _ln_stats(v):
    mu = jnp.mean(v, -1, keepdims=True)
    d = v - mu
    var = jnp.mean(d * d, -1, keepdims=True)
    rstd = lax.rsqrt(var + LN_EPS)
    return d * rstd, rstd


def _ln_bwd(dxh, xh, rstd):
    return rstd * (dxh - jnp.mean(dxh, -1, keepdims=True) - xh * jnp.mean(dxh * xh, -1, keepdims=True))


_GELU_C = math.sqrt(2.0 / math.pi)


def _gelu(y):
    return 0.5 * y * (1.0 + jnp.tanh(_GELU_C * (y + 0.044715 * y * y * y)))


def _gelu_grad(y):
    t = jnp.tanh(_GELU_C * (y + 0.044715 * y * y * y))
    return 0.5 * (1.0 + t) + 0.5 * y * (1.0 - t * t) * (_GELU_C * (1.0 + 3 * 0.044715 * y * y))


def _full(shape):
    return pl.BlockSpec(shape, lambda *_: (0,) * len(shape))


def _hbm(*args):
    return [pltpu.with_memory_space_constraint(a, pltpu.HBM) if getattr(a, "ndim", 0) >= 2 else a for a in args]


def _ffn_fwd(x, mod3, wg, wu, wd, ls, lng, lnb):
    S, D = x.shape
    Fs = wg.shape[-2]
    ts = TOK_TILE

    def body(x_ref, mod_ref, wg_ref, wu_ref, wd_ref, lng_ref, lnb_ref, xo_ref, f_ref, g_ref, u_ref, h_ref, acc_sc):
        j = pl.program_id(1)

        @pl.when(j == 0)
        def _():
            xh, _ = _ln_stats(x_ref[...])
            h_ref[...] = (xh * (1.0 + mod_ref[1:2, :]) + mod_ref[0:1, :]).astype(bf16)
            acc_sc[...] = jnp.zeros_like(acc_sc)

        h = h_ref[...]
        g = _dot_nt(h, wg_ref[0, 0])
        u = _dot_nt(h, wu_ref[0, 0])
        g_ref[0] = g.astype(bf16)
        u_ref[0] = u.astype(bf16)
        a = (g * jax.nn.sigmoid(g) * u).astype(bf16)
        acc_sc[...] += _dot(a, wd_ref[0, 0])

        @pl.when(j == N_CHIPS - 1)
        def _():
            f = acc_sc[...]
            f_ref[...] = f
            r = ALPHA * x_ref[...] + (FFN_RES * mod_ref[2:3, :]) * f
            rh, _ = _ln_stats(r)
            xo_ref[...] = rh * lng_ref[...] + lnb_ref[...]

    tok = pl.BlockSpec((ts, D), lambda i, j: (i, 0))
    wrow = pl.BlockSpec((1, 1, Fs, D), lambda i, j: (j, ls, 0, 0))
    hid = pl.BlockSpec((1, ts, Fs), lambda i, j: (j, i, 0))
    return pl.pallas_call(
        body, name="ffn_fwd", grid=(S // ts, N_CHIPS),
        in_specs=[tok, _full((3, D)), wrow, wrow, wrow, _full((1, D)), _full((1, D))],
        out_specs=[tok, tok, hid, hid, tok],
        out_shape=[jax.ShapeDtypeStruct((S, D), f32), jax.ShapeDtypeStruct((S, D), f32),
                   jax.ShapeDtypeStruct((N_CHIPS, S, Fs), bf16), jax.ShapeDtypeStruct((N_CHIPS, S, Fs), bf16),
                   jax.ShapeDtypeStruct((S, D), bf16)],
        scratch_shapes=[pltpu.VMEM((ts, D), f32)],
        compiler_params=_cp(("parallel", "arbitrary"), 56),
    )(*_hbm(x, mod3, wg, wu, wd, lng, lnb))


def _ffn_bwd(dxo, x, f, g, u, mod3, wg, wu, wd, ls, lng):
    S, D = x.shape
    Fs = wg.shape[-2]
    ts = TOK_TILE

    def body(dxo_ref, x_ref, f_ref, g_ref, u_ref, mod_ref, wg_ref, wu_ref, wd_ref, lng_ref,
             dx_ref, dg_ref, du_ref, a_ref, df_ref, dmod_ref, dlng_ref, dlnb_ref,
             dr_sc, df_sc, acc_sc):
        i = pl.program_id(0)
        j = pl.program_id(1)

        @pl.when((i == 0) & (j == 0))
        def _():
            dmod_ref[...] = jnp.zeros_like(dmod_ref)
            dlng_ref[...] = jnp.zeros_like(dlng_ref)
            dlnb_ref[...] = jnp.zeros_like(dlnb_ref)

        @pl.when(j == 0)
        def _():
            xv = x_ref[...]
            fv = f_ref[...]
            gate = mod_ref[2:3, :]
            rh, rstd = _ln_stats(ALPHA * xv + (FFN_RES * gate) * fv)
            dy = dxo_ref[...]
            dlng_ref[...] += jnp.sum(dy * rh, 0, keepdims=True)
            dlnb_ref[...] += jnp.sum(dy, 0, keepdims=True)
            dr = _ln_bwd(dy * lng_ref[...], rh, rstd)
            dr_sc[...] = dr
            dmod_ref[2:3, :] += jnp.sum(FFN_RES * dr * fv, 0, keepdims=True)
            df = ((FFN_RES * gate) * dr).astype(bf16)
            df_sc[...] = df
            df_ref[...] = df
            acc_sc[...] = jnp.zeros_like(acc_sc)

        da = _dot_nt(df_sc[...], wd_ref[0, 0])
        gv = g_ref[0].astype(f32)
        uv = u_ref[0].astype(f32)
        sg = jax.nn.sigmoid(gv)
        si = gv * sg
        a_ref[0] = (si * uv).astype(bf16)
        dgv = (da * uv * (sg * (1.0 + gv * (1.0 - sg)))).astype(bf16)
        duv = (da * si).astype(bf16)
        dg_ref[0] = dgv
        du_ref[0] = duv
        acc_sc[...] += _dot(dgv, wg_ref[0, 0]) + _dot(duv, wu_ref[0, 0])

        @pl.when(j == N_CHIPS - 1)
        def _():
            dh = acc_sc[...]
            xh, rstd0 = _ln_stats(x_ref[...])
            dmod_ref[0:1, :] += jnp.sum(dh, 0, keepdims=True)
            dmod_ref[1:2, :] += jnp.sum(dh * xh, 0, keepdims=True)
            dx_ref[...] = _ln_bwd(dh * (1.0 + mod_ref[1:2, :]), xh, rstd0) + ALPHA * dr_sc[...]

    tok = pl.BlockSpec((ts, D), lambda i, j: (i, 0))
    wrow = pl.BlockSpec((1, 1, Fs, D), lambda i, j: (j, ls, 0, 0))
    hid = pl.BlockSpec((1, ts, Fs), lambda i, j: (j, i, 0))
    hid_shape = jax.ShapeDtypeStruct((N_CHIPS, S, Fs), bf16)
    return pl.pallas_call(
        body, name="ffn_bwd", grid=(S // ts, N_CHIPS),
        in_specs=[tok, tok, tok, hid, hid, _full((3, D)), wrow, wrow, wrow, _full((1, D))],
        out_specs=[tok, hid, hid, hid, tok, _full((3, D)), _full((1, D)), _full((1, D))],
        out_shape=[jax.ShapeDtypeStruct((S, D), f32), hid_shape, hid_shape, hid_shape,
                   jax.ShapeDtypeStruct((S, D), bf16),
                   jax.ShapeDtypeStruct((3, D), f32), jax.ShapeDtypeStruct((1, D), f32), jax.ShapeDtypeStruct((1, D), f32)],
        scratch_shapes=[pltpu.VMEM((ts, D), f32), pltpu.VMEM((ts, D), bf16), pltpu.VMEM((ts, D), f32)],
        compiler_params=_cp(("arbitrary", "arbitrary"), 56),
    )(*_hbm(dxo, x, f, g, u, mod3, wg, wu, wd, lng))


def _ffn_wgrad(h, dg, du, a, df, gwg, gwu, gwd, ls):
    S, D = h.shape
    Fs = dg.shape[-1]
    tk = 2 * TOK_TILE
    nk = S // tk

    def body(h_ref, dg_ref, du_ref, a_ref, df_ref, _g0, _g1, _g2, gwg_ref, gwu_ref, gwd_ref, ag_sc, au_sc, ad_sc):
        k = pl.program_id(1)

        @pl.when(k == 0)
        def _():
            ag_sc[...] = jnp.zeros_like(ag_sc)
            au_sc[...] = jnp.zeros_like(au_sc)
            ad_sc[...] = jnp.zeros_like(ad_sc)

        hv = h_ref[...]
        ag_sc[...] += _dot_tn(dg_ref[0], hv)
        au_sc[...] += _dot_tn(du_ref[0], hv)
        ad_sc[...] += _dot_tn(a_ref[0], df_ref[...])

        @pl.when(k == nk - 1)
        def _():
            gwg_ref[0, 0] = ag_sc[...].astype(bf16)
            gwu_ref[0, 0] = au_sc[...].astype(bf16)
            gwd_ref[0, 0] = ad_sc[...].astype(bf16)

    tok = pl.BlockSpec((tk, D), lambda p, k: (k, 0))
    hid = pl.BlockSpec((1, tk, Fs), lambda p, k: (p, k, 0))
    anyspec = pl.BlockSpec(memory_space=pl.ANY)
    orow = pl.BlockSpec((1, 1, Fs, D), lambda p, k: (p, ls, 0, 0))
    return pl.pallas_call(
        body, name="ffn_wgrad", grid=(N_CHIPS, nk),
        in_specs=[tok, hid, hid, hid, tok, anyspec, anyspec, anyspec],
        out_specs=[orow, orow, orow],
        out_shape=[jax.ShapeDtypeStruct(gwg.shape, bf16), jax.ShapeDtypeStruct(gwu.shape, bf16),
                   jax.ShapeDtypeStruct(gwd.shape, bf16)],
        scratch_shapes=[pltpu.VMEM((Fs, D), f32), pltpu.VMEM((Fs, D), f32), pltpu.VMEM((Fs, D), f32)],
        input_output_aliases={5: 0, 6: 1, 7: 2},
        compiler_params=_cp(("parallel", "arbitrary"), 48),
    )(*_hbm(h, dg, du, a, df, gwg, gwu, gwd))


_LANES = 128
_QKV_BLOCKS = D_ATT // _LANES


def _res_spec(lead, d, width, index):
    return pl.BlockSpec((lead, d, TOK_TILE // d, width), index)


def _res_spec3(d, width):
    return pl.BlockSpec((d, TOK_TILE // d, width), lambda i: (0, i, 0))


def _rows_to_residues(tile_bufs, d, put):
    for r in range(d):
        for cb, buf in enumerate(tile_bufs):
            put(r, cb, buf[pl.ds(r, TOK_TILE // d, stride=d), :])


def _residues_to_rows(tile_bufs, d, get):
    for r in range(d):
        for cb, buf in enumerate(tile_bufs):
            buf[pl.ds(r, TOK_TILE // d, stride=d), :] = get(r, cb)


def _mix_in_fwd(x, mod3, w_in, l):
    S, D = x.shape
    N = w_in.shape[-1]
    ts = TOK_TILE

    def body(x_ref, mod_ref, w_ref, o1_ref, o4_ref, o16_ref, zr_ref, h_ref, *bufs):
        j = pl.program_id(1)

        @pl.when(j == 0)
        def _():
            xh, _ = _ln_stats(x_ref[...])
            h_ref[...] = (xh * (1.0 + mod_ref[1:2, :]) + mod_ref[0:1, :]).astype(bf16)

        z = _dot(h_ref[...], w_ref[0, 0])

        @pl.when(j == N_CHIPS - 1)
        def _():
            zr_ref[...] = z

        @pl.when(j < N_CHIPS - 1)
        def _():
            zz = z * jnp.where(j == 0, HEAD_DIM ** -0.5, 1.0)
            o1_ref[j, 0] = zz.astype(bf16)
            for cb, buf in enumerate(bufs):
                buf[...] = zz[:, _LANES * cb:_LANES * (cb + 1)]
            for d, o_ref in zip(DILATIONS[1:], (o4_ref, o16_ref)):
                def put(r, cb, piece, o_ref=o_ref):
                    o_ref[j, r, :, _LANES * cb:_LANES * (cb + 1)] = piece.astype(bf16)
                _rows_to_residues(bufs, d, put)

    tok = pl.BlockSpec((ts, D), lambda i, j: (i, 0))
    res = [_res_spec(3, d, N, lambda i, j: (0, 0, i, 0)) for d in DILATIONS]
    return pl.pallas_call(
        body, name="mix_in_fwd", grid=(S // ts, N_CHIPS),
        in_specs=[tok, _full((3, D)), pl.BlockSpec((1, 1, D, N), lambda i, j: (j, l, 0, 0))],
        out_specs=res + [pl.BlockSpec((ts, N), lambda i, j: (i, 0)), tok],
        out_shape=[jax.ShapeDtypeStruct((3, d, S // d, N), bf16) for d in DILATIONS]
        + [jax.ShapeDtypeStruct((S, N), f32), jax.ShapeDtypeStruct((S, D), bf16)],
        scratch_shapes=[pltpu.VMEM((ts, _LANES), f32)] * _QKV_BLOCKS,
        compiler_params=_cp(("parallel", "arbitrary"), 40),
    )(*_hbm(x, mod3, w_in))


def _mix_in_bwd(dqkv, d_rest, dx_res, x, mod3, w_in, l):
    S, D = x.shape
    N = w_in.shape[-1]
    ts = TOK_TILE

    def body(d1_ref, d4_ref, d16_ref, dr_ref, dxr_ref, x_ref, mod_ref, w_ref, dx_ref, dmod_ref, dz_ref, acc_sc, *bufs):
        i = pl.program_id(0)
        j = pl.program_id(1)

        @pl.when((i == 0) & (j == 0))
        def _():
            dmod_ref[...] = jnp.zeros_like(dmod_ref)

        @pl.when(j == 0)
        def _():
            acc_sc[...] = jnp.zeros_like(acc_sc)

        @pl.when(j == N_CHIPS - 1)
        def _():
            dz_ref[0] = dr_ref[...]

        @pl.when(j < N_CHIPS - 1)
        def _():
            for d, d_ref, tile_bufs in ((4, d4_ref, bufs[:_QKV_BLOCKS]), (16, d16_ref, bufs[_QKV_BLOCKS:])):
                _residues_to_rows(tile_bufs, d, lambda r, cb, d_ref=d_ref: d_ref[0, r, :, _LANES * cb:_LANES * (cb + 1)].astype(f32))
            for cb in range(_QKV_BLOCKS):
                cols = slice(_LANES * cb, _LANES * (cb + 1))
                dz_ref[0, :, cols] = (d1_ref[0, 0, :, cols].astype(f32) + bufs[cb][...] + bufs[_QKV_BLOCKS + cb][...]).astype(bf16)

        acc_sc[...] += _dot_nt(dz_ref[0], w_ref[0, 0])

        @pl.when(j == N_CHIPS - 1)
        def _():
            dh = acc_sc[...]
            xh, rstd0 = _ln_stats(x_ref[...])
            dmod_ref[0:1, :] += jnp.sum(dh, 0, keepdims=True)
            dmod_ref[1:2, :] += jnp.sum(dh * xh, 0, keepdims=True)
            dx_ref[...] = _ln_bwd(dh * (1.0 + mod_ref[1:2, :]), xh, rstd0) + dxr_ref[...]

    tok = pl.BlockSpec((ts, D), lambda i, j: (i, 0))
    res = [_res_spec(1, d, N, lambda i, j: (jnp.minimum(j, 2), 0, i, 0)) for d in DILATIONS]
    return pl.pallas_call(
        body, name="mix_in_bwd", grid=(S // ts, N_CHIPS),
        in_specs=res + [pl.BlockSpec((ts, N), lambda i, j: (i, 0)), tok, tok, _full((3, D)),
                        pl.BlockSpec((1, 1, D, N), lambda i, j: (j, l, 0, 0))],
        out_specs=[tok, _full((3, D)), pl.BlockSpec((1, ts, N), lambda i, j: (j, i, 0))],
        out_shape=[jax.ShapeDtypeStruct((S, D), f32), jax.ShapeDtypeStruct((3, D), f32),
                   jax.ShapeDtypeStruct((N_CHIPS, S, N), bf16)],
        scratch_shapes=[pltpu.VMEM((ts, D), f32)] + [pltpu.VMEM((ts, _LANES), f32)] * (2 * _QKV_BLOCKS),
        compiler_params=_cp(("arbitrary", "arbitrary"), 40),
    )(*_hbm(*dqkv, d_rest, dx_res, x, mod3, w_in))


def _mix_in_wgrad(h, dz, gw, l):
    S, D = h.shape
    N = dz.shape[-1]
    tk = TOK_TILE
    nk = S // tk

    def body(h_ref, dz_ref, _g, gw_ref, acc_sc):
        k = pl.program_id(1)

        @pl.when(k == 0)
        def _():
            acc_sc[...] = jnp.zeros_like(acc_sc)

        acc_sc[...] += _dot_tn(h_ref[...], dz_ref[0])

        @pl.when(k == nk - 1)
        def _():
            gw_ref[0, 0] = acc_sc[...].astype(bf16)

    return pl.pallas_call(
        body, name="mix_in_wgrad", grid=(N_CHIPS, nk),
        in_specs=[pl.BlockSpec((tk, D), lambda p, k: (k, 0)), pl.BlockSpec((1, tk, N), lambda p, k: (p, k, 0)),
                  pl.BlockSpec(memory_space=pl.ANY)],
        out_specs=pl.BlockSpec((1, 1, D, N), lambda p, k: (p, l, 0, 0)),
        out_shape=jax.ShapeDtypeStruct(gw.shape, bf16),
        scratch_shapes=[pltpu.VMEM((D, N), f32)],
        input_output_aliases={2: 0},
        compiler_params=_cp(("parallel", "arbitrary"), 40),
    )(*_hbm(h, dz, gw))


def _mix_out_fwd(x, y_att, y_ssm, y_pool, mod3, w_out, l, lng, lnb):
    S, D = x.shape
    ts = TOK_TILE

    def body(x_ref, ya_ref, ys_ref, yp_ref, mod_ref, w_ref, lng_ref, lnb_ref, xo_ref, y_ref):
        ya = ya_ref[...].astype(bf16)
        y = (_dot(ya[:, 0:256], w_ref[0, 0]) + _dot(ya[:, 256:512], w_ref[1, 0])
             + _dot(ys_ref[...].astype(bf16), w_ref[2, 0]) + _dot(yp_ref[...].astype(bf16), w_ref[3, 0]))
        y_ref[...] = y
        rh, _ = _ln_stats(ALPHA * x_ref[...] + mod_ref[2:3, :] * y)
        xo_ref[...] = rh * lng_ref[...] + lnb_ref[...]

    tok = pl.BlockSpec((ts, D), lambda i: (i, 0))
    return pl.pallas_call(
        body, name="mix_out_fwd", grid=(S // ts,),
        in_specs=[tok, pl.BlockSpec((ts, D_ATT), lambda i: (i, 0)), pl.BlockSpec((ts, D_SSM), lambda i: (i, 0)),
                  pl.BlockSpec((ts, D_POOL), lambda i: (i, 0)), _full((3, D)),
                  pl.BlockSpec((N_CHIPS, 1, 256, D), lambda i: (0, l, 0, 0)), _full((1, D)), _full((1, D))],
        out_specs=[tok, tok],
        out_shape=[jax.ShapeDtypeStruct((S, D), f32), jax.ShapeDtypeStruct((S, D), f32)],
        compiler_params=_cp(("parallel",), 40),
    )(*_hbm(x, y_att, y_ssm, y_pool, mod3, w_out, lng, lnb))


def _mix_out_bwd(dxo, x, y, y_att, y_ssm, y_pool, mod3, w_out, l, lng, gw_out):
    S, D = x.shape
    ts = TOK_TILE
    nt = S // ts

    def body(dxo_ref, x_ref, y_ref, ya_ref, ys_ref, yp_ref, mod_ref, w_ref, lng_ref, _g,
             dxr_ref, da_ref, ds_ref, dp_ref, dgate_ref, dlng_ref, dlnb_ref, gw_ref, acc_sc):
        i = pl.program_id(0)

        @pl.when(i == 0)
        def _():
            dgate_ref[...] = jnp.zeros_like(dgate_ref)
            dlng_ref[...] = jnp.zeros_like(dlng_ref)
            dlnb_ref[...] = jnp.zeros_like(dlnb_ref)
            acc_sc[...] = jnp.zeros_like(acc_sc)

        gate = mod_ref[2:3, :]
        yv = y_ref[...]
        rh, rstd = _ln_stats(ALPHA * x_ref[...] + gate * yv)
        dy_out = dxo_ref[...]
        dlng_ref[...] += jnp.sum(dy_out * rh, 0, keepdims=True)
        dlnb_ref[...] += jnp.sum(dy_out, 0, keepdims=True)
        dr = _ln_bwd(dy_out * lng_ref[...], rh, rstd)
        dxr_ref[...] = ALPHA * dr
        dgate_ref[...] += jnp.sum(dr * yv, 0, keepdims=True)
        dy = (gate * dr).astype(bf16)
        da_ref[:, 0:256] = _dot_nt(dy, w_ref[0, 0])
        da_ref[:, 256:512] = _dot_nt(dy, w_ref[1, 0])
        ds_ref[...] = _dot_nt(dy, w_ref[2, 0])
        dp_ref[...] = _dot_nt(dy, w_ref[3, 0])
        ya = ya_ref[...].astype(bf16)
        acc_sc[0] += _dot_tn(ya[:, 0:256], dy)
        acc_sc[1] += _dot_tn(ya[:, 256:512], dy)
        acc_sc[2] += _dot_tn(ys_ref[...].astype(bf16), dy)
        acc_sc[3] += _dot_tn(yp_ref[...].astype(bf16), dy)

        @pl.when(i == nt - 1)
        def _():
            gw_ref[:, 0] = acc_sc[...].astype(bf16)

    tok = pl.BlockSpec((ts, D), lambda i: (i, 0))
    t512 = pl.BlockSpec((ts, D_ATT), lambda i: (i, 0))
    t256 = pl.BlockSpec((ts, 256), lambda i: (i, 0))
    wspec = pl.BlockSpec((N_CHIPS, 1, 256, D), lambda i: (0, l, 0, 0))
    return pl.pallas_call(
        body, name="mix_out_bwd", grid=(nt,),
        in_specs=[tok, tok, tok, t512, t256, t256, _full((3, D)), wspec, _full((1, D)), pl.BlockSpec(memory_space=pl.ANY)],
        out_specs=[tok, t512, t256, t256, _full((1, D)), _full((1, D)), _full((1, D)), wspec],
        out_shape=[jax.ShapeDtypeStruct((S, D), f32), jax.ShapeDtypeStruct((S, D_ATT), f32),
                   jax.ShapeDtypeStruct((S, D_SSM), f32), jax.ShapeDtypeStruct((S, D_POOL), f32),
                   jax.ShapeDtypeStruct((1, D), f32), jax.ShapeDtypeStruct((1, D), f32), jax.ShapeDtypeStruct((1, D), f32),
                   jax.ShapeDtypeStruct(gw_out.shape, bf16)],
        scratch_shapes=[pltpu.VMEM((N_CHIPS, 256, D), f32)],
        input_output_aliases={9: 7},
        compiler_params=_cp(("arbitrary",), 48),
    )(*_hbm(dxo, x, y, y_att, y_ssm, y_pool, mod3, w_out, lng, gw_out))


def _t5_bucket(dist):
    max_exact = N_BUCKETS // 2
    d = np.maximum(dist, 1).astype(np.float32)
    large = max_exact + (np.log(d / max_exact) / math.log(MAX_DISTANCE / max_exact)
                         * (N_BUCKETS - max_exact)).astype(np.int32)
    large = np.minimum(large, N_BUCKETS - 1)
    return np.where(dist < max_exact, dist, large).astype(np.int32)


def _bucket_table():
    q = ATT_BLOCK
    i = np.arange(q)[:, None]
    j = np.arange(2 * q)[None, :]
    r = i + q - j
    in_band = (r >= 0) & (r <= q)
    tabs = [np.where(in_band, _t5_bucket(np.clip(r, 0, None) * d), -1) for d in DILATIONS]
    return np.stack(tabs).astype(np.int32)


def _bias_fwd(rel_bias, table):
    def body(rb_ref, tab_ref, out_ref):
        for b in range(3):
            tb = tab_ref[b]
            for h in range(N_HEADS):
                def pick(k, acc):
                    return jnp.where(tb == k, rb_ref[k, h], acc)
                out_ref[b, h] = lax.fori_loop(0, N_BUCKETS, pick, jnp.where(tb < 0, NEG, 0.0).astype(f32))

    return pl.pallas_call(
        body, name="bias_fwd",
        in_specs=[pl.BlockSpec(memory_space=pltpu.SMEM), pl.BlockSpec(memory_space=pltpu.VMEM)],
        out_specs=pl.BlockSpec(memory_space=pltpu.VMEM),
        out_shape=jax.ShapeDtypeStruct((3, N_HEADS, ATT_BLOCK, 2 * ATT_BLOCK), f32),
    )(rel_bias, table)


def _bias_bwd(dbias, table):
    def body(db_ref, tab_ref, out_ref):
        def per_bucket(k, c):
            for h in range(N_HEADS):
                tot = jnp.zeros((), f32)
                for b in range(3):
                    tot = tot + jnp.sum(jnp.where(tab_ref[b] == k, db_ref[b, h], 0.0))
                out_ref[k, h] = tot
            return c
        lax.fori_loop(0, N_BUCKETS, per_bucket, 0)

    return pl.pallas_call(
        body, name="bias_bwd",
        in_specs=[pl.BlockSpec(memory_space=pltpu.VMEM), pl.BlockSpec(memory_space=pltpu.VMEM)],
        out_specs=pl.BlockSpec(memory_space=pltpu.SMEM),
        out_shape=jax.ShapeDtypeStruct((N_BUCKETS, N_HEADS), f32),
    )(dbias, table)


def _att_unit(u, nbr):
    rows = pl.ds(pl.multiple_of(u * ATT_BLOCK, ATT_BLOCK), ATT_BLOCK)
    prev = pl.ds(pl.multiple_of(jnp.maximum(u - 1, 0) * ATT_BLOCK, ATT_BLOCK), ATT_BLOCK)
    return rows, prev, (u % nbr) != 0


_N_PAIRS = N_HEADS // 2


def _pair_rows(t):
    lane = lax.broadcasted_iota(jnp.int32, t.shape, 1)
    zero = jnp.zeros_like(t)
    return jnp.concatenate([jnp.where(lane < HEAD_DIM, t, zero), jnp.where(lane >= HEAD_DIM, t, zero)], axis=0)


def _pair_cols(big):
    lane = lax.broadcasted_iota(jnp.int32, (ATT_BLOCK, _LANES), 1)
    return jnp.where(lane < HEAD_DIM, big[:ATT_BLOCK], big[ATT_BLOCK:])


def _pair_column(ref, rows, hp):
    t = ref[rows, :]
    return jnp.concatenate([t[:, 2 * hp:2 * hp + 1], t[:, 2 * hp + 1:2 * hp + 2]], axis=0)


def _pair_band(ref, rows, prev, nbr, hp):
    lanes = pl.ds(_LANES * hp, _LANES)
    cur = ref[0, rows, lanes]
    return cur if nbr == 1 else jnp.concatenate([ref[0, prev, lanes], cur], axis=0)


def _pair_scores(q_ref, k_ref, b_ref, rows, prev, valid_prev, nbr, hp):
    qbd = _pair_rows(q_ref[0, rows, pl.ds(_LANES * hp, _LANES)])
    kb = _pair_band(k_ref, rows, prev, nbr, hp)
    bias = b_ref[0, 2 * hp:2 * hp + 2].reshape(2 * ATT_BLOCK, 2 * ATT_BLOCK)
    if nbr == 1:
        return qbd, kb, _dot_nt(qbd, kb) + bias[:, ATT_BLOCK:]
    s = _dot_nt(qbd, kb) + bias
    col = lax.broadcasted_iota(jnp.int32, s.shape, 1)
    return qbd, kb, jnp.where((col >= ATT_BLOCK) | valid_prev, s, NEG)


def _qkv_specs(S, branch):
    return ([pl.BlockSpec((1, S, D_ATT), lambda i, t=t: (t, 0, 0)) for t in range(3)],
            pl.BlockSpec((1, N_HEADS, ATT_BLOCK, 2 * ATT_BLOCK), lambda i: (branch, 0, 0, 0)))


def _att_fwd(qkv, bias, branch):
    S = qkv.shape[1]
    nbr = BLOCKS_PER_RESIDUE[branch]

    def body(q_ref, k_ref, v_ref, b_ref, o_ref, lse_ref):
        lse_ref[...] = jnp.zeros_like(lse_ref)

        def unit(u, c):
            rows, prev, valid_prev = _att_unit(u, nbr)
            for hp in range(_N_PAIRS):
                _, _, s = _pair_scores(q_ref, k_ref, b_ref, rows, prev, valid_prev, nbr, hp)
                m = jnp.max(s, -1, keepdims=True)
                p = jnp.exp(s - m)
                den = jnp.sum(p, -1, keepdims=True)
                big = _dot(p.astype(bf16), _pair_band(v_ref, rows, prev, nbr, hp))
                o_ref[rows, pl.ds(_LANES * hp, _LANES)] = _pair_cols(big / den)
                lse = m + jnp.log(den)
                lse_ref[rows, pl.ds(2 * hp, 1)] = lse[:ATT_BLOCK]
                lse_ref[rows, pl.ds(2 * hp + 1, 1)] = lse[ATT_BLOCK:]
            return c

        lax.fori_loop(0, N_UNITS, unit, 0)

    qkv_specs, bspec = _qkv_specs(S, branch)
    return pl.pallas_call(
        body, name="att_fwd", grid=(1,),
        in_specs=qkv_specs + [bspec],
        out_specs=[pl.BlockSpec((S, D_ATT), lambda i: (0, 0)), pl.BlockSpec((S, _LANES), lambda i: (0, 0))],
        out_shape=[jax.ShapeDtypeStruct((S, D_ATT), f32), jax.ShapeDtypeStruct((S, _LANES), f32)],
        compiler_params=_cp(("arbitrary",), 40),
    )(*_hbm(qkv, qkv, qkv, bias))


def _att_bwd(qkv, do, lse, crow, bias, branch):
    S = qkv.shape[1]
    nbr = BLOCKS_PER_RESIDUE[branch]

    def body(q_ref, k_ref, v_ref, do_ref, lse_ref, c_ref, b_ref, dqkv_ref, db_ref, dk_sc, dv_sc):
        dk_sc[...] = jnp.zeros_like(dk_sc)
        dv_sc[...] = jnp.zeros_like(dv_sc)
        db_ref[...] = jnp.zeros_like(db_ref)

        def unit(u, c):
            rows, prev, valid_prev = _att_unit(u, nbr)
            for hp in range(_N_PAIRS):
                lanes = pl.ds(_LANES * hp, _LANES)
                qbd, kb, s = _pair_scores(q_ref, k_ref, b_ref, rows, prev, valid_prev, nbr, hp)
                p = jnp.exp(s - _pair_column(lse_ref, rows, hp))
                dobd = _pair_rows(do_ref[rows, lanes])
                ds = p * (_dot_nt(dobd, _pair_band(v_ref, rows, prev, nbr, hp)) - _pair_column(c_ref, rows, hp))
                if nbr == 1:
                    db_ref[2 * hp:2 * hp + 2, :, ATT_BLOCK:] += ds.reshape(2, ATT_BLOCK, ATT_BLOCK)
                else:
                    db_ref[2 * hp:2 * hp + 2] += ds.reshape(2, ATT_BLOCK, 2 * ATT_BLOCK)
                dsb = ds.astype(bf16)
                dqkv_ref[0, rows, lanes] = (HEAD_DIM ** -0.5 * _pair_cols(_dot(dsb, kb))).astype(bf16)
                dkb = _dot_tn(dsb, qbd)
                dvb = _dot_tn(p.astype(bf16), dobd)
                if nbr == 1:
                    dk_sc[rows, lanes] += dkb
                    dv_sc[rows, lanes] += dvb
                else:
                    dk_sc[prev, lanes] += dkb[:ATT_BLOCK]
                    dv_sc[prev, lanes] += dvb[:ATT_BLOCK]
                    dk_sc[rows, lanes] += dkb[ATT_BLOCK:]
                    dv_sc[rows, lanes] += dvb[ATT_BLOCK:]
            return c

        lax.fori_loop(0, N_UNITS, unit, 0)
        dqkv_ref[1] = dk_sc[...].astype(bf16)
        dqkv_ref[2] = dv_sc[...].astype(bf16)

    qkv_specs, bspec = _qkv_specs(S, branch)
    row = pl.BlockSpec((S, _LANES), lambda i: (0, 0))
    return pl.pallas_call(
        body, name="att_bwd", grid=(1,),
        in_specs=qkv_specs + [pl.BlockSpec((S, D_ATT), lambda i: (0, 0)), row, row, bspec],
        out_specs=[pl.BlockSpec((3, S, D_ATT), lambda i: (0, 0, 0)),
                   pl.BlockSpec((N_HEADS, ATT_BLOCK, 2 * ATT_BLOCK), lambda i: (0, 0, 0))],
        out_shape=[jax.ShapeDtypeStruct((3, S, D_ATT), bf16), jax.ShapeDtypeStruct((N_HEADS, ATT_BLOCK, 2 * ATT_BLOCK), f32)],
        scratch_shapes=[pltpu.VMEM((S, D_ATT), f32), pltpu.VMEM((S, D_ATT), f32)],
        compiler_params=_cp(("arbitrary",), 48),
    )(*_hbm(qkv, qkv, qkv, do, lse, crow, bias))


def _branch_weights(lse_ref):
    l0, l1, l2 = lse_ref[0], lse_ref[1], lse_ref[2]
    m = jnp.maximum(jnp.maximum(l0, l1), l2)
    e0, e1, e2 = jnp.exp(l0 - m), jnp.exp(l1 - m), jnp.exp(l2 - m)
    tot = e0 + e1 + e2
    return e0 / tot, e1 / tot, e2 / tot


def _att_merge(os, lses):
    S = os[0].shape[0] * os[0].shape[1]
    ts = TOK_TILE

    def body(o1_ref, o4_ref, o16_ref, l1_ref, l4_ref, l16_ref, y_ref, lt_ref, *bufs):
        obufs = (bufs[:_QKV_BLOCKS], bufs[_QKV_BLOCKS:2 * _QKV_BLOCKS])
        lt_ref[0] = l1_ref[0]
        for k, (d, o_ref, l_ref) in enumerate(((4, o4_ref, l4_ref), (16, o16_ref, l16_ref))):
            _residues_to_rows(obufs[k], d, lambda r, cb, o_ref=o_ref: o_ref[r, :, _LANES * cb:_LANES * (cb + 1)])
            _residues_to_rows([bufs[2 * _QKV_BLOCKS + k]], d, lambda r, cb, l_ref=l_ref: l_ref[r])
            lt_ref[1 + k] = bufs[2 * _QKV_BLOCKS + k][...]
        w = _branch_weights(lt_ref)
        for h in range(N_HEADS):
            cs = slice(HEAD_DIM * h, HEAD_DIM * (h + 1))
            half = slice(HEAD_DIM * (h % 2), HEAD_DIM * (h % 2 + 1))
            y_ref[:, cs] = (w[0][:, h:h + 1] * o1_ref[0, :, cs] + w[1][:, h:h + 1] * obufs[0][h // 2][:, half]
                            + w[2][:, h:h + 1] * obufs[1][h // 2][:, half])

    return pl.pallas_call(
        body, name="att_merge", grid=(S // ts,),
        in_specs=[_res_spec3(d, D_ATT) for d in DILATIONS] + [_res_spec3(d, _LANES) for d in DILATIONS],
        out_specs=[pl.BlockSpec((ts, D_ATT), lambda i: (i, 0)), pl.BlockSpec((3, ts, _LANES), lambda i: (0, i, 0))],
        out_shape=[jax.ShapeDtypeStruct((S, D_ATT), f32), jax.ShapeDtypeStruct((3, S, _LANES), f32)],
        scratch_shapes=[pltpu.VMEM((ts, _LANES), f32)] * (2 * _QKV_BLOCKS + 2),
        compiler_params=_cp(("parallel",)),
    )(*_hbm(*os, *lses))


def _att_merge_bwd(dy, y, lse3):
    S = dy.shape[0]
    ts = TOK_TILE

    def body(dy_ref, y_ref, lse_ref, do1_ref, do4_ref, do16_ref, c1_ref, c4_ref, c16_ref, *bufs):
        dobufs = (bufs[:_QKV_BLOCKS], bufs[_QKV_BLOCKS:2 * _QKV_BLOCKS], bufs[2 * _QKV_BLOCKS:3 * _QKV_BLOCKS])
        cbufs = bufs[3 * _QKV_BLOCKS:]
        w = _branch_weights(lse_ref)
        for cb in cbufs:
            cb[...] = jnp.zeros_like(cb)
        for h in range(N_HEADS):
            cs = slice(HEAD_DIM * h, HEAD_DIM * (h + 1))
            half = slice(HEAD_DIM * (h % 2), HEAD_DIM * (h % 2 + 1))
            dyh = dy_ref[:, cs]
            t = jnp.sum(dyh * y_ref[:, cs], -1, keepdims=True)
            for p in range(3):
                wp = w[p][:, h:h + 1]
                dobufs[p][h // 2][:, half] = wp * dyh
                cbufs[p][:, h:h + 1] = wp * t
        for cb in range(_QKV_BLOCKS):
            do1_ref[0, :, _LANES * cb:_LANES * (cb + 1)] = dobufs[0][cb][...].astype(bf16)
        c1_ref[0] = cbufs[0][...]
        for k, (d, do_ref, c_ref) in enumerate(((4, do4_ref, c4_ref), (16, do16_ref, c16_ref))):
            def put_do(r, cb, piece, do_ref=do_ref):
                do_ref[r, :, _LANES * cb:_LANES * (cb + 1)] = piece.astype(bf16)

            def put_c(r, cb, piece, c_ref=c_ref):
                c_ref[r] = piece

            _rows_to_residues(dobufs[1 + k], d, put_do)
            _rows_to_residues([cbufs[1 + k]], d, put_c)

    return pl.pallas_call(
        body, name="att_merge_bwd", grid=(S // ts,),
        in_specs=[pl.BlockSpec((ts, D_ATT), lambda i: (i, 0)), pl.BlockSpec((ts, D_ATT), lambda i: (i, 0)),
                  pl.BlockSpec((3, ts, _LANES), lambda i: (0, i, 0))],
        out_specs=[_res_spec3(d, D_ATT) for d in DILATIONS] + [_res_spec3(d, _LANES) for d in DILATIONS],
        out_shape=[jax.ShapeDtypeStruct((d, S // d, D_ATT), bf16) for d in DILATIONS]
        + [jax.ShapeDtypeStruct((d, S // d, _LANES), f32) for d in DILATIONS],
        scratch_shapes=[pltpu.VMEM((ts, _LANES), f32)] * (3 * _QKV_BLOCKS + 3),
        compiler_params=_cp(("parallel",)),
    )(*_hbm(dy, y, lse3))


_SSM_ROWS = 256


def _scan_in_place(sr_ref, si_ref, a_ref, reverse):
    S, N = sr_ref.shape
    nst = S // SCAN_SEG
    ar = jnp.broadcast_to(a_ref[0:1, :], (SCAN_SEG, N))
    ai = jnp.broadcast_to(a_ref[1:2, :], (SCAN_SEG, N))
    if reverse:
        ai = -ai
    row = lax.broadcasted_iota(jnp.int32, (SCAN_SEG, N), 0)
    zero = jnp.zeros((SCAN_SEG, N), f32)

    def tile(t):
        return pl.ds(pl.multiple_of((nst - 1 - t if reverse else t) * SCAN_SEG, SCAN_SEG), SCAN_SEG)

    def local(t, c):
        sr, si, pr, pi = c
        rows = tile(t)
        nsr = ar * sr - ai * si + sr_ref[rows, :]
        nsi = ar * si + ai * sr + si_ref[rows, :]
        sr_ref[rows, :] = nsr
        si_ref[rows, :] = nsi
        return nsr, nsi, ar * pr - ai * pi, ar * pi + ai * pr

    fr, fi, apr, api = lax.fori_loop(0, nst, local, (zero, zero, zero + 1.0, zero))

    def shift(v):
        if reverse:
            return jnp.where(row == SCAN_SEG - 1, 0.0, pltpu.roll(v, SCAN_SEG - 1, axis=0))
        return jnp.where(row == 0, 0.0, pltpu.roll(v, 1, axis=0))

    cr, ci = zero, zero
    for _ in range(SCAN_SEG - 1):
        cr, ci = shift(fr + apr * cr - api * ci), shift(fi + apr * ci + api * cr)

    def fix(t, c):
        pr, pi = c
        npr, npi = ar * pr - ai * pi, ar * pi + ai * pr
        rows = tile(t)
        sr_ref[rows, :] += npr * cr - npi * ci
        si_ref[rows, :] += npr * ci + npi * cr
        return npr, npi

    lax.fori_loop(0, nst, fix, (zero + 1.0, zero))


def _ssm_states(u, bre, bim, a2, l):
    S = u.shape[0]

    def body(u_ref, br_ref, bi_ref, a2_ref, sr_ref, si_ref):
        a_ref = a2_ref.at[l]
        brb = br_ref[l].astype(bf16)
        bib = bi_ref[l].astype(bf16)

        def project(t, c):
            rows = pl.ds(pl.multiple_of(t * _SSM_ROWS, _SSM_ROWS), _SSM_ROWS)
            ub = u_ref[rows, :].astype(bf16)
            sr_ref[rows, :] = _dot(ub, brb)
            si_ref[rows, :] = _dot(ub, bib)
            return c

        lax.fori_loop(0, S // _SSM_ROWS, project, 0)
        _scan_in_place(sr_ref, si_ref, a_ref, False)

    vm = pl.BlockSpec(memory_space=pltpu.VMEM)
    return pl.pallas_call(
        body, name="ssm_states", in_specs=[vm] * 4, out_specs=[vm, vm],
        out_shape=[jax.ShapeDtypeStruct((S, D_STATE), f32)] * 2,
        compiler_params=_cp(None, 48),
    )(u, bre, bim, a2)


def _ssm_out(sr, si, u, cre, cim, l, dskip, glu_w, glu_b):
    S = u.shape[0]
    ts = TOK_TILE

    def body(sr_ref, si_ref, u_ref, cr_ref, ci_ref, d_ref, w_ref, b_ref, out_ref, y_ref):
        y = (_dot(sr_ref[...].astype(bf16), cr_ref[0].astype(bf16))
             - _dot(si_ref[...].astype(bf16), ci_ref[0].astype(bf16)) + d_ref[...] * u_ref[...])
        y_ref[...] = y
        z = _dot(_gelu(y).astype(bf16), w_ref[...].astype(bf16)) + b_ref[...]
        out_ref[...] = y * jax.nn.sigmoid(z)

    st = pl.BlockSpec((ts, D_STATE), lambda i: (i, 0))
    ch = pl.BlockSpec((ts, D_SSM), lambda i: (i, 0))
    c_l = pl.BlockSpec((1, D_STATE, D_SSM), lambda i: (l, 0, 0))
    return pl.pallas_call(
        body, name="ssm_out", grid=(S // ts,),
        in_specs=[st, st, ch, c_l, c_l, _full((1, D_SSM)), _full((D_SSM, D_SSM)), _full((1, D_SSM))],
        out_specs=[ch, ch],
        out_shape=[jax.ShapeDtypeStruct((S, D_SSM), f32)] * 2,
        compiler_params=_cp(("parallel",)),
    )(*_hbm(sr, si, u, cre, cim, dskip, glu_w, glu_b))


def _ssm_out_bwd(dout, y, u, sr, si, dskip, glu_w, glu_b):
    S = u.shape[0]
    ts = TOK_TILE

    def body(do_ref, y_ref, u_ref, sr_ref, si_ref, d_ref, w_ref, b_ref,
             dy_ref, du_ref, dcr_ref, dci_ref, dd_ref, dgb_ref, dgw_ref):
        @pl.when(pl.program_id(0) == 0)
        def _():
            for r in (dcr_ref, dci_ref, dd_ref, dgb_ref, dgw_ref):
                r[...] = jnp.zeros_like(r)

        y = y_ref[...]
        dout = do_ref[...]
        wb = w_ref[...].astype(bf16)
        ge = _gelu(y).astype(bf16)
        sz = jax.nn.sigmoid(_dot(ge, wb) + b_ref[...])
        dz = dout * y * sz * (1.0 - sz)
        dzb = dz.astype(bf16)
        dgb_ref[...] += jnp.sum(dz, 0, keepdims=True)
        dgw_ref[...] += _dot_tn(ge, dzb)
        dy = dout * sz + _gelu_grad(y) * _dot_nt(dzb, wb)
        uv = u_ref[...]
        dd_ref[...] += jnp.sum(dy * uv, 0, keepdims=True)
        du_ref[...] = dy * d_ref[...]
        dy_ref[...] = dy
        dyb = dy.astype(bf16)
        dcr_ref[...] += _dot_tn(sr_ref[...].astype(bf16), dyb)
        dci_ref[...] -= _dot_tn(si_ref[...].astype(bf16), dyb)

    st = pl.BlockSpec((ts, D_STATE), lambda i: (i, 0))
    ch = pl.BlockSpec((ts, D_SSM), lambda i: (i, 0))
    c_full = _full((D_STATE, D_SSM))
    return pl.pallas_call(
        body, name="ssm_out_bwd", grid=(S // ts,),
        in_specs=[ch, ch, ch, st, st, _full((1, D_SSM)), _full((D_SSM, D_SSM)), _full((1, D_SSM))],
        out_specs=[ch, ch, c_full, c_full, _full((1, D_SSM)), _full((1, D_SSM)), _full((D_SSM, D_SSM))],
        out_shape=[jax.ShapeDtypeStruct((S, D_SSM), f32), jax.ShapeDtypeStruct((S, D_SSM), f32),
                   jax.ShapeDtypeStruct((D_STATE, D_SSM), f32), jax.ShapeDtypeStruct((D_STATE, D_SSM), f32),
                   jax.ShapeDtypeStruct((1, D_SSM), f32), jax.ShapeDtypeStruct((1, D_SSM), f32),
                   jax.ShapeDtypeStruct((D_SSM, D_SSM), f32)],
        compiler_params=_cp(("arbitrary",), 40),
    )(*_hbm(dout, y, u, sr, si, dskip, glu_w, glu_b))


def _ssm_states_bwd(dy, du_skip, u, sr, si, cre, cim, bre, bim, a2, l):
    S = u.shape[0]
    N = D_STATE
    nst = S // SCAN_SEG
    nproj = S // _SSM_ROWS

    def body(dy_ref, dus_ref, u_ref, sr_ref, si_ref, cr_ref, ci_ref, br_ref, bi_ref, a2_ref,
             du_ref, dbr_ref, dbi_ref, da_ref, lr_ref, li_ref):
        a_ref = a2_ref.at[l]
        crb = cr_ref[l].astype(bf16)
        cib = ci_ref[l].astype(bf16)

        def project(t, c):
            rows = pl.ds(pl.multiple_of(t * _SSM_ROWS, _SSM_ROWS), _SSM_ROWS)
            dyb = dy_ref[rows, :].astype(bf16)
            lr_ref[rows, :] = _dot_nt(dyb, crb)
            li_ref[rows, :] = -_dot_nt(dyb, cib)
            return c

        lax.fori_loop(0, nproj, project, 0)
        _scan_in_place(lr_ref, li_ref, a_ref, True)

        row = lax.broadcasted_iota(jnp.int32, (SCAN_SEG, N), 0)
        last = pl.ds((nst - 1) * SCAN_SEG, SCAN_SEG)
        pr = jnp.where(row == 0, 0.0, pltpu.roll(sr_ref[last, :], 1, axis=0))
        pi = jnp.where(row == 0, 0.0, pltpu.roll(si_ref[last, :], 1, axis=0))
        first = pl.ds(0, SCAN_SEG)
        acc_r = lr_ref[first, :] * pr + li_ref[first, :] * pi
        acc_i = li_ref[first, :] * pr - lr_ref[first, :] * pi

        def step(t, c):
            acc_r, acc_i = c
            rows = pl.ds(pl.multiple_of(t * SCAN_SEG, SCAN_SEG), SCAN_SEG)
            prev = pl.ds(pl.multiple_of((t - 1) * SCAN_SEG, SCAN_SEG), SCAN_SEG)
            lrv, liv, srv, siv = lr_ref[rows, :], li_ref[rows, :], sr_ref[prev, :], si_ref[prev, :]
            return acc_r + lrv * srv + liv * siv, acc_i + liv * srv - lrv * siv

        acc_r, acc_i = lax.fori_loop(1, nst, step, (acc_r, acc_i))
        da_ref[0:1, :] = jnp.sum(acc_r, 0, keepdims=True)
        da_ref[1:2, :] = jnp.sum(acc_i, 0, keepdims=True)

        brb = br_ref[l].astype(bf16)
        bib = bi_ref[l].astype(bf16)
        dbr_ref[...] = jnp.zeros_like(dbr_ref)
        dbi_ref[...] = jnp.zeros_like(dbi_ref)

        def back(t, c):
            rows = pl.ds(pl.multiple_of(t * _SSM_ROWS, _SSM_ROWS), _SSM_ROWS)
            lrb = lr_ref[rows, :].astype(bf16)
            lib = li_ref[rows, :].astype(bf16)
            du_ref[rows, :] = dus_ref[rows, :] + _dot_nt(lrb, brb) + _dot_nt(lib, bib)
            ub = u_ref[rows, :].astype(bf16)
            dbr_ref[...] += _dot_tn(ub, lrb)
            dbi_ref[...] += _dot_tn(ub, lib)
            return c

        lax.fori_loop(0, nproj, back, 0)

    vm = pl.BlockSpec(memory_space=pltpu.VMEM)
    return pl.pallas_call(
        body, name="ssm_states_bwd", in_specs=[vm] * 10, out_specs=[vm] * 4,
        out_shape=[jax.ShapeDtypeStruct((S, D_SSM), f32), jax.ShapeDtypeStruct((D_SSM, D_STATE), f32),
                   jax.ShapeDtypeStruct((D_SSM, D_STATE), f32), jax.ShapeDtypeStruct((2, D_STATE), f32)],
        scratch_shapes=[pltpu.VMEM((S, D_STATE), f32), pltpu.VMEM((S, D_STATE), f32)],
        compiler_params=_cp(None, 56),
    )(dy, du_skip, u, sr, si, cre, cim, bre, bim, a2)


_POOL_TILE = 256


def _window_sums(xt, back):
    n = xt.shape[0]
    out = []
    ws = xt
    for k in (1, 2, 4, 8):
        ws = ws + pltpu.roll(ws, k if back else n - k, axis=0)
        out.append(ws)
    return out


def _pool_count(r0, w):
    t = r0 + lax.broadcasted_iota(jnp.int32, (_POOL_TILE, POOL_GROUP), 0)
    return jnp.minimum(t + 1, w).astype(f32)


def _pool_fwd(u_pad, pool_w, pool_scale):
    S = u_pad.shape[0] - POOL_HALO
    nt = S // _POOL_TILE

    def body(u_ref, w_ref, sc_ref, y_ref):
        def tile(t, c):
            r0 = pl.multiple_of(t * _POOL_TILE, _POOL_TILE)
            for g, w in enumerate(POOL_WINDOWS):
                cs = pl.ds(POOL_GROUP * g, POOL_GROUP)
                xt = u_ref[pl.ds(r0, _POOL_TILE + POOL_HALO), cs]
                ws = _window_sums(xt, True)[g][POOL_HALO:, :]
                pooled = ws / _pool_count(r0, w) - xt[POOL_HALO:, :]
                y_ref[pl.ds(r0, _POOL_TILE), cs] = _dot(pooled.astype(bf16), w_ref[g].astype(bf16)) * sc_ref[:, cs]
            return c
        lax.fori_loop(0, nt, tile, 0)

    vm = pl.BlockSpec(memory_space=pltpu.VMEM)
    return pl.pallas_call(
        body, name="pool_fwd", in_specs=[vm, vm, vm], out_specs=vm,
        out_shape=jax.ShapeDtypeStruct((S, D_POOL), f32),
    )(u_pad, pool_w, pool_scale)


def _pool_bwd(dy_pad, u_pad, pool_w, pool_scale):
    S = u_pad.shape[0] - POOL_HALO
    nt = S // _POOL_TILE
    n = _POOL_TILE + POOL_HALO

    def body(dy_ref, u_ref, w_ref, sc_ref, du_ref, dw_ref, dsc_ref):
        dw_ref[...] = jnp.zeros_like(dw_ref)
        dsc_ref[...] = jnp.zeros_like(dsc_ref)

        def tile(t, c):
            r0 = pl.multiple_of(t * _POOL_TILE, _POOL_TILE)
            for g, w in enumerate(POOL_WINDOWS):
                cs = pl.ds(POOL_GROUP * g, POOL_GROUP)
                wb = w_ref[g].astype(bf16)
                xt = u_ref[pl.ds(r0, n), cs]
                pooled = (_window_sums(xt, True)[g][POOL_HALO:, :] / _pool_count(r0, w) - xt[POOL_HALO:, :]).astype(bf16)
                dy = dy_ref[pl.ds(r0, _POOL_TILE), cs]
                dsc_ref[:, cs] += jnp.sum(dy * _dot(pooled, wb), 0, keepdims=True)
                dw_ref[g] += _dot_tn(pooled, (dy * sc_ref[:, cs]).astype(bf16))
                dyh = (dy_ref[pl.ds(r0, n), cs] * sc_ref[:, cs]).astype(bf16)
                dpl = _dot_nt(dyh, wb)
                cnt = jnp.minimum(r0 + lax.broadcasted_iota(jnp.int32, (n, POOL_GROUP), 0) + 1, w).astype(f32)
                lead = _window_sums(dpl / cnt, False)[g]
                du_ref[pl.ds(r0, _POOL_TILE), cs] = lead[:_POOL_TILE, :] - dpl[:_POOL_TILE, :]
            return c
        lax.fori_loop(0, nt, tile, 0)

    vm = pl.BlockSpec(memory_space=pltpu.VMEM)
    return pl.pallas_call(
        body, name="pool_bwd", in_specs=[vm, vm, vm, vm], out_specs=[vm, vm, vm],
        out_shape=[jax.ShapeDtypeStruct((S, D_POOL), f32), jax.ShapeDtypeStruct((4, POOL_GROUP, POOL_GROUP), f32),
                   jax.ShapeDtypeStruct((1, D_POOL), f32)],
    )(dy_pad, u_pad, pool_w, pool_scale)


def _loss_head(y, target):
    S, D = y.shape
    ts = TOK_TILE

    def body(y_ref, t_ref, loss_ref, dy_ref):
        @pl.when(pl.program_id(0) == 0)
        def _():
            loss_ref[...] = jnp.zeros_like(loss_ref)

        d = y_ref[...] - t_ref[...]
        dy_ref[...] = d * (1.0 / D)
        loss_ref[...] += 0.5 * jnp.sum(jnp.sum(d * d, -1, keepdims=True) * (1.0 / D), 0, keepdims=True)

    tok = pl.BlockSpec((ts, D), lambda i: (i, 0))
    return pl.pallas_call(
        body, name="loss_head", grid=(S // ts,),
        in_specs=[tok, tok], out_specs=[_full((1, 1)), tok],
        out_shape=[jax.ShapeDtypeStruct((1, 1), f32), jax.ShapeDtypeStruct((S, D), f32)],
        compiler_params=_cp(("arbitrary",)),
    )(*_hbm(y, target))


_ADA_COLS = 768


def _ada_fwd(c_all, ada_w, ada_b_cols):
    L, D, N = ada_w.shape
    B = c_all.shape[0]

    def body(c_ref, w_ref, b_ref, out_ref):
        cv = c_ref[...]
        cond = (cv * jax.nn.sigmoid(cv)).astype(bf16)
        out_ref[0] = _dot(cond, w_ref[0].astype(bf16)) + b_ref[0]

    return pl.pallas_call(
        body, name="ada_fwd", grid=(L, N // _ADA_COLS),
        in_specs=[_full((B, D)), pl.BlockSpec((1, D, _ADA_COLS), lambda l, j: (l, 0, j)),
                  pl.BlockSpec((1, 1, _ADA_COLS), lambda l, j: (l, 0, j))],
        out_specs=pl.BlockSpec((1, B, _ADA_COLS), lambda l, j: (l, 0, j)),
        out_shape=jax.ShapeDtypeStruct((L, B, N), f32),
        compiler_params=_cp(("parallel", "parallel")),
    )(c_all, ada_w, ada_b_cols)


def _ada_wgrad(c_all_t, dmod_cols):
    D, B = c_all_t.shape
    L, _, N = dmod_cols.shape

    def body(ct_ref, dm_ref, out_ref):
        cv = ct_ref[...]
        cond = cv * jax.nn.sigmoid(cv)
        acc = cond[:, 0:1] * dm_ref[0, 0:1, :]
        for b in range(1, B):
            acc = acc + cond[:, b:b + 1] * dm_ref[0, b:b + 1, :]
        out_ref[0] = acc

    return pl.pallas_call(
        body, name="ada_wgrad", grid=(L, N // _ADA_COLS),
        in_specs=[_full((D, B)), pl.BlockSpec((1, B, _ADA_COLS), lambda l, j: (l, 0, j))],
        out_specs=pl.BlockSpec((1, D, _ADA_COLS), lambda l, j: (l, 0, j)),
        out_shape=jax.ShapeDtypeStruct((L, D, N), f32),
        compiler_params=_cp(("parallel", "parallel")),
    )(c_all_t, dmod_cols)


def _adam_math(w, g, m, v):
    m = ADAM_B1 * m + (1.0 - ADAM_B1) * g
    v = ADAM_B2 * v + (1.0 - ADAM_B2) * (g * g)
    m_hat = m / (1.0 - ADAM_B1 ** ADAM_STEP)
    v_hat = v / (1.0 - ADAM_B2 ** ADAM_STEP)
    delta = -ADAM_LR * (m_hat / (jnp.sqrt(v_hat) + ADAM_EPS) + ADAM_WD * w)
    return delta, m, v


def _adamw(w, m, v, g, row_tile, row0=0, outs=None):
    R, C = w.shape
    b0 = row0 // row_tile

    def body(w_ref, m_ref, v_ref, g_ref, _0, _1, _2, _3, g_out, d_out, m_out, v_out):
        gv = g_ref[...]
        delta, mn, vn = _adam_math(w_ref[...], gv, m_ref[...], v_ref[...])
        g_out[...] = gv
        d_out[...] = delta
        m_out[...] = mn
        v_out[...] = vn

    pspec = pl.BlockSpec((row_tile, C), lambda i: (b0 + i, 0))
    gspec = pl.BlockSpec((row_tile, C), lambda i: (i, 0))
    anyspec = pl.BlockSpec(memory_space=pl.ANY)
    shp = jax.ShapeDtypeStruct((R, C), f32)
    if outs is None:
        outs = [lax.empty((R, C), f32) for _ in range(4)]
    return pl.pallas_call(
        body, name="adamw", grid=(g.shape[0] // row_tile,),
        in_specs=[pspec] * 3 + [gspec] + [anyspec] * 4, out_specs=[pspec] * 4, out_shape=[shp] * 4,
        input_output_aliases={4: 0, 5: 1, 6: 2, 7: 3},
        compiler_params=_cp(("parallel",), 40),
    )(*_hbm(w, m, v, g, *outs))


def _pair_sum(g5s, gots, pc):
    n = len(g5s)

    def body(pc_ref, *refs):
        for own, got, out in zip(refs[:n], refs[n:2 * n], refs[2 * n:]):
            out[0, 0] = (own[0, 0, 0].astype(f32) + got[0, 0].astype(f32)).astype(bf16)

    def half(g):
        return pl.BlockSpec((1, 1) + g.shape[-2:], lambda p, pc: (p, 0, 0, 0))

    gs = pltpu.PrefetchScalarGridSpec(
        num_scalar_prefetch=1, grid=(N_CHIPS,),
        in_specs=[pl.BlockSpec((1, 1, 1) + g.shape[-2:], lambda p, pc: (p, 0, pc[1], 0, 0)) for g in g5s]
        + [half(g) for g in gots],
        out_specs=[half(g) for g in gots],
    )
    return pl.pallas_call(
        body, name="pair_sum", grid_spec=gs, out_shape=[jax.ShapeDtypeStruct(g.shape, bf16) for g in gots],
        compiler_params=_cp(("parallel",), 48),
    )(pc, *_hbm(*g5s, *gots))


_SUM_STEPS = 2


def _sum_shards(hsums, recvs, pc):
    n = len(hsums)

    def body(pc_ref, *refs):
        for own, got, out in zip(refs[:n], refs[n:2 * n], refs[2 * n:]):
            acc = own[0, 0].astype(f32)
            for j in range(3):
                acc = acc + got[j, 0].astype(f32)
            out[0, 0] = acc

    def rows(h):
        return (h.shape[2] // _SUM_STEPS, h.shape[3])

    gs = pltpu.PrefetchScalarGridSpec(
        num_scalar_prefetch=1, grid=(_SUM_STEPS,),
        in_specs=[pl.BlockSpec((1, 1) + rows(h), lambda i, pc: (pc[0], 0, i, 0)) for h in hsums]
        + [pl.BlockSpec((3, 1) + rows(h), lambda i, pc: (0, 0, i, 0)) for h in hsums],
        out_specs=[pl.BlockSpec((1, 1) + rows(h), lambda i, pc: (0, pc[1], i, 0)) for h in hsums],
    )
    return pl.pallas_call(
        body, name="sum_shards", grid_spec=gs,
        out_shape=[jax.ShapeDtypeStruct((1, 2) + h.shape[2:], f32) for h in hsums],
        compiler_params=_cp(("parallel",), 48),
    )(pc, *_hbm(*hsums, *recvs))


def _sum8(packs):
    _, R, C = packs.shape
    tr = R // 8 if R % 64 == 0 else R

    def body(p_ref, out_ref):
        acc = p_ref[0]
        for d in range(1, 8):
            acc = acc + p_ref[d]
        out_ref[...] = acc

    return pl.pallas_call(
        body, name="sum8", grid=(R // tr,),
        in_specs=[pl.BlockSpec((8, tr, C), lambda i: (0, i, 0))],
        out_specs=pl.BlockSpec((tr, C), lambda i: (i, 0)),
        out_shape=jax.ShapeDtypeStruct((R, C), f32),
        compiler_params=_cp(("parallel",)),
    )(packs)


def _allgather8(x_shard):
    m_per, n = x_shard.shape

    def body(x_ref, out_ref, send_sems, recv_sems, local_sem):
        x, y, c = lax.axis_index("x"), lax.axis_index("y"), lax.axis_index("c")
        me, sibling = (x, y, c), (x, y, 1 - c)
        chips = [(1 - x, y), (x, 1 - y), (1 - x, 1 - y)]

        def rows(px, py, pc):
            return out_ref.at[pl.ds((4 * px + 2 * py + pc) * m_per, m_per), :]

        def copy(k, block, to, src=None):
            return pltpu.make_async_remote_copy(
                src_ref=rows(*block) if src is None else src, dst_ref=rows(*block),
                send_sem=send_sems.at[k], recv_sem=recv_sems.at[k], device_id=to, device_id_type=MESH)

        mine = pltpu.make_async_copy(x_ref, rows(*me), local_sem)
        mine.start()
        first = [copy(0, me, sibling, src=x_ref)]
        first += [copy(1 + j, me, (*chip, c), src=x_ref) for j, chip in enumerate(chips)]
        for cp in first:
            cp.start()
        passed = [copy(4 + j, (*chip, c), sibling) for j, chip in enumerate(chips)]
        for j, chip in enumerate(chips):
            copy(1 + j, (*chip, c), me).wait_recv()
            passed[j].start()
        copy(0, sibling, me).wait_recv()
        for j, chip in enumerate(chips):
            copy(4 + j, (*chip, 1 - c), me).wait_recv()
        for cp in first + passed:
            cp.wait_send()
        mine.wait()

    return pl.pallas_call(
        body, name="allgather8",
        out_shape=jax.ShapeDtypeStruct((8 * m_per, n), x_shard.dtype),
        in_specs=[pl.BlockSpec(memory_space=pltpu.VMEM)],
        out_specs=pl.BlockSpec(memory_space=pltpu.VMEM),
        scratch_shapes=[pltpu.SemaphoreType.DMA((7,)), pltpu.SemaphoreType.DMA((7,)), pltpu.SemaphoreType.DMA],
        compiler_params=_cp(None, 48),
    )(x_shard)


def _other_chips():
    x, y = lax.axis_index("x"), lax.axis_index("y")
    return [(1 - x, y), (x, 1 - y), (1 - x, 1 - y)]


_HBM = pl.BlockSpec(memory_space=pltpu.HBM)
_SEM = pl.BlockSpec(memory_space=pltpu.SEMAPHORE)
_EFFECT = pltpu.SideEffectType.DATAFLOW_SIDE_EFFECTING


def _gather_copies(srcs, lands, send_sems, recv_sems):
    x, y, c = lax.axis_index("x"), lax.axis_index("y"), lax.axis_index("c")
    return [pltpu.make_async_remote_copy(
        src_ref=srcs[a].at[:, c], dst_ref=lands[a].at[2 * x + y, :, c], send_sem=send_sems.at[3 * a + j],
        recv_sem=recv_sems.at[3 * a + j], device_id=(cx, cy, c), device_id_type=MESH)
        for a in range(len(srcs)) for j, (cx, cy) in enumerate(_other_chips())]


def _gather_start(chunks, after, name):
    sizes = [len(srcs) for srcs, _ in chunks]
    flat = [t for srcs, lands in chunks for t in list(srcs) + list(lands)]
    nflat = len(flat)
    nsem = 2 * len(chunks)

    def body(*refs):
        ins, sems, token = refs[:nflat], refs[nflat + 1:nflat + 1 + nsem], refs[-1]
        off = 0
        for k, n in enumerate(sizes):
            for cp in _gather_copies(ins[off:off + n], ins[off + n:off + 2 * n], sems[2 * k], sems[2 * k + 1]):
                cp.start()
            off += 2 * n
        token[...] = jnp.zeros_like(token)

    res = pl.pallas_call(
        body, name=name,
        out_shape=[pltpu.SemaphoreType.DMA((3 * n,)) for n in sizes for _ in range(2)]
        + [pltpu.HBM(t.shape, t.dtype) for t in flat] + [jax.ShapeDtypeStruct((8, 128), f32)],
        in_specs=[_HBM] * nflat + [pl.BlockSpec(memory_space=pl.ANY)],
        out_specs=[_SEM] * nsem + [_HBM] * nflat + [pl.BlockSpec(memory_space=pltpu.VMEM)],
        input_output_aliases={i: nsem + i for i in range(nflat)},
        compiler_params=pltpu.CompilerParams(has_side_effects=_EFFECT),
    )(*[pltpu.with_memory_space_constraint(t, pltpu.HBM) for t in flat], after)
    out, off = [], nsem
    for k, n in enumerate(sizes):
        out.append((res[2 * k], res[2 * k + 1], res[off:off + n], res[off + n:off + 2 * n]))
        off += 2 * n
    return out, res[-1]


def _gather_wait(send_sems, recv_sems, srcs, lands, after, name):
    n = len(srcs)

    def body(*refs):
        for cp in _gather_copies(refs[:n], refs[n:2 * n], refs[2 * n], refs[2 * n + 1]):
            cp.wait_send()
            cp.wait_recv()

    res = pl.pallas_call(
        body, name=name,
        out_shape=[pltpu.HBM(t.shape, t.dtype) for t in list(srcs) + list(lands)],
        in_specs=[_HBM] * (2 * n) + [_SEM, _SEM] + [pl.BlockSpec(memory_space=pl.ANY)] * len(after),
        out_specs=[_HBM] * (2 * n),
        input_output_aliases={i: i for i in range(2 * n)},
        compiler_params=pltpu.CompilerParams(has_side_effects=_EFFECT),
    )(*srcs, *lands, send_sems, recv_sems, *after)
    return res[n:]


def _split_start(make_copies, arrays, nsem, name, after=()):
    n, na = len(arrays), len(after)

    def body(*refs):
        for cp in make_copies(refs[:n], refs[n + na], refs[n + na + 1]):
            cp.start()
        refs[-1][...] = jnp.zeros_like(refs[-1])

    res = pl.pallas_call(
        body, name=name,
        out_shape=[pltpu.SemaphoreType.DMA((nsem,)), pltpu.SemaphoreType.DMA((nsem,))]
        + [pltpu.HBM(t.shape, t.dtype) for t in arrays] + [jax.ShapeDtypeStruct((8, 128), f32)],
        in_specs=[_HBM] * n + [pl.BlockSpec(memory_space=pl.ANY)] * na,
        out_specs=[_SEM, _SEM] + [_HBM] * n + [pl.BlockSpec(memory_space=pltpu.VMEM)],
        input_output_aliases={i: i + 2 for i in range(n)},
        compiler_params=pltpu.CompilerParams(has_side_effects=_EFFECT),
    )(*[pltpu.with_memory_space_constraint(t, pltpu.HBM) for t in arrays], *after)
    return (res[0], res[1], res[2:2 + n]), res[-1]


def _split_wait(make_copies, send_sems, recv_sems, arrays, after, name):
    n = len(arrays)

    def body(*refs):
        for cp in make_copies(refs[:n], refs[n], refs[n + 1]):
            cp.wait_send()
            cp.wait_recv()

    return pl.pallas_call(
        body, name=name,
        out_shape=[pltpu.HBM(t.shape, t.dtype) for t in arrays],
        in_specs=[_HBM] * n + [_SEM, _SEM] + [pl.BlockSpec(memory_space=pl.ANY)] * len(after),
        out_specs=[_HBM] * n, input_output_aliases={i: i for i in range(n)},
        compiler_params=pltpu.CompilerParams(has_side_effects=_EFFECT),
    )(*arrays, send_sems, recv_sems, *after)


def _sibling():
    return lax.axis_index("x"), lax.axis_index("y"), 1 - lax.axis_index("c")


def _forward_copies(lands, send_sems, recv_sems):
    c = lax.axis_index("c")
    return [pltpu.make_async_remote_copy(
        src_ref=lands[a].at[2 * cx + cy, :, c], dst_ref=lands[a].at[2 * cx + cy, :, c], send_sem=send_sems.at[3 * a + j],
        recv_sem=recv_sems.at[3 * a + j], device_id=_sibling(), device_id_type=MESH)
        for a in range(len(lands)) for j, (cx, cy) in enumerate(_other_chips())]


def _swap_copies(fulls, send_sems, recv_sems):
    c = lax.axis_index("c")
    return [pltpu.make_async_remote_copy(src_ref=t.at[:, c], dst_ref=t.at[:, c], send_sem=send_sems.at[a],
                                         recv_sem=recv_sems.at[a], device_id=_sibling(), device_id_type=MESH)
            for a, t in enumerate(fulls)]


def _pair_copies(refs, send_sems, recv_sems):
    n = len(refs) // 2
    c = lax.axis_index("c")
    return [pltpu.make_async_remote_copy(src_ref=refs[a].at[:, :, 1 - c], dst_ref=refs[n + a], send_sem=send_sems.at[a],
                                         recv_sem=recv_sems.at[a], device_id=_sibling(), device_id_type=MESH)
            for a in range(n)]


def _scatter_copies(srcs, lands, send_sems, recv_sems):
    c = lax.axis_index("c")
    return [pltpu.make_async_remote_copy(
        src_ref=srcs[a].at[2 * cx + cy], dst_ref=lands[a].at[j], send_sem=send_sems.at[3 * a + j],
        recv_sem=recv_sems.at[3 * a + j], device_id=(cx, cy, c), device_id_type=MESH)
        for a in range(len(srcs)) for j, (cx, cy) in enumerate(_other_chips())]


def _scatter_start(hsums, name, after=()):
    n = len(hsums)
    na = len(after)

    def body(*refs):
        srcs, lands = refs[:n], refs[n:2 * n]
        send_sems, recv_sems = refs[2 * n + na], refs[2 * n + na + 1]
        for cp in _scatter_copies(srcs, lands, send_sems, recv_sems):
            cp.start()
        refs[-1][...] = jnp.zeros_like(refs[-1])

    lands = [lax.empty((3,) + g.shape[1:], g.dtype) for g in hsums]
    res = pl.pallas_call(
        body, name=name,
        out_shape=[pltpu.SemaphoreType.DMA((3 * n,)), pltpu.SemaphoreType.DMA((3 * n,))]
        + [pltpu.HBM(g.shape, g.dtype) for g in hsums] + [pltpu.HBM(g.shape, g.dtype) for g in lands]
        + [jax.ShapeDtypeStruct((8, 128), f32)],
        in_specs=[_HBM] * (2 * n) + [pl.BlockSpec(memory_space=pl.ANY)] * na,
        out_specs=[_SEM, _SEM] + [_HBM] * (2 * n) + [pl.BlockSpec(memory_space=pltpu.VMEM)],
        input_output_aliases={i: i + 2 for i in range(2 * n)},
        compiler_params=pltpu.CompilerParams(has_side_effects=_EFFECT),
    )(*[pltpu.with_memory_space_constraint(t, pltpu.HBM) for t in list(hsums) + lands], *after)
    return (res[0], res[1], res[2:2 + n], res[2 + n:2 + 2 * n]), res[-1]


def _scatter_wait(send_sems, recv_sems, srcs, lands, after, name):
    n = len(srcs)
    extra = list(after)

    def body(*refs):
        s_refs, l_refs = refs[:n], refs[n:2 * n]
        ss, rs = refs[2 * n], refs[2 * n + 1]
        for cp in _scatter_copies(s_refs, l_refs, ss, rs):
            cp.wait_send()
            cp.wait_recv()

    res = pl.pallas_call(
        body, name=name,
        out_shape=[pltpu.HBM(g.shape, g.dtype) for g in srcs] + [pltpu.HBM(g.shape, g.dtype) for g in lands],
        in_specs=[_HBM] * (2 * n) + [_SEM, _SEM] + [pl.BlockSpec(memory_space=pl.ANY)] * len(extra),
        out_specs=[_HBM] * (2 * n),
        input_output_aliases={i: i for i in range(2 * n)},
        compiler_params=pltpu.CompilerParams(has_side_effects=_EFFECT),
    )(*srcs, *lands, send_sems, recv_sems, *extra)
    return res[:n], res[n:]


def _plane_copies(src, land, send_sems, recv_sems):
    x, y, c = lax.axis_index("x"), lax.axis_index("y"), lax.axis_index("c")
    return [pltpu.make_async_remote_copy(src_ref=src, dst_ref=land.at[2 * x + y, c], send_sem=send_sems.at[j],
                                         recv_sem=recv_sems.at[j], device_id=(cx, cy, c), device_id_type=MESH)
            for j, (cx, cy) in enumerate(_other_chips())]


def _plane_start(pack, land, name):
    def body(src, lnd, send_sems, recv_sems, _s, _l, token):
        for cp in _plane_copies(src, lnd, send_sems, recv_sems):
            cp.start()
        token[...] = jnp.zeros_like(token)

    res = pl.pallas_call(
        body, name=name,
        out_shape=[pltpu.SemaphoreType.DMA((3,)), pltpu.SemaphoreType.DMA((3,)), pltpu.HBM(pack.shape, pack.dtype),
                   pltpu.HBM(land.shape, land.dtype), jax.ShapeDtypeStruct((8, 128), f32)],
        in_specs=[_HBM, _HBM], out_specs=[_SEM, _SEM, _HBM, _HBM, pl.BlockSpec(memory_space=pltpu.VMEM)],
        input_output_aliases={0: 2, 1: 3},
        compiler_params=pltpu.CompilerParams(has_side_effects=_EFFECT),
    )(pltpu.with_memory_space_constraint(pack, pltpu.HBM), pltpu.with_memory_space_constraint(land, pltpu.HBM))
    return res[:4], res[4]


def _plane_wait(send_sems, recv_sems, pack, land, after, name):
    def body(src, lnd, ss, rs, *_):
        for cp in _plane_copies(src, lnd, ss, rs):
            cp.wait_send()
            cp.wait_recv()

    return pl.pallas_call(
        body, name=name,
        out_shape=[pltpu.HBM(pack.shape, pack.dtype), pltpu.HBM(land.shape, land.dtype)],
        in_specs=[_HBM, _HBM, _SEM, _SEM] + [pl.BlockSpec(memory_space=pl.ANY)] * len(after),
        out_specs=[_HBM, _HBM], input_output_aliases={0: 0, 1: 1},
        compiler_params=pltpu.CompilerParams(has_side_effects=_EFFECT),
    )(pack, land, send_sems, recv_sems, *after)[1]


def _swap_halves(fulls):
    n = len(fulls)

    def body(*refs):
        ins, outs = refs[:n], refs[n:2 * n]
        send_sems, recv_sems = refs[2 * n:]
        c = lax.axis_index("c")
        sibling = (lax.axis_index("x"), lax.axis_index("y"), 1 - c)
        copies = []
        for a in range(n):
            cp = pltpu.make_async_remote_copy(src_ref=outs[a].at[:, c], dst_ref=outs[a].at[:, c], send_sem=send_sems.at[a],
                                              recv_sem=recv_sems.at[a], device_id=sibling, device_id_type=MESH)
            cp.start()
            copies.append(cp)
        for a, cp in enumerate(copies):
            cp.wait_send()
            theirs = outs[a].at[:, 1 - c]
            pltpu.make_async_remote_copy(src_ref=theirs, dst_ref=theirs, send_sem=send_sems.at[a], recv_sem=recv_sems.at[a],
                                         device_id=sibling, device_id_type=MESH).wait_recv()

    hbm = pl.BlockSpec(memory_space=pl.ANY)
    return pl.pallas_call(
        body, name="swap_halves",
        out_shape=[jax.ShapeDtypeStruct(p.shape, p.dtype) for p in fulls],
        in_specs=[hbm] * n, out_specs=[hbm] * n,
        input_output_aliases={a: a for a in range(n)},
        scratch_shapes=[pltpu.SemaphoreType.DMA((n,)), pltpu.SemaphoreType.DMA((n,))],
    )(*fulls)


def _to_segments(t):
    s, c = t.shape
    return t.reshape(SCAN_SEG, s // SCAN_SEG, c).transpose(1, 0, 2).reshape(s, c)


def _from_segments(t):
    s, c = t.shape
    return t.reshape(s // SCAN_SEG, SCAN_SEG, c).transpose(1, 0, 2).reshape(s, c)


def _ssm_operators(a_re, a_im, log_dt, b_re, b_im, c_re, c_im):
    lam = lax.complex(a_re, a_im)
    dt = jnp.exp(log_dt)[:, None]
    a_bar = jnp.exp(lam * dt)
    b_bar = ((a_bar - 1.0) / lam)[:, :, None] * lax.complex(b_re, b_im)
    eye = jnp.eye(N_GROUPS, dtype=f32)

    def embed_b(t):
        return (jnp.transpose(t, (0, 2, 1))[:, :, None, :] * eye[:, None, :, None]).reshape(D_SSM, D_STATE)

    def embed_c(t):
        return (jnp.transpose(t, (0, 2, 1))[:, :, None, :] * eye[:, None, :, None]).reshape(D_STATE, D_SSM)

    a2 = jnp.stack([a_bar.real.reshape(D_STATE), a_bar.imag.reshape(D_STATE)])
    return a2, embed_b(b_bar.real), embed_b(b_bar.imag), embed_c(c_re), embed_c(c_im)


def _local_step(x, target, mod, small, ffn_weights, mix_weights, grads_done, ffn_bwd_issued):
    table = jnp.asarray(_bucket_table())
    bias = small["att_bias"]
    L = DEPTH
    saved = []
    ssm_names = ("ssm_a_re", "ssm_a_im", "ssm_log_dt", "ssm_b_re", "ssm_b_im", "ssm_c_re", "ssm_c_im")
    ssm_ops_vjp = []
    for l in range(L):
        sv = {}
        m9 = mod[l]
        sv["x0"] = x
        sv["w0"] = ffn_weights(l, 0, x)
        x, sv["f0"], sv["g0"], sv["u0"], sv["h0"] = _ffn_fwd(x, m9[0:3], *sv["w0"], 0, small["ln_g"][l, 0:1], small["ln_b"][l, 0:1])
        sv["x1"] = x
        sv["w1"] = mix_weights(l, x)
        *qkv, z_rest, sv["h1"] = _mix_in_fwd(x, m9[3:6], sv["w1"][0], 0)
        S = x.shape[0]
        qkv = [t.reshape(3, S, D_ATT) for t in qkv]
        att = [_att_fwd(qkv[b], bias, b) for b in range(3)]
        y_att, lse3 = _att_merge([att[b][0].reshape(d, S // d, D_ATT) for b, d in enumerate(DILATIONS)],
                                 [att[b][1].reshape(d, S // d, _LANES) for b, d in enumerate(DILATIONS)])
        sv.update(qkv=qkv, lse=[a[1] for a in att], lse3=lse3, y_att=y_att)

        ops, ops_vjp = jax.vjp(_ssm_operators, *[small[k][l] for k in ssm_names])
        ssm_ops_vjp.append(ops_vjp)
        a2, bre, bim, cre, cim = [t[None] for t in ops]
        u_ssm = _to_segments(z_rest[:, :D_SSM])
        sr, si = _ssm_states(u_ssm, bre, bim, a2, 0)
        dskip = small["ssm_d"][l][None, :]
        glu_b = small["glu_b"][l][None, :]
        out_seg, y_seg = _ssm_out(sr, si, u_ssm, cre, cim, 0, dskip, small["glu_w"][l], glu_b)
        y_ssm = _from_segments(out_seg)
        sv.update(ssm_ops=(a2, bre, bim, cre, cim), u_ssm=u_ssm, sr=sr, si=si, y_seg=y_seg, y_ssm=y_ssm)

        u_pool = jnp.concatenate([jnp.zeros((POOL_HALO, D_POOL), f32), z_rest[:, D_SSM:]])
        y_pool = _pool_fwd(u_pool, small["pool_w"][l], small["pool_scale"][l][None, :])
        sv.update(u_pool=u_pool, y_pool=y_pool)

        x, sv["ymix"] = _mix_out_fwd(x, y_att, y_ssm, y_pool, m9[3:6], sv["w1"][1], 0, small["ln_g"][l, 1:2], small["ln_b"][l, 1:2])
        sv["x2"] = x
        sv["w2"] = ffn_weights(l, 1, x)
        x, sv["f2"], sv["g2"], sv["u2"], sv["h2"] = _ffn_fwd(x, m9[6:9], *sv["w2"], 0, small["ln_g"][l, 2:3], small["ln_b"][l, 2:3])
        saved.append(sv)

    loss, dx = _loss_head(x, target)

    dmod = [None] * L
    dln_g = [None] * L
    dln_b = [None] * L
    sg = {k: [None] * L for k in ssm_names + ("ssm_d", "glu_w", "glu_b", "pool_w", "pool_scale")}
    dbias_tot = None
    order_after = jnp.zeros((), f32)
    for l in reversed(range(L)):
        sv = saved[l]
        m9 = mod[l] + order_after

        def fresh(like):
            return [lax.empty(t.shape, bf16) for t in like]

        dx, dg, du, a, df, dm2, dlg2, dlb2 = _ffn_bwd(dx, sv["x2"], sv["f2"], sv["g2"], sv["u2"], m9[6:9], *sv["w2"], 0,
                                                     small["ln_g"][l, 2:3])
        m9 = m9 + ffn_bwd_issued(l, 1, dx)
        g_ffn1 = _ffn_wgrad(sv["h2"], dg, du, a, df, *fresh(sv["w2"]), 0)
        dxr, d_att, d_ssm, d_pool, dgate1, dlg1, dlb1, g_w_out = _mix_out_bwd(
            dx, sv["x1"], sv["ymix"], sv["y_att"], sv["y_ssm"], sv["y_pool"], m9[3:6], sv["w1"][1], 0, small["ln_g"][l, 1:2],
            fresh(sv["w1"])[1])
        S = d_att.shape[0]
        merged = _att_merge_bwd(d_att, sv["y_att"], sv["lse3"])
        dqkv, dbias = [], []
        for b, d in enumerate(DILATIONS):
            dq_b, db_b = _att_bwd(sv["qkv"][b], merged[b].reshape(S, D_ATT), sv["lse"][b], merged[3 + b].reshape(S, _LANES), bias, b)
            dqkv.append(dq_b.reshape(3, d, S // d, D_ATT))
            dbias.append(db_b)
        dbias = jnp.stack(dbias)
        dbias_tot = dbias if dbias_tot is None else dbias_tot + dbias
        d_seg = _to_segments(d_ssm)
        dskip = small["ssm_d"][l][None, :]
        glu_b = small["glu_b"][l][None, :]
        dy_seg, du_skip, dcre, dcim, dd, dglu_b, dglu_w = _ssm_out_bwd(
            d_seg, sv["y_seg"], sv["u_ssm"], sv["sr"], sv["si"], dskip, small["glu_w"][l], glu_b)
        a2, bre, bim, cre, cim = sv["ssm_ops"]
        du_seg, dbre, dbim, da2 = _ssm_states_bwd(dy_seg, du_skip, sv["u_ssm"], sv["sr"], sv["si"], cre, cim, bre, bim, a2, 0)
        for k, t in zip(ssm_names, ssm_ops_vjp[l]((da2, dbre, dbim, dcre, dcim))):
            sg[k][l] = t
        sg["ssm_d"][l] = dd[0]
        sg["glu_b"][l] = dglu_b[0]
        sg["glu_w"][l] = dglu_w
        du_ssm = _from_segments(du_seg)
        dyp = jnp.concatenate([d_pool, jnp.zeros((POOL_HALO, D_POOL), f32)])
        du_pool, dpw, dps = _pool_bwd(dyp, sv["u_pool"], small["pool_w"][l], small["pool_scale"][l][None, :])
        sg["pool_w"][l] = dpw
        sg["pool_scale"][l] = dps[0]
        d_rest = jnp.concatenate([du_ssm, du_pool], axis=1).astype(bf16)
        dx, dm1, dz = _mix_in_bwd(dqkv, d_rest, dxr, sv["x1"], m9[3:6], sv["w1"][0], 0)
        g_w_in = _mix_in_wgrad(sv["h1"], dz, fresh(sv["w1"])[0], 0)
        ffn_names = ("ffn_w_gate", "ffn_w_up", "ffn_w_down")
        m9 = m9 + grads_done(l, 1, list(zip(ffn_names, [(2 * l + 1) * FF_SHARD] * 3, g_ffn1))
                             + [("w_in", l * D_MODEL, g_w_in), ("w_out", l * 256, g_w_out)])
        dm1 = jnp.concatenate([dm1[0:2], dgate1])
        dx, dg, du, a, df, dm0, dlg0, dlb0 = _ffn_bwd(dx, sv["x0"], sv["f0"], sv["g0"], sv["u0"], m9[0:3], *sv["w0"], 0,
                                                     small["ln_g"][l, 0:1])
        issued = ffn_bwd_issued(l, 0, dx)
        g_ffn0 = _ffn_wgrad(sv["h0"], dg, du, a, df, *fresh(sv["w0"]), 0)
        order_after = grads_done(l, 0, list(zip(ffn_names, [2 * l * FF_SHARD] * 3, g_ffn0))) + issued
        dmod[l] = jnp.concatenate([dm0 + issued, dm1, dm2])
        dln_g[l] = jnp.concatenate([dlg0, dlg1, dlg2])
        dln_b[l] = jnp.concatenate([dlb0, dlb1, dlb2])

    small_grads = {k: jnp.stack(v) for k, v in sg.items()}
    small_grads["rel_bias"] = _bias_bwd(dbias_tot, table)
    small_grads["ln_g"] = jnp.stack(dln_g)
    small_grads["ln_b"] = jnp.stack(dln_b)
    return loss, dx, jnp.stack(dmod), small_grads


_TILE_ELEMS = 8 * 128


def _pack_rows(shapes):
    out, row = [], 0
    for s in shapes:
        nr = -(-int(np.prod(s)) // _TILE_ELEMS) * 8
        out.append((row, nr))
        row += nr
    return out


def _pack(arrs):
    parts = []
    for a in arrs:
        flat = a.reshape(-1).astype(f32)
        npad = -(-flat.shape[0] // _TILE_ELEMS) * _TILE_ELEMS
        parts.append(jnp.pad(flat, (0, npad - flat.shape[0])).reshape(npad // 128, 128))
    return jnp.concatenate(parts, axis=0)


def _unpack(buf, shapes):
    return [buf[row:row + nr].reshape(-1)[:int(np.prod(s))].reshape(s) for s, (row, nr) in zip(shapes, _pack_rows(shapes))]


_REPL = ("rel_bias", "ada_b", "ssm_a_re", "ssm_a_im", "ssm_log_dt", "ssm_b_re", "ssm_b_im", "ssm_c_re", "ssm_c_im",
         "ssm_d", "glu_b", "pool_w", "pool_scale")
_SMALL_SHARDED = ("ln_g", "ln_b", "glu_w")
_BIG = ("ffn_w_gate", "ffn_w_up", "ffn_w_down", "w_in", "w_out")
_ORDER = ("rel_bias", "ada_w", "ada_b", "ln_g", "ln_b", "ffn_w_gate", "ffn_w_up", "ffn_w_down", "w_in", "w_out",
          "ssm_a_re", "ssm_a_im", "ssm_log_dt", "ssm_b_re", "ssm_b_im", "ssm_c_re", "ssm_c_im", "ssm_d", "glu_w",
          "glu_b", "pool_w", "pool_scale")


def kernel(x, c, rel_bias, ada_w, ada_b, ln_g, ln_b, ffn_w_gate, ffn_w_up, ffn_w_down, w_in, w_out, ssm_a_re, ssm_a_im, ssm_log_dt, ssm_b_re, ssm_b_im, ssm_c_re, ssm_c_im, ssm_d, glu_w, glu_b, pool_w, pool_scale, loss_target, m_rel_bias, m_ada_w, m_ada_b, m_ln_g, m_ln_b, m_ffn_w_gate, m_ffn_w_up, m_ffn_w_down, m_w_in, m_w_out, m_ssm_a_re, m_ssm_a_im, m_ssm_log_dt, m_ssm_b_re, m_ssm_b_im, m_ssm_c_re, m_ssm_c_im, m_ssm_d, m_glu_w, m_glu_b, m_pool_w, m_pool_scale, v_rel_bias, v_ada_w, v_ada_b, v_ln_g, v_ln_b, v_ffn_w_gate, v_ffn_w_up, v_ffn_w_down, v_w_in, v_w_out, v_ssm_a_re, v_ssm_a_im, v_ssm_log_dt, v_ssm_b_re, v_ssm_b_im, v_ssm_c_re, v_ssm_c_im, v_ssm_d, v_glu_w, v_glu_b, v_pool_w, v_pool_scale):
    args = dict(locals())
    w = {k: args[k] for k in _ORDER}
    m = {k: args["m_" + k] for k in _ORDER}
    v = {k: args["v_" + k] for k in _ORDER}
    L, D = DEPTH, D_MODEL
    ax, ay, ac = lax.axis_index("x"), lax.axis_index("y"), lax.axis_index("c")
    p_me = 2 * ax + ay
    dev = 4 * ax + 2 * ay + ac

    transposed = ("ffn_w_gate", "ffn_w_up")
    for d in (w, m, v):
        for name in transposed:
            d[name] = jnp.swapaxes(d[name], 2, 3)

    def halves(t):
        return t.astype(bf16).reshape(1, 2, t.shape[0] // 2, t.shape[1])

    def landing(src):
        return lax.dynamic_update_slice(lax.empty((N_CHIPS,) + src.shape, bf16), src[None], (p_me, 0, 0, 0, 0))

    chunk_keys = [("ffn", 0, 0), ("mix", 0), ("ffn", 0, 1), ("ffn", 1, 0), ("mix", 1), ("ffn", 1, 1)]
    chunk_srcs = []
    for key in chunk_keys:
        if key[0] == "ffn":
            chunk_srcs.append([halves(w[name][key[1], key[2]]) for name in ("ffn_w_gate", "ffn_w_up", "ffn_w_down")])
        else:
            chunk_srcs.append([halves(w_in[key[1]]), halves(w_out[key[1]])])

    pack = _pack([c, ln_g, ln_b, glu_w])
    rows = pack.shape[0]
    allp = _allgather8(pack).reshape(8, rows, 128)
    chunks = [(srcs, [landing(t) for t in srcs]) for srcs in chunk_srcs]
    first_in_flight, first_begun = _gather_start(chunks[:1], allp, "gather_start_first")
    c_all = allp[:, :8].reshape(8, D) + first_begun[0, 0]
    by_chip = allp[0::2]

    fwd_rows = _pack_rows([c.shape, ln_g.shape, ln_b.shape, glu_w.shape])

    def sharded(part, shape, axis):
        row0, nrows = fwd_rows[part]
        t = by_chip[:, row0:row0 + nrows].reshape(N_CHIPS, -1)[:, :int(np.prod(shape))].reshape((N_CHIPS,) + shape)
        return jnp.concatenate([t[p] for p in range(N_CHIPS)], axis=axis)

    ln_g_full = sharded(1, ln_g.shape, 2)
    ln_b_full = sharded(2, ln_b.shape, 2)
    glu_w_full = sharded(3, glu_w.shape, 1)

    ncol = ada_w.shape[-1]
    ada_b_cols = lax.dynamic_slice_in_dim(ada_b, p_me * ncol, ncol, axis=1)[:, None, :]
    mod_part = _ada_fwd(c_all, ada_w, ada_b_cols)
    mrows = L * 8 * ncol // 128
    mod_pack = mod_part.reshape(mrows, 128)
    mod_land = lax.dynamic_update_slice(lax.empty((N_CHIPS, 2, mrows, 128), f32), mod_pack[None, None], (p_me, ac, 0, 0))
    mod_in_flight, _ = _plane_start(mod_pack, mod_land, "mod_start")
    att_bias = _bias_fwd(rel_bias, jnp.asarray(_bucket_table()))
    small_names = _REPL + _SMALL_SHARDED
    small_packs = [_pack([d[k] for k in small_names]) for d in (w, m, v)]
    mod_land = _plane_wait(*mod_in_flight, [att_bias] + small_packs + [t for _, lands in chunks[1:] for t in lands], "mod_wait")
    mod_all = _swap_halves([mod_land])[0].reshape(8, L, 8, ncol)
    mod_mine = lax.dynamic_index_in_dim(mod_all, dev, axis=2, keepdims=False)
    mod = jnp.concatenate([mod_mine[2 * p] for p in range(N_CHIPS)], axis=-1).reshape(L, 9, D)

    rest_in_flight, rest_begun = _gather_start(chunks[1:], mod, "gather_start_rest")
    in_flight = first_in_flight + rest_in_flight

    forwarding = {}

    def forward(k, after):
        lands = _gather_wait(*in_flight[k], [after, rest_begun], "gather_wait_%d" % k)
        forwarding[k], begun = _split_start(_forward_copies, lands, 3 * len(lands), "gather_forward_start_%d" % k)
        return begun

    def gathered(key, after):
        k = chunk_keys.index(key)
        order = [after]
        if k not in forwarding:
            order.append(forward(k, after))
        if 3 <= k + 1 < len(chunk_keys):
            order.append(forward(k + 1, after))
        lands = _split_wait(_forward_copies, *forwarding[k], order, "gather_forward_wait_%d" % k)
        return [t.reshape(N_CHIPS, 1, 2 * t.shape[3], t.shape[4]) for t in lands]

    pc = jnp.stack([p_me, ac]).astype(jnp.int32)
    groups = {}
    scattering = {}

    pairing = {}

    def start_pairs(tag, after=()):
        g5 = [g.reshape(g.shape[:2] + (2, g.shape[2] // 2, g.shape[3])) for _, _, g in groups[tag]]
        gots = [lax.empty(g.shape[:2] + g.shape[3:], bf16) for g in g5]
        pairing[tag], begun = _split_start(_pair_copies, g5 + gots, len(g5), "pair_exchange_start_%s" % tag, after)
        return begun

    def start_group(tag, after):
        arrays = _split_wait(_pair_copies, *pairing[tag], after, "pair_exchange_wait_%s" % tag)
        n = len(arrays) // 2
        hsum = _pair_sum(arrays[:n], arrays[n:], pc)
        scattering[tag], begun = _scatter_start(hsum, "scatter_start_%s" % tag)
        return begun

    def grads_done(l, s, grads):
        if l == 1:
            groups.setdefault("l1", []).extend(grads)
            return start_pairs("l1")[0, 0] if s == 0 else jnp.zeros((), f32)
        groups["l0a" if s == 1 else "l0b"] = grads
        return start_pairs("l0a")[0, 0] if s == 1 else jnp.zeros((), f32)

    def ffn_bwd_issued(l, s, dx):
        if l == 0:
            return start_group("l1" if s == 1 else "l0a", [dx])[0, 0]
        return jnp.zeros((), f32)

    small = {k: w[k] for k in _REPL if k != "ada_b"}
    small.update(ln_g=ln_g_full, ln_b=ln_b_full, glu_w=glu_w_full, att_bias=att_bias)
    loss_dev, grad_x, dmod, sgrads = _local_step(
        x[0], loss_target[0], mod, small, lambda l, s, after: gathered(("ffn", l, s), after),
        lambda l, after: gathered(("mix", l), after), grads_done, ffn_bwd_issued)

    names = ("rel_bias", "ln_g", "ln_b", "ssm_a_re", "ssm_a_im", "ssm_log_dt", "ssm_b_re", "ssm_b_im", "ssm_c_re",
             "ssm_c_im", "ssm_d", "glu_w", "glu_b", "pool_w", "pool_scale")
    gpack = _pack([dmod] + [sgrads[k] for k in names] + [loss_dev])
    grows = gpack.shape[0]
    land = lax.dynamic_update_slice(lax.empty((N_CHIPS, 2, grows, 128), f32), gpack[None, None], (p_me, ac, 0, 0))
    small_in_flight, small_begun = _plane_start(gpack, land, "small_grads_start")
    l0b_begun = start_group("l0b", [start_pairs("l0b", (small_begun,))])

    out_g, out_d, out_m, out_v = {}, {}, {}, {}
    row_tile = dict(zip(_BIG, (352, 352, 352, 256, 256)))
    big = {name: None for name in _BIG}

    swapping = {}

    def reduce_group(tag, after):
        hsum, recv = _scatter_wait(*scattering[tag], after, "scatter_wait_%s" % tag)
        full = _sum_shards(hsum, recv, pc)
        swapping[tag], begun = _split_start(_swap_copies, full, len(full), "swap_halves_start_%s" % tag)
        return [begun]

    def update_group(tag, after):
        full = _split_wait(_swap_copies, *swapping[tag], after, "swap_halves_wait_%s" % tag)
        for (name, row0, _), g in zip(groups[tag], full):
            shp = w[name].shape
            r2 = (int(np.prod(shp[:-1])), shp[-1])
            big[name] = _adamw(w[name].reshape(r2), m[name].reshape(r2), v[name].reshape(r2), g.reshape(-1, shp[-1]),
                               row_tile[name], row0, big[name])
        return [big[name][1] for name, _, _ in groups[tag]]

    after = reduce_group("l0a", reduce_group("l1", [grad_x, l0b_begun]))
    after = update_group("l0a", update_group("l1", after))

    land = _plane_wait(*small_in_flight, after, "small_grads_wait")
    gall = _swap_halves([land])[0].reshape(8, grows, 128)
    gsum = _unpack(_sum8(gall), [(L, 9 * D)] + [sgrads[k].shape for k in names] + [(1, 1)])
    loss = gsum[-1][0, 0]
    red = dict(zip(("ada_b",) + names, gsum))
    red["ln_g"] = lax.dynamic_slice_in_dim(red["ln_g"], p_me * 256, 256, axis=2)
    red["ln_b"] = lax.dynamic_slice_in_dim(red["ln_b"], p_me * 256, 256, axis=2)
    red["glu_w"] = lax.dynamic_slice_in_dim(red["glu_w"], p_me * 64, 64, axis=1)

    dmod_all = gall[:, :L * 9 * D // 128].reshape(8, L, 9 * D)
    dmod_cols = jnp.transpose(lax.dynamic_slice_in_dim(dmod_all, p_me * ncol, ncol, axis=2), (1, 0, 2))
    g_ada_w = _ada_wgrad(jnp.transpose(c_all), dmod_cols)

    r2 = (L * D, ncol)
    res = _adamw(ada_w.reshape(r2), m["ada_w"].reshape(r2), v["ada_w"].reshape(r2), g_ada_w.reshape(r2), 128)
    out_g["ada_w"], out_d["ada_w"], out_m["ada_w"], out_v["ada_w"] = [t.reshape(ada_w.shape) for t in res]

    res_small = _adamw(*small_packs, _pack([red[k] for k in small_names]), small_packs[0].shape[0])
    for t, dst in zip(res_small, (out_g, out_d, out_m, out_v)):
        for k, a in zip(small_names, _unpack(t, [w[k].shape for k in small_names])):
            dst[k] = a

    update_group("l0b", reduce_group("l0b", [res_small[1], res[1]]))
    for name in _BIG:
        res = [t.reshape(w[name].shape) for t in big[name]]
        out_g[name], out_d[name], out_m[name], out_v[name] = [jnp.swapaxes(t, 2, 3) for t in res] if name in transposed else res

    return (loss, grad_x[None], *[out_g[k] for k in _ORDER], *[out_d[k] for k in _ORDER],
            *[out_m[k] for k in _ORDER], *[out_v[k] for k in _ORDER])
```

```python
import math

import numpy as np
import jax
import jax.numpy as jnp
from jax import lax
from jax.experimental import pallas as pl
from jax.experimental.pallas import tpu as pltpu

f32 = jnp.float32
bf16 = jnp.bfloat16
MESH = pl.DeviceIdType.MESH

D_MODEL = 1024
SEQ = 2048
DEPTH = 2
HEAD_DIM = 64
N_HEADS = 8
D_ATT = 512
DILATIONS = (1, 4, 16)
BLOCKS_PER_RESIDUE = (16, 4, 1)
ATT_BLOCK = 128
N_UNITS = SEQ // ATT_BLOCK
N_GROUPS = 16
SSM_STATE = 64
D_SSM = 256
D_STATE = N_GROUPS * SSM_STATE
POOL_WINDOWS = (2, 4, 8, 16)
POOL_GROUP = 64
D_POOL = 256
POOL_HALO = 16
D_FF = 2816
N_BUCKETS = 32
MAX_DISTANCE = 2048
ALPHA = (2 * DEPTH) ** 0.25
FFN_RES = 0.5
LN_EPS = 1e-5
NEG = -1e30
N_CHIPS = 4
FF_SHARD = D_FF // N_CHIPS
SCAN_SEG = 8

ADAM_LR, ADAM_B1, ADAM_B2, ADAM_EPS, ADAM_WD, ADAM_STEP = 0.001, 0.9, 0.999, 1e-08, 0.01, 10

TOK_TILE = 512


def _cp(dims=None, vmem_mb=None):
    kw = {}
    if dims is not None:
        kw["dimension_semantics"] = dims
    if vmem_mb is not None:
        kw["vmem_limit_bytes"] = vmem_mb << 20
    return pltpu.CompilerParams(**kw)


def _dot(a, b):
    return jnp.dot(a, b, preferred_element_type=f32)


def _dot_nt(a, b):
    return lax.dot_general(a, b, (((1,), (1,)), ((), ())), preferred_element_type=f32)


def _dot_tn(a, b):
    return lax.dot_general(a, b, (((0,), (0,)), ((), ())), preferred_element_type=f32)


def _ln_stats(v):
    mu = jnp.mean(v, -1, keepdims=True)
    d = v - mu
    var = jnp.mean(d * d, -1, keepdims=True)
    rstd = lax.rsqrt(var + LN_EPS)
    return d * rstd, rstd


def _ln_bwd(dxh, xh, rstd):
    return rstd * (dxh - jnp.mean(dxh, -1, keepdims=True) - xh * jnp.mean(dxh * xh, -1, keepdims=True))


_GELU_C = math.sqrt(2.0 / math.pi)


def _gelu(y):
    return 0.5 * y * (1.0 + jnp.tanh(_GELU_C * (y + 0.044715 * y * y * y)))


def _gelu_grad(y):
    t = jnp.tanh(_GELU_C * (y + 0.044715 * y * y * y))
    return 0.5 * (1.0 + t) + 0.5 * y * (1.0 - t * t) * (_GELU_C * (1.0 + 3 * 0.044715 * y * y))


def _full(shape):
    return pl.BlockSpec(shape, lambda *_: (0,) * len(shape))


def _hbm(*args):
    return [pltpu.with_memory_space_constraint(a, pltpu.HBM) if getattr(a, "ndim", 0) >= 2 else a for a in args]


def _ffn_fwd(x, mod3, wg, wu, wd, ls, lng, lnb):
    S, D = x.shape
    Fs = wg.shape[-2]
    ts = TOK_TILE

    def body(x_ref, mod_ref, wg_ref, wu_ref, wd_ref, lng_ref, lnb_ref, xo_ref, f_ref, g_ref, u_ref, h_ref, acc_sc):
        j = pl.program_id(1)

        @pl.when(j == 0)
        def _():
            xh, _ = _ln_stats(x_ref[...])
            h_ref[...] = (xh * (1.0 + mod_ref[1:2, :]) + mod_ref[0:1, :]).astype(bf16)
            acc_sc[...] = jnp.zeros_like(acc_sc)

        h = h_ref[...]
        g = _dot_nt(h, wg_ref[0, 0])
        u = _dot_nt(h, wu_ref[0, 0])
        g_ref[0] = g.astype(bf16)
        u_ref[0] = u.astype(bf16)
        a = (g * jax.nn.sigmoid(g) * u).astype(bf16)
        acc_sc[...] += _dot(a, wd_ref[0, 0])

        @pl.when(j == N_CHIPS - 1)
        def _():
            f = acc_sc[...]
            f_ref[...] = f
            r = ALPHA * x_ref[...] + (FFN_RES * mod_ref[2:3, :]) * f
            rh, _ = _ln_stats(r)
            xo_ref[...] = rh * lng_ref[...] + lnb_ref[...]

    tok = pl.BlockSpec((ts, D), lambda i, j: (i, 0))
    wrow = pl.BlockSpec((1, 1, Fs, D), lambda i, j: (j, ls, 0, 0))
    hid = pl.BlockSpec((1, ts, Fs), lambda i, j: (j, i, 0))
    return pl.pallas_call(
        body, name="ffn_fwd", grid=(S // ts, N_CHIPS),
        in_specs=[tok, _full((3, D)), wrow, wrow, wrow, _full((1, D)), _full((1, D))],
        out_specs=[tok, tok, hid, hid, tok],
        out_shape=[jax.ShapeDtypeStruct((S, D), f32), jax.ShapeDtypeStruct((S, D), f32),
                   jax.ShapeDtypeStruct((N_CHIPS, S, Fs), bf16), jax.ShapeDtypeStruct((N_CHIPS, S, Fs), bf16),
                   jax.ShapeDtypeStruct((S, D), bf16)],
        scratch_shapes=[pltpu.VMEM((ts, D), f32)],
        compiler_params=_cp(("parallel", "arbitrary"), 56),
    )(*_hbm(x, mod3, wg, wu, wd, lng, lnb))


def _ffn_bwd(dxo, x, f, g, u, mod3, wg, wu, wd, ls, lng):
    S, D = x.shape
    Fs = wg.shape[-2]
    ts = TOK_TILE

    def body(dxo_ref, x_ref, f_ref, g_ref, u_ref, mod_ref, wg_ref, wu_ref, wd_ref, lng_ref,
             dx_ref, dg_ref, du_ref, a_ref, df_ref, dmod_ref, dlng_ref, dlnb_ref,
             dr_sc, df_sc, acc_sc):
        i = pl.program_id(0)
        j = pl.program_id(1)

        @pl.when((i == 0) & (j == 0))
        def _():
            dmod_ref[...] = jnp.zeros_like(dmod_ref)
            dlng_ref[...] = jnp.zeros_like(dlng_ref)
            dlnb_ref[...] = jnp.zeros_like(dlnb_ref)

        @pl.when(j == 0)
        def _():
            xv = x_ref[...]
            fv = f_ref[...]
            gate = mod_ref[2:3, :]
            rh, rstd = _ln_stats(ALPHA * xv + (FFN_RES * gate) * fv)
            dy = dxo_ref[...]
            dlng_ref[...] += jnp.sum(dy * rh, 0, keepdims=True)
            dlnb_ref[...] += jnp.sum(dy, 0, keepdims=True)
            dr = _ln_bwd(dy * lng_ref[...], rh, rstd)
            dr_sc[...] = dr
            dmod_ref[2:3, :] += jnp.sum(FFN_RES * dr * fv, 0, keepdims=True)
            df = ((FFN_RES * gate) * dr).astype(bf16)
            df_sc[...] = df
            df_ref[...] = df
            acc_sc[...] = jnp.zeros_like(acc_sc)

        da = _dot_nt(df_sc[...], wd_ref[0, 0])
        gv = g_ref[0].astype(f32)
        uv = u_ref[0].astype(f32)
        sg = jax.nn.sigmoid(gv)
        si = gv * sg
        a_ref[0] = (si * uv).astype(bf16)
        dgv = (da * uv * (sg * (1.0 + gv * (1.0 - sg)))).astype(bf16)
        duv = (da * si).astype(bf16)
        dg_ref[0] = dgv
        du_ref[0] = duv
        acc_sc[...] += _dot(dgv, wg_ref[0, 0]) + _dot(duv, wu_ref[0, 0])

        @pl.when(j == N_CHIPS - 1)
        def _():
            dh = acc_sc[...]
            xh, rstd0 = _ln_stats(x_ref[...])
            dmod_ref[0:1, :] += jnp.sum(dh, 0, keepdims=True)
            dmod_ref[1:2, :] += jnp.sum(dh * xh, 0, keepdims=True)
            dx_ref[...] = _ln_bwd(dh * (1.0 + mod_ref[1:2, :]), xh, rstd0) + ALPHA * dr_sc[...]

    tok = pl.BlockSpec((ts, D), lambda i, j: (i, 0))
    wrow = pl.BlockSpec((1, 1, Fs, D), lambda i, j: (j, ls, 0, 0))
    hid = pl.BlockSpec((1, ts, Fs), lambda i, j: (j, i, 0))
    hid_shape = jax.ShapeDtypeStruct((N_CHIPS, S, Fs), bf16)
    return pl.pallas_call(
        body, name="ffn_bwd", grid=(S // ts, N_CHIPS),
        in_specs=[tok, tok, tok, hid, hid, _full((3, D)), wrow, wrow, wrow, _full((1, D))],
        out_specs=[tok, hid, hid, hid, tok, _full((3, D)), _full((1, D)), _full((1, D))],
        out_shape=[jax.ShapeDtypeStruct((S, D), f32), hid_shape, hid_shape, hid_shape,
                   jax.ShapeDtypeStruct((S, D), bf16),
                   jax.ShapeDtypeStruct((3, D), f32), jax.ShapeDtypeStruct((1, D), f32), jax.ShapeDtypeStruct((1, D), f32)],
        scratch_shapes=[pltpu.VMEM((ts, D), f32), pltpu.VMEM((ts, D), bf16), pltpu.VMEM((ts, D), f32)],
        compiler_params=_cp(("arbitrary", "arbitrary"), 56),
    )(*_hbm(dxo, x, f, g, u, mod3, wg, wu, wd, lng))


def _ffn_wgrad(h, dg, du, a, df, gwg, gwu, gwd, ls):
    S, D = h.shape
    Fs = dg.shape[-1]
    tk = 2 * TOK_TILE
    nk = S // tk

    def body(h_ref, dg_ref, du_ref, a_ref, df_ref, _g0, _g1, _g2, gwg_ref, gwu_ref, gwd_ref, ag_sc, au_sc, ad_sc):
        k = pl.program_id(1)

        @pl.when(k == 0)
        def _():
            ag_sc[...] = jnp.zeros_like(ag_sc)
            au_sc[...] = jnp.zeros_like(au_sc)
            ad_sc[...] = jnp.zeros_like(ad_sc)

        hv = h_ref[...]
        ag_sc[...] += _dot_tn(dg_ref[0], hv)
        au_sc[...] += _dot_tn(du_ref[0], hv)
        ad_sc[...] += _dot_tn(a_ref[0], df_ref[...])

        @pl.when(k == nk - 1)
        def _():
            gwg_ref[0, 0] = ag_sc[...].astype(bf16)
            gwu_ref[0, 0] = au_sc[...].astype(bf16)
            gwd_ref[0, 0] = ad_sc[...].astype(bf16)

    tok = pl.BlockSpec((tk, D), lambda p, k: (k, 0))
    hid = pl.BlockSpec((1, tk, Fs), lambda p, k: (p, k, 0))
    anyspec = pl.BlockSpec(memory_space=pl.ANY)
    orow = pl.BlockSpec((1, 1, Fs, D), lambda p, k: (p, ls, 0, 0))
    return pl.pallas_call(
        body, name="ffn_wgrad", grid=(N_CHIPS, nk),
        in_specs=[tok, hid, hid, hid, tok, anyspec, anyspec, anyspec],
        out_specs=[orow, orow, orow],
        out_shape=[jax.ShapeDtypeStruct(gwg.shape, bf16), jax.ShapeDtypeStruct(gwu.shape, bf16),
                   jax.ShapeDtypeStruct(gwd.shape, bf16)],
        scratch_shapes=[pltpu.VMEM((Fs, D), f32), pltpu.VMEM((Fs, D), f32), pltpu.VMEM((Fs, D), f32)],
        input_output_aliases={5: 0, 6: 1, 7: 2},
        compiler_params=_cp(("parallel", "arbitrary"), 48),
    )(*_hbm(h, dg, du, a, df, gwg, gwu, gwd))


_LANES = 128
_QKV_BLOCKS = D_ATT // _LANES


def _res_spec(lead, d, width, index):
    return pl.BlockSpec((lead, d, TOK_TILE // d, width), index)


def _res_spec3(d, width):
    return pl.BlockSpec((d, TOK_TILE // d, width), lambda i: (0, i, 0))


def _rows_to_residues(tile_bufs, d, put):
    for r in range(d):
        for cb, buf in enumerate(tile_bufs):
            put(r, cb, buf[pl.ds(r, TOK_TILE // d, stride=d), :])


def _residues_to_rows(tile_bufs, d, get):
    for r in range(d):
        for cb, buf in enumerate(tile_bufs):
            buf[pl.ds(r, TOK_TILE // d, stride=d), :] = get(r, cb)


def _mix_in_fwd(x, mod3, w_in, l):
    S, D = x.shape
    N = w_in.shape[-1]
    ts = TOK_TILE

    def body(x_ref, mod_ref, w_ref, o1_ref, o4_ref, o16_ref, zr_ref, h_ref, *bufs):
        j = pl.program_id(1)

        @pl.when(j == 0)
        def _():
            xh, _ = _ln_stats(x_ref[...])
            h_ref[...] = (xh * (1.0 + mod_ref[1:2, :]) + mod_ref[0:1, :]).astype(bf16)

        z = _dot(h_ref[...], w_ref[0, 0])

        @pl.when(j == N_CHIPS - 1)
        def _():
            zr_ref[...] = z

        @pl.when(j < N_CHIPS - 1)
        def _():
            zz = z * jnp.where(j == 0, HEAD_DIM ** -0.5, 1.0)
            o1_ref[j, 0] = zz.astype(bf16)
            for cb, buf in enumerate(bufs):
                buf[...] = zz[:, _LANES * cb:_LANES * (cb + 1)]
            for d, o_ref in zip(DILATIONS[1:], (o4_ref, o16_ref)):
                def put(r, cb, piece, o_ref=o_ref):
                    o_ref[j, r, :, _LANES * cb:_LANES * (cb + 1)] = piece.astype(bf16)
                _rows_to_residues(bufs, d, put)

    tok = pl.BlockSpec((ts, D), lambda i, j: (i, 0))
    res = [_res_spec(3, d, N, lambda i, j: (0, 0, i, 0)) for d in DILATIONS]
    return pl.pallas_call(
        body, name="mix_in_fwd", grid=(S // ts, N_CHIPS),
        in_specs=[tok, _full((3, D)), pl.BlockSpec((1, 1, D, N), lambda i, j: (j, l, 0, 0))],
        out_specs=res + [pl.BlockSpec((ts, N), lambda i, j: (i, 0)), tok],
        out_shape=[jax.ShapeDtypeStruct((3, d, S // d, N), bf16) for d in DILATIONS]
        + [jax.ShapeDtypeStruct((S, N), f32), jax.ShapeDtypeStruct((S, D), bf16)],
        scratch_shapes=[pltpu.VMEM((ts, _LANES), f32)] * _QKV_BLOCKS,
        compiler_params=_cp(("parallel", "arbitrary"), 40),
    )(*_hbm(x, mod3, w_in))


def _mix_in_bwd(dqkv, d_rest, dx_res, x, mod3, w_in, l):
    S, D = x.shape
    N = w_in.shape[-1]
    ts = TOK_TILE

    def body(d1_ref, d4_ref, d16_ref, dr_ref, dxr_ref, x_ref, mod_ref, w_ref, dx_ref, dmod_ref, dz_ref, acc_sc, *bufs):
        i = pl.program_id(0)
        j = pl.program_id(1)

        @pl.when((i == 0) & (j == 0))
        def _():
            dmod_ref[...] = jnp.zeros_like(dmod_ref)

        @pl.when(j == 0)
        def _():
            acc_sc[...] = jnp.zeros_like(acc_sc)

        @pl.when(j == N_CHIPS - 1)
        def _():
            dz_ref[0] = dr_ref[...]

        @pl.when(j < N_CHIPS - 1)
        def _():
            for d, d_ref, tile_bufs in ((4, d4_ref, bufs[:_QKV_BLOCKS]), (16, d16_ref, bufs[_QKV_BLOCKS:])):
                _residues_to_rows(tile_bufs, d, lambda r, cb, d_ref=d_ref: d_ref[0, r, :, _LANES * cb:_LANES * (cb + 1)].astype(f32))
            for cb in range(_QKV_BLOCKS):
                cols = slice(_LANES * cb, _LANES * (cb + 1))
                dz_ref[0, :, cols] = (d1_ref[0, 0, :, cols].astype(f32) + bufs[cb][...] + bufs[_QKV_BLOCKS + cb][...]).astype(bf16)

        acc_sc[...] += _dot_nt(dz_ref[0], w_ref[0, 0])

        @pl.when(j == N_CHIPS - 1)
        def _():
            dh = acc_sc[...]
            xh, rstd0 = _ln_stats(x_ref[...])
            dmod_ref[0:1, :] += jnp.sum(dh, 0, keepdims=True)
            dmod_ref[1:2, :] += jnp.sum(dh * xh, 0, keepdims=True)
            dx_ref[...] = _ln_bwd(dh * (1.0 + mod_ref[1:2, :]), xh, rstd0) + dxr_ref[...]

    tok = pl.BlockSpec((ts, D), lambda i, j: (i, 0))
    res = [_res_spec(1, d, N, lambda i, j: (jnp.minimum(j, 2), 0, i, 0)) for d in DILATIONS]
    return pl.pallas_call(
        body, name="mix_in_bwd", grid=(S // ts, N_CHIPS),
        in_specs=res + [pl.BlockSpec((ts, N), lambda i, j: (i, 0)), tok, tok, _full((3, D)),
                        pl.BlockSpec((1, 1, D, N), lambda i, j: (j, l, 0, 0))],
        out_specs=[tok, _full((3, D)), pl.BlockSpec((1, ts, N), lambda i, j: (j, i, 0))],
        out_shape=[jax.ShapeDtypeStruct((S, D), f32), jax.ShapeDtypeStruct((3, D), f32),
                   jax.ShapeDtypeStruct((N_CHIPS, S, N), bf16)],
        scratch_shapes=[pltpu.VMEM((ts, D), f32)] + [pltpu.VMEM((ts, _LANES), f32)] * (2 * _QKV_BLOCKS),
        compiler_params=_cp(("arbitrary", "arbitrary"), 40),
    )(*_hbm(*dqkv, d_rest, dx_res, x, mod3, w_in))


def _mix_in_wgrad(h, dz, gw, l):
    S, D = h.shape
    N = dz.shape[-1]
    tk = TOK_TILE
    nk = S // tk

    def body(h_ref, dz_ref, _g, gw_ref, acc_sc):
        k = pl.program_id(1)

        @pl.when(k == 0)
        def _():
            acc_sc[...] = jnp.zeros_like(acc_sc)

        acc_sc[...] += _dot_tn(h_ref[...], dz_ref[0])

        @pl.when(k == nk - 1)
        def _():
            gw_ref[0, 0] = acc_sc[...].astype(bf16)

    return pl.pallas_call(
        body, name="mix_in_wgrad", grid=(N_CHIPS, nk),
        in_specs=[pl.BlockSpec((tk, D), lambda p, k: (k, 0)), pl.BlockSpec((1, tk, N), lambda p, k: (p, k, 0)),
                  pl.BlockSpec(memory_space=pl.ANY)],
        out_specs=pl.BlockSpec((1, 1, D, N), lambda p, k: (p, l, 0, 0)),
        out_shape=jax.ShapeDtypeStruct(gw.shape, bf16),
        scratch_shapes=[pltpu.VMEM((D, N), f32)],
        input_output_aliases={2: 0},
        compiler_params=_cp(("parallel", "arbitrary"), 40),
    )(*_hbm(h, dz, gw))


def _mix_out_fwd(x, y_att, y_ssm, y_pool, mod3, w_out, l, lng, lnb):
    S, D = x.shape
    ts = TOK_TILE

    def body(x_ref, ya_ref, ys_ref, yp_ref, mod_ref, w_ref, lng_ref, lnb_ref, xo_ref, y_ref):
        ya = ya_ref[...].astype(bf16)
        y = (_dot(ya[:, 0:256], w_ref[0, 0]) + _dot(ya[:, 256:512], w_ref[1, 0])
             + _dot(ys_ref[...].astype(bf16), w_ref[2, 0]) + _dot(yp_ref[...].astype(bf16), w_ref[3, 0]))
        y_ref[...] = y
        rh, _ = _ln_stats(ALPHA * x_ref[...] + mod_ref[2:3, :] * y)
        xo_ref[...] = rh * lng_ref[...] + lnb_ref[...]

    tok = pl.BlockSpec((ts, D), lambda i: (i, 0))
    return pl.pallas_call(
        body, name="mix_out_fwd", grid=(S // ts,),
        in_specs=[tok, pl.BlockSpec((ts, D_ATT), lambda i: (i, 0)), pl.BlockSpec((ts, D_SSM), lambda i: (i, 0)),
                  pl.BlockSpec((ts, D_POOL), lambda i: (i, 0)), _full((3, D)),
                  pl.BlockSpec((N_CHIPS, 1, 256, D), lambda i: (0, l, 0, 0)), _full((1, D)), _full((1, D))],
        out_specs=[tok, tok],
        out_shape=[jax.ShapeDtypeStruct((S, D), f32), jax.ShapeDtypeStruct((S, D), f32)],
        compiler_params=_cp(("parallel",), 40),
    )(*_hbm(x, y_att, y_ssm, y_pool, mod3, w_out, lng, lnb))


def _mix_out_bwd(dxo, x, y, y_att, y_ssm, y_pool, mod3, w_out, l, lng, gw_out):
    S, D = x.shape
    ts = TOK_TILE
    nt = S // ts

    def body(dxo_ref, x_ref, y_ref, ya_ref, ys_ref, yp_ref, mod_ref, w_ref, lng_ref, _g,
             dxr_ref, da_ref, ds_ref, dp_ref, dgate_ref, dlng_ref, dlnb_ref, gw_ref, acc_sc):
        i = pl.program_id(0)

        @pl.when(i == 0)
        def _():
            dgate_ref[...] = jnp.zeros_like(dgate_ref)
            dlng_ref[...] = jnp.zeros_like(dlng_ref)
            dlnb_ref[...] = jnp.zeros_like(dlnb_ref)
            acc_sc[...] = jnp.zeros_like(acc_sc)

        gate = mod_ref[2:3, :]
        yv = y_ref[...]
        rh, rstd = _ln_stats(ALPHA * x_ref[...] + gate * yv)
        dy_out = dxo_ref[...]
        dlng_ref[...] += jnp.sum(dy_out * rh, 0, keepdims=True)
        dlnb_ref[...] += jnp.sum(dy_out, 0, keepdims=True)
        dr = _ln_bwd(dy_out * lng_ref[...], rh, rstd)
        dxr_ref[...] = ALPHA * dr
        dgate_ref[...] += jnp.sum(dr * yv, 0, keepdims=True)
        dy = (gate * dr).astype(bf16)
        da_ref[:, 0:256] = _dot_nt(dy, w_ref[0, 0])
        da_ref[:, 256:512] = _dot_nt(dy, w_ref[1, 0])
        ds_ref[...] = _dot_nt(dy, w_ref[2, 0])
        dp_ref[...] = _dot_nt(dy, w_ref[3, 0])
        ya = ya_ref[...].astype(bf16)
        acc_sc[0] += _dot_tn(ya[:, 0:256], dy)
        acc_sc[1] += _dot_tn(ya[:, 256:512], dy)
        acc_sc[2] += _dot_tn(ys_ref[...].astype(bf16), dy)
        acc_sc[3] += _dot_tn(yp_ref[...].astype(bf16), dy)

        @pl.when(i == nt - 1)
        def _():
            gw_ref[:, 0] = acc_sc[...].astype(bf16)

    tok = pl.BlockSpec((ts, D), lambda i: (i, 0))
    t512 = pl.BlockSpec((ts, D_ATT), lambda i: (i, 0))
    t256 = pl.BlockSpec((ts, 256), lambda i: (i, 0))
    wspec = pl.BlockSpec((N_CHIPS, 1, 256, D), lambda i: (0, l, 0, 0))
    return pl.pallas_call(
        body, name="mix_out_bwd", grid=(nt,),
        in_specs=[tok, tok, tok, t512, t256, t256, _full((3, D)), wspec, _full((1, D)), pl.BlockSpec(memory_space=pl.ANY)],
        out_specs=[tok, t512, t256, t256, _full((1, D)), _full((1, D)), _full((1, D)), wspec],
        out_shape=[jax.ShapeDtypeStruct((S, D), f32), jax.ShapeDtypeStruct((S, D_ATT), f32),
                   jax.ShapeDtypeStruct((S, D_SSM), f32), jax.ShapeDtypeStruct((S, D_POOL), f32),
                   jax.ShapeDtypeStruct((1, D), f32), jax.ShapeDtypeStruct((1, D), f32), jax.ShapeDtypeStruct((1, D), f32),
                   jax.ShapeDtypeStruct(gw_out.shape, bf16)],
        scratch_shapes=[pltpu.VMEM((N_CHIPS, 256, D), f32)],
        input_output_aliases={9: 7},
        compiler_params=_cp(("arbitrary",), 48),
    )(*_hbm(dxo, x, y, y_att, y_ssm, y_pool, mod3, w_out, lng, gw_out))


def _t5_bucket(dist):
    max_exact = N_BUCKETS // 2
    d = np.maximum(dist, 1).astype(np.float32)
    large = max_exact + (np.log(d / max_exact) / math.log(MAX_DISTANCE / max_exact)
                         * (N_BUCKETS - max_exact)).astype(np.int32)
    large = np.minimum(large, N_BUCKETS - 1)
    return np.where(dist < max_exact, dist, large).astype(np.int32)


def _bucket_table():
    q = ATT_BLOCK
    i = np.arange(q)[:, None]
    j = np.arange(2 * q)[None, :]
    r = i + q - j
    in_band = (r >= 0) & (r <= q)
    tabs = [np.where(in_band, _t5_bucket(np.clip(r, 0, None) * d), -1) for d in DILATIONS]
    return np.stack(tabs).astype(np.int32)


def _bias_fwd(rel_bias, table):
    def body(rb_ref, tab_ref, out_ref):
        for b in range(3):
            tb = tab_ref[b]
            for h in range(N_HEADS):
                def pick(k, acc):
                    return jnp.where(tb == k, rb_ref[k, h], acc)
                out_ref[b, h] = lax.fori_loop(0, N_BUCKETS, pick, jnp.where(tb < 0, NEG, 0.0).astype(f32))

    return pl.pallas_call(
        body, name="bias_fwd",
        in_specs=[pl.BlockSpec(memory_space=pltpu.SMEM), pl.BlockSpec(memory_space=pltpu.VMEM)],
        out_specs=pl.BlockSpec(memory_space=pltpu.VMEM),
        out_shape=jax.ShapeDtypeStruct((3, N_HEADS, ATT_BLOCK, 2 * ATT_BLOCK), f32),
    )(rel_bias, table)


def _bias_bwd(dbias, table):
    def body(db_ref, tab_ref, out_ref):
        def per_bucket(k, c):
            for h in range(N_HEADS):
                tot = jnp.zeros((), f32)
                for b in range(3):
                    tot = tot + jnp.sum(jnp.where(tab_ref[b] == k, db_ref[b, h], 0.0))
                out_ref[k, h] = tot
            return c
        lax.fori_loop(0, N_BUCKETS, per_bucket, 0)

    return pl.pallas_call(
        body, name="bias_bwd",
        in_specs=[pl.BlockSpec(memory_space=pltpu.VMEM), pl.BlockSpec(memory_space=pltpu.VMEM)],
        out_specs=pl.BlockSpec(memory_space=pltpu.SMEM),
        out_shape=jax.ShapeDtypeStruct((N_BUCKETS, N_HEADS), f32),
    )(dbias, table)


def _att_unit(u, nbr):
    rows = pl.ds(pl.multiple_of(u * ATT_BLOCK, ATT_BLOCK), ATT_BLOCK)
    prev = pl.ds(pl.multiple_of(jnp.maximum(u - 1, 0) * ATT_BLOCK, ATT_BLOCK), ATT_BLOCK)
    return rows, prev, (u % nbr) != 0


_N_PAIRS = N_HEADS // 2


def _pair_rows(t):
    lane = lax.broadcasted_iota(jnp.int32, t.shape, 1)
    zero = jnp.zeros_like(t)
    return jnp.concatenate([jnp.where(lane < HEAD_DIM, t, zero), jnp.where(lane >= HEAD_DIM, t, zero)], axis=0)


def _pair_cols(big):
    lane = lax.broadcasted_iota(jnp.int32, (ATT_BLOCK, _LANES), 1)
    return jnp.where(lane < HEAD_DIM, big[:ATT_BLOCK], big[ATT_BLOCK:])


def _pair_column(ref, rows, hp):
    t = ref[rows, :]
    return jnp.concatenate([t[:, 2 * hp:2 * hp + 1], t[:, 2 * hp + 1:2 * hp + 2]], axis=0)


def _pair_band(ref, rows, prev, nbr, hp):
    lanes = pl.ds(_LANES * hp, _LANES)
    cur = ref[0, rows, lanes]
    return cur if nbr == 1 else jnp.concatenate([ref[0, prev, lanes], cur], axis=0)


def _pair_scores(q_ref, k_ref, b_ref, rows, prev, valid_prev, nbr, hp):
    qbd = _pair_rows(q_ref[0, rows, pl.ds(_LANES * hp, _LANES)])
    kb = _pair_band(k_ref, rows, prev, nbr, hp)
    bias = b_ref[0, 2 * hp:2 * hp + 2].reshape(2 * ATT_BLOCK, 2 * ATT_BLOCK)
    if nbr == 1:
        return qbd, kb, _dot_nt(qbd, kb) + bias[:, ATT_BLOCK:]
    s = _dot_nt(qbd, kb) + bias
    col = lax.broadcasted_iota(jnp.int32, s.shape, 1)
    return qbd, kb, jnp.where((col >= ATT_BLOCK) | valid_prev, s, NEG)


def _qkv_specs(S, branch):
    return ([pl.BlockSpec((1, S, D_ATT), lambda i, t=t: (t, 0, 0)) for t in range(3)],
            pl.BlockSpec((1, N_HEADS, ATT_BLOCK, 2 * ATT_BLOCK), lambda i: (branch, 0, 0, 0)))


def _att_fwd(qkv, bias, branch):
    S = qkv.shape[1]
    nbr = BLOCKS_PER_RESIDUE[branch]

    def body(q_ref, k_ref, v_ref, b_ref, o_ref, lse_ref):
        lse_ref[...] = jnp.zeros_like(lse_ref)

        def unit(u, c):
            rows, prev, valid_prev = _att_unit(u, nbr)
            for hp in range(_N_PAIRS):
                _, _, s = _pair_scores(q_ref, k_ref, b_ref, rows, prev, valid_prev, nbr, hp)
                m = jnp.max(s, -1, keepdims=True)
                p = jnp.exp(s - m)
                den = jnp.sum(p, -1, keepdims=True)
                big = _dot(p.astype(bf16), _pair_band(v_ref, rows, prev, nbr, hp))
                o_ref[rows, pl.ds(_LANES * hp, _LANES)] = _pair_cols(big / den)
                lse = m + jnp.log(den)
                lse_ref[rows, pl.ds(2 * hp, 1)] = lse[:ATT_BLOCK]
                lse_ref[rows, pl.ds(2 * hp + 1, 1)] = lse[ATT_BLOCK:]
            return c

        lax.fori_loop(0, N_UNITS, unit, 0)

    qkv_specs, bspec = _qkv_specs(S, branch)
    return pl.pallas_call(
        body, name="att_fwd", grid=(1,),
        in_specs=qkv_specs + [bspec],
        out_specs=[pl.BlockSpec((S, D_ATT), lambda i: (0, 0)), pl.BlockSpec((S, _LANES), lambda i: (0, 0))],
        out_shape=[jax.ShapeDtypeStruct((S, D_ATT), f32), jax.ShapeDtypeStruct((S, _LANES), f32)],
        compiler_params=_cp(("arbitrary",), 40),
    )(*_hbm(qkv, qkv, qkv, bias))


def _att_bwd(qkv, do, lse, crow, bias, branch):
    S = qkv.shape[1]
    nbr = BLOCKS_PER_RESIDUE[branch]

    def body(q_ref, k_ref, v_ref, do_ref, lse_ref, c_ref, b_ref, dqkv_ref, db_ref, dk_sc, dv_sc):
        dk_sc[...] = jnp.zeros_like(dk_sc)
        dv_sc[...] = jnp.zeros_like(dv_sc)
        db_ref[...] = jnp.zeros_like(db_ref)

        def unit(u, c):
            rows, prev, valid_prev = _att_unit(u, nbr)
            for hp in range(_N_PAIRS):
                lanes = pl.ds(_LANES * hp, _LANES)
                qbd, kb, s = _pair_scores(q_ref, k_ref, b_ref, rows, prev, valid_prev, nbr, hp)
                p = jnp.exp(s - _pair_column(lse_ref, rows, hp))
                dobd = _pair_rows(do_ref[rows, lanes])
                ds = p * (_dot_nt(dobd, _pair_band(v_ref, rows, prev, nbr, hp)) - _pair_column(c_ref, rows, hp))
                if nbr == 1:
                    db_ref[2 * hp:2 * hp + 2, :, ATT_BLOCK:] += ds.reshape(2, ATT_BLOCK, ATT_BLOCK)
                else:
                    db_ref[2 * hp:2 * hp + 2] += ds.reshape(2, ATT_BLOCK, 2 * ATT_BLOCK)
                dsb = ds.astype(bf16)
                dqkv_ref[0, rows, lanes] = (HEAD_DIM ** -0.5 * _pair_cols(_dot(dsb, kb))).astype(bf16)
                dkb = _dot_tn(dsb, qbd)
                dvb = _dot_tn(p.astype(bf16), dobd)
                if nbr == 1:
                    dk_sc[rows, lanes] += dkb
                    dv_sc[rows, lanes] += dvb
                else:
                    dk_sc[prev, lanes] += dkb[:ATT_BLOCK]
                    dv_sc[prev, lanes] += dvb[:ATT_BLOCK]
                    dk_sc[rows, lanes] += dkb[ATT_BLOCK:]
                    dv_sc[rows, lanes] += dvb[ATT_BLOCK:]
            return c

        lax.fori_loop(0, N_UNITS, unit, 0)
        dqkv_ref[1] = dk_sc[...].astype(bf16)
        dqkv_ref[2] = dv_sc[...].astype(bf16)

    qkv_specs, bspec = _qkv_specs(S, branch)
    row = pl.BlockSpec((S, _LANES), lambda i: (0, 0))
    return pl.pallas_call(
        body, name="att_bwd", grid=(1,),
        in_specs=qkv_specs + [pl.BlockSpec((S, D_ATT), lambda i: (0, 0)), row, row, bspec],
        out_specs=[pl.BlockSpec((3, S, D_ATT), lambda i: (0, 0, 0)),
                   pl.BlockSpec((N_HEADS, ATT_BLOCK, 2 * ATT_BLOCK), lambda i: (0, 0, 0))],
        out_shape=[jax.ShapeDtypeStruct((3, S, D_ATT), bf16), jax.ShapeDtypeStruct((N_HEADS, ATT_BLOCK, 2 * ATT_BLOCK), f32)],
        scratch_shapes=[pltpu.VMEM((S, D_ATT), f32), pltpu.VMEM((S, D_ATT), f32)],
        compiler_params=_cp(("arbitrary",), 48),
    )(*_hbm(qkv, qkv, qkv, do, lse, crow, bias))


def _branch_weights(lse_ref):
    l0, l1, l2 = lse_ref[0], lse_ref[1], lse_ref[2]
    m = jnp.maximum(jnp.maximum(l0, l1), l2)
    e0, e1, e2 = jnp.exp(l0 - m), jnp.exp(l1 - m), jnp.exp(l2 - m)
    tot = e0 + e1 + e2
    return e0 / tot, e1 / tot, e2 / tot


def _att_merge(os, lses):
    S = os[0].shape[0] * os[0].shape[1]
    ts = TOK_TILE

    def body(o1_ref, o4_ref, o16_ref, l1_ref, l4_ref, l16_ref, y_ref, lt_ref, *bufs):
        obufs = (bufs[:_QKV_BLOCKS], bufs[_QKV_BLOCKS:2 * _QKV_BLOCKS])
        lt_ref[0] = l1_ref[0]
        for k, (d, o_ref, l_ref) in enumerate(((4, o4_ref, l4_ref), (16, o16_ref, l16_ref))):
            _residues_to_rows(obufs[k], d, lambda r, cb, o_ref=o_ref: o_ref[r, :, _LANES * cb:_LANES * (cb + 1)])
            _residues_to_rows([bufs[2 * _QKV_BLOCKS + k]], d, lambda r, cb, l_ref=l_ref: l_ref[r])
            lt_ref[1 + k] = bufs[2 * _QKV_BLOCKS + k][...]
        w = _branch_weights(lt_ref)
        for h in range(N_HEADS):
            cs = slice(HEAD_DIM * h, HEAD_DIM * (h + 1))
            half = slice(HEAD_DIM * (h % 2), HEAD_DIM * (h % 2 + 1))
            y_ref[:, cs] = (w[0][:, h:h + 1] * o1_ref[0, :, cs] + w[1][:, h:h + 1] * obufs[0][h // 2][:, half]
                            + w[2][:, h:h + 1] * obufs[1][h // 2][:, half])

    return pl.pallas_call(
        body, name="att_merge", grid=(S // ts,),
        in_specs=[_res_spec3(d, D_ATT) for d in DILATIONS] + [_res_spec3(d, _LANES) for d in DILATIONS],
        out_specs=[pl.BlockSpec((ts, D_ATT), lambda i: (i, 0)), pl.BlockSpec((3, ts, _LANES), lambda i: (0, i, 0))],
        out_shape=[jax.ShapeDtypeStruct((S, D_ATT), f32), jax.ShapeDtypeStruct((3, S, _LANES), f32)],
        scratch_shapes=[pltpu.VMEM((ts, _LANES), f32)] * (2 * _QKV_BLOCKS + 2),
        compiler_params=_cp(("parallel",)),
    )(*_hbm(*os, *lses))


def _att_merge_bwd(dy, y, lse3):
    S = dy.shape[0]
    ts = TOK_TILE

    def body(dy_ref, y_ref, lse_ref, do1_ref, do4_ref, do16_ref, c1_ref, c4_ref, c16_ref, *bufs):
        dobufs = (bufs[:_QKV_BLOCKS], bufs[_QKV_BLOCKS:2 * _QKV_BLOCKS], bufs[2 * _QKV_BLOCKS:3 * _QKV_BLOCKS])
        cbufs = bufs[3 * _QKV_BLOCKS:]
        w = _branch_weights(lse_ref)
        for cb in cbufs:
            cb[...] = jnp.zeros_like(cb)
        for h in range(N_HEADS):
            cs = slice(HEAD_DIM * h, HEAD_DIM * (h + 1))
            half = slice(HEAD_DIM * (h % 2), HEAD_DIM * (h % 2 + 1))
            dyh = dy_ref[:, cs]
            t = jnp.sum(dyh * y_ref[:, cs], -1, keepdims=True)
            for p in range(3):
                wp = w[p][:, h:h + 1]
                dobufs[p][h // 2][:, half] = wp * dyh
                cbufs[p][:, h:h + 1] = wp * t
        for cb in range(_QKV_BLOCKS):
            do1_ref[0, :, _LANES * cb:_LANES * (cb + 1)] = dobufs[0][cb][...].astype(bf16)
        c1_ref[0] = cbufs[0][...]
        for k, (d, do_ref, c_ref) in enumerate(((4, do4_ref, c4_ref), (16, do16_ref, c16_ref))):
            def put_do(r, cb, piece, do_ref=do_ref):
                do_ref[r, :, _LANES * cb:_LANES * (cb + 1)] = piece.astype(bf16)

            def put_c(r, cb, piece, c_ref=c_ref):
                c_ref[r] = piece

            _rows_to_residues(dobufs[1 + k], d, put_do)
            _rows_to_residues([cbufs[1 + k]], d, put_c)

    return pl.pallas_call(
        body, name="att_merge_bwd", grid=(S // ts,),
        in_specs=[pl.BlockSpec((ts, D_ATT), lambda i: (i, 0)), pl.BlockSpec((ts, D_ATT), lambda i: (i, 0)),
                  pl.BlockSpec((3, ts, _LANES), lambda i: (0, i, 0))],
        out_specs=[_res_spec3(d, D_ATT) for d in DILATIONS] + [_res_spec3(d, _LANES) for d in DILATIONS],
        out_shape=[jax.ShapeDtypeStruct((d, S // d, D_ATT), bf16) for d in DILATIONS]
        + [jax.ShapeDtypeStruct((d, S // d, _LANES), f32) for d in DILATIONS],
        scratch_shapes=[pltpu.VMEM((ts, _LANES), f32)] * (3 * _QKV_BLOCKS + 3),
        compiler_params=_cp(("parallel",)),
    )(*_hbm(dy, y, lse3))


_SSM_ROWS = 256


def _scan_in_place(sr_ref, si_ref, a_ref, reverse):
    S, N = sr_ref.shape
    nst = S // SCAN_SEG
    ar = jnp.broadcast_to(a_ref[0:1, :], (SCAN_SEG, N))
    ai = jnp.broadcast_to(a_ref[1:2, :], (SCAN_SEG, N))
    if reverse:
        ai = -ai
    row = lax.broadcasted_iota(jnp.int32, (SCAN_SEG, N), 0)
    zero = jnp.zeros((SCAN_SEG, N), f32)

    def tile(t):
        return pl.ds(pl.multiple_of((nst - 1 - t if reverse else t) * SCAN_SEG, SCAN_SEG), SCAN_SEG)

    def local(t, c):
        sr, si, pr, pi = c
        rows = tile(t)
        nsr = ar * sr - ai * si + sr_ref[rows, :]
        nsi = ar * si + ai * sr + si_ref[rows, :]
        sr_ref[rows, :] = nsr
        si_ref[rows, :] = nsi
        return nsr, nsi, ar * pr - ai * pi, ar * pi + ai * pr

    fr, fi, apr, api = lax.fori_loop(0, nst, local, (zero, zero, zero + 1.0, zero))

    def shift(v):
        if reverse:
            return jnp.where(row == SCAN_SEG - 1, 0.0, pltpu.roll(v, SCAN_SEG - 1, axis=0))
        return jnp.where(row == 0, 0.0, pltpu.roll(v, 1, axis=0))

    cr, ci = zero, zero
    for _ in range(SCAN_SEG - 1):
        cr, ci = shift(fr + apr * cr - api * ci), shift(fi + apr * ci + api * cr)

    def fix(t, c):
        pr, pi = c
        npr, npi = ar * pr - ai * pi, ar * pi + ai * pr
        rows = tile(t)
        sr_ref[rows, :] += npr * cr - npi * ci
        si_ref[rows, :] += npr * ci + npi * cr
        return npr, npi

    lax.fori_loop(0, nst, fix, (zero + 1.0, zero))


def _ssm_states(u, bre, bim, a2, l):
    S = u.shape[0]

    def body(u_ref, br_ref, bi_ref, a2_ref, sr_ref, si_ref):
        a_ref = a2_ref.at[l]
        brb = br_ref[l].astype(bf16)
        bib = bi_ref[l].astype(bf16)

        def project(t, c):
            rows = pl.ds(pl.multiple_of(t * _SSM_ROWS, _SSM_ROWS), _SSM_ROWS)
            ub = u_ref[rows, :].astype(bf16)
            sr_ref[rows, :] = _dot(ub, brb)
            si_ref[rows, :] = _dot(ub, bib)
            return c

        lax.fori_loop(0, S // _SSM_ROWS, project, 0)
        _scan_in_place(sr_ref, si_ref, a_ref, False)

    vm = pl.BlockSpec(memory_space=pltpu.VMEM)
    return pl.pallas_call(
        body, name="ssm_states", in_specs=[vm] * 4, out_specs=[vm, vm],
        out_shape=[jax.ShapeDtypeStruct((S, D_STATE), f32)] * 2,
        compiler_params=_cp(None, 48),
    )(u, bre, bim, a2)


def _ssm_out(sr, si, u, cre, cim, l, dskip, glu_w, glu_b):
    S = u.shape[0]
    ts = TOK_TILE

    def body(sr_ref, si_ref, u_ref, cr_ref, ci_ref, d_ref, w_ref, b_ref, out_ref, y_ref):
        y = (_dot(sr_ref[...].astype(bf16), cr_ref[0].astype(bf16))
             - _dot(si_ref[...].astype(bf16), ci_ref[0].astype(bf16)) + d_ref[...] * u_ref[...])
        y_ref[...] = y
        z = _dot(_gelu(y).astype(bf16), w_ref[...].astype(bf16)) + b_ref[...]
        out_ref[...] = y * jax.nn.sigmoid(z)

    st = pl.BlockSpec((ts, D_STATE), lambda i: (i, 0))
    ch = pl.BlockSpec((ts, D_SSM), lambda i: (i, 0))
    c_l = pl.BlockSpec((1, D_STATE, D_SSM), lambda i: (l, 0, 0))
    return pl.pallas_call(
        body, name="ssm_out", grid=(S // ts,),
        in_specs=[st, st, ch, c_l, c_l, _full((1, D_SSM)), _full((D_SSM, D_SSM)), _full((1, D_SSM))],
        out_specs=[ch, ch],
        out_shape=[jax.ShapeDtypeStruct((S, D_SSM), f32)] * 2,
        compiler_params=_cp(("parallel",)),
    )(*_hbm(sr, si, u, cre, cim, dskip, glu_w, glu_b))


def _ssm_out_bwd(dout, y, u, sr, si, dskip, glu_w, glu_b):
    S = u.shape[0]
    ts = TOK_TILE

    def body(do_ref, y_ref, u_ref, sr_ref, si_ref, d_ref, w_ref, b_ref,
             dy_ref, du_ref, dcr_ref, dci_ref, dd_ref, dgb_ref, dgw_ref):
        @pl.when(pl.program_id(0) == 0)
        def _():
            for r in (dcr_ref, dci_ref, dd_ref, dgb_ref, dgw_ref):
                r[...] = jnp.zeros_like(r)

        y = y_ref[...]
        dout = do_ref[...]
        wb = w_ref[...].astype(bf16)
        ge = _gelu(y).astype(bf16)
        sz = jax.nn.sigmoid(_dot(ge, wb) + b_ref[...])
        dz = dout * y * sz * (1.0 - sz)
        dzb = dz.astype(bf16)
        dgb_ref[...] += jnp.sum(dz, 0, keepdims=True)
        dgw_ref[...] += _dot_tn(ge, dzb)
        dy = dout * sz + _gelu_grad(y) * _dot_nt(dzb, wb)
        uv = u_ref[...]
        dd_ref[...] += jnp.sum(dy * uv, 0, keepdims=True)
        du_ref[...] = dy * d_ref[...]
        dy_ref[...] = dy
        dyb = dy.astype(bf16)
        dcr_ref[...] += _dot_tn(sr_ref[...].astype(bf16), dyb)
        dci_ref[...] -= _dot_tn(si_ref[...].astype(bf16), dyb)

    st = pl.BlockSpec((ts, D_STATE), lambda i: (i, 0))
    ch = pl.BlockSpec((ts, D_SSM), lambda i: (i, 0))
    c_full = _full((D_STATE, D_SSM))
    return pl.pallas_call(
        body, name="ssm_out_bwd", grid=(S // ts,),
        in_specs=[ch, ch, ch, st, st, _full((1, D_SSM)), _full((D_SSM, D_SSM)), _full((1, D_SSM))],
        out_specs=[ch, ch, c_full, c_full, _full((1, D_SSM)), _full((1, D_SSM)), _full((D_SSM, D_SSM))],
        out_shape=[jax.ShapeDtypeStruct((S, D_SSM), f32), jax.ShapeDtypeStruct((S, D_SSM), f32),
                   jax.ShapeDtypeStruct((D_STATE, D_SSM), f32), jax.ShapeDtypeStruct((D_STATE, D_SSM), f32),
                   jax.ShapeDtypeStruct((1, D_SSM), f32), jax.ShapeDtypeStruct((1, D_SSM), f32),
                   jax.ShapeDtypeStruct((D_SSM, D_SSM), f32)],
        compiler_params=_cp(("arbitrary",), 40),
    )(*_hbm(dout, y, u, sr, si, dskip, glu_w, glu_b))


def _ssm_states_bwd(dy, du_skip, u, sr, si, cre, cim, bre, bim, a2, l):
    S = u.shape[0]
    N = D_STATE
    nst = S // SCAN_SEG
    nproj = S // _SSM_ROWS

    def body(dy_ref, dus_ref, u_ref, sr_ref, si_ref, cr_ref, ci_ref, br_ref, bi_ref, a2_ref,
             du_ref, dbr_ref, dbi_ref, da_ref, lr_ref, li_ref):
        a_ref = a2_ref.at[l]
        crb = cr_ref[l].astype(bf16)
        cib = ci_ref[l].astype(bf16)

        def project(t, c):
            rows = pl.ds(pl.multiple_of(t * _SSM_ROWS, _SSM_ROWS), _SSM_ROWS)
            dyb = dy_ref[rows, :].astype(bf16)
            lr_ref[rows, :] = _dot_nt(dyb, crb)
            li_ref[rows, :] = -_dot_nt(dyb, cib)
            return c

        lax.fori_loop(0, nproj, project, 0)
        _scan_in_place(lr_ref, li_ref, a_ref, True)

        row = lax.broadcasted_iota(jnp.int32, (SCAN_SEG, N), 0)
        last = pl.ds((nst - 1) * SCAN_SEG, SCAN_SEG)
        pr = jnp.where(row == 0, 0.0, pltpu.roll(sr_ref[last, :], 1, axis=0))
        pi = jnp.where(row == 0, 0.0, pltpu.roll(si_ref[last, :], 1, axis=0))
        first = pl.ds(0, SCAN_SEG)
        acc_r = lr_ref[first, :] * pr + li_ref[first, :] * pi
        acc_i = li_ref[first, :] * pr - lr_ref[first, :] * pi

        def step(t, c):
            acc_r, acc_i = c
            rows = pl.ds(pl.multiple_of(t * SCAN_SEG, SCAN_SEG), SCAN_SEG)
            prev = pl.ds(pl.multiple_of((t - 1) * SCAN_SEG, SCAN_SEG), SCAN_SEG)
            lrv, liv, srv, siv = lr_ref[rows, :], li_ref[rows, :], sr_ref[prev, :], si_ref[prev, :]
            return acc_r + lrv * srv + liv * siv, acc_i + liv * srv - lrv * siv

        acc_r, acc_i = lax.fori_loop(1, nst, step, (acc_r, acc_i))
        da_ref[0:1, :] = jnp.sum(acc_r, 0, keepdims=True)
        da_ref[1:2, :] = jnp.sum(acc_i, 0, keepdims=True)

        brb = br_ref[l].astype(bf16)
        bib = bi_ref[l].astype(bf16)
        dbr_ref[...] = jnp.zeros_like(dbr_ref)
        dbi_ref[...] = jnp.zeros_like(dbi_ref)

        def back(t, c):
            rows = pl.ds(pl.multiple_of(t * _SSM_ROWS, _SSM_ROWS), _SSM_ROWS)
            lrb = lr_ref[rows, :].astype(bf16)
            lib = li_ref[rows, :].astype(bf16)
            du_ref[rows, :] = dus_ref[rows, :] + _dot_nt(lrb, brb) + _dot_nt(lib, bib)
            ub = u_ref[rows, :].astype(bf16)
            dbr_ref[...] += _dot_tn(ub, lrb)
            dbi_ref[...] += _dot_tn(ub, lib)
            return c

        lax.fori_loop(0, nproj, back, 0)

    vm = pl.BlockSpec(memory_space=pltpu.VMEM)
    return pl.pallas_call(
        body, name="ssm_states_bwd", in_specs=[vm] * 10, out_specs=[vm] * 4,
        out_shape=[jax.ShapeDtypeStruct((S, D_SSM), f32), jax.ShapeDtypeStruct((D_SSM, D_STATE), f32),
                   jax.ShapeDtypeStruct((D_SSM, D_STATE), f32), jax.ShapeDtypeStruct((2, D_STATE), f32)],
        scratch_shapes=[pltpu.VMEM((S, D_STATE), f32), pltpu.VMEM((S, D_STATE), f32)],
        compiler_params=_cp(None, 56),
    )(dy, du_skip, u, sr, si, cre, cim, bre, bim, a2)


_POOL_TILE = 256


def _window_sums(xt, back):
    n = xt.shape[0]
    out = []
    ws = xt
    for k in (1, 2, 4, 8):
        ws = ws + pltpu.roll(ws, k if back else n - k, axis=0)
        out.append(ws)
    return out


def _pool_count(r0, w):
    t = r0 + lax.broadcasted_iota(jnp.int32, (_POOL_TILE, POOL_GROUP), 0)
    return jnp.minimum(t + 1, w).astype(f32)


def _pool_fwd(u_pad, pool_w, pool_scale):
    S = u_pad.shape[0] - POOL_HALO
    nt = S // _POOL_TILE

    def body(u_ref, w_ref, sc_ref, y_ref):
        def tile(t, c):
            r0 = pl.multiple_of(t * _POOL_TILE, _POOL_TILE)
            for g, w in enumerate(POOL_WINDOWS):
                cs = pl.ds(POOL_GROUP * g, POOL_GROUP)
                xt = u_ref[pl.ds(r0, _POOL_TILE + POOL_HALO), cs]
                ws = _window_sums(xt, True)[g][POOL_HALO:, :]
                pooled = ws / _pool_count(r0, w) - xt[POOL_HALO:, :]
                y_ref[pl.ds(r0, _POOL_TILE), cs] = _dot(pooled.astype(bf16), w_ref[g].astype(bf16)) * sc_ref[:, cs]
            return c
        lax.fori_loop(0, nt, tile, 0)

    vm = pl.BlockSpec(memory_space=pltpu.VMEM)
    return pl.pallas_call(
        body, name="pool_fwd", in_specs=[vm, vm, vm], out_specs=vm,
        out_shape=jax.ShapeDtypeStruct((S, D_POOL), f32),
    )(u_pad, pool_w, pool_scale)


def _pool_bwd(dy_pad, u_pad, pool_w, pool_scale):
    S = u_pad.shape[0] - POOL_HALO
    nt = S // _POOL_TILE
    n = _POOL_TILE + POOL_HALO

    def body(dy_ref, u_ref, w_ref, sc_ref, du_ref, dw_ref, dsc_ref):
        dw_ref[...] = jnp.zeros_like(dw_ref)
        dsc_ref[...] = jnp.zeros_like(dsc_ref)

        def tile(t, c):
            r0 = pl.multiple_of(t * _POOL_TILE, _POOL_TILE)
            for g, w in enumerate(POOL_WINDOWS):
                cs = pl.ds(POOL_GROUP * g, POOL_GROUP)
                wb = w_ref[g].astype(bf16)
                xt = u_ref[pl.ds(r0, n), cs]
                pooled = (_window_sums(xt, True)[g][POOL_HALO:, :] / _pool_count(r0, w) - xt[POOL_HALO:, :]).astype(bf16)
                dy = dy_ref[pl.ds(r0, _POOL_TILE), cs]
                dsc_ref[:, cs] += jnp.sum(dy * _dot(pooled, wb), 0, keepdims=True)
                dw_ref[g] += _dot_tn(pooled, (dy * sc_ref[:, cs]).astype(bf16))
                dyh = (dy_ref[pl.ds(r0, n), cs] * sc_ref[:, cs]).astype(bf16)
                dpl = _dot_nt(dyh, wb)
                cnt = jnp.minimum(r0 + lax.broadcasted_iota(jnp.int32, (n, POOL_GROUP), 0) + 1, w).astype(f32)
                lead = _window_sums(dpl / cnt, False)[g]
                du_ref[pl.ds(r0, _POOL_TILE), cs] = lead[:_POOL_TILE, :] - dpl[:_POOL_TILE, :]
            return c
        lax.fori_loop(0, nt, tile, 0)

    vm = pl.BlockSpec(memory_space=pltpu.VMEM)
    return pl.pallas_call(
        body, name="pool_bwd", in_specs=[vm, vm, vm, vm], out_specs=[vm, vm, vm],
        out_shape=[jax.ShapeDtypeStruct((S, D_POOL), f32), jax.ShapeDtypeStruct((4, POOL_GROUP, POOL_GROUP), f32),
                   jax.ShapeDtypeStruct((1, D_POOL), f32)],
    )(dy_pad, u_pad, pool_w, pool_scale)


def _loss_head(y, target):
    S, D = y.shape
    ts = TOK_TILE

    def body(y_ref, t_ref, loss_ref, dy_ref):
        @pl.when(pl.program_id(0) == 0)
        def _():
            loss_ref[...] = jnp.zeros_like(loss_ref)

        d = y_ref[...] - t_ref[...]
        dy_ref[...] = d * (1.0 / D)
        loss_ref[...] += 0.5 * jnp.sum(jnp.sum(d * d, -1, keepdims=True) * (1.0 / D), 0, keepdims=True)

    tok = pl.BlockSpec((ts, D), lambda i: (i, 0))
    return pl.pallas_call(
        body, name="loss_head", grid=(S // ts,),
        in_specs=[tok, tok], out_specs=[_full((1, 1)), tok],
        out_shape=[jax.ShapeDtypeStruct((1, 1), f32), jax.ShapeDtypeStruct((S, D), f32)],
        compiler_params=_cp(("arbitrary",)),
    )(*_hbm(y, target))


_ADA_COLS = 768


def _ada_fwd(c_all, ada_w, ada_b_cols):
    L, D, N = ada_w.shape
    B = c_all.shape[0]

    def body(c_ref, w_ref, b_ref, out_ref):
        cv = c_ref[...]
        cond = (cv * jax.nn.sigmoid(cv)).astype(bf16)
        out_ref[0] = _dot(cond, w_ref[0].astype(bf16)) + b_ref[0]

    return pl.pallas_call(
        body, name="ada_fwd", grid=(L, N // _ADA_COLS),
        in_specs=[_full((B, D)), pl.BlockSpec((1, D, _ADA_COLS), lambda l, j: (l, 0, j)),
                  pl.BlockSpec((1, 1, _ADA_COLS), lambda l, j: (l, 0, j))],
        out_specs=pl.BlockSpec((1, B, _ADA_COLS), lambda l, j: (l, 0, j)),
        out_shape=jax.ShapeDtypeStruct((L, B, N), f32),
        compiler_params=_cp(("parallel", "parallel")),
    )(c_all, ada_w, ada_b_cols)


def _ada_wgrad(c_all_t, dmod_cols):
    D, B = c_all_t.shape
    L, _, N = dmod_cols.shape

    def body(ct_ref, dm_ref, out_ref):
        cv = ct_ref[...]
        cond = cv * jax.nn.sigmoid(cv)
        acc = cond[:, 0:1] * dm_ref[0, 0:1, :]
        for b in range(1, B):
            acc = acc + cond[:, b:b + 1] * dm_ref[0, b:b + 1, :]
        out_ref[0] = acc

    return pl.pallas_call(
        body, name="ada_wgrad", grid=(L, N // _ADA_COLS),
        in_specs=[_full((D, B)), pl.BlockSpec((1, B, _ADA_COLS), lambda l, j: (l, 0, j))],
        out_specs=pl.BlockSpec((1, D, _ADA_COLS), lambda l, j: (l, 0, j)),
        out_shape=jax.ShapeDtypeStruct((L, D, N), f32),
        compiler_params=_cp(("parallel", "parallel")),
    )(c_all_t, dmod_cols)


def _adam_math(w, g, m, v):
    m = ADAM_B1 * m + (1.0 - ADAM_B1) * g
    v = ADAM_B2 * v + (1.0 - ADAM_B2) * (g * g)
    m_hat = m / (1.0 - ADAM_B1 ** ADAM_STEP)
    v_hat = v / (1.0 - ADAM_B2 ** ADAM_STEP)
    delta = -ADAM_LR * (m_hat / (jnp.sqrt(v_hat) + ADAM_EPS) + ADAM_WD * w)
    return delta, m, v


def _adamw(w, m, v, g, row_tile, row0=0, outs=None):
    R, C = w.shape
    b0 = row0 // row_tile

    def body(w_ref, m_ref, v_ref, g_ref, _0, _1, _2, _3, g_out, d_out, m_out, v_out):
        gv = g_ref[...]
        delta, mn, vn = _adam_math(w_ref[...], gv, m_ref[...], v_ref[...])
        g_out[...] = gv
        d_out[...] = delta
        m_out[...] = mn
        v_out[...] = vn

    pspec = pl.BlockSpec((row_tile, C), lambda i: (b0 + i, 0))
    gspec = pl.BlockSpec((row_tile, C), lambda i: (i, 0))
    anyspec = pl.BlockSpec(memory_space=pl.ANY)
    shp = jax.ShapeDtypeStruct((R, C), f32)
    if outs is None:
        outs = [lax.empty((R, C), f32) for _ in range(4)]
    return pl.pallas_call(
        body, name="adamw", grid=(g.shape[0] // row_tile,),
        in_specs=[pspec] * 3 + [gspec] + [anyspec] * 4, out_specs=[pspec] * 4, out_shape=[shp] * 4,
        input_output_aliases={4: 0, 5: 1, 6: 2, 7: 3},
        compiler_params=_cp(("parallel",), 40),
    )(*_hbm(w, m, v, g, *outs))


def _pair_sum(g5s, gots, pc):
    n = len(g5s)

    def body(pc_ref, *refs):
        for own, got, out in zip(refs[:n], refs[n:2 * n], refs[2 * n:]):
            out[0, 0] = (own[0, 0, 0].astype(f32) + got[0, 0].astype(f32)).astype(bf16)

    def half(g):
        return pl.BlockSpec((1, 1) + g.shape[-2:], lambda p, pc: (p, 0, 0, 0))

    gs = pltpu.PrefetchScalarGridSpec(
        num_scalar_prefetch=1, grid=(N_CHIPS,),
        in_specs=[pl.BlockSpec((1, 1, 1) + g.shape[-2:], lambda p, pc: (p, 0, pc[1], 0, 0)) for g in g5s]
        + [half(g) for g in gots],
        out_specs=[half(g) for g in gots],
    )
    return pl.pallas_call(
        body, name="pair_sum", grid_spec=gs, out_shape=[jax.ShapeDtypeStruct(g.shape, bf16) for g in gots],
        compiler_params=_cp(("parallel",), 48),
    )(pc, *_hbm(*g5s, *gots))


_SUM_STEPS = 2


def _sum_shards(hsums, recvs, pc):
    n = len(hsums)

    def body(pc_ref, *refs):
        for own, got, out in zip(refs[:n], refs[n:2 * n], refs[2 * n:]):
            acc = own[0, 0].astype(f32)
            for j in range(3):
                acc = acc + got[j, 0].astype(f32)
            out[0, 0] = acc

    def rows(h):
        return (h.shape[2] // _SUM_STEPS, h.shape[3])

    gs = pltpu.PrefetchScalarGridSpec(
        num_scalar_prefetch=1, grid=(_SUM_STEPS,),
        in_specs=[pl.BlockSpec((1, 1) + rows(h), lambda i, pc: (pc[0], 0, i, 0)) for h in hsums]
        + [pl.BlockSpec((3, 1) + rows(h), lambda i, pc: (0, 0, i, 0)) for h in hsums],
        out_specs=[pl.BlockSpec((1, 1) + rows(h), lambda i, pc: (0, pc[1], i, 0)) for h in hsums],
    )
    return pl.pallas_call(
        body, name="sum_shards", grid_spec=gs,
        out_shape=[jax.ShapeDtypeStruct((1, 2) + h.shape[2:], f32) for h in hsums],
        compiler_params=_cp(("parallel",), 48),
    )(pc, *_hbm(*hsums, *recvs))


def _sum8(packs):
    _, R, C = packs.shape
    tr = R // 8 if R % 64 == 0 else R

    def body(p_ref, out_ref):
        acc = p_ref[0]
        for d in range(1, 8):
            acc = acc + p_ref[d]
        out_ref[...] = acc

    return pl.pallas_call(
        body, name="sum8", grid=(R // tr,),
        in_specs=[pl.BlockSpec((8, tr, C), lambda i: (0, i, 0))],
        out_specs=pl.BlockSpec((tr, C), lambda i: (i, 0)),
        out_shape=jax.ShapeDtypeStruct((R, C), f32),
        compiler_params=_cp(("parallel",)),
    )(packs)


def _allgather8(x_shard):
    m_per, n = x_shard.shape

    def body(x_ref, out_ref, send_sems, recv_sems, local_sem):
        x, y, c = lax.axis_index("x"), lax.axis_index("y"), lax.axis_index("c")
        me, sibling = (x, y, c), (x, y, 1 - c)
        chips = [(1 - x, y), (x, 1 - y), (1 - x, 1 - y)]

        def rows(px, py, pc):
            return out_ref.at[pl.ds((4 * px + 2 * py + pc) * m_per, m_per), :]

        def copy(k, block, to, src=None):
            return pltpu.make_async_remote_copy(
                src_ref=rows(*block) if src is None else src, dst_ref=rows(*block),
                send_sem=send_sems.at[k], recv_sem=recv_sems.at[k], device_id=to, device_id_type=MESH)

        mine = pltpu.make_async_copy(x_ref, rows(*me), local_sem)
        mine.start()
        first = [copy(0, me, sibling, src=x_ref)]
        first += [copy(1 + j, me, (*chip, c), src=x_ref) for j, chip in enumerate(chips)]
        for cp in first:
            cp.start()
        passed = [copy(4 + j, (*chip, c), sibling) for j, chip in enumerate(chips)]
        for j, chip in enumerate(chips):
            copy(1 + j, (*chip, c), me).wait_recv()
            passed[j].start()
        copy(0, sibling, me).wait_recv()
        for j, chip in enumerate(chips):
            copy(4 + j, (*chip, 1 - c), me).wait_recv()
        for cp in first + passed:
            cp.wait_send()
        mine.wait()

    return pl.pallas_call(
        body, name="allgather8",
        out_shape=jax.ShapeDtypeStruct((8 * m_per, n), x_shard.dtype),
        in_specs=[pl.BlockSpec(memory_space=pltpu.VMEM)],
        out_specs=pl.BlockSpec(memory_space=pltpu.VMEM),
        scratch_shapes=[pltpu.SemaphoreType.DMA((7,)), pltpu.SemaphoreType.DMA((7,)), pltpu.SemaphoreType.DMA],
        compiler_params=_cp(None, 48),
    )(x_shard)


def _other_chips():
    x, y = lax.axis_index("x"), lax.axis_index("y")
    return [(1 - x, y), (x, 1 - y), (1 - x, 1 - y)]


_HBM = pl.BlockSpec(memory_space=pltpu.HBM)
_SEM = pl.BlockSpec(memory_space=pltpu.SEMAPHORE)
_EFFECT = pltpu.SideEffectType.DATAFLOW_SIDE_EFFECTING


def _gather_copies(srcs, lands, send_sems, recv_sems):
    x, y, c = lax.axis_index("x"), lax.axis_index("y"), lax.axis_index("c")
    return [pltpu.make_async_remote_copy(
        src_ref=srcs[a].at[:, c], dst_ref=lands[a].at[2 * x + y, :, c], send_sem=send_sems.at[3 * a + j],
        recv_sem=recv_sems.at[3 * a + j], device_id=(cx, cy, c), device_id_type=MESH)
        for a in range(len(srcs)) for j, (cx, cy) in enumerate(_other_chips())]


def _gather_start(chunks, after, name):
    sizes = [len(srcs) for srcs, _ in chunks]
    flat = [t for srcs, lands in chunks for t in list(srcs) + list(lands)]
    nflat = len(flat)
    nsem = 2 * len(chunks)

    def body(*refs):
        ins, sems, token = refs[:nflat], refs[nflat + 1:nflat + 1 + nsem], refs[-1]
        off = 0
        for k, n in enumerate(sizes):
            for cp in _gather_copies(ins[off:off + n], ins[off + n:off + 2 * n], sems[2 * k], sems[2 * k + 1]):
                cp.start()
            off += 2 * n
        token[...] = jnp.zeros_like(token)

    res = pl.pallas_call(
        body, name=name,
        out_shape=[pltpu.SemaphoreType.DMA((3 * n,)) for n in sizes for _ in range(2)]
        + [pltpu.HBM(t.shape, t.dtype) for t in flat] + [jax.ShapeDtypeStruct((8, 128), f32)],
        in_specs=[_HBM] * nflat + [pl.BlockSpec(memory_space=pl.ANY)],
        out_specs=[_SEM] * nsem + [_HBM] * nflat + [pl.BlockSpec(memory_space=pltpu.VMEM)],
        input_output_aliases={i: nsem + i for i in range(nflat)},
        compiler_params=pltpu.CompilerParams(has_side_effects=_EFFECT),
    )(*[pltpu.with_memory_space_constraint(t, pltpu.HBM) for t in flat], after)
    out, off = [], nsem
    for k, n in enumerate(sizes):
        out.append((res[2 * k], res[2 * k + 1], res[off:off + n], res[off + n:off + 2 * n]))
        off += 2 * n
    return out, res[-1]


def _gather_wait(send_sems, recv_sems, srcs, lands, after, name):
    n = len(srcs)

    def body(*refs):
        for cp in _gather_copies(refs[:n], refs[n:2 * n], refs[2 * n], refs[2 * n + 1]):
            cp.wait_send()
            cp.wait_recv()

    res = pl.pallas_call(
        body, name=name,
        out_shape=[pltpu.HBM(t.shape, t.dtype) for t in list(srcs) + list(lands)],
        in_specs=[_HBM] * (2 * n) + [_SEM, _SEM] + [pl.BlockSpec(memory_space=pl.ANY)] * len(after),
        out_specs=[_HBM] * (2 * n),
        input_output_aliases={i: i for i in range(2 * n)},
        compiler_params=pltpu.CompilerParams(has_side_effects=_EFFECT),
    )(*srcs, *lands, send_sems, recv_sems, *after)
    return res[n:]


def _split_start(make_copies, arrays, nsem, name, after=()):
    n, na = len(arrays), len(after)

    def body(*refs):
        for cp in make_copies(refs[:n], refs[n + na], refs[n + na + 1]):
            cp.start()
        refs[-1][...] = jnp.zeros_like(refs[-1])

    res = pl.pallas_call(
        body, name=name,
        out_shape=[pltpu.SemaphoreType.DMA((nsem,)), pltpu.SemaphoreType.DMA((nsem,))]
        + [pltpu.HBM(t.shape, t.dtype) for t in arrays] + [jax.ShapeDtypeStruct((8, 128), f32)],
        in_specs=[_HBM] * n + [pl.BlockSpec(memory_space=pl.ANY)] * na,
        out_specs=[_SEM, _SEM] + [_HBM] * n + [pl.BlockSpec(memory_space=pltpu.VMEM)],
        input_output_aliases={i: i + 2 for i in range(n)},
        compiler_params=pltpu.CompilerParams(has_side_effects=_EFFECT),
    )(*[pltpu.with_memory_space_constraint(t, pltpu.HBM) for t in arrays], *after)
    return (res[0], res[1], res[2:2 + n]), res[-1]


def _split_wait(make_copies, send_sems, recv_sems, arrays, after, name):
    n = len(arrays)

    def body(*refs):
        for cp in make_copies(refs[:n], refs[n], refs[n + 1]):
            cp.wait_send()
            cp.wait_recv()

    return pl.pallas_call(
        body, name=name,
        out_shape=[pltpu.HBM(t.shape, t.dtype) for t in arrays],
        in_specs=[_HBM] * n + [_SEM, _SEM] + [pl.BlockSpec(memory_space=pl.ANY)] * len(after),
        out_specs=[_HBM] * n, input_output_aliases={i: i for i in range(n)},
        compiler_params=pltpu.CompilerParams(has_side_effects=_EFFECT),
    )(*arrays, send_sems, recv_sems, *after)


def _sibling():
    return lax.axis_index("x"), lax.axis_index("y"), 1 - lax.axis_index("c")


def _forward_copies(lands, send_sems, recv_sems):
    c = lax.axis_index("c")
    return [pltpu.make_async_remote_copy(
        src_ref=lands[a].at[2 * cx + cy, :, c], dst_ref=lands[a].at[2 * cx + cy, :, c], send_sem=send_sems.at[3 * a + j],
        recv_sem=recv_sems.at[3 * a + j], device_id=_sibling(), device_id_type=MESH)
        for a in range(len(lands)) for j, (cx, cy) in enumerate(_other_chips())]


def _swap_copies(fulls, send_sems, recv_sems):
    c = lax.axis_index("c")
    return [pltpu.make_async_remote_copy(src_ref=t.at[:, c], dst_ref=t.at[:, c], send_sem=send_sems.at[a],
                                         recv_sem=recv_sems.at[a], device_id=_sibling(), device_id_type=MESH)
            for a, t in enumerate(fulls)]


def _pair_copies(refs, send_sems, recv_sems):
    n = len(refs) // 2
    c = lax.axis_index("c")
    return [pltpu.make_async_remote_copy(src_ref=refs[a].at[:, :, 1 - c], dst_ref=refs[n + a], send_sem=send_sems.at[a],
                                         recv_sem=recv_sems.at[a], device_id=_sibling(), device_id_type=MESH)
            for a in range(n)]


def _scatter_copies(srcs, lands, send_sems, recv_sems):
    c = lax.axis_index("c")
    return [pltpu.make_async_remote_copy(
        src_ref=srcs[a].at[2 * cx + cy], dst_ref=lands[a].at[j], send_sem=send_sems.at[3 * a + j],
        recv_sem=recv_sems.at[3 * a + j], device_id=(cx, cy, c), device_id_type=MESH)
        for a in range(len(srcs)) for j, (cx, cy) in enumerate(_other_chips())]


def _scatter_start(hsums, name, after=()):
    n = len(hsums)
    na = len(after)

    def body(*refs):
        srcs, lands = refs[:n], refs[n:2 * n]
        send_sems, recv_sems = refs[2 * n + na], refs[2 * n + na + 1]
        for cp in _scatter_copies(srcs, lands, send_sems, recv_sems):
            cp.start()
        refs[-1][...] = jnp.zeros_like(refs[-1])

    lands = [lax.empty((3,) + g.shape[1:], g.dtype) for g in hsums]
    res = pl.pallas_call(
        body, name=name,
        out_shape=[pltpu.SemaphoreType.DMA((3 * n,)), pltpu.SemaphoreType.DMA((3 * n,))]
        + [pltpu.HBM(g.shape, g.dtype) for g in hsums] + [pltpu.HBM(g.shape, g.dtype) for g in lands]
        + [jax.ShapeDtypeStruct((8, 128), f32)],
        in_specs=[_HBM] * (2 * n) + [pl.BlockSpec(memory_space=pl.ANY)] * na,
        out_specs=[_SEM, _SEM] + [_HBM] * (2 * n) + [pl.BlockSpec(memory_space=pltpu.VMEM)],
        input_output_aliases={i: i + 2 for i in range(2 * n)},
        compiler_params=pltpu.CompilerParams(has_side_effects=_EFFECT),
    )(*[pltpu.with_memory_space_constraint(t, pltpu.HBM) for t in list(hsums) + lands], *after)
    return (res[0], res[1], res[2:2 + n], res[2 + n:2 + 2 * n]), res[-1]


def _scatter_wait(send_sems, recv_sems, srcs, lands, after, name):
    n = len(srcs)
    extra = list(after)

    def body(*refs):
        s_refs, l_refs = refs[:n], refs[n:2 * n]
        ss, rs = refs[2 * n], refs[2 * n + 1]
        for cp in _scatter_copies(s_refs, l_refs, ss, rs):
            cp.wait_send()
            cp.wait_recv()

    res = pl.pallas_call(
        body, name=name,
        out_shape=[pltpu.HBM(g.shape, g.dtype) for g in srcs] + [pltpu.HBM(g.shape, g.dtype) for g in lands],
        in_specs=[_HBM] * (2 * n) + [_SEM, _SEM] + [pl.BlockSpec(memory_space=pl.ANY)] * len(extra),
        out_specs=[_HBM] * (2 * n),
        input_output_aliases={i: i for i in range(2 * n)},
        compiler_params=pltpu.CompilerParams(has_side_effects=_EFFECT),
    )(*srcs, *lands, send_sems, recv_sems, *extra)
    return res[:n], res[n:]


def _plane_copies(src, land, send_sems, recv_sems):
    x, y, c = lax.axis_index("x"), lax.axis_index("y"), lax.axis_index("c")
    return [pltpu.make_async_remote_copy(src_ref=src, dst_ref=land.at[2 * x + y, c], send_sem=send_sems.at[j],
                                         recv_sem=recv_sems.at[j], device_id=(cx, cy, c), device_id_type=MESH)
            for j, (cx, cy) in enumerate(_other_chips())]


def _plane_start(pack, land, name):
    def body(src, lnd, send_sems, recv_sems, _s, _l, token):
        for cp in _plane_copies(src, lnd, send_sems, recv_sems):
            cp.start()
        token[...] = jnp.zeros_like(token)

    res = pl.pallas_call(
        body, name=name,
        out_shape=[pltpu.SemaphoreType.DMA((3,)), pltpu.SemaphoreType.DMA((3,)), pltpu.HBM(pack.shape, pack.dtype),
                   pltpu.HBM(land.shape, land.dtype), jax.ShapeDtypeStruct((8, 128), f32)],
        in_specs=[_HBM, _HBM], out_specs=[_SEM, _SEM, _HBM, _HBM, pl.BlockSpec(memory_space=pltpu.VMEM)],
        input_output_aliases={0: 2, 1: 3},
        compiler_params=pltpu.CompilerParams(has_side_effects=_EFFECT),
    )(pltpu.with_memory_space_constraint(pack, pltpu.HBM), pltpu.with_memory_space_constraint(land, pltpu.HBM))
    return res[:4], res[4]


def _plane_wait(send_sems, recv_sems, pack, land, after, name):
    def body(src, lnd, ss, rs, *_):
        for cp in _plane_copies(src, lnd, ss, rs):
            cp.wait_send()
            cp.wait_recv()

    return pl.pallas_call(
        body, name=name,
        out_shape=[pltpu.HBM(pack.shape, pack.dtype), pltpu.HBM(land.shape, land.dtype)],
        in_specs=[_HBM, _HBM, _SEM, _SEM] + [pl.BlockSpec(memory_space=pl.ANY)] * len(after),
        out_specs=[_HBM, _HBM], input_output_aliases={0: 0, 1: 1},
        compiler_params=pltpu.CompilerParams(has_side_effects=_EFFECT),
    )(pack, land, send_sems, recv_sems, *after)[1]


def _swap_halves(fulls):
    n = len(fulls)

    def body(*refs):
        ins, outs = refs[:n], refs[n:2 * n]
        send_sems, recv_sems = refs[2 * n:]
        c = lax.axis_index("c")
        sibling = (lax.axis_index("x"), lax.axis_index("y"), 1 - c)
        copies = []
        for a in range(n):
            cp = pltpu.make_async_remote_copy(src_ref=outs[a].at[:, c], dst_ref=outs[a].at[:, c], send_sem=send_sems.at[a],
                                              recv_sem=recv_sems.at[a], device_id=sibling, device_id_type=MESH)
            cp.start()
            copies.append(cp)
        for a, cp in enumerate(copies):
            cp.wait_send()
            theirs = outs[a].at[:, 1 - c]
            pltpu.make_async_remote_copy(src_ref=theirs, dst_ref=theirs, send_sem=send_sems.at[a], recv_sem=recv_sems.at[a],
                                         device_id=sibling, device_id_type=MESH).wait_recv()

    hbm = pl.BlockSpec(memory_space=pl.ANY)
    return pl.pallas_call(
        body, name="swap_halves",
        out_shape=[jax.ShapeDtypeStruct(p.shape, p.dtype) for p in fulls],
        in_specs=[hbm] * n, out_specs=[hbm] * n,
        input_output_aliases={a: a for a in range(n)},
        scratch_shapes=[pltpu.SemaphoreType.DMA((n,)), pltpu.SemaphoreType.DMA((n,))],
    )(*fulls)


def _to_segments(t):
    s, c = t.shape
    return t.reshape(SCAN_SEG, s // SCAN_SEG, c).transpose(1, 0, 2).reshape(s, c)


def _from_segments(t):
    s, c = t.shape
    return t.reshape(s // SCAN_SEG, SCAN_SEG, c).transpose(1, 0, 2).reshape(s, c)


def _ssm_operators(a_re, a_im, log_dt, b_re, b_im, c_re, c_im):
    lam = lax.complex(a_re, a_im)
    dt = jnp.exp(log_dt)[:, None]
    a_bar = jnp.exp(lam * dt)
    b_bar = ((a_bar - 1.0) / lam)[:, :, None] * lax.complex(b_re, b_im)
    eye = jnp.eye(N_GROUPS, dtype=f32)

    def embed_b(t):
        return (jnp.transpose(t, (0, 2, 1))[:, :, None, :] * eye[:, None, :, None]).reshape(D_SSM, D_STATE)

    def embed_c(t):
        return (jnp.transpose(t, (0, 2, 1))[:, :, None, :] * eye[:, None, :, None]).reshape(D_STATE, D_SSM)

    a2 = jnp.stack([a_bar.real.reshape(D_STATE), a_bar.imag.reshape(D_STATE)])
    return a2, embed_b(b_bar.real), embed_b(b_bar.imag), embed_c(c_re), embed_c(c_im)


def _local_step(x, target, mod, small, ffn_weights, mix_weights, grads_done, ffn_bwd_issued):
    table = jnp.asarray(_bucket_table())
    bias = small["att_bias"]
    L = DEPTH
    saved = []
    ssm_names = ("ssm_a_re", "ssm_a_im", "ssm_log_dt", "ssm_b_re", "ssm_b_im", "ssm_c_re", "ssm_c_im")
    ssm_ops_vjp = []
    for l in range(L):
        sv = {}
        m9 = mod[l]
        sv["x0"] = x
        sv["w0"] = ffn_weights(l, 0, x)
        x, sv["f0"], sv["g0"], sv["u0"], sv["h0"] = _ffn_fwd(x, m9[0:3], *sv["w0"], 0, small["ln_g"][l, 0:1], small["ln_b"][l, 0:1])
        sv["x1"] = x
        sv["w1"] = mix_weights(l, x)
        *qkv, z_rest, sv["h1"] = _mix_in_fwd(x, m9[3:6], sv["w1"][0], 0)
        S = x.shape[0]
        qkv = [t.reshape(3, S, D_ATT) for t in qkv]
        att = [_att_fwd(qkv[b], bias, b) for b in range(3)]
        y_att, lse3 = _att_merge([att[b][0].reshape(d, S // d, D_ATT) for b, d in enumerate(DILATIONS)],
                                 [att[b][1].reshape(d, S // d, _LANES) for b, d in enumerate(DILATIONS)])
        sv.update(qkv=qkv, lse=[a[1] for a in att], lse3=lse3, y_att=y_att)

        ops, ops_vjp = jax.vjp(_ssm_operators, *[small[k][l] for k in ssm_names])
        ssm_ops_vjp.append(ops_vjp)
        a2, bre, bim, cre, cim = [t[None] for t in ops]
        u_ssm = _to_segments(z_rest[:, :D_SSM])
        sr, si = _ssm_states(u_ssm, bre, bim, a2, 0)
        dskip = small["ssm_d"][l][None, :]
        glu_b = small["glu_b"][l][None, :]
        out_seg, y_seg = _ssm_out(sr, si, u_ssm, cre, cim, 0, dskip, small["glu_w"][l], glu_b)
        y_ssm = _from_segments(out_seg)
        sv.update(ssm_ops=(a2, bre, bim, cre, cim), u_ssm=u_ssm, sr=sr, si=si, y_seg=y_seg, y_ssm=y_ssm)

        u_pool = jnp.concatenate([jnp.zeros((POOL_HALO, D_POOL), f32), z_rest[:, D_SSM:]])
        y_pool = _pool_fwd(u_pool, small["pool_w"][l], small["pool_scale"][l][None, :])
        sv.update(u_pool=u_pool, y_pool=y_pool)

        x, sv["ymix"] = _mix_out_fwd(x, y_att, y_ssm, y_pool, m9[3:6], sv["w1"][1], 0, small["ln_g"][l, 1:2], small["ln_b"][l, 1:2])
        sv["x2"] = x
        sv["w2"] = ffn_weights(l, 1, x)
        x, sv["f2"], sv["g2"], sv["u2"], sv["h2"] = _ffn_fwd(x, m9[6:9], *sv["w2"], 0, small["ln_g"][l, 2:3], small["ln_b"][l, 2:3])
        saved.append(sv)

    loss, dx = _loss_head(x, target)

    dmod = [None] * L
    dln_g = [None] * L
    dln_b = [None] * L
    sg = {k: [None] * L for k in ssm_names + ("ssm_d", "glu_w", "glu_b", "pool_w", "pool_scale")}
    dbias_tot = None
    order_after = jnp.zeros((), f32)
    for l in reversed(range(L)):
        sv = saved[l]
        m9 = mod[l] + order_after

        def fresh(like):
            return [lax.empty(t.shape, bf16) for t in like]

        dx, dg, du, a, df, dm2, dlg2, dlb2 = _ffn_bwd(dx, sv["x2"], sv["f2"], sv["g2"], sv["u2"], m9[6:9], *sv["w2"], 0,
                                                     small["ln_g"][l, 2:3])
        m9 = m9 + ffn_bwd_issued(l, 1, dx)
        g_ffn1 = _ffn_wgrad(sv["h2"], dg, du, a, df, *fresh(sv["w2"]), 0)
        dxr, d_att, d_ssm, d_pool, dgate1, dlg1, dlb1, g_w_out = _mix_out_bwd(
            dx, sv["x1"], sv["ymix"], sv["y_att"], sv["y_ssm"], sv["y_pool"], m9[3:6], sv["w1"][1], 0, small["ln_g"][l, 1:2],
            fresh(sv["w1"])[1])
        S = d_att.shape[0]
        merged = _att_merge_bwd(d_att, sv["y_att"], sv["lse3"])
        dqkv, dbias = [], []
        for b, d in enumerate(DILATIONS):
            dq_b, db_b = _att_bwd(sv["qkv"][b], merged[b].reshape(S, D_ATT), sv["lse"][b], merged[3 + b].reshape(S, _LANES), bias, b)
            dqkv.append(dq_b.reshape(3, d, S // d, D_ATT))
            dbias.append(db_b)
        dbias = jnp.stack(dbias)
        dbias_tot = dbias if dbias_tot is None else dbias_tot + dbias
        d_seg = _to_segments(d_ssm)
        dskip = small["ssm_d"][l][None, :]
        glu_b = small["glu_b"][l][None, :]
        dy_seg, du_skip, dcre, dcim, dd, dglu_b, dglu_w = _ssm_out_bwd(
            d_seg, sv["y_seg"], sv["u_ssm"], sv["sr"], sv["si"], dskip, small["glu_w"][l], glu_b)
        a2, bre, bim, cre, cim = sv["ssm_ops"]
        du_seg, dbre, dbim, da2 = _ssm_states_bwd(dy_seg, du_skip, sv["u_ssm"], sv["sr"], sv["si"], cre, cim, bre, bim, a2, 0)
        for k, t in zip(ssm_names, ssm_ops_vjp[l]((da2, dbre, dbim, dcre, dcim))):
            sg[k][l] = t
        sg["ssm_d"][l] = dd[0]
        sg["glu_b"][l] = dglu_b[0]
        sg["glu_w"][l] = dglu_w
        du_ssm = _from_segments(du_seg)
        dyp = jnp.concatenate([d_pool, jnp.zeros((POOL_HALO, D_POOL), f32)])
        du_pool, dpw, dps = _pool_bwd(dyp, sv["u_pool"], small["pool_w"][l], small["pool_scale"][l][None, :])
        sg["pool_w"][l] = dpw
        sg["pool_scale"][l] = dps[0]
        d_rest = jnp.concatenate([du_ssm, du_pool], axis=1).astype(bf16)
        dx, dm1, dz = _mix_in_bwd(dqkv, d_rest, dxr, sv["x1"], m9[3:6], sv["w1"][0], 0)
        g_w_in = _mix_in_wgrad(sv["h1"], dz, fresh(sv["w1"])[0], 0)
        ffn_names = ("ffn_w_gate", "ffn_w_up", "ffn_w_down")
        m9 = m9 + grads_done(l, 1, list(zip(ffn_names, [(2 * l + 1) * FF_SHARD] * 3, g_ffn1))
                             + [("w_in", l * D_MODEL, g_w_in), ("w_out", l * 256, g_w_out)])
        dm1 = jnp.concatenate([dm1[0:2], dgate1])
        dx, dg, du, a, df, dm0, dlg0, dlb0 = _ffn_bwd(dx, sv["x0"], sv["f0"], sv["g0"], sv["u0"], m9[0:3], *sv["w0"], 0,
                                                     small["ln_g"][l, 0:1])
        issued = ffn_bwd_issued(l, 0, dx)
        g_ffn0 = _ffn_wgrad(sv["h0"], dg, du, a, df, *fresh(sv["w0"]), 0)
        order_after = grads_done(l, 0, list(zip(ffn_names, [2 * l * FF_SHARD] * 3, g_ffn0))) + issued
        dmod[l] = jnp.concatenate([dm0 + issued, dm1, dm2])
        dln_g[l] = jnp.concatenate([dlg0, dlg1, dlg2])
        dln_b[l] = jnp.concatenate([dlb0, dlb1, dlb2])

    small_grads = {k: jnp.stack(v) for k, v in sg.items()}
    small_grads["rel_bias"] = _bias_bwd(dbias_tot, table)
    small_grads["ln_g"] = jnp.stack(dln_g)
    small_grads["ln_b"] = jnp.stack(dln_b)
    return loss, dx, jnp.stack(dmod), small_grads


_TILE_ELEMS = 8 * 128


def _pack_rows(shapes):
    out, row = [], 0
    for s in shapes:
        nr = -(-int(np.prod(s)) // _TILE_ELEMS) * 8
        out.append((row, nr))
        row += nr
    return out


def _pack(arrs):
    parts = []
    for a in arrs:
        flat = a.reshape(-1).astype(f32)
        npad = -(-flat.shape[0] // _TILE_ELEMS) * _TILE_ELEMS
        parts.append(jnp.pad(flat, (0, npad - flat.shape[0])).reshape(npad // 128, 128))
    return jnp.concatenate(parts, axis=0)


def _unpack(buf, shapes):
    return [buf[row:row + nr].reshape(-1)[:int(np.prod(s))].reshape(s) for s, (row, nr) in zip(shapes, _pack_rows(shapes))]


_REPL = ("rel_bias", "ada_b", "ssm_a_re", "ssm_a_im", "ssm_log_dt", "ssm_b_re", "ssm_b_im", "ssm_c_re", "ssm_c_im",
         "ssm_d", "glu_b", "pool_w", "pool_scale")
_SMALL_SHARDED = ("ln_g", "ln_b", "glu_w")
_BIG = ("ffn_w_gate", "ffn_w_up", "ffn_w_down", "w_in", "w_out")
_ORDER = ("rel_bias", "ada_w", "ada_b", "ln_g", "ln_b", "ffn_w_gate", "ffn_w_up", "ffn_w_down", "w_in", "w_out",
          "ssm_a_re", "ssm_a_im", "ssm_log_dt", "ssm_b_re", "ssm_b_im", "ssm_c_re", "ssm_c_im", "ssm_d", "glu_w",
          "glu_b", "pool_w", "pool_scale")


def kernel(x, c, rel_bias, ada_w, ada_b, ln_g, ln_b, ffn_w_gate, ffn_w_up, ffn_w_down, w_in, w_out, ssm_a_re, ssm_a_im, ssm_log_dt, ssm_b_re, ssm_b_im, ssm_c_re, ssm_c_im, ssm_d, glu_w, glu_b, pool_w, pool_scale, loss_target, m_rel_bias, m_ada_w, m_ada_b, m_ln_g, m_ln_b, m_ffn_w_gate, m_ffn_w_up, m_ffn_w_down, m_w_in, m_w_out, m_ssm_a_re, m_ssm_a_im, m_ssm_log_dt, m_ssm_b_re, m_ssm_b_im, m_ssm_c_re, m_ssm_c_im, m_ssm_d, m_glu_w, m_glu_b, m_pool_w, m_pool_scale, v_rel_bias, v_ada_w, v_ada_b, v_ln_g, v_ln_b, v_ffn_w_gate, v_ffn_w_up, v_ffn_w_down, v_w_in, v_w_out, v_ssm_a_re, v_ssm_a_im, v_ssm_log_dt, v_ssm_b_re, v_ssm_b_im, v_ssm_c_re, v_ssm_c_im, v_ssm_d, v_glu_w, v_glu_b, v_pool_w, v_pool_scale):
    args = dict(locals())
    w = {k: args[k] for k in _ORDER}
    m = {k: args["m_" + k] for k in _ORDER}
    v = {k: args["v_" + k] for k in _ORDER}
    L, D = DEPTH, D_MODEL
    ax, ay, ac = lax.axis_index("x"), lax.axis_index("y"), lax.axis_index("c")
    p_me = 2 * ax + ay
    dev = 4 * ax + 2 * ay + ac

    transposed = ("ffn_w_gate", "ffn_w_up")
    for d in (w, m, v):
        for name in transposed:
            d[name] = jnp.swapaxes(d[name], 2, 3)

    def halves(t):
        return t.astype(bf16).reshape(1, 2, t.shape[0] // 2, t.shape[1])

    def landing(src):
        return lax.dynamic_update_slice(lax.empty((N_CHIPS,) + src.shape, bf16), src[None], (p_me, 0, 0, 0, 0))

    chunk_keys = [("ffn", 0, 0), ("mix", 0), ("ffn", 0, 1), ("ffn", 1, 0), ("mix", 1), ("ffn", 1, 1)]
    chunk_srcs = []
    for key in chunk_keys:
        if key[0] == "ffn":
            chunk_srcs.append([halves(w[name][key[1], key[2]]) for name in ("ffn_w_gate", "ffn_w_up", "ffn_w_down")])
        else:
            chunk_srcs.append([halves(w_in[key[1]]), halves(w_out[key[1]])])

    pack = _pack([c, ln_g, ln_b, glu_w])
    rows = pack.shape[0]
    allp = _allgather8(pack).reshape(8, rows, 128)
    chunks = [(srcs, [landing(t) for t in srcs]) for srcs in chunk_srcs]
    first_in_flight, first_begun = _gather_start(chunks[:1], allp, "gather_start_first")
    c_all = allp[:, :8].reshape(8, D) + first_begun[0, 0]
    by_chip = allp[0::2]

    fwd_rows = _pack_rows([c.shape, ln_g.shape, ln_b.shape, glu_w.shape])

    def sharded(part, shape, axis):
        row0, nrows = fwd_rows[part]
        t = by_chip[:, row0:row0 + nrows].reshape(N_CHIPS, -1)[:, :int(np.prod(shape))].reshape((N_CHIPS,) + shape)
        return jnp.concatenate([t[p] for p in range(N_CHIPS)], axis=axis)

    ln_g_full = sharded(1, ln_g.shape, 2)
    ln_b_full = sharded(2, ln_b.shape, 2)
    glu_w_full = sharded(3, glu_w.shape, 1)

    ncol = ada_w.shape[-1]
    ada_b_cols = lax.dynamic_slice_in_dim(ada_b, p_me * ncol, ncol, axis=1)[:, None, :]
    mod_part = _ada_fwd(c_all, ada_w, ada_b_cols)
    mrows = L * 8 * ncol // 128
    mod_pack = mod_part.reshape(mrows, 128)
    mod_land = lax.dynamic_update_slice(lax.empty((N_CHIPS, 2, mrows, 128), f32), mod_pack[None, None], (p_me, ac, 0, 0))
    mod_in_flight, _ = _plane_start(mod_pack, mod_land, "mod_start")
    att_bias = _bias_fwd(rel_bias, jnp.asarray(_bucket_table()))
    small_names = _REPL + _SMALL_SHARDED
    small_packs = [_pack([d[k] for k in small_names]) for d in (w, m, v)]
    mod_land = _plane_wait(*mod_in_flight, [att_bias] + small_packs + [t for _, lands in chunks[1:] for t in lands], "mod_wait")
    mod_all = _swap_halves([mod_land])[0].reshape(8, L, 8, ncol)
    mod_mine = lax.dynamic_index_in_dim(mod_all, dev, axis=2, keepdims=False)
    mod = jnp.concatenate([mod_mine[2 * p] for p in range(N_CHIPS)], axis=-1).reshape(L, 9, D)

    rest_in_flight, rest_begun = _gather_start(chunks[1:], mod, "gather_start_rest")
    in_flight = first_in_flight + rest_in_flight

    forwarding = {}

    def forward(k, after):
        lands = _gather_wait(*in_flight[k], [after, rest_begun], "gather_wait_%d" % k)
        forwarding[k], begun = _split_start(_forward_copies, lands, 3 * len(lands), "gather_forward_start_%d" % k)
        return begun

    def gathered(key, after):
        k = chunk_keys.index(key)
        order = [after]
        if k not in forwarding:
            order.append(forward(k, after))
        if 3 <= k + 1 < len(chunk_keys):
            order.append(forward(k + 1, after))
        lands = _split_wait(_forward_copies, *forwarding[k], order, "gather_forward_wait_%d" % k)
        return [t.reshape(N_CHIPS, 1, 2 * t.shape[3], t.shape[4]) for t in lands]

    pc = jnp.stack([p_me, ac]).astype(jnp.int32)
    groups = {}
    scattering = {}

    pairing = {}

    def start_pairs(tag, after=()):
        g5 = [g.reshape(g.shape[:2] + (2, g.shape[2] // 2, g.shape[3])) for _, _, g in groups[tag]]
        gots = [lax.empty(g.shape[:2] + g.shape[3:], bf16) for g in g5]
        pairing[tag], begun = _split_start(_pair_copies, g5 + gots, len(g5), "pair_exchange_start_%s" % tag, after)
        return begun

    def start_group(tag, after):
        arrays = _split_wait(_pair_copies, *pairing[tag], after, "pair_exchange_wait_%s" % tag)
        n = len(arrays) // 2
        hsum = _pair_sum(arrays[:n], arrays[n:], pc)
        scattering[tag], begun = _scatter_start(hsum, "scatter_start_%s" % tag)
        return begun

    def grads_done(l, s, grads):
        if l == 1:
            groups.setdefault("l1", []).extend(grads)
            return start_pairs("l1")[0, 0] if s == 0 else jnp.zeros((), f32)
        groups["l0a" if s == 1 else "l0b"] = grads
        return start_pairs("l0a")[0, 0] if s == 1 else jnp.zeros((), f32)

    swapping = {}

    def reduce_group(tag, after):
        hsum, recv = _scatter_wait(*scattering[tag], after, "scatter_wait_%s" % tag)
        full = _sum_shards(hsum, recv, pc)
        swapping[tag], begun = _split_start(_swap_copies, full, len(full), "swap_halves_start_%s" % tag)
        return [begun]

    def ffn_bwd_issued(l, s, dx):
        if l == 1:
            return jnp.zeros((), f32)
        if s == 1:
            return start_group("l1", [dx])[0, 0]
        return start_group("l0a", [dx])[0, 0] + reduce_group("l1", [dx])[0][0, 0]

    small = {k: w[k] for k in _REPL if k != "ada_b"}
    small.update(ln_g=ln_g_full, ln_b=ln_b_full, glu_w=glu_w_full, att_bias=att_bias)
    loss_dev, grad_x, dmod, sgrads = _local_step(
        x[0], loss_target[0], mod, small, lambda l, s, after: gathered(("ffn", l, s), after),
        lambda l, after: gathered(("mix", l), after), grads_done, ffn_bwd_issued)

    names = ("rel_bias", "ln_g", "ln_b", "ssm_a_re", "ssm_a_im", "ssm_log_dt", "ssm_b_re", "ssm_b_im", "ssm_c_re",
             "ssm_c_im", "ssm_d", "glu_w", "glu_b", "pool_w", "pool_scale")
    gpack = _pack([dmod] + [sgrads[k] for k in names] + [loss_dev])
    grows = gpack.shape[0]
    land = lax.dynamic_update_slice(lax.empty((N_CHIPS, 2, grows, 128), f32), gpack[None, None], (p_me, ac, 0, 0))
    small_in_flight, small_begun = _plane_start(gpack, land, "small_grads_start")
    l0b_begun = start_group("l0b", [start_pairs("l0b", (small_begun,))])

    out_g, out_d, out_m, out_v = {}, {}, {}, {}
    row_tile = dict(zip(_BIG, (352, 352, 352, 256, 256)))
    big = {name: None for name in _BIG}

    def update_group(tag, after):
        full = _split_wait(_swap_copies, *swapping[tag], after, "swap_halves_wait_%s" % tag)
        for (name, row0, _), g in zip(groups[tag], full):
            shp = w[name].shape
            r2 = (int(np.prod(shp[:-1])), shp[-1])
            big[name] = _adamw(w[name].reshape(r2), m[name].reshape(r2), v[name].reshape(r2), g.reshape(-1, shp[-1]),
                               row_tile[name], row0, big[name])
        return [big[name][1] for name, _, _ in groups[tag]]

    after = reduce_group("l0a", update_group("l1", [grad_x, l0b_begun]))

    land = _plane_wait(*small_in_flight, after, "small_grads_wait")
    gall = _swap_halves([land])[0].reshape(8, grows, 128)
    gsum = _unpack(_sum8(gall), [(L, 9 * D)] + [sgrads[k].shape for k in names] + [(1, 1)])
    loss = gsum[-1][0, 0]
    red = dict(zip(("ada_b",) + names, gsum))
    red["ln_g"] = lax.dynamic_slice_in_dim(red["ln_g"], p_me * 256, 256, axis=2)
    red["ln_b"] = lax.dynamic_slice_in_dim(red["ln_b"], p_me * 256, 256, axis=2)
    red["glu_w"] = lax.dynamic_slice_in_dim(red["glu_w"], p_me * 64, 64, axis=1)

    dmod_all = gall[:, :L * 9 * D // 128].reshape(8, L, 9 * D)
    dmod_cols = jnp.transpose(lax.dynamic_slice_in_dim(dmod_all, p_me * ncol, ncol, axis=2), (1, 0, 2))
    g_ada_w = _ada_wgrad(jnp.transpose(c_all), dmod_cols)

    r2 = (L * D, ncol)
    res = _adamw(ada_w.reshape(r2), m["ada_w"].reshape(r2), v["ada_w"].reshape(r2), g_ada_w.reshape(r2), 128)
    out_g["ada_w"], out_d["ada_w"], out_m["ada_w"], out_v["ada_w"] = [t.reshape(ada_w.shape) for t in res]

    res_small = _adamw(*small_packs, _pack([red[k] for k in small_names]), small_packs[0].shape[0])
    for t, dst in zip(res_small, (out_g, out_d, out_m, out_v)):
        for k, a in zip(small_names, _unpack(t, [w[k].shape for k in small_names])):
            dst[k] = a

    update_group("l0b", reduce_group("l0b", update_group("l0a", [res_small[1], res[1]])))
    for name in _BIG:
        res = [t.reshape(w[name].shape) for t in big[name]]
        out_g[name], out_d[name], out_m[name], out_v[name] = [jnp.swapaxes(t, 2, 3) for t in res] if name in transposed else res

    return (loss, grad_x[None], *[out_g[k] for k in _ORDER], *[out_d[k] for k in _ORDER],
            *[out_m[k] for k in _ORDER], *[out_v[k] for k in _ORDER])
```

```python
import math

import numpy as np
import jax
import jax.numpy as jnp
from jax import lax
from jax.experimental import pallas as pl
from jax.experimental.pallas import tpu as pltpu

f32 = jnp.float32
bf16 = jnp.bfloat16
MESH = pl.DeviceIdType.MESH

D_MODEL = 1024
SEQ = 2048
DEPTH = 2
HEAD_DIM = 64
N_HEADS = 8
D_ATT = 512
DILATIONS = (1, 4, 16)
BLOCKS_PER_RESIDUE = (16, 4, 1)
ATT_BLOCK = 128
N_UNITS = SEQ // ATT_BLOCK
N_GROUPS = 16
SSM_STATE = 64
D_SSM = 256
D_STATE = N_GROUPS * SSM_STATE
POOL_WINDOWS = (2, 4, 8, 16)
POOL_GROUP = 64
D_POOL = 256
POOL_HALO = 16
D_FF = 2816
N_BUCKETS = 32
MAX_DISTANCE = 2048
ALPHA = (2 * DEPTH) ** 0.25
FFN_RES = 0.5
LN_EPS = 1e-5
NEG = -1e30
N_CHIPS = 4
FF_SHARD = D_FF // N_CHIPS
SCAN_SEG = 8

ADAM_LR, ADAM_B1, ADAM_B2, ADAM_EPS, ADAM_WD, ADAM_STEP = 0.001, 0.9, 0.999, 1e-08, 0.01, 10

TOK_TILE = 512


def _cp(dims=None, vmem_mb=None):
    kw = {}
    if dims is not None:
        kw["dimension_semantics"] = dims
    if vmem_mb is not None:
        kw["vmem_limit_bytes"] = vmem_mb << 20
    return pltpu.CompilerParams(**kw)


def _dot(a, b):
    return jnp.dot(a, b, preferred_element_type=f32)


def _dot_nt(a, b):
    return lax.dot_general(a, b, (((1,), (1,)), ((), ())), preferred_element_type=f32)


def _dot_tn(a, b):
    return lax.dot_general(a, b, (((0,), (0,)), ((), ())), preferred_element_type=f32)


def _ln_stats(v):
    mu = jnp.mean(v, -1, keepdims=True)
    d = v - mu
    var = jnp.mean(d * d, -1, keepdims=True)
    rstd = lax.rsqrt(var + LN_EPS)
    return d * rstd, rstd


def _ln_bwd(dxh, xh, rstd):
    return rstd * (dxh - jnp.mean(dxh, -1, keepdims=True) - xh * jnp.mean(dxh * xh, -1, keepdims=True))


_GELU_C = math.sqrt(2.0 / math.pi)


def _gelu(y):
    return 0.5 * y * (1.0 + jnp.tanh(_GELU_C * (y + 0.044715 * y * y * y)))


def _gelu_grad(y):
    t = jnp.tanh(_GELU_C * (y + 0.044715 * y * y * y))
    return 0.5 * (1.0 + t) + 0.5 * y * (1.0 - t * t) * (_GELU_C * (1.0 + 3 * 0.044715 * y * y))


def _full(shape):
    return pl.BlockSpec(shape, lambda *_: (0,) * len(shape))


def _hbm(*args):
    return [pltpu.with_memory_space_constraint(a, pltpu.HBM) if getattr(a, "ndim", 0) >= 2 else a for a in args]


def _ffn_fwd(x, mod3, wg, wu, wd, ls, lng, lnb):
    S, D = x.shape
    Fs = wg.shape[-2]
    ts = TOK_TILE

    def body(x_ref, mod_ref, wg_ref, wu_ref, wd_ref, lng_ref, lnb_ref, xo_ref, f_ref, g_ref, u_ref, h_ref, acc_sc):
        j = pl.program_id(1)

        @pl.when(j == 0)
        def _():
            xh, _ = _ln_stats(x_ref[...])
            h_ref[...] = (xh * (1.0 + mod_ref[1:2, :]) + mod_ref[0:1, :]).astype(bf16)
            acc_sc[...] = jnp.zeros_like(acc_sc)

        h = h_ref[...]
        g = _dot_nt(h, wg_ref[0, 0])
        u = _dot_nt(h, wu_ref[0, 0])
        g_ref[0] = g.astype(bf16)
        u_ref[0] = u.astype(bf16)
        a = (g * jax.nn.sigmoid(g) * u).astype(bf16)
        acc_sc[...] += _dot(a, wd_ref[0, 0])

        @pl.when(j == N_CHIPS - 1)
        def _():
            f = acc_sc[...]
            f_ref[...] = f
            r = ALPHA * x_ref[...] + (FFN_RES * mod_ref[2:3, :]) * f
            rh, _ = _ln_stats(r)
            xo_ref[...] = rh * lng_ref[...] + lnb_ref[...]

    tok = pl.BlockSpec((ts, D), lambda i, j: (i, 0))
    wrow = pl.BlockSpec((1, 1, Fs, D), lambda i, j: (j, ls, 0, 0))
    hid = pl.BlockSpec((1, ts, Fs), lambda i, j: (j, i, 0))
    return pl.pallas_call(
        body, name="ffn_fwd", grid=(S // ts, N_CHIPS),
        in_specs=[tok, _full((3, D)), wrow, wrow, wrow, _full((1, D)), _full((1, D))],
        out_specs=[tok, tok, hid, hid, tok],
        out_shape=[jax.ShapeDtypeStruct((S, D), f32), jax.ShapeDtypeStruct((S, D), f32),
                   jax.ShapeDtypeStruct((N_CHIPS, S, Fs), bf16), jax.ShapeDtypeStruct((N_CHIPS, S, Fs), bf16),
                   jax.ShapeDtypeStruct((S, D), bf16)],
        scratch_shapes=[pltpu.VMEM((ts, D), f32)],
        compiler_params=_cp(("parallel", "arbitrary"), 56),
    )(*_hbm(x, mod3, wg, wu, wd, lng, lnb))


def _ffn_bwd(dxo, x, f, g, u, mod3, wg, wu, wd, ls, lng):
    S, D = x.shape
    Fs = wg.shape[-2]
    ts = TOK_TILE

    def body(dxo_ref, x_ref, f_ref, g_ref, u_ref, mod_ref, wg_ref, wu_ref, wd_ref, lng_ref,
             dx_ref, dg_ref, du_ref, a_ref, df_ref, dmod_ref, dlng_ref, dlnb_ref,
             dr_sc, df_sc, acc_sc):
        i = pl.program_id(0)
        j = pl.program_id(1)

        @pl.when((i == 0) & (j == 0))
        def _():
            dmod_ref[...] = jnp.zeros_like(dmod_ref)
            dlng_ref[...] = jnp.zeros_like(dlng_ref)
            dlnb_ref[...] = jnp.zeros_like(dlnb_ref)

        @pl.when(j == 0)
        def _():
            xv = x_ref[...]
            fv = f_ref[...]
            gate = mod_ref[2:3, :]
            rh, rstd = _ln_stats(ALPHA * xv + (FFN_RES * gate) * fv)
            dy = dxo_ref[...]
            dlng_ref[...] += jnp.sum(dy * rh, 0, keepdims=True)
            dlnb_ref[...] += jnp.sum(dy, 0, keepdims=True)
            dr = _ln_bwd(dy * lng_ref[...], rh, rstd)
            dr_sc[...] = dr
            dmod_ref[2:3, :] += jnp.sum(FFN_RES * dr * fv, 0, keepdims=True)
            df = ((FFN_RES * gate) * dr).astype(bf16)
            df_sc[...] = df
            df_ref[...] = df
            acc_sc[...] = jnp.zeros_like(acc_sc)

        da = _dot_nt(df_sc[...], wd_ref[0, 0])
        gv = g_ref[0].astype(f32)
        uv = u_ref[0].astype(f32)
        sg = jax.nn.sigmoid(gv)
        si = gv * sg
        a_ref[0] = (si * uv).astype(bf16)
        dgv = (da * uv * (sg * (1.0 + gv * (1.0 - sg)))).astype(bf16)
        duv = (da * si).astype(bf16)
        dg_ref[0] = dgv
        du_ref[0] = duv
        acc_sc[...] += _dot(dgv, wg_ref[0, 0]) + _dot(duv, wu_ref[0, 0])

        @pl.when(j == N_CHIPS - 1)
        def _():
            dh = acc_sc[...]
            xh, rstd0 = _ln_stats(x_ref[...])
            dmod_ref[0:1, :] += jnp.sum(dh, 0, keepdims=True)
            dmod_ref[1:2, :] += jnp.sum(dh * xh, 0, keepdims=True)
            dx_ref[...] = _ln_bwd(dh * (1.0 + mod_ref[1:2, :]), xh, rstd0) + ALPHA * dr_sc[...]

    tok = pl.BlockSpec((ts, D), lambda i, j: (i, 0))
    wrow = pl.BlockSpec((1, 1, Fs, D), lambda i, j: (j, ls, 0, 0))
    hid = pl.BlockSpec((1, ts, Fs), lambda i, j: (j, i, 0))
    hid_shape = jax.ShapeDtypeStruct((N_CHIPS, S, Fs), bf16)
    return pl.pallas_call(
        body, name="ffn_bwd", grid=(S // ts, N_CHIPS),
        in_specs=[tok, tok, tok, hid, hid, _full((3, D)), wrow, wrow, wrow, _full((1, D))],
        out_specs=[tok, hid, hid, hid, tok, _full((3, D)), _full((1, D)), _full((1, D))],
        out_shape=[jax.ShapeDtypeStruct((S, D), f32), hid_shape, hid_shape, hid_shape,
                   jax.ShapeDtypeStruct((S, D), bf16),
                   jax.ShapeDtypeStruct((3, D), f32), jax.ShapeDtypeStruct((1, D), f32), jax.ShapeDtypeStruct((1, D), f32)],
        scratch_shapes=[pltpu.VMEM((ts, D), f32), pltpu.VMEM((ts, D), bf16), pltpu.VMEM((ts, D), f32)],
        compiler_params=_cp(("arbitrary", "arbitrary"), 56),
    )(*_hbm(dxo, x, f, g, u, mod3, wg, wu, wd, lng))


def _ffn_wgrad(h, dg, du, a, df, gwg, gwu, gwd, ls):
    S, D = h.shape
    Fs = dg.shape[-1]
    tk = 2 * TOK_TILE
    nk = S // tk

    def body(h_ref, dg_ref, du_ref, a_ref, df_ref, _g0, _g1, _g2, gwg_ref, gwu_ref, gwd_ref, ag_sc, au_sc, ad_sc):
        k = pl.program_id(1)

        @pl.when(k == 0)
        def _():
            ag_sc[...] = jnp.zeros_like(ag_sc)
            au_sc[...] = jnp.zeros_like(au_sc)
            ad_sc[...] = jnp.zeros_like(ad_sc)

        hv = h_ref[...]
        ag_sc[...] += _dot_tn(dg_ref[0], hv)
        au_sc[...] += _dot_tn(du_ref[0], hv)
        ad_sc[...] += _dot_tn(a_ref[0], df_ref[...])

        @pl.when(k == nk - 1)
        def _():
            gwg_ref[0, 0] = ag_sc[...].astype(bf16)
            gwu_ref[0, 0] = au_sc[...].astype(bf16)
            gwd_ref[0, 0] = ad_sc[...].astype(bf16)

    tok = pl.BlockSpec((tk, D), lambda p, k: (k, 0))
    hid = pl.BlockSpec((1, tk, Fs), lambda p, k: (p, k, 0))
    anyspec = pl.BlockSpec(memory_space=pl.ANY)
    orow = pl.BlockSpec((1, 1, Fs, D), lambda p, k: (p, ls, 0, 0))
    return pl.pallas_call(
        body, name="ffn_wgrad", grid=(N_CHIPS, nk),
        in_specs=[tok, hid, hid, hid, tok, anyspec, anyspec, anyspec],
        out_specs=[orow, orow, orow],
        out_shape=[jax.ShapeDtypeStruct(gwg.shape, bf16), jax.ShapeDtypeStruct(gwu.shape, bf16),
                   jax.ShapeDtypeStruct(gwd.shape, bf16)],
        scratch_shapes=[pltpu.VMEM((Fs, D), f32), pltpu.VMEM((Fs, D), f32), pltpu.VMEM((Fs, D), f32)],
        input_output_aliases={5: 0, 6: 1, 7: 2},
        compiler_params=_cp(("parallel", "arbitrary"), 48),
    )(*_hbm(h, dg, du, a, df, gwg, gwu, gwd))


_LANES = 128
_QKV_BLOCKS = D_ATT // _LANES


def _res_spec(lead, d, width, index):
    return pl.BlockSpec((lead, d, TOK_TILE // d, width), index)


def _res_spec3(d, width):
    return pl.BlockSpec((d, TOK_TILE // d, width), lambda i: (0, i, 0))


def _rows_to_residues(tile_bufs, d, put):
    for r in range(d):
        for cb, buf in enumerate(tile_bufs):
            put(r, cb, buf[pl.ds(r, TOK_TILE // d, stride=d), :])


def _residues_to_rows(tile_bufs, d, get):
    for r in range(d):
        for cb, buf in enumerate(tile_bufs):
            buf[pl.ds(r, TOK_TILE // d, stride=d), :] = get(r, cb)


def _mix_in_fwd(x, mod3, w_in, l):
    S, D = x.shape
    N = w_in.shape[-1]
    ts = TOK_TILE

    def body(x_ref, mod_ref, w_ref, o1_ref, o4_ref, o16_ref, zr_ref, h_ref, *bufs):
        j = pl.program_id(1)

        @pl.when(j == 0)
        def _():
            xh, _ = _ln_stats(x_ref[...])
            h_ref[...] = (xh * (1.0 + mod_ref[1:2, :]) + mod_ref[0:1, :]).astype(bf16)

        z = _dot(h_ref[...], w_ref[0, 0])

        @pl.when(j == N_CHIPS - 1)
        def _():
            zr_ref[...] = z

        @pl.when(j < N_CHIPS - 1)
        def _():
            zz = z * jnp.where(j == 0, HEAD_DIM ** -0.5, 1.0)
            o1_ref[j, 0] = zz.astype(bf16)
            for cb, buf in enumerate(bufs):
                buf[...] = zz[:, _LANES * cb:_LANES * (cb + 1)]
            for d, o_ref in zip(DILATIONS[1:], (o4_ref, o16_ref)):
                def put(r, cb, piece, o_ref=o_ref):
                    o_ref[j, r, :, _LANES * cb:_LANES * (cb + 1)] = piece.astype(bf16)
                _rows_to_residues(bufs, d, put)

    tok = pl.BlockSpec((ts, D), lambda i, j: (i, 0))
    res = [_res_spec(3, d, N, lambda i, j: (0, 0, i, 0)) for d in DILATIONS]
    return pl.pallas_call(
        body, name="mix_in_fwd", grid=(S // ts, N_CHIPS),
        in_specs=[tok, _full((3, D)), pl.BlockSpec((1, 1, D, N), lambda i, j: (j, l, 0, 0))],
        out_specs=res + [pl.BlockSpec((ts, N), lambda i, j: (i, 0)), tok],
        out_shape=[jax.ShapeDtypeStruct((3, d, S // d, N), bf16) for d in DILATIONS]
        + [jax.ShapeDtypeStruct((S, N), f32), jax.ShapeDtypeStruct((S, D), bf16)],
        scratch_shapes=[pltpu.VMEM((ts, _LANES), f32)] * _QKV_BLOCKS,
        compiler_params=_cp(("parallel", "arbitrary"), 40),
    )(*_hbm(x, mod3, w_in))


def _mix_in_bwd(dqkv, d_rest, dx_res, x, mod3, w_in, l):
    S, D = x.shape
    N = w_in.shape[-1]
    ts = TOK_TILE

    def body(d1_ref, d4_ref, d16_ref, dr_ref, dxr_ref, x_ref, mod_ref, w_ref, dx_ref, dmod_ref, dz_ref, acc_sc, *bufs):
        i = pl.program_id(0)
        j = pl.program_id(1)

        @pl.when((i == 0) & (j == 0))
        def _():
            dmod_ref[...] = jnp.zeros_like(dmod_ref)

        @pl.when(j == 0)
        def _():
            acc_sc[...] = jnp.zeros_like(acc_sc)

        @pl.when(j == N_CHIPS - 1)
        def _():
            dz_ref[0] = dr_ref[...]

        @pl.when(j < N_CHIPS - 1)
        def _():
            for d, d_ref, tile_bufs in ((4, d4_ref, bufs[:_QKV_BLOCKS]), (16, d16_ref, bufs[_QKV_BLOCKS:])):
                _residues_to_rows(tile_bufs, d, lambda r, cb, d_ref=d_ref: d_ref[0, r, :, _LANES * cb:_LANES * (cb + 1)].astype(f32))
            for cb in range(_QKV_BLOCKS):
                cols = slice(_LANES * cb, _LANES * (cb + 1))
                dz_ref[0, :, cols] = (d1_ref[0, 0, :, cols].astype(f32) + bufs[cb][...] + bufs[_QKV_BLOCKS + cb][...]).astype(bf16)

        acc_sc[...] += _dot_nt(dz_ref[0], w_ref[0, 0])

        @pl.when(j == N_CHIPS - 1)
        def _():
            dh = acc_sc[...]
            xh, rstd0 = _ln_stats(x_ref[...])
            dmod_ref[0:1, :] += jnp.sum(dh, 0, keepdims=True)
            dmod_ref[1:2, :] += jnp.sum(dh * xh, 0, keepdims=True)
            dx_ref[...] = _ln_bwd(dh * (1.0 + mod_ref[1:2, :]), xh, rstd0) + dxr_ref[...]

    tok = pl.BlockSpec((ts, D), lambda i, j: (i, 0))
    res = [_res_spec(1, d, N, lambda i, j: (jnp.minimum(j, 2), 0, i, 0)) for d in DILATIONS]
    return pl.pallas_call(
        body, name="mix_in_bwd", grid=(S // ts, N_CHIPS),
        in_specs=res + [pl.BlockSpec((ts, N), lambda i, j: (i, 0)), tok, tok, _full((3, D)),
                        pl.BlockSpec((1, 1, D, N), lambda i, j: (j, l, 0, 0))],
        out_specs=[tok, _full((3, D)), pl.BlockSpec((1, ts, N), lambda i, j: (j, i, 0))],
        out_shape=[jax.ShapeDtypeStruct((S, D), f32), jax.ShapeDtypeStruct((3, D), f32),
                   jax.ShapeDtypeStruct((N_CHIPS, S, N), bf16)],
        scratch_shapes=[pltpu.VMEM((ts, D), f32)] + [pltpu.VMEM((ts, _LANES), f32)] * (2 * _QKV_BLOCKS),
        compiler_params=_cp(("arbitrary", "arbitrary"), 40),
    )(*_hbm(*dqkv, d_rest, dx_res, x, mod3, w_in))


def _mix_in_wgrad(h, dz, gw, l):
    S, D = h.shape
    N = dz.shape[-1]
    tk = TOK_TILE
    nk = S // tk

    def body(h_ref, dz_ref, _g, gw_ref, acc_sc):
        k = pl.program_id(1)

        @pl.when(k == 0)
        def _():
            acc_sc[...] = jnp.zeros_like(acc_sc)

        acc_sc[...] += _dot_tn(h_ref[...], dz_ref[0])

        @pl.when(k == nk - 1)
        def _():
            gw_ref[0, 0] = acc_sc[...].astype(bf16)

    return pl.pallas_call(
        body, name="mix_in_wgrad", grid=(N_CHIPS, nk),
        in_specs=[pl.BlockSpec((tk, D), lambda p, k: (k, 0)), pl.BlockSpec((1, tk, N), lambda p, k: (p, k, 0)),
                  pl.BlockSpec(memory_space=pl.ANY)],
        out_specs=pl.BlockSpec((1, 1, D, N), lambda p, k: (p, l, 0, 0)),
        out_shape=jax.ShapeDtypeStruct(gw.shape, bf16),
        scratch_shapes=[pltpu.VMEM((D, N), f32)],
        input_output_aliases={2: 0},
        compiler_params=_cp(("parallel", "arbitrary"), 40),
    )(*_hbm(h, dz, gw))


def _mix_out_fwd(x, y_att, y_ssm, y_pool, mod3, w_out, l, lng, lnb):
    S, D = x.shape
    ts = TOK_TILE

    def body(x_ref, ya_ref, ys_ref, yp_ref, mod_ref, w_ref, lng_ref, lnb_ref, xo_ref, y_ref):
        ya = ya_ref[...].astype(bf16)
        y = (_dot(ya[:, 0:256], w_ref[0, 0]) + _dot(ya[:, 256:512], w_ref[1, 0])
             + _dot(ys_ref[...].astype(bf16), w_ref[2, 0]) + _dot(yp_ref[...].astype(bf16), w_ref[3, 0]))
        y_ref[...] = y
        rh, _ = _ln_stats(ALPHA * x_ref[...] + mod_ref[2:3, :] * y)
        xo_ref[...] = rh * lng_ref[...] + lnb_ref[...]

    tok = pl.BlockSpec((ts, D), lambda i: (i, 0))
    return pl.pallas_call(
        body, name="mix_out_fwd", grid=(S // ts,),
        in_specs=[tok, pl.BlockSpec((ts, D_ATT), lambda i: (i, 0)), pl.BlockSpec((ts, D_SSM), lambda i: (i, 0)),
                  pl.BlockSpec((ts, D_POOL), lambda i: (i, 0)), _full((3, D)),
                  pl.BlockSpec((N_CHIPS, 1, 256, D), lambda i: (0, l, 0, 0)), _full((1, D)), _full((1, D))],
        out_specs=[tok, tok],
        out_shape=[jax.ShapeDtypeStruct((S, D), f32), jax.ShapeDtypeStruct((S, D), f32)],
        compiler_params=_cp(("parallel",), 40),
    )(*_hbm(x, y_att, y_ssm, y_pool, mod3, w_out, lng, lnb))


def _mix_out_bwd(dxo, x, y, y_att, y_ssm, y_pool, mod3, w_out, l, lng, gw_out):
    S, D = x.shape
    ts = TOK_TILE
    nt = S // ts

    def body(dxo_ref, x_ref, y_ref, ya_ref, ys_ref, yp_ref, mod_ref, w_ref, lng_ref, _g,
             dxr_ref, da_ref, ds_ref, dp_ref, dgate_ref, dlng_ref, dlnb_ref, gw_ref, acc_sc):
        i = pl.program_id(0)

        @pl.when(i == 0)
        def _():
            dgate_ref[...] = jnp.zeros_like(dgate_ref)
            dlng_ref[...] = jnp.zeros_like(dlng_ref)
            dlnb_ref[...] = jnp.zeros_like(dlnb_ref)
            acc_sc[...] = jnp.zeros_like(acc_sc)

        gate = mod_ref[2:3, :]
        yv = y_ref[...]
        rh, rstd = _ln_stats(ALPHA * x_ref[...] + gate * yv)
        dy_out = dxo_ref[...]
        dlng_ref[...] += jnp.sum(dy_out * rh, 0, keepdims=True)
        dlnb_ref[...] += jnp.sum(dy_out, 0, keepdims=True)
        dr = _ln_bwd(dy_out * lng_ref[...], rh, rstd)
        dxr_ref[...] = ALPHA * dr
        dgate_ref[...] += jnp.sum(dr * yv, 0, keepdims=True)
        dy = (gate * dr).astype(bf16)
        da_ref[:, 0:256] = _dot_nt(dy, w_ref[0, 0])
        da_ref[:, 256:512] = _dot_nt(dy, w_ref[1, 0])
        ds_ref[...] = _dot_nt(dy, w_ref[2, 0])
        dp_ref[...] = _dot_nt(dy, w_ref[3, 0])
        ya = ya_ref[...].astype(bf16)
        acc_sc[0] += _dot_tn(ya[:, 0:256], dy)
        acc_sc[1] += _dot_tn(ya[:, 256:512], dy)
        acc_sc[2] += _dot_tn(ys_ref[...].astype(bf16), dy)
        acc_sc[3] += _dot_tn(yp_ref[...].astype(bf16), dy)

        @pl.when(i == nt - 1)
        def _():
            gw_ref[:, 0] = acc_sc[...].astype(bf16)

    tok = pl.BlockSpec((ts, D), lambda i: (i, 0))
    t512 = pl.BlockSpec((ts, D_ATT), lambda i: (i, 0))
    t256 = pl.BlockSpec((ts, 256), lambda i: (i, 0))
    wspec = pl.BlockSpec((N_CHIPS, 1, 256, D), lambda i: (0, l, 0, 0))
    return pl.pallas_call(
        body, name="mix_out_bwd", grid=(nt,),
        in_specs=[tok, tok, tok, t512, t256, t256, _full((3, D)), wspec, _full((1, D)), pl.BlockSpec(memory_space=pl.ANY)],
        out_specs=[tok, t512, t256, t256, _full((1, D)), _full((1, D)), _full((1, D)), wspec],
        out_shape=[jax.ShapeDtypeStruct((S, D), f32), jax.ShapeDtypeStruct((S, D_ATT), f32),
                   jax.ShapeDtypeStruct((S, D_SSM), f32), jax.ShapeDtypeStruct((S, D_POOL), f32),
                   jax.ShapeDtypeStruct((1, D), f32), jax.ShapeDtypeStruct((1, D), f32), jax.ShapeDtypeStruct((1, D), f32),
                   jax.ShapeDtypeStruct(gw_out.shape, bf16)],
        scratch_shapes=[pltpu.VMEM((N_CHIPS, 256, D), f32)],
        input_output_aliases={9: 7},
        compiler_params=_cp(("arbitrary",), 48),
    )(*_hbm(dxo, x, y, y_att, y_ssm, y_pool, mod3, w_out, lng, gw_out))


def _t5_bucket(dist):
    max_exact = N_BUCKETS // 2
    d = np.maximum(dist, 1).astype(np.float32)
    large = max_exact + (np.log(d / max_exact) / math.log(MAX_DISTANCE / max_exact)
                         * (N_BUCKETS - max_exact)).astype(np.int32)
    large = np.minimum(large, N_BUCKETS - 1)
    return np.where(dist < max_exact, dist, large).astype(np.int32)


def _bucket_table():
    q = ATT_BLOCK
    i = np.arange(q)[:, None]
    j = np.arange(2 * q)[None, :]
    r = i + q - j
    in_band = (r >= 0) & (r <= q)
    tabs = [np.where(in_band, _t5_bucket(np.clip(r, 0, None) * d), -1) for d in DILATIONS]
    return np.stack(tabs).astype(np.int32)


def _bias_fwd(rel_bias, table):
    def body(rb_ref, tab_ref, out_ref):
        for b in range(3):
            tb = tab_ref[b]
            for h in range(N_HEADS):
                def pick(k, acc):
                    return jnp.where(tb == k, rb_ref[k, h], acc)
                out_ref[b, h] = lax.fori_loop(0, N_BUCKETS, pick, jnp.where(tb < 0, NEG, 0.0).astype(f32))

    return pl.pallas_call(
        body, name="bias_fwd",
        in_specs=[pl.BlockSpec(memory_space=pltpu.SMEM), pl.BlockSpec(memory_space=pltpu.VMEM)],
        out_specs=pl.BlockSpec(memory_space=pltpu.VMEM),
        out_shape=jax.ShapeDtypeStruct((3, N_HEADS, ATT_BLOCK, 2 * ATT_BLOCK), f32),
    )(rel_bias, table)


def _bias_bwd(dbias, table):
    def body(db_ref, tab_ref, out_ref):
        def per_bucket(k, c):
            for h in range(N_HEADS):
                tot = jnp.zeros((), f32)
                for b in range(3):
                    tot = tot + jnp.sum(jnp.where(tab_ref[b] == k, db_ref[b, h], 0.0))
                out_ref[k, h] = tot
            return c
        lax.fori_loop(0, N_BUCKETS, per_bucket, 0)

    return pl.pallas_call(
        body, name="bias_bwd",
        in_specs=[pl.BlockSpec(memory_space=pltpu.VMEM), pl.BlockSpec(memory_space=pltpu.VMEM)],
        out_specs=pl.BlockSpec(memory_space=pltpu.SMEM),
        out_shape=jax.ShapeDtypeStruct((N_BUCKETS, N_HEADS), f32),
    )(dbias, table)


def _att_unit(u, nbr):
    rows = pl.ds(pl.multiple_of(u * ATT_BLOCK, ATT_BLOCK), ATT_BLOCK)
    prev = pl.ds(pl.multiple_of(jnp.maximum(u - 1, 0) * ATT_BLOCK, ATT_BLOCK), ATT_BLOCK)
    return rows, prev, (u % nbr) != 0


_N_PAIRS = N_HEADS // 2


def _pair_rows(t):
    lane = lax.broadcasted_iota(jnp.int32, t.shape, 1)
    zero = jnp.zeros_like(t)
    return jnp.concatenate([jnp.where(lane < HEAD_DIM, t, zero), jnp.where(lane >= HEAD_DIM, t, zero)], axis=0)


def _pair_cols(big):
    lane = lax.broadcasted_iota(jnp.int32, (ATT_BLOCK, _LANES), 1)
    return jnp.where(lane < HEAD_DIM, big[:ATT_BLOCK], big[ATT_BLOCK:])


def _pair_column(ref, rows, hp):
    t = ref[rows, :]
    return jnp.concatenate([t[:, 2 * hp:2 * hp + 1], t[:, 2 * hp + 1:2 * hp + 2]], axis=0)


def _pair_band(ref, rows, prev, nbr, hp):
    lanes = pl.ds(_LANES * hp, _LANES)
    cur = ref[0, rows, lanes]
    return cur if nbr == 1 else jnp.concatenate([ref[0, prev, lanes], cur], axis=0)


def _pair_scores(q_ref, k_ref, b_ref, rows, prev, valid_prev, nbr, hp):
    qbd = _pair_rows(q_ref[0, rows, pl.ds(_LANES * hp, _LANES)])
    kb = _pair_band(k_ref, rows, prev, nbr, hp)
    bias = b_ref[0, 2 * hp:2 * hp + 2].reshape(2 * ATT_BLOCK, 2 * ATT_BLOCK)
    if nbr == 1:
        return qbd, kb, _dot_nt(qbd, kb) + bias[:, ATT_BLOCK:]
    s = _dot_nt(qbd, kb) + bias
    col = lax.broadcasted_iota(jnp.int32, s.shape, 1)
    return qbd, kb, jnp.where((col >= ATT_BLOCK) | valid_prev, s, NEG)


def _qkv_specs(S, branch):
    return ([pl.BlockSpec((1, S, D_ATT), lambda i, t=t: (t, 0, 0)) for t in range(3)],
            pl.BlockSpec((1, N_HEADS, ATT_BLOCK, 2 * ATT_BLOCK), lambda i: (branch, 0, 0, 0)))


def _att_fwd(qkv, bias, branch):
    S = qkv.shape[1]
    nbr = BLOCKS_PER_RESIDUE[branch]

    def body(q_ref, k_ref, v_ref, b_ref, o_ref, lse_ref):
        lse_ref[...] = jnp.zeros_like(lse_ref)

        def unit(u, c):
            rows, prev, valid_prev = _att_unit(u, nbr)
            for hp in range(_N_PAIRS):
                _, _, s = _pair_scores(q_ref, k_ref, b_ref, rows, prev, valid_prev, nbr, hp)
                m = jnp.max(s, -1, keepdims=True)
                p = jnp.exp(s - m)
                den = jnp.sum(p, -1, keepdims=True)
                big = _dot(p.astype(bf16), _pair_band(v_ref, rows, prev, nbr, hp))
                o_ref[rows, pl.ds(_LANES * hp, _LANES)] = _pair_cols(big / den)
                lse = m + jnp.log(den)
                lse_ref[rows, pl.ds(2 * hp, 1)] = lse[:ATT_BLOCK]
                lse_ref[rows, pl.ds(2 * hp + 1, 1)] = lse[ATT_BLOCK:]
            return c

        lax.fori_loop(0, N_UNITS, unit, 0)

    qkv_specs, bspec = _qkv_specs(S, branch)
    return pl.pallas_call(
        body, name="att_fwd", grid=(1,),
        in_specs=qkv_specs + [bspec],
        out_specs=[pl.BlockSpec((S, D_ATT), lambda i: (0, 0)), pl.BlockSpec((S, _LANES), lambda i: (0, 0))],
        out_shape=[jax.ShapeDtypeStruct((S, D_ATT), f32), jax.ShapeDtypeStruct((S, _LANES), f32)],
        compiler_params=_cp(("arbitrary",), 40),
    )(*_hbm(qkv, qkv, qkv, bias))


def _att_bwd(qkv, do, lse, crow, bias, branch):
    S = qkv.shape[1]
    nbr = BLOCKS_PER_RESIDUE[branch]

    def body(q_ref, k_ref, v_ref, do_ref, lse_ref, c_ref, b_ref, dqkv_ref, db_ref, dk_sc, dv_sc):
        dk_sc[...] = jnp.zeros_like(dk_sc)
        dv_sc[...] = jnp.zeros_like(dv_sc)
        db_ref[...] = jnp.zeros_like(db_ref)

        def unit(u, c):
            rows, prev, valid_prev = _att_unit(u, nbr)
            for hp in range(_N_PAIRS):
                lanes = pl.ds(_LANES * hp, _LANES)
                qbd, kb, s = _pair_scores(q_ref, k_ref, b_ref, rows, prev, valid_prev, nbr, hp)
                p = jnp.exp(s - _pair_column(lse_ref, rows, hp))
                dobd = _pair_rows(do_ref[rows, lanes])
                ds = p * (_dot_nt(dobd, _pair_band(v_ref, rows, prev, nbr, hp)) - _pair_column(c_ref, rows, hp))
                if nbr == 1:
                    db_ref[2 * hp:2 * hp + 2, :, ATT_BLOCK:] += ds.reshape(2, ATT_BLOCK, ATT_BLOCK)
                else:
                    db_ref[2 * hp:2 * hp + 2] += ds.reshape(2, ATT_BLOCK, 2 * ATT_BLOCK)
                dsb = ds.astype(bf16)
                dqkv_ref[0, rows, lanes] = (HEAD_DIM ** -0.5 * _pair_cols(_dot(dsb, kb))).astype(bf16)
                dkb = _dot_tn(dsb, qbd)
                dvb = _dot_tn(p.astype(bf16), dobd)
                if nbr == 1:
                    dk_sc[rows, lanes] += dkb
                    dv_sc[rows, lanes] += dvb
                else:
                    dk_sc[prev, lanes] += dkb[:ATT_BLOCK]
                    dv_sc[prev, lanes] += dvb[:ATT_BLOCK]
                    dk_sc[rows, lanes] += dkb[ATT_BLOCK:]
                    dv_sc[rows, lanes] += dvb[ATT_BLOCK:]
            return c

        lax.fori_loop(0, N_UNITS, unit, 0)
        dqkv_ref[1] = dk_sc[...].astype(bf16)
        dqkv_ref[2] = dv_sc[...].astype(bf16)

    qkv_specs, bspec = _qkv_specs(S, branch)
    row = pl.BlockSpec((S, _LANES), lambda i: (0, 0))
    return pl.pallas_call(
        body, name="att_bwd", grid=(1,),
        in_specs=qkv_specs + [pl.BlockSpec((S, D_ATT), lambda i: (0, 0)), row, row, bspec],
        out_specs=[pl.BlockSpec((3, S, D_ATT), lambda i: (0, 0, 0)),
                   pl.BlockSpec((N_HEADS, ATT_BLOCK, 2 * ATT_BLOCK), lambda i: (0, 0, 0))],
        out_shape=[jax.ShapeDtypeStruct((3, S, D_ATT), bf16), jax.ShapeDtypeStruct((N_HEADS, ATT_BLOCK, 2 * ATT_BLOCK), f32)],
        scratch_shapes=[pltpu.VMEM((S, D_ATT), f32), pltpu.VMEM((S, D_ATT), f32)],
        compiler_params=_cp(("arbitrary",), 48),
    )(*_hbm(qkv, qkv, qkv, do, lse, crow, bias))


def _branch_weights(lse_ref):
    l0, l1, l2 = lse_ref[0], lse_ref[1], lse_ref[2]
    m = jnp.maximum(jnp.maximum(l0, l1), l2)
    e0, e1, e2 = jnp.exp(l0 - m), jnp.exp(l1 - m), jnp.exp(l2 - m)
    tot = e0 + e1 + e2
    return e0 / tot, e1 / tot, e2 / tot


def _att_merge(os, lses):
    S = os[0].shape[0] * os[0].shape[1]
    ts = TOK_TILE

    def body(o1_ref, o4_ref, o16_ref, l1_ref, l4_ref, l16_ref, y_ref, lt_ref, *bufs):
        obufs = (bufs[:_QKV_BLOCKS], bufs[_QKV_BLOCKS:2 * _QKV_BLOCKS])
        lt_ref[0] = l1_ref[0]
        for k, (d, o_ref, l_ref) in enumerate(((4, o4_ref, l4_ref), (16, o16_ref, l16_ref))):
            _residues_to_rows(obufs[k], d, lambda r, cb, o_ref=o_ref: o_ref[r, :, _LANES * cb:_LANES * (cb + 1)])
            _residues_to_rows([bufs[2 * _QKV_BLOCKS + k]], d, lambda r, cb, l_ref=l_ref: l_ref[r])
            lt_ref[1 + k] = bufs[2 * _QKV_BLOCKS + k][...]
        w = _branch_weights(lt_ref)
        for h in range(N_HEADS):
            cs = slice(HEAD_DIM * h, HEAD_DIM * (h + 1))
            half = slice(HEAD_DIM * (h % 2), HEAD_DIM * (h % 2 + 1))
            y_ref[:, cs] = (w[0][:, h:h + 1] * o1_ref[0, :, cs] + w[1][:, h:h + 1] * obufs[0][h // 2][:, half]
                            + w[2][:, h:h + 1] * obufs[1][h // 2][:, half])

    return pl.pallas_call(
        body, name="att_merge", grid=(S // ts,),
        in_specs=[_res_spec3(d, D_ATT) for d in DILATIONS] + [_res_spec3(d, _LANES) for d in DILATIONS],
        out_specs=[pl.BlockSpec((ts, D_ATT), lambda i: (i, 0)), pl.BlockSpec((3, ts, _LANES), lambda i: (0, i, 0))],
        out_shape=[jax.ShapeDtypeStruct((S, D_ATT), f32), jax.ShapeDtypeStruct((3, S, _LANES), f32)],
        scratch_shapes=[pltpu.VMEM((ts, _LANES), f32)] * (2 * _QKV_BLOCKS + 2),
        compiler_params=_cp(("parallel",)),
    )(*_hbm(*os, *lses))


def _att_merge_bwd(dy, y, lse3):
    S = dy.shape[0]
    ts = TOK_TILE

    def body(dy_ref, y_ref, lse_ref, do1_ref, do4_ref, do16_ref, c1_ref, c4_ref, c16_ref, *bufs):
        dobufs = (bufs[:_QKV_BLOCKS], bufs[_QKV_BLOCKS:2 * _QKV_BLOCKS], bufs[2 * _QKV_BLOCKS:3 * _QKV_BLOCKS])
        cbufs = bufs[3 * _QKV_BLOCKS:]
        w = _branch_weights(lse_ref)
        for cb in cbufs:
            cb[...] = jnp.zeros_like(cb)
        for h in range(N_HEADS):
            cs = slice(HEAD_DIM * h, HEAD_DIM * (h + 1))
            half = slice(HEAD_DIM * (h % 2), HEAD_DIM * (h % 2 + 1))
            dyh = dy_ref[:, cs]
            t = jnp.sum(dyh * y_ref[:, cs], -1, keepdims=True)
            for p in range(3):
                wp = w[p][:, h:h + 1]
                dobufs[p][h // 2][:, half] = wp * dyh
                cbufs[p][:, h:h + 1] = wp * t
        for cb in range(_QKV_BLOCKS):
            do1_ref[0, :, _LANES * cb:_LANES * (cb + 1)] = dobufs[0][cb][...].astype(bf16)
        c1_ref[0] = cbufs[0][...]
        for k, (d, do_ref, c_ref) in enumerate(((4, do4_ref, c4_ref), (16, do16_ref, c16_ref))):
            def put_do(r, cb, piece, do_ref=do_ref):
                do_ref[r, :, _LANES * cb:_LANES * (cb + 1)] = piece.astype(bf16)

            def put_c(r, cb, piece, c_ref=c_ref):
                c_ref[r] = piece

            _rows_to_residues(dobufs[1 + k], d, put_do)
            _rows_to_residues([cbufs[1 + k]], d, put_c)

    return pl.pallas_call(
        body, name="att_merge_bwd", grid=(S // ts,),
        in_specs=[pl.BlockSpec((ts, D_ATT), lambda i: (i, 0)), pl.BlockSpec((ts, D_ATT), lambda i: (i, 0)),
                  pl.BlockSpec((3, ts, _LANES), lambda i: (0, i, 0))],
        out_specs=[_res_spec3(d, D_ATT) for d in DILATIONS] + [_res_spec3(d, _LANES) for d in DILATIONS],
        out_shape=[jax.ShapeDtypeStruct((d, S // d, D_ATT), bf16) for d in DILATIONS]
        + [jax.ShapeDtypeStruct((d, S // d, _LANES), f32) for d in DILATIONS],
        scratch_shapes=[pltpu.VMEM((ts, _LANES), f32)] * (3 * _QKV_BLOCKS + 3),
        compiler_params=_cp(("parallel",)),
    )(*_hbm(dy, y, lse3))


_SSM_ROWS = 256


def _scan_in_place(sr_ref, si_ref, a_ref, reverse):
    S, N = sr_ref.shape
    nst = S // SCAN_SEG
    ar = jnp.broadcast_to(a_ref[0:1, :], (SCAN_SEG, N))
    ai = jnp.broadcast_to(a_ref[1:2, :], (SCAN_SEG, N))
    if reverse:
        ai = -ai
    row = lax.broadcasted_iota(jnp.int32, (SCAN_SEG, N), 0)
    zero = jnp.zeros((SCAN_SEG, N), f32)

    def tile(t):
        return pl.ds(pl.multiple_of((nst - 1 - t if reverse else t) * SCAN_SEG, SCAN_SEG), SCAN_SEG)

    def local(t, c):
        sr, si, pr, pi = c
        rows = tile(t)
        nsr = ar * sr - ai * si + sr_ref[rows, :]
        nsi = ar * si + ai * sr + si_ref[rows, :]
        sr_ref[rows, :] = nsr
        si_ref[rows, :] = nsi
        return nsr, nsi, ar * pr - ai * pi, ar * pi + ai * pr

    fr, fi, apr, api = lax.fori_loop(0, nst, local, (zero, zero, zero + 1.0, zero))

    def shift(v):
        if reverse:
            return jnp.where(row == SCAN_SEG - 1, 0.0, pltpu.roll(v, SCAN_SEG - 1, axis=0))
        return jnp.where(row == 0, 0.0, pltpu.roll(v, 1, axis=0))

    cr, ci = zero, zero
    for _ in range(SCAN_SEG - 1):
        cr, ci = shift(fr + apr * cr - api * ci), shift(fi + apr * ci + api * cr)

    def fix(t, c):
        pr, pi = c
        npr, npi = ar * pr - ai * pi, ar * pi + ai * pr
        rows = tile(t)
        sr_ref[rows, :] += npr * cr - npi * ci
        si_ref[rows, :] += npr * ci + npi * cr
        return npr, npi

    lax.fori_loop(0, nst, fix, (zero + 1.0, zero))


def _ssm_states(u, bre, bim, a2, l):
    S = u.shape[0]

    def body(u_ref, br_ref, bi_ref, a2_ref, sr_ref, si_ref):
        a_ref = a2_ref.at[l]
        brb = br_ref[l].astype(bf16)
        bib = bi_ref[l].astype(bf16)

        def project(t, c):
            rows = pl.ds(pl.multiple_of(t * _SSM_ROWS, _SSM_ROWS), _SSM_ROWS)
            ub = u_ref[rows, :].astype(bf16)
            sr_ref[rows, :] = _dot(ub, brb)
            si_ref[rows, :] = _dot(ub, bib)
            return c

        lax.fori_loop(0, S // _SSM_ROWS, project, 0)
        _scan_in_place(sr_ref, si_ref, a_ref, False)

    vm = pl.BlockSpec(memory_space=pltpu.VMEM)
    return pl.pallas_call(
        body, name="ssm_states", in_specs=[vm] * 4, out_specs=[vm, vm],
        out_shape=[jax.ShapeDtypeStruct((S, D_STATE), f32)] * 2,
        compiler_params=_cp(None, 48),
    )(u, bre, bim, a2)


def _ssm_out(sr, si, u, cre, cim, l, dskip, glu_w, glu_b):
    S = u.shape[0]
    ts = TOK_TILE

    def body(sr_ref, si_ref, u_ref, cr_ref, ci_ref, d_ref, w_ref, b_ref, out_ref, y_ref):
        y = (_dot(sr_ref[...].astype(bf16), cr_ref[0].astype(bf16))
             - _dot(si_ref[...].astype(bf16), ci_ref[0].astype(bf16)) + d_ref[...] * u_ref[...])
        y_ref[...] = y
        z = _dot(_gelu(y).astype(bf16), w_ref[...].astype(bf16)) + b_ref[...]
        out_ref[...] = y * jax.nn.sigmoid(z)

    st = pl.BlockSpec((ts, D_STATE), lambda i: (i, 0))
    ch = pl.BlockSpec((ts, D_SSM), lambda i: (i, 0))
    c_l = pl.BlockSpec((1, D_STATE, D_SSM), lambda i: (l, 0, 0))
    return pl.pallas_call(
        body, name="ssm_out", grid=(S // ts,),
        in_specs=[st, st, ch, c_l, c_l, _full((1, D_SSM)), _full((D_SSM, D_SSM)), _full((1, D_SSM))],
        out_specs=[ch, ch],
        out_shape=[jax.ShapeDtypeStruct((S, D_SSM), f32)] * 2,
        compiler_params=_cp(("parallel",)),
    )(*_hbm(sr, si, u, cre, cim, dskip, glu_w, glu_b))


def _ssm_out_bwd(dout, y, u, sr, si, dskip, glu_w, glu_b):
    S = u.shape[0]
    ts = TOK_TILE

    def body(do_ref, y_ref, u_ref, sr_ref, si_ref, d_ref, w_ref, b_ref,
             dy_ref, du_ref, dcr_ref, dci_ref, dd_ref, dgb_ref, dgw_ref):
        @pl.when(pl.program_id(0) == 0)
        def _():
            for r in (dcr_ref, dci_ref, dd_ref, dgb_ref, dgw_ref):
                r[...] = jnp.zeros_like(r)

        y = y_ref[...]
        dout = do_ref[...]
        wb = w_ref[...].astype(bf16)
        ge = _gelu(y).astype(bf16)
        sz = jax.nn.sigmoid(_dot(ge, wb) + b_ref[...])
        dz = dout * y * sz * (1.0 - sz)
        dzb = dz.astype(bf16)
        dgb_ref[...] += jnp.sum(dz, 0, keepdims=True)
        dgw_ref[...] += _dot_tn(ge, dzb)
        dy = dout * sz + _gelu_grad(y) * _dot_nt(dzb, wb)
        uv = u_ref[...]
        dd_ref[...] += jnp.sum(dy * uv, 0, keepdims=True)
        du_ref[...] = dy * d_ref[...]
        dy_ref[...] = dy
        dyb = dy.astype(bf16)
        dcr_ref[...] += _dot_tn(sr_ref[...].astype(bf16), dyb)
        dci_ref[...] -= _dot_tn(si_ref[...].astype(bf16), dyb)

    st = pl.BlockSpec((ts, D_STATE), lambda i: (i, 0))
    ch = pl.BlockSpec((ts, D_SSM), lambda i: (i, 0))
    c_full = _full((D_STATE, D_SSM))
    return pl.pallas_call(
        body, name="ssm_out_bwd", grid=(S // ts,),
        in_specs=[ch, ch, ch, st, st, _full((1, D_SSM)), _full((D_SSM, D_SSM)), _full((1, D_SSM))],
        out_specs=[ch, ch, c_full, c_full, _full((1, D_SSM)), _full((1, D_SSM)), _full((D_SSM, D_SSM))],
        out_shape=[jax.ShapeDtypeStruct((S, D_SSM), f32), jax.ShapeDtypeStruct((S, D_SSM), f32),
                   jax.ShapeDtypeStruct((D_STATE, D_SSM), f32), jax.ShapeDtypeStruct((D_STATE, D_SSM), f32),
                   jax.ShapeDtypeStruct((1, D_SSM), f32), jax.ShapeDtypeStruct((1, D_SSM), f32),
                   jax.ShapeDtypeStruct((D_SSM, D_SSM), f32)],
        compiler_params=_cp(("arbitrary",), 40),
    )(*_hbm(dout, y, u, sr, si, dskip, glu_w, glu_b))


def _ssm_states_bwd(dy, du_skip, u, sr, si, cre, cim, bre, bim, a2, l):
    S = u.shape[0]
    N = D_STATE
    nst = S // SCAN_SEG
    nproj = S // _SSM_ROWS

    def body(dy_ref, dus_ref, u_ref, sr_ref, si_ref, cr_ref, ci_ref, br_ref, bi_ref, a2_ref,
             du_ref, dbr_ref, dbi_ref, da_ref, lr_ref, li_ref):
        a_ref = a2_ref.at[l]
        crb = cr_ref[l].astype(bf16)
        cib = ci_ref[l].astype(bf16)

        def project(t, c):
            rows = pl.ds(pl.multiple_of(t * _SSM_ROWS, _SSM_ROWS), _SSM_ROWS)
            dyb = dy_ref[rows, :].astype(bf16)
            lr_ref[rows, :] = _dot_nt(dyb, crb)
            li_ref[rows, :] = -_dot_nt(dyb, cib)
            return c

        lax.fori_loop(0, nproj, project, 0)
        _scan_in_place(lr_ref, li_ref, a_ref, True)

        row = lax.broadcasted_iota(jnp.int32, (SCAN_SEG, N), 0)
        last = pl.ds((nst - 1) * SCAN_SEG, SCAN_SEG)
        pr = jnp.where(row == 0, 0.0, pltpu.roll(sr_ref[last, :], 1, axis=0))
        pi = jnp.where(row == 0, 0.0, pltpu.roll(si_ref[last, :], 1, axis=0))
        first = pl.ds(0, SCAN_SEG)
        acc_r = lr_ref[first, :] * pr + li_ref[first, :] * pi
        acc_i = li_ref[first, :] * pr - lr_ref[first, :] * pi

        def step(t, c):
            acc_r, acc_i = c
            rows = pl.ds(pl.multiple_of(t * SCAN_SEG, SCAN_SEG), SCAN_SEG)
            prev = pl.ds(pl.multiple_of((t - 1) * SCAN_SEG, SCAN_SEG), SCAN_SEG)
            lrv, liv, srv, siv = lr_ref[rows, :], li_ref[rows, :], sr_ref[prev, :], si_ref[prev, :]
            return acc_r + lrv * srv + liv * siv, acc_i + liv * srv - lrv * siv

        acc_r, acc_i = lax.fori_loop(1, nst, step, (acc_r, acc_i))
        da_ref[0:1, :] = jnp.sum(acc_r, 0, keepdims=True)
        da_ref[1:2, :] = jnp.sum(acc_i, 0, keepdims=True)

        brb = br_ref[l].astype(bf16)
        bib = bi_ref[l].astype(bf16)
        dbr_ref[...] = jnp.zeros_like(dbr_ref)
        dbi_ref[...] = jnp.zeros_like(dbi_ref)

        def back(t, c):
            rows = pl.ds(pl.multiple_of(t * _SSM_ROWS, _SSM_ROWS), _SSM_ROWS)
            lrb = lr_ref[rows, :].astype(bf16)
            lib = li_ref[rows, :].astype(bf16)
            du_ref[rows, :] = dus_ref[rows, :] + _dot_nt(lrb, brb) + _dot_nt(lib, bib)
            ub = u_ref[rows, :].astype(bf16)
            dbr_ref[...] += _dot_tn(ub, lrb)
            dbi_ref[...] += _dot_tn(ub, lib)
            return c

        lax.fori_loop(0, nproj, back, 0)

    vm = pl.BlockSpec(memory_space=pltpu.VMEM)
    return pl.pallas_call(
        body, name="ssm_states_bwd", in_specs=[vm] * 10, out_specs=[vm] * 4,
        out_shape=[jax.ShapeDtypeStruct((S, D_SSM), f32), jax.ShapeDtypeStruct((D_SSM, D_STATE), f32),
                   jax.ShapeDtypeStruct((D_SSM, D_STATE), f32), jax.ShapeDtypeStruct((2, D_STATE), f32)],
        scratch_shapes=[pltpu.VMEM((S, D_STATE), f32), pltpu.VMEM((S, D_STATE), f32)],
        compiler_params=_cp(None, 56),
    )(dy, du_skip, u, sr, si, cre, cim, bre, bim, a2)


_POOL_TILE = 256


def _window_sums(xt, back):
    n = xt.shape[0]
    out = []
    ws = xt
    for k in (1, 2, 4, 8):
        ws = ws + pltpu.roll(ws, k if back else n - k, axis=0)
        out.append(ws)
    return out


def _pool_count(r0, w):
    t = r0 + lax.broadcasted_iota(jnp.int32, (_POOL_TILE, POOL_GROUP), 0)
    return jnp.minimum(t + 1, w).astype(f32)


def _pool_fwd(u_pad, pool_w, pool_scale):
    S = u_pad.shape[0] - POOL_HALO
    nt = S // _POOL_TILE

    def body(u_ref, w_ref, sc_ref, y_ref):
        def tile(t, c):
            r0 = pl.multiple_of(t * _POOL_TILE, _POOL_TILE)
            for g, w in enumerate(POOL_WINDOWS):
                cs = pl.ds(POOL_GROUP * g, POOL_GROUP)
                xt = u_ref[pl.ds(r0, _POOL_TILE + POOL_HALO), cs]
                ws = _window_sums(xt, True)[g][POOL_HALO:, :]
                pooled = ws / _pool_count(r0, w) - xt[POOL_HALO:, :]
                y_ref[pl.ds(r0, _POOL_TILE), cs] = _dot(pooled.astype(bf16), w_ref[g].astype(bf16)) * sc_ref[:, cs]
            return c
        lax.fori_loop(0, nt, tile, 0)

    vm = pl.BlockSpec(memory_space=pltpu.VMEM)
    return pl.pallas_call(
        body, name="pool_fwd", in_specs=[vm, vm, vm], out_specs=vm,
        out_shape=jax.ShapeDtypeStruct((S, D_POOL), f32),
    )(u_pad, pool_w, pool_scale)


def _pool_bwd(dy_pad, u_pad, pool_w, pool_scale):
    S = u_pad.shape[0] - POOL_HALO
    nt = S // _POOL_TILE
    n = _POOL_TILE + POOL_HALO

    def body(dy_ref, u_ref, w_ref, sc_ref, du_ref, dw_ref, dsc_ref):
        dw_ref[...] = jnp.zeros_like(dw_ref)
        dsc_ref[...] = jnp.zeros_like(dsc_ref)

        def tile(t, c):
            r0 = pl.multiple_of(t * _POOL_TILE, _POOL_TILE)
            for g, w in enumerate(POOL_WINDOWS):
                cs = pl.ds(POOL_GROUP * g, POOL_GROUP)
                wb = w_ref[g].astype(bf16)
                xt = u_ref[pl.ds(r0, n), cs]
                pooled = (_window_sums(xt, True)[g][POOL_HALO:, :] / _pool_count(r0, w) - xt[POOL_HALO:, :]).astype(bf16)
                dy = dy_ref[pl.ds(r0, _POOL_TILE), cs]
                dsc_ref[:, cs] += jnp.sum(dy * _dot(pooled, wb), 0, keepdims=True)
                dw_ref[g] += _dot_tn(pooled, (dy * sc_ref[:, cs]).astype(bf16))
                dyh = (dy_ref[pl.ds(r0, n), cs] * sc_ref[:, cs]).astype(bf16)
                dpl = _dot_nt(dyh, wb)
                cnt = jnp.minimum(r0 + lax.broadcasted_iota(jnp.int32, (n, POOL_GROUP), 0) + 1, w).astype(f32)
                lead = _window_sums(dpl / cnt, False)[g]
                du_ref[pl.ds(r0, _POOL_TILE), cs] = lead[:_POOL_TILE, :] - dpl[:_POOL_TILE, :]
            return c
        lax.fori_loop(0, nt, tile, 0)

    vm = pl.BlockSpec(memory_space=pltpu.VMEM)
    return pl.pallas_call(
        body, name="pool_bwd", in_specs=[vm, vm, vm, vm], out_specs=[vm, vm, vm],
        out_shape=[jax.ShapeDtypeStruct((S, D_POOL), f32), jax.ShapeDtypeStruct((4, POOL_GROUP, POOL_GROUP), f32),
                   jax.ShapeDtypeStruct((1, D_POOL), f32)],
    )(dy_pad, u_pad, pool_w, pool_scale)


def _loss_head(y, target):
    S, D = y.shape
    ts = TOK_TILE

    def body(y_ref, t_ref, loss_ref, dy_ref):
        @pl.when(pl.program_id(0) == 0)
        def _():
            loss_ref[...] = jnp.zeros_like(loss_ref)

        d = y_ref[...] - t_ref[...]
        dy_ref[...] = d * (1.0 / D)
        loss_ref[...] += 0.5 * jnp.sum(jnp.sum(d * d, -1, keepdims=True) * (1.0 / D), 0, keepdims=True)

    tok = pl.BlockSpec((ts, D), lambda i: (i, 0))
    return pl.pallas_call(
        body, name="loss_head", grid=(S // ts,),
        in_specs=[tok, tok], out_specs=[_full((1, 1)), tok],
        out_shape=[jax.ShapeDtypeStruct((1, 1), f32), jax.ShapeDtypeStruct((S, D), f32)],
        compiler_params=_cp(("arbitrary",)),
    )(*_hbm(y, target))


_ADA_COLS = 768


def _ada_fwd(c_all, ada_w, ada_b_cols):
    L, D, N = ada_w.shape
    B = c_all.shape[0]

    def body(c_ref, w_ref, b_ref, out_ref):
        cv = c_ref[...]
        cond = (cv * jax.nn.sigmoid(cv)).astype(bf16)
        out_ref[0] = _dot(cond, w_ref[0].astype(bf16)) + b_ref[0]

    return pl.pallas_call(
        body, name="ada_fwd", grid=(L, N // _ADA_COLS),
        in_specs=[_full((B, D)), pl.BlockSpec((1, D, _ADA_COLS), lambda l, j: (l, 0, j)),
                  pl.BlockSpec((1, 1, _ADA_COLS), lambda l, j: (l, 0, j))],
        out_specs=pl.BlockSpec((1, B, _ADA_COLS), lambda l, j: (l, 0, j)),
        out_shape=jax.ShapeDtypeStruct((L, B, N), f32),
        compiler_params=_cp(("parallel", "parallel")),
    )(c_all, ada_w, ada_b_cols)


def _ada_wgrad(c_all_t, dmod_cols):
    D, B = c_all_t.shape
    L, _, N = dmod_cols.shape

    def body(ct_ref, dm_ref, out_ref):
        cv = ct_ref[...]
        cond = cv * jax.nn.sigmoid(cv)
        acc = cond[:, 0:1] * dm_ref[0, 0:1, :]
        for b in range(1, B):
            acc = acc + cond[:, b:b + 1] * dm_ref[0, b:b + 1, :]
        out_ref[0] = acc

    return pl.pallas_call(
        body, name="ada_wgrad", grid=(L, N // _ADA_COLS),
        in_specs=[_full((D, B)), pl.BlockSpec((1, B, _ADA_COLS), lambda l, j: (l, 0, j))],
        out_specs=pl.BlockSpec((1, D, _ADA_COLS), lambda l, j: (l, 0, j)),
        out_shape=jax.ShapeDtypeStruct((L, D, N), f32),
        compiler_params=_cp(("parallel", "parallel")),
    )(c_all_t, dmod_cols)


def _adam_math(w, g, m, v):
    m = ADAM_B1 * m + (1.0 - ADAM_B1) * g
    v = ADAM_B2 * v + (1.0 - ADAM_B2) * (g * g)
    m_hat = m / (1.0 - ADAM_B1 ** ADAM_STEP)
    v_hat = v / (1.0 - ADAM_B2 ** ADAM_STEP)
    delta = -ADAM_LR * (m_hat / (jnp.sqrt(v_hat) + ADAM_EPS) + ADAM_WD * w)
    return delta, m, v


def _adamw(w, m, v, g, row_tile, row0=0, outs=None):
    R, C = w.shape
    b0 = row0 // row_tile

    def body(w_ref, m_ref, v_ref, g_ref, _0, _1, _2, _3, g_out, d_out, m_out, v_out):
        gv = g_ref[...]
        delta, mn, vn = _adam_math(w_ref[...], gv, m_ref[...], v_ref[...])
        g_out[...] = gv
        d_out[...] = delta
        m_out[...] = mn
        v_out[...] = vn

    pspec = pl.BlockSpec((row_tile, C), lambda i: (b0 + i, 0))
    gspec = pl.BlockSpec((row_tile, C), lambda i: (i, 0))
    anyspec = pl.BlockSpec(memory_space=pl.ANY)
    shp = jax.ShapeDtypeStruct((R, C), f32)
    if outs is None:
        outs = [lax.empty((R, C), f32) for _ in range(4)]
    return pl.pallas_call(
        body, name="adamw", grid=(g.shape[0] // row_tile,),
        in_specs=[pspec] * 3 + [gspec] + [anyspec] * 4, out_specs=[pspec] * 4, out_shape=[shp] * 4,
        input_output_aliases={4: 0, 5: 1, 6: 2, 7: 3},
        compiler_params=_cp(("parallel",), 40),
    )(*_hbm(w, m, v, g, *outs))


def _pair_sum(g5s, gots, pc):
    n = len(g5s)

    def body(pc_ref, *refs):
        for own, got, out in zip(refs[:n], refs[n:2 * n], refs[2 * n:]):
            out[0, 0] = (own[0, 0, 0].astype(f32) + got[0, 0].astype(f32)).astype(bf16)

    def half(g):
        return pl.BlockSpec((1, 1) + g.shape[-2:], lambda p, pc: (p, 0, 0, 0))

    gs = pltpu.PrefetchScalarGridSpec(
        num_scalar_prefetch=1, grid=(N_CHIPS,),
        in_specs=[pl.BlockSpec((1, 1, 1) + g.shape[-2:], lambda p, pc: (p, 0, pc[1], 0, 0)) for g in g5s]
        + [half(g) for g in gots],
        out_specs=[half(g) for g in gots],
    )
    return pl.pallas_call(
        body, name="pair_sum", grid_spec=gs, out_shape=[jax.ShapeDtypeStruct(g.shape, bf16) for g in gots],
        compiler_params=_cp(("parallel",), 48),
    )(pc, *_hbm(*g5s, *gots))


_SUM_STEPS = 2


def _sum_shards(hsums, recvs, pc):
    n = len(hsums)

    def body(pc_ref, *refs):
        for own, got, out in zip(refs[:n], refs[n:2 * n], refs[2 * n:]):
            acc = own[0, 0].astype(f32)
            for j in range(3):
                acc = acc + got[j, 0].astype(f32)
            out[0, 0] = acc

    def rows(h):
        return (h.shape[2] // _SUM_STEPS, h.shape[3])

    gs = pltpu.PrefetchScalarGridSpec(
        num_scalar_prefetch=1, grid=(_SUM_STEPS,),
        in_specs=[pl.BlockSpec((1, 1) + rows(h), lambda i, pc: (pc[0], 0, i, 0)) for h in hsums]
        + [pl.BlockSpec((3, 1) + rows(h), lambda i, pc: (0, 0, i, 0)) for h in hsums],
        out_specs=[pl.BlockSpec((1, 1) + rows(h), lambda i, pc: (0, pc[1], i, 0)) for h in hsums],
    )
    return pl.pallas_call(
        body, name="sum_shards", grid_spec=gs,
        out_shape=[jax.ShapeDtypeStruct((1, 2) + h.shape[2:], f32) for h in hsums],
        compiler_params=_cp(("parallel",), 48),
    )(pc, *_hbm(*hsums, *recvs))


def _sum8(packs):
    _, R, C = packs.shape
    tr = R // 8 if R % 64 == 0 else R

    def body(p_ref, out_ref):
        acc = p_ref[0]
        for d in range(1, 8):
            acc = acc + p_ref[d]
        out_ref[...] = acc

    return pl.pallas_call(
        body, name="sum8", grid=(R // tr,),
        in_specs=[pl.BlockSpec((8, tr, C), lambda i: (0, i, 0))],
        out_specs=pl.BlockSpec((tr, C), lambda i: (i, 0)),
        out_shape=jax.ShapeDtypeStruct((R, C), f32),
        compiler_params=_cp(("parallel",)),
    )(packs)


def _allgather8(x_shard):
    m_per, n = x_shard.shape

    def body(x_ref, out_ref, send_sems, recv_sems, local_sem):
        x, y, c = lax.axis_index("x"), lax.axis_index("y"), lax.axis_index("c")
        me, sibling = (x, y, c), (x, y, 1 - c)
        chips = [(1 - x, y), (x, 1 - y), (1 - x, 1 - y)]

        def rows(px, py, pc):
            return out_ref.at[pl.ds((4 * px + 2 * py + pc) * m_per, m_per), :]

        def copy(k, block, to, src=None):
            return pltpu.make_async_remote_copy(
                src_ref=rows(*block) if src is None else src, dst_ref=rows(*block),
                send_sem=send_sems.at[k], recv_sem=recv_sems.at[k], device_id=to, device_id_type=MESH)

        mine = pltpu.make_async_copy(x_ref, rows(*me), local_sem)
        mine.start()
        first = [copy(0, me, sibling, src=x_ref)]
        first += [copy(1 + j, me, (*chip, c), src=x_ref) for j, chip in enumerate(chips)]
        for cp in first:
            cp.start()
        passed = [copy(4 + j, (*chip, c), sibling) for j, chip in enumerate(chips)]
        for j, chip in enumerate(chips):
            copy(1 + j, (*chip, c), me).wait_recv()
            passed[j].start()
        copy(0, sibling, me).wait_recv()
        for j, chip in enumerate(chips):
            copy(4 + j, (*chip, 1 - c), me).wait_recv()
        for cp in first + passed:
            cp.wait_send()
        mine.wait()

    return pl.pallas_call(
        body, name="allgather8",
        out_shape=jax.ShapeDtypeStruct((8 * m_per, n), x_shard.dtype),
        in_specs=[pl.BlockSpec(memory_space=pltpu.VMEM)],
        out_specs=pl.BlockSpec(memory_space=pltpu.VMEM),
        scratch_shapes=[pltpu.SemaphoreType.DMA((7,)), pltpu.SemaphoreType.DMA((7,)), pltpu.SemaphoreType.DMA],
        compiler_params=_cp(None, 48),
    )(x_shard)


def _other_chips():
    x, y = lax.axis_index("x"), lax.axis_index("y")
    return [(1 - x, y), (x, 1 - y), (1 - x, 1 - y)]


_HBM = pl.BlockSpec(memory_space=pltpu.HBM)
_SEM = pl.BlockSpec(memory_space=pltpu.SEMAPHORE)
_EFFECT = pltpu.SideEffectType.DATAFLOW_SIDE_EFFECTING


def _gather_copies(srcs, lands, send_sems, recv_sems):
    x, y, c = lax.axis_index("x"), lax.axis_index("y"), lax.axis_index("c")
    return [pltpu.make_async_remote_copy(
        src_ref=srcs[a].at[:, c], dst_ref=lands[a].at[2 * x + y, :, c], send_sem=send_sems.at[3 * a + j],
        recv_sem=recv_sems.at[3 * a + j], device_id=(cx, cy, c), device_id_type=MESH)
        for a in range(len(srcs)) for j, (cx, cy) in enumerate(_other_chips())]


def _gather_start(chunks, after, name):
    sizes = [len(srcs) for srcs, _ in chunks]
    flat = [t for srcs, lands in chunks for t in list(srcs) + list(lands)]
    nflat = len(flat)
    nsem = 2 * len(chunks)

    def body(*refs):
        ins, sems, token = refs[:nflat], refs[nflat + 1:nflat + 1 + nsem], refs[-1]
        off = 0
        for k, n in enumerate(sizes):
            for cp in _gather_copies(ins[off:off + n], ins[off + n:off + 2 * n], sems[2 * k], sems[2 * k + 1]):
                cp.start()
            off += 2 * n
        token[...] = jnp.zeros_like(token)

    res = pl.pallas_call(
        body, name=name,
        out_shape=[pltpu.SemaphoreType.DMA((3 * n,)) for n in sizes for _ in range(2)]
        + [pltpu.HBM(t.shape, t.dtype) for t in flat] + [jax.ShapeDtypeStruct((8, 128), f32)],
        in_specs=[_HBM] * nflat + [pl.BlockSpec(memory_space=pl.ANY)],
        out_specs=[_SEM] * nsem + [_HBM] * nflat + [pl.BlockSpec(memory_space=pltpu.VMEM)],
        input_output_aliases={i: nsem + i for i in range(nflat)},
        compiler_params=pltpu.CompilerParams(has_side_effects=_EFFECT),
    )(*[pltpu.with_memory_space_constraint(t, pltpu.HBM) for t in flat], after)
    out, off = [], nsem
    for k, n in enumerate(sizes):
        out.append((res[2 * k], res[2 * k + 1], res[off:off + n], res[off + n:off + 2 * n]))
        off += 2 * n
    return out, res[-1]


def _gather_wait(send_sems, recv_sems, srcs, lands, after, name):
    n = len(srcs)

    def body(*refs):
        for cp in _gather_copies(refs[:n], refs[n:2 * n], refs[2 * n], refs[2 * n + 1]):
            cp.wait_send()
            cp.wait_recv()

    res = pl.pallas_call(
        body, name=name,
        out_shape=[pltpu.HBM(t.shape, t.dtype) for t in list(srcs) + list(lands)],
        in_specs=[_HBM] * (2 * n) + [_SEM, _SEM] + [pl.BlockSpec(memory_space=pl.ANY)] * len(after),
        out_specs=[_HBM] * (2 * n),
        input_output_aliases={i: i for i in range(2 * n)},
        compiler_params=pltpu.CompilerParams(has_side_effects=_EFFECT),
    )(*srcs, *lands, send_sems, recv_sems, *after)
    return res[n:]


def _split_start(make_copies, arrays, nsem, name, after=()):
    n, na = len(arrays), len(after)

    def body(*refs):
        for cp in make_copies(refs[:n], refs[n + na], refs[n + na + 1]):
            cp.start()
        refs[-1][...] = jnp.zeros_like(refs[-1])

    res = pl.pallas_call(
        body, name=name,
        out_shape=[pltpu.SemaphoreType.DMA((nsem,)), pltpu.SemaphoreType.DMA((nsem,))]
        + [pltpu.HBM(t.shape, t.dtype) for t in arrays] + [jax.ShapeDtypeStruct((8, 128), f32)],
        in_specs=[_HBM] * n + [pl.BlockSpec(memory_space=pl.ANY)] * na,
        out_specs=[_SEM, _SEM] + [_HBM] * n + [pl.BlockSpec(memory_space=pltpu.VMEM)],
        input_output_aliases={i: i + 2 for i in range(n)},
        compiler_params=pltpu.CompilerParams(has_side_effects=_EFFECT),
    )(*[pltpu.with_memory_space_constraint(t, pltpu.HBM) for t in arrays], *after)
    return (res[0], res[1], res[2:2 + n]), res[-1]


def _split_wait(make_copies, send_sems, recv_sems, arrays, after, name):
    n = len(arrays)

    def body(*refs):
        for cp in make_copies(refs[:n], refs[n], refs[n + 1]):
            cp.wait_send()
            cp.wait_recv()

    return pl.pallas_call(
        body, name=name,
        out_shape=[pltpu.HBM(t.shape, t.dtype) for t in arrays],
        in_specs=[_HBM] * n + [_SEM, _SEM] + [pl.BlockSpec(memory_space=pl.ANY)] * len(after),
        out_specs=[_HBM] * n, input_output_aliases={i: i for i in range(n)},
        compiler_params=pltpu.CompilerParams(has_side_effects=_EFFECT),
    )(*arrays, send_sems, recv_sems, *after)


def _sibling():
    return lax.axis_index("x"), lax.axis_index("y"), 1 - lax.axis_index("c")


def _forward_copies(lands, send_sems, recv_sems):
    c = lax.axis_index("c")
    return [pltpu.make_async_remote_copy(
        src_ref=lands[a].at[2 * cx + cy, :, c], dst_ref=lands[a].at[2 * cx + cy, :, c], send_sem=send_sems.at[3 * a + j],
        recv_sem=recv_sems.at[3 * a + j], device_id=_sibling(), device_id_type=MESH)
        for a in range(len(lands)) for j, (cx, cy) in enumerate(_other_chips())]


def _swap_copies(fulls, send_sems, recv_sems):
    c = lax.axis_index("c")
    return [pltpu.make_async_remote_copy(src_ref=t.at[:, c], dst_ref=t.at[:, c], send_sem=send_sems.at[a],
                                         recv_sem=recv_sems.at[a], device_id=_sibling(), device_id_type=MESH)
            for a, t in enumerate(fulls)]


def _pair_copies(refs, send_sems, recv_sems):
    n = len(refs) // 2
    c = lax.axis_index("c")
    return [pltpu.make_async_remote_copy(src_ref=refs[a].at[:, :, 1 - c], dst_ref=refs[n + a], send_sem=send_sems.at[a],
                                         recv_sem=recv_sems.at[a], device_id=_sibling(), device_id_type=MESH)
            for a in range(n)]


def _scatter_copies(srcs, lands, send_sems, recv_sems):
    c = lax.axis_index("c")
    return [pltpu.make_async_remote_copy(
        src_ref=srcs[a].at[2 * cx + cy], dst_ref=lands[a].at[j], send_sem=send_sems.at[3 * a + j],
        recv_sem=recv_sems.at[3 * a + j], device_id=(cx, cy, c), device_id_type=MESH)
        for a in range(len(srcs)) for j, (cx, cy) in enumerate(_other_chips())]


def _scatter_start(hsums, name, after=()):
    n = len(hsums)
    na = len(after)

    def body(*refs):
        srcs, lands = refs[:n], refs[n:2 * n]
        send_sems, recv_sems = refs[2 * n + na], refs[2 * n + na + 1]
        for cp in _scatter_copies(srcs, lands, send_sems, recv_sems):
            cp.start()
        refs[-1][...] = jnp.zeros_like(refs[-1])

    lands = [lax.empty((3,) + g.shape[1:], g.dtype) for g in hsums]
    res = pl.pallas_call(
        body, name=name,
        out_shape=[pltpu.SemaphoreType.DMA((3 * n,)), pltpu.SemaphoreType.DMA((3 * n,))]
        + [pltpu.HBM(g.shape, g.dtype) for g in hsums] + [pltpu.HBM(g.shape, g.dtype) for g in lands]
        + [jax.ShapeDtypeStruct((8, 128), f32)],
        in_specs=[_HBM] * (2 * n) + [pl.BlockSpec(memory_space=pl.ANY)] * na,
        out_specs=[_SEM, _SEM] + [_HBM] * (2 * n) + [pl.BlockSpec(memory_space=pltpu.VMEM)],
        input_output_aliases={i: i + 2 for i in range(2 * n)},
        compiler_params=pltpu.CompilerParams(has_side_effects=_EFFECT),
    )(*[pltpu.with_memory_space_constraint(t, pltpu.HBM) for t in list(hsums) + lands], *after)
    return (res[0], res[1], res[2:2 + n], res[2 + n:2 + 2 * n]), res[-1]


def _scatter_wait(send_sems, recv_sems, srcs, lands, after, name):
    n = len(srcs)
    extra = list(after)

    def body(*refs):
        s_refs, l_refs = refs[:n], refs[n:2 * n]
        ss, rs = refs[2 * n], refs[2 * n + 1]
        for cp in _scatter_copies(s_refs, l_refs, ss, rs):
            cp.wait_send()
            cp.wait_recv()

    res = pl.pallas_call(
        body, name=name,
        out_shape=[pltpu.HBM(g.shape, g.dtype) for g in srcs] + [pltpu.HBM(g.shape, g.dtype) for g in lands],
        in_specs=[_HBM] * (2 * n) + [_SEM, _SEM] + [pl.BlockSpec(memory_space=pl.ANY)] * len(extra),
        out_specs=[_HBM] * (2 * n),
        input_output_aliases={i: i for i in range(2 * n)},
        compiler_params=pltpu.CompilerParams(has_side_effects=_EFFECT),
    )(*srcs, *lands, send_sems, recv_sems, *extra)
    return res[:n], res[n:]


def _plane_copies(src, land, send_sems, recv_sems):
    x, y, c = lax.axis_index("x"), lax.axis_index("y"), lax.axis_index("c")
    return [pltpu.make_async_remote_copy(src_ref=src, dst_ref=land.at[2 * x + y, c], send_sem=send_sems.at[j],
                                         recv_sem=recv_sems.at[j], device_id=(cx, cy, c), device_id_type=MESH)
            for j, (cx, cy) in enumerate(_other_chips())]


def _plane_start(pack, land, name):
    def body(src, lnd, send_sems, recv_sems, _s, _l, token):
        for cp in _plane_copies(src, lnd, send_sems, recv_sems):
            cp.start()
        token[...] = jnp.zeros_like(token)

    res = pl.pallas_call(
        body, name=name,
        out_shape=[pltpu.SemaphoreType.DMA((3,)), pltpu.SemaphoreType.DMA((3,)), pltpu.HBM(pack.shape, pack.dtype),
                   pltpu.HBM(land.shape, land.dtype), jax.ShapeDtypeStruct((8, 128), f32)],
        in_specs=[_HBM, _HBM], out_specs=[_SEM, _SEM, _HBM, _HBM, pl.BlockSpec(memory_space=pltpu.VMEM)],
        input_output_aliases={0: 2, 1: 3},
        compiler_params=pltpu.CompilerParams(has_side_effects=_EFFECT),
    )(pltpu.with_memory_space_constraint(pack, pltpu.HBM), pltpu.with_memory_space_constraint(land, pltpu.HBM))
    return res[:4], res[4]


def _plane_wait(send_sems, recv_sems, pack, land, after, name):
    def body(src, lnd, ss, rs, *_):
        for cp in _plane_copies(src, lnd, ss, rs):
            cp.wait_send()
            cp.wait_recv()

    return pl.pallas_call(
        body, name=name,
        out_shape=[pltpu.HBM(pack.shape, pack.dtype), pltpu.HBM(land.shape, land.dtype)],
        in_specs=[_HBM, _HBM, _SEM, _SEM] + [pl.BlockSpec(memory_space=pl.ANY)] * len(after),
        out_specs=[_HBM, _HBM], input_output_aliases={0: 0, 1: 1},
        compiler_params=pltpu.CompilerParams(has_side_effects=_EFFECT),
    )(pack, land, send_sems, recv_sems, *after)[1]


def _swap_halves(fulls):
    n = len(fulls)

    def body(*refs):
        ins, outs = refs[:n], refs[n:2 * n]
        send_sems, recv_sems = refs[2 * n:]
        c = lax.axis_index("c")
        sibling = (lax.axis_index("x"), lax.axis_index("y"), 1 - c)
        copies = []
        for a in range(n):
            cp = pltpu.make_async_remote_copy(src_ref=outs[a].at[:, c], dst_ref=outs[a].at[:, c], send_sem=send_sems.at[a],
                                              recv_sem=recv_sems.at[a], device_id=sibling, device_id_type=MESH)
            cp.start()
            copies.append(cp)
        for a, cp in enumerate(copies):
            cp.wait_send()
            theirs = outs[a].at[:, 1 - c]
            pltpu.make_async_remote_copy(src_ref=theirs, dst_ref=theirs, send_sem=send_sems.at[a], recv_sem=recv_sems.at[a],
                                         device_id=sibling, device_id_type=MESH).wait_recv()

    hbm = pl.BlockSpec(memory_space=pl.ANY)
    return pl.pallas_call(
        body, name="swap_halves",
        out_shape=[jax.ShapeDtypeStruct(p.shape, p.dtype) for p in fulls],
        in_specs=[hbm] * n, out_specs=[hbm] * n,
        input_output_aliases={a: a for a in range(n)},
        scratch_shapes=[pltpu.SemaphoreType.DMA((n,)), pltpu.SemaphoreType.DMA((n,))],
    )(*fulls)


def _to_segments(t):
    s, c = t.shape
    return t.reshape(SCAN_SEG, s // SCAN_SEG, c).transpose(1, 0, 2).reshape(s, c)


def _from_segments(t):
    s, c = t.shape
    return t.reshape(s // SCAN_SEG, SCAN_SEG, c).transpose(1, 0, 2).reshape(s, c)


def _ssm_operators(a_re, a_im, log_dt, b_re, b_im, c_re, c_im):
    lam = lax.complex(a_re, a_im)
    dt = jnp.exp(log_dt)[:, None]
    a_bar = jnp.exp(lam * dt)
    b_bar = ((a_bar - 1.0) / lam)[:, :, None] * lax.complex(b_re, b_im)
    eye = jnp.eye(N_GROUPS, dtype=f32)

    def embed_b(t):
        return (jnp.transpose(t, (0, 2, 1))[:, :, None, :] * eye[:, None, :, None]).reshape(D_SSM, D_STATE)

    def embed_c(t):
        return (jnp.transpose(t, (0, 2, 1))[:, :, None, :] * eye[:, None, :, None]).reshape(D_STATE, D_SSM)

    a2 = jnp.stack([a_bar.real.reshape(D_STATE), a_bar.imag.reshape(D_STATE)])
    return a2, embed_b(b_bar.real), embed_b(b_bar.imag), embed_c(c_re), embed_c(c_im)


def _local_step(x, target, mod, small, ffn_weights, mix_weights, grads_done, ffn_bwd_issued):
    table = jnp.asarray(_bucket_table())
    bias = small["att_bias"]
    L = DEPTH
    saved = []
    ssm_names = ("ssm_a_re", "ssm_a_im", "ssm_log_dt", "ssm_b_re", "ssm_b_im", "ssm_c_re", "ssm_c_im")
    ssm_ops_vjp = []
    for l in range(L):
        sv = {}
        m9 = mod[l]
        sv["x0"] = x
        sv["w0"] = ffn_weights(l, 0, x)
        x, sv["f0"], sv["g0"], sv["u0"], sv["h0"] = _ffn_fwd(x, m9[0:3], *sv["w0"], 0, small["ln_g"][l, 0:1], small["ln_b"][l, 0:1])
        sv["x1"] = x
        sv["w1"] = mix_weights(l, x)
        *qkv, z_rest, sv["h1"] = _mix_in_fwd(x, m9[3:6], sv["w1"][0], 0)
        S = x.shape[0]
        qkv = [t.reshape(3, S, D_ATT) for t in qkv]
        att = [_att_fwd(qkv[b], bias, b) for b in range(3)]
        y_att, lse3 = _att_merge([att[b][0].reshape(d, S // d, D_ATT) for b, d in enumerate(DILATIONS)],
                                 [att[b][1].reshape(d, S // d, _LANES) for b, d in enumerate(DILATIONS)])
        sv.update(qkv=qkv, lse=[a[1] for a in att], lse3=lse3, y_att=y_att)

        ops, ops_vjp = jax.vjp(_ssm_operators, *[small[k][l] for k in ssm_names])
        ssm_ops_vjp.append(ops_vjp)
        a2, bre, bim, cre, cim = [t[None] for t in ops]
        u_ssm = _to_segments(z_rest[:, :D_SSM])
        sr, si = _ssm_states(u_ssm, bre, bim, a2, 0)
        dskip = small["ssm_d"][l][None, :]
        glu_b = small["glu_b"][l][None, :]
        out_seg, y_seg = _ssm_out(sr, si, u_ssm, cre, cim, 0, dskip, small["glu_w"][l], glu_b)
        y_ssm = _from_segments(out_seg)
        sv.update(ssm_ops=(a2, bre, bim, cre, cim), u_ssm=u_ssm, sr=sr, si=si, y_seg=y_seg, y_ssm=y_ssm)

        u_pool = jnp.concatenate([jnp.zeros((POOL_HALO, D_POOL), f32), z_rest[:, D_SSM:]])
        y_pool = _pool_fwd(u_pool, small["pool_w"][l], small["pool_scale"][l][None, :])
        sv.update(u_pool=u_pool, y_pool=y_pool)

        x, sv["ymix"] = _mix_out_fwd(x, y_att, y_ssm, y_pool, m9[3:6], sv["w1"][1], 0, small["ln_g"][l, 1:2], small["ln_b"][l, 1:2])
        sv["x2"] = x
        sv["w2"] = ffn_weights(l, 1, x)
        x, sv["f2"], sv["g2"], sv["u2"], sv["h2"] = _ffn_fwd(x, m9[6:9], *sv["w2"], 0, small["ln_g"][l, 2:3], small["ln_b"][l, 2:3])
        saved.append(sv)

    loss, dx = _loss_head(x, target)

    dmod = [None] * L
    dln_g = [None] * L
    dln_b = [None] * L
    sg = {k: [None] * L for k in ssm_names + ("ssm_d", "glu_w", "glu_b", "pool_w", "pool_scale")}
    dbias_tot = None
    order_after = jnp.zeros((), f32)
    for l in reversed(range(L)):
        sv = saved[l]
        m9 = mod[l] + order_after

        def fresh(like):
            return [lax.empty(t.shape, bf16) for t in like]

        dx, dg, du, a, df, dm2, dlg2, dlb2 = _ffn_bwd(dx, sv["x2"], sv["f2"], sv["g2"], sv["u2"], m9[6:9], *sv["w2"], 0,
                                                     small["ln_g"][l, 2:3])
        m9 = m9 + ffn_bwd_issued(l, 1, dx)
        g_ffn1 = _ffn_wgrad(sv["h2"], dg, du, a, df, *fresh(sv["w2"]), 0)
        dxr, d_att, d_ssm, d_pool, dgate1, dlg1, dlb1, g_w_out = _mix_out_bwd(
            dx, sv["x1"], sv["ymix"], sv["y_att"], sv["y_ssm"], sv["y_pool"], m9[3:6], sv["w1"][1], 0, small["ln_g"][l, 1:2],
            fresh(sv["w1"])[1])
        S = d_att.shape[0]
        merged = _att_merge_bwd(d_att, sv["y_att"], sv["lse3"])
        dqkv, dbias = [], []
        for b, d in enumerate(DILATIONS):
            dq_b, db_b = _att_bwd(sv["qkv"][b], merged[b].reshape(S, D_ATT), sv["lse"][b], merged[3 + b].reshape(S, _LANES), bias, b)
            dqkv.append(dq_b.reshape(3, d, S // d, D_ATT))
            dbias.append(db_b)
        dbias = jnp.stack(dbias)
        dbias_tot = dbias if dbias_tot is None else dbias_tot + dbias
        d_seg = _to_segments(d_ssm)
        dskip = small["ssm_d"][l][None, :]
        glu_b = small["glu_b"][l][None, :]
        dy_seg, du_skip, dcre, dcim, dd, dglu_b, dglu_w = _ssm_out_bwd(
            d_seg, sv["y_seg"], sv["u_ssm"], sv["sr"], sv["si"], dskip, small["glu_w"][l], glu_b)
        a2, bre, bim, cre, cim = sv["ssm_ops"]
        du_seg, dbre, dbim, da2 = _ssm_states_bwd(dy_seg, du_skip, sv["u_ssm"], sv["sr"], sv["si"], cre, cim, bre, bim, a2, 0)
        for k, t in zip(ssm_names, ssm_ops_vjp[l]((da2, dbre, dbim, dcre, dcim))):
            sg[k][l] = t
        sg["ssm_d"][l] = dd[0]
        sg["glu_b"][l] = dglu_b[0]
        sg["glu_w"][l] = dglu_w
        du_ssm = _from_segments(du_seg)
        dyp = jnp.concatenate([d_pool, jnp.zeros((POOL_HALO, D_POOL), f32)])
        du_pool, dpw, dps = _pool_bwd(dyp, sv["u_pool"], small["pool_w"][l], small["pool_scale"][l][None, :])
        sg["pool_w"][l] = dpw
        sg["pool_scale"][l] = dps[0]
        d_rest = jnp.concatenate([du_ssm, du_pool], axis=1).astype(bf16)
        dx, dm1, dz = _mix_in_bwd(dqkv, d_rest, dxr, sv["x1"], m9[3:6], sv["w1"][0], 0)
        g_w_in = _mix_in_wgrad(sv["h1"], dz, fresh(sv["w1"])[0], 0)
        ffn_names = ("ffn_w_gate", "ffn_w_up", "ffn_w_down")
        m9 = m9 + grads_done(l, 1, list(zip(ffn_names, [(2 * l + 1) * FF_SHARD] * 3, g_ffn1))
                             + [("w_in", l * D_MODEL, g_w_in), ("w_out", l * 256, g_w_out)])
        dm1 = jnp.concatenate([dm1[0:2], dgate1])
        dx, dg, du, a, df, dm0, dlg0, dlb0 = _ffn_bwd(dx, sv["x0"], sv["f0"], sv["g0"], sv["u0"], m9[0:3], *sv["w0"], 0,
                                                     small["ln_g"][l, 0:1])
        issued = ffn_bwd_issued(l, 0, dx)
        g_ffn0 = _ffn_wgrad(sv["h0"], dg, du, a, df, *fresh(sv["w0"]), 0)
        order_after = grads_done(l, 0, list(zip(ffn_names, [2 * l * FF_SHARD] * 3, g_ffn0))) + issued
        dmod[l] = jnp.concatenate([dm0 + issued, dm1, dm2])
        dln_g[l] = jnp.concatenate([dlg0, dlg1, dlg2])
        dln_b[l] = jnp.concatenate([dlb0, dlb1, dlb2])

    small_grads = {k: jnp.stack(v) for k, v in sg.items()}
    small_grads["rel_bias"] = _bias_bwd(dbias_tot, table)
    small_grads["ln_g"] = jnp.stack(dln_g)
    small_grads["ln_b"] = jnp.stack(dln_b)
    return loss, dx, jnp.stack(dmod), small_grads


_TILE_ELEMS = 8 * 128


def _pack_rows(shapes):
    out, row = [], 0
    for s in shapes:
        nr = -(-int(np.prod(s)) // _TILE_ELEMS) * 8
        out.append((row, nr))
        row += nr
    return out


def _pack(arrs):
    parts = []
    for a in arrs:
        flat = a.reshape(-1).astype(f32)
        npad = -(-flat.shape[0] // _TILE_ELEMS) * _TILE_ELEMS
        parts.append(jnp.pad(flat, (0, npad - flat.shape[0])).reshape(npad // 128, 128))
    return jnp.concatenate(parts, axis=0)


def _unpack(buf, shapes):
    return [buf[row:row + nr].reshape(-1)[:int(np.prod(s))].reshape(s) for s, (row, nr) in zip(shapes, _pack_rows(shapes))]


_REPL = ("rel_bias", "ada_b", "ssm_a_re", "ssm_a_im", "ssm_log_dt", "ssm_b_re", "ssm_b_im", "ssm_c_re", "ssm_c_im",
         "ssm_d", "glu_b", "pool_w", "pool_scale")
_SMALL_SHARDED = ("ln_g", "ln_b", "glu_w")
_BIG = ("ffn_w_gate", "ffn_w_up", "ffn_w_down", "w_in", "w_out")
_ORDER = ("rel_bias", "ada_w", "ada_b", "ln_g", "ln_b", "ffn_w_gate", "ffn_w_up", "ffn_w_down", "w_in", "w_out",
          "ssm_a_re", "ssm_a_im", "ssm_log_dt", "ssm_b_re", "ssm_b_im", "ssm_c_re", "ssm_c_im", "ssm_d", "glu_w",
          "glu_b", "pool_w", "pool_scale")


def kernel(x, c, rel_bias, ada_w, ada_b, ln_g, ln_b, ffn_w_gate, ffn_w_up, ffn_w_down, w_in, w_out, ssm_a_re, ssm_a_im, ssm_log_dt, ssm_b_re, ssm_b_im, ssm_c_re, ssm_c_im, ssm_d, glu_w, glu_b, pool_w, pool_scale, loss_target, m_rel_bias, m_ada_w, m_ada_b, m_ln_g, m_ln_b, m_ffn_w_gate, m_ffn_w_up, m_ffn_w_down, m_w_in, m_w_out, m_ssm_a_re, m_ssm_a_im, m_ssm_log_dt, m_ssm_b_re, m_ssm_b_im, m_ssm_c_re, m_ssm_c_im, m_ssm_d, m_glu_w, m_glu_b, m_pool_w, m_pool_scale, v_rel_bias, v_ada_w, v_ada_b, v_ln_g, v_ln_b, v_ffn_w_gate, v_ffn_w_up, v_ffn_w_down, v_w_in, v_w_out, v_ssm_a_re, v_ssm_a_im, v_ssm_log_dt, v_ssm_b_re, v_ssm_b_im, v_ssm_c_re, v_ssm_c_im, v_ssm_d, v_glu_w, v_glu_b, v_pool_w, v_pool_scale):
    args = dict(locals())
    w = {k: args[k] for k in _ORDER}
    m = {k: args["m_" + k] for k in _ORDER}
    v = {k: args["v_" + k] for k in _ORDER}
    L, D = DEPTH, D_MODEL
    ax, ay, ac = lax.axis_index("x"), lax.axis_index("y"), lax.axis_index("c")
    p_me = 2 * ax + ay
    dev = 4 * ax + 2 * ay + ac

    transposed = ("ffn_w_gate", "ffn_w_up")
    for d in (w, m, v):
        for name in transposed:
            d[name] = jnp.swapaxes(d[name], 2, 3)

    def halves(t):
        return t.astype(bf16).reshape(1, 2, t.shape[0] // 2, t.shape[1])

    def landing(src):
        return lax.dynamic_update_slice(lax.empty((N_CHIPS,) + src.shape, bf16), src[None], (p_me, 0, 0, 0, 0))

    chunk_keys = [("ffn", 0, 0), ("mix", 0), ("ffn", 0, 1), ("ffn", 1, 0), ("mix", 1), ("ffn", 1, 1)]
    chunk_srcs = []
    for key in chunk_keys:
        if key[0] == "ffn":
            chunk_srcs.append([halves(w[name][key[1], key[2]]) for name in ("ffn_w_gate", "ffn_w_up", "ffn_w_down")])
        else:
            chunk_srcs.append([halves(w_in[key[1]]), halves(w_out[key[1]])])

    pack = _pack([c, ln_g, ln_b, glu_w])
    rows = pack.shape[0]
    allp = _allgather8(pack).reshape(8, rows, 128)
    chunks = [(srcs, [landing(t) for t in srcs]) for srcs in chunk_srcs]
    first_in_flight, first_begun = _gather_start(chunks[:1], allp, "gather_start_first")
    c_all = allp[:, :8].reshape(8, D) + first_begun[0, 0]
    by_chip = allp[0::2]

    fwd_rows = _pack_rows([c.shape, ln_g.shape, ln_b.shape, glu_w.shape])

    def sharded(part, shape, axis):
        row0, nrows = fwd_rows[part]
        t = by_chip[:, row0:row0 + nrows].reshape(N_CHIPS, -1)[:, :int(np.prod(shape))].reshape((N_CHIPS,) + shape)
        return jnp.concatenate([t[p] for p in range(N_CHIPS)], axis=axis)

    ln_g_full = sharded(1, ln_g.shape, 2)
    ln_b_full = sharded(2, ln_b.shape, 2)
    glu_w_full = sharded(3, glu_w.shape, 1)

    ncol = ada_w.shape[-1]
    ada_b_cols = lax.dynamic_slice_in_dim(ada_b, p_me * ncol, ncol, axis=1)[:, None, :]
    mod_part = _ada_fwd(c_all, ada_w, ada_b_cols)
    mrows = L * 8 * ncol // 128
    mod_pack = mod_part.reshape(mrows, 128)
    mod_land = lax.dynamic_update_slice(lax.empty((N_CHIPS, 2, mrows, 128), f32), mod_pack[None, None], (p_me, ac, 0, 0))
    mod_in_flight, _ = _plane_start(mod_pack, mod_land, "mod_start")
    att_bias = _bias_fwd(rel_bias, jnp.asarray(_bucket_table()))
    small_names = _REPL + _SMALL_SHARDED
    small_packs = [_pack([d[k] for k in small_names]) for d in (w, m, v)]
    mod_land = _plane_wait(*mod_in_flight, [att_bias] + small_packs + [t for _, lands in chunks[1:] for t in lands], "mod_wait")
    mod_all = _swap_halves([mod_land])[0].reshape(8, L, 8, ncol)
    mod_mine = lax.dynamic_index_in_dim(mod_all, dev, axis=2, keepdims=False)
    mod = jnp.concatenate([mod_mine[2 * p] for p in range(N_CHIPS)], axis=-1).reshape(L, 9, D)

    rest_in_flight, rest_begun = _gather_start(chunks[1:], mod, "gather_start_rest")
    in_flight = first_in_flight + rest_in_flight

    forwarding = {}

    def forward(k, after):
        lands = _gather_wait(*in_flight[k], [after, rest_begun], "gather_wait_%d" % k)
        forwarding[k], begun = _split_start(_forward_copies, lands, 3 * len(lands), "gather_forward_start_%d" % k)
        return begun

    def gathered(key, after):
        k = chunk_keys.index(key)
        order = [after]
        if k not in forwarding:
            order.append(forward(k, after))
        if 3 <= k + 1 < len(chunk_keys):
            order.append(forward(k + 1, after))
        lands = _split_wait(_forward_copies, *forwarding[k], order, "gather_forward_wait_%d" % k)
        return [t.reshape(N_CHIPS, 1, 2 * t.shape[3], t.shape[4]) for t in lands]

    pc = jnp.stack([p_me, ac]).astype(jnp.int32)
    groups = {}
    scattering = {}

    pairing = {}

    def start_pairs(tag, after=()):
        g5 = [g.reshape(g.shape[:2] + (2, g.shape[2] // 2, g.shape[3])) for _, _, g in groups[tag]]
        gots = [lax.empty(g.shape[:2] + g.shape[3:], bf16) for g in g5]
        pairing[tag], begun = _split_start(_pair_copies, g5 + gots, len(g5), "pair_exchange_start_%s" % tag, after)
        return begun

    def start_group(tag, after):
        arrays = _split_wait(_pair_copies, *pairing[tag], after, "pair_exchange_wait_%s" % tag)
        n = len(arrays) // 2
        hsum = _pair_sum(arrays[:n], arrays[n:], pc)
        scattering[tag], begun = _scatter_start(hsum, "scatter_start_%s" % tag)
        return begun

    def grads_done(l, s, grads):
        if l == 1:
            groups.setdefault("l1", []).extend(grads)
            return start_pairs("l1")[0, 0] if s == 0 else jnp.zeros((), f32)
        groups["l0a" if s == 1 else "l0b"] = grads
        return start_pairs("l0a")[0, 0] if s == 1 else jnp.zeros((), f32)

    swapping = {}

    def reduce_group(tag, after):
        hsum, recv = _scatter_wait(*scattering[tag], after, "scatter_wait_%s" % tag)
        full = _sum_shards(hsum, recv, pc)
        swapping[tag], begun = _split_start(_swap_copies, full, len(full), "swap_halves_start_%s" % tag)
        return [begun]

    def ffn_bwd_issued(l, s, dx):
        if l == 1:
            return jnp.zeros((), f32)
        return start_group("l1" if s == 1 else "l0a", [dx])[0, 0]

    small = {k: w[k] for k in _REPL if k != "ada_b"}
    small.update(ln_g=ln_g_full, ln_b=ln_b_full, glu_w=glu_w_full, att_bias=att_bias)
    loss_dev, grad_x, dmod, sgrads = _local_step(
        x[0], loss_target[0], mod, small, lambda l, s, after: gathered(("ffn", l, s), after),
        lambda l, after: gathered(("mix", l), after), grads_done, ffn_bwd_issued)
    loss = lax.psum(loss_dev[0, 0], ("x", "y", "c"))

    names = ("rel_bias", "ln_g", "ln_b", "ssm_a_re", "ssm_a_im", "ssm_log_dt", "ssm_b_re", "ssm_b_im", "ssm_c_re",
             "ssm_c_im", "ssm_d", "glu_w", "glu_b", "pool_w", "pool_scale")
    gpack = _pack([dmod] + [sgrads[k] for k in names])
    grows = gpack.shape[0]
    land = lax.dynamic_update_slice(lax.empty((N_CHIPS, 2, grows, 128), f32), gpack[None, None], (p_me, ac, 0, 0))
    small_in_flight, small_begun = _plane_start(gpack, land, "small_grads_start")
    l0b_begun = start_group("l0b", [start_pairs("l0b", (small_begun,))])

    out_g, out_d, out_m, out_v = {}, {}, {}, {}
    row_tile = dict(zip(_BIG, (352, 352, 352, 256, 256)))
    big = {name: None for name in _BIG}

    def update_group(tag, after):
        full = _split_wait(_swap_copies, *swapping[tag], after, "swap_halves_wait_%s" % tag)
        for (name, row0, _), g in zip(groups[tag], full):
            shp = w[name].shape
            r2 = (int(np.prod(shp[:-1])), shp[-1])
            big[name] = _adamw(w[name].reshape(r2), m[name].reshape(r2), v[name].reshape(r2), g.reshape(-1, shp[-1]),
                               row_tile[name], row0, big[name])
        return [big[name][1] for name, _, _ in groups[tag]]

    after = reduce_group("l0a", reduce_group("l1", [grad_x, l0b_begun]))
    after = update_group("l0a", update_group("l1", after))

    land = _plane_wait(*small_in_flight, after, "small_grads_wait")
    gall = _swap_halves([land])[0].reshape(8, grows, 128)
    gsum = _unpack(_sum8(gall), [(L, 9 * D)] + [sgrads[k].shape for k in names])
    red = dict(zip(("ada_b",) + names, gsum))
    red["ln_g"] = lax.dynamic_slice_in_dim(red["ln_g"], p_me * 256, 256, axis=2)
    red["ln_b"] = lax.dynamic_slice_in_dim(red["ln_b"], p_me * 256, 256, axis=2)
    red["glu_w"] = lax.dynamic_slice_in_dim(red["glu_w"], p_me * 64, 64, axis=1)

    dmod_all = gall[:, :L * 9 * D // 128].reshape(8, L, 9 * D)
    dmod_cols = jnp.transpose(lax.dynamic_slice_in_dim(dmod_all, p_me * ncol, ncol, axis=2), (1, 0, 2))
    g_ada_w = _ada_wgrad(jnp.transpose(c_all), dmod_cols)

    r2 = (L * D, ncol)
    res = _adamw(ada_w.reshape(r2), m["ada_w"].reshape(r2), v["ada_w"].reshape(r2), g_ada_w.reshape(r2), 128)
    out_g["ada_w"], out_d["ada_w"], out_m["ada_w"], out_v["ada_w"] = [t.reshape(ada_w.shape) for t in res]

    res_small = _adamw(*small_packs, _pack([red[k] for k in small_names]), small_packs[0].shape[0])
    for t, dst in zip(res_small, (out_g, out_d, out_m, out_v)):
        for k, a in zip(small_names, _unpack(t, [w[k].shape for k in small_names])):
            dst[k] = a

    update_group("l0b", reduce_group("l0b", [res_small[1], res[1]]))
    for name in _BIG:
        res = [t.reshape(w[name].shape) for t in big[name]]
        out_g[name], out_d[name], out_m[name], out_v[name] = [jnp.swapaxes(t, 2, 3) for t in res] if name in transposed else res

    return (loss, grad_x[None], *[out_g[k] for k in _ORDER], *[out_d[k] for k in _ORDER],
            *[out_m[k] for k in _ORDER], *[out_v[k] for k in _ORDER])
```

```python
import math

import numpy as np
import jax
import jax.numpy as jnp
from jax import lax
from jax.experimental import pallas as pl
from jax.experimental.pallas import tpu as pltpu

f32 = jnp.float32
bf16 = jnp.bfloat16
MESH = pl.DeviceIdType.MESH

D_MODEL = 1024
SEQ = 2048
DEPTH = 2
HEAD_DIM = 64
N_HEADS = 8
D_ATT = 512
DILATIONS = (1, 4, 16)
BLOCKS_PER_RESIDUE = (16, 4, 1)
ATT_BLOCK = 128
N_UNITS = SEQ // ATT_BLOCK
N_GROUPS = 16
SSM_STATE = 64
D_SSM = 256
D_STATE = N_GROUPS * SSM_STATE
POOL_WINDOWS = (2, 4, 8, 16)
POOL_GROUP = 64
D_POOL = 256
POOL_HALO = 16
D_FF = 2816
N_BUCKETS = 32
MAX_DISTANCE = 2048
ALPHA = (2 * DEPTH) ** 0.25
FFN_RES = 0.5
LN_EPS = 1e-5
NEG = -1e30
N_CHIPS = 4
FF_SHARD = D_FF // N_CHIPS
SCAN_SEG = 8

ADAM_LR, ADAM_B1, ADAM_B2, ADAM_EPS, ADAM_WD, ADAM_STEP = 0.001, 0.9, 0.999, 1e-08, 0.01, 10

TOK_TILE = 512


def _cp(dims=None, vmem_mb=None):
    kw = {}
    if dims is not None:
        kw["dimension_semantics"] = dims
    if vmem_mb is not None:
        kw["vmem_limit_bytes"] = vmem_mb << 20
    return pltpu.CompilerParams(**kw)


def _dot(a, b):
    return jnp.dot(a, b, preferred_element_type=f32)


def _dot_nt(a, b):
    return lax.dot_general(a, b, (((1,), (1,)), ((), ())), preferred_element_type=f32)


def _dot_tn(a, b):
    return lax.dot_general(a, b, (((0,), (0,)), ((), ())), preferred_element_type=f32)


def _ln_stats(v):
    mu = jnp.mean(v, -1, keepdims=True)
    d = v - mu
    var = jnp.mean(d * d, -1, keepdims=True)
    rstd = lax.rsqrt(var + LN_EPS)
    return d * rstd, rstd


def _ln_bwd(dxh, xh, rstd):
    return rstd * (dxh - jnp.mean(dxh, -1, keepdims=True) - xh * jnp.mean(dxh * xh, -1, keepdims=True))


_GELU_C = math.sqrt(2.0 / math.pi)


def _gelu(y):
    return 0.5 * y * (1.0 + jnp.tanh(_GELU_C * (y + 0.044715 * y * y * y)))


def _gelu_grad(y):
    t = jnp.tanh(_GELU_C * (y + 0.044715 * y * y * y))
    return 0.5 * (1.0 + t) + 0.5 * y * (1.0 - t * t) * (_GELU_C * (1.0 + 3 * 0.044715 * y * y))


def _full(shape):
    return pl.BlockSpec(shape, lambda *_: (0,) * len(shape))


def _hbm(*args):
    return [pltpu.with_memory_space_constraint(a, pltpu.HBM) if getattr(a, "ndim", 0) >= 2 else a for a in args]


def _ffn_fwd(x, mod3, wg, wu, wd, ls, lng, lnb):
    S, D = x.shape
    Fs = wg.shape[-2]
    ts = 2 * TOK_TILE

    def body(x_ref, mod_ref, wg_ref, wu_ref, wd_ref, lng_ref, lnb_ref, xo_ref, f_ref, g_ref, u_ref, h_ref, acc_sc):
        j = pl.program_id(1)

        @pl.when(j == 0)
        def _():
            xh, _ = _ln_stats(x_ref[...])
            h_ref[...] = (xh * (1.0 + mod_ref[1:2, :]) + mod_ref[0:1, :]).astype(bf16)
            acc_sc[...] = jnp.zeros_like(acc_sc)

        h = h_ref[...]
        g = _dot_nt(h, wg_ref[0, 0])
        u = _dot_nt(h, wu_ref[0, 0])
        g_ref[0] = g.astype(bf16)
        u_ref[0] = u.astype(bf16)
        a = (g * jax.nn.sigmoid(g) * u).astype(bf16)
        acc_sc[...] += _dot(a, wd_ref[0, 0])

        @pl.when(j == N_CHIPS - 1)
        def _():
            f = acc_sc[...]
            f_ref[...] = f
            r = ALPHA * x_ref[...] + (FFN_RES * mod_ref[2:3, :]) * f
            rh, _ = _ln_stats(r)
            xo_ref[...] = rh * lng_ref[...] + lnb_ref[...]

    tok = pl.BlockSpec((ts, D), lambda i, j: (i, 0))
    wrow = pl.BlockSpec((1, 1, Fs, D), lambda i, j: (j, ls, 0, 0))
    hid = pl.BlockSpec((1, ts, Fs), lambda i, j: (j, i, 0))
    return pl.pallas_call(
        body, name="ffn_fwd", grid=(S // ts, N_CHIPS),
        in_specs=[tok, _full((3, D)), wrow, wrow, wrow, _full((1, D)), _full((1, D))],
        out_specs=[tok, tok, hid, hid, tok],
        out_shape=[jax.ShapeDtypeStruct((S, D), f32), jax.ShapeDtypeStruct((S, D), f32),
                   jax.ShapeDtypeStruct((N_CHIPS, S, Fs), bf16), jax.ShapeDtypeStruct((N_CHIPS, S, Fs), bf16),
                   jax.ShapeDtypeStruct((S, D), bf16)],
        scratch_shapes=[pltpu.VMEM((ts, D), f32)],
        compiler_params=_cp(("parallel", "arbitrary"), 56),
    )(*_hbm(x, mod3, wg, wu, wd, lng, lnb))


def _ffn_bwd(dxo, x, f, g, u, mod3, wg, wu, wd, ls, lng):
    S, D = x.shape
    Fs = wg.shape[-2]
    ts = TOK_TILE

    def body(dxo_ref, x_ref, f_ref, g_ref, u_ref, mod_ref, wg_ref, wu_ref, wd_ref, lng_ref,
             dx_ref, dg_ref, du_ref, a_ref, df_ref, dmod_ref, dlng_ref, dlnb_ref,
             dr_sc, df_sc, acc_sc):
        i = pl.program_id(0)
        j = pl.program_id(1)

        @pl.when((i == 0) & (j == 0))
        def _():
            dmod_ref[...] = jnp.zeros_like(dmod_ref)
            dlng_ref[...] = jnp.zeros_like(dlng_ref)
            dlnb_ref[...] = jnp.zeros_like(dlnb_ref)

        @pl.when(j == 0)
        def _():
            xv = x_ref[...]
            fv = f_ref[...]
            gate = mod_ref[2:3, :]
            rh, rstd = _ln_stats(ALPHA * xv + (FFN_RES * gate) * fv)
            dy = dxo_ref[...]
            dlng_ref[...] += jnp.sum(dy * rh, 0, keepdims=True)
            dlnb_ref[...] += jnp.sum(dy, 0, keepdims=True)
            dr = _ln_bwd(dy * lng_ref[...], rh, rstd)
            dr_sc[...] = dr
            dmod_ref[2:3, :] += jnp.sum(FFN_RES * dr * fv, 0, keepdims=True)
            df = ((FFN_RES * gate) * dr).astype(bf16)
            df_sc[...] = df
            df_ref[...] = df
            acc_sc[...] = jnp.zeros_like(acc_sc)

        da = _dot_nt(df_sc[...], wd_ref[0, 0])
        gv = g_ref[0].astype(f32)
        uv = u_ref[0].astype(f32)
        sg = jax.nn.sigmoid(gv)
        si = gv * sg
        a_ref[0] = (si * uv).astype(bf16)
        dgv = (da * uv * (sg * (1.0 + gv * (1.0 - sg)))).astype(bf16)
        duv = (da * si).astype(bf16)
        dg_ref[0] = dgv
        du_ref[0] = duv
        acc_sc[...] += _dot(dgv, wg_ref[0, 0]) + _dot(duv, wu_ref[0, 0])

        @pl.when(j == N_CHIPS - 1)
        def _():
            dh = acc_sc[...]
            xh, rstd0 = _ln_stats(x_ref[...])
            dmod_ref[0:1, :] += jnp.sum(dh, 0, keepdims=True)
            dmod_ref[1:2, :] += jnp.sum(dh * xh, 0, keepdims=True)
            dx_ref[...] = _ln_bwd(dh * (1.0 + mod_ref[1:2, :]), xh, rstd0) + ALPHA * dr_sc[...]

    tok = pl.BlockSpec((ts, D), lambda i, j: (i, 0))
    wrow = pl.BlockSpec((1, 1, Fs, D), lambda i, j: (j, ls, 0, 0))
    hid = pl.BlockSpec((1, ts, Fs), lambda i, j: (j, i, 0))
    hid_shape = jax.ShapeDtypeStruct((N_CHIPS, S, Fs), bf16)
    return pl.pallas_call(
        body, name="ffn_bwd", grid=(S // ts, N_CHIPS),
        in_specs=[tok, tok, tok, hid, hid, _full((3, D)), wrow, wrow, wrow, _full((1, D))],
        out_specs=[tok, hid, hid, hid, tok, _full((3, D)), _full((1, D)), _full((1, D))],
        out_shape=[jax.ShapeDtypeStruct((S, D), f32), hid_shape, hid_shape, hid_shape,
                   jax.ShapeDtypeStruct((S, D), bf16),
                   jax.ShapeDtypeStruct((3, D), f32), jax.ShapeDtypeStruct((1, D), f32), jax.ShapeDtypeStruct((1, D), f32)],
        scratch_shapes=[pltpu.VMEM((ts, D), f32), pltpu.VMEM((ts, D), bf16), pltpu.VMEM((ts, D), f32)],
        compiler_params=_cp(("arbitrary", "arbitrary"), 56),
    )(*_hbm(dxo, x, f, g, u, mod3, wg, wu, wd, lng))


def _ffn_wgrad(h, dg, du, a, df, gwg, gwu, gwd, ls):
    S, D = h.shape
    Fs = dg.shape[-1]
    tk = 2 * TOK_TILE
    nk = S // tk

    def body(h_ref, dg_ref, du_ref, a_ref, df_ref, _g0, _g1, _g2, gwg_ref, gwu_ref, gwd_ref, ag_sc, au_sc, ad_sc):
        k = pl.program_id(1)

        @pl.when(k == 0)
        def _():
            ag_sc[...] = jnp.zeros_like(ag_sc)
            au_sc[...] = jnp.zeros_like(au_sc)
            ad_sc[...] = jnp.zeros_like(ad_sc)

        hv = h_ref[...]
        ag_sc[...] += _dot_tn(dg_ref[0], hv)
        au_sc[...] += _dot_tn(du_ref[0], hv)
        ad_sc[...] += _dot_tn(a_ref[0], df_ref[...])

        @pl.when(k == nk - 1)
        def _():
            gwg_ref[0, 0] = ag_sc[...].astype(bf16)
            gwu_ref[0, 0] = au_sc[...].astype(bf16)
            gwd_ref[0, 0] = ad_sc[...].astype(bf16)

    tok = pl.BlockSpec((tk, D), lambda p, k: (k, 0))
    hid = pl.BlockSpec((1, tk, Fs), lambda p, k: (p, k, 0))
    anyspec = pl.BlockSpec(memory_space=pl.ANY)
    orow = pl.BlockSpec((1, 1, Fs, D), lambda p, k: (p, ls, 0, 0))
    return pl.pallas_call(
        body, name="ffn_wgrad", grid=(N_CHIPS, nk),
        in_specs=[tok, hid, hid, hid, tok, anyspec, anyspec, anyspec],
        out_specs=[orow, orow, orow],
        out_shape=[jax.ShapeDtypeStruct(gwg.shape, bf16), jax.ShapeDtypeStruct(gwu.shape, bf16),
                   jax.ShapeDtypeStruct(gwd.shape, bf16)],
        scratch_shapes=[pltpu.VMEM((Fs, D), f32), pltpu.VMEM((Fs, D), f32), pltpu.VMEM((Fs, D), f32)],
        input_output_aliases={5: 0, 6: 1, 7: 2},
        compiler_params=_cp(("parallel", "arbitrary"), 48),
    )(*_hbm(h, dg, du, a, df, gwg, gwu, gwd))


_LANES = 128
_QKV_BLOCKS = D_ATT // _LANES


def _res_spec(lead, d, width, index):
    return pl.BlockSpec((lead, d, TOK_TILE // d, width), index)


def _res_spec3(d, width):
    return pl.BlockSpec((d, TOK_TILE // d, width), lambda i: (0, i, 0))


def _rows_to_residues(tile_bufs, d, put):
    for r in range(d):
        for cb, buf in enumerate(tile_bufs):
            put(r, cb, buf[pl.ds(r, TOK_TILE // d, stride=d), :])


def _residues_to_rows(tile_bufs, d, get):
    for r in range(d):
        for cb, buf in enumerate(tile_bufs):
            buf[pl.ds(r, TOK_TILE // d, stride=d), :] = get(r, cb)


def _mix_in_fwd(x, mod3, w_in, l):
    S, D = x.shape
    N = w_in.shape[-1]
    ts = TOK_TILE

    def body(x_ref, mod_ref, w_ref, o1_ref, o4_ref, o16_ref, zr_ref, h_ref, *bufs):
        j = pl.program_id(1)

        @pl.when(j == 0)
        def _():
            xh, _ = _ln_stats(x_ref[...])
            h_ref[...] = (xh * (1.0 + mod_ref[1:2, :]) + mod_ref[0:1, :]).astype(bf16)

        z = _dot(h_ref[...], w_ref[0, 0])

        @pl.when(j == N_CHIPS - 1)
        def _():
            zr_ref[...] = z

        @pl.when(j < N_CHIPS - 1)
        def _():
            zz = z * jnp.where(j == 0, HEAD_DIM ** -0.5, 1.0)
            o1_ref[j, 0] = zz.astype(bf16)
            for cb, buf in enumerate(bufs):
                buf[...] = zz[:, _LANES * cb:_LANES * (cb + 1)]
            for d, o_ref in zip(DILATIONS[1:], (o4_ref, o16_ref)):
                def put(r, cb, piece, o_ref=o_ref):
                    o_ref[j, r, :, _LANES * cb:_LANES * (cb + 1)] = piece.astype(bf16)
                _rows_to_residues(bufs, d, put)

    tok = pl.BlockSpec((ts, D), lambda i, j: (i, 0))
    res = [_res_spec(3, d, N, lambda i, j: (0, 0, i, 0)) for d in DILATIONS]
    return pl.pallas_call(
        body, name="mix_in_fwd", grid=(S // ts, N_CHIPS),
        in_specs=[tok, _full((3, D)), pl.BlockSpec((1, 1, D, N), lambda i, j: (j, l, 0, 0))],
        out_specs=res + [pl.BlockSpec((ts, N), lambda i, j: (i, 0)), tok],
        out_shape=[jax.ShapeDtypeStruct((3, d, S // d, N), bf16) for d in DILATIONS]
        + [jax.ShapeDtypeStruct((S, N), f32), jax.ShapeDtypeStruct((S, D), bf16)],
        scratch_shapes=[pltpu.VMEM((ts, _LANES), f32)] * _QKV_BLOCKS,
        compiler_params=_cp(("parallel", "arbitrary"), 40),
    )(*_hbm(x, mod3, w_in))


def _mix_in_bwd(dqkv, d_rest, dx_res, x, mod3, w_in, l):
    S, D = x.shape
    N = w_in.shape[-1]
    ts = TOK_TILE

    def body(d1_ref, d4_ref, d16_ref, dr_ref, dxr_ref, x_ref, mod_ref, w_ref, dx_ref, dmod_ref, dz_ref, acc_sc, *bufs):
        i = pl.program_id(0)
        j = pl.program_id(1)

        @pl.when((i == 0) & (j == 0))
        def _():
            dmod_ref[...] = jnp.zeros_like(dmod_ref)

        @pl.when(j == 0)
        def _():
            acc_sc[...] = jnp.zeros_like(acc_sc)

        @pl.when(j == N_CHIPS - 1)
        def _():
            dz_ref[0] = dr_ref[...]

        @pl.when(j < N_CHIPS - 1)
        def _():
            for d, d_ref, tile_bufs in ((4, d4_ref, bufs[:_QKV_BLOCKS]), (16, d16_ref, bufs[_QKV_BLOCKS:])):
                _residues_to_rows(tile_bufs, d, lambda r, cb, d_ref=d_ref: d_ref[0, r, :, _LANES * cb:_LANES * (cb + 1)].astype(f32))
            for cb in range(_QKV_BLOCKS):
                cols = slice(_LANES * cb, _LANES * (cb + 1))
                dz_ref[0, :, cols] = (d1_ref[0, 0, :, cols].astype(f32) + bufs[cb][...] + bufs[_QKV_BLOCKS + cb][...]).astype(bf16)

        acc_sc[...] += _dot_nt(dz_ref[0], w_ref[0, 0])

        @pl.when(j == N_CHIPS - 1)
        def _():
            dh = acc_sc[...]
            xh, rstd0 = _ln_stats(x_ref[...])
            dmod_ref[0:1, :] += jnp.sum(dh, 0, keepdims=True)
            dmod_ref[1:2, :] += jnp.sum(dh * xh, 0, keepdims=True)
            dx_ref[...] = _ln_bwd(dh * (1.0 + mod_ref[1:2, :]), xh, rstd0) + dxr_ref[...]

    tok = pl.BlockSpec((ts, D), lambda i, j: (i, 0))
    res = [_res_spec(1, d, N, lambda i, j: (jnp.minimum(j, 2), 0, i, 0)) for d in DILATIONS]
    return pl.pallas_call(
        body, name="mix_in_bwd", grid=(S // ts, N_CHIPS),
        in_specs=res + [pl.BlockSpec((ts, N), lambda i, j: (i, 0)), tok, tok, _full((3, D)),
                        pl.BlockSpec((1, 1, D, N), lambda i, j: (j, l, 0, 0))],
        out_specs=[tok, _full((3, D)), pl.BlockSpec((1, ts, N), lambda i, j: (j, i, 0))],
        out_shape=[jax.ShapeDtypeStruct((S, D), f32), jax.ShapeDtypeStruct((3, D), f32),
                   jax.ShapeDtypeStruct((N_CHIPS, S, N), bf16)],
        scratch_shapes=[pltpu.VMEM((ts, D), f32)] + [pltpu.VMEM((ts, _LANES), f32)] * (2 * _QKV_BLOCKS),
        compiler_params=_cp(("arbitrary", "arbitrary"), 40),
    )(*_hbm(*dqkv, d_rest, dx_res, x, mod3, w_in))


def _mix_in_wgrad(h, dz, gw, l):
    S, D = h.shape
    N = dz.shape[-1]
    tk = TOK_TILE
    nk = S // tk

    def body(h_ref, dz_ref, _g, gw_ref, acc_sc):
        k = pl.program_id(1)

        @pl.when(k == 0)
        def _():
            acc_sc[...] = jnp.zeros_like(acc_sc)

        acc_sc[...] += _dot_tn(h_ref[...], dz_ref[0])

        @pl.when(k == nk - 1)
        def _():
            gw_ref[0, 0] = acc_sc[...].astype(bf16)

    return pl.pallas_call(
        body, name="mix_in_wgrad", grid=(N_CHIPS, nk),
        in_specs=[pl.BlockSpec((tk, D), lambda p, k: (k, 0)), pl.BlockSpec((1, tk, N), lambda p, k: (p, k, 0)),
                  pl.BlockSpec(memory_space=pl.ANY)],
        out_specs=pl.BlockSpec((1, 1, D, N), lambda p, k: (p, l, 0, 0)),
        out_shape=jax.ShapeDtypeStruct(gw.shape, bf16),
        scratch_shapes=[pltpu.VMEM((D, N), f32)],
        input_output_aliases={2: 0},
        compiler_params=_cp(("parallel", "arbitrary"), 40),
    )(*_hbm(h, dz, gw))


def _mix_out_fwd(x, y_att, y_ssm, y_pool, mod3, w_out, l, lng, lnb):
    S, D = x.shape
    ts = TOK_TILE

    def body(x_ref, ya_ref, ys_ref, yp_ref, mod_ref, w_ref, lng_ref, lnb_ref, xo_ref, y_ref):
        ya = ya_ref[...].astype(bf16)
        y = (_dot(ya[:, 0:256], w_ref[0, 0]) + _dot(ya[:, 256:512], w_ref[1, 0])
             + _dot(ys_ref[...].astype(bf16), w_ref[2, 0]) + _dot(yp_ref[...].astype(bf16), w_ref[3, 0]))
        y_ref[...] = y
        rh, _ = _ln_stats(ALPHA * x_ref[...] + mod_ref[2:3, :] * y)
        xo_ref[...] = rh * lng_ref[...] + lnb_ref[...]

    tok = pl.BlockSpec((ts, D), lambda i: (i, 0))
    return pl.pallas_call(
        body, name="mix_out_fwd", grid=(S // ts,),
        in_specs=[tok, pl.BlockSpec((ts, D_ATT), lambda i: (i, 0)), pl.BlockSpec((ts, D_SSM), lambda i: (i, 0)),
                  pl.BlockSpec((ts, D_POOL), lambda i: (i, 0)), _full((3, D)),
                  pl.BlockSpec((N_CHIPS, 1, 256, D), lambda i: (0, l, 0, 0)), _full((1, D)), _full((1, D))],
        out_specs=[tok, tok],
        out_shape=[jax.ShapeDtypeStruct((S, D), f32), jax.ShapeDtypeStruct((S, D), f32)],
        compiler_params=_cp(("parallel",), 40),
    )(*_hbm(x, y_att, y_ssm, y_pool, mod3, w_out, lng, lnb))


def _mix_out_bwd(dxo, x, y, y_att, y_ssm, y_pool, mod3, w_out, l, lng, gw_out):
    S, D = x.shape
    ts = TOK_TILE
    nt = S // ts

    def body(dxo_ref, x_ref, y_ref, ya_ref, ys_ref, yp_ref, mod_ref, w_ref, lng_ref, _g,
             dxr_ref, da_ref, ds_ref, dp_ref, dgate_ref, dlng_ref, dlnb_ref, gw_ref, acc_sc):
        i = pl.program_id(0)

        @pl.when(i == 0)
        def _():
            dgate_ref[...] = jnp.zeros_like(dgate_ref)
            dlng_ref[...] = jnp.zeros_like(dlng_ref)
            dlnb_ref[...] = jnp.zeros_like(dlnb_ref)
            acc_sc[...] = jnp.zeros_like(acc_sc)

        gate = mod_ref[2:3, :]
        yv = y_ref[...]
        rh, rstd = _ln_stats(ALPHA * x_ref[...] + gate * yv)
        dy_out = dxo_ref[...]
        dlng_ref[...] += jnp.sum(dy_out * rh, 0, keepdims=True)
        dlnb_ref[...] += jnp.sum(dy_out, 0, keepdims=True)
        dr = _ln_bwd(dy_out * lng_ref[...], rh, rstd)
        dxr_ref[...] = ALPHA * dr
        dgate_ref[...] += jnp.sum(dr * yv, 0, keepdims=True)
        dy = (gate * dr).astype(bf16)
        da_ref[:, 0:256] = _dot_nt(dy, w_ref[0, 0])
        da_ref[:, 256:512] = _dot_nt(dy, w_ref[1, 0])
        ds_ref[...] = _dot_nt(dy, w_ref[2, 0])
        dp_ref[...] = _dot_nt(dy, w_ref[3, 0])
        ya = ya_ref[...].astype(bf16)
        acc_sc[0] += _dot_tn(ya[:, 0:256], dy)
        acc_sc[1] += _dot_tn(ya[:, 256:512], dy)
        acc_sc[2] += _dot_tn(ys_ref[...].astype(bf16), dy)
        acc_sc[3] += _dot_tn(yp_ref[...].astype(bf16), dy)

        @pl.when(i == nt - 1)
        def _():
            gw_ref[:, 0] = acc_sc[...].astype(bf16)

    tok = pl.BlockSpec((ts, D), lambda i: (i, 0))
    t512 = pl.BlockSpec((ts, D_ATT), lambda i: (i, 0))
    t256 = pl.BlockSpec((ts, 256), lambda i: (i, 0))
    wspec = pl.BlockSpec((N_CHIPS, 1, 256, D), lambda i: (0, l, 0, 0))
    return pl.pallas_call(
        body, name="mix_out_bwd", grid=(nt,),
        in_specs=[tok, tok, tok, t512, t256, t256, _full((3, D)), wspec, _full((1, D)), pl.BlockSpec(memory_space=pl.ANY)],
        out_specs=[tok, t512, t256, t256, _full((1, D)), _full((1, D)), _full((1, D)), wspec],
        out_shape=[jax.ShapeDtypeStruct((S, D), f32), jax.ShapeDtypeStruct((S, D_ATT), f32),
                   jax.ShapeDtypeStruct((S, D_SSM), f32), jax.ShapeDtypeStruct((S, D_POOL), f32),
                   jax.ShapeDtypeStruct((1, D), f32), jax.ShapeDtypeStruct((1, D), f32), jax.ShapeDtypeStruct((1, D), f32),
                   jax.ShapeDtypeStruct(gw_out.shape, bf16)],
        scratch_shapes=[pltpu.VMEM((N_CHIPS, 256, D), f32)],
        input_output_aliases={9: 7},
        compiler_params=_cp(("arbitrary",), 48),
    )(*_hbm(dxo, x, y, y_att, y_ssm, y_pool, mod3, w_out, lng, gw_out))


def _t5_bucket(dist):
    max_exact = N_BUCKETS // 2
    d = np.maximum(dist, 1).astype(np.float32)
    large = max_exact + (np.log(d / max_exact) / math.log(MAX_DISTANCE / max_exact)
                         * (N_BUCKETS - max_exact)).astype(np.int32)
    large = np.minimum(large, N_BUCKETS - 1)
    return np.where(dist < max_exact, dist, large).astype(np.int32)


def _bucket_table():
    q = ATT_BLOCK
    i = np.arange(q)[:, None]
    j = np.arange(2 * q)[None, :]
    r = i + q - j
    in_band = (r >= 0) & (r <= q)
    tabs = [np.where(in_band, _t5_bucket(np.clip(r, 0, None) * d), -1) for d in DILATIONS]
    return np.stack(tabs).astype(np.int32)


def _bias_fwd(rel_bias, table):
    def body(rb_ref, tab_ref, out_ref):
        for b in range(3):
            tb = tab_ref[b]
            for h in range(N_HEADS):
                def pick(k, acc):
                    return jnp.where(tb == k, rb_ref[k, h], acc)
                out_ref[b, h] = lax.fori_loop(0, N_BUCKETS, pick, jnp.where(tb < 0, NEG, 0.0).astype(f32))

    return pl.pallas_call(
        body, name="bias_fwd",
        in_specs=[pl.BlockSpec(memory_space=pltpu.SMEM), pl.BlockSpec(memory_space=pltpu.VMEM)],
        out_specs=pl.BlockSpec(memory_space=pltpu.VMEM),
        out_shape=jax.ShapeDtypeStruct((3, N_HEADS, ATT_BLOCK, 2 * ATT_BLOCK), f32),
    )(rel_bias, table)


def _bias_bwd(dbias, table):
    def body(db_ref, tab_ref, out_ref):
        def per_bucket(k, c):
            for h in range(N_HEADS):
                tot = jnp.zeros((), f32)
                for b in range(3):
                    tot = tot + jnp.sum(jnp.where(tab_ref[b] == k, db_ref[b, h], 0.0))
                out_ref[k, h] = tot
            return c
        lax.fori_loop(0, N_BUCKETS, per_bucket, 0)

    return pl.pallas_call(
        body, name="bias_bwd",
        in_specs=[pl.BlockSpec(memory_space=pltpu.VMEM), pl.BlockSpec(memory_space=pltpu.VMEM)],
        out_specs=pl.BlockSpec(memory_space=pltpu.SMEM),
        out_shape=jax.ShapeDtypeStruct((N_BUCKETS, N_HEADS), f32),
    )(dbias, table)


def _att_unit(u, nbr):
    rows = pl.ds(pl.multiple_of(u * ATT_BLOCK, ATT_BLOCK), ATT_BLOCK)
    prev = pl.ds(pl.multiple_of(jnp.maximum(u - 1, 0) * ATT_BLOCK, ATT_BLOCK), ATT_BLOCK)
    return rows, prev, (u % nbr) != 0


_N_PAIRS = N_HEADS // 2


def _pair_rows(t):
    lane = lax.broadcasted_iota(jnp.int32, t.shape, 1)
    zero = jnp.zeros_like(t)
    return jnp.concatenate([jnp.where(lane < HEAD_DIM, t, zero), jnp.where(lane >= HEAD_DIM, t, zero)], axis=0)


def _pair_cols(big):
    lane = lax.broadcasted_iota(jnp.int32, (ATT_BLOCK, _LANES), 1)
    return jnp.where(lane < HEAD_DIM, big[:ATT_BLOCK], big[ATT_BLOCK:])


def _pair_column(ref, rows, hp):
    t = ref[rows, :]
    return jnp.concatenate([t[:, 2 * hp:2 * hp + 1], t[:, 2 * hp + 1:2 * hp + 2]], axis=0)


def _pair_band(ref, rows, prev, nbr, hp):
    lanes = pl.ds(_LANES * hp, _LANES)
    cur = ref[0, rows, lanes]
    return cur if nbr == 1 else jnp.concatenate([ref[0, prev, lanes], cur], axis=0)


def _pair_scores(q_ref, k_ref, b_ref, rows, prev, valid_prev, nbr, hp):
    qbd = _pair_rows(q_ref[0, rows, pl.ds(_LANES * hp, _LANES)])
    kb = _pair_band(k_ref, rows, prev, nbr, hp)
    bias = b_ref[0, 2 * hp:2 * hp + 2].reshape(2 * ATT_BLOCK, 2 * ATT_BLOCK)
    if nbr == 1:
        return qbd, kb, _dot_nt(qbd, kb) + bias[:, ATT_BLOCK:]
    s = _dot_nt(qbd, kb) + bias
    col = lax.broadcasted_iota(jnp.int32, s.shape, 1)
    return qbd, kb, jnp.where((col >= ATT_BLOCK) | valid_prev, s, NEG)


def _qkv_specs(S, branch):
    return ([pl.BlockSpec((1, S, D_ATT), lambda i, t=t: (t, 0, 0)) for t in range(3)],
            pl.BlockSpec((1, N_HEADS, ATT_BLOCK, 2 * ATT_BLOCK), lambda i: (branch, 0, 0, 0)))


def _att_fwd(qkv, bias, branch):
    S = qkv.shape[1]
    nbr = BLOCKS_PER_RESIDUE[branch]

    def body(q_ref, k_ref, v_ref, b_ref, o_ref, lse_ref):
        lse_ref[...] = jnp.zeros_like(lse_ref)

        def unit(u, c):
            rows, prev, valid_prev = _att_unit(u, nbr)
            for hp in range(_N_PAIRS):
                _, _, s = _pair_scores(q_ref, k_ref, b_ref, rows, prev, valid_prev, nbr, hp)
                m = jnp.max(s, -1, keepdims=True)
                p = jnp.exp(s - m)
                den = jnp.sum(p, -1, keepdims=True)
                big = _dot(p.astype(bf16), _pair_band(v_ref, rows, prev, nbr, hp))
                o_ref[rows, pl.ds(_LANES * hp, _LANES)] = _pair_cols(big / den)
                lse = m + jnp.log(den)
                lse_ref[rows, pl.ds(2 * hp, 1)] = lse[:ATT_BLOCK]
                lse_ref[rows, pl.ds(2 * hp + 1, 1)] = lse[ATT_BLOCK:]
            return c

        lax.fori_loop(0, N_UNITS, unit, 0)

    qkv_specs, bspec = _qkv_specs(S, branch)
    return pl.pallas_call(
        body, name="att_fwd", grid=(1,),
        in_specs=qkv_specs + [bspec],
        out_specs=[pl.BlockSpec((S, D_ATT), lambda i: (0, 0)), pl.BlockSpec((S, _LANES), lambda i: (0, 0))],
        out_shape=[jax.ShapeDtypeStruct((S, D_ATT), f32), jax.ShapeDtypeStruct((S, _LANES), f32)],
        compiler_params=_cp(("arbitrary",), 40),
    )(*_hbm(qkv, qkv, qkv, bias))


def _att_bwd(qkv, do, lse, crow, bias, branch):
    S = qkv.shape[1]
    nbr = BLOCKS_PER_RESIDUE[branch]

    def body(q_ref, k_ref, v_ref, do_ref, lse_ref, c_ref, b_ref, dqkv_ref, db_ref, dk_sc, dv_sc):
        dk_sc[...] = jnp.zeros_like(dk_sc)
        dv_sc[...] = jnp.zeros_like(dv_sc)
        db_ref[...] = jnp.zeros_like(db_ref)

        def unit(u, c):
            rows, prev, valid_prev = _att_unit(u, nbr)
            for hp in range(_N_PAIRS):
                lanes = pl.ds(_LANES * hp, _LANES)
                qbd, kb, s = _pair_scores(q_ref, k_ref, b_ref, rows, prev, valid_prev, nbr, hp)
                p = jnp.exp(s - _pair_column(lse_ref, rows, hp))
                dobd = _pair_rows(do_ref[rows, lanes])
                ds = p * (_dot_nt(dobd, _pair_band(v_ref, rows, prev, nbr, hp)) - _pair_column(c_ref, rows, hp))
                if nbr == 1:
                    db_ref[2 * hp:2 * hp + 2, :, ATT_BLOCK:] += ds.reshape(2, ATT_BLOCK, ATT_BLOCK)
                else:
                    db_ref[2 * hp:2 * hp + 2] += ds.reshape(2, ATT_BLOCK, 2 * ATT_BLOCK)
                dsb = ds.astype(bf16)
                dqkv_ref[0, rows, lanes] = (HEAD_DIM ** -0.5 * _pair_cols(_dot(dsb, kb))).astype(bf16)
                dkb = _dot_tn(dsb, qbd)
                dvb = _dot_tn(p.astype(bf16), dobd)
                if nbr == 1:
                    dk_sc[rows, lanes] += dkb
                    dv_sc[rows, lanes] += dvb
                else:
                    dk_sc[prev, lanes] += dkb[:ATT_BLOCK]
                    dv_sc[prev, lanes] += dvb[:ATT_BLOCK]
                    dk_sc[rows, lanes] += dkb[ATT_BLOCK:]
                    dv_sc[rows, lanes] += dvb[ATT_BLOCK:]
            return c

        lax.fori_loop(0, N_UNITS, unit, 0)
        dqkv_ref[1] = dk_sc[...].astype(bf16)
        dqkv_ref[2] = dv_sc[...].astype(bf16)

    qkv_specs, bspec = _qkv_specs(S, branch)
    row = pl.BlockSpec((S, _LANES), lambda i: (0, 0))
    return pl.pallas_call(
        body, name="att_bwd", grid=(1,),
        in_specs=qkv_specs + [pl.BlockSpec((S, D_ATT), lambda i: (0, 0)), row, row, bspec],
        out_specs=[pl.BlockSpec((3, S, D_ATT), lambda i: (0, 0, 0)),
                   pl.BlockSpec((N_HEADS, ATT_BLOCK, 2 * ATT_BLOCK), lambda i: (0, 0, 0))],
        out_shape=[jax.ShapeDtypeStruct((3, S, D_ATT), bf16), jax.ShapeDtypeStruct((N_HEADS, ATT_BLOCK, 2 * ATT_BLOCK), f32)],
        scratch_shapes=[pltpu.VMEM((S, D_ATT), f32), pltpu.VMEM((S, D_ATT), f32)],
        compiler_params=_cp(("arbitrary",), 48),
    )(*_hbm(qkv, qkv, qkv, do, lse, crow, bias))


def _branch_weights(lse_ref):
    l0, l1, l2 = lse_ref[0], lse_ref[1], lse_ref[2]
    m = jnp.maximum(jnp.maximum(l0, l1), l2)
    e0, e1, e2 = jnp.exp(l0 - m), jnp.exp(l1 - m), jnp.exp(l2 - m)
    tot = e0 + e1 + e2
    return e0 / tot, e1 / tot, e2 / tot


def _att_merge(os, lses):
    S = os[0].shape[0] * os[0].shape[1]
    ts = TOK_TILE

    def body(o1_ref, o4_ref, o16_ref, l1_ref, l4_ref, l16_ref, y_ref, lt_ref, *bufs):
        obufs = (bufs[:_QKV_BLOCKS], bufs[_QKV_BLOCKS:2 * _QKV_BLOCKS])
        lt_ref[0] = l1_ref[0]
        for k, (d, o_ref, l_ref) in enumerate(((4, o4_ref, l4_ref), (16, o16_ref, l16_ref))):
            _residues_to_rows(obufs[k], d, lambda r, cb, o_ref=o_ref: o_ref[r, :, _LANES * cb:_LANES * (cb + 1)])
            _residues_to_rows([bufs[2 * _QKV_BLOCKS + k]], d, lambda r, cb, l_ref=l_ref: l_ref[r])
            lt_ref[1 + k] = bufs[2 * _QKV_BLOCKS + k][...]
        w = _branch_weights(lt_ref)
        for h in range(N_HEADS):
            cs = slice(HEAD_DIM * h, HEAD_DIM * (h + 1))
            half = slice(HEAD_DIM * (h % 2), HEAD_DIM * (h % 2 + 1))
            y_ref[:, cs] = (w[0][:, h:h + 1] * o1_ref[0, :, cs] + w[1][:, h:h + 1] * obufs[0][h // 2][:, half]
                            + w[2][:, h:h + 1] * obufs[1][h // 2][:, half])

    return pl.pallas_call(
        body, name="att_merge", grid=(S // ts,),
        in_specs=[_res_spec3(d, D_ATT) for d in DILATIONS] + [_res_spec3(d, _LANES) for d in DILATIONS],
        out_specs=[pl.BlockSpec((ts, D_ATT), lambda i: (i, 0)), pl.BlockSpec((3, ts, _LANES), lambda i: (0, i, 0))],
        out_shape=[jax.ShapeDtypeStruct((S, D_ATT), f32), jax.ShapeDtypeStruct((3, S, _LANES), f32)],
        scratch_shapes=[pltpu.VMEM((ts, _LANES), f32)] * (2 * _QKV_BLOCKS + 2),
        compiler_params=_cp(("parallel",)),
    )(*_hbm(*os, *lses))


def _att_merge_bwd(dy, y, lse3):
    S = dy.shape[0]
    ts = TOK_TILE

    def body(dy_ref, y_ref, lse_ref, do1_ref, do4_ref, do16_ref, c1_ref, c4_ref, c16_ref, *bufs):
        dobufs = (bufs[:_QKV_BLOCKS], bufs[_QKV_BLOCKS:2 * _QKV_BLOCKS], bufs[2 * _QKV_BLOCKS:3 * _QKV_BLOCKS])
        cbufs = bufs[3 * _QKV_BLOCKS:]
        w = _branch_weights(lse_ref)
        for cb in cbufs:
            cb[...] = jnp.zeros_like(cb)
        for h in range(N_HEADS):
            cs = slice(HEAD_DIM * h, HEAD_DIM * (h + 1))
            half = slice(HEAD_DIM * (h % 2), HEAD_DIM * (h % 2 + 1))
            dyh = dy_ref[:, cs]
            t = jnp.sum(dyh * y_ref[:, cs], -1, keepdims=True)
            for p in range(3):
                wp = w[p][:, h:h + 1]
                dobufs[p][h // 2][:, half] = wp * dyh
                cbufs[p][:, h:h + 1] = wp * t
        for cb in range(_QKV_BLOCKS):
            do1_ref[0, :, _LANES * cb:_LANES * (cb + 1)] = dobufs[0][cb][...].astype(bf16)
        c1_ref[0] = cbufs[0][...]
        for k, (d, do_ref, c_ref) in enumerate(((4, do4_ref, c4_ref), (16, do16_ref, c16_ref))):
            def put_do(r, cb, piece, do_ref=do_ref):
                do_ref[r, :, _LANES * cb:_LANES * (cb + 1)] = piece.astype(bf16)

            def put_c(r, cb, piece, c_ref=c_ref):
                c_ref[r] = piece

            _rows_to_residues(dobufs[1 + k], d, put_do)
            _rows_to_residues([cbufs[1 + k]], d, put_c)

    return pl.pallas_call(
        body, name="att_merge_bwd", grid=(S // ts,),
        in_specs=[pl.BlockSpec((ts, D_ATT), lambda i: (i, 0)), pl.BlockSpec((ts, D_ATT), lambda i: (i, 0)),
                  pl.BlockSpec((3, ts, _LANES), lambda i: (0, i, 0))],
        out_specs=[_res_spec3(d, D_ATT) for d in DILATIONS] + [_res_spec3(d, _LANES) for d in DILATIONS],
        out_shape=[jax.ShapeDtypeStruct((d, S // d, D_ATT), bf16) for d in DILATIONS]
        + [jax.ShapeDtypeStruct((d, S // d, _LANES), f32) for d in DILATIONS],
        scratch_shapes=[pltpu.VMEM((ts, _LANES), f32)] * (3 * _QKV_BLOCKS + 3),
        compiler_params=_cp(("parallel",)),
    )(*_hbm(dy, y, lse3))


_SSM_ROWS = 256


def _scan_in_place(sr_ref, si_ref, a_ref, reverse):
    S, N = sr_ref.shape
    nst = S // SCAN_SEG
    ar = jnp.broadcast_to(a_ref[0:1, :], (SCAN_SEG, N))
    ai = jnp.broadcast_to(a_ref[1:2, :], (SCAN_SEG, N))
    if reverse:
        ai = -ai
    row = lax.broadcasted_iota(jnp.int32, (SCAN_SEG, N), 0)
    zero = jnp.zeros((SCAN_SEG, N), f32)

    def tile(t):
        return pl.ds(pl.multiple_of((nst - 1 - t if reverse else t) * SCAN_SEG, SCAN_SEG), SCAN_SEG)

    def local(t, c):
        sr, si, pr, pi = c
        rows = tile(t)
        nsr = ar * sr - ai * si + sr_ref[rows, :]
        nsi = ar * si + ai * sr + si_ref[rows, :]
        sr_ref[rows, :] = nsr
        si_ref[rows, :] = nsi
        return nsr, nsi, ar * pr - ai * pi, ar * pi + ai * pr

    fr, fi, apr, api = lax.fori_loop(0, nst, local, (zero, zero, zero + 1.0, zero))

    def shift(v):
        if reverse:
            return jnp.where(row == SCAN_SEG - 1, 0.0, pltpu.roll(v, SCAN_SEG - 1, axis=0))
        return jnp.where(row == 0, 0.0, pltpu.roll(v, 1, axis=0))

    cr, ci = zero, zero
    for _ in range(SCAN_SEG - 1):
        cr, ci = shift(fr + apr * cr - api * ci), shift(fi + apr * ci + api * cr)

    def fix(t, c):
        pr, pi = c
        npr, npi = ar * pr - ai * pi, ar * pi + ai * pr
        rows = tile(t)
        sr_ref[rows, :] += npr * cr - npi * ci
        si_ref[rows, :] += npr * ci + npi * cr
        return npr, npi

    lax.fori_loop(0, nst, fix, (zero + 1.0, zero))


def _ssm_states(u, bre, bim, a2, l):
    S = u.shape[0]

    def body(u_ref, br_ref, bi_ref, a2_ref, sr_ref, si_ref):
        a_ref = a2_ref.at[l]
        brb = br_ref[l].astype(bf16)
        bib = bi_ref[l].astype(bf16)

        def project(t, c):
            rows = pl.ds(pl.multiple_of(t * _SSM_ROWS, _SSM_ROWS), _SSM_ROWS)
            ub = u_ref[rows, :].astype(bf16)
            sr_ref[rows, :] = _dot(ub, brb)
            si_ref[rows, :] = _dot(ub, bib)
            return c

        lax.fori_loop(0, S // _SSM_ROWS, project, 0)
        _scan_in_place(sr_ref, si_ref, a_ref, False)

    vm = pl.BlockSpec(memory_space=pltpu.VMEM)
    return pl.pallas_call(
        body, name="ssm_states", in_specs=[vm] * 4, out_specs=[vm, vm],
        out_shape=[jax.ShapeDtypeStruct((S, D_STATE), f32)] * 2,
        compiler_params=_cp(None, 48),
    )(u, bre, bim, a2)


def _ssm_out(sr, si, u, cre, cim, l, dskip, glu_w, glu_b):
    S = u.shape[0]
    ts = TOK_TILE

    def body(sr_ref, si_ref, u_ref, cr_ref, ci_ref, d_ref, w_ref, b_ref, out_ref, y_ref):
        y = (_dot(sr_ref[...].astype(bf16), cr_ref[0].astype(bf16))
             - _dot(si_ref[...].astype(bf16), ci_ref[0].astype(bf16)) + d_ref[...] * u_ref[...])
        y_ref[...] = y
        z = _dot(_gelu(y).astype(bf16), w_ref[...].astype(bf16)) + b_ref[...]
        out_ref[...] = y * jax.nn.sigmoid(z)

    st = pl.BlockSpec((ts, D_STATE), lambda i: (i, 0))
    ch = pl.BlockSpec((ts, D_SSM), lambda i: (i, 0))
    c_l = pl.BlockSpec((1, D_STATE, D_SSM), lambda i: (l, 0, 0))
    return pl.pallas_call(
        body, name="ssm_out", grid=(S // ts,),
        in_specs=[st, st, ch, c_l, c_l, _full((1, D_SSM)), _full((D_SSM, D_SSM)), _full((1, D_SSM))],
        out_specs=[ch, ch],
        out_shape=[jax.ShapeDtypeStruct((S, D_SSM), f32)] * 2,
        compiler_params=_cp(("parallel",)),
    )(*_hbm(sr, si, u, cre, cim, dskip, glu_w, glu_b))


def _ssm_out_bwd(dout, y, u, sr, si, dskip, glu_w, glu_b):
    S = u.shape[0]
    ts = TOK_TILE

    def body(do_ref, y_ref, u_ref, sr_ref, si_ref, d_ref, w_ref, b_ref,
             dy_ref, du_ref, dcr_ref, dci_ref, dd_ref, dgb_ref, dgw_ref):
        @pl.when(pl.program_id(0) == 0)
        def _():
            for r in (dcr_ref, dci_ref, dd_ref, dgb_ref, dgw_ref):
                r[...] = jnp.zeros_like(r)

        y = y_ref[...]
        dout = do_ref[...]
        wb = w_ref[...].astype(bf16)
        ge = _gelu(y).astype(bf16)
        sz = jax.nn.sigmoid(_dot(ge, wb) + b_ref[...])
        dz = dout * y * sz * (1.0 - sz)
        dzb = dz.astype(bf16)
        dgb_ref[...] += jnp.sum(dz, 0, keepdims=True)
        dgw_ref[...] += _dot_tn(ge, dzb)
        dy = dout * sz + _gelu_grad(y) * _dot_nt(dzb, wb)
        uv = u_ref[...]
        dd_ref[...] += jnp.sum(dy * uv, 0, keepdims=True)
        du_ref[...] = dy * d_ref[...]
        dy_ref[...] = dy
        dyb = dy.astype(bf16)
        dcr_ref[...] += _dot_tn(sr_ref[...].astype(bf16), dyb)
        dci_ref[...] -= _dot_tn(si_ref[...].astype(bf16), dyb)

    st = pl.BlockSpec((ts, D_STATE), lambda i: (i, 0))
    ch = pl.BlockSpec((ts, D_SSM), lambda i: (i, 0))
    c_full = _full((D_STATE, D_SSM))
    return pl.pallas_call(
        body, name="ssm_out_bwd", grid=(S // ts,),
        in_specs=[ch, ch, ch, st, st, _full((1, D_SSM)), _full((D_SSM, D_SSM)), _full((1, D_SSM))],
        out_specs=[ch, ch, c_full, c_full, _full((1, D_SSM)), _full((1, D_SSM)), _full((D_SSM, D_SSM))],
        out_shape=[jax.ShapeDtypeStruct((S, D_SSM), f32), jax.ShapeDtypeStruct((S, D_SSM), f32),
                   jax.ShapeDtypeStruct((D_STATE, D_SSM), f32), jax.ShapeDtypeStruct((D_STATE, D_SSM), f32),
                   jax.ShapeDtypeStruct((1, D_SSM), f32), jax.ShapeDtypeStruct((1, D_SSM), f32),
                   jax.ShapeDtypeStruct((D_SSM, D_SSM), f32)],
        compiler_params=_cp(("arbitrary",), 40),
    )(*_hbm(dout, y, u, sr, si, dskip, glu_w, glu_b))


def _ssm_states_bwd(dy, du_skip, u, sr, si, cre, cim, bre, bim, a2, l):
    S = u.shape[0]
    N = D_STATE
    nst = S // SCAN_SEG
    nproj = S // _SSM_ROWS

    def body(dy_ref, dus_ref, u_ref, sr_ref, si_ref, cr_ref, ci_ref, br_ref, bi_ref, a2_ref,
             du_ref, dbr_ref, dbi_ref, da_ref, lr_ref, li_ref):
        a_ref = a2_ref.at[l]
        crb = cr_ref[l].astype(bf16)
        cib = ci_ref[l].astype(bf16)

        def project(t, c):
            rows = pl.ds(pl.multiple_of(t * _SSM_ROWS, _SSM_ROWS), _SSM_ROWS)
            dyb = dy_ref[rows, :].astype(bf16)
            lr_ref[rows, :] = _dot_nt(dyb, crb)
            li_ref[rows, :] = -_dot_nt(dyb, cib)
            return c

        lax.fori_loop(0, nproj, project, 0)
        _scan_in_place(lr_ref, li_ref, a_ref, True)

        row = lax.broadcasted_iota(jnp.int32, (SCAN_SEG, N), 0)
        last = pl.ds((nst - 1) * SCAN_SEG, SCAN_SEG)
        pr = jnp.where(row == 0, 0.0, pltpu.roll(sr_ref[last, :], 1, axis=0))
        pi = jnp.where(row == 0, 0.0, pltpu.roll(si_ref[last, :], 1, axis=0))
        first = pl.ds(0, SCAN_SEG)
        acc_r = lr_ref[first, :] * pr + li_ref[first, :] * pi
        acc_i = li_ref[first, :] * pr - lr_ref[first, :] * pi

        def step(t, c):
            acc_r, acc_i = c
            rows = pl.ds(pl.multiple_of(t * SCAN_SEG, SCAN_SEG), SCAN_SEG)
            prev = pl.ds(pl.multiple_of((t - 1) * SCAN_SEG, SCAN_SEG), SCAN_SEG)
            lrv, liv, srv, siv = lr_ref[rows, :], li_ref[rows, :], sr_ref[prev, :], si_ref[prev, :]
            return acc_r + lrv * srv + liv * siv, acc_i + liv * srv - lrv * siv

        acc_r, acc_i = lax.fori_loop(1, nst, step, (acc_r, acc_i))
        da_ref[0:1, :] = jnp.sum(acc_r, 0, keepdims=True)
        da_ref[1:2, :] = jnp.sum(acc_i, 0, keepdims=True)

        brb = br_ref[l].astype(bf16)
        bib = bi_ref[l].astype(bf16)
        dbr_ref[...] = jnp.zeros_like(dbr_ref)
        dbi_ref[...] = jnp.zeros_like(dbi_ref)

        def back(t, c):
            rows = pl.ds(pl.multiple_of(t * _SSM_ROWS, _SSM_ROWS), _SSM_ROWS)
            lrb = lr_ref[rows, :].astype(bf16)
            lib = li_ref[rows, :].astype(bf16)
            du_ref[rows, :] = dus_ref[rows, :] + _dot_nt(lrb, brb) + _dot_nt(lib, bib)
            ub = u_ref[rows, :].astype(bf16)
            dbr_ref[...] += _dot_tn(ub, lrb)
            dbi_ref[...] += _dot_tn(ub, lib)
            return c

        lax.fori_loop(0, nproj, back, 0)

    vm = pl.BlockSpec(memory_space=pltpu.VMEM)
    return pl.pallas_call(
        body, name="ssm_states_bwd", in_specs=[vm] * 10, out_specs=[vm] * 4,
        out_shape=[jax.ShapeDtypeStruct((S, D_SSM), f32), jax.ShapeDtypeStruct((D_SSM, D_STATE), f32),
                   jax.ShapeDtypeStruct((D_SSM, D_STATE), f32), jax.ShapeDtypeStruct((2, D_STATE), f32)],
        scratch_shapes=[pltpu.VMEM((S, D_STATE), f32), pltpu.VMEM((S, D_STATE), f32)],
        compiler_params=_cp(None, 56),
    )(dy, du_skip, u, sr, si, cre, cim, bre, bim, a2)


_POOL_TILE = 256


def _window_sums(xt, back):
    n = xt.shape[0]
    out = []
    ws = xt
    for k in (1, 2, 4, 8):
        ws = ws + pltpu.roll(ws, k if back else n - k, axis=0)
        out.append(ws)
    return out


def _pool_count(r0, w):
    t = r0 + lax.broadcasted_iota(jnp.int32, (_POOL_TILE, POOL_GROUP), 0)
    return jnp.minimum(t + 1, w).astype(f32)


def _pool_fwd(u_pad, pool_w, pool_scale):
    S = u_pad.shape[0] - POOL_HALO
    nt = S // _POOL_TILE

    def body(u_ref, w_ref, sc_ref, y_ref):
        def tile(t, c):
            r0 = pl.multiple_of(t * _POOL_TILE, _POOL_TILE)
            for g, w in enumerate(POOL_WINDOWS):
                cs = pl.ds(POOL_GROUP * g, POOL_GROUP)
                xt = u_ref[pl.ds(r0, _POOL_TILE + POOL_HALO), cs]
                ws = _window_sums(xt, True)[g][POOL_HALO:, :]
                pooled = ws / _pool_count(r0, w) - xt[POOL_HALO:, :]
                y_ref[pl.ds(r0, _POOL_TILE), cs] = _dot(pooled.astype(bf16), w_ref[g].astype(bf16)) * sc_ref[:, cs]
            return c
        lax.fori_loop(0, nt, tile, 0)

    vm = pl.BlockSpec(memory_space=pltpu.VMEM)
    return pl.pallas_call(
        body, name="pool_fwd", in_specs=[vm, vm, vm], out_specs=vm,
        out_shape=jax.ShapeDtypeStruct((S, D_POOL), f32),
    )(u_pad, pool_w, pool_scale)


def _pool_bwd(dy_pad, u_pad, pool_w, pool_scale):
    S = u_pad.shape[0] - POOL_HALO
    nt = S // _POOL_TILE
    n = _POOL_TILE + POOL_HALO

    def body(dy_ref, u_ref, w_ref, sc_ref, du_ref, dw_ref, dsc_ref):
        dw_ref[...] = jnp.zeros_like(dw_ref)
        dsc_ref[...] = jnp.zeros_like(dsc_ref)

        def tile(t, c):
            r0 = pl.multiple_of(t * _POOL_TILE, _POOL_TILE)
            for g, w in enumerate(POOL_WINDOWS):
                cs = pl.ds(POOL_GROUP * g, POOL_GROUP)
                wb = w_ref[g].astype(bf16)
                xt = u_ref[pl.ds(r0, n), cs]
                pooled = (_window_sums(xt, True)[g][POOL_HALO:, :] / _pool_count(r0, w) - xt[POOL_HALO:, :]).astype(bf16)
                dy = dy_ref[pl.ds(r0, _POOL_TILE), cs]
                dsc_ref[:, cs] += jnp.sum(dy * _dot(pooled, wb), 0, keepdims=True)
                dw_ref[g] += _dot_tn(pooled, (dy * sc_ref[:, cs]).astype(bf16))
                dyh = (dy_ref[pl.ds(r0, n), cs] * sc_ref[:, cs]).astype(bf16)
                dpl = _dot_nt(dyh, wb)
                cnt = jnp.minimum(r0 + lax.broadcasted_iota(jnp.int32, (n, POOL_GROUP), 0) + 1, w).astype(f32)
                lead = _window_sums(dpl / cnt, False)[g]
                du_ref[pl.ds(r0, _POOL_TILE), cs] = lead[:_POOL_TILE, :] - dpl[:_POOL_TILE, :]
            return c
        lax.fori_loop(0, nt, tile, 0)

    vm = pl.BlockSpec(memory_space=pltpu.VMEM)
    return pl.pallas_call(
        body, name="pool_bwd", in_specs=[vm, vm, vm, vm], out_specs=[vm, vm, vm],
        out_shape=[jax.ShapeDtypeStruct((S, D_POOL), f32), jax.ShapeDtypeStruct((4, POOL_GROUP, POOL_GROUP), f32),
                   jax.ShapeDtypeStruct((1, D_POOL), f32)],
    )(dy_pad, u_pad, pool_w, pool_scale)


def _loss_head(y, target):
    S, D = y.shape
    ts = TOK_TILE

    def body(y_ref, t_ref, loss_ref, dy_ref):
        @pl.when(pl.program_id(0) == 0)
        def _():
            loss_ref[...] = jnp.zeros_like(loss_ref)

        d = y_ref[...] - t_ref[...]
        dy_ref[...] = d * (1.0 / D)
        loss_ref[...] += 0.5 * jnp.sum(jnp.sum(d * d, -1, keepdims=True) * (1.0 / D), 0, keepdims=True)

    tok = pl.BlockSpec((ts, D), lambda i: (i, 0))
    return pl.pallas_call(
        body, name="loss_head", grid=(S // ts,),
        in_specs=[tok, tok], out_specs=[_full((1, 1)), tok],
        out_shape=[jax.ShapeDtypeStruct((1, 1), f32), jax.ShapeDtypeStruct((S, D), f32)],
        compiler_params=_cp(("arbitrary",)),
    )(*_hbm(y, target))


_ADA_COLS = 768


def _ada_fwd(c_all, ada_w, ada_b_cols):
    L, D, N = ada_w.shape
    B = c_all.shape[0]

    def body(c_ref, w_ref, b_ref, out_ref):
        cv = c_ref[...]
        cond = (cv * jax.nn.sigmoid(cv)).astype(bf16)
        out_ref[0] = _dot(cond, w_ref[0].astype(bf16)) + b_ref[0]

    return pl.pallas_call(
        body, name="ada_fwd", grid=(L, N // _ADA_COLS),
        in_specs=[_full((B, D)), pl.BlockSpec((1, D, _ADA_COLS), lambda l, j: (l, 0, j)),
                  pl.BlockSpec((1, 1, _ADA_COLS), lambda l, j: (l, 0, j))],
        out_specs=pl.BlockSpec((1, B, _ADA_COLS), lambda l, j: (l, 0, j)),
        out_shape=jax.ShapeDtypeStruct((L, B, N), f32),
        compiler_params=_cp(("parallel", "parallel")),
    )(c_all, ada_w, ada_b_cols)


def _ada_wgrad(c_all_t, dmod_cols):
    D, B = c_all_t.shape
    L, _, N = dmod_cols.shape

    def body(ct_ref, dm_ref, out_ref):
        cv = ct_ref[...]
        cond = cv * jax.nn.sigmoid(cv)
        acc = cond[:, 0:1] * dm_ref[0, 0:1, :]
        for b in range(1, B):
            acc = acc + cond[:, b:b + 1] * dm_ref[0, b:b + 1, :]
        out_ref[0] = acc

    return pl.pallas_call(
        body, name="ada_wgrad", grid=(L, N // _ADA_COLS),
        in_specs=[_full((D, B)), pl.BlockSpec((1, B, _ADA_COLS), lambda l, j: (l, 0, j))],
        out_specs=pl.BlockSpec((1, D, _ADA_COLS), lambda l, j: (l, 0, j)),
        out_shape=jax.ShapeDtypeStruct((L, D, N), f32),
        compiler_params=_cp(("parallel", "parallel")),
    )(c_all_t, dmod_cols)


def _adam_math(w, g, m, v):
    m = ADAM_B1 * m + (1.0 - ADAM_B1) * g
    v = ADAM_B2 * v + (1.0 - ADAM_B2) * (g * g)
    m_hat = m / (1.0 - ADAM_B1 ** ADAM_STEP)
    v_hat = v / (1.0 - ADAM_B2 ** ADAM_STEP)
    delta = -ADAM_LR * (m_hat / (jnp.sqrt(v_hat) + ADAM_EPS) + ADAM_WD * w)
    return delta, m, v


def _adamw(w, m, v, g, row_tile, row0=0, outs=None):
    R, C = w.shape
    b0 = row0 // row_tile

    def body(w_ref, m_ref, v_ref, g_ref, _0, _1, _2, _3, g_out, d_out, m_out, v_out):
        gv = g_ref[...]
        delta, mn, vn = _adam_math(w_ref[...], gv, m_ref[...], v_ref[...])
        g_out[...] = gv
        d_out[...] = delta
        m_out[...] = mn
        v_out[...] = vn

    pspec = pl.BlockSpec((row_tile, C), lambda i: (b0 + i, 0))
    gspec = pl.BlockSpec((row_tile, C), lambda i: (i, 0))
    anyspec = pl.BlockSpec(memory_space=pl.ANY)
    shp = jax.ShapeDtypeStruct((R, C), f32)
    if outs is None:
        outs = [lax.empty((R, C), f32) for _ in range(4)]
    return pl.pallas_call(
        body, name="adamw", grid=(g.shape[0] // row_tile,),
        in_specs=[pspec] * 3 + [gspec] + [anyspec] * 4, out_specs=[pspec] * 4, out_shape=[shp] * 4,
        input_output_aliases={4: 0, 5: 1, 6: 2, 7: 3},
        compiler_params=_cp(("parallel",), 40),
    )(*_hbm(w, m, v, g, *outs))


def _pair_sum(g5s, gots, pc):
    n = len(g5s)

    def body(pc_ref, *refs):
        for own, got, out in zip(refs[:n], refs[n:2 * n], refs[2 * n:]):
            out[0, 0] = (own[0, 0, 0].astype(f32) + got[0, 0].astype(f32)).astype(bf16)

    def half(g):
        return pl.BlockSpec((1, 1) + g.shape[-2:], lambda p, pc: (p, 0, 0, 0))

    gs = pltpu.PrefetchScalarGridSpec(
        num_scalar_prefetch=1, grid=(N_CHIPS,),
        in_specs=[pl.BlockSpec((1, 1, 1) + g.shape[-2:], lambda p, pc: (p, 0, pc[1], 0, 0)) for g in g5s]
        + [half(g) for g in gots],
        out_specs=[half(g) for g in gots],
    )
    return pl.pallas_call(
        body, name="pair_sum", grid_spec=gs, out_shape=[jax.ShapeDtypeStruct(g.shape, bf16) for g in gots],
        compiler_params=_cp(("parallel",), 48),
    )(pc, *_hbm(*g5s, *gots))


_SUM_STEPS = 2


def _sum_shards(hsums, recvs, pc):
    n = len(hsums)

    def body(pc_ref, *refs):
        for own, got, out in zip(refs[:n], refs[n:2 * n], refs[2 * n:]):
            acc = own[0, 0].astype(f32)
            for j in range(3):
                acc = acc + got[j, 0].astype(f32)
            out[0, 0] = acc

    def rows(h):
        return (h.shape[2] // _SUM_STEPS, h.shape[3])

    gs = pltpu.PrefetchScalarGridSpec(
        num_scalar_prefetch=1, grid=(_SUM_STEPS,),
        in_specs=[pl.BlockSpec((1, 1) + rows(h), lambda i, pc: (pc[0], 0, i, 0)) for h in hsums]
        + [pl.BlockSpec((3, 1) + rows(h), lambda i, pc: (0, 0, i, 0)) for h in hsums],
        out_specs=[pl.BlockSpec((1, 1) + rows(h), lambda i, pc: (0, pc[1], i, 0)) for h in hsums],
    )
    return pl.pallas_call(
        body, name="sum_shards", grid_spec=gs,
        out_shape=[jax.ShapeDtypeStruct((1, 2) + h.shape[2:], f32) for h in hsums],
        compiler_params=_cp(("parallel",), 48),
    )(pc, *_hbm(*hsums, *recvs))


def _sum8(packs):
    _, R, C = packs.shape
    tr = R // 8 if R % 64 == 0 else R

    def body(p_ref, out_ref):
        acc = p_ref[0]
        for d in range(1, 8):
            acc = acc + p_ref[d]
        out_ref[...] = acc

    return pl.pallas_call(
        body, name="sum8", grid=(R // tr,),
        in_specs=[pl.BlockSpec((8, tr, C), lambda i: (0, i, 0))],
        out_specs=pl.BlockSpec((tr, C), lambda i: (i, 0)),
        out_shape=jax.ShapeDtypeStruct((R, C), f32),
        compiler_params=_cp(("parallel",)),
    )(packs)


def _allgather8(x_shard):
    m_per, n = x_shard.shape

    def body(x_ref, out_ref, send_sems, recv_sems, local_sem):
        x, y, c = lax.axis_index("x"), lax.axis_index("y"), lax.axis_index("c")
        me, sibling = (x, y, c), (x, y, 1 - c)
        chips = [(1 - x, y), (x, 1 - y), (1 - x, 1 - y)]

        def rows(px, py, pc):
            return out_ref.at[pl.ds((4 * px + 2 * py + pc) * m_per, m_per), :]

        def copy(k, block, to, src=None):
            return pltpu.make_async_remote_copy(
                src_ref=rows(*block) if src is None else src, dst_ref=rows(*block),
                send_sem=send_sems.at[k], recv_sem=recv_sems.at[k], device_id=to, device_id_type=MESH)

        mine = pltpu.make_async_copy(x_ref, rows(*me), local_sem)
        mine.start()
        first = [copy(0, me, sibling, src=x_ref)]
        first += [copy(1 + j, me, (*chip, c), src=x_ref) for j, chip in enumerate(chips)]
        for cp in first:
            cp.start()
        passed = [copy(4 + j, (*chip, c), sibling) for j, chip in enumerate(chips)]
        for j, chip in enumerate(chips):
            copy(1 + j, (*chip, c), me).wait_recv()
            passed[j].start()
        copy(0, sibling, me).wait_recv()
        for j, chip in enumerate(chips):
            copy(4 + j, (*chip, 1 - c), me).wait_recv()
        for cp in first + passed:
            cp.wait_send()
        mine.wait()

    return pl.pallas_call(
        body, name="allgather8",
        out_shape=jax.ShapeDtypeStruct((8 * m_per, n), x_shard.dtype),
        in_specs=[pl.BlockSpec(memory_space=pltpu.VMEM)],
        out_specs=pl.BlockSpec(memory_space=pltpu.VMEM),
        scratch_shapes=[pltpu.SemaphoreType.DMA((7,)), pltpu.SemaphoreType.DMA((7,)), pltpu.SemaphoreType.DMA],
        compiler_params=_cp(None, 48),
    )(x_shard)


def _other_chips():
    x, y = lax.axis_index("x"), lax.axis_index("y")
    return [(1 - x, y), (x, 1 - y), (1 - x, 1 - y)]


_HBM = pl.BlockSpec(memory_space=pltpu.HBM)
_SEM = pl.BlockSpec(memory_space=pltpu.SEMAPHORE)
_EFFECT = pltpu.SideEffectType.DATAFLOW_SIDE_EFFECTING


def _gather_copies(srcs, lands, send_sems, recv_sems):
    x, y, c = lax.axis_index("x"), lax.axis_index("y"), lax.axis_index("c")
    return [pltpu.make_async_remote_copy(
        src_ref=srcs[a].at[:, c], dst_ref=lands[a].at[2 * x + y, :, c], send_sem=send_sems.at[3 * a + j],
        recv_sem=recv_sems.at[3 * a + j], device_id=(cx, cy, c), device_id_type=MESH)
        for a in range(len(srcs)) for j, (cx, cy) in enumerate(_other_chips())]


def _gather_start(chunks, after, name):
    sizes = [len(srcs) for srcs, _ in chunks]
    flat = [t for srcs, lands in chunks for t in list(srcs) + list(lands)]
    nflat = len(flat)
    nsem = 2 * len(chunks)

    def body(*refs):
        ins, sems, token = refs[:nflat], refs[nflat + 1:nflat + 1 + nsem], refs[-1]
        off = 0
        for k, n in enumerate(sizes):
            for cp in _gather_copies(ins[off:off + n], ins[off + n:off + 2 * n], sems[2 * k], sems[2 * k + 1]):
                cp.start()
            off += 2 * n
        token[...] = jnp.zeros_like(token)

    res = pl.pallas_call(
        body, name=name,
        out_shape=[pltpu.SemaphoreType.DMA((3 * n,)) for n in sizes for _ in range(2)]
        + [pltpu.HBM(t.shape, t.dtype) for t in flat] + [jax.ShapeDtypeStruct((8, 128), f32)],
        in_specs=[_HBM] * nflat + [pl.BlockSpec(memory_space=pl.ANY)],
        out_specs=[_SEM] * nsem + [_HBM] * nflat + [pl.BlockSpec(memory_space=pltpu.VMEM)],
        input_output_aliases={i: nsem + i for i in range(nflat)},
        compiler_params=pltpu.CompilerParams(has_side_effects=_EFFECT),
    )(*[pltpu.with_memory_space_constraint(t, pltpu.HBM) for t in flat], after)
    out, off = [], nsem
    for k, n in enumerate(sizes):
        out.append((res[2 * k], res[2 * k + 1], res[off:off + n], res[off + n:off + 2 * n]))
        off += 2 * n
    return out, res[-1]


def _gather_wait(send_sems, recv_sems, srcs, lands, after, name):
    n = len(srcs)

    def body(*refs):
        for cp in _gather_copies(refs[:n], refs[n:2 * n], refs[2 * n], refs[2 * n + 1]):
            cp.wait_send()
            cp.wait_recv()

    res = pl.pallas_call(
        body, name=name,
        out_shape=[pltpu.HBM(t.shape, t.dtype) for t in list(srcs) + list(lands)],
        in_specs=[_HBM] * (2 * n) + [_SEM, _SEM] + [pl.BlockSpec(memory_space=pl.ANY)] * len(after),
        out_specs=[_HBM] * (2 * n),
        input_output_aliases={i: i for i in range(2 * n)},
        compiler_params=pltpu.CompilerParams(has_side_effects=_EFFECT),
    )(*srcs, *lands, send_sems, recv_sems, *after)
    return res[n:]


def _split_start(make_copies, arrays, nsem, name, after=()):
    n, na = len(arrays), len(after)

    def body(*refs):
        for cp in make_copies(refs[:n], refs[n + na], refs[n + na + 1]):
            cp.start()
        refs[-1][...] = jnp.zeros_like(refs[-1])

    res = pl.pallas_call(
        body, name=name,
        out_shape=[pltpu.SemaphoreType.DMA((nsem,)), pltpu.SemaphoreType.DMA((nsem,))]
        + [pltpu.HBM(t.shape, t.dtype) for t in arrays] + [jax.ShapeDtypeStruct((8, 128), f32)],
        in_specs=[_HBM] * n + [pl.BlockSpec(memory_space=pl.ANY)] * na,
        out_specs=[_SEM, _SEM] + [_HBM] * n + [pl.BlockSpec(memory_space=pltpu.VMEM)],
        input_output_aliases={i: i + 2 for i in range(n)},
        compiler_params=pltpu.CompilerParams(has_side_effects=_EFFECT),
    )(*[pltpu.with_memory_space_constraint(t, pltpu.HBM) for t in arrays], *after)
    return (res[0], res[1], res[2:2 + n]), res[-1]


def _split_wait(make_copies, send_sems, recv_sems, arrays, after, name):
    n = len(arrays)

    def body(*refs):
        for cp in make_copies(refs[:n], refs[n], refs[n + 1]):
            cp.wait_send()
            cp.wait_recv()

    return pl.pallas_call(
        body, name=name,
        out_shape=[pltpu.HBM(t.shape, t.dtype) for t in arrays],
        in_specs=[_HBM] * n + [_SEM, _SEM] + [pl.BlockSpec(memory_space=pl.ANY)] * len(after),
        out_specs=[_HBM] * n, input_output_aliases={i: i for i in range(n)},
        compiler_params=pltpu.CompilerParams(has_side_effects=_EFFECT),
    )(*arrays, send_sems, recv_sems, *after)


def _sibling():
    return lax.axis_index("x"), lax.axis_index("y"), 1 - lax.axis_index("c")


def _forward_copies(lands, send_sems, recv_sems):
    c = lax.axis_index("c")
    return [pltpu.make_async_remote_copy(
        src_ref=lands[a].at[2 * cx + cy, :, c], dst_ref=lands[a].at[2 * cx + cy, :, c], send_sem=send_sems.at[3 * a + j],
        recv_sem=recv_sems.at[3 * a + j], device_id=_sibling(), device_id_type=MESH)
        for a in range(len(lands)) for j, (cx, cy) in enumerate(_other_chips())]


def _swap_copies(fulls, send_sems, recv_sems):
    c = lax.axis_index("c")
    return [pltpu.make_async_remote_copy(src_ref=t.at[:, c], dst_ref=t.at[:, c], send_sem=send_sems.at[a],
                                         recv_sem=recv_sems.at[a], device_id=_sibling(), device_id_type=MESH)
            for a, t in enumerate(fulls)]


def _pair_copies(refs, send_sems, recv_sems):
    n = len(refs) // 2
    c = lax.axis_index("c")
    return [pltpu.make_async_remote_copy(src_ref=refs[a].at[:, :, 1 - c], dst_ref=refs[n + a], send_sem=send_sems.at[a],
                                         recv_sem=recv_sems.at[a], device_id=_sibling(), device_id_type=MESH)
            for a in range(n)]


def _scatter_copies(srcs, lands, send_sems, recv_sems):
    c = lax.axis_index("c")
    return [pltpu.make_async_remote_copy(
        src_ref=srcs[a].at[2 * cx + cy], dst_ref=lands[a].at[j], send_sem=send_sems.at[3 * a + j],
        recv_sem=recv_sems.at[3 * a + j], device_id=(cx, cy, c), device_id_type=MESH)
        for a in range(len(srcs)) for j, (cx, cy) in enumerate(_other_chips())]


def _scatter_start(hsums, name, after=()):
    n = len(hsums)
    na = len(after)

    def body(*refs):
        srcs, lands = refs[:n], refs[n:2 * n]
        send_sems, recv_sems = refs[2 * n + na], refs[2 * n + na + 1]
        for cp in _scatter_copies(srcs, lands, send_sems, recv_sems):
            cp.start()
        refs[-1][...] = jnp.zeros_like(refs[-1])

    lands = [lax.empty((3,) + g.shape[1:], g.dtype) for g in hsums]
    res = pl.pallas_call(
        body, name=name,
        out_shape=[pltpu.SemaphoreType.DMA((3 * n,)), pltpu.SemaphoreType.DMA((3 * n,))]
        + [pltpu.HBM(g.shape, g.dtype) for g in hsums] + [pltpu.HBM(g.shape, g.dtype) for g in lands]
        + [jax.ShapeDtypeStruct((8, 128), f32)],
        in_specs=[_HBM] * (2 * n) + [pl.BlockSpec(memory_space=pl.ANY)] * na,
        out_specs=[_SEM, _SEM] + [_HBM] * (2 * n) + [pl.BlockSpec(memory_space=pltpu.VMEM)],
        input_output_aliases={i: i + 2 for i in range(2 * n)},
        compiler_params=pltpu.CompilerParams(has_side_effects=_EFFECT),
    )(*[pltpu.with_memory_space_constraint(t, pltpu.HBM) for t in list(hsums) + lands], *after)
    return (res[0], res[1], res[2:2 + n], res[2 + n:2 + 2 * n]), res[-1]


def _scatter_wait(send_sems, recv_sems, srcs, lands, after, name):
    n = len(srcs)
    extra = list(after)

    def body(*refs):
        s_refs, l_refs = refs[:n], refs[n:2 * n]
        ss, rs = refs[2 * n], refs[2 * n + 1]
        for cp in _scatter_copies(s_refs, l_refs, ss, rs):
            cp.wait_send()
            cp.wait_recv()

    res = pl.pallas_call(
        body, name=name,
        out_shape=[pltpu.HBM(g.shape, g.dtype) for g in srcs] + [pltpu.HBM(g.shape, g.dtype) for g in lands],
        in_specs=[_HBM] * (2 * n) + [_SEM, _SEM] + [pl.BlockSpec(memory_space=pl.ANY)] * len(extra),
        out_specs=[_HBM] * (2 * n),
        input_output_aliases={i: i for i in range(2 * n)},
        compiler_params=pltpu.CompilerParams(has_side_effects=_EFFECT),
    )(*srcs, *lands, send_sems, recv_sems, *extra)
    return res[:n], res[n:]


def _plane_copies(src, land, send_sems, recv_sems):
    x, y, c = lax.axis_index("x"), lax.axis_index("y"), lax.axis_index("c")
    return [pltpu.make_async_remote_copy(src_ref=src, dst_ref=land.at[2 * x + y, c], send_sem=send_sems.at[j],
                                         recv_sem=recv_sems.at[j], device_id=(cx, cy, c), device_id_type=MESH)
            for j, (cx, cy) in enumerate(_other_chips())]


def _plane_start(pack, land, name):
    def body(src, lnd, send_sems, recv_sems, _s, _l, token):
        for cp in _plane_copies(src, lnd, send_sems, recv_sems):
            cp.start()
        token[...] = jnp.zeros_like(token)

    res = pl.pallas_call(
        body, name=name,
        out_shape=[pltpu.SemaphoreType.DMA((3,)), pltpu.SemaphoreType.DMA((3,)), pltpu.HBM(pack.shape, pack.dtype),
                   pltpu.HBM(land.shape, land.dtype), jax.ShapeDtypeStruct((8, 128), f32)],
        in_specs=[_HBM, _HBM], out_specs=[_SEM, _SEM, _HBM, _HBM, pl.BlockSpec(memory_space=pltpu.VMEM)],
        input_output_aliases={0: 2, 1: 3},
        compiler_params=pltpu.CompilerParams(has_side_effects=_EFFECT),
    )(pltpu.with_memory_space_constraint(pack, pltpu.HBM), pltpu.with_memory_space_constraint(land, pltpu.HBM))
    return res[:4], res[4]


def _plane_wait(send_sems, recv_sems, pack, land, after, name):
    def body(src, lnd, ss, rs, *_):
        for cp in _plane_copies(src, lnd, ss, rs):
            cp.wait_send()
            cp.wait_recv()

    return pl.pallas_call(
        body, name=name,
        out_shape=[pltpu.HBM(pack.shape, pack.dtype), pltpu.HBM(land.shape, land.dtype)],
        in_specs=[_HBM, _HBM, _SEM, _SEM] + [pl.BlockSpec(memory_space=pl.ANY)] * len(after),
        out_specs=[_HBM, _HBM], input_output_aliases={0: 0, 1: 1},
        compiler_params=pltpu.CompilerParams(has_side_effects=_EFFECT),
    )(pack, land, send_sems, recv_sems, *after)[1]


def _swap_halves(fulls):
    n = len(fulls)

    def body(*refs):
        ins, outs = refs[:n], refs[n:2 * n]
        send_sems, recv_sems = refs[2 * n:]
        c = lax.axis_index("c")
        sibling = (lax.axis_index("x"), lax.axis_index("y"), 1 - c)
        copies = []
        for a in range(n):
            cp = pltpu.make_async_remote_copy(src_ref=outs[a].at[:, c], dst_ref=outs[a].at[:, c], send_sem=send_sems.at[a],
                                              recv_sem=recv_sems.at[a], device_id=sibling, device_id_type=MESH)
            cp.start()
            copies.append(cp)
        for a, cp in enumerate(copies):
            cp.wait_send()
            theirs = outs[a].at[:, 1 - c]
            pltpu.make_async_remote_copy(src_ref=theirs, dst_ref=theirs, send_sem=send_sems.at[a], recv_sem=recv_sems.at[a],
                                         device_id=sibling, device_id_type=MESH).wait_recv()

    hbm = pl.BlockSpec(memory_space=pl.ANY)
    return pl.pallas_call(
        body, name="swap_halves",
        out_shape=[jax.ShapeDtypeStruct(p.shape, p.dtype) for p in fulls],
        in_specs=[hbm] * n, out_specs=[hbm] * n,
        input_output_aliases={a: a for a in range(n)},
        scratch_shapes=[pltpu.SemaphoreType.DMA((n,)), pltpu.SemaphoreType.DMA((n,))],
    )(*fulls)


def _to_segments(t):
    s, c = t.shape
    return t.reshape(SCAN_SEG, s // SCAN_SEG, c).transpose(1, 0, 2).reshape(s, c)


def _from_segments(t):
    s, c = t.shape
    return t.reshape(s // SCAN_SEG, SCAN_SEG, c).transpose(1, 0, 2).reshape(s, c)


def _ssm_operators(a_re, a_im, log_dt, b_re, b_im, c_re, c_im):
    lam = lax.complex(a_re, a_im)
    dt = jnp.exp(log_dt)[:, None]
    a_bar = jnp.exp(lam * dt)
    b_bar = ((a_bar - 1.0) / lam)[:, :, None] * lax.complex(b_re, b_im)
    eye = jnp.eye(N_GROUPS, dtype=f32)

    def embed_b(t):
        return (jnp.transpose(t, (0, 2, 1))[:, :, None, :] * eye[:, None, :, None]).reshape(D_SSM, D_STATE)

    def embed_c(t):
        return (jnp.transpose(t, (0, 2, 1))[:, :, None, :] * eye[:, None, :, None]).reshape(D_STATE, D_SSM)

    a2 = jnp.stack([a_bar.real.reshape(D_STATE), a_bar.imag.reshape(D_STATE)])
    return a2, embed_b(b_bar.real), embed_b(b_bar.imag), embed_c(c_re), embed_c(c_im)


def _local_step(x, target, mod, small, ffn_weights, mix_weights, grads_done, ffn_bwd_issued):
    table = jnp.asarray(_bucket_table())
    bias = small["att_bias"]
    L = DEPTH
    saved = []
    ssm_names = ("ssm_a_re", "ssm_a_im", "ssm_log_dt", "ssm_b_re", "ssm_b_im", "ssm_c_re", "ssm_c_im")
    ssm_ops_vjp = []
    for l in range(L):
        sv = {}
        m9 = mod[l]
        sv["x0"] = x
        sv["w0"] = ffn_weights(l, 0, x)
        x, sv["f0"], sv["g0"], sv["u0"], sv["h0"] = _ffn_fwd(x, m9[0:3], *sv["w0"], 0, small["ln_g"][l, 0:1], small["ln_b"][l, 0:1])
        sv["x1"] = x
        sv["w1"] = mix_weights(l, x)
        *qkv, z_rest, sv["h1"] = _mix_in_fwd(x, m9[3:6], sv["w1"][0], 0)
        S = x.shape[0]
        qkv = [t.reshape(3, S, D_ATT) for t in qkv]
        att = [_att_fwd(qkv[b], bias, b) for b in range(3)]
        y_att, lse3 = _att_merge([att[b][0].reshape(d, S // d, D_ATT) for b, d in enumerate(DILATIONS)],
                                 [att[b][1].reshape(d, S // d, _LANES) for b, d in enumerate(DILATIONS)])
        sv.update(qkv=qkv, lse=[a[1] for a in att], lse3=lse3, y_att=y_att)

        ops, ops_vjp = jax.vjp(_ssm_operators, *[small[k][l] for k in ssm_names])
        ssm_ops_vjp.append(ops_vjp)
        a2, bre, bim, cre, cim = [t[None] for t in ops]
        u_ssm = _to_segments(z_rest[:, :D_SSM])
        sr, si = _ssm_states(u_ssm, bre, bim, a2, 0)
        dskip = small["ssm_d"][l][None, :]
        glu_b = small["glu_b"][l][None, :]
        out_seg, y_seg = _ssm_out(sr, si, u_ssm, cre, cim, 0, dskip, small["glu_w"][l], glu_b)
        y_ssm = _from_segments(out_seg)
        sv.update(ssm_ops=(a2, bre, bim, cre, cim), u_ssm=u_ssm, sr=sr, si=si, y_seg=y_seg, y_ssm=y_ssm)

        u_pool = jnp.concatenate([jnp.zeros((POOL_HALO, D_POOL), f32), z_rest[:, D_SSM:]])
        y_pool = _pool_fwd(u_pool, small["pool_w"][l], small["pool_scale"][l][None, :])
        sv.update(u_pool=u_pool, y_pool=y_pool)

        x, sv["ymix"] = _mix_out_fwd(x, y_att, y_ssm, y_pool, m9[3:6], sv["w1"][1], 0, small["ln_g"][l, 1:2], small["ln_b"][l, 1:2])
        sv["x2"] = x
        sv["w2"] = ffn_weights(l, 1, x)
        x, sv["f2"], sv["g2"], sv["u2"], sv["h2"] = _ffn_fwd(x, m9[6:9], *sv["w2"], 0, small["ln_g"][l, 2:3], small["ln_b"][l, 2:3])
        saved.append(sv)

    loss, dx = _loss_head(x, target)

    dmod = [None] * L
    dln_g = [None] * L
    dln_b = [None] * L
    sg = {k: [None] * L for k in ssm_names + ("ssm_d", "glu_w", "glu_b", "pool_w", "pool_scale")}
    dbias_tot = None
    order_after = jnp.zeros((), f32)
    for l in reversed(range(L)):
        sv = saved[l]
        m9 = mod[l] + order_after

        def fresh(like):
            return [lax.empty(t.shape, bf16) for t in like]

        dx, dg, du, a, df, dm2, dlg2, dlb2 = _ffn_bwd(dx, sv["x2"], sv["f2"], sv["g2"], sv["u2"], m9[6:9], *sv["w2"], 0,
                                                     small["ln_g"][l, 2:3])
        m9 = m9 + ffn_bwd_issued(l, 1, dx)
        g_ffn1 = _ffn_wgrad(sv["h2"], dg, du, a, df, *fresh(sv["w2"]), 0)
        dxr, d_att, d_ssm, d_pool, dgate1, dlg1, dlb1, g_w_out = _mix_out_bwd(
            dx, sv["x1"], sv["ymix"], sv["y_att"], sv["y_ssm"], sv["y_pool"], m9[3:6], sv["w1"][1], 0, small["ln_g"][l, 1:2],
            fresh(sv["w1"])[1])
        S = d_att.shape[0]
        merged = _att_merge_bwd(d_att, sv["y_att"], sv["lse3"])
        dqkv, dbias = [], []
        for b, d in enumerate(DILATIONS):
            dq_b, db_b = _att_bwd(sv["qkv"][b], merged[b].reshape(S, D_ATT), sv["lse"][b], merged[3 + b].reshape(S, _LANES), bias, b)
            dqkv.append(dq_b.reshape(3, d, S // d, D_ATT))
            dbias.append(db_b)
        dbias = jnp.stack(dbias)
        dbias_tot = dbias if dbias_tot is None else dbias_tot + dbias
        d_seg = _to_segments(d_ssm)
        dskip = small["ssm_d"][l][None, :]
        glu_b = small["glu_b"][l][None, :]
        dy_seg, du_skip, dcre, dcim, dd, dglu_b, dglu_w = _ssm_out_bwd(
            d_seg, sv["y_seg"], sv["u_ssm"], sv["sr"], sv["si"], dskip, small["glu_w"][l], glu_b)
        a2, bre, bim, cre, cim = sv["ssm_ops"]
        du_seg, dbre, dbim, da2 = _ssm_states_bwd(dy_seg, du_skip, sv["u_ssm"], sv["sr"], sv["si"], cre, cim, bre, bim, a2, 0)
        for k, t in zip(ssm_names, ssm_ops_vjp[l]((da2, dbre, dbim, dcre, dcim))):
            sg[k][l] = t
        sg["ssm_d"][l] = dd[0]
        sg["glu_b"][l] = dglu_b[0]
        sg["glu_w"][l] = dglu_w
        du_ssm = _from_segments(du_seg)
        dyp = jnp.concatenate([d_pool, jnp.zeros((POOL_HALO, D_POOL), f32)])
        du_pool, dpw, dps = _pool_bwd(dyp, sv["u_pool"], small["pool_w"][l], small["pool_scale"][l][None, :])
        sg["pool_w"][l] = dpw
        sg["pool_scale"][l] = dps[0]
        d_rest = jnp.concatenate([du_ssm, du_pool], axis=1).astype(bf16)
        dx, dm1, dz = _mix_in_bwd(dqkv, d_rest, dxr, sv["x1"], m9[3:6], sv["w1"][0], 0)
        g_w_in = _mix_in_wgrad(sv["h1"], dz, fresh(sv["w1"])[0], 0)
        ffn_names = ("ffn_w_gate", "ffn_w_up", "ffn_w_down")
        m9 = m9 + grads_done(l, 1, list(zip(ffn_names, [(2 * l + 1) * FF_SHARD] * 3, g_ffn1))
                             + [("w_in", l * D_MODEL, g_w_in), ("w_out", l * 256, g_w_out)])
        dm1 = jnp.concatenate([dm1[0:2], dgate1])
        dx, dg, du, a, df, dm0, dlg0, dlb0 = _ffn_bwd(dx, sv["x0"], sv["f0"], sv["g0"], sv["u0"], m9[0:3], *sv["w0"], 0,
                                                     small["ln_g"][l, 0:1])
        issued = ffn_bwd_issued(l, 0, dx)
        g_ffn0 = _ffn_wgrad(sv["h0"], dg, du, a, df, *fresh(sv["w0"]), 0)
        order_after = grads_done(l, 0, list(zip(ffn_names, [2 * l * FF_SHARD] * 3, g_ffn0))) + issued
        dmod[l] = jnp.concatenate([dm0 + issued, dm1, dm2])
        dln_g[l] = jnp.concatenate([dlg0, dlg1, dlg2])
        dln_b[l] = jnp.concatenate([dlb0, dlb1, dlb2])

    small_grads = {k: jnp.stack(v) for k, v in sg.items()}
    small_grads["rel_bias"] = _bias_bwd(dbias_tot, table)
    small_grads["ln_g"] = jnp.stack(dln_g)
    small_grads["ln_b"] = jnp.stack(dln_b)
    return loss, dx, jnp.stack(dmod), small_grads


_TILE_ELEMS = 8 * 128


def _pack_rows(shapes):
    out, row = [], 0
    for s in shapes:
        nr = -(-int(np.prod(s)) // _TILE_ELEMS) * 8
        out.append((row, nr))
        row += nr
    return out


def _pack(arrs):
    parts = []
    for a in arrs:
        flat = a.reshape(-1).astype(f32)
        npad = -(-flat.shape[0] // _TILE_ELEMS) * _TILE_ELEMS
        parts.append(jnp.pad(flat, (0, npad - flat.shape[0])).reshape(npad // 128, 128))
    return jnp.concatenate(parts, axis=0)


def _unpack(buf, shapes):
    return [buf[row:row + nr].reshape(-1)[:int(np.prod(s))].reshape(s) for s, (row, nr) in zip(shapes, _pack_rows(shapes))]


_REPL = ("rel_bias", "ada_b", "ssm_a_re", "ssm_a_im", "ssm_log_dt", "ssm_b_re", "ssm_b_im", "ssm_c_re", "ssm_c_im",
         "ssm_d", "glu_b", "pool_w", "pool_scale")
_SMALL_SHARDED = ("ln_g", "ln_b", "glu_w")
_BIG = ("ffn_w_gate", "ffn_w_up", "ffn_w_down", "w_in", "w_out")
_ORDER = ("rel_bias", "ada_w", "ada_b", "ln_g", "ln_b", "ffn_w_gate", "ffn_w_up", "ffn_w_down", "w_in", "w_out",
          "ssm_a_re", "ssm_a_im", "ssm_log_dt", "ssm_b_re", "ssm_b_im", "ssm_c_re", "ssm_c_im", "ssm_d", "glu_w",
          "glu_b", "pool_w", "pool_scale")


def kernel(x, c, rel_bias, ada_w, ada_b, ln_g, ln_b, ffn_w_gate, ffn_w_up, ffn_w_down, w_in, w_out, ssm_a_re, ssm_a_im, ssm_log_dt, ssm_b_re, ssm_b_im, ssm_c_re, ssm_c_im, ssm_d, glu_w, glu_b, pool_w, pool_scale, loss_target, m_rel_bias, m_ada_w, m_ada_b, m_ln_g, m_ln_b, m_ffn_w_gate, m_ffn_w_up, m_ffn_w_down, m_w_in, m_w_out, m_ssm_a_re, m_ssm_a_im, m_ssm_log_dt, m_ssm_b_re, m_ssm_b_im, m_ssm_c_re, m_ssm_c_im, m_ssm_d, m_glu_w, m_glu_b, m_pool_w, m_pool_scale, v_rel_bias, v_ada_w, v_ada_b, v_ln_g, v_ln_b, v_ffn_w_gate, v_ffn_w_up, v_ffn_w_down, v_w_in, v_w_out, v_ssm_a_re, v_ssm_a_im, v_ssm_log_dt, v_ssm_b_re, v_ssm_b_im, v_ssm_c_re, v_ssm_c_im, v_ssm_d, v_glu_w, v_glu_b, v_pool_w, v_pool_scale):
    args = dict(locals())
    w = {k: args[k] for k in _ORDER}
    m = {k: args["m_" + k] for k in _ORDER}
    v = {k: args["v_" + k] for k in _ORDER}
    L, D = DEPTH, D_MODEL
    ax, ay, ac = lax.axis_index("x"), lax.axis_index("y"), lax.axis_index("c")
    p_me = 2 * ax + ay
    dev = 4 * ax + 2 * ay + ac

    transposed = ("ffn_w_gate", "ffn_w_up")
    for d in (w, m, v):
        for name in transposed:
            d[name] = jnp.swapaxes(d[name], 2, 3)

    def halves(t):
        return t.astype(bf16).reshape(1, 2, t.shape[0] // 2, t.shape[1])

    def landing(src):
        return lax.dynamic_update_slice(lax.empty((N_CHIPS,) + src.shape, bf16), src[None], (p_me, 0, 0, 0, 0))

    chunk_keys = [("ffn", 0, 0), ("mix", 0), ("ffn", 0, 1), ("ffn", 1, 0), ("mix", 1), ("ffn", 1, 1)]
    chunk_srcs = []
    for key in chunk_keys:
        if key[0] == "ffn":
            chunk_srcs.append([halves(w[name][key[1], key[2]]) for name in ("ffn_w_gate", "ffn_w_up", "ffn_w_down")])
        else:
            chunk_srcs.append([halves(w_in[key[1]]), halves(w_out[key[1]])])

    pack = _pack([c, ln_g, ln_b, glu_w])
    rows = pack.shape[0]
    allp = _allgather8(pack).reshape(8, rows, 128)
    chunks = [(srcs, [landing(t) for t in srcs]) for srcs in chunk_srcs]
    first_in_flight, first_begun = _gather_start(chunks[:1], allp, "gather_start_first")
    c_all = allp[:, :8].reshape(8, D) + first_begun[0, 0]
    by_chip = allp[0::2]

    fwd_rows = _pack_rows([c.shape, ln_g.shape, ln_b.shape, glu_w.shape])

    def sharded(part, shape, axis):
        row0, nrows = fwd_rows[part]
        t = by_chip[:, row0:row0 + nrows].reshape(N_CHIPS, -1)[:, :int(np.prod(shape))].reshape((N_CHIPS,) + shape)
        return jnp.concatenate([t[p] for p in range(N_CHIPS)], axis=axis)

    ln_g_full = sharded(1, ln_g.shape, 2)
    ln_b_full = sharded(2, ln_b.shape, 2)
    glu_w_full = sharded(3, glu_w.shape, 1)

    ncol = ada_w.shape[-1]
    ada_b_cols = lax.dynamic_slice_in_dim(ada_b, p_me * ncol, ncol, axis=1)[:, None, :]
    mod_part = _ada_fwd(c_all, ada_w, ada_b_cols)
    mrows = L * 8 * ncol // 128
    mod_pack = mod_part.reshape(mrows, 128)
    mod_land = lax.dynamic_update_slice(lax.empty((N_CHIPS, 2, mrows, 128), f32), mod_pack[None, None], (p_me, ac, 0, 0))
    mod_in_flight, _ = _plane_start(mod_pack, mod_land, "mod_start")
    att_bias = _bias_fwd(rel_bias, jnp.asarray(_bucket_table()))
    small_names = _REPL + _SMALL_SHARDED
    small_packs = [_pack([d[k] for k in small_names]) for d in (w, m, v)]
    mod_land = _plane_wait(*mod_in_flight, [att_bias] + small_packs + [t for _, lands in chunks[1:] for t in lands], "mod_wait")
    mod_all = _swap_halves([mod_land])[0].reshape(8, L, 8, ncol)
    mod_mine = lax.dynamic_index_in_dim(mod_all, dev, axis=2, keepdims=False)
    mod = jnp.concatenate([mod_mine[2 * p] for p in range(N_CHIPS)], axis=-1).reshape(L, 9, D)

    rest_in_flight, rest_begun = _gather_start(chunks[1:], mod, "gather_start_rest")
    in_flight = first_in_flight + rest_in_flight

    forwarding = {}

    def forward(k, after):
        lands = _gather_wait(*in_flight[k], [after, rest_begun], "gather_wait_%d" % k)
        forwarding[k], begun = _split_start(_forward_copies, lands, 3 * len(lands), "gather_forward_start_%d" % k)
        return begun

    def gathered(key, after):
        k = chunk_keys.index(key)
        order = [after]
        if k not in forwarding:
            order.append(forward(k, after))
        if 3 <= k + 1 < len(chunk_keys):
            order.append(forward(k + 1, after))
        lands = _split_wait(_forward_copies, *forwarding[k], order, "gather_forward_wait_%d" % k)
        return [t.reshape(N_CHIPS, 1, 2 * t.shape[3], t.shape[4]) for t in lands]

    pc = jnp.stack([p_me, ac]).astype(jnp.int32)
    groups = {}
    scattering = {}

    pairing = {}

    def start_pairs(tag, after=()):
        g5 = [g.reshape(g.shape[:2] + (2, g.shape[2] // 2, g.shape[3])) for _, _, g in groups[tag]]
        gots = [lax.empty(g.shape[:2] + g.shape[3:], bf16) for g in g5]
        pairing[tag], begun = _split_start(_pair_copies, g5 + gots, len(g5), "pair_exchange_start_%s" % tag, after)
        return begun

    def start_group(tag, after):
        arrays = _split_wait(_pair_copies, *pairing[tag], after, "pair_exchange_wait_%s" % tag)
        n = len(arrays) // 2
        hsum = _pair_sum(arrays[:n], arrays[n:], pc)
        scattering[tag], begun = _scatter_start(hsum, "scatter_start_%s" % tag)
        return begun

    def grads_done(l, s, grads):
        if l == 1:
            groups.setdefault("l1", []).extend(grads)
            return start_pairs("l1")[0, 0] if s == 0 else jnp.zeros((), f32)
        groups["l0a" if s == 1 else "l0b"] = grads
        return start_pairs("l0a")[0, 0] if s == 1 else jnp.zeros((), f32)

    swapping = {}

    def reduce_group(tag, after):
        hsum, recv = _scatter_wait(*scattering[tag], after, "scatter_wait_%s" % tag)
        full = _sum_shards(hsum, recv, pc)
        swapping[tag], begun = _split_start(_swap_copies, full, len(full), "swap_halves_start_%s" % tag)
        return [begun]

    def ffn_bwd_issued(l, s, dx):
        if l == 1:
            return jnp.zeros((), f32)
        return start_group("l1" if s == 1 else "l0a", [dx])[0, 0]

    small = {k: w[k] for k in _REPL if k != "ada_b"}
    small.update(ln_g=ln_g_full, ln_b=ln_b_full, glu_w=glu_w_full, att_bias=att_bias)
    loss_dev, grad_x, dmod, sgrads = _local_step(
        x[0], loss_target[0], mod, small, lambda l, s, after: gathered(("ffn", l, s), after),
        lambda l, after: gathered(("mix", l), after), grads_done, ffn_bwd_issued)
    loss = lax.psum(loss_dev[0, 0], ("x", "y", "c"))

    names = ("rel_bias", "ln_g", "ln_b", "ssm_a_re", "ssm_a_im", "ssm_log_dt", "ssm_b_re", "ssm_b_im", "ssm_c_re",
             "ssm_c_im", "ssm_d", "glu_w", "glu_b", "pool_w", "pool_scale")
    gpack = _pack([dmod] + [sgrads[k] for k in names])
    grows = gpack.shape[0]
    land = lax.dynamic_update_slice(lax.empty((N_CHIPS, 2, grows, 128), f32), gpack[None, None], (p_me, ac, 0, 0))
    small_in_flight, small_begun = _plane_start(gpack, land, "small_grads_start")
    l0b_begun = start_group("l0b", [start_pairs("l0b", (small_begun,))])

    out_g, out_d, out_m, out_v = {}, {}, {}, {}
    row_tile = dict(zip(_BIG, (352, 352, 352, 256, 256)))
    big = {name: None for name in _BIG}

    def update_group(tag, after):
        full = _split_wait(_swap_copies, *swapping[tag], after, "swap_halves_wait_%s" % tag)
        for (name, row0, _), g in zip(groups[tag], full):
            shp = w[name].shape
            r2 = (int(np.prod(shp[:-1])), shp[-1])
            big[name] = _adamw(w[name].reshape(r2), m[name].reshape(r2), v[name].reshape(r2), g.reshape(-1, shp[-1]),
                               row_tile[name], row0, big[name])
        return [big[name][1] for name, _, _ in groups[tag]]

    after = reduce_group("l0a", reduce_group("l1", [grad_x, l0b_begun]))
    after = update_group("l0a", update_group("l1", after))

    land = _plane_wait(*small_in_flight, after, "small_grads_wait")
    gall = _swap_halves([land])[0].reshape(8, grows, 128)
    gsum = _unpack(_sum8(gall), [(L, 9 * D)] + [sgrads[k].shape for k in names])
    red = dict(zip(("ada_b",) + names, gsum))
    red["ln_g"] = lax.dynamic_slice_in_dim(red["ln_g"], p_me * 256, 256, axis=2)
    red["ln_b"] = lax.dynamic_slice_in_dim(red["ln_b"], p_me * 256, 256, axis=2)
    red["glu_w"] = lax.dynamic_slice_in_dim(red["glu_w"], p_me * 64, 64, axis=1)

    dmod_all = gall[:, :L * 9 * D // 128].reshape(8, L, 9 * D)
    dmod_cols = jnp.transpose(lax.dynamic_slice_in_dim(dmod_all, p_me * ncol, ncol, axis=2), (1, 0, 2))
    g_ada_w = _ada_wgrad(jnp.transpose(c_all), dmod_cols)

    r2 = (L * D, ncol)
    res = _adamw(ada_w.reshape(r2), m["ada_w"].reshape(r2), v["ada_w"].reshape(r2), g_ada_w.reshape(r2), 128)
    out_g["ada_w"], out_d["ada_w"], out_m["ada_w"], out_v["ada_w"] = [t.reshape(ada_w.shape) for t in res]

    res_small = _adamw(*small_packs, _pack([red[k] for k in small_names]), small_packs[0].shape[0])
    for t, dst in zip(res_small, (out_g, out_d, out_m, out_v)):
        for k, a in zip(small_names, _unpack(t, [w[k].shape for k in small_names])):
            dst[k] = a

    update_group("l0b", reduce_group("l0b", [res_small[1], res[1]]))
    for name in _BIG:
        res = [t.reshape(w[name].shape) for t in big[name]]
        out_g[name], out_d[name], out_m[name], out_v[name] = [jnp.swapaxes(t, 2, 3) for t in res] if name in transposed else res

    return (loss, grad_x[None], *[out_g[k] for k in _ORDER], *[out_d[k] for k in _ORDER],
            *[out_m[k] for k in _ORDER], *[out_v[k] for k in _ORDER])
```

```python
import math

import numpy as np
import jax
import jax.numpy as jnp
from jax import lax
from jax.experimental import pallas as pl
from jax.experimental.pallas import tpu as pltpu

f32 = jnp.float32
bf16 = jnp.bfloat16
MESH = pl.DeviceIdType.MESH

D_MODEL = 1024
SEQ = 2048
DEPTH = 2
HEAD_DIM = 64
N_HEADS = 8
D_ATT = 512
DILATIONS = (1, 4, 16)
BLOCKS_PER_RESIDUE = (16, 4, 1)
ATT_BLOCK = 128
N_UNITS = SEQ // ATT_BLOCK
N_GROUPS = 16
SSM_STATE = 64
D_SSM = 256
D_STATE = N_GROUPS * SSM_STATE
POOL_WINDOWS = (2, 4, 8, 16)
POOL_GROUP = 64
D_POOL = 256
POOL_HALO = 16
D_FF = 2816
N_BUCKETS = 32
MAX_DISTANCE = 2048
ALPHA = (2 * DEPTH) ** 0.25
FFN_RES = 0.5
LN_EPS = 1e-5
NEG = -1e30
N_CHIPS = 4
FF_SHARD = D_FF // N_CHIPS
SCAN_SEG = 8

ADAM_LR, ADAM_B1, ADAM_B2, ADAM_EPS, ADAM_WD, ADAM_STEP = 0.001, 0.9, 0.999, 1e-08, 0.01, 10

TOK_TILE = 512


def _cp(dims=None, vmem_mb=None):
    kw = {}
    if dims is not None:
        kw["dimension_semantics"] = dims
    if vmem_mb is not None:
        kw["vmem_limit_bytes"] = vmem_mb << 20
    return pltpu.CompilerParams(**kw)


def _dot(a, b):
    return jnp.dot(a, b, preferred_element_type=f32)


def _dot_nt(a, b):
    return lax.dot_general(a, b, (((1,), (1,)), ((), ())), preferred_element_type=f32)


def _dot_tn(a, b):
    return lax.dot_general(a, b, (((0,), (0,)), ((), ())), preferred_element_type=f32)


def _ln_stats(v):
    mu = jnp.mean(v, -1, keepdims=True)
    d = v - mu
    var = jnp.mean(d * d, -1, keepdims=True)
    rstd = lax.rsqrt(var + LN_EPS)
    return d * rstd, rstd


def _ln_bwd(dxh, xh, rstd):
    return rstd * (dxh - jnp.mean(dxh, -1, keepdims=True) - xh * jnp.mean(dxh * xh, -1, keepdims=True))


_GELU_C = math.sqrt(2.0 / math.pi)


def _gelu(y):
    return 0.5 * y * (1.0 + jnp.tanh(_GELU_C * (y + 0.044715 * y * y * y)))


def _gelu_grad(y):
    t = jnp.tanh(_GELU_C * (y + 0.044715 * y * y * y))
    return 0.5 * (1.0 + t) + 0.5 * y * (1.0 - t * t) * (_GELU_C * (1.0 + 3 * 0.044715 * y * y))


def _full(shape):
    return pl.BlockSpec(shape, lambda *_: (0,) * len(shape))


def _hbm(*args):
    return [pltpu.with_memory_space_constraint(a, pltpu.HBM) if getattr(a, "ndim", 0) >= 2 else a for a in args]


def _ffn_fwd(x, mod3, wg, wu, wd, ls, lng, lnb):
    S, D = x.shape
    Fs = wg.shape[-2]
    ts = 2 * TOK_TILE

    def body(x_ref, mod_ref, wg_ref, wu_ref, wd_ref, lng_ref, lnb_ref, xo_ref, f_ref, g_ref, u_ref, h_ref, acc_sc):
        j = pl.program_id(1)

        @pl.when(j == 0)
        def _():
            xh, _ = _ln_stats(x_ref[...])
            h_ref[...] = (xh * (1.0 + mod_ref[1:2, :]) + mod_ref[0:1, :]).astype(bf16)
            acc_sc[...] = jnp.zeros_like(acc_sc)

        h = h_ref[...]
        g = _dot_nt(h, wg_ref[0, 0])
        u = _dot_nt(h, wu_ref[0, 0])
        g_ref[0] = g.astype(bf16)
        u_ref[0] = u.astype(bf16)
        a = (g * jax.nn.sigmoid(g) * u).astype(bf16)
        acc_sc[...] += _dot(a, wd_ref[0, 0])

        @pl.when(j == N_CHIPS - 1)
        def _():
            f = acc_sc[...]
            f_ref[...] = f
            r = ALPHA * x_ref[...] + (FFN_RES * mod_ref[2:3, :]) * f
            rh, _ = _ln_stats(r)
            xo_ref[...] = rh * lng_ref[...] + lnb_ref[...]

    tok = pl.BlockSpec((ts, D), lambda i, j: (i, 0))
    wrow = pl.BlockSpec((1, 1, Fs, D), lambda i, j: (j, ls, 0, 0))
    hid = pl.BlockSpec((1, ts, Fs), lambda i, j: (j, i, 0))
    return pl.pallas_call(
        body, name="ffn_fwd", grid=(S // ts, N_CHIPS),
        in_specs=[tok, _full((3, D)), wrow, wrow, wrow, _full((1, D)), _full((1, D))],
        out_specs=[tok, tok, hid, hid, tok],
        out_shape=[jax.ShapeDtypeStruct((S, D), f32), jax.ShapeDtypeStruct((S, D), f32),
                   jax.ShapeDtypeStruct((N_CHIPS, S, Fs), bf16), jax.ShapeDtypeStruct((N_CHIPS, S, Fs), bf16),
                   jax.ShapeDtypeStruct((S, D), bf16)],
        scratch_shapes=[pltpu.VMEM((ts, D), f32)],
        compiler_params=_cp(("parallel", "arbitrary"), 56),
    )(*_hbm(x, mod3, wg, wu, wd, lng, lnb))


def _ffn_bwd(dxo, x, f, g, u, mod3, wg, wu, wd, ls, lng):
    S, D = x.shape
    Fs = wg.shape[-2]
    ts = TOK_TILE

    def body(dxo_ref, x_ref, f_ref, g_ref, u_ref, mod_ref, wg_ref, wu_ref, wd_ref, lng_ref,
             dx_ref, dg_ref, du_ref, a_ref, df_ref, dmod_ref, dlng_ref, dlnb_ref,
             dr_sc, df_sc, acc_sc):
        i = pl.program_id(0)
        j = pl.program_id(1)

        @pl.when((i == 0) & (j == 0))
        def _():
            dmod_ref[...] = jnp.zeros_like(dmod_ref)
            dlng_ref[...] = jnp.zeros_like(dlng_ref)
            dlnb_ref[...] = jnp.zeros_like(dlnb_ref)

        @pl.when(j == 0)
        def _():
            xv = x_ref[...]
            fv = f_ref[...]
            gate = mod_ref[2:3, :]
            rh, rstd = _ln_stats(ALPHA * xv + (FFN_RES * gate) * fv)
            dy = dxo_ref[...]
            dlng_ref[...] += jnp.sum(dy * rh, 0, keepdims=True)
            dlnb_ref[...] += jnp.sum(dy, 0, keepdims=True)
            dr = _ln_bwd(dy * lng_ref[...], rh, rstd)
            dr_sc[...] = dr
            dmod_ref[2:3, :] += jnp.sum(FFN_RES * dr * fv, 0, keepdims=True)
            df = ((FFN_RES * gate) * dr).astype(bf16)
            df_sc[...] = df
            df_ref[...] = df
            acc_sc[...] = jnp.zeros_like(acc_sc)

        da = _dot_nt(df_sc[...], wd_ref[0, 0])
        gv = g_ref[0].astype(f32)
        uv = u_ref[0].astype(f32)
        sg = jax.nn.sigmoid(gv)
        si = gv * sg
        a_ref[0] = (si * uv).astype(bf16)
        dgv = (da * uv * (sg * (1.0 + gv * (1.0 - sg)))).astype(bf16)
        duv = (da * si).astype(bf16)
        dg_ref[0] = dgv
        du_ref[0] = duv
        acc_sc[...] += _dot(dgv, wg_ref[0, 0]) + _dot(duv, wu_ref[0, 0])

        @pl.when(j == N_CHIPS - 1)
        def _():
            dh = acc_sc[...]
            xh, rstd0 = _ln_stats(x_ref[...])
            dmod_ref[0:1, :] += jnp.sum(dh, 0, keepdims=True)
            dmod_ref[1:2, :] += jnp.sum(dh * xh, 0, keepdims=True)
            dx_ref[...] = _ln_bwd(dh * (1.0 + mod_ref[1:2, :]), xh, rstd0) + ALPHA * dr_sc[...]

    tok = pl.BlockSpec((ts, D), lambda i, j: (i, 0))
    wrow = pl.BlockSpec((1, 1, Fs, D), lambda i, j: (j, ls, 0, 0))
    hid = pl.BlockSpec((1, ts, Fs), lambda i, j: (j, i, 0))
    hid_shape = jax.ShapeDtypeStruct((N_CHIPS, S, Fs), bf16)
    return pl.pallas_call(
        body, name="ffn_bwd", grid=(S // ts, N_CHIPS),
        in_specs=[tok, tok, tok, hid, hid, _full((3, D)), wrow, wrow, wrow, _full((1, D))],
        out_specs=[tok, hid, hid, hid, tok, _full((3, D)), _full((1, D)), _full((1, D))],
        out_shape=[jax.ShapeDtypeStruct((S, D), f32), hid_shape, hid_shape, hid_shape,
                   jax.ShapeDtypeStruct((S, D), bf16),
                   jax.ShapeDtypeStruct((3, D), f32), jax.ShapeDtypeStruct((1, D), f32), jax.ShapeDtypeStruct((1, D), f32)],
        scratch_shapes=[pltpu.VMEM((ts, D), f32), pltpu.VMEM((ts, D), bf16), pltpu.VMEM((ts, D), f32)],
        compiler_params=_cp(("arbitrary", "arbitrary"), 56),
    )(*_hbm(dxo, x, f, g, u, mod3, wg, wu, wd, lng))


def _ffn_wgrad(h, dg, du, a, df, gwg, gwu, gwd, ls):
    S, D = h.shape
    Fs = dg.shape[-1]
    tk = 2 * TOK_TILE
    nk = S // tk

    def body(h_ref, dg_ref, du_ref, a_ref, df_ref, _g0, _g1, _g2, gwg_ref, gwu_ref, gwd_ref, ag_sc, au_sc, ad_sc):
        k = pl.program_id(1)

        @pl.when(k == 0)
        def _():
            ag_sc[...] = jnp.zeros_like(ag_sc)
            au_sc[...] = jnp.zeros_like(au_sc)
            ad_sc[...] = jnp.zeros_like(ad_sc)

        hv = h_ref[...]
        ag_sc[...] += _dot_tn(dg_ref[0], hv)
        au_sc[...] += _dot_tn(du_ref[0], hv)
        ad_sc[...] += _dot_tn(a_ref[0], df_ref[...])

        @pl.when(k == nk - 1)
        def _():
            gwg_ref[0, 0] = ag_sc[...].astype(bf16)
            gwu_ref[0, 0] = au_sc[...].astype(bf16)
            gwd_ref[0, 0] = ad_sc[...].astype(bf16)

    tok = pl.BlockSpec((tk, D), lambda p, k: (k, 0))
    hid = pl.BlockSpec((1, tk, Fs), lambda p, k: (p, k, 0))
    anyspec = pl.BlockSpec(memory_space=pl.ANY)
    orow = pl.BlockSpec((1, 1, Fs, D), lambda p, k: (p, ls, 0, 0))
    return pl.pallas_call(
        body, name="ffn_wgrad", grid=(N_CHIPS, nk),
        in_specs=[tok, hid, hid, hid, tok, anyspec, anyspec, anyspec],
        out_specs=[orow, orow, orow],
        out_shape=[jax.ShapeDtypeStruct(gwg.shape, bf16), jax.ShapeDtypeStruct(gwu.shape, bf16),
                   jax.ShapeDtypeStruct(gwd.shape, bf16)],
        scratch_shapes=[pltpu.VMEM((Fs, D), f32), pltpu.VMEM((Fs, D), f32), pltpu.VMEM((Fs, D), f32)],
        input_output_aliases={5: 0, 6: 1, 7: 2},
        compiler_params=_cp(("parallel", "arbitrary"), 48),
    )(*_hbm(h, dg, du, a, df, gwg, gwu, gwd))


_LANES = 128
_QKV_BLOCKS = D_ATT // _LANES


def _res_spec(lead, d, width, index):
    return pl.BlockSpec((lead, d, TOK_TILE // d, width), index)


def _res_spec3(d, width):
    return pl.BlockSpec((d, TOK_TILE // d, width), lambda i: (0, i, 0))


def _rows_to_residues(tile_bufs, d, put):
    for r in range(d):
        for cb, buf in enumerate(tile_bufs):
            put(r, cb, buf[pl.ds(r, TOK_TILE // d, stride=d), :])


def _residues_to_rows(tile_bufs, d, get):
    for r in range(d):
        for cb, buf in enumerate(tile_bufs):
            buf[pl.ds(r, TOK_TILE // d, stride=d), :] = get(r, cb)


def _mix_in_fwd(x, mod3, w_in, l):
    S, D = x.shape
    N = w_in.shape[-1]
    ts = TOK_TILE

    def body(x_ref, mod_ref, w_ref, o1_ref, o4_ref, o16_ref, zr_ref, h_ref, *bufs):
        j = pl.program_id(1)

        @pl.when(j == 0)
        def _():
            xh, _ = _ln_stats(x_ref[...])
            h_ref[...] = (xh * (1.0 + mod_ref[1:2, :]) + mod_ref[0:1, :]).astype(bf16)

        z = _dot(h_ref[...], w_ref[0, 0])

        @pl.when(j == N_CHIPS - 1)
        def _():
            zr_ref[...] = z

        @pl.when(j < N_CHIPS - 1)
        def _():
            zz = z * jnp.where(j == 0, HEAD_DIM ** -0.5, 1.0)
            o1_ref[j, 0] = zz.astype(bf16)
            for cb, buf in enumerate(bufs):
                buf[...] = zz[:, _LANES * cb:_LANES * (cb + 1)]
            for d, o_ref in zip(DILATIONS[1:], (o4_ref, o16_ref)):
                def put(r, cb, piece, o_ref=o_ref):
                    o_ref[j, r, :, _LANES * cb:_LANES * (cb + 1)] = piece.astype(bf16)
                _rows_to_residues(bufs, d, put)

    tok = pl.BlockSpec((ts, D), lambda i, j: (i, 0))
    res = [_res_spec(3, d, N, lambda i, j: (0, 0, i, 0)) for d in DILATIONS]
    return pl.pallas_call(
        body, name="mix_in_fwd", grid=(S // ts, N_CHIPS),
        in_specs=[tok, _full((3, D)), pl.BlockSpec((1, 1, D, N), lambda i, j: (j, l, 0, 0))],
        out_specs=res + [pl.BlockSpec((ts, N), lambda i, j: (i, 0)), tok],
        out_shape=[jax.ShapeDtypeStruct((3, d, S // d, N), bf16) for d in DILATIONS]
        + [jax.ShapeDtypeStruct((S, N), f32), jax.ShapeDtypeStruct((S, D), bf16)],
        scratch_shapes=[pltpu.VMEM((ts, _LANES), f32)] * _QKV_BLOCKS,
        compiler_params=_cp(("parallel", "arbitrary"), 40),
    )(*_hbm(x, mod3, w_in))


def _mix_in_bwd(dqkv, d_rest, dx_res, x, mod3, w_in, l):
    S, D = x.shape
    N = w_in.shape[-1]
    ts = TOK_TILE

    def body(d1_ref, d4_ref, d16_ref, dr_ref, dxr_ref, x_ref, mod_ref, w_ref, dx_ref, dmod_ref, dz_ref, acc_sc, *bufs):
        i = pl.program_id(0)
        j = pl.program_id(1)

        @pl.when((i == 0) & (j == 0))
        def _():
            dmod_ref[...] = jnp.zeros_like(dmod_ref)

        @pl.when(j == 0)
        def _():
            acc_sc[...] = jnp.zeros_like(acc_sc)

        @pl.when(j == N_CHIPS - 1)
        def _():
            dz_ref[0] = dr_ref[...]

        @pl.when(j < N_CHIPS - 1)
        def _():
            for d, d_ref, tile_bufs in ((4, d4_ref, bufs[:_QKV_BLOCKS]), (16, d16_ref, bufs[_QKV_BLOCKS:])):
                _residues_to_rows(tile_bufs, d, lambda r, cb, d_ref=d_ref: d_ref[0, r, :, _LANES * cb:_LANES * (cb + 1)].astype(f32))
            for cb in range(_QKV_BLOCKS):
                cols = slice(_LANES * cb, _LANES * (cb + 1))
                dz_ref[0, :, cols] = (d1_ref[0, 0, :, cols].astype(f32) + bufs[cb][...] + bufs[_QKV_BLOCKS + cb][...]).astype(bf16)

        acc_sc[...] += _dot_nt(dz_ref[0], w_ref[0, 0])

        @pl.when(j == N_CHIPS - 1)
        def _():
            dh = acc_sc[...]
            xh, rstd0 = _ln_stats(x_ref[...])
            dmod_ref[0:1, :] += jnp.sum(dh, 0, keepdims=True)
            dmod_ref[1:2, :] += jnp.sum(dh * xh, 0, keepdims=True)
            dx_ref[...] = _ln_bwd(dh * (1.0 + mod_ref[1:2, :]), xh, rstd0) + dxr_ref[...]

    tok = pl.BlockSpec((ts, D), lambda i, j: (i, 0))
    res = [_res_spec(1, d, N, lambda i, j: (jnp.minimum(j, 2), 0, i, 0)) for d in DILATIONS]
    return pl.pallas_call(
        body, name="mix_in_bwd", grid=(S // ts, N_CHIPS),
        in_specs=res + [pl.BlockSpec((ts, N), lambda i, j: (i, 0)), tok, tok, _full((3, D)),
                        pl.BlockSpec((1, 1, D, N), lambda i, j: (j, l, 0, 0))],
        out_specs=[tok, _full((3, D)), pl.BlockSpec((1, ts, N), lambda i, j: (j, i, 0))],
        out_shape=[jax.ShapeDtypeStruct((S, D), f32), jax.ShapeDtypeStruct((3, D), f32),
                   jax.ShapeDtypeStruct((N_CHIPS, S, N), bf16)],
        scratch_shapes=[pltpu.VMEM((ts, D), f32)] + [pltpu.VMEM((ts, _LANES), f32)] * (2 * _QKV_BLOCKS),
        compiler_params=_cp(("arbitrary", "arbitrary"), 40),
    )(*_hbm(*dqkv, d_rest, dx_res, x, mod3, w_in))


def _mix_in_wgrad(h, dz, gw, l):
    S, D = h.shape
    N = dz.shape[-1]
    tk = 2 * TOK_TILE
    nk = S // tk

    def body(h_ref, dz_ref, _g, gw_ref, acc_sc):
        k = pl.program_id(1)

        @pl.when(k == 0)
        def _():
            acc_sc[...] = jnp.zeros_like(acc_sc)

        acc_sc[...] += _dot_tn(h_ref[...], dz_ref[0])

        @pl.when(k == nk - 1)
        def _():
            gw_ref[0, 0] = acc_sc[...].astype(bf16)

    return pl.pallas_call(
        body, name="mix_in_wgrad", grid=(N_CHIPS, nk),
        in_specs=[pl.BlockSpec((tk, D), lambda p, k: (k, 0)), pl.BlockSpec((1, tk, N), lambda p, k: (p, k, 0)),
                  pl.BlockSpec(memory_space=pl.ANY)],
        out_specs=pl.BlockSpec((1, 1, D, N), lambda p, k: (p, l, 0, 0)),
        out_shape=jax.ShapeDtypeStruct(gw.shape, bf16),
        scratch_shapes=[pltpu.VMEM((D, N), f32)],
        input_output_aliases={2: 0},
        compiler_params=_cp(("parallel", "arbitrary"), 40),
    )(*_hbm(h, dz, gw))


def _mix_out_fwd(x, y_att, y_ssm, y_pool, mod3, w_out, l, lng, lnb):
    S, D = x.shape
    ts = TOK_TILE

    def body(x_ref, ya_ref, ys_ref, yp_ref, mod_ref, w_ref, lng_ref, lnb_ref, xo_ref, y_ref):
        ya = ya_ref[...].astype(bf16)
        y = (_dot(ya[:, 0:256], w_ref[0, 0]) + _dot(ya[:, 256:512], w_ref[1, 0])
             + _dot(ys_ref[...].astype(bf16), w_ref[2, 0]) + _dot(yp_ref[...].astype(bf16), w_ref[3, 0]))
        y_ref[...] = y
        rh, _ = _ln_stats(ALPHA * x_ref[...] + mod_ref[2:3, :] * y)
        xo_ref[...] = rh * lng_ref[...] + lnb_ref[...]

    tok = pl.BlockSpec((ts, D), lambda i: (i, 0))
    return pl.pallas_call(
        body, name="mix_out_fwd", grid=(S // ts,),
        in_specs=[tok, pl.BlockSpec((ts, D_ATT), lambda i: (i, 0)), pl.BlockSpec((ts, D_SSM), lambda i: (i, 0)),
                  pl.BlockSpec((ts, D_POOL), lambda i: (i, 0)), _full((3, D)),
                  pl.BlockSpec((N_CHIPS, 1, 256, D), lambda i: (0, l, 0, 0)), _full((1, D)), _full((1, D))],
        out_specs=[tok, tok],
        out_shape=[jax.ShapeDtypeStruct((S, D), f32), jax.ShapeDtypeStruct((S, D), f32)],
        compiler_params=_cp(("parallel",), 40),
    )(*_hbm(x, y_att, y_ssm, y_pool, mod3, w_out, lng, lnb))


def _mix_out_bwd(dxo, x, y, y_att, y_ssm, y_pool, mod3, w_out, l, lng, gw_out):
    S, D = x.shape
    ts = TOK_TILE
    nt = S // ts

    def body(dxo_ref, x_ref, y_ref, ya_ref, ys_ref, yp_ref, mod_ref, w_ref, lng_ref, _g,
             dxr_ref, da_ref, ds_ref, dp_ref, dgate_ref, dlng_ref, dlnb_ref, gw_ref, acc_sc):
        i = pl.program_id(0)

        @pl.when(i == 0)
        def _():
            dgate_ref[...] = jnp.zeros_like(dgate_ref)
            dlng_ref[...] = jnp.zeros_like(dlng_ref)
            dlnb_ref[...] = jnp.zeros_like(dlnb_ref)
            acc_sc[...] = jnp.zeros_like(acc_sc)

        gate = mod_ref[2:3, :]
        yv = y_ref[...]
        rh, rstd = _ln_stats(ALPHA * x_ref[...] + gate * yv)
        dy_out = dxo_ref[...]
        dlng_ref[...] += jnp.sum(dy_out * rh, 0, keepdims=True)
        dlnb_ref[...] += jnp.sum(dy_out, 0, keepdims=True)
        dr = _ln_bwd(dy_out * lng_ref[...], rh, rstd)
        dxr_ref[...] = ALPHA * dr
        dgate_ref[...] += jnp.sum(dr * yv, 0, keepdims=True)
        dy = (gate * dr).astype(bf16)
        da_ref[:, 0:256] = _dot_nt(dy, w_ref[0, 0])
        da_ref[:, 256:512] = _dot_nt(dy, w_ref[1, 0])
        ds_ref[...] = _dot_nt(dy, w_ref[2, 0])
        dp_ref[...] = _dot_nt(dy, w_ref[3, 0])
        ya = ya_ref[...].astype(bf16)
        acc_sc[0] += _dot_tn(ya[:, 0:256], dy)
        acc_sc[1] += _dot_tn(ya[:, 256:512], dy)
        acc_sc[2] += _dot_tn(ys_ref[...].astype(bf16), dy)
        acc_sc[3] += _dot_tn(yp_ref[...].astype(bf16), dy)

        @pl.when(i == nt - 1)
        def _():
            gw_ref[:, 0] = acc_sc[...].astype(bf16)

    tok = pl.BlockSpec((ts, D), lambda i: (i, 0))
    t512 = pl.BlockSpec((ts, D_ATT), lambda i: (i, 0))
    t256 = pl.BlockSpec((ts, 256), lambda i: (i, 0))
    wspec = pl.BlockSpec((N_CHIPS, 1, 256, D), lambda i: (0, l, 0, 0))
    return pl.pallas_call(
        body, name="mix_out_bwd", grid=(nt,),
        in_specs=[tok, tok, tok, t512, t256, t256, _full((3, D)), wspec, _full((1, D)), pl.BlockSpec(memory_space=pl.ANY)],
        out_specs=[tok, t512, t256, t256, _full((1, D)), _full((1, D)), _full((1, D)), wspec],
        out_shape=[jax.ShapeDtypeStruct((S, D), f32), jax.ShapeDtypeStruct((S, D_ATT), f32),
                   jax.ShapeDtypeStruct((S, D_SSM), f32), jax.ShapeDtypeStruct((S, D_POOL), f32),
                   jax.ShapeDtypeStruct((1, D), f32), jax.ShapeDtypeStruct((1, D), f32), jax.ShapeDtypeStruct((1, D), f32),
                   jax.ShapeDtypeStruct(gw_out.shape, bf16)],
        scratch_shapes=[pltpu.VMEM((N_CHIPS, 256, D), f32)],
        input_output_aliases={9: 7},
        compiler_params=_cp(("arbitrary",), 48),
    )(*_hbm(dxo, x, y, y_att, y_ssm, y_pool, mod3, w_out, lng, gw_out))


def _t5_bucket(dist):
    max_exact = N_BUCKETS // 2
    d = np.maximum(dist, 1).astype(np.float32)
    large = max_exact + (np.log(d / max_exact) / math.log(MAX_DISTANCE / max_exact)
                         * (N_BUCKETS - max_exact)).astype(np.int32)
    large = np.minimum(large, N_BUCKETS - 1)
    return np.where(dist < max_exact, dist, large).astype(np.int32)


def _bucket_table():
    q = ATT_BLOCK
    i = np.arange(q)[:, None]
    j = np.arange(2 * q)[None, :]
    r = i + q - j
    in_band = (r >= 0) & (r <= q)
    tabs = [np.where(in_band, _t5_bucket(np.clip(r, 0, None) * d), -1) for d in DILATIONS]
    return np.stack(tabs).astype(np.int32)


def _bias_fwd(rel_bias, table):
    def body(rb_ref, tab_ref, out_ref):
        for b in range(3):
            tb = tab_ref[b]
            for h in range(N_HEADS):
                def pick(k, acc):
                    return jnp.where(tb == k, rb_ref[k, h], acc)
                out_ref[b, h] = lax.fori_loop(0, N_BUCKETS, pick, jnp.where(tb < 0, NEG, 0.0).astype(f32))

    return pl.pallas_call(
        body, name="bias_fwd",
        in_specs=[pl.BlockSpec(memory_space=pltpu.SMEM), pl.BlockSpec(memory_space=pltpu.VMEM)],
        out_specs=pl.BlockSpec(memory_space=pltpu.VMEM),
        out_shape=jax.ShapeDtypeStruct((3, N_HEADS, ATT_BLOCK, 2 * ATT_BLOCK), f32),
    )(rel_bias, table)


def _bias_bwd(dbias, table):
    def body(db_ref, tab_ref, out_ref):
        def per_bucket(k, c):
            for h in range(N_HEADS):
                tot = jnp.zeros((), f32)
                for b in range(3):
                    tot = tot + jnp.sum(jnp.where(tab_ref[b] == k, db_ref[b, h], 0.0))
                out_ref[k, h] = tot
            return c
        lax.fori_loop(0, N_BUCKETS, per_bucket, 0)

    return pl.pallas_call(
        body, name="bias_bwd",
        in_specs=[pl.BlockSpec(memory_space=pltpu.VMEM), pl.BlockSpec(memory_space=pltpu.VMEM)],
        out_specs=pl.BlockSpec(memory_space=pltpu.SMEM),
        out_shape=jax.ShapeDtypeStruct((N_BUCKETS, N_HEADS), f32),
    )(dbias, table)


def _att_unit(u, nbr):
    rows = pl.ds(pl.multiple_of(u * ATT_BLOCK, ATT_BLOCK), ATT_BLOCK)
    prev = pl.ds(pl.multiple_of(jnp.maximum(u - 1, 0) * ATT_BLOCK, ATT_BLOCK), ATT_BLOCK)
    return rows, prev, (u % nbr) != 0


_N_PAIRS = N_HEADS // 2


def _pair_rows(t):
    lane = lax.broadcasted_iota(jnp.int32, t.shape, 1)
    zero = jnp.zeros_like(t)
    return jnp.concatenate([jnp.where(lane < HEAD_DIM, t, zero), jnp.where(lane >= HEAD_DIM, t, zero)], axis=0)


def _pair_cols(big):
    lane = lax.broadcasted_iota(jnp.int32, (ATT_BLOCK, _LANES), 1)
    return jnp.where(lane < HEAD_DIM, big[:ATT_BLOCK], big[ATT_BLOCK:])


def _pair_column(ref, rows, hp):
    t = ref[rows, :]
    return jnp.concatenate([t[:, 2 * hp:2 * hp + 1], t[:, 2 * hp + 1:2 * hp + 2]], axis=0)


def _pair_band(ref, rows, prev, nbr, hp):
    lanes = pl.ds(_LANES * hp, _LANES)
    cur = ref[0, rows, lanes]
    return cur if nbr == 1 else jnp.concatenate([ref[0, prev, lanes], cur], axis=0)


def _pair_scores(q_ref, k_ref, b_ref, rows, prev, valid_prev, nbr, hp):
    qbd = _pair_rows(q_ref[0, rows, pl.ds(_LANES * hp, _LANES)])
    kb = _pair_band(k_ref, rows, prev, nbr, hp)
    bias = b_ref[0, 2 * hp:2 * hp + 2].reshape(2 * ATT_BLOCK, 2 * ATT_BLOCK)
    if nbr == 1:
        return qbd, kb, _dot_nt(qbd, kb) + bias[:, ATT_BLOCK:]
    s = _dot_nt(qbd, kb) + bias
    col = lax.broadcasted_iota(jnp.int32, s.shape, 1)
    return qbd, kb, jnp.where((col >= ATT_BLOCK) | valid_prev, s, NEG)


def _qkv_specs(S, branch):
    return ([pl.BlockSpec((1, S, D_ATT), lambda i, t=t: (t, 0, 0)) for t in range(3)],
            pl.BlockSpec((1, N_HEADS, ATT_BLOCK, 2 * ATT_BLOCK), lambda i: (branch, 0, 0, 0)))


def _att_fwd(qkv, bias, branch):
    S = qkv.shape[1]
    nbr = BLOCKS_PER_RESIDUE[branch]

    def body(q_ref, k_ref, v_ref, b_ref, o_ref, lse_ref):
        lse_ref[...] = jnp.zeros_like(lse_ref)

        def unit(u, c):
            rows, prev, valid_prev = _att_unit(u, nbr)
            for hp in range(_N_PAIRS):
                _, _, s = _pair_scores(q_ref, k_ref, b_ref, rows, prev, valid_prev, nbr, hp)
                m = jnp.max(s, -1, keepdims=True)
                p = jnp.exp(s - m)
                den = jnp.sum(p, -1, keepdims=True)
                big = _dot(p.astype(bf16), _pair_band(v_ref, rows, prev, nbr, hp))
                o_ref[rows, pl.ds(_LANES * hp, _LANES)] = _pair_cols(big / den)
                lse = m + jnp.log(den)
                lse_ref[rows, pl.ds(2 * hp, 1)] = lse[:ATT_BLOCK]
                lse_ref[rows, pl.ds(2 * hp + 1, 1)] = lse[ATT_BLOCK:]
            return c

        lax.fori_loop(0, N_UNITS, unit, 0)

    qkv_specs, bspec = _qkv_specs(S, branch)
    return pl.pallas_call(
        body, name="att_fwd", grid=(1,),
        in_specs=qkv_specs + [bspec],
        out_specs=[pl.BlockSpec((S, D_ATT), lambda i: (0, 0)), pl.BlockSpec((S, _LANES), lambda i: (0, 0))],
        out_shape=[jax.ShapeDtypeStruct((S, D_ATT), f32), jax.ShapeDtypeStruct((S, _LANES), f32)],
        compiler_params=_cp(("arbitrary",), 40),
    )(*_hbm(qkv, qkv, qkv, bias))


def _att_bwd(qkv, do, lse, crow, bias, branch):
    S = qkv.shape[1]
    nbr = BLOCKS_PER_RESIDUE[branch]

    def body(q_ref, k_ref, v_ref, do_ref, lse_ref, c_ref, b_ref, dqkv_ref, db_ref, dk_sc, dv_sc):
        dk_sc[...] = jnp.zeros_like(dk_sc)
        dv_sc[...] = jnp.zeros_like(dv_sc)
        db_ref[...] = jnp.zeros_like(db_ref)

        def unit(u, c):
            rows, prev, valid_prev = _att_unit(u, nbr)
            for hp in range(_N_PAIRS):
                lanes = pl.ds(_LANES * hp, _LANES)
                qbd, kb, s = _pair_scores(q_ref, k_ref, b_ref, rows, prev, valid_prev, nbr, hp)
                p = jnp.exp(s - _pair_column(lse_ref, rows, hp))
                dobd = _pair_rows(do_ref[rows, lanes])
                ds = p * (_dot_nt(dobd, _pair_band(v_ref, rows, prev, nbr, hp)) - _pair_column(c_ref, rows, hp))
                if nbr == 1:
                    db_ref[2 * hp:2 * hp + 2, :, ATT_BLOCK:] += ds.reshape(2, ATT_BLOCK, ATT_BLOCK)
                else:
                    db_ref[2 * hp:2 * hp + 2] += ds.reshape(2, ATT_BLOCK, 2 * ATT_BLOCK)
                dsb = ds.astype(bf16)
                dqkv_ref[0, rows, lanes] = (HEAD_DIM ** -0.5 * _pair_cols(_dot(dsb, kb))).astype(bf16)
                dkb = _dot_tn(dsb, qbd)
                dvb = _dot_tn(p.astype(bf16), dobd)
                if nbr == 1:
                    dk_sc[rows, lanes] += dkb
                    dv_sc[rows, lanes] += dvb
                else:
                    dk_sc[prev, lanes] += dkb[:ATT_BLOCK]
                    dv_sc[prev, lanes] += dvb[:ATT_BLOCK]
                    dk_sc[rows, lanes] += dkb[ATT_BLOCK:]
                    dv_sc[rows, lanes] += dvb[ATT_BLOCK:]
            return c

        lax.fori_loop(0, N_UNITS, unit, 0)
        dqkv_ref[1] = dk_sc[...].astype(bf16)
        dqkv_ref[2] = dv_sc[...].astype(bf16)

    qkv_specs, bspec = _qkv_specs(S, branch)
    row = pl.BlockSpec((S, _LANES), lambda i: (0, 0))
    return pl.pallas_call(
        body, name="att_bwd", grid=(1,),
        in_specs=qkv_specs + [pl.BlockSpec((S, D_ATT), lambda i: (0, 0)), row, row, bspec],
        out_specs=[pl.BlockSpec((3, S, D_ATT), lambda i: (0, 0, 0)),
                   pl.BlockSpec((N_HEADS, ATT_BLOCK, 2 * ATT_BLOCK), lambda i: (0, 0, 0))],
        out_shape=[jax.ShapeDtypeStruct((3, S, D_ATT), bf16), jax.ShapeDtypeStruct((N_HEADS, ATT_BLOCK, 2 * ATT_BLOCK), f32)],
        scratch_shapes=[pltpu.VMEM((S, D_ATT), f32), pltpu.VMEM((S, D_ATT), f32)],
        compiler_params=_cp(("arbitrary",), 48),
    )(*_hbm(qkv, qkv, qkv, do, lse, crow, bias))


def _branch_weights(lse_ref):
    l0, l1, l2 = lse_ref[0], lse_ref[1], lse_ref[2]
    m = jnp.maximum(jnp.maximum(l0, l1), l2)
    e0, e1, e2 = jnp.exp(l0 - m), jnp.exp(l1 - m), jnp.exp(l2 - m)
    tot = e0 + e1 + e2
    return e0 / tot, e1 / tot, e2 / tot


def _att_merge(os, lses):
    S = os[0].shape[0] * os[0].shape[1]
    ts = TOK_TILE

    def body(o1_ref, o4_ref, o16_ref, l1_ref, l4_ref, l16_ref, y_ref, lt_ref, *bufs):
        obufs = (bufs[:_QKV_BLOCKS], bufs[_QKV_BLOCKS:2 * _QKV_BLOCKS])
        lt_ref[0] = l1_ref[0]
        for k, (d, o_ref, l_ref) in enumerate(((4, o4_ref, l4_ref), (16, o16_ref, l16_ref))):
            _residues_to_rows(obufs[k], d, lambda r, cb, o_ref=o_ref: o_ref[r, :, _LANES * cb:_LANES * (cb + 1)])
            _residues_to_rows([bufs[2 * _QKV_BLOCKS + k]], d, lambda r, cb, l_ref=l_ref: l_ref[r])
            lt_ref[1 + k] = bufs[2 * _QKV_BLOCKS + k][...]
        w = _branch_weights(lt_ref)
        for h in range(N_HEADS):
            cs = slice(HEAD_DIM * h, HEAD_DIM * (h + 1))
            half = slice(HEAD_DIM * (h % 2), HEAD_DIM * (h % 2 + 1))
            y_ref[:, cs] = (w[0][:, h:h + 1] * o1_ref[0, :, cs] + w[1][:, h:h + 1] * obufs[0][h // 2][:, half]
                            + w[2][:, h:h + 1] * obufs[1][h // 2][:, half])

    return pl.pallas_call(
        body, name="att_merge", grid=(S // ts,),
        in_specs=[_res_spec3(d, D_ATT) for d in DILATIONS] + [_res_spec3(d, _LANES) for d in DILATIONS],
        out_specs=[pl.BlockSpec((ts, D_ATT), lambda i: (i, 0)), pl.BlockSpec((3, ts, _LANES), lambda i: (0, i, 0))],
        out_shape=[jax.ShapeDtypeStruct((S, D_ATT), f32), jax.ShapeDtypeStruct((3, S, _LANES), f32)],
        scratch_shapes=[pltpu.VMEM((ts, _LANES), f32)] * (2 * _QKV_BLOCKS + 2),
        compiler_params=_cp(("parallel",)),
    )(*_hbm(*os, *lses))


def _att_merge_bwd(dy, y, lse3):
    S = dy.shape[0]
    ts = TOK_TILE

    def body(dy_ref, y_ref, lse_ref, do1_ref, do4_ref, do16_ref, c1_ref, c4_ref, c16_ref, *bufs):
        dobufs = (bufs[:_QKV_BLOCKS], bufs[_QKV_BLOCKS:2 * _QKV_BLOCKS], bufs[2 * _QKV_BLOCKS:3 * _QKV_BLOCKS])
        cbufs = bufs[3 * _QKV_BLOCKS:]
        w = _branch_weights(lse_ref)
        for cb in cbufs:
            cb[...] = jnp.zeros_like(cb)
        for h in range(N_HEADS):
            cs = slice(HEAD_DIM * h, HEAD_DIM * (h + 1))
            half = slice(HEAD_DIM * (h % 2), HEAD_DIM * (h % 2 + 1))
            dyh = dy_ref[:, cs]
            t = jnp.sum(dyh * y_ref[:, cs], -1, keepdims=True)
            for p in range(3):
                wp = w[p][:, h:h + 1]
                dobufs[p][h // 2][:, half] = wp * dyh
                cbufs[p][:, h:h + 1] = wp * t
        for cb in range(_QKV_BLOCKS):
            do1_ref[0, :, _LANES * cb:_LANES * (cb + 1)] = dobufs[0][cb][...].astype(bf16)
        c1_ref[0] = cbufs[0][...]
        for k, (d, do_ref, c_ref) in enumerate(((4, do4_ref, c4_ref), (16, do16_ref, c16_ref))):
            def put_do(r, cb, piece, do_ref=do_ref):
                do_ref[r, :, _LANES * cb:_LANES * (cb + 1)] = piece.astype(bf16)

            def put_c(r, cb, piece, c_ref=c_ref):
                c_ref[r] = piece

            _rows_to_residues(dobufs[1 + k], d, put_do)
            _rows_to_residues([cbufs[1 + k]], d, put_c)

    return pl.pallas_call(
        body, name="att_merge_bwd", grid=(S // ts,),
        in_specs=[pl.BlockSpec((ts, D_ATT), lambda i: (i, 0)), pl.BlockSpec((ts, D_ATT), lambda i: (i, 0)),
                  pl.BlockSpec((3, ts, _LANES), lambda i: (0, i, 0))],
        out_specs=[_res_spec3(d, D_ATT) for d in DILATIONS] + [_res_spec3(d, _LANES) for d in DILATIONS],
        out_shape=[jax.ShapeDtypeStruct((d, S // d, D_ATT), bf16) for d in DILATIONS]
        + [jax.ShapeDtypeStruct((d, S // d, _LANES), f32) for d in DILATIONS],
        scratch_shapes=[pltpu.VMEM((ts, _LANES), f32)] * (3 * _QKV_BLOCKS + 3),
        compiler_params=_cp(("parallel",)),
    )(*_hbm(dy, y, lse3))


_SSM_ROWS = 256


def _scan_in_place(sr_ref, si_ref, a_ref, reverse):
    S, N = sr_ref.shape
    nst = S // SCAN_SEG
    ar = jnp.broadcast_to(a_ref[0:1, :], (SCAN_SEG, N))
    ai = jnp.broadcast_to(a_ref[1:2, :], (SCAN_SEG, N))
    if reverse:
        ai = -ai
    row = lax.broadcasted_iota(jnp.int32, (SCAN_SEG, N), 0)
    zero = jnp.zeros((SCAN_SEG, N), f32)

    def tile(t):
        return pl.ds(pl.multiple_of((nst - 1 - t if reverse else t) * SCAN_SEG, SCAN_SEG), SCAN_SEG)

    def local(t, c):
        sr, si, pr, pi = c
        rows = tile(t)
        nsr = ar * sr - ai * si + sr_ref[rows, :]
        nsi = ar * si + ai * sr + si_ref[rows, :]
        sr_ref[rows, :] = nsr
        si_ref[rows, :] = nsi
        return nsr, nsi, ar * pr - ai * pi, ar * pi + ai * pr

    fr, fi, apr, api = lax.fori_loop(0, nst, local, (zero, zero, zero + 1.0, zero))

    def shift(v):
        if reverse:
            return jnp.where(row == SCAN_SEG - 1, 0.0, pltpu.roll(v, SCAN_SEG - 1, axis=0))
        return jnp.where(row == 0, 0.0, pltpu.roll(v, 1, axis=0))

    cr, ci = zero, zero
    for _ in range(SCAN_SEG - 1):
        cr, ci = shift(fr + apr * cr - api * ci), shift(fi + apr * ci + api * cr)

    def fix(t, c):
        pr, pi = c
        npr, npi = ar * pr - ai * pi, ar * pi + ai * pr
        rows = tile(t)
        sr_ref[rows, :] += npr * cr - npi * ci
        si_ref[rows, :] += npr * ci + npi * cr
        return npr, npi

    lax.fori_loop(0, nst, fix, (zero + 1.0, zero))


def _ssm_states(u, bre, bim, a2, l):
    S = u.shape[0]

    def body(u_ref, br_ref, bi_ref, a2_ref, sr_ref, si_ref):
        a_ref = a2_ref.at[l]
        brb = br_ref[l].astype(bf16)
        bib = bi_ref[l].astype(bf16)

        def project(t, c):
            rows = pl.ds(pl.multiple_of(t * _SSM_ROWS, _SSM_ROWS), _SSM_ROWS)
            ub = u_ref[rows, :].astype(bf16)
            sr_ref[rows, :] = _dot(ub, brb)
            si_ref[rows, :] = _dot(ub, bib)
            return c

        lax.fori_loop(0, S // _SSM_ROWS, project, 0)
        _scan_in_place(sr_ref, si_ref, a_ref, False)

    vm = pl.BlockSpec(memory_space=pltpu.VMEM)
    return pl.pallas_call(
        body, name="ssm_states", in_specs=[vm] * 4, out_specs=[vm, vm],
        out_shape=[jax.ShapeDtypeStruct((S, D_STATE), f32)] * 2,
        compiler_params=_cp(None, 48),
    )(u, bre, bim, a2)


def _ssm_out(sr, si, u, cre, cim, l, dskip, glu_w, glu_b):
    S = u.shape[0]
    ts = TOK_TILE

    def body(sr_ref, si_ref, u_ref, cr_ref, ci_ref, d_ref, w_ref, b_ref, out_ref, y_ref):
        y = (_dot(sr_ref[...].astype(bf16), cr_ref[0].astype(bf16))
             - _dot(si_ref[...].astype(bf16), ci_ref[0].astype(bf16)) + d_ref[...] * u_ref[...])
        y_ref[...] = y
        z = _dot(_gelu(y).astype(bf16), w_ref[...].astype(bf16)) + b_ref[...]
        out_ref[...] = y * jax.nn.sigmoid(z)

    st = pl.BlockSpec((ts, D_STATE), lambda i: (i, 0))
    ch = pl.BlockSpec((ts, D_SSM), lambda i: (i, 0))
    c_l = pl.BlockSpec((1, D_STATE, D_SSM), lambda i: (l, 0, 0))
    return pl.pallas_call(
        body, name="ssm_out", grid=(S // ts,),
        in_specs=[st, st, ch, c_l, c_l, _full((1, D_SSM)), _full((D_SSM, D_SSM)), _full((1, D_SSM))],
        out_specs=[ch, ch],
        out_shape=[jax.ShapeDtypeStruct((S, D_SSM), f32)] * 2,
        compiler_params=_cp(("parallel",)),
    )(*_hbm(sr, si, u, cre, cim, dskip, glu_w, glu_b))


def _ssm_out_bwd(dout, y, u, sr, si, dskip, glu_w, glu_b):
    S = u.shape[0]
    ts = TOK_TILE

    def body(do_ref, y_ref, u_ref, sr_ref, si_ref, d_ref, w_ref, b_ref,
             dy_ref, du_ref, dcr_ref, dci_ref, dd_ref, dgb_ref, dgw_ref):
        @pl.when(pl.program_id(0) == 0)
        def _():
            for r in (dcr_ref, dci_ref, dd_ref, dgb_ref, dgw_ref):
                r[...] = jnp.zeros_like(r)

        y = y_ref[...]
        dout = do_ref[...]
        wb = w_ref[...].astype(bf16)
        ge = _gelu(y).astype(bf16)
        sz = jax.nn.sigmoid(_dot(ge, wb) + b_ref[...])
        dz = dout * y * sz * (1.0 - sz)
        dzb = dz.astype(bf16)
        dgb_ref[...] += jnp.sum(dz, 0, keepdims=True)
        dgw_ref[...] += _dot_tn(ge, dzb)
        dy = dout * sz + _gelu_grad(y) * _dot_nt(dzb, wb)
        uv = u_ref[...]
        dd_ref[...] += jnp.sum(dy * uv, 0, keepdims=True)
        du_ref[...] = dy * d_ref[...]
        dy_ref[...] = dy
        dyb = dy.astype(bf16)
        dcr_ref[...] += _dot_tn(sr_ref[...].astype(bf16), dyb)
        dci_ref[...] -= _dot_tn(si_ref[...].astype(bf16), dyb)

    st = pl.BlockSpec((ts, D_STATE), lambda i: (i, 0))
    ch = pl.BlockSpec((ts, D_SSM), lambda i: (i, 0))
    c_full = _full((D_STATE, D_SSM))
    return pl.pallas_call(
        body, name="ssm_out_bwd", grid=(S // ts,),
        in_specs=[ch, ch, ch, st, st, _full((1, D_SSM)), _full((D_SSM, D_SSM)), _full((1, D_SSM))],
        out_specs=[ch, ch, c_full, c_full, _full((1, D_SSM)), _full((1, D_SSM)), _full((D_SSM, D_SSM))],
        out_shape=[jax.ShapeDtypeStruct((S, D_SSM), f32), jax.ShapeDtypeStruct((S, D_SSM), f32),
                   jax.ShapeDtypeStruct((D_STATE, D_SSM), f32), jax.ShapeDtypeStruct((D_STATE, D_SSM), f32),
                   jax.ShapeDtypeStruct((1, D_SSM), f32), jax.ShapeDtypeStruct((1, D_SSM), f32),
                   jax.ShapeDtypeStruct((D_SSM, D_SSM), f32)],
        compiler_params=_cp(("arbitrary",), 40),
    )(*_hbm(dout, y, u, sr, si, dskip, glu_w, glu_b))


def _ssm_states_bwd(dy, du_skip, u, sr, si, cre, cim, bre, bim, a2, l):
    S = u.shape[0]
    N = D_STATE
    nst = S // SCAN_SEG
    nproj = S // _SSM_ROWS

    def body(dy_ref, dus_ref, u_ref, sr_ref, si_ref, cr_ref, ci_ref, br_ref, bi_ref, a2_ref,
             du_ref, dbr_ref, dbi_ref, da_ref, lr_ref, li_ref):
        a_ref = a2_ref.at[l]
        crb = cr_ref[l].astype(bf16)
        cib = ci_ref[l].astype(bf16)

        def project(t, c):
            rows = pl.ds(pl.multiple_of(t * _SSM_ROWS, _SSM_ROWS), _SSM_ROWS)
            dyb = dy_ref[rows, :].astype(bf16)
            lr_ref[rows, :] = _dot_nt(dyb, crb)
            li_ref[rows, :] = -_dot_nt(dyb, cib)
            return c

        lax.fori_loop(0, nproj, project, 0)
        _scan_in_place(lr_ref, li_ref, a_ref, True)

        row = lax.broadcasted_iota(jnp.int32, (SCAN_SEG, N), 0)
        last = pl.ds((nst - 1) * SCAN_SEG, SCAN_SEG)
        pr = jnp.where(row == 0, 0.0, pltpu.roll(sr_ref[last, :], 1, axis=0))
        pi = jnp.where(row == 0, 0.0, pltpu.roll(si_ref[last, :], 1, axis=0))
        first = pl.ds(0, SCAN_SEG)
        acc_r = lr_ref[first, :] * pr + li_ref[first, :] * pi
        acc_i = li_ref[first, :] * pr - lr_ref[first, :] * pi

        def step(t, c):
            acc_r, acc_i = c
            rows = pl.ds(pl.multiple_of(t * SCAN_SEG, SCAN_SEG), SCAN_SEG)
            prev = pl.ds(pl.multiple_of((t - 1) * SCAN_SEG, SCAN_SEG), SCAN_SEG)
            lrv, liv, srv, siv = lr_ref[rows, :], li_ref[rows, :], sr_ref[prev, :], si_ref[prev, :]
            return acc_r + lrv * srv + liv * siv, acc_i + liv * srv - lrv * siv

        acc_r, acc_i = lax.fori_loop(1, nst, step, (acc_r, acc_i))
        da_ref[0:1, :] = jnp.sum(acc_r, 0, keepdims=True)
        da_ref[1:2, :] = jnp.sum(acc_i, 0, keepdims=True)

        brb = br_ref[l].astype(bf16)
        bib = bi_ref[l].astype(bf16)
        dbr_ref[...] = jnp.zeros_like(dbr_ref)
        dbi_ref[...] = jnp.zeros_like(dbi_ref)

        def back(t, c):
            rows = pl.ds(pl.multiple_of(t * _SSM_ROWS, _SSM_ROWS), _SSM_ROWS)
            lrb = lr_ref[rows, :].astype(bf16)
            lib = li_ref[rows, :].astype(bf16)
            du_ref[rows, :] = dus_ref[rows, :] + _dot_nt(lrb, brb) + _dot_nt(lib, bib)
            ub = u_ref[rows, :].astype(bf16)
            dbr_ref[...] += _dot_tn(ub, lrb)
            dbi_ref[...] += _dot_tn(ub, lib)
            return c

        lax.fori_loop(0, nproj, back, 0)

    vm = pl.BlockSpec(memory_space=pltpu.VMEM)
    return pl.pallas_call(
        body, name="ssm_states_bwd", in_specs=[vm] * 10, out_specs=[vm] * 4,
        out_shape=[jax.ShapeDtypeStruct((S, D_SSM), f32), jax.ShapeDtypeStruct((D_SSM, D_STATE), f32),
                   jax.ShapeDtypeStruct((D_SSM, D_STATE), f32), jax.ShapeDtypeStruct((2, D_STATE), f32)],
        scratch_shapes=[pltpu.VMEM((S, D_STATE), f32), pltpu.VMEM((S, D_STATE), f32)],
        compiler_params=_cp(None, 56),
    )(dy, du_skip, u, sr, si, cre, cim, bre, bim, a2)


_POOL_TILE = 256


def _window_sums(xt, back):
    n = xt.shape[0]
    out = []
    ws = xt
    for k in (1, 2, 4, 8):
        ws = ws + pltpu.roll(ws, k if back else n - k, axis=0)
        out.append(ws)
    return out


def _pool_count(r0, w):
    t = r0 + lax.broadcasted_iota(jnp.int32, (_POOL_TILE, POOL_GROUP), 0)
    return jnp.minimum(t + 1, w).astype(f32)


def _pool_fwd(u_pad, pool_w, pool_scale):
    S = u_pad.shape[0] - POOL_HALO
    nt = S // _POOL_TILE

    def body(u_ref, w_ref, sc_ref, y_ref):
        def tile(t, c):
            r0 = pl.multiple_of(t * _POOL_TILE, _POOL_TILE)
            for g, w in enumerate(POOL_WINDOWS):
                cs = pl.ds(POOL_GROUP * g, POOL_GROUP)
                xt = u_ref[pl.ds(r0, _POOL_TILE + POOL_HALO), cs]
                ws = _window_sums(xt, True)[g][POOL_HALO:, :]
                pooled = ws / _pool_count(r0, w) - xt[POOL_HALO:, :]
                y_ref[pl.ds(r0, _POOL_TILE), cs] = _dot(pooled.astype(bf16), w_ref[g].astype(bf16)) * sc_ref[:, cs]
            return c
        lax.fori_loop(0, nt, tile, 0)

    vm = pl.BlockSpec(memory_space=pltpu.VMEM)
    return pl.pallas_call(
        body, name="pool_fwd", in_specs=[vm, vm, vm], out_specs=vm,
        out_shape=jax.ShapeDtypeStruct((S, D_POOL), f32),
    )(u_pad, pool_w, pool_scale)


def _pool_bwd(dy_pad, u_pad, pool_w, pool_scale):
    S = u_pad.shape[0] - POOL_HALO
    nt = S // _POOL_TILE
    n = _POOL_TILE + POOL_HALO

    def body(dy_ref, u_ref, w_ref, sc_ref, du_ref, dw_ref, dsc_ref):
        dw_ref[...] = jnp.zeros_like(dw_ref)
        dsc_ref[...] = jnp.zeros_like(dsc_ref)

        def tile(t, c):
            r0 = pl.multiple_of(t * _POOL_TILE, _POOL_TILE)
            for g, w in enumerate(POOL_WINDOWS):
                cs = pl.ds(POOL_GROUP * g, POOL_GROUP)
                wb = w_ref[g].astype(bf16)
                xt = u_ref[pl.ds(r0, n), cs]
                pooled = (_window_sums(xt, True)[g][POOL_HALO:, :] / _pool_count(r0, w) - xt[POOL_HALO:, :]).astype(bf16)
                dy = dy_ref[pl.ds(r0, _POOL_TILE), cs]
                dsc_ref[:, cs] += jnp.sum(dy * _dot(pooled, wb), 0, keepdims=True)
                dw_ref[g] += _dot_tn(pooled, (dy * sc_ref[:, cs]).astype(bf16))
                dyh = (dy_ref[pl.ds(r0, n), cs] * sc_ref[:, cs]).astype(bf16)
                dpl = _dot_nt(dyh, wb)
                cnt = jnp.minimum(r0 + lax.broadcasted_iota(jnp.int32, (n, POOL_GROUP), 0) + 1, w).astype(f32)
                lead = _window_sums(dpl / cnt, False)[g]
                du_ref[pl.ds(r0, _POOL_TILE), cs] = lead[:_POOL_TILE, :] - dpl[:_POOL_TILE, :]
            return c
        lax.fori_loop(0, nt, tile, 0)

    vm = pl.BlockSpec(memory_space=pltpu.VMEM)
    return pl.pallas_call(
        body, name="pool_bwd", in_specs=[vm, vm, vm, vm], out_specs=[vm, vm, vm],
        out_shape=[jax.ShapeDtypeStruct((S, D_POOL), f32), jax.ShapeDtypeStruct((4, POOL_GROUP, POOL_GROUP), f32),
                   jax.ShapeDtypeStruct((1, D_POOL), f32)],
    )(dy_pad, u_pad, pool_w, pool_scale)


def _loss_head(y, target):
    S, D = y.shape
    ts = TOK_TILE

    def body(y_ref, t_ref, loss_ref, dy_ref):
        @pl.when(pl.program_id(0) == 0)
        def _():
            loss_ref[...] = jnp.zeros_like(loss_ref)

        d = y_ref[...] - t_ref[...]
        dy_ref[...] = d * (1.0 / D)
        loss_ref[...] += 0.5 * jnp.sum(jnp.sum(d * d, -1, keepdims=True) * (1.0 / D), 0, keepdims=True)

    tok = pl.BlockSpec((ts, D), lambda i: (i, 0))
    return pl.pallas_call(
        body, name="loss_head", grid=(S // ts,),
        in_specs=[tok, tok], out_specs=[_full((1, 1)), tok],
        out_shape=[jax.ShapeDtypeStruct((1, 1), f32), jax.ShapeDtypeStruct((S, D), f32)],
        compiler_params=_cp(("arbitrary",)),
    )(*_hbm(y, target))


_ADA_COLS = 768


def _ada_fwd(c_all, ada_w, ada_b_cols):
    L, D, N = ada_w.shape
    B = c_all.shape[0]

    def body(c_ref, w_ref, b_ref, out_ref):
        cv = c_ref[...]
        cond = (cv * jax.nn.sigmoid(cv)).astype(bf16)
        out_ref[0] = _dot(cond, w_ref[0].astype(bf16)) + b_ref[0]

    return pl.pallas_call(
        body, name="ada_fwd", grid=(L, N // _ADA_COLS),
        in_specs=[_full((B, D)), pl.BlockSpec((1, D, _ADA_COLS), lambda l, j: (l, 0, j)),
                  pl.BlockSpec((1, 1, _ADA_COLS), lambda l, j: (l, 0, j))],
        out_specs=pl.BlockSpec((1, B, _ADA_COLS), lambda l, j: (l, 0, j)),
        out_shape=jax.ShapeDtypeStruct((L, B, N), f32),
        compiler_params=_cp(("parallel", "parallel")),
    )(c_all, ada_w, ada_b_cols)


def _ada_wgrad(c_all_t, dmod_cols):
    D, B = c_all_t.shape
    L, _, N = dmod_cols.shape

    def body(ct_ref, dm_ref, out_ref):
        cv = ct_ref[...]
        cond = cv * jax.nn.sigmoid(cv)
        acc = cond[:, 0:1] * dm_ref[0, 0:1, :]
        for b in range(1, B):
            acc = acc + cond[:, b:b + 1] * dm_ref[0, b:b + 1, :]
        out_ref[0] = acc

    return pl.pallas_call(
        body, name="ada_wgrad", grid=(L, N // _ADA_COLS),
        in_specs=[_full((D, B)), pl.BlockSpec((1, B, _ADA_COLS), lambda l, j: (l, 0, j))],
        out_specs=pl.BlockSpec((1, D, _ADA_COLS), lambda l, j: (l, 0, j)),
        out_shape=jax.ShapeDtypeStruct((L, D, N), f32),
        compiler_params=_cp(("parallel", "parallel")),
    )(c_all_t, dmod_cols)


def _adam_math(w, g, m, v):
    m = ADAM_B1 * m + (1.0 - ADAM_B1) * g
    v = ADAM_B2 * v + (1.0 - ADAM_B2) * (g * g)
    m_hat = m / (1.0 - ADAM_B1 ** ADAM_STEP)
    v_hat = v / (1.0 - ADAM_B2 ** ADAM_STEP)
    delta = -ADAM_LR * (m_hat / (jnp.sqrt(v_hat) + ADAM_EPS) + ADAM_WD * w)
    return delta, m, v


def _adamw(w, m, v, g, row_tile, row0=0, outs=None):
    R, C = w.shape
    b0 = row0 // row_tile

    def body(w_ref, m_ref, v_ref, g_ref, _0, _1, _2, _3, g_out, d_out, m_out, v_out):
        gv = g_ref[...]
        delta, mn, vn = _adam_math(w_ref[...], gv, m_ref[...], v_ref[...])
        g_out[...] = gv
        d_out[...] = delta
        m_out[...] = mn
        v_out[...] = vn

    pspec = pl.BlockSpec((row_tile, C), lambda i: (b0 + i, 0))
    gspec = pl.BlockSpec((row_tile, C), lambda i: (i, 0))
    anyspec = pl.BlockSpec(memory_space=pl.ANY)
    shp = jax.ShapeDtypeStruct((R, C), f32)
    if outs is None:
        outs = [lax.empty((R, C), f32) for _ in range(4)]
    return pl.pallas_call(
        body, name="adamw", grid=(g.shape[0] // row_tile,),
        in_specs=[pspec] * 3 + [gspec] + [anyspec] * 4, out_specs=[pspec] * 4, out_shape=[shp] * 4,
        input_output_aliases={4: 0, 5: 1, 6: 2, 7: 3},
        compiler_params=_cp(("parallel",), 40),
    )(*_hbm(w, m, v, g, *outs))


def _pair_sum(g5s, gots, pc):
    n = len(g5s)

    def body(pc_ref, *refs):
        for own, got, out in zip(refs[:n], refs[n:2 * n], refs[2 * n:]):
            out[0, 0] = (own[0, 0, 0].astype(f32) + got[0, 0].astype(f32)).astype(bf16)

    def half(g):
        return pl.BlockSpec((1, 1) + g.shape[-2:], lambda p, pc: (p, 0, 0, 0))

    gs = pltpu.PrefetchScalarGridSpec(
        num_scalar_prefetch=1, grid=(N_CHIPS,),
        in_specs=[pl.BlockSpec((1, 1, 1) + g.shape[-2:], lambda p, pc: (p, 0, pc[1], 0, 0)) for g in g5s]
        + [half(g) for g in gots],
        out_specs=[half(g) for g in gots],
    )
    return pl.pallas_call(
        body, name="pair_sum", grid_spec=gs, out_shape=[jax.ShapeDtypeStruct(g.shape, bf16) for g in gots],
        compiler_params=_cp(("parallel",), 48),
    )(pc, *_hbm(*g5s, *gots))


_SUM_STEPS = 2


def _sum_shards(hsums, recvs, pc):
    n = len(hsums)

    def body(pc_ref, *refs):
        for own, got, out in zip(refs[:n], refs[n:2 * n], refs[2 * n:]):
            acc = own[0, 0].astype(f32)
            for j in range(3):
                acc = acc + got[j, 0].astype(f32)
            out[0, 0] = acc

    def rows(h):
        return (h.shape[2] // _SUM_STEPS, h.shape[3])

    gs = pltpu.PrefetchScalarGridSpec(
        num_scalar_prefetch=1, grid=(_SUM_STEPS,),
        in_specs=[pl.BlockSpec((1, 1) + rows(h), lambda i, pc: (pc[0], 0, i, 0)) for h in hsums]
        + [pl.BlockSpec((3, 1) + rows(h), lambda i, pc: (0, 0, i, 0)) for h in hsums],
        out_specs=[pl.BlockSpec((1, 1) + rows(h), lambda i, pc: (0, pc[1], i, 0)) for h in hsums],
    )
    return pl.pallas_call(
        body, name="sum_shards", grid_spec=gs,
        out_shape=[jax.ShapeDtypeStruct((1, 2) + h.shape[2:], f32) for h in hsums],
        compiler_params=_cp(("parallel",), 48),
    )(pc, *_hbm(*hsums, *recvs))


def _sum8(packs):
    _, R, C = packs.shape
    tr = R // 8 if R % 64 == 0 else R

    def body(p_ref, out_ref):
        acc = p_ref[0]
        for d in range(1, 8):
            acc = acc + p_ref[d]
        out_ref[...] = acc

    return pl.pallas_call(
        body, name="sum8", grid=(R // tr,),
        in_specs=[pl.BlockSpec((8, tr, C), lambda i: (0, i, 0))],
        out_specs=pl.BlockSpec((tr, C), lambda i: (i, 0)),
        out_shape=jax.ShapeDtypeStruct((R, C), f32),
        compiler_params=_cp(("parallel",)),
    )(packs)


def _allgather8(x_shard):
    m_per, n = x_shard.shape

    def body(x_ref, out_ref, send_sems, recv_sems, local_sem):
        x, y, c = lax.axis_index("x"), lax.axis_index("y"), lax.axis_index("c")
        me, sibling = (x, y, c), (x, y, 1 - c)
        chips = [(1 - x, y), (x, 1 - y), (1 - x, 1 - y)]

        def rows(px, py, pc):
            return out_ref.at[pl.ds((4 * px + 2 * py + pc) * m_per, m_per), :]

        def copy(k, block, to, src=None):
            return pltpu.make_async_remote_copy(
                src_ref=rows(*block) if src is None else src, dst_ref=rows(*block),
                send_sem=send_sems.at[k], recv_sem=recv_sems.at[k], device_id=to, device_id_type=MESH)

        mine = pltpu.make_async_copy(x_ref, rows(*me), local_sem)
        mine.start()
        first = [copy(0, me, sibling, src=x_ref)]
        first += [copy(1 + j, me, (*chip, c), src=x_ref) for j, chip in enumerate(chips)]
        for cp in first:
            cp.start()
        passed = [copy(4 + j, (*chip, c), sibling) for j, chip in enumerate(chips)]
        for j, chip in enumerate(chips):
            copy(1 + j, (*chip, c), me).wait_recv()
            passed[j].start()
        copy(0, sibling, me).wait_recv()
        for j, chip in enumerate(chips):
            copy(4 + j, (*chip, 1 - c), me).wait_recv()
        for cp in first + passed:
            cp.wait_send()
        mine.wait()

    return pl.pallas_call(
        body, name="allgather8",
        out_shape=jax.ShapeDtypeStruct((8 * m_per, n), x_shard.dtype),
        in_specs=[pl.BlockSpec(memory_space=pltpu.VMEM)],
        out_specs=pl.BlockSpec(memory_space=pltpu.VMEM),
        scratch_shapes=[pltpu.SemaphoreType.DMA((7,)), pltpu.SemaphoreType.DMA((7,)), pltpu.SemaphoreType.DMA],
        compiler_params=_cp(None, 48),
    )(x_shard)


def _other_chips():
    x, y = lax.axis_index("x"), lax.axis_index("y")
    return [(1 - x, y), (x, 1 - y), (1 - x, 1 - y)]


_HBM = pl.BlockSpec(memory_space=pltpu.HBM)
_SEM = pl.BlockSpec(memory_space=pltpu.SEMAPHORE)
_EFFECT = pltpu.SideEffectType.DATAFLOW_SIDE_EFFECTING


def _gather_copies(srcs, lands, send_sems, recv_sems):
    x, y, c = lax.axis_index("x"), lax.axis_index("y"), lax.axis_index("c")
    return [pltpu.make_async_remote_copy(
        src_ref=srcs[a].at[:, c], dst_ref=lands[a].at[2 * x + y, :, c], send_sem=send_sems.at[3 * a + j],
        recv_sem=recv_sems.at[3 * a + j], device_id=(cx, cy, c), device_id_type=MESH)
        for a in range(len(srcs)) for j, (cx, cy) in enumerate(_other_chips())]


def _gather_start(chunks, after, name):
    sizes = [len(srcs) for srcs, _ in chunks]
    flat = [t for srcs, lands in chunks for t in list(srcs) + list(lands)]
    nflat = len(flat)
    nsem = 2 * len(chunks)

    def body(*refs):
        ins, sems, token = refs[:nflat], refs[nflat + 1:nflat + 1 + nsem], refs[-1]
        off = 0
        for k, n in enumerate(sizes):
            for cp in _gather_copies(ins[off:off + n], ins[off + n:off + 2 * n], sems[2 * k], sems[2 * k + 1]):
                cp.start()
            off += 2 * n
        token[...] = jnp.zeros_like(token)

    res = pl.pallas_call(
        body, name=name,
        out_shape=[pltpu.SemaphoreType.DMA((3 * n,)) for n in sizes for _ in range(2)]
        + [pltpu.HBM(t.shape, t.dtype) for t in flat] + [jax.ShapeDtypeStruct((8, 128), f32)],
        in_specs=[_HBM] * nflat + [pl.BlockSpec(memory_space=pl.ANY)],
        out_specs=[_SEM] * nsem + [_HBM] * nflat + [pl.BlockSpec(memory_space=pltpu.VMEM)],
        input_output_aliases={i: nsem + i for i in range(nflat)},
        compiler_params=pltpu.CompilerParams(has_side_effects=_EFFECT),
    )(*[pltpu.with_memory_space_constraint(t, pltpu.HBM) for t in flat], after)
    out, off = [], nsem
    for k, n in enumerate(sizes):
        out.append((res[2 * k], res[2 * k + 1], res[off:off + n], res[off + n:off + 2 * n]))
        off += 2 * n
    return out, res[-1]


def _gather_wait(send_sems, recv_sems, srcs, lands, after, name):
    n = len(srcs)

    def body(*refs):
        for cp in _gather_copies(refs[:n], refs[n:2 * n], refs[2 * n], refs[2 * n + 1]):
            cp.wait_send()
            cp.wait_recv()

    res = pl.pallas_call(
        body, name=name,
        out_shape=[pltpu.HBM(t.shape, t.dtype) for t in list(srcs) + list(lands)],
        in_specs=[_HBM] * (2 * n) + [_SEM, _SEM] + [pl.BlockSpec(memory_space=pl.ANY)] * len(after),
        out_specs=[_HBM] * (2 * n),
        input_output_aliases={i: i for i in range(2 * n)},
        compiler_params=pltpu.CompilerParams(has_side_effects=_EFFECT),
    )(*srcs, *lands, send_sems, recv_sems, *after)
    return res[n:]


def _split_start(make_copies, arrays, nsem, name, after=()):
    n, na = len(arrays), len(after)

    def body(*refs):
        for cp in make_copies(refs[:n], refs[n + na], refs[n + na + 1]):
            cp.start()
        refs[-1][...] = jnp.zeros_like(refs[-1])

    res = pl.pallas_call(
        body, name=name,
        out_shape=[pltpu.SemaphoreType.DMA((nsem,)), pltpu.SemaphoreType.DMA((nsem,))]
        + [pltpu.HBM(t.shape, t.dtype) for t in arrays] + [jax.ShapeDtypeStruct((8, 128), f32)],
        in_specs=[_HBM] * n + [pl.BlockSpec(memory_space=pl.ANY)] * na,
        out_specs=[_SEM, _SEM] + [_HBM] * n + [pl.BlockSpec(memory_space=pltpu.VMEM)],
        input_output_aliases={i: i + 2 for i in range(n)},
        compiler_params=pltpu.CompilerParams(has_side_effects=_EFFECT),
    )(*[pltpu.with_memory_space_constraint(t, pltpu.HBM) for t in arrays], *after)
    return (res[0], res[1], res[2:2 + n]), res[-1]


def _split_wait(make_copies, send_sems, recv_sems, arrays, after, name):
    n = len(arrays)

    def body(*refs):
        for cp in make_copies(refs[:n], refs[n], refs[n + 1]):
            cp.wait_send()
            cp.wait_recv()

    return pl.pallas_call(
        body, name=name,
        out_shape=[pltpu.HBM(t.shape, t.dtype) for t in arrays],
        in_specs=[_HBM] * n + [_SEM, _SEM] + [pl.BlockSpec(memory_space=pl.ANY)] * len(after),
        out_specs=[_HBM] * n, input_output_aliases={i: i for i in range(n)},
        compiler_params=pltpu.CompilerParams(has_side_effects=_EFFECT),
    )(*arrays, send_sems, recv_sems, *after)


def _sibling():
    return lax.axis_index("x"), lax.axis_index("y"), 1 - lax.axis_index("c")


def _forward_copies(lands, send_sems, recv_sems):
    c = lax.axis_index("c")
    return [pltpu.make_async_remote_copy(
        src_ref=lands[a].at[2 * cx + cy, :, c], dst_ref=lands[a].at[2 * cx + cy, :, c], send_sem=send_sems.at[3 * a + j],
        recv_sem=recv_sems.at[3 * a + j], device_id=_sibling(), device_id_type=MESH)
        for a in range(len(lands)) for j, (cx, cy) in enumerate(_other_chips())]


def _swap_copies(fulls, send_sems, recv_sems):
    c = lax.axis_index("c")
    return [pltpu.make_async_remote_copy(src_ref=t.at[:, c], dst_ref=t.at[:, c], send_sem=send_sems.at[a],
                                         recv_sem=recv_sems.at[a], device_id=_sibling(), device_id_type=MESH)
            for a, t in enumerate(fulls)]


def _pair_copies(refs, send_sems, recv_sems):
    n = len(refs) // 2
    c = lax.axis_index("c")
    return [pltpu.make_async_remote_copy(src_ref=refs[a].at[:, :, 1 - c], dst_ref=refs[n + a], send_sem=send_sems.at[a],
                                         recv_sem=recv_sems.at[a], device_id=_sibling(), device_id_type=MESH)
            for a in range(n)]


def _scatter_copies(srcs, lands, send_sems, recv_sems):
    c = lax.axis_index("c")
    return [pltpu.make_async_remote_copy(
        src_ref=srcs[a].at[2 * cx + cy], dst_ref=lands[a].at[j], send_sem=send_sems.at[3 * a + j],
        recv_sem=recv_sems.at[3 * a + j], device_id=(cx, cy, c), device_id_type=MESH)
        for a in range(len(srcs)) for j, (cx, cy) in enumerate(_other_chips())]


def _scatter_start(hsums, name, after=()):
    n = len(hsums)
    na = len(after)

    def body(*refs):
        srcs, lands = refs[:n], refs[n:2 * n]
        send_sems, recv_sems = refs[2 * n + na], refs[2 * n + na + 1]
        for cp in _scatter_copies(srcs, lands, send_sems, recv_sems):
            cp.start()
        refs[-1][...] = jnp.zeros_like(refs[-1])

    lands = [lax.empty((3,) + g.shape[1:], g.dtype) for g in hsums]
    res = pl.pallas_call(
        body, name=name,
        out_shape=[pltpu.SemaphoreType.DMA((3 * n,)), pltpu.SemaphoreType.DMA((3 * n,))]
        + [pltpu.HBM(g.shape, g.dtype) for g in hsums] + [pltpu.HBM(g.shape, g.dtype) for g in lands]
        + [jax.ShapeDtypeStruct((8, 128), f32)],
        in_specs=[_HBM] * (2 * n) + [pl.BlockSpec(memory_space=pl.ANY)] * na,
        out_specs=[_SEM, _SEM] + [_HBM] * (2 * n) + [pl.BlockSpec(memory_space=pltpu.VMEM)],
        input_output_aliases={i: i + 2 for i in range(2 * n)},
        compiler_params=pltpu.CompilerParams(has_side_effects=_EFFECT),
    )(*[pltpu.with_memory_space_constraint(t, pltpu.HBM) for t in list(hsums) + lands], *after)
    return (res[0], res[1], res[2:2 + n], res[2 + n:2 + 2 * n]), res[-1]


def _scatter_wait(send_sems, recv_sems, srcs, lands, after, name):
    n = len(srcs)
    extra = list(after)

    def body(*refs):
        s_refs, l_refs = refs[:n], refs[n:2 * n]
        ss, rs = refs[2 * n], refs[2 * n + 1]
        for cp in _scatter_copies(s_refs, l_refs, ss, rs):
            cp.wait_send()
            cp.wait_recv()

    res = pl.pallas_call(
        body, name=name,
        out_shape=[pltpu.HBM(g.shape, g.dtype) for g in srcs] + [pltpu.HBM(g.shape, g.dtype) for g in lands],
        in_specs=[_HBM] * (2 * n) + [_SEM, _SEM] + [pl.BlockSpec(memory_space=pl.ANY)] * len(extra),
        out_specs=[_HBM] * (2 * n),
        input_output_aliases={i: i for i in range(2 * n)},
        compiler_params=pltpu.CompilerParams(has_side_effects=_EFFECT),
    )(*srcs, *lands, send_sems, recv_sems, *extra)
    return res[:n], res[n:]


def _plane_copies(src, land, send_sems, recv_sems):
    x, y, c = lax.axis_index("x"), lax.axis_index("y"), lax.axis_index("c")
    return [pltpu.make_async_remote_copy(src_ref=src, dst_ref=land.at[2 * x + y, c], send_sem=send_sems.at[j],
                                         recv_sem=recv_sems.at[j], device_id=(cx, cy, c), device_id_type=MESH)
            for j, (cx, cy) in enumerate(_other_chips())]


def _plane_start(pack, land, name):
    def body(src, lnd, send_sems, recv_sems, _s, _l, token):
        for cp in _plane_copies(src, lnd, send_sems, recv_sems):
            cp.start()
        token[...] = jnp.zeros_like(token)

    res = pl.pallas_call(
        body, name=name,
        out_shape=[pltpu.SemaphoreType.DMA((3,)), pltpu.SemaphoreType.DMA((3,)), pltpu.HBM(pack.shape, pack.dtype),
                   pltpu.HBM(land.shape, land.dtype), jax.ShapeDtypeStruct((8, 128), f32)],
        in_specs=[_HBM, _HBM], out_specs=[_SEM, _SEM, _HBM, _HBM, pl.BlockSpec(memory_space=pltpu.VMEM)],
        input_output_aliases={0: 2, 1: 3},
        compiler_params=pltpu.CompilerParams(has_side_effects=_EFFECT),
    )(pltpu.with_memory_space_constraint(pack, pltpu.HBM), pltpu.with_memory_space_constraint(land, pltpu.HBM))
    return res[:4], res[4]


def _plane_wait(send_sems, recv_sems, pack, land, after, name):
    def body(src, lnd, ss, rs, *_):
        for cp in _plane_copies(src, lnd, ss, rs):
            cp.wait_send()
            cp.wait_recv()

    return pl.pallas_call(
        body, name=name,
        out_shape=[pltpu.HBM(pack.shape, pack.dtype), pltpu.HBM(land.shape, land.dtype)],
        in_specs=[_HBM, _HBM, _SEM, _SEM] + [pl.BlockSpec(memory_space=pl.ANY)] * len(after),
        out_specs=[_HBM, _HBM], input_output_aliases={0: 0, 1: 1},
        compiler_params=pltpu.CompilerParams(has_side_effects=_EFFECT),
    )(pack, land, send_sems, recv_sems, *after)[1]


def _swap_halves(fulls):
    n = len(fulls)

    def body(*refs):
        ins, outs = refs[:n], refs[n:2 * n]
        send_sems, recv_sems = refs[2 * n:]
        c = lax.axis_index("c")
        sibling = (lax.axis_index("x"), lax.axis_index("y"), 1 - c)
        copies = []
        for a in range(n):
            cp = pltpu.make_async_remote_copy(src_ref=outs[a].at[:, c], dst_ref=outs[a].at[:, c], send_sem=send_sems.at[a],
                                              recv_sem=recv_sems.at[a], device_id=sibling, device_id_type=MESH)
            cp.start()
            copies.append(cp)
        for a, cp in enumerate(copies):
            cp.wait_send()
            theirs = outs[a].at[:, 1 - c]
            pltpu.make_async_remote_copy(src_ref=theirs, dst_ref=theirs, send_sem=send_sems.at[a], recv_sem=recv_sems.at[a],
                                         device_id=sibling, device_id_type=MESH).wait_recv()

    hbm = pl.BlockSpec(memory_space=pl.ANY)
    return pl.pallas_call(
        body, name="swap_halves",
        out_shape=[jax.ShapeDtypeStruct(p.shape, p.dtype) for p in fulls],
        in_specs=[hbm] * n, out_specs=[hbm] * n,
        input_output_aliases={a: a for a in range(n)},
        scratch_shapes=[pltpu.SemaphoreType.DMA((n,)), pltpu.SemaphoreType.DMA((n,))],
    )(*fulls)


def _to_segments(t):
    s, c = t.shape
    return t.reshape(SCAN_SEG, s // SCAN_SEG, c).transpose(1, 0, 2).reshape(s, c)


def _from_segments(t):
    s, c = t.shape
    return t.reshape(s // SCAN_SEG, SCAN_SEG, c).transpose(1, 0, 2).reshape(s, c)


def _ssm_operators(a_re, a_im, log_dt, b_re, b_im, c_re, c_im):
    lam = lax.complex(a_re, a_im)
    dt = jnp.exp(log_dt)[:, None]
    a_bar = jnp.exp(lam * dt)
    b_bar = ((a_bar - 1.0) / lam)[:, :, None] * lax.complex(b_re, b_im)
    eye = jnp.eye(N_GROUPS, dtype=f32)

    def embed_b(t):
        return (jnp.transpose(t, (0, 2, 1))[:, :, None, :] * eye[:, None, :, None]).reshape(D_SSM, D_STATE)

    def embed_c(t):
        return (jnp.transpose(t, (0, 2, 1))[:, :, None, :] * eye[:, None, :, None]).reshape(D_STATE, D_SSM)

    a2 = jnp.stack([a_bar.real.reshape(D_STATE), a_bar.imag.reshape(D_STATE)])
    return a2, embed_b(b_bar.real), embed_b(b_bar.imag), embed_c(c_re), embed_c(c_im)


def _local_step(x, target, mod, small, ffn_weights, mix_weights, grads_done, ffn_bwd_issued):
    table = jnp.asarray(_bucket_table())
    bias = small["att_bias"]
    L = DEPTH
    saved = []
    ssm_names = ("ssm_a_re", "ssm_a_im", "ssm_log_dt", "ssm_b_re", "ssm_b_im", "ssm_c_re", "ssm_c_im")
    ssm_ops_vjp = []
    for l in range(L):
        sv = {}
        m9 = mod[l]
        sv["x0"] = x
        sv["w0"] = ffn_weights(l, 0, x)
        x, sv["f0"], sv["g0"], sv["u0"], sv["h0"] = _ffn_fwd(x, m9[0:3], *sv["w0"], 0, small["ln_g"][l, 0:1], small["ln_b"][l, 0:1])
        sv["x1"] = x
        sv["w1"] = mix_weights(l, x)
        *qkv, z_rest, sv["h1"] = _mix_in_fwd(x, m9[3:6], sv["w1"][0], 0)
        S = x.shape[0]
        qkv = [t.reshape(3, S, D_ATT) for t in qkv]
        att = [_att_fwd(qkv[b], bias, b) for b in range(3)]
        y_att, lse3 = _att_merge([att[b][0].reshape(d, S // d, D_ATT) for b, d in enumerate(DILATIONS)],
                                 [att[b][1].reshape(d, S // d, _LANES) for b, d in enumerate(DILATIONS)])
        sv.update(qkv=qkv, lse=[a[1] for a in att], lse3=lse3, y_att=y_att)

        ops, ops_vjp = jax.vjp(_ssm_operators, *[small[k][l] for k in ssm_names])
        ssm_ops_vjp.append(ops_vjp)
        a2, bre, bim, cre, cim = [t[None] for t in ops]
        u_ssm = _to_segments(z_rest[:, :D_SSM])
        sr, si = _ssm_states(u_ssm, bre, bim, a2, 0)
        dskip = small["ssm_d"][l][None, :]
        glu_b = small["glu_b"][l][None, :]
        out_seg, y_seg = _ssm_out(sr, si, u_ssm, cre, cim, 0, dskip, small["glu_w"][l], glu_b)
        y_ssm = _from_segments(out_seg)
        sv.update(ssm_ops=(a2, bre, bim, cre, cim), u_ssm=u_ssm, sr=sr, si=si, y_seg=y_seg, y_ssm=y_ssm)

        u_pool = jnp.concatenate([jnp.zeros((POOL_HALO, D_POOL), f32), z_rest[:, D_SSM:]])
        y_pool = _pool_fwd(u_pool, small["pool_w"][l], small["pool_scale"][l][None, :])
        sv.update(u_pool=u_pool, y_pool=y_pool)

        x, sv["ymix"] = _mix_out_fwd(x, y_att, y_ssm, y_pool, m9[3:6], sv["w1"][1], 0, small["ln_g"][l, 1:2], small["ln_b"][l, 1:2])
        sv["x2"] = x
        sv["w2"] = ffn_weights(l, 1, x)
        x, sv["f2"], sv["g2"], sv["u2"], sv["h2"] = _ffn_fwd(x, m9[6:9], *sv["w2"], 0, small["ln_g"][l, 2:3], small["ln_b"][l, 2:3])
        saved.append(sv)

    loss, dx = _loss_head(x, target)

    dmod = [None] * L
    dln_g = [None] * L
    dln_b = [None] * L
    sg = {k: [None] * L for k in ssm_names + ("ssm_d", "glu_w", "glu_b", "pool_w", "pool_scale")}
    dbias_tot = None
    order_after = jnp.zeros((), f32)
    for l in reversed(range(L)):
        sv = saved[l]
        m9 = mod[l] + order_after

        def fresh(like):
            return [lax.empty(t.shape, bf16) for t in like]

        dx, dg, du, a, df, dm2, dlg2, dlb2 = _ffn_bwd(dx, sv["x2"], sv["f2"], sv["g2"], sv["u2"], m9[6:9], *sv["w2"], 0,
                                                     small["ln_g"][l, 2:3])
        m9 = m9 + ffn_bwd_issued(l, 1, dx)
        g_ffn1 = _ffn_wgrad(sv["h2"], dg, du, a, df, *fresh(sv["w2"]), 0)
        dxr, d_att, d_ssm, d_pool, dgate1, dlg1, dlb1, g_w_out = _mix_out_bwd(
            dx, sv["x1"], sv["ymix"], sv["y_att"], sv["y_ssm"], sv["y_pool"], m9[3:6], sv["w1"][1], 0, small["ln_g"][l, 1:2],
            fresh(sv["w1"])[1])
        S = d_att.shape[0]
        merged = _att_merge_bwd(d_att, sv["y_att"], sv["lse3"])
        dqkv, dbias = [], []
        for b, d in enumerate(DILATIONS):
            dq_b, db_b = _att_bwd(sv["qkv"][b], merged[b].reshape(S, D_ATT), sv["lse"][b], merged[3 + b].reshape(S, _LANES), bias, b)
            dqkv.append(dq_b.reshape(3, d, S // d, D_ATT))
            dbias.append(db_b)
        dbias = jnp.stack(dbias)
        dbias_tot = dbias if dbias_tot is None else dbias_tot + dbias
        d_seg = _to_segments(d_ssm)
        dskip = small["ssm_d"][l][None, :]
        glu_b = small["glu_b"][l][None, :]
        dy_seg, du_skip, dcre, dcim, dd, dglu_b, dglu_w = _ssm_out_bwd(
            d_seg, sv["y_seg"], sv["u_ssm"], sv["sr"], sv["si"], dskip, small["glu_w"][l], glu_b)
        a2, bre, bim, cre, cim = sv["ssm_ops"]
        du_seg, dbre, dbim, da2 = _ssm_states_bwd(dy_seg, du_skip, sv["u_ssm"], sv["sr"], sv["si"], cre, cim, bre, bim, a2, 0)
        for k, t in zip(ssm_names, ssm_ops_vjp[l]((da2, dbre, dbim, dcre, dcim))):
            sg[k][l] = t
        sg["ssm_d"][l] = dd[0]
        sg["glu_b"][l] = dglu_b[0]
        sg["glu_w"][l] = dglu_w
        du_ssm = _from_segments(du_seg)
        dyp = jnp.concatenate([d_pool, jnp.zeros((POOL_HALO, D_POOL), f32)])
        du_pool, dpw, dps = _pool_bwd(dyp, sv["u_pool"], small["pool_w"][l], small["pool_scale"][l][None, :])
        sg["pool_w"][l] = dpw
        sg["pool_scale"][l] = dps[0]
        d_rest = jnp.concatenate([du_ssm, du_pool], axis=1).astype(bf16)
        dx, dm1, dz = _mix_in_bwd(dqkv, d_rest, dxr, sv["x1"], m9[3:6], sv["w1"][0], 0)
        g_w_in = _mix_in_wgrad(sv["h1"], dz, fresh(sv["w1"])[0], 0)
        ffn_names = ("ffn_w_gate", "ffn_w_up", "ffn_w_down")
        m9 = m9 + grads_done(l, 1, list(zip(ffn_names, [(2 * l + 1) * FF_SHARD] * 3, g_ffn1))
                             + [("w_in", l * D_MODEL, g_w_in), ("w_out", l * 256, g_w_out)])
        dm1 = jnp.concatenate([dm1[0:2], dgate1])
        dx, dg, du, a, df, dm0, dlg0, dlb0 = _ffn_bwd(dx, sv["x0"], sv["f0"], sv["g0"], sv["u0"], m9[0:3], *sv["w0"], 0,
                                                     small["ln_g"][l, 0:1])
        issued = ffn_bwd_issued(l, 0, dx)
        g_ffn0 = _ffn_wgrad(sv["h0"], dg, du, a, df, *fresh(sv["w0"]), 0)
        order_after = grads_done(l, 0, list(zip(ffn_names, [2 * l * FF_SHARD] * 3, g_ffn0))) + issued
        dmod[l] = jnp.concatenate([dm0 + issued, dm1, dm2])
        dln_g[l] = jnp.concatenate([dlg0, dlg1, dlg2])
        dln_b[l] = jnp.concatenate([dlb0, dlb1, dlb2])

    small_grads = {k: jnp.stack(v) for k, v in sg.items()}
    small_grads["rel_bias"] = _bias_bwd(dbias_tot, table)
    small_grads["ln_g"] = jnp.stack(dln_g)
    small_grads["ln_b"] = jnp.stack(dln_b)
    return loss, dx, jnp.stack(dmod), small_grads


_TILE_ELEMS = 8 * 128


def _pack_rows(shapes):
    out, row = [], 0
    for s in shapes:
        nr = -(-int(np.prod(s)) // _TILE_ELEMS) * 8
        out.append((row, nr))
        row += nr
    return out


def _pack(arrs):
    parts = []
    for a in arrs:
        flat = a.reshape(-1).astype(f32)
        npad = -(-flat.shape[0] // _TILE_ELEMS) * _TILE_ELEMS
        parts.append(jnp.pad(flat, (0, npad - flat.shape[0])).reshape(npad // 128, 128))
    return jnp.concatenate(parts, axis=0)


def _unpack(buf, shapes):
    return [buf[row:row + nr].reshape(-1)[:int(np.prod(s))].reshape(s) for s, (row, nr) in zip(shapes, _pack_rows(shapes))]


_REPL = ("rel_bias", "ada_b", "ssm_a_re", "ssm_a_im", "ssm_log_dt", "ssm_b_re", "ssm_b_im", "ssm_c_re", "ssm_c_im",
         "ssm_d", "glu_b", "pool_w", "pool_scale")
_SMALL_SHARDED = ("ln_g", "ln_b", "glu_w")
_BIG = ("ffn_w_gate", "ffn_w_up", "ffn_w_down", "w_in", "w_out")
_ORDER = ("rel_bias", "ada_w", "ada_b", "ln_g", "ln_b", "ffn_w_gate", "ffn_w_up", "ffn_w_down", "w_in", "w_out",
          "ssm_a_re", "ssm_a_im", "ssm_log_dt", "ssm_b_re", "ssm_b_im", "ssm_c_re", "ssm_c_im", "ssm_d", "glu_w",
          "glu_b", "pool_w", "pool_scale")


def kernel(x, c, rel_bias, ada_w, ada_b, ln_g, ln_b, ffn_w_gate, ffn_w_up, ffn_w_down, w_in, w_out, ssm_a_re, ssm_a_im, ssm_log_dt, ssm_b_re, ssm_b_im, ssm_c_re, ssm_c_im, ssm_d, glu_w, glu_b, pool_w, pool_scale, loss_target, m_rel_bias, m_ada_w, m_ada_b, m_ln_g, m_ln_b, m_ffn_w_gate, m_ffn_w_up, m_ffn_w_down, m_w_in, m_w_out, m_ssm_a_re, m_ssm_a_im, m_ssm_log_dt, m_ssm_b_re, m_ssm_b_im, m_ssm_c_re, m_ssm_c_im, m_ssm_d, m_glu_w, m_glu_b, m_pool_w, m_pool_scale, v_rel_bias, v_ada_w, v_ada_b, v_ln_g, v_ln_b, v_ffn_w_gate, v_ffn_w_up, v_ffn_w_down, v_w_in, v_w_out, v_ssm_a_re, v_ssm_a_im, v_ssm_log_dt, v_ssm_b_re, v_ssm_b_im, v_ssm_c_re, v_ssm_c_im, v_ssm_d, v_glu_w, v_glu_b, v_pool_w, v_pool_scale):
    args = dict(locals())
    w = {k: args[k] for k in _ORDER}
    m = {k: args["m_" + k] for k in _ORDER}
    v = {k: args["v_" + k] for k in _ORDER}
    L, D = DEPTH, D_MODEL
    ax, ay, ac = lax.axis_index("x"), lax.axis_index("y"), lax.axis_index("c")
    p_me = 2 * ax + ay
    dev = 4 * ax + 2 * ay + ac

    transposed = ("ffn_w_gate", "ffn_w_up")
    for d in (w, m, v):
        for name in transposed:
            d[name] = jnp.swapaxes(d[name], 2, 3)

    def halves(t):
        return t.astype(bf16).reshape(1, 2, t.shape[0] // 2, t.shape[1])

    def landing(src):
        return lax.dynamic_update_slice(lax.empty((N_CHIPS,) + src.shape, bf16), src[None], (p_me, 0, 0, 0, 0))

    chunk_keys = [("ffn", 0, 0), ("mix", 0), ("ffn", 0, 1), ("ffn", 1, 0), ("mix", 1), ("ffn", 1, 1)]
    chunk_srcs = []
    for key in chunk_keys:
        if key[0] == "ffn":
            chunk_srcs.append([halves(w[name][key[1], key[2]]) for name in ("ffn_w_gate", "ffn_w_up", "ffn_w_down")])
        else:
            chunk_srcs.append([halves(w_in[key[1]]), halves(w_out[key[1]])])

    pack = _pack([c, ln_g, ln_b, glu_w])
    rows = pack.shape[0]
    allp = _allgather8(pack).reshape(8, rows, 128)
    chunks = [(srcs, [landing(t) for t in srcs]) for srcs in chunk_srcs]
    first_in_flight, first_begun = _gather_start(chunks[:1], allp, "gather_start_first")
    c_all = allp[:, :8].reshape(8, D) + first_begun[0, 0]
    by_chip = allp[0::2]

    fwd_rows = _pack_rows([c.shape, ln_g.shape, ln_b.shape, glu_w.shape])

    def sharded(part, shape, axis):
        row0, nrows = fwd_rows[part]
        t = by_chip[:, row0:row0 + nrows].reshape(N_CHIPS, -1)[:, :int(np.prod(shape))].reshape((N_CHIPS,) + shape)
        return jnp.concatenate([t[p] for p in range(N_CHIPS)], axis=axis)

    ln_g_full = sharded(1, ln_g.shape, 2)
    ln_b_full = sharded(2, ln_b.shape, 2)
    glu_w_full = sharded(3, glu_w.shape, 1)

    ncol = ada_w.shape[-1]
    ada_b_cols = lax.dynamic_slice_in_dim(ada_b, p_me * ncol, ncol, axis=1)[:, None, :]
    mod_part = _ada_fwd(c_all, ada_w, ada_b_cols)
    mrows = L * 8 * ncol // 128
    mod_pack = mod_part.reshape(mrows, 128)
    mod_land = lax.dynamic_update_slice(lax.empty((N_CHIPS, 2, mrows, 128), f32), mod_pack[None, None], (p_me, ac, 0, 0))
    mod_in_flight, _ = _plane_start(mod_pack, mod_land, "mod_start")
    att_bias = _bias_fwd(rel_bias, jnp.asarray(_bucket_table()))
    small_names = _REPL + _SMALL_SHARDED
    small_packs = [_pack([d[k] for k in small_names]) for d in (w, m, v)]
    mod_land = _plane_wait(*mod_in_flight, [att_bias] + small_packs + [t for _, lands in chunks[1:] for t in lands], "mod_wait")
    mod_all = _swap_halves([mod_land])[0].reshape(8, L, 8, ncol)
    mod_mine = lax.dynamic_index_in_dim(mod_all, dev, axis=2, keepdims=False)
    mod = jnp.concatenate([mod_mine[2 * p] for p in range(N_CHIPS)], axis=-1).reshape(L, 9, D)

    rest_in_flight, rest_begun = _gather_start(chunks[1:], mod, "gather_start_rest")
    in_flight = first_in_flight + rest_in_flight

    forwarding = {}

    def forward(k, after):
        lands = _gather_wait(*in_flight[k], [after, rest_begun], "gather_wait_%d" % k)
        forwarding[k], begun = _split_start(_forward_copies, lands, 3 * len(lands), "gather_forward_start_%d" % k)
        return begun

    def gathered(key, after):
        k = chunk_keys.index(key)
        order = [after]
        if k not in forwarding:
            order.append(forward(k, after))
        if 3 <= k + 1 < len(chunk_keys):
            order.append(forward(k + 1, after))
        lands = _split_wait(_forward_copies, *forwarding[k], order, "gather_forward_wait_%d" % k)
        return [t.reshape(N_CHIPS, 1, 2 * t.shape[3], t.shape[4]) for t in lands]

    pc = jnp.stack([p_me, ac]).astype(jnp.int32)
    groups = {}
    scattering = {}

    pairing = {}

    def start_pairs(tag, after=()):
        g5 = [g.reshape(g.shape[:2] + (2, g.shape[2] // 2, g.shape[3])) for _, _, g in groups[tag]]
        gots = [lax.empty(g.shape[:2] + g.shape[3:], bf16) for g in g5]
        pairing[tag], begun = _split_start(_pair_copies, g5 + gots, len(g5), "pair_exchange_start_%s" % tag, after)
        return begun

    def start_group(tag, after):
        arrays = _split_wait(_pair_copies, *pairing[tag], after, "pair_exchange_wait_%s" % tag)
        n = len(arrays) // 2
        hsum = _pair_sum(arrays[:n], arrays[n:], pc)
        scattering[tag], begun = _scatter_start(hsum, "scatter_start_%s" % tag)
        return begun

    def grads_done(l, s, grads):
        if l == 1:
            groups.setdefault("l1", []).extend(grads)
            return start_pairs("l1")[0, 0] if s == 0 else jnp.zeros((), f32)
        groups["l0a" if s == 1 else "l0b"] = grads
        return start_pairs("l0a")[0, 0] if s == 1 else jnp.zeros((), f32)

    swapping = {}

    def reduce_group(tag, after):
        hsum, recv = _scatter_wait(*scattering[tag], after, "scatter_wait_%s" % tag)
        full = _sum_shards(hsum, recv, pc)
        swapping[tag], begun = _split_start(_swap_copies, full, len(full), "swap_halves_start_%s" % tag)
        return [begun]

    def ffn_bwd_issued(l, s, dx):
        if l == 1:
            return jnp.zeros((), f32)
        return start_group("l1" if s == 1 else "l0a", [dx])[0, 0]

    small = {k: w[k] for k in _REPL if k != "ada_b"}
    small.update(ln_g=ln_g_full, ln_b=ln_b_full, glu_w=glu_w_full, att_bias=att_bias)
    loss_dev, grad_x, dmod, sgrads = _local_step(
        x[0], loss_target[0], mod, small, lambda l, s, after: gathered(("ffn", l, s), after),
        lambda l, after: gathered(("mix", l), after), grads_done, ffn_bwd_issued)
    loss = lax.psum(loss_dev[0, 0], ("x", "y", "c"))

    names = ("rel_bias", "ln_g", "ln_b", "ssm_a_re", "ssm_a_im", "ssm_log_dt", "ssm_b_re", "ssm_b_im", "ssm_c_re",
             "ssm_c_im", "ssm_d", "glu_w", "glu_b", "pool_w", "pool_scale")
    gpack = _pack([dmod] + [sgrads[k] for k in names])
    grows = gpack.shape[0]
    land = lax.dynamic_update_slice(lax.empty((N_CHIPS, 2, grows, 128), f32), gpack[None, None], (p_me, ac, 0, 0))
    small_in_flight, small_begun = _plane_start(gpack, land, "small_grads_start")
    l0b_begun = start_group("l0b", [start_pairs("l0b", (small_begun,))])

    out_g, out_d, out_m, out_v = {}, {}, {}, {}
    row_tile = dict(zip(_BIG, (352, 352, 352, 256, 256)))
    big = {name: None for name in _BIG}

    def update_group(tag, after):
        full = _split_wait(_swap_copies, *swapping[tag], after, "swap_halves_wait_%s" % tag)
        for (name, row0, _), g in zip(groups[tag], full):
            shp = w[name].shape
            r2 = (int(np.prod(shp[:-1])), shp[-1])
            big[name] = _adamw(w[name].reshape(r2), m[name].reshape(r2), v[name].reshape(r2), g.reshape(-1, shp[-1]),
                               row_tile[name], row0, big[name])
        return [big[name][1] for name, _, _ in groups[tag]]

    after = reduce_group("l0a", reduce_group("l1", [grad_x, l0b_begun]))
    after = update_group("l0a", update_group("l1", after))

    land = _plane_wait(*small_in_flight, after, "small_grads_wait")
    gall = _swap_halves([land])[0].reshape(8, grows, 128)
    gsum = _unpack(_sum8(gall), [(L, 9 * D)] + [sgrads[k].shape for k in names])
    red = dict(zip(("ada_b",) + names, gsum))
    red["ln_g"] = lax.dynamic_slice_in_dim(red["ln_g"], p_me * 256, 256, axis=2)
    red["ln_b"] = lax.dynamic_slice_in_dim(red["ln_b"], p_me * 256, 256, axis=2)
    red["glu_w"] = lax.dynamic_slice_in_dim(red["glu_w"], p_me * 64, 64, axis=1)

    dmod_all = gall[:, :L * 9 * D // 128].reshape(8, L, 9 * D)
    dmod_cols = jnp.transpose(lax.dynamic_slice_in_dim(dmod_all, p_me * ncol, ncol, axis=2), (1, 0, 2))
    g_ada_w = _ada_wgrad(jnp.transpose(c_all), dmod_cols)

    r2 = (L * D, ncol)
    res = _adamw(ada_w.reshape(r2), m["ada_w"].reshape(r2), v["ada_w"].reshape(r2), g_ada_w.reshape(r2), 128)
    out_g["ada_w"], out_d["ada_w"], out_m["ada_w"], out_v["ada_w"] = [t.reshape(ada_w.shape) for t in res]

    res_small = _adamw(*small_packs, _pack([red[k] for k in small_names]), small_packs[0].shape[0])
    for t, dst in zip(res_small, (out_g, out_d, out_m, out_v)):
        for k, a in zip(small_names, _unpack(t, [w[k].shape for k in small_names])):
            dst[k] = a

    update_group("l0b", reduce_group("l0b", [res_small[1], res[1]]))
    for name in _BIG:
        res = [t.reshape(w[name].shape) for t in big[name]]
        out_g[name], out_d[name], out_m[name], out_v[name] = [jnp.swapaxes(t, 2, 3) for t in res] if name in transposed else res

    return (loss, grad_x[None], *[out_g[k] for k in _ORDER], *[out_d[k] for k in _ORDER],
            *[out_m[k] for k in _ORDER], *[out_v[k] for k in _ORDER])
```

```python
import math

import numpy as np
import jax
import jax.numpy as jnp
from jax import lax
from jax.experimental import pallas as pl
from jax.experimental.pallas import tpu as pltpu

f32 = jnp.float32
bf16 = jnp.bfloat16
MESH = pl.DeviceIdType.MESH

D_MODEL = 1024
SEQ = 2048
DEPTH = 2
HEAD_DIM = 64
N_HEADS = 8
D_ATT = 512
DILATIONS = (1, 4, 16)
BLOCKS_PER_RESIDUE = (16, 4, 1)
ATT_BLOCK = 128
N_UNITS = SEQ // ATT_BLOCK
N_GROUPS = 16
SSM_STATE = 64
D_SSM = 256
D_STATE = N_GROUPS * SSM_STATE
POOL_WINDOWS = (2, 4, 8, 16)
POOL_GROUP = 64
D_POOL = 256
POOL_HALO = 16
D_FF = 2816
N_BUCKETS = 32
MAX_DISTANCE = 2048
ALPHA = (2 * DEPTH) ** 0.25
FFN_RES = 0.5
LN_EPS = 1e-5
NEG = -1e30
N_CHIPS = 4
FF_SHARD = D_FF // N_CHIPS
SCAN_SEG = 8

ADAM_LR, ADAM_B1, ADAM_B2, ADAM_EPS, ADAM_WD, ADAM_STEP = 0.001, 0.9, 0.999, 1e-08, 0.01, 10

TOK_TILE = 512


def _cp(dims=None, vmem_mb=None):
    kw = {}
    if dims is not None:
        kw["dimension_semantics"] = dims
    if vmem_mb is not None:
        kw["vmem_limit_bytes"] = vmem_mb << 20
    return pltpu.CompilerParams(**kw)


def _dot(a, b):
    return jnp.dot(a, b, preferred_element_type=f32)


def _dot_nt(a, b):
    return lax.dot_general(a, b, (((1,), (1,)), ((), ())), preferred_element_type=f32)


def _dot_tn(a, b):
    return lax.dot_general(a, b, (((0,), (0,)), ((), ())), preferred_element_type=f32)


def _ln_stats(v):
    mu = jnp.mean(v, -1, keepdims=True)
    d = v - mu
    var = jnp.mean(d * d, -1, keepdims=True)
    rstd = lax.rsqrt(var + LN_EPS)
    return d * rstd, rstd


def _ln_bwd(dxh, xh, rstd):
    return rstd * (dxh - jnp.mean(dxh, -1, keepdims=True) - xh * jnp.mean(dxh * xh, -1, keepdims=True))


_GELU_C = math.sqrt(2.0 / math.pi)


def _gelu(y):
    return 0.5 * y * (1.0 + jnp.tanh(_GELU_C * (y + 0.044715 * y * y * y)))


def _gelu_grad(y):
    t = jnp.tanh(_GELU_C * (y + 0.044715 * y * y * y))
    return 0.5 * (1.0 + t) + 0.5 * y * (1.0 - t * t) * (_GELU_C * (1.0 + 3 * 0.044715 * y * y))


def _full(shape):
    return pl.BlockSpec(shape, lambda *_: (0,) * len(shape))


def _hbm(*args):
    return [pltpu.with_memory_space_constraint(a, pltpu.HBM) if getattr(a, "ndim", 0) >= 2 else a for a in args]


def _ffn_fwd(x, mod3, wg, wu, wd, ls, lng, lnb):
    S, D = x.shape
    Fs = wg.shape[-2]
    ts = 2 * TOK_TILE

    def body(x_ref, mod_ref, wg_ref, wu_ref, wd_ref, lng_ref, lnb_ref, xo_ref, f_ref, g_ref, u_ref, h_ref, acc_sc):
        j = pl.program_id(1)

        @pl.when(j == 0)
        def _():
            xh, _ = _ln_stats(x_ref[...])
            h_ref[...] = (xh * (1.0 + mod_ref[1:2, :]) + mod_ref[0:1, :]).astype(bf16)
            acc_sc[...] = jnp.zeros_like(acc_sc)

        h = h_ref[...]
        g = _dot_nt(h, wg_ref[0, 0])
        u = _dot_nt(h, wu_ref[0, 0])
        g_ref[0] = g.astype(bf16)
        u_ref[0] = u.astype(bf16)
        a = (g * jax.nn.sigmoid(g) * u).astype(bf16)
        acc_sc[...] += _dot(a, wd_ref[0, 0])

        @pl.when(j == N_CHIPS - 1)
        def _():
            f = acc_sc[...]
            f_ref[...] = f
            r = ALPHA * x_ref[...] + (FFN_RES * mod_ref[2:3, :]) * f
            rh, _ = _ln_stats(r)
            xo_ref[...] = rh * lng_ref[...] + lnb_ref[...]

    tok = pl.BlockSpec((ts, D), lambda i, j: (i, 0))
    wrow = pl.BlockSpec((1, 1, Fs, D), lambda i, j: (j, ls, 0, 0))
    hid = pl.BlockSpec((1, ts, Fs), lambda i, j: (j, i, 0))
    return pl.pallas_call(
        body, name="ffn_fwd", grid=(S // ts, N_CHIPS),
        in_specs=[tok, _full((3, D)), wrow, wrow, wrow, _full((1, D)), _full((1, D))],
        out_specs=[tok, tok, hid, hid, tok],
        out_shape=[jax.ShapeDtypeStruct((S, D), f32), jax.ShapeDtypeStruct((S, D), f32),
                   jax.ShapeDtypeStruct((N_CHIPS, S, Fs), bf16), jax.ShapeDtypeStruct((N_CHIPS, S, Fs), bf16),
                   jax.ShapeDtypeStruct((S, D), bf16)],
        scratch_shapes=[pltpu.VMEM((ts, D), f32)],
        compiler_params=_cp(("parallel", "arbitrary"), 56),
    )(*_hbm(x, mod3, wg, wu, wd, lng, lnb))


def _ffn_bwd(dxo, x, f, g, u, mod3, wg, wu, wd, ls, lng):
    S, D = x.shape
    Fs = wg.shape[-2]
    ts = TOK_TILE

    def body(dxo_ref, x_ref, f_ref, g_ref, u_ref, mod_ref, wg_ref, wu_ref, wd_ref, lng_ref,
             dx_ref, dg_ref, du_ref, a_ref, df_ref, dmod_ref, dlng_ref, dlnb_ref,
             dr_sc, df_sc, acc_sc):
        i = pl.program_id(0)
        j = pl.program_id(1)

        @pl.when((i == 0) & (j == 0))
        def _():
            dmod_ref[...] = jnp.zeros_like(dmod_ref)
            dlng_ref[...] = jnp.zeros_like(dlng_ref)
            dlnb_ref[...] = jnp.zeros_like(dlnb_ref)

        @pl.when(j == 0)
        def _():
            xv = x_ref[...]
            fv = f_ref[...]
            gate = mod_ref[2:3, :]
            rh, rstd = _ln_stats(ALPHA * xv + (FFN_RES * gate) * fv)
            dy = dxo_ref[...]
            dlng_ref[...] += jnp.sum(dy * rh, 0, keepdims=True)
            dlnb_ref[...] += jnp.sum(dy, 0, keepdims=True)
            dr = _ln_bwd(dy * lng_ref[...], rh, rstd)
            dr_sc[...] = dr
            dmod_ref[2:3, :] += jnp.sum(FFN_RES * dr * fv, 0, keepdims=True)
            df = ((FFN_RES * gate) * dr).astype(bf16)
            df_sc[...] = df
            df_ref[...] = df
            acc_sc[...] = jnp.zeros_like(acc_sc)

        da = _dot_nt(df_sc[...], wd_ref[0, 0])
        gv = g_ref[0].astype(f32)
        uv = u_ref[0].astype(f32)
        sg = jax.nn.sigmoid(gv)
        si = gv * sg
        a_ref[0] = (si * uv).astype(bf16)
        dgv = (da * uv * (sg * (1.0 + gv * (1.0 - sg)))).astype(bf16)
        duv = (da * si).astype(bf16)
        dg_ref[0] = dgv
        du_ref[0] = duv
        acc_sc[...] += _dot(dgv, wg_ref[0, 0]) + _dot(duv, wu_ref[0, 0])

        @pl.when(j == N_CHIPS - 1)
        def _():
            dh = acc_sc[...]
            xh, rstd0 = _ln_stats(x_ref[...])
            dmod_ref[0:1, :] += jnp.sum(dh, 0, keepdims=True)
            dmod_ref[1:2, :] += jnp.sum(dh * xh, 0, keepdims=True)
            dx_ref[...] = _ln_bwd(dh * (1.0 + mod_ref[1:2, :]), xh, rstd0) + ALPHA * dr_sc[...]

    tok = pl.BlockSpec((ts, D), lambda i, j: (i, 0))
    wrow = pl.BlockSpec((1, 1, Fs, D), lambda i, j: (j, ls, 0, 0))
    hid = pl.BlockSpec((1, ts, Fs), lambda i, j: (j, i, 0))
    hid_shape = jax.ShapeDtypeStruct((N_CHIPS, S, Fs), bf16)
    return pl.pallas_call(
        body, name="ffn_bwd", grid=(S // ts, N_CHIPS),
        in_specs=[tok, tok, tok, hid, hid, _full((3, D)), wrow, wrow, wrow, _full((1, D))],
        out_specs=[tok, hid, hid, hid, tok, _full((3, D)), _full((1, D)), _full((1, D))],
        out_shape=[jax.ShapeDtypeStruct((S, D), f32), hid_shape, hid_shape, hid_shape,
                   jax.ShapeDtypeStruct((S, D), bf16),
                   jax.ShapeDtypeStruct((3, D), f32), jax.ShapeDtypeStruct((1, D), f32), jax.ShapeDtypeStruct((1, D), f32)],
        scratch_shapes=[pltpu.VMEM((ts, D), f32), pltpu.VMEM((ts, D), bf16), pltpu.VMEM((ts, D), f32)],
        compiler_params=_cp(("arbitrary", "arbitrary"), 56),
    )(*_hbm(dxo, x, f, g, u, mod3, wg, wu, wd, lng))


def _ffn_wgrad(h, dg, du, a, df, gwg, gwu, gwd, ls):
    S, D = h.shape
    Fs = dg.shape[-1]
    tk = 2 * TOK_TILE
    nk = S // tk

    def body(h_ref, dg_ref, du_ref, a_ref, df_ref, _g0, _g1, _g2, gwg_ref, gwu_ref, gwd_ref, ag_sc, au_sc, ad_sc):
        k = pl.program_id(1)

        @pl.when(k == 0)
        def _():
            ag_sc[...] = jnp.zeros_like(ag_sc)
            au_sc[...] = jnp.zeros_like(au_sc)
            ad_sc[...] = jnp.zeros_like(ad_sc)

        hv = h_ref[...]
        ag_sc[...] += _dot_tn(dg_ref[0], hv)
        au_sc[...] += _dot_tn(du_ref[0], hv)
        ad_sc[...] += _dot_tn(a_ref[0], df_ref[...])

        @pl.when(k == nk - 1)
        def _():
            gwg_ref[0, 0] = ag_sc[...].astype(bf16)
            gwu_ref[0, 0] = au_sc[...].astype(bf16)
            gwd_ref[0, 0] = ad_sc[...].astype(bf16)

    tok = pl.BlockSpec((tk, D), lambda p, k: (k, 0))
    hid = pl.BlockSpec((1, tk, Fs), lambda p, k: (p, k, 0))
    anyspec = pl.BlockSpec(memory_space=pl.ANY)
    orow = pl.BlockSpec((1, 1, Fs, D), lambda p, k: (p, ls, 0, 0))
    return pl.pallas_call(
        body, name="ffn_wgrad", grid=(N_CHIPS, nk),
        in_specs=[tok, hid, hid, hid, tok, anyspec, anyspec, anyspec],
        out_specs=[orow, orow, orow],
        out_shape=[jax.ShapeDtypeStruct(gwg.shape, bf16), jax.ShapeDtypeStruct(gwu.shape, bf16),
                   jax.ShapeDtypeStruct(gwd.shape, bf16)],
        scratch_shapes=[pltpu.VMEM((Fs, D), f32), pltpu.VMEM((Fs, D), f32), pltpu.VMEM((Fs, D), f32)],
        input_output_aliases={5: 0, 6: 1, 7: 2},
        compiler_params=_cp(("parallel", "arbitrary"), 48),
    )(*_hbm(h, dg, du, a, df, gwg, gwu, gwd))


_LANES = 128
_QKV_BLOCKS = D_ATT // _LANES


def _res_spec(lead, d, width, index):
    return pl.BlockSpec((lead, d, TOK_TILE // d, width), index)


def _res_spec3(d, width):
    return pl.BlockSpec((d, TOK_TILE // d, width), lambda i: (0, i, 0))


def _rows_to_residues(tile_bufs, d, put):
    for r in range(d):
        for cb, buf in enumerate(tile_bufs):
            put(r, cb, buf[pl.ds(r, TOK_TILE // d, stride=d), :])


def _residues_to_rows(tile_bufs, d, get):
    for r in range(d):
        for cb, buf in enumerate(tile_bufs):
            buf[pl.ds(r, TOK_TILE // d, stride=d), :] = get(r, cb)


def _mix_in_fwd(x, mod3, w_in, l):
    S, D = x.shape
    N = w_in.shape[-1]
    ts = TOK_TILE

    def body(x_ref, mod_ref, w_ref, o1_ref, o4_ref, o16_ref, zr_ref, h_ref, *bufs):
        j = pl.program_id(1)

        @pl.when(j == 0)
        def _():
            xh, _ = _ln_stats(x_ref[...])
            h_ref[...] = (xh * (1.0 + mod_ref[1:2, :]) + mod_ref[0:1, :]).astype(bf16)

        z = _dot(h_ref[...], w_ref[0, 0])

        @pl.when(j == N_CHIPS - 1)
        def _():
            zr_ref[...] = z

        @pl.when(j < N_CHIPS - 1)
        def _():
            zz = z * jnp.where(j == 0, HEAD_DIM ** -0.5, 1.0)
            o1_ref[j, 0] = zz.astype(bf16)
            for cb, buf in enumerate(bufs):
                buf[...] = zz[:, _LANES * cb:_LANES * (cb + 1)]
            for d, o_ref in zip(DILATIONS[1:], (o4_ref, o16_ref)):
                def put(r, cb, piece, o_ref=o_ref):
                    o_ref[j, r, :, _LANES * cb:_LANES * (cb + 1)] = piece.astype(bf16)
                _rows_to_residues(bufs, d, put)

    tok = pl.BlockSpec((ts, D), lambda i, j: (i, 0))
    res = [_res_spec(3, d, N, lambda i, j: (0, 0, i, 0)) for d in DILATIONS]
    return pl.pallas_call(
        body, name="mix_in_fwd", grid=(S // ts, N_CHIPS),
        in_specs=[tok, _full((3, D)), pl.BlockSpec((1, 1, D, N), lambda i, j: (j, l, 0, 0))],
        out_specs=res + [pl.BlockSpec((ts, N), lambda i, j: (i, 0)), tok],
        out_shape=[jax.ShapeDtypeStruct((3, d, S // d, N), bf16) for d in DILATIONS]
        + [jax.ShapeDtypeStruct((S, N), f32), jax.ShapeDtypeStruct((S, D), bf16)],
        scratch_shapes=[pltpu.VMEM((ts, _LANES), f32)] * _QKV_BLOCKS,
        compiler_params=_cp(("parallel", "arbitrary"), 40),
    )(*_hbm(x, mod3, w_in))


def _mix_in_bwd(dqkv, d_rest, dx_res, x, mod3, w_in, l):
    S, D = x.shape
    N = w_in.shape[-1]
    ts = TOK_TILE

    def body(d1_ref, d4_ref, d16_ref, dr_ref, dxr_ref, x_ref, mod_ref, w_ref, dx_ref, dmod_ref, dz_ref, acc_sc, *bufs):
        i = pl.program_id(0)
        j = pl.program_id(1)

        @pl.when((i == 0) & (j == 0))
        def _():
            dmod_ref[...] = jnp.zeros_like(dmod_ref)

        @pl.when(j == 0)
        def _():
            acc_sc[...] = jnp.zeros_like(acc_sc)

        @pl.when(j == N_CHIPS - 1)
        def _():
            dz_ref[0] = dr_ref[...]

        @pl.when(j < N_CHIPS - 1)
        def _():
            for d, d_ref, tile_bufs in ((4, d4_ref, bufs[:_QKV_BLOCKS]), (16, d16_ref, bufs[_QKV_BLOCKS:])):
                _residues_to_rows(tile_bufs, d, lambda r, cb, d_ref=d_ref: d_ref[0, r, :, _LANES * cb:_LANES * (cb + 1)].astype(f32))
            for cb in range(_QKV_BLOCKS):
                cols = slice(_LANES * cb, _LANES * (cb + 1))
                dz_ref[0, :, cols] = (d1_ref[0, 0, :, cols].astype(f32) + bufs[cb][...] + bufs[_QKV_BLOCKS + cb][...]).astype(bf16)

        acc_sc[...] += _dot_nt(dz_ref[0], w_ref[0, 0])

        @pl.when(j == N_CHIPS - 1)
        def _():
            dh = acc_sc[...]
            xh, rstd0 = _ln_stats(x_ref[...])
            dmod_ref[0:1, :] += jnp.sum(dh, 0, keepdims=True)
            dmod_ref[1:2, :] += jnp.sum(dh * xh, 0, keepdims=True)
            dx_ref[...] = _ln_bwd(dh * (1.0 + mod_ref[1:2, :]), xh, rstd0) + dxr_ref[...]

    tok = pl.BlockSpec((ts, D), lambda i, j: (i, 0))
    res = [_res_spec(1, d, N, lambda i, j: (jnp.minimum(j, 2), 0, i, 0)) for d in DILATIONS]
    return pl.pallas_call(
        body, name="mix_in_bwd", grid=(S // ts, N_CHIPS),
        in_specs=res + [pl.BlockSpec((ts, N), lambda i, j: (i, 0)), tok, tok, _full((3, D)),
                        pl.BlockSpec((1, 1, D, N), lambda i, j: (j, l, 0, 0))],
        out_specs=[tok, _full((3, D)), pl.BlockSpec((1, ts, N), lambda i, j: (j, i, 0))],
        out_shape=[jax.ShapeDtypeStruct((S, D), f32), jax.ShapeDtypeStruct((3, D), f32),
                   jax.ShapeDtypeStruct((N_CHIPS, S, N), bf16)],
        scratch_shapes=[pltpu.VMEM((ts, D), f32)] + [pltpu.VMEM((ts, _LANES), f32)] * (2 * _QKV_BLOCKS),
        compiler_params=_cp(("arbitrary", "arbitrary"), 40),
    )(*_hbm(*dqkv, d_rest, dx_res, x, mod3, w_in))


def _mix_in_wgrad(h, dz, gw, l):
    S, D = h.shape
    N = dz.shape[-1]
    tk = 2 * TOK_TILE
    nk = S // tk

    def body(h_ref, dz_ref, _g, gw_ref, acc_sc):
        k = pl.program_id(1)

        @pl.when(k == 0)
        def _():
            acc_sc[...] = jnp.zeros_like(acc_sc)

        acc_sc[...] += _dot_tn(h_ref[...], dz_ref[0])

        @pl.when(k == nk - 1)
        def _():
            gw_ref[0, 0] = acc_sc[...].astype(bf16)

    return pl.pallas_call(
        body, name="mix_in_wgrad", grid=(N_CHIPS, nk),
        in_specs=[pl.BlockSpec((tk, D), lambda p, k: (k, 0)), pl.BlockSpec((1, tk, N), lambda p, k: (p, k, 0)),
                  pl.BlockSpec(memory_space=pl.ANY)],
        out_specs=pl.BlockSpec((1, 1, D, N), lambda p, k: (p, l, 0, 0)),
        out_shape=jax.ShapeDtypeStruct(gw.shape, bf16),
        scratch_shapes=[pltpu.VMEM((D, N), f32)],
        input_output_aliases={2: 0},
        compiler_params=_cp(("parallel", "arbitrary"), 40),
    )(*_hbm(h, dz, gw))


def _mix_out_fwd(x, y_att, y_ssm, y_pool, mod3, w_out, l, lng, lnb):
    S, D = x.shape
    ts = TOK_TILE

    def body(x_ref, ya_ref, ys_ref, yp_ref, mod_ref, w_ref, lng_ref, lnb_ref, xo_ref, y_ref):
        ya = ya_ref[...].astype(bf16)
        y = (_dot(ya[:, 0:256], w_ref[0, 0]) + _dot(ya[:, 256:512], w_ref[1, 0])
             + _dot(ys_ref[...].astype(bf16), w_ref[2, 0]) + _dot(yp_ref[...].astype(bf16), w_ref[3, 0]))
        y_ref[...] = y
        rh, _ = _ln_stats(ALPHA * x_ref[...] + mod_ref[2:3, :] * y)
        xo_ref[...] = rh * lng_ref[...] + lnb_ref[...]

    tok = pl.BlockSpec((ts, D), lambda i: (i, 0))
    return pl.pallas_call(
        body, name="mix_out_fwd", grid=(S // ts,),
        in_specs=[tok, pl.BlockSpec((ts, D_ATT), lambda i: (i, 0)), pl.BlockSpec((ts, D_SSM), lambda i: (i, 0)),
                  pl.BlockSpec((ts, D_POOL), lambda i: (i, 0)), _full((3, D)),
                  pl.BlockSpec((N_CHIPS, 1, 256, D), lambda i: (0, l, 0, 0)), _full((1, D)), _full((1, D))],
        out_specs=[tok, tok],
        out_shape=[jax.ShapeDtypeStruct((S, D), f32), jax.ShapeDtypeStruct((S, D), f32)],
        compiler_params=_cp(("parallel",), 40),
    )(*_hbm(x, y_att, y_ssm, y_pool, mod3, w_out, lng, lnb))


def _mix_out_bwd(dxo, x, y, y_att, y_ssm, y_pool, mod3, w_out, l, lng, gw_out):
    S, D = x.shape
    ts = TOK_TILE
    nt = S // ts

    def body(dxo_ref, x_ref, y_ref, ya_ref, ys_ref, yp_ref, mod_ref, w_ref, lng_ref, _g,
             dxr_ref, da_ref, ds_ref, dp_ref, dgate_ref, dlng_ref, dlnb_ref, gw_ref, acc_sc):
        i = pl.program_id(0)

        @pl.when(i == 0)
        def _():
            dgate_ref[...] = jnp.zeros_like(dgate_ref)
            dlng_ref[...] = jnp.zeros_like(dlng_ref)
            dlnb_ref[...] = jnp.zeros_like(dlnb_ref)
            acc_sc[...] = jnp.zeros_like(acc_sc)

        gate = mod_ref[2:3, :]
        yv = y_ref[...]
        rh, rstd = _ln_stats(ALPHA * x_ref[...] + gate * yv)
        dy_out = dxo_ref[...]
        dlng_ref[...] += jnp.sum(dy_out * rh, 0, keepdims=True)
        dlnb_ref[...] += jnp.sum(dy_out, 0, keepdims=True)
        dr = _ln_bwd(dy_out * lng_ref[...], rh, rstd)
        dxr_ref[...] = ALPHA * dr
        dgate_ref[...] += jnp.sum(dr * yv, 0, keepdims=True)
        dy = (gate * dr).astype(bf16)
        da_ref[:, 0:256] = _dot_nt(dy, w_ref[0, 0])
        da_ref[:, 256:512] = _dot_nt(dy, w_ref[1, 0])
        ds_ref[...] = _dot_nt(dy, w_ref[2, 0])
        dp_ref[...] = _dot_nt(dy, w_ref[3, 0])
        ya = ya_ref[...].astype(bf16)
        acc_sc[0] += _dot_tn(ya[:, 0:256], dy)
        acc_sc[1] += _dot_tn(ya[:, 256:512], dy)
        acc_sc[2] += _dot_tn(ys_ref[...].astype(bf16), dy)
        acc_sc[3] += _dot_tn(yp_ref[...].astype(bf16), dy)

        @pl.when(i == nt - 1)
        def _():
            gw_ref[:, 0] = acc_sc[...].astype(bf16)

    tok = pl.BlockSpec((ts, D), lambda i: (i, 0))
    t512 = pl.BlockSpec((ts, D_ATT), lambda i: (i, 0))
    t256 = pl.BlockSpec((ts, 256), lambda i: (i, 0))
    wspec = pl.BlockSpec((N_CHIPS, 1, 256, D), lambda i: (0, l, 0, 0))
    return pl.pallas_call(
        body, name="mix_out_bwd", grid=(nt,),
        in_specs=[tok, tok, tok, t512, t256, t256, _full((3, D)), wspec, _full((1, D)), pl.BlockSpec(memory_space=pl.ANY)],
        out_specs=[tok, t512, t256, t256, _full((1, D)), _full((1, D)), _full((1, D)), wspec],
        out_shape=[jax.ShapeDtypeStruct((S, D), f32), jax.ShapeDtypeStruct((S, D_ATT), f32),
                   jax.ShapeDtypeStruct((S, D_SSM), f32), jax.ShapeDtypeStruct((S, D_POOL), f32),
                   jax.ShapeDtypeStruct((1, D), f32), jax.ShapeDtypeStruct((1, D), f32), jax.ShapeDtypeStruct((1, D), f32),
                   jax.ShapeDtypeStruct(gw_out.shape, bf16)],
        scratch_shapes=[pltpu.VMEM((N_CHIPS, 256, D), f32)],
        input_output_aliases={9: 7},
        compiler_params=_cp(("arbitrary",), 48),
    )(*_hbm(dxo, x, y, y_att, y_ssm, y_pool, mod3, w_out, lng, gw_out))


def _t5_bucket(dist):
    max_exact = N_BUCKETS // 2
    d = np.maximum(dist, 1).astype(np.float32)
    large = max_exact + (np.log(d / max_exact) / math.log(MAX_DISTANCE / max_exact)
                         * (N_BUCKETS - max_exact)).astype(np.int32)
    large = np.minimum(large, N_BUCKETS - 1)
    return np.where(dist < max_exact, dist, large).astype(np.int32)


def _bucket_table():
    q = ATT_BLOCK
    i = np.arange(q)[:, None]
    j = np.arange(2 * q)[None, :]
    r = i + q - j
    in_band = (r >= 0) & (r <= q)
    tabs = [np.where(in_band, _t5_bucket(np.clip(r, 0, None) * d), -1) for d in DILATIONS]
    return np.stack(tabs).astype(np.int32)


def _bias_fwd(rel_bias, table):
    def body(rb_ref, tab_ref, out_ref):
        for b in range(3):
            tb = tab_ref[b]
            for h in range(N_HEADS):
                def pick(k, acc):
                    return jnp.where(tb == k, rb_ref[k, h], acc)
                out_ref[b, h] = lax.fori_loop(0, N_BUCKETS, pick, jnp.where(tb < 0, NEG, 0.0).astype(f32))

    return pl.pallas_call(
        body, name="bias_fwd",
        in_specs=[pl.BlockSpec(memory_space=pltpu.SMEM), pl.BlockSpec(memory_space=pltpu.VMEM)],
        out_specs=pl.BlockSpec(memory_space=pltpu.VMEM),
        out_shape=jax.ShapeDtypeStruct((3, N_HEADS, ATT_BLOCK, 2 * ATT_BLOCK), f32),
    )(rel_bias, table)


def _bias_bwd(dbias, table):
    def body(db_ref, tab_ref, out_ref):
        def per_bucket(k, c):
            for h in range(N_HEADS):
                tot = jnp.zeros((), f32)
                for b in range(3):
                    tot = tot + jnp.sum(jnp.where(tab_ref[b] == k, db_ref[b, h], 0.0))
                out_ref[k, h] = tot
            return c
        lax.fori_loop(0, N_BUCKETS, per_bucket, 0)

    return pl.pallas_call(
        body, name="bias_bwd",
        in_specs=[pl.BlockSpec(memory_space=pltpu.VMEM), pl.BlockSpec(memory_space=pltpu.VMEM)],
        out_specs=pl.BlockSpec(memory_space=pltpu.SMEM),
        out_shape=jax.ShapeDtypeStruct((N_BUCKETS, N_HEADS), f32),
    )(dbias, table)


def _att_unit(u, nbr):
    rows = pl.ds(pl.multiple_of(u * ATT_BLOCK, ATT_BLOCK), ATT_BLOCK)
    prev = pl.ds(pl.multiple_of(jnp.maximum(u - 1, 0) * ATT_BLOCK, ATT_BLOCK), ATT_BLOCK)
    return rows, prev, (u % nbr) != 0


_N_PAIRS = N_HEADS // 2


def _pair_rows(t):
    lane = lax.broadcasted_iota(jnp.int32, t.shape, 1)
    zero = jnp.zeros_like(t)
    return jnp.concatenate([jnp.where(lane < HEAD_DIM, t, zero), jnp.where(lane >= HEAD_DIM, t, zero)], axis=0)


def _pair_cols(big):
    lane = lax.broadcasted_iota(jnp.int32, (ATT_BLOCK, _LANES), 1)
    return jnp.where(lane < HEAD_DIM, big[:ATT_BLOCK], big[ATT_BLOCK:])


def _pair_column(ref, rows, hp):
    t = ref[rows, :]
    return jnp.concatenate([t[:, 2 * hp:2 * hp + 1], t[:, 2 * hp + 1:2 * hp + 2]], axis=0)


def _pair_band(ref, rows, prev, nbr, hp):
    lanes = pl.ds(_LANES * hp, _LANES)
    cur = ref[0, rows, lanes]
    return cur if nbr == 1 else jnp.concatenate([ref[0, prev, lanes], cur], axis=0)


def _pair_scores(q_ref, k_ref, b_ref, rows, prev, valid_prev, nbr, hp):
    qbd = _pair_rows(q_ref[0, rows, pl.ds(_LANES * hp, _LANES)])
    kb = _pair_band(k_ref, rows, prev, nbr, hp)
    bias = b_ref[0, 2 * hp:2 * hp + 2].reshape(2 * ATT_BLOCK, 2 * ATT_BLOCK)
    if nbr == 1:
        return qbd, kb, _dot_nt(qbd, kb) + bias[:, ATT_BLOCK:]
    s = _dot_nt(qbd, kb) + bias
    col = lax.broadcasted_iota(jnp.int32, s.shape, 1)
    return qbd, kb, jnp.where((col >= ATT_BLOCK) | valid_prev, s, NEG)


def _qkv_specs(S, branch):
    return ([pl.BlockSpec((1, S, D_ATT), lambda i, t=t: (t, 0, 0)) for t in range(3)],
            pl.BlockSpec((1, N_HEADS, ATT_BLOCK, 2 * ATT_BLOCK), lambda i: (branch, 0, 0, 0)))


def _att_fwd(qkv, bias, branch):
    S = qkv.shape[1]
    nbr = BLOCKS_PER_RESIDUE[branch]

    def body(q_ref, k_ref, v_ref, b_ref, o_ref, lse_ref):
        lse_ref[...] = jnp.zeros_like(lse_ref)

        def unit(u, c):
            rows, prev, valid_prev = _att_unit(u, nbr)
            for hp in range(_N_PAIRS):
                _, _, s = _pair_scores(q_ref, k_ref, b_ref, rows, prev, valid_prev, nbr, hp)
                m = jnp.max(s, -1, keepdims=True)
                p = jnp.exp(s - m)
                den = jnp.sum(p, -1, keepdims=True)
                big = _dot(p.astype(bf16), _pair_band(v_ref, rows, prev, nbr, hp))
                o_ref[rows, pl.ds(_LANES * hp, _LANES)] = _pair_cols(big / den)
                lse = m + jnp.log(den)
                lse_ref[rows, pl.ds(2 * hp, 1)] = lse[:ATT_BLOCK]
                lse_ref[rows, pl.ds(2 * hp + 1, 1)] = lse[ATT_BLOCK:]
            return c

        lax.fori_loop(0, N_UNITS, unit, 0)

    qkv_specs, bspec = _qkv_specs(S, branch)
    return pl.pallas_call(
        body, name="att_fwd", grid=(1,),
        in_specs=qkv_specs + [bspec],
        out_specs=[pl.BlockSpec((S, D_ATT), lambda i: (0, 0)), pl.BlockSpec((S, _LANES), lambda i: (0, 0))],
        out_shape=[jax.ShapeDtypeStruct((S, D_ATT), f32), jax.ShapeDtypeStruct((S, _LANES), f32)],
        compiler_params=_cp(("arbitrary",), 40),
    )(*_hbm(qkv, qkv, qkv, bias))


def _att_bwd(qkv, do, lse, crow, bias, branch):
    S = qkv.shape[1]
    nbr = BLOCKS_PER_RESIDUE[branch]

    def body(q_ref, k_ref, v_ref, do_ref, lse_ref, c_ref, b_ref, dqkv_ref, db_ref, dk_sc, dv_sc):
        dk_sc[...] = jnp.zeros_like(dk_sc)
        dv_sc[...] = jnp.zeros_like(dv_sc)
        db_ref[...] = jnp.zeros_like(db_ref)

        def unit(u, c):
            rows, prev, valid_prev = _att_unit(u, nbr)
            for hp in range(_N_PAIRS):
                lanes = pl.ds(_LANES * hp, _LANES)
                qbd, kb, s = _pair_scores(q_ref, k_ref, b_ref, rows, prev, valid_prev, nbr, hp)
                p = jnp.exp(s - _pair_column(lse_ref, rows, hp))
                dobd = _pair_rows(do_ref[rows, lanes])
                ds = p * (_dot_nt(dobd, _pair_band(v_ref, rows, prev, nbr, hp)) - _pair_column(c_ref, rows, hp))
                if nbr == 1:
                    db_ref[2 * hp:2 * hp + 2, :, ATT_BLOCK:] += ds.reshape(2, ATT_BLOCK, ATT_BLOCK)
                else:
                    db_ref[2 * hp:2 * hp + 2] += ds.reshape(2, ATT_BLOCK, 2 * ATT_BLOCK)
                dsb = ds.astype(bf16)
                dqkv_ref[0, rows, lanes] = (HEAD_DIM ** -0.5 * _pair_cols(_dot(dsb, kb))).astype(bf16)
                dkb = _dot_tn(dsb, qbd)
                dvb = _dot_tn(p.astype(bf16), dobd)
                if nbr == 1:
                    dk_sc[rows, lanes] += dkb
                    dv_sc[rows, lanes] += dvb
                else:
                    dk_sc[prev, lanes] += dkb[:ATT_BLOCK]
                    dv_sc[prev, lanes] += dvb[:ATT_BLOCK]
                    dk_sc[rows, lanes] += dkb[ATT_BLOCK:]
                    dv_sc[rows, lanes] += dvb[ATT_BLOCK:]
            return c

        lax.fori_loop(0, N_UNITS, unit, 0)
        dqkv_ref[1] = dk_sc[...].astype(bf16)
        dqkv_ref[2] = dv_sc[...].astype(bf16)

    qkv_specs, bspec = _qkv_specs(S, branch)
    row = pl.BlockSpec((S, _LANES), lambda i: (0, 0))
    return pl.pallas_call(
        body, name="att_bwd", grid=(1,),
        in_specs=qkv_specs + [pl.BlockSpec((S, D_ATT), lambda i: (0, 0)), row, row, bspec],
        out_specs=[pl.BlockSpec((3, S, D_ATT), lambda i: (0, 0, 0)),
                   pl.BlockSpec((N_HEADS, ATT_BLOCK, 2 * ATT_BLOCK), lambda i: (0, 0, 0))],
        out_shape=[jax.ShapeDtypeStruct((3, S, D_ATT), bf16), jax.ShapeDtypeStruct((N_HEADS, ATT_BLOCK, 2 * ATT_BLOCK), f32)],
        scratch_shapes=[pltpu.VMEM((S, D_ATT), f32), pltpu.VMEM((S, D_ATT), f32)],
        compiler_params=_cp(("arbitrary",), 48),
    )(*_hbm(qkv, qkv, qkv, do, lse, crow, bias))


def _branch_weights(lse_ref):
    l0, l1, l2 = lse_ref[0], lse_ref[1], lse_ref[2]
    m = jnp.maximum(jnp.maximum(l0, l1), l2)
    e0, e1, e2 = jnp.exp(l0 - m), jnp.exp(l1 - m), jnp.exp(l2 - m)
    tot = e0 + e1 + e2
    return e0 / tot, e1 / tot, e2 / tot


def _att_merge(os, lses):
    S = os[0].shape[0] * os[0].shape[1]
    ts = TOK_TILE

    def body(o1_ref, o4_ref, o16_ref, l1_ref, l4_ref, l16_ref, y_ref, lt_ref, *bufs):
        obufs = (bufs[:_QKV_BLOCKS], bufs[_QKV_BLOCKS:2 * _QKV_BLOCKS])
        lt_ref[0] = l1_ref[0]
        for k, (d, o_ref, l_ref) in enumerate(((4, o4_ref, l4_ref), (16, o16_ref, l16_ref))):
            _residues_to_rows(obufs[k], d, lambda r, cb, o_ref=o_ref: o_ref[r, :, _LANES * cb:_LANES * (cb + 1)])
            _residues_to_rows([bufs[2 * _QKV_BLOCKS + k]], d, lambda r, cb, l_ref=l_ref: l_ref[r])
            lt_ref[1 + k] = bufs[2 * _QKV_BLOCKS + k][...]
        w = _branch_weights(lt_ref)
        for h in range(N_HEADS):
            cs = slice(HEAD_DIM * h, HEAD_DIM * (h + 1))
            half = slice(HEAD_DIM * (h % 2), HEAD_DIM * (h % 2 + 1))
            y_ref[:, cs] = (w[0][:, h:h + 1] * o1_ref[0, :, cs] + w[1][:, h:h + 1] * obufs[0][h // 2][:, half]
                            + w[2][:, h:h + 1] * obufs[1][h // 2][:, half])

    return pl.pallas_call(
        body, name="att_merge", grid=(S // ts,),
        in_specs=[_res_spec3(d, D_ATT) for d in DILATIONS] + [_res_spec3(d, _LANES) for d in DILATIONS],
        out_specs=[pl.BlockSpec((ts, D_ATT), lambda i: (i, 0)), pl.BlockSpec((3, ts, _LANES), lambda i: (0, i, 0))],
        out_shape=[jax.ShapeDtypeStruct((S, D_ATT), f32), jax.ShapeDtypeStruct((3, S, _LANES), f32)],
        scratch_shapes=[pltpu.VMEM((ts, _LANES), f32)] * (2 * _QKV_BLOCKS + 2),
        compiler_params=_cp(("parallel",)),
    )(*_hbm(*os, *lses))


def _att_merge_bwd(dy, y, lse3):
    S = dy.shape[0]
    ts = TOK_TILE

    def body(dy_ref, y_ref, lse_ref, do1_ref, do4_ref, do16_ref, c1_ref, c4_ref, c16_ref, *bufs):
        dobufs = (bufs[:_QKV_BLOCKS], bufs[_QKV_BLOCKS:2 * _QKV_BLOCKS], bufs[2 * _QKV_BLOCKS:3 * _QKV_BLOCKS])
        cbufs = bufs[3 * _QKV_BLOCKS:]
        w = _branch_weights(lse_ref)
        for cb in cbufs:
            cb[...] = jnp.zeros_like(cb)
        for h in range(N_HEADS):
            cs = slice(HEAD_DIM * h, HEAD_DIM * (h + 1))
            half = slice(HEAD_DIM * (h % 2), HEAD_DIM * (h % 2 + 1))
            dyh = dy_ref[:, cs]
            t = jnp.sum(dyh * y_ref[:, cs], -1, keepdims=True)
            for p in range(3):
                wp = w[p][:, h:h + 1]
                dobufs[p][h // 2][:, half] = wp * dyh
                cbufs[p][:, h:h + 1] = wp * t
        for cb in range(_QKV_BLOCKS):
            do1_ref[0, :, _LANES * cb:_LANES * (cb + 1)] = dobufs[0][cb][...].astype(bf16)
        c1_ref[0] = cbufs[0][...]
        for k, (d, do_ref, c_ref) in enumerate(((4, do4_ref, c4_ref), (16, do16_ref, c16_ref))):
            def put_do(r, cb, piece, do_ref=do_ref):
                do_ref[r, :, _LANES * cb:_LANES * (cb + 1)] = piece.astype(bf16)

            def put_c(r, cb, piece, c_ref=c_ref):
                c_ref[r] = piece

            _rows_to_residues(dobufs[1 + k], d, put_do)
            _rows_to_residues([cbufs[1 + k]], d, put_c)

    return pl.pallas_call(
        body, name="att_merge_bwd", grid=(S // ts,),
        in_specs=[pl.BlockSpec((ts, D_ATT), lambda i: (i, 0)), pl.BlockSpec((ts, D_ATT), lambda i: (i, 0)),
                  pl.BlockSpec((3, ts, _LANES), lambda i: (0, i, 0))],
        out_specs=[_res_spec3(d, D_ATT) for d in DILATIONS] + [_res_spec3(d, _LANES) for d in DILATIONS],
        out_shape=[jax.ShapeDtypeStruct((d, S // d, D_ATT), bf16) for d in DILATIONS]
        + [jax.ShapeDtypeStruct((d, S // d, _LANES), f32) for d in DILATIONS],
        scratch_shapes=[pltpu.VMEM((ts, _LANES), f32)] * (3 * _QKV_BLOCKS + 3),
        compiler_params=_cp(("parallel",)),
    )(*_hbm(dy, y, lse3))


_SSM_ROWS = 256


def _scan_in_place(sr_ref, si_ref, a_ref, reverse):
    S, N = sr_ref.shape
    nst = S // SCAN_SEG
    ar = jnp.broadcast_to(a_ref[0:1, :], (SCAN_SEG, N))
    ai = jnp.broadcast_to(a_ref[1:2, :], (SCAN_SEG, N))
    if reverse:
        ai = -ai
    row = lax.broadcasted_iota(jnp.int32, (SCAN_SEG, N), 0)
    zero = jnp.zeros((SCAN_SEG, N), f32)

    def tile(t):
        return pl.ds(pl.multiple_of((nst - 1 - t if reverse else t) * SCAN_SEG, SCAN_SEG), SCAN_SEG)

    def local(t, c):
        sr, si, pr, pi = c
        rows = tile(t)
        nsr = ar * sr - ai * si + sr_ref[rows, :]
        nsi = ar * si + ai * sr + si_ref[rows, :]
        sr_ref[rows, :] = nsr
        si_ref[rows, :] = nsi
        return nsr, nsi, ar * pr - ai * pi, ar * pi + ai * pr

    fr, fi, apr, api = lax.fori_loop(0, nst, local, (zero, zero, zero + 1.0, zero))

    def shift(v):
        if reverse:
            return jnp.where(row == SCAN_SEG - 1, 0.0, pltpu.roll(v, SCAN_SEG - 1, axis=0))
        return jnp.where(row == 0, 0.0, pltpu.roll(v, 1, axis=0))

    cr, ci = zero, zero
    for _ in range(SCAN_SEG - 1):
        cr, ci = shift(fr + apr * cr - api * ci), shift(fi + apr * ci + api * cr)

    def fix(t, c):
        pr, pi = c
        npr, npi = ar * pr - ai * pi, ar * pi + ai * pr
        rows = tile(t)
        sr_ref[rows, :] += npr * cr - npi * ci
        si_ref[rows, :] += npr * ci + npi * cr
        return npr, npi

    lax.fori_loop(0, nst, fix, (zero + 1.0, zero))


def _ssm_states(u, bre, bim, a2, l):
    S = u.shape[0]

    def body(u_ref, br_ref, bi_ref, a2_ref, sr_ref, si_ref):
        a_ref = a2_ref.at[l]
        brb = br_ref[l].astype(bf16)
        bib = bi_ref[l].astype(bf16)

        def project(t, c):
            rows = pl.ds(pl.multiple_of(t * _SSM_ROWS, _SSM_ROWS), _SSM_ROWS)
            ub = u_ref[rows, :].astype(bf16)
            sr_ref[rows, :] = _dot(ub, brb)
            si_ref[rows, :] = _dot(ub, bib)
            return c

        lax.fori_loop(0, S // _SSM_ROWS, project, 0)
        _scan_in_place(sr_ref, si_ref, a_ref, False)

    vm = pl.BlockSpec(memory_space=pltpu.VMEM)
    return pl.pallas_call(
        body, name="ssm_states", in_specs=[vm] * 4, out_specs=[vm, vm],
        out_shape=[jax.ShapeDtypeStruct((S, D_STATE), f32)] * 2,
        compiler_params=_cp(None, 48),
    )(u, bre, bim, a2)


def _ssm_out(sr, si, u, cre, cim, l, dskip, glu_w, glu_b):
    S = u.shape[0]
    ts = TOK_TILE

    def body(sr_ref, si_ref, u_ref, cr_ref, ci_ref, d_ref, w_ref, b_ref, out_ref, y_ref):
        y = (_dot(sr_ref[...].astype(bf16), cr_ref[0].astype(bf16))
             - _dot(si_ref[...].astype(bf16), ci_ref[0].astype(bf16)) + d_ref[...] * u_ref[...])
        y_ref[...] = y
        z = _dot(_gelu(y).astype(bf16), w_ref[...].astype(bf16)) + b_ref[...]
        out_ref[...] = y * jax.nn.sigmoid(z)

    st = pl.BlockSpec((ts, D_STATE), lambda i: (i, 0))
    ch = pl.BlockSpec((ts, D_SSM), lambda i: (i, 0))
    c_l = pl.BlockSpec((1, D_STATE, D_SSM), lambda i: (l, 0, 0))
    return pl.pallas_call(
        body, name="ssm_out", grid=(S // ts,),
        in_specs=[st, st, ch, c_l, c_l, _full((1, D_SSM)), _full((D_SSM, D_SSM)), _full((1, D_SSM))],
        out_specs=[ch, ch],
        out_shape=[jax.ShapeDtypeStruct((S, D_SSM), f32)] * 2,
        compiler_params=_cp(("parallel",)),
    )(*_hbm(sr, si, u, cre, cim, dskip, glu_w, glu_b))


def _ssm_out_bwd(dout, y, u, dskip, glu_w, glu_b):
    S = u.shape[0]
    ts = TOK_TILE

    def body(do_ref, y_ref, u_ref, d_ref, w_ref, b_ref, dy_ref, du_ref, dd_ref, dgb_ref, dgw_ref):
        @pl.when(pl.program_id(0) == 0)
        def _():
            for r in (dd_ref, dgb_ref, dgw_ref):
                r[...] = jnp.zeros_like(r)

        y = y_ref[...]
        dout = do_ref[...]
        wb = w_ref[...].astype(bf16)
        ge = _gelu(y).astype(bf16)
        sz = jax.nn.sigmoid(_dot(ge, wb) + b_ref[...])
        dz = dout * y * sz * (1.0 - sz)
        dzb = dz.astype(bf16)
        dgb_ref[...] += jnp.sum(dz, 0, keepdims=True)
        dgw_ref[...] += _dot_tn(ge, dzb)
        dy = dout * sz + _gelu_grad(y) * _dot_nt(dzb, wb)
        uv = u_ref[...]
        dd_ref[...] += jnp.sum(dy * uv, 0, keepdims=True)
        du_ref[...] = dy * d_ref[...]
        dy_ref[...] = dy

    ch = pl.BlockSpec((ts, D_SSM), lambda i: (i, 0))
    return pl.pallas_call(
        body, name="ssm_out_bwd", grid=(S // ts,),
        in_specs=[ch, ch, ch, _full((1, D_SSM)), _full((D_SSM, D_SSM)), _full((1, D_SSM))],
        out_specs=[ch, ch, _full((1, D_SSM)), _full((1, D_SSM)), _full((D_SSM, D_SSM))],
        out_shape=[jax.ShapeDtypeStruct((S, D_SSM), f32), jax.ShapeDtypeStruct((S, D_SSM), f32),
                   jax.ShapeDtypeStruct((1, D_SSM), f32), jax.ShapeDtypeStruct((1, D_SSM), f32),
                   jax.ShapeDtypeStruct((D_SSM, D_SSM), f32)],
        compiler_params=_cp(("arbitrary",), 40),
    )(*_hbm(dout, y, u, dskip, glu_w, glu_b))


def _ssm_states_bwd(dy, du_skip, u, sr, si, cre, cim, bre, bim, a2, l):
    S = u.shape[0]
    N = D_STATE
    nst = S // SCAN_SEG
    nproj = S // _SSM_ROWS

    def body(dy_ref, dus_ref, u_ref, sr_ref, si_ref, cr_ref, ci_ref, br_ref, bi_ref, a2_ref,
             du_ref, dbr_ref, dbi_ref, da_ref, dcr_ref, dci_ref, lr_ref, li_ref):
        a_ref = a2_ref.at[l]
        crb = cr_ref[l].astype(bf16)
        cib = ci_ref[l].astype(bf16)
        dcr_ref[...] = jnp.zeros_like(dcr_ref)
        dci_ref[...] = jnp.zeros_like(dci_ref)

        def project(t, c):
            rows = pl.ds(pl.multiple_of(t * _SSM_ROWS, _SSM_ROWS), _SSM_ROWS)
            dyb = dy_ref[rows, :].astype(bf16)
            lr_ref[rows, :] = _dot_nt(dyb, crb)
            li_ref[rows, :] = -_dot_nt(dyb, cib)
            dcr_ref[...] += _dot_tn(sr_ref[rows, :].astype(bf16), dyb)
            dci_ref[...] -= _dot_tn(si_ref[rows, :].astype(bf16), dyb)
            return c

        lax.fori_loop(0, nproj, project, 0)
        _scan_in_place(lr_ref, li_ref, a_ref, True)

        row = lax.broadcasted_iota(jnp.int32, (SCAN_SEG, N), 0)
        last = pl.ds((nst - 1) * SCAN_SEG, SCAN_SEG)
        pr = jnp.where(row == 0, 0.0, pltpu.roll(sr_ref[last, :], 1, axis=0))
        pi = jnp.where(row == 0, 0.0, pltpu.roll(si_ref[last, :], 1, axis=0))
        first = pl.ds(0, SCAN_SEG)
        acc_r = lr_ref[first, :] * pr + li_ref[first, :] * pi
        acc_i = li_ref[first, :] * pr - lr_ref[first, :] * pi

        def step(t, c):
            acc_r, acc_i = c
            rows = pl.ds(pl.multiple_of(t * SCAN_SEG, SCAN_SEG), SCAN_SEG)
            prev = pl.ds(pl.multiple_of((t - 1) * SCAN_SEG, SCAN_SEG), SCAN_SEG)
            lrv, liv, srv, siv = lr_ref[rows, :], li_ref[rows, :], sr_ref[prev, :], si_ref[prev, :]
            return acc_r + lrv * srv + liv * siv, acc_i + liv * srv - lrv * siv

        acc_r, acc_i = lax.fori_loop(1, nst, step, (acc_r, acc_i))
        da_ref[0:1, :] = jnp.sum(acc_r, 0, keepdims=True)
        da_ref[1:2, :] = jnp.sum(acc_i, 0, keepdims=True)

        brb = br_ref[l].astype(bf16)
        bib = bi_ref[l].astype(bf16)
        dbr_ref[...] = jnp.zeros_like(dbr_ref)
        dbi_ref[...] = jnp.zeros_like(dbi_ref)

        def back(t, c):
            rows = pl.ds(pl.multiple_of(t * _SSM_ROWS, _SSM_ROWS), _SSM_ROWS)
            lrb = lr_ref[rows, :].astype(bf16)
            lib = li_ref[rows, :].astype(bf16)
            du_ref[rows, :] = dus_ref[rows, :] + _dot_nt(lrb, brb) + _dot_nt(lib, bib)
            ub = u_ref[rows, :].astype(bf16)
            dbr_ref[...] += _dot_tn(ub, lrb)
            dbi_ref[...] += _dot_tn(ub, lib)
            return c

        lax.fori_loop(0, nproj, back, 0)

    vm = pl.BlockSpec(memory_space=pltpu.VMEM)
    return pl.pallas_call(
        body, name="ssm_states_bwd", in_specs=[vm] * 10, out_specs=[vm] * 6,
        out_shape=[jax.ShapeDtypeStruct((S, D_SSM), f32), jax.ShapeDtypeStruct((D_SSM, D_STATE), f32),
                   jax.ShapeDtypeStruct((D_SSM, D_STATE), f32), jax.ShapeDtypeStruct((2, D_STATE), f32),
                   jax.ShapeDtypeStruct((D_STATE, D_SSM), f32), jax.ShapeDtypeStruct((D_STATE, D_SSM), f32)],
        scratch_shapes=[pltpu.VMEM((S, D_STATE), f32), pltpu.VMEM((S, D_STATE), f32)],
        compiler_params=_cp(None, 56),
    )(dy, du_skip, u, sr, si, cre, cim, bre, bim, a2)


_POOL_TILE = 256


def _window_sums(xt, back):
    n = xt.shape[0]
    out = []
    ws = xt
    for k in (1, 2, 4, 8):
        ws = ws + pltpu.roll(ws, k if back else n - k, axis=0)
        out.append(ws)
    return out


def _pool_count(r0, w):
    t = r0 + lax.broadcasted_iota(jnp.int32, (_POOL_TILE, POOL_GROUP), 0)
    return jnp.minimum(t + 1, w).astype(f32)


def _pool_fwd(u_pad, pool_w, pool_scale):
    S = u_pad.shape[0] - POOL_HALO
    nt = S // _POOL_TILE

    def body(u_ref, w_ref, sc_ref, y_ref):
        def tile(t, c):
            r0 = pl.multiple_of(t * _POOL_TILE, _POOL_TILE)
            for g, w in enumerate(POOL_WINDOWS):
                cs = pl.ds(POOL_GROUP * g, POOL_GROUP)
                xt = u_ref[pl.ds(r0, _POOL_TILE + POOL_HALO), cs]
                ws = _window_sums(xt, True)[g][POOL_HALO:, :]
                pooled = ws / _pool_count(r0, w) - xt[POOL_HALO:, :]
                y_ref[pl.ds(r0, _POOL_TILE), cs] = _dot(pooled.astype(bf16), w_ref[g].astype(bf16)) * sc_ref[:, cs]
            return c
        lax.fori_loop(0, nt, tile, 0)

    vm = pl.BlockSpec(memory_space=pltpu.VMEM)
    return pl.pallas_call(
        body, name="pool_fwd", in_specs=[vm, vm, vm], out_specs=vm,
        out_shape=jax.ShapeDtypeStruct((S, D_POOL), f32),
    )(u_pad, pool_w, pool_scale)


def _pool_bwd(dy_pad, u_pad, pool_w, pool_scale):
    S = u_pad.shape[0] - POOL_HALO
    nt = S // _POOL_TILE
    n = _POOL_TILE + POOL_HALO

    def body(dy_ref, u_ref, w_ref, sc_ref, du_ref, dw_ref, dsc_ref):
        dw_ref[...] = jnp.zeros_like(dw_ref)
        dsc_ref[...] = jnp.zeros_like(dsc_ref)

        def tile(t, c):
            r0 = pl.multiple_of(t * _POOL_TILE, _POOL_TILE)
            for g, w in enumerate(POOL_WINDOWS):
                cs = pl.ds(POOL_GROUP * g, POOL_GROUP)
                wb = w_ref[g].astype(bf16)
                xt = u_ref[pl.ds(r0, n), cs]
                pooled = (_window_sums(xt, True)[g][POOL_HALO:, :] / _pool_count(r0, w) - xt[POOL_HALO:, :]).astype(bf16)
                dy = dy_ref[pl.ds(r0, _POOL_TILE), cs]
                dsc_ref[:, cs] += jnp.sum(dy * _dot(pooled, wb), 0, keepdims=True)
                dw_ref[g] += _dot_tn(pooled, (dy * sc_ref[:, cs]).astype(bf16))
                dyh = (dy_ref[pl.ds(r0, n), cs] * sc_ref[:, cs]).astype(bf16)
                dpl = _dot_nt(dyh, wb)
                cnt = jnp.minimum(r0 + lax.broadcasted_iota(jnp.int32, (n, POOL_GROUP), 0) + 1, w).astype(f32)
                lead = _window_sums(dpl / cnt, False)[g]
                du_ref[pl.ds(r0, _POOL_TILE), cs] = lead[:_POOL_TILE, :] - dpl[:_POOL_TILE, :]
            return c
        lax.fori_loop(0, nt, tile, 0)

    vm = pl.BlockSpec(memory_space=pltpu.VMEM)
    return pl.pallas_call(
        body, name="pool_bwd", in_specs=[vm, vm, vm, vm], out_specs=[vm, vm, vm],
        out_shape=[jax.ShapeDtypeStruct((S, D_POOL), f32), jax.ShapeDtypeStruct((4, POOL_GROUP, POOL_GROUP), f32),
                   jax.ShapeDtypeStruct((1, D_POOL), f32)],
    )(dy_pad, u_pad, pool_w, pool_scale)


def _loss_head(y, target):
    S, D = y.shape
    ts = TOK_TILE

    def body(y_ref, t_ref, loss_ref, dy_ref):
        @pl.when(pl.program_id(0) == 0)
        def _():
            loss_ref[...] = jnp.zeros_like(loss_ref)

        d = y_ref[...] - t_ref[...]
        dy_ref[...] = d * (1.0 / D)
        loss_ref[...] += 0.5 * jnp.sum(jnp.sum(d * d, -1, keepdims=True) * (1.0 / D), 0, keepdims=True)

    tok = pl.BlockSpec((ts, D), lambda i: (i, 0))
    return pl.pallas_call(
        body, name="loss_head", grid=(S // ts,),
        in_specs=[tok, tok], out_specs=[_full((1, 1)), tok],
        out_shape=[jax.ShapeDtypeStruct((1, 1), f32), jax.ShapeDtypeStruct((S, D), f32)],
        compiler_params=_cp(("arbitrary",)),
    )(*_hbm(y, target))


_ADA_COLS = 768


def _ada_fwd(c_all, ada_w, ada_b_cols):
    L, D, N = ada_w.shape
    B = c_all.shape[0]

    def body(c_ref, w_ref, b_ref, out_ref):
        cv = c_ref[...]
        cond = (cv * jax.nn.sigmoid(cv)).astype(bf16)
        out_ref[0] = _dot(cond, w_ref[0].astype(bf16)) + b_ref[0]

    return pl.pallas_call(
        body, name="ada_fwd", grid=(L, N // _ADA_COLS),
        in_specs=[_full((B, D)), pl.BlockSpec((1, D, _ADA_COLS), lambda l, j: (l, 0, j)),
                  pl.BlockSpec((1, 1, _ADA_COLS), lambda l, j: (l, 0, j))],
        out_specs=pl.BlockSpec((1, B, _ADA_COLS), lambda l, j: (l, 0, j)),
        out_shape=jax.ShapeDtypeStruct((L, B, N), f32),
        compiler_params=_cp(("parallel", "parallel")),
    )(c_all, ada_w, ada_b_cols)


def _ada_wgrad(c_all_t, dmod_cols):
    D, B = c_all_t.shape
    L, _, N = dmod_cols.shape

    def body(ct_ref, dm_ref, out_ref):
        cv = ct_ref[...]
        cond = cv * jax.nn.sigmoid(cv)
        acc = cond[:, 0:1] * dm_ref[0, 0:1, :]
        for b in range(1, B):
            acc = acc + cond[:, b:b + 1] * dm_ref[0, b:b + 1, :]
        out_ref[0] = acc

    return pl.pallas_call(
        body, name="ada_wgrad", grid=(L, N // _ADA_COLS),
        in_specs=[_full((D, B)), pl.BlockSpec((1, B, _ADA_COLS), lambda l, j: (l, 0, j))],
        out_specs=pl.BlockSpec((1, D, _ADA_COLS), lambda l, j: (l, 0, j)),
        out_shape=jax.ShapeDtypeStruct((L, D, N), f32),
        compiler_params=_cp(("parallel", "parallel")),
    )(c_all_t, dmod_cols)


def _adam_math(w, g, m, v):
    m = ADAM_B1 * m + (1.0 - ADAM_B1) * g
    v = ADAM_B2 * v + (1.0 - ADAM_B2) * (g * g)
    m_hat = m / (1.0 - ADAM_B1 ** ADAM_STEP)
    v_hat = v / (1.0 - ADAM_B2 ** ADAM_STEP)
    delta = -ADAM_LR * (m_hat / (jnp.sqrt(v_hat) + ADAM_EPS) + ADAM_WD * w)
    return delta, m, v


def _adamw(w, m, v, g, row_tile, row0=0, outs=None):
    R, C = w.shape
    b0 = row0 // row_tile

    def body(w_ref, m_ref, v_ref, g_ref, _0, _1, _2, _3, g_out, d_out, m_out, v_out):
        gv = g_ref[...]
        delta, mn, vn = _adam_math(w_ref[...], gv, m_ref[...], v_ref[...])
        g_out[...] = gv
        d_out[...] = delta
        m_out[...] = mn
        v_out[...] = vn

    pspec = pl.BlockSpec((row_tile, C), lambda i: (b0 + i, 0))
    gspec = pl.BlockSpec((row_tile, C), lambda i: (i, 0))
    anyspec = pl.BlockSpec(memory_space=pl.ANY)
    shp = jax.ShapeDtypeStruct((R, C), f32)
    if outs is None:
        outs = [lax.empty((R, C), f32) for _ in range(4)]
    return pl.pallas_call(
        body, name="adamw", grid=(g.shape[0] // row_tile,),
        in_specs=[pspec] * 3 + [gspec] + [anyspec] * 4, out_specs=[pspec] * 4, out_shape=[shp] * 4,
        input_output_aliases={4: 0, 5: 1, 6: 2, 7: 3},
        compiler_params=_cp(("parallel",), 40),
    )(*_hbm(w, m, v, g, *outs))


def _pair_sum(g5s, gots, pc):
    n = len(g5s)

    def body(pc_ref, *refs):
        for own, got, out in zip(refs[:n], refs[n:2 * n], refs[2 * n:]):
            out[0, 0] = (own[0, 0, 0].astype(f32) + got[0, 0].astype(f32)).astype(bf16)

    def half(g):
        return pl.BlockSpec((1, 1) + g.shape[-2:], lambda p, pc: (p, 0, 0, 0))

    gs = pltpu.PrefetchScalarGridSpec(
        num_scalar_prefetch=1, grid=(N_CHIPS,),
        in_specs=[pl.BlockSpec((1, 1, 1) + g.shape[-2:], lambda p, pc: (p, 0, pc[1], 0, 0)) for g in g5s]
        + [half(g) for g in gots],
        out_specs=[half(g) for g in gots],
    )
    return pl.pallas_call(
        body, name="pair_sum", grid_spec=gs, out_shape=[jax.ShapeDtypeStruct(g.shape, bf16) for g in gots],
        compiler_params=_cp(("parallel",), 48),
    )(pc, *_hbm(*g5s, *gots))


_SUM_STEPS = 2


def _sum_shards(hsums, recvs, pc):
    n = len(hsums)

    def body(pc_ref, *refs):
        for own, got, out in zip(refs[:n], refs[n:2 * n], refs[2 * n:]):
            acc = own[0, 0].astype(f32)
            for j in range(3):
                acc = acc + got[j, 0].astype(f32)
            out[0, 0] = acc

    def rows(h):
        return (h.shape[2] // _SUM_STEPS, h.shape[3])

    gs = pltpu.PrefetchScalarGridSpec(
        num_scalar_prefetch=1, grid=(_SUM_STEPS,),
        in_specs=[pl.BlockSpec((1, 1) + rows(h), lambda i, pc: (pc[0], 0, i, 0)) for h in hsums]
        + [pl.BlockSpec((3, 1) + rows(h), lambda i, pc: (0, 0, i, 0)) for h in hsums],
        out_specs=[pl.BlockSpec((1, 1) + rows(h), lambda i, pc: (0, pc[1], i, 0)) for h in hsums],
    )
    return pl.pallas_call(
        body, name="sum_shards", grid_spec=gs,
        out_shape=[jax.ShapeDtypeStruct((1, 2) + h.shape[2:], f32) for h in hsums],
        compiler_params=_cp(("parallel",), 48),
    )(pc, *_hbm(*hsums, *recvs))


def _sum8(packs):
    _, R, C = packs.shape
    tr = R // 8 if R % 64 == 0 else R

    def body(p_ref, out_ref):
        acc = p_ref[0]
        for d in range(1, 8):
            acc = acc + p_ref[d]
        out_ref[...] = acc

    return pl.pallas_call(
        body, name="sum8", grid=(R // tr,),
        in_specs=[pl.BlockSpec((8, tr, C), lambda i: (0, i, 0))],
        out_specs=pl.BlockSpec((tr, C), lambda i: (i, 0)),
        out_shape=jax.ShapeDtypeStruct((R, C), f32),
        compiler_params=_cp(("parallel",)),
    )(packs)


def _allgather8(x_shard):
    m_per, n = x_shard.shape

    def body(x_ref, out_ref, send_sems, recv_sems, local_sem):
        x, y, c = lax.axis_index("x"), lax.axis_index("y"), lax.axis_index("c")
        me, sibling = (x, y, c), (x, y, 1 - c)
        chips = [(1 - x, y), (x, 1 - y), (1 - x, 1 - y)]

        def rows(px, py, pc):
            return out_ref.at[pl.ds((4 * px + 2 * py + pc) * m_per, m_per), :]

        def copy(k, block, to, src=None):
            return pltpu.make_async_remote_copy(
                src_ref=rows(*block) if src is None else src, dst_ref=rows(*block),
                send_sem=send_sems.at[k], recv_sem=recv_sems.at[k], device_id=to, device_id_type=MESH)

        mine = pltpu.make_async_copy(x_ref, rows(*me), local_sem)
        mine.start()
        first = [copy(0, me, sibling, src=x_ref)]
        first += [copy(1 + j, me, (*chip, c), src=x_ref) for j, chip in enumerate(chips)]
        for cp in first:
            cp.start()
        passed = [copy(4 + j, (*chip, c), sibling) for j, chip in enumerate(chips)]
        for j, chip in enumerate(chips):
            copy(1 + j, (*chip, c), me).wait_recv()
            passed[j].start()
        copy(0, sibling, me).wait_recv()
        for j, chip in enumerate(chips):
            copy(4 + j, (*chip, 1 - c), me).wait_recv()
        for cp in first + passed:
            cp.wait_send()
        mine.wait()

    return pl.pallas_call(
        body, name="allgather8",
        out_shape=jax.ShapeDtypeStruct((8 * m_per, n), x_shard.dtype),
        in_specs=[pl.BlockSpec(memory_space=pltpu.VMEM)],
        out_specs=pl.BlockSpec(memory_space=pltpu.VMEM),
        scratch_shapes=[pltpu.SemaphoreType.DMA((7,)), pltpu.SemaphoreType.DMA((7,)), pltpu.SemaphoreType.DMA],
        compiler_params=_cp(None, 48),
    )(x_shard)


def _other_chips():
    x, y = lax.axis_index("x"), lax.axis_index("y")
    return [(1 - x, y), (x, 1 - y), (1 - x, 1 - y)]


_HBM = pl.BlockSpec(memory_space=pltpu.HBM)
_SEM = pl.BlockSpec(memory_space=pltpu.SEMAPHORE)
_EFFECT = pltpu.SideEffectType.DATAFLOW_SIDE_EFFECTING


def _gather_copies(srcs, lands, send_sems, recv_sems):
    x, y, c = lax.axis_index("x"), lax.axis_index("y"), lax.axis_index("c")
    return [pltpu.make_async_remote_copy(
        src_ref=srcs[a].at[:, c], dst_ref=lands[a].at[2 * x + y, :, c], send_sem=send_sems.at[3 * a + j],
        recv_sem=recv_sems.at[3 * a + j], device_id=(cx, cy, c), device_id_type=MESH)
        for a in range(len(srcs)) for j, (cx, cy) in enumerate(_other_chips())]


def _gather_start(chunks, after, name):
    sizes = [len(srcs) for srcs, _ in chunks]
    flat = [t for srcs, lands in chunks for t in list(srcs) + list(lands)]
    nflat = len(flat)
    nsem = 2 * len(chunks)

    def body(*refs):
        ins, sems, token = refs[:nflat], refs[nflat + 1:nflat + 1 + nsem], refs[-1]
        off = 0
        for k, n in enumerate(sizes):
            for cp in _gather_copies(ins[off:off + n], ins[off + n:off + 2 * n], sems[2 * k], sems[2 * k + 1]):
                cp.start()
            off += 2 * n
        token[...] = jnp.zeros_like(token)

    res = pl.pallas_call(
        body, name=name,
        out_shape=[pltpu.SemaphoreType.DMA((3 * n,)) for n in sizes for _ in range(2)]
        + [pltpu.HBM(t.shape, t.dtype) for t in flat] + [jax.ShapeDtypeStruct((8, 128), f32)],
        in_specs=[_HBM] * nflat + [pl.BlockSpec(memory_space=pl.ANY)],
        out_specs=[_SEM] * nsem + [_HBM] * nflat + [pl.BlockSpec(memory_space=pltpu.VMEM)],
        input_output_aliases={i: nsem + i for i in range(nflat)},
        compiler_params=pltpu.CompilerParams(has_side_effects=_EFFECT),
    )(*[pltpu.with_memory_space_constraint(t, pltpu.HBM) for t in flat], after)
    out, off = [], nsem
    for k, n in enumerate(sizes):
        out.append((res[2 * k], res[2 * k + 1], res[off:off + n], res[off + n:off + 2 * n]))
        off += 2 * n
    return out, res[-1]


def _gather_wait(send_sems, recv_sems, srcs, lands, after, name):
    n = len(srcs)

    def body(*refs):
        for cp in _gather_copies(refs[:n], refs[n:2 * n], refs[2 * n], refs[2 * n + 1]):
            cp.wait_send()
            cp.wait_recv()

    res = pl.pallas_call(
        body, name=name,
        out_shape=[pltpu.HBM(t.shape, t.dtype) for t in list(srcs) + list(lands)],
        in_specs=[_HBM] * (2 * n) + [_SEM, _SEM] + [pl.BlockSpec(memory_space=pl.ANY)] * len(after),
        out_specs=[_HBM] * (2 * n),
        input_output_aliases={i: i for i in range(2 * n)},
        compiler_params=pltpu.CompilerParams(has_side_effects=_EFFECT),
    )(*srcs, *lands, send_sems, recv_sems, *after)
    return res[n:]


def _split_start(make_copies, arrays, nsem, name, after=()):
    n, na = len(arrays), len(after)

    def body(*refs):
        for cp in make_copies(refs[:n], refs[n + na], refs[n + na + 1]):
            cp.start()
        refs[-1][...] = jnp.zeros_like(refs[-1])

    res = pl.pallas_call(
        body, name=name,
        out_shape=[pltpu.SemaphoreType.DMA((nsem,)), pltpu.SemaphoreType.DMA((nsem,))]
        + [pltpu.HBM(t.shape, t.dtype) for t in arrays] + [jax.ShapeDtypeStruct((8, 128), f32)],
        in_specs=[_HBM] * n + [pl.BlockSpec(memory_space=pl.ANY)] * na,
        out_specs=[_SEM, _SEM] + [_HBM] * n + [pl.BlockSpec(memory_space=pltpu.VMEM)],
        input_output_aliases={i: i + 2 for i in range(n)},
        compiler_params=pltpu.CompilerParams(has_side_effects=_EFFECT),
    )(*[pltpu.with_memory_space_constraint(t, pltpu.HBM) for t in arrays], *after)
    return (res[0], res[1], res[2:2 + n]), res[-1]


def _split_wait(make_copies, send_sems, recv_sems, arrays, after, name):
    n = len(arrays)

    def body(*refs):
        for cp in make_copies(refs[:n], refs[n], refs[n + 1]):
            cp.wait_send()
            cp.wait_recv()

    return pl.pallas_call(
        body, name=name,
        out_shape=[pltpu.HBM(t.shape, t.dtype) for t in arrays],
        in_specs=[_HBM] * n + [_SEM, _SEM] + [pl.BlockSpec(memory_space=pl.ANY)] * len(after),
        out_specs=[_HBM] * n, input_output_aliases={i: i for i in range(n)},
        compiler_params=pltpu.CompilerParams(has_side_effects=_EFFECT),
    )(*arrays, send_sems, recv_sems, *after)


def _sibling():
    return lax.axis_index("x"), lax.axis_index("y"), 1 - lax.axis_index("c")


def _forward_copies(lands, send_sems, recv_sems):
    c = lax.axis_index("c")
    return [pltpu.make_async_remote_copy(
        src_ref=lands[a].at[2 * cx + cy, :, c], dst_ref=lands[a].at[2 * cx + cy, :, c], send_sem=send_sems.at[3 * a + j],
        recv_sem=recv_sems.at[3 * a + j], device_id=_sibling(), device_id_type=MESH)
        for a in range(len(lands)) for j, (cx, cy) in enumerate(_other_chips())]


def _swap_copies(fulls, send_sems, recv_sems):
    c = lax.axis_index("c")
    return [pltpu.make_async_remote_copy(src_ref=t.at[:, c], dst_ref=t.at[:, c], send_sem=send_sems.at[a],
                                         recv_sem=recv_sems.at[a], device_id=_sibling(), device_id_type=MESH)
            for a, t in enumerate(fulls)]


def _pair_copies(refs, send_sems, recv_sems):
    n = len(refs) // 2
    c = lax.axis_index("c")
    return [pltpu.make_async_remote_copy(src_ref=refs[a].at[:, :, 1 - c], dst_ref=refs[n + a], send_sem=send_sems.at[a],
                                         recv_sem=recv_sems.at[a], device_id=_sibling(), device_id_type=MESH)
            for a in range(n)]


def _scatter_copies(srcs, lands, send_sems, recv_sems):
    c = lax.axis_index("c")
    return [pltpu.make_async_remote_copy(
        src_ref=srcs[a].at[2 * cx + cy], dst_ref=lands[a].at[j], send_sem=send_sems.at[3 * a + j],
        recv_sem=recv_sems.at[3 * a + j], device_id=(cx, cy, c), device_id_type=MESH)
        for a in range(len(srcs)) for j, (cx, cy) in enumerate(_other_chips())]


def _scatter_start(hsums, name, after=()):
    n = len(hsums)
    na = len(after)

    def body(*refs):
        srcs, lands = refs[:n], refs[n:2 * n]
        send_sems, recv_sems = refs[2 * n + na], refs[2 * n + na + 1]
        for cp in _scatter_copies(srcs, lands, send_sems, recv_sems):
            cp.start()
        refs[-1][...] = jnp.zeros_like(refs[-1])

    lands = [lax.empty((3,) + g.shape[1:], g.dtype) for g in hsums]
    res = pl.pallas_call(
        body, name=name,
        out_shape=[pltpu.SemaphoreType.DMA((3 * n,)), pltpu.SemaphoreType.DMA((3 * n,))]
        + [pltpu.HBM(g.shape, g.dtype) for g in hsums] + [pltpu.HBM(g.shape, g.dtype) for g in lands]
        + [jax.ShapeDtypeStruct((8, 128), f32)],
        in_specs=[_HBM] * (2 * n) + [pl.BlockSpec(memory_space=pl.ANY)] * na,
        out_specs=[_SEM, _SEM] + [_HBM] * (2 * n) + [pl.BlockSpec(memory_space=pltpu.VMEM)],
        input_output_aliases={i: i + 2 for i in range(2 * n)},
        compiler_params=pltpu.CompilerParams(has_side_effects=_EFFECT),
    )(*[pltpu.with_memory_space_constraint(t, pltpu.HBM) for t in list(hsums) + lands], *after)
    return (res[0], res[1], res[2:2 + n], res[2 + n:2 + 2 * n]), res[-1]


def _scatter_wait(send_sems, recv_sems, srcs, lands, after, name):
    n = len(srcs)
    extra = list(after)

    def body(*refs):
        s_refs, l_refs = refs[:n], refs[n:2 * n]
        ss, rs = refs[2 * n], refs[2 * n + 1]
        for cp in _scatter_copies(s_refs, l_refs, ss, rs):
            cp.wait_send()
            cp.wait_recv()

    res = pl.pallas_call(
        body, name=name,
        out_shape=[pltpu.HBM(g.shape, g.dtype) for g in srcs] + [pltpu.HBM(g.shape, g.dtype) for g in lands],
        in_specs=[_HBM] * (2 * n) + [_SEM, _SEM] + [pl.BlockSpec(memory_space=pl.ANY)] * len(extra),
        out_specs=[_HBM] * (2 * n),
        input_output_aliases={i: i for i in range(2 * n)},
        compiler_params=pltpu.CompilerParams(has_side_effects=_EFFECT),
    )(*srcs, *lands, send_sems, recv_sems, *extra)
    return res[:n], res[n:]


def _plane_copies(src, land, send_sems, recv_sems):
    x, y, c = lax.axis_index("x"), lax.axis_index("y"), lax.axis_index("c")
    return [pltpu.make_async_remote_copy(src_ref=src, dst_ref=land.at[2 * x + y, c], send_sem=send_sems.at[j],
                                         recv_sem=recv_sems.at[j], device_id=(cx, cy, c), device_id_type=MESH)
            for j, (cx, cy) in enumerate(_other_chips())]


def _plane_start(pack, land, name):
    def body(src, lnd, send_sems, recv_sems, _s, _l, token):
        for cp in _plane_copies(src, lnd, send_sems, recv_sems):
            cp.start()
        token[...] = jnp.zeros_like(token)

    res = pl.pallas_call(
        body, name=name,
        out_shape=[pltpu.SemaphoreType.DMA((3,)), pltpu.SemaphoreType.DMA((3,)), pltpu.HBM(pack.shape, pack.dtype),
                   pltpu.HBM(land.shape, land.dtype), jax.ShapeDtypeStruct((8, 128), f32)],
        in_specs=[_HBM, _HBM], out_specs=[_SEM, _SEM, _HBM, _HBM, pl.BlockSpec(memory_space=pltpu.VMEM)],
        input_output_aliases={0: 2, 1: 3},
        compiler_params=pltpu.CompilerParams(has_side_effects=_EFFECT),
    )(pltpu.with_memory_space_constraint(pack, pltpu.HBM), pltpu.with_memory_space_constraint(land, pltpu.HBM))
    return res[:4], res[4]


def _plane_wait(send_sems, recv_sems, pack, land, after, name):
    def body(src, lnd, ss, rs, *_):
        for cp in _plane_copies(src, lnd, ss, rs):
            cp.wait_send()
            cp.wait_recv()

    return pl.pallas_call(
        body, name=name,
        out_shape=[pltpu.HBM(pack.shape, pack.dtype), pltpu.HBM(land.shape, land.dtype)],
        in_specs=[_HBM, _HBM, _SEM, _SEM] + [pl.BlockSpec(memory_space=pl.ANY)] * len(after),
        out_specs=[_HBM, _HBM], input_output_aliases={0: 0, 1: 1},
        compiler_params=pltpu.CompilerParams(has_side_effects=_EFFECT),
    )(pack, land, send_sems, recv_sems, *after)[1]


def _swap_halves(fulls):
    n = len(fulls)

    def body(*refs):
        ins, outs = refs[:n], refs[n:2 * n]
        send_sems, recv_sems = refs[2 * n:]
        c = lax.axis_index("c")
        sibling = (lax.axis_index("x"), lax.axis_index("y"), 1 - c)
        copies = []
        for a in range(n):
            cp = pltpu.make_async_remote_copy(src_ref=outs[a].at[:, c], dst_ref=outs[a].at[:, c], send_sem=send_sems.at[a],
                                              recv_sem=recv_sems.at[a], device_id=sibling, device_id_type=MESH)
            cp.start()
            copies.append(cp)
        for a, cp in enumerate(copies):
            cp.wait_send()
            theirs = outs[a].at[:, 1 - c]
            pltpu.make_async_remote_copy(src_ref=theirs, dst_ref=theirs, send_sem=send_sems.at[a], recv_sem=recv_sems.at[a],
                                         device_id=sibling, device_id_type=MESH).wait_recv()

    hbm = pl.BlockSpec(memory_space=pl.ANY)
    return pl.pallas_call(
        body, name="swap_halves",
        out_shape=[jax.ShapeDtypeStruct(p.shape, p.dtype) for p in fulls],
        in_specs=[hbm] * n, out_specs=[hbm] * n,
        input_output_aliases={a: a for a in range(n)},
        scratch_shapes=[pltpu.SemaphoreType.DMA((n,)), pltpu.SemaphoreType.DMA((n,))],
    )(*fulls)


def _to_segments(t):
    s, c = t.shape
    return t.reshape(SCAN_SEG, s // SCAN_SEG, c).transpose(1, 0, 2).reshape(s, c)


def _from_segments(t):
    s, c = t.shape
    return t.reshape(s // SCAN_SEG, SCAN_SEG, c).transpose(1, 0, 2).reshape(s, c)


def _ssm_operators(a_re, a_im, log_dt, b_re, b_im, c_re, c_im):
    lam = lax.complex(a_re, a_im)
    dt = jnp.exp(log_dt)[:, None]
    a_bar = jnp.exp(lam * dt)
    b_bar = ((a_bar - 1.0) / lam)[:, :, None] * lax.complex(b_re, b_im)
    eye = jnp.eye(N_GROUPS, dtype=f32)

    def embed_b(t):
        return (jnp.transpose(t, (0, 2, 1))[:, :, None, :] * eye[:, None, :, None]).reshape(D_SSM, D_STATE)

    def embed_c(t):
        return (jnp.transpose(t, (0, 2, 1))[:, :, None, :] * eye[:, None, :, None]).reshape(D_STATE, D_SSM)

    a2 = jnp.stack([a_bar.real.reshape(D_STATE), a_bar.imag.reshape(D_STATE)])
    return a2, embed_b(b_bar.real), embed_b(b_bar.imag), embed_c(c_re), embed_c(c_im)


def _local_step(x, target, mod, small, ffn_weights, mix_weights, grads_done, ffn_bwd_issued):
    table = jnp.asarray(_bucket_table())
    bias = small["att_bias"]
    L = DEPTH
    saved = []
    ssm_names = ("ssm_a_re", "ssm_a_im", "ssm_log_dt", "ssm_b_re", "ssm_b_im", "ssm_c_re", "ssm_c_im")
    ssm_ops_vjp = []
    for l in range(L):
        sv = {}
        m9 = mod[l]
        sv["x0"] = x
        sv["w0"] = ffn_weights(l, 0, x)
        x, sv["f0"], sv["g0"], sv["u0"], sv["h0"] = _ffn_fwd(x, m9[0:3], *sv["w0"], 0, small["ln_g"][l, 0:1], small["ln_b"][l, 0:1])
        sv["x1"] = x
        sv["w1"] = mix_weights(l, x)
        *qkv, z_rest, sv["h1"] = _mix_in_fwd(x, m9[3:6], sv["w1"][0], 0)
        S = x.shape[0]
        qkv = [t.reshape(3, S, D_ATT) for t in qkv]
        att = [_att_fwd(qkv[b], bias, b) for b in range(3)]
        y_att, lse3 = _att_merge([att[b][0].reshape(d, S // d, D_ATT) for b, d in enumerate(DILATIONS)],
                                 [att[b][1].reshape(d, S // d, _LANES) for b, d in enumerate(DILATIONS)])
        sv.update(qkv=qkv, lse=[a[1] for a in att], lse3=lse3, y_att=y_att)

        ops, ops_vjp = jax.vjp(_ssm_operators, *[small[k][l] for k in ssm_names])
        ssm_ops_vjp.append(ops_vjp)
        a2, bre, bim, cre, cim = [t[None] for t in ops]
        u_ssm = _to_segments(z_rest[:, :D_SSM])
        sr, si = _ssm_states(u_ssm, bre, bim, a2, 0)
        dskip = small["ssm_d"][l][None, :]
        glu_b = small["glu_b"][l][None, :]
        out_seg, y_seg = _ssm_out(sr, si, u_ssm, cre, cim, 0, dskip, small["glu_w"][l], glu_b)
        y_ssm = _from_segments(out_seg)
        sv.update(ssm_ops=(a2, bre, bim, cre, cim), u_ssm=u_ssm, sr=sr, si=si, y_seg=y_seg, y_ssm=y_ssm)

        u_pool = jnp.concatenate([jnp.zeros((POOL_HALO, D_POOL), f32), z_rest[:, D_SSM:]])
        y_pool = _pool_fwd(u_pool, small["pool_w"][l], small["pool_scale"][l][None, :])
        sv.update(u_pool=u_pool, y_pool=y_pool)

        x, sv["ymix"] = _mix_out_fwd(x, y_att, y_ssm, y_pool, m9[3:6], sv["w1"][1], 0, small["ln_g"][l, 1:2], small["ln_b"][l, 1:2])
        sv["x2"] = x
        sv["w2"] = ffn_weights(l, 1, x)
        x, sv["f2"], sv["g2"], sv["u2"], sv["h2"] = _ffn_fwd(x, m9[6:9], *sv["w2"], 0, small["ln_g"][l, 2:3], small["ln_b"][l, 2:3])
        saved.append(sv)

    loss, dx = _loss_head(x, target)

    dmod = [None] * L
    dln_g = [None] * L
    dln_b = [None] * L
    sg = {k: [None] * L for k in ssm_names + ("ssm_d", "glu_w", "glu_b", "pool_w", "pool_scale")}
    dbias_tot = None
    order_after = jnp.zeros((), f32)
    for l in reversed(range(L)):
        sv = saved[l]
        m9 = mod[l] + order_after

        def fresh(like):
            return [lax.empty(t.shape, bf16) for t in like]

        dx, dg, du, a, df, dm2, dlg2, dlb2 = _ffn_bwd(dx, sv["x2"], sv["f2"], sv["g2"], sv["u2"], m9[6:9], *sv["w2"], 0,
                                                     small["ln_g"][l, 2:3])
        m9 = m9 + ffn_bwd_issued(l, 1, dx)
        g_ffn1 = _ffn_wgrad(sv["h2"], dg, du, a, df, *fresh(sv["w2"]), 0)
        dxr, d_att, d_ssm, d_pool, dgate1, dlg1, dlb1, g_w_out = _mix_out_bwd(
            dx, sv["x1"], sv["ymix"], sv["y_att"], sv["y_ssm"], sv["y_pool"], m9[3:6], sv["w1"][1], 0, small["ln_g"][l, 1:2],
            fresh(sv["w1"])[1])
        S = d_att.shape[0]
        merged = _att_merge_bwd(d_att, sv["y_att"], sv["lse3"])
        dqkv, dbias = [], []
        for b, d in enumerate(DILATIONS):
            dq_b, db_b = _att_bwd(sv["qkv"][b], merged[b].reshape(S, D_ATT), sv["lse"][b], merged[3 + b].reshape(S, _LANES), bias, b)
            dqkv.append(dq_b.reshape(3, d, S // d, D_ATT))
            dbias.append(db_b)
        dbias = jnp.stack(dbias)
        dbias_tot = dbias if dbias_tot is None else dbias_tot + dbias
        d_seg = _to_segments(d_ssm)
        dskip = small["ssm_d"][l][None, :]
        glu_b = small["glu_b"][l][None, :]
        dy_seg, du_skip, dd, dglu_b, dglu_w = _ssm_out_bwd(d_seg, sv["y_seg"], sv["u_ssm"], dskip, small["glu_w"][l], glu_b)
        a2, bre, bim, cre, cim = sv["ssm_ops"]
        du_seg, dbre, dbim, da2, dcre, dcim = _ssm_states_bwd(dy_seg, du_skip, sv["u_ssm"], sv["sr"], sv["si"], cre, cim,
                                                              bre, bim, a2, 0)
        for k, t in zip(ssm_names, ssm_ops_vjp[l]((da2, dbre, dbim, dcre, dcim))):
            sg[k][l] = t
        sg["ssm_d"][l] = dd[0]
        sg["glu_b"][l] = dglu_b[0]
        sg["glu_w"][l] = dglu_w
        du_ssm = _from_segments(du_seg)
        dyp = jnp.concatenate([d_pool, jnp.zeros((POOL_HALO, D_POOL), f32)])
        du_pool, dpw, dps = _pool_bwd(dyp, sv["u_pool"], small["pool_w"][l], small["pool_scale"][l][None, :])
        sg["pool_w"][l] = dpw
        sg["pool_scale"][l] = dps[0]
        d_rest = jnp.concatenate([du_ssm, du_pool], axis=1).astype(bf16)
        dx, dm1, dz = _mix_in_bwd(dqkv, d_rest, dxr, sv["x1"], m9[3:6], sv["w1"][0], 0)
        g_w_in = _mix_in_wgrad(sv["h1"], dz, fresh(sv["w1"])[0], 0)
        ffn_names = ("ffn_w_gate", "ffn_w_up", "ffn_w_down")
        m9 = m9 + grads_done(l, 1, list(zip(ffn_names, [(2 * l + 1) * FF_SHARD] * 3, g_ffn1))
                             + [("w_in", l * D_MODEL, g_w_in), ("w_out", l * 256, g_w_out)])
        dm1 = jnp.concatenate([dm1[0:2], dgate1])
        dx, dg, du, a, df, dm0, dlg0, dlb0 = _ffn_bwd(dx, sv["x0"], sv["f0"], sv["g0"], sv["u0"], m9[0:3], *sv["w0"], 0,
                                                     small["ln_g"][l, 0:1])
        issued = ffn_bwd_issued(l, 0, dx)
        g_ffn0 = _ffn_wgrad(sv["h0"], dg, du, a, df, *fresh(sv["w0"]), 0)
        order_after = grads_done(l, 0, list(zip(ffn_names, [2 * l * FF_SHARD] * 3, g_ffn0))) + issued
        dmod[l] = jnp.concatenate([dm0 + issued, dm1, dm2])
        dln_g[l] = jnp.concatenate([dlg0, dlg1, dlg2])
        dln_b[l] = jnp.concatenate([dlb0, dlb1, dlb2])

    small_grads = {k: jnp.stack(v) for k, v in sg.items()}
    small_grads["rel_bias"] = _bias_bwd(dbias_tot, table)
    small_grads["ln_g"] = jnp.stack(dln_g)
    small_grads["ln_b"] = jnp.stack(dln_b)
    return loss, dx, jnp.stack(dmod), small_grads


_TILE_ELEMS = 8 * 128


def _pack_rows(shapes):
    out, row = [], 0
    for s in shapes:
        nr = -(-int(np.prod(s)) // _TILE_ELEMS) * 8
        out.append((row, nr))
        row += nr
    return out


def _pack(arrs):
    parts = []
    for a in arrs:
        flat = a.reshape(-1).astype(f32)
        npad = -(-flat.shape[0] // _TILE_ELEMS) * _TILE_ELEMS
        parts.append(jnp.pad(flat, (0, npad - flat.shape[0])).reshape(npad // 128, 128))
    return jnp.concatenate(parts, axis=0)


def _unpack(buf, shapes):
    return [buf[row:row + nr].reshape(-1)[:int(np.prod(s))].reshape(s) for s, (row, nr) in zip(shapes, _pack_rows(shapes))]


_REPL = ("rel_bias", "ada_b", "ssm_a_re", "ssm_a_im", "ssm_log_dt", "ssm_b_re", "ssm_b_im", "ssm_c_re", "ssm_c_im",
         "ssm_d", "glu_b", "pool_w", "pool_scale")
_SMALL_SHARDED = ("ln_g", "ln_b", "glu_w")
_BIG = ("ffn_w_gate", "ffn_w_up", "ffn_w_down", "w_in", "w_out")
_ORDER = ("rel_bias", "ada_w", "ada_b", "ln_g", "ln_b", "ffn_w_gate", "ffn_w_up", "ffn_w_down", "w_in", "w_out",
          "ssm_a_re", "ssm_a_im", "ssm_log_dt", "ssm_b_re", "ssm_b_im", "ssm_c_re", "ssm_c_im", "ssm_d", "glu_w",
          "glu_b", "pool_w", "pool_scale")


def kernel(x, c, rel_bias, ada_w, ada_b, ln_g, ln_b, ffn_w_gate, ffn_w_up, ffn_w_down, w_in, w_out, ssm_a_re, ssm_a_im, ssm_log_dt, ssm_b_re, ssm_b_im, ssm_c_re, ssm_c_im, ssm_d, glu_w, glu_b, pool_w, pool_scale, loss_target, m_rel_bias, m_ada_w, m_ada_b, m_ln_g, m_ln_b, m_ffn_w_gate, m_ffn_w_up, m_ffn_w_down, m_w_in, m_w_out, m_ssm_a_re, m_ssm_a_im, m_ssm_log_dt, m_ssm_b_re, m_ssm_b_im, m_ssm_c_re, m_ssm_c_im, m_ssm_d, m_glu_w, m_glu_b, m_pool_w, m_pool_scale, v_rel_bias, v_ada_w, v_ada_b, v_ln_g, v_ln_b, v_ffn_w_gate, v_ffn_w_up, v_ffn_w_down, v_w_in, v_w_out, v_ssm_a_re, v_ssm_a_im, v_ssm_log_dt, v_ssm_b_re, v_ssm_b_im, v_ssm_c_re, v_ssm_c_im, v_ssm_d, v_glu_w, v_glu_b, v_pool_w, v_pool_scale):
    args = dict(locals())
    w = {k: args[k] for k in _ORDER}
    m = {k: args["m_" + k] for k in _ORDER}
    v = {k: args["v_" + k] for k in _ORDER}
    L, D = DEPTH, D_MODEL
    ax, ay, ac = lax.axis_index("x"), lax.axis_index("y"), lax.axis_index("c")
    p_me = 2 * ax + ay
    dev = 4 * ax + 2 * ay + ac

    transposed = ("ffn_w_gate", "ffn_w_up")
    for d in (w, m, v):
        for name in transposed:
            d[name] = jnp.swapaxes(d[name], 2, 3)

    def halves(t):
        return t.astype(bf16).reshape(1, 2, t.shape[0] // 2, t.shape[1])

    def landing(src):
        return lax.dynamic_update_slice(lax.empty((N_CHIPS,) + src.shape, bf16), src[None], (p_me, 0, 0, 0, 0))

    chunk_keys = [("ffn", 0, 0), ("mix", 0), ("ffn", 0, 1), ("ffn", 1, 0), ("mix", 1), ("ffn", 1, 1)]
    chunk_srcs = []
    for key in chunk_keys:
        if key[0] == "ffn":
            chunk_srcs.append([halves(w[name][key[1], key[2]]) for name in ("ffn_w_gate", "ffn_w_up", "ffn_w_down")])
        else:
            chunk_srcs.append([halves(w_in[key[1]]), halves(w_out[key[1]])])

    pack = _pack([c, ln_g, ln_b, glu_w])
    rows = pack.shape[0]
    allp = _allgather8(pack).reshape(8, rows, 128)
    chunks = [(srcs, [landing(t) for t in srcs]) for srcs in chunk_srcs]
    first_in_flight, first_begun = _gather_start(chunks[:1], allp, "gather_start_first")
    c_all = allp[:, :8].reshape(8, D) + first_begun[0, 0]
    by_chip = allp[0::2]

    fwd_rows = _pack_rows([c.shape, ln_g.shape, ln_b.shape, glu_w.shape])

    def sharded(part, shape, axis):
        row0, nrows = fwd_rows[part]
        t = by_chip[:, row0:row0 + nrows].reshape(N_CHIPS, -1)[:, :int(np.prod(shape))].reshape((N_CHIPS,) + shape)
        return jnp.concatenate([t[p] for p in range(N_CHIPS)], axis=axis)

    ln_g_full = sharded(1, ln_g.shape, 2)
    ln_b_full = sharded(2, ln_b.shape, 2)
    glu_w_full = sharded(3, glu_w.shape, 1)

    ncol = ada_w.shape[-1]
    ada_b_cols = lax.dynamic_slice_in_dim(ada_b, p_me * ncol, ncol, axis=1)[:, None, :]
    mod_part = _ada_fwd(c_all, ada_w, ada_b_cols)
    mrows = L * 8 * ncol // 128
    mod_pack = mod_part.reshape(mrows, 128)
    mod_land = lax.dynamic_update_slice(lax.empty((N_CHIPS, 2, mrows, 128), f32), mod_pack[None, None], (p_me, ac, 0, 0))
    mod_in_flight, _ = _plane_start(mod_pack, mod_land, "mod_start")
    att_bias = _bias_fwd(rel_bias, jnp.asarray(_bucket_table()))
    small_names = _REPL + _SMALL_SHARDED
    small_packs = [_pack([d[k] for k in small_names]) for d in (w, m, v)]
    mod_land = _plane_wait(*mod_in_flight, [att_bias] + small_packs + [t for _, lands in chunks[1:] for t in lands], "mod_wait")
    mod_all = _swap_halves([mod_land])[0].reshape(8, L, 8, ncol)
    mod_mine = lax.dynamic_index_in_dim(mod_all, dev, axis=2, keepdims=False)
    mod = jnp.concatenate([mod_mine[2 * p] for p in range(N_CHIPS)], axis=-1).reshape(L, 9, D)

    rest_in_flight, rest_begun = _gather_start(chunks[1:], mod, "gather_start_rest")
    in_flight = first_in_flight + rest_in_flight

    forwarding = {}

    def forward(k, after):
        lands = _gather_wait(*in_flight[k], [after, rest_begun], "gather_wait_%d" % k)
        forwarding[k], begun = _split_start(_forward_copies, lands, 3 * len(lands), "gather_forward_start_%d" % k)
        return begun

    def gathered(key, after):
        k = chunk_keys.index(key)
        order = [after]
        if k not in forwarding:
            order.append(forward(k, after))
        if 3 <= k + 1 < len(chunk_keys):
            order.append(forward(k + 1, after))
        lands = _split_wait(_forward_copies, *forwarding[k], order, "gather_forward_wait_%d" % k)
        return [t.reshape(N_CHIPS, 1, 2 * t.shape[3], t.shape[4]) for t in lands]

    pc = jnp.stack([p_me, ac]).astype(jnp.int32)
    groups = {}
    scattering = {}

    pairing = {}

    def start_pairs(tag, after=()):
        g5 = [g.reshape(g.shape[:2] + (2, g.shape[2] // 2, g.shape[3])) for _, _, g in groups[tag]]
        gots = [lax.empty(g.shape[:2] + g.shape[3:], bf16) for g in g5]
        pairing[tag], begun = _split_start(_pair_copies, g5 + gots, len(g5), "pair_exchange_start_%s" % tag, after)
        return begun

    def start_group(tag, after):
        arrays = _split_wait(_pair_copies, *pairing[tag], after, "pair_exchange_wait_%s" % tag)
        n = len(arrays) // 2
        hsum = _pair_sum(arrays[:n], arrays[n:], pc)
        scattering[tag], begun = _scatter_start(hsum, "scatter_start_%s" % tag)
        return begun

    def grads_done(l, s, grads):
        if l == 1:
            groups.setdefault("l1", []).extend(grads)
            return start_pairs("l1")[0, 0] if s == 0 else jnp.zeros((), f32)
        groups["l0a" if s == 1 else "l0b"] = grads
        return start_pairs("l0a")[0, 0] if s == 1 else jnp.zeros((), f32)

    swapping = {}

    def reduce_group(tag, after):
        hsum, recv = _scatter_wait(*scattering[tag], after, "scatter_wait_%s" % tag)
        full = _sum_shards(hsum, recv, pc)
        swapping[tag], begun = _split_start(_swap_copies, full, len(full), "swap_halves_start_%s" % tag)
        return [begun]

    def ffn_bwd_issued(l, s, dx):
        if l == 1:
            return jnp.zeros((), f32)
        return start_group("l1" if s == 1 else "l0a", [dx])[0, 0]

    small = {k: w[k] for k in _REPL if k != "ada_b"}
    small.update(ln_g=ln_g_full, ln_b=ln_b_full, glu_w=glu_w_full, att_bias=att_bias)
    loss_dev, grad_x, dmod, sgrads = _local_step(
        x[0], loss_target[0], mod, small, lambda l, s, after: gathered(("ffn", l, s), after),
        lambda l, after: gathered(("mix", l), after), grads_done, ffn_bwd_issued)
    loss = lax.psum(loss_dev[0, 0], ("x", "y", "c"))

    names = ("rel_bias", "ln_g", "ln_b", "ssm_a_re", "ssm_a_im", "ssm_log_dt", "ssm_b_re", "ssm_b_im", "ssm_c_re",
             "ssm_c_im", "ssm_d", "glu_w", "glu_b", "pool_w", "pool_scale")
    gpack = _pack([dmod] + [sgrads[k] for k in names])
    grows = gpack.shape[0]
    land = lax.dynamic_update_slice(lax.empty((N_CHIPS, 2, grows, 128), f32), gpack[None, None], (p_me, ac, 0, 0))
    small_in_flight, small_begun = _plane_start(gpack, land, "small_grads_start")
    l0b_begun = start_group("l0b", [start_pairs("l0b", (small_begun,))])

    out_g, out_d, out_m, out_v = {}, {}, {}, {}
    row_tile = dict(zip(_BIG, (352, 352, 352, 256, 256)))
    big = {name: None for name in _BIG}

    def update_group(tag, after):
        full = _split_wait(_swap_copies, *swapping[tag], after, "swap_halves_wait_%s" % tag)
        for (name, row0, _), g in zip(groups[tag], full):
            shp = w[name].shape
            r2 = (int(np.prod(shp[:-1])), shp[-1])
            big[name] = _adamw(w[name].reshape(r2), m[name].reshape(r2), v[name].reshape(r2), g.reshape(-1, shp[-1]),
                               row_tile[name], row0, big[name])
        return [big[name][1] for name, _, _ in groups[tag]]

    after = reduce_group("l0a", reduce_group("l1", [grad_x, l0b_begun]))
    after = update_group("l0a", update_group("l1", after))

    land = _plane_wait(*small_in_flight, after, "small_grads_wait")
    gall = _swap_halves([land])[0].reshape(8, grows, 128)
    gsum = _unpack(_sum8(gall), [(L, 9 * D)] + [sgrads[k].shape for k in names])
    red = dict(zip(("ada_b",) + names, gsum))
    red["ln_g"] = lax.dynamic_slice_in_dim(red["ln_g"], p_me * 256, 256, axis=2)
    red["ln_b"] = lax.dynamic_slice_in_dim(red["ln_b"], p_me * 256, 256, axis=2)
    red["glu_w"] = lax.dynamic_slice_in_dim(red["glu_w"], p_me * 64, 64, axis=1)

    dmod_all = gall[:, :L * 9 * D // 128].reshape(8, L, 9 * D)
    dmod_cols = jnp.transpose(lax.dynamic_slice_in_dim(dmod_all, p_me * ncol, ncol, axis=2), (1, 0, 2))
    g_ada_w = _ada_wgrad(jnp.transpose(c_all), dmod_cols)

    r2 = (L * D, ncol)
    res = _adamw(ada_w.reshape(r2), m["ada_w"].reshape(r2), v["ada_w"].reshape(r2), g_ada_w.reshape(r2), 128)
    out_g["ada_w"], out_d["ada_w"], out_m["ada_w"], out_v["ada_w"] = [t.reshape(ada_w.shape) for t in res]

    res_small = _adamw(*small_packs, _pack([red[k] for k in small_names]), small_packs[0].shape[0])
    for t, dst in zip(res_small, (out_g, out_d, out_m, out_v)):
        for k, a in zip(small_names, _unpack(t, [w[k].shape for k in small_names])):
            dst[k] = a

    update_group("l0b", reduce_group("l0b", [res_small[1], res[1]]))
    for name in _BIG:
        res = [t.reshape(w[name].shape) for t in big[name]]
        out_g[name], out_d[name], out_m[name], out_v[name] = [jnp.swapaxes(t, 2, 3) for t in res] if name in transposed else res

    return (loss, grad_x[None], *[out_g[k] for k in _ORDER], *[out_d[k] for k in _ORDER],
            *[out_m[k] for k in _ORDER], *[out_v[k] for k in _ORDER])
```

```python
import math

import numpy as np
import jax
import jax.numpy as jnp
from jax import lax
from jax.experimental import pallas as pl
from jax.experimental.pallas import tpu as pltpu

f32 = jnp.float32
bf16 = jnp.bfloat16
MESH = pl.DeviceIdType.MESH

D_MODEL = 1024
SEQ = 2048
DEPTH = 2
HEAD_DIM = 64
N_HEADS = 8
D_ATT = 512
DILATIONS = (1, 4, 16)
BLOCKS_PER_RESIDUE = (16, 4, 1)
ATT_BLOCK = 128
N_UNITS = SEQ // ATT_BLOCK
N_GROUPS = 16
SSM_STATE = 64
D_SSM = 256
D_STATE = N_GROUPS * SSM_STATE
POOL_WINDOWS = (2, 4, 8, 16)
POOL_GROUP = 64
D_POOL = 256
POOL_HALO = 16
D_FF = 2816
N_BUCKETS = 32
MAX_DISTANCE = 2048
ALPHA = (2 * DEPTH) ** 0.25
FFN_RES = 0.5
LN_EPS = 1e-5
NEG = -1e30
N_CHIPS = 4
FF_SHARD = D_FF // N_CHIPS
SCAN_SEG = 8

ADAM_LR, ADAM_B1, ADAM_B2, ADAM_EPS, ADAM_WD, ADAM_STEP = 0.001, 0.9, 0.999, 1e-08, 0.01, 10

TOK_TILE = 512


def _cp(dims=None, vmem_mb=None):
    kw = {}
    if dims is not None:
        kw["dimension_semantics"] = dims
    if vmem_mb is not None:
        kw["vmem_limit_bytes"] = vmem_mb << 20
    return pltpu.CompilerParams(**kw)


def _dot(a, b):
    return jnp.dot(a, b, preferred_element_type=f32)


def _dot_nt(a, b):
    return lax.dot_general(a, b, (((1,), (1,)), ((), ())), preferred_element_type=f32)


def _dot_tn(a, b):
    return lax.dot_general(a, b, (((0,), (0,)), ((), ())), preferred_element_type=f32)


def _ln_stats(v):
    mu = jnp.mean(v, -1, keepdims=True)
    d = v - mu
    var = jnp.mean(d * d, -1, keepdims=True)
    rstd = lax.rsqrt(var + LN_EPS)
    return d * rstd, rstd


def _ln_bwd(dxh, xh, rstd):
    return rstd * (dxh - jnp.mean(dxh, -1, keepdims=True) - xh * jnp.mean(dxh * xh, -1, keepdims=True))


_GELU_C = math.sqrt(2.0 / math.pi)


def _gelu(y):
    return 0.5 * y * (1.0 + jnp.tanh(_GELU_C * (y + 0.044715 * y * y * y)))


def _gelu_grad(y):
    t = jnp.tanh(_GELU_C * (y + 0.044715 * y * y * y))
    return 0.5 * (1.0 + t) + 0.5 * y * (1.0 - t * t) * (_GELU_C * (1.0 + 3 * 0.044715 * y * y))


def _full(shape):
    return pl.BlockSpec(shape, lambda *_: (0,) * len(shape))


def _hbm(*args):
    return [pltpu.with_memory_space_constraint(a, pltpu.HBM) if getattr(a, "ndim", 0) >= 2 else a for a in args]


def _ffn_fwd(x, mod3, wg, wu, wd, ls, lng, lnb):
    S, D = x.shape
    Fs = wg.shape[-2]
    ts = 2 * TOK_TILE

    def body(x_ref, mod_ref, wg_ref, wu_ref, wd_ref, lng_ref, lnb_ref, xo_ref, f_ref, g_ref, u_ref, h_ref, acc_sc):
        j = pl.program_id(1)

        @pl.when(j == 0)
        def _():
            xh, _ = _ln_stats(x_ref[...])
            h_ref[...] = (xh * (1.0 + mod_ref[1:2, :]) + mod_ref[0:1, :]).astype(bf16)
            acc_sc[...] = jnp.zeros_like(acc_sc)

        h = h_ref[...]
        g = _dot_nt(h, wg_ref[0, 0])
        u = _dot_nt(h, wu_ref[0, 0])
        g_ref[0] = g.astype(bf16)
        u_ref[0] = u.astype(bf16)
        a = (g * jax.nn.sigmoid(g) * u).astype(bf16)
        acc_sc[...] += _dot(a, wd_ref[0, 0])

        @pl.when(j == N_CHIPS - 1)
        def _():
            f = acc_sc[...]
            f_ref[...] = f
            r = ALPHA * x_ref[...] + (FFN_RES * mod_ref[2:3, :]) * f
            rh, _ = _ln_stats(r)
            xo_ref[...] = rh * lng_ref[...] + lnb_ref[...]

    tok = pl.BlockSpec((ts, D), lambda i, j: (i, 0))
    wrow = pl.BlockSpec((1, 1, Fs, D), lambda i, j: (j, ls, 0, 0))
    hid = pl.BlockSpec((1, ts, Fs), lambda i, j: (j, i, 0))
    return pl.pallas_call(
        body, name="ffn_fwd", grid=(S // ts, N_CHIPS),
        in_specs=[tok, _full((3, D)), wrow, wrow, wrow, _full((1, D)), _full((1, D))],
        out_specs=[tok, tok, hid, hid, tok],
        out_shape=[jax.ShapeDtypeStruct((S, D), f32), jax.ShapeDtypeStruct((S, D), f32),
                   jax.ShapeDtypeStruct((N_CHIPS, S, Fs), bf16), jax.ShapeDtypeStruct((N_CHIPS, S, Fs), bf16),
                   jax.ShapeDtypeStruct((S, D), bf16)],
        scratch_shapes=[pltpu.VMEM((ts, D), f32)],
        compiler_params=_cp(("parallel", "arbitrary"), 56),
    )(*_hbm(x, mod3, wg, wu, wd, lng, lnb))


def _ffn_bwd(dxo, x, f, g, u, mod3, wg, wu, wd, ls, lng):
    S, D = x.shape
    Fs = wg.shape[-2]
    ts = TOK_TILE

    def body(dxo_ref, x_ref, f_ref, g_ref, u_ref, mod_ref, wg_ref, wu_ref, wd_ref, lng_ref,
             dx_ref, dg_ref, du_ref, a_ref, df_ref, dmod_ref, dlng_ref, dlnb_ref,
             dr_sc, df_sc, acc_sc):
        i = pl.program_id(0)
        j = pl.program_id(1)

        @pl.when((i == 0) & (j == 0))
        def _():
            dmod_ref[...] = jnp.zeros_like(dmod_ref)
            dlng_ref[...] = jnp.zeros_like(dlng_ref)
            dlnb_ref[...] = jnp.zeros_like(dlnb_ref)

        @pl.when(j == 0)
        def _():
            xv = x_ref[...]
            fv = f_ref[...]
            gate = mod_ref[2:3, :]
            rh, rstd = _ln_stats(ALPHA * xv + (FFN_RES * gate) * fv)
            dy = dxo_ref[...]
            dlng_ref[...] += jnp.sum(dy * rh, 0, keepdims=True)
            dlnb_ref[...] += jnp.sum(dy, 0, keepdims=True)
            dr = _ln_bwd(dy * lng_ref[...], rh, rstd)
            dr_sc[...] = dr
            dmod_ref[2:3, :] += jnp.sum(FFN_RES * dr * fv, 0, keepdims=True)
            df = ((FFN_RES * gate) * dr).astype(bf16)
            df_sc[...] = df
            df_ref[...] = df
            acc_sc[...] = jnp.zeros_like(acc_sc)

        da = _dot_nt(df_sc[...], wd_ref[0, 0])
        gv = g_ref[0].astype(f32)
        uv = u_ref[0].astype(f32)
        sg = jax.nn.sigmoid(gv)
        si = gv * sg
        a_ref[0] = (si * uv).astype(bf16)
        dgv = (da * uv * (sg * (1.0 + gv * (1.0 - sg)))).astype(bf16)
        duv = (da * si).astype(bf16)
        dg_ref[0] = dgv
        du_ref[0] = duv
        acc_sc[...] += _dot(dgv, wg_ref[0, 0]) + _dot(duv, wu_ref[0, 0])

        @pl.when(j == N_CHIPS - 1)
        def _():
            dh = acc_sc[...]
            xh, rstd0 = _ln_stats(x_ref[...])
            dmod_ref[0:1, :] += jnp.sum(dh, 0, keepdims=True)
            dmod_ref[1:2, :] += jnp.sum(dh * xh, 0, keepdims=True)
            dx_ref[...] = _ln_bwd(dh * (1.0 + mod_ref[1:2, :]), xh, rstd0) + ALPHA * dr_sc[...]

    tok = pl.BlockSpec((ts, D), lambda i, j: (i, 0))
    wrow = pl.BlockSpec((1, 1, Fs, D), lambda i, j: (j, ls, 0, 0))
    hid = pl.BlockSpec((1, ts, Fs), lambda i, j: (j, i, 0))
    hid_shape = jax.ShapeDtypeStruct((N_CHIPS, S, Fs), bf16)
    return pl.pallas_call(
        body, name="ffn_bwd", grid=(S // ts, N_CHIPS),
        in_specs=[tok, tok, tok, hid, hid, _full((3, D)), wrow, wrow, wrow, _full((1, D))],
        out_specs=[tok, hid, hid, hid, tok, _full((3, D)), _full((1, D)), _full((1, D))],
        out_shape=[jax.ShapeDtypeStruct((S, D), f32), hid_shape, hid_shape, hid_shape,
                   jax.ShapeDtypeStruct((S, D), bf16),
                   jax.ShapeDtypeStruct((3, D), f32), jax.ShapeDtypeStruct((1, D), f32), jax.ShapeDtypeStruct((1, D), f32)],
        scratch_shapes=[pltpu.VMEM((ts, D), f32), pltpu.VMEM((ts, D), bf16), pltpu.VMEM((ts, D), f32)],
        compiler_params=_cp(("arbitrary", "arbitrary"), 56),
    )(*_hbm(dxo, x, f, g, u, mod3, wg, wu, wd, lng))


def _ffn_wgrad(h, dg, du, a, df, gwg, gwu, gwd, ls):
    S, D = h.shape
    Fs = dg.shape[-1]
    tk = 2 * TOK_TILE
    nk = S // tk

    def body(h_ref, dg_ref, du_ref, a_ref, df_ref, _g0, _g1, _g2, gwg_ref, gwu_ref, gwd_ref, ag_sc, au_sc, ad_sc):
        k = pl.program_id(1)

        @pl.when(k == 0)
        def _():
            ag_sc[...] = jnp.zeros_like(ag_sc)
            au_sc[...] = jnp.zeros_like(au_sc)
            ad_sc[...] = jnp.zeros_like(ad_sc)

        hv = h_ref[...]
        ag_sc[...] += _dot_tn(dg_ref[0], hv)
        au_sc[...] += _dot_tn(du_ref[0], hv)
        ad_sc[...] += _dot_tn(a_ref[0], df_ref[...])

        @pl.when(k == nk - 1)
        def _():
            gwg_ref[0, 0] = ag_sc[...].astype(bf16)
            gwu_ref[0, 0] = au_sc[...].astype(bf16)
            gwd_ref[0, 0] = ad_sc[...].astype(bf16)

    tok = pl.BlockSpec((tk, D), lambda p, k: (k, 0))
    hid = pl.BlockSpec((1, tk, Fs), lambda p, k: (p, k, 0))
    anyspec = pl.BlockSpec(memory_space=pl.ANY)
    orow = pl.BlockSpec((1, 1, Fs, D), lambda p, k: (p, ls, 0, 0))
    return pl.pallas_call(
        body, name="ffn_wgrad", grid=(N_CHIPS, nk),
        in_specs=[tok, hid, hid, hid, tok, anyspec, anyspec, anyspec],
        out_specs=[orow, orow, orow],
        out_shape=[jax.ShapeDtypeStruct(gwg.shape, bf16), jax.ShapeDtypeStruct(gwu.shape, bf16),
                   jax.ShapeDtypeStruct(gwd.shape, bf16)],
        scratch_shapes=[pltpu.VMEM((Fs, D), f32), pltpu.VMEM((Fs, D), f32), pltpu.VMEM((Fs, D), f32)],
        input_output_aliases={5: 0, 6: 1, 7: 2},
        compiler_params=_cp(("parallel", "arbitrary"), 48),
    )(*_hbm(h, dg, du, a, df, gwg, gwu, gwd))


_LANES = 128
_QKV_BLOCKS = D_ATT // _LANES


def _res_spec(lead, d, width, index):
    return pl.BlockSpec((lead, d, TOK_TILE // d, width), index)


def _res_spec3(d, width):
    return pl.BlockSpec((d, TOK_TILE // d, width), lambda i: (0, i, 0))


def _rows_to_residues(tile_bufs, d, put):
    for r in range(d):
        for cb, buf in enumerate(tile_bufs):
            put(r, cb, buf[pl.ds(r, TOK_TILE // d, stride=d), :])


def _residues_to_rows(tile_bufs, d, get):
    for r in range(d):
        for cb, buf in enumerate(tile_bufs):
            buf[pl.ds(r, TOK_TILE // d, stride=d), :] = get(r, cb)


def _mix_in_fwd(x, mod3, w_in, l):
    S, D = x.shape
    N = w_in.shape[-1]
    ts = TOK_TILE

    def body(x_ref, mod_ref, w_ref, o1_ref, o4_ref, o16_ref, zr_ref, h_ref, *bufs):
        j = pl.program_id(1)

        @pl.when(j == 0)
        def _():
            xh, _ = _ln_stats(x_ref[...])
            h_ref[...] = (xh * (1.0 + mod_ref[1:2, :]) + mod_ref[0:1, :]).astype(bf16)

        z = _dot(h_ref[...], w_ref[0, 0])

        @pl.when(j == N_CHIPS - 1)
        def _():
            zr_ref[...] = z

        @pl.when(j < N_CHIPS - 1)
        def _():
            zz = z * jnp.where(j == 0, HEAD_DIM ** -0.5, 1.0)
            o1_ref[j, 0] = zz.astype(bf16)
            for cb, buf in enumerate(bufs):
                buf[...] = zz[:, _LANES * cb:_LANES * (cb + 1)]
            for d, o_ref in zip(DILATIONS[1:], (o4_ref, o16_ref)):
                def put(r, cb, piece, o_ref=o_ref):
                    o_ref[j, r, :, _LANES * cb:_LANES * (cb + 1)] = piece.astype(bf16)
                _rows_to_residues(bufs, d, put)

    tok = pl.BlockSpec((ts, D), lambda i, j: (i, 0))
    res = [_res_spec(3, d, N, lambda i, j: (0, 0, i, 0)) for d in DILATIONS]
    return pl.pallas_call(
        body, name="mix_in_fwd", grid=(S // ts, N_CHIPS),
        in_specs=[tok, _full((3, D)), pl.BlockSpec((1, 1, D, N), lambda i, j: (j, l, 0, 0))],
        out_specs=res + [pl.BlockSpec((ts, N), lambda i, j: (i, 0)), tok],
        out_shape=[jax.ShapeDtypeStruct((3, d, S // d, N), bf16) for d in DILATIONS]
        + [jax.ShapeDtypeStruct((S, N), f32), jax.ShapeDtypeStruct((S, D), bf16)],
        scratch_shapes=[pltpu.VMEM((ts, _LANES), f32)] * _QKV_BLOCKS,
        compiler_params=_cp(("parallel", "arbitrary"), 40),
    )(*_hbm(x, mod3, w_in))


def _mix_in_bwd(dqkv, d_rest, dx_res, x, mod3, w_in, l):
    S, D = x.shape
    N = w_in.shape[-1]
    ts = TOK_TILE

    def body(d1_ref, d4_ref, d16_ref, dr_ref, dxr_ref, x_ref, mod_ref, w_ref, dx_ref, dmod_ref, dz_ref, acc_sc, *bufs):
        i = pl.program_id(0)
        j = pl.program_id(1)

        @pl.when((i == 0) & (j == 0))
        def _():
            dmod_ref[...] = jnp.zeros_like(dmod_ref)

        @pl.when(j == 0)
        def _():
            acc_sc[...] = jnp.zeros_like(acc_sc)

        @pl.when(j == N_CHIPS - 1)
        def _():
            dz_ref[0] = dr_ref[...]

        @pl.when(j < N_CHIPS - 1)
        def _():
            for d, d_ref, tile_bufs in ((4, d4_ref, bufs[:_QKV_BLOCKS]), (16, d16_ref, bufs[_QKV_BLOCKS:])):
                _residues_to_rows(tile_bufs, d, lambda r, cb, d_ref=d_ref: d_ref[0, r, :, _LANES * cb:_LANES * (cb + 1)].astype(f32))
            for cb in range(_QKV_BLOCKS):
                cols = slice(_LANES * cb, _LANES * (cb + 1))
                dz_ref[0, :, cols] = (d1_ref[0, 0, :, cols].astype(f32) + bufs[cb][...] + bufs[_QKV_BLOCKS + cb][...]).astype(bf16)

        acc_sc[...] += _dot_nt(dz_ref[0], w_ref[0, 0])

        @pl.when(j == N_CHIPS - 1)
        def _():
            dh = acc_sc[...]
            xh, rstd0 = _ln_stats(x_ref[...])
            dmod_ref[0:1, :] += jnp.sum(dh, 0, keepdims=True)
            dmod_ref[1:2, :] += jnp.sum(dh * xh, 0, keepdims=True)
            dx_ref[...] = _ln_bwd(dh * (1.0 + mod_ref[1:2, :]), xh, rstd0) + dxr_ref[...]

    tok = pl.BlockSpec((ts, D), lambda i, j: (i, 0))
    res = [_res_spec(1, d, N, lambda i, j: (jnp.minimum(j, 2), 0, i, 0)) for d in DILATIONS]
    return pl.pallas_call(
        body, name="mix_in_bwd", grid=(S // ts, N_CHIPS),
        in_specs=res + [pl.BlockSpec((ts, N), lambda i, j: (i, 0)), tok, tok, _full((3, D)),
                        pl.BlockSpec((1, 1, D, N), lambda i, j: (j, l, 0, 0))],
        out_specs=[tok, _full((3, D)), pl.BlockSpec((1, ts, N), lambda i, j: (j, i, 0))],
        out_shape=[jax.ShapeDtypeStruct((S, D), f32), jax.ShapeDtypeStruct((3, D), f32),
                   jax.ShapeDtypeStruct((N_CHIPS, S, N), bf16)],
        scratch_shapes=[pltpu.VMEM((ts, D), f32)] + [pltpu.VMEM((ts, _LANES), f32)] * (2 * _QKV_BLOCKS),
        compiler_params=_cp(("arbitrary", "arbitrary"), 40),
    )(*_hbm(*dqkv, d_rest, dx_res, x, mod3, w_in))


def _mix_in_wgrad(h, dz, gw, l):
    S, D = h.shape
    N = dz.shape[-1]
    tk = 2 * TOK_TILE
    nk = S // tk

    def body(h_ref, dz_ref, _g, gw_ref, acc_sc):
        k = pl.program_id(1)

        @pl.when(k == 0)
        def _():
            acc_sc[...] = jnp.zeros_like(acc_sc)

        acc_sc[...] += _dot_tn(h_ref[...], dz_ref[0])

        @pl.when(k == nk - 1)
        def _():
            gw_ref[0, 0] = acc_sc[...].astype(bf16)

    return pl.pallas_call(
        body, name="mix_in_wgrad", grid=(N_CHIPS, nk),
        in_specs=[pl.BlockSpec((tk, D), lambda p, k: (k, 0)), pl.BlockSpec((1, tk, N), lambda p, k: (p, k, 0)),
                  pl.BlockSpec(memory_space=pl.ANY)],
        out_specs=pl.BlockSpec((1, 1, D, N), lambda p, k: (p, l, 0, 0)),
        out_shape=jax.ShapeDtypeStruct(gw.shape, bf16),
        scratch_shapes=[pltpu.VMEM((D, N), f32)],
        input_output_aliases={2: 0},
        compiler_params=_cp(("parallel", "arbitrary"), 40),
    )(*_hbm(h, dz, gw))


def _mix_out_fwd(x, y_att, y_ssm, y_pool, mod3, w_out, l, lng, lnb):
    S, D = x.shape
    ts = TOK_TILE

    def body(x_ref, ya_ref, ys_ref, yp_ref, mod_ref, w_ref, lng_ref, lnb_ref, xo_ref, y_ref):
        ya = ya_ref[...].astype(bf16)
        y = (_dot(ya[:, 0:256], w_ref[0, 0]) + _dot(ya[:, 256:512], w_ref[1, 0])
             + _dot(ys_ref[...].astype(bf16), w_ref[2, 0]) + _dot(yp_ref[...].astype(bf16), w_ref[3, 0]))
        y_ref[...] = y
        rh, _ = _ln_stats(ALPHA * x_ref[...] + mod_ref[2:3, :] * y)
        xo_ref[...] = rh * lng_ref[...] + lnb_ref[...]

    tok = pl.BlockSpec((ts, D), lambda i: (i, 0))
    return pl.pallas_call(
        body, name="mix_out_fwd", grid=(S // ts,),
        in_specs=[tok, pl.BlockSpec((ts, D_ATT), lambda i: (i, 0)), pl.BlockSpec((ts, D_SSM), lambda i: (i, 0)),
                  pl.BlockSpec((ts, D_POOL), lambda i: (i, 0)), _full((3, D)),
                  pl.BlockSpec((N_CHIPS, 1, 256, D), lambda i: (0, l, 0, 0)), _full((1, D)), _full((1, D))],
        out_specs=[tok, tok],
        out_shape=[jax.ShapeDtypeStruct((S, D), f32), jax.ShapeDtypeStruct((S, D), f32)],
        compiler_params=_cp(("parallel",), 40),
    )(*_hbm(x, y_att, y_ssm, y_pool, mod3, w_out, lng, lnb))


def _mix_out_bwd(dxo, x, y, y_att, y_ssm, y_pool, mod3, w_out, l, lng, gw_out):
    S, D = x.shape
    ts = TOK_TILE
    nt = S // ts

    def body(dxo_ref, x_ref, y_ref, ya_ref, ys_ref, yp_ref, mod_ref, w_ref, lng_ref, _g,
             dxr_ref, da_ref, ds_ref, dp_ref, dgate_ref, dlng_ref, dlnb_ref, gw_ref, acc_sc):
        i = pl.program_id(0)

        @pl.when(i == 0)
        def _():
            dgate_ref[...] = jnp.zeros_like(dgate_ref)
            dlng_ref[...] = jnp.zeros_like(dlng_ref)
            dlnb_ref[...] = jnp.zeros_like(dlnb_ref)
            acc_sc[...] = jnp.zeros_like(acc_sc)

        gate = mod_ref[2:3, :]
        yv = y_ref[...]
        rh, rstd = _ln_stats(ALPHA * x_ref[...] + gate * yv)
        dy_out = dxo_ref[...]
        dlng_ref[...] += jnp.sum(dy_out * rh, 0, keepdims=True)
        dlnb_ref[...] += jnp.sum(dy_out, 0, keepdims=True)
        dr = _ln_bwd(dy_out * lng_ref[...], rh, rstd)
        dxr_ref[...] = ALPHA * dr
        dgate_ref[...] += jnp.sum(dr * yv, 0, keepdims=True)
        dy = (gate * dr).astype(bf16)
        da_ref[:, 0:256] = _dot_nt(dy, w_ref[0, 0])
        da_ref[:, 256:512] = _dot_nt(dy, w_ref[1, 0])
        ds_ref[...] = _dot_nt(dy, w_ref[2, 0])
        dp_ref[...] = _dot_nt(dy, w_ref[3, 0])
        ya = ya_ref[...].astype(bf16)
        acc_sc[0] += _dot_tn(ya[:, 0:256], dy)
        acc_sc[1] += _dot_tn(ya[:, 256:512], dy)
        acc_sc[2] += _dot_tn(ys_ref[...].astype(bf16), dy)
        acc_sc[3] += _dot_tn(yp_ref[...].astype(bf16), dy)

        @pl.when(i == nt - 1)
        def _():
            gw_ref[:, 0] = acc_sc[...].astype(bf16)

    tok = pl.BlockSpec((ts, D), lambda i: (i, 0))
    t512 = pl.BlockSpec((ts, D_ATT), lambda i: (i, 0))
    t256 = pl.BlockSpec((ts, 256), lambda i: (i, 0))
    wspec = pl.BlockSpec((N_CHIPS, 1, 256, D), lambda i: (0, l, 0, 0))
    return pl.pallas_call(
        body, name="mix_out_bwd", grid=(nt,),
        in_specs=[tok, tok, tok, t512, t256, t256, _full((3, D)), wspec, _full((1, D)), pl.BlockSpec(memory_space=pl.ANY)],
        out_specs=[tok, t512, t256, t256, _full((1, D)), _full((1, D)), _full((1, D)), wspec],
        out_shape=[jax.ShapeDtypeStruct((S, D), f32), jax.ShapeDtypeStruct((S, D_ATT), f32),
                   jax.ShapeDtypeStruct((S, D_SSM), f32), jax.ShapeDtypeStruct((S, D_POOL), f32),
                   jax.ShapeDtypeStruct((1, D), f32), jax.ShapeDtypeStruct((1, D), f32), jax.ShapeDtypeStruct((1, D), f32),
                   jax.ShapeDtypeStruct(gw_out.shape, bf16)],
        scratch_shapes=[pltpu.VMEM((N_CHIPS, 256, D), f32)],
        input_output_aliases={9: 7},
        compiler_params=_cp(("arbitrary",), 48),
    )(*_hbm(dxo, x, y, y_att, y_ssm, y_pool, mod3, w_out, lng, gw_out))


def _t5_bucket(dist):
    max_exact = N_BUCKETS // 2
    d = np.maximum(dist, 1).astype(np.float32)
    large = max_exact + (np.log(d / max_exact) / math.log(MAX_DISTANCE / max_exact)
                         * (N_BUCKETS - max_exact)).astype(np.int32)
    large = np.minimum(large, N_BUCKETS - 1)
    return np.where(dist < max_exact, dist, large).astype(np.int32)


def _bucket_table():
    q = ATT_BLOCK
    i = np.arange(q)[:, None]
    j = np.arange(2 * q)[None, :]
    r = i + q - j
    in_band = (r >= 0) & (r <= q)
    tabs = [np.where(in_band, _t5_bucket(np.clip(r, 0, None) * d), -1) for d in DILATIONS]
    return np.stack(tabs).astype(np.int32)


def _bias_fwd(rel_bias, table):
    def body(rb_ref, tab_ref, out_ref):
        for b in range(3):
            tb = tab_ref[b]
            for h in range(N_HEADS):
                def pick(k, acc):
                    return jnp.where(tb == k, rb_ref[k, h], acc)
                out_ref[b, h] = lax.fori_loop(0, N_BUCKETS, pick, jnp.where(tb < 0, NEG, 0.0).astype(f32))

    return pl.pallas_call(
        body, name="bias_fwd",
        in_specs=[pl.BlockSpec(memory_space=pltpu.SMEM), pl.BlockSpec(memory_space=pltpu.VMEM)],
        out_specs=pl.BlockSpec(memory_space=pltpu.VMEM),
        out_shape=jax.ShapeDtypeStruct((3, N_HEADS, ATT_BLOCK, 2 * ATT_BLOCK), f32),
    )(rel_bias, table)


def _bias_bwd(dbias, table):
    def body(db_ref, tab_ref, out_ref):
        def per_bucket(k, c):
            for h in range(N_HEADS):
                tot = jnp.zeros((), f32)
                for b in range(3):
                    tot = tot + jnp.sum(jnp.where(tab_ref[b] == k, db_ref[b, h], 0.0))
                out_ref[k, h] = tot
            return c
        lax.fori_loop(0, N_BUCKETS, per_bucket, 0)

    return pl.pallas_call(
        body, name="bias_bwd",
        in_specs=[pl.BlockSpec(memory_space=pltpu.VMEM), pl.BlockSpec(memory_space=pltpu.VMEM)],
        out_specs=pl.BlockSpec(memory_space=pltpu.SMEM),
        out_shape=jax.ShapeDtypeStruct((N_BUCKETS, N_HEADS), f32),
    )(dbias, table)


def _att_unit(u, nbr):
    rows = pl.ds(pl.multiple_of(u * ATT_BLOCK, ATT_BLOCK), ATT_BLOCK)
    prev = pl.ds(pl.multiple_of(jnp.maximum(u - 1, 0) * ATT_BLOCK, ATT_BLOCK), ATT_BLOCK)
    return rows, prev, (u % nbr) != 0


_N_PAIRS = N_HEADS // 2


def _pair_rows(t):
    lane = lax.broadcasted_iota(jnp.int32, t.shape, 1)
    zero = jnp.zeros_like(t)
    return jnp.concatenate([jnp.where(lane < HEAD_DIM, t, zero), jnp.where(lane >= HEAD_DIM, t, zero)], axis=0)


def _pair_cols(big):
    lane = lax.broadcasted_iota(jnp.int32, (ATT_BLOCK, _LANES), 1)
    return jnp.where(lane < HEAD_DIM, big[:ATT_BLOCK], big[ATT_BLOCK:])


def _pair_column(ref, rows, hp):
    t = ref[rows, :]
    return jnp.concatenate([t[:, 2 * hp:2 * hp + 1], t[:, 2 * hp + 1:2 * hp + 2]], axis=0)


def _pair_band(ref, rows, prev, nbr, hp):
    lanes = pl.ds(_LANES * hp, _LANES)
    cur = ref[0, rows, lanes]
    return cur if nbr == 1 else jnp.concatenate([ref[0, prev, lanes], cur], axis=0)


def _pair_scores(q_ref, k_ref, b_ref, rows, prev, valid_prev, nbr, hp):
    qbd = _pair_rows(q_ref[0, rows, pl.ds(_LANES * hp, _LANES)])
    kb = _pair_band(k_ref, rows, prev, nbr, hp)
    bias = b_ref[0, 2 * hp:2 * hp + 2].reshape(2 * ATT_BLOCK, 2 * ATT_BLOCK)
    if nbr == 1:
        return qbd, kb, _dot_nt(qbd, kb) + bias[:, ATT_BLOCK:]
    s = _dot_nt(qbd, kb) + bias
    col = lax.broadcasted_iota(jnp.int32, s.shape, 1)
    return qbd, kb, jnp.where((col >= ATT_BLOCK) | valid_prev, s, NEG)


def _qkv_specs(S, branch):
    return ([pl.BlockSpec((1, S, D_ATT), lambda i, t=t: (t, 0, 0)) for t in range(3)],
            pl.BlockSpec((1, N_HEADS, ATT_BLOCK, 2 * ATT_BLOCK), lambda i: (branch, 0, 0, 0)))


def _att_fwd(qkv, bias, branch):
    S = qkv.shape[1]
    nbr = BLOCKS_PER_RESIDUE[branch]

    def body(q_ref, k_ref, v_ref, b_ref, o_ref, lse_ref):
        lse_ref[...] = jnp.zeros_like(lse_ref)

        def unit(u, c):
            rows, prev, valid_prev = _att_unit(u, nbr)
            for hp in range(_N_PAIRS):
                _, _, s = _pair_scores(q_ref, k_ref, b_ref, rows, prev, valid_prev, nbr, hp)
                m = jnp.max(s, -1, keepdims=True)
                p = jnp.exp(s - m)
                den = jnp.sum(p, -1, keepdims=True)
                big = _dot(p.astype(bf16), _pair_band(v_ref, rows, prev, nbr, hp))
                o_ref[rows, pl.ds(_LANES * hp, _LANES)] = _pair_cols(big / den)
                lse = m + jnp.log(den)
                lse_ref[rows, pl.ds(2 * hp, 1)] = lse[:ATT_BLOCK]
                lse_ref[rows, pl.ds(2 * hp + 1, 1)] = lse[ATT_BLOCK:]
            return c

        lax.fori_loop(0, N_UNITS, unit, 0)

    qkv_specs, bspec = _qkv_specs(S, branch)
    return pl.pallas_call(
        body, name="att_fwd", grid=(1,),
        in_specs=qkv_specs + [bspec],
        out_specs=[pl.BlockSpec((S, D_ATT), lambda i: (0, 0)), pl.BlockSpec((S, _LANES), lambda i: (0, 0))],
        out_shape=[jax.ShapeDtypeStruct((S, D_ATT), f32), jax.ShapeDtypeStruct((S, _LANES), f32)],
        compiler_params=_cp(("arbitrary",), 40),
    )(*_hbm(qkv, qkv, qkv, bias))


def _att_bwd(qkv, do, lse, crow, bias, branch):
    S = qkv.shape[1]
    nbr = BLOCKS_PER_RESIDUE[branch]

    def body(q_ref, k_ref, v_ref, do_ref, lse_ref, c_ref, b_ref, dqkv_ref, db_ref, dk_sc, dv_sc):
        dk_sc[...] = jnp.zeros_like(dk_sc)
        dv_sc[...] = jnp.zeros_like(dv_sc)
        db_ref[...] = jnp.zeros_like(db_ref)

        def unit(u, c):
            rows, prev, valid_prev = _att_unit(u, nbr)
            for hp in range(_N_PAIRS):
                lanes = pl.ds(_LANES * hp, _LANES)
                qbd, kb, s = _pair_scores(q_ref, k_ref, b_ref, rows, prev, valid_prev, nbr, hp)
                p = jnp.exp(s - _pair_column(lse_ref, rows, hp))
                dobd = _pair_rows(do_ref[rows, lanes])
                ds = p * (_dot_nt(dobd, _pair_band(v_ref, rows, prev, nbr, hp)) - _pair_column(c_ref, rows, hp))
                if nbr == 1:
                    db_ref[2 * hp:2 * hp + 2, :, ATT_BLOCK:] += ds.reshape(2, ATT_BLOCK, ATT_BLOCK)
                else:
                    db_ref[2 * hp:2 * hp + 2] += ds.reshape(2, ATT_BLOCK, 2 * ATT_BLOCK)
                dsb = ds.astype(bf16)
                dqkv_ref[0, rows, lanes] = (HEAD_DIM ** -0.5 * _pair_cols(_dot(dsb, kb))).astype(bf16)
                dkb = _dot_tn(dsb, qbd)
                dvb = _dot_tn(p.astype(bf16), dobd)
                if nbr == 1:
                    dk_sc[rows, lanes] += dkb
                    dv_sc[rows, lanes] += dvb
                else:
                    dk_sc[prev, lanes] += dkb[:ATT_BLOCK]
                    dv_sc[prev, lanes] += dvb[:ATT_BLOCK]
                    dk_sc[rows, lanes] += dkb[ATT_BLOCK:]
                    dv_sc[rows, lanes] += dvb[ATT_BLOCK:]
            return c

        lax.fori_loop(0, N_UNITS, unit, 0)
        dqkv_ref[1] = dk_sc[...].astype(bf16)
        dqkv_ref[2] = dv_sc[...].astype(bf16)

    qkv_specs, bspec = _qkv_specs(S, branch)
    row = pl.BlockSpec((S, _LANES), lambda i: (0, 0))
    return pl.pallas_call(
        body, name="att_bwd", grid=(1,),
        in_specs=qkv_specs + [pl.BlockSpec((S, D_ATT), lambda i: (0, 0)), row, row, bspec],
        out_specs=[pl.BlockSpec((3, S, D_ATT), lambda i: (0, 0, 0)),
                   pl.BlockSpec((N_HEADS, ATT_BLOCK, 2 * ATT_BLOCK), lambda i: (0, 0, 0))],
        out_shape=[jax.ShapeDtypeStruct((3, S, D_ATT), bf16), jax.ShapeDtypeStruct((N_HEADS, ATT_BLOCK, 2 * ATT_BLOCK), f32)],
        scratch_shapes=[pltpu.VMEM((S, D_ATT), f32), pltpu.VMEM((S, D_ATT), f32)],
        compiler_params=_cp(("arbitrary",), 48),
    )(*_hbm(qkv, qkv, qkv, do, lse, crow, bias))


def _branch_weights(lse_ref):
    l0, l1, l2 = lse_ref[0], lse_ref[1], lse_ref[2]
    m = jnp.maximum(jnp.maximum(l0, l1), l2)
    e0, e1, e2 = jnp.exp(l0 - m), jnp.exp(l1 - m), jnp.exp(l2 - m)
    tot = e0 + e1 + e2
    return e0 / tot, e1 / tot, e2 / tot


def _att_merge(os, lses):
    S = os[0].shape[0] * os[0].shape[1]
    ts = TOK_TILE

    def body(o1_ref, o4_ref, o16_ref, l1_ref, l4_ref, l16_ref, y_ref, lt_ref, *bufs):
        obufs = (bufs[:_QKV_BLOCKS], bufs[_QKV_BLOCKS:2 * _QKV_BLOCKS])
        lt_ref[0] = l1_ref[0]
        for k, (d, o_ref, l_ref) in enumerate(((4, o4_ref, l4_ref), (16, o16_ref, l16_ref))):
            _residues_to_rows(obufs[k], d, lambda r, cb, o_ref=o_ref: o_ref[r, :, _LANES * cb:_LANES * (cb + 1)])
            _residues_to_rows([bufs[2 * _QKV_BLOCKS + k]], d, lambda r, cb, l_ref=l_ref: l_ref[r])
            lt_ref[1 + k] = bufs[2 * _QKV_BLOCKS + k][...]
        w = _branch_weights(lt_ref)
        for h in range(N_HEADS):
            cs = slice(HEAD_DIM * h, HEAD_DIM * (h + 1))
            half = slice(HEAD_DIM * (h % 2), HEAD_DIM * (h % 2 + 1))
            y_ref[:, cs] = (w[0][:, h:h + 1] * o1_ref[0, :, cs] + w[1][:, h:h + 1] * obufs[0][h // 2][:, half]
                            + w[2][:, h:h + 1] * obufs[1][h // 2][:, half])

    return pl.pallas_call(
        body, name="att_merge", grid=(S // ts,),
        in_specs=[_res_spec3(d, D_ATT) for d in DILATIONS] + [_res_spec3(d, _LANES) for d in DILATIONS],
        out_specs=[pl.BlockSpec((ts, D_ATT), lambda i: (i, 0)), pl.BlockSpec((3, ts, _LANES), lambda i: (0, i, 0))],
        out_shape=[jax.ShapeDtypeStruct((S, D_ATT), f32), jax.ShapeDtypeStruct((3, S, _LANES), f32)],
        scratch_shapes=[pltpu.VMEM((ts, _LANES), f32)] * (2 * _QKV_BLOCKS + 2),
        compiler_params=_cp(("parallel",)),
    )(*_hbm(*os, *lses))


def _att_merge_bwd(dy, y, lse3):
    S = dy.shape[0]
    ts = TOK_TILE

    def body(dy_ref, y_ref, lse_ref, do1_ref, do4_ref, do16_ref, c1_ref, c4_ref, c16_ref, *bufs):
        dobufs = (bufs[:_QKV_BLOCKS], bufs[_QKV_BLOCKS:2 * _QKV_BLOCKS], bufs[2 * _QKV_BLOCKS:3 * _QKV_BLOCKS])
        cbufs = bufs[3 * _QKV_BLOCKS:]
        w = _branch_weights(lse_ref)
        for cb in cbufs:
            cb[...] = jnp.zeros_like(cb)
        for h in range(N_HEADS):
            cs = slice(HEAD_DIM * h, HEAD_DIM * (h + 1))
            half = slice(HEAD_DIM * (h % 2), HEAD_DIM * (h % 2 + 1))
            dyh = dy_ref[:, cs]
            t = jnp.sum(dyh * y_ref[:, cs], -1, keepdims=True)
            for p in range(3):
                wp = w[p][:, h:h + 1]
                dobufs[p][h // 2][:, half] = wp * dyh
                cbufs[p][:, h:h + 1] = wp * t
        for cb in range(_QKV_BLOCKS):
            do1_ref[0, :, _LANES * cb:_LANES * (cb + 1)] = dobufs[0][cb][...].astype(bf16)
        c1_ref[0] = cbufs[0][...]
        for k, (d, do_ref, c_ref) in enumerate(((4, do4_ref, c4_ref), (16, do16_ref, c16_ref))):
            def put_do(r, cb, piece, do_ref=do_ref):
                do_ref[r, :, _LANES * cb:_LANES * (cb + 1)] = piece.astype(bf16)

            def put_c(r, cb, piece, c_ref=c_ref):
                c_ref[r] = piece

            _rows_to_residues(dobufs[1 + k], d, put_do)
            _rows_to_residues([cbufs[1 + k]], d, put_c)

    return pl.pallas_call(
        body, name="att_merge_bwd", grid=(S // ts,),
        in_specs=[pl.BlockSpec((ts, D_ATT), lambda i: (i, 0)), pl.BlockSpec((ts, D_ATT), lambda i: (i, 0)),
                  pl.BlockSpec((3, ts, _LANES), lambda i: (0, i, 0))],
        out_specs=[_res_spec3(d, D_ATT) for d in DILATIONS] + [_res_spec3(d, _LANES) for d in DILATIONS],
        out_shape=[jax.ShapeDtypeStruct((d, S // d, D_ATT), bf16) for d in DILATIONS]
        + [jax.ShapeDtypeStruct((d, S // d, _LANES), f32) for d in DILATIONS],
        scratch_shapes=[pltpu.VMEM((ts, _LANES), f32)] * (3 * _QKV_BLOCKS + 3),
        compiler_params=_cp(("parallel",)),
    )(*_hbm(dy, y, lse3))


_SSM_ROWS = 256


def _scan_in_place(sr_ref, si_ref, a_ref, reverse):
    S, N = sr_ref.shape
    nst = S // SCAN_SEG
    ar = jnp.broadcast_to(a_ref[0:1, :], (SCAN_SEG, N))
    ai = jnp.broadcast_to(a_ref[1:2, :], (SCAN_SEG, N))
    if reverse:
        ai = -ai
    row = lax.broadcasted_iota(jnp.int32, (SCAN_SEG, N), 0)
    zero = jnp.zeros((SCAN_SEG, N), f32)

    def tile(t):
        return pl.ds(pl.multiple_of((nst - 1 - t if reverse else t) * SCAN_SEG, SCAN_SEG), SCAN_SEG)

    def local(t, c):
        sr, si, pr, pi = c
        rows = tile(t)
        nsr = ar * sr - ai * si + sr_ref[rows, :]
        nsi = ar * si + ai * sr + si_ref[rows, :]
        sr_ref[rows, :] = nsr
        si_ref[rows, :] = nsi
        return nsr, nsi, ar * pr - ai * pi, ar * pi + ai * pr

    fr, fi, apr, api = lax.fori_loop(0, nst, local, (zero, zero, zero + 1.0, zero))

    def shift(v):
        if reverse:
            return jnp.where(row == SCAN_SEG - 1, 0.0, pltpu.roll(v, SCAN_SEG - 1, axis=0))
        return jnp.where(row == 0, 0.0, pltpu.roll(v, 1, axis=0))

    cr, ci = zero, zero
    for _ in range(SCAN_SEG - 1):
        cr, ci = shift(fr + apr * cr - api * ci), shift(fi + apr * ci + api * cr)

    def fix(t, c):
        pr, pi = c
        npr, npi = ar * pr - ai * pi, ar * pi + ai * pr
        rows = tile(t)
        sr_ref[rows, :] += npr * cr - npi * ci
        si_ref[rows, :] += npr * ci + npi * cr
        return npr, npi

    lax.fori_loop(0, nst, fix, (zero + 1.0, zero))


def _ssm_fwd(u, bre, bim, a2, cre, cim, l, dskip, glu_w, glu_b):
    S = u.shape[0]
    nproj = S // _SSM_ROWS

    def body(u_ref, br_ref, bi_ref, a2_ref, cr_ref, ci_ref, d_ref, w_ref, b_ref, sr_ref, si_ref, out_ref, y_ref):
        a_ref = a2_ref.at[l]
        brb = br_ref[l].astype(bf16)
        bib = bi_ref[l].astype(bf16)

        def project(t, c):
            rows = pl.ds(pl.multiple_of(t * _SSM_ROWS, _SSM_ROWS), _SSM_ROWS)
            ub = u_ref[rows, :].astype(bf16)
            sr_ref[rows, :] = _dot(ub, brb)
            si_ref[rows, :] = _dot(ub, bib)
            return c

        lax.fori_loop(0, nproj, project, 0)
        _scan_in_place(sr_ref, si_ref, a_ref, False)

        crb = cr_ref[l].astype(bf16)
        cib = ci_ref[l].astype(bf16)
        wb = w_ref[...].astype(bf16)

        def read_out(t, c):
            rows = pl.ds(pl.multiple_of(t * _SSM_ROWS, _SSM_ROWS), _SSM_ROWS)
            y = (_dot(sr_ref[rows, :].astype(bf16), crb) - _dot(si_ref[rows, :].astype(bf16), cib)
                 + d_ref[...] * u_ref[rows, :])
            y_ref[rows, :] = y
            z = _dot(_gelu(y).astype(bf16), wb) + b_ref[...]
            out_ref[rows, :] = y * jax.nn.sigmoid(z)
            return c

        lax.fori_loop(0, nproj, read_out, 0)

    vm = pl.BlockSpec(memory_space=pltpu.VMEM)
    return pl.pallas_call(
        body, name="ssm_fwd", in_specs=[vm] * 9, out_specs=[vm] * 4,
        out_shape=[jax.ShapeDtypeStruct((S, D_STATE), f32)] * 2 + [jax.ShapeDtypeStruct((S, D_SSM), f32)] * 2,
        compiler_params=_cp(None, 48),
    )(u, bre, bim, a2, cre, cim, dskip, glu_w, glu_b)


def _ssm_out_bwd(dout, y, u, dskip, glu_w, glu_b):
    S = u.shape[0]
    ts = TOK_TILE

    def body(do_ref, y_ref, u_ref, d_ref, w_ref, b_ref, dy_ref, du_ref, dd_ref, dgb_ref, dgw_ref):
        @pl.when(pl.program_id(0) == 0)
        def _():
            for r in (dd_ref, dgb_ref, dgw_ref):
                r[...] = jnp.zeros_like(r)

        y = y_ref[...]
        dout = do_ref[...]
        wb = w_ref[...].astype(bf16)
        ge = _gelu(y).astype(bf16)
        sz = jax.nn.sigmoid(_dot(ge, wb) + b_ref[...])
        dz = dout * y * sz * (1.0 - sz)
        dzb = dz.astype(bf16)
        dgb_ref[...] += jnp.sum(dz, 0, keepdims=True)
        dgw_ref[...] += _dot_tn(ge, dzb)
        dy = dout * sz + _gelu_grad(y) * _dot_nt(dzb, wb)
        uv = u_ref[...]
        dd_ref[...] += jnp.sum(dy * uv, 0, keepdims=True)
        du_ref[...] = dy * d_ref[...]
        dy_ref[...] = dy

    ch = pl.BlockSpec((ts, D_SSM), lambda i: (i, 0))
    return pl.pallas_call(
        body, name="ssm_out_bwd", grid=(S // ts,),
        in_specs=[ch, ch, ch, _full((1, D_SSM)), _full((D_SSM, D_SSM)), _full((1, D_SSM))],
        out_specs=[ch, ch, _full((1, D_SSM)), _full((1, D_SSM)), _full((D_SSM, D_SSM))],
        out_shape=[jax.ShapeDtypeStruct((S, D_SSM), f32), jax.ShapeDtypeStruct((S, D_SSM), f32),
                   jax.ShapeDtypeStruct((1, D_SSM), f32), jax.ShapeDtypeStruct((1, D_SSM), f32),
                   jax.ShapeDtypeStruct((D_SSM, D_SSM), f32)],
        compiler_params=_cp(("arbitrary",), 40),
    )(*_hbm(dout, y, u, dskip, glu_w, glu_b))


def _ssm_states_bwd(dy, du_skip, u, sr, si, cre, cim, bre, bim, a2, l):
    S = u.shape[0]
    N = D_STATE
    nst = S // SCAN_SEG
    nproj = S // _SSM_ROWS

    def body(dy_ref, dus_ref, u_ref, sr_ref, si_ref, cr_ref, ci_ref, br_ref, bi_ref, a2_ref,
             du_ref, dbr_ref, dbi_ref, da_ref, dcr_ref, dci_ref, lr_ref, li_ref):
        a_ref = a2_ref.at[l]
        crb = cr_ref[l].astype(bf16)
        cib = ci_ref[l].astype(bf16)
        dcr_ref[...] = jnp.zeros_like(dcr_ref)
        dci_ref[...] = jnp.zeros_like(dci_ref)

        def project(t, c):
            rows = pl.ds(pl.multiple_of(t * _SSM_ROWS, _SSM_ROWS), _SSM_ROWS)
            dyb = dy_ref[rows, :].astype(bf16)
            lr_ref[rows, :] = _dot_nt(dyb, crb)
            li_ref[rows, :] = -_dot_nt(dyb, cib)
            dcr_ref[...] += _dot_tn(sr_ref[rows, :].astype(bf16), dyb)
            dci_ref[...] -= _dot_tn(si_ref[rows, :].astype(bf16), dyb)
            return c

        lax.fori_loop(0, nproj, project, 0)
        _scan_in_place(lr_ref, li_ref, a_ref, True)

        row = lax.broadcasted_iota(jnp.int32, (SCAN_SEG, N), 0)
        last = pl.ds((nst - 1) * SCAN_SEG, SCAN_SEG)
        pr = jnp.where(row == 0, 0.0, pltpu.roll(sr_ref[last, :], 1, axis=0))
        pi = jnp.where(row == 0, 0.0, pltpu.roll(si_ref[last, :], 1, axis=0))
        first = pl.ds(0, SCAN_SEG)
        acc_r = lr_ref[first, :] * pr + li_ref[first, :] * pi
        acc_i = li_ref[first, :] * pr - lr_ref[first, :] * pi

        def step(t, c):
            acc_r, acc_i = c
            rows = pl.ds(pl.multiple_of(t * SCAN_SEG, SCAN_SEG), SCAN_SEG)
            prev = pl.ds(pl.multiple_of((t - 1) * SCAN_SEG, SCAN_SEG), SCAN_SEG)
            lrv, liv, srv, siv = lr_ref[rows, :], li_ref[rows, :], sr_ref[prev, :], si_ref[prev, :]
            return acc_r + lrv * srv + liv * siv, acc_i + liv * srv - lrv * siv

        acc_r, acc_i = lax.fori_loop(1, nst, step, (acc_r, acc_i))
        da_ref[0:1, :] = jnp.sum(acc_r, 0, keepdims=True)
        da_ref[1:2, :] = jnp.sum(acc_i, 0, keepdims=True)

        brb = br_ref[l].astype(bf16)
        bib = bi_ref[l].astype(bf16)
        dbr_ref[...] = jnp.zeros_like(dbr_ref)
        dbi_ref[...] = jnp.zeros_like(dbi_ref)

        def back(t, c):
            rows = pl.ds(pl.multiple_of(t * _SSM_ROWS, _SSM_ROWS), _SSM_ROWS)
            lrb = lr_ref[rows, :].astype(bf16)
            lib = li_ref[rows, :].astype(bf16)
            du_ref[rows, :] = dus_ref[rows, :] + _dot_nt(lrb, brb) + _dot_nt(lib, bib)
            ub = u_ref[rows, :].astype(bf16)
            dbr_ref[...] += _dot_tn(ub, lrb)
            dbi_ref[...] += _dot_tn(ub, lib)
            return c

        lax.fori_loop(0, nproj, back, 0)

    vm = pl.BlockSpec(memory_space=pltpu.VMEM)
    return pl.pallas_call(
        body, name="ssm_states_bwd", in_specs=[vm] * 10, out_specs=[vm] * 6,
        out_shape=[jax.ShapeDtypeStruct((S, D_SSM), f32), jax.ShapeDtypeStruct((D_SSM, D_STATE), f32),
                   jax.ShapeDtypeStruct((D_SSM, D_STATE), f32), jax.ShapeDtypeStruct((2, D_STATE), f32),
                   jax.ShapeDtypeStruct((D_STATE, D_SSM), f32), jax.ShapeDtypeStruct((D_STATE, D_SSM), f32)],
        scratch_shapes=[pltpu.VMEM((S, D_STATE), f32), pltpu.VMEM((S, D_STATE), f32)],
        compiler_params=_cp(None, 56),
    )(dy, du_skip, u, sr, si, cre, cim, bre, bim, a2)


_POOL_TILE = 256


def _window_sums(xt, back):
    n = xt.shape[0]
    out = []
    ws = xt
    for k in (1, 2, 4, 8):
        ws = ws + pltpu.roll(ws, k if back else n - k, axis=0)
        out.append(ws)
    return out


def _pool_count(r0, w):
    t = r0 + lax.broadcasted_iota(jnp.int32, (_POOL_TILE, POOL_GROUP), 0)
    return jnp.minimum(t + 1, w).astype(f32)


def _pool_fwd(u_pad, pool_w, pool_scale):
    S = u_pad.shape[0] - POOL_HALO
    nt = S // _POOL_TILE

    def body(u_ref, w_ref, sc_ref, y_ref):
        def tile(t, c):
            r0 = pl.multiple_of(t * _POOL_TILE, _POOL_TILE)
            for g, w in enumerate(POOL_WINDOWS):
                cs = pl.ds(POOL_GROUP * g, POOL_GROUP)
                xt = u_ref[pl.ds(r0, _POOL_TILE + POOL_HALO), cs]
                ws = _window_sums(xt, True)[g][POOL_HALO:, :]
                pooled = ws / _pool_count(r0, w) - xt[POOL_HALO:, :]
                y_ref[pl.ds(r0, _POOL_TILE), cs] = _dot(pooled.astype(bf16), w_ref[g].astype(bf16)) * sc_ref[:, cs]
            return c
        lax.fori_loop(0, nt, tile, 0)

    vm = pl.BlockSpec(memory_space=pltpu.VMEM)
    return pl.pallas_call(
        body, name="pool_fwd", in_specs=[vm, vm, vm], out_specs=vm,
        out_shape=jax.ShapeDtypeStruct((S, D_POOL), f32),
    )(u_pad, pool_w, pool_scale)


def _pool_bwd(dy_pad, u_pad, pool_w, pool_scale):
    S = u_pad.shape[0] - POOL_HALO
    nt = S // _POOL_TILE
    n = _POOL_TILE + POOL_HALO

    def body(dy_ref, u_ref, w_ref, sc_ref, du_ref, dw_ref, dsc_ref):
        dw_ref[...] = jnp.zeros_like(dw_ref)
        dsc_ref[...] = jnp.zeros_like(dsc_ref)

        def tile(t, c):
            r0 = pl.multiple_of(t * _POOL_TILE, _POOL_TILE)
            for g, w in enumerate(POOL_WINDOWS):
                cs = pl.ds(POOL_GROUP * g, POOL_GROUP)
                wb = w_ref[g].astype(bf16)
                xt = u_ref[pl.ds(r0, n), cs]
                pooled = (_window_sums(xt, True)[g][POOL_HALO:, :] / _pool_count(r0, w) - xt[POOL_HALO:, :]).astype(bf16)
                dy = dy_ref[pl.ds(r0, _POOL_TILE), cs]
                dsc_ref[:, cs] += jnp.sum(dy * _dot(pooled, wb), 0, keepdims=True)
                dw_ref[g] += _dot_tn(pooled, (dy * sc_ref[:, cs]).astype(bf16))
                dyh = (dy_ref[pl.ds(r0, n), cs] * sc_ref[:, cs]).astype(bf16)
                dpl = _dot_nt(dyh, wb)
                cnt = jnp.minimum(r0 + lax.broadcasted_iota(jnp.int32, (n, POOL_GROUP), 0) + 1, w).astype(f32)
                lead = _window_sums(dpl / cnt, False)[g]
                du_ref[pl.ds(r0, _POOL_TILE), cs] = lead[:_POOL_TILE, :] - dpl[:_POOL_TILE, :]
            return c
        lax.fori_loop(0, nt, tile, 0)

    vm = pl.BlockSpec(memory_space=pltpu.VMEM)
    return pl.pallas_call(
        body, name="pool_bwd", in_specs=[vm, vm, vm, vm], out_specs=[vm, vm, vm],
        out_shape=[jax.ShapeDtypeStruct((S, D_POOL), f32), jax.ShapeDtypeStruct((4, POOL_GROUP, POOL_GROUP), f32),
                   jax.ShapeDtypeStruct((1, D_POOL), f32)],
    )(dy_pad, u_pad, pool_w, pool_scale)


def _loss_head(y, target):
    S, D = y.shape
    ts = TOK_TILE

    def body(y_ref, t_ref, loss_ref, dy_ref):
        @pl.when(pl.program_id(0) == 0)
        def _():
            loss_ref[...] = jnp.zeros_like(loss_ref)

        d = y_ref[...] - t_ref[...]
        dy_ref[...] = d * (1.0 / D)
        loss_ref[...] += 0.5 * jnp.sum(jnp.sum(d * d, -1, keepdims=True) * (1.0 / D), 0, keepdims=True)

    tok = pl.BlockSpec((ts, D), lambda i: (i, 0))
    return pl.pallas_call(
        body, name="loss_head", grid=(S // ts,),
        in_specs=[tok, tok], out_specs=[_full((1, 1)), tok],
        out_shape=[jax.ShapeDtypeStruct((1, 1), f32), jax.ShapeDtypeStruct((S, D), f32)],
        compiler_params=_cp(("arbitrary",)),
    )(*_hbm(y, target))


_ADA_COLS = 768


def _ada_fwd(c_all, ada_w, ada_b_cols):
    L, D, N = ada_w.shape
    B = c_all.shape[0]

    def body(c_ref, w_ref, b_ref, out_ref):
        cv = c_ref[...]
        cond = (cv * jax.nn.sigmoid(cv)).astype(bf16)
        out_ref[0] = _dot(cond, w_ref[0].astype(bf16)) + b_ref[0]

    return pl.pallas_call(
        body, name="ada_fwd", grid=(L, N // _ADA_COLS),
        in_specs=[_full((B, D)), pl.BlockSpec((1, D, _ADA_COLS), lambda l, j: (l, 0, j)),
                  pl.BlockSpec((1, 1, _ADA_COLS), lambda l, j: (l, 0, j))],
        out_specs=pl.BlockSpec((1, B, _ADA_COLS), lambda l, j: (l, 0, j)),
        out_shape=jax.ShapeDtypeStruct((L, B, N), f32),
        compiler_params=_cp(("parallel", "parallel")),
    )(c_all, ada_w, ada_b_cols)


def _ada_wgrad(c_all_t, dmod_cols):
    D, B = c_all_t.shape
    L, _, N = dmod_cols.shape

    def body(ct_ref, dm_ref, out_ref):
        cv = ct_ref[...]
        cond = cv * jax.nn.sigmoid(cv)
        acc = cond[:, 0:1] * dm_ref[0, 0:1, :]
        for b in range(1, B):
            acc = acc + cond[:, b:b + 1] * dm_ref[0, b:b + 1, :]
        out_ref[0] = acc

    return pl.pallas_call(
        body, name="ada_wgrad", grid=(L, N // _ADA_COLS),
        in_specs=[_full((D, B)), pl.BlockSpec((1, B, _ADA_COLS), lambda l, j: (l, 0, j))],
        out_specs=pl.BlockSpec((1, D, _ADA_COLS), lambda l, j: (l, 0, j)),
        out_shape=jax.ShapeDtypeStruct((L, D, N), f32),
        compiler_params=_cp(("parallel", "parallel")),
    )(c_all_t, dmod_cols)


def _adam_math(w, g, m, v):
    m = ADAM_B1 * m + (1.0 - ADAM_B1) * g
    v = ADAM_B2 * v + (1.0 - ADAM_B2) * (g * g)
    m_hat = m / (1.0 - ADAM_B1 ** ADAM_STEP)
    v_hat = v / (1.0 - ADAM_B2 ** ADAM_STEP)
    delta = -ADAM_LR * (m_hat / (jnp.sqrt(v_hat) + ADAM_EPS) + ADAM_WD * w)
    return delta, m, v


def _adamw(w, m, v, g, row_tile, row0=0, outs=None):
    R, C = w.shape
    b0 = row0 // row_tile

    def body(w_ref, m_ref, v_ref, g_ref, _0, _1, _2, _3, g_out, d_out, m_out, v_out):
        gv = g_ref[...]
        delta, mn, vn = _adam_math(w_ref[...], gv, m_ref[...], v_ref[...])
        g_out[...] = gv
        d_out[...] = delta
        m_out[...] = mn
        v_out[...] = vn

    pspec = pl.BlockSpec((row_tile, C), lambda i: (b0 + i, 0))
    gspec = pl.BlockSpec((row_tile, C), lambda i: (i, 0))
    anyspec = pl.BlockSpec(memory_space=pl.ANY)
    shp = jax.ShapeDtypeStruct((R, C), f32)
    if outs is None:
        outs = [lax.empty((R, C), f32) for _ in range(4)]
    return pl.pallas_call(
        body, name="adamw", grid=(g.shape[0] // row_tile,),
        in_specs=[pspec] * 3 + [gspec] + [anyspec] * 4, out_specs=[pspec] * 4, out_shape=[shp] * 4,
        input_output_aliases={4: 0, 5: 1, 6: 2, 7: 3},
        compiler_params=_cp(("parallel",), 40),
    )(*_hbm(w, m, v, g, *outs))


def _pair_sum(g5s, gots, pc):
    n = len(g5s)

    def body(pc_ref, *refs):
        for own, got, out in zip(refs[:n], refs[n:2 * n], refs[2 * n:]):
            out[0, 0] = (own[0, 0, 0].astype(f32) + got[0, 0].astype(f32)).astype(bf16)

    def half(g):
        return pl.BlockSpec((1, 1) + g.shape[-2:], lambda p, pc: (p, 0, 0, 0))

    gs = pltpu.PrefetchScalarGridSpec(
        num_scalar_prefetch=1, grid=(N_CHIPS,),
        in_specs=[pl.BlockSpec((1, 1, 1) + g.shape[-2:], lambda p, pc: (p, 0, pc[1], 0, 0)) for g in g5s]
        + [half(g) for g in gots],
        out_specs=[half(g) for g in gots],
    )
    return pl.pallas_call(
        body, name="pair_sum", grid_spec=gs, out_shape=[jax.ShapeDtypeStruct(g.shape, bf16) for g in gots],
        compiler_params=_cp(("parallel",), 48),
    )(pc, *_hbm(*g5s, *gots))


_SUM_STEPS = 2


def _sum_shards(hsums, recvs, pc):
    n = len(hsums)

    def body(pc_ref, *refs):
        for own, got, out in zip(refs[:n], refs[n:2 * n], refs[2 * n:]):
            acc = own[0, 0].astype(f32)
            for j in range(3):
                acc = acc + got[j, 0].astype(f32)
            out[0, 0] = acc

    def rows(h):
        return (h.shape[2] // _SUM_STEPS, h.shape[3])

    gs = pltpu.PrefetchScalarGridSpec(
        num_scalar_prefetch=1, grid=(_SUM_STEPS,),
        in_specs=[pl.BlockSpec((1, 1) + rows(h), lambda i, pc: (pc[0], 0, i, 0)) for h in hsums]
        + [pl.BlockSpec((3, 1) + rows(h), lambda i, pc: (0, 0, i, 0)) for h in hsums],
        out_specs=[pl.BlockSpec((1, 1) + rows(h), lambda i, pc: (0, pc[1], i, 0)) for h in hsums],
    )
    return pl.pallas_call(
        body, name="sum_shards", grid_spec=gs,
        out_shape=[jax.ShapeDtypeStruct((1, 2) + h.shape[2:], f32) for h in hsums],
        compiler_params=_cp(("parallel",), 48),
    )(pc, *_hbm(*hsums, *recvs))


def _sum8(packs):
    _, R, C = packs.shape
    tr = R // 8 if R % 64 == 0 else R

    def body(p_ref, out_ref):
        acc = p_ref[0]
        for d in range(1, 8):
            acc = acc + p_ref[d]
        out_ref[...] = acc

    return pl.pallas_call(
        body, name="sum8", grid=(R // tr,),
        in_specs=[pl.BlockSpec((8, tr, C), lambda i: (0, i, 0))],
        out_specs=pl.BlockSpec((tr, C), lambda i: (i, 0)),
        out_shape=jax.ShapeDtypeStruct((R, C), f32),
        compiler_params=_cp(("parallel",)),
    )(packs)


def _allgather8(x_shard):
    m_per, n = x_shard.shape

    def body(x_ref, out_ref, send_sems, recv_sems, local_sem):
        x, y, c = lax.axis_index("x"), lax.axis_index("y"), lax.axis_index("c")
        me, sibling = (x, y, c), (x, y, 1 - c)
        chips = [(1 - x, y), (x, 1 - y), (1 - x, 1 - y)]

        def rows(px, py, pc):
            return out_ref.at[pl.ds((4 * px + 2 * py + pc) * m_per, m_per), :]

        def copy(k, block, to, src=None):
            return pltpu.make_async_remote_copy(
                src_ref=rows(*block) if src is None else src, dst_ref=rows(*block),
                send_sem=send_sems.at[k], recv_sem=recv_sems.at[k], device_id=to, device_id_type=MESH)

        mine = pltpu.make_async_copy(x_ref, rows(*me), local_sem)
        mine.start()
        first = [copy(0, me, sibling, src=x_ref)]
        first += [copy(1 + j, me, (*chip, c), src=x_ref) for j, chip in enumerate(chips)]
        for cp in first:
            cp.start()
        passed = [copy(4 + j, (*chip, c), sibling) for j, chip in enumerate(chips)]
        for j, chip in enumerate(chips):
            copy(1 + j, (*chip, c), me).wait_recv()
            passed[j].start()
        copy(0, sibling, me).wait_recv()
        for j, chip in enumerate(chips):
            copy(4 + j, (*chip, 1 - c), me).wait_recv()
        for cp in first + passed:
            cp.wait_send()
        mine.wait()

    return pl.pallas_call(
        body, name="allgather8",
        out_shape=jax.ShapeDtypeStruct((8 * m_per, n), x_shard.dtype),
        in_specs=[pl.BlockSpec(memory_space=pltpu.VMEM)],
        out_specs=pl.BlockSpec(memory_space=pltpu.VMEM),
        scratch_shapes=[pltpu.SemaphoreType.DMA((7,)), pltpu.SemaphoreType.DMA((7,)), pltpu.SemaphoreType.DMA],
        compiler_params=_cp(None, 48),
    )(x_shard)


def _other_chips():
    x, y = lax.axis_index("x"), lax.axis_index("y")
    return [(1 - x, y), (x, 1 - y), (1 - x, 1 - y)]


_HBM = pl.BlockSpec(memory_space=pltpu.HBM)
_SEM = pl.BlockSpec(memory_space=pltpu.SEMAPHORE)
_EFFECT = pltpu.SideEffectType.DATAFLOW_SIDE_EFFECTING


def _gather_copies(srcs, lands, send_sems, recv_sems):
    x, y, c = lax.axis_index("x"), lax.axis_index("y"), lax.axis_index("c")
    return [pltpu.make_async_remote_copy(
        src_ref=srcs[a].at[:, c], dst_ref=lands[a].at[2 * x + y, :, c], send_sem=send_sems.at[3 * a + j],
        recv_sem=recv_sems.at[3 * a + j], device_id=(cx, cy, c), device_id_type=MESH)
        for a in range(len(srcs)) for j, (cx, cy) in enumerate(_other_chips())]


def _gather_start(chunks, after, name):
    sizes = [len(srcs) for srcs, _ in chunks]
    flat = [t for srcs, lands in chunks for t in list(srcs) + list(lands)]
    nflat = len(flat)
    nsem = 2 * len(chunks)

    def body(*refs):
        ins, sems, token = refs[:nflat], refs[nflat + 1:nflat + 1 + nsem], refs[-1]
        off = 0
        for k, n in enumerate(sizes):
            for cp in _gather_copies(ins[off:off + n], ins[off + n:off + 2 * n], sems[2 * k], sems[2 * k + 1]):
                cp.start()
            off += 2 * n
        token[...] = jnp.zeros_like(token)

    res = pl.pallas_call(
        body, name=name,
        out_shape=[pltpu.SemaphoreType.DMA((3 * n,)) for n in sizes for _ in range(2)]
        + [pltpu.HBM(t.shape, t.dtype) for t in flat] + [jax.ShapeDtypeStruct((8, 128), f32)],
        in_specs=[_HBM] * nflat + [pl.BlockSpec(memory_space=pl.ANY)],
        out_specs=[_SEM] * nsem + [_HBM] * nflat + [pl.BlockSpec(memory_space=pltpu.VMEM)],
        input_output_aliases={i: nsem + i for i in range(nflat)},
        compiler_params=pltpu.CompilerParams(has_side_effects=_EFFECT),
    )(*[pltpu.with_memory_space_constraint(t, pltpu.HBM) for t in flat], after)
    out, off = [], nsem
    for k, n in enumerate(sizes):
        out.append((res[2 * k], res[2 * k + 1], res[off:off + n], res[off + n:off + 2 * n]))
        off += 2 * n
    return out, res[-1]


def _gather_wait(send_sems, recv_sems, srcs, lands, after, name):
    n = len(srcs)

    def body(*refs):
        for cp in _gather_copies(refs[:n], refs[n:2 * n], refs[2 * n], refs[2 * n + 1]):
            cp.wait_send()
            cp.wait_recv()

    res = pl.pallas_call(
        body, name=name,
        out_shape=[pltpu.HBM(t.shape, t.dtype) for t in list(srcs) + list(lands)],
        in_specs=[_HBM] * (2 * n) + [_SEM, _SEM] + [pl.BlockSpec(memory_space=pl.ANY)] * len(after),
        out_specs=[_HBM] * (2 * n),
        input_output_aliases={i: i for i in range(2 * n)},
        compiler_params=pltpu.CompilerParams(has_side_effects=_EFFECT),
    )(*srcs, *lands, send_sems, recv_sems, *after)
    return res[n:]


def _split_start(make_copies, arrays, nsem, name, after=()):
    n, na = len(arrays), len(after)

    def body(*refs):
        for cp in make_copies(refs[:n], refs[n + na], refs[n + na + 1]):
            cp.start()
        refs[-1][...] = jnp.zeros_like(refs[-1])

    res = pl.pallas_call(
        body, name=name,
        out_shape=[pltpu.SemaphoreType.DMA((nsem,)), pltpu.SemaphoreType.DMA((nsem,))]
        + [pltpu.HBM(t.shape, t.dtype) for t in arrays] + [jax.ShapeDtypeStruct((8, 128), f32)],
        in_specs=[_HBM] * n + [pl.BlockSpec(memory_space=pl.ANY)] * na,
        out_specs=[_SEM, _SEM] + [_HBM] * n + [pl.BlockSpec(memory_space=pltpu.VMEM)],
        input_output_aliases={i: i + 2 for i in range(n)},
        compiler_params=pltpu.CompilerParams(has_side_effects=_EFFECT),
    )(*[pltpu.with_memory_space_constraint(t, pltpu.HBM) for t in arrays], *after)
    return (res[0], res[1], res[2:2 + n]), res[-1]


def _split_wait(make_copies, send_sems, recv_sems, arrays, after, name):
    n = len(arrays)

    def body(*refs):
        for cp in make_copies(refs[:n], refs[n], refs[n + 1]):
            cp.wait_send()
            cp.wait_recv()

    return pl.pallas_call(
        body, name=name,
        out_shape=[pltpu.HBM(t.shape, t.dtype) for t in arrays],
        in_specs=[_HBM] * n + [_SEM, _SEM] + [pl.BlockSpec(memory_space=pl.ANY)] * len(after),
        out_specs=[_HBM] * n, input_output_aliases={i: i for i in range(n)},
        compiler_params=pltpu.CompilerParams(has_side_effects=_EFFECT),
    )(*arrays, send_sems, recv_sems, *after)


def _sibling():
    return lax.axis_index("x"), lax.axis_index("y"), 1 - lax.axis_index("c")


def _forward_copies(lands, send_sems, recv_sems):
    c = lax.axis_index("c")
    return [pltpu.make_async_remote_copy(
        src_ref=lands[a].at[2 * cx + cy, :, c], dst_ref=lands[a].at[2 * cx + cy, :, c], send_sem=send_sems.at[3 * a + j],
        recv_sem=recv_sems.at[3 * a + j], device_id=_sibling(), device_id_type=MESH)
        for a in range(len(lands)) for j, (cx, cy) in enumerate(_other_chips())]


def _swap_copies(fulls, send_sems, recv_sems):
    c = lax.axis_index("c")
    return [pltpu.make_async_remote_copy(src_ref=t.at[:, c], dst_ref=t.at[:, c], send_sem=send_sems.at[a],
                                         recv_sem=recv_sems.at[a], device_id=_sibling(), device_id_type=MESH)
            for a, t in enumerate(fulls)]


def _pair_copies(refs, send_sems, recv_sems):
    n = len(refs) // 2
    c = lax.axis_index("c")
    return [pltpu.make_async_remote_copy(src_ref=refs[a].at[:, :, 1 - c], dst_ref=refs[n + a], send_sem=send_sems.at[a],
                                         recv_sem=recv_sems.at[a], device_id=_sibling(), device_id_type=MESH)
            for a in range(n)]


def _scatter_copies(srcs, lands, send_sems, recv_sems):
    c = lax.axis_index("c")
    return [pltpu.make_async_remote_copy(
        src_ref=srcs[a].at[2 * cx + cy], dst_ref=lands[a].at[j], send_sem=send_sems.at[3 * a + j],
        recv_sem=recv_sems.at[3 * a + j], device_id=(cx, cy, c), device_id_type=MESH)
        for a in range(len(srcs)) for j, (cx, cy) in enumerate(_other_chips())]


def _scatter_start(hsums, name, after=()):
    n = len(hsums)
    na = len(after)

    def body(*refs):
        srcs, lands = refs[:n], refs[n:2 * n]
        send_sems, recv_sems = refs[2 * n + na], refs[2 * n + na + 1]
        for cp in _scatter_copies(srcs, lands, send_sems, recv_sems):
            cp.start()
        refs[-1][...] = jnp.zeros_like(refs[-1])

    lands = [lax.empty((3,) + g.shape[1:], g.dtype) for g in hsums]
    res = pl.pallas_call(
        body, name=name,
        out_shape=[pltpu.SemaphoreType.DMA((3 * n,)), pltpu.SemaphoreType.DMA((3 * n,))]
        + [pltpu.HBM(g.shape, g.dtype) for g in hsums] + [pltpu.HBM(g.shape, g.dtype) for g in lands]
        + [jax.ShapeDtypeStruct((8, 128), f32)],
        in_specs=[_HBM] * (2 * n) + [pl.BlockSpec(memory_space=pl.ANY)] * na,
        out_specs=[_SEM, _SEM] + [_HBM] * (2 * n) + [pl.BlockSpec(memory_space=pltpu.VMEM)],
        input_output_aliases={i: i + 2 for i in range(2 * n)},
        compiler_params=pltpu.CompilerParams(has_side_effects=_EFFECT),
    )(*[pltpu.with_memory_space_constraint(t, pltpu.HBM) for t in list(hsums) + lands], *after)
    return (res[0], res[1], res[2:2 + n], res[2 + n:2 + 2 * n]), res[-1]


def _scatter_wait(send_sems, recv_sems, srcs, lands, after, name):
    n = len(srcs)
    extra = list(after)

    def body(*refs):
        s_refs, l_refs = refs[:n], refs[n:2 * n]
        ss, rs = refs[2 * n], refs[2 * n + 1]
        for cp in _scatter_copies(s_refs, l_refs, ss, rs):
            cp.wait_send()
            cp.wait_recv()

    res = pl.pallas_call(
        body, name=name,
        out_shape=[pltpu.HBM(g.shape, g.dtype) for g in srcs] + [pltpu.HBM(g.shape, g.dtype) for g in lands],
        in_specs=[_HBM] * (2 * n) + [_SEM, _SEM] + [pl.BlockSpec(memory_space=pl.ANY)] * len(extra),
        out_specs=[_HBM] * (2 * n),
        input_output_aliases={i: i for i in range(2 * n)},
        compiler_params=pltpu.CompilerParams(has_side_effects=_EFFECT),
    )(*srcs, *lands, send_sems, recv_sems, *extra)
    return res[:n], res[n:]


def _plane_copies(src, land, send_sems, recv_sems):
    x, y, c = lax.axis_index("x"), lax.axis_index("y"), lax.axis_index("c")
    return [pltpu.make_async_remote_copy(src_ref=src, dst_ref=land.at[2 * x + y, c], send_sem=send_sems.at[j],
                                         recv_sem=recv_sems.at[j], device_id=(cx, cy, c), device_id_type=MESH)
            for j, (cx, cy) in enumerate(_other_chips())]


def _plane_start(pack, land, name):
    def body(src, lnd, send_sems, recv_sems, _s, _l, token):
        for cp in _plane_copies(src, lnd, send_sems, recv_sems):
            cp.start()
        token[...] = jnp.zeros_like(token)

    res = pl.pallas_call(
        body, name=name,
        out_shape=[pltpu.SemaphoreType.DMA((3,)), pltpu.SemaphoreType.DMA((3,)), pltpu.HBM(pack.shape, pack.dtype),
                   pltpu.HBM(land.shape, land.dtype), jax.ShapeDtypeStruct((8, 128), f32)],
        in_specs=[_HBM, _HBM], out_specs=[_SEM, _SEM, _HBM, _HBM, pl.BlockSpec(memory_space=pltpu.VMEM)],
        input_output_aliases={0: 2, 1: 3},
        compiler_params=pltpu.CompilerParams(has_side_effects=_EFFECT),
    )(pltpu.with_memory_space_constraint(pack, pltpu.HBM), pltpu.with_memory_space_constraint(land, pltpu.HBM))
    return res[:4], res[4]


def _plane_wait(send_sems, recv_sems, pack, land, after, name):
    def body(src, lnd, ss, rs, *_):
        for cp in _plane_copies(src, lnd, ss, rs):
            cp.wait_send()
            cp.wait_recv()

    return pl.pallas_call(
        body, name=name,
        out_shape=[pltpu.HBM(pack.shape, pack.dtype), pltpu.HBM(land.shape, land.dtype)],
        in_specs=[_HBM, _HBM, _SEM, _SEM] + [pl.BlockSpec(memory_space=pl.ANY)] * len(after),
        out_specs=[_HBM, _HBM], input_output_aliases={0: 0, 1: 1},
        compiler_params=pltpu.CompilerParams(has_side_effects=_EFFECT),
    )(pack, land, send_sems, recv_sems, *after)[1]


def _swap_halves(fulls):
    n = len(fulls)

    def body(*refs):
        ins, outs = refs[:n], refs[n:2 * n]
        send_sems, recv_sems = refs[2 * n:]
        c = lax.axis_index("c")
        sibling = (lax.axis_index("x"), lax.axis_index("y"), 1 - c)
        copies = []
        for a in range(n):
            cp = pltpu.make_async_remote_copy(src_ref=outs[a].at[:, c], dst_ref=outs[a].at[:, c], send_sem=send_sems.at[a],
                                              recv_sem=recv_sems.at[a], device_id=sibling, device_id_type=MESH)
            cp.start()
            copies.append(cp)
        for a, cp in enumerate(copies):
            cp.wait_send()
            theirs = outs[a].at[:, 1 - c]
            pltpu.make_async_remote_copy(src_ref=theirs, dst_ref=theirs, send_sem=send_sems.at[a], recv_sem=recv_sems.at[a],
                                         device_id=sibling, device_id_type=MESH).wait_recv()

    hbm = pl.BlockSpec(memory_space=pl.ANY)
    return pl.pallas_call(
        body, name="swap_halves",
        out_shape=[jax.ShapeDtypeStruct(p.shape, p.dtype) for p in fulls],
        in_specs=[hbm] * n, out_specs=[hbm] * n,
        input_output_aliases={a: a for a in range(n)},
        scratch_shapes=[pltpu.SemaphoreType.DMA((n,)), pltpu.SemaphoreType.DMA((n,))],
    )(*fulls)


def _to_segments(t):
    s, c = t.shape
    return t.reshape(SCAN_SEG, s // SCAN_SEG, c).transpose(1, 0, 2).reshape(s, c)


def _from_segments(t):
    s, c = t.shape
    return t.reshape(s // SCAN_SEG, SCAN_SEG, c).transpose(1, 0, 2).reshape(s, c)


def _ssm_operators(a_re, a_im, log_dt, b_re, b_im, c_re, c_im):
    lam = lax.complex(a_re, a_im)
    dt = jnp.exp(log_dt)[:, None]
    a_bar = jnp.exp(lam * dt)
    b_bar = ((a_bar - 1.0) / lam)[:, :, None] * lax.complex(b_re, b_im)
    eye = jnp.eye(N_GROUPS, dtype=f32)

    def embed_b(t):
        return (jnp.transpose(t, (0, 2, 1))[:, :, None, :] * eye[:, None, :, None]).reshape(D_SSM, D_STATE)

    def embed_c(t):
        return (jnp.transpose(t, (0, 2, 1))[:, :, None, :] * eye[:, None, :, None]).reshape(D_STATE, D_SSM)

    a2 = jnp.stack([a_bar.real.reshape(D_STATE), a_bar.imag.reshape(D_STATE)])
    return a2, embed_b(b_bar.real), embed_b(b_bar.imag), embed_c(c_re), embed_c(c_im)


def _local_step(x, target, mod, small, ffn_weights, mix_weights, grads_done, ffn_bwd_issued):
    table = jnp.asarray(_bucket_table())
    bias = small["att_bias"]
    L = DEPTH
    saved = []
    ssm_names = ("ssm_a_re", "ssm_a_im", "ssm_log_dt", "ssm_b_re", "ssm_b_im", "ssm_c_re", "ssm_c_im")
    ssm_ops_vjp = []
    for l in range(L):
        sv = {}
        m9 = mod[l]
        sv["x0"] = x
        sv["w0"] = ffn_weights(l, 0, x)
        x, sv["f0"], sv["g0"], sv["u0"], sv["h0"] = _ffn_fwd(x, m9[0:3], *sv["w0"], 0, small["ln_g"][l, 0:1], small["ln_b"][l, 0:1])
        sv["x1"] = x
        sv["w1"] = mix_weights(l, x)
        *qkv, z_rest, sv["h1"] = _mix_in_fwd(x, m9[3:6], sv["w1"][0], 0)
        S = x.shape[0]
        qkv = [t.reshape(3, S, D_ATT) for t in qkv]
        att = [_att_fwd(qkv[b], bias, b) for b in range(3)]
        y_att, lse3 = _att_merge([att[b][0].reshape(d, S // d, D_ATT) for b, d in enumerate(DILATIONS)],
                                 [att[b][1].reshape(d, S // d, _LANES) for b, d in enumerate(DILATIONS)])
        sv.update(qkv=qkv, lse=[a[1] for a in att], lse3=lse3, y_att=y_att)

        ops, ops_vjp = jax.vjp(_ssm_operators, *[small[k][l] for k in ssm_names])
        ssm_ops_vjp.append(ops_vjp)
        a2, bre, bim, cre, cim = [t[None] for t in ops]
        u_ssm = _to_segments(z_rest[:, :D_SSM])
        dskip = small["ssm_d"][l][None, :]
        glu_b = small["glu_b"][l][None, :]
        sr, si, out_seg, y_seg = _ssm_fwd(u_ssm, bre, bim, a2, cre, cim, 0, dskip, small["glu_w"][l], glu_b)
        y_ssm = _from_segments(out_seg)
        sv.update(ssm_ops=(a2, bre, bim, cre, cim), u_ssm=u_ssm, sr=sr, si=si, y_seg=y_seg, y_ssm=y_ssm)

        u_pool = jnp.concatenate([jnp.zeros((POOL_HALO, D_POOL), f32), z_rest[:, D_SSM:]])
        y_pool = _pool_fwd(u_pool, small["pool_w"][l], small["pool_scale"][l][None, :])
        sv.update(u_pool=u_pool, y_pool=y_pool)

        x, sv["ymix"] = _mix_out_fwd(x, y_att, y_ssm, y_pool, m9[3:6], sv["w1"][1], 0, small["ln_g"][l, 1:2], small["ln_b"][l, 1:2])
        sv["x2"] = x
        sv["w2"] = ffn_weights(l, 1, x)
        x, sv["f2"], sv["g2"], sv["u2"], sv["h2"] = _ffn_fwd(x, m9[6:9], *sv["w2"], 0, small["ln_g"][l, 2:3], small["ln_b"][l, 2:3])
        saved.append(sv)

    loss, dx = _loss_head(x, target)

    dmod = [None] * L
    dln_g = [None] * L
    dln_b = [None] * L
    sg = {k: [None] * L for k in ssm_names + ("ssm_d", "glu_w", "glu_b", "pool_w", "pool_scale")}
    dbias_tot = None
    order_after = jnp.zeros((), f32)
    for l in reversed(range(L)):
        sv = saved[l]
        m9 = mod[l] + order_after

        def fresh(like):
            return [lax.empty(t.shape, bf16) for t in like]

        dx, dg, du, a, df, dm2, dlg2, dlb2 = _ffn_bwd(dx, sv["x2"], sv["f2"], sv["g2"], sv["u2"], m9[6:9], *sv["w2"], 0,
                                                     small["ln_g"][l, 2:3])
        m9 = m9 + ffn_bwd_issued(l, 1, dx)
        g_ffn1 = _ffn_wgrad(sv["h2"], dg, du, a, df, *fresh(sv["w2"]), 0)
        dxr, d_att, d_ssm, d_pool, dgate1, dlg1, dlb1, g_w_out = _mix_out_bwd(
            dx, sv["x1"], sv["ymix"], sv["y_att"], sv["y_ssm"], sv["y_pool"], m9[3:6], sv["w1"][1], 0, small["ln_g"][l, 1:2],
            fresh(sv["w1"])[1])
        S = d_att.shape[0]
        merged = _att_merge_bwd(d_att, sv["y_att"], sv["lse3"])
        dqkv, dbias = [], []
        for b, d in enumerate(DILATIONS):
            dq_b, db_b = _att_bwd(sv["qkv"][b], merged[b].reshape(S, D_ATT), sv["lse"][b], merged[3 + b].reshape(S, _LANES), bias, b)
            dqkv.append(dq_b.reshape(3, d, S // d, D_ATT))
            dbias.append(db_b)
        dbias = jnp.stack(dbias)
        dbias_tot = dbias if dbias_tot is None else dbias_tot + dbias
        d_seg = _to_segments(d_ssm)
        dskip = small["ssm_d"][l][None, :]
        glu_b = small["glu_b"][l][None, :]
        dy_seg, du_skip, dd, dglu_b, dglu_w = _ssm_out_bwd(d_seg, sv["y_seg"], sv["u_ssm"], dskip, small["glu_w"][l], glu_b)
        a2, bre, bim, cre, cim = sv["ssm_ops"]
        du_seg, dbre, dbim, da2, dcre, dcim = _ssm_states_bwd(dy_seg, du_skip, sv["u_ssm"], sv["sr"], sv["si"], cre, cim,
                                                              bre, bim, a2, 0)
        for k, t in zip(ssm_names, ssm_ops_vjp[l]((da2, dbre, dbim, dcre, dcim))):
            sg[k][l] = t
        sg["ssm_d"][l] = dd[0]
        sg["glu_b"][l] = dglu_b[0]
        sg["glu_w"][l] = dglu_w
        du_ssm = _from_segments(du_seg)
        dyp = jnp.concatenate([d_pool, jnp.zeros((POOL_HALO, D_POOL), f32)])
        du_pool, dpw, dps = _pool_bwd(dyp, sv["u_pool"], small["pool_w"][l], small["pool_scale"][l][None, :])
        sg["pool_w"][l] = dpw
        sg["pool_scale"][l] = dps[0]
        d_rest = jnp.concatenate([du_ssm, du_pool], axis=1).astype(bf16)
        dx, dm1, dz = _mix_in_bwd(dqkv, d_rest, dxr, sv["x1"], m9[3:6], sv["w1"][0], 0)
        g_w_in = _mix_in_wgrad(sv["h1"], dz, fresh(sv["w1"])[0], 0)
        ffn_names = ("ffn_w_gate", "ffn_w_up", "ffn_w_down")
        m9 = m9 + grads_done(l, 1, list(zip(ffn_names, [(2 * l + 1) * FF_SHARD] * 3, g_ffn1))
                             + [("w_in", l * D_MODEL, g_w_in), ("w_out", l * 256, g_w_out)])
        dm1 = jnp.concatenate([dm1[0:2], dgate1])
        dx, dg, du, a, df, dm0, dlg0, dlb0 = _ffn_bwd(dx, sv["x0"], sv["f0"], sv["g0"], sv["u0"], m9[0:3], *sv["w0"], 0,
                                                     small["ln_g"][l, 0:1])
        issued = ffn_bwd_issued(l, 0, dx)
        g_ffn0 = _ffn_wgrad(sv["h0"], dg, du, a, df, *fresh(sv["w0"]), 0)
        order_after = grads_done(l, 0, list(zip(ffn_names, [2 * l * FF_SHARD] * 3, g_ffn0))) + issued
        dmod[l] = jnp.concatenate([dm0 + issued, dm1, dm2])
        dln_g[l] = jnp.concatenate([dlg0, dlg1, dlg2])
        dln_b[l] = jnp.concatenate([dlb0, dlb1, dlb2])

    small_grads = {k: jnp.stack(v) for k, v in sg.items()}
    small_grads["rel_bias"] = _bias_bwd(dbias_tot, table)
    small_grads["ln_g"] = jnp.stack(dln_g)
    small_grads["ln_b"] = jnp.stack(dln_b)
    return loss, dx, jnp.stack(dmod), small_grads


_TILE_ELEMS = 8 * 128


def _pack_rows(shapes):
    out, row = [], 0
    for s in shapes:
        nr = -(-int(np.prod(s)) // _TILE_ELEMS) * 8
        out.append((row, nr))
        row += nr
    return out


def _pack(arrs):
    parts = []
    for a in arrs:
        flat = a.reshape(-1).astype(f32)
        npad = -(-flat.shape[0] // _TILE_ELEMS) * _TILE_ELEMS
        parts.append(jnp.pad(flat, (0, npad - flat.shape[0])).reshape(npad // 128, 128))
    return jnp.concatenate(parts, axis=0)


def _unpack(buf, shapes):
    return [buf[row:row + nr].reshape(-1)[:int(np.prod(s))].reshape(s) for s, (row, nr) in zip(shapes, _pack_rows(shapes))]


_REPL = ("rel_bias", "ada_b", "ssm_a_re", "ssm_a_im", "ssm_log_dt", "ssm_b_re", "ssm_b_im", "ssm_c_re", "ssm_c_im",
         "ssm_d", "glu_b", "pool_w", "pool_scale")
_SMALL_SHARDED = ("ln_g", "ln_b", "glu_w")
_BIG = ("ffn_w_gate", "ffn_w_up", "ffn_w_down", "w_in", "w_out")
_ORDER = ("rel_bias", "ada_w", "ada_b", "ln_g", "ln_b", "ffn_w_gate", "ffn_w_up", "ffn_w_down", "w_in", "w_out",
          "ssm_a_re", "ssm_a_im", "ssm_log_dt", "ssm_b_re", "ssm_b_im", "ssm_c_re", "ssm_c_im", "ssm_d", "glu_w",
          "glu_b", "pool_w", "pool_scale")


def kernel(x, c, rel_bias, ada_w, ada_b, ln_g, ln_b, ffn_w_gate, ffn_w_up, ffn_w_down, w_in, w_out, ssm_a_re, ssm_a_im, ssm_log_dt, ssm_b_re, ssm_b_im, ssm_c_re, ssm_c_im, ssm_d, glu_w, glu_b, pool_w, pool_scale, loss_target, m_rel_bias, m_ada_w, m_ada_b, m_ln_g, m_ln_b, m_ffn_w_gate, m_ffn_w_up, m_ffn_w_down, m_w_in, m_w_out, m_ssm_a_re, m_ssm_a_im, m_ssm_log_dt, m_ssm_b_re, m_ssm_b_im, m_ssm_c_re, m_ssm_c_im, m_ssm_d, m_glu_w, m_glu_b, m_pool_w, m_pool_scale, v_rel_bias, v_ada_w, v_ada_b, v_ln_g, v_ln_b, v_ffn_w_gate, v_ffn_w_up, v_ffn_w_down, v_w_in, v_w_out, v_ssm_a_re, v_ssm_a_im, v_ssm_log_dt, v_ssm_b_re, v_ssm_b_im, v_ssm_c_re, v_ssm_c_im, v_ssm_d, v_glu_w, v_glu_b, v_pool_w, v_pool_scale):
    args = dict(locals())
    w = {k: args[k] for k in _ORDER}
    m = {k: args["m_" + k] for k in _ORDER}
    v = {k: args["v_" + k] for k in _ORDER}
    L, D = DEPTH, D_MODEL
    ax, ay, ac = lax.axis_index("x"), lax.axis_index("y"), lax.axis_index("c")
    p_me = 2 * ax + ay
    dev = 4 * ax + 2 * ay + ac

    transposed = ("ffn_w_gate", "ffn_w_up")
    for d in (w, m, v):
        for name in transposed:
            d[name] = jnp.swapaxes(d[name], 2, 3)

    def halves(t):
        return t.astype(bf16).reshape(1, 2, t.shape[0] // 2, t.shape[1])

    def landing(src):
        return lax.dynamic_update_slice(lax.empty((N_CHIPS,) + src.shape, bf16), src[None], (p_me, 0, 0, 0, 0))

    chunk_keys = [("ffn", 0, 0), ("mix", 0), ("ffn", 0, 1), ("ffn", 1, 0), ("mix", 1), ("ffn", 1, 1)]
    chunk_srcs = []
    for key in chunk_keys:
        if key[0] == "ffn":
            chunk_srcs.append([halves(w[name][key[1], key[2]]) for name in ("ffn_w_gate", "ffn_w_up", "ffn_w_down")])
        else:
            chunk_srcs.append([halves(w_in[key[1]]), halves(w_out[key[1]])])

    pack = _pack([c, ln_g, ln_b, glu_w])
    rows = pack.shape[0]
    allp = _allgather8(pack).reshape(8, rows, 128)
    chunks = [(srcs, [landing(t) for t in srcs]) for srcs in chunk_srcs]
    first_in_flight, first_begun = _gather_start(chunks[:1], allp, "gather_start_first")
    c_all = allp[:, :8].reshape(8, D) + first_begun[0, 0]
    by_chip = allp[0::2]

    fwd_rows = _pack_rows([c.shape, ln_g.shape, ln_b.shape, glu_w.shape])

    def sharded(part, shape, axis):
        row0, nrows = fwd_rows[part]
        t = by_chip[:, row0:row0 + nrows].reshape(N_CHIPS, -1)[:, :int(np.prod(shape))].reshape((N_CHIPS,) + shape)
        return jnp.concatenate([t[p] for p in range(N_CHIPS)], axis=axis)

    ln_g_full = sharded(1, ln_g.shape, 2)
    ln_b_full = sharded(2, ln_b.shape, 2)
    glu_w_full = sharded(3, glu_w.shape, 1)

    ncol = ada_w.shape[-1]
    ada_b_cols = lax.dynamic_slice_in_dim(ada_b, p_me * ncol, ncol, axis=1)[:, None, :]
    mod_part = _ada_fwd(c_all, ada_w, ada_b_cols)
    mrows = L * 8 * ncol // 128
    mod_pack = mod_part.reshape(mrows, 128)
    mod_land = lax.dynamic_update_slice(lax.empty((N_CHIPS, 2, mrows, 128), f32), mod_pack[None, None], (p_me, ac, 0, 0))
    mod_in_flight, _ = _plane_start(mod_pack, mod_land, "mod_start")
    att_bias = _bias_fwd(rel_bias, jnp.asarray(_bucket_table()))
    small_names = _REPL + _SMALL_SHARDED
    small_packs = [_pack([d[k] for k in small_names]) for d in (w, m, v)]
    mod_land = _plane_wait(*mod_in_flight, [att_bias] + small_packs + [t for _, lands in chunks[1:] for t in lands], "mod_wait")
    mod_all = _swap_halves([mod_land])[0].reshape(8, L, 8, ncol)
    mod_mine = lax.dynamic_index_in_dim(mod_all, dev, axis=2, keepdims=False)
    mod = jnp.concatenate([mod_mine[2 * p] for p in range(N_CHIPS)], axis=-1).reshape(L, 9, D)

    rest_in_flight, rest_begun = _gather_start(chunks[1:], mod, "gather_start_rest")
    in_flight = first_in_flight + rest_in_flight

    forwarding = {}

    def forward(k, after):
        lands = _gather_wait(*in_flight[k], [after, rest_begun], "gather_wait_%d" % k)
        forwarding[k], begun = _split_start(_forward_copies, lands, 3 * len(lands), "gather_forward_start_%d" % k)
        return begun

    def gathered(key, after):
        k = chunk_keys.index(key)
        order = [after]
        if k not in forwarding:
            order.append(forward(k, after))
        if 3 <= k + 1 < len(chunk_keys):
            order.append(forward(k + 1, after))
        lands = _split_wait(_forward_copies, *forwarding[k], order, "gather_forward_wait_%d" % k)
        return [t.reshape(N_CHIPS, 1, 2 * t.shape[3], t.shape[4]) for t in lands]

    pc = jnp.stack([p_me, ac]).astype(jnp.int32)
    groups = {}
    scattering = {}

    pairing = {}

    def start_pairs(tag, after=()):
        g5 = [g.reshape(g.shape[:2] + (2, g.shape[2] // 2, g.shape[3])) for _, _, g in groups[tag]]
        gots = [lax.empty(g.shape[:2] + g.shape[3:], bf16) for g in g5]
        pairing[tag], begun = _split_start(_pair_copies, g5 + gots, len(g5), "pair_exchange_start_%s" % tag, after)
        return begun

    def start_group(tag, after):
        arrays = _split_wait(_pair_copies, *pairing[tag], after, "pair_exchange_wait_%s" % tag)
        n = len(arrays) // 2
        hsum = _pair_sum(arrays[:n], arrays[n:], pc)
        scattering[tag], begun = _scatter_start(hsum, "scatter_start_%s" % tag)
        return begun

    def grads_done(l, s, grads):
        if l == 1:
            groups.setdefault("l1", []).extend(grads)
            return start_pairs("l1")[0, 0] if s == 0 else jnp.zeros((), f32)
        groups["l0a" if s == 1 else "l0b"] = grads
        return start_pairs("l0a")[0, 0] if s == 1 else jnp.zeros((), f32)

    swapping = {}

    def reduce_group(tag, after):
        hsum, recv = _scatter_wait(*scattering[tag], after, "scatter_wait_%s" % tag)
        full = _sum_shards(hsum, recv, pc)
        swapping[tag], begun = _split_start(_swap_copies, full, len(full), "swap_halves_start_%s" % tag)
        return [begun]

    def ffn_bwd_issued(l, s, dx):
        if l == 1:
            return jnp.zeros((), f32)
        return start_group("l1" if s == 1 else "l0a", [dx])[0, 0]

    small = {k: w[k] for k in _REPL if k != "ada_b"}
    small.update(ln_g=ln_g_full, ln_b=ln_b_full, glu_w=glu_w_full, att_bias=att_bias)
    loss_dev, grad_x, dmod, sgrads = _local_step(
        x[0], loss_target[0], mod, small, lambda l, s, after: gathered(("ffn", l, s), after),
        lambda l, after: gathered(("mix", l), after), grads_done, ffn_bwd_issued)
    loss = lax.psum(loss_dev[0, 0], ("x", "y", "c"))

    names = ("rel_bias", "ln_g", "ln_b", "ssm_a_re", "ssm_a_im", "ssm_log_dt", "ssm_b_re", "ssm_b_im", "ssm_c_re",
             "ssm_c_im", "ssm_d", "glu_w", "glu_b", "pool_w", "pool_scale")
    gpack = _pack([dmod] + [sgrads[k] for k in names])
    grows = gpack.shape[0]
    land = lax.dynamic_update_slice(lax.empty((N_CHIPS, 2, grows, 128), f32), gpack[None, None], (p_me, ac, 0, 0))
    small_in_flight, small_begun = _plane_start(gpack, land, "small_grads_start")
    l0b_begun = start_group("l0b", [start_pairs("l0b", (small_begun,))])

    out_g, out_d, out_m, out_v = {}, {}, {}, {}
    row_tile = dict(zip(_BIG, (352, 352, 352, 256, 256)))
    big = {name: None for name in _BIG}

    def update_group(tag, after):
        full = _split_wait(_swap_copies, *swapping[tag], after, "swap_halves_wait_%s" % tag)
        for (name, row0, _), g in zip(groups[tag], full):
            shp = w[name].shape
            r2 = (int(np.prod(shp[:-1])), shp[-1])
            big[name] = _adamw(w[name].reshape(r2), m[name].reshape(r2), v[name].reshape(r2), g.reshape(-1, shp[-1]),
                               row_tile[name], row0, big[name])
        return [big[name][1] for name, _, _ in groups[tag]]

    after = reduce_group("l0a", reduce_group("l1", [grad_x, l0b_begun]))
    after = update_group("l0a", update_group("l1", after))

    land = _plane_wait(*small_in_flight, after, "small_grads_wait")
    gall = _swap_halves([land])[0].reshape(8, grows, 128)
    gsum = _unpack(_sum8(gall), [(L, 9 * D)] + [sgrads[k].shape for k in names])
    red = dict(zip(("ada_b",) + names, gsum))
    red["ln_g"] = lax.dynamic_slice_in_dim(red["ln_g"], p_me * 256, 256, axis=2)
    red["ln_b"] = lax.dynamic_slice_in_dim(red["ln_b"], p_me * 256, 256, axis=2)
    red["glu_w"] = lax.dynamic_slice_in_dim(red["glu_w"], p_me * 64, 64, axis=1)

    dmod_all = gall[:, :L * 9 * D // 128].reshape(8, L, 9 * D)
    dmod_cols = jnp.transpose(lax.dynamic_slice_in_dim(dmod_all, p_me * ncol, ncol, axis=2), (1, 0, 2))
    g_ada_w = _ada_wgrad(jnp.transpose(c_all), dmod_cols)

    r2 = (L * D, ncol)
    res = _adamw(ada_w.reshape(r2), m["ada_w"].reshape(r2), v["ada_w"].reshape(r2), g_ada_w.reshape(r2), 128)
    out_g["ada_w"], out_d["ada_w"], out_m["ada_w"], out_v["ada_w"] = [t.reshape(ada_w.shape) for t in res]

    res_small = _adamw(*small_packs, _pack([red[k] for k in small_names]), small_packs[0].shape[0])
    for t, dst in zip(res_small, (out_g, out_d, out_m, out_v)):
        for k, a in zip(small_names, _unpack(t, [w[k].shape for k in small_names])):
            dst[k] = a

    update_group("l0b", reduce_group("l0b", [res_small[1], res[1]]))
    for name in _BIG:
        res = [t.reshape(w[name].shape) for t in big[name]]
        out_g[name], out_d[name], out_m[name], out_v[name] = [jnp.swapaxes(t, 2, 3) for t in res] if name in transposed else res

    return (loss, grad_x[None], *[out_g[k] for k in _ORDER], *[out_d[k] for k in _ORDER],
            *[out_m[k] for k in _ORDER], *[out_v[k] for k in _ORDER])
```

```python
import math

import numpy as np
import jax
import jax.numpy as jnp
from jax import lax
from jax.experimental import pallas as pl
from jax.experimental.pallas import tpu as pltpu

f32 = jnp.float32
bf16 = jnp.bfloat16
MESH = pl.DeviceIdType.MESH

D_MODEL = 1024
SEQ = 2048
DEPTH = 2
HEAD_DIM = 64
N_HEADS = 8
D_ATT = 512
DILATIONS = (1, 4, 16)
BLOCKS_PER_RESIDUE = (16, 4, 1)
ATT_BLOCK = 128
N_UNITS = SEQ // ATT_BLOCK
N_GROUPS = 16
SSM_STATE = 64
D_SSM = 256
D_STATE = N_GROUPS * SSM_STATE
POOL_WINDOWS = (2, 4, 8, 16)
POOL_GROUP = 64
D_POOL = 256
POOL_HALO = 16
D_FF = 2816
N_BUCKETS = 32
MAX_DISTANCE = 2048
ALPHA = (2 * DEPTH) ** 0.25
FFN_RES = 0.5
LN_EPS = 1e-5
NEG = -1e30
N_CHIPS = 4
FF_SHARD = D_FF // N_CHIPS
SCAN_SEG = 8

ADAM_LR, ADAM_B1, ADAM_B2, ADAM_EPS, ADAM_WD, ADAM_STEP = 0.001, 0.9, 0.999, 1e-08, 0.01, 10

TOK_TILE = 512


def _cp(dims=None, vmem_mb=None):
    kw = {}
    if dims is not None:
        kw["dimension_semantics"] = dims
    if vmem_mb is not None:
        kw["vmem_limit_bytes"] = vmem_mb << 20
    return pltpu.CompilerParams(**kw)


def _dot(a, b):
    return jnp.dot(a, b, preferred_element_type=f32)


def _dot_nt(a, b):
    return lax.dot_general(a, b, (((1,), (1,)), ((), ())), preferred_element_type=f32)


def _dot_tn(a, b):
    return lax.dot_general(a, b, (((0,), (0,)), ((), ())), preferred_element_type=f32)


def _ln_stats(v):
    mu = jnp.mean(v, -1, keepdims=True)
    d = v - mu
    var = jnp.mean(d * d, -1, keepdims=True)
    rstd = lax.rsqrt(var + LN_EPS)
    return d * rstd, rstd


def _ln_bwd(dxh, xh, rstd):
    return rstd * (dxh - jnp.mean(dxh, -1, keepdims=True) - xh * jnp.mean(dxh * xh, -1, keepdims=True))


_GELU_C = math.sqrt(2.0 / math.pi)


def _gelu(y):
    return 0.5 * y * (1.0 + jnp.tanh(_GELU_C * (y + 0.044715 * y * y * y)))


def _gelu_grad(y):
    t = jnp.tanh(_GELU_C * (y + 0.044715 * y * y * y))
    return 0.5 * (1.0 + t) + 0.5 * y * (1.0 - t * t) * (_GELU_C * (1.0 + 3 * 0.044715 * y * y))


def _full(shape):
    return pl.BlockSpec(shape, lambda *_: (0,) * len(shape))


def _hbm(*args):
    return [pltpu.with_memory_space_constraint(a, pltpu.HBM) if getattr(a, "ndim", 0) >= 2 else a for a in args]


def _ffn_fwd(x, mod3, wg, wu, wd, ls, lng, lnb):
    S, D = x.shape
    Fs = wg.shape[-2]
    ts = 2 * TOK_TILE

    def body(x_ref, mod_ref, wg_ref, wu_ref, wd_ref, lng_ref, lnb_ref, xo_ref, f_ref, g_ref, u_ref, a_ref, h_ref):
        j = pl.program_id(1)

        @pl.when(j == 0)
        def _():
            xh, _ = _ln_stats(x_ref[...])
            h_ref[...] = (xh * (1.0 + mod_ref[1:2, :]) + mod_ref[0:1, :]).astype(bf16)
            f_ref[...] = jnp.zeros_like(f_ref)

        h = h_ref[...]
        g = _dot_nt(h, wg_ref[0, 0])
        u = _dot_nt(h, wu_ref[0, 0])
        g_ref[0] = g.astype(bf16)
        u_ref[0] = u.astype(bf16)
        a = (g * jax.nn.sigmoid(g) * u).astype(bf16)
        a_ref[0] = a
        f_ref[...] += _dot(a, wd_ref[0, 0])

        @pl.when(j == N_CHIPS - 1)
        def _():
            f = f_ref[...]
            r = ALPHA * x_ref[...] + (FFN_RES * mod_ref[2:3, :]) * f
            rh, _ = _ln_stats(r)
            xo_ref[...] = rh * lng_ref[...] + lnb_ref[...]

    tok = pl.BlockSpec((ts, D), lambda i, j: (i, 0))
    wrow = pl.BlockSpec((1, 1, Fs, D), lambda i, j: (j, ls, 0, 0))
    hid = pl.BlockSpec((1, ts, Fs), lambda i, j: (j, i, 0))
    return pl.pallas_call(
        body, name="ffn_fwd", grid=(S // ts, N_CHIPS),
        in_specs=[tok, _full((3, D)), wrow, wrow, wrow, _full((1, D)), _full((1, D))],
        out_specs=[tok, tok, hid, hid, hid, tok],
        out_shape=[jax.ShapeDtypeStruct((S, D), f32), jax.ShapeDtypeStruct((S, D), f32)]
        + [jax.ShapeDtypeStruct((N_CHIPS, S, Fs), bf16)] * 3 + [jax.ShapeDtypeStruct((S, D), bf16)],
        compiler_params=_cp(("parallel", "arbitrary"), 56),
    )(*_hbm(x, mod3, wg, wu, wd, lng, lnb))


def _ffn_bwd(dxo, x, f, g, u, mod3, wg, wu, wd, ls, lng):
    S, D = x.shape
    Fs = wg.shape[-2]
    ts = TOK_TILE

    def body(dxo_ref, x_ref, f_ref, g_ref, u_ref, mod_ref, wg_ref, wu_ref, wd_ref, lng_ref,
             dx_ref, dg_ref, du_ref, df_ref, dmod_ref, dlng_ref, dlnb_ref,
             dr_sc, df_sc, acc_sc):
        i = pl.program_id(0)
        j = pl.program_id(1)

        @pl.when((i == 0) & (j == 0))
        def _():
            dmod_ref[...] = jnp.zeros_like(dmod_ref)
            dlng_ref[...] = jnp.zeros_like(dlng_ref)
            dlnb_ref[...] = jnp.zeros_like(dlnb_ref)

        @pl.when(j == 0)
        def _():
            xv = x_ref[...]
            fv = f_ref[...]
            gate = mod_ref[2:3, :]
            rh, rstd = _ln_stats(ALPHA * xv + (FFN_RES * gate) * fv)
            dy = dxo_ref[...]
            dlng_ref[...] += jnp.sum(dy * rh, 0, keepdims=True)
            dlnb_ref[...] += jnp.sum(dy, 0, keepdims=True)
            dr = _ln_bwd(dy * lng_ref[...], rh, rstd)
            dr_sc[...] = dr
            dmod_ref[2:3, :] += jnp.sum(FFN_RES * dr * fv, 0, keepdims=True)
            df = ((FFN_RES * gate) * dr).astype(bf16)
            df_sc[...] = df
            df_ref[...] = df
            acc_sc[...] = jnp.zeros_like(acc_sc)

        da = _dot_nt(df_sc[...], wd_ref[0, 0])
        gv = g_ref[0].astype(f32)
        uv = u_ref[0].astype(f32)
        sg = jax.nn.sigmoid(gv)
        si = gv * sg
        dgv = (da * uv * (sg * (1.0 + gv * (1.0 - sg)))).astype(bf16)
        duv = (da * si).astype(bf16)
        dg_ref[0] = dgv
        du_ref[0] = duv
        acc_sc[...] += _dot(dgv, wg_ref[0, 0]) + _dot(duv, wu_ref[0, 0])

        @pl.when(j == N_CHIPS - 1)
        def _():
            dh = acc_sc[...]
            xh, rstd0 = _ln_stats(x_ref[...])
            dmod_ref[0:1, :] += jnp.sum(dh, 0, keepdims=True)
            dmod_ref[1:2, :] += jnp.sum(dh * xh, 0, keepdims=True)
            dx_ref[...] = _ln_bwd(dh * (1.0 + mod_ref[1:2, :]), xh, rstd0) + ALPHA * dr_sc[...]

    tok = pl.BlockSpec((ts, D), lambda i, j: (i, 0))
    wrow = pl.BlockSpec((1, 1, Fs, D), lambda i, j: (j, ls, 0, 0))
    hid = pl.BlockSpec((1, ts, Fs), lambda i, j: (j, i, 0))
    hid_shape = jax.ShapeDtypeStruct((N_CHIPS, S, Fs), bf16)
    return pl.pallas_call(
        body, name="ffn_bwd", grid=(S // ts, N_CHIPS),
        in_specs=[tok, tok, tok, hid, hid, _full((3, D)), wrow, wrow, wrow, _full((1, D))],
        out_specs=[tok, hid, hid, tok, _full((3, D)), _full((1, D)), _full((1, D))],
        out_shape=[jax.ShapeDtypeStruct((S, D), f32), hid_shape, hid_shape,
                   jax.ShapeDtypeStruct((S, D), bf16),
                   jax.ShapeDtypeStruct((3, D), f32), jax.ShapeDtypeStruct((1, D), f32), jax.ShapeDtypeStruct((1, D), f32)],
        scratch_shapes=[pltpu.VMEM((ts, D), f32), pltpu.VMEM((ts, D), bf16), pltpu.VMEM((ts, D), f32)],
        compiler_params=_cp(("arbitrary", "arbitrary"), 56),
    )(*_hbm(dxo, x, f, g, u, mod3, wg, wu, wd, lng))


def _ffn_wgrad(h, dg, du, a, df, gwg, gwu, gwd, ls):
    S, D = h.shape
    Fs = dg.shape[-1]
    tk = 2 * TOK_TILE
    nk = S // tk

    def body(h_ref, dg_ref, du_ref, a_ref, df_ref, _g0, _g1, _g2, gwg_ref, gwu_ref, gwd_ref, ag_sc, au_sc, ad_sc):
        k = pl.program_id(1)

        @pl.when(k == 0)
        def _():
            ag_sc[...] = jnp.zeros_like(ag_sc)
            au_sc[...] = jnp.zeros_like(au_sc)
            ad_sc[...] = jnp.zeros_like(ad_sc)

        hv = h_ref[...]
        ag_sc[...] += _dot_tn(dg_ref[0], hv)
        au_sc[...] += _dot_tn(du_ref[0], hv)
        ad_sc[...] += _dot_tn(a_ref[0], df_ref[...])

        @pl.when(k == nk - 1)
        def _():
            gwg_ref[0, 0] = ag_sc[...].astype(bf16)
            gwu_ref[0, 0] = au_sc[...].astype(bf16)
            gwd_ref[0, 0] = ad_sc[...].astype(bf16)

    tok = pl.BlockSpec((tk, D), lambda p, k: (k, 0))
    hid = pl.BlockSpec((1, tk, Fs), lambda p, k: (p, k, 0))
    anyspec = pl.BlockSpec(memory_space=pl.ANY)
    orow = pl.BlockSpec((1, 1, Fs, D), lambda p, k: (p, ls, 0, 0))
    return pl.pallas_call(
        body, name="ffn_wgrad", grid=(N_CHIPS, nk),
        in_specs=[tok, hid, hid, hid, tok, anyspec, anyspec, anyspec],
        out_specs=[orow, orow, orow],
        out_shape=[jax.ShapeDtypeStruct(gwg.shape, bf16), jax.ShapeDtypeStruct(gwu.shape, bf16),
                   jax.ShapeDtypeStruct(gwd.shape, bf16)],
        scratch_shapes=[pltpu.VMEM((Fs, D), f32), pltpu.VMEM((Fs, D), f32), pltpu.VMEM((Fs, D), f32)],
        input_output_aliases={5: 0, 6: 1, 7: 2},
        compiler_params=_cp(("parallel", "arbitrary"), 48),
    )(*_hbm(h, dg, du, a, df, gwg, gwu, gwd))


_LANES = 128
_QKV_BLOCKS = D_ATT // _LANES


def _res_spec(lead, d, width, index):
    return pl.BlockSpec((lead, d, TOK_TILE // d, width), index)


def _res_spec3(d, width):
    return pl.BlockSpec((d, TOK_TILE // d, width), lambda i: (0, i, 0))


def _rows_to_residues(tile_bufs, d, put):
    for r in range(d):
        for cb, buf in enumerate(tile_bufs):
            put(r, cb, buf[pl.ds(r, TOK_TILE // d, stride=d), :])


def _residues_to_rows(tile_bufs, d, get):
    for r in range(d):
        for cb, buf in enumerate(tile_bufs):
            buf[pl.ds(r, TOK_TILE // d, stride=d), :] = get(r, cb)


def _mix_in_fwd(x, mod3, w_in, l):
    S, D = x.shape
    N = w_in.shape[-1]
    ts = TOK_TILE

    def body(x_ref, mod_ref, w_ref, o1_ref, o4_ref, o16_ref, zr_ref, h_ref, *bufs):
        j = pl.program_id(1)

        @pl.when(j == 0)
        def _():
            xh, _ = _ln_stats(x_ref[...])
            h_ref[...] = (xh * (1.0 + mod_ref[1:2, :]) + mod_ref[0:1, :]).astype(bf16)

        z = _dot(h_ref[...], w_ref[0, 0])

        @pl.when(j == N_CHIPS - 1)
        def _():
            zr_ref[...] = z

        @pl.when(j < N_CHIPS - 1)
        def _():
            zz = z * jnp.where(j == 0, HEAD_DIM ** -0.5, 1.0)
            o1_ref[j, 0] = zz.astype(bf16)
            for cb, buf in enumerate(bufs):
                buf[...] = zz[:, _LANES * cb:_LANES * (cb + 1)]
            for d, o_ref in zip(DILATIONS[1:], (o4_ref, o16_ref)):
                def put(r, cb, piece, o_ref=o_ref):
                    o_ref[j, r, :, _LANES * cb:_LANES * (cb + 1)] = piece.astype(bf16)
                _rows_to_residues(bufs, d, put)

    tok = pl.BlockSpec((ts, D), lambda i, j: (i, 0))
    res = [_res_spec(3, d, N, lambda i, j: (0, 0, i, 0)) for d in DILATIONS]
    return pl.pallas_call(
        body, name="mix_in_fwd", grid=(S // ts, N_CHIPS),
        in_specs=[tok, _full((3, D)), pl.BlockSpec((1, 1, D, N), lambda i, j: (j, l, 0, 0))],
        out_specs=res + [pl.BlockSpec((ts, N), lambda i, j: (i, 0)), tok],
        out_shape=[jax.ShapeDtypeStruct((3, d, S // d, N), bf16) for d in DILATIONS]
        + [jax.ShapeDtypeStruct((S, N), f32), jax.ShapeDtypeStruct((S, D), bf16)],
        scratch_shapes=[pltpu.VMEM((ts, _LANES), f32)] * _QKV_BLOCKS,
        compiler_params=_cp(("parallel", "arbitrary"), 40),
    )(*_hbm(x, mod3, w_in))


def _mix_in_bwd(dqkv, d_rest, dx_res, x, mod3, w_in, l):
    S, D = x.shape
    N = w_in.shape[-1]
    ts = TOK_TILE

    def body(d1_ref, d4_ref, d16_ref, dr_ref, dxr_ref, x_ref, mod_ref, w_ref, dx_ref, dmod_ref, dz_ref, acc_sc, *bufs):
        i = pl.program_id(0)
        j = pl.program_id(1)

        @pl.when((i == 0) & (j == 0))
        def _():
            dmod_ref[...] = jnp.zeros_like(dmod_ref)

        @pl.when(j == 0)
        def _():
            acc_sc[...] = jnp.zeros_like(acc_sc)

        @pl.when(j == N_CHIPS - 1)
        def _():
            dz_ref[0] = dr_ref[...]

        @pl.when(j < N_CHIPS - 1)
        def _():
            for d, d_ref, tile_bufs in ((4, d4_ref, bufs[:_QKV_BLOCKS]), (16, d16_ref, bufs[_QKV_BLOCKS:])):
                _residues_to_rows(tile_bufs, d, lambda r, cb, d_ref=d_ref: d_ref[0, r, :, _LANES * cb:_LANES * (cb + 1)].astype(f32))
            for cb in range(_QKV_BLOCKS):
                cols = slice(_LANES * cb, _LANES * (cb + 1))
                dz_ref[0, :, cols] = (d1_ref[0, 0, :, cols].astype(f32) + bufs[cb][...] + bufs[_QKV_BLOCKS + cb][...]).astype(bf16)

        acc_sc[...] += _dot_nt(dz_ref[0], w_ref[0, 0])

        @pl.when(j == N_CHIPS - 1)
        def _():
            dh = acc_sc[...]
            xh, rstd0 = _ln_stats(x_ref[...])
            dmod_ref[0:1, :] += jnp.sum(dh, 0, keepdims=True)
            dmod_ref[1:2, :] += jnp.sum(dh * xh, 0, keepdims=True)
            dx_ref[...] = _ln_bwd(dh * (1.0 + mod_ref[1:2, :]), xh, rstd0) + dxr_ref[...]

    tok = pl.BlockSpec((ts, D), lambda i, j: (i, 0))
    res = [_res_spec(1, d, N, lambda i, j: (jnp.minimum(j, 2), 0, i, 0)) for d in DILATIONS]
    return pl.pallas_call(
        body, name="mix_in_bwd", grid=(S // ts, N_CHIPS),
        in_specs=res + [pl.BlockSpec((ts, N), lambda i, j: (i, 0)), tok, tok, _full((3, D)),
                        pl.BlockSpec((1, 1, D, N), lambda i, j: (j, l, 0, 0))],
        out_specs=[tok, _full((3, D)), pl.BlockSpec((1, ts, N), lambda i, j: (j, i, 0))],
        out_shape=[jax.ShapeDtypeStruct((S, D), f32), jax.ShapeDtypeStruct((3, D), f32),
                   jax.ShapeDtypeStruct((N_CHIPS, S, N), bf16)],
        scratch_shapes=[pltpu.VMEM((ts, D), f32)] + [pltpu.VMEM((ts, _LANES), f32)] * (2 * _QKV_BLOCKS),
        compiler_params=_cp(("arbitrary", "arbitrary"), 40),
    )(*_hbm(*dqkv, d_rest, dx_res, x, mod3, w_in))


def _mix_in_wgrad(h, dz, gw, l):
    S, D = h.shape
    N = dz.shape[-1]
    tk = 2 * TOK_TILE
    nk = S // tk

    def body(h_ref, dz_ref, _g, gw_ref, acc_sc):
        k = pl.program_id(1)

        @pl.when(k == 0)
        def _():
            acc_sc[...] = jnp.zeros_like(acc_sc)

        acc_sc[...] += _dot_tn(h_ref[...], dz_ref[0])

        @pl.when(k == nk - 1)
        def _():
            gw_ref[0, 0] = acc_sc[...].astype(bf16)

    return pl.pallas_call(
        body, name="mix_in_wgrad", grid=(N_CHIPS, nk),
        in_specs=[pl.BlockSpec((tk, D), lambda p, k: (k, 0)), pl.BlockSpec((1, tk, N), lambda p, k: (p, k, 0)),
                  pl.BlockSpec(memory_space=pl.ANY)],
        out_specs=pl.BlockSpec((1, 1, D, N), lambda p, k: (p, l, 0, 0)),
        out_shape=jax.ShapeDtypeStruct(gw.shape, bf16),
        scratch_shapes=[pltpu.VMEM((D, N), f32)],
        input_output_aliases={2: 0},
        compiler_params=_cp(("parallel", "arbitrary"), 40),
    )(*_hbm(h, dz, gw))


def _mix_out_fwd(x, y_att, y_ssm, y_pool, mod3, w_out, l, lng, lnb):
    S, D = x.shape
    ts = TOK_TILE

    def body(x_ref, ya_ref, ys_ref, yp_ref, mod_ref, w_ref, lng_ref, lnb_ref, xo_ref, y_ref):
        ya = ya_ref[...].astype(bf16)
        y = (_dot(ya[:, 0:256], w_ref[0, 0]) + _dot(ya[:, 256:512], w_ref[1, 0])
             + _dot(ys_ref[...].astype(bf16), w_ref[2, 0]) + _dot(yp_ref[...].astype(bf16), w_ref[3, 0]))
        y_ref[...] = y
        rh, _ = _ln_stats(ALPHA * x_ref[...] + mod_ref[2:3, :] * y)
        xo_ref[...] = rh * lng_ref[...] + lnb_ref[...]

    tok = pl.BlockSpec((ts, D), lambda i: (i, 0))
    return pl.pallas_call(
        body, name="mix_out_fwd", grid=(S // ts,),
        in_specs=[tok, pl.BlockSpec((ts, D_ATT), lambda i: (i, 0)), pl.BlockSpec((ts, D_SSM), lambda i: (i, 0)),
                  pl.BlockSpec((ts, D_POOL), lambda i: (i, 0)), _full((3, D)),
                  pl.BlockSpec((N_CHIPS, 1, 256, D), lambda i: (0, l, 0, 0)), _full((1, D)), _full((1, D))],
        out_specs=[tok, tok],
        out_shape=[jax.ShapeDtypeStruct((S, D), f32), jax.ShapeDtypeStruct((S, D), f32)],
        compiler_params=_cp(("parallel",), 40),
    )(*_hbm(x, y_att, y_ssm, y_pool, mod3, w_out, lng, lnb))


def _mix_out_bwd(dxo, x, y, y_att, y_ssm, y_pool, mod3, w_out, l, lng, gw_out):
    S, D = x.shape
    ts = TOK_TILE
    nt = S // ts

    def body(dxo_ref, x_ref, y_ref, ya_ref, ys_ref, yp_ref, mod_ref, w_ref, lng_ref, _g,
             dxr_ref, da_ref, ds_ref, dp_ref, dgate_ref, dlng_ref, dlnb_ref, gw_ref, acc_sc):
        i = pl.program_id(0)

        @pl.when(i == 0)
        def _():
            dgate_ref[...] = jnp.zeros_like(dgate_ref)
            dlng_ref[...] = jnp.zeros_like(dlng_ref)
            dlnb_ref[...] = jnp.zeros_like(dlnb_ref)
            acc_sc[...] = jnp.zeros_like(acc_sc)

        gate = mod_ref[2:3, :]
        yv = y_ref[...]
        rh, rstd = _ln_stats(ALPHA * x_ref[...] + gate * yv)
        dy_out = dxo_ref[...]
        dlng_ref[...] += jnp.sum(dy_out * rh, 0, keepdims=True)
        dlnb_ref[...] += jnp.sum(dy_out, 0, keepdims=True)
        dr = _ln_bwd(dy_out * lng_ref[...], rh, rstd)
        dxr_ref[...] = ALPHA * dr
        dgate_ref[...] += jnp.sum(dr * yv, 0, keepdims=True)
        dy = (gate * dr).astype(bf16)
        da_ref[:, 0:256] = _dot_nt(dy, w_ref[0, 0])
        da_ref[:, 256:512] = _dot_nt(dy, w_ref[1, 0])
        ds_ref[...] = _dot_nt(dy, w_ref[2, 0])
        dp_ref[...] = _dot_nt(dy, w_ref[3, 0])
        ya = ya_ref[...].astype(bf16)
        acc_sc[0] += _dot_tn(ya[:, 0:256], dy)
        acc_sc[1] += _dot_tn(ya[:, 256:512], dy)
        acc_sc[2] += _dot_tn(ys_ref[...].astype(bf16), dy)
        acc_sc[3] += _dot_tn(yp_ref[...].astype(bf16), dy)

        @pl.when(i == nt - 1)
        def _():
            gw_ref[:, 0] = acc_sc[...].astype(bf16)

    tok = pl.BlockSpec((ts, D), lambda i: (i, 0))
    t512 = pl.BlockSpec((ts, D_ATT), lambda i: (i, 0))
    t256 = pl.BlockSpec((ts, 256), lambda i: (i, 0))
    wspec = pl.BlockSpec((N_CHIPS, 1, 256, D), lambda i: (0, l, 0, 0))
    return pl.pallas_call(
        body, name="mix_out_bwd", grid=(nt,),
        in_specs=[tok, tok, tok, t512, t256, t256, _full((3, D)), wspec, _full((1, D)), pl.BlockSpec(memory_space=pl.ANY)],
        out_specs=[tok, t512, t256, t256, _full((1, D)), _full((1, D)), _full((1, D)), wspec],
        out_shape=[jax.ShapeDtypeStruct((S, D), f32), jax.ShapeDtypeStruct((S, D_ATT), f32),
                   jax.ShapeDtypeStruct((S, D_SSM), f32), jax.ShapeDtypeStruct((S, D_POOL), f32),
                   jax.ShapeDtypeStruct((1, D), f32), jax.ShapeDtypeStruct((1, D), f32), jax.ShapeDtypeStruct((1, D), f32),
                   jax.ShapeDtypeStruct(gw_out.shape, bf16)],
        scratch_shapes=[pltpu.VMEM((N_CHIPS, 256, D), f32)],
        input_output_aliases={9: 7},
        compiler_params=_cp(("arbitrary",), 48),
    )(*_hbm(dxo, x, y, y_att, y_ssm, y_pool, mod3, w_out, lng, gw_out))


def _t5_bucket(dist):
    max_exact = N_BUCKETS // 2
    d = np.maximum(dist, 1).astype(np.float32)
    large = max_exact + (np.log(d / max_exact) / math.log(MAX_DISTANCE / max_exact)
                         * (N_BUCKETS - max_exact)).astype(np.int32)
    large = np.minimum(large, N_BUCKETS - 1)
    return np.where(dist < max_exact, dist, large).astype(np.int32)


def _bucket_table():
    q = ATT_BLOCK
    i = np.arange(q)[:, None]
    j = np.arange(2 * q)[None, :]
    r = i + q - j
    in_band = (r >= 0) & (r <= q)
    tabs = [np.where(in_band, _t5_bucket(np.clip(r, 0, None) * d), -1) for d in DILATIONS]
    return np.stack(tabs).astype(np.int32)


def _bias_fwd(rel_bias, table):
    def body(rb_ref, tab_ref, out_ref):
        for b in range(3):
            tb = tab_ref[b]
            for h in range(N_HEADS):
                def pick(k, acc):
                    return jnp.where(tb == k, rb_ref[k, h], acc)
                out_ref[b, h] = lax.fori_loop(0, N_BUCKETS, pick, jnp.where(tb < 0, NEG, 0.0).astype(f32))

    return pl.pallas_call(
        body, name="bias_fwd",
        in_specs=[pl.BlockSpec(memory_space=pltpu.SMEM), pl.BlockSpec(memory_space=pltpu.VMEM)],
        out_specs=pl.BlockSpec(memory_space=pltpu.VMEM),
        out_shape=jax.ShapeDtypeStruct((3, N_HEADS, ATT_BLOCK, 2 * ATT_BLOCK), f32),
    )(rel_bias, table)


def _bias_bwd(dbias, table):
    def body(db_ref, tab_ref, out_ref):
        def per_bucket(k, c):
            for h in range(N_HEADS):
                tot = jnp.zeros((), f32)
                for b in range(3):
                    tot = tot + jnp.sum(jnp.where(tab_ref[b] == k, db_ref[b, h], 0.0))
                out_ref[k, h] = tot
            return c
        lax.fori_loop(0, N_BUCKETS, per_bucket, 0)

    return pl.pallas_call(
        body, name="bias_bwd",
        in_specs=[pl.BlockSpec(memory_space=pltpu.VMEM), pl.BlockSpec(memory_space=pltpu.VMEM)],
        out_specs=pl.BlockSpec(memory_space=pltpu.SMEM),
        out_shape=jax.ShapeDtypeStruct((N_BUCKETS, N_HEADS), f32),
    )(dbias, table)


def _att_unit(u, nbr):
    rows = pl.ds(pl.multiple_of(u * ATT_BLOCK, ATT_BLOCK), ATT_BLOCK)
    prev = pl.ds(pl.multiple_of(jnp.maximum(u - 1, 0) * ATT_BLOCK, ATT_BLOCK), ATT_BLOCK)
    return rows, prev, (u % nbr) != 0


_N_PAIRS = N_HEADS // 2


def _pair_rows(t):
    lane = lax.broadcasted_iota(jnp.int32, t.shape, 1)
    zero = jnp.zeros_like(t)
    return jnp.concatenate([jnp.where(lane < HEAD_DIM, t, zero), jnp.where(lane >= HEAD_DIM, t, zero)], axis=0)


def _pair_cols(big):
    lane = lax.broadcasted_iota(jnp.int32, (ATT_BLOCK, _LANES), 1)
    return jnp.where(lane < HEAD_DIM, big[:ATT_BLOCK], big[ATT_BLOCK:])


def _pair_column(ref, rows, hp):
    t = ref[rows, :]
    return jnp.concatenate([t[:, 2 * hp:2 * hp + 1], t[:, 2 * hp + 1:2 * hp + 2]], axis=0)


def _pair_band(ref, rows, prev, nbr, hp):
    lanes = pl.ds(_LANES * hp, _LANES)
    cur = ref[0, rows, lanes]
    return cur if nbr == 1 else jnp.concatenate([ref[0, prev, lanes], cur], axis=0)


def _pair_scores(q_ref, k_ref, b_ref, rows, prev, valid_prev, nbr, hp):
    qbd = _pair_rows(q_ref[0, rows, pl.ds(_LANES * hp, _LANES)])
    kb = _pair_band(k_ref, rows, prev, nbr, hp)
    bias = b_ref[0, 2 * hp:2 * hp + 2].reshape(2 * ATT_BLOCK, 2 * ATT_BLOCK)
    if nbr == 1:
        return qbd, kb, _dot_nt(qbd, kb) + bias[:, ATT_BLOCK:]
    s = _dot_nt(qbd, kb) + bias
    col = lax.broadcasted_iota(jnp.int32, s.shape, 1)
    return qbd, kb, jnp.where((col >= ATT_BLOCK) | valid_prev, s, NEG)


def _qkv_specs(S, branch):
    return ([pl.BlockSpec((1, S, D_ATT), lambda i, t=t: (t, 0, 0)) for t in range(3)],
            pl.BlockSpec((1, N_HEADS, ATT_BLOCK, 2 * ATT_BLOCK), lambda i: (branch, 0, 0, 0)))


def _att_fwd(qkv, bias, branch):
    S = qkv.shape[1]
    nbr = BLOCKS_PER_RESIDUE[branch]

    def body(q_ref, k_ref, v_ref, b_ref, o_ref, lse_ref):
        lse_ref[...] = jnp.zeros_like(lse_ref)

        def unit(u, c):
            rows, prev, valid_prev = _att_unit(u, nbr)
            for hp in range(_N_PAIRS):
                _, _, s = _pair_scores(q_ref, k_ref, b_ref, rows, prev, valid_prev, nbr, hp)
                m = jnp.max(s, -1, keepdims=True)
                p = jnp.exp(s - m)
                den = jnp.sum(p, -1, keepdims=True)
                big = _dot(p.astype(bf16), _pair_band(v_ref, rows, prev, nbr, hp))
                o_ref[rows, pl.ds(_LANES * hp, _LANES)] = _pair_cols(big / den)
                lse = m + jnp.log(den)
                lse_ref[rows, pl.ds(2 * hp, 1)] = lse[:ATT_BLOCK]
                lse_ref[rows, pl.ds(2 * hp + 1, 1)] = lse[ATT_BLOCK:]
            return c

        lax.fori_loop(0, N_UNITS, unit, 0)

    qkv_specs, bspec = _qkv_specs(S, branch)
    return pl.pallas_call(
        body, name="att_fwd", grid=(1,),
        in_specs=qkv_specs + [bspec],
        out_specs=[pl.BlockSpec((S, D_ATT), lambda i: (0, 0)), pl.BlockSpec((S, _LANES), lambda i: (0, 0))],
        out_shape=[jax.ShapeDtypeStruct((S, D_ATT), f32), jax.ShapeDtypeStruct((S, _LANES), f32)],
        compiler_params=_cp(("arbitrary",), 40),
    )(*_hbm(qkv, qkv, qkv, bias))


def _att_bwd(qkv, do, lse, crow, bias, branch):
    S = qkv.shape[1]
    nbr = BLOCKS_PER_RESIDUE[branch]

    def body(q_ref, k_ref, v_ref, do_ref, lse_ref, c_ref, b_ref, dqkv_ref, db_ref, dk_sc, dv_sc):
        dk_sc[...] = jnp.zeros_like(dk_sc)
        dv_sc[...] = jnp.zeros_like(dv_sc)
        db_ref[...] = jnp.zeros_like(db_ref)

        def unit(u, c):
            rows, prev, valid_prev = _att_unit(u, nbr)
            for hp in range(_N_PAIRS):
                lanes = pl.ds(_LANES * hp, _LANES)
                qbd, kb, s = _pair_scores(q_ref, k_ref, b_ref, rows, prev, valid_prev, nbr, hp)
                p = jnp.exp(s - _pair_column(lse_ref, rows, hp))
                dobd = _pair_rows(do_ref[rows, lanes])
                ds = p * (_dot_nt(dobd, _pair_band(v_ref, rows, prev, nbr, hp)) - _pair_column(c_ref, rows, hp))
                if nbr == 1:
                    db_ref[2 * hp:2 * hp + 2, :, ATT_BLOCK:] += ds.reshape(2, ATT_BLOCK, ATT_BLOCK)
                else:
                    db_ref[2 * hp:2 * hp + 2] += ds.reshape(2, ATT_BLOCK, 2 * ATT_BLOCK)
                dsb = ds.astype(bf16)
                dqkv_ref[0, rows, lanes] = (HEAD_DIM ** -0.5 * _pair_cols(_dot(dsb, kb))).astype(bf16)
                dkb = _dot_tn(dsb, qbd)
                dvb = _dot_tn(p.astype(bf16), dobd)
                if nbr == 1:
                    dk_sc[rows, lanes] += dkb
                    dv_sc[rows, lanes] += dvb
                else:
                    dk_sc[prev, lanes] += dkb[:ATT_BLOCK]
                    dv_sc[prev, lanes] += dvb[:ATT_BLOCK]
                    dk_sc[rows, lanes] += dkb[ATT_BLOCK:]
                    dv_sc[rows, lanes] += dvb[ATT_BLOCK:]
            return c

        lax.fori_loop(0, N_UNITS, unit, 0)
        dqkv_ref[1] = dk_sc[...].astype(bf16)
        dqkv_ref[2] = dv_sc[...].astype(bf16)

    qkv_specs, bspec = _qkv_specs(S, branch)
    row = pl.BlockSpec((S, _LANES), lambda i: (0, 0))
    return pl.pallas_call(
        body, name="att_bwd", grid=(1,),
        in_specs=qkv_specs + [pl.BlockSpec((S, D_ATT), lambda i: (0, 0)), row, row, bspec],
        out_specs=[pl.BlockSpec((3, S, D_ATT), lambda i: (0, 0, 0)),
                   pl.BlockSpec((N_HEADS, ATT_BLOCK, 2 * ATT_BLOCK), lambda i: (0, 0, 0))],
        out_shape=[jax.ShapeDtypeStruct((3, S, D_ATT), bf16), jax.ShapeDtypeStruct((N_HEADS, ATT_BLOCK, 2 * ATT_BLOCK), f32)],
        scratch_shapes=[pltpu.VMEM((S, D_ATT), f32), pltpu.VMEM((S, D_ATT), f32)],
        compiler_params=_cp(("arbitrary",), 48),
    )(*_hbm(qkv, qkv, qkv, do, lse, crow, bias))


def _branch_weights(lse_ref):
    l0, l1, l2 = lse_ref[0], lse_ref[1], lse_ref[2]
    m = jnp.maximum(jnp.maximum(l0, l1), l2)
    e0, e1, e2 = jnp.exp(l0 - m), jnp.exp(l1 - m), jnp.exp(l2 - m)
    tot = e0 + e1 + e2
    return e0 / tot, e1 / tot, e2 / tot


def _att_merge(os, lses):
    S = os[0].shape[0] * os[0].shape[1]
    ts = TOK_TILE

    def body(o1_ref, o4_ref, o16_ref, l1_ref, l4_ref, l16_ref, y_ref, lt_ref, *bufs):
        obufs = (bufs[:_QKV_BLOCKS], bufs[_QKV_BLOCKS:2 * _QKV_BLOCKS])
        lt_ref[0] = l1_ref[0]
        for k, (d, o_ref, l_ref) in enumerate(((4, o4_ref, l4_ref), (16, o16_ref, l16_ref))):
            _residues_to_rows(obufs[k], d, lambda r, cb, o_ref=o_ref: o_ref[r, :, _LANES * cb:_LANES * (cb + 1)])
            _residues_to_rows([bufs[2 * _QKV_BLOCKS + k]], d, lambda r, cb, l_ref=l_ref: l_ref[r])
            lt_ref[1 + k] = bufs[2 * _QKV_BLOCKS + k][...]
        w = _branch_weights(lt_ref)
        for h in range(N_HEADS):
            cs = slice(HEAD_DIM * h, HEAD_DIM * (h + 1))
            half = slice(HEAD_DIM * (h % 2), HEAD_DIM * (h % 2 + 1))
            y_ref[:, cs] = (w[0][:, h:h + 1] * o1_ref[0, :, cs] + w[1][:, h:h + 1] * obufs[0][h // 2][:, half]
                            + w[2][:, h:h + 1] * obufs[1][h // 2][:, half])

    return pl.pallas_call(
        body, name="att_merge", grid=(S // ts,),
        in_specs=[_res_spec3(d, D_ATT) for d in DILATIONS] + [_res_spec3(d, _LANES) for d in DILATIONS],
        out_specs=[pl.BlockSpec((ts, D_ATT), lambda i: (i, 0)), pl.BlockSpec((3, ts, _LANES), lambda i: (0, i, 0))],
        out_shape=[jax.ShapeDtypeStruct((S, D_ATT), f32), jax.ShapeDtypeStruct((3, S, _LANES), f32)],
        scratch_shapes=[pltpu.VMEM((ts, _LANES), f32)] * (2 * _QKV_BLOCKS + 2),
        compiler_params=_cp(("parallel",)),
    )(*_hbm(*os, *lses))


def _att_merge_bwd(dy, y, lse3):
    S = dy.shape[0]
    ts = TOK_TILE

    def body(dy_ref, y_ref, lse_ref, do1_ref, do4_ref, do16_ref, c1_ref, c4_ref, c16_ref, *bufs):
        dobufs = (bufs[:_QKV_BLOCKS], bufs[_QKV_BLOCKS:2 * _QKV_BLOCKS], bufs[2 * _QKV_BLOCKS:3 * _QKV_BLOCKS])
        cbufs = bufs[3 * _QKV_BLOCKS:]
        w = _branch_weights(lse_ref)
        for cb in cbufs:
            cb[...] = jnp.zeros_like(cb)
        for h in range(N_HEADS):
            cs = slice(HEAD_DIM * h, HEAD_DIM * (h + 1))
            half = slice(HEAD_DIM * (h % 2), HEAD_DIM * (h % 2 + 1))
            dyh = dy_ref[:, cs]
            t = jnp.sum(dyh * y_ref[:, cs], -1, keepdims=True)
            for p in range(3):
                wp = w[p][:, h:h + 1]
                dobufs[p][h // 2][:, half] = wp * dyh
                cbufs[p][:, h:h + 1] = wp * t
        for cb in range(_QKV_BLOCKS):
            do1_ref[0, :, _LANES * cb:_LANES * (cb + 1)] = dobufs[0][cb][...].astype(bf16)
        c1_ref[0] = cbufs[0][...]
        for k, (d, do_ref, c_ref) in enumerate(((4, do4_ref, c4_ref), (16, do16_ref, c16_ref))):
            def put_do(r, cb, piece, do_ref=do_ref):
                do_ref[r, :, _LANES * cb:_LANES * (cb + 1)] = piece.astype(bf16)

            def put_c(r, cb, piece, c_ref=c_ref):
                c_ref[r] = piece

            _rows_to_residues(dobufs[1 + k], d, put_do)
            _rows_to_residues([cbufs[1 + k]], d, put_c)

    return pl.pallas_call(
        body, name="att_merge_bwd", grid=(S // ts,),
        in_specs=[pl.BlockSpec((ts, D_ATT), lambda i: (i, 0)), pl.BlockSpec((ts, D_ATT), lambda i: (i, 0)),
                  pl.BlockSpec((3, ts, _LANES), lambda i: (0, i, 0))],
        out_specs=[_res_spec3(d, D_ATT) for d in DILATIONS] + [_res_spec3(d, _LANES) for d in DILATIONS],
        out_shape=[jax.ShapeDtypeStruct((d, S // d, D_ATT), bf16) for d in DILATIONS]
        + [jax.ShapeDtypeStruct((d, S // d, _LANES), f32) for d in DILATIONS],
        scratch_shapes=[pltpu.VMEM((ts, _LANES), f32)] * (3 * _QKV_BLOCKS + 3),
        compiler_params=_cp(("parallel",)),
    )(*_hbm(dy, y, lse3))


_SSM_ROWS = 256


def _scan_in_place(sr_ref, si_ref, a_ref, reverse):
    S, N = sr_ref.shape
    nst = S // SCAN_SEG
    ar = jnp.broadcast_to(a_ref[0:1, :], (SCAN_SEG, N))
    ai = jnp.broadcast_to(a_ref[1:2, :], (SCAN_SEG, N))
    if reverse:
        ai = -ai
    row = lax.broadcasted_iota(jnp.int32, (SCAN_SEG, N), 0)
    zero = jnp.zeros((SCAN_SEG, N), f32)

    def tile(t):
        return pl.ds(pl.multiple_of((nst - 1 - t if reverse else t) * SCAN_SEG, SCAN_SEG), SCAN_SEG)

    def local(t, c):
        sr, si, pr, pi = c
        rows = tile(t)
        nsr = ar * sr - ai * si + sr_ref[rows, :]
        nsi = ar * si + ai * sr + si_ref[rows, :]
        sr_ref[rows, :] = nsr
        si_ref[rows, :] = nsi
        return nsr, nsi, ar * pr - ai * pi, ar * pi + ai * pr

    fr, fi, apr, api = lax.fori_loop(0, nst, local, (zero, zero, zero + 1.0, zero))

    def shift(v):
        if reverse:
            return jnp.where(row == SCAN_SEG - 1, 0.0, pltpu.roll(v, SCAN_SEG - 1, axis=0))
        return jnp.where(row == 0, 0.0, pltpu.roll(v, 1, axis=0))

    cr, ci = zero, zero
    for _ in range(SCAN_SEG - 1):
        cr, ci = shift(fr + apr * cr - api * ci), shift(fi + apr * ci + api * cr)

    def fix(t, c):
        pr, pi = c
        npr, npi = ar * pr - ai * pi, ar * pi + ai * pr
        rows = tile(t)
        sr_ref[rows, :] += npr * cr - npi * ci
        si_ref[rows, :] += npr * ci + npi * cr
        return npr, npi

    lax.fori_loop(0, nst, fix, (zero + 1.0, zero))


def _ssm_fwd(u, bre, bim, a2, cre, cim, l, dskip, glu_w, glu_b):
    S = u.shape[0]
    nproj = S // _SSM_ROWS

    def body(u_ref, br_ref, bi_ref, a2_ref, cr_ref, ci_ref, d_ref, w_ref, b_ref, sr_ref, si_ref, out_ref, y_ref):
        a_ref = a2_ref.at[l]
        brb = br_ref[l].astype(bf16)
        bib = bi_ref[l].astype(bf16)

        def project(t, c):
            rows = pl.ds(pl.multiple_of(t * _SSM_ROWS, _SSM_ROWS), _SSM_ROWS)
            ub = u_ref[rows, :].astype(bf16)
            sr_ref[rows, :] = _dot(ub, brb)
            si_ref[rows, :] = _dot(ub, bib)
            return c

        lax.fori_loop(0, nproj, project, 0)
        _scan_in_place(sr_ref, si_ref, a_ref, False)

        crb = cr_ref[l].astype(bf16)
        cib = ci_ref[l].astype(bf16)
        wb = w_ref[...].astype(bf16)

        def read_out(t, c):
            rows = pl.ds(pl.multiple_of(t * _SSM_ROWS, _SSM_ROWS), _SSM_ROWS)
            y = (_dot(sr_ref[rows, :].astype(bf16), crb) - _dot(si_ref[rows, :].astype(bf16), cib)
                 + d_ref[...] * u_ref[rows, :])
            y_ref[rows, :] = y
            z = _dot(_gelu(y).astype(bf16), wb) + b_ref[...]
            out_ref[rows, :] = y * jax.nn.sigmoid(z)
            return c

        lax.fori_loop(0, nproj, read_out, 0)

    vm = pl.BlockSpec(memory_space=pltpu.VMEM)
    return pl.pallas_call(
        body, name="ssm_fwd", in_specs=[vm] * 9, out_specs=[vm] * 4,
        out_shape=[jax.ShapeDtypeStruct((S, D_STATE), f32)] * 2 + [jax.ShapeDtypeStruct((S, D_SSM), f32)] * 2,
        compiler_params=_cp(None, 48),
    )(u, bre, bim, a2, cre, cim, dskip, glu_w, glu_b)


def _ssm_out_bwd(dout, y, u, dskip, glu_w, glu_b):
    S = u.shape[0]
    ts = TOK_TILE

    def body(do_ref, y_ref, u_ref, d_ref, w_ref, b_ref, dy_ref, du_ref, dd_ref, dgb_ref, dgw_ref):
        @pl.when(pl.program_id(0) == 0)
        def _():
            for r in (dd_ref, dgb_ref, dgw_ref):
                r[...] = jnp.zeros_like(r)

        y = y_ref[...]
        dout = do_ref[...]
        wb = w_ref[...].astype(bf16)
        ge = _gelu(y).astype(bf16)
        sz = jax.nn.sigmoid(_dot(ge, wb) + b_ref[...])
        dz = dout * y * sz * (1.0 - sz)
        dzb = dz.astype(bf16)
        dgb_ref[...] += jnp.sum(dz, 0, keepdims=True)
        dgw_ref[...] += _dot_tn(ge, dzb)
        dy = dout * sz + _gelu_grad(y) * _dot_nt(dzb, wb)
        uv = u_ref[...]
        dd_ref[...] += jnp.sum(dy * uv, 0, keepdims=True)
        du_ref[...] = dy * d_ref[...]
        dy_ref[...] = dy

    ch = pl.BlockSpec((ts, D_SSM), lambda i: (i, 0))
    return pl.pallas_call(
        body, name="ssm_out_bwd", grid=(S // ts,),
        in_specs=[ch, ch, ch, _full((1, D_SSM)), _full((D_SSM, D_SSM)), _full((1, D_SSM))],
        out_specs=[ch, ch, _full((1, D_SSM)), _full((1, D_SSM)), _full((D_SSM, D_SSM))],
        out_shape=[jax.ShapeDtypeStruct((S, D_SSM), f32), jax.ShapeDtypeStruct((S, D_SSM), f32),
                   jax.ShapeDtypeStruct((1, D_SSM), f32), jax.ShapeDtypeStruct((1, D_SSM), f32),
                   jax.ShapeDtypeStruct((D_SSM, D_SSM), f32)],
        compiler_params=_cp(("arbitrary",), 40),
    )(*_hbm(dout, y, u, dskip, glu_w, glu_b))


def _ssm_states_bwd(dy, du_skip, u, sr, si, cre, cim, bre, bim, a2, l):
    S = u.shape[0]
    N = D_STATE
    nst = S // SCAN_SEG
    nproj = S // _SSM_ROWS

    def body(dy_ref, dus_ref, u_ref, sr_ref, si_ref, cr_ref, ci_ref, br_ref, bi_ref, a2_ref,
             du_ref, dbr_ref, dbi_ref, da_ref, dcr_ref, dci_ref, lr_ref, li_ref):
        a_ref = a2_ref.at[l]
        crb = cr_ref[l].astype(bf16)
        cib = ci_ref[l].astype(bf16)
        dcr_ref[...] = jnp.zeros_like(dcr_ref)
        dci_ref[...] = jnp.zeros_like(dci_ref)

        def project(t, c):
            rows = pl.ds(pl.multiple_of(t * _SSM_ROWS, _SSM_ROWS), _SSM_ROWS)
            dyb = dy_ref[rows, :].astype(bf16)
            lr_ref[rows, :] = _dot_nt(dyb, crb)
            li_ref[rows, :] = -_dot_nt(dyb, cib)
            dcr_ref[...] += _dot_tn(sr_ref[rows, :].astype(bf16), dyb)
            dci_ref[...] -= _dot_tn(si_ref[rows, :].astype(bf16), dyb)
            return c

        lax.fori_loop(0, nproj, project, 0)
        _scan_in_place(lr_ref, li_ref, a_ref, True)

        row = lax.broadcasted_iota(jnp.int32, (SCAN_SEG, N), 0)
        last = pl.ds((nst - 1) * SCAN_SEG, SCAN_SEG)
        pr = jnp.where(row == 0, 0.0, pltpu.roll(sr_ref[last, :], 1, axis=0))
        pi = jnp.where(row == 0, 0.0, pltpu.roll(si_ref[last, :], 1, axis=0))
        first = pl.ds(0, SCAN_SEG)
        acc_r = lr_ref[first, :] * pr + li_ref[first, :] * pi
        acc_i = li_ref[first, :] * pr - lr_ref[first, :] * pi

        def step(t, c):
            acc_r, acc_i = c
            rows = pl.ds(pl.multiple_of(t * SCAN_SEG, SCAN_SEG), SCAN_SEG)
            prev = pl.ds(pl.multiple_of((t - 1) * SCAN_SEG, SCAN_SEG), SCAN_SEG)
            lrv, liv, srv, siv = lr_ref[rows, :], li_ref[rows, :], sr_ref[prev, :], si_ref[prev, :]
            return acc_r + lrv * srv + liv * siv, acc_i + liv * srv - lrv * siv

        acc_r, acc_i = lax.fori_loop(1, nst, step, (acc_r, acc_i))
        da_ref[0:1, :] = jnp.sum(acc_r, 0, keepdims=True)
        da_ref[1:2, :] = jnp.sum(acc_i, 0, keepdims=True)

        brb = br_ref[l].astype(bf16)
        bib = bi_ref[l].astype(bf16)
        dbr_ref[...] = jnp.zeros_like(dbr_ref)
        dbi_ref[...] = jnp.zeros_like(dbi_ref)

        def back(t, c):
            rows = pl.ds(pl.multiple_of(t * _SSM_ROWS, _SSM_ROWS), _SSM_ROWS)
            lrb = lr_ref[rows, :].astype(bf16)
            lib = li_ref[rows, :].astype(bf16)
            du_ref[rows, :] = dus_ref[rows, :] + _dot_nt(lrb, brb) + _dot_nt(lib, bib)
            ub = u_ref[rows, :].astype(bf16)
            dbr_ref[...] += _dot_tn(ub, lrb)
            dbi_ref[...] += _dot_tn(ub, lib)
            return c

        lax.fori_loop(0, nproj, back, 0)

    vm = pl.BlockSpec(memory_space=pltpu.VMEM)
    return pl.pallas_call(
        body, name="ssm_states_bwd", in_specs=[vm] * 10, out_specs=[vm] * 6,
        out_shape=[jax.ShapeDtypeStruct((S, D_SSM), f32), jax.ShapeDtypeStruct((D_SSM, D_STATE), f32),
                   jax.ShapeDtypeStruct((D_SSM, D_STATE), f32), jax.ShapeDtypeStruct((2, D_STATE), f32),
                   jax.ShapeDtypeStruct((D_STATE, D_SSM), f32), jax.ShapeDtypeStruct((D_STATE, D_SSM), f32)],
        scratch_shapes=[pltpu.VMEM((S, D_STATE), f32), pltpu.VMEM((S, D_STATE), f32)],
        compiler_params=_cp(None, 56),
    )(dy, du_skip, u, sr, si, cre, cim, bre, bim, a2)


_POOL_TILE = 256


def _window_sums(xt, back):
    n = xt.shape[0]
    out = []
    ws = xt
    for k in (1, 2, 4, 8):
        ws = ws + pltpu.roll(ws, k if back else n - k, axis=0)
        out.append(ws)
    return out


def _pool_count(r0, w):
    t = r0 + lax.broadcasted_iota(jnp.int32, (_POOL_TILE, POOL_GROUP), 0)
    return jnp.minimum(t + 1, w).astype(f32)


def _pool_fwd(u_pad, pool_w, pool_scale):
    S = u_pad.shape[0] - POOL_HALO
    nt = S // _POOL_TILE

    def body(u_ref, w_ref, sc_ref, y_ref):
        def tile(t, c):
            r0 = pl.multiple_of(t * _POOL_TILE, _POOL_TILE)
            for g, w in enumerate(POOL_WINDOWS):
                cs = pl.ds(POOL_GROUP * g, POOL_GROUP)
                xt = u_ref[pl.ds(r0, _POOL_TILE + POOL_HALO), cs]
                ws = _window_sums(xt, True)[g][POOL_HALO:, :]
                pooled = ws / _pool_count(r0, w) - xt[POOL_HALO:, :]
                y_ref[pl.ds(r0, _POOL_TILE), cs] = _dot(pooled.astype(bf16), w_ref[g].astype(bf16)) * sc_ref[:, cs]
            return c
        lax.fori_loop(0, nt, tile, 0)

    vm = pl.BlockSpec(memory_space=pltpu.VMEM)
    return pl.pallas_call(
        body, name="pool_fwd", in_specs=[vm, vm, vm], out_specs=vm,
        out_shape=jax.ShapeDtypeStruct((S, D_POOL), f32),
    )(u_pad, pool_w, pool_scale)


def _pool_bwd(dy_pad, u_pad, pool_w, pool_scale):
    S = u_pad.shape[0] - POOL_HALO
    nt = S // _POOL_TILE
    n = _POOL_TILE + POOL_HALO

    def body(dy_ref, u_ref, w_ref, sc_ref, du_ref, dw_ref, dsc_ref):
        dw_ref[...] = jnp.zeros_like(dw_ref)
        dsc_ref[...] = jnp.zeros_like(dsc_ref)

        def tile(t, c):
            r0 = pl.multiple_of(t * _POOL_TILE, _POOL_TILE)
            for g, w in enumerate(POOL_WINDOWS):
                cs = pl.ds(POOL_GROUP * g, POOL_GROUP)
                wb = w_ref[g].astype(bf16)
                xt = u_ref[pl.ds(r0, n), cs]
                pooled = (_window_sums(xt, True)[g][POOL_HALO:, :] / _pool_count(r0, w) - xt[POOL_HALO:, :]).astype(bf16)
                dy = dy_ref[pl.ds(r0, _POOL_TILE), cs]
                dsc_ref[:, cs] += jnp.sum(dy * _dot(pooled, wb), 0, keepdims=True)
                dw_ref[g] += _dot_tn(pooled, (dy * sc_ref[:, cs]).astype(bf16))
                dyh = (dy_ref[pl.ds(r0, n), cs] * sc_ref[:, cs]).astype(bf16)
                dpl = _dot_nt(dyh, wb)
                cnt = jnp.minimum(r0 + lax.broadcasted_iota(jnp.int32, (n, POOL_GROUP), 0) + 1, w).astype(f32)
                lead = _window_sums(dpl / cnt, False)[g]
                du_ref[pl.ds(r0, _POOL_TILE), cs] = lead[:_POOL_TILE, :] - dpl[:_POOL_TILE, :]
            return c
        lax.fori_loop(0, nt, tile, 0)

    vm = pl.BlockSpec(memory_space=pltpu.VMEM)
    return pl.pallas_call(
        body, name="pool_bwd", in_specs=[vm, vm, vm, vm], out_specs=[vm, vm, vm],
        out_shape=[jax.ShapeDtypeStruct((S, D_POOL), f32), jax.ShapeDtypeStruct((4, POOL_GROUP, POOL_GROUP), f32),
                   jax.ShapeDtypeStruct((1, D_POOL), f32)],
    )(dy_pad, u_pad, pool_w, pool_scale)


def _loss_head(y, target):
    S, D = y.shape
    ts = TOK_TILE

    def body(y_ref, t_ref, loss_ref, dy_ref):
        @pl.when(pl.program_id(0) == 0)
        def _():
            loss_ref[...] = jnp.zeros_like(loss_ref)

        d = y_ref[...] - t_ref[...]
        dy_ref[...] = d * (1.0 / D)
        loss_ref[...] += 0.5 * jnp.sum(jnp.sum(d * d, -1, keepdims=True) * (1.0 / D), 0, keepdims=True)

    tok = pl.BlockSpec((ts, D), lambda i: (i, 0))
    return pl.pallas_call(
        body, name="loss_head", grid=(S // ts,),
        in_specs=[tok, tok], out_specs=[_full((1, 1)), tok],
        out_shape=[jax.ShapeDtypeStruct((1, 1), f32), jax.ShapeDtypeStruct((S, D), f32)],
        compiler_params=_cp(("arbitrary",)),
    )(*_hbm(y, target))


_ADA_COLS = 768


def _ada_fwd(c_all, ada_w, ada_b_cols):
    L, D, N = ada_w.shape
    B = c_all.shape[0]

    def body(c_ref, w_ref, b_ref, out_ref):
        cv = c_ref[...]
        cond = (cv * jax.nn.sigmoid(cv)).astype(bf16)
        out_ref[0] = _dot(cond, w_ref[0].astype(bf16)) + b_ref[0]

    return pl.pallas_call(
        body, name="ada_fwd", grid=(L, N // _ADA_COLS),
        in_specs=[_full((B, D)), pl.BlockSpec((1, D, _ADA_COLS), lambda l, j: (l, 0, j)),
                  pl.BlockSpec((1, 1, _ADA_COLS), lambda l, j: (l, 0, j))],
        out_specs=pl.BlockSpec((1, B, _ADA_COLS), lambda l, j: (l, 0, j)),
        out_shape=jax.ShapeDtypeStruct((L, B, N), f32),
        compiler_params=_cp(("parallel", "parallel")),
    )(c_all, ada_w, ada_b_cols)


def _ada_wgrad(c_all_t, dmod_cols):
    D, B = c_all_t.shape
    L, _, N = dmod_cols.shape

    def body(ct_ref, dm_ref, out_ref):
        cv = ct_ref[...]
        cond = cv * jax.nn.sigmoid(cv)
        acc = cond[:, 0:1] * dm_ref[0, 0:1, :]
        for b in range(1, B):
            acc = acc + cond[:, b:b + 1] * dm_ref[0, b:b + 1, :]
        out_ref[0] = acc

    return pl.pallas_call(
        body, name="ada_wgrad", grid=(L, N // _ADA_COLS),
        in_specs=[_full((D, B)), pl.BlockSpec((1, B, _ADA_COLS), lambda l, j: (l, 0, j))],
        out_specs=pl.BlockSpec((1, D, _ADA_COLS), lambda l, j: (l, 0, j)),
        out_shape=jax.ShapeDtypeStruct((L, D, N), f32),
        compiler_params=_cp(("parallel", "parallel")),
    )(c_all_t, dmod_cols)


def _adam_math(w, g, m, v):
    m = ADAM_B1 * m + (1.0 - ADAM_B1) * g
    v = ADAM_B2 * v + (1.0 - ADAM_B2) * (g * g)
    m_hat = m / (1.0 - ADAM_B1 ** ADAM_STEP)
    v_hat = v / (1.0 - ADAM_B2 ** ADAM_STEP)
    delta = -ADAM_LR * (m_hat / (jnp.sqrt(v_hat) + ADAM_EPS) + ADAM_WD * w)
    return delta, m, v


def _adamw(w, m, v, g, row_tile, row0=0, outs=None):
    R, C = w.shape
    b0 = row0 // row_tile

    def body(w_ref, m_ref, v_ref, g_ref, _0, _1, _2, _3, g_out, d_out, m_out, v_out):
        gv = g_ref[...]
        delta, mn, vn = _adam_math(w_ref[...], gv, m_ref[...], v_ref[...])
        g_out[...] = gv
        d_out[...] = delta
        m_out[...] = mn
        v_out[...] = vn

    pspec = pl.BlockSpec((row_tile, C), lambda i: (b0 + i, 0))
    gspec = pl.BlockSpec((row_tile, C), lambda i: (i, 0))
    anyspec = pl.BlockSpec(memory_space=pl.ANY)
    shp = jax.ShapeDtypeStruct((R, C), f32)
    if outs is None:
        outs = [lax.empty((R, C), f32) for _ in range(4)]
    return pl.pallas_call(
        body, name="adamw", grid=(g.shape[0] // row_tile,),
        in_specs=[pspec] * 3 + [gspec] + [anyspec] * 4, out_specs=[pspec] * 4, out_shape=[shp] * 4,
        input_output_aliases={4: 0, 5: 1, 6: 2, 7: 3},
        compiler_params=_cp(("parallel",), 40),
    )(*_hbm(w, m, v, g, *outs))


def _pair_sum(g5s, gots, pc):
    n = len(g5s)

    def body(pc_ref, *refs):
        for own, got, out in zip(refs[:n], refs[n:2 * n], refs[2 * n:]):
            out[0, 0] = (own[0, 0, 0].astype(f32) + got[0, 0].astype(f32)).astype(bf16)

    def half(g):
        return pl.BlockSpec((1, 1) + g.shape[-2:], lambda p, pc: (p, 0, 0, 0))

    gs = pltpu.PrefetchScalarGridSpec(
        num_scalar_prefetch=1, grid=(N_CHIPS,),
        in_specs=[pl.BlockSpec((1, 1, 1) + g.shape[-2:], lambda p, pc: (p, 0, pc[1], 0, 0)) for g in g5s]
        + [half(g) for g in gots],
        out_specs=[half(g) for g in gots],
    )
    return pl.pallas_call(
        body, name="pair_sum", grid_spec=gs, out_shape=[jax.ShapeDtypeStruct(g.shape, bf16) for g in gots],
        compiler_params=_cp(("parallel",), 48),
    )(pc, *_hbm(*g5s, *gots))


_SUM_STEPS = 2


def _sum_shards(hsums, recvs, pc):
    n = len(hsums)

    def body(pc_ref, *refs):
        for own, got, out in zip(refs[:n], refs[n:2 * n], refs[2 * n:]):
            acc = own[0, 0].astype(f32)
            for j in range(3):
                acc = acc + got[j, 0].astype(f32)
            out[0, 0] = acc

    def rows(h):
        return (h.shape[2] // _SUM_STEPS, h.shape[3])

    gs = pltpu.PrefetchScalarGridSpec(
        num_scalar_prefetch=1, grid=(_SUM_STEPS,),
        in_specs=[pl.BlockSpec((1, 1) + rows(h), lambda i, pc: (pc[0], 0, i, 0)) for h in hsums]
        + [pl.BlockSpec((3, 1) + rows(h), lambda i, pc: (0, 0, i, 0)) for h in hsums],
        out_specs=[pl.BlockSpec((1, 1) + rows(h), lambda i, pc: (0, pc[1], i, 0)) for h in hsums],
    )
    return pl.pallas_call(
        body, name="sum_shards", grid_spec=gs,
        out_shape=[jax.ShapeDtypeStruct((1, 2) + h.shape[2:], f32) for h in hsums],
        compiler_params=_cp(("parallel",), 48),
    )(pc, *_hbm(*hsums, *recvs))


def _sum8(packs):
    _, R, C = packs.shape
    tr = R // 8 if R % 64 == 0 else R

    def body(p_ref, out_ref):
        acc = p_ref[0]
        for d in range(1, 8):
            acc = acc + p_ref[d]
        out_ref[...] = acc

    return pl.pallas_call(
        body, name="sum8", grid=(R // tr,),
        in_specs=[pl.BlockSpec((8, tr, C), lambda i: (0, i, 0))],
        out_specs=pl.BlockSpec((tr, C), lambda i: (i, 0)),
        out_shape=jax.ShapeDtypeStruct((R, C), f32),
        compiler_params=_cp(("parallel",)),
    )(packs)


def _allgather8(x_shard):
    m_per, n = x_shard.shape

    def body(x_ref, out_ref, send_sems, recv_sems, local_sem):
        x, y, c = lax.axis_index("x"), lax.axis_index("y"), lax.axis_index("c")
        me, sibling = (x, y, c), (x, y, 1 - c)
        chips = [(1 - x, y), (x, 1 - y), (1 - x, 1 - y)]

        def rows(px, py, pc):
            return out_ref.at[pl.ds((4 * px + 2 * py + pc) * m_per, m_per), :]

        def copy(k, block, to, src=None):
            return pltpu.make_async_remote_copy(
                src_ref=rows(*block) if src is None else src, dst_ref=rows(*block),
                send_sem=send_sems.at[k], recv_sem=recv_sems.at[k], device_id=to, device_id_type=MESH)

        mine = pltpu.make_async_copy(x_ref, rows(*me), local_sem)
        mine.start()
        first = [copy(0, me, sibling, src=x_ref)]
        first += [copy(1 + j, me, (*chip, c), src=x_ref) for j, chip in enumerate(chips)]
        for cp in first:
            cp.start()
        passed = [copy(4 + j, (*chip, c), sibling) for j, chip in enumerate(chips)]
        for j, chip in enumerate(chips):
            copy(1 + j, (*chip, c), me).wait_recv()
            passed[j].start()
        copy(0, sibling, me).wait_recv()
        for j, chip in enumerate(chips):
            copy(4 + j, (*chip, 1 - c), me).wait_recv()
        for cp in first + passed:
            cp.wait_send()
        mine.wait()

    return pl.pallas_call(
        body, name="allgather8",
        out_shape=jax.ShapeDtypeStruct((8 * m_per, n), x_shard.dtype),
        in_specs=[pl.BlockSpec(memory_space=pltpu.VMEM)],
        out_specs=pl.BlockSpec(memory_space=pltpu.VMEM),
        scratch_shapes=[pltpu.SemaphoreType.DMA((7,)), pltpu.SemaphoreType.DMA((7,)), pltpu.SemaphoreType.DMA],
        compiler_params=_cp(None, 48),
    )(x_shard)


def _other_chips():
    x, y = lax.axis_index("x"), lax.axis_index("y")
    return [(1 - x, y), (x, 1 - y), (1 - x, 1 - y)]


_HBM = pl.BlockSpec(memory_space=pltpu.HBM)
_SEM = pl.BlockSpec(memory_space=pltpu.SEMAPHORE)
_EFFECT = pltpu.SideEffectType.DATAFLOW_SIDE_EFFECTING


def _gather_copies(srcs, lands, send_sems, recv_sems):
    x, y, c = lax.axis_index("x"), lax.axis_index("y"), lax.axis_index("c")
    return [pltpu.make_async_remote_copy(
        src_ref=srcs[a].at[:, c], dst_ref=lands[a].at[2 * x + y, :, c], send_sem=send_sems.at[3 * a + j],
        recv_sem=recv_sems.at[3 * a + j], device_id=(cx, cy, c), device_id_type=MESH)
        for a in range(len(srcs)) for j, (cx, cy) in enumerate(_other_chips())]


def _gather_start(chunks, after, name):
    sizes = [len(srcs) for srcs, _ in chunks]
    flat = [t for srcs, lands in chunks for t in list(srcs) + list(lands)]
    nflat = len(flat)
    nsem = 2 * len(chunks)

    def body(*refs):
        ins, sems, token = refs[:nflat], refs[nflat + 1:nflat + 1 + nsem], refs[-1]
        off = 0
        for k, n in enumerate(sizes):
            for cp in _gather_copies(ins[off:off + n], ins[off + n:off + 2 * n], sems[2 * k], sems[2 * k + 1]):
                cp.start()
            off += 2 * n
        token[...] = jnp.zeros_like(token)

    res = pl.pallas_call(
        body, name=name,
        out_shape=[pltpu.SemaphoreType.DMA((3 * n,)) for n in sizes for _ in range(2)]
        + [pltpu.HBM(t.shape, t.dtype) for t in flat] + [jax.ShapeDtypeStruct((8, 128), f32)],
        in_specs=[_HBM] * nflat + [pl.BlockSpec(memory_space=pl.ANY)],
        out_specs=[_SEM] * nsem + [_HBM] * nflat + [pl.BlockSpec(memory_space=pltpu.VMEM)],
        input_output_aliases={i: nsem + i for i in range(nflat)},
        compiler_params=pltpu.CompilerParams(has_side_effects=_EFFECT),
    )(*[pltpu.with_memory_space_constraint(t, pltpu.HBM) for t in flat], after)
    out, off = [], nsem
    for k, n in enumerate(sizes):
        out.append((res[2 * k], res[2 * k + 1], res[off:off + n], res[off + n:off + 2 * n]))
        off += 2 * n
    return out, res[-1]


def _gather_wait(send_sems, recv_sems, srcs, lands, after, name):
    n = len(srcs)

    def body(*refs):
        for cp in _gather_copies(refs[:n], refs[n:2 * n], refs[2 * n], refs[2 * n + 1]):
            cp.wait_send()
            cp.wait_recv()

    res = pl.pallas_call(
        body, name=name,
        out_shape=[pltpu.HBM(t.shape, t.dtype) for t in list(srcs) + list(lands)],
        in_specs=[_HBM] * (2 * n) + [_SEM, _SEM] + [pl.BlockSpec(memory_space=pl.ANY)] * len(after),
        out_specs=[_HBM] * (2 * n),
        input_output_aliases={i: i for i in range(2 * n)},
        compiler_params=pltpu.CompilerParams(has_side_effects=_EFFECT),
    )(*srcs, *lands, send_sems, recv_sems, *after)
    return res[n:]


def _split_start(make_copies, arrays, nsem, name, after=()):
    n, na = len(arrays), len(after)

    def body(*refs):
        for cp in make_copies(refs[:n], refs[n + na], refs[n + na + 1]):
            cp.start()
        refs[-1][...] = jnp.zeros_like(refs[-1])

    res = pl.pallas_call(
        body, name=name,
        out_shape=[pltpu.SemaphoreType.DMA((nsem,)), pltpu.SemaphoreType.DMA((nsem,))]
        + [pltpu.HBM(t.shape, t.dtype) for t in arrays] + [jax.ShapeDtypeStruct((8, 128), f32)],
        in_specs=[_HBM] * n + [pl.BlockSpec(memory_space=pl.ANY)] * na,
        out_specs=[_SEM, _SEM] + [_HBM] * n + [pl.BlockSpec(memory_space=pltpu.VMEM)],
        input_output_aliases={i: i + 2 for i in range(n)},
        compiler_params=pltpu.CompilerParams(has_side_effects=_EFFECT),
    )(*[pltpu.with_memory_space_constraint(t, pltpu.HBM) for t in arrays], *after)
    return (res[0], res[1], res[2:2 + n]), res[-1]


def _split_wait(make_copies, send_sems, recv_sems, arrays, after, name):
    n = len(arrays)

    def body(*refs):
        for cp in make_copies(refs[:n], refs[n], refs[n + 1]):
            cp.wait_send()
            cp.wait_recv()

    return pl.pallas_call(
        body, name=name,
        out_shape=[pltpu.HBM(t.shape, t.dtype) for t in arrays],
        in_specs=[_HBM] * n + [_SEM, _SEM] + [pl.BlockSpec(memory_space=pl.ANY)] * len(after),
        out_specs=[_HBM] * n, input_output_aliases={i: i for i in range(n)},
        compiler_params=pltpu.CompilerParams(has_side_effects=_EFFECT),
    )(*arrays, send_sems, recv_sems, *after)


def _sibling():
    return lax.axis_index("x"), lax.axis_index("y"), 1 - lax.axis_index("c")


def _forward_copies(lands, send_sems, recv_sems):
    c = lax.axis_index("c")
    return [pltpu.make_async_remote_copy(
        src_ref=lands[a].at[2 * cx + cy, :, c], dst_ref=lands[a].at[2 * cx + cy, :, c], send_sem=send_sems.at[3 * a + j],
        recv_sem=recv_sems.at[3 * a + j], device_id=_sibling(), device_id_type=MESH)
        for a in range(len(lands)) for j, (cx, cy) in enumerate(_other_chips())]


def _swap_copies(fulls, send_sems, recv_sems):
    c = lax.axis_index("c")
    return [pltpu.make_async_remote_copy(src_ref=t.at[:, c], dst_ref=t.at[:, c], send_sem=send_sems.at[a],
                                         recv_sem=recv_sems.at[a], device_id=_sibling(), device_id_type=MESH)
            for a, t in enumerate(fulls)]


def _pair_copies(refs, send_sems, recv_sems):
    n = len(refs) // 2
    c = lax.axis_index("c")
    return [pltpu.make_async_remote_copy(src_ref=refs[a].at[:, :, 1 - c], dst_ref=refs[n + a], send_sem=send_sems.at[a],
                                         recv_sem=recv_sems.at[a], device_id=_sibling(), device_id_type=MESH)
            for a in range(n)]


def _scatter_copies(srcs, lands, send_sems, recv_sems):
    c = lax.axis_index("c")
    return [pltpu.make_async_remote_copy(
        src_ref=srcs[a].at[2 * cx + cy], dst_ref=lands[a].at[j], send_sem=send_sems.at[3 * a + j],
        recv_sem=recv_sems.at[3 * a + j], device_id=(cx, cy, c), device_id_type=MESH)
        for a in range(len(srcs)) for j, (cx, cy) in enumerate(_other_chips())]


def _scatter_start(hsums, name, after=()):
    n = len(hsums)
    na = len(after)

    def body(*refs):
        srcs, lands = refs[:n], refs[n:2 * n]
        send_sems, recv_sems = refs[2 * n + na], refs[2 * n + na + 1]
        for cp in _scatter_copies(srcs, lands, send_sems, recv_sems):
            cp.start()
        refs[-1][...] = jnp.zeros_like(refs[-1])

    lands = [lax.empty((3,) + g.shape[1:], g.dtype) for g in hsums]
    res = pl.pallas_call(
        body, name=name,
        out_shape=[pltpu.SemaphoreType.DMA((3 * n,)), pltpu.SemaphoreType.DMA((3 * n,))]
        + [pltpu.HBM(g.shape, g.dtype) for g in hsums] + [pltpu.HBM(g.shape, g.dtype) for g in lands]
        + [jax.ShapeDtypeStruct((8, 128), f32)],
        in_specs=[_HBM] * (2 * n) + [pl.BlockSpec(memory_space=pl.ANY)] * na,
        out_specs=[_SEM, _SEM] + [_HBM] * (2 * n) + [pl.BlockSpec(memory_space=pltpu.VMEM)],
        input_output_aliases={i: i + 2 for i in range(2 * n)},
        compiler_params=pltpu.CompilerParams(has_side_effects=_EFFECT),
    )(*[pltpu.with_memory_space_constraint(t, pltpu.HBM) for t in list(hsums) + lands], *after)
    return (res[0], res[1], res[2:2 + n], res[2 + n:2 + 2 * n]), res[-1]


def _scatter_wait(send_sems, recv_sems, srcs, lands, after, name):
    n = len(srcs)
    extra = list(after)

    def body(*refs):
        s_refs, l_refs = refs[:n], refs[n:2 * n]
        ss, rs = refs[2 * n], refs[2 * n + 1]
        for cp in _scatter_copies(s_refs, l_refs, ss, rs):
            cp.wait_send()
            cp.wait_recv()

    res = pl.pallas_call(
        body, name=name,
        out_shape=[pltpu.HBM(g.shape, g.dtype) for g in srcs] + [pltpu.HBM(g.shape, g.dtype) for g in lands],
        in_specs=[_HBM] * (2 * n) + [_SEM, _SEM] + [pl.BlockSpec(memory_space=pl.ANY)] * len(extra),
        out_specs=[_HBM] * (2 * n),
        input_output_aliases={i: i for i in range(2 * n)},
        compiler_params=pltpu.CompilerParams(has_side_effects=_EFFECT),
    )(*srcs, *lands, send_sems, recv_sems, *extra)
    return res[:n], res[n:]


def _plane_copies(src, land, send_sems, recv_sems):
    x, y, c = lax.axis_index("x"), lax.axis_index("y"), lax.axis_index("c")
    return [pltpu.make_async_remote_copy(src_ref=src, dst_ref=land.at[2 * x + y, c], send_sem=send_sems.at[j],
                                         recv_sem=recv_sems.at[j], device_id=(cx, cy, c), device_id_type=MESH)
            for j, (cx, cy) in enumerate(_other_chips())]


def _plane_start(pack, land, name):
    def body(src, lnd, send_sems, recv_sems, _s, _l, token):
        for cp in _plane_copies(src, lnd, send_sems, recv_sems):
            cp.start()
        token[...] = jnp.zeros_like(token)

    res = pl.pallas_call(
        body, name=name,
        out_shape=[pltpu.SemaphoreType.DMA((3,)), pltpu.SemaphoreType.DMA((3,)), pltpu.HBM(pack.shape, pack.dtype),
                   pltpu.HBM(land.shape, land.dtype), jax.ShapeDtypeStruct((8, 128), f32)],
        in_specs=[_HBM, _HBM], out_specs=[_SEM, _SEM, _HBM, _HBM, pl.BlockSpec(memory_space=pltpu.VMEM)],
        input_output_aliases={0: 2, 1: 3},
        compiler_params=pltpu.CompilerParams(has_side_effects=_EFFECT),
    )(pltpu.with_memory_space_constraint(pack, pltpu.HBM), pltpu.with_memory_space_constraint(land, pltpu.HBM))
    return res[:4], res[4]


def _plane_wait(send_sems, recv_sems, pack, land, after, name):
    def body(src, lnd, ss, rs, *_):
        for cp in _plane_copies(src, lnd, ss, rs):
            cp.wait_send()
            cp.wait_recv()

    return pl.pallas_call(
        body, name=name,
        out_shape=[pltpu.HBM(pack.shape, pack.dtype), pltpu.HBM(land.shape, land.dtype)],
        in_specs=[_HBM, _HBM, _SEM, _SEM] + [pl.BlockSpec(memory_space=pl.ANY)] * len(after),
        out_specs=[_HBM, _HBM], input_output_aliases={0: 0, 1: 1},
        compiler_params=pltpu.CompilerParams(has_side_effects=_EFFECT),
    )(pack, land, send_sems, recv_sems, *after)[1]


def _swap_halves(fulls):
    n = len(fulls)

    def body(*refs):
        ins, outs = refs[:n], refs[n:2 * n]
        send_sems, recv_sems = refs[2 * n:]
        c = lax.axis_index("c")
        sibling = (lax.axis_index("x"), lax.axis_index("y"), 1 - c)
        copies = []
        for a in range(n):
            cp = pltpu.make_async_remote_copy(src_ref=outs[a].at[:, c], dst_ref=outs[a].at[:, c], send_sem=send_sems.at[a],
                                              recv_sem=recv_sems.at[a], device_id=sibling, device_id_type=MESH)
            cp.start()
            copies.append(cp)
        for a, cp in enumerate(copies):
            cp.wait_send()
            theirs = outs[a].at[:, 1 - c]
            pltpu.make_async_remote_copy(src_ref=theirs, dst_ref=theirs, send_sem=send_sems.at[a], recv_sem=recv_sems.at[a],
                                         device_id=sibling, device_id_type=MESH).wait_recv()

    hbm = pl.BlockSpec(memory_space=pl.ANY)
    return pl.pallas_call(
        body, name="swap_halves",
        out_shape=[jax.ShapeDtypeStruct(p.shape, p.dtype) for p in fulls],
        in_specs=[hbm] * n, out_specs=[hbm] * n,
        input_output_aliases={a: a for a in range(n)},
        scratch_shapes=[pltpu.SemaphoreType.DMA((n,)), pltpu.SemaphoreType.DMA((n,))],
    )(*fulls)


def _to_segments(t):
    s, c = t.shape
    return t.reshape(SCAN_SEG, s // SCAN_SEG, c).transpose(1, 0, 2).reshape(s, c)


def _from_segments(t):
    s, c = t.shape
    return t.reshape(s // SCAN_SEG, SCAN_SEG, c).transpose(1, 0, 2).reshape(s, c)


def _ssm_operators(a_re, a_im, log_dt, b_re, b_im, c_re, c_im):
    lam = lax.complex(a_re, a_im)
    dt = jnp.exp(log_dt)[:, None]
    a_bar = jnp.exp(lam * dt)
    b_bar = ((a_bar - 1.0) / lam)[:, :, None] * lax.complex(b_re, b_im)
    eye = jnp.eye(N_GROUPS, dtype=f32)

    def embed_b(t):
        return (jnp.transpose(t, (0, 2, 1))[:, :, None, :] * eye[:, None, :, None]).reshape(D_SSM, D_STATE)

    def embed_c(t):
        return (jnp.transpose(t, (0, 2, 1))[:, :, None, :] * eye[:, None, :, None]).reshape(D_STATE, D_SSM)

    a2 = jnp.stack([a_bar.real.reshape(D_STATE), a_bar.imag.reshape(D_STATE)])
    return a2, embed_b(b_bar.real), embed_b(b_bar.imag), embed_c(c_re), embed_c(c_im)


def _local_step(x, target, mod, small, ffn_weights, mix_weights, grads_done, ffn_bwd_issued):
    table = jnp.asarray(_bucket_table())
    bias = small["att_bias"]
    L = DEPTH
    saved = []
    ssm_names = ("ssm_a_re", "ssm_a_im", "ssm_log_dt", "ssm_b_re", "ssm_b_im", "ssm_c_re", "ssm_c_im")
    ssm_ops_vjp = []
    for l in range(L):
        sv = {}
        m9 = mod[l]
        sv["x0"] = x
        sv["w0"] = ffn_weights(l, 0, x)
        x, sv["f0"], sv["g0"], sv["u0"], sv["a0"], sv["h0"] = _ffn_fwd(x, m9[0:3], *sv["w0"], 0, small["ln_g"][l, 0:1], small["ln_b"][l, 0:1])
        sv["x1"] = x
        sv["w1"] = mix_weights(l, x)
        *qkv, z_rest, sv["h1"] = _mix_in_fwd(x, m9[3:6], sv["w1"][0], 0)
        S = x.shape[0]
        qkv = [t.reshape(3, S, D_ATT) for t in qkv]
        att = [_att_fwd(qkv[b], bias, b) for b in range(3)]
        y_att, lse3 = _att_merge([att[b][0].reshape(d, S // d, D_ATT) for b, d in enumerate(DILATIONS)],
                                 [att[b][1].reshape(d, S // d, _LANES) for b, d in enumerate(DILATIONS)])
        sv.update(qkv=qkv, lse=[a[1] for a in att], lse3=lse3, y_att=y_att)

        ops, ops_vjp = jax.vjp(_ssm_operators, *[small[k][l] for k in ssm_names])
        ssm_ops_vjp.append(ops_vjp)
        a2, bre, bim, cre, cim = [t[None] for t in ops]
        u_ssm = _to_segments(z_rest[:, :D_SSM])
        dskip = small["ssm_d"][l][None, :]
        glu_b = small["glu_b"][l][None, :]
        sr, si, out_seg, y_seg = _ssm_fwd(u_ssm, bre, bim, a2, cre, cim, 0, dskip, small["glu_w"][l], glu_b)
        y_ssm = _from_segments(out_seg)
        sv.update(ssm_ops=(a2, bre, bim, cre, cim), u_ssm=u_ssm, sr=sr, si=si, y_seg=y_seg, y_ssm=y_ssm)

        u_pool = jnp.concatenate([jnp.zeros((POOL_HALO, D_POOL), f32), z_rest[:, D_SSM:]])
        y_pool = _pool_fwd(u_pool, small["pool_w"][l], small["pool_scale"][l][None, :])
        sv.update(u_pool=u_pool, y_pool=y_pool)

        x, sv["ymix"] = _mix_out_fwd(x, y_att, y_ssm, y_pool, m9[3:6], sv["w1"][1], 0, small["ln_g"][l, 1:2], small["ln_b"][l, 1:2])
        sv["x2"] = x
        sv["w2"] = ffn_weights(l, 1, x)
        x, sv["f2"], sv["g2"], sv["u2"], sv["a2"], sv["h2"] = _ffn_fwd(x, m9[6:9], *sv["w2"], 0, small["ln_g"][l, 2:3], small["ln_b"][l, 2:3])
        saved.append(sv)

    loss, dx = _loss_head(x, target)

    dmod = [None] * L
    dln_g = [None] * L
    dln_b = [None] * L
    sg = {k: [None] * L for k in ssm_names + ("ssm_d", "glu_w", "glu_b", "pool_w", "pool_scale")}
    dbias_tot = None
    order_after = jnp.zeros((), f32)
    for l in reversed(range(L)):
        sv = saved[l]
        m9 = mod[l] + order_after

        def fresh(like):
            return [lax.empty(t.shape, bf16) for t in like]

        dx, dg, du, df, dm2, dlg2, dlb2 = _ffn_bwd(dx, sv["x2"], sv["f2"], sv["g2"], sv["u2"], m9[6:9], *sv["w2"], 0,
                                                     small["ln_g"][l, 2:3])
        m9 = m9 + ffn_bwd_issued(l, 1, dx)
        g_ffn1 = _ffn_wgrad(sv["h2"], dg, du, sv["a2"], df,*fresh(sv["w2"]), 0)
        dxr, d_att, d_ssm, d_pool, dgate1, dlg1, dlb1, g_w_out = _mix_out_bwd(
            dx, sv["x1"], sv["ymix"], sv["y_att"], sv["y_ssm"], sv["y_pool"], m9[3:6], sv["w1"][1], 0, small["ln_g"][l, 1:2],
            fresh(sv["w1"])[1])
        S = d_att.shape[0]
        merged = _att_merge_bwd(d_att, sv["y_att"], sv["lse3"])
        dqkv, dbias = [], []
        for b, d in enumerate(DILATIONS):
            dq_b, db_b = _att_bwd(sv["qkv"][b], merged[b].reshape(S, D_ATT), sv["lse"][b], merged[3 + b].reshape(S, _LANES), bias, b)
            dqkv.append(dq_b.reshape(3, d, S // d, D_ATT))
            dbias.append(db_b)
        dbias = jnp.stack(dbias)
        dbias_tot = dbias if dbias_tot is None else dbias_tot + dbias
        d_seg = _to_segments(d_ssm)
        dskip = small["ssm_d"][l][None, :]
        glu_b = small["glu_b"][l][None, :]
        dy_seg, du_skip, dd, dglu_b, dglu_w = _ssm_out_bwd(d_seg, sv["y_seg"], sv["u_ssm"], dskip, small["glu_w"][l], glu_b)
        a2, bre, bim, cre, cim = sv["ssm_ops"]
        du_seg, dbre, dbim, da2, dcre, dcim = _ssm_states_bwd(dy_seg, du_skip, sv["u_ssm"], sv["sr"], sv["si"], cre, cim,
                                                              bre, bim, a2, 0)
        for k, t in zip(ssm_names, ssm_ops_vjp[l]((da2, dbre, dbim, dcre, dcim))):
            sg[k][l] = t
        sg["ssm_d"][l] = dd[0]
        sg["glu_b"][l] = dglu_b[0]
        sg["glu_w"][l] = dglu_w
        du_ssm = _from_segments(du_seg)
        dyp = jnp.concatenate([d_pool, jnp.zeros((POOL_HALO, D_POOL), f32)])
        du_pool, dpw, dps = _pool_bwd(dyp, sv["u_pool"], small["pool_w"][l], small["pool_scale"][l][None, :])
        sg["pool_w"][l] = dpw
        sg["pool_scale"][l] = dps[0]
        d_rest = jnp.concatenate([du_ssm, du_pool], axis=1).astype(bf16)
        dx, dm1, dz = _mix_in_bwd(dqkv, d_rest, dxr, sv["x1"], m9[3:6], sv["w1"][0], 0)
        g_w_in = _mix_in_wgrad(sv["h1"], dz, fresh(sv["w1"])[0], 0)
        ffn_names = ("ffn_w_gate", "ffn_w_up", "ffn_w_down")
        m9 = m9 + grads_done(l, 1, list(zip(ffn_names, [(2 * l + 1) * FF_SHARD] * 3, g_ffn1))
                             + [("w_in", l * D_MODEL, g_w_in), ("w_out", l * 256, g_w_out)])
        dm1 = jnp.concatenate([dm1[0:2], dgate1])
        dx, dg, du, df, dm0, dlg0, dlb0 = _ffn_bwd(dx, sv["x0"], sv["f0"], sv["g0"], sv["u0"], m9[0:3], *sv["w0"], 0,
                                                     small["ln_g"][l, 0:1])
        issued = ffn_bwd_issued(l, 0, dx)
        g_ffn0 = _ffn_wgrad(sv["h0"], dg, du, sv["a0"], df,*fresh(sv["w0"]), 0)
        order_after = grads_done(l, 0, list(zip(ffn_names, [2 * l * FF_SHARD] * 3, g_ffn0))) + issued
        dmod[l] = jnp.concatenate([dm0 + issued, dm1, dm2])
        dln_g[l] = jnp.concatenate([dlg0, dlg1, dlg2])
        dln_b[l] = jnp.concatenate([dlb0, dlb1, dlb2])

    small_grads = {k: jnp.stack(v) for k, v in sg.items()}
    small_grads["rel_bias"] = _bias_bwd(dbias_tot, table)
    small_grads["ln_g"] = jnp.stack(dln_g)
    small_grads["ln_b"] = jnp.stack(dln_b)
    return loss, dx, jnp.stack(dmod), small_grads


_TILE_ELEMS = 8 * 128


def _pack_rows(shapes):
    out, row = [], 0
    for s in shapes:
        nr = -(-int(np.prod(s)) // _TILE_ELEMS) * 8
        out.append((row, nr))
        row += nr
    return out


def _pack(arrs):
    parts = []
    for a in arrs:
        flat = a.reshape(-1).astype(f32)
        npad = -(-flat.shape[0] // _TILE_ELEMS) * _TILE_ELEMS
        parts.append(jnp.pad(flat, (0, npad - flat.shape[0])).reshape(npad // 128, 128))
    return jnp.concatenate(parts, axis=0)


def _unpack(buf, shapes):
    return [buf[row:row + nr].reshape(-1)[:int(np.prod(s))].reshape(s) for s, (row, nr) in zip(shapes, _pack_rows(shapes))]


_REPL = ("rel_bias", "ada_b", "ssm_a_re", "ssm_a_im", "ssm_log_dt", "ssm_b_re", "ssm_b_im", "ssm_c_re", "ssm_c_im",
         "ssm_d", "glu_b", "pool_w", "pool_scale")
_SMALL_SHARDED = ("ln_g", "ln_b", "glu_w")
_BIG = ("ffn_w_gate", "ffn_w_up", "ffn_w_down", "w_in", "w_out")
_ORDER = ("rel_bias", "ada_w", "ada_b", "ln_g", "ln_b", "ffn_w_gate", "ffn_w_up", "ffn_w_down", "w_in", "w_out",
          "ssm_a_re", "ssm_a_im", "ssm_log_dt", "ssm_b_re", "ssm_b_im", "ssm_c_re", "ssm_c_im", "ssm_d", "glu_w",
          "glu_b", "pool_w", "pool_scale")


def kernel(x, c, rel_bias, ada_w, ada_b, ln_g, ln_b, ffn_w_gate, ffn_w_up, ffn_w_down, w_in, w_out, ssm_a_re, ssm_a_im, ssm_log_dt, ssm_b_re, ssm_b_im, ssm_c_re, ssm_c_im, ssm_d, glu_w, glu_b, pool_w, pool_scale, loss_target, m_rel_bias, m_ada_w, m_ada_b, m_ln_g, m_ln_b, m_ffn_w_gate, m_ffn_w_up, m_ffn_w_down, m_w_in, m_w_out, m_ssm_a_re, m_ssm_a_im, m_ssm_log_dt, m_ssm_b_re, m_ssm_b_im, m_ssm_c_re, m_ssm_c_im, m_ssm_d, m_glu_w, m_glu_b, m_pool_w, m_pool_scale, v_rel_bias, v_ada_w, v_ada_b, v_ln_g, v_ln_b, v_ffn_w_gate, v_ffn_w_up, v_ffn_w_down, v_w_in, v_w_out, v_ssm_a_re, v_ssm_a_im, v_ssm_log_dt, v_ssm_b_re, v_ssm_b_im, v_ssm_c_re, v_ssm_c_im, v_ssm_d, v_glu_w, v_glu_b, v_pool_w, v_pool_scale):
    args = dict(locals())
    w = {k: args[k] for k in _ORDER}
    m = {k: args["m_" + k] for k in _ORDER}
    v = {k: args["v_" + k] for k in _ORDER}
    L, D = DEPTH, D_MODEL
    ax, ay, ac = lax.axis_index("x"), lax.axis_index("y"), lax.axis_index("c")
    p_me = 2 * ax + ay
    dev = 4 * ax + 2 * ay + ac

    transposed = ("ffn_w_gate", "ffn_w_up")
    for d in (w, m, v):
        for name in transposed:
            d[name] = jnp.swapaxes(d[name], 2, 3)

    def halves(t):
        return t.astype(bf16).reshape(1, 2, t.shape[0] // 2, t.shape[1])

    def landing(src):
        return lax.dynamic_update_slice(lax.empty((N_CHIPS,) + src.shape, bf16), src[None], (p_me, 0, 0, 0, 0))

    chunk_keys = [("ffn", 0, 0), ("mix", 0), ("ffn", 0, 1), ("ffn", 1, 0), ("mix", 1), ("ffn", 1, 1)]
    chunk_srcs = []
    for key in chunk_keys:
        if key[0] == "ffn":
            chunk_srcs.append([halves(w[name][key[1], key[2]]) for name in ("ffn_w_gate", "ffn_w_up", "ffn_w_down")])
        else:
            chunk_srcs.append([halves(w_in[key[1]]), halves(w_out[key[1]])])

    pack = _pack([c, ln_g, ln_b, glu_w])
    rows = pack.shape[0]
    allp = _allgather8(pack).reshape(8, rows, 128)
    chunks = [(srcs, [landing(t) for t in srcs]) for srcs in chunk_srcs]
    first_in_flight, first_begun = _gather_start(chunks[:1], allp, "gather_start_first")
    c_all = allp[:, :8].reshape(8, D) + first_begun[0, 0]
    by_chip = allp[0::2]

    fwd_rows = _pack_rows([c.shape, ln_g.shape, ln_b.shape, glu_w.shape])

    def sharded(part, shape, axis):
        row0, nrows = fwd_rows[part]
        t = by_chip[:, row0:row0 + nrows].reshape(N_CHIPS, -1)[:, :int(np.prod(shape))].reshape((N_CHIPS,) + shape)
        return jnp.concatenate([t[p] for p in range(N_CHIPS)], axis=axis)

    ln_g_full = sharded(1, ln_g.shape, 2)
    ln_b_full = sharded(2, ln_b.shape, 2)
    glu_w_full = sharded(3, glu_w.shape, 1)

    ncol = ada_w.shape[-1]
    ada_b_cols = lax.dynamic_slice_in_dim(ada_b, p_me * ncol, ncol, axis=1)[:, None, :]
    mod_part = _ada_fwd(c_all, ada_w, ada_b_cols)
    mrows = L * 8 * ncol // 128
    mod_pack = mod_part.reshape(mrows, 128)
    mod_land = lax.dynamic_update_slice(lax.empty((N_CHIPS, 2, mrows, 128), f32), mod_pack[None, None], (p_me, ac, 0, 0))
    mod_in_flight, _ = _plane_start(mod_pack, mod_land, "mod_start")
    att_bias = _bias_fwd(rel_bias, jnp.asarray(_bucket_table()))
    small_names = _REPL + _SMALL_SHARDED
    small_packs = [_pack([d[k] for k in small_names]) for d in (w, m, v)]
    mod_land = _plane_wait(*mod_in_flight, [att_bias] + small_packs + [t for _, lands in chunks[1:] for t in lands], "mod_wait")
    mod_all = _swap_halves([mod_land])[0].reshape(8, L, 8, ncol)
    mod_mine = lax.dynamic_index_in_dim(mod_all, dev, axis=2, keepdims=False)
    mod = jnp.concatenate([mod_mine[2 * p] for p in range(N_CHIPS)], axis=-1).reshape(L, 9, D)

    rest_in_flight, rest_begun = _gather_start(chunks[1:], mod, "gather_start_rest")
    in_flight = first_in_flight + rest_in_flight

    forwarding = {}

    def forward(k, after):
        lands = _gather_wait(*in_flight[k], [after, rest_begun], "gather_wait_%d" % k)
        forwarding[k], begun = _split_start(_forward_copies, lands, 3 * len(lands), "gather_forward_start_%d" % k)
        return begun

    def gathered(key, after):
        k = chunk_keys.index(key)
        order = [after]
        if k not in forwarding:
            order.append(forward(k, after))
        if 3 <= k + 1 < len(chunk_keys):
            order.append(forward(k + 1, after))
        lands = _split_wait(_forward_copies, *forwarding[k], order, "gather_forward_wait_%d" % k)
        return [t.reshape(N_CHIPS, 1, 2 * t.shape[3], t.shape[4]) for t in lands]

    pc = jnp.stack([p_me, ac]).astype(jnp.int32)
    groups = {}
    scattering = {}

    pairing = {}

    def start_pairs(tag, after=()):
        g5 = [g.reshape(g.shape[:2] + (2, g.shape[2] // 2, g.shape[3])) for _, _, g in groups[tag]]
        gots = [lax.empty(g.shape[:2] + g.shape[3:], bf16) for g in g5]
        pairing[tag], begun = _split_start(_pair_copies, g5 + gots, len(g5), "pair_exchange_start_%s" % tag, after)
        return begun

    def start_group(tag, after):
        arrays = _split_wait(_pair_copies, *pairing[tag], after, "pair_exchange_wait_%s" % tag)
        n = len(arrays) // 2
        hsum = _pair_sum(arrays[:n], arrays[n:], pc)
        scattering[tag], begun = _scatter_start(hsum, "scatter_start_%s" % tag)
        return begun

    def grads_done(l, s, grads):
        if l == 1:
            groups.setdefault("l1", []).extend(grads)
            return start_pairs("l1")[0, 0] if s == 0 else jnp.zeros((), f32)
        groups["l0a" if s == 1 else "l0b"] = grads
        return start_pairs("l0a")[0, 0] if s == 1 else jnp.zeros((), f32)

    swapping = {}

    def reduce_group(tag, after):
        hsum, recv = _scatter_wait(*scattering[tag], after, "scatter_wait_%s" % tag)
        full = _sum_shards(hsum, recv, pc)
        swapping[tag], begun = _split_start(_swap_copies, full, len(full), "swap_halves_start_%s" % tag)
        return [begun]

    def ffn_bwd_issued(l, s, dx):
        if l == 1:
            return jnp.zeros((), f32)
        return start_group("l1" if s == 1 else "l0a", [dx])[0, 0]

    small = {k: w[k] for k in _REPL if k != "ada_b"}
    small.update(ln_g=ln_g_full, ln_b=ln_b_full, glu_w=glu_w_full, att_bias=att_bias)
    loss_dev, grad_x, dmod, sgrads = _local_step(
        x[0], loss_target[0], mod, small, lambda l, s, after: gathered(("ffn", l, s), after),
        lambda l, after: gathered(("mix", l), after), grads_done, ffn_bwd_issued)
    loss = lax.psum(loss_dev[0, 0], ("x", "y", "c"))

    names = ("rel_bias", "ln_g", "ln_b", "ssm_a_re", "ssm_a_im", "ssm_log_dt", "ssm_b_re", "ssm_b_im", "ssm_c_re",
             "ssm_c_im", "ssm_d", "glu_w", "glu_b", "pool_w", "pool_scale")
    gpack = _pack([dmod] + [sgrads[k] for k in names])
    grows = gpack.shape[0]
    land = lax.dynamic_update_slice(lax.empty((N_CHIPS, 2, grows, 128), f32), gpack[None, None], (p_me, ac, 0, 0))
    small_in_flight, small_begun = _plane_start(gpack, land, "small_grads_start")
    l0b_begun = start_group("l0b", [start_pairs("l0b", (small_begun,))])

    out_g, out_d, out_m, out_v = {}, {}, {}, {}
    row_tile = dict(zip(_BIG, (352, 352, 352, 256, 256)))
    big = {name: None for name in _BIG}

    def update_group(tag, after):
        full = _split_wait(_swap_copies, *swapping[tag], after, "swap_halves_wait_%s" % tag)
        for (name, row0, _), g in zip(groups[tag], full):
            shp = w[name].shape
            r2 = (int(np.prod(shp[:-1])), shp[-1])
            big[name] = _adamw(w[name].reshape(r2), m[name].reshape(r2), v[name].reshape(r2), g.reshape(-1, shp[-1]),
                               row_tile[name], row0, big[name])
        return [big[name][1] for name, _, _ in groups[tag]]

    after = reduce_group("l0a", reduce_group("l1", [grad_x, l0b_begun]))
    after = update_group("l0a", update_group("l1", after))

    land = _plane_wait(*small_in_flight, after, "small_grads_wait")
    gall = _swap_halves([land])[0].reshape(8, grows, 128)
    gsum = _unpack(_sum8(gall), [(L, 9 * D)] + [sgrads[k].shape for k in names])
    red = dict(zip(("ada_b",) + names, gsum))
    red["ln_g"] = lax.dynamic_slice_in_dim(red["ln_g"], p_me * 256, 256, axis=2)
    red["ln_b"] = lax.dynamic_slice_in_dim(red["ln_b"], p_me * 256, 256, axis=2)
    red["glu_w"] = lax.dynamic_slice_in_dim(red["glu_w"], p_me * 64, 64, axis=1)

    dmod_all = gall[:, :L * 9 * D // 128].reshape(8, L, 9 * D)
    dmod_cols = jnp.transpose(lax.dynamic_slice_in_dim(dmod_all, p_me * ncol, ncol, axis=2), (1, 0, 2))
    g_ada_w = _ada_wgrad(jnp.transpose(c_all), dmod_cols)

    r2 = (L * D, ncol)
    res = _adamw(ada_w.reshape(r2), m["ada_w"].reshape(r2), v["ada_w"].reshape(r2), g_ada_w.reshape(r2), 128)
    out_g["ada_w"], out_d["ada_w"], out_m["ada_w"], out_v["ada_w"] = [t.reshape(ada_w.shape) for t in res]

    res_small = _adamw(*small_packs, _pack([red[k] for k in small_names]), small_packs[0].shape[0])
    for t, dst in zip(res_small, (out_g, out_d, out_m, out_v)):
        for k, a in zip(small_names, _unpack(t, [w[k].shape for k in small_names])):
            dst[k] = a

    update_group("l0b", reduce_group("l0b", [res_small[1], res[1]]))
    for name in _BIG:
        res = [t.reshape(w[name].shape) for t in big[name]]
        out_g[name], out_d[name], out_m[name], out_v[name] = [jnp.swapaxes(t, 2, 3) for t in res] if name in transposed else res

    return (loss, grad_x[None], *[out_g[k] for k in _ORDER], *[out_d[k] for k in _ORDER],
            *[out_m[k] for k in _ORDER], *[out_v[k] for k in _ORDER])
```

```python
import math

import numpy as np
import jax
import jax.numpy as jnp
from jax import lax
from jax.experimental import pallas as pl
from jax.experimental.pallas import tpu as pltpu

f32 = jnp.float32
bf16 = jnp.bfloat16
MESH = pl.DeviceIdType.MESH

D_MODEL = 1024
SEQ = 2048
DEPTH = 2
HEAD_DIM = 64
N_HEADS = 8
D_ATT = 512
DILATIONS = (1, 4, 16)
BLOCKS_PER_RESIDUE = (16, 4, 1)
ATT_BLOCK = 128
N_UNITS = SEQ // ATT_BLOCK
N_GROUPS = 16
SSM_STATE = 64
D_SSM = 256
D_STATE = N_GROUPS * SSM_STATE
POOL_WINDOWS = (2, 4, 8, 16)
POOL_GROUP = 64
D_POOL = 256
POOL_HALO = 16
D_FF = 2816
N_BUCKETS = 32
MAX_DISTANCE = 2048
ALPHA = (2 * DEPTH) ** 0.25
FFN_RES = 0.5
LN_EPS = 1e-5
NEG = -1e30
N_CHIPS = 4
FF_SHARD = D_FF // N_CHIPS
SCAN_SEG = 8

ADAM_LR, ADAM_B1, ADAM_B2, ADAM_EPS, ADAM_WD, ADAM_STEP = 0.001, 0.9, 0.999, 1e-08, 0.01, 10

TOK_TILE = 512


def _cp(dims=None, vmem_mb=None):
    kw = {}
    if dims is not None:
        kw["dimension_semantics"] = dims
    if vmem_mb is not None:
        kw["vmem_limit_bytes"] = vmem_mb << 20
    return pltpu.CompilerParams(**kw)


def _dot(a, b):
    return jnp.dot(a, b, preferred_element_type=f32)


def _dot_nt(a, b):
    return lax.dot_general(a, b, (((1,), (1,)), ((), ())), preferred_element_type=f32)


def _dot_tn(a, b):
    return lax.dot_general(a, b, (((0,), (0,)), ((), ())), preferred_element_type=f32)


def _ln_stats(v):
    mu = jnp.mean(v, -1, keepdims=True)
    d = v - mu
    var = jnp.mean(d * d, -1, keepdims=True)
    rstd = lax.rsqrt(var + LN_EPS)
    return d * rstd, rstd


def _ln_bwd(dxh, xh, rstd):
    return rstd * (dxh - jnp.mean(dxh, -1, keepdims=True) - xh * jnp.mean(dxh * xh, -1, keepdims=True))


_GELU_C = math.sqrt(2.0 / math.pi)


def _gelu(y):
    return 0.5 * y * (1.0 + jnp.tanh(_GELU_C * (y + 0.044715 * y * y * y)))


def _gelu_grad(y):
    t = jnp.tanh(_GELU_C * (y + 0.044715 * y * y * y))
    return 0.5 * (1.0 + t) + 0.5 * y * (1.0 - t * t) * (_GELU_C * (1.0 + 3 * 0.044715 * y * y))


def _full(shape):
    return pl.BlockSpec(shape, lambda *_: (0,) * len(shape))


def _hbm(*args):
    return [pltpu.with_memory_space_constraint(a, pltpu.HBM) if getattr(a, "ndim", 0) >= 2 else a for a in args]


def _ffn_fwd(x, mod3, wg, wu, wd, ls, lng, lnb):
    S, D = x.shape
    Fs = wg.shape[-2]
    ts = 2 * TOK_TILE

    def body(x_ref, mod_ref, wg_ref, wu_ref, wd_ref, lng_ref, lnb_ref, xo_ref, f_ref, ag_ref, au_ref, a_ref, h_ref):
        j = pl.program_id(1)

        @pl.when(j == 0)
        def _():
            xh, _ = _ln_stats(x_ref[...])
            h_ref[...] = (xh * (1.0 + mod_ref[1:2, :]) + mod_ref[0:1, :]).astype(bf16)
            f_ref[...] = jnp.zeros_like(f_ref)

        h = h_ref[...]
        g = _dot_nt(h, wg_ref[0, 0])
        u = _dot_nt(h, wu_ref[0, 0])
        sg = jax.nn.sigmoid(g)
        si = g * sg
        ag_ref[0] = (u * (sg * (1.0 + g * (1.0 - sg)))).astype(bf16)
        au_ref[0] = si.astype(bf16)
        a = (si * u).astype(bf16)
        a_ref[0] = a
        f_ref[...] += _dot(a, wd_ref[0, 0])

        @pl.when(j == N_CHIPS - 1)
        def _():
            f = f_ref[...]
            r = ALPHA * x_ref[...] + (FFN_RES * mod_ref[2:3, :]) * f
            rh, _ = _ln_stats(r)
            xo_ref[...] = rh * lng_ref[...] + lnb_ref[...]

    tok = pl.BlockSpec((ts, D), lambda i, j: (i, 0))
    wrow = pl.BlockSpec((1, 1, Fs, D), lambda i, j: (j, ls, 0, 0))
    hid = pl.BlockSpec((1, ts, Fs), lambda i, j: (j, i, 0))
    return pl.pallas_call(
        body, name="ffn_fwd", grid=(S // ts, N_CHIPS),
        in_specs=[tok, _full((3, D)), wrow, wrow, wrow, _full((1, D)), _full((1, D))],
        out_specs=[tok, tok, hid, hid, hid, tok],
        out_shape=[jax.ShapeDtypeStruct((S, D), f32), jax.ShapeDtypeStruct((S, D), f32)]
        + [jax.ShapeDtypeStruct((N_CHIPS, S, Fs), bf16)] * 3 + [jax.ShapeDtypeStruct((S, D), bf16)],
        compiler_params=_cp(("parallel", "arbitrary"), 62),
    )(*_hbm(x, mod3, wg, wu, wd, lng, lnb))


def _ffn_bwd(dxo, x, f, ag, au, mod3, wg, wu, wd, ls, lng):
    S, D = x.shape
    Fs = wg.shape[-2]
    ts = TOK_TILE

    def body(dxo_ref, x_ref, f_ref, ag_ref, au_ref, mod_ref, wg_ref, wu_ref, wd_ref, lng_ref,
             dx_ref, dg_ref, du_ref, df_ref, dmod_ref, dlng_ref, dlnb_ref,
             dr_sc, df_sc, acc_sc):
        i = pl.program_id(0)
        j = pl.program_id(1)

        @pl.when((i == 0) & (j == 0))
        def _():
            dmod_ref[...] = jnp.zeros_like(dmod_ref)
            dlng_ref[...] = jnp.zeros_like(dlng_ref)
            dlnb_ref[...] = jnp.zeros_like(dlnb_ref)

        @pl.when(j == 0)
        def _():
            xv = x_ref[...]
            fv = f_ref[...]
            gate = mod_ref[2:3, :]
            rh, rstd = _ln_stats(ALPHA * xv + (FFN_RES * gate) * fv)
            dy = dxo_ref[...]
            dlng_ref[...] += jnp.sum(dy * rh, 0, keepdims=True)
            dlnb_ref[...] += jnp.sum(dy, 0, keepdims=True)
            dr = _ln_bwd(dy * lng_ref[...], rh, rstd)
            dr_sc[...] = dr
            dmod_ref[2:3, :] += jnp.sum(FFN_RES * dr * fv, 0, keepdims=True)
            df = ((FFN_RES * gate) * dr).astype(bf16)
            df_sc[...] = df
            df_ref[...] = df
            acc_sc[...] = jnp.zeros_like(acc_sc)

        da = _dot_nt(df_sc[...], wd_ref[0, 0])
        dgv = (da * ag_ref[0].astype(f32)).astype(bf16)
        duv = (da * au_ref[0].astype(f32)).astype(bf16)
        dg_ref[0] = dgv
        du_ref[0] = duv
        acc_sc[...] += _dot(dgv, wg_ref[0, 0]) + _dot(duv, wu_ref[0, 0])

        @pl.when(j == N_CHIPS - 1)
        def _():
            dh = acc_sc[...]
            xh, rstd0 = _ln_stats(x_ref[...])
            dmod_ref[0:1, :] += jnp.sum(dh, 0, keepdims=True)
            dmod_ref[1:2, :] += jnp.sum(dh * xh, 0, keepdims=True)
            dx_ref[...] = _ln_bwd(dh * (1.0 + mod_ref[1:2, :]), xh, rstd0) + ALPHA * dr_sc[...]

    tok = pl.BlockSpec((ts, D), lambda i, j: (i, 0))
    wrow = pl.BlockSpec((1, 1, Fs, D), lambda i, j: (j, ls, 0, 0))
    hid = pl.BlockSpec((1, ts, Fs), lambda i, j: (j, i, 0))
    hid_shape = jax.ShapeDtypeStruct((N_CHIPS, S, Fs), bf16)
    return pl.pallas_call(
        body, name="ffn_bwd", grid=(S // ts, N_CHIPS),
        in_specs=[tok, tok, tok, hid, hid, _full((3, D)), wrow, wrow, wrow, _full((1, D))],
        out_specs=[tok, hid, hid, tok, _full((3, D)), _full((1, D)), _full((1, D))],
        out_shape=[jax.ShapeDtypeStruct((S, D), f32), hid_shape, hid_shape,
                   jax.ShapeDtypeStruct((S, D), bf16),
                   jax.ShapeDtypeStruct((3, D), f32), jax.ShapeDtypeStruct((1, D), f32), jax.ShapeDtypeStruct((1, D), f32)],
        scratch_shapes=[pltpu.VMEM((ts, D), f32), pltpu.VMEM((ts, D), bf16), pltpu.VMEM((ts, D), f32)],
        compiler_params=_cp(("arbitrary", "arbitrary"), 56),
    )(*_hbm(dxo, x, f, ag, au, mod3, wg, wu, wd, lng))


def _ffn_wgrad(h, dg, du, a, df, gwg, gwu, gwd, ls):
    S, D = h.shape
    Fs = dg.shape[-1]
    tk = 2 * TOK_TILE
    nk = S // tk

    def body(h_ref, dg_ref, du_ref, a_ref, df_ref, _g0, _g1, _g2, gwg_ref, gwu_ref, gwd_ref, ag_sc, au_sc, ad_sc):
        k = pl.program_id(1)

        @pl.when(k == 0)
        def _():
            ag_sc[...] = jnp.zeros_like(ag_sc)
            au_sc[...] = jnp.zeros_like(au_sc)
            ad_sc[...] = jnp.zeros_like(ad_sc)

        hv = h_ref[...]
        ag_sc[...] += _dot_tn(dg_ref[0], hv)
        au_sc[...] += _dot_tn(du_ref[0], hv)
        ad_sc[...] += _dot_tn(a_ref[0], df_ref[...])

        @pl.when(k == nk - 1)
        def _():
            gwg_ref[0, 0] = ag_sc[...].astype(bf16)
            gwu_ref[0, 0] = au_sc[...].astype(bf16)
            gwd_ref[0, 0] = ad_sc[...].astype(bf16)

    tok = pl.BlockSpec((tk, D), lambda p, k: (k, 0))
    hid = pl.BlockSpec((1, tk, Fs), lambda p, k: (p, k, 0))
    anyspec = pl.BlockSpec(memory_space=pl.ANY)
    orow = pl.BlockSpec((1, 1, Fs, D), lambda p, k: (p, ls, 0, 0))
    return pl.pallas_call(
        body, name="ffn_wgrad", grid=(N_CHIPS, nk),
        in_specs=[tok, hid, hid, hid, tok, anyspec, anyspec, anyspec],
        out_specs=[orow, orow, orow],
        out_shape=[jax.ShapeDtypeStruct(gwg.shape, bf16), jax.ShapeDtypeStruct(gwu.shape, bf16),
                   jax.ShapeDtypeStruct(gwd.shape, bf16)],
        scratch_shapes=[pltpu.VMEM((Fs, D), f32), pltpu.VMEM((Fs, D), f32), pltpu.VMEM((Fs, D), f32)],
        input_output_aliases={5: 0, 6: 1, 7: 2},
        compiler_params=_cp(("parallel", "arbitrary"), 48),
    )(*_hbm(h, dg, du, a, df, gwg, gwu, gwd))


_LANES = 128
_QKV_BLOCKS = D_ATT // _LANES


def _res_spec(lead, d, width, index):
    return pl.BlockSpec((lead, d, TOK_TILE // d, width), index)


def _res_spec3(d, width):
    return pl.BlockSpec((d, TOK_TILE // d, width), lambda i: (0, i, 0))


def _rows_to_residues(tile_bufs, d, put):
    for r in range(d):
        for cb, buf in enumerate(tile_bufs):
            put(r, cb, buf[pl.ds(r, TOK_TILE // d, stride=d), :])


def _residues_to_rows(tile_bufs, d, get):
    for r in range(d):
        for cb, buf in enumerate(tile_bufs):
            buf[pl.ds(r, TOK_TILE // d, stride=d), :] = get(r, cb)


def _mix_in_fwd(x, mod3, w_in, l):
    S, D = x.shape
    N = w_in.shape[-1]
    ts = TOK_TILE

    def body(x_ref, mod_ref, w_ref, o1_ref, o4_ref, o16_ref, zr_ref, h_ref, *bufs):
        j = pl.program_id(1)

        @pl.when(j == 0)
        def _():
            xh, _ = _ln_stats(x_ref[...])
            h_ref[...] = (xh * (1.0 + mod_ref[1:2, :]) + mod_ref[0:1, :]).astype(bf16)

        z = _dot(h_ref[...], w_ref[0, 0])

        @pl.when(j == N_CHIPS - 1)
        def _():
            zr_ref[...] = z

        @pl.when(j < N_CHIPS - 1)
        def _():
            zz = z * jnp.where(j == 0, HEAD_DIM ** -0.5, 1.0)
            o1_ref[j, 0] = zz.astype(bf16)
            for cb, buf in enumerate(bufs):
                buf[...] = zz[:, _LANES * cb:_LANES * (cb + 1)]
            for d, o_ref in zip(DILATIONS[1:], (o4_ref, o16_ref)):
                def put(r, cb, piece, o_ref=o_ref):
                    o_ref[j, r, :, _LANES * cb:_LANES * (cb + 1)] = piece.astype(bf16)
                _rows_to_residues(bufs, d, put)

    tok = pl.BlockSpec((ts, D), lambda i, j: (i, 0))
    res = [_res_spec(3, d, N, lambda i, j: (0, 0, i, 0)) for d in DILATIONS]
    return pl.pallas_call(
        body, name="mix_in_fwd", grid=(S // ts, N_CHIPS),
        in_specs=[tok, _full((3, D)), pl.BlockSpec((1, 1, D, N), lambda i, j: (j, l, 0, 0))],
        out_specs=res + [pl.BlockSpec((ts, N), lambda i, j: (i, 0)), tok],
        out_shape=[jax.ShapeDtypeStruct((3, d, S // d, N), bf16) for d in DILATIONS]
        + [jax.ShapeDtypeStruct((S, N), f32), jax.ShapeDtypeStruct((S, D), bf16)],
        scratch_shapes=[pltpu.VMEM((ts, _LANES), f32)] * _QKV_BLOCKS,
        compiler_params=_cp(("parallel", "arbitrary"), 40),
    )(*_hbm(x, mod3, w_in))


def _mix_in_bwd(dqkv, d_rest, dx_res, x, mod3, w_in, l):
    S, D = x.shape
    N = w_in.shape[-1]
    ts = TOK_TILE

    def body(d1_ref, d4_ref, d16_ref, dr_ref, dxr_ref, x_ref, mod_ref, w_ref, dx_ref, dmod_ref, dz_ref, acc_sc, *bufs):
        i = pl.program_id(0)
        j = pl.program_id(1)

        @pl.when((i == 0) & (j == 0))
        def _():
            dmod_ref[...] = jnp.zeros_like(dmod_ref)

        @pl.when(j == 0)
        def _():
            acc_sc[...] = jnp.zeros_like(acc_sc)

        @pl.when(j == N_CHIPS - 1)
        def _():
            dz_ref[0] = dr_ref[...]

        @pl.when(j < N_CHIPS - 1)
        def _():
            for d, d_ref, tile_bufs in ((4, d4_ref, bufs[:_QKV_BLOCKS]), (16, d16_ref, bufs[_QKV_BLOCKS:])):
                _residues_to_rows(tile_bufs, d, lambda r, cb, d_ref=d_ref: d_ref[0, r, :, _LANES * cb:_LANES * (cb + 1)].astype(f32))
            for cb in range(_QKV_BLOCKS):
                cols = slice(_LANES * cb, _LANES * (cb + 1))
                dz_ref[0, :, cols] = (d1_ref[0, 0, :, cols].astype(f32) + bufs[cb][...] + bufs[_QKV_BLOCKS + cb][...]).astype(bf16)

        acc_sc[...] += _dot_nt(dz_ref[0], w_ref[0, 0])

        @pl.when(j == N_CHIPS - 1)
        def _():
            dh = acc_sc[...]
            xh, rstd0 = _ln_stats(x_ref[...])
            dmod_ref[0:1, :] += jnp.sum(dh, 0, keepdims=True)
            dmod_ref[1:2, :] += jnp.sum(dh * xh, 0, keepdims=True)
            dx_ref[...] = _ln_bwd(dh * (1.0 + mod_ref[1:2, :]), xh, rstd0) + dxr_ref[...]

    tok = pl.BlockSpec((ts, D), lambda i, j: (i, 0))
    res = [_res_spec(1, d, N, lambda i, j: (jnp.minimum(j, 2), 0, i, 0)) for d in DILATIONS]
    return pl.pallas_call(
        body, name="mix_in_bwd", grid=(S // ts, N_CHIPS),
        in_specs=res + [pl.BlockSpec((ts, N), lambda i, j: (i, 0)), tok, tok, _full((3, D)),
                        pl.BlockSpec((1, 1, D, N), lambda i, j: (j, l, 0, 0))],
        out_specs=[tok, _full((3, D)), pl.BlockSpec((1, ts, N), lambda i, j: (j, i, 0))],
        out_shape=[jax.ShapeDtypeStruct((S, D), f32), jax.ShapeDtypeStruct((3, D), f32),
                   jax.ShapeDtypeStruct((N_CHIPS, S, N), bf16)],
        scratch_shapes=[pltpu.VMEM((ts, D), f32)] + [pltpu.VMEM((ts, _LANES), f32)] * (2 * _QKV_BLOCKS),
        compiler_params=_cp(("arbitrary", "arbitrary"), 40),
    )(*_hbm(*dqkv, d_rest, dx_res, x, mod3, w_in))


def _mix_in_wgrad(h, dz, gw, l):
    S, D = h.shape
    N = dz.shape[-1]
    tk = 2 * TOK_TILE
    nk = S // tk

    def body(h_ref, dz_ref, _g, gw_ref, acc_sc):
        k = pl.program_id(1)

        @pl.when(k == 0)
        def _():
            acc_sc[...] = jnp.zeros_like(acc_sc)

        acc_sc[...] += _dot_tn(h_ref[...], dz_ref[0])

        @pl.when(k == nk - 1)
        def _():
            gw_ref[0, 0] = acc_sc[...].astype(bf16)

    return pl.pallas_call(
        body, name="mix_in_wgrad", grid=(N_CHIPS, nk),
        in_specs=[pl.BlockSpec((tk, D), lambda p, k: (k, 0)), pl.BlockSpec((1, tk, N), lambda p, k: (p, k, 0)),
                  pl.BlockSpec(memory_space=pl.ANY)],
        out_specs=pl.BlockSpec((1, 1, D, N), lambda p, k: (p, l, 0, 0)),
        out_shape=jax.ShapeDtypeStruct(gw.shape, bf16),
        scratch_shapes=[pltpu.VMEM((D, N), f32)],
        input_output_aliases={2: 0},
        compiler_params=_cp(("parallel", "arbitrary"), 40),
    )(*_hbm(h, dz, gw))


def _mix_out_fwd(x, y_att, y_ssm, y_pool, mod3, w_out, l, lng, lnb):
    S, D = x.shape
    ts = TOK_TILE

    def body(x_ref, ya_ref, ys_ref, yp_ref, mod_ref, w_ref, lng_ref, lnb_ref, xo_ref, y_ref):
        ya = ya_ref[...].astype(bf16)
        y = (_dot(ya[:, 0:256], w_ref[0, 0]) + _dot(ya[:, 256:512], w_ref[1, 0])
             + _dot(ys_ref[...].astype(bf16), w_ref[2, 0]) + _dot(yp_ref[...].astype(bf16), w_ref[3, 0]))
        y_ref[...] = y
        rh, _ = _ln_stats(ALPHA * x_ref[...] + mod_ref[2:3, :] * y)
        xo_ref[...] = rh * lng_ref[...] + lnb_ref[...]

    tok = pl.BlockSpec((ts, D), lambda i: (i, 0))
    return pl.pallas_call(
        body, name="mix_out_fwd", grid=(S // ts,),
        in_specs=[tok, pl.BlockSpec((ts, D_ATT), lambda i: (i, 0)), pl.BlockSpec((ts, D_SSM), lambda i: (i, 0)),
                  pl.BlockSpec((ts, D_POOL), lambda i: (i, 0)), _full((3, D)),
                  pl.BlockSpec((N_CHIPS, 1, 256, D), lambda i: (0, l, 0, 0)), _full((1, D)), _full((1, D))],
        out_specs=[tok, tok],
        out_shape=[jax.ShapeDtypeStruct((S, D), f32), jax.ShapeDtypeStruct((S, D), f32)],
        compiler_params=_cp(("parallel",), 40),
    )(*_hbm(x, y_att, y_ssm, y_pool, mod3, w_out, lng, lnb))


def _mix_out_bwd(dxo, x, y, y_att, y_ssm, y_pool, mod3, w_out, l, lng, gw_out):
    S, D = x.shape
    ts = TOK_TILE
    nt = S // ts

    def body(dxo_ref, x_ref, y_ref, ya_ref, ys_ref, yp_ref, mod_ref, w_ref, lng_ref, _g,
             dxr_ref, da_ref, ds_ref, dp_ref, dgate_ref, dlng_ref, dlnb_ref, gw_ref, acc_sc):
        i = pl.program_id(0)

        @pl.when(i == 0)
        def _():
            dgate_ref[...] = jnp.zeros_like(dgate_ref)
            dlng_ref[...] = jnp.zeros_like(dlng_ref)
            dlnb_ref[...] = jnp.zeros_like(dlnb_ref)
            acc_sc[...] = jnp.zeros_like(acc_sc)

        gate = mod_ref[2:3, :]
        yv = y_ref[...]
        rh, rstd = _ln_stats(ALPHA * x_ref[...] + gate * yv)
        dy_out = dxo_ref[...]
        dlng_ref[...] += jnp.sum(dy_out * rh, 0, keepdims=True)
        dlnb_ref[...] += jnp.sum(dy_out, 0, keepdims=True)
        dr = _ln_bwd(dy_out * lng_ref[...], rh, rstd)
        dxr_ref[...] = ALPHA * dr
        dgate_ref[...] += jnp.sum(dr * yv, 0, keepdims=True)
        dy = (gate * dr).astype(bf16)
        da_ref[:, 0:256] = _dot_nt(dy, w_ref[0, 0])
        da_ref[:, 256:512] = _dot_nt(dy, w_ref[1, 0])
        ds_ref[...] = _dot_nt(dy, w_ref[2, 0])
        dp_ref[...] = _dot_nt(dy, w_ref[3, 0])
        ya = ya_ref[...].astype(bf16)
        acc_sc[0] += _dot_tn(ya[:, 0:256], dy)
        acc_sc[1] += _dot_tn(ya[:, 256:512], dy)
        acc_sc[2] += _dot_tn(ys_ref[...].astype(bf16), dy)
        acc_sc[3] += _dot_tn(yp_ref[...].astype(bf16), dy)

        @pl.when(i == nt - 1)
        def _():
            gw_ref[:, 0] = acc_sc[...].astype(bf16)

    tok = pl.BlockSpec((ts, D), lambda i: (i, 0))
    t512 = pl.BlockSpec((ts, D_ATT), lambda i: (i, 0))
    t256 = pl.BlockSpec((ts, 256), lambda i: (i, 0))
    wspec = pl.BlockSpec((N_CHIPS, 1, 256, D), lambda i: (0, l, 0, 0))
    return pl.pallas_call(
        body, name="mix_out_bwd", grid=(nt,),
        in_specs=[tok, tok, tok, t512, t256, t256, _full((3, D)), wspec, _full((1, D)), pl.BlockSpec(memory_space=pl.ANY)],
        out_specs=[tok, t512, t256, t256, _full((1, D)), _full((1, D)), _full((1, D)), wspec],
        out_shape=[jax.ShapeDtypeStruct((S, D), f32), jax.ShapeDtypeStruct((S, D_ATT), f32),
                   jax.ShapeDtypeStruct((S, D_SSM), f32), jax.ShapeDtypeStruct((S, D_POOL), f32),
                   jax.ShapeDtypeStruct((1, D), f32), jax.ShapeDtypeStruct((1, D), f32), jax.ShapeDtypeStruct((1, D), f32),
                   jax.ShapeDtypeStruct(gw_out.shape, bf16)],
        scratch_shapes=[pltpu.VMEM((N_CHIPS, 256, D), f32)],
        input_output_aliases={9: 7},
        compiler_params=_cp(("arbitrary",), 48),
    )(*_hbm(dxo, x, y, y_att, y_ssm, y_pool, mod3, w_out, lng, gw_out))


def _t5_bucket(dist):
    max_exact = N_BUCKETS // 2
    d = np.maximum(dist, 1).astype(np.float32)
    large = max_exact + (np.log(d / max_exact) / math.log(MAX_DISTANCE / max_exact)
                         * (N_BUCKETS - max_exact)).astype(np.int32)
    large = np.minimum(large, N_BUCKETS - 1)
    return np.where(dist < max_exact, dist, large).astype(np.int32)


def _bucket_table():
    q = ATT_BLOCK
    i = np.arange(q)[:, None]
    j = np.arange(2 * q)[None, :]
    r = i + q - j
    in_band = (r >= 0) & (r <= q)
    tabs = [np.where(in_band, _t5_bucket(np.clip(r, 0, None) * d), -1) for d in DILATIONS]
    return np.stack(tabs).astype(np.int32)


def _bias_fwd(rel_bias, table):
    def body(rb_ref, tab_ref, out_ref):
        for b in range(3):
            tb = tab_ref[b]
            for h in range(N_HEADS):
                def pick(k, acc):
                    return jnp.where(tb == k, rb_ref[k, h], acc)
                out_ref[b, h] = lax.fori_loop(0, N_BUCKETS, pick, jnp.where(tb < 0, NEG, 0.0).astype(f32))

    return pl.pallas_call(
        body, name="bias_fwd",
        in_specs=[pl.BlockSpec(memory_space=pltpu.SMEM), pl.BlockSpec(memory_space=pltpu.VMEM)],
        out_specs=pl.BlockSpec(memory_space=pltpu.VMEM),
        out_shape=jax.ShapeDtypeStruct((3, N_HEADS, ATT_BLOCK, 2 * ATT_BLOCK), f32),
    )(rel_bias, table)


def _bias_bwd(dbias, table):
    def body(db_ref, tab_ref, out_ref):
        def per_bucket(k, c):
            for h in range(N_HEADS):
                tot = jnp.zeros((), f32)
                for b in range(3):
                    tot = tot + jnp.sum(jnp.where(tab_ref[b] == k, db_ref[b, h], 0.0))
                out_ref[k, h] = tot
            return c
        lax.fori_loop(0, N_BUCKETS, per_bucket, 0)

    return pl.pallas_call(
        body, name="bias_bwd",
        in_specs=[pl.BlockSpec(memory_space=pltpu.VMEM), pl.BlockSpec(memory_space=pltpu.VMEM)],
        out_specs=pl.BlockSpec(memory_space=pltpu.SMEM),
        out_shape=jax.ShapeDtypeStruct((N_BUCKETS, N_HEADS), f32),
    )(dbias, table)


def _att_unit(u, nbr):
    rows = pl.ds(pl.multiple_of(u * ATT_BLOCK, ATT_BLOCK), ATT_BLOCK)
    prev = pl.ds(pl.multiple_of(jnp.maximum(u - 1, 0) * ATT_BLOCK, ATT_BLOCK), ATT_BLOCK)
    return rows, prev, (u % nbr) != 0


_N_PAIRS = N_HEADS // 2


def _pair_rows(t):
    lane = lax.broadcasted_iota(jnp.int32, t.shape, 1)
    zero = jnp.zeros_like(t)
    return jnp.concatenate([jnp.where(lane < HEAD_DIM, t, zero), jnp.where(lane >= HEAD_DIM, t, zero)], axis=0)


def _pair_cols(big):
    lane = lax.broadcasted_iota(jnp.int32, (ATT_BLOCK, _LANES), 1)
    return jnp.where(lane < HEAD_DIM, big[:ATT_BLOCK], big[ATT_BLOCK:])


def _pair_column(ref, rows, hp):
    t = ref[rows, :]
    return jnp.concatenate([t[:, 2 * hp:2 * hp + 1], t[:, 2 * hp + 1:2 * hp + 2]], axis=0)


def _pair_band(ref, rows, prev, nbr, hp):
    lanes = pl.ds(_LANES * hp, _LANES)
    cur = ref[0, rows, lanes]
    return cur if nbr == 1 else jnp.concatenate([ref[0, prev, lanes], cur], axis=0)


def _pair_scores(q_ref, k_ref, b_ref, rows, prev, valid_prev, nbr, hp):
    qbd = _pair_rows(q_ref[0, rows, pl.ds(_LANES * hp, _LANES)])
    kb = _pair_band(k_ref, rows, prev, nbr, hp)
    bias = b_ref[0, 2 * hp:2 * hp + 2].reshape(2 * ATT_BLOCK, 2 * ATT_BLOCK)
    if nbr == 1:
        return qbd, kb, _dot_nt(qbd, kb) + bias[:, ATT_BLOCK:]
    s = _dot_nt(qbd, kb) + bias
    col = lax.broadcasted_iota(jnp.int32, s.shape, 1)
    return qbd, kb, jnp.where((col >= ATT_BLOCK) | valid_prev, s, NEG)


def _qkv_specs(S, branch):
    return ([pl.BlockSpec((1, S, D_ATT), lambda i, t=t: (t, 0, 0)) for t in range(3)],
            pl.BlockSpec((1, N_HEADS, ATT_BLOCK, 2 * ATT_BLOCK), lambda i: (branch, 0, 0, 0)))


def _att_fwd(qkv, bias, branch):
    S = qkv.shape[1]
    nbr = BLOCKS_PER_RESIDUE[branch]

    def body(q_ref, k_ref, v_ref, b_ref, o_ref, lse_ref):
        lse_ref[...] = jnp.zeros_like(lse_ref)

        def unit(u, c):
            rows, prev, valid_prev = _att_unit(u, nbr)
            for hp in range(_N_PAIRS):
                _, _, s = _pair_scores(q_ref, k_ref, b_ref, rows, prev, valid_prev, nbr, hp)
                m = jnp.max(s, -1, keepdims=True)
                p = jnp.exp(s - m)
                den = jnp.sum(p, -1, keepdims=True)
                big = _dot(p.astype(bf16), _pair_band(v_ref, rows, prev, nbr, hp))
                o_ref[rows, pl.ds(_LANES * hp, _LANES)] = _pair_cols(big / den)
                lse = m + jnp.log(den)
                lse_ref[rows, pl.ds(2 * hp, 1)] = lse[:ATT_BLOCK]
                lse_ref[rows, pl.ds(2 * hp + 1, 1)] = lse[ATT_BLOCK:]
            return c

        lax.fori_loop(0, N_UNITS, unit, 0)

    qkv_specs, bspec = _qkv_specs(S, branch)
    return pl.pallas_call(
        body, name="att_fwd", grid=(1,),
        in_specs=qkv_specs + [bspec],
        out_specs=[pl.BlockSpec((S, D_ATT), lambda i: (0, 0)), pl.BlockSpec((S, _LANES), lambda i: (0, 0))],
        out_shape=[jax.ShapeDtypeStruct((S, D_ATT), f32), jax.ShapeDtypeStruct((S, _LANES), f32)],
        compiler_params=_cp(("arbitrary",), 40),
    )(*_hbm(qkv, qkv, qkv, bias))


def _att_bwd(qkv, do, lse, crow, bias, branch):
    S = qkv.shape[1]
    nbr = BLOCKS_PER_RESIDUE[branch]

    def body(q_ref, k_ref, v_ref, do_ref, lse_ref, c_ref, b_ref, dqkv_ref, db_ref, dk_sc, dv_sc):
        dk_sc[...] = jnp.zeros_like(dk_sc)
        dv_sc[...] = jnp.zeros_like(dv_sc)
        db_ref[...] = jnp.zeros_like(db_ref)

        def unit(u, c):
            rows, prev, valid_prev = _att_unit(u, nbr)
            for hp in range(_N_PAIRS):
                lanes = pl.ds(_LANES * hp, _LANES)
                qbd, kb, s = _pair_scores(q_ref, k_ref, b_ref, rows, prev, valid_prev, nbr, hp)
                p = jnp.exp(s - _pair_column(lse_ref, rows, hp))
                dobd = _pair_rows(do_ref[rows, lanes])
                ds = p * (_dot_nt(dobd, _pair_band(v_ref, rows, prev, nbr, hp)) - _pair_column(c_ref, rows, hp))
                if nbr == 1:
                    db_ref[2 * hp:2 * hp + 2, :, ATT_BLOCK:] += ds.reshape(2, ATT_BLOCK, ATT_BLOCK)
                else:
                    db_ref[2 * hp:2 * hp + 2] += ds.reshape(2, ATT_BLOCK, 2 * ATT_BLOCK)
                dsb = ds.astype(bf16)
                dqkv_ref[0, rows, lanes] = (HEAD_DIM ** -0.5 * _pair_cols(_dot(dsb, kb))).astype(bf16)
                dkb = _dot_tn(dsb, qbd)
                dvb = _dot_tn(p.astype(bf16), dobd)
                if nbr == 1:
                    dk_sc[rows, lanes] += dkb
                    dv_sc[rows, lanes] += dvb
                else:
                    dk_sc[prev, lanes] += dkb[:ATT_BLOCK]
                    dv_sc[prev, lanes] += dvb[:ATT_BLOCK]
                    dk_sc[rows, lanes] += dkb[ATT_BLOCK:]
                    dv_sc[rows, lanes] += dvb[ATT_BLOCK:]
            return c

        lax.fori_loop(0, N_UNITS, unit, 0)
        dqkv_ref[1] = dk_sc[...].astype(bf16)
        dqkv_ref[2] = dv_sc[...].astype(bf16)

    qkv_specs, bspec = _qkv_specs(S, branch)
    row = pl.BlockSpec((S, _LANES), lambda i: (0, 0))
    return pl.pallas_call(
        body, name="att_bwd", grid=(1,),
        in_specs=qkv_specs + [pl.BlockSpec((S, D_ATT), lambda i: (0, 0)), row, row, bspec],
        out_specs=[pl.BlockSpec((3, S, D_ATT), lambda i: (0, 0, 0)),
                   pl.BlockSpec((N_HEADS, ATT_BLOCK, 2 * ATT_BLOCK), lambda i: (0, 0, 0))],
        out_shape=[jax.ShapeDtypeStruct((3, S, D_ATT), bf16), jax.ShapeDtypeStruct((N_HEADS, ATT_BLOCK, 2 * ATT_BLOCK), f32)],
        scratch_shapes=[pltpu.VMEM((S, D_ATT), f32), pltpu.VMEM((S, D_ATT), f32)],
        compiler_params=_cp(("arbitrary",), 48),
    )(*_hbm(qkv, qkv, qkv, do, lse, crow, bias))


def _branch_weights(lse_ref):
    l0, l1, l2 = lse_ref[0], lse_ref[1], lse_ref[2]
    m = jnp.maximum(jnp.maximum(l0, l1), l2)
    e0, e1, e2 = jnp.exp(l0 - m), jnp.exp(l1 - m), jnp.exp(l2 - m)
    tot = e0 + e1 + e2
    return e0 / tot, e1 / tot, e2 / tot


def _att_merge(os, lses):
    S = os[0].shape[0] * os[0].shape[1]
    ts = TOK_TILE

    def body(o1_ref, o4_ref, o16_ref, l1_ref, l4_ref, l16_ref, y_ref, lt_ref, *bufs):
        obufs = (bufs[:_QKV_BLOCKS], bufs[_QKV_BLOCKS:2 * _QKV_BLOCKS])
        lt_ref[0] = l1_ref[0]
        for k, (d, o_ref, l_ref) in enumerate(((4, o4_ref, l4_ref), (16, o16_ref, l16_ref))):
            _residues_to_rows(obufs[k], d, lambda r, cb, o_ref=o_ref: o_ref[r, :, _LANES * cb:_LANES * (cb + 1)])
            _residues_to_rows([bufs[2 * _QKV_BLOCKS + k]], d, lambda r, cb, l_ref=l_ref: l_ref[r])
            lt_ref[1 + k] = bufs[2 * _QKV_BLOCKS + k][...]
        w = _branch_weights(lt_ref)
        for h in range(N_HEADS):
            cs = slice(HEAD_DIM * h, HEAD_DIM * (h + 1))
            half = slice(HEAD_DIM * (h % 2), HEAD_DIM * (h % 2 + 1))
            y_ref[:, cs] = (w[0][:, h:h + 1] * o1_ref[0, :, cs] + w[1][:, h:h + 1] * obufs[0][h // 2][:, half]
                            + w[2][:, h:h + 1] * obufs[1][h // 2][:, half])

    return pl.pallas_call(
        body, name="att_merge", grid=(S // ts,),
        in_specs=[_res_spec3(d, D_ATT) for d in DILATIONS] + [_res_spec3(d, _LANES) for d in DILATIONS],
        out_specs=[pl.BlockSpec((ts, D_ATT), lambda i: (i, 0)), pl.BlockSpec((3, ts, _LANES), lambda i: (0, i, 0))],
        out_shape=[jax.ShapeDtypeStruct((S, D_ATT), f32), jax.ShapeDtypeStruct((3, S, _LANES), f32)],
        scratch_shapes=[pltpu.VMEM((ts, _LANES), f32)] * (2 * _QKV_BLOCKS + 2),
        compiler_params=_cp(("parallel",)),
    )(*_hbm(*os, *lses))


def _att_merge_bwd(dy, y, lse3):
    S = dy.shape[0]
    ts = TOK_TILE

    def body(dy_ref, y_ref, lse_ref, do1_ref, do4_ref, do16_ref, c1_ref, c4_ref, c16_ref, *bufs):
        dobufs = (bufs[:_QKV_BLOCKS], bufs[_QKV_BLOCKS:2 * _QKV_BLOCKS], bufs[2 * _QKV_BLOCKS:3 * _QKV_BLOCKS])
        cbufs = bufs[3 * _QKV_BLOCKS:]
        w = _branch_weights(lse_ref)
        for cb in cbufs:
            cb[...] = jnp.zeros_like(cb)
        for h in range(N_HEADS):
            cs = slice(HEAD_DIM * h, HEAD_DIM * (h + 1))
            half = slice(HEAD_DIM * (h % 2), HEAD_DIM * (h % 2 + 1))
            dyh = dy_ref[:, cs]
            t = jnp.sum(dyh * y_ref[:, cs], -1, keepdims=True)
            for p in range(3):
                wp = w[p][:, h:h + 1]
                dobufs[p][h // 2][:, half] = wp * dyh
                cbufs[p][:, h:h + 1] = wp * t
        for cb in range(_QKV_BLOCKS):
            do1_ref[0, :, _LANES * cb:_LANES * (cb + 1)] = dobufs[0][cb][...].astype(bf16)
        c1_ref[0] = cbufs[0][...]
        for k, (d, do_ref, c_ref) in enumerate(((4, do4_ref, c4_ref), (16, do16_ref, c16_ref))):
            def put_do(r, cb, piece, do_ref=do_ref):
                do_ref[r, :, _LANES * cb:_LANES * (cb + 1)] = piece.astype(bf16)

            def put_c(r, cb, piece, c_ref=c_ref):
                c_ref[r] = piece

            _rows_to_residues(dobufs[1 + k], d, put_do)
            _rows_to_residues([cbufs[1 + k]], d, put_c)

    return pl.pallas_call(
        body, name="att_merge_bwd", grid=(S // ts,),
        in_specs=[pl.BlockSpec((ts, D_ATT), lambda i: (i, 0)), pl.BlockSpec((ts, D_ATT), lambda i: (i, 0)),
                  pl.BlockSpec((3, ts, _LANES), lambda i: (0, i, 0))],
        out_specs=[_res_spec3(d, D_ATT) for d in DILATIONS] + [_res_spec3(d, _LANES) for d in DILATIONS],
        out_shape=[jax.ShapeDtypeStruct((d, S // d, D_ATT), bf16) for d in DILATIONS]
        + [jax.ShapeDtypeStruct((d, S // d, _LANES), f32) for d in DILATIONS],
        scratch_shapes=[pltpu.VMEM((ts, _LANES), f32)] * (3 * _QKV_BLOCKS + 3),
        compiler_params=_cp(("parallel",)),
    )(*_hbm(dy, y, lse3))


_SSM_ROWS = 256


def _scan_in_place(sr_ref, si_ref, a_ref, reverse):
    S, N = sr_ref.shape
    nst = S // SCAN_SEG
    ar = jnp.broadcast_to(a_ref[0:1, :], (SCAN_SEG, N))
    ai = jnp.broadcast_to(a_ref[1:2, :], (SCAN_SEG, N))
    if reverse:
        ai = -ai
    row = lax.broadcasted_iota(jnp.int32, (SCAN_SEG, N), 0)
    zero = jnp.zeros((SCAN_SEG, N), f32)

    def tile(t):
        return pl.ds(pl.multiple_of((nst - 1 - t if reverse else t) * SCAN_SEG, SCAN_SEG), SCAN_SEG)

    def local(t, c):
        sr, si, pr, pi = c
        rows = tile(t)
        nsr = ar * sr - ai * si + sr_ref[rows, :]
        nsi = ar * si + ai * sr + si_ref[rows, :]
        sr_ref[rows, :] = nsr
        si_ref[rows, :] = nsi
        return nsr, nsi, ar * pr - ai * pi, ar * pi + ai * pr

    fr, fi, apr, api = lax.fori_loop(0, nst, local, (zero, zero, zero + 1.0, zero))

    def shift(v):
        if reverse:
            return jnp.where(row == SCAN_SEG - 1, 0.0, pltpu.roll(v, SCAN_SEG - 1, axis=0))
        return jnp.where(row == 0, 0.0, pltpu.roll(v, 1, axis=0))

    cr, ci = zero, zero
    for _ in range(SCAN_SEG - 1):
        cr, ci = shift(fr + apr * cr - api * ci), shift(fi + apr * ci + api * cr)

    def fix(t, c):
        pr, pi = c
        npr, npi = ar * pr - ai * pi, ar * pi + ai * pr
        rows = tile(t)
        sr_ref[rows, :] += npr * cr - npi * ci
        si_ref[rows, :] += npr * ci + npi * cr
        return npr, npi

    lax.fori_loop(0, nst, fix, (zero + 1.0, zero))


def _ssm_fwd(u, bre, bim, a2, cre, cim, l, dskip, glu_w, glu_b):
    S = u.shape[0]
    nproj = S // _SSM_ROWS

    def body(u_ref, br_ref, bi_ref, a2_ref, cr_ref, ci_ref, d_ref, w_ref, b_ref, sr_ref, si_ref, out_ref, y_ref):
        a_ref = a2_ref.at[l]
        brb = br_ref[l].astype(bf16)
        bib = bi_ref[l].astype(bf16)

        def project(t, c):
            rows = pl.ds(pl.multiple_of(t * _SSM_ROWS, _SSM_ROWS), _SSM_ROWS)
            ub = u_ref[rows, :].astype(bf16)
            sr_ref[rows, :] = _dot(ub, brb)
            si_ref[rows, :] = _dot(ub, bib)
            return c

        lax.fori_loop(0, nproj, project, 0)
        _scan_in_place(sr_ref, si_ref, a_ref, False)

        crb = cr_ref[l].astype(bf16)
        cib = ci_ref[l].astype(bf16)
        wb = w_ref[...].astype(bf16)

        def read_out(t, c):
            rows = pl.ds(pl.multiple_of(t * _SSM_ROWS, _SSM_ROWS), _SSM_ROWS)
            y = (_dot(sr_ref[rows, :].astype(bf16), crb) - _dot(si_ref[rows, :].astype(bf16), cib)
                 + d_ref[...] * u_ref[rows, :])
            y_ref[rows, :] = y
            z = _dot(_gelu(y).astype(bf16), wb) + b_ref[...]
            out_ref[rows, :] = y * jax.nn.sigmoid(z)
            return c

        lax.fori_loop(0, nproj, read_out, 0)

    vm = pl.BlockSpec(memory_space=pltpu.VMEM)
    return pl.pallas_call(
        body, name="ssm_fwd", in_specs=[vm] * 9, out_specs=[vm] * 4,
        out_shape=[jax.ShapeDtypeStruct((S, D_STATE), f32)] * 2 + [jax.ShapeDtypeStruct((S, D_SSM), f32)] * 2,
        compiler_params=_cp(None, 48),
    )(u, bre, bim, a2, cre, cim, dskip, glu_w, glu_b)


def _ssm_out_bwd(dout, y, u, dskip, glu_w, glu_b):
    S = u.shape[0]
    ts = TOK_TILE

    def body(do_ref, y_ref, u_ref, d_ref, w_ref, b_ref, dy_ref, du_ref, dd_ref, dgb_ref, dgw_ref):
        @pl.when(pl.program_id(0) == 0)
        def _():
            for r in (dd_ref, dgb_ref, dgw_ref):
                r[...] = jnp.zeros_like(r)

        y = y_ref[...]
        dout = do_ref[...]
        wb = w_ref[...].astype(bf16)
        ge = _gelu(y).astype(bf16)
        sz = jax.nn.sigmoid(_dot(ge, wb) + b_ref[...])
        dz = dout * y * sz * (1.0 - sz)
        dzb = dz.astype(bf16)
        dgb_ref[...] += jnp.sum(dz, 0, keepdims=True)
        dgw_ref[...] += _dot_tn(ge, dzb)
        dy = dout * sz + _gelu_grad(y) * _dot_nt(dzb, wb)
        uv = u_ref[...]
        dd_ref[...] += jnp.sum(dy * uv, 0, keepdims=True)
        du_ref[...] = dy * d_ref[...]
        dy_ref[...] = dy

    ch = pl.BlockSpec((ts, D_SSM), lambda i: (i, 0))
    return pl.pallas_call(
        body, name="ssm_out_bwd", grid=(S // ts,),
        in_specs=[ch, ch, ch, _full((1, D_SSM)), _full((D_SSM, D_SSM)), _full((1, D_SSM))],
        out_specs=[ch, ch, _full((1, D_SSM)), _full((1, D_SSM)), _full((D_SSM, D_SSM))],
        out_shape=[jax.ShapeDtypeStruct((S, D_SSM), f32), jax.ShapeDtypeStruct((S, D_SSM), f32),
                   jax.ShapeDtypeStruct((1, D_SSM), f32), jax.ShapeDtypeStruct((1, D_SSM), f32),
                   jax.ShapeDtypeStruct((D_SSM, D_SSM), f32)],
        compiler_params=_cp(("arbitrary",), 40),
    )(*_hbm(dout, y, u, dskip, glu_w, glu_b))


def _ssm_states_bwd(dy, du_skip, u, sr, si, cre, cim, bre, bim, a2, l):
    S = u.shape[0]
    N = D_STATE
    nst = S // SCAN_SEG
    nproj = S // _SSM_ROWS

    def body(dy_ref, dus_ref, u_ref, sr_ref, si_ref, cr_ref, ci_ref, br_ref, bi_ref, a2_ref,
             du_ref, dbr_ref, dbi_ref, da_ref, dcr_ref, dci_ref, lr_ref, li_ref):
        a_ref = a2_ref.at[l]
        crb = cr_ref[l].astype(bf16)
        cib = ci_ref[l].astype(bf16)
        dcr_ref[...] = jnp.zeros_like(dcr_ref)
        dci_ref[...] = jnp.zeros_like(dci_ref)

        def project(t, c):
            rows = pl.ds(pl.multiple_of(t * _SSM_ROWS, _SSM_ROWS), _SSM_ROWS)
            dyb = dy_ref[rows, :].astype(bf16)
            lr_ref[rows, :] = _dot_nt(dyb, crb)
            li_ref[rows, :] = -_dot_nt(dyb, cib)
            dcr_ref[...] += _dot_tn(sr_ref[rows, :].astype(bf16), dyb)
            dci_ref[...] -= _dot_tn(si_ref[rows, :].astype(bf16), dyb)
            return c

        lax.fori_loop(0, nproj, project, 0)
        _scan_in_place(lr_ref, li_ref, a_ref, True)

        row = lax.broadcasted_iota(jnp.int32, (SCAN_SEG, N), 0)
        last = pl.ds((nst - 1) * SCAN_SEG, SCAN_SEG)
        pr = jnp.where(row == 0, 0.0, pltpu.roll(sr_ref[last, :], 1, axis=0))
        pi = jnp.where(row == 0, 0.0, pltpu.roll(si_ref[last, :], 1, axis=0))
        first = pl.ds(0, SCAN_SEG)
        acc_r = lr_ref[first, :] * pr + li_ref[first, :] * pi
        acc_i = li_ref[first, :] * pr - lr_ref[first, :] * pi

        def step(t, c):
            acc_r, acc_i = c
            rows = pl.ds(pl.multiple_of(t * SCAN_SEG, SCAN_SEG), SCAN_SEG)
            prev = pl.ds(pl.multiple_of((t - 1) * SCAN_SEG, SCAN_SEG), SCAN_SEG)
            lrv, liv, srv, siv = lr_ref[rows, :], li_ref[rows, :], sr_ref[prev, :], si_ref[prev, :]
            return acc_r + lrv * srv + liv * siv, acc_i + liv * srv - lrv * siv

        acc_r, acc_i = lax.fori_loop(1, nst, step, (acc_r, acc_i))
        da_ref[0:1, :] = jnp.sum(acc_r, 0, keepdims=True)
        da_ref[1:2, :] = jnp.sum(acc_i, 0, keepdims=True)

        brb = br_ref[l].astype(bf16)
        bib = bi_ref[l].astype(bf16)
        dbr_ref[...] = jnp.zeros_like(dbr_ref)
        dbi_ref[...] = jnp.zeros_like(dbi_ref)

        def back(t, c):
            rows = pl.ds(pl.multiple_of(t * _SSM_ROWS, _SSM_ROWS), _SSM_ROWS)
            lrb = lr_ref[rows, :].astype(bf16)
            lib = li_ref[rows, :].astype(bf16)
            du_ref[rows, :] = dus_ref[rows, :] + _dot_nt(lrb, brb) + _dot_nt(lib, bib)
            ub = u_ref[rows, :].astype(bf16)
            dbr_ref[...] += _dot_tn(ub, lrb)
            dbi_ref[...] += _dot_tn(ub, lib)
            return c

        lax.fori_loop(0, nproj, back, 0)

    vm = pl.BlockSpec(memory_space=pltpu.VMEM)
    return pl.pallas_call(
        body, name="ssm_states_bwd", in_specs=[vm] * 10, out_specs=[vm] * 6,
        out_shape=[jax.ShapeDtypeStruct((S, D_SSM), f32), jax.ShapeDtypeStruct((D_SSM, D_STATE), f32),
                   jax.ShapeDtypeStruct((D_SSM, D_STATE), f32), jax.ShapeDtypeStruct((2, D_STATE), f32),
                   jax.ShapeDtypeStruct((D_STATE, D_SSM), f32), jax.ShapeDtypeStruct((D_STATE, D_SSM), f32)],
        scratch_shapes=[pltpu.VMEM((S, D_STATE), f32), pltpu.VMEM((S, D_STATE), f32)],
        compiler_params=_cp(None, 56),
    )(dy, du_skip, u, sr, si, cre, cim, bre, bim, a2)


_POOL_TILE = 256


def _window_sums(xt, back):
    n = xt.shape[0]
    out = []
    ws = xt
    for k in (1, 2, 4, 8):
        ws = ws + pltpu.roll(ws, k if back else n - k, axis=0)
        out.append(ws)
    return out


def _pool_count(r0, w):
    t = r0 + lax.broadcasted_iota(jnp.int32, (_POOL_TILE, POOL_GROUP), 0)
    return jnp.minimum(t + 1, w).astype(f32)


def _pool_fwd(u_pad, pool_w, pool_scale):
    S = u_pad.shape[0] - POOL_HALO
    nt = S // _POOL_TILE

    def body(u_ref, w_ref, sc_ref, y_ref):
        def tile(t, c):
            r0 = pl.multiple_of(t * _POOL_TILE, _POOL_TILE)
            for g, w in enumerate(POOL_WINDOWS):
                cs = pl.ds(POOL_GROUP * g, POOL_GROUP)
                xt = u_ref[pl.ds(r0, _POOL_TILE + POOL_HALO), cs]
                ws = _window_sums(xt, True)[g][POOL_HALO:, :]
                pooled = ws / _pool_count(r0, w) - xt[POOL_HALO:, :]
                y_ref[pl.ds(r0, _POOL_TILE), cs] = _dot(pooled.astype(bf16), w_ref[g].astype(bf16)) * sc_ref[:, cs]
            return c
        lax.fori_loop(0, nt, tile, 0)

    vm = pl.BlockSpec(memory_space=pltpu.VMEM)
    return pl.pallas_call(
        body, name="pool_fwd", in_specs=[vm, vm, vm], out_specs=vm,
        out_shape=jax.ShapeDtypeStruct((S, D_POOL), f32),
    )(u_pad, pool_w, pool_scale)


def _pool_bwd(dy_pad, u_pad, pool_w, pool_scale):
    S = u_pad.shape[0] - POOL_HALO
    nt = S // _POOL_TILE
    n = _POOL_TILE + POOL_HALO

    def body(dy_ref, u_ref, w_ref, sc_ref, du_ref, dw_ref, dsc_ref):
        dw_ref[...] = jnp.zeros_like(dw_ref)
        dsc_ref[...] = jnp.zeros_like(dsc_ref)

        def tile(t, c):
            r0 = pl.multiple_of(t * _POOL_TILE, _POOL_TILE)
            for g, w in enumerate(POOL_WINDOWS):
                cs = pl.ds(POOL_GROUP * g, POOL_GROUP)
                wb = w_ref[g].astype(bf16)
                xt = u_ref[pl.ds(r0, n), cs]
                pooled = (_window_sums(xt, True)[g][POOL_HALO:, :] / _pool_count(r0, w) - xt[POOL_HALO:, :]).astype(bf16)
                dy = dy_ref[pl.ds(r0, _POOL_TILE), cs]
                dsc_ref[:, cs] += jnp.sum(dy * _dot(pooled, wb), 0, keepdims=True)
                dw_ref[g] += _dot_tn(pooled, (dy * sc_ref[:, cs]).astype(bf16))
                dyh = (dy_ref[pl.ds(r0, n), cs] * sc_ref[:, cs]).astype(bf16)
                dpl = _dot_nt(dyh, wb)
                cnt = jnp.minimum(r0 + lax.broadcasted_iota(jnp.int32, (n, POOL_GROUP), 0) + 1, w).astype(f32)
                lead = _window_sums(dpl / cnt, False)[g]
                du_ref[pl.ds(r0, _POOL_TILE), cs] = lead[:_POOL_TILE, :] - dpl[:_POOL_TILE, :]
            return c
        lax.fori_loop(0, nt, tile, 0)

    vm = pl.BlockSpec(memory_space=pltpu.VMEM)
    return pl.pallas_call(
        body, name="pool_bwd", in_specs=[vm, vm, vm, vm], out_specs=[vm, vm, vm],
        out_shape=[jax.ShapeDtypeStruct((S, D_POOL), f32), jax.ShapeDtypeStruct((4, POOL_GROUP, POOL_GROUP), f32),
                   jax.ShapeDtypeStruct((1, D_POOL), f32)],
    )(dy_pad, u_pad, pool_w, pool_scale)


def _loss_head(y, target):
    S, D = y.shape
    ts = TOK_TILE

    def body(y_ref, t_ref, loss_ref, dy_ref):
        @pl.when(pl.program_id(0) == 0)
        def _():
            loss_ref[...] = jnp.zeros_like(loss_ref)

        d = y_ref[...] - t_ref[...]
        dy_ref[...] = d * (1.0 / D)
        loss_ref[...] += 0.5 * jnp.sum(jnp.sum(d * d, -1, keepdims=True) * (1.0 / D), 0, keepdims=True)

    tok = pl.BlockSpec((ts, D), lambda i: (i, 0))
    return pl.pallas_call(
        body, name="loss_head", grid=(S // ts,),
        in_specs=[tok, tok], out_specs=[_full((1, 1)), tok],
        out_shape=[jax.ShapeDtypeStruct((1, 1), f32), jax.ShapeDtypeStruct((S, D), f32)],
        compiler_params=_cp(("arbitrary",)),
    )(*_hbm(y, target))


_ADA_COLS = 768


def _ada_fwd(c_all, ada_w, ada_b_cols):
    L, D, N = ada_w.shape
    B = c_all.shape[0]

    def body(c_ref, w_ref, b_ref, out_ref):
        cv = c_ref[...]
        cond = (cv * jax.nn.sigmoid(cv)).astype(bf16)
        out_ref[0] = _dot(cond, w_ref[0].astype(bf16)) + b_ref[0]

    return pl.pallas_call(
        body, name="ada_fwd", grid=(L, N // _ADA_COLS),
        in_specs=[_full((B, D)), pl.BlockSpec((1, D, _ADA_COLS), lambda l, j: (l, 0, j)),
                  pl.BlockSpec((1, 1, _ADA_COLS), lambda l, j: (l, 0, j))],
        out_specs=pl.BlockSpec((1, B, _ADA_COLS), lambda l, j: (l, 0, j)),
        out_shape=jax.ShapeDtypeStruct((L, B, N), f32),
        compiler_params=_cp(("parallel", "parallel")),
    )(c_all, ada_w, ada_b_cols)


def _ada_wgrad(c_all_t, dmod_cols):
    D, B = c_all_t.shape
    L, _, N = dmod_cols.shape

    def body(ct_ref, dm_ref, out_ref):
        cv = ct_ref[...]
        cond = cv * jax.nn.sigmoid(cv)
        acc = cond[:, 0:1] * dm_ref[0, 0:1, :]
        for b in range(1, B):
            acc = acc + cond[:, b:b + 1] * dm_ref[0, b:b + 1, :]
        out_ref[0] = acc

    return pl.pallas_call(
        body, name="ada_wgrad", grid=(L, N // _ADA_COLS),
        in_specs=[_full((D, B)), pl.BlockSpec((1, B, _ADA_COLS), lambda l, j: (l, 0, j))],
        out_specs=pl.BlockSpec((1, D, _ADA_COLS), lambda l, j: (l, 0, j)),
        out_shape=jax.ShapeDtypeStruct((L, D, N), f32),
        compiler_params=_cp(("parallel", "parallel")),
    )(c_all_t, dmod_cols)


def _adam_math(w, g, m, v):
    m = ADAM_B1 * m + (1.0 - ADAM_B1) * g
    v = ADAM_B2 * v + (1.0 - ADAM_B2) * (g * g)
    m_hat = m / (1.0 - ADAM_B1 ** ADAM_STEP)
    v_hat = v / (1.0 - ADAM_B2 ** ADAM_STEP)
    delta = -ADAM_LR * (m_hat / (jnp.sqrt(v_hat) + ADAM_EPS) + ADAM_WD * w)
    return delta, m, v


def _adamw(w, m, v, g, row_tile, row0=0, outs=None):
    R, C = w.shape
    b0 = row0 // row_tile

    def body(w_ref, m_ref, v_ref, g_ref, _0, _1, _2, _3, g_out, d_out, m_out, v_out):
        gv = g_ref[...]
        delta, mn, vn = _adam_math(w_ref[...], gv, m_ref[...], v_ref[...])
        g_out[...] = gv
        d_out[...] = delta
        m_out[...] = mn
        v_out[...] = vn

    pspec = pl.BlockSpec((row_tile, C), lambda i: (b0 + i, 0))
    gspec = pl.BlockSpec((row_tile, C), lambda i: (i, 0))
    anyspec = pl.BlockSpec(memory_space=pl.ANY)
    shp = jax.ShapeDtypeStruct((R, C), f32)
    if outs is None:
        outs = [lax.empty((R, C), f32) for _ in range(4)]
    return pl.pallas_call(
        body, name="adamw", grid=(g.shape[0] // row_tile,),
        in_specs=[pspec] * 3 + [gspec] + [anyspec] * 4, out_specs=[pspec] * 4, out_shape=[shp] * 4,
        input_output_aliases={4: 0, 5: 1, 6: 2, 7: 3},
        compiler_params=_cp(("parallel",), 40),
    )(*_hbm(w, m, v, g, *outs))


def _pair_sum(g5s, gots, pc):
    n = len(g5s)

    def body(pc_ref, *refs):
        for own, got, out in zip(refs[:n], refs[n:2 * n], refs[2 * n:]):
            out[0, 0] = (own[0, 0, 0].astype(f32) + got[0, 0].astype(f32)).astype(bf16)

    def half(g):
        return pl.BlockSpec((1, 1) + g.shape[-2:], lambda p, pc: (p, 0, 0, 0))

    gs = pltpu.PrefetchScalarGridSpec(
        num_scalar_prefetch=1, grid=(N_CHIPS,),
        in_specs=[pl.BlockSpec((1, 1, 1) + g.shape[-2:], lambda p, pc: (p, 0, pc[1], 0, 0)) for g in g5s]
        + [half(g) for g in gots],
        out_specs=[half(g) for g in gots],
    )
    return pl.pallas_call(
        body, name="pair_sum", grid_spec=gs, out_shape=[jax.ShapeDtypeStruct(g.shape, bf16) for g in gots],
        compiler_params=_cp(("parallel",), 48),
    )(pc, *_hbm(*g5s, *gots))


_SUM_STEPS = 2


def _sum_shards(hsums, recvs, pc):
    n = len(hsums)

    def body(pc_ref, *refs):
        for own, got, out in zip(refs[:n], refs[n:2 * n], refs[2 * n:]):
            acc = own[0, 0].astype(f32)
            for j in range(3):
                acc = acc + got[j, 0].astype(f32)
            out[0, 0] = acc

    def rows(h):
        return (h.shape[2] // _SUM_STEPS, h.shape[3])

    gs = pltpu.PrefetchScalarGridSpec(
        num_scalar_prefetch=1, grid=(_SUM_STEPS,),
        in_specs=[pl.BlockSpec((1, 1) + rows(h), lambda i, pc: (pc[0], 0, i, 0)) for h in hsums]
        + [pl.BlockSpec((3, 1) + rows(h), lambda i, pc: (0, 0, i, 0)) for h in hsums],
        out_specs=[pl.BlockSpec((1, 1) + rows(h), lambda i, pc: (0, pc[1], i, 0)) for h in hsums],
    )
    return pl.pallas_call(
        body, name="sum_shards", grid_spec=gs,
        out_shape=[jax.ShapeDtypeStruct((1, 2) + h.shape[2:], f32) for h in hsums],
        compiler_params=_cp(("parallel",), 48),
    )(pc, *_hbm(*hsums, *recvs))


def _sum8(packs):
    _, R, C = packs.shape
    tr = R // 8 if R % 64 == 0 else R

    def body(p_ref, out_ref):
        acc = p_ref[0]
        for d in range(1, 8):
            acc = acc + p_ref[d]
        out_ref[...] = acc

    return pl.pallas_call(
        body, name="sum8", grid=(R // tr,),
        in_specs=[pl.BlockSpec((8, tr, C), lambda i: (0, i, 0))],
        out_specs=pl.BlockSpec((tr, C), lambda i: (i, 0)),
        out_shape=jax.ShapeDtypeStruct((R, C), f32),
        compiler_params=_cp(("parallel",)),
    )(packs)


def _allgather8(x_shard):
    m_per, n = x_shard.shape

    def body(x_ref, out_ref, send_sems, recv_sems, local_sem):
        x, y, c = lax.axis_index("x"), lax.axis_index("y"), lax.axis_index("c")
        me, sibling = (x, y, c), (x, y, 1 - c)
        chips = [(1 - x, y), (x, 1 - y), (1 - x, 1 - y)]

        def rows(px, py, pc):
            return out_ref.at[pl.ds((4 * px + 2 * py + pc) * m_per, m_per), :]

        def copy(k, block, to, src=None):
            return pltpu.make_async_remote_copy(
                src_ref=rows(*block) if src is None else src, dst_ref=rows(*block),
                send_sem=send_sems.at[k], recv_sem=recv_sems.at[k], device_id=to, device_id_type=MESH)

        mine = pltpu.make_async_copy(x_ref, rows(*me), local_sem)
        mine.start()
        first = [copy(0, me, sibling, src=x_ref)]
        first += [copy(1 + j, me, (*chip, c), src=x_ref) for j, chip in enumerate(chips)]
        for cp in first:
            cp.start()
        passed = [copy(4 + j, (*chip, c), sibling) for j, chip in enumerate(chips)]
        for j, chip in enumerate(chips):
            copy(1 + j, (*chip, c), me).wait_recv()
            passed[j].start()
        copy(0, sibling, me).wait_recv()
        for j, chip in enumerate(chips):
            copy(4 + j, (*chip, 1 - c), me).wait_recv()
        for cp in first + passed:
            cp.wait_send()
        mine.wait()

    return pl.pallas_call(
        body, name="allgather8",
        out_shape=jax.ShapeDtypeStruct((8 * m_per, n), x_shard.dtype),
        in_specs=[pl.BlockSpec(memory_space=pltpu.VMEM)],
        out_specs=pl.BlockSpec(memory_space=pltpu.VMEM),
        scratch_shapes=[pltpu.SemaphoreType.DMA((7,)), pltpu.SemaphoreType.DMA((7,)), pltpu.SemaphoreType.DMA],
        compiler_params=_cp(None, 48),
    )(x_shard)


def _other_chips():
    x, y = lax.axis_index("x"), lax.axis_index("y")
    return [(1 - x, y), (x, 1 - y), (1 - x, 1 - y)]


_HBM = pl.BlockSpec(memory_space=pltpu.HBM)
_SEM = pl.BlockSpec(memory_space=pltpu.SEMAPHORE)
_EFFECT = pltpu.SideEffectType.DATAFLOW_SIDE_EFFECTING


def _gather_copies(srcs, lands, send_sems, recv_sems):
    x, y, c = lax.axis_index("x"), lax.axis_index("y"), lax.axis_index("c")
    return [pltpu.make_async_remote_copy(
        src_ref=srcs[a].at[:, c], dst_ref=lands[a].at[2 * x + y, :, c], send_sem=send_sems.at[3 * a + j],
        recv_sem=recv_sems.at[3 * a + j], device_id=(cx, cy, c), device_id_type=MESH)
        for a in range(len(srcs)) for j, (cx, cy) in enumerate(_other_chips())]


def _gather_start(chunks, after, name):
    sizes = [len(srcs) for srcs, _ in chunks]
    flat = [t for srcs, lands in chunks for t in list(srcs) + list(lands)]
    nflat = len(flat)
    nsem = 2 * len(chunks)

    def body(*refs):
        ins, sems, token = refs[:nflat], refs[nflat + 1:nflat + 1 + nsem], refs[-1]
        off = 0
        for k, n in enumerate(sizes):
            for cp in _gather_copies(ins[off:off + n], ins[off + n:off + 2 * n], sems[2 * k], sems[2 * k + 1]):
                cp.start()
            off += 2 * n
        token[...] = jnp.zeros_like(token)

    res = pl.pallas_call(
        body, name=name,
        out_shape=[pltpu.SemaphoreType.DMA((3 * n,)) for n in sizes for _ in range(2)]
        + [pltpu.HBM(t.shape, t.dtype) for t in flat] + [jax.ShapeDtypeStruct((8, 128), f32)],
        in_specs=[_HBM] * nflat + [pl.BlockSpec(memory_space=pl.ANY)],
        out_specs=[_SEM] * nsem + [_HBM] * nflat + [pl.BlockSpec(memory_space=pltpu.VMEM)],
        input_output_aliases={i: nsem + i for i in range(nflat)},
        compiler_params=pltpu.CompilerParams(has_side_effects=_EFFECT),
    )(*[pltpu.with_memory_space_constraint(t, pltpu.HBM) for t in flat], after)
    out, off = [], nsem
    for k, n in enumerate(sizes):
        out.append((res[2 * k], res[2 * k + 1], res[off:off + n], res[off + n:off + 2 * n]))
        off += 2 * n
    return out, res[-1]


def _gather_wait(send_sems, recv_sems, srcs, lands, after, name):
    n = len(srcs)

    def body(*refs):
        for cp in _gather_copies(refs[:n], refs[n:2 * n], refs[2 * n], refs[2 * n + 1]):
            cp.wait_send()
            cp.wait_recv()

    res = pl.pallas_call(
        body, name=name,
        out_shape=[pltpu.HBM(t.shape, t.dtype) for t in list(srcs) + list(lands)],
        in_specs=[_HBM] * (2 * n) + [_SEM, _SEM] + [pl.BlockSpec(memory_space=pl.ANY)] * len(after),
        out_specs=[_HBM] * (2 * n),
        input_output_aliases={i: i for i in range(2 * n)},
        compiler_params=pltpu.CompilerParams(has_side_effects=_EFFECT),
    )(*srcs, *lands, send_sems, recv_sems, *after)
    return res[n:]


def _split_start(make_copies, arrays, nsem, name, after=()):
    n, na = len(arrays), len(after)

    def body(*refs):
        for cp in make_copies(refs[:n], refs[n + na], refs[n + na + 1]):
            cp.start()
        refs[-1][...] = jnp.zeros_like(refs[-1])

    res = pl.pallas_call(
        body, name=name,
        out_shape=[pltpu.SemaphoreType.DMA((nsem,)), pltpu.SemaphoreType.DMA((nsem,))]
        + [pltpu.HBM(t.shape, t.dtype) for t in arrays] + [jax.ShapeDtypeStruct((8, 128), f32)],
        in_specs=[_HBM] * n + [pl.BlockSpec(memory_space=pl.ANY)] * na,
        out_specs=[_SEM, _SEM] + [_HBM] * n + [pl.BlockSpec(memory_space=pltpu.VMEM)],
        input_output_aliases={i: i + 2 for i in range(n)},
        compiler_params=pltpu.CompilerParams(has_side_effects=_EFFECT),
    )(*[pltpu.with_memory_space_constraint(t, pltpu.HBM) for t in arrays], *after)
    return (res[0], res[1], res[2:2 + n]), res[-1]


def _split_wait(make_copies, send_sems, recv_sems, arrays, after, name):
    n = len(arrays)

    def body(*refs):
        for cp in make_copies(refs[:n], refs[n], refs[n + 1]):
            cp.wait_send()
            cp.wait_recv()

    return pl.pallas_call(
        body, name=name,
        out_shape=[pltpu.HBM(t.shape, t.dtype) for t in arrays],
        in_specs=[_HBM] * n + [_SEM, _SEM] + [pl.BlockSpec(memory_space=pl.ANY)] * len(after),
        out_specs=[_HBM] * n, input_output_aliases={i: i for i in range(n)},
        compiler_params=pltpu.CompilerParams(has_side_effects=_EFFECT),
    )(*arrays, send_sems, recv_sems, *after)


def _sibling():
    return lax.axis_index("x"), lax.axis_index("y"), 1 - lax.axis_index("c")


def _forward_copies(lands, send_sems, recv_sems):
    c = lax.axis_index("c")
    return [pltpu.make_async_remote_copy(
        src_ref=lands[a].at[2 * cx + cy, :, c], dst_ref=lands[a].at[2 * cx + cy, :, c], send_sem=send_sems.at[3 * a + j],
        recv_sem=recv_sems.at[3 * a + j], device_id=_sibling(), device_id_type=MESH)
        for a in range(len(lands)) for j, (cx, cy) in enumerate(_other_chips())]


def _swap_copies(fulls, send_sems, recv_sems):
    c = lax.axis_index("c")
    return [pltpu.make_async_remote_copy(src_ref=t.at[:, c], dst_ref=t.at[:, c], send_sem=send_sems.at[a],
                                         recv_sem=recv_sems.at[a], device_id=_sibling(), device_id_type=MESH)
            for a, t in enumerate(fulls)]


def _pair_copies(refs, send_sems, recv_sems):
    n = len(refs) // 2
    c = lax.axis_index("c")
    return [pltpu.make_async_remote_copy(src_ref=refs[a].at[:, :, 1 - c], dst_ref=refs[n + a], send_sem=send_sems.at[a],
                                         recv_sem=recv_sems.at[a], device_id=_sibling(), device_id_type=MESH)
            for a in range(n)]


def _scatter_copies(srcs, lands, send_sems, recv_sems):
    c = lax.axis_index("c")
    return [pltpu.make_async_remote_copy(
        src_ref=srcs[a].at[2 * cx + cy], dst_ref=lands[a].at[j], send_sem=send_sems.at[3 * a + j],
        recv_sem=recv_sems.at[3 * a + j], device_id=(cx, cy, c), device_id_type=MESH)
        for a in range(len(srcs)) for j, (cx, cy) in enumerate(_other_chips())]


def _scatter_start(hsums, name, after=()):
    n = len(hsums)
    na = len(after)

    def body(*refs):
        srcs, lands = refs[:n], refs[n:2 * n]
        send_sems, recv_sems = refs[2 * n + na], refs[2 * n + na + 1]
        for cp in _scatter_copies(srcs, lands, send_sems, recv_sems):
            cp.start()
        refs[-1][...] = jnp.zeros_like(refs[-1])

    lands = [lax.empty((3,) + g.shape[1:], g.dtype) for g in hsums]
    res = pl.pallas_call(
        body, name=name,
        out_shape=[pltpu.SemaphoreType.DMA((3 * n,)), pltpu.SemaphoreType.DMA((3 * n,))]
        + [pltpu.HBM(g.shape, g.dtype) for g in hsums] + [pltpu.HBM(g.shape, g.dtype) for g in lands]
        + [jax.ShapeDtypeStruct((8, 128), f32)],
        in_specs=[_HBM] * (2 * n) + [pl.BlockSpec(memory_space=pl.ANY)] * na,
        out_specs=[_SEM, _SEM] + [_HBM] * (2 * n) + [pl.BlockSpec(memory_space=pltpu.VMEM)],
        input_output_aliases={i: i + 2 for i in range(2 * n)},
        compiler_params=pltpu.CompilerParams(has_side_effects=_EFFECT),
    )(*[pltpu.with_memory_space_constraint(t, pltpu.HBM) for t in list(hsums) + lands], *after)
    return (res[0], res[1], res[2:2 + n], res[2 + n:2 + 2 * n]), res[-1]


def _scatter_wait(send_sems, recv_sems, srcs, lands, after, name):
    n = len(srcs)
    extra = list(after)

    def body(*refs):
        s_refs, l_refs = refs[:n], refs[n:2 * n]
        ss, rs = refs[2 * n], refs[2 * n + 1]
        for cp in _scatter_copies(s_refs, l_refs, ss, rs):
            cp.wait_send()
            cp.wait_recv()

    res = pl.pallas_call(
        body, name=name,
        out_shape=[pltpu.HBM(g.shape, g.dtype) for g in srcs] + [pltpu.HBM(g.shape, g.dtype) for g in lands],
        in_specs=[_HBM] * (2 * n) + [_SEM, _SEM] + [pl.BlockSpec(memory_space=pl.ANY)] * len(extra),
        out_specs=[_HBM] * (2 * n),
        input_output_aliases={i: i for i in range(2 * n)},
        compiler_params=pltpu.CompilerParams(has_side_effects=_EFFECT),
    )(*srcs, *lands, send_sems, recv_sems, *extra)
    return res[:n], res[n:]


def _plane_copies(src, land, send_sems, recv_sems):
    x, y, c = lax.axis_index("x"), lax.axis_index("y"), lax.axis_index("c")
    return [pltpu.make_async_remote_copy(src_ref=src, dst_ref=land.at[2 * x + y, c], send_sem=send_sems.at[j],
                                         recv_sem=recv_sems.at[j], device_id=(cx, cy, c), device_id_type=MESH)
            for j, (cx, cy) in enumerate(_other_chips())]


def _plane_start(pack, land, name):
    def body(src, lnd, send_sems, recv_sems, _s, _l, token):
        for cp in _plane_copies(src, lnd, send_sems, recv_sems):
            cp.start()
        token[...] = jnp.zeros_like(token)

    res = pl.pallas_call(
        body, name=name,
        out_shape=[pltpu.SemaphoreType.DMA((3,)), pltpu.SemaphoreType.DMA((3,)), pltpu.HBM(pack.shape, pack.dtype),
                   pltpu.HBM(land.shape, land.dtype), jax.ShapeDtypeStruct((8, 128), f32)],
        in_specs=[_HBM, _HBM], out_specs=[_SEM, _SEM, _HBM, _HBM, pl.BlockSpec(memory_space=pltpu.VMEM)],
        input_output_aliases={0: 2, 1: 3},
        compiler_params=pltpu.CompilerParams(has_side_effects=_EFFECT),
    )(pltpu.with_memory_space_constraint(pack, pltpu.HBM), pltpu.with_memory_space_constraint(land, pltpu.HBM))
    return res[:4], res[4]


def _plane_wait(send_sems, recv_sems, pack, land, after, name):
    def body(src, lnd, ss, rs, *_):
        for cp in _plane_copies(src, lnd, ss, rs):
            cp.wait_send()
            cp.wait_recv()

    return pl.pallas_call(
        body, name=name,
        out_shape=[pltpu.HBM(pack.shape, pack.dtype), pltpu.HBM(land.shape, land.dtype)],
        in_specs=[_HBM, _HBM, _SEM, _SEM] + [pl.BlockSpec(memory_space=pl.ANY)] * len(after),
        out_specs=[_HBM, _HBM], input_output_aliases={0: 0, 1: 1},
        compiler_params=pltpu.CompilerParams(has_side_effects=_EFFECT),
    )(pack, land, send_sems, recv_sems, *after)[1]


def _swap_halves(fulls):
    n = len(fulls)

    def body(*refs):
        ins, outs = refs[:n], refs[n:2 * n]
        send_sems, recv_sems = refs[2 * n:]
        c = lax.axis_index("c")
        sibling = (lax.axis_index("x"), lax.axis_index("y"), 1 - c)
        copies = []
        for a in range(n):
            cp = pltpu.make_async_remote_copy(src_ref=outs[a].at[:, c], dst_ref=outs[a].at[:, c], send_sem=send_sems.at[a],
                                              recv_sem=recv_sems.at[a], device_id=sibling, device_id_type=MESH)
            cp.start()
            copies.append(cp)
        for a, cp in enumerate(copies):
            cp.wait_send()
            theirs = outs[a].at[:, 1 - c]
            pltpu.make_async_remote_copy(src_ref=theirs, dst_ref=theirs, send_sem=send_sems.at[a], recv_sem=recv_sems.at[a],
                                         device_id=sibling, device_id_type=MESH).wait_recv()

    hbm = pl.BlockSpec(memory_space=pl.ANY)
    return pl.pallas_call(
        body, name="swap_halves",
        out_shape=[jax.ShapeDtypeStruct(p.shape, p.dtype) for p in fulls],
        in_specs=[hbm] * n, out_specs=[hbm] * n,
        input_output_aliases={a: a for a in range(n)},
        scratch_shapes=[pltpu.SemaphoreType.DMA((n,)), pltpu.SemaphoreType.DMA((n,))],
    )(*fulls)


def _to_segments(t):
    s, c = t.shape
    return t.reshape(SCAN_SEG, s // SCAN_SEG, c).transpose(1, 0, 2).reshape(s, c)


def _from_segments(t):
    s, c = t.shape
    return t.reshape(s // SCAN_SEG, SCAN_SEG, c).transpose(1, 0, 2).reshape(s, c)


def _ssm_operators(a_re, a_im, log_dt, b_re, b_im, c_re, c_im):
    lam = lax.complex(a_re, a_im)
    dt = jnp.exp(log_dt)[:, None]
    a_bar = jnp.exp(lam * dt)
    b_bar = ((a_bar - 1.0) / lam)[:, :, None] * lax.complex(b_re, b_im)
    eye = jnp.eye(N_GROUPS, dtype=f32)

    def embed_b(t):
        return (jnp.transpose(t, (0, 2, 1))[:, :, None, :] * eye[:, None, :, None]).reshape(D_SSM, D_STATE)

    def embed_c(t):
        return (jnp.transpose(t, (0, 2, 1))[:, :, None, :] * eye[:, None, :, None]).reshape(D_STATE, D_SSM)

    a2 = jnp.stack([a_bar.real.reshape(D_STATE), a_bar.imag.reshape(D_STATE)])
    return a2, embed_b(b_bar.real), embed_b(b_bar.imag), embed_c(c_re), embed_c(c_im)


def _local_step(x, target, mod, small, ffn_weights, mix_weights, grads_done, ffn_bwd_issued):
    table = jnp.asarray(_bucket_table())
    bias = small["att_bias"]
    L = DEPTH
    saved = []
    ssm_names = ("ssm_a_re", "ssm_a_im", "ssm_log_dt", "ssm_b_re", "ssm_b_im", "ssm_c_re", "ssm_c_im")
    ssm_ops_vjp = []
    for l in range(L):
        sv = {}
        m9 = mod[l]
        sv["x0"] = x
        sv["w0"] = ffn_weights(l, 0, x)
        x, sv["f0"], sv["g0"], sv["u0"], sv["a0"], sv["h0"] = _ffn_fwd(x, m9[0:3], *sv["w0"], 0, small["ln_g"][l, 0:1], small["ln_b"][l, 0:1])
        sv["x1"] = x
        sv["w1"] = mix_weights(l, x)
        *qkv, z_rest, sv["h1"] = _mix_in_fwd(x, m9[3:6], sv["w1"][0], 0)
        S = x.shape[0]
        qkv = [t.reshape(3, S, D_ATT) for t in qkv]
        att = [_att_fwd(qkv[b], bias, b) for b in range(3)]
        y_att, lse3 = _att_merge([att[b][0].reshape(d, S // d, D_ATT) for b, d in enumerate(DILATIONS)],
                                 [att[b][1].reshape(d, S // d, _LANES) for b, d in enumerate(DILATIONS)])
        sv.update(qkv=qkv, lse=[a[1] for a in att], lse3=lse3, y_att=y_att)

        ops, ops_vjp = jax.vjp(_ssm_operators, *[small[k][l] for k in ssm_names])
        ssm_ops_vjp.append(ops_vjp)
        a2, bre, bim, cre, cim = [t[None] for t in ops]
        u_ssm = _to_segments(z_rest[:, :D_SSM])
        dskip = small["ssm_d"][l][None, :]
        glu_b = small["glu_b"][l][None, :]
        sr, si, out_seg, y_seg = _ssm_fwd(u_ssm, bre, bim, a2, cre, cim, 0, dskip, small["glu_w"][l], glu_b)
        y_ssm = _from_segments(out_seg)
        sv.update(ssm_ops=(a2, bre, bim, cre, cim), u_ssm=u_ssm, sr=sr, si=si, y_seg=y_seg, y_ssm=y_ssm)

        u_pool = jnp.concatenate([jnp.zeros((POOL_HALO, D_POOL), f32), z_rest[:, D_SSM:]])
        y_pool = _pool_fwd(u_pool, small["pool_w"][l], small["pool_scale"][l][None, :])
        sv.update(u_pool=u_pool, y_pool=y_pool)

        x, sv["ymix"] = _mix_out_fwd(x, y_att, y_ssm, y_pool, m9[3:6], sv["w1"][1], 0, small["ln_g"][l, 1:2], small["ln_b"][l, 1:2])
        sv["x2"] = x
        sv["w2"] = ffn_weights(l, 1, x)
        x, sv["f2"], sv["g2"], sv["u2"], sv["a2"], sv["h2"] = _ffn_fwd(x, m9[6:9], *sv["w2"], 0, small["ln_g"][l, 2:3], small["ln_b"][l, 2:3])
        saved.append(sv)

    loss, dx = _loss_head(x, target)

    dmod = [None] * L
    dln_g = [None] * L
    dln_b = [None] * L
    sg = {k: [None] * L for k in ssm_names + ("ssm_d", "glu_w", "glu_b", "pool_w", "pool_scale")}
    dbias_tot = None
    order_after = jnp.zeros((), f32)
    for l in reversed(range(L)):
        sv = saved[l]
        m9 = mod[l] + order_after

        def fresh(like):
            return [lax.empty(t.shape, bf16) for t in like]

        dx, dg, du, df, dm2, dlg2, dlb2 = _ffn_bwd(dx, sv["x2"], sv["f2"], sv["g2"], sv["u2"], m9[6:9], *sv["w2"], 0,
                                                     small["ln_g"][l, 2:3])
        m9 = m9 + ffn_bwd_issued(l, 1, dx)
        g_ffn1 = _ffn_wgrad(sv["h2"], dg, du, sv["a2"], df,*fresh(sv["w2"]), 0)
        dxr, d_att, d_ssm, d_pool, dgate1, dlg1, dlb1, g_w_out = _mix_out_bwd(
            dx, sv["x1"], sv["ymix"], sv["y_att"], sv["y_ssm"], sv["y_pool"], m9[3:6], sv["w1"][1], 0, small["ln_g"][l, 1:2],
            fresh(sv["w1"])[1])
        S = d_att.shape[0]
        merged = _att_merge_bwd(d_att, sv["y_att"], sv["lse3"])
        dqkv, dbias = [], []
        for b, d in enumerate(DILATIONS):
            dq_b, db_b = _att_bwd(sv["qkv"][b], merged[b].reshape(S, D_ATT), sv["lse"][b], merged[3 + b].reshape(S, _LANES), bias, b)
            dqkv.append(dq_b.reshape(3, d, S // d, D_ATT))
            dbias.append(db_b)
        dbias = jnp.stack(dbias)
        dbias_tot = dbias if dbias_tot is None else dbias_tot + dbias
        d_seg = _to_segments(d_ssm)
        dskip = small["ssm_d"][l][None, :]
        glu_b = small["glu_b"][l][None, :]
        dy_seg, du_skip, dd, dglu_b, dglu_w = _ssm_out_bwd(d_seg, sv["y_seg"], sv["u_ssm"], dskip, small["glu_w"][l], glu_b)
        a2, bre, bim, cre, cim = sv["ssm_ops"]
        du_seg, dbre, dbim, da2, dcre, dcim = _ssm_states_bwd(dy_seg, du_skip, sv["u_ssm"], sv["sr"], sv["si"], cre, cim,
                                                              bre, bim, a2, 0)
        for k, t in zip(ssm_names, ssm_ops_vjp[l]((da2, dbre, dbim, dcre, dcim))):
            sg[k][l] = t
        sg["ssm_d"][l] = dd[0]
        sg["glu_b"][l] = dglu_b[0]
        sg["glu_w"][l] = dglu_w
        du_ssm = _from_segments(du_seg)
        dyp = jnp.concatenate([d_pool, jnp.zeros((POOL_HALO, D_POOL), f32)])
        du_pool, dpw, dps = _pool_bwd(dyp, sv["u_pool"], small["pool_w"][l], small["pool_scale"][l][None, :])
        sg["pool_w"][l] = dpw
        sg["pool_scale"][l] = dps[0]
        d_rest = jnp.concatenate([du_ssm, du_pool], axis=1).astype(bf16)
        dx, dm1, dz = _mix_in_bwd(dqkv, d_rest, dxr, sv["x1"], m9[3:6], sv["w1"][0], 0)
        g_w_in = _mix_in_wgrad(sv["h1"], dz, fresh(sv["w1"])[0], 0)
        ffn_names = ("ffn_w_gate", "ffn_w_up", "ffn_w_down")
        m9 = m9 + grads_done(l, 1, list(zip(ffn_names, [(2 * l + 1) * FF_SHARD] * 3, g_ffn1))
                             + [("w_in", l * D_MODEL, g_w_in), ("w_out", l * 256, g_w_out)])
        dm1 = jnp.concatenate([dm1[0:2], dgate1])
        dx, dg, du, df, dm0, dlg0, dlb0 = _ffn_bwd(dx, sv["x0"], sv["f0"], sv["g0"], sv["u0"], m9[0:3], *sv["w0"], 0,
                                                     small["ln_g"][l, 0:1])
        issued = ffn_bwd_issued(l, 0, dx)
        g_ffn0 = _ffn_wgrad(sv["h0"], dg, du, sv["a0"], df,*fresh(sv["w0"]), 0)
        order_after = grads_done(l, 0, list(zip(ffn_names, [2 * l * FF_SHARD] * 3, g_ffn0))) + issued
        dmod[l] = jnp.concatenate([dm0 + issued, dm1, dm2])
        dln_g[l] = jnp.concatenate([dlg0, dlg1, dlg2])
        dln_b[l] = jnp.concatenate([dlb0, dlb1, dlb2])

    small_grads = {k: jnp.stack(v) for k, v in sg.items()}
    small_grads["rel_bias"] = _bias_bwd(dbias_tot, table)
    small_grads["ln_g"] = jnp.stack(dln_g)
    small_grads["ln_b"] = jnp.stack(dln_b)
    return loss, dx, jnp.stack(dmod), small_grads


_TILE_ELEMS = 8 * 128


def _pack_rows(shapes):
    out, row = [], 0
    for s in shapes:
        nr = -(-int(np.prod(s)) // _TILE_ELEMS) * 8
        out.append((row, nr))
        row += nr
    return out


def _pack(arrs):
    parts = []
    for a in arrs:
        flat = a.reshape(-1).astype(f32)
        npad = -(-flat.shape[0] // _TILE_ELEMS) * _TILE_ELEMS
        parts.append(jnp.pad(flat, (0, npad - flat.shape[0])).reshape(npad // 128, 128))
    return jnp.concatenate(parts, axis=0)


def _unpack(buf, shapes):
    return [buf[row:row + nr].reshape(-1)[:int(np.prod(s))].reshape(s) for s, (row, nr) in zip(shapes, _pack_rows(shapes))]


_REPL = ("rel_bias", "ada_b", "ssm_a_re", "ssm_a_im", "ssm_log_dt", "ssm_b_re", "ssm_b_im", "ssm_c_re", "ssm_c_im",
         "ssm_d", "glu_b", "pool_w", "pool_scale")
_SMALL_SHARDED = ("ln_g", "ln_b", "glu_w")
_BIG = ("ffn_w_gate", "ffn_w_up", "ffn_w_down", "w_in", "w_out")
_ORDER = ("rel_bias", "ada_w", "ada_b", "ln_g", "ln_b", "ffn_w_gate", "ffn_w_up", "ffn_w_down", "w_in", "w_out",
          "ssm_a_re", "ssm_a_im", "ssm_log_dt", "ssm_b_re", "ssm_b_im", "ssm_c_re", "ssm_c_im", "ssm_d", "glu_w",
          "glu_b", "pool_w", "pool_scale")


def kernel(x, c, rel_bias, ada_w, ada_b, ln_g, ln_b, ffn_w_gate, ffn_w_up, ffn_w_down, w_in, w_out, ssm_a_re, ssm_a_im, ssm_log_dt, ssm_b_re, ssm_b_im, ssm_c_re, ssm_c_im, ssm_d, glu_w, glu_b, pool_w, pool_scale, loss_target, m_rel_bias, m_ada_w, m_ada_b, m_ln_g, m_ln_b, m_ffn_w_gate, m_ffn_w_up, m_ffn_w_down, m_w_in, m_w_out, m_ssm_a_re, m_ssm_a_im, m_ssm_log_dt, m_ssm_b_re, m_ssm_b_im, m_ssm_c_re, m_ssm_c_im, m_ssm_d, m_glu_w, m_glu_b, m_pool_w, m_pool_scale, v_rel_bias, v_ada_w, v_ada_b, v_ln_g, v_ln_b, v_ffn_w_gate, v_ffn_w_up, v_ffn_w_down, v_w_in, v_w_out, v_ssm_a_re, v_ssm_a_im, v_ssm_log_dt, v_ssm_b_re, v_ssm_b_im, v_ssm_c_re, v_ssm_c_im, v_ssm_d, v_glu_w, v_glu_b, v_pool_w, v_pool_scale):
    args = dict(locals())
    w = {k: args[k] for k in _ORDER}
    m = {k: args["m_" + k] for k in _ORDER}
    v = {k: args["v_" + k] for k in _ORDER}
    L, D = DEPTH, D_MODEL
    ax, ay, ac = lax.axis_index("x"), lax.axis_index("y"), lax.axis_index("c")
    p_me = 2 * ax + ay
    dev = 4 * ax + 2 * ay + ac

    transposed = ("ffn_w_gate", "ffn_w_up")
    for d in (w, m, v):
        for name in transposed:
            d[name] = jnp.swapaxes(d[name], 2, 3)

    def halves(t):
        return t.astype(bf16).reshape(1, 2, t.shape[0] // 2, t.shape[1])

    def landing(src):
        return lax.dynamic_update_slice(lax.empty((N_CHIPS,) + src.shape, bf16), src[None], (p_me, 0, 0, 0, 0))

    chunk_keys = [("ffn", 0, 0), ("mix", 0), ("ffn", 0, 1), ("ffn", 1, 0), ("mix", 1), ("ffn", 1, 1)]
    chunk_srcs = []
    for key in chunk_keys:
        if key[0] == "ffn":
            chunk_srcs.append([halves(w[name][key[1], key[2]]) for name in ("ffn_w_gate", "ffn_w_up", "ffn_w_down")])
        else:
            chunk_srcs.append([halves(w_in[key[1]]), halves(w_out[key[1]])])

    pack = _pack([c, ln_g, ln_b, glu_w])
    rows = pack.shape[0]
    allp = _allgather8(pack).reshape(8, rows, 128)
    chunks = [(srcs, [landing(t) for t in srcs]) for srcs in chunk_srcs]
    first_in_flight, first_begun = _gather_start(chunks[:1], allp, "gather_start_first")
    c_all = allp[:, :8].reshape(8, D) + first_begun[0, 0]
    by_chip = allp[0::2]

    fwd_rows = _pack_rows([c.shape, ln_g.shape, ln_b.shape, glu_w.shape])

    def sharded(part, shape, axis):
        row0, nrows = fwd_rows[part]
        t = by_chip[:, row0:row0 + nrows].reshape(N_CHIPS, -1)[:, :int(np.prod(shape))].reshape((N_CHIPS,) + shape)
        return jnp.concatenate([t[p] for p in range(N_CHIPS)], axis=axis)

    ln_g_full = sharded(1, ln_g.shape, 2)
    ln_b_full = sharded(2, ln_b.shape, 2)
    glu_w_full = sharded(3, glu_w.shape, 1)

    ncol = ada_w.shape[-1]
    ada_b_cols = lax.dynamic_slice_in_dim(ada_b, p_me * ncol, ncol, axis=1)[:, None, :]
    mod_part = _ada_fwd(c_all, ada_w, ada_b_cols)
    mrows = L * 8 * ncol // 128
    mod_pack = mod_part.reshape(mrows, 128)
    mod_land = lax.dynamic_update_slice(lax.empty((N_CHIPS, 2, mrows, 128), f32), mod_pack[None, None], (p_me, ac, 0, 0))
    mod_in_flight, _ = _plane_start(mod_pack, mod_land, "mod_start")
    att_bias = _bias_fwd(rel_bias, jnp.asarray(_bucket_table()))
    small_names = _REPL + _SMALL_SHARDED
    small_packs = [_pack([d[k] for k in small_names]) for d in (w, m, v)]
    mod_land = _plane_wait(*mod_in_flight, [att_bias] + small_packs + [t for _, lands in chunks[1:] for t in lands], "mod_wait")
    mod_all = _swap_halves([mod_land])[0].reshape(8, L, 8, ncol)
    mod_mine = lax.dynamic_index_in_dim(mod_all, dev, axis=2, keepdims=False)
    mod = jnp.concatenate([mod_mine[2 * p] for p in range(N_CHIPS)], axis=-1).reshape(L, 9, D)

    rest_in_flight, rest_begun = _gather_start(chunks[1:], mod, "gather_start_rest")
    in_flight = first_in_flight + rest_in_flight

    forwarding = {}

    def forward(k, after):
        lands = _gather_wait(*in_flight[k], [after, rest_begun], "gather_wait_%d" % k)
        forwarding[k], begun = _split_start(_forward_copies, lands, 3 * len(lands), "gather_forward_start_%d" % k)
        return begun

    def gathered(key, after):
        k = chunk_keys.index(key)
        order = [after]
        if k not in forwarding:
            order.append(forward(k, after))
        if 3 <= k + 1 < len(chunk_keys):
            order.append(forward(k + 1, after))
        lands = _split_wait(_forward_copies, *forwarding[k], order, "gather_forward_wait_%d" % k)
        return [t.reshape(N_CHIPS, 1, 2 * t.shape[3], t.shape[4]) for t in lands]

    pc = jnp.stack([p_me, ac]).astype(jnp.int32)
    groups = {}
    scattering = {}

    pairing = {}

    def start_pairs(tag, after=()):
        g5 = [g.reshape(g.shape[:2] + (2, g.shape[2] // 2, g.shape[3])) for _, _, g in groups[tag]]
        gots = [lax.empty(g.shape[:2] + g.shape[3:], bf16) for g in g5]
        pairing[tag], begun = _split_start(_pair_copies, g5 + gots, len(g5), "pair_exchange_start_%s" % tag, after)
        return begun

    def start_group(tag, after):
        arrays = _split_wait(_pair_copies, *pairing[tag], after, "pair_exchange_wait_%s" % tag)
        n = len(arrays) // 2
        hsum = _pair_sum(arrays[:n], arrays[n:], pc)
        scattering[tag], begun = _scatter_start(hsum, "scatter_start_%s" % tag)
        return begun

    def grads_done(l, s, grads):
        if l == 1:
            groups.setdefault("l1", []).extend(grads)
            return start_pairs("l1")[0, 0] if s == 0 else jnp.zeros((), f32)
        groups["l0a" if s == 1 else "l0b"] = grads
        return start_pairs("l0a")[0, 0] if s == 1 else jnp.zeros((), f32)

    swapping = {}

    def reduce_group(tag, after):
        hsum, recv = _scatter_wait(*scattering[tag], after, "scatter_wait_%s" % tag)
        full = _sum_shards(hsum, recv, pc)
        swapping[tag], begun = _split_start(_swap_copies, full, len(full), "swap_halves_start_%s" % tag)
        return [begun]

    def ffn_bwd_issued(l, s, dx):
        if l == 1:
            return jnp.zeros((), f32)
        return start_group("l1" if s == 1 else "l0a", [dx])[0, 0]

    small = {k: w[k] for k in _REPL if k != "ada_b"}
    small.update(ln_g=ln_g_full, ln_b=ln_b_full, glu_w=glu_w_full, att_bias=att_bias)
    loss_dev, grad_x, dmod, sgrads = _local_step(
        x[0], loss_target[0], mod, small, lambda l, s, after: gathered(("ffn", l, s), after),
        lambda l, after: gathered(("mix", l), after), grads_done, ffn_bwd_issued)
    loss = lax.psum(loss_dev[0, 0], ("x", "y", "c"))

    names = ("rel_bias", "ln_g", "ln_b", "ssm_a_re", "ssm_a_im", "ssm_log_dt", "ssm_b_re", "ssm_b_im", "ssm_c_re",
             "ssm_c_im", "ssm_d", "glu_w", "glu_b", "pool_w", "pool_scale")
    gpack = _pack([dmod] + [sgrads[k] for k in names])
    grows = gpack.shape[0]
    land = lax.dynamic_update_slice(lax.empty((N_CHIPS, 2, grows, 128), f32), gpack[None, None], (p_me, ac, 0, 0))
    small_in_flight, small_begun = _plane_start(gpack, land, "small_grads_start")
    l0b_begun = start_group("l0b", [start_pairs("l0b", (small_begun,))])

    out_g, out_d, out_m, out_v = {}, {}, {}, {}
    row_tile = dict(zip(_BIG, (352, 352, 352, 256, 256)))
    big = {name: None for name in _BIG}

    def update_group(tag, after):
        full = _split_wait(_swap_copies, *swapping[tag], after, "swap_halves_wait_%s" % tag)
        for (name, row0, _), g in zip(groups[tag], full):
            shp = w[name].shape
            r2 = (int(np.prod(shp[:-1])), shp[-1])
            big[name] = _adamw(w[name].reshape(r2), m[name].reshape(r2), v[name].reshape(r2), g.reshape(-1, shp[-1]),
                               row_tile[name], row0, big[name])
        return [big[name][1] for name, _, _ in groups[tag]]

    after = reduce_group("l0a", reduce_group("l1", [grad_x, l0b_begun]))
    after = update_group("l0a", update_group("l1", after))

    land = _plane_wait(*small_in_flight, after, "small_grads_wait")
    gall = _swap_halves([land])[0].reshape(8, grows, 128)
    gsum = _unpack(_sum8(gall), [(L, 9 * D)] + [sgrads[k].shape for k in names])
    red = dict(zip(("ada_b",) + names, gsum))
    red["ln_g"] = lax.dynamic_slice_in_dim(red["ln_g"], p_me * 256, 256, axis=2)
    red["ln_b"] = lax.dynamic_slice_in_dim(red["ln_b"], p_me * 256, 256, axis=2)
    red["glu_w"] = lax.dynamic_slice_in_dim(red["glu_w"], p_me * 64, 64, axis=1)

    dmod_all = gall[:, :L * 9 * D // 128].reshape(8, L, 9 * D)
    dmod_cols = jnp.transpose(lax.dynamic_slice_in_dim(dmod_all, p_me * ncol, ncol, axis=2), (1, 0, 2))
    g_ada_w = _ada_wgrad(jnp.transpose(c_all), dmod_cols)

    r2 = (L * D, ncol)
    res = _adamw(ada_w.reshape(r2), m["ada_w"].reshape(r2), v["ada_w"].reshape(r2), g_ada_w.reshape(r2), 128)
    out_g["ada_w"], out_d["ada_w"], out_m["ada_w"], out_v["ada_w"] = [t.reshape(ada_w.shape) for t in res]

    res_small = _adamw(*small_packs, _pack([red[k] for k in small_names]), small_packs[0].shape[0])
    for t, dst in zip(res_small, (out_g, out_d, out_m, out_v)):
        for k, a in zip(small_names, _unpack(t, [w[k].shape for k in small_names])):
            dst[k] = a

    update_group("l0b", reduce_group("l0b", [res_small[1], res[1]]))
    for name in _BIG:
        res = [t.reshape(w[name].shape) for t in big[name]]
        out_g[name], out_d[name], out_m[name], out_v[name] = [jnp.swapaxes(t, 2, 3) for t in res] if name in transposed else res

    return (loss, grad_x[None], *[out_g[k] for k in _ORDER], *[out_d[k] for k in _ORDER],
            *[out_m[k] for k in _ORDER], *[out_v[k] for k in _ORDER])
```

```python
import math

import numpy as np
import jax
import jax.numpy as jnp
from jax import lax
from jax.experimental import pallas as pl
from jax.experimental.pallas import tpu as pltpu

f32 = jnp.float32
bf16 = jnp.bfloat16
MESH = pl.DeviceIdType.MESH

D_MODEL = 1024
SEQ = 2048
DEPTH = 2
HEAD_DIM = 64
N_HEADS = 8
D_ATT = 512
DILATIONS = (1, 4, 16)
BLOCKS_PER_RESIDUE = (16, 4, 1)
ATT_BLOCK = 128
N_UNITS = SEQ // ATT_BLOCK
N_GROUPS = 16
SSM_STATE = 64
D_SSM = 256
D_STATE = N_GROUPS * SSM_STATE
POOL_WINDOWS = (2, 4, 8, 16)
POOL_GROUP = 64
D_POOL = 256
POOL_HALO = 16
D_FF = 2816
N_BUCKETS = 32
MAX_DISTANCE = 2048
ALPHA = (2 * DEPTH) ** 0.25
FFN_RES = 0.5
LN_EPS = 1e-5
NEG = -1e30
N_CHIPS = 4
FF_SHARD = D_FF // N_CHIPS
SCAN_SEG = 8

ADAM_LR, ADAM_B1, ADAM_B2, ADAM_EPS, ADAM_WD, ADAM_STEP = 0.001, 0.9, 0.999, 1e-08, 0.01, 10

TOK_TILE = 512


def _cp(dims=None, vmem_mb=None):
    kw = {}
    if dims is not None:
        kw["dimension_semantics"] = dims
    if vmem_mb is not None:
        kw["vmem_limit_bytes"] = vmem_mb << 20
    return pltpu.CompilerParams(**kw)


def _dot(a, b):
    return jnp.dot(a, b, preferred_element_type=f32)


def _dot_nt(a, b):
    return lax.dot_general(a, b, (((1,), (1,)), ((), ())), preferred_element_type=f32)


def _dot_tn(a, b):
    return lax.dot_general(a, b, (((0,), (0,)), ((), ())), preferred_element_type=f32)


def _ln_stats(v):
    mu = jnp.mean(v, -1, keepdims=True)
    d = v - mu
    var = jnp.mean(d * d, -1, keepdims=True)
    rstd = lax.rsqrt(var + LN_EPS)
    return d * rstd, rstd


def _ln_bwd(dxh, xh, rstd):
    return rstd * (dxh - jnp.mean(dxh, -1, keepdims=True) - xh * jnp.mean(dxh * xh, -1, keepdims=True))


_GELU_C = math.sqrt(2.0 / math.pi)


def _gelu(y):
    return 0.5 * y * (1.0 + jnp.tanh(_GELU_C * (y + 0.044715 * y * y * y)))


def _gelu_grad(y):
    t = jnp.tanh(_GELU_C * (y + 0.044715 * y * y * y))
    return 0.5 * (1.0 + t) + 0.5 * y * (1.0 - t * t) * (_GELU_C * (1.0 + 3 * 0.044715 * y * y))


def _full(shape):
    return pl.BlockSpec(shape, lambda *_: (0,) * len(shape))


def _hbm(*args):
    return [pltpu.with_memory_space_constraint(a, pltpu.HBM) if getattr(a, "ndim", 0) >= 2 else a for a in args]


def _ffn_fwd(x, mod3, wg, wu, wd, ls, lng, lnb):
    S, D = x.shape
    Fs = wg.shape[-2]
    ts = 2 * TOK_TILE

    def body(x_ref, mod_ref, wg_ref, wu_ref, wd_ref, lng_ref, lnb_ref, xo_ref, f_ref, g_ref, u_ref, a_ref, h_ref):
        j = pl.program_id(1)

        @pl.when(j == 0)
        def _():
            xh, _ = _ln_stats(x_ref[...])
            h_ref[...] = (xh * (1.0 + mod_ref[1:2, :]) + mod_ref[0:1, :]).astype(bf16)
            f_ref[...] = jnp.zeros_like(f_ref)

        h = h_ref[...]
        g = _dot_nt(h, wg_ref[0, 0])
        u = _dot_nt(h, wu_ref[0, 0])
        g_ref[0] = g.astype(bf16)
        u_ref[0] = u.astype(bf16)
        a = (g * jax.nn.sigmoid(g) * u).astype(bf16)
        a_ref[0] = a
        f_ref[...] += _dot(a, wd_ref[0, 0])

        @pl.when(j == N_CHIPS - 1)
        def _():
            f = f_ref[...]
            r = ALPHA * x_ref[...] + (FFN_RES * mod_ref[2:3, :]) * f
            rh, _ = _ln_stats(r)
            xo_ref[...] = rh * lng_ref[...] + lnb_ref[...]

    tok = pl.BlockSpec((ts, D), lambda i, j: (i, 0))
    wrow = pl.BlockSpec((1, 1, Fs, D), lambda i, j: (j, ls, 0, 0))
    hid = pl.BlockSpec((1, ts, Fs), lambda i, j: (j, i, 0))
    return pl.pallas_call(
        body, name="ffn_fwd", grid=(S // ts, N_CHIPS),
        in_specs=[tok, _full((3, D)), wrow, wrow, wrow, _full((1, D)), _full((1, D))],
        out_specs=[tok, tok, hid, hid, hid, tok],
        out_shape=[jax.ShapeDtypeStruct((S, D), f32), jax.ShapeDtypeStruct((S, D), f32)]
        + [jax.ShapeDtypeStruct((N_CHIPS, S, Fs), bf16)] * 3 + [jax.ShapeDtypeStruct((S, D), bf16)],
        compiler_params=_cp(("parallel", "arbitrary"), 56),
    )(*_hbm(x, mod3, wg, wu, wd, lng, lnb))


def _ffn_bwd(dxo, x, f, g, u, mod3, wg, wu, wd, ls, lng):
    S, D = x.shape
    Fs = wg.shape[-2]
    ts = TOK_TILE

    def body(dxo_ref, x_ref, f_ref, g_ref, u_ref, mod_ref, wg_ref, wu_ref, wd_ref, lng_ref,
             dx_ref, dg_ref, du_ref, df_ref, dmod_ref, dlng_ref, dlnb_ref,
             dr_sc, df_sc, acc_sc):
        i = pl.program_id(0)
        j = pl.program_id(1)

        @pl.when((i == 0) & (j == 0))
        def _():
            dmod_ref[...] = jnp.zeros_like(dmod_ref)
            dlng_ref[...] = jnp.zeros_like(dlng_ref)
            dlnb_ref[...] = jnp.zeros_like(dlnb_ref)

        @pl.when(j == 0)
        def _():
            xv = x_ref[...]
            fv = f_ref[...]
            gate = mod_ref[2:3, :]
            rh, rstd = _ln_stats(ALPHA * xv + (FFN_RES * gate) * fv)
            dy = dxo_ref[...]
            dlng_ref[...] += jnp.sum(dy * rh, 0, keepdims=True)
            dlnb_ref[...] += jnp.sum(dy, 0, keepdims=True)
            dr = _ln_bwd(dy * lng_ref[...], rh, rstd)
            dr_sc[...] = dr
            dmod_ref[2:3, :] += jnp.sum(FFN_RES * dr * fv, 0, keepdims=True)
            df = ((FFN_RES * gate) * dr).astype(bf16)
            df_sc[...] = df
            df_ref[...] = df
            acc_sc[...] = jnp.zeros_like(acc_sc)

        da = _dot_nt(df_sc[...], wd_ref[0, 0])
        gv = g_ref[0].astype(f32)
        uv = u_ref[0].astype(f32)
        sg = jax.nn.sigmoid(gv)
        si = gv * sg
        dgv = (da * uv * (sg * (1.0 + gv * (1.0 - sg)))).astype(bf16)
        duv = (da * si).astype(bf16)
        dg_ref[0] = dgv
        du_ref[0] = duv
        acc_sc[...] += _dot(dgv, wg_ref[0, 0]) + _dot(duv, wu_ref[0, 0])

        @pl.when(j == N_CHIPS - 1)
        def _():
            dh = acc_sc[...]
            xh, rstd0 = _ln_stats(x_ref[...])
            dmod_ref[0:1, :] += jnp.sum(dh, 0, keepdims=True)
            dmod_ref[1:2, :] += jnp.sum(dh * xh, 0, keepdims=True)
            dx_ref[...] = _ln_bwd(dh * (1.0 + mod_ref[1:2, :]), xh, rstd0) + ALPHA * dr_sc[...]

    tok = pl.BlockSpec((ts, D), lambda i, j: (i, 0))
    wrow = pl.BlockSpec((1, 1, Fs, D), lambda i, j: (j, ls, 0, 0))
    hid = pl.BlockSpec((1, ts, Fs), lambda i, j: (j, i, 0))
    hid_shape = jax.ShapeDtypeStruct((N_CHIPS, S, Fs), bf16)
    return pl.pallas_call(
        body, name="ffn_bwd", grid=(S // ts, N_CHIPS),
        in_specs=[tok, tok, tok, hid, hid, _full((3, D)), wrow, wrow, wrow, _full((1, D))],
        out_specs=[tok, hid, hid, tok, _full((3, D)), _full((1, D)), _full((1, D))],
        out_shape=[jax.ShapeDtypeStruct((S, D), f32), hid_shape, hid_shape,
                   jax.ShapeDtypeStruct((S, D), bf16),
                   jax.ShapeDtypeStruct((3, D), f32), jax.ShapeDtypeStruct((1, D), f32), jax.ShapeDtypeStruct((1, D), f32)],
        scratch_shapes=[pltpu.VMEM((ts, D), f32), pltpu.VMEM((ts, D), bf16), pltpu.VMEM((ts, D), f32)],
        compiler_params=_cp(("arbitrary", "arbitrary"), 56),
    )(*_hbm(dxo, x, f, g, u, mod3, wg, wu, wd, lng))


def _ffn_wgrad(h, dg, du, a, df, gwg, gwu, gwd, ls):
    S, D = h.shape
    Fs = dg.shape[-1]
    tk = 2 * TOK_TILE
    nk = S // tk

    def body(h_ref, dg_ref, du_ref, a_ref, df_ref, _g0, _g1, _g2, gwg_ref, gwu_ref, gwd_ref, ag_sc, au_sc, ad_sc):
        k = pl.program_id(1)

        @pl.when(k == 0)
        def _():
            ag_sc[...] = jnp.zeros_like(ag_sc)
            au_sc[...] = jnp.zeros_like(au_sc)
            ad_sc[...] = jnp.zeros_like(ad_sc)

        hv = h_ref[...]
        ag_sc[...] += _dot_tn(dg_ref[0], hv)
        au_sc[...] += _dot_tn(du_ref[0], hv)
        ad_sc[...] += _dot_tn(a_ref[0], df_ref[...])

        @pl.when(k == nk - 1)
        def _():
            gwg_ref[0, 0] = ag_sc[...].astype(bf16)
            gwu_ref[0, 0] = au_sc[...].astype(bf16)
            gwd_ref[0, 0] = ad_sc[...].astype(bf16)

    tok = pl.BlockSpec((tk, D), lambda p, k: (k, 0))
    hid = pl.BlockSpec((1, tk, Fs), lambda p, k: (p, k, 0))
    anyspec = pl.BlockSpec(memory_space=pl.ANY)
    orow = pl.BlockSpec((1, 1, Fs, D), lambda p, k: (p, ls, 0, 0))
    return pl.pallas_call(
        body, name="ffn_wgrad", grid=(N_CHIPS, nk),
        in_specs=[tok, hid, hid, hid, tok, anyspec, anyspec, anyspec],
        out_specs=[orow, orow, orow],
        out_shape=[jax.ShapeDtypeStruct(gwg.shape, bf16), jax.ShapeDtypeStruct(gwu.shape, bf16),
                   jax.ShapeDtypeStruct(gwd.shape, bf16)],
        scratch_shapes=[pltpu.VMEM((Fs, D), f32), pltpu.VMEM((Fs, D), f32), pltpu.VMEM((Fs, D), f32)],
        input_output_aliases={5: 0, 6: 1, 7: 2},
        compiler_params=_cp(("parallel", "arbitrary"), 48),
    )(*_hbm(h, dg, du, a, df, gwg, gwu, gwd))


_LANES = 128
_QKV_BLOCKS = D_ATT // _LANES


def _res_spec(lead, d, width, index):
    return pl.BlockSpec((lead, d, TOK_TILE // d, width), index)


def _res_spec3(d, width):
    return pl.BlockSpec((d, TOK_TILE // d, width), lambda i: (0, i, 0))


def _rows_to_residues(tile_bufs, d, put):
    for r in range(d):
        for cb, buf in enumerate(tile_bufs):
            put(r, cb, buf[pl.ds(r, TOK_TILE // d, stride=d), :])


def _residues_to_rows(tile_bufs, d, get):
    for r in range(d):
        for cb, buf in enumerate(tile_bufs):
            buf[pl.ds(r, TOK_TILE // d, stride=d), :] = get(r, cb)


def _mix_in_fwd(x, mod3, w_in, l):
    S, D = x.shape
    N = w_in.shape[-1]
    ts = TOK_TILE

    def body(x_ref, mod_ref, w_ref, o1_ref, o4_ref, o16_ref, zr_ref, h_ref, *bufs):
        j = pl.program_id(1)

        @pl.when(j == 0)
        def _():
            xh, _ = _ln_stats(x_ref[...])
            h_ref[...] = (xh * (1.0 + mod_ref[1:2, :]) + mod_ref[0:1, :]).astype(bf16)

        z = _dot(h_ref[...], w_ref[0, 0])

        @pl.when(j == N_CHIPS - 1)
        def _():
            zr_ref[...] = z

        @pl.when(j < N_CHIPS - 1)
        def _():
            zz = z * jnp.where(j == 0, HEAD_DIM ** -0.5, 1.0)
            o1_ref[j, 0] = zz.astype(bf16)
            for cb, buf in enumerate(bufs):
                buf[...] = zz[:, _LANES * cb:_LANES * (cb + 1)]
            for d, o_ref in zip(DILATIONS[1:], (o4_ref, o16_ref)):
                def put(r, cb, piece, o_ref=o_ref):
                    o_ref[j, r, :, _LANES * cb:_LANES * (cb + 1)] = piece.astype(bf16)
                _rows_to_residues(bufs, d, put)

    tok = pl.BlockSpec((ts, D), lambda i, j: (i, 0))
    res = [_res_spec(3, d, N, lambda i, j: (0, 0, i, 0)) for d in DILATIONS]
    return pl.pallas_call(
        body, name="mix_in_fwd", grid=(S // ts, N_CHIPS),
        in_specs=[tok, _full((3, D)), pl.BlockSpec((1, 1, D, N), lambda i, j: (j, l, 0, 0))],
        out_specs=res + [pl.BlockSpec((ts, N), lambda i, j: (i, 0)), tok],
        out_shape=[jax.ShapeDtypeStruct((3, d, S // d, N), bf16) for d in DILATIONS]
        + [jax.ShapeDtypeStruct((S, N), f32), jax.ShapeDtypeStruct((S, D), bf16)],
        scratch_shapes=[pltpu.VMEM((ts, _LANES), f32)] * _QKV_BLOCKS,
        compiler_params=_cp(("parallel", "arbitrary"), 40),
    )(*_hbm(x, mod3, w_in))


def _mix_in_bwd(dqkv, d_rest, dx_res, x, mod3, w_in, l):
    S, D = x.shape
    N = w_in.shape[-1]
    ts = TOK_TILE

    def body(d1_ref, d4_ref, d16_ref, dr_ref, dxr_ref, x_ref, mod_ref, w_ref, dx_ref, dmod_ref, dz_ref, acc_sc, *bufs):
        i = pl.program_id(0)
        j = pl.program_id(1)

        @pl.when((i == 0) & (j == 0))
        def _():
            dmod_ref[...] = jnp.zeros_like(dmod_ref)

        @pl.when(j == 0)
        def _():
            acc_sc[...] = jnp.zeros_like(acc_sc)

        @pl.when(j == N_CHIPS - 1)
        def _():
            dz_ref[0] = dr_ref[...]

        @pl.when(j < N_CHIPS - 1)
        def _():
            for d, d_ref, tile_bufs in ((4, d4_ref, bufs[:_QKV_BLOCKS]), (16, d16_ref, bufs[_QKV_BLOCKS:])):
                _residues_to_rows(tile_bufs, d, lambda r, cb, d_ref=d_ref: d_ref[0, r, :, _LANES * cb:_LANES * (cb + 1)].astype(f32))
            for cb in range(_QKV_BLOCKS):
                cols = slice(_LANES * cb, _LANES * (cb + 1))
                dz_ref[0, :, cols] = (d1_ref[0, 0, :, cols].astype(f32) + bufs[cb][...] + bufs[_QKV_BLOCKS + cb][...]).astype(bf16)

        acc_sc[...] += _dot_nt(dz_ref[0], w_ref[0, 0])

        @pl.when(j == N_CHIPS - 1)
        def _():
            dh = acc_sc[...]
            xh, rstd0 = _ln_stats(x_ref[...])
            dmod_ref[0:1, :] += jnp.sum(dh, 0, keepdims=True)
            dmod_ref[1:2, :] += jnp.sum(dh * xh, 0, keepdims=True)
            dx_ref[...] = _ln_bwd(dh * (1.0 + mod_ref[1:2, :]), xh, rstd0) + dxr_ref[...]

    tok = pl.BlockSpec((ts, D), lambda i, j: (i, 0))
    res = [_res_spec(1, d, N, lambda i, j: (jnp.minimum(j, 2), 0, i, 0)) for d in DILATIONS]
    return pl.pallas_call(
        body, name="mix_in_bwd", grid=(S // ts, N_CHIPS),
        in_specs=res + [pl.BlockSpec((ts, N), lambda i, j: (i, 0)), tok, tok, _full((3, D)),
                        pl.BlockSpec((1, 1, D, N), lambda i, j: (j, l, 0, 0))],
        out_specs=[tok, _full((3, D)), pl.BlockSpec((1, ts, N), lambda i, j: (j, i, 0))],
        out_shape=[jax.ShapeDtypeStruct((S, D), f32), jax.ShapeDtypeStruct((3, D), f32),
                   jax.ShapeDtypeStruct((N_CHIPS, S, N), bf16)],
        scratch_shapes=[pltpu.VMEM((ts, D), f32)] + [pltpu.VMEM((ts, _LANES), f32)] * (2 * _QKV_BLOCKS),
        compiler_params=_cp(("arbitrary", "arbitrary"), 40),
    )(*_hbm(*dqkv, d_rest, dx_res, x, mod3, w_in))


def _mix_in_wgrad(h, dz, gw, l):
    S, D = h.shape
    N = dz.shape[-1]
    tk = 2 * TOK_TILE
    nk = S // tk

    def body(h_ref, dz_ref, _g, gw_ref, acc_sc):
        k = pl.program_id(1)

        @pl.when(k == 0)
        def _():
            acc_sc[...] = jnp.zeros_like(acc_sc)

        acc_sc[...] += _dot_tn(h_ref[...], dz_ref[0])

        @pl.when(k == nk - 1)
        def _():
            gw_ref[0, 0] = acc_sc[...].astype(bf16)

    return pl.pallas_call(
        body, name="mix_in_wgrad", grid=(N_CHIPS, nk),
        in_specs=[pl.BlockSpec((tk, D), lambda p, k: (k, 0)), pl.BlockSpec((1, tk, N), lambda p, k: (p, k, 0)),
                  pl.BlockSpec(memory_space=pl.ANY)],
        out_specs=pl.BlockSpec((1, 1, D, N), lambda p, k: (p, l, 0, 0)),
        out_shape=jax.ShapeDtypeStruct(gw.shape, bf16),
        scratch_shapes=[pltpu.VMEM((D, N), f32)],
        input_output_aliases={2: 0},
        compiler_params=_cp(("parallel", "arbitrary"), 40),
    )(*_hbm(h, dz, gw))


def _mix_out_fwd(x, y_att, y_ssm, y_pool, mod3, w_out, l, lng, lnb):
    S, D = x.shape
    ts = TOK_TILE

    def body(x_ref, ya_ref, ys_ref, yp_ref, mod_ref, w_ref, lng_ref, lnb_ref, xo_ref, y_ref):
        ya = ya_ref[...].astype(bf16)
        y = (_dot(ya[:, 0:256], w_ref[0, 0]) + _dot(ya[:, 256:512], w_ref[1, 0])
             + _dot(ys_ref[...].astype(bf16), w_ref[2, 0]) + _dot(yp_ref[...].astype(bf16), w_ref[3, 0]))
        y_ref[...] = y
        rh, _ = _ln_stats(ALPHA * x_ref[...] + mod_ref[2:3, :] * y)
        xo_ref[...] = rh * lng_ref[...] + lnb_ref[...]

    tok = pl.BlockSpec((ts, D), lambda i: (i, 0))
    return pl.pallas_call(
        body, name="mix_out_fwd", grid=(S // ts,),
        in_specs=[tok, pl.BlockSpec((ts, D_ATT), lambda i: (i, 0)), pl.BlockSpec((ts, D_SSM), lambda i: (i, 0)),
                  pl.BlockSpec((ts, D_POOL), lambda i: (i, 0)), _full((3, D)),
                  pl.BlockSpec((N_CHIPS, 1, 256, D), lambda i: (0, l, 0, 0)), _full((1, D)), _full((1, D))],
        out_specs=[tok, tok],
        out_shape=[jax.ShapeDtypeStruct((S, D), f32), jax.ShapeDtypeStruct((S, D), f32)],
        compiler_params=_cp(("parallel",), 40),
    )(*_hbm(x, y_att, y_ssm, y_pool, mod3, w_out, lng, lnb))


def _mix_out_bwd(dxo, x, y, y_att, y_ssm, y_pool, mod3, w_out, l, lng, gw_out):
    S, D = x.shape
    ts = TOK_TILE
    nt = S // ts

    def body(dxo_ref, x_ref, y_ref, ya_ref, ys_ref, yp_ref, mod_ref, w_ref, lng_ref, _g,
             dxr_ref, da_ref, ds_ref, dp_ref, dgate_ref, dlng_ref, dlnb_ref, gw_ref, acc_sc):
        i = pl.program_id(0)

        @pl.when(i == 0)
        def _():
            dgate_ref[...] = jnp.zeros_like(dgate_ref)
            dlng_ref[...] = jnp.zeros_like(dlng_ref)
            dlnb_ref[...] = jnp.zeros_like(dlnb_ref)
            acc_sc[...] = jnp.zeros_like(acc_sc)

        gate = mod_ref[2:3, :]
        yv = y_ref[...]
        rh, rstd = _ln_stats(ALPHA * x_ref[...] + gate * yv)
        dy_out = dxo_ref[...]
        dlng_ref[...] += jnp.sum(dy_out * rh, 0, keepdims=True)
        dlnb_ref[...] += jnp.sum(dy_out, 0, keepdims=True)
        dr = _ln_bwd(dy_out * lng_ref[...], rh, rstd)
        dxr_ref[...] = ALPHA * dr
        dgate_ref[...] += jnp.sum(dr * yv, 0, keepdims=True)
        dy = (gate * dr).astype(bf16)
        da_ref[:, 0:256] = _dot_nt(dy, w_ref[0, 0])
        da_ref[:, 256:512] = _dot_nt(dy, w_ref[1, 0])
        ds_ref[...] = _dot_nt(dy, w_ref[2, 0])
        dp_ref[...] = _dot_nt(dy, w_ref[3, 0])
        ya = ya_ref[...].astype(bf16)
        acc_sc[0] += _dot_tn(ya[:, 0:256], dy)
        acc_sc[1] += _dot_tn(ya[:, 256:512], dy)
        acc_sc[2] += _dot_tn(ys_ref[...].astype(bf16), dy)
        acc_sc[3] += _dot_tn(yp_ref[...].astype(bf16), dy)

        @pl.when(i == nt - 1)
        def _():
            gw_ref[:, 0] = acc_sc[...].astype(bf16)

    tok = pl.BlockSpec((ts, D), lambda i: (i, 0))
    t512 = pl.BlockSpec((ts, D_ATT), lambda i: (i, 0))
    t256 = pl.BlockSpec((ts, 256), lambda i: (i, 0))
    wspec = pl.BlockSpec((N_CHIPS, 1, 256, D), lambda i: (0, l, 0, 0))
    return pl.pallas_call(
        body, name="mix_out_bwd", grid=(nt,),
        in_specs=[tok, tok, tok, t512, t256, t256, _full((3, D)), wspec, _full((1, D)), pl.BlockSpec(memory_space=pl.ANY)],
        out_specs=[tok, t512, t256, t256, _full((1, D)), _full((1, D)), _full((1, D)), wspec],
        out_shape=[jax.ShapeDtypeStruct((S, D), f32), jax.ShapeDtypeStruct((S, D_ATT), f32),
                   jax.ShapeDtypeStruct((S, D_SSM), f32), jax.ShapeDtypeStruct((S, D_POOL), f32),
                   jax.ShapeDtypeStruct((1, D), f32), jax.ShapeDtypeStruct((1, D), f32), jax.ShapeDtypeStruct((1, D), f32),
                   jax.ShapeDtypeStruct(gw_out.shape, bf16)],
        scratch_shapes=[pltpu.VMEM((N_CHIPS, 256, D), f32)],
        input_output_aliases={9: 7},
        compiler_params=_cp(("arbitrary",), 48),
    )(*_hbm(dxo, x, y, y_att, y_ssm, y_pool, mod3, w_out, lng, gw_out))


def _t5_bucket(dist):
    max_exact = N_BUCKETS // 2
    d = np.maximum(dist, 1).astype(np.float32)
    large = max_exact + (np.log(d / max_exact) / math.log(MAX_DISTANCE / max_exact)
                         * (N_BUCKETS - max_exact)).astype(np.int32)
    large = np.minimum(large, N_BUCKETS - 1)
    return np.where(dist < max_exact, dist, large).astype(np.int32)


def _bucket_table():
    q = ATT_BLOCK
    i = np.arange(q)[:, None]
    j = np.arange(2 * q)[None, :]
    r = i + q - j
    in_band = (r >= 0) & (r <= q)
    tabs = [np.where(in_band, _t5_bucket(np.clip(r, 0, None) * d), -1) for d in DILATIONS]
    return np.stack(tabs).astype(np.int32)


def _bias_fwd(rel_bias, table):
    def body(rb_ref, tab_ref, out_ref):
        for b in range(3):
            tb = tab_ref[b]
            for h in range(N_HEADS):
                def pick(k, acc):
                    return jnp.where(tb == k, rb_ref[k, h], acc)
                out_ref[b, h] = lax.fori_loop(0, N_BUCKETS, pick, jnp.where(tb < 0, NEG, 0.0).astype(f32))

    return pl.pallas_call(
        body, name="bias_fwd",
        in_specs=[pl.BlockSpec(memory_space=pltpu.SMEM), pl.BlockSpec(memory_space=pltpu.VMEM)],
        out_specs=pl.BlockSpec(memory_space=pltpu.VMEM),
        out_shape=jax.ShapeDtypeStruct((3, N_HEADS, ATT_BLOCK, 2 * ATT_BLOCK), f32),
    )(rel_bias, table)


def _bias_bwd(dbias, table):
    def body(db_ref, tab_ref, out_ref):
        def per_bucket(k, c):
            for h in range(N_HEADS):
                tot = jnp.zeros((), f32)
                for b in range(3):
                    tot = tot + jnp.sum(jnp.where(tab_ref[b] == k, db_ref[b, h], 0.0))
                out_ref[k, h] = tot
            return c
        lax.fori_loop(0, N_BUCKETS, per_bucket, 0)

    return pl.pallas_call(
        body, name="bias_bwd",
        in_specs=[pl.BlockSpec(memory_space=pltpu.VMEM), pl.BlockSpec(memory_space=pltpu.VMEM)],
        out_specs=pl.BlockSpec(memory_space=pltpu.SMEM),
        out_shape=jax.ShapeDtypeStruct((N_BUCKETS, N_HEADS), f32),
    )(dbias, table)


def _att_unit(u, nbr):
    rows = pl.ds(pl.multiple_of(u * ATT_BLOCK, ATT_BLOCK), ATT_BLOCK)
    prev = pl.ds(pl.multiple_of(jnp.maximum(u - 1, 0) * ATT_BLOCK, ATT_BLOCK), ATT_BLOCK)
    return rows, prev, (u % nbr) != 0


_N_PAIRS = N_HEADS // 2
_ATT_FWD_GROUPS = 4


def _pair_rows(t):
    lane = lax.broadcasted_iota(jnp.int32, t.shape, 1)
    zero = jnp.zeros_like(t)
    return jnp.concatenate([jnp.where(lane < HEAD_DIM, t, zero), jnp.where(lane >= HEAD_DIM, t, zero)], axis=0)


def _pair_cols(big):
    lane = lax.broadcasted_iota(jnp.int32, (ATT_BLOCK, _LANES), 1)
    return jnp.where(lane < HEAD_DIM, big[:ATT_BLOCK], big[ATT_BLOCK:])


def _pair_column(ref, rows, hp):
    t = ref[rows, :]
    return jnp.concatenate([t[:, 2 * hp:2 * hp + 1], t[:, 2 * hp + 1:2 * hp + 2]], axis=0)


def _pair_band(ref, rows, prev, nbr, hp):
    lanes = pl.ds(_LANES * hp, _LANES)
    cur = ref[0, rows, lanes]
    return cur if nbr == 1 else jnp.concatenate([ref[0, prev, lanes], cur], axis=0)


def _pair_scores(q_ref, k_ref, b_ref, rows, prev, valid_prev, nbr, hp, q_rows=None):
    qbd = _pair_rows(q_ref[0, rows if q_rows is None else q_rows, pl.ds(_LANES * hp, _LANES)])
    kb = _pair_band(k_ref, rows, prev, nbr, hp)
    bias = b_ref[0, 2 * hp:2 * hp + 2].reshape(2 * ATT_BLOCK, 2 * ATT_BLOCK)
    if nbr == 1:
        return qbd, kb, _dot_nt(qbd, kb) + bias[:, ATT_BLOCK:]
    s = _dot_nt(qbd, kb) + bias
    col = lax.broadcasted_iota(jnp.int32, s.shape, 1)
    return qbd, kb, jnp.where((col >= ATT_BLOCK) | valid_prev, s, NEG)


def _qkv_specs(S, branch):
    return ([pl.BlockSpec((1, S, D_ATT), lambda i, t=t: (t, 0, 0)) for t in range(3)],
            pl.BlockSpec((1, N_HEADS, ATT_BLOCK, 2 * ATT_BLOCK), lambda i: (branch, 0, 0, 0)))


def _att_fwd(qkv, bias, branch):
    S = qkv.shape[1]
    nbr = BLOCKS_PER_RESIDUE[branch]

    group_rows = S // _ATT_FWD_GROUPS
    group_units = N_UNITS // _ATT_FWD_GROUPS

    def body(q_ref, k_ref, v_ref, b_ref, o_ref, lse_ref):
        lse_ref[...] = jnp.zeros_like(lse_ref)
        first = pl.program_id(0) * group_units

        def unit(lu, c):
            rows, prev, valid_prev = _att_unit(first + lu, nbr)
            own = pl.ds(pl.multiple_of(lu * ATT_BLOCK, ATT_BLOCK), ATT_BLOCK)
            for hp in range(_N_PAIRS):
                _, _, s = _pair_scores(q_ref, k_ref, b_ref, rows, prev, valid_prev, nbr, hp, q_rows=own)
                m = jnp.max(s, -1, keepdims=True)
                p = jnp.exp(s - m)
                den = jnp.sum(p, -1, keepdims=True)
                big = _dot(p.astype(bf16), _pair_band(v_ref, rows, prev, nbr, hp))
                o_ref[own, pl.ds(_LANES * hp, _LANES)] = _pair_cols(big / den)
                lse = m + jnp.log(den)
                lse_ref[own, pl.ds(2 * hp, 1)] = lse[:ATT_BLOCK]
                lse_ref[own, pl.ds(2 * hp + 1, 1)] = lse[ATT_BLOCK:]
            return c

        lax.fori_loop(0, group_units, unit, 0)

    qkv_specs, bspec = _qkv_specs(S, branch)
    qkv_specs[0] = pl.BlockSpec((1, group_rows, D_ATT), lambda i: (0, i, 0))
    return pl.pallas_call(
        body, name="att_fwd", grid=(_ATT_FWD_GROUPS,),
        in_specs=qkv_specs + [bspec],
        out_specs=[pl.BlockSpec((group_rows, D_ATT), lambda i: (i, 0)), pl.BlockSpec((group_rows, _LANES), lambda i: (i, 0))],
        out_shape=[jax.ShapeDtypeStruct((S, D_ATT), f32), jax.ShapeDtypeStruct((S, _LANES), f32)],
        compiler_params=_cp(("arbitrary",), 40),
    )(*_hbm(qkv, qkv, qkv, bias))


def _att_bwd(qkv, do, lse, crow, bias, branch):
    S = qkv.shape[1]
    nbr = BLOCKS_PER_RESIDUE[branch]

    def body(q_ref, k_ref, v_ref, do_ref, lse_ref, c_ref, b_ref, dqkv_ref, db_ref, dk_sc, dv_sc):
        dk_sc[...] = jnp.zeros_like(dk_sc)
        dv_sc[...] = jnp.zeros_like(dv_sc)
        db_ref[...] = jnp.zeros_like(db_ref)

        def unit(u, c):
            rows, prev, valid_prev = _att_unit(u, nbr)
            for hp in range(_N_PAIRS):
                lanes = pl.ds(_LANES * hp, _LANES)
                qbd, kb, s = _pair_scores(q_ref, k_ref, b_ref, rows, prev, valid_prev, nbr, hp)
                p = jnp.exp(s - _pair_column(lse_ref, rows, hp))
                dobd = _pair_rows(do_ref[rows, lanes])
                ds = p * (_dot_nt(dobd, _pair_band(v_ref, rows, prev, nbr, hp)) - _pair_column(c_ref, rows, hp))
                if nbr == 1:
                    db_ref[2 * hp:2 * hp + 2, :, ATT_BLOCK:] += ds.reshape(2, ATT_BLOCK, ATT_BLOCK)
                else:
                    db_ref[2 * hp:2 * hp + 2] += ds.reshape(2, ATT_BLOCK, 2 * ATT_BLOCK)
                dsb = ds.astype(bf16)
                dqkv_ref[0, rows, lanes] = (HEAD_DIM ** -0.5 * _pair_cols(_dot(dsb, kb))).astype(bf16)
                dkb = _dot_tn(dsb, qbd)
                dvb = _dot_tn(p.astype(bf16), dobd)
                if nbr == 1:
                    dk_sc[rows, lanes] += dkb
                    dv_sc[rows, lanes] += dvb
                else:
                    dk_sc[prev, lanes] += dkb[:ATT_BLOCK]
                    dv_sc[prev, lanes] += dvb[:ATT_BLOCK]
                    dk_sc[rows, lanes] += dkb[ATT_BLOCK:]
                    dv_sc[rows, lanes] += dvb[ATT_BLOCK:]
            return c

        lax.fori_loop(0, N_UNITS, unit, 0)
        dqkv_ref[1] = dk_sc[...].astype(bf16)
        dqkv_ref[2] = dv_sc[...].astype(bf16)

    qkv_specs, bspec = _qkv_specs(S, branch)
    row = pl.BlockSpec((S, _LANES), lambda i: (0, 0))
    return pl.pallas_call(
        body, name="att_bwd", grid=(1,),
        in_specs=qkv_specs + [pl.BlockSpec((S, D_ATT), lambda i: (0, 0)), row, row, bspec],
        out_specs=[pl.BlockSpec((3, S, D_ATT), lambda i: (0, 0, 0)),
                   pl.BlockSpec((N_HEADS, ATT_BLOCK, 2 * ATT_BLOCK), lambda i: (0, 0, 0))],
        out_shape=[jax.ShapeDtypeStruct((3, S, D_ATT), bf16), jax.ShapeDtypeStruct((N_HEADS, ATT_BLOCK, 2 * ATT_BLOCK), f32)],
        scratch_shapes=[pltpu.VMEM((S, D_ATT), f32), pltpu.VMEM((S, D_ATT), f32)],
        compiler_params=_cp(("arbitrary",), 48),
    )(*_hbm(qkv, qkv, qkv, do, lse, crow, bias))


def _branch_weights(lse_ref):
    l0, l1, l2 = lse_ref[0], lse_ref[1], lse_ref[2]
    m = jnp.maximum(jnp.maximum(l0, l1), l2)
    e0, e1, e2 = jnp.exp(l0 - m), jnp.exp(l1 - m), jnp.exp(l2 - m)
    tot = e0 + e1 + e2
    return e0 / tot, e1 / tot, e2 / tot


def _att_merge(os, lses):
    S = os[0].shape[0] * os[0].shape[1]
    ts = TOK_TILE

    def body(o1_ref, o4_ref, o16_ref, l1_ref, l4_ref, l16_ref, y_ref, lt_ref, *bufs):
        obufs = (bufs[:_QKV_BLOCKS], bufs[_QKV_BLOCKS:2 * _QKV_BLOCKS])
        lt_ref[0] = l1_ref[0]
        for k, (d, o_ref, l_ref) in enumerate(((4, o4_ref, l4_ref), (16, o16_ref, l16_ref))):
            _residues_to_rows(obufs[k], d, lambda r, cb, o_ref=o_ref: o_ref[r, :, _LANES * cb:_LANES * (cb + 1)])
            _residues_to_rows([bufs[2 * _QKV_BLOCKS + k]], d, lambda r, cb, l_ref=l_ref: l_ref[r])
            lt_ref[1 + k] = bufs[2 * _QKV_BLOCKS + k][...]
        w = _branch_weights(lt_ref)
        for h in range(N_HEADS):
            cs = slice(HEAD_DIM * h, HEAD_DIM * (h + 1))
            half = slice(HEAD_DIM * (h % 2), HEAD_DIM * (h % 2 + 1))
            y_ref[:, cs] = (w[0][:, h:h + 1] * o1_ref[0, :, cs] + w[1][:, h:h + 1] * obufs[0][h // 2][:, half]
                            + w[2][:, h:h + 1] * obufs[1][h // 2][:, half])

    return pl.pallas_call(
        body, name="att_merge", grid=(S // ts,),
        in_specs=[_res_spec3(d, D_ATT) for d in DILATIONS] + [_res_spec3(d, _LANES) for d in DILATIONS],
        out_specs=[pl.BlockSpec((ts, D_ATT), lambda i: (i, 0)), pl.BlockSpec((3, ts, _LANES), lambda i: (0, i, 0))],
        out_shape=[jax.ShapeDtypeStruct((S, D_ATT), f32), jax.ShapeDtypeStruct((3, S, _LANES), f32)],
        scratch_shapes=[pltpu.VMEM((ts, _LANES), f32)] * (2 * _QKV_BLOCKS + 2),
        compiler_params=_cp(("parallel",)),
    )(*_hbm(*os, *lses))


def _att_merge_bwd(dy, y, lse3):
    S = dy.shape[0]
    ts = TOK_TILE

    def body(dy_ref, y_ref, lse_ref, do1_ref, do4_ref, do16_ref, c1_ref, c4_ref, c16_ref, *bufs):
        dobufs = (bufs[:_QKV_BLOCKS], bufs[_QKV_BLOCKS:2 * _QKV_BLOCKS], bufs[2 * _QKV_BLOCKS:3 * _QKV_BLOCKS])
        cbufs = bufs[3 * _QKV_BLOCKS:]
        w = _branch_weights(lse_ref)
        for cb in cbufs:
            cb[...] = jnp.zeros_like(cb)
        for h in range(N_HEADS):
            cs = slice(HEAD_DIM * h, HEAD_DIM * (h + 1))
            half = slice(HEAD_DIM * (h % 2), HEAD_DIM * (h % 2 + 1))
            dyh = dy_ref[:, cs]
            t = jnp.sum(dyh * y_ref[:, cs], -1, keepdims=True)
            for p in range(3):
                wp = w[p][:, h:h + 1]
                dobufs[p][h // 2][:, half] = wp * dyh
                cbufs[p][:, h:h + 1] = wp * t
        for cb in range(_QKV_BLOCKS):
            do1_ref[0, :, _LANES * cb:_LANES * (cb + 1)] = dobufs[0][cb][...].astype(bf16)
        c1_ref[0] = cbufs[0][...]
        for k, (d, do_ref, c_ref) in enumerate(((4, do4_ref, c4_ref), (16, do16_ref, c16_ref))):
            def put_do(r, cb, piece, do_ref=do_ref):
                do_ref[r, :, _LANES * cb:_LANES * (cb + 1)] = piece.astype(bf16)

            def put_c(r, cb, piece, c_ref=c_ref):
                c_ref[r] = piece

            _rows_to_residues(dobufs[1 + k], d, put_do)
            _rows_to_residues([cbufs[1 + k]], d, put_c)

    return pl.pallas_call(
        body, name="att_merge_bwd", grid=(S // ts,),
        in_specs=[pl.BlockSpec((ts, D_ATT), lambda i: (i, 0)), pl.BlockSpec((ts, D_ATT), lambda i: (i, 0)),
                  pl.BlockSpec((3, ts, _LANES), lambda i: (0, i, 0))],
        out_specs=[_res_spec3(d, D_ATT) for d in DILATIONS] + [_res_spec3(d, _LANES) for d in DILATIONS],
        out_shape=[jax.ShapeDtypeStruct((d, S // d, D_ATT), bf16) for d in DILATIONS]
        + [jax.ShapeDtypeStruct((d, S // d, _LANES), f32) for d in DILATIONS],
        scratch_shapes=[pltpu.VMEM((ts, _LANES), f32)] * (3 * _QKV_BLOCKS + 3),
        compiler_params=_cp(("parallel",)),
    )(*_hbm(dy, y, lse3))


_SSM_ROWS = 256


def _scan_in_place(sr_ref, si_ref, a_ref, reverse):
    S, N = sr_ref.shape
    nst = S // SCAN_SEG
    ar = jnp.broadcast_to(a_ref[0:1, :], (SCAN_SEG, N))
    ai = jnp.broadcast_to(a_ref[1:2, :], (SCAN_SEG, N))
    if reverse:
        ai = -ai
    row = lax.broadcasted_iota(jnp.int32, (SCAN_SEG, N), 0)
    zero = jnp.zeros((SCAN_SEG, N), f32)

    def tile(t):
        return pl.ds(pl.multiple_of((nst - 1 - t if reverse else t) * SCAN_SEG, SCAN_SEG), SCAN_SEG)

    def local(t, c):
        sr, si, pr, pi = c
        rows = tile(t)
        nsr = ar * sr - ai * si + sr_ref[rows, :]
        nsi = ar * si + ai * sr + si_ref[rows, :]
        sr_ref[rows, :] = nsr
        si_ref[rows, :] = nsi
        return nsr, nsi, ar * pr - ai * pi, ar * pi + ai * pr

    fr, fi, apr, api = lax.fori_loop(0, nst, local, (zero, zero, zero + 1.0, zero))

    def shift(v):
        if reverse:
            return jnp.where(row == SCAN_SEG - 1, 0.0, pltpu.roll(v, SCAN_SEG - 1, axis=0))
        return jnp.where(row == 0, 0.0, pltpu.roll(v, 1, axis=0))

    cr, ci = zero, zero
    for _ in range(SCAN_SEG - 1):
        cr, ci = shift(fr + apr * cr - api * ci), shift(fi + apr * ci + api * cr)

    def fix(t, c):
        pr, pi = c
        npr, npi = ar * pr - ai * pi, ar * pi + ai * pr
        rows = tile(t)
        sr_ref[rows, :] += npr * cr - npi * ci
        si_ref[rows, :] += npr * ci + npi * cr
        return npr, npi

    lax.fori_loop(0, nst, fix, (zero + 1.0, zero))


def _ssm_fwd(u, bre, bim, a2, cre, cim, l, dskip, glu_w, glu_b):
    S = u.shape[0]
    nproj = S // _SSM_ROWS

    def body(u_ref, br_ref, bi_ref, a2_ref, cr_ref, ci_ref, d_ref, w_ref, b_ref, sr_ref, si_ref, out_ref, y_ref):
        a_ref = a2_ref.at[l]
        brb = br_ref[l].astype(bf16)
        bib = bi_ref[l].astype(bf16)

        def project(t, c):
            rows = pl.ds(pl.multiple_of(t * _SSM_ROWS, _SSM_ROWS), _SSM_ROWS)
            ub = u_ref[rows, :].astype(bf16)
            sr_ref[rows, :] = _dot(ub, brb)
            si_ref[rows, :] = _dot(ub, bib)
            return c

        lax.fori_loop(0, nproj, project, 0)
        _scan_in_place(sr_ref, si_ref, a_ref, False)

        crb = cr_ref[l].astype(bf16)
        cib = ci_ref[l].astype(bf16)
        wb = w_ref[...].astype(bf16)

        def read_out(t, c):
            rows = pl.ds(pl.multiple_of(t * _SSM_ROWS, _SSM_ROWS), _SSM_ROWS)
            y = (_dot(sr_ref[rows, :].astype(bf16), crb) - _dot(si_ref[rows, :].astype(bf16), cib)
                 + d_ref[...] * u_ref[rows, :])
            y_ref[rows, :] = y
            z = _dot(_gelu(y).astype(bf16), wb) + b_ref[...]
            out_ref[rows, :] = y * jax.nn.sigmoid(z)
            return c

        lax.fori_loop(0, nproj, read_out, 0)

    vm = pl.BlockSpec(memory_space=pltpu.VMEM)
    return pl.pallas_call(
        body, name="ssm_fwd", in_specs=[vm] * 9, out_specs=[vm] * 4,
        out_shape=[jax.ShapeDtypeStruct((S, D_STATE), f32)] * 2 + [jax.ShapeDtypeStruct((S, D_SSM), f32)] * 2,
        compiler_params=_cp(None, 48),
    )(u, bre, bim, a2, cre, cim, dskip, glu_w, glu_b)


def _ssm_out_bwd(dout, y, u, dskip, glu_w, glu_b):
    S = u.shape[0]
    ts = TOK_TILE

    def body(do_ref, y_ref, u_ref, d_ref, w_ref, b_ref, dy_ref, du_ref, dd_ref, dgb_ref, dgw_ref):
        @pl.when(pl.program_id(0) == 0)
        def _():
            for r in (dd_ref, dgb_ref, dgw_ref):
                r[...] = jnp.zeros_like(r)

        y = y_ref[...]
        dout = do_ref[...]
        wb = w_ref[...].astype(bf16)
        ge = _gelu(y).astype(bf16)
        sz = jax.nn.sigmoid(_dot(ge, wb) + b_ref[...])
        dz = dout * y * sz * (1.0 - sz)
        dzb = dz.astype(bf16)
        dgb_ref[...] += jnp.sum(dz, 0, keepdims=True)
        dgw_ref[...] += _dot_tn(ge, dzb)
        dy = dout * sz + _gelu_grad(y) * _dot_nt(dzb, wb)
        uv = u_ref[...]
        dd_ref[...] += jnp.sum(dy * uv, 0, keepdims=True)
        du_ref[...] = dy * d_ref[...]
        dy_ref[...] = dy

    ch = pl.BlockSpec((ts, D_SSM), lambda i: (i, 0))
    return pl.pallas_call(
        body, name="ssm_out_bwd", grid=(S // ts,),
        in_specs=[ch, ch, ch, _full((1, D_SSM)), _full((D_SSM, D_SSM)), _full((1, D_SSM))],
        out_specs=[ch, ch, _full((1, D_SSM)), _full((1, D_SSM)), _full((D_SSM, D_SSM))],
        out_shape=[jax.ShapeDtypeStruct((S, D_SSM), f32), jax.ShapeDtypeStruct((S, D_SSM), f32),
                   jax.ShapeDtypeStruct((1, D_SSM), f32), jax.ShapeDtypeStruct((1, D_SSM), f32),
                   jax.ShapeDtypeStruct((D_SSM, D_SSM), f32)],
        compiler_params=_cp(("arbitrary",), 40),
    )(*_hbm(dout, y, u, dskip, glu_w, glu_b))


def _ssm_states_bwd(dy, du_skip, u, sr, si, cre, cim, bre, bim, a2, l):
    S = u.shape[0]
    N = D_STATE
    nst = S // SCAN_SEG
    nproj = S // _SSM_ROWS

    def body(dy_ref, dus_ref, u_ref, sr_ref, si_ref, cr_ref, ci_ref, br_ref, bi_ref, a2_ref,
             du_ref, dbr_ref, dbi_ref, da_ref, dcr_ref, dci_ref, lr_ref, li_ref):
        a_ref = a2_ref.at[l]
        crb = cr_ref[l].astype(bf16)
        cib = ci_ref[l].astype(bf16)
        dcr_ref[...] = jnp.zeros_like(dcr_ref)
        dci_ref[...] = jnp.zeros_like(dci_ref)

        def project(t, c):
            rows = pl.ds(pl.multiple_of(t * _SSM_ROWS, _SSM_ROWS), _SSM_ROWS)
            dyb = dy_ref[rows, :].astype(bf16)
            lr_ref[rows, :] = _dot_nt(dyb, crb)
            li_ref[rows, :] = -_dot_nt(dyb, cib)
            dcr_ref[...] += _dot_tn(sr_ref[rows, :].astype(bf16), dyb)
            dci_ref[...] -= _dot_tn(si_ref[rows, :].astype(bf16), dyb)
            return c

        lax.fori_loop(0, nproj, project, 0)
        _scan_in_place(lr_ref, li_ref, a_ref, True)

        row = lax.broadcasted_iota(jnp.int32, (SCAN_SEG, N), 0)
        last = pl.ds((nst - 1) * SCAN_SEG, SCAN_SEG)
        pr = jnp.where(row == 0, 0.0, pltpu.roll(sr_ref[last, :], 1, axis=0))
        pi = jnp.where(row == 0, 0.0, pltpu.roll(si_ref[last, :], 1, axis=0))
        first = pl.ds(0, SCAN_SEG)
        acc_r = lr_ref[first, :] * pr + li_ref[first, :] * pi
        acc_i = li_ref[first, :] * pr - lr_ref[first, :] * pi

        def step(t, c):
            acc_r, acc_i = c
            rows = pl.ds(pl.multiple_of(t * SCAN_SEG, SCAN_SEG), SCAN_SEG)
            prev = pl.ds(pl.multiple_of((t - 1) * SCAN_SEG, SCAN_SEG), SCAN_SEG)
            lrv, liv, srv, siv = lr_ref[rows, :], li_ref[rows, :], sr_ref[prev, :], si_ref[prev, :]
            return acc_r + lrv * srv + liv * siv, acc_i + liv * srv - lrv * siv

        acc_r, acc_i = lax.fori_loop(1, nst, step, (acc_r, acc_i))
        da_ref[0:1, :] = jnp.sum(acc_r, 0, keepdims=True)
        da_ref[1:2, :] = jnp.sum(acc_i, 0, keepdims=True)

        brb = br_ref[l].astype(bf16)
        bib = bi_ref[l].astype(bf16)
        dbr_ref[...] = jnp.zeros_like(dbr_ref)
        dbi_ref[...] = jnp.zeros_like(dbi_ref)

        def back(t, c):
            rows = pl.ds(pl.multiple_of(t * _SSM_ROWS, _SSM_ROWS), _SSM_ROWS)
            lrb = lr_ref[rows, :].astype(bf16)
            lib = li_ref[rows, :].astype(bf16)
            du_ref[rows, :] = dus_ref[rows, :] + _dot_nt(lrb, brb) + _dot_nt(lib, bib)
            ub = u_ref[rows, :].astype(bf16)
            dbr_ref[...] += _dot_tn(ub, lrb)
            dbi_ref[...] += _dot_tn(ub, lib)
            return c

        lax.fori_loop(0, nproj, back, 0)

    vm = pl.BlockSpec(memory_space=pltpu.VMEM)
    return pl.pallas_call(
        body, name="ssm_states_bwd", in_specs=[vm] * 10, out_specs=[vm] * 6,
        out_shape=[jax.ShapeDtypeStruct((S, D_SSM), f32), jax.ShapeDtypeStruct((D_SSM, D_STATE), f32),
                   jax.ShapeDtypeStruct((D_SSM, D_STATE), f32), jax.ShapeDtypeStruct((2, D_STATE), f32),
                   jax.ShapeDtypeStruct((D_STATE, D_SSM), f32), jax.ShapeDtypeStruct((D_STATE, D_SSM), f32)],
        scratch_shapes=[pltpu.VMEM((S, D_STATE), f32), pltpu.VMEM((S, D_STATE), f32)],
        compiler_params=_cp(None, 56),
    )(dy, du_skip, u, sr, si, cre, cim, bre, bim, a2)


_POOL_TILE = 256


def _window_sums(xt, back):
    n = xt.shape[0]
    out = []
    ws = xt
    for k in (1, 2, 4, 8):
        ws = ws + pltpu.roll(ws, k if back else n - k, axis=0)
        out.append(ws)
    return out


def _pool_count(r0, w):
    t = r0 + lax.broadcasted_iota(jnp.int32, (_POOL_TILE, POOL_GROUP), 0)
    return jnp.minimum(t + 1, w).astype(f32)


def _pool_fwd(u_pad, pool_w, pool_scale):
    S = u_pad.shape[0] - POOL_HALO
    nt = S // _POOL_TILE

    def body(u_ref, w_ref, sc_ref, y_ref):
        def tile(t, c):
            r0 = pl.multiple_of(t * _POOL_TILE, _POOL_TILE)
            for g, w in enumerate(POOL_WINDOWS):
                cs = pl.ds(POOL_GROUP * g, POOL_GROUP)
                xt = u_ref[pl.ds(r0, _POOL_TILE + POOL_HALO), cs]
                ws = _window_sums(xt, True)[g][POOL_HALO:, :]
                pooled = ws / _pool_count(r0, w) - xt[POOL_HALO:, :]
                y_ref[pl.ds(r0, _POOL_TILE), cs] = _dot(pooled.astype(bf16), w_ref[g].astype(bf16)) * sc_ref[:, cs]
            return c
        lax.fori_loop(0, nt, tile, 0)

    vm = pl.BlockSpec(memory_space=pltpu.VMEM)
    return pl.pallas_call(
        body, name="pool_fwd", in_specs=[vm, vm, vm], out_specs=vm,
        out_shape=jax.ShapeDtypeStruct((S, D_POOL), f32),
    )(u_pad, pool_w, pool_scale)


def _pool_bwd(dy_pad, u_pad, pool_w, pool_scale):
    S = u_pad.shape[0] - POOL_HALO
    nt = S // _POOL_TILE
    n = _POOL_TILE + POOL_HALO

    def body(dy_ref, u_ref, w_ref, sc_ref, du_ref, dw_ref, dsc_ref):
        dw_ref[...] = jnp.zeros_like(dw_ref)
        dsc_ref[...] = jnp.zeros_like(dsc_ref)

        def tile(t, c):
            r0 = pl.multiple_of(t * _POOL_TILE, _POOL_TILE)
            for g, w in enumerate(POOL_WINDOWS):
                cs = pl.ds(POOL_GROUP * g, POOL_GROUP)
                wb = w_ref[g].astype(bf16)
                xt = u_ref[pl.ds(r0, n), cs]
                pooled = (_window_sums(xt, True)[g][POOL_HALO:, :] / _pool_count(r0, w) - xt[POOL_HALO:, :]).astype(bf16)
                dy = dy_ref[pl.ds(r0, _POOL_TILE), cs]
                dsc_ref[:, cs] += jnp.sum(dy * _dot(pooled, wb), 0, keepdims=True)
                dw_ref[g] += _dot_tn(pooled, (dy * sc_ref[:, cs]).astype(bf16))
                dyh = (dy_ref[pl.ds(r0, n), cs] * sc_ref[:, cs]).astype(bf16)
                dpl = _dot_nt(dyh, wb)
                cnt = jnp.minimum(r0 + lax.broadcasted_iota(jnp.int32, (n, POOL_GROUP), 0) + 1, w).astype(f32)
                lead = _window_sums(dpl / cnt, False)[g]
                du_ref[pl.ds(r0, _POOL_TILE), cs] = lead[:_POOL_TILE, :] - dpl[:_POOL_TILE, :]
            return c
        lax.fori_loop(0, nt, tile, 0)

    vm = pl.BlockSpec(memory_space=pltpu.VMEM)
    return pl.pallas_call(
        body, name="pool_bwd", in_specs=[vm, vm, vm, vm], out_specs=[vm, vm, vm],
        out_shape=[jax.ShapeDtypeStruct((S, D_POOL), f32), jax.ShapeDtypeStruct((4, POOL_GROUP, POOL_GROUP), f32),
                   jax.ShapeDtypeStruct((1, D_POOL), f32)],
    )(dy_pad, u_pad, pool_w, pool_scale)


def _loss_head(y, target):
    S, D = y.shape
    ts = TOK_TILE

    def body(y_ref, t_ref, loss_ref, dy_ref):
        @pl.when(pl.program_id(0) == 0)
        def _():
            loss_ref[...] = jnp.zeros_like(loss_ref)

        d = y_ref[...] - t_ref[...]
        dy_ref[...] = d * (1.0 / D)
        loss_ref[...] += 0.5 * jnp.sum(jnp.sum(d * d, -1, keepdims=True) * (1.0 / D), 0, keepdims=True)

    tok = pl.BlockSpec((ts, D), lambda i: (i, 0))
    return pl.pallas_call(
        body, name="loss_head", grid=(S // ts,),
        in_specs=[tok, tok], out_specs=[_full((1, 1)), tok],
        out_shape=[jax.ShapeDtypeStruct((1, 1), f32), jax.ShapeDtypeStruct((S, D), f32)],
        compiler_params=_cp(("arbitrary",)),
    )(*_hbm(y, target))


_ADA_COLS = 768


def _ada_fwd(c_all, ada_w, ada_b_cols):
    L, D, N = ada_w.shape
    B = c_all.shape[0]

    def body(c_ref, w_ref, b_ref, out_ref):
        cv = c_ref[...]
        cond = (cv * jax.nn.sigmoid(cv)).astype(bf16)
        out_ref[0] = _dot(cond, w_ref[0].astype(bf16)) + b_ref[0]

    return pl.pallas_call(
        body, name="ada_fwd", grid=(L, N // _ADA_COLS),
        in_specs=[_full((B, D)), pl.BlockSpec((1, D, _ADA_COLS), lambda l, j: (l, 0, j)),
                  pl.BlockSpec((1, 1, _ADA_COLS), lambda l, j: (l, 0, j))],
        out_specs=pl.BlockSpec((1, B, _ADA_COLS), lambda l, j: (l, 0, j)),
        out_shape=jax.ShapeDtypeStruct((L, B, N), f32),
        compiler_params=_cp(("parallel", "parallel")),
    )(c_all, ada_w, ada_b_cols)


def _ada_wgrad(c_all_t, dmod_cols):
    D, B = c_all_t.shape
    L, _, N = dmod_cols.shape

    def body(ct_ref, dm_ref, out_ref):
        cv = ct_ref[...]
        cond = cv * jax.nn.sigmoid(cv)
        acc = cond[:, 0:1] * dm_ref[0, 0:1, :]
        for b in range(1, B):
            acc = acc + cond[:, b:b + 1] * dm_ref[0, b:b + 1, :]
        out_ref[0] = acc

    return pl.pallas_call(
        body, name="ada_wgrad", grid=(L, N // _ADA_COLS),
        in_specs=[_full((D, B)), pl.BlockSpec((1, B, _ADA_COLS), lambda l, j: (l, 0, j))],
        out_specs=pl.BlockSpec((1, D, _ADA_COLS), lambda l, j: (l, 0, j)),
        out_shape=jax.ShapeDtypeStruct((L, D, N), f32),
        compiler_params=_cp(("parallel", "parallel")),
    )(c_all_t, dmod_cols)


def _adam_math(w, g, m, v):
    m = ADAM_B1 * m + (1.0 - ADAM_B1) * g
    v = ADAM_B2 * v + (1.0 - ADAM_B2) * (g * g)
    m_hat = m / (1.0 - ADAM_B1 ** ADAM_STEP)
    v_hat = v / (1.0 - ADAM_B2 ** ADAM_STEP)
    delta = -ADAM_LR * (m_hat / (jnp.sqrt(v_hat) + ADAM_EPS) + ADAM_WD * w)
    return delta, m, v


def _adamw(w, m, v, g, row_tile, row0=0, outs=None):
    R, C = w.shape
    b0 = row0 // row_tile

    def body(w_ref, m_ref, v_ref, g_ref, _0, _1, _2, _3, g_out, d_out, m_out, v_out):
        gv = g_ref[...]
        delta, mn, vn = _adam_math(w_ref[...], gv, m_ref[...], v_ref[...])
        g_out[...] = gv
        d_out[...] = delta
        m_out[...] = mn
        v_out[...] = vn

    pspec = pl.BlockSpec((row_tile, C), lambda i: (b0 + i, 0))
    gspec = pl.BlockSpec((row_tile, C), lambda i: (i, 0))
    anyspec = pl.BlockSpec(memory_space=pl.ANY)
    shp = jax.ShapeDtypeStruct((R, C), f32)
    if outs is None:
        outs = [lax.empty((R, C), f32) for _ in range(4)]
    return pl.pallas_call(
        body, name="adamw", grid=(g.shape[0] // row_tile,),
        in_specs=[pspec] * 3 + [gspec] + [anyspec] * 4, out_specs=[pspec] * 4, out_shape=[shp] * 4,
        input_output_aliases={4: 0, 5: 1, 6: 2, 7: 3},
        compiler_params=_cp(("parallel",), 40),
    )(*_hbm(w, m, v, g, *outs))


def _pair_sum(g5s, gots, pc):
    n = len(g5s)

    def body(pc_ref, *refs):
        for own, got, out in zip(refs[:n], refs[n:2 * n], refs[2 * n:]):
            out[0, 0] = (own[0, 0, 0].astype(f32) + got[0, 0].astype(f32)).astype(bf16)

    def half(g):
        return pl.BlockSpec((1, 1) + g.shape[-2:], lambda p, pc: (p, 0, 0, 0))

    gs = pltpu.PrefetchScalarGridSpec(
        num_scalar_prefetch=1, grid=(N_CHIPS,),
        in_specs=[pl.BlockSpec((1, 1, 1) + g.shape[-2:], lambda p, pc: (p, 0, pc[1], 0, 0)) for g in g5s]
        + [half(g) for g in gots],
        out_specs=[half(g) for g in gots],
    )
    return pl.pallas_call(
        body, name="pair_sum", grid_spec=gs, out_shape=[jax.ShapeDtypeStruct(g.shape, bf16) for g in gots],
        compiler_params=_cp(("parallel",), 48),
    )(pc, *_hbm(*g5s, *gots))


_SUM_STEPS = 2


def _sum_shards(hsums, recvs, pc):
    n = len(hsums)

    def body(pc_ref, *refs):
        for own, got, out in zip(refs[:n], refs[n:2 * n], refs[2 * n:]):
            acc = own[0, 0].astype(f32)
            for j in range(3):
                acc = acc + got[j, 0].astype(f32)
            out[0, 0] = acc

    def rows(h):
        return (h.shape[2] // _SUM_STEPS, h.shape[3])

    gs = pltpu.PrefetchScalarGridSpec(
        num_scalar_prefetch=1, grid=(_SUM_STEPS,),
        in_specs=[pl.BlockSpec((1, 1) + rows(h), lambda i, pc: (pc[0], 0, i, 0)) for h in hsums]
        + [pl.BlockSpec((3, 1) + rows(h), lambda i, pc: (0, 0, i, 0)) for h in hsums],
        out_specs=[pl.BlockSpec((1, 1) + rows(h), lambda i, pc: (0, pc[1], i, 0)) for h in hsums],
    )
    return pl.pallas_call(
        body, name="sum_shards", grid_spec=gs,
        out_shape=[jax.ShapeDtypeStruct((1, 2) + h.shape[2:], f32) for h in hsums],
        compiler_params=_cp(("parallel",), 48),
    )(pc, *_hbm(*hsums, *recvs))


def _sum8(packs):
    _, R, C = packs.shape
    tr = R // 8 if R % 64 == 0 else R

    def body(p_ref, out_ref):
        acc = p_ref[0]
        for d in range(1, 8):
            acc = acc + p_ref[d]
        out_ref[...] = acc

    return pl.pallas_call(
        body, name="sum8", grid=(R // tr,),
        in_specs=[pl.BlockSpec((8, tr, C), lambda i: (0, i, 0))],
        out_specs=pl.BlockSpec((tr, C), lambda i: (i, 0)),
        out_shape=jax.ShapeDtypeStruct((R, C), f32),
        compiler_params=_cp(("parallel",)),
    )(packs)


def _allgather8(x_shard):
    m_per, n = x_shard.shape

    def body(x_ref, out_ref, send_sems, recv_sems, local_sem):
        x, y, c = lax.axis_index("x"), lax.axis_index("y"), lax.axis_index("c")
        me, sibling = (x, y, c), (x, y, 1 - c)
        chips = [(1 - x, y), (x, 1 - y), (1 - x, 1 - y)]

        def rows(px, py, pc):
            return out_ref.at[pl.ds((4 * px + 2 * py + pc) * m_per, m_per), :]

        def copy(k, block, to, src=None):
            return pltpu.make_async_remote_copy(
                src_ref=rows(*block) if src is None else src, dst_ref=rows(*block),
                send_sem=send_sems.at[k], recv_sem=recv_sems.at[k], device_id=to, device_id_type=MESH)

        mine = pltpu.make_async_copy(x_ref, rows(*me), local_sem)
        mine.start()
        first = [copy(0, me, sibling, src=x_ref)]
        first += [copy(1 + j, me, (*chip, c), src=x_ref) for j, chip in enumerate(chips)]
        for cp in first:
            cp.start()
        passed = [copy(4 + j, (*chip, c), sibling) for j, chip in enumerate(chips)]
        for j, chip in enumerate(chips):
            copy(1 + j, (*chip, c), me).wait_recv()
            passed[j].start()
        copy(0, sibling, me).wait_recv()
        for j, chip in enumerate(chips):
            copy(4 + j, (*chip, 1 - c), me).wait_recv()
        for cp in first + passed:
            cp.wait_send()
        mine.wait()

    return pl.pallas_call(
        body, name="allgather8",
        out_shape=jax.ShapeDtypeStruct((8 * m_per, n), x_shard.dtype),
        in_specs=[pl.BlockSpec(memory_space=pltpu.VMEM)],
        out_specs=pl.BlockSpec(memory_space=pltpu.VMEM),
        scratch_shapes=[pltpu.SemaphoreType.DMA((7,)), pltpu.SemaphoreType.DMA((7,)), pltpu.SemaphoreType.DMA],
        compiler_params=_cp(None, 48),
    )(x_shard)


def _other_chips():
    x, y = lax.axis_index("x"), lax.axis_index("y")
    return [(1 - x, y), (x, 1 - y), (1 - x, 1 - y)]


_HBM = pl.BlockSpec(memory_space=pltpu.HBM)
_SEM = pl.BlockSpec(memory_space=pltpu.SEMAPHORE)
_EFFECT = pltpu.SideEffectType.DATAFLOW_SIDE_EFFECTING


def _gather_copies(srcs, lands, send_sems, recv_sems):
    x, y, c = lax.axis_index("x"), lax.axis_index("y"), lax.axis_index("c")
    return [pltpu.make_async_remote_copy(
        src_ref=srcs[a].at[:, c], dst_ref=lands[a].at[2 * x + y, :, c], send_sem=send_sems.at[3 * a + j],
        recv_sem=recv_sems.at[3 * a + j], device_id=(cx, cy, c), device_id_type=MESH)
        for a in range(len(srcs)) for j, (cx, cy) in enumerate(_other_chips())]


def _gather_start(chunks, after, name):
    sizes = [len(srcs) for srcs, _ in chunks]
    flat = [t for srcs, lands in chunks for t in list(srcs) + list(lands)]
    nflat = len(flat)
    nsem = 2 * len(chunks)

    def body(*refs):
        ins, sems, token = refs[:nflat], refs[nflat + 1:nflat + 1 + nsem], refs[-1]
        off = 0
        for k, n in enumerate(sizes):
            for cp in _gather_copies(ins[off:off + n], ins[off + n:off + 2 * n], sems[2 * k], sems[2 * k + 1]):
                cp.start()
            off += 2 * n
        token[...] = jnp.zeros_like(token)

    res = pl.pallas_call(
        body, name=name,
        out_shape=[pltpu.SemaphoreType.DMA((3 * n,)) for n in sizes for _ in range(2)]
        + [pltpu.HBM(t.shape, t.dtype) for t in flat] + [jax.ShapeDtypeStruct((8, 128), f32)],
        in_specs=[_HBM] * nflat + [pl.BlockSpec(memory_space=pl.ANY)],
        out_specs=[_SEM] * nsem + [_HBM] * nflat + [pl.BlockSpec(memory_space=pltpu.VMEM)],
        input_output_aliases={i: nsem + i for i in range(nflat)},
        compiler_params=pltpu.CompilerParams(has_side_effects=_EFFECT),
    )(*[pltpu.with_memory_space_constraint(t, pltpu.HBM) for t in flat], after)
    out, off = [], nsem
    for k, n in enumerate(sizes):
        out.append((res[2 * k], res[2 * k + 1], res[off:off + n], res[off + n:off + 2 * n]))
        off += 2 * n
    return out, res[-1]


def _gather_wait(send_sems, recv_sems, srcs, lands, after, name):
    n = len(srcs)

    def body(*refs):
        for cp in _gather_copies(refs[:n], refs[n:2 * n], refs[2 * n], refs[2 * n + 1]):
            cp.wait_send()
            cp.wait_recv()

    res = pl.pallas_call(
        body, name=name,
        out_shape=[pltpu.HBM(t.shape, t.dtype) for t in list(srcs) + list(lands)],
        in_specs=[_HBM] * (2 * n) + [_SEM, _SEM] + [pl.BlockSpec(memory_space=pl.ANY)] * len(after),
        out_specs=[_HBM] * (2 * n),
        input_output_aliases={i: i for i in range(2 * n)},
        compiler_params=pltpu.CompilerParams(has_side_effects=_EFFECT),
    )(*srcs, *lands, send_sems, recv_sems, *after)
    return res[n:]


def _split_start(make_copies, arrays, nsem, name, after=()):
    n, na = len(arrays), len(after)

    def body(*refs):
        for cp in make_copies(refs[:n], refs[n + na], refs[n + na + 1]):
            cp.start()
        refs[-1][...] = jnp.zeros_like(refs[-1])

    res = pl.pallas_call(
        body, name=name,
        out_shape=[pltpu.SemaphoreType.DMA((nsem,)), pltpu.SemaphoreType.DMA((nsem,))]
        + [pltpu.HBM(t.shape, t.dtype) for t in arrays] + [jax.ShapeDtypeStruct((8, 128), f32)],
        in_specs=[_HBM] * n + [pl.BlockSpec(memory_space=pl.ANY)] * na,
        out_specs=[_SEM, _SEM] + [_HBM] * n + [pl.BlockSpec(memory_space=pltpu.VMEM)],
        input_output_aliases={i: i + 2 for i in range(n)},
        compiler_params=pltpu.CompilerParams(has_side_effects=_EFFECT),
    )(*[pltpu.with_memory_space_constraint(t, pltpu.HBM) for t in arrays], *after)
    return (res[0], res[1], res[2:2 + n]), res[-1]


def _split_wait(make_copies, send_sems, recv_sems, arrays, after, name):
    n = len(arrays)

    def body(*refs):
        for cp in make_copies(refs[:n], refs[n], refs[n + 1]):
            cp.wait_send()
            cp.wait_recv()

    return pl.pallas_call(
        body, name=name,
        out_shape=[pltpu.HBM(t.shape, t.dtype) for t in arrays],
        in_specs=[_HBM] * n + [_SEM, _SEM] + [pl.BlockSpec(memory_space=pl.ANY)] * len(after),
        out_specs=[_HBM] * n, input_output_aliases={i: i for i in range(n)},
        compiler_params=pltpu.CompilerParams(has_side_effects=_EFFECT),
    )(*arrays, send_sems, recv_sems, *after)


def _sibling():
    return lax.axis_index("x"), lax.axis_index("y"), 1 - lax.axis_index("c")


def _forward_copies(lands, send_sems, recv_sems):
    c = lax.axis_index("c")
    return [pltpu.make_async_remote_copy(
        src_ref=lands[a].at[2 * cx + cy, :, c], dst_ref=lands[a].at[2 * cx + cy, :, c], send_sem=send_sems.at[3 * a + j],
        recv_sem=recv_sems.at[3 * a + j], device_id=_sibling(), device_id_type=MESH)
        for a in range(len(lands)) for j, (cx, cy) in enumerate(_other_chips())]


def _swap_copies(fulls, send_sems, recv_sems):
    c = lax.axis_index("c")
    return [pltpu.make_async_remote_copy(src_ref=t.at[:, c], dst_ref=t.at[:, c], send_sem=send_sems.at[a],
                                         recv_sem=recv_sems.at[a], device_id=_sibling(), device_id_type=MESH)
            for a, t in enumerate(fulls)]


def _pair_copies(refs, send_sems, recv_sems):
    n = len(refs) // 2
    c = lax.axis_index("c")
    return [pltpu.make_async_remote_copy(src_ref=refs[a].at[:, :, 1 - c], dst_ref=refs[n + a], send_sem=send_sems.at[a],
                                         recv_sem=recv_sems.at[a], device_id=_sibling(), device_id_type=MESH)
            for a in range(n)]


def _scatter_copies(srcs, lands, send_sems, recv_sems):
    c = lax.axis_index("c")
    return [pltpu.make_async_remote_copy(
        src_ref=srcs[a].at[2 * cx + cy], dst_ref=lands[a].at[j], send_sem=send_sems.at[3 * a + j],
        recv_sem=recv_sems.at[3 * a + j], device_id=(cx, cy, c), device_id_type=MESH)
        for a in range(len(srcs)) for j, (cx, cy) in enumerate(_other_chips())]


def _scatter_start(hsums, name, after=()):
    n = len(hsums)
    na = len(after)

    def body(*refs):
        srcs, lands = refs[:n], refs[n:2 * n]
        send_sems, recv_sems = refs[2 * n + na], refs[2 * n + na + 1]
        for cp in _scatter_copies(srcs, lands, send_sems, recv_sems):
            cp.start()
        refs[-1][...] = jnp.zeros_like(refs[-1])

    lands = [lax.empty((3,) + g.shape[1:], g.dtype) for g in hsums]
    res = pl.pallas_call(
        body, name=name,
        out_shape=[pltpu.SemaphoreType.DMA((3 * n,)), pltpu.SemaphoreType.DMA((3 * n,))]
        + [pltpu.HBM(g.shape, g.dtype) for g in hsums] + [pltpu.HBM(g.shape, g.dtype) for g in lands]
        + [jax.ShapeDtypeStruct((8, 128), f32)],
        in_specs=[_HBM] * (2 * n) + [pl.BlockSpec(memory_space=pl.ANY)] * na,
        out_specs=[_SEM, _SEM] + [_HBM] * (2 * n) + [pl.BlockSpec(memory_space=pltpu.VMEM)],
        input_output_aliases={i: i + 2 for i in range(2 * n)},
        compiler_params=pltpu.CompilerParams(has_side_effects=_EFFECT),
    )(*[pltpu.with_memory_space_constraint(t, pltpu.HBM) for t in list(hsums) + lands], *after)
    return (res[0], res[1], res[2:2 + n], res[2 + n:2 + 2 * n]), res[-1]


def _scatter_wait(send_sems, recv_sems, srcs, lands, after, name):
    n = len(srcs)
    extra = list(after)

    def body(*refs):
        s_refs, l_refs = refs[:n], refs[n:2 * n]
        ss, rs = refs[2 * n], refs[2 * n + 1]
        for cp in _scatter_copies(s_refs, l_refs, ss, rs):
            cp.wait_send()
            cp.wait_recv()

    res = pl.pallas_call(
        body, name=name,
        out_shape=[pltpu.HBM(g.shape, g.dtype) for g in srcs] + [pltpu.HBM(g.shape, g.dtype) for g in lands],
        in_specs=[_HBM] * (2 * n) + [_SEM, _SEM] + [pl.BlockSpec(memory_space=pl.ANY)] * len(extra),
        out_specs=[_HBM] * (2 * n),
        input_output_aliases={i: i for i in range(2 * n)},
        compiler_params=pltpu.CompilerParams(has_side_effects=_EFFECT),
    )(*srcs, *lands, send_sems, recv_sems, *extra)
    return res[:n], res[n:]


def _plane_copies(src, land, send_sems, recv_sems):
    x, y, c = lax.axis_index("x"), lax.axis_index("y"), lax.axis_index("c")
    return [pltpu.make_async_remote_copy(src_ref=src, dst_ref=land.at[2 * x + y, c], send_sem=send_sems.at[j],
                                         recv_sem=recv_sems.at[j], device_id=(cx, cy, c), device_id_type=MESH)
            for j, (cx, cy) in enumerate(_other_chips())]


def _plane_start(pack, land, name):
    def body(src, lnd, send_sems, recv_sems, _s, _l, token):
        for cp in _plane_copies(src, lnd, send_sems, recv_sems):
            cp.start()
        token[...] = jnp.zeros_like(token)

    res = pl.pallas_call(
        body, name=name,
        out_shape=[pltpu.SemaphoreType.DMA((3,)), pltpu.SemaphoreType.DMA((3,)), pltpu.HBM(pack.shape, pack.dtype),
                   pltpu.HBM(land.shape, land.dtype), jax.ShapeDtypeStruct((8, 128), f32)],
        in_specs=[_HBM, _HBM], out_specs=[_SEM, _SEM, _HBM, _HBM, pl.BlockSpec(memory_space=pltpu.VMEM)],
        input_output_aliases={0: 2, 1: 3},
        compiler_params=pltpu.CompilerParams(has_side_effects=_EFFECT),
    )(pltpu.with_memory_space_constraint(pack, pltpu.HBM), pltpu.with_memory_space_constraint(land, pltpu.HBM))
    return res[:4], res[4]


def _plane_wait(send_sems, recv_sems, pack, land, after, name):
    def body(src, lnd, ss, rs, *_):
        for cp in _plane_copies(src, lnd, ss, rs):
            cp.wait_send()
            cp.wait_recv()

    return pl.pallas_call(
        body, name=name,
        out_shape=[pltpu.HBM(pack.shape, pack.dtype), pltpu.HBM(land.shape, land.dtype)],
        in_specs=[_HBM, _HBM, _SEM, _SEM] + [pl.BlockSpec(memory_space=pl.ANY)] * len(after),
        out_specs=[_HBM, _HBM], input_output_aliases={0: 0, 1: 1},
        compiler_params=pltpu.CompilerParams(has_side_effects=_EFFECT),
    )(pack, land, send_sems, recv_sems, *after)[1]


def _swap_halves(fulls):
    n = len(fulls)

    def body(*refs):
        ins, outs = refs[:n], refs[n:2 * n]
        send_sems, recv_sems = refs[2 * n:]
        c = lax.axis_index("c")
        sibling = (lax.axis_index("x"), lax.axis_index("y"), 1 - c)
        copies = []
        for a in range(n):
            cp = pltpu.make_async_remote_copy(src_ref=outs[a].at[:, c], dst_ref=outs[a].at[:, c], send_sem=send_sems.at[a],
                                              recv_sem=recv_sems.at[a], device_id=sibling, device_id_type=MESH)
            cp.start()
            copies.append(cp)
        for a, cp in enumerate(copies):
            cp.wait_send()
            theirs = outs[a].at[:, 1 - c]
            pltpu.make_async_remote_copy(src_ref=theirs, dst_ref=theirs, send_sem=send_sems.at[a], recv_sem=recv_sems.at[a],
                                         device_id=sibling, device_id_type=MESH).wait_recv()

    hbm = pl.BlockSpec(memory_space=pl.ANY)
    return pl.pallas_call(
        body, name="swap_halves",
        out_shape=[jax.ShapeDtypeStruct(p.shape, p.dtype) for p in fulls],
        in_specs=[hbm] * n, out_specs=[hbm] * n,
        input_output_aliases={a: a for a in range(n)},
        scratch_shapes=[pltpu.SemaphoreType.DMA((n,)), pltpu.SemaphoreType.DMA((n,))],
    )(*fulls)


def _to_segments(t):
    s, c = t.shape
    return t.reshape(SCAN_SEG, s // SCAN_SEG, c).transpose(1, 0, 2).reshape(s, c)


def _from_segments(t):
    s, c = t.shape
    return t.reshape(s // SCAN_SEG, SCAN_SEG, c).transpose(1, 0, 2).reshape(s, c)


def _ssm_operators(a_re, a_im, log_dt, b_re, b_im, c_re, c_im):
    lam = lax.complex(a_re, a_im)
    dt = jnp.exp(log_dt)[:, None]
    a_bar = jnp.exp(lam * dt)
    b_bar = ((a_bar - 1.0) / lam)[:, :, None] * lax.complex(b_re, b_im)
    eye = jnp.eye(N_GROUPS, dtype=f32)

    def embed_b(t):
        return (jnp.transpose(t, (0, 2, 1))[:, :, None, :] * eye[:, None, :, None]).reshape(D_SSM, D_STATE)

    def embed_c(t):
        return (jnp.transpose(t, (0, 2, 1))[:, :, None, :] * eye[:, None, :, None]).reshape(D_STATE, D_SSM)

    a2 = jnp.stack([a_bar.real.reshape(D_STATE), a_bar.imag.reshape(D_STATE)])
    return a2, embed_b(b_bar.real), embed_b(b_bar.imag), embed_c(c_re), embed_c(c_im)


def _local_step(x, target, mod, small, ffn_weights, mix_weights, grads_done, ffn_bwd_issued):
    table = jnp.asarray(_bucket_table())
    bias = small["att_bias"]
    L = DEPTH
    saved = []
    ssm_names = ("ssm_a_re", "ssm_a_im", "ssm_log_dt", "ssm_b_re", "ssm_b_im", "ssm_c_re", "ssm_c_im")
    ssm_ops_vjp = []
    for l in range(L):
        sv = {}
        m9 = mod[l]
        sv["x0"] = x
        sv["w0"] = ffn_weights(l, 0, x)
        x, sv["f0"], sv["g0"], sv["u0"], sv["a0"], sv["h0"] = _ffn_fwd(x, m9[0:3], *sv["w0"], 0, small["ln_g"][l, 0:1], small["ln_b"][l, 0:1])
        sv["x1"] = x
        sv["w1"] = mix_weights(l, x)
        *qkv, z_rest, sv["h1"] = _mix_in_fwd(x, m9[3:6], sv["w1"][0], 0)
        S = x.shape[0]
        qkv = [t.reshape(3, S, D_ATT) for t in qkv]
        att = [_att_fwd(qkv[b], bias, b) for b in range(3)]
        y_att, lse3 = _att_merge([att[b][0].reshape(d, S // d, D_ATT) for b, d in enumerate(DILATIONS)],
                                 [att[b][1].reshape(d, S // d, _LANES) for b, d in enumerate(DILATIONS)])
        sv.update(qkv=qkv, lse=[a[1] for a in att], lse3=lse3, y_att=y_att)

        ops, ops_vjp = jax.vjp(_ssm_operators, *[small[k][l] for k in ssm_names])
        ssm_ops_vjp.append(ops_vjp)
        a2, bre, bim, cre, cim = [t[None] for t in ops]
        u_ssm = _to_segments(z_rest[:, :D_SSM])
        dskip = small["ssm_d"][l][None, :]
        glu_b = small["glu_b"][l][None, :]
        sr, si, out_seg, y_seg = _ssm_fwd(u_ssm, bre, bim, a2, cre, cim, 0, dskip, small["glu_w"][l], glu_b)
        y_ssm = _from_segments(out_seg)
        sv.update(ssm_ops=(a2, bre, bim, cre, cim), u_ssm=u_ssm, sr=sr, si=si, y_seg=y_seg, y_ssm=y_ssm)

        u_pool = jnp.concatenate([jnp.zeros((POOL_HALO, D_POOL), f32), z_rest[:, D_SSM:]])
        y_pool = _pool_fwd(u_pool, small["pool_w"][l], small["pool_scale"][l][None, :])
        sv.update(u_pool=u_pool, y_pool=y_pool)

        x, sv["ymix"] = _mix_out_fwd(x, y_att, y_ssm, y_pool, m9[3:6], sv["w1"][1], 0, small["ln_g"][l, 1:2], small["ln_b"][l, 1:2])
        sv["x2"] = x
        sv["w2"] = ffn_weights(l, 1, x)
        x, sv["f2"], sv["g2"], sv["u2"], sv["a2"], sv["h2"] = _ffn_fwd(x, m9[6:9], *sv["w2"], 0, small["ln_g"][l, 2:3], small["ln_b"][l, 2:3])
        saved.append(sv)

    loss, dx = _loss_head(x, target)

    dmod = [None] * L
    dln_g = [None] * L
    dln_b = [None] * L
    sg = {k: [None] * L for k in ssm_names + ("ssm_d", "glu_w", "glu_b", "pool_w", "pool_scale")}
    dbias_tot = None
    order_after = jnp.zeros((), f32)
    for l in reversed(range(L)):
        sv = saved[l]
        m9 = mod[l] + order_after

        def fresh(like):
            return [lax.empty(t.shape, bf16) for t in like]

        dx, dg, du, df, dm2, dlg2, dlb2 = _ffn_bwd(dx, sv["x2"], sv["f2"], sv["g2"], sv["u2"], m9[6:9], *sv["w2"], 0,
                                                     small["ln_g"][l, 2:3])
        m9 = m9 + ffn_bwd_issued(l, 1, dx)
        g_ffn1 = _ffn_wgrad(sv["h2"], dg, du, sv["a2"], df,*fresh(sv["w2"]), 0)
        dxr, d_att, d_ssm, d_pool, dgate1, dlg1, dlb1, g_w_out = _mix_out_bwd(
            dx, sv["x1"], sv["ymix"], sv["y_att"], sv["y_ssm"], sv["y_pool"], m9[3:6], sv["w1"][1], 0, small["ln_g"][l, 1:2],
            fresh(sv["w1"])[1])
        S = d_att.shape[0]
        merged = _att_merge_bwd(d_att, sv["y_att"], sv["lse3"])
        dqkv, dbias = [], []
        for b, d in enumerate(DILATIONS):
            dq_b, db_b = _att_bwd(sv["qkv"][b], merged[b].reshape(S, D_ATT), sv["lse"][b], merged[3 + b].reshape(S, _LANES), bias, b)
            dqkv.append(dq_b.reshape(3, d, S // d, D_ATT))
            dbias.append(db_b)
        dbias = jnp.stack(dbias)
        dbias_tot = dbias if dbias_tot is None else dbias_tot + dbias
        d_seg = _to_segments(d_ssm)
        dskip = small["ssm_d"][l][None, :]
        glu_b = small["glu_b"][l][None, :]
        dy_seg, du_skip, dd, dglu_b, dglu_w = _ssm_out_bwd(d_seg, sv["y_seg"], sv["u_ssm"], dskip, small["glu_w"][l], glu_b)
        a2, bre, bim, cre, cim = sv["ssm_ops"]
        du_seg, dbre, dbim, da2, dcre, dcim = _ssm_states_bwd(dy_seg, du_skip, sv["u_ssm"], sv["sr"], sv["si"], cre, cim,
                                                              bre, bim, a2, 0)
        for k, t in zip(ssm_names, ssm_ops_vjp[l]((da2, dbre, dbim, dcre, dcim))):
            sg[k][l] = t
        sg["ssm_d"][l] = dd[0]
        sg["glu_b"][l] = dglu_b[0]
        sg["glu_w"][l] = dglu_w
        du_ssm = _from_segments(du_seg)
        dyp = jnp.concatenate([d_pool, jnp.zeros((POOL_HALO, D_POOL), f32)])
        du_pool, dpw, dps = _pool_bwd(dyp, sv["u_pool"], small["pool_w"][l], small["pool_scale"][l][None, :])
        sg["pool_w"][l] = dpw
        sg["pool_scale"][l] = dps[0]
        d_rest = jnp.concatenate([du_ssm, du_pool], axis=1).astype(bf16)
        dx, dm1, dz = _mix_in_bwd(dqkv, d_rest, dxr, sv["x1"], m9[3:6], sv["w1"][0], 0)
        g_w_in = _mix_in_wgrad(sv["h1"], dz, fresh(sv["w1"])[0], 0)
        ffn_names = ("ffn_w_gate", "ffn_w_up", "ffn_w_down")
        m9 = m9 + grads_done(l, 1, list(zip(ffn_names, [(2 * l + 1) * FF_SHARD] * 3, g_ffn1))
                             + [("w_in", l * D_MODEL, g_w_in), ("w_out", l * 256, g_w_out)])
        dm1 = jnp.concatenate([dm1[0:2], dgate1])
        dx, dg, du, df, dm0, dlg0, dlb0 = _ffn_bwd(dx, sv["x0"], sv["f0"], sv["g0"], sv["u0"], m9[0:3], *sv["w0"], 0,
                                                     small["ln_g"][l, 0:1])
        issued = ffn_bwd_issued(l, 0, dx)
        g_ffn0 = _ffn_wgrad(sv["h0"], dg, du, sv["a0"], df,*fresh(sv["w0"]), 0)
        order_after = grads_done(l, 0, list(zip(ffn_names, [2 * l * FF_SHARD] * 3, g_ffn0))) + issued
        dmod[l] = jnp.concatenate([dm0 + issued, dm1, dm2])
        dln_g[l] = jnp.concatenate([dlg0, dlg1, dlg2])
        dln_b[l] = jnp.concatenate([dlb0, dlb1, dlb2])

    small_grads = {k: jnp.stack(v) for k, v in sg.items()}
    small_grads["rel_bias"] = _bias_bwd(dbias_tot, table)
    small_grads["ln_g"] = jnp.stack(dln_g)
    small_grads["ln_b"] = jnp.stack(dln_b)
    return loss, dx, jnp.stack(dmod), small_grads


_TILE_ELEMS = 8 * 128


def _pack_rows(shapes):
    out, row = [], 0
    for s in shapes:
        nr = -(-int(np.prod(s)) // _TILE_ELEMS) * 8
        out.append((row, nr))
        row += nr
    return out


def _pack(arrs):
    parts = []
    for a in arrs:
        flat = a.reshape(-1).astype(f32)
        npad = -(-flat.shape[0] // _TILE_ELEMS) * _TILE_ELEMS
        parts.append(jnp.pad(flat, (0, npad - flat.shape[0])).reshape(npad // 128, 128))
    return jnp.concatenate(parts, axis=0)


def _unpack(buf, shapes):
    return [buf[row:row + nr].reshape(-1)[:int(np.prod(s))].reshape(s) for s, (row, nr) in zip(shapes, _pack_rows(shapes))]


_REPL = ("rel_bias", "ada_b", "ssm_a_re", "ssm_a_im", "ssm_log_dt", "ssm_b_re", "ssm_b_im", "ssm_c_re", "ssm_c_im",
         "ssm_d", "glu_b", "pool_w", "pool_scale")
_SMALL_SHARDED = ("ln_g", "ln_b", "glu_w")
_BIG = ("ffn_w_gate", "ffn_w_up", "ffn_w_down", "w_in", "w_out")
_ORDER = ("rel_bias", "ada_w", "ada_b", "ln_g", "ln_b", "ffn_w_gate", "ffn_w_up", "ffn_w_down", "w_in", "w_out",
          "ssm_a_re", "ssm_a_im", "ssm_log_dt", "ssm_b_re", "ssm_b_im", "ssm_c_re", "ssm_c_im", "ssm_d", "glu_w",
          "glu_b", "pool_w", "pool_scale")


def kernel(x, c, rel_bias, ada_w, ada_b, ln_g, ln_b, ffn_w_gate, ffn_w_up, ffn_w_down, w_in, w_out, ssm_a_re, ssm_a_im, ssm_log_dt, ssm_b_re, ssm_b_im, ssm_c_re, ssm_c_im, ssm_d, glu_w, glu_b, pool_w, pool_scale, loss_target, m_rel_bias, m_ada_w, m_ada_b, m_ln_g, m_ln_b, m_ffn_w_gate, m_ffn_w_up, m_ffn_w_down, m_w_in, m_w_out, m_ssm_a_re, m_ssm_a_im, m_ssm_log_dt, m_ssm_b_re, m_ssm_b_im, m_ssm_c_re, m_ssm_c_im, m_ssm_d, m_glu_w, m_glu_b, m_pool_w, m_pool_scale, v_rel_bias, v_ada_w, v_ada_b, v_ln_g, v_ln_b, v_ffn_w_gate, v_ffn_w_up, v_ffn_w_down, v_w_in, v_w_out, v_ssm_a_re, v_ssm_a_im, v_ssm_log_dt, v_ssm_b_re, v_ssm_b_im, v_ssm_c_re, v_ssm_c_im, v_ssm_d, v_glu_w, v_glu_b, v_pool_w, v_pool_scale):
    args = dict(locals())
    w = {k: args[k] for k in _ORDER}
    m = {k: args["m_" + k] for k in _ORDER}
    v = {k: args["v_" + k] for k in _ORDER}
    L, D = DEPTH, D_MODEL
    ax, ay, ac = lax.axis_index("x"), lax.axis_index("y"), lax.axis_index("c")
    p_me = 2 * ax + ay
    dev = 4 * ax + 2 * ay + ac

    transposed = ("ffn_w_gate", "ffn_w_up")
    for d in (w, m, v):
        for name in transposed:
            d[name] = jnp.swapaxes(d[name], 2, 3)

    def halves(t):
        return t.astype(bf16).reshape(1, 2, t.shape[0] // 2, t.shape[1])

    def landing(src):
        return lax.dynamic_update_slice(lax.empty((N_CHIPS,) + src.shape, bf16), src[None], (p_me, 0, 0, 0, 0))

    chunk_keys = [("ffn", 0, 0), ("mix", 0), ("ffn", 0, 1), ("ffn", 1, 0), ("mix", 1), ("ffn", 1, 1)]
    chunk_srcs = []
    for key in chunk_keys:
        if key[0] == "ffn":
            chunk_srcs.append([halves(w[name][key[1], key[2]]) for name in ("ffn_w_gate", "ffn_w_up", "ffn_w_down")])
        else:
            chunk_srcs.append([halves(w_in[key[1]]), halves(w_out[key[1]])])

    pack = _pack([c, ln_g, ln_b, glu_w])
    rows = pack.shape[0]
    allp = _allgather8(pack).reshape(8, rows, 128)
    chunks = [(srcs, [landing(t) for t in srcs]) for srcs in chunk_srcs]
    first_in_flight, first_begun = _gather_start(chunks[:1], allp, "gather_start_first")
    c_all = allp[:, :8].reshape(8, D) + first_begun[0, 0]
    by_chip = allp[0::2]

    fwd_rows = _pack_rows([c.shape, ln_g.shape, ln_b.shape, glu_w.shape])

    def sharded(part, shape, axis):
        row0, nrows = fwd_rows[part]
        t = by_chip[:, row0:row0 + nrows].reshape(N_CHIPS, -1)[:, :int(np.prod(shape))].reshape((N_CHIPS,) + shape)
        return jnp.concatenate([t[p] for p in range(N_CHIPS)], axis=axis)

    ln_g_full = sharded(1, ln_g.shape, 2)
    ln_b_full = sharded(2, ln_b.shape, 2)
    glu_w_full = sharded(3, glu_w.shape, 1)

    ncol = ada_w.shape[-1]
    ada_b_cols = lax.dynamic_slice_in_dim(ada_b, p_me * ncol, ncol, axis=1)[:, None, :]
    mod_part = _ada_fwd(c_all, ada_w, ada_b_cols)
    mrows = L * 8 * ncol // 128
    mod_pack = mod_part.reshape(mrows, 128)
    mod_land = lax.dynamic_update_slice(lax.empty((N_CHIPS, 2, mrows, 128), f32), mod_pack[None, None], (p_me, ac, 0, 0))
    mod_in_flight, _ = _plane_start(mod_pack, mod_land, "mod_start")
    att_bias = _bias_fwd(rel_bias, jnp.asarray(_bucket_table()))
    small_names = _REPL + _SMALL_SHARDED
    small_packs = [_pack([d[k] for k in small_names]) for d in (w, m, v)]
    mod_land = _plane_wait(*mod_in_flight, [att_bias] + small_packs + [t for _, lands in chunks[1:] for t in lands], "mod_wait")
    mod_all = _swap_halves([mod_land])[0].reshape(8, L, 8, ncol)
    mod_mine = lax.dynamic_index_in_dim(mod_all, dev, axis=2, keepdims=False)
    mod = jnp.concatenate([mod_mine[2 * p] for p in range(N_CHIPS)], axis=-1).reshape(L, 9, D)

    rest_in_flight, rest_begun = _gather_start(chunks[1:], mod, "gather_start_rest")
    in_flight = first_in_flight + rest_in_flight

    forwarding = {}

    def forward(k, after):
        lands = _gather_wait(*in_flight[k], [after, rest_begun], "gather_wait_%d" % k)
        forwarding[k], begun = _split_start(_forward_copies, lands, 3 * len(lands), "gather_forward_start_%d" % k)
        return begun

    def gathered(key, after):
        k = chunk_keys.index(key)
        order = [after]
        if k not in forwarding:
            order.append(forward(k, after))
        if 3 <= k + 1 < len(chunk_keys):
            order.append(forward(k + 1, after))
        lands = _split_wait(_forward_copies, *forwarding[k], order, "gather_forward_wait_%d" % k)
        return [t.reshape(N_CHIPS, 1, 2 * t.shape[3], t.shape[4]) for t in lands]

    pc = jnp.stack([p_me, ac]).astype(jnp.int32)
    groups = {}
    scattering = {}

    pairing = {}

    def start_pairs(tag, after=()):
        g5 = [g.reshape(g.shape[:2] + (2, g.shape[2] // 2, g.shape[3])) for _, _, g in groups[tag]]
        gots = [lax.empty(g.shape[:2] + g.shape[3:], bf16) for g in g5]
        pairing[tag], begun = _split_start(_pair_copies, g5 + gots, len(g5), "pair_exchange_start_%s" % tag, after)
        return begun

    def start_group(tag, after):
        arrays = _split_wait(_pair_copies, *pairing[tag], after, "pair_exchange_wait_%s" % tag)
        n = len(arrays) // 2
        hsum = _pair_sum(arrays[:n], arrays[n:], pc)
        scattering[tag], begun = _scatter_start(hsum, "scatter_start_%s" % tag)
        return begun

    def grads_done(l, s, grads):
        if l == 1:
            groups.setdefault("l1", []).extend(grads)
            return start_pairs("l1")[0, 0] if s == 0 else jnp.zeros((), f32)
        groups["l0a" if s == 1 else "l0b"] = grads
        return start_pairs("l0a")[0, 0] if s == 1 else jnp.zeros((), f32)

    swapping = {}

    def reduce_group(tag, after):
        hsum, recv = _scatter_wait(*scattering[tag], after, "scatter_wait_%s" % tag)
        full = _sum_shards(hsum, recv, pc)
        swapping[tag], begun = _split_start(_swap_copies, full, len(full), "swap_halves_start_%s" % tag)
        return [begun]

    def ffn_bwd_issued(l, s, dx):
        if l == 1:
            return jnp.zeros((), f32)
        return start_group("l1" if s == 1 else "l0a", [dx])[0, 0]

    small = {k: w[k] for k in _REPL if k != "ada_b"}
    small.update(ln_g=ln_g_full, ln_b=ln_b_full, glu_w=glu_w_full, att_bias=att_bias)
    loss_dev, grad_x, dmod, sgrads = _local_step(
        x[0], loss_target[0], mod, small, lambda l, s, after: gathered(("ffn", l, s), after),
        lambda l, after: gathered(("mix", l), after), grads_done, ffn_bwd_issued)
    loss = lax.psum(loss_dev[0, 0], ("x", "y", "c"))

    names = ("rel_bias", "ln_g", "ln_b", "ssm_a_re", "ssm_a_im", "ssm_log_dt", "ssm_b_re", "ssm_b_im", "ssm_c_re",
             "ssm_c_im", "ssm_d", "glu_w", "glu_b", "pool_w", "pool_scale")
    gpack = _pack([dmod] + [sgrads[k] for k in names])
    grows = gpack.shape[0]
    land = lax.dynamic_update_slice(lax.empty((N_CHIPS, 2, grows, 128), f32), gpack[None, None], (p_me, ac, 0, 0))
    small_in_flight, small_begun = _plane_start(gpack, land, "small_grads_start")
    l0b_begun = start_group("l0b", [start_pairs("l0b", (small_begun,))])

    out_g, out_d, out_m, out_v = {}, {}, {}, {}
    row_tile = dict(zip(_BIG, (352, 352, 352, 256, 256)))
    big = {name: None for name in _BIG}

    def update_group(tag, after):
        full = _split_wait(_swap_copies, *swapping[tag], after, "swap_halves_wait_%s" % tag)
        for (name, row0, _), g in zip(groups[tag], full):
            shp = w[name].shape
            r2 = (int(np.prod(shp[:-1])), shp[-1])
            big[name] = _adamw(w[name].reshape(r2), m[name].reshape(r2), v[name].reshape(r2), g.reshape(-1, shp[-1]),
                               row_tile[name], row0, big[name])
        return [big[name][1] for name, _, _ in groups[tag]]

    after = reduce_group("l0a", reduce_group("l1", [grad_x, l0b_begun]))
    after = update_group("l0a", update_group("l1", after))

    land = _plane_wait(*small_in_flight, after, "small_grads_wait")
    gall = _swap_halves([land])[0].reshape(8, grows, 128)
    gsum = _unpack(_sum8(gall), [(L, 9 * D)] + [sgrads[k].shape for k in names])
    red = dict(zip(("ada_b",) + names, gsum))
    red["ln_g"] = lax.dynamic_slice_in_dim(red["ln_g"], p_me * 256, 256, axis=2)
    red["ln_b"] = lax.dynamic_slice_in_dim(red["ln_b"], p_me * 256, 256, axis=2)
    red["glu_w"] = lax.dynamic_slice_in_dim(red["glu_w"], p_me * 64, 64, axis=1)

    dmod_all = gall[:, :L * 9 * D // 128].reshape(8, L, 9 * D)
    dmod_cols = jnp.transpose(lax.dynamic_slice_in_dim(dmod_all, p_me * ncol, ncol, axis=2), (1, 0, 2))
    g_ada_w = _ada_wgrad(jnp.transpose(c_all), dmod_cols)

    r2 = (L * D, ncol)
    res = _adamw(ada_w.reshape(r2), m["ada_w"].reshape(r2), v["ada_w"].reshape(r2), g_ada_w.reshape(r2), 128)
    out_g["ada_w"], out_d["ada_w"], out_m["ada_w"], out_v["ada_w"] = [t.reshape(ada_w.shape) for t in res]

    res_small = _adamw(*small_packs, _pack([red[k] for k in small_names]), small_packs[0].shape[0])
    for t, dst in zip(res_small, (out_g, out_d, out_m, out_v)):
        for k, a in zip(small_names, _unpack(t, [w[k].shape for k in small_names])):
            dst[k] = a

    update_group("l0b", reduce_group("l0b", [res_small[1], res[1]]))
    for name in _BIG:
        res = [t.reshape(w[name].shape) for t in big[name]]
        out_g[name], out_d[name], out_m[name], out_v[name] = [jnp.swapaxes(t, 2, 3) for t in res] if name in transposed else res

    return (loss, grad_x[None], *[out_g[k] for k in _ORDER], *[out_d[k] for k in _ORDER],
            *[out_m[k] for k in _ORDER], *[out_v[k] for k in _ORDER])
```
